```python
import math
import jax, jax.numpy as jnp
from jax import lax
import numpy as np

D_MODEL = 1024
BATCH = 8
SEQ = 4096
DEPTH = 2

GRID_W = 64
D_FF = 2816
NORM_EPS = 1e-6
N_BRANCHES = 3
S5_GROUPS = 32
S5_GROUP_CH = 16
S5_STATE = 64
S5_WIDTH = S5_GROUPS * S5_GROUP_CH
S5_DT_MIN = 1e-3
S5_DT_MAX = 1e-1
GLA_HEADS = 4
GLA_HEAD_DIM = 128
GLA_WIDTH = GLA_HEADS * GLA_HEAD_DIM
GLA_LOWRANK = 16
GLA_TAU = 16.0
GLA_CHUNK = 64
ATTN_Q_HEADS = 8
ATTN_KV_HEADS = 2
ATTN_HEAD_DIM = 64
ATTN_WIDTH = ATTN_Q_HEADS * ATTN_HEAD_DIM
ATTN_KV_WIDTH = ATTN_KV_HEADS * ATTN_HEAD_DIM
ATTN_BLOCK = 128
ROPE_BASE = 10000.0
IN_SPLITS = (S5_WIDTH, GLA_WIDTH, GLA_WIDTH, GLA_WIDTH, GLA_WIDTH, GLA_LOWRANK, GLA_LOWRANK, ATTN_WIDTH, ATTN_KV_WIDTH, ATTN_KV_WIDTH)
IN_WIDTH = sum(IN_SPLITS)

kernel_name = 'hybrid_s5_gla_gqa_macaron_encoder'

F32 = jnp.float32


def rms_norm(x, gain):
    x32 = x.astype(F32)
    y = x32 * lax.rsqrt(jnp.mean(x32 * x32, axis=-1, keepdims=True) + NORM_EPS)
    return (y * gain.astype(F32)).astype(x.dtype)


def swiglu_ffn(h, w_gate, w_up, w_down):
    return (jax.nn.silu(h @ w_gate) * (h @ w_up)) @ w_down


def _linear_recurrence(left, right):
    a_l, b_l = left
    a_r, b_r = right
    return a_r * a_l, a_r * b_l + b_r


def s5_scan_dir(u32, lam_re, lam_im, log_dt, b_re, b_im, c_re, c_im, reverse):
    lam = lax.complex(lam_re.astype(F32), lam_im.astype(F32))
    dt = jnp.exp(log_dt.astype(F32))[:, None]
    lam_bar = jnp.exp(lam * dt)
    b = lax.complex(b_re.astype(F32), b_im.astype(F32))
    b_bar = ((lam_bar - 1.0) / lam)[..., None] * b
    bu = jnp.einsum('blgh,gph->blgp', u32.astype(jnp.complex64), b_bar)
    a = jnp.broadcast_to(lam_bar, bu.shape)
    _, states = lax.associative_scan(_linear_recurrence, (a, bu), axis=1, reverse=reverse)
    c = lax.complex(c_re.astype(F32), c_im.astype(F32))
    return jnp.real(jnp.einsum('blgp,ghp->blgh', states, c))


def s5_branch(u, lam_re, lam_im, log_dt, b_re, b_im, c_re, c_im, d_skip, w_glu):
    bsz, seq_len, _ = u.shape
    u32 = u.astype(F32)
    ug = u32.reshape(bsz, seq_len, S5_GROUPS, S5_GROUP_CH)
    y = (s5_scan_dir(ug, lam_re[0], lam_im[0], log_dt[0], b_re[0], b_im[0], c_re[0], c_im[0], False)
         + s5_scan_dir(ug, lam_re[1], lam_im[1], log_dt[1], b_re[1], b_im[1], c_re[1], c_im[1], True))
    y = y.reshape(bsz, seq_len, S5_WIDTH) + d_skip.astype(F32) * u32
    y = jax.nn.gelu(y).astype(u.dtype)
    return y * jax.nn.sigmoid(y @ w_glu)


def gla_chunked(q, k, v, log_a):
    bsz, seq_len, nh, dk = q.shape
    dv = v.shape[-1]
    n_chunks = seq_len // GLA_CHUNK

    def chunks(t):
        return t.reshape(bsz, n_chunks, GLA_CHUNK, nh, t.shape[-1])

    q, k, v, log_a = chunks(q), chunks(k), chunks(v), chunks(log_a)
    b = jnp.cumsum(log_a, axis=2)
    b_last = b[:, :, -1]
    q_dec = q * jnp.exp(b)
    k_dec = k * jnp.exp(-b)
    mask = jnp.tril(jnp.ones((GLA_CHUNK, GLA_CHUNK), dtype=bool))
    scores = jnp.where(mask, jnp.einsum('bnihd,bnjhd->bnhij', q_dec, k_dec), 0.0)
    o_intra = jnp.einsum('bnhij,bnjhe->bnihe', scores, v)
    k_to_end = k * jnp.exp(b_last[:, :, None] - b)
    chunk_kv = jnp.einsum('bnjhd,bnjhe->nbhde', k_to_end, v)
    chunk_decay = jnp.exp(jnp.moveaxis(b_last, 1, 0))

    def step(state, inp):
        decay, kv = inp
        return decay[..., None] * state + kv, state

    init = jnp.zeros((bsz, nh, dk, dv), F32)
    _, prev_states = lax.scan(step, init, (chunk_decay, chunk_kv))
    o_inter = jnp.einsum('bnihd,nbhde->bnihe', q_dec, prev_states)
    return (o_intra + o_inter).reshape(bsz, seq_len, nh, dv)


def gla_branch(q, k, v, gate, z_f, z_b, w_alpha, b_alpha, norm_gain):
    bsz, seq_len, _ = q.shape

    def heads(t):
        return t.astype(F32).reshape(bsz, seq_len, GLA_HEADS, GLA_HEAD_DIM)

    qh = heads(q) * GLA_HEAD_DIM ** -0.5
    kh = heads(k)
    vh = heads(v)

    def log_gate(z, w, bias):
        logits = (z @ w + bias).astype(F32)
        return heads(jax.nn.log_sigmoid(logits) / GLA_TAU)

    la_f = log_gate(z_f, w_alpha[0], b_alpha[0])
    la_b = log_gate(z_b, w_alpha[1], b_alpha[1])

    def flip(t):
        return jnp.flip(t, axis=1)

    o_f = gla_chunked(qh, kh, vh, la_f)
    o_b = flip(gla_chunked(flip(qh), flip(kh), flip(vh), flip(la_b)))
    o = rms_norm(o_f + o_b, norm_gain).reshape(bsz, seq_len, GLA_WIDTH).astype(q.dtype)
    return o * jax.nn.silu(gate)


def rope_1d(x, pos):
    d = x.shape[-1]
    half = d // 2
    inv_freq = ROPE_BASE ** (-jnp.arange(half, dtype=F32) * 2.0 / d)
    ang = pos.astype(F32)[:, None] * inv_freq[None, :]
    cos = jnp.cos(ang)[:, None, :]
    sin = jnp.sin(ang)[:, None, :]
    x1, x2 = x[..., :half], x[..., half:]
    return jnp.concatenate([x1 * cos - x2 * sin, x2 * cos + x1 * sin], axis=-1)


def axial_rope(x, rows, cols):
    half = x.shape[-1] // 2
    return jnp.concatenate([rope_1d(x[..., :half], rows), rope_1d(x[..., half:], cols)], axis=-1)


def attn_branch(q, k, v, q_gain, k_gain):
    bsz, seq_len, _ = q.shape
    n_rows = seq_len // GRID_W
    rows = jnp.repeat(jnp.arange(n_rows, dtype=jnp.int32), GRID_W)
    cols = jnp.tile(jnp.arange(GRID_W, dtype=jnp.int32), n_rows)
    qh = q.astype(F32).reshape(bsz, seq_len, ATTN_Q_HEADS, ATTN_HEAD_DIM)
    kh = k.astype(F32).reshape(bsz, seq_len, ATTN_KV_HEADS, ATTN_HEAD_DIM)
    vh = v.astype(F32).reshape(bsz, seq_len, ATTN_KV_HEADS, ATTN_HEAD_DIM)
    qh = axial_rope(rms_norm(qh, q_gain), rows, cols) * ATTN_HEAD_DIM ** -0.5
    kh = axial_rope(rms_norm(kh, k_gain), rows, cols)
    group = ATTN_Q_HEADS // ATTN_KV_HEADS
    n_blocks = seq_len // ATTN_BLOCK
    q_blocks = qh.reshape(bsz, n_blocks, ATTN_BLOCK, ATTN_KV_HEADS, group, ATTN_HEAD_DIM)
    q_blocks = jnp.moveaxis(q_blocks, 1, 0)

    def attend(qb):
        s = jnp.einsum('bqkgd,bskd->bkgqs', qb, kh)
        p = jax.nn.softmax(s, axis=-1)
        return jnp.einsum('bkgqs,bskd->bqkgd', p, vh)

    out = lax.map(attend, q_blocks)
    out = jnp.moveaxis(out, 0, 1).reshape(bsz, seq_len, ATTN_WIDTH)
    return out.astype(q.dtype)


def _fwd_setup_inputs(seed: int = 0) -> dict:
    key = jax.random.key(seed)
    keys = iter(jax.random.split(key, 40))

    def normal(shape, scale):
        return scale * jax.random.normal(next(keys), shape, F32)

    def gain(shape):
        return 1.0 + normal(shape, 0.02)

    G, H, P = S5_GROUPS, S5_GROUP_CH, S5_STATE
    x = normal((BATCH, SEQ, D_MODEL), 1.0)
    ffn1_norm = gain((DEPTH, D_MODEL))
    ffn1_w_gate = normal((DEPTH, D_MODEL, D_FF), D_MODEL ** -0.5)
    ffn1_w_up = normal((DEPTH, D_MODEL, D_FF), D_MODEL ** -0.5)
    ffn1_w_down = normal((DEPTH, D_FF, D_MODEL), D_FF ** -0.5)
    mix_norm = gain((DEPTH, D_MODEL))
    w_in = normal((DEPTH, D_MODEL, IN_WIDTH), D_MODEL ** -0.5)
    s5_lambda_re = -0.5 + normal((DEPTH, 2, G, P), 0.01)
    s5_lambda_im = math.pi * jnp.arange(P, dtype=F32) + normal((DEPTH, 2, G, P), 0.01)
    s5_log_dt = jax.random.uniform(next(keys), (DEPTH, 2, G), F32, math.log(S5_DT_MIN), math.log(S5_DT_MAX))
    s5_b_re = normal((DEPTH, 2, G, P, H), (0.5 / H) ** 0.5)
    s5_b_im = normal((DEPTH, 2, G, P, H), (0.5 / H) ** 0.5)
    s5_c_re = normal((DEPTH, 2, G, H, P), (0.5 / P) ** 0.5)
    s5_c_im = normal((DEPTH, 2, G, H, P), (0.5 / P) ** 0.5)
    s5_d = normal((DEPTH, S5_WIDTH), 1.0)
    s5_w_glu = normal((DEPTH, S5_WIDTH, S5_WIDTH), S5_WIDTH ** -0.5)
    gla_w_alpha = normal((DEPTH, 2, GLA_LOWRANK, GLA_WIDTH), GLA_LOWRANK ** -0.5)
    gla_b_alpha = normal((DEPTH, 2, GLA_WIDTH), 0.1)
    gla_norm = gain((DEPTH, GLA_HEAD_DIM))
    attn_q_norm = gain((DEPTH, ATTN_HEAD_DIM))
    attn_k_norm = gain((DEPTH, ATTN_HEAD_DIM))
    w_branch_s5 = normal((DEPTH, S5_WIDTH, D_MODEL), S5_WIDTH ** -0.5)
    w_branch_gla = normal((DEPTH, GLA_WIDTH, D_MODEL), GLA_WIDTH ** -0.5)
    w_branch_attn = normal((DEPTH, ATTN_WIDTH, D_MODEL), ATTN_WIDTH ** -0.5)
    w_merge_gate = normal((DEPTH, D_MODEL, N_BRANCHES * D_MODEL), D_MODEL ** -0.5)
    b_merge_gate = normal((DEPTH, N_BRANCHES * D_MODEL), 0.01)
    w_out = normal((DEPTH, D_MODEL, D_MODEL), D_MODEL ** -0.5)
    ffn2_norm = gain((DEPTH, D_MODEL))
    ffn2_w_gate = normal((DEPTH, D_MODEL, D_FF), D_MODEL ** -0.5)
    ffn2_w_up = normal((DEPTH, D_MODEL, D_FF), D_MODEL ** -0.5)
    ffn2_w_down = normal((DEPTH, D_FF, D_MODEL), D_FF ** -0.5)
    final_norm = gain((D_MODEL,))
    return {'x': x, 'ffn1_norm': ffn1_norm, 'ffn1_w_gate': ffn1_w_gate, 'ffn1_w_up': ffn1_w_up,
            'ffn1_w_down': ffn1_w_down, 'mix_norm': mix_norm, 'w_in': w_in,
            's5_lambda_re': s5_lambda_re, 's5_lambda_im': s5_lambda_im, 's5_log_dt': s5_log_dt,
            's5_b_re': s5_b_re, 's5_b_im': s5_b_im, 's5_c_re': s5_c_re, 's5_c_im': s5_c_im,
            's5_d': s5_d, 's5_w_glu': s5_w_glu, 'gla_w_alpha': gla_w_alpha, 'gla_b_alpha': gla_b_alpha,
            'gla_norm': gla_norm, 'attn_q_norm': attn_q_norm, 'attn_k_norm': attn_k_norm,
            'w_branch_s5': w_branch_s5, 'w_branch_gla': w_branch_gla, 'w_branch_attn': w_branch_attn,
            'w_merge_gate': w_merge_gate, 'b_merge_gate': b_merge_gate, 'w_out': w_out,
            'ffn2_norm': ffn2_norm, 'ffn2_w_gate': ffn2_w_gate, 'ffn2_w_up': ffn2_w_up,
            'ffn2_w_down': ffn2_w_down, 'final_norm': final_norm}


def _fwd_reference(x, ffn1_norm, ffn1_w_gate, ffn1_w_up, ffn1_w_down, mix_norm, w_in,
              s5_lambda_re, s5_lambda_im, s5_log_dt, s5_b_re, s5_b_im, s5_c_re, s5_c_im,
              s5_d, s5_w_glu, gla_w_alpha, gla_b_alpha, gla_norm, attn_q_norm, attn_k_norm,
              w_branch_s5, w_branch_gla, w_branch_attn, w_merge_gate, b_merge_gate, w_out,
              ffn2_norm, ffn2_w_gate, ffn2_w_up, ffn2_w_down, final_norm):
    bsz, seq_len, _ = x.shape
    split_at = [int(c) for c in np.cumsum(IN_SPLITS)[:-1]]
    for i in range(DEPTH):
        h = rms_norm(x, ffn1_norm[i])
        x = x + 0.5 * swiglu_ffn(h, ffn1_w_gate[i], ffn1_w_up[i], ffn1_w_down[i])

        h = rms_norm(x, mix_norm[i])
        (s5_u, gla_q, gla_k, gla_v, gla_g, gla_zf, gla_zb,
         at_q, at_k, at_v) = jnp.split(h @ w_in[i], split_at, axis=-1)
        y_s5 = s5_branch(s5_u, s5_lambda_re[i], s5_lambda_im[i], s5_log_dt[i], s5_b_re[i], s5_b_im[i],
                         s5_c_re[i], s5_c_im[i], s5_d[i], s5_w_glu[i])
        y_gla = gla_branch(gla_q, gla_k, gla_v, gla_g, gla_zf, gla_zb,
                           gla_w_alpha[i], gla_b_alpha[i], gla_norm[i])
        y_attn = attn_branch(at_q, at_k, at_v, attn_q_norm[i], attn_k_norm[i])
        gates = jax.nn.sigmoid(h @ w_merge_gate[i] + b_merge_gate[i])
        gates = gates.reshape(bsz, seq_len, N_BRANCHES, D_MODEL)
        merged = (gates[:, :, 0] * (y_s5 @ w_branch_s5[i])
                  + gates[:, :, 1] * (y_gla @ w_branch_gla[i])
                  + gates[:, :, 2] * (y_attn @ w_branch_attn[i]))
        x = x + merged @ w_out[i]

        h = rms_norm(x, ffn2_norm[i])
        x = x + 0.5 * swiglu_ffn(h, ffn2_w_gate[i], ffn2_w_up[i], ffn2_w_down[i])
    return rms_norm(x, final_norm)


import jax as _jax
import jax.numpy as _jnp

TWIN_FORMAT = 'train_step'
FWD_PARAMS = ['x', 'ffn1_norm', 'ffn1_w_gate', 'ffn1_w_up', 'ffn1_w_down', 'mix_norm', 'w_in', 's5_lambda_re', 's5_lambda_im', 's5_log_dt', 's5_b_re', 's5_b_im', 's5_c_re', 's5_c_im', 's5_d', 's5_w_glu', 'gla_w_alpha', 'gla_b_alpha', 'gla_norm', 'attn_q_norm', 'attn_k_norm', 'w_branch_s5', 'w_branch_gla', 'w_branch_attn', 'w_merge_gate', 'b_merge_gate', 'w_out', 'ffn2_norm', 'ffn2_w_gate', 'ffn2_w_up', 'ffn2_w_down', 'final_norm']
TWIN_WEIGHTS = ['ffn1_norm', 'ffn1_w_gate', 'ffn1_w_up', 'ffn1_w_down', 'mix_norm', 'w_in', 's5_lambda_re', 's5_lambda_im', 's5_log_dt', 's5_b_re', 's5_b_im', 's5_c_re', 's5_c_im', 's5_d', 's5_w_glu', 'gla_w_alpha', 'gla_b_alpha', 'gla_norm', 'attn_q_norm', 'attn_k_norm', 'w_branch_s5', 'w_branch_gla', 'w_branch_attn', 'w_merge_gate', 'b_merge_gate', 'w_out', 'ffn2_norm', 'ffn2_w_gate', 'ffn2_w_up', 'ffn2_w_down', 'final_norm']
TWIN_DIFF_INPUT = 'x'
TWIN_INPUTS = ['x', 'ffn1_norm', 'ffn1_w_gate', 'ffn1_w_up', 'ffn1_w_down', 'mix_norm', 'w_in', 's5_lambda_re', 's5_lambda_im', 's5_log_dt', 's5_b_re', 's5_b_im', 's5_c_re', 's5_c_im', 's5_d', 's5_w_glu', 'gla_w_alpha', 'gla_b_alpha', 'gla_norm', 'attn_q_norm', 'attn_k_norm', 'w_branch_s5', 'w_branch_gla', 'w_branch_attn', 'w_merge_gate', 'b_merge_gate', 'w_out', 'ffn2_norm', 'ffn2_w_gate', 'ffn2_w_up', 'ffn2_w_down', 'final_norm', 'loss_target', 'm_ffn1_norm', 'm_ffn1_w_gate', 'm_ffn1_w_up', 'm_ffn1_w_down', 'm_mix_norm', 'm_w_in', 'm_s5_lambda_re', 'm_s5_lambda_im', 'm_s5_log_dt', 'm_s5_b_re', 'm_s5_b_im', 'm_s5_c_re', 'm_s5_c_im', 'm_s5_d', 'm_s5_w_glu', 'm_gla_w_alpha', 'm_gla_b_alpha', 'm_gla_norm', 'm_attn_q_norm', 'm_attn_k_norm', 'm_w_branch_s5', 'm_w_branch_gla', 'm_w_branch_attn', 'm_w_merge_gate', 'm_b_merge_gate', 'm_w_out', 'm_ffn2_norm', 'm_ffn2_w_gate', 'm_ffn2_w_up', 'm_ffn2_w_down', 'm_final_norm', 'v_ffn1_norm', 'v_ffn1_w_gate', 'v_ffn1_w_up', 'v_ffn1_w_down', 'v_mix_norm', 'v_w_in', 'v_s5_lambda_re', 'v_s5_lambda_im', 'v_s5_log_dt', 'v_s5_b_re', 'v_s5_b_im', 'v_s5_c_re', 'v_s5_c_im', 'v_s5_d', 'v_s5_w_glu', 'v_gla_w_alpha', 'v_gla_b_alpha', 'v_gla_norm', 'v_attn_q_norm', 'v_attn_k_norm', 'v_w_branch_s5', 'v_w_branch_gla', 'v_w_branch_attn', 'v_w_merge_gate', 'v_b_merge_gate', 'v_w_out', 'v_ffn2_norm', 'v_ffn2_w_gate', 'v_ffn2_w_up', 'v_ffn2_w_down', 'v_final_norm']
TWIN_OUTPUTS = ['loss', 'grad_x', 'grad_ffn1_norm', 'grad_ffn1_w_gate', 'grad_ffn1_w_up', 'grad_ffn1_w_down', 'grad_mix_norm', 'grad_w_in', 'grad_s5_lambda_re', 'grad_s5_lambda_im', 'grad_s5_log_dt', 'grad_s5_b_re', 'grad_s5_b_im', 'grad_s5_c_re', 'grad_s5_c_im', 'grad_s5_d', 'grad_s5_w_glu', 'grad_gla_w_alpha', 'grad_gla_b_alpha', 'grad_gla_norm', 'grad_attn_q_norm', 'grad_attn_k_norm', 'grad_w_branch_s5', 'grad_w_branch_gla', 'grad_w_branch_attn', 'grad_w_merge_gate', 'grad_b_merge_gate', 'grad_w_out', 'grad_ffn2_norm', 'grad_ffn2_w_gate', 'grad_ffn2_w_up', 'grad_ffn2_w_down', 'grad_final_norm', 'delta_ffn1_norm', 'delta_ffn1_w_gate', 'delta_ffn1_w_up', 'delta_ffn1_w_down', 'delta_mix_norm', 'delta_w_in', 'delta_s5_lambda_re', 'delta_s5_lambda_im', 'delta_s5_log_dt', 'delta_s5_b_re', 'delta_s5_b_im', 'delta_s5_c_re', 'delta_s5_c_im', 'delta_s5_d', 'delta_s5_w_glu', 'delta_gla_w_alpha', 'delta_gla_b_alpha', 'delta_gla_norm', 'delta_attn_q_norm', 'delta_attn_k_norm', 'delta_w_branch_s5', 'delta_w_branch_gla', 'delta_w_branch_attn', 'delta_w_merge_gate', 'delta_b_merge_gate', 'delta_w_out', 'delta_ffn2_norm', 'delta_ffn2_w_gate', 'delta_ffn2_w_up', 'delta_ffn2_w_down', 'delta_final_norm', 'new_m_ffn1_norm', 'new_m_ffn1_w_gate', 'new_m_ffn1_w_up', 'new_m_ffn1_w_down', 'new_m_mix_norm', 'new_m_w_in', 'new_m_s5_lambda_re', 'new_m_s5_lambda_im', 'new_m_s5_log_dt', 'new_m_s5_b_re', 'new_m_s5_b_im', 'new_m_s5_c_re', 'new_m_s5_c_im', 'new_m_s5_d', 'new_m_s5_w_glu', 'new_m_gla_w_alpha', 'new_m_gla_b_alpha', 'new_m_gla_norm', 'new_m_attn_q_norm', 'new_m_attn_k_norm', 'new_m_w_branch_s5', 'new_m_w_branch_gla', 'new_m_w_branch_attn', 'new_m_w_merge_gate', 'new_m_b_merge_gate', 'new_m_w_out', 'new_m_ffn2_norm', 'new_m_ffn2_w_gate', 'new_m_ffn2_w_up', 'new_m_ffn2_w_down', 'new_m_final_norm', 'new_v_ffn1_norm', 'new_v_ffn1_w_gate', 'new_v_ffn1_w_up', 'new_v_ffn1_w_down', 'new_v_mix_norm', 'new_v_w_in', 'new_v_s5_lambda_re', 'new_v_s5_lambda_im', 'new_v_s5_log_dt', 'new_v_s5_b_re', 'new_v_s5_b_im', 'new_v_s5_c_re', 'new_v_s5_c_im', 'new_v_s5_d', 'new_v_s5_w_glu', 'new_v_gla_w_alpha', 'new_v_gla_b_alpha', 'new_v_gla_norm', 'new_v_attn_q_norm', 'new_v_attn_k_norm', 'new_v_w_branch_s5', 'new_v_w_branch_gla', 'new_v_w_branch_attn', 'new_v_w_merge_gate', 'new_v_b_merge_gate', 'new_v_w_out', 'new_v_ffn2_norm', 'new_v_ffn2_w_gate', 'new_v_ffn2_w_up', 'new_v_ffn2_w_down', 'new_v_final_norm']
TWIN_LEAF_KINDS = {'loss': 'loss', 'grad_x': 'grad_x', 'grad_ffn1_norm': 'grad_w', 'grad_ffn1_w_gate': 'grad_w', 'grad_ffn1_w_up': 'grad_w', 'grad_ffn1_w_down': 'grad_w', 'grad_mix_norm': 'grad_w', 'grad_w_in': 'grad_w', 'grad_s5_lambda_re': 'grad_w', 'grad_s5_lambda_im': 'grad_w', 'grad_s5_log_dt': 'grad_w', 'grad_s5_b_re': 'grad_w', 'grad_s5_b_im': 'grad_w', 'grad_s5_c_re': 'grad_w', 'grad_s5_c_im': 'grad_w', 'grad_s5_d': 'grad_w', 'grad_s5_w_glu': 'grad_w', 'grad_gla_w_alpha': 'grad_w', 'grad_gla_b_alpha': 'grad_w', 'grad_gla_norm': 'grad_w', 'grad_attn_q_norm': 'grad_w', 'grad_attn_k_norm': 'grad_w', 'grad_w_branch_s5': 'grad_w', 'grad_w_branch_gla': 'grad_w', 'grad_w_branch_attn': 'grad_w', 'grad_w_merge_gate': 'grad_w', 'grad_b_merge_gate': 'grad_w', 'grad_w_out': 'grad_w', 'grad_ffn2_norm': 'grad_w', 'grad_ffn2_w_gate': 'grad_w', 'grad_ffn2_w_up': 'grad_w', 'grad_ffn2_w_down': 'grad_w', 'grad_final_norm': 'grad_w', 'delta_ffn1_norm': 'delta_w', 'delta_ffn1_w_gate': 'delta_w', 'delta_ffn1_w_up': 'delta_w', 'delta_ffn1_w_down': 'delta_w', 'delta_mix_norm': 'delta_w', 'delta_w_in': 'delta_w', 'delta_s5_lambda_re': 'delta_w', 'delta_s5_lambda_im': 'delta_w', 'delta_s5_log_dt': 'delta_w', 'delta_s5_b_re': 'delta_w', 'delta_s5_b_im': 'delta_w', 'delta_s5_c_re': 'delta_w', 'delta_s5_c_im': 'delta_w', 'delta_s5_d': 'delta_w', 'delta_s5_w_glu': 'delta_w', 'delta_gla_w_alpha': 'delta_w', 'delta_gla_b_alpha': 'delta_w', 'delta_gla_norm': 'delta_w', 'delta_attn_q_norm': 'delta_w', 'delta_attn_k_norm': 'delta_w', 'delta_w_branch_s5': 'delta_w', 'delta_w_branch_gla': 'delta_w', 'delta_w_branch_attn': 'delta_w', 'delta_w_merge_gate': 'delta_w', 'delta_b_merge_gate': 'delta_w', 'delta_w_out': 'delta_w', 'delta_ffn2_norm': 'delta_w', 'delta_ffn2_w_gate': 'delta_w', 'delta_ffn2_w_up': 'delta_w', 'delta_ffn2_w_down': 'delta_w', 'delta_final_norm': 'delta_w', 'new_m_ffn1_norm': 'new_m', 'new_m_ffn1_w_gate': 'new_m', 'new_m_ffn1_w_up': 'new_m', 'new_m_ffn1_w_down': 'new_m', 'new_m_mix_norm': 'new_m', 'new_m_w_in': 'new_m', 'new_m_s5_lambda_re': 'new_m', 'new_m_s5_lambda_im': 'new_m', 'new_m_s5_log_dt': 'new_m', 'new_m_s5_b_re': 'new_m', 'new_m_s5_b_im': 'new_m', 'new_m_s5_c_re': 'new_m', 'new_m_s5_c_im': 'new_m', 'new_m_s5_d': 'new_m', 'new_m_s5_w_glu': 'new_m', 'new_m_gla_w_alpha': 'new_m', 'new_m_gla_b_alpha': 'new_m', 'new_m_gla_norm': 'new_m', 'new_m_attn_q_norm': 'new_m', 'new_m_attn_k_norm': 'new_m', 'new_m_w_branch_s5': 'new_m', 'new_m_w_branch_gla': 'new_m', 'new_m_w_branch_attn': 'new_m', 'new_m_w_merge_gate': 'new_m', 'new_m_b_merge_gate': 'new_m', 'new_m_w_out': 'new_m', 'new_m_ffn2_norm': 'new_m', 'new_m_ffn2_w_gate': 'new_m', 'new_m_ffn2_w_up': 'new_m', 'new_m_ffn2_w_down': 'new_m', 'new_m_final_norm': 'new_m', 'new_v_ffn1_norm': 'new_v', 'new_v_ffn1_w_gate': 'new_v', 'new_v_ffn1_w_up': 'new_v', 'new_v_ffn1_w_down': 'new_v', 'new_v_mix_norm': 'new_v', 'new_v_w_in': 'new_v', 'new_v_s5_lambda_re': 'new_v', 'new_v_s5_lambda_im': 'new_v', 'new_v_s5_log_dt': 'new_v', 'new_v_s5_b_re': 'new_v', 'new_v_s5_b_im': 'new_v', 'new_v_s5_c_re': 'new_v', 'new_v_s5_c_im': 'new_v', 'new_v_s5_d': 'new_v', 'new_v_s5_w_glu': 'new_v', 'new_v_gla_w_alpha': 'new_v', 'new_v_gla_b_alpha': 'new_v', 'new_v_gla_norm': 'new_v', 'new_v_attn_q_norm': 'new_v', 'new_v_attn_k_norm': 'new_v', 'new_v_w_branch_s5': 'new_v', 'new_v_w_branch_gla': 'new_v', 'new_v_w_branch_attn': 'new_v', 'new_v_w_merge_gate': 'new_v', 'new_v_b_merge_gate': 'new_v', 'new_v_w_out': 'new_v', 'new_v_ffn2_norm': 'new_v', 'new_v_ffn2_w_gate': 'new_v', 'new_v_ffn2_w_up': 'new_v', 'new_v_ffn2_w_down': 'new_v', 'new_v_final_norm': 'new_v'}


def _forward(args):
    return _fwd_reference(*[args[k] for k in FWD_PARAMS])


def _output_shape():
    def fwd():
        inp = _fwd_setup_inputs(0)
        return _fwd_reference(*[inp[k] for k in FWD_PARAMS])
    out = _jax.eval_shape(fwd)
    return out.shape, out.dtype

N_MICROBATCH = 1
ADAM_LR = 0.001
ADAM_B1 = 0.9
ADAM_B2 = 0.999
ADAM_EPS = 1e-08
ADAM_WD = 0.01
ADAM_STEP = 10
PER_EXAMPLE_BATCH_AXIS = {'x': 0, 'loss_target': 0}
SHARED_INPUTS = []
_WEIGHT_DTYPES = {'ffn1_norm': _jnp.float32, 'ffn1_w_gate': _jnp.float32, 'ffn1_w_up': _jnp.float32, 'ffn1_w_down': _jnp.float32, 'mix_norm': _jnp.float32, 'w_in': _jnp.float32, 's5_lambda_re': _jnp.float32, 's5_lambda_im': _jnp.float32, 's5_log_dt': _jnp.float32, 's5_b_re': _jnp.float32, 's5_b_im': _jnp.float32, 's5_c_re': _jnp.float32, 's5_c_im': _jnp.float32, 's5_d': _jnp.float32, 's5_w_glu': _jnp.float32, 'gla_w_alpha': _jnp.float32, 'gla_b_alpha': _jnp.float32, 'gla_norm': _jnp.float32, 'attn_q_norm': _jnp.float32, 'attn_k_norm': _jnp.float32, 'w_branch_s5': _jnp.float32, 'w_branch_gla': _jnp.float32, 'w_branch_attn': _jnp.float32, 'w_merge_gate': _jnp.float32, 'b_merge_gate': _jnp.float32, 'w_out': _jnp.float32, 'ffn2_norm': _jnp.float32, 'ffn2_w_gate': _jnp.float32, 'ffn2_w_up': _jnp.float32, 'ffn2_w_down': _jnp.float32, 'final_norm': _jnp.float32}
MOMENT_SCALE = {'ffn1_norm': 8.975034e-02, 'ffn1_w_gate': 3.802947e-02, 'ffn1_w_up': 3.686782e-02, 'ffn1_w_down': 6.113379e-02, 'mix_norm': 1.272773e-01, 'w_in': 6.659453e-02, 's5_lambda_re': 2.597115e-03, 's5_lambda_im': 2.330884e-03, 's5_log_dt': 1.605293e+00, 's5_b_re': 1.573407e-03, 's5_b_im': 1.598195e-03, 's5_c_re': 3.188365e-03, 's5_c_im': 3.143106e-03, 's5_d': 5.443599e-02, 's5_w_glu': 1.366117e-02, 'gla_w_alpha': 6.084894e-03, 'gla_b_alpha': 2.483067e-02, 'gla_norm': 1.673682e-01, 'attn_q_norm': 3.815254e-02, 'attn_k_norm': 4.153133e-02, 'w_branch_s5': 3.453745e-02, 'w_branch_gla': 5.575542e-02, 'w_branch_attn': 1.048406e-02, 'w_merge_gate': 1.474344e-02, 'b_merge_gate': 1.471685e-02, 'w_out': 6.525183e-02, 'ffn2_norm': 6.938876e-02, 'ffn2_w_gate': 2.990571e-02, 'ffn2_w_up': 2.895507e-02, 'ffn2_w_down': 4.815333e-02, 'final_norm': 3.201350e+01}


def _to_microbatches(a, axis):
    t = _jnp.moveaxis(a, axis, 0)
    t = t.reshape((N_MICROBATCH, t.shape[0] // N_MICROBATCH) + t.shape[1:])
    return _jnp.moveaxis(t, 1, axis + 1)


def setup_inputs(seed: int = 0) -> dict:
    inp = _fwd_setup_inputs(seed)
    key = _jax.random.fold_in(_jax.random.key(seed), 7919)
    shape, _ = _output_shape()
    out = dict(inp)
    out["loss_target"] = _jax.random.normal(_jax.random.fold_in(key, 0), shape, _jnp.float32)
    for i, name in enumerate(TWIN_WEIGHTS):
        w = inp[name].astype(_jnp.float32)
        if MOMENT_SCALE is None:
            s = _jnp.sqrt(_jnp.mean(_jnp.square(w)) + 1e-30)
        else:
            s = MOMENT_SCALE[name]
        km, kv = _jax.random.split(_jax.random.fold_in(key, i + 1))
        out[name] = w
        out["m_" + name] = s * _jax.random.normal(km, w.shape, _jnp.float32)
        out["v_" + name] = (s * s) * _jax.random.uniform(kv, w.shape, _jnp.float32, 0.5, 1.5)
    if N_MICROBATCH > 1:
        for name, axis in PER_EXAMPLE_BATCH_AXIS.items():
            out[name] = _to_microbatches(out[name], axis)
    return {'x': out['x'], 'ffn1_norm': out['ffn1_norm'], 'ffn1_w_gate': out['ffn1_w_gate'], 'ffn1_w_up': out['ffn1_w_up'], 'ffn1_w_down': out['ffn1_w_down'], 'mix_norm': out['mix_norm'], 'w_in': out['w_in'], 's5_lambda_re': out['s5_lambda_re'], 's5_lambda_im': out['s5_lambda_im'], 's5_log_dt': out['s5_log_dt'], 's5_b_re': out['s5_b_re'], 's5_b_im': out['s5_b_im'], 's5_c_re': out['s5_c_re'], 's5_c_im': out['s5_c_im'], 's5_d': out['s5_d'], 's5_w_glu': out['s5_w_glu'], 'gla_w_alpha': out['gla_w_alpha'], 'gla_b_alpha': out['gla_b_alpha'], 'gla_norm': out['gla_norm'], 'attn_q_norm': out['attn_q_norm'], 'attn_k_norm': out['attn_k_norm'], 'w_branch_s5': out['w_branch_s5'], 'w_branch_gla': out['w_branch_gla'], 'w_branch_attn': out['w_branch_attn'], 'w_merge_gate': out['w_merge_gate'], 'b_merge_gate': out['b_merge_gate'], 'w_out': out['w_out'], 'ffn2_norm': out['ffn2_norm'], 'ffn2_w_gate': out['ffn2_w_gate'], 'ffn2_w_up': out['ffn2_w_up'], 'ffn2_w_down': out['ffn2_w_down'], 'final_norm': out['final_norm'], 'loss_target': out['loss_target'], 'm_ffn1_norm': out['m_ffn1_norm'], 'm_ffn1_w_gate': out['m_ffn1_w_gate'], 'm_ffn1_w_up': out['m_ffn1_w_up'], 'm_ffn1_w_down': out['m_ffn1_w_down'], 'm_mix_norm': out['m_mix_norm'], 'm_w_in': out['m_w_in'], 'm_s5_lambda_re': out['m_s5_lambda_re'], 'm_s5_lambda_im': out['m_s5_lambda_im'], 'm_s5_log_dt': out['m_s5_log_dt'], 'm_s5_b_re': out['m_s5_b_re'], 'm_s5_b_im': out['m_s5_b_im'], 'm_s5_c_re': out['m_s5_c_re'], 'm_s5_c_im': out['m_s5_c_im'], 'm_s5_d': out['m_s5_d'], 'm_s5_w_glu': out['m_s5_w_glu'], 'm_gla_w_alpha': out['m_gla_w_alpha'], 'm_gla_b_alpha': out['m_gla_b_alpha'], 'm_gla_norm': out['m_gla_norm'], 'm_attn_q_norm': out['m_attn_q_norm'], 'm_attn_k_norm': out['m_attn_k_norm'], 'm_w_branch_s5': out['m_w_branch_s5'], 'm_w_branch_gla': out['m_w_branch_gla'], 'm_w_branch_attn': out['m_w_branch_attn'], 'm_w_merge_gate': out['m_w_merge_gate'], 'm_b_merge_gate': out['m_b_merge_gate'], 'm_w_out': out['m_w_out'], 'm_ffn2_norm': out['m_ffn2_norm'], 'm_ffn2_w_gate': out['m_ffn2_w_gate'], 'm_ffn2_w_up': out['m_ffn2_w_up'], 'm_ffn2_w_down': out['m_ffn2_w_down'], 'm_final_norm': out['m_final_norm'], 'v_ffn1_norm': out['v_ffn1_norm'], 'v_ffn1_w_gate': out['v_ffn1_w_gate'], 'v_ffn1_w_up': out['v_ffn1_w_up'], 'v_ffn1_w_down': out['v_ffn1_w_down'], 'v_mix_norm': out['v_mix_norm'], 'v_w_in': out['v_w_in'], 'v_s5_lambda_re': out['v_s5_lambda_re'], 'v_s5_lambda_im': out['v_s5_lambda_im'], 'v_s5_log_dt': out['v_s5_log_dt'], 'v_s5_b_re': out['v_s5_b_re'], 'v_s5_b_im': out['v_s5_b_im'], 'v_s5_c_re': out['v_s5_c_re'], 'v_s5_c_im': out['v_s5_c_im'], 'v_s5_d': out['v_s5_d'], 'v_s5_w_glu': out['v_s5_w_glu'], 'v_gla_w_alpha': out['v_gla_w_alpha'], 'v_gla_b_alpha': out['v_gla_b_alpha'], 'v_gla_norm': out['v_gla_norm'], 'v_attn_q_norm': out['v_attn_q_norm'], 'v_attn_k_norm': out['v_attn_k_norm'], 'v_w_branch_s5': out['v_w_branch_s5'], 'v_w_branch_gla': out['v_w_branch_gla'], 'v_w_branch_attn': out['v_w_branch_attn'], 'v_w_merge_gate': out['v_w_merge_gate'], 'v_b_merge_gate': out['v_b_merge_gate'], 'v_w_out': out['v_w_out'], 'v_ffn2_norm': out['v_ffn2_norm'], 'v_ffn2_w_gate': out['v_ffn2_w_gate'], 'v_ffn2_w_up': out['v_ffn2_w_up'], 'v_ffn2_w_down': out['v_ffn2_w_down'], 'v_final_norm': out['v_final_norm']}


def _loss(weights, diff, rest, loss_target):
    with _jax.named_scope("forward"):
        args = {**rest, TWIN_DIFF_INPUT: diff, **{k: w.astype(_WEIGHT_DTYPES[k]) for k, w in weights.items()}}
        y = _forward(args)
    with _jax.named_scope("loss_head"):
        err = _jnp.square(y.astype(_jnp.float32) - loss_target)
        return 0.5 * _jnp.sum(_jnp.mean(err, axis=-1)) if err.ndim else 0.5 * err


def _adamw(w, g, m, v):
    m = ADAM_B1 * m + (1.0 - ADAM_B1) * g
    v = ADAM_B2 * v + (1.0 - ADAM_B2) * _jnp.square(g)
    m_hat = m / (1.0 - ADAM_B1 ** ADAM_STEP)
    v_hat = v / (1.0 - ADAM_B2 ** ADAM_STEP)
    delta = -ADAM_LR * (m_hat / (_jnp.sqrt(v_hat) + ADAM_EPS) + ADAM_WD * w)
    return delta, m, v


def reference(x, ffn1_norm, ffn1_w_gate, ffn1_w_up, ffn1_w_down, mix_norm, w_in, s5_lambda_re, s5_lambda_im, s5_log_dt, s5_b_re, s5_b_im, s5_c_re, s5_c_im, s5_d, s5_w_glu, gla_w_alpha, gla_b_alpha, gla_norm, attn_q_norm, attn_k_norm, w_branch_s5, w_branch_gla, w_branch_attn, w_merge_gate, b_merge_gate, w_out, ffn2_norm, ffn2_w_gate, ffn2_w_up, ffn2_w_down, final_norm, loss_target, m_ffn1_norm, m_ffn1_w_gate, m_ffn1_w_up, m_ffn1_w_down, m_mix_norm, m_w_in, m_s5_lambda_re, m_s5_lambda_im, m_s5_log_dt, m_s5_b_re, m_s5_b_im, m_s5_c_re, m_s5_c_im, m_s5_d, m_s5_w_glu, m_gla_w_alpha, m_gla_b_alpha, m_gla_norm, m_attn_q_norm, m_attn_k_norm, m_w_branch_s5, m_w_branch_gla, m_w_branch_attn, m_w_merge_gate, m_b_merge_gate, m_w_out, m_ffn2_norm, m_ffn2_w_gate, m_ffn2_w_up, m_ffn2_w_down, m_final_norm, v_ffn1_norm, v_ffn1_w_gate, v_ffn1_w_up, v_ffn1_w_down, v_mix_norm, v_w_in, v_s5_lambda_re, v_s5_lambda_im, v_s5_log_dt, v_s5_b_re, v_s5_b_im, v_s5_c_re, v_s5_c_im, v_s5_d, v_s5_w_glu, v_gla_w_alpha, v_gla_b_alpha, v_gla_norm, v_attn_q_norm, v_attn_k_norm, v_w_branch_s5, v_w_branch_gla, v_w_branch_attn, v_w_merge_gate, v_b_merge_gate, v_w_out, v_ffn2_norm, v_ffn2_w_gate, v_ffn2_w_up, v_ffn2_w_down, v_final_norm):
    given = dict(x=x, ffn1_norm=ffn1_norm, ffn1_w_gate=ffn1_w_gate, ffn1_w_up=ffn1_w_up, ffn1_w_down=ffn1_w_down, mix_norm=mix_norm, w_in=w_in, s5_lambda_re=s5_lambda_re, s5_lambda_im=s5_lambda_im, s5_log_dt=s5_log_dt, s5_b_re=s5_b_re, s5_b_im=s5_b_im, s5_c_re=s5_c_re, s5_c_im=s5_c_im, s5_d=s5_d, s5_w_glu=s5_w_glu, gla_w_alpha=gla_w_alpha, gla_b_alpha=gla_b_alpha, gla_norm=gla_norm, attn_q_norm=attn_q_norm, attn_k_norm=attn_k_norm, w_branch_s5=w_branch_s5, w_branch_gla=w_branch_gla, w_branch_attn=w_branch_attn, w_merge_gate=w_merge_gate, b_merge_gate=b_merge_gate, w_out=w_out, ffn2_norm=ffn2_norm, ffn2_w_gate=ffn2_w_gate, ffn2_w_up=ffn2_w_up, ffn2_w_down=ffn2_w_down, final_norm=final_norm, loss_target=loss_target, m_ffn1_norm=m_ffn1_norm, m_ffn1_w_gate=m_ffn1_w_gate, m_ffn1_w_up=m_ffn1_w_up, m_ffn1_w_down=m_ffn1_w_down, m_mix_norm=m_mix_norm, m_w_in=m_w_in, m_s5_lambda_re=m_s5_lambda_re, m_s5_lambda_im=m_s5_lambda_im, m_s5_log_dt=m_s5_log_dt, m_s5_b_re=m_s5_b_re, m_s5_b_im=m_s5_b_im, m_s5_c_re=m_s5_c_re, m_s5_c_im=m_s5_c_im, m_s5_d=m_s5_d, m_s5_w_glu=m_s5_w_glu, m_gla_w_alpha=m_gla_w_alpha, m_gla_b_alpha=m_gla_b_alpha, m_gla_norm=m_gla_norm, m_attn_q_norm=m_attn_q_norm, m_attn_k_norm=m_attn_k_norm, m_w_branch_s5=m_w_branch_s5, m_w_branch_gla=m_w_branch_gla, m_w_branch_attn=m_w_branch_attn, m_w_merge_gate=m_w_merge_gate, m_b_merge_gate=m_b_merge_gate, m_w_out=m_w_out, m_ffn2_norm=m_ffn2_norm, m_ffn2_w_gate=m_ffn2_w_gate, m_ffn2_w_up=m_ffn2_w_up, m_ffn2_w_down=m_ffn2_w_down, m_final_norm=m_final_norm, v_ffn1_norm=v_ffn1_norm, v_ffn1_w_gate=v_ffn1_w_gate, v_ffn1_w_up=v_ffn1_w_up, v_ffn1_w_down=v_ffn1_w_down, v_mix_norm=v_mix_norm, v_w_in=v_w_in, v_s5_lambda_re=v_s5_lambda_re, v_s5_lambda_im=v_s5_lambda_im, v_s5_log_dt=v_s5_log_dt, v_s5_b_re=v_s5_b_re, v_s5_b_im=v_s5_b_im, v_s5_c_re=v_s5_c_re, v_s5_c_im=v_s5_c_im, v_s5_d=v_s5_d, v_s5_w_glu=v_s5_w_glu, v_gla_w_alpha=v_gla_w_alpha, v_gla_b_alpha=v_gla_b_alpha, v_gla_norm=v_gla_norm, v_attn_q_norm=v_attn_q_norm, v_attn_k_norm=v_attn_k_norm, v_w_branch_s5=v_w_branch_s5, v_w_branch_gla=v_w_branch_gla, v_w_branch_attn=v_w_branch_attn, v_w_merge_gate=v_w_merge_gate, v_b_merge_gate=v_b_merge_gate, v_w_out=v_w_out, v_ffn2_norm=v_ffn2_norm, v_ffn2_w_gate=v_ffn2_w_gate, v_ffn2_w_up=v_ffn2_w_up, v_ffn2_w_down=v_ffn2_w_down, v_final_norm=v_final_norm)
    weights = {n: given[n] for n in TWIN_WEIGHTS}
    shared = {n: given[n] for n in SHARED_INPUTS}
    per_example = {n: given[n] for n in ['x']}
    grad_fn = _jax.value_and_grad(_loss, argnums=(0, 1))

    def one_microbatch(ex, loss_target):
        ex = dict(ex)
        diff = ex.pop(TWIN_DIFF_INPUT)
        return grad_fn(weights, diff, {**shared, **ex}, loss_target)

    if N_MICROBATCH == 1:
        loss, (grad_w, grad_x) = one_microbatch(per_example, given["loss_target"])
    else:
        def body(carry, xs):
            loss_sum, grad_sum = carry
            l_k, (gw_k, gx_k) = one_microbatch(xs[0], xs[1])
            with _jax.named_scope("update"):
                return (loss_sum + l_k, _jax.tree.map(_jnp.add, grad_sum, gw_k)), gx_k

        init = (_jnp.zeros((), _jnp.float32), _jax.tree.map(_jnp.zeros_like, weights))
        (loss, grad_w), grad_x = _jax.lax.scan(body, init, (per_example, given["loss_target"]))
    with _jax.named_scope("update"):
        delta_w, new_m, new_v = {}, {}, {}
        for n in TWIN_WEIGHTS:
            delta_w[n], new_m[n], new_v[n] = _adamw(weights[n], grad_w[n], given["m_" + n], given["v_" + n])
    return (loss, grad_x, *[grad_w[n] for n in TWIN_WEIGHTS], *[delta_w[n] for n in TWIN_WEIGHTS],
            *[new_m[n] for n in TWIN_WEIGHTS], *[new_v[n] for n in TWIN_WEIGHTS])
```

```python
import functools
import math

import jax
import jax.numpy as jnp
import numpy as np
from jax import lax
from jax.experimental import pallas as pl
from jax.experimental.pallas import tpu as pltpu

F32 = jnp.float32
BF16 = jnp.bfloat16
HI = lax.Precision.HIGHEST

D_MODEL = 1024
D_FF = 2816
DEPTH = 2
EPS = 1e-6
S5_G, S5_H, S5_P = 32, 16, 64
S5_W = S5_G * S5_H
S5_N = S5_G * S5_P
GLA_H, GLA_D = 4, 128
GLA_W = GLA_H * GLA_D
GLA_R = 16
GLA_TAU = 16.0
GLA_C = 64
AT_QH, AT_KH, AT_D = 8, 2, 64
AT_W = AT_QH * AT_D
AT_KW = AT_KH * AT_D
GRID_W = 64
ROPE_BASE = 10000.0
ZP_W = 3456
ZC_GQ, ZC_GK, ZC_GV, ZC_AK, ZC_AV, ZC_Z = 4, 8, 12, 24, 25, 26

ADAM_LR, ADAM_B1, ADAM_B2, ADAM_EPS, ADAM_WD, ADAM_STEP = 0.001, 0.9, 0.999, 1e-08, 0.01, 10

VMEM_LIMIT_V7X = 56 * 1024 * 1024
LANE = 128
SUBLANE = 8

W_NAMES = ['ffn1_norm', 'ffn1_w_gate', 'ffn1_w_up', 'ffn1_w_down', 'mix_norm', 'w_in', 's5_lambda_re', 's5_lambda_im',
           's5_log_dt', 's5_b_re', 's5_b_im', 's5_c_re', 's5_c_im', 's5_d', 's5_w_glu', 'gla_w_alpha', 'gla_b_alpha',
           'gla_norm', 'attn_q_norm', 'attn_k_norm', 'w_branch_s5', 'w_branch_gla', 'w_branch_attn', 'w_merge_gate',
           'b_merge_gate', 'w_out', 'ffn2_norm', 'ffn2_w_gate', 'ffn2_w_up', 'ffn2_w_down', 'final_norm']
SHARD_AXIS = {'ffn1_w_gate': 2, 'ffn1_w_up': 2, 'ffn1_w_down': 1, 'w_in': 2, 's5_w_glu': 1, 'gla_w_alpha': 3,
              'gla_b_alpha': 2, 'w_branch_s5': 2, 'w_branch_gla': 2, 'w_branch_attn': 2, 'w_merge_gate': 2,
              'w_out': 1, 'ffn2_w_gate': 2, 'ffn2_w_up': 2, 'ffn2_w_down': 1}
F32_GATHER = ['gla_w_alpha', 'gla_b_alpha']
BF16_GATHER = [n for n in W_NAMES if n in SHARD_AXIS and n not in F32_GATHER]
SHARDED = [n for n in W_NAMES if n in SHARD_AXIS]
REPLICATED = [n for n in W_NAMES if n not in SHARD_AXIS]
N_CHIPS = 4
MESH = pl.DeviceIdType.MESH


def _tile(n, target, mult):
    best = None
    for t in range(mult, min(n, target) + 1, mult):
        if n % t == 0:
            best = t
    return best if best is not None else n


def _cparams(sem, vmem_bytes):
    limit = int(min(VMEM_LIMIT_V7X, max(32 * 1024 * 1024, vmem_bytes * 5 // 4)))
    return pltpu.CompilerParams(dimension_semantics=sem, vmem_limit_bytes=limit)


def _cw(arr, cb, width):
    return (arr, cb, width)


def _win(a):
    if isinstance(a, tuple):
        return a
    return (a, 0, a.shape[-1])


def _nbytes(shape, dtype):
    return int(np.prod(shape)) * jnp.dtype(dtype).itemsize


def _rowwise(name, fn, rows, fulls, outs, accs=(), tl=256):
    rows = [_win(r) for r in rows]
    n_tok = rows[0][0].shape[0]
    tl = _tile(n_tok, tl, 16)
    nr, nf, no = len(rows), len(fulls), len(outs)

    def body(*refs):
        vals = [r[...] for r in refs[:nr + nf]]
        res = fn(*vals)
        if not isinstance(res, (tuple, list)):
            res = (res,)
        for k in range(no):
            o = refs[nr + nf + k]
            o[...] = res[k].astype(o.dtype)
        if accs:
            @pl.when(pl.program_id(0) == 0)
            def _():
                for k in range(len(accs)):
                    a = refs[nr + nf + no + k]
                    a[...] = jnp.zeros(a.shape, a.dtype)
            for k in range(len(accs)):
                a = refs[nr + nf + no + k]
                a[...] += res[no + k]

    in_specs = [pl.BlockSpec((tl, w), functools.partial(lambda i, cb: (i, cb), cb=cb)) for (_, cb, w) in rows]
    in_specs += [pl.BlockSpec(f.shape, functools.partial(lambda i, nd: (0,) * nd, nd=f.ndim)) for f in fulls]
    out_specs = [pl.BlockSpec((tl, c), lambda i: (i, 0)) for (c, _) in outs]
    out_specs += [pl.BlockSpec((r, c), lambda i: (0, 0)) for (r, c) in accs]
    out_shape = [jax.ShapeDtypeStruct((n_tok, c), dt) for (c, dt) in outs]
    out_shape += [jax.ShapeDtypeStruct((r, c), F32) for (r, c) in accs]
    vm = 2 * sum(_nbytes((tl, w), a.dtype) for (a, _, w) in rows)
    vm += 2 * sum(_nbytes(f.shape, f.dtype) for f in fulls)
    vm += 2 * sum(_nbytes((tl, c), dt) for (c, dt) in outs)
    vm += 8 * max([_nbytes((tl, w), F32) for (_, _, w) in rows] + [_nbytes((tl, c), F32) for (c, _) in outs])
    return pl.pallas_call(
        body, name=name, grid=(n_tok // tl,), in_specs=in_specs, out_specs=out_specs, out_shape=out_shape,
        compiler_params=_cparams(("arbitrary",) if accs else ("parallel",), vm),
    )(*[r[0] for r in rows], *fulls)


def _mm(name, a, b, *, ta=False, tb=False, out_dtype=F32, alpha=1.0, add=None, tm=512, tn=512):
    a_arr, a_cb, a_w = _win(a)
    b_arr, b_cb, b_w = _win(b)
    if ta:
        kdim, m = a_arr.shape[0], a_w
    else:
        m, kdim = a_arr.shape[0], a_w
    if tb:
        n, kb = b_arr.shape[0], b_w
    else:
        kb, n = b_arr.shape[0], b_w
    assert kdim == kb, (name, kdim, kb)
    tm = _tile(m, tm, LANE if ta else 16)
    tn = _tile(n, tn, LANE)
    if ta:
        a_spec = pl.BlockSpec((kdim, tm), lambda i, j: (0, a_cb * (a_w // tm) + i))
        a_blk = (kdim, tm)
    else:
        a_spec = pl.BlockSpec((tm, kdim), lambda i, j: (i, a_cb))
        a_blk = (tm, kdim)
    if tb:
        b_spec = pl.BlockSpec((tn, kdim), lambda i, j: (j, b_cb))
        b_blk = (tn, kdim)
    else:
        b_spec = pl.BlockSpec((kdim, tn), lambda i, j: (0, b_cb * (b_w // tn) + j))
        b_blk = (kdim, tn)
    dims = (((0 if ta else 1,), (1 if tb else 0,)), ((), ()))
    has_add = add is not None

    def body(*refs):
        a_ref, b_ref = refs[0], refs[1]
        o_ref = refs[-1]
        acc = lax.dot_general(a_ref[...].astype(BF16), b_ref[...].astype(BF16), dims, preferred_element_type=F32)
        if alpha != 1.0:
            acc = acc * alpha
        if has_add:
            acc = acc + refs[2][...].astype(F32)
        o_ref[...] = acc.astype(o_ref.dtype)

    in_specs = [a_spec, b_spec]
    ops = [a_arr, b_arr]
    if has_add:
        in_specs.append(pl.BlockSpec((tm, tn), lambda i, j: (i, j)))
        ops.append(add)
    vm = 2 * (_nbytes(a_blk, a_arr.dtype) + _nbytes(b_blk, b_arr.dtype) + _nbytes((tm, tn), out_dtype))
    vm += _nbytes(a_blk, BF16) + _nbytes(b_blk, BF16) + 3 * _nbytes((tm, tn), F32)
    return pl.pallas_call(
        body, name=name, grid=(m // tm, n // tn), in_specs=in_specs,
        out_specs=pl.BlockSpec((tm, tn), lambda i, j: (i, j)),
        out_shape=jax.ShapeDtypeStruct((m, n), out_dtype),
        compiler_params=_cparams(("parallel", "parallel"), vm),
    )(*ops)


def _sigmoid(x):
    return 1.0 / (1.0 + jnp.exp(-x))


def _rms(x, gain):
    r = lax.rsqrt(jnp.mean(x * x, axis=-1, keepdims=True) + EPS)
    return x * r * gain


def _rms_bwd(x, gain, dh):
    r = lax.rsqrt(jnp.mean(x * x, axis=-1, keepdims=True) + EPS)
    xh = x * r
    dg = jnp.sum(dh * xh, axis=0, keepdims=True)
    dxh = dh * gain
    dx = r * (dxh - xh * jnp.mean(dxh * xh, axis=-1, keepdims=True))
    return dx, dg


_GELU_C = math.sqrt(2.0 / math.pi)


def _gelu(y):
    return 0.5 * y * (1.0 + jnp.tanh(_GELU_C * (y + 0.044715 * y * y * y)))


def _gelu_grad(y):
    th = jnp.tanh(_GELU_C * (y + 0.044715 * y * y * y))
    return 0.5 * (1.0 + th) + 0.5 * y * (1.0 - th * th) * _GELU_C * (1.0 + 3.0 * 0.044715 * y * y)


def _log_sigmoid(x):
    return jnp.minimum(x, 0.0) - jnp.log(1.0 + jnp.exp(-jnp.abs(x)))


def _dot(a, b, dims=(((1,), (0,)), ((), ())), precision=None):
    return lax.dot_general(a, b, dims, preferred_element_type=F32, precision=precision)


_NT = (((1,), (1,)), ((), ()))
_TN = (((0,), (0,)), ((), ()))


def _scan(name, b, a_re, a_im, *, rev, xin=None, tb=256, lc=512):
    n_tok = b.shape[0]
    tb = _tile(n_tok, tb, 16)
    nt, nb, nc = n_tok // tb, tb // SUBLANE, S5_N // lc
    with_da = xin is not None

    def body(*refs):
        if with_da:
            b_ref, are_ref, aim_ref, x_ref, o_ref, da_ref, pw, last, of32, xf32 = refs
        else:
            b_ref, are_ref, aim_ref, o_ref, pw, last, of32 = refs
        t = pl.program_id(0)
        row = lax.broadcasted_iota(jnp.int32, (SUBLANE, lc), 0)

        def cmul(ar, ai, xr, xi):
            return ar * xr - ai * xi, ar * xi + ai * xr

        def hs(c, xr, xi):
            cs = pl.ds(c * lc, lc)
            for k, slot in ((1, 0), (2, 2), (4, 4)):
                ar, ai = pw[slot, :, cs], pw[slot + 1, :, cs]
                if rev:
                    sr, si = pltpu.roll(xr, SUBLANE - k, 0), pltpu.roll(xi, SUBLANE - k, 0)
                    keep = row < SUBLANE - k
                else:
                    sr, si = pltpu.roll(xr, k, 0), pltpu.roll(xi, k, 0)
                    keep = row >= k
                sr, si = jnp.where(keep, sr, 0.0), jnp.where(keep, si, 0.0)
                pr, pi = cmul(ar, ai, sr, si)
                xr, xi = xr + pr, xi + pi
            return xr, xi

        @pl.when(t == 0)
        def _():
            last[...] = jnp.zeros(last.shape, F32)
            if with_da:
                da_ref[...] = jnp.zeros(da_ref.shape, F32)
            for c in range(nc):
                cs = pl.ds(c * lc, lc)
                a1r = jnp.broadcast_to(are_ref[:, cs], (SUBLANE, lc))
                a1i = jnp.broadcast_to(aim_ref[:, cs], (SUBLANE, lc))
                a2r, a2i = cmul(a1r, a1i, a1r, a1i)
                a4r, a4i = cmul(a2r, a2i, a2r, a2i)
                for slot, v in enumerate((a1r, a1i, a2r, a2i, a4r, a4i)):
                    pw[slot, :, cs] = v
                first = SUBLANE - 1 if rev else 0
                pr, pi = hs(c, jnp.where(row == first, a1r, 0.0), jnp.where(row == first, a1i, 0.0))
                pw[6, :, cs] = pr
                pw[7, :, cs] = pi

        if with_da:
            xf32[...] = x_ref[...].astype(F32)
        edge = 0 if rev else SUBLANE - 1

        def blk(j, carry):
            jj = (nb - 1 - j) if rev else j
            r0 = pl.multiple_of(jj * SUBLANE, SUBLANE)
            for c in range(nc):
                cre, cim = pl.ds(c * lc, lc), pl.ds(S5_N + c * lc, lc)
                xr, xi = hs(c, b_ref[pl.ds(r0, SUBLANE), cre], b_ref[pl.ds(r0, SUBLANE), cim])
                cr = jnp.broadcast_to(last[edge:edge + 1, cre], (SUBLANE, lc))
                ci = jnp.broadcast_to(last[edge:edge + 1, cim], (SUBLANE, lc))
                pr, pi = cmul(pw[6, :, cre], pw[7, :, cre], cr, ci)
                xr, xi = xr + pr, xi + pi
                if with_da:
                    if rev:
                        qr, qi = pltpu.roll(xr, SUBLANE - 1, 0), pltpu.roll(xi, SUBLANE - 1, 0)
                        fill = row == SUBLANE - 1
                    else:
                        qr, qi = pltpu.roll(xr, 1, 0), pltpu.roll(xi, 1, 0)
                        fill = row == 0
                    qr, qi = jnp.where(fill, cr, qr), jnp.where(fill, ci, qi)
                    zr, zi = xf32[pl.ds(r0, SUBLANE), cre], xf32[pl.ds(r0, SUBLANE), cim]
                    da_ref[:, cre] += zr * qr + zi * qi
                    da_ref[:, cim] += zr * qi - zi * qr
                last[:, cre] = xr
                last[:, cim] = xi
                of32[pl.ds(r0, SUBLANE), cre] = xr
                of32[pl.ds(r0, SUBLANE), cim] = xi
            return carry

        lax.fori_loop(0, nb, blk, 0)
        o_ref[...] = of32[...].astype(o_ref.dtype)

    tmap = (lambda t: (nt - 1 - t, 0)) if rev else (lambda t: (t, 0))
    in_specs = [pl.BlockSpec((tb, 2 * S5_N), tmap), pl.BlockSpec((1, S5_N), lambda t: (0, 0)),
                pl.BlockSpec((1, S5_N), lambda t: (0, 0))]
    ops = [b, a_re, a_im]
    out_specs = [pl.BlockSpec((tb, 2 * S5_N), tmap)]
    out_shape = [jax.ShapeDtypeStruct((n_tok, 2 * S5_N), BF16)]
    scratch = [pltpu.VMEM((8, SUBLANE, S5_N), F32), pltpu.VMEM((SUBLANE, 2 * S5_N), F32),
               pltpu.VMEM((tb, 2 * S5_N), F32)]
    if with_da:
        in_specs.append(pl.BlockSpec((tb, 2 * S5_N), tmap))
        ops.append(xin)
        out_specs.append(pl.BlockSpec((SUBLANE, 2 * S5_N), lambda t: (0, 0)))
        out_shape.append(jax.ShapeDtypeStruct((SUBLANE, 2 * S5_N), F32))
        scratch.append(pltpu.VMEM((tb, 2 * S5_N), F32))
    vm = 2 * _nbytes((tb, 2 * S5_N), F32) + 4 * _nbytes((tb, 2 * S5_N), BF16) + 2 * _nbytes((tb, 2 * S5_N), F32)
    res = pl.pallas_call(
        body, name=name, grid=(nt,), in_specs=in_specs, out_specs=out_specs, out_shape=out_shape,
        scratch_shapes=scratch, compiler_params=_cparams(("arbitrary",), vm),
    )(*ops)
    return res if with_da else res[0]


def _gla_chunk_terms(la, q, k, rev):
    ri = lax.broadcasted_iota(jnp.int32, (GLA_C, GLA_C), 0)
    ci = lax.broadcasted_iota(jnp.int32, (GLA_C, GLA_C), 1)
    tri = (ri <= ci) if rev else (ri >= ci)
    trif = tri.astype(F32)
    bcum = _dot(trif, la, precision=HI)
    blast = jnp.sum(la, axis=0, keepdims=True)
    eb, enb = jnp.exp(bcum), jnp.exp(-bcum)
    ee = jnp.exp(blast - bcum)
    return tri, trif, blast, eb, enb, ee, q * eb, k * enb, k * ee


def _gla_fwd(name, z, la, *, rev):
    n_tok = z.shape[0]
    nck = n_tok // GLA_C
    scale = GLA_D ** -0.5

    def body(q_ref, k_ref, v_ref, la_ref, o_ref, s_ref, st):
        n = pl.program_id(1)

        @pl.when(n == 0)
        def _():
            st[...] = jnp.zeros(st.shape, F32)

        v = v_ref[...].astype(BF16)
        tri, _, blast, _, _, _, qd, kd, ke = _gla_chunk_terms(la_ref[...], q_ref[...] * scale, k_ref[...], rev)
        qd16 = qd.astype(BF16)
        sc = jnp.where(tri, _dot(qd16, kd.astype(BF16), _NT), 0.0)
        s16 = st[...].astype(BF16)
        s_ref[0, 0] = s16
        o_ref[...] = _dot(sc.astype(BF16), v) + _dot(qd16, s16, _NT)
        st[...] = st[...] * jnp.exp(blast) + _dot(v, ke.astype(BF16), _TN)

    cmap = (lambda h, n: nck - 1 - n) if rev else (lambda h, n: n)

    def zspec(cb):
        return pl.BlockSpec((GLA_C, GLA_D), lambda h, n: (cmap(h, n), cb + h))

    return pl.pallas_call(
        body, name=name, grid=(GLA_H, nck),
        in_specs=[zspec(ZC_GQ), zspec(ZC_GK), zspec(ZC_GV), zspec(0)],
        out_specs=[zspec(0), pl.BlockSpec((1, 1, GLA_D, GLA_D), lambda h, n: (h, n, 0, 0))],
        out_shape=[jax.ShapeDtypeStruct((n_tok, GLA_W), F32),
                   jax.ShapeDtypeStruct((GLA_H, nck, GLA_D, GLA_D), BF16)],
        scratch_shapes=[pltpu.VMEM((GLA_D, GLA_D), F32)],
        compiler_params=_cparams(("parallel", "arbitrary"), 4 * 1024 * 1024),
    )(z, z, z, la)


def _gla_bwd(name, z, la, s_prev, do, *, rev):
    n_tok = z.shape[0]
    nck = n_tok // GLA_C
    scale = GLA_D ** -0.5

    def body(q_ref, k_ref, v_ref, la_ref, s_ref, do_ref, dq_ref, dk_ref, dv_ref, dla_ref, dst):
        n = pl.program_id(1)

        @pl.when(n == 0)
        def _():
            dst[...] = jnp.zeros(dst.shape, F32)

        v = v_ref[...].astype(BF16)
        do16 = do_ref[...].astype(BF16)
        tri, trif, blast, eb, enb, ee, qd, kd, ke = _gla_chunk_terms(la_ref[...], q_ref[...] * scale, k_ref[...], rev)
        qd16, kd16, ke16 = qd.astype(BF16), kd.astype(BF16), ke.astype(BF16)
        sc16 = jnp.where(tri, _dot(qd16, kd16, _NT), 0.0).astype(BF16)
        dsc16 = jnp.where(tri, _dot(do16, v, _NT), 0.0).astype(BF16)
        s16 = s_ref[0, 0]
        dsn = dst[...]
        dsn16 = dsn.astype(BF16)
        dec = jnp.exp(blast)
        dqd = _dot(dsc16, kd16) + _dot(do16, s16)
        dkd = _dot(dsc16, qd16, _TN)
        dke = _dot(v, dsn16)
        dv_ref[...] = _dot(sc16, do16, _TN) + _dot(ke16, dsn16, _NT)
        ddec = jnp.sum(dsn * s16.astype(F32), axis=0, keepdims=True)
        dst[...] = dsn * dec + _dot(do16, qd16, _TN)
        dq_ref[...] = dqd * eb * scale
        dk_ref[...] = dkd * enb + dke * ee
        db = dqd * qd - dkd * kd - dke * ke
        dblast = jnp.sum(dke * ke, axis=0, keepdims=True) + ddec * dec
        dla_ref[...] = _dot(trif, db, _TN, precision=HI) + dblast

    cmap = (lambda h, n: n) if rev else (lambda h, n: nck - 1 - n)

    def zspec(cb):
        return pl.BlockSpec((GLA_C, GLA_D), lambda h, n: (cmap(h, n), cb + h))

    o4 = jax.ShapeDtypeStruct((n_tok, GLA_W), F32)
    return pl.pallas_call(
        body, name=name, grid=(GLA_H, nck),
        in_specs=[zspec(ZC_GQ), zspec(ZC_GK), zspec(ZC_GV), zspec(0),
                  pl.BlockSpec((1, 1, GLA_D, GLA_D), lambda h, n: (h, nck - 1 - n, 0, 0)), zspec(0)],
        out_specs=[zspec(0)] * 4, out_shape=[o4] * 4,
        scratch_shapes=[pltpu.VMEM((GLA_D, GLA_D), F32)],
        compiler_params=_cparams(("parallel", "arbitrary"), 4 * 1024 * 1024),
    )(z, z, z, la, s_prev, do)


def _attn_fwd(name, q, k, v, tq=256):
    qh, n_tok, d = q.shape
    grp = qh // k.shape[0]
    tq = _tile(n_tok, tq, 16)

    def body(q_ref, k_ref, v_ref, o_ref, l_ref):
        s = _dot(q_ref[0], k_ref[0], _NT)
        m = jnp.max(s, axis=-1, keepdims=True)
        p = jnp.exp(s - m)
        den = jnp.sum(p, axis=-1, keepdims=True)
        o_ref[0] = _dot(p.astype(BF16), v_ref[0]) / den
        l_ref[0] = m + jnp.log(den)

    vm = 4 * _nbytes((tq, n_tok), F32) + 8 * _nbytes((n_tok, LANE), BF16)
    return pl.pallas_call(
        body, name=name, grid=(qh, n_tok // tq),
        in_specs=[pl.BlockSpec((1, tq, d), lambda h, i: (h, i, 0)),
                  pl.BlockSpec((1, n_tok, d), lambda h, i: (h // grp, 0, 0)),
                  pl.BlockSpec((1, n_tok, d), lambda h, i: (h // grp, 0, 0))],
        out_specs=[pl.BlockSpec((1, tq, d), lambda h, i: (h, i, 0)), pl.BlockSpec((1, tq, 1), lambda h, i: (h, i, 0))],
        out_shape=[jax.ShapeDtypeStruct((qh, n_tok, d), F32), jax.ShapeDtypeStruct((qh, n_tok, 1), F32)],
        compiler_params=_cparams(("parallel", "parallel"), vm),
    )(q, k, v)


def _attn_bwd(name, q, k, v, o, lse, do, tq=256):
    qh, n_tok, d = q.shape
    kh = k.shape[0]
    grp = qh // kh
    tq = _tile(n_tok, tq, 16)
    nq = n_tok // tq

    def body(q_ref, k_ref, v_ref, o_ref, l_ref, do_ref, dq_ref, dk_ref, dv_ref):
        @pl.when((pl.program_id(1) == 0) & (pl.program_id(2) == 0))
        def _():
            dk_ref[...] = jnp.zeros(dk_ref.shape, F32)
            dv_ref[...] = jnp.zeros(dv_ref.shape, F32)

        q16, k16, v16 = q_ref[0], k_ref[0], v_ref[0]
        dof = do_ref[0]
        do16 = dof.astype(BF16)
        p = jnp.exp(_dot(q16, k16, _NT) - l_ref[0])
        dp = _dot(do16, v16, _NT)
        delta = jnp.sum(dof * o_ref[0], axis=-1, keepdims=True)
        ds16 = (p * (dp - delta)).astype(BF16)
        dq_ref[0] = _dot(ds16, k16)
        dk_ref[0] += _dot(ds16, q16, _TN)
        dv_ref[0] += _dot(p.astype(BF16), do16, _TN)

    qspec = pl.BlockSpec((1, tq, d), lambda g, j, i: (g * grp + j, i, 0))
    kspec = pl.BlockSpec((1, n_tok, d), lambda g, j, i: (g, 0, 0))
    vm = 6 * _nbytes((tq, n_tok), F32) + 12 * _nbytes((n_tok, LANE), F32)
    return pl.pallas_call(
        body, name=name, grid=(kh, grp, nq),
        in_specs=[qspec, kspec, kspec, qspec, pl.BlockSpec((1, tq, 1), lambda g, j, i: (g * grp + j, i, 0)), qspec],
        out_specs=[qspec, kspec, kspec],
        out_shape=[jax.ShapeDtypeStruct((qh, n_tok, d), F32), jax.ShapeDtypeStruct((kh, n_tok, d), F32),
                   jax.ShapeDtypeStruct((kh, n_tok, d), F32)],
        compiler_params=_cparams(("parallel", "arbitrary", "arbitrary"), vm),
    )(q, k, v, o, lse, do)


def _block_diag(t):
    g, a, b = t.shape
    eye = jnp.eye(g, dtype=t.dtype)
    return (t[:, :, None, :] * eye[:, None, :, None]).reshape(g * a, g * b)


def _s5_prep(lam_re, lam_im, log_dt, b_re, b_im, c_re, c_im):
    dt = jnp.exp(log_dt)[:, None]
    er = jnp.exp(lam_re * dt)
    a_re, a_im = er * jnp.cos(lam_im * dt), er * jnp.sin(lam_im * dt)
    nr, ni = a_re - 1.0, a_im
    den = lam_re * lam_re + lam_im * lam_im
    f_re, f_im = (nr * lam_re + ni * lam_im) / den, (ni * lam_re - nr * lam_im) / den
    bb_re = f_re[..., None] * b_re - f_im[..., None] * b_im
    bb_im = f_re[..., None] * b_im + f_im[..., None] * b_re
    w_in = jnp.concatenate([_block_diag(bb_re.transpose(0, 2, 1)), _block_diag(bb_im.transpose(0, 2, 1))], axis=1)
    w_out = jnp.concatenate([_block_diag(c_re.transpose(0, 2, 1)), -_block_diag(c_im.transpose(0, 2, 1))], axis=0)
    return a_re.reshape(1, S5_N), a_im.reshape(1, S5_N), w_in, w_out


def _rope_tables(n_tok):
    quarter = AT_D // 4
    d = np.arange(AT_D)
    e = d % (AT_D // 2)
    inv = (ROPE_BASE ** (-(e % quarter).astype(np.float64) * 2.0 / (AT_D // 2))).astype(np.float32)
    sign = np.where(e < quarter, -1.0, 1.0).astype(np.float32)
    rot = np.zeros((AT_D, AT_D), np.float32)
    rot[np.where(e < quarter, d + quarter, d - quarter), d] = 1.0
    pos = jnp.arange(n_tok, dtype=jnp.int32)
    axis_pos = jnp.where(jnp.asarray(d // (AT_D // 2) == 0)[None, :], (pos // GRID_W)[:, None], (pos % GRID_W)[:, None])
    ang = axis_pos.astype(F32) * jnp.asarray(inv)[None, :]
    return jnp.cos(ang), jnp.sin(ang) * jnp.asarray(sign)[None, :], rot


def _head_consts(n_tok, heads):
    cos, sin, rot = _rope_tables(n_tok)
    eye = np.eye(heads, dtype=np.float32)
    mean = np.kron(eye, np.full((AT_D, AT_D), 1.0 / AT_D, np.float32))
    return jnp.tile(cos, (1, heads)), jnp.tile(sin, (1, heads)), jnp.asarray(mean), jnp.asarray(np.kron(eye, rot))


def _pad_w_in(w):
    return jnp.concatenate([w[:, :2560], w[:, 2592:3360], w[:, 2560:2592], jnp.zeros((w.shape[0], 96), w.dtype)], axis=1)


def _unpad_w_in(g):
    return jnp.concatenate([g[:, :2560], g[:, 3328:3360], g[:, 2560:3328]], axis=1)


def _ffn_fwd(x, g, wg, wu, wd):
    h = _rowwise("ffn_norm", lambda xv, gv: _rms(xv, gv), [x], [g], [(D_MODEL, BF16)])[0]
    a = _mm("ffn_gate", h, wg, out_dtype=BF16)
    b = _mm("ffn_up", h, wu, out_dtype=BF16)

    def act(av, bv):
        av, bv = av.astype(F32), bv.astype(F32)
        return av * _sigmoid(av) * bv

    s = _rowwise("ffn_act", act, [a, b], [], [(D_FF, BF16)])[0]
    return _mm("ffn_down", s, wd, alpha=0.5, add=x), (x, h, a, b, s)


def _ffn_bwd(dy, saved, g, wg, wu, wd):
    x, h, a, b, s = saved
    ds = _mm("ffn_ds", dy, wd, tb=True, alpha=0.5, out_dtype=BF16)

    def dact(dsv, av, bv):
        dsv, av, bv = dsv.astype(F32), av.astype(F32), bv.astype(F32)
        sg = _sigmoid(av)
        return dsv * bv * sg * (1.0 + av * (1.0 - sg)), dsv * av * sg

    da, db = _rowwise("ffn_dact", dact, [ds, a, b], [], [(D_FF, BF16), (D_FF, BF16)])
    dwd = _mm("ffn_dwd", s, dy, ta=True, alpha=0.5)
    dwg = _mm("ffn_dwg", h, da, ta=True)
    dwu = _mm("ffn_dwu", h, db, ta=True)
    dh = _mm("ffn_dh_gate", da, wg, tb=True)
    dh = _mm("ffn_dh_up", db, wu, tb=True, add=dh)

    def dnorm(dyv, xv, dhv, gv):
        dx, dg = _rms_bwd(xv, gv, dhv)
        return dyv + dx, dg

    dx, dg = _rowwise("ffn_dnorm", dnorm, [dy, x, dh], [g], [(D_MODEL, F32)], accs=[(1, D_MODEL)])
    return dx, dict(norm=dg[0], w_gate=dwg, w_up=dwu, w_down=dwd)


def _alpha_pads(w_alpha):
    z = jnp.zeros((LANE - 2 * GLA_R, GLA_W), F32)
    wf = jnp.concatenate([w_alpha[0], jnp.zeros((GLA_R, GLA_W), F32), z], axis=0)
    wb = jnp.concatenate([jnp.zeros((GLA_R, GLA_W), F32), w_alpha[1], z], axis=0)
    return wf, wb


def _heads(t, nh):
    return t.reshape(t.shape[0], nh, AT_D).transpose(1, 0, 2)


def _unheads(t):
    return t.transpose(1, 0, 2).reshape(t.shape[1], t.shape[0] * AT_D)


def _mixer_fwd(x, w):
    n_tok = x.shape[0]
    h = _rowwise("mix_norm", lambda xv, gv: _rms(xv, gv), [x], [w['mix_norm']], [(D_MODEL, BF16)])[0]
    z = _mm("mix_in", h, w['w_in_p'])
    gp = _mm("mix_gate", h, w['w_merge_gate'], out_dtype=BF16)
    u = _cw(z, 0, S5_W)

    xs, yy = [], None
    for d, rev in ((0, False), (1, True)):
        a_re, a_im, wb, wc = w['s5'][d]
        bu = _mm("s5_bu", u, wb)
        xs.append(_scan("s5_scan_rev" if rev else "s5_scan_fwd", bu, a_re, a_im, rev=rev))
        yy = _mm("s5_y", xs[d], wc, add=yy)

    def s5_post(yv, uv, dv, wglu):
        y0 = yv + dv * uv
        y1 = _gelu(y0)
        t = _dot(y1.astype(BF16), wglu)
        return y0, t, y1 * _sigmoid(t)

    y0, t_glu, y_s5 = _rowwise("s5_post", s5_post, [yy, u], [w['s5_d'], w['s5_w_glu']],
                               [(S5_W, F32), (S5_W, F32), (S5_W, BF16)])

    wa_f, wa_b = _alpha_pads(w['gla_w_alpha'])
    ba_f, ba_b = w['gla_b_alpha'][0:1], w['gla_b_alpha'][1:2]
    zz = _cw(z, ZC_Z, LANE)

    def gla_prep(zv, waf, wab, bf, bb):
        z16 = zv.astype(BF16)
        lf = _dot(z16, waf.astype(BF16)) + bf
        lb = _dot(z16, wab.astype(BF16)) + bb
        return _log_sigmoid(lf) / GLA_TAU, _log_sigmoid(lb) / GLA_TAU

    la_f, la_b = _rowwise("gla_prep", gla_prep, [zz], [wa_f, wa_b, ba_f, ba_b], [(GLA_W, F32), (GLA_W, F32)])
    o_f, s_f = _gla_fwd("gla_fwd", z, la_f, rev=False)
    o_b, s_b = _gla_fwd("gla_rev", z, la_b, rev=True)
    gate = _cw(z, 4, GLA_W)

    def gla_post(of, ob, gv, gn):
        o = of + ob
        outs = []
        for hh in range(GLA_H):
            sl = slice(hh * GLA_D, (hh + 1) * GLA_D)
            gh = gv[:, sl]
            outs.append(_rms(o[:, sl], gn) * (gh * _sigmoid(gh)))
        return jnp.concatenate(outs, axis=1)

    y_gla = _rowwise("gla_post", gla_post, [o_f, o_b, gate], [w['gla_norm']], [(GLA_W, BF16)])[0]

    cq, sq, mq, rq = _head_consts(n_tok, AT_QH)
    ck, sk, mk, rk = _head_consts(n_tok, AT_KH)
    gq = jnp.tile(w['attn_q_norm'], (1, AT_QH))
    gk = jnp.tile(w['attn_k_norm'], (1, AT_KH))

    def rope_fwd(v, gain, cos, sin, mean, rot, scale):
        r = lax.rsqrt(_dot(v * v, mean, precision=HI) + EPS)
        vn = v * r * gain
        return (vn * cos + _dot(vn, rot, precision=HI) * sin) * scale

    def attn_prep(qv, kv, cqv, sqv, ckv, skv, gqv, gkv, mqv, rqv, mkv, rkv):
        return rope_fwd(qv, gqv, cqv, sqv, mqv, rqv, AT_D ** -0.5), rope_fwd(kv, gkv, ckv, skv, mkv, rkv, 1.0)

    q_raw, k_raw = _cw(z, 5, AT_W), _cw(z, ZC_AK, AT_KW)
    q16, k16 = _rowwise("attn_prep", attn_prep, [q_raw, k_raw, cq, sq, ck, sk], [gq, gk, mq, rq, mk, rk],
                        [(AT_W, BF16), (AT_KW, BF16)])
    v16 = z[:, ZC_AV * LANE:(ZC_AV + 1) * LANE].astype(BF16)
    qh, kh, vh = _heads(q16, AT_QH), _heads(k16, AT_KH), _heads(v16, AT_KH)
    o_at, lse = _attn_fwd("attn_fwd", qh, kh, vh)
    y_at = _unheads(o_at)

    p0 = _mm("mix_branch", y_s5, w['w_branch_s5'], out_dtype=BF16)
    p1 = _mm("mix_branch", y_gla, w['w_branch_gla'], out_dtype=BF16)
    p2 = _mm("mix_branch_f32", y_at, w['w_branch_attn'], out_dtype=BF16)

    def merge(gpv, a0, a1, a2, bias):
        g = _sigmoid(gpv.astype(F32) + bias)
        return (g[:, :D_MODEL] * a0.astype(F32) + g[:, D_MODEL:2 * D_MODEL] * a1.astype(F32)
                + g[:, 2 * D_MODEL:] * a2.astype(F32))

    merged = _rowwise("mix_merge", merge, [gp, p0, p1, p2], [w['b_merge_gate']], [(D_MODEL, BF16)])[0]
    xo = _mm("mix_out", merged, w['w_out'], add=x)
    saved = dict(x=x, h=h, z=z, gp=gp, xs=xs, y0=y0, t_glu=t_glu, y_s5=y_s5, la=(la_f, la_b), o=(o_f, o_b), s=(s_f, s_b),
                 y_gla=y_gla, qh=qh, kh=kh, vh=vh, o_at=o_at, lse=lse, y_at=y_at, p=(p0, p1, p2), merged=merged)
    return xo, saved


def _mixer_bwd(dy, sv, w, s5_raw):
    z, h = sv['z'], sv['h']
    n_tok = z.shape[0]
    grads = {}
    dmerged = _mm("mix_dmerged", dy, w['w_out'], tb=True, out_dtype=BF16)
    grads['w_out'] = _mm("mix_dwout", sv['merged'], dy, ta=True)

    def dmerge(dm, gpv, a0, a1, a2, bias):
        dm = dm.astype(F32)
        g = _sigmoid(gpv.astype(F32) + bias)
        ps = (a0.astype(F32), a1.astype(F32), a2.astype(F32))
        dps, dgs = [], []
        for k in range(3):
            gk = g[:, k * D_MODEL:(k + 1) * D_MODEL]
            dps.append(dm * gk)
            dgs.append(dm * ps[k] * gk * (1.0 - gk))
        dgp = jnp.concatenate(dgs, axis=1)
        return dps[0], dps[1], dps[2], dgp, jnp.sum(dgp, axis=0, keepdims=True)

    dp0, dp1, dp2, dgp, dbm = _rowwise("mix_dmerge", dmerge, [dmerged, sv['gp'], *sv['p']], [w['b_merge_gate']],
                                       [(D_MODEL, BF16)] * 3 + [(3 * D_MODEL, BF16)], accs=[(1, 3 * D_MODEL)])
    grads['b_merge_gate'] = dbm[0]
    grads['w_branch_s5'] = _mm("mix_dbranch", sv['y_s5'], dp0, ta=True)
    grads['w_branch_gla'] = _mm("mix_dbranch", sv['y_gla'], dp1, ta=True)
    grads['w_branch_attn'] = _mm("mix_dbranch_f32", sv['y_at'], dp2, ta=True)
    dy_s5 = _mm("mix_dy", dp0, w['w_branch_s5'], tb=True)
    dy_gla = _mm("mix_dy", dp1, w['w_branch_gla'], tb=True)
    dy_at = _mm("mix_dy", dp2, w['w_branch_attn'], tb=True)
    grads['w_merge_gate'] = _mm("mix_dwgate", h, dgp, ta=True)
    dh = _mm("mix_dh_gate", dgp, w['w_merge_gate'], tb=True)
    u = _cw(z, 0, S5_W)

    def s5_post_bwd(dyv, y0, t, uv, dv, wglu):
        sg = _sigmoid(t)
        y1 = _gelu(y0)
        dt = dyv * y1 * sg * (1.0 - sg)
        dy1 = dyv * sg + _dot(dt.astype(BF16), wglu, _NT)
        dy0 = dy1 * _gelu_grad(y0)
        return dy0, dt, y1, dy0 * dv, jnp.sum(dy0 * uv, axis=0, keepdims=True)

    dy0, dt_glu, y1, du, dd = _rowwise("s5_post_bwd", s5_post_bwd, [dy_s5, sv['y0'], sv['t_glu'], u],
                                       [w['s5_d'], w['s5_w_glu']],
                                       [(S5_W, BF16), (S5_W, BF16), (S5_W, BF16), (S5_W, F32)], accs=[(1, S5_W)])
    grads['s5_d'] = dd[0]
    grads['s5_w_glu'] = _mm("s5_dwglu", y1, dt_glu, ta=True)
    s5_cot = []
    for d, rev in ((0, False), (1, True)):
        a_re, a_im, wb, wc = w['s5'][d]
        gx = _mm("s5_gx", dy0, wc, tb=True)
        gs, da = _scan("s5_adj_fwd" if rev else "s5_adj_rev", gx, a_re, -a_im, rev=not rev, xin=sv['xs'][d])
        dwc = _mm("s5_dwc", sv['xs'][d], dy0, ta=True)
        dwb = _mm("s5_dwb", u, gs, ta=True)
        du = _mm("s5_du", gs, wb, tb=True, add=du)
        da = jnp.sum(da, axis=0, keepdims=True)
        s5_cot.append((da[:, :S5_N], da[:, S5_N:], dwb, dwc))
    for d in range(2):
        _, pull = jax.vjp(_s5_prep, *s5_raw[d])
        s5_cot[d] = pull(s5_cot[d])
    for k, nm in enumerate(('s5_lambda_re', 's5_lambda_im', 's5_log_dt', 's5_b_re', 's5_b_im', 's5_c_re', 's5_c_im')):
        grads[nm] = jnp.stack([s5_cot[0][k], s5_cot[1][k]])

    gate = _cw(z, 4, GLA_W)

    def gla_post_bwd(dyv, of, ob, gv, gn):
        o = of + ob
        dos, dgates = [], []
        dgn = jnp.zeros((1, GLA_D), F32)
        for hh in range(GLA_H):
            sl = slice(hh * GLA_D, (hh + 1) * GLA_D)
            gh, dyh = gv[:, sl], dyv[:, sl]
            sg = _sigmoid(gh)
            dgates.append(dyh * _rms(o[:, sl], gn) * sg * (1.0 + gh * (1.0 - sg)))
            dx, dg = _rms_bwd(o[:, sl], gn, dyh * gh * sg)
            dos.append(dx)
            dgn = dgn + dg
        return jnp.concatenate(dos, axis=1), jnp.concatenate(dgates, axis=1), dgn

    do, dgate, dgn = _rowwise("gla_post_bwd", gla_post_bwd, [dy_gla, *sv['o'], gate], [w['gla_norm']],
                              [(GLA_W, F32), (GLA_W, F32)], accs=[(1, GLA_D)])
    grads['gla_norm'] = dgn[0]
    dq_f, dk_f, dv_f, dla_f = _gla_bwd("gla_bwd_fwd", z, sv['la'][0], sv['s'][0], do, rev=False)
    dq_b, dk_b, dv_b, dla_b = _gla_bwd("gla_bwd_rev", z, sv['la'][1], sv['s'][1], do, rev=True)
    wa_f, wa_b = _alpha_pads(w['gla_w_alpha'])
    ba_f, ba_b = w['gla_b_alpha'][0:1], w['gla_b_alpha'][1:2]
    zz = _cw(z, ZC_Z, LANE)

    def gla_prep_bwd(dlf, dlb, zv, waf, wab, bf, bb):
        z16 = zv.astype(BF16)
        waf16, wab16 = waf.astype(BF16), wab.astype(BF16)
        xf = _dot(z16, waf16) + bf
        xb = _dot(z16, wab16) + bb
        df = dlf * (_sigmoid(-xf) / GLA_TAU)
        db = dlb * (_sigmoid(-xb) / GLA_TAU)
        dz = _dot(df.astype(BF16), waf16, _NT) + _dot(db.astype(BF16), wab16, _NT)
        return df, db, dz, jnp.sum(df, axis=0, keepdims=True), jnp.sum(db, axis=0, keepdims=True)

    dlog_f, dlog_b, dzz, dba_f, dba_b = _rowwise(
        "gla_prep_bwd", gla_prep_bwd, [dla_f, dla_b, zz], [wa_f, wa_b, ba_f, ba_b],
        [(GLA_W, BF16), (GLA_W, BF16), (LANE, F32)], accs=[(1, GLA_W), (1, GLA_W)])
    dwa_f = _mm("gla_dwa", zz, dlog_f, ta=True)
    dwa_b = _mm("gla_dwa", zz, dlog_b, ta=True)
    grads['gla_w_alpha'] = jnp.stack([dwa_f[:GLA_R], dwa_b[GLA_R:2 * GLA_R]])
    grads['gla_b_alpha'] = jnp.concatenate([dba_f, dba_b], axis=0)

    dqh, dkh, dvh = _attn_bwd("attn_bwd", sv['qh'], sv['kh'], sv['vh'], sv['o_at'], sv['lse'], _heads(dy_at, AT_QH))
    cq, sq, mq, rq = _head_consts(n_tok, AT_QH)
    ck, sk, mk, rk = _head_consts(n_tok, AT_KH)
    gq = jnp.tile(w['attn_q_norm'], (1, AT_QH))
    gk = jnp.tile(w['attn_k_norm'], (1, AT_KH))

    def rope_bwd(dr, v, gain, cos, sin, mean, rot, scale):
        dr = dr * scale
        dvn = dr * cos + _dot(dr * sin, rot, _NT, precision=HI)
        r = lax.rsqrt(_dot(v * v, mean, precision=HI) + EPS)
        vh = v * r
        dg = jnp.sum(dvn * vh, axis=0, keepdims=True)
        dvh = dvn * gain
        return r * (dvh - vh * _dot(dvh * vh, mean, precision=HI)), dg

    def attn_prep_bwd(dq, dk, qv, kv, cqv, sqv, ckv, skv, gqv, gkv, mqv, rqv, mkv, rkv):
        dqo, dgq = rope_bwd(dq, qv, gqv, cqv, sqv, mqv, rqv, AT_D ** -0.5)
        dko, dgk = rope_bwd(dk, kv, gkv, ckv, skv, mkv, rkv, 1.0)
        return dqo, dko, dgq, dgk

    q_raw, k_raw = _cw(z, 5, AT_W), _cw(z, ZC_AK, AT_KW)
    daq, dak, dgq, dgk = _rowwise("attn_prep_bwd", attn_prep_bwd, [_unheads(dqh), _unheads(dkh), q_raw, k_raw, cq, sq, ck, sk],
                                  [gq, gk, mq, rq, mk, rk], [(AT_W, F32), (AT_KW, F32)], accs=[(1, AT_W), (1, AT_KW)])
    grads['attn_q_norm'] = dgq.reshape(AT_QH, AT_D).sum(axis=0)
    grads['attn_k_norm'] = dgk.reshape(AT_KH, AT_D).sum(axis=0)
    dav = _unheads(dvh)

    def assemble(duv, qf, qb, kf, kb, vf, vb, dg, aq, ak, av, zzv):
        return jnp.concatenate([duv, qf + qb, kf + kb, vf + vb, dg, aq, ak, av, zzv], axis=1)

    dz = _rowwise("mix_dz", assemble, [du, dq_f, dq_b, dk_f, dk_b, dv_f, dv_b, dgate, daq, dak, dav, dzz], [],
                  [(ZP_W, BF16)])[0]
    grads['w_in'] = _unpad_w_in(_mm("mix_dwin", h, dz, ta=True))
    dh = _mm("mix_dh_in", dz, w['w_in_p'], tb=True, add=dh)

    def dnorm(dyv, xv, dhv, gv):
        dx, dg = _rms_bwd(xv, gv, dhv)
        return dyv + dx, dg

    dx, dg = _rowwise("ffn_dnorm", dnorm, [dy, sv['x'], dh], [w['mix_norm']], [(D_MODEL, F32)], accs=[(1, D_MODEL)])
    grads['mix_norm'] = dg[0]
    return dx, grads


def _layer_weights(wf, i):
    w = {k: wf[k][i] for k in W_NAMES if k != 'final_norm'}
    for k in ('ffn1_norm', 'mix_norm', 'ffn2_norm', 's5_d', 'gla_norm', 'attn_q_norm', 'attn_k_norm', 'b_merge_gate'):
        w[k] = w[k].reshape(1, -1)
    w['w_in_p'] = _pad_w_in(w['w_in'])
    s5_raw = [tuple(wf[k][i, d] for k in ('s5_lambda_re', 's5_lambda_im', 's5_log_dt', 's5_b_re', 's5_b_im',
                                           's5_c_re', 's5_c_im')) for d in range(2)]
    w['s5'] = [_s5_prep(*s5_raw[d]) for d in range(2)]
    return w, s5_raw


def _device_step(x, target, wf):
    lw = [_layer_weights(wf, i) for i in range(DEPTH)]
    saved = []
    for i in range(DEPTH):
        w, _ = lw[i]
        x, s1 = _ffn_fwd(x, w['ffn1_norm'], w['ffn1_w_gate'], w['ffn1_w_up'], w['ffn1_w_down'])
        x, s2 = _mixer_fwd(x, w)
        x, s3 = _ffn_fwd(x, w['ffn2_norm'], w['ffn2_w_gate'], w['ffn2_w_up'], w['ffn2_w_down'])
        saved.append((s1, s2, s3))
    gfin = wf['final_norm'].reshape(1, -1)

    def head(xv, tv, gv):
        e = _rms(xv, gv) - tv
        dx, dg = _rms_bwd(xv, gv, e * (1.0 / D_MODEL))
        part = jnp.sum(e * e, axis=0, keepdims=True)
        return dx, dg, part

    dx, dgfin, part = _rowwise("loss_head", head, [x, target], [gfin], [(D_MODEL, F32)],
                               accs=[(1, D_MODEL), (1, D_MODEL)])
    loss = (0.5 / D_MODEL) * jnp.sum(part)
    per_layer = []
    for i in reversed(range(DEPTH)):
        w, s5_raw = lw[i]
        s1, s2, s3 = saved[i]
        g = {}
        dx, g3 = _ffn_bwd(dx, s3, w['ffn2_norm'], w['ffn2_w_gate'], w['ffn2_w_up'], w['ffn2_w_down'])
        dx, g2 = _mixer_bwd(dx, s2, w, s5_raw)
        dx, g1 = _ffn_bwd(dx, s1, w['ffn1_norm'], w['ffn1_w_gate'], w['ffn1_w_up'], w['ffn1_w_down'])
        g.update({'ffn2_' + k: v for k, v in g3.items()})
        g.update(g2)
        g.update({'ffn1_' + k: v for k, v in g1.items()})
        per_layer.append(g)
    per_layer.reverse()
    grads = {k: jnp.stack([per_layer[i][k] for i in range(DEPTH)]) for k in W_NAMES if k != 'final_norm'}
    grads['final_norm'] = dgfin[0]
    return loss, dx, grads


PACK_C = 1024
ROW_TILE = 512


def _rows_of(shape):
    return -(-int(np.prod(shape)) // PACK_C)


def _as_rows(t, lead=0):
    head = t.shape[:lead]
    flat = t.reshape(head + (-1,))
    rows = -(-flat.shape[-1] // PACK_C)
    pad = rows * PACK_C - flat.shape[-1]
    if pad:
        flat = jnp.pad(flat, [(0, 0)] * lead + [(0, pad)])
    return flat.reshape(head + (rows, PACK_C))


def _pack(items, total_rows, lead=0):
    parts = [_as_rows(t, lead) for t in items]
    used = sum(p.shape[lead] for p in parts)
    if total_rows > used:
        parts.append(jnp.zeros(parts[0].shape[:lead] + (total_rows - used, PACK_C), parts[0].dtype))
    return jnp.concatenate(parts, axis=lead)


def _unpack(rows, shapes, lead=0):
    out, r0 = [], 0
    head = rows.shape[:lead]
    for shp in shapes:
        n, size = _rows_of(shp), int(np.prod(shp))
        piece = lax.slice_in_dim(rows, r0, r0 + n, axis=lead).reshape(head + (n * PACK_C,))
        out.append(lax.slice_in_dim(piece, 0, size, axis=lead).reshape(head + tuple(shp)))
        r0 += n
    return out


def _round_up(n, m):
    return -(-n // m) * m


def _to_shards(g, axis):
    shp = g.shape
    g = g.reshape(shp[:axis] + (N_CHIPS, shp[axis] // N_CHIPS) + shp[axis + 1:])
    return jnp.moveaxis(g, axis, 0)


def _from_shards(s, axis):
    s = jnp.moveaxis(s, 0, axis)
    shp = s.shape
    return s.reshape(shp[:axis] + (shp[axis] * shp[axis + 1],) + shp[axis + 2:])


_HBM = pl.BlockSpec(memory_space=pltpu.HBM)


def _place():
    x, y, c = lax.axis_index("x"), lax.axis_index("y"), lax.axis_index("c")
    chips = [(1 - x, y), (x, 1 - y), (1 - x, 1 - y)]
    return x, y, c, chips


def _rcopy(src, dst, send, recv, k, dev):
    return pltpu.make_async_remote_copy(src_ref=src, dst_ref=dst, send_sem=send.at[k], recv_sem=recv.at[k],
                                        device_id=dev, device_id_type=MESH)


def _ag_chips(name, pack):
    def body(p_ref, o_ref, send, recv, lsem):
        x, y, c, chips = _place()
        me = 2 * x + y
        mine = pltpu.make_async_copy(p_ref, o_ref.at[me], lsem)
        mine.start()
        first = [_rcopy(p_ref.at[c], o_ref.at[me, c], send, recv, k, (px, py, c)) for k, (px, py) in enumerate(chips)]
        for cp in first:
            cp.start()
        passed = []
        for k, (px, py) in enumerate(chips):
            blk = o_ref.at[2 * px + py, c]
            _rcopy(blk, blk, send, recv, k, (px, py, c)).wait_recv()
            fw = _rcopy(blk, blk, send, recv, 3 + k, (x, y, 1 - c))
            fw.start()
            passed.append(fw)
        for k, (px, py) in enumerate(chips):
            blk = o_ref.at[2 * px + py, 1 - c]
            _rcopy(blk, blk, send, recv, 3 + k, (x, y, 1 - c)).wait_recv()
        for cp in first + passed:
            cp.wait_send()
        mine.wait()

    return pl.pallas_call(
        body, name=name, in_specs=[_HBM], out_specs=_HBM,
        out_shape=jax.ShapeDtypeStruct((N_CHIPS,) + pack.shape, pack.dtype),
        scratch_shapes=[pltpu.SemaphoreType.DMA((6,)), pltpu.SemaphoreType.DMA((6,)), pltpu.SemaphoreType.DMA],
    )(pack)


def _pair_swap(name, g):
    ns = g.shape[0]

    def body(g_ref, r_ref, send, recv):
        x, y, c, _ = _place()
        cps = [_rcopy(g_ref.at[j, 1 - c], r_ref.at[j], send, recv, j, (x, y, 1 - c)) for j in range(ns)]
        for cp in cps:
            cp.start()
        for cp in cps:
            cp.wait_recv()
        for cp in cps:
            cp.wait_send()

    return pl.pallas_call(
        body, name=name, in_specs=[_HBM], out_specs=_HBM,
        out_shape=jax.ShapeDtypeStruct((ns,) + g.shape[2:], g.dtype),
        scratch_shapes=[pltpu.SemaphoreType.DMA((ns,)), pltpu.SemaphoreType.DMA((ns,))],
    )(g)


def _chip_exchange(name, part):
    def body(p_ref, o_ref, send, recv, lsem):
        x, y, c, chips = _place()
        me = 2 * x + y
        mine = pltpu.make_async_copy(p_ref.at[me], o_ref.at[me], lsem)
        mine.start()
        cps = [_rcopy(p_ref.at[2 * px + py], o_ref.at[me], send, recv, k, (px, py, c)) for k, (px, py) in enumerate(chips)]
        for cp in cps:
            cp.start()
        for k, (px, py) in enumerate(chips):
            blk = o_ref.at[2 * px + py]
            _rcopy(blk, blk, send, recv, k, (px, py, c)).wait_recv()
        for cp in cps:
            cp.wait_send()
        mine.wait()

    return pl.pallas_call(
        body, name=name, in_specs=[_HBM], out_specs=_HBM, out_shape=jax.ShapeDtypeStruct(part.shape, part.dtype),
        scratch_shapes=[pltpu.SemaphoreType.DMA((3,)), pltpu.SemaphoreType.DMA((3,)), pltpu.SemaphoreType.DMA],
    )(part)


def _pair_share(name, red):
    def body(r_ref, o_ref, send, recv, lsem):
        x, y, c, _ = _place()
        mine = pltpu.make_async_copy(r_ref, o_ref.at[c], lsem)
        mine.start()
        cp = _rcopy(r_ref, o_ref.at[c], send, recv, 0, (x, y, 1 - c))
        cp.start()
        blk = o_ref.at[1 - c]
        _rcopy(blk, blk, send, recv, 0, (x, y, 1 - c)).wait_recv()
        cp.wait_send()
        mine.wait()

    return pl.pallas_call(
        body, name=name, in_specs=[_HBM], out_specs=_HBM, out_shape=jax.ShapeDtypeStruct((2,) + red.shape, red.dtype),
        scratch_shapes=[pltpu.SemaphoreType.DMA((1,)), pltpu.SemaphoreType.DMA((1,)), pltpu.SemaphoreType.DMA],
    )(red)


def _add_own_half(name, g, r):
    ns, _, rh, cc = g.shape
    tl = _tile(rh, ROW_TILE, SUBLANE)
    core = lax.axis_index("c").astype(jnp.int32).reshape(1)

    def body(c_ref, g_ref, r_ref, o_ref):
        o_ref[...] = g_ref[0] + r_ref[...]

    return pl.pallas_call(
        body, name=name,
        grid_spec=pltpu.PrefetchScalarGridSpec(
            num_scalar_prefetch=1, grid=(ns, rh // tl),
            in_specs=[pl.BlockSpec((1, 1, tl, cc), lambda j, i, c_ref: (j, c_ref[0], i, 0)),
                      pl.BlockSpec((1, tl, cc), lambda j, i, c_ref: (j, i, 0))],
            out_specs=pl.BlockSpec((1, tl, cc), lambda j, i, c_ref: (j, i, 0))),
        out_shape=jax.ShapeDtypeStruct(r.shape, r.dtype),
        compiler_params=_cparams(("parallel", "parallel"), 8 * tl * cc * 4),
    )(core, g, r)


def _sum_slots(name, t):
    ns, rh, cc = t.shape
    tl = _tile(rh, ROW_TILE, SUBLANE)

    def body(t_ref, o_ref):
        acc = t_ref[0]
        for j in range(1, ns):
            acc = acc + t_ref[j]
        o_ref[...] = acc

    return pl.pallas_call(
        body, name=name, grid=(rh // tl,), in_specs=[pl.BlockSpec((ns, tl, cc), lambda i: (0, i, 0))],
        out_specs=pl.BlockSpec((tl, cc), lambda i: (i, 0)), out_shape=jax.ShapeDtypeStruct((rh, cc), t.dtype),
        compiler_params=_cparams(("parallel",), 2 * (ns + 1) * tl * cc * 4),
    )(t)


def _reduce_scatter(gpack):
    part = _add_own_half("rs_add_pair", gpack, _pair_swap("rs_pair_swap", gpack))
    red = _sum_slots("rs_sum_chips", _chip_exchange("rs_chip_exchange", part))
    both = _pair_share("rs_pair_share", red)
    return both.reshape(2 * both.shape[1], both.shape[2])


def _adamw(name, g, w, m, v):
    c1 = 1.0 - ADAM_B1 ** ADAM_STEP
    c2 = 1.0 - ADAM_B2 ** ADAM_STEP

    def upd(gv, wv, mv, vv):
        mn = ADAM_B1 * mv + (1.0 - ADAM_B1) * gv
        vn = ADAM_B2 * vv + (1.0 - ADAM_B2) * (gv * gv)
        delta = -ADAM_LR * ((mn / c1) / (jnp.sqrt(vn / c2) + ADAM_EPS) + ADAM_WD * wv)
        return delta, mn, vn

    return _rowwise(name, upd, [g, w, m, v], [], [(PACK_C, F32)] * 3, tl=ROW_TILE)


def _train_step(a):
    x, target = a['x'][0], a['loss_target'][0]

    wf = {n: a[n] for n in REPLICATED}
    for names, dt, mult in ((BF16_GATHER, BF16, 32), (F32_GATHER, F32, 16)):
        shapes = [a[n].shape for n in names]
        rows = _round_up(sum(_rows_of(s) for s in shapes), mult)
        pack = _pack([a[n].astype(dt) for n in names], rows)
        got = _ag_chips("ag_weights", pack.reshape(2, rows // 2, PACK_C)).reshape(N_CHIPS, rows, PACK_C)
        for n, s in zip(names, _unpack(got, shapes, lead=1)):
            wf[n] = _from_shards(s, SHARD_AXIS[n])

    loss, gx, grads = _device_step(x, target, wf)
    loss = lax.psum(loss, ("x", "y", "c"))

    sh_shapes = [a[n].shape for n in SHARDED]
    rep_shapes = [a[n].shape for n in REPLICATED]
    r_sh = _round_up(sum(_rows_of(s) for s in sh_shapes), ROW_TILE)
    rep_rows = _round_up(sum(_rows_of(s) for s in rep_shapes), N_CHIPS)
    quarter = rep_rows // N_CHIPS
    r_q = _round_up(quarter, ROW_TILE)
    sh_pack = _pack([_to_shards(grads[n], SHARD_AXIS[n]) for n in SHARDED], r_sh, lead=1)
    rep_pack = _pack([grads[n] for n in REPLICATED], rep_rows).reshape(N_CHIPS, quarter, PACK_C)
    rep_pack = jnp.pad(rep_pack, ((0, 0), (0, r_q - quarter), (0, 0)))
    gpack = jnp.concatenate([sh_pack, rep_pack], axis=1)
    red = _reduce_scatter(gpack.reshape(N_CHIPS, 2, (r_sh + r_q) // 2, PACK_C))
    g_sh = red[:r_sh]
    rep_all = _ag_chips("ag_replicated", red[r_sh:].reshape(2, r_q // 2, PACK_C)).reshape(N_CHIPS, r_q, PACK_C)
    g_rep = rep_all[:, :quarter].reshape(rep_rows, PACK_C)

    outs = {}
    for names, shapes, g, rows in ((SHARDED, sh_shapes, g_sh, r_sh),
                                   (REPLICATED, rep_shapes, g_rep, _round_up(rep_rows, LANE))):
        if g.shape[0] < rows:
            g = jnp.pad(g, ((0, rows - g.shape[0]), (0, 0)))
        w, m, v = (_pack([a[p + n] for n in names], rows) for p in ('', 'm_', 'v_'))
        delta, mn, vn = _adamw("adamw", g, w, m, v)
        for kind, arr in (('grad', g), ('delta', delta), ('new_m', mn), ('new_v', vn)):
            for n, t in zip(names, _unpack(arr, shapes)):
                outs[kind + '_' + n] = t
    res = [loss, gx[None]]
    for kind in ('grad', 'delta', 'new_m', 'new_v'):
        res += [outs[kind + '_' + n] for n in W_NAMES]
    return tuple(res)


def kernel(x, ffn1_norm, ffn1_w_gate, ffn1_w_up, ffn1_w_down, mix_norm, w_in, s5_lambda_re, s5_lambda_im, s5_log_dt, s5_b_re, s5_b_im, s5_c_re, s5_c_im, s5_d, s5_w_glu, gla_w_alpha, gla_b_alpha, gla_norm, attn_q_norm, attn_k_norm, w_branch_s5, w_branch_gla, w_branch_attn, w_merge_gate, b_merge_gate, w_out, ffn2_norm, ffn2_w_gate, ffn2_w_up, ffn2_w_down, final_norm, loss_target, m_ffn1_norm, m_ffn1_w_gate, m_ffn1_w_up, m_ffn1_w_down, m_mix_norm, m_w_in, m_s5_lambda_re, m_s5_lambda_im, m_s5_log_dt, m_s5_b_re, m_s5_b_im, m_s5_c_re, m_s5_c_im, m_s5_d, m_s5_w_glu, m_gla_w_alpha, m_gla_b_alpha, m_gla_norm, m_attn_q_norm, m_attn_k_norm, m_w_branch_s5, m_w_branch_gla, m_w_branch_attn, m_w_merge_gate, m_b_merge_gate, m_w_out, m_ffn2_norm, m_ffn2_w_gate, m_ffn2_w_up, m_ffn2_w_down, m_final_norm, v_ffn1_norm, v_ffn1_w_gate, v_ffn1_w_up, v_ffn1_w_down, v_mix_norm, v_w_in, v_s5_lambda_re, v_s5_lambda_im, v_s5_log_dt, v_s5_b_re, v_s5_b_im, v_s5_c_re, v_s5_c_im, v_s5_d, v_s5_w_glu, v_gla_w_alpha, v_gla_b_alpha, v_gla_norm, v_attn_q_norm, v_attn_k_norm, v_w_branch_s5, v_w_branch_gla, v_w_branch_attn, v_w_merge_gate, v_b_merge_gate, v_w_out, v_ffn2_norm, v_ffn2_w_gate, v_ffn2_w_up, v_ffn2_w_down, v_final_norm):
    return _train_step(dict(locals()))
```

```python
import functools
import math

import jax
import jax.numpy as jnp
import numpy as np
from jax import lax
from jax.experimental import pallas as pl
from jax.experimental.pallas import tpu as pltpu

F32 = jnp.float32
BF16 = jnp.bfloat16
HI = lax.Precision.HIGHEST

D_MODEL = 1024
D_FF = 2816
DEPTH = 2
EPS = 1e-6
S5_G, S5_H, S5_P = 32, 16, 64
S5_W = S5_G * S5_H
S5_N = S5_G * S5_P
GLA_H, GLA_D = 4, 128
GLA_W = GLA_H * GLA_D
GLA_R = 16
GLA_TAU = 16.0
GLA_C = 64
AT_QH, AT_KH, AT_D = 8, 2, 64
AT_W = AT_QH * AT_D
AT_KW = AT_KH * AT_D
GRID_W = 64
ROPE_BASE = 10000.0
ZP_W = 3456
ZC_GQ, ZC_GK, ZC_GV, ZC_AK, ZC_AV, ZC_Z = 4, 8, 12, 24, 25, 26

ADAM_LR, ADAM_B1, ADAM_B2, ADAM_EPS, ADAM_WD, ADAM_STEP = 0.001, 0.9, 0.999, 1e-08, 0.01, 10

VMEM_LIMIT_V7X = 56 * 1024 * 1024
LANE = 128
SUBLANE = 8

W_NAMES = ['ffn1_norm', 'ffn1_w_gate', 'ffn1_w_up', 'ffn1_w_down', 'mix_norm', 'w_in', 's5_lambda_re', 's5_lambda_im',
           's5_log_dt', 's5_b_re', 's5_b_im', 's5_c_re', 's5_c_im', 's5_d', 's5_w_glu', 'gla_w_alpha', 'gla_b_alpha',
           'gla_norm', 'attn_q_norm', 'attn_k_norm', 'w_branch_s5', 'w_branch_gla', 'w_branch_attn', 'w_merge_gate',
           'b_merge_gate', 'w_out', 'ffn2_norm', 'ffn2_w_gate', 'ffn2_w_up', 'ffn2_w_down', 'final_norm']
SHARD_AXIS = {'ffn1_w_gate': 2, 'ffn1_w_up': 2, 'ffn1_w_down': 1, 'w_in': 2, 's5_w_glu': 1, 'gla_w_alpha': 3,
              'gla_b_alpha': 2, 'w_branch_s5': 2, 'w_branch_gla': 2, 'w_branch_attn': 2, 'w_merge_gate': 2,
              'w_out': 1, 'ffn2_w_gate': 2, 'ffn2_w_up': 2, 'ffn2_w_down': 1}
F32_GATHER = ['gla_w_alpha', 'gla_b_alpha']
BF16_GATHER = [n for n in W_NAMES if n in SHARD_AXIS and n not in F32_GATHER]
SHARDED = [n for n in W_NAMES if n in SHARD_AXIS]
REPLICATED = [n for n in W_NAMES if n not in SHARD_AXIS]
N_CHIPS = 4
MESH = pl.DeviceIdType.MESH


def _tile(n, target, mult):
    best = None
    for t in range(mult, min(n, target) + 1, mult):
        if n % t == 0:
            best = t
    return best if best is not None else n


def _cparams(sem, vmem_bytes):
    limit = int(min(VMEM_LIMIT_V7X, max(32 * 1024 * 1024, vmem_bytes * 5 // 4)))
    return pltpu.CompilerParams(dimension_semantics=sem, vmem_limit_bytes=limit)


def _cw(arr, cb, width):
    return (arr, cb, width)


def _win(a):
    if isinstance(a, tuple):
        return a
    return (a, 0, a.shape[-1])


def _nbytes(shape, dtype):
    return int(np.prod(shape)) * jnp.dtype(dtype).itemsize


def _rowwise(name, fn, rows, fulls, outs, accs=(), tl=256):
    rows = [_win(r) for r in rows]
    n_tok = rows[0][0].shape[0]
    tl = _tile(n_tok, tl, 16)
    nr, nf, no = len(rows), len(fulls), len(outs)

    def body(*refs):
        vals = [r[...] for r in refs[:nr + nf]]
        res = fn(*vals)
        if not isinstance(res, (tuple, list)):
            res = (res,)
        for k in range(no):
            o = refs[nr + nf + k]
            o[...] = res[k].astype(o.dtype)
        if accs:
            @pl.when(pl.program_id(0) == 0)
            def _():
                for k in range(len(accs)):
                    a = refs[nr + nf + no + k]
                    a[...] = jnp.zeros(a.shape, a.dtype)
            for k in range(len(accs)):
                a = refs[nr + nf + no + k]
                a[...] += res[no + k]

    in_specs = [pl.BlockSpec((tl, w), functools.partial(lambda i, cb: (i, cb), cb=cb)) for (_, cb, w) in rows]
    in_specs += [pl.BlockSpec(f.shape, functools.partial(lambda i, nd: (0,) * nd, nd=f.ndim)) for f in fulls]
    out_specs = [pl.BlockSpec((tl, c), lambda i: (i, 0)) for (c, _) in outs]
    out_specs += [pl.BlockSpec((r, c), lambda i: (0, 0)) for (r, c) in accs]
    out_shape = [jax.ShapeDtypeStruct((n_tok, c), dt) for (c, dt) in outs]
    out_shape += [jax.ShapeDtypeStruct((r, c), F32) for (r, c) in accs]
    vm = 2 * sum(_nbytes((tl, w), a.dtype) for (a, _, w) in rows)
    vm += 2 * sum(_nbytes(f.shape, f.dtype) for f in fulls)
    vm += 2 * sum(_nbytes((tl, c), dt) for (c, dt) in outs)
    vm += 8 * max([_nbytes((tl, w), F32) for (_, _, w) in rows] + [_nbytes((tl, c), F32) for (c, _) in outs])
    return pl.pallas_call(
        body, name=name, grid=(n_tok // tl,), in_specs=in_specs, out_specs=out_specs, out_shape=out_shape,
        compiler_params=_cparams(("arbitrary",) if accs else ("parallel",), vm),
    )(*[r[0] for r in rows], *fulls)


def _mm(name, a, b, *, ta=False, tb=False, out_dtype=F32, alpha=1.0, add=None, tm=512, tn=512):
    a_arr, a_cb, a_w = _win(a)
    b_arr, b_cb, b_w = _win(b)
    if ta:
        kdim, m = a_arr.shape[0], a_w
    else:
        m, kdim = a_arr.shape[0], a_w
    if tb:
        n, kb = b_arr.shape[0], b_w
    else:
        kb, n = b_arr.shape[0], b_w
    assert kdim == kb, (name, kdim, kb)
    tm = _tile(m, tm, LANE if ta else 16)
    tn = _tile(n, tn, LANE)
    if ta:
        a_spec = pl.BlockSpec((kdim, tm), lambda i, j: (0, a_cb * (a_w // tm) + i))
        a_blk = (kdim, tm)
    else:
        a_spec = pl.BlockSpec((tm, kdim), lambda i, j: (i, a_cb))
        a_blk = (tm, kdim)
    if tb:
        b_spec = pl.BlockSpec((tn, kdim), lambda i, j: (j, b_cb))
        b_blk = (tn, kdim)
    else:
        b_spec = pl.BlockSpec((kdim, tn), lambda i, j: (0, b_cb * (b_w // tn) + j))
        b_blk = (kdim, tn)
    dims = (((0 if ta else 1,), (1 if tb else 0,)), ((), ()))
    has_add = add is not None

    def body(*refs):
        a_ref, b_ref = refs[0], refs[1]
        o_ref = refs[-1]
        acc = lax.dot_general(a_ref[...].astype(BF16), b_ref[...].astype(BF16), dims, preferred_element_type=F32)
        if alpha != 1.0:
            acc = acc * alpha
        if has_add:
            acc = acc + refs[2][...].astype(F32)
        o_ref[...] = acc.astype(o_ref.dtype)

    in_specs = [a_spec, b_spec]
    ops = [a_arr, b_arr]
    if has_add:
        in_specs.append(pl.BlockSpec((tm, tn), lambda i, j: (i, j)))
        ops.append(add)
    vm = 2 * (_nbytes(a_blk, a_arr.dtype) + _nbytes(b_blk, b_arr.dtype) + _nbytes((tm, tn), out_dtype))
    vm += _nbytes(a_blk, BF16) + _nbytes(b_blk, BF16) + 3 * _nbytes((tm, tn), F32)
    return pl.pallas_call(
        body, name=name, grid=(m // tm, n // tn), in_specs=in_specs,
        out_specs=pl.BlockSpec((tm, tn), lambda i, j: (i, j)),
        out_shape=jax.ShapeDtypeStruct((m, n), out_dtype),
        compiler_params=_cparams(("parallel", "parallel"), vm),
    )(*ops)


def _sigmoid(x):
    return 1.0 / (1.0 + jnp.exp(-x))


def _rms(x, gain):
    r = lax.rsqrt(jnp.mean(x * x, axis=-1, keepdims=True) + EPS)
    return x * r * gain


def _rms_bwd(x, gain, dh):
    r = lax.rsqrt(jnp.mean(x * x, axis=-1, keepdims=True) + EPS)
    xh = x * r
    dg = jnp.sum(dh * xh, axis=0, keepdims=True)
    dxh = dh * gain
    dx = r * (dxh - xh * jnp.mean(dxh * xh, axis=-1, keepdims=True))
    return dx, dg


_GELU_C = math.sqrt(2.0 / math.pi)


def _gelu(y):
    return 0.5 * y * (1.0 + jnp.tanh(_GELU_C * (y + 0.044715 * y * y * y)))


def _gelu_grad(y):
    th = jnp.tanh(_GELU_C * (y + 0.044715 * y * y * y))
    return 0.5 * (1.0 + th) + 0.5 * y * (1.0 - th * th) * _GELU_C * (1.0 + 3.0 * 0.044715 * y * y)


def _log_sigmoid(x):
    return jnp.minimum(x, 0.0) - jnp.log(1.0 + jnp.exp(-jnp.abs(x)))


def _dot(a, b, dims=(((1,), (0,)), ((), ())), precision=None):
    return lax.dot_general(a, b, dims, preferred_element_type=F32, precision=precision)


_NT = (((1,), (1,)), ((), ()))
_TN = (((0,), (0,)), ((), ()))


def _scan(name, b, a_re, a_im, *, rev, xin=None, tb=256, lc=512):
    n_tok = b.shape[0]
    tb = _tile(n_tok, tb, 16)
    nt, nb, nc = n_tok // tb, tb // SUBLANE, S5_N // lc
    with_da = xin is not None

    def body(*refs):
        if with_da:
            b_ref, are_ref, aim_ref, x_ref, o_ref, da_ref, pw, last, of32, xf32 = refs
        else:
            b_ref, are_ref, aim_ref, o_ref, pw, last, of32 = refs
        t = pl.program_id(0)
        row = lax.broadcasted_iota(jnp.int32, (SUBLANE, lc), 0)

        def cmul(ar, ai, xr, xi):
            return ar * xr - ai * xi, ar * xi + ai * xr

        def hs(c, xr, xi):
            cs = pl.ds(c * lc, lc)
            for k, slot in ((1, 0), (2, 2), (4, 4)):
                ar, ai = pw[slot, :, cs], pw[slot + 1, :, cs]
                if rev:
                    sr, si = pltpu.roll(xr, SUBLANE - k, 0), pltpu.roll(xi, SUBLANE - k, 0)
                    keep = row < SUBLANE - k
                else:
                    sr, si = pltpu.roll(xr, k, 0), pltpu.roll(xi, k, 0)
                    keep = row >= k
                sr, si = jnp.where(keep, sr, 0.0), jnp.where(keep, si, 0.0)
                pr, pi = cmul(ar, ai, sr, si)
                xr, xi = xr + pr, xi + pi
            return xr, xi

        @pl.when(t == 0)
        def _():
            last[...] = jnp.zeros(last.shape, F32)
            if with_da:
                da_ref[...] = jnp.zeros(da_ref.shape, F32)
            for c in range(nc):
                cs = pl.ds(c * lc, lc)
                a1r = jnp.broadcast_to(are_ref[:, cs], (SUBLANE, lc))
                a1i = jnp.broadcast_to(aim_ref[:, cs], (SUBLANE, lc))
                a2r, a2i = cmul(a1r, a1i, a1r, a1i)
                a4r, a4i = cmul(a2r, a2i, a2r, a2i)
                for slot, v in enumerate((a1r, a1i, a2r, a2i, a4r, a4i)):
                    pw[slot, :, cs] = v
                first = SUBLANE - 1 if rev else 0
                pr, pi = hs(c, jnp.where(row == first, a1r, 0.0), jnp.where(row == first, a1i, 0.0))
                pw[6, :, cs] = pr
                pw[7, :, cs] = pi

        if with_da:
            xf32[...] = x_ref[...].astype(F32)
        edge = 0 if rev else SUBLANE - 1

        def blk(j, carry):
            jj = (nb - 1 - j) if rev else j
            r0 = pl.multiple_of(jj * SUBLANE, SUBLANE)
            for c in range(nc):
                cre, cim = pl.ds(c * lc, lc), pl.ds(S5_N + c * lc, lc)
                xr, xi = hs(c, b_ref[pl.ds(r0, SUBLANE), cre], b_ref[pl.ds(r0, SUBLANE), cim])
                cr = jnp.broadcast_to(last[edge:edge + 1, cre], (SUBLANE, lc))
                ci = jnp.broadcast_to(last[edge:edge + 1, cim], (SUBLANE, lc))
                pr, pi = cmul(pw[6, :, cre], pw[7, :, cre], cr, ci)
                xr, xi = xr + pr, xi + pi
                if with_da:
                    if rev:
                        qr, qi = pltpu.roll(xr, SUBLANE - 1, 0), pltpu.roll(xi, SUBLANE - 1, 0)
                        fill = row == SUBLANE - 1
                    else:
                        qr, qi = pltpu.roll(xr, 1, 0), pltpu.roll(xi, 1, 0)
                        fill = row == 0
                    qr, qi = jnp.where(fill, cr, qr), jnp.where(fill, ci, qi)
                    zr, zi = xf32[pl.ds(r0, SUBLANE), cre], xf32[pl.ds(r0, SUBLANE), cim]
                    da_ref[:, cre] += zr * qr + zi * qi
                    da_ref[:, cim] += zr * qi - zi * qr
                last[:, cre] = xr
                last[:, cim] = xi
                of32[pl.ds(r0, SUBLANE), cre] = xr
                of32[pl.ds(r0, SUBLANE), cim] = xi
            return carry

        lax.fori_loop(0, nb, blk, 0)
        o_ref[...] = of32[...].astype(o_ref.dtype)

    tmap = (lambda t: (nt - 1 - t, 0)) if rev else (lambda t: (t, 0))
    in_specs = [pl.BlockSpec((tb, 2 * S5_N), tmap), pl.BlockSpec((1, S5_N), lambda t: (0, 0)),
                pl.BlockSpec((1, S5_N), lambda t: (0, 0))]
    ops = [b, a_re, a_im]
    out_specs = [pl.BlockSpec((tb, 2 * S5_N), tmap)]
    out_shape = [jax.ShapeDtypeStruct((n_tok, 2 * S5_N), BF16)]
    scratch = [pltpu.VMEM((8, SUBLANE, S5_N), F32), pltpu.VMEM((SUBLANE, 2 * S5_N), F32),
               pltpu.VMEM((tb, 2 * S5_N), F32)]
    if with_da:
        in_specs.append(pl.BlockSpec((tb, 2 * S5_N), tmap))
        ops.append(xin)
        out_specs.append(pl.BlockSpec((SUBLANE, 2 * S5_N), lambda t: (0, 0)))
        out_shape.append(jax.ShapeDtypeStruct((SUBLANE, 2 * S5_N), F32))
        scratch.append(pltpu.VMEM((tb, 2 * S5_N), F32))
    vm = 2 * _nbytes((tb, 2 * S5_N), F32) + 4 * _nbytes((tb, 2 * S5_N), BF16) + 2 * _nbytes((tb, 2 * S5_N), F32)
    res = pl.pallas_call(
        body, name=name, grid=(nt,), in_specs=in_specs, out_specs=out_specs, out_shape=out_shape,
        scratch_shapes=scratch, compiler_params=_cparams(("arbitrary",), vm),
    )(*ops)
    return res if with_da else res[0]


def _gla_chunk_terms(la, q, k, rev):
    ri = lax.broadcasted_iota(jnp.int32, (GLA_C, GLA_C), 0)
    ci = lax.broadcasted_iota(jnp.int32, (GLA_C, GLA_C), 1)
    tri = (ri <= ci) if rev else (ri >= ci)
    trif = tri.astype(F32)
    bcum = _dot(trif, la, precision=HI)
    blast = jnp.sum(la, axis=0, keepdims=True)
    eb, enb = jnp.exp(bcum), jnp.exp(-bcum)
    ee = jnp.exp(blast - bcum)
    return tri, trif, blast, eb, enb, ee, q * eb, k * enb, k * ee


def _gla_fwd(name, z, la, *, rev):
    n_tok = z.shape[0]
    nck = n_tok // GLA_C
    scale = GLA_D ** -0.5

    def body(q_ref, k_ref, v_ref, la_ref, o_ref, s_ref, st):
        n = pl.program_id(1)

        @pl.when(n == 0)
        def _():
            st[...] = jnp.zeros(st.shape, F32)

        v = v_ref[...].astype(BF16)
        tri, _, blast, _, _, _, qd, kd, ke = _gla_chunk_terms(la_ref[...], q_ref[...] * scale, k_ref[...], rev)
        qd16 = qd.astype(BF16)
        sc = jnp.where(tri, _dot(qd16, kd.astype(BF16), _NT), 0.0)
        s16 = st[...].astype(BF16)
        s_ref[0, 0] = s16
        o_ref[...] = _dot(sc.astype(BF16), v) + _dot(qd16, s16, _NT)
        st[...] = st[...] * jnp.exp(blast) + _dot(v, ke.astype(BF16), _TN)

    cmap = (lambda h, n: nck - 1 - n) if rev else (lambda h, n: n)

    def zspec(cb):
        return pl.BlockSpec((GLA_C, GLA_D), lambda h, n: (cmap(h, n), cb + h))

    return pl.pallas_call(
        body, name=name, grid=(GLA_H, nck),
        in_specs=[zspec(ZC_GQ), zspec(ZC_GK), zspec(ZC_GV), zspec(0)],
        out_specs=[zspec(0), pl.BlockSpec((1, 1, GLA_D, GLA_D), lambda h, n: (h, n, 0, 0))],
        out_shape=[jax.ShapeDtypeStruct((n_tok, GLA_W), F32),
                   jax.ShapeDtypeStruct((GLA_H, nck, GLA_D, GLA_D), BF16)],
        scratch_shapes=[pltpu.VMEM((GLA_D, GLA_D), F32)],
        compiler_params=_cparams(("parallel", "arbitrary"), 4 * 1024 * 1024),
    )(z, z, z, la)


def _gla_bwd(name, z, la, s_prev, do, *, rev):
    n_tok = z.shape[0]
    nck = n_tok // GLA_C
    scale = GLA_D ** -0.5

    def body(q_ref, k_ref, v_ref, la_ref, s_ref, do_ref, dq_ref, dk_ref, dv_ref, dla_ref, dst):
        n = pl.program_id(1)

        @pl.when(n == 0)
        def _():
            dst[...] = jnp.zeros(dst.shape, F32)

        v = v_ref[...].astype(BF16)
        do16 = do_ref[...].astype(BF16)
        tri, trif, blast, eb, enb, ee, qd, kd, ke = _gla_chunk_terms(la_ref[...], q_ref[...] * scale, k_ref[...], rev)
        qd16, kd16, ke16 = qd.astype(BF16), kd.astype(BF16), ke.astype(BF16)
        sc16 = jnp.where(tri, _dot(qd16, kd16, _NT), 0.0).astype(BF16)
        dsc16 = jnp.where(tri, _dot(do16, v, _NT), 0.0).astype(BF16)
        s16 = s_ref[0, 0]
        dsn = dst[...]
        dsn16 = dsn.astype(BF16)
        dec = jnp.exp(blast)
        dqd = _dot(dsc16, kd16) + _dot(do16, s16)
        dkd = _dot(dsc16, qd16, _TN)
        dke = _dot(v, dsn16)
        dv_ref[...] = _dot(sc16, do16, _TN) + _dot(ke16, dsn16, _NT)
        ddec = jnp.sum(dsn * s16.astype(F32), axis=0, keepdims=True)
        dst[...] = dsn * dec + _dot(do16, qd16, _TN)
        dq_ref[...] = dqd * eb * scale
        dk_ref[...] = dkd * enb + dke * ee
        db = dqd * qd - dkd * kd - dke * ke
        dblast = jnp.sum(dke * ke, axis=0, keepdims=True) + ddec * dec
        dla_ref[...] = _dot(trif, db, _TN, precision=HI) + dblast

    cmap = (lambda h, n: n) if rev else (lambda h, n: nck - 1 - n)

    def zspec(cb):
        return pl.BlockSpec((GLA_C, GLA_D), lambda h, n: (cmap(h, n), cb + h))

    o4 = jax.ShapeDtypeStruct((n_tok, GLA_W), F32)
    return pl.pallas_call(
        body, name=name, grid=(GLA_H, nck),
        in_specs=[zspec(ZC_GQ), zspec(ZC_GK), zspec(ZC_GV), zspec(0),
                  pl.BlockSpec((1, 1, GLA_D, GLA_D), lambda h, n: (h, nck - 1 - n, 0, 0)), zspec(0)],
        out_specs=[zspec(0)] * 4, out_shape=[o4] * 4,
        scratch_shapes=[pltpu.VMEM((GLA_D, GLA_D), F32)],
        compiler_params=_cparams(("parallel", "arbitrary"), 4 * 1024 * 1024),
    )(z, z, z, la, s_prev, do)


def _attn_fwd(name, q, k, v, tq=256):
    qh, n_tok, d = q.shape
    grp = qh // k.shape[0]
    tq = _tile(n_tok, tq, 16)

    def body(q_ref, k_ref, v_ref, o_ref, l_ref):
        s = _dot(q_ref[0], k_ref[0], _NT)
        m = jnp.max(s, axis=-1, keepdims=True)
        p = jnp.exp(s - m)
        den = jnp.sum(p, axis=-1, keepdims=True)
        o_ref[0] = _dot(p.astype(BF16), v_ref[0]) / den
        l_ref[0] = m + jnp.log(den)

    vm = 4 * _nbytes((tq, n_tok), F32) + 8 * _nbytes((n_tok, LANE), BF16)
    return pl.pallas_call(
        body, name=name, grid=(qh, n_tok // tq),
        in_specs=[pl.BlockSpec((1, tq, d), lambda h, i: (h, i, 0)),
                  pl.BlockSpec((1, n_tok, d), lambda h, i: (h // grp, 0, 0)),
                  pl.BlockSpec((1, n_tok, d), lambda h, i: (h // grp, 0, 0))],
        out_specs=[pl.BlockSpec((1, tq, d), lambda h, i: (h, i, 0)), pl.BlockSpec((1, tq, 1), lambda h, i: (h, i, 0))],
        out_shape=[jax.ShapeDtypeStruct((qh, n_tok, d), F32), jax.ShapeDtypeStruct((qh, n_tok, 1), F32)],
        compiler_params=_cparams(("parallel", "parallel"), vm),
    )(q, k, v)


def _attn_bwd(name, q, k, v, o, lse, do, tq=256):
    qh, n_tok, d = q.shape
    kh = k.shape[0]
    grp = qh // kh
    tq = _tile(n_tok, tq, 16)
    nq = n_tok // tq

    def body(q_ref, k_ref, v_ref, o_ref, l_ref, do_ref, dq_ref, dk_ref, dv_ref):
        @pl.when((pl.program_id(1) == 0) & (pl.program_id(2) == 0))
        def _():
            dk_ref[...] = jnp.zeros(dk_ref.shape, F32)
            dv_ref[...] = jnp.zeros(dv_ref.shape, F32)

        q16, k16, v16 = q_ref[0], k_ref[0], v_ref[0]
        dof = do_ref[0]
        do16 = dof.astype(BF16)
        p = jnp.exp(_dot(q16, k16, _NT) - l_ref[0])
        dp = _dot(do16, v16, _NT)
        delta = jnp.sum(dof * o_ref[0], axis=-1, keepdims=True)
        ds16 = (p * (dp - delta)).astype(BF16)
        dq_ref[0] = _dot(ds16, k16)
        dk_ref[0] += _dot(ds16, q16, _TN)
        dv_ref[0] += _dot(p.astype(BF16), do16, _TN)

    qspec = pl.BlockSpec((1, tq, d), lambda g, j, i: (g * grp + j, i, 0))
    kspec = pl.BlockSpec((1, n_tok, d), lambda g, j, i: (g, 0, 0))
    vm = 6 * _nbytes((tq, n_tok), F32) + 12 * _nbytes((n_tok, LANE), F32)
    return pl.pallas_call(
        body, name=name, grid=(kh, grp, nq),
        in_specs=[qspec, kspec, kspec, qspec, pl.BlockSpec((1, tq, 1), lambda g, j, i: (g * grp + j, i, 0)), qspec],
        out_specs=[qspec, kspec, kspec],
        out_shape=[jax.ShapeDtypeStruct((qh, n_tok, d), F32), jax.ShapeDtypeStruct((kh, n_tok, d), F32),
                   jax.ShapeDtypeStruct((kh, n_tok, d), F32)],
        compiler_params=_cparams(("parallel", "arbitrary", "arbitrary"), vm),
    )(q, k, v, o, lse, do)


def _block_diag(t):
    g, a, b = t.shape
    eye = jnp.eye(g, dtype=t.dtype)
    return (t[:, :, None, :] * eye[:, None, :, None]).reshape(g * a, g * b)


def _s5_prep(lam_re, lam_im, log_dt, b_re, b_im, c_re, c_im):
    dt = jnp.exp(log_dt)[:, None]
    er = jnp.exp(lam_re * dt)
    a_re, a_im = er * jnp.cos(lam_im * dt), er * jnp.sin(lam_im * dt)
    nr, ni = a_re - 1.0, a_im
    den = lam_re * lam_re + lam_im * lam_im
    f_re, f_im = (nr * lam_re + ni * lam_im) / den, (ni * lam_re - nr * lam_im) / den
    bb_re = f_re[..., None] * b_re - f_im[..., None] * b_im
    bb_im = f_re[..., None] * b_im + f_im[..., None] * b_re
    w_in = jnp.concatenate([_block_diag(bb_re.transpose(0, 2, 1)), _block_diag(bb_im.transpose(0, 2, 1))], axis=1)
    w_out = jnp.concatenate([_block_diag(c_re.transpose(0, 2, 1)), -_block_diag(c_im.transpose(0, 2, 1))], axis=0)
    return a_re.reshape(1, S5_N), a_im.reshape(1, S5_N), w_in, w_out


def _rope_tables(n_tok):
    quarter = AT_D // 4
    d = np.arange(AT_D)
    e = d % (AT_D // 2)
    inv = (ROPE_BASE ** (-(e % quarter).astype(np.float64) * 2.0 / (AT_D // 2))).astype(np.float32)
    sign = np.where(e < quarter, -1.0, 1.0).astype(np.float32)
    rot = np.zeros((AT_D, AT_D), np.float32)
    rot[np.where(e < quarter, d + quarter, d - quarter), d] = 1.0
    pos = jnp.arange(n_tok, dtype=jnp.int32)
    axis_pos = jnp.where(jnp.asarray(d // (AT_D // 2) == 0)[None, :], (pos // GRID_W)[:, None], (pos % GRID_W)[:, None])
    ang = axis_pos.astype(F32) * jnp.asarray(inv)[None, :]
    return jnp.cos(ang), jnp.sin(ang) * jnp.asarray(sign)[None, :], rot


def _head_consts(n_tok, heads):
    cos, sin, rot = _rope_tables(n_tok)
    eye = np.eye(heads, dtype=np.float32)
    mean = np.kron(eye, np.full((AT_D, AT_D), 1.0 / AT_D, np.float32))
    return jnp.tile(cos, (1, heads)), jnp.tile(sin, (1, heads)), jnp.asarray(mean), jnp.asarray(np.kron(eye, rot))


def _pad_w_in(w):
    return jnp.concatenate([w[:, :2560], w[:, 2592:3360], w[:, 2560:2592], jnp.zeros((w.shape[0], 96), w.dtype)], axis=1)


def _unpad_w_in(g):
    return jnp.concatenate([g[:, :2560], g[:, 3328:3360], g[:, 2560:3328]], axis=1)


def _ffn_fwd(x, g, wg, wu, wd, d):
    n_tok = x.shape[0]
    fq = wg.shape[-1]
    h = _rowwise("ffn_norm", lambda xv, gv: _rms(xv, gv), [x], [g], [(D_MODEL, BF16)])[0]
    tm = _tile(n_tok, 512, 16)

    def up(h_ref, wg_ref, wu_ref, a_ref, b_ref, s_ref):
        hv = h_ref[...]
        av, bv = _dot(hv, wg_ref[0, 0]), _dot(hv, wu_ref[0, 0])
        a_ref[0] = av.astype(BF16)
        b_ref[0] = bv.astype(BF16)
        s_ref[0] = (av * _sigmoid(av) * bv).astype(BF16)

    wspec = pl.BlockSpec((1, 1, D_MODEL, fq), lambda i, j: (j, d, 0, 0))
    aspec = pl.BlockSpec((1, tm, fq), lambda i, j: (j, i, 0))
    act = jax.ShapeDtypeStruct((N_CHIPS, n_tok, fq), BF16)
    a, b, s = pl.pallas_call(
        up, name="ffn_up", grid=(n_tok // tm, N_CHIPS),
        in_specs=[pl.BlockSpec((tm, D_MODEL), lambda i, j: (i, 0)), wspec, wspec],
        out_specs=[aspec, aspec, aspec], out_shape=[act, act, act],
        compiler_params=_cparams(("parallel", "parallel"), 4 * D_MODEL * fq * 2 + 16 * tm * fq * 4),
    )(h, wg, wu)
    tn = _tile(D_MODEL, 512, LANE)

    def down(s_ref, w_ref, x_ref, o_ref):
        acc = _dot(s_ref[0], w_ref[0, 0])
        for j in range(1, N_CHIPS):
            acc = acc + _dot(s_ref[j], w_ref[j, 0])
        o_ref[...] = x_ref[...] + 0.5 * acc

    xo = pl.pallas_call(
        down, name="ffn_down", grid=(n_tok // tm, D_MODEL // tn),
        in_specs=[pl.BlockSpec((N_CHIPS, tm, fq), lambda i, n: (0, i, 0)),
                  pl.BlockSpec((N_CHIPS, 1, fq, tn), lambda i, n: (0, d, 0, n)),
                  pl.BlockSpec((tm, tn), lambda i, n: (i, n))],
        out_specs=pl.BlockSpec((tm, tn), lambda i, n: (i, n)), out_shape=jax.ShapeDtypeStruct((n_tok, D_MODEL), F32),
        compiler_params=_cparams(("parallel", "parallel"), 4 * N_CHIPS * (tm + tn) * fq * 2 + 6 * tm * tn * 4),
    )(s, wd, x)
    return xo, (x, h, a, b, s)


def _ffn_bwd(dy, dy16, saved, g, wg, wu, wd, d, into):
    x, h, a, b, s = saved
    n_tok = x.shape[0]
    fq = wg.shape[-1]
    tm = _tile(n_tok, 512, 16)

    def dact(dy_ref, w_ref, a_ref, b_ref, da_ref, db_ref):
        ds = 0.5 * _dot(dy_ref[...], w_ref[0, 0], _NT)
        av, bv = a_ref[0].astype(F32), b_ref[0].astype(F32)
        sg = _sigmoid(av)
        da_ref[0] = (ds * bv * sg * (1.0 + av * (1.0 - sg))).astype(BF16)
        db_ref[0] = (ds * av * sg).astype(BF16)

    aspec = pl.BlockSpec((1, tm, fq), lambda i, j: (j, i, 0))
    act = jax.ShapeDtypeStruct((N_CHIPS, n_tok, fq), BF16)
    da, db = pl.pallas_call(
        dact, name="ffn_dact", grid=(n_tok // tm, N_CHIPS),
        in_specs=[pl.BlockSpec((tm, D_MODEL), lambda i, j: (i, 0)),
                  pl.BlockSpec((1, 1, fq, D_MODEL), lambda i, j: (j, d, 0, 0)), aspec, aspec],
        out_specs=[aspec, aspec], out_shape=[act, act],
        compiler_params=_cparams(("parallel", "parallel"), 4 * D_MODEL * fq * 2 + 16 * tm * fq * 4),
    )(dy16, wd, a, b)

    any_spec = pl.BlockSpec(memory_space=pl.ANY)
    alias = {} if into is None else {2: 0}
    tn = _tile(D_MODEL, 512, LANE)

    def dw_down(s_ref, dy_ref, *rest):
        rest[-1][0, 0] = 0.5 * _dot(s_ref[0], dy_ref[...], _TN)

    dwd = pl.pallas_call(
        dw_down, name="ffn_dw_down", grid=(N_CHIPS, D_MODEL // tn),
        in_specs=[pl.BlockSpec((1, n_tok, fq), lambda j, n: (j, 0, 0)), pl.BlockSpec((n_tok, tn), lambda j, n: (0, n))]
        + ([] if into is None else [any_spec]),
        out_specs=pl.BlockSpec((1, 1, fq, tn), lambda j, n: (j, d, 0, n)),
        out_shape=jax.ShapeDtypeStruct((N_CHIPS, DEPTH, fq, D_MODEL), F32), input_output_aliases=alias,
        compiler_params=_cparams(("parallel", "parallel"), 6 * n_tok * (fq + tn) * 2 + 4 * fq * tn * 4),
    )(s, dy16, *([] if into is None else [into[2]]))

    def dw_up_call(t, prev):
        def dw_up(h_ref, t_ref, *rest):
            rest[-1][0, 0] = _dot(h_ref[...], t_ref[0], _TN)

        return pl.pallas_call(
            dw_up, name="ffn_dw_up", grid=(N_CHIPS, D_MODEL // tn),
            in_specs=[pl.BlockSpec((n_tok, tn), lambda j, m: (0, m)), pl.BlockSpec((1, n_tok, fq), lambda j, m: (j, 0, 0))]
            + ([] if prev is None else [any_spec]),
            out_specs=pl.BlockSpec((1, 1, tn, fq), lambda j, m: (j, d, m, 0)),
            out_shape=jax.ShapeDtypeStruct((N_CHIPS, DEPTH, D_MODEL, fq), F32), input_output_aliases=alias,
            compiler_params=_cparams(("parallel", "parallel"), 6 * n_tok * (fq + tn) * 2 + 4 * fq * tn * 4),
        )(h, t, *([] if prev is None else [prev]))

    dwg = dw_up_call(da, None if into is None else into[0])
    dwu = dw_up_call(db, None if into is None else into[1])
    tr = _tile(n_tok, 256, 16)

    def dh_norm(da_ref, db_ref, wg_ref, wu_ref, dy_ref, x_ref, g_ref, dx_ref, dx16_ref, dg_ref):
        dh = _dot(da_ref[0], wg_ref[0, 0], _NT) + _dot(db_ref[0], wu_ref[0, 0], _NT)
        for j in range(1, N_CHIPS):
            dh = dh + _dot(da_ref[j], wg_ref[j, 0], _NT) + _dot(db_ref[j], wu_ref[j, 0], _NT)
        dx, dg = _rms_bwd(x_ref[...], g_ref[...], dh)
        dx = dy_ref[...] + dx
        dx_ref[...] = dx
        dx16_ref[...] = dx.astype(BF16)

        @pl.when(pl.program_id(0) == 0)
        def _():
            dg_ref[...] = jnp.zeros(dg_ref.shape, F32)

        dg_ref[...] += dg

    tspec = pl.BlockSpec((N_CHIPS, tr, fq), lambda i: (0, i, 0))
    wspec = pl.BlockSpec((N_CHIPS, 1, D_MODEL, fq), lambda i: (0, d, 0, 0))
    rspec = pl.BlockSpec((tr, D_MODEL), lambda i: (i, 0))
    vspec = pl.BlockSpec((1, D_MODEL), lambda i: (0, 0))
    dx, dx16, dg = pl.pallas_call(
        dh_norm, name="ffn_dh_norm", grid=(n_tok // tr,),
        in_specs=[tspec, tspec, wspec, wspec, rspec, rspec, vspec], out_specs=[rspec, rspec, vspec],
        out_shape=[jax.ShapeDtypeStruct((n_tok, D_MODEL), F32), jax.ShapeDtypeStruct((n_tok, D_MODEL), BF16),
                   jax.ShapeDtypeStruct((1, D_MODEL), F32)],
        compiler_params=_cparams(("arbitrary",), 4 * N_CHIPS * D_MODEL * fq * 2 + 4 * N_CHIPS * tr * fq * 2
                                 + 16 * tr * D_MODEL * 4),
    )(da, db, wg, wu, dy, x, g)
    return dx, dx16, dg[0], (dwg, dwu, dwd)


def _alpha_pads(w_alpha):
    z = jnp.zeros((LANE - 2 * GLA_R, GLA_W), F32)
    wf = jnp.concatenate([w_alpha[0], jnp.zeros((GLA_R, GLA_W), F32), z], axis=0)
    wb = jnp.concatenate([jnp.zeros((GLA_R, GLA_W), F32), w_alpha[1], z], axis=0)
    return wf, wb


def _heads(t, nh):
    return t.reshape(t.shape[0], nh, AT_D).transpose(1, 0, 2)


def _unheads(t):
    return t.transpose(1, 0, 2).reshape(t.shape[1], t.shape[0] * AT_D)


def _mixer_fwd(x, w):
    n_tok = x.shape[0]
    h = _rowwise("mix_norm", lambda xv, gv: _rms(xv, gv), [x], [w['mix_norm']], [(D_MODEL, BF16)])[0]
    z = _mm("mix_in", h, w['w_in_p'])
    gp = _mm("mix_gate", h, w['w_merge_gate'], out_dtype=BF16)
    u = _cw(z, 0, S5_W)

    xs, yy = [], None
    for d, rev in ((0, False), (1, True)):
        a_re, a_im, wb, wc = w['s5'][d]
        bu = _mm("s5_bu", u, wb)
        xs.append(_scan("s5_scan_rev" if rev else "s5_scan_fwd", bu, a_re, a_im, rev=rev))
        yy = _mm("s5_y", xs[d], wc, add=yy)

    def s5_post(yv, uv, dv, wglu):
        y0 = yv + dv * uv
        y1 = _gelu(y0)
        t = _dot(y1.astype(BF16), wglu)
        return y0, t, y1 * _sigmoid(t)

    y0, t_glu, y_s5 = _rowwise("s5_post", s5_post, [yy, u], [w['s5_d'], w['s5_w_glu']],
                               [(S5_W, F32), (S5_W, F32), (S5_W, BF16)])

    wa_f, wa_b = _alpha_pads(w['gla_w_alpha'])
    ba_f, ba_b = w['gla_b_alpha'][0:1], w['gla_b_alpha'][1:2]
    zz = _cw(z, ZC_Z, LANE)

    def gla_prep(zv, waf, wab, bf, bb):
        z16 = zv.astype(BF16)
        lf = _dot(z16, waf.astype(BF16)) + bf
        lb = _dot(z16, wab.astype(BF16)) + bb
        return _log_sigmoid(lf) / GLA_TAU, _log_sigmoid(lb) / GLA_TAU

    la_f, la_b = _rowwise("gla_prep", gla_prep, [zz], [wa_f, wa_b, ba_f, ba_b], [(GLA_W, F32), (GLA_W, F32)])
    o_f, s_f = _gla_fwd("gla_fwd", z, la_f, rev=False)
    o_b, s_b = _gla_fwd("gla_rev", z, la_b, rev=True)
    gate = _cw(z, 4, GLA_W)

    def gla_post(of, ob, gv, gn):
        o = of + ob
        outs = []
        for hh in range(GLA_H):
            sl = slice(hh * GLA_D, (hh + 1) * GLA_D)
            gh = gv[:, sl]
            outs.append(_rms(o[:, sl], gn) * (gh * _sigmoid(gh)))
        return jnp.concatenate(outs, axis=1)

    y_gla = _rowwise("gla_post", gla_post, [o_f, o_b, gate], [w['gla_norm']], [(GLA_W, BF16)])[0]

    cq, sq, mq, rq = _head_consts(n_tok, AT_QH)
    ck, sk, mk, rk = _head_consts(n_tok, AT_KH)
    gq = jnp.tile(w['attn_q_norm'], (1, AT_QH))
    gk = jnp.tile(w['attn_k_norm'], (1, AT_KH))

    def rope_fwd(v, gain, cos, sin, mean, rot, scale):
        r = lax.rsqrt(_dot(v * v, mean, precision=HI) + EPS)
        vn = v * r * gain
        return (vn * cos + _dot(vn, rot, precision=HI) * sin) * scale

    def attn_prep(qv, kv, cqv, sqv, ckv, skv, gqv, gkv, mqv, rqv, mkv, rkv):
        return rope_fwd(qv, gqv, cqv, sqv, mqv, rqv, AT_D ** -0.5), rope_fwd(kv, gkv, ckv, skv, mkv, rkv, 1.0)

    q_raw, k_raw = _cw(z, 5, AT_W), _cw(z, ZC_AK, AT_KW)
    q16, k16 = _rowwise("attn_prep", attn_prep, [q_raw, k_raw, cq, sq, ck, sk], [gq, gk, mq, rq, mk, rk],
                        [(AT_W, BF16), (AT_KW, BF16)])
    v16 = z[:, ZC_AV * LANE:(ZC_AV + 1) * LANE].astype(BF16)
    qh, kh, vh = _heads(q16, AT_QH), _heads(k16, AT_KH), _heads(v16, AT_KH)
    o_at, lse = _attn_fwd("attn_fwd", qh, kh, vh)
    y_at = _unheads(o_at)

    p0 = _mm("mix_branch", y_s5, w['w_branch_s5'], out_dtype=BF16)
    p1 = _mm("mix_branch", y_gla, w['w_branch_gla'], out_dtype=BF16)
    p2 = _mm("mix_branch_f32", y_at, w['w_branch_attn'], out_dtype=BF16)

    def merge(gpv, a0, a1, a2, bias):
        g = _sigmoid(gpv.astype(F32) + bias)
        return (g[:, :D_MODEL] * a0.astype(F32) + g[:, D_MODEL:2 * D_MODEL] * a1.astype(F32)
                + g[:, 2 * D_MODEL:] * a2.astype(F32))

    merged = _rowwise("mix_merge", merge, [gp, p0, p1, p2], [w['b_merge_gate']], [(D_MODEL, BF16)])[0]
    xo = _mm("mix_out", merged, w['w_out'], add=x)
    saved = dict(x=x, h=h, z=z, gp=gp, xs=xs, y0=y0, t_glu=t_glu, y_s5=y_s5, la=(la_f, la_b), o=(o_f, o_b), s=(s_f, s_b),
                 y_gla=y_gla, qh=qh, kh=kh, vh=vh, o_at=o_at, lse=lse, y_at=y_at, p=(p0, p1, p2), merged=merged)
    return xo, saved


def _mixer_bwd(dy, dy16, sv, w, s5_raw):
    z, h = sv['z'], sv['h']
    n_tok = z.shape[0]
    grads = {}
    dmerged = _mm("mix_dmerged", dy16, w['w_out'], tb=True, out_dtype=BF16)
    grads['w_out'] = _mm("mix_dwout", sv['merged'], dy16, ta=True)

    def dmerge(dm, gpv, a0, a1, a2, bias):
        dm = dm.astype(F32)
        g = _sigmoid(gpv.astype(F32) + bias)
        ps = (a0.astype(F32), a1.astype(F32), a2.astype(F32))
        dps, dgs = [], []
        for k in range(3):
            gk = g[:, k * D_MODEL:(k + 1) * D_MODEL]
            dps.append(dm * gk)
            dgs.append(dm * ps[k] * gk * (1.0 - gk))
        dgp = jnp.concatenate(dgs, axis=1)
        return dps[0], dps[1], dps[2], dgp, jnp.sum(dgp, axis=0, keepdims=True)

    dp0, dp1, dp2, dgp, dbm = _rowwise("mix_dmerge", dmerge, [dmerged, sv['gp'], *sv['p']], [w['b_merge_gate']],
                                       [(D_MODEL, BF16)] * 3 + [(3 * D_MODEL, BF16)], accs=[(1, 3 * D_MODEL)])
    grads['b_merge_gate'] = dbm[0]
    grads['w_branch_s5'] = _mm("mix_dbranch", sv['y_s5'], dp0, ta=True)
    grads['w_branch_gla'] = _mm("mix_dbranch", sv['y_gla'], dp1, ta=True)
    grads['w_branch_attn'] = _mm("mix_dbranch_f32", sv['y_at'], dp2, ta=True)
    dy_s5 = _mm("mix_dy", dp0, w['w_branch_s5'], tb=True)
    dy_gla = _mm("mix_dy", dp1, w['w_branch_gla'], tb=True)
    dy_at = _mm("mix_dy", dp2, w['w_branch_attn'], tb=True)
    grads['w_merge_gate'] = _mm("mix_dwgate", h, dgp, ta=True)
    dh = _mm("mix_dh_gate", dgp, w['w_merge_gate'], tb=True)
    u = _cw(z, 0, S5_W)

    def s5_post_bwd(dyv, y0, t, uv, dv, wglu):
        sg = _sigmoid(t)
        y1 = _gelu(y0)
        dt = dyv * y1 * sg * (1.0 - sg)
        dy1 = dyv * sg + _dot(dt.astype(BF16), wglu, _NT)
        dy0 = dy1 * _gelu_grad(y0)
        return dy0, dt, y1, dy0 * dv, jnp.sum(dy0 * uv, axis=0, keepdims=True)

    dy0, dt_glu, y1, du, dd = _rowwise("s5_post_bwd", s5_post_bwd, [dy_s5, sv['y0'], sv['t_glu'], u],
                                       [w['s5_d'], w['s5_w_glu']],
                                       [(S5_W, BF16), (S5_W, BF16), (S5_W, BF16), (S5_W, F32)], accs=[(1, S5_W)])
    grads['s5_d'] = dd[0]
    grads['s5_w_glu'] = _mm("s5_dwglu", y1, dt_glu, ta=True)
    s5_cot = []
    for d, rev in ((0, False), (1, True)):
        a_re, a_im, wb, wc = w['s5'][d]
        gx = _mm("s5_gx", dy0, wc, tb=True)
        gs, da = _scan("s5_adj_fwd" if rev else "s5_adj_rev", gx, a_re, -a_im, rev=not rev, xin=sv['xs'][d])
        dwc = _mm("s5_dwc", sv['xs'][d], dy0, ta=True)
        dwb = _mm("s5_dwb", u, gs, ta=True)
        du = _mm("s5_du", gs, wb, tb=True, add=du)
        da = jnp.sum(da, axis=0, keepdims=True)
        s5_cot.append((da[:, :S5_N], da[:, S5_N:], dwb, dwc))
    for d in range(2):
        _, pull = jax.vjp(_s5_prep, *s5_raw[d])
        s5_cot[d] = pull(s5_cot[d])
    for k, nm in enumerate(('s5_lambda_re', 's5_lambda_im', 's5_log_dt', 's5_b_re', 's5_b_im', 's5_c_re', 's5_c_im')):
        grads[nm] = jnp.stack([s5_cot[0][k], s5_cot[1][k]])

    gate = _cw(z, 4, GLA_W)

    def gla_post_bwd(dyv, of, ob, gv, gn):
        o = of + ob
        dos, dgates = [], []
        dgn = jnp.zeros((1, GLA_D), F32)
        for hh in range(GLA_H):
            sl = slice(hh * GLA_D, (hh + 1) * GLA_D)
            gh, dyh = gv[:, sl], dyv[:, sl]
            sg = _sigmoid(gh)
            dgates.append(dyh * _rms(o[:, sl], gn) * sg * (1.0 + gh * (1.0 - sg)))
            dx, dg = _rms_bwd(o[:, sl], gn, dyh * gh * sg)
            dos.append(dx)
            dgn = dgn + dg
        return jnp.concatenate(dos, axis=1), jnp.concatenate(dgates, axis=1), dgn

    do, dgate, dgn = _rowwise("gla_post_bwd", gla_post_bwd, [dy_gla, *sv['o'], gate], [w['gla_norm']],
                              [(GLA_W, F32), (GLA_W, F32)], accs=[(1, GLA_D)])
    grads['gla_norm'] = dgn[0]
    dq_f, dk_f, dv_f, dla_f = _gla_bwd("gla_bwd_fwd", z, sv['la'][0], sv['s'][0], do, rev=False)
    dq_b, dk_b, dv_b, dla_b = _gla_bwd("gla_bwd_rev", z, sv['la'][1], sv['s'][1], do, rev=True)
    wa_f, wa_b = _alpha_pads(w['gla_w_alpha'])
    ba_f, ba_b = w['gla_b_alpha'][0:1], w['gla_b_alpha'][1:2]
    zz = _cw(z, ZC_Z, LANE)

    def gla_prep_bwd(dlf, dlb, zv, waf, wab, bf, bb):
        z16 = zv.astype(BF16)
        waf16, wab16 = waf.astype(BF16), wab.astype(BF16)
        xf = _dot(z16, waf16) + bf
        xb = _dot(z16, wab16) + bb
        df = dlf * (_sigmoid(-xf) / GLA_TAU)
        db = dlb * (_sigmoid(-xb) / GLA_TAU)
        dz = _dot(df.astype(BF16), waf16, _NT) + _dot(db.astype(BF16), wab16, _NT)
        return df, db, dz, jnp.sum(df, axis=0, keepdims=True), jnp.sum(db, axis=0, keepdims=True)

    dlog_f, dlog_b, dzz, dba_f, dba_b = _rowwise(
        "gla_prep_bwd", gla_prep_bwd, [dla_f, dla_b, zz], [wa_f, wa_b, ba_f, ba_b],
        [(GLA_W, BF16), (GLA_W, BF16), (LANE, F32)], accs=[(1, GLA_W), (1, GLA_W)])
    dwa_f = _mm("gla_dwa", zz, dlog_f, ta=True)
    dwa_b = _mm("gla_dwa", zz, dlog_b, ta=True)
    grads['gla_w_alpha'] = jnp.stack([dwa_f[:GLA_R], dwa_b[GLA_R:2 * GLA_R]])
    grads['gla_b_alpha'] = jnp.concatenate([dba_f, dba_b], axis=0)

    dqh, dkh, dvh = _attn_bwd("attn_bwd", sv['qh'], sv['kh'], sv['vh'], sv['o_at'], sv['lse'], _heads(dy_at, AT_QH))
    cq, sq, mq, rq = _head_consts(n_tok, AT_QH)
    ck, sk, mk, rk = _head_consts(n_tok, AT_KH)
    gq = jnp.tile(w['attn_q_norm'], (1, AT_QH))
    gk = jnp.tile(w['attn_k_norm'], (1, AT_KH))

    def rope_bwd(dr, v, gain, cos, sin, mean, rot, scale):
        dr = dr * scale
        dvn = dr * cos + _dot(dr * sin, rot, _NT, precision=HI)
        r = lax.rsqrt(_dot(v * v, mean, precision=HI) + EPS)
        vh = v * r
        dg = jnp.sum(dvn * vh, axis=0, keepdims=True)
        dvh = dvn * gain
        return r * (dvh - vh * _dot(dvh * vh, mean, precision=HI)), dg

    def attn_prep_bwd(dq, dk, qv, kv, cqv, sqv, ckv, skv, gqv, gkv, mqv, rqv, mkv, rkv):
        dqo, dgq = rope_bwd(dq, qv, gqv, cqv, sqv, mqv, rqv, AT_D ** -0.5)
        dko, dgk = rope_bwd(dk, kv, gkv, ckv, skv, mkv, rkv, 1.0)
        return dqo, dko, dgq, dgk

    q_raw, k_raw = _cw(z, 5, AT_W), _cw(z, ZC_AK, AT_KW)
    daq, dak, dgq, dgk = _rowwise("attn_prep_bwd", attn_prep_bwd, [_unheads(dqh), _unheads(dkh), q_raw, k_raw, cq, sq, ck, sk],
                                  [gq, gk, mq, rq, mk, rk], [(AT_W, F32), (AT_KW, F32)], accs=[(1, AT_W), (1, AT_KW)])
    grads['attn_q_norm'] = dgq.reshape(AT_QH, AT_D).sum(axis=0)
    grads['attn_k_norm'] = dgk.reshape(AT_KH, AT_D).sum(axis=0)
    dav = _unheads(dvh)

    def assemble(duv, qf, qb, kf, kb, vf, vb, dg, aq, ak, av, zzv):
        return jnp.concatenate([duv, qf + qb, kf + kb, vf + vb, dg, aq, ak, av, zzv], axis=1)

    dz = _rowwise("mix_dz", assemble, [du, dq_f, dq_b, dk_f, dk_b, dv_f, dv_b, dgate, daq, dak, dav, dzz], [],
                  [(ZP_W, BF16)])[0]
    grads['w_in'] = _unpad_w_in(_mm("mix_dwin", h, dz, ta=True))
    dh = _mm("mix_dh_in", dz, w['w_in_p'], tb=True, add=dh)

    def dnorm(dyv, xv, dhv, gv):
        dx, dg = _rms_bwd(xv, gv, dhv)
        dx = dyv + dx
        return dx, dx, dg

    dx, dx16, dg = _rowwise("mix_dnorm", dnorm, [dy, sv['x'], dh], [w['mix_norm']], [(D_MODEL, F32), (D_MODEL, BF16)],
                            accs=[(1, D_MODEL)])
    grads['mix_norm'] = dg[0]
    return dx, dx16, grads


FFN_NAMES = [p + k for p in ('ffn1_', 'ffn2_') for k in ('w_gate', 'w_up', 'w_down')]
MISC_SHARDED = [n for n in SHARDED if n not in FFN_NAMES]


def _layer_weights(ws, rep, i):
    w = {k: _from_shards(ws[k][:, i], SHARD_AXIS[k] - 1) for k in MISC_SHARDED}
    for k in ('ffn1_norm', 'mix_norm', 'ffn2_norm', 's5_d', 'gla_norm', 'attn_q_norm', 'attn_k_norm', 'b_merge_gate'):
        w[k] = rep[k][i].reshape(1, -1)
    w['w_in_p'] = _pad_w_in(w['w_in'])
    s5_raw = [tuple(rep[k][i, d] for k in ('s5_lambda_re', 's5_lambda_im', 's5_log_dt', 's5_b_re', 's5_b_im',
                                            's5_c_re', 's5_c_im')) for d in range(2)]
    w['s5'] = [_s5_prep(*s5_raw[d]) for d in range(2)]
    return w, s5_raw


def _device_step(x, target, ws, rep):
    lw = [_layer_weights(ws, rep, i) for i in range(DEPTH)]
    ffn = {p: (ws[p + 'w_gate'], ws[p + 'w_up'], ws[p + 'w_down']) for p in ('ffn1_', 'ffn2_')}
    saved = []
    for i in range(DEPTH):
        w, _ = lw[i]
        x, s1 = _ffn_fwd(x, w['ffn1_norm'], *ffn['ffn1_'], i)
        x, s2 = _mixer_fwd(x, w)
        x, s3 = _ffn_fwd(x, w['ffn2_norm'], *ffn['ffn2_'], i)
        saved.append((s1, s2, s3))
    gfin = rep['final_norm'].reshape(1, -1)

    def head(xv, tv, gv):
        e = _rms(xv, gv) - tv
        dx, dg = _rms_bwd(xv, gv, e * (1.0 / D_MODEL))
        part = jnp.sum(e * e, axis=0, keepdims=True)
        return dx, dx, dg, part

    dx, dx16, dgfin, part = _rowwise("loss_head", head, [x, target], [gfin], [(D_MODEL, F32), (D_MODEL, BF16)],
                                     accs=[(1, D_MODEL), (1, D_MODEL)])
    loss = (0.5 / D_MODEL) * jnp.sum(part)
    per_layer = []
    dffn = {'ffn1_': None, 'ffn2_': None}
    for i in reversed(range(DEPTH)):
        w, s5_raw = lw[i]
        s1, s2, s3 = saved[i]
        dx, dx16, n3, dffn['ffn2_'] = _ffn_bwd(dx, dx16, s3, w['ffn2_norm'], *ffn['ffn2_'], i, dffn['ffn2_'])
        dx, dx16, g = _mixer_bwd(dx, dx16, s2, w, s5_raw)
        dx, dx16, n1, dffn['ffn1_'] = _ffn_bwd(dx, dx16, s1, w['ffn1_norm'], *ffn['ffn1_'], i, dffn['ffn1_'])
        g['ffn2_norm'], g['ffn1_norm'] = n3, n1
        per_layer.append(g)
    per_layer.reverse()
    stacked = {k: jnp.stack([per_layer[i][k] for i in range(DEPTH)]) for k in per_layer[0]}
    gs = {k: _to_shards(stacked[k], SHARD_AXIS[k]) for k in MISC_SHARDED}
    for p in ('ffn1_', 'ffn2_'):
        gs[p + 'w_gate'], gs[p + 'w_up'], gs[p + 'w_down'] = dffn[p]
    grep = {k: stacked[k] for k in REPLICATED if k != 'final_norm'}
    grep['final_norm'] = dgfin[0]
    return loss, dx, gs, grep


PACK_C = 1024
ROW_TILE = 512


def _rows_of(shape):
    return -(-int(np.prod(shape)) // PACK_C)


def _as_rows(t, lead=0):
    head = t.shape[:lead]
    flat = t.reshape(head + (-1,))
    rows = -(-flat.shape[-1] // PACK_C)
    pad = rows * PACK_C - flat.shape[-1]
    if pad:
        flat = jnp.pad(flat, [(0, 0)] * lead + [(0, pad)])
    return flat.reshape(head + (rows, PACK_C))


def _pack(items, total_rows, lead=0):
    parts = [_as_rows(t, lead) for t in items]
    used = sum(p.shape[lead] for p in parts)
    if total_rows > used:
        parts.append(jnp.zeros(parts[0].shape[:lead] + (total_rows - used, PACK_C), parts[0].dtype))
    return jnp.concatenate(parts, axis=lead)


def _unpack(rows, shapes, lead=0):
    out, r0 = [], 0
    head = rows.shape[:lead]
    for shp in shapes:
        n, size = _rows_of(shp), int(np.prod(shp))
        piece = lax.slice_in_dim(rows, r0, r0 + n, axis=lead).reshape(head + (n * PACK_C,))
        out.append(lax.slice_in_dim(piece, 0, size, axis=lead).reshape(head + tuple(shp)))
        r0 += n
    return out


def _round_up(n, m):
    return -(-n // m) * m


def _to_shards(g, axis):
    shp = g.shape
    g = g.reshape(shp[:axis] + (N_CHIPS, shp[axis] // N_CHIPS) + shp[axis + 1:])
    return jnp.moveaxis(g, axis, 0)


def _from_shards(s, axis):
    s = jnp.moveaxis(s, 0, axis)
    shp = s.shape
    return s.reshape(shp[:axis] + (shp[axis] * shp[axis + 1],) + shp[axis + 2:])


_HBM = pl.BlockSpec(memory_space=pltpu.HBM)


def _place():
    x, y, c = lax.axis_index("x"), lax.axis_index("y"), lax.axis_index("c")
    chips = [(1 - x, y), (x, 1 - y), (1 - x, 1 - y)]
    return x, y, c, chips


def _rcopy(src, dst, send, recv, k, dev):
    return pltpu.make_async_remote_copy(src_ref=src, dst_ref=dst, send_sem=send.at[k], recv_sem=recv.at[k],
                                        device_id=dev, device_id_type=MESH)


def _comm_call(name, body, arrays, out_shapes, n_sems, n_local):
    return pl.pallas_call(
        body, name=name, in_specs=[_HBM] * len(arrays), out_specs=[_HBM] * len(out_shapes), out_shape=out_shapes,
        scratch_shapes=[pltpu.SemaphoreType.DMA((n_sems,)), pltpu.SemaphoreType.DMA((n_sems,)),
                        pltpu.SemaphoreType.DMA((max(n_local, 1),))],
    )(*arrays)


def _ag_chips(name, packs):
    n = len(packs)

    def body(*refs):
        p, o, (send, recv, lsem) = refs[:n], refs[n:2 * n], refs[2 * n:]
        x, y, c, chips = _place()
        me = 2 * x + y
        mine = [pltpu.make_async_copy(p[a], o[a].at[me], lsem.at[a]) for a in range(n)]
        first = [_rcopy(p[a].at[c], o[a].at[me, c], send, recv, 6 * a + k, (px, py, c))
                 for k, (px, py) in enumerate(chips) for a in range(n)]
        for cp in mine + first:
            cp.start()
        passed = []
        for k, (px, py) in enumerate(chips):
            for a in range(n):
                blk = o[a].at[2 * px + py, c]
                _rcopy(blk, blk, send, recv, 6 * a + k, (px, py, c)).wait_recv()
                fw = _rcopy(blk, blk, send, recv, 6 * a + 3 + k, (x, y, 1 - c))
                fw.start()
                passed.append(fw)
        for k, (px, py) in enumerate(chips):
            for a in range(n):
                blk = o[a].at[2 * px + py, 1 - c]
                _rcopy(blk, blk, send, recv, 6 * a + 3 + k, (x, y, 1 - c)).wait_recv()
        for cp in first + passed:
            cp.wait_send()
        for cp in mine:
            cp.wait()

    return _comm_call(name, body, packs, [jax.ShapeDtypeStruct((N_CHIPS,) + t.shape, t.dtype) for t in packs], 6 * n, n)


def _pair_swap(name, gs):
    n = len(gs)
    ns = gs[0].shape[0]

    def body(*refs):
        g, r, (send, recv, _) = refs[:n], refs[n:2 * n], refs[2 * n:]
        x, y, c, _ = _place()
        cps = [_rcopy(g[a].at[j, 1 - c], r[a].at[j], send, recv, ns * a + j, (x, y, 1 - c))
               for a in range(n) for j in range(ns)]
        for cp in cps:
            cp.start()
        for cp in cps:
            cp.wait_recv()
        for cp in cps:
            cp.wait_send()

    return _comm_call(name, body, gs, [jax.ShapeDtypeStruct((ns,) + t.shape[2:], t.dtype) for t in gs], ns * n, 0)


def _chip_exchange(name, parts):
    n = len(parts)

    def body(*refs):
        p, o, (send, recv, lsem) = refs[:n], refs[n:2 * n], refs[2 * n:]
        x, y, c, chips = _place()
        me = 2 * x + y
        mine = [pltpu.make_async_copy(p[a].at[me], o[a].at[me], lsem.at[a]) for a in range(n)]
        cps = [_rcopy(p[a].at[2 * px + py], o[a].at[me], send, recv, 3 * a + k, (px, py, c))
               for k, (px, py) in enumerate(chips) for a in range(n)]
        for cp in mine + cps:
            cp.start()
        for k, (px, py) in enumerate(chips):
            for a in range(n):
                blk = o[a].at[2 * px + py]
                _rcopy(blk, blk, send, recv, 3 * a + k, (px, py, c)).wait_recv()
        for cp in cps:
            cp.wait_send()
        for cp in mine:
            cp.wait()

    return _comm_call(name, body, parts, [jax.ShapeDtypeStruct(t.shape, t.dtype) for t in parts], 3 * n, n)


def _pair_share(name, reds):
    n = len(reds)

    def body(*refs):
        r, o, (send, recv, lsem) = refs[:n], refs[n:2 * n], refs[2 * n:]
        x, y, c, _ = _place()
        mine = [pltpu.make_async_copy(r[a], o[a].at[c], lsem.at[a]) for a in range(n)]
        cps = [_rcopy(r[a], o[a].at[c], send, recv, a, (x, y, 1 - c)) for a in range(n)]
        for cp in mine + cps:
            cp.start()
        for a in range(n):
            blk = o[a].at[1 - c]
            _rcopy(blk, blk, send, recv, a, (x, y, 1 - c)).wait_recv()
        for cp in cps:
            cp.wait_send()
        for cp in mine:
            cp.wait()

    return _comm_call(name, body, reds, [jax.ShapeDtypeStruct((2,) + t.shape, t.dtype) for t in reds], n, n)


def _add_own_half(name, g, r):
    ns, _, rh, cc = g.shape
    tl = _tile(rh, ROW_TILE, 16)
    core = lax.axis_index("c").astype(jnp.int32).reshape(1)

    def body(c_ref, g_ref, r_ref, o_ref):
        o_ref[...] = (g_ref[0] + r_ref[...]).astype(o_ref.dtype)

    return pl.pallas_call(
        body, name=name,
        grid_spec=pltpu.PrefetchScalarGridSpec(
            num_scalar_prefetch=1, grid=(ns, rh // tl),
            in_specs=[pl.BlockSpec((1, 1, tl, cc), lambda j, i, c_ref: (j, c_ref[0], i, 0)),
                      pl.BlockSpec((1, tl, cc), lambda j, i, c_ref: (j, i, 0))],
            out_specs=pl.BlockSpec((1, tl, cc), lambda j, i, c_ref: (j, i, 0))),
        out_shape=jax.ShapeDtypeStruct(r.shape, BF16),
        compiler_params=_cparams(("parallel", "parallel"), 8 * tl * cc * 4),
    )(core, g, r)


def _sum_slots(name, t):
    ns, rh, cc = t.shape
    tl = _tile(rh, ROW_TILE, 16)

    def body(t_ref, o_ref):
        acc = t_ref[0].astype(F32)
        for j in range(1, ns):
            acc = acc + t_ref[j].astype(F32)
        o_ref[...] = acc

    return pl.pallas_call(
        body, name=name, grid=(rh // tl,), in_specs=[pl.BlockSpec((ns, tl, cc), lambda i: (0, i, 0))],
        out_specs=pl.BlockSpec((tl, cc), lambda i: (i, 0)), out_shape=jax.ShapeDtypeStruct((rh, cc), F32),
        compiler_params=_cparams(("parallel",), 2 * (ns + 1) * tl * cc * 4),
    )(t)


def _reduce_scatter(gs):
    swapped = _pair_swap("rs_pair_swap", gs)
    parts = [_add_own_half("rs_add_pair", g, r) for g, r in zip(gs, swapped)]
    reds = [_sum_slots("rs_sum_chips", t) for t in _chip_exchange("rs_chip_exchange", parts)]
    return _pair_share("rs_pair_share", reds)


def _adamw(name, g, w, m, v):
    c1 = 1.0 - ADAM_B1 ** ADAM_STEP
    c2 = 1.0 - ADAM_B2 ** ADAM_STEP

    def upd(gv, wv, mv, vv):
        mn = ADAM_B1 * mv + (1.0 - ADAM_B1) * gv
        vn = ADAM_B2 * vv + (1.0 - ADAM_B2) * (gv * gv)
        delta = -ADAM_LR * ((mn / c1) / (jnp.sqrt(vn / c2) + ADAM_EPS) + ADAM_WD * wv)
        return delta, mn, vn

    return _rowwise(name, upd, [g, w, m, v], [], [(g.shape[1], F32)] * 3, tl=ROW_TILE)


def _train_step(a):
    x, target = a['x'][0], a['loss_target'][0]

    names = BF16_GATHER + F32_GATHER
    got = _ag_chips("ag_weights", [a[n].astype(BF16) for n in BF16_GATHER] + [a[n] for n in F32_GATHER])
    ws = dict(zip(names, got))

    loss, gx, gs, grep = _device_step(x, target, ws, {n: a[n] for n in REPLICATED})
    loss = lax.psum(loss, ("x", "y", "c"))

    misc_shapes = [a[n].shape[1:] for n in MISC_SHARDED]
    rep_shapes = [a[n].shape for n in REPLICATED]
    misc_rows = sum(_rows_of(s) for s in misc_shapes)
    piece = _round_up(-(-sum(_rows_of(s) for s in rep_shapes) // (2 * N_CHIPS)), SUBLANE)
    rep_rows = 2 * N_CHIPS * piece
    rep_pack = _pack([grep[n] for n in REPLICATED], rep_rows).reshape(N_CHIPS, 2, piece, PACK_C)
    total = _round_up(misc_rows + piece, ROW_TILE // 2)
    misc = _pack([gs[n] for n in MISC_SHARDED] + [rep_pack], total, lead=2)
    reds = _reduce_scatter([gs[n] for n in FFN_NAMES] + [misc])
    grad = dict(zip(FFN_NAMES, reds[:-1]))
    grad.update(zip(MISC_SHARDED, _unpack(reds[-1], misc_shapes, lead=1)))
    rep_mine = reds[-1][:, misc_rows:misc_rows + piece]
    rep_all = _ag_chips("ag_replicated", [rep_mine])[0].reshape(rep_rows, PACK_C)

    outs = {}
    for n in SHARDED:
        two_d = (-1, a[n].shape[-1])
        res3 = _adamw("adamw", *(t.reshape(two_d) for t in (grad[n], a[n], a['m_' + n], a['v_' + n])))
        for kind, arr in zip(('grad', 'delta', 'new_m', 'new_v'), (grad[n], *res3)):
            outs[kind + '_' + n] = arr.reshape(a[n].shape)
    w, m, v = (_pack([a[p + n] for n in REPLICATED], rep_rows) for p in ('', 'm_', 'v_'))
    res3 = _adamw("adamw_replicated", rep_all, w, m, v)
    for kind, arr in zip(('grad', 'delta', 'new_m', 'new_v'), (rep_all, *res3)):
        for n, t in zip(REPLICATED, _unpack(arr, rep_shapes)):
            outs[kind + '_' + n] = t
    res = [loss, gx[None]]
    for kind in ('grad', 'delta', 'new_m', 'new_v'):
        res += [outs[kind + '_' + n] for n in W_NAMES]
    return tuple(res)


def kernel(x, ffn1_norm, ffn1_w_gate, ffn1_w_up, ffn1_w_down, mix_norm, w_in, s5_lambda_re, s5_lambda_im, s5_log_dt, s5_b_re, s5_b_im, s5_c_re, s5_c_im, s5_d, s5_w_glu, gla_w_alpha, gla_b_alpha, gla_norm, attn_q_norm, attn_k_norm, w_branch_s5, w_branch_gla, w_branch_attn, w_merge_gate, b_merge_gate, w_out, ffn2_norm, ffn2_w_gate, ffn2_w_up, ffn2_w_down, final_norm, loss_target, m_ffn1_norm, m_ffn1_w_gate, m_ffn1_w_up, m_ffn1_w_down, m_mix_norm, m_w_in, m_s5_lambda_re, m_s5_lambda_im, m_s5_log_dt, m_s5_b_re, m_s5_b_im, m_s5_c_re, m_s5_c_im, m_s5_d, m_s5_w_glu, m_gla_w_alpha, m_gla_b_alpha, m_gla_norm, m_attn_q_norm, m_attn_k_norm, m_w_branch_s5, m_w_branch_gla, m_w_branch_attn, m_w_merge_gate, m_b_merge_gate, m_w_out, m_ffn2_norm, m_ffn2_w_gate, m_ffn2_w_up, m_ffn2_w_down, m_final_norm, v_ffn1_norm, v_ffn1_w_gate, v_ffn1_w_up, v_ffn1_w_down, v_mix_norm, v_w_in, v_s5_lambda_re, v_s5_lambda_im, v_s5_log_dt, v_s5_b_re, v_s5_b_im, v_s5_c_re, v_s5_c_im, v_s5_d, v_s5_w_glu, v_gla_w_alpha, v_gla_b_alpha, v_gla_norm, v_attn_q_norm, v_attn_k_norm, v_w_branch_s5, v_w_branch_gla, v_w_branch_attn, v_w_merge_gate, v_b_merge_gate, v_w_out, v_ffn2_norm, v_ffn2_w_gate, v_ffn2_w_up, v_ffn2_w_down, v_final_norm):
    return _train_step(dict(locals()))
```

```python
import functools
import math

import jax
import jax.numpy as jnp
import numpy as np
from jax import lax
from jax.experimental import pallas as pl
from jax.experimental.pallas import tpu as pltpu

F32 = jnp.float32
BF16 = jnp.bfloat16
HI = lax.Precision.HIGHEST

D_MODEL = 1024
D_FF = 2816
DEPTH = 2
EPS = 1e-6
S5_G, S5_H, S5_P = 32, 16, 64
S5_W = S5_G * S5_H
S5_N = S5_G * S5_P
GLA_H, GLA_D = 4, 128
GLA_W = GLA_H * GLA_D
GLA_R = 16
GLA_TAU = 16.0
GLA_C = 64
AT_QH, AT_KH, AT_D = 8, 2, 64
AT_W = AT_QH * AT_D
AT_KW = AT_KH * AT_D
GRID_W = 64
ROPE_BASE = 10000.0
ZP_W = 3456
ZC_GQ, ZC_GK, ZC_GV, ZC_AK, ZC_AV, ZC_Z = 4, 8, 12, 24, 25, 26

ADAM_LR, ADAM_B1, ADAM_B2, ADAM_EPS, ADAM_WD, ADAM_STEP = 0.001, 0.9, 0.999, 1e-08, 0.01, 10

VMEM_LIMIT_V7X = 56 * 1024 * 1024
LANE = 128
SUBLANE = 8

W_NAMES = ['ffn1_norm', 'ffn1_w_gate', 'ffn1_w_up', 'ffn1_w_down', 'mix_norm', 'w_in', 's5_lambda_re', 's5_lambda_im',
           's5_log_dt', 's5_b_re', 's5_b_im', 's5_c_re', 's5_c_im', 's5_d', 's5_w_glu', 'gla_w_alpha', 'gla_b_alpha',
           'gla_norm', 'attn_q_norm', 'attn_k_norm', 'w_branch_s5', 'w_branch_gla', 'w_branch_attn', 'w_merge_gate',
           'b_merge_gate', 'w_out', 'ffn2_norm', 'ffn2_w_gate', 'ffn2_w_up', 'ffn2_w_down', 'final_norm']
SHARD_AXIS = {'ffn1_w_gate': 2, 'ffn1_w_up': 2, 'ffn1_w_down': 1, 'w_in': 2, 's5_w_glu': 1, 'gla_w_alpha': 3,
              'gla_b_alpha': 2, 'w_branch_s5': 2, 'w_branch_gla': 2, 'w_branch_attn': 2, 'w_merge_gate': 2,
              'w_out': 1, 'ffn2_w_gate': 2, 'ffn2_w_up': 2, 'ffn2_w_down': 1}
F32_GATHER = ['gla_w_alpha', 'gla_b_alpha']
BF16_GATHER = [n for n in W_NAMES if n in SHARD_AXIS and n not in F32_GATHER]
SHARDED = [n for n in W_NAMES if n in SHARD_AXIS]
REPLICATED = [n for n in W_NAMES if n not in SHARD_AXIS]
N_CHIPS = 4
MESH = pl.DeviceIdType.MESH


def _tile(n, target, mult):
    best = None
    for t in range(mult, min(n, target) + 1, mult):
        if n % t == 0:
            best = t
    return best if best is not None else n


def _cparams(sem, vmem_bytes):
    limit = int(min(VMEM_LIMIT_V7X, max(32 * 1024 * 1024, vmem_bytes * 5 // 4)))
    return pltpu.CompilerParams(dimension_semantics=sem, vmem_limit_bytes=limit)


def _cw(arr, cb, width):
    return (arr, cb, width)


def _win(a):
    if isinstance(a, tuple):
        return a
    return (a, 0, a.shape[-1])


def _nbytes(shape, dtype):
    return int(np.prod(shape)) * jnp.dtype(dtype).itemsize


def _rowwise(name, fn, rows, fulls, outs, accs=(), tl=256):
    rows = [_win(r) for r in rows]
    n_tok = rows[0][0].shape[0]
    tl = _tile(n_tok, tl, 16)
    nr, nf, no = len(rows), len(fulls), len(outs)

    def body(*refs):
        vals = [r[...] for r in refs[:nr + nf]]
        res = fn(*vals)
        if not isinstance(res, (tuple, list)):
            res = (res,)
        for k in range(no):
            o = refs[nr + nf + k]
            o[...] = res[k].astype(o.dtype)
        if accs:
            @pl.when(pl.program_id(0) == 0)
            def _():
                for k in range(len(accs)):
                    a = refs[nr + nf + no + k]
                    a[...] = jnp.zeros(a.shape, a.dtype)
            for k in range(len(accs)):
                a = refs[nr + nf + no + k]
                a[...] += res[no + k]

    in_specs = [pl.BlockSpec((tl, w), functools.partial(lambda i, cb: (i, cb), cb=cb)) for (_, cb, w) in rows]
    in_specs += [pl.BlockSpec(f.shape, functools.partial(lambda i, nd: (0,) * nd, nd=f.ndim)) for f in fulls]
    out_specs = [pl.BlockSpec((tl, c), lambda i: (i, 0)) for (c, _) in outs]
    out_specs += [pl.BlockSpec((r, c), lambda i: (0, 0)) for (r, c) in accs]
    out_shape = [jax.ShapeDtypeStruct((n_tok, c), dt) for (c, dt) in outs]
    out_shape += [jax.ShapeDtypeStruct((r, c), F32) for (r, c) in accs]
    vm = 2 * sum(_nbytes((tl, w), a.dtype) for (a, _, w) in rows)
    vm += 2 * sum(_nbytes(f.shape, f.dtype) for f in fulls)
    vm += 2 * sum(_nbytes((tl, c), dt) for (c, dt) in outs)
    vm += 8 * max([_nbytes((tl, w), F32) for (_, _, w) in rows] + [_nbytes((tl, c), F32) for (c, _) in outs])
    return pl.pallas_call(
        body, name=name, grid=(n_tok // tl,), in_specs=in_specs, out_specs=out_specs, out_shape=out_shape,
        compiler_params=_cparams(("arbitrary",) if accs else ("parallel",), vm),
    )(*[r[0] for r in rows], *fulls)


def _mm(name, a, b, *, ta=False, tb=False, out_dtype=F32, alpha=1.0, add=None, tm=512, tn=512):
    a_arr, a_cb, a_w = _win(a)
    b_arr, b_cb, b_w = _win(b)
    if ta:
        kdim, m = a_arr.shape[0], a_w
    else:
        m, kdim = a_arr.shape[0], a_w
    if tb:
        n, kb = b_arr.shape[0], b_w
    else:
        kb, n = b_arr.shape[0], b_w
    assert kdim == kb, (name, kdim, kb)
    tm = _tile(m, tm, LANE if ta else 16)
    tn = _tile(n, tn, LANE)
    if ta:
        a_spec = pl.BlockSpec((kdim, tm), lambda i, j: (0, a_cb * (a_w // tm) + i))
        a_blk = (kdim, tm)
    else:
        a_spec = pl.BlockSpec((tm, kdim), lambda i, j: (i, a_cb))
        a_blk = (tm, kdim)
    if tb:
        b_spec = pl.BlockSpec((tn, kdim), lambda i, j: (j, b_cb))
        b_blk = (tn, kdim)
    else:
        b_spec = pl.BlockSpec((kdim, tn), lambda i, j: (0, b_cb * (b_w // tn) + j))
        b_blk = (kdim, tn)
    dims = (((0 if ta else 1,), (1 if tb else 0,)), ((), ()))
    has_add = add is not None

    def body(*refs):
        a_ref, b_ref = refs[0], refs[1]
        o_ref = refs[-1]
        acc = lax.dot_general(a_ref[...].astype(BF16), b_ref[...].astype(BF16), dims, preferred_element_type=F32)
        if alpha != 1.0:
            acc = acc * alpha
        if has_add:
            acc = acc + refs[2][...].astype(F32)
        o_ref[...] = acc.astype(o_ref.dtype)

    in_specs = [a_spec, b_spec]
    ops = [a_arr, b_arr]
    if has_add:
        in_specs.append(pl.BlockSpec((tm, tn), lambda i, j: (i, j)))
        ops.append(add)
    vm = 2 * (_nbytes(a_blk, a_arr.dtype) + _nbytes(b_blk, b_arr.dtype) + _nbytes((tm, tn), out_dtype))
    vm += _nbytes(a_blk, BF16) + _nbytes(b_blk, BF16) + 3 * _nbytes((tm, tn), F32)
    return pl.pallas_call(
        body, name=name, grid=(m // tm, n // tn), in_specs=in_specs,
        out_specs=pl.BlockSpec((tm, tn), lambda i, j: (i, j)),
        out_shape=jax.ShapeDtypeStruct((m, n), out_dtype),
        compiler_params=_cparams(("parallel", "parallel"), vm),
    )(*ops)


def _sigmoid(x):
    return 1.0 / (1.0 + jnp.exp(-x))


def _rms(x, gain):
    r = lax.rsqrt(jnp.mean(x * x, axis=-1, keepdims=True) + EPS)
    return x * r * gain


def _rms_bwd(x, gain, dh):
    r = lax.rsqrt(jnp.mean(x * x, axis=-1, keepdims=True) + EPS)
    xh = x * r
    dg = jnp.sum(dh * xh, axis=0, keepdims=True)
    dxh = dh * gain
    dx = r * (dxh - xh * jnp.mean(dxh * xh, axis=-1, keepdims=True))
    return dx, dg


_GELU_C = math.sqrt(2.0 / math.pi)


def _gelu(y):
    return 0.5 * y * (1.0 + jnp.tanh(_GELU_C * (y + 0.044715 * y * y * y)))


def _gelu_grad(y):
    th = jnp.tanh(_GELU_C * (y + 0.044715 * y * y * y))
    return 0.5 * (1.0 + th) + 0.5 * y * (1.0 - th * th) * _GELU_C * (1.0 + 3.0 * 0.044715 * y * y)


def _log_sigmoid(x):
    return jnp.minimum(x, 0.0) - jnp.log(1.0 + jnp.exp(-jnp.abs(x)))


def _dot(a, b, dims=(((1,), (0,)), ((), ())), precision=None):
    return lax.dot_general(a, b, dims, preferred_element_type=F32, precision=precision)


_NT = (((1,), (1,)), ((), ()))
_TN = (((0,), (0,)), ((), ()))


def _scan(name, b, a_re, a_im, *, rev, xin=None, tb=256, lc=512):
    n_tok = b.shape[0]
    tb = _tile(n_tok, tb, 16)
    nt, nb, nc = n_tok // tb, tb // SUBLANE, S5_N // lc
    with_da = xin is not None

    def body(*refs):
        if with_da:
            b_ref, are_ref, aim_ref, x_ref, o_ref, da_ref, pw, last, of32, xf32 = refs
        else:
            b_ref, are_ref, aim_ref, o_ref, pw, last, of32 = refs
        t = pl.program_id(0)
        row = lax.broadcasted_iota(jnp.int32, (SUBLANE, lc), 0)

        def cmul(ar, ai, xr, xi):
            return ar * xr - ai * xi, ar * xi + ai * xr

        def hs(c, xr, xi):
            cs = pl.ds(c * lc, lc)
            for k, slot in ((1, 0), (2, 2), (4, 4)):
                ar, ai = pw[slot, :, cs], pw[slot + 1, :, cs]
                if rev:
                    sr, si = pltpu.roll(xr, SUBLANE - k, 0), pltpu.roll(xi, SUBLANE - k, 0)
                    keep = row < SUBLANE - k
                else:
                    sr, si = pltpu.roll(xr, k, 0), pltpu.roll(xi, k, 0)
                    keep = row >= k
                sr, si = jnp.where(keep, sr, 0.0), jnp.where(keep, si, 0.0)
                pr, pi = cmul(ar, ai, sr, si)
                xr, xi = xr + pr, xi + pi
            return xr, xi

        @pl.when(t == 0)
        def _():
            last[...] = jnp.zeros(last.shape, F32)
            if with_da:
                da_ref[...] = jnp.zeros(da_ref.shape, F32)
            for c in range(nc):
                cs = pl.ds(c * lc, lc)
                a1r = jnp.broadcast_to(are_ref[:, cs], (SUBLANE, lc))
                a1i = jnp.broadcast_to(aim_ref[:, cs], (SUBLANE, lc))
                a2r, a2i = cmul(a1r, a1i, a1r, a1i)
                a4r, a4i = cmul(a2r, a2i, a2r, a2i)
                for slot, v in enumerate((a1r, a1i, a2r, a2i, a4r, a4i)):
                    pw[slot, :, cs] = v
                first = SUBLANE - 1 if rev else 0
                pr, pi = hs(c, jnp.where(row == first, a1r, 0.0), jnp.where(row == first, a1i, 0.0))
                pw[6, :, cs] = pr
                pw[7, :, cs] = pi

        if with_da:
            xf32[...] = x_ref[...].astype(F32)
        edge = 0 if rev else SUBLANE - 1

        def blk(j, carry):
            jj = (nb - 1 - j) if rev else j
            r0 = pl.multiple_of(jj * SUBLANE, SUBLANE)
            for c in range(nc):
                cre, cim = pl.ds(c * lc, lc), pl.ds(S5_N + c * lc, lc)
                xr, xi = hs(c, b_ref[pl.ds(r0, SUBLANE), cre], b_ref[pl.ds(r0, SUBLANE), cim])
                cr = jnp.broadcast_to(last[edge:edge + 1, cre], (SUBLANE, lc))
                ci = jnp.broadcast_to(last[edge:edge + 1, cim], (SUBLANE, lc))
                pr, pi = cmul(pw[6, :, cre], pw[7, :, cre], cr, ci)
                xr, xi = xr + pr, xi + pi
                if with_da:
                    if rev:
                        qr, qi = pltpu.roll(xr, SUBLANE - 1, 0), pltpu.roll(xi, SUBLANE - 1, 0)
                        fill = row == SUBLANE - 1
                    else:
                        qr, qi = pltpu.roll(xr, 1, 0), pltpu.roll(xi, 1, 0)
                        fill = row == 0
                    qr, qi = jnp.where(fill, cr, qr), jnp.where(fill, ci, qi)
                    zr, zi = xf32[pl.ds(r0, SUBLANE), cre], xf32[pl.ds(r0, SUBLANE), cim]
                    da_ref[:, cre] += zr * qr + zi * qi
                    da_ref[:, cim] += zr * qi - zi * qr
                last[:, cre] = xr
                last[:, cim] = xi
                of32[pl.ds(r0, SUBLANE), cre] = xr
                of32[pl.ds(r0, SUBLANE), cim] = xi
            return carry

        lax.fori_loop(0, nb, blk, 0)
        o_ref[...] = of32[...].astype(o_ref.dtype)

    tmap = (lambda t: (nt - 1 - t, 0)) if rev else (lambda t: (t, 0))
    in_specs = [pl.BlockSpec((tb, 2 * S5_N), tmap), pl.BlockSpec((1, S5_N), lambda t: (0, 0)),
                pl.BlockSpec((1, S5_N), lambda t: (0, 0))]
    ops = [b, a_re, a_im]
    out_specs = [pl.BlockSpec((tb, 2 * S5_N), tmap)]
    out_shape = [jax.ShapeDtypeStruct((n_tok, 2 * S5_N), BF16)]
    scratch = [pltpu.VMEM((8, SUBLANE, S5_N), F32), pltpu.VMEM((SUBLANE, 2 * S5_N), F32),
               pltpu.VMEM((tb, 2 * S5_N), F32)]
    if with_da:
        in_specs.append(pl.BlockSpec((tb, 2 * S5_N), tmap))
        ops.append(xin)
        out_specs.append(pl.BlockSpec((SUBLANE, 2 * S5_N), lambda t: (0, 0)))
        out_shape.append(jax.ShapeDtypeStruct((SUBLANE, 2 * S5_N), F32))
        scratch.append(pltpu.VMEM((tb, 2 * S5_N), F32))
    vm = 2 * _nbytes((tb, 2 * S5_N), F32) + 4 * _nbytes((tb, 2 * S5_N), BF16) + 2 * _nbytes((tb, 2 * S5_N), F32)
    res = pl.pallas_call(
        body, name=name, grid=(nt,), in_specs=in_specs, out_specs=out_specs, out_shape=out_shape,
        scratch_shapes=scratch, compiler_params=_cparams(("arbitrary",), vm),
    )(*ops)
    return res if with_da else res[0]


def _gla_chunk_terms(la, q, k, rev):
    ri = lax.broadcasted_iota(jnp.int32, (GLA_C, GLA_C), 0)
    ci = lax.broadcasted_iota(jnp.int32, (GLA_C, GLA_C), 1)
    tri = (ri <= ci) if rev else (ri >= ci)
    trif = tri.astype(F32)
    bcum = _dot(trif, la, precision=HI)
    blast = jnp.sum(la, axis=0, keepdims=True)
    eb, enb = jnp.exp(bcum), jnp.exp(-bcum)
    ee = jnp.exp(blast - bcum)
    return tri, trif, blast, eb, enb, ee, q * eb, k * enb, k * ee


def _gla_fwd(name, z, la, *, rev):
    n_tok = z.shape[0]
    nck = n_tok // GLA_C
    scale = GLA_D ** -0.5

    def body(q_ref, k_ref, v_ref, la_ref, o_ref, s_ref, st):
        n = pl.program_id(1)

        @pl.when(n == 0)
        def _():
            st[...] = jnp.zeros(st.shape, F32)

        v = v_ref[...].astype(BF16)
        tri, _, blast, _, _, _, qd, kd, ke = _gla_chunk_terms(la_ref[...], q_ref[...] * scale, k_ref[...], rev)
        qd16 = qd.astype(BF16)
        sc = jnp.where(tri, _dot(qd16, kd.astype(BF16), _NT), 0.0)
        s16 = st[...].astype(BF16)
        s_ref[0, 0] = s16
        o_ref[...] = _dot(sc.astype(BF16), v) + _dot(qd16, s16, _NT)
        st[...] = st[...] * jnp.exp(blast) + _dot(v, ke.astype(BF16), _TN)

    cmap = (lambda h, n: nck - 1 - n) if rev else (lambda h, n: n)

    def zspec(cb):
        return pl.BlockSpec((GLA_C, GLA_D), lambda h, n: (cmap(h, n), cb + h))

    return pl.pallas_call(
        body, name=name, grid=(GLA_H, nck),
        in_specs=[zspec(ZC_GQ), zspec(ZC_GK), zspec(ZC_GV), zspec(0)],
        out_specs=[zspec(0), pl.BlockSpec((1, 1, GLA_D, GLA_D), lambda h, n: (h, n, 0, 0))],
        out_shape=[jax.ShapeDtypeStruct((n_tok, GLA_W), F32),
                   jax.ShapeDtypeStruct((GLA_H, nck, GLA_D, GLA_D), BF16)],
        scratch_shapes=[pltpu.VMEM((GLA_D, GLA_D), F32)],
        compiler_params=_cparams(("parallel", "arbitrary"), 4 * 1024 * 1024),
    )(z, z, z, la)


def _gla_bwd(name, z, la, s_prev, do, *, rev):
    n_tok = z.shape[0]
    nck = n_tok // GLA_C
    scale = GLA_D ** -0.5

    def body(q_ref, k_ref, v_ref, la_ref, s_ref, do_ref, dq_ref, dk_ref, dv_ref, dla_ref, dst):
        n = pl.program_id(1)

        @pl.when(n == 0)
        def _():
            dst[...] = jnp.zeros(dst.shape, F32)

        v = v_ref[...].astype(BF16)
        do16 = do_ref[...].astype(BF16)
        tri, trif, blast, eb, enb, ee, qd, kd, ke = _gla_chunk_terms(la_ref[...], q_ref[...] * scale, k_ref[...], rev)
        qd16, kd16, ke16 = qd.astype(BF16), kd.astype(BF16), ke.astype(BF16)
        sc16 = jnp.where(tri, _dot(qd16, kd16, _NT), 0.0).astype(BF16)
        dsc16 = jnp.where(tri, _dot(do16, v, _NT), 0.0).astype(BF16)
        s16 = s_ref[0, 0]
        dsn = dst[...]
        dsn16 = dsn.astype(BF16)
        dec = jnp.exp(blast)
        dqd = _dot(dsc16, kd16) + _dot(do16, s16)
        dkd = _dot(dsc16, qd16, _TN)
        dke = _dot(v, dsn16)
        dv_ref[...] = _dot(sc16, do16, _TN) + _dot(ke16, dsn16, _NT)
        ddec = jnp.sum(dsn * s16.astype(F32), axis=0, keepdims=True)
        dst[...] = dsn * dec + _dot(do16, qd16, _TN)
        dq_ref[...] = dqd * eb * scale
        dk_ref[...] = dkd * enb + dke * ee
        db = dqd * qd - dkd * kd - dke * ke
        dblast = jnp.sum(dke * ke, axis=0, keepdims=True) + ddec * dec
        dla_ref[...] = _dot(trif, db, _TN, precision=HI) + dblast

    cmap = (lambda h, n: n) if rev else (lambda h, n: nck - 1 - n)

    def zspec(cb):
        return pl.BlockSpec((GLA_C, GLA_D), lambda h, n: (cmap(h, n), cb + h))

    o4 = jax.ShapeDtypeStruct((n_tok, GLA_W), F32)
    return pl.pallas_call(
        body, name=name, grid=(GLA_H, nck),
        in_specs=[zspec(ZC_GQ), zspec(ZC_GK), zspec(ZC_GV), zspec(0),
                  pl.BlockSpec((1, 1, GLA_D, GLA_D), lambda h, n: (h, nck - 1 - n, 0, 0)), zspec(0)],
        out_specs=[zspec(0)] * 4, out_shape=[o4] * 4,
        scratch_shapes=[pltpu.VMEM((GLA_D, GLA_D), F32)],
        compiler_params=_cparams(("parallel", "arbitrary"), 4 * 1024 * 1024),
    )(z, z, z, la, s_prev, do)


_PAIRS = AT_W // LANE
_PAIRS_PER_KV = _PAIRS // AT_KH


def _attn_fwd(name, q, kd, vd, tq=256):
    n_tok = q.shape[0]
    tq = _tile(n_tok, tq, 16)

    def body(q_ref, k_ref, v_ref, o_ref, l_ref):
        qv, kv, vv = q_ref[...], k_ref[...], v_ref[...]
        low = lax.broadcasted_iota(jnp.int32, qv.shape, 1) < AT_D
        res = []
        for keep in (low, jnp.logical_not(low)):
            s = _dot(jnp.where(keep, qv, jnp.zeros_like(qv)), kv, _NT)
            m = jnp.max(s, axis=-1, keepdims=True)
            p = jnp.exp(s - m)
            den = jnp.sum(p, axis=-1, keepdims=True)
            res.append((_dot(p.astype(BF16), vv) / den, m + jnp.log(den)))
        o_ref[...] = jnp.where(low, res[0][0], res[1][0])
        l_ref[...] = jnp.where(low, res[0][1], res[1][1])

    qspec = pl.BlockSpec((tq, LANE), lambda p, i: (i, p))
    kspec = pl.BlockSpec((n_tok, LANE), lambda p, i: (0, p // _PAIRS_PER_KV))
    vm = 5 * _nbytes((tq, n_tok), F32) + 8 * _nbytes((n_tok, LANE), BF16)
    out = jax.ShapeDtypeStruct((n_tok, AT_W), F32)
    return pl.pallas_call(
        body, name=name, grid=(_PAIRS, n_tok // tq), in_specs=[qspec, kspec, kspec], out_specs=[qspec, qspec],
        out_shape=[out, out], compiler_params=_cparams(("parallel", "parallel"), vm),
    )(q, kd, vd)


def _attn_bwd(name, q, kd, vd, o, lse, do, tq=256):
    n_tok = q.shape[0]
    tq = _tile(n_tok, tq, 16)

    def body(q_ref, k_ref, v_ref, o_ref, l_ref, do_ref, dq_ref, dk_ref, dv_ref):
        @pl.when((pl.program_id(1) == 0) & (pl.program_id(2) == 0))
        def _():
            dk_ref[...] = jnp.zeros(dk_ref.shape, F32)
            dv_ref[...] = jnp.zeros(dv_ref.shape, F32)

        qv, kv, vv = q_ref[...], k_ref[...], v_ref[...]
        dof, lv = do_ref[...], l_ref[...]
        do16 = dof.astype(BF16)
        doo = dof * o_ref[...]
        low = lax.broadcasted_iota(jnp.int32, qv.shape, 1) < AT_D
        dq = jnp.zeros(qv.shape, F32)
        for keep in (low, jnp.logical_not(low)):
            qm = jnp.where(keep, qv, jnp.zeros_like(qv))
            dom = jnp.where(keep, do16, jnp.zeros_like(do16))
            lh = jnp.max(jnp.where(keep, lv, -1e30), axis=-1, keepdims=True)
            p = jnp.exp(_dot(qm, kv, _NT) - lh)
            delta = jnp.sum(jnp.where(keep, doo, 0.0), axis=-1, keepdims=True)
            ds16 = (p * (_dot(dom, vv, _NT) - delta)).astype(BF16)
            dq = jnp.where(keep, _dot(ds16, kv), dq)
            dk_ref[...] += _dot(ds16, qm, _TN)
            dv_ref[...] += _dot(p.astype(BF16), dom, _TN)
        dq_ref[...] = dq

    qspec = pl.BlockSpec((tq, LANE), lambda g, j, i: (i, g * _PAIRS_PER_KV + j))
    kspec = pl.BlockSpec((n_tok, LANE), lambda g, j, i: (0, g))
    vm = 7 * _nbytes((tq, n_tok), F32) + 12 * _nbytes((n_tok, LANE), F32)
    dup = jax.ShapeDtypeStruct((n_tok, AT_KH * LANE), F32)
    return pl.pallas_call(
        body, name=name, grid=(AT_KH, _PAIRS_PER_KV, n_tok // tq),
        in_specs=[qspec, kspec, kspec, qspec, qspec, qspec], out_specs=[qspec, kspec, kspec],
        out_shape=[jax.ShapeDtypeStruct((n_tok, AT_W), F32), dup, dup],
        compiler_params=_cparams(("parallel", "arbitrary", "arbitrary"), vm),
    )(q, kd, vd, o, lse, do)


def _block_diag(t):
    g, a, b = t.shape
    eye = jnp.eye(g, dtype=t.dtype)
    return (t[:, :, None, :] * eye[:, None, :, None]).reshape(g * a, g * b)


def _s5_prep(lam_re, lam_im, log_dt, b_re, b_im, c_re, c_im):
    dt = jnp.exp(log_dt)[:, None]
    er = jnp.exp(lam_re * dt)
    a_re, a_im = er * jnp.cos(lam_im * dt), er * jnp.sin(lam_im * dt)
    nr, ni = a_re - 1.0, a_im
    den = lam_re * lam_re + lam_im * lam_im
    f_re, f_im = (nr * lam_re + ni * lam_im) / den, (ni * lam_re - nr * lam_im) / den
    bb_re = f_re[..., None] * b_re - f_im[..., None] * b_im
    bb_im = f_re[..., None] * b_im + f_im[..., None] * b_re
    w_in = jnp.concatenate([_block_diag(bb_re.transpose(0, 2, 1)), _block_diag(bb_im.transpose(0, 2, 1))], axis=1)
    w_out = jnp.concatenate([_block_diag(c_re.transpose(0, 2, 1)), -_block_diag(c_im.transpose(0, 2, 1))], axis=0)
    return a_re.reshape(1, S5_N), a_im.reshape(1, S5_N), w_in, w_out


def _rope_tables(n_tok):
    quarter = AT_D // 4
    d = np.arange(AT_D)
    e = d % (AT_D // 2)
    inv = (ROPE_BASE ** (-(e % quarter).astype(np.float64) * 2.0 / (AT_D // 2))).astype(np.float32)
    sign = np.where(e < quarter, -1.0, 1.0).astype(np.float32)
    rot = np.zeros((AT_D, AT_D), np.float32)
    rot[np.where(e < quarter, d + quarter, d - quarter), d] = 1.0
    pos = jnp.arange(n_tok, dtype=jnp.int32)
    axis_pos = jnp.where(jnp.asarray(d // (AT_D // 2) == 0)[None, :], (pos // GRID_W)[:, None], (pos % GRID_W)[:, None])
    ang = axis_pos.astype(F32) * jnp.asarray(inv)[None, :]
    return jnp.cos(ang), jnp.sin(ang) * jnp.asarray(sign)[None, :], rot


def _head_consts(n_tok, heads):
    cos, sin, rot = _rope_tables(n_tok)
    eye = np.eye(heads, dtype=np.float32)
    mean = np.kron(eye, np.full((AT_D, AT_D), 1.0 / AT_D, np.float32))
    return jnp.tile(cos, (1, heads)), jnp.tile(sin, (1, heads)), jnp.asarray(mean), jnp.asarray(np.kron(eye, rot))


def _pad_w_in(w):
    return jnp.concatenate([w[:, :2560], w[:, 2592:3360], w[:, 2560:2592], jnp.zeros((w.shape[0], 96), w.dtype)], axis=1)


def _unpad_w_in(g):
    return jnp.concatenate([g[:, :2560], g[:, 3328:3360], g[:, 2560:3328]], axis=1)


def _ffn_fwd(x, g, wg, wu, wd, d):
    n_tok = x.shape[0]
    fq = wg.shape[-1]
    h = _rowwise("ffn_norm", lambda xv, gv: _rms(xv, gv), [x], [g], [(D_MODEL, BF16)])[0]
    tm = _tile(n_tok, 512, 16)

    def up(h_ref, wg_ref, wu_ref, a_ref, b_ref, s_ref):
        hv = h_ref[...]
        av, bv = _dot(hv, wg_ref[0, 0]), _dot(hv, wu_ref[0, 0])
        a_ref[0] = av.astype(BF16)
        b_ref[0] = bv.astype(BF16)
        s_ref[0] = (av * _sigmoid(av) * bv).astype(BF16)

    wspec = pl.BlockSpec((1, 1, D_MODEL, fq), lambda i, j: (j, d, 0, 0))
    aspec = pl.BlockSpec((1, tm, fq), lambda i, j: (j, i, 0))
    act = jax.ShapeDtypeStruct((N_CHIPS, n_tok, fq), BF16)
    a, b, s = pl.pallas_call(
        up, name="ffn_up", grid=(n_tok // tm, N_CHIPS),
        in_specs=[pl.BlockSpec((tm, D_MODEL), lambda i, j: (i, 0)), wspec, wspec],
        out_specs=[aspec, aspec, aspec], out_shape=[act, act, act],
        compiler_params=_cparams(("parallel", "parallel"), 4 * D_MODEL * fq * 2 + 16 * tm * fq * 4),
    )(h, wg, wu)
    tn = _tile(D_MODEL, 512, LANE)

    def down(s_ref, w_ref, x_ref, o_ref):
        acc = _dot(s_ref[0], w_ref[0, 0])
        for j in range(1, N_CHIPS):
            acc = acc + _dot(s_ref[j], w_ref[j, 0])
        o_ref[...] = x_ref[...] + 0.5 * acc

    xo = pl.pallas_call(
        down, name="ffn_down", grid=(n_tok // tm, D_MODEL // tn),
        in_specs=[pl.BlockSpec((N_CHIPS, tm, fq), lambda i, n: (0, i, 0)),
                  pl.BlockSpec((N_CHIPS, 1, fq, tn), lambda i, n: (0, d, 0, n)),
                  pl.BlockSpec((tm, tn), lambda i, n: (i, n))],
        out_specs=pl.BlockSpec((tm, tn), lambda i, n: (i, n)), out_shape=jax.ShapeDtypeStruct((n_tok, D_MODEL), F32),
        compiler_params=_cparams(("parallel", "parallel"), 4 * N_CHIPS * (tm + tn) * fq * 2 + 6 * tm * tn * 4),
    )(s, wd, x)
    return xo, (x, h, a, b, s)


def _ffn_bwd(dy, dy16, saved, g, wg, wu, wd, d, into):
    x, h, a, b, s = saved
    n_tok = x.shape[0]
    fq = wg.shape[-1]
    tm = _tile(n_tok, 512, 16)

    def dact(dy_ref, w_ref, a_ref, b_ref, da_ref, db_ref):
        ds = 0.5 * _dot(dy_ref[...], w_ref[0, 0], _NT)
        av, bv = a_ref[0].astype(F32), b_ref[0].astype(F32)
        sg = _sigmoid(av)
        da_ref[0] = (ds * bv * sg * (1.0 + av * (1.0 - sg))).astype(BF16)
        db_ref[0] = (ds * av * sg).astype(BF16)

    aspec = pl.BlockSpec((1, tm, fq), lambda i, j: (j, i, 0))
    act = jax.ShapeDtypeStruct((N_CHIPS, n_tok, fq), BF16)
    da, db = pl.pallas_call(
        dact, name="ffn_dact", grid=(n_tok // tm, N_CHIPS),
        in_specs=[pl.BlockSpec((tm, D_MODEL), lambda i, j: (i, 0)),
                  pl.BlockSpec((1, 1, fq, D_MODEL), lambda i, j: (j, d, 0, 0)), aspec, aspec],
        out_specs=[aspec, aspec], out_shape=[act, act],
        compiler_params=_cparams(("parallel", "parallel"), 4 * D_MODEL * fq * 2 + 16 * tm * fq * 4),
    )(dy16, wd, a, b)

    any_spec = pl.BlockSpec(memory_space=pl.ANY)
    alias = {} if into is None else {2: 0}
    tn = _tile(D_MODEL, 512, LANE)

    def dw_down(s_ref, dy_ref, *rest):
        rest[-1][0, 0] = 0.5 * _dot(s_ref[0], dy_ref[...], _TN)

    dwd = pl.pallas_call(
        dw_down, name="ffn_dw_down", grid=(N_CHIPS, D_MODEL // tn),
        in_specs=[pl.BlockSpec((1, n_tok, fq), lambda j, n: (j, 0, 0)), pl.BlockSpec((n_tok, tn), lambda j, n: (0, n))]
        + ([] if into is None else [any_spec]),
        out_specs=pl.BlockSpec((1, 1, fq, tn), lambda j, n: (j, d, 0, n)),
        out_shape=jax.ShapeDtypeStruct((N_CHIPS, DEPTH, fq, D_MODEL), F32), input_output_aliases=alias,
        compiler_params=_cparams(("parallel", "parallel"), 6 * n_tok * (fq + tn) * 2 + 4 * fq * tn * 4),
    )(s, dy16, *([] if into is None else [into[2]]))

    def dw_up_call(t, prev):
        def dw_up(h_ref, t_ref, *rest):
            rest[-1][0, 0] = _dot(h_ref[...], t_ref[0], _TN)

        return pl.pallas_call(
            dw_up, name="ffn_dw_up", grid=(N_CHIPS, D_MODEL // tn),
            in_specs=[pl.BlockSpec((n_tok, tn), lambda j, m: (0, m)), pl.BlockSpec((1, n_tok, fq), lambda j, m: (j, 0, 0))]
            + ([] if prev is None else [any_spec]),
            out_specs=pl.BlockSpec((1, 1, tn, fq), lambda j, m: (j, d, m, 0)),
            out_shape=jax.ShapeDtypeStruct((N_CHIPS, DEPTH, D_MODEL, fq), F32), input_output_aliases=alias,
            compiler_params=_cparams(("parallel", "parallel"), 6 * n_tok * (fq + tn) * 2 + 4 * fq * tn * 4),
        )(h, t, *([] if prev is None else [prev]))

    dwg = dw_up_call(da, None if into is None else into[0])
    dwu = dw_up_call(db, None if into is None else into[1])
    tr = _tile(n_tok, 256, 16)

    def dh_norm(da_ref, db_ref, wg_ref, wu_ref, dy_ref, x_ref, g_ref, dx_ref, dx16_ref, dg_ref):
        dh = _dot(da_ref[0], wg_ref[0, 0], _NT) + _dot(db_ref[0], wu_ref[0, 0], _NT)
        for j in range(1, N_CHIPS):
            dh = dh + _dot(da_ref[j], wg_ref[j, 0], _NT) + _dot(db_ref[j], wu_ref[j, 0], _NT)
        dx, dg = _rms_bwd(x_ref[...], g_ref[...], dh)
        dx = dy_ref[...] + dx
        dx_ref[...] = dx
        dx16_ref[...] = dx.astype(BF16)

        @pl.when(pl.program_id(0) == 0)
        def _():
            dg_ref[...] = jnp.zeros(dg_ref.shape, F32)

        dg_ref[...] += dg

    tspec = pl.BlockSpec((N_CHIPS, tr, fq), lambda i: (0, i, 0))
    wspec = pl.BlockSpec((N_CHIPS, 1, D_MODEL, fq), lambda i: (0, d, 0, 0))
    rspec = pl.BlockSpec((tr, D_MODEL), lambda i: (i, 0))
    vspec = pl.BlockSpec((1, D_MODEL), lambda i: (0, 0))
    dx, dx16, dg = pl.pallas_call(
        dh_norm, name="ffn_dh_norm", grid=(n_tok // tr,),
        in_specs=[tspec, tspec, wspec, wspec, rspec, rspec, vspec], out_specs=[rspec, rspec, vspec],
        out_shape=[jax.ShapeDtypeStruct((n_tok, D_MODEL), F32), jax.ShapeDtypeStruct((n_tok, D_MODEL), BF16),
                   jax.ShapeDtypeStruct((1, D_MODEL), F32)],
        compiler_params=_cparams(("arbitrary",), 4 * N_CHIPS * D_MODEL * fq * 2 + 4 * N_CHIPS * tr * fq * 2
                                 + 16 * tr * D_MODEL * 4),
    )(da, db, wg, wu, dy, x, g)
    return dx, dx16, dg[0], (dwg, dwu, dwd)


def _alpha_pads(w_alpha):
    z = jnp.zeros((LANE - 2 * GLA_R, GLA_W), F32)
    wf = jnp.concatenate([w_alpha[0], jnp.zeros((GLA_R, GLA_W), F32), z], axis=0)
    wb = jnp.concatenate([jnp.zeros((GLA_R, GLA_W), F32), w_alpha[1], z], axis=0)
    return wf, wb


def _dup_heads(t):
    low = lax.broadcasted_iota(jnp.int32, t.shape, 1) < AT_D
    h0, h1 = jnp.where(low, t, 0.0), jnp.where(low, 0.0, t)
    return jnp.concatenate([h0 + pltpu.roll(h0, AT_D, 1), h1 + pltpu.roll(h1, AT_D, 1)], axis=1)


def _fold_heads(t):
    low = lax.broadcasted_iota(jnp.int32, (t.shape[0], LANE), 1) < AT_D
    g0, g1 = t[:, :LANE], t[:, LANE:]
    return jnp.where(low, g0 + pltpu.roll(g0, AT_D, 1), g1 + pltpu.roll(g1, AT_D, 1))


def _mixer_fwd(x, w):
    n_tok = x.shape[0]
    h = _rowwise("mix_norm", lambda xv, gv: _rms(xv, gv), [x], [w['mix_norm']], [(D_MODEL, BF16)])[0]
    z = _mm("mix_in", h, w['w_in_p'])
    gp = _mm("mix_gate", h, w['w_merge_gate'], out_dtype=BF16)
    u = _cw(z, 0, S5_W)

    xs, yy = [], None
    for d, rev in ((0, False), (1, True)):
        a_re, a_im, wb, wc = w['s5'][d]
        bu = _mm("s5_bu", u, wb)
        xs.append(_scan("s5_scan_rev" if rev else "s5_scan_fwd", bu, a_re, a_im, rev=rev))
        yy = _mm("s5_y", xs[d], wc, add=yy)

    def s5_post(yv, uv, dv, wglu):
        y0 = yv + dv * uv
        y1 = _gelu(y0)
        t = _dot(y1.astype(BF16), wglu)
        return y0, t, y1 * _sigmoid(t)

    y0, t_glu, y_s5 = _rowwise("s5_post", s5_post, [yy, u], [w['s5_d'], w['s5_w_glu']],
                               [(S5_W, F32), (S5_W, F32), (S5_W, BF16)])

    wa_f, wa_b = _alpha_pads(w['gla_w_alpha'])
    ba_f, ba_b = w['gla_b_alpha'][0:1], w['gla_b_alpha'][1:2]
    zz = _cw(z, ZC_Z, LANE)

    def gla_prep(zv, waf, wab, bf, bb):
        z16 = zv.astype(BF16)
        lf = _dot(z16, waf.astype(BF16)) + bf
        lb = _dot(z16, wab.astype(BF16)) + bb
        return _log_sigmoid(lf) / GLA_TAU, _log_sigmoid(lb) / GLA_TAU

    la_f, la_b = _rowwise("gla_prep", gla_prep, [zz], [wa_f, wa_b, ba_f, ba_b], [(GLA_W, F32), (GLA_W, F32)])
    o_f, s_f = _gla_fwd("gla_fwd", z, la_f, rev=False)
    o_b, s_b = _gla_fwd("gla_rev", z, la_b, rev=True)
    gate = _cw(z, 4, GLA_W)

    def gla_post(of, ob, gv, gn):
        o = of + ob
        outs = []
        for hh in range(GLA_H):
            sl = slice(hh * GLA_D, (hh + 1) * GLA_D)
            gh = gv[:, sl]
            outs.append(_rms(o[:, sl], gn) * (gh * _sigmoid(gh)))
        return jnp.concatenate(outs, axis=1)

    y_gla = _rowwise("gla_post", gla_post, [o_f, o_b, gate], [w['gla_norm']], [(GLA_W, BF16)])[0]

    cq, sq, mq, rq = _head_consts(n_tok, AT_QH)
    ck, sk, mk, rk = _head_consts(n_tok, AT_KH)
    gq = jnp.tile(w['attn_q_norm'], (1, AT_QH))
    gk = jnp.tile(w['attn_k_norm'], (1, AT_KH))

    def rope_fwd(v, gain, cos, sin, mean, rot, scale):
        r = lax.rsqrt(_dot(v * v, mean, precision=HI) + EPS)
        vn = v * r * gain
        return (vn * cos + _dot(vn, rot, precision=HI) * sin) * scale

    def attn_prep(qv, kv, vv, cqv, sqv, ckv, skv, gqv, gkv, mqv, rqv, mkv, rkv):
        kr = rope_fwd(kv, gkv, ckv, skv, mkv, rkv, 1.0)
        return rope_fwd(qv, gqv, cqv, sqv, mqv, rqv, AT_D ** -0.5), _dup_heads(kr), _dup_heads(vv)

    q_raw, k_raw, v_raw = _cw(z, 5, AT_W), _cw(z, ZC_AK, AT_KW), _cw(z, ZC_AV, AT_KW)
    q16, kd, vd = _rowwise("attn_prep", attn_prep, [q_raw, k_raw, v_raw, cq, sq, ck, sk], [gq, gk, mq, rq, mk, rk],
                           [(AT_W, BF16), (2 * LANE, BF16), (2 * LANE, BF16)])
    y_at, lse = _attn_fwd("attn_fwd", q16, kd, vd)

    p0 = _mm("mix_branch", y_s5, w['w_branch_s5'], out_dtype=BF16)
    p1 = _mm("mix_branch", y_gla, w['w_branch_gla'], out_dtype=BF16)
    p2 = _mm("mix_branch_f32", y_at, w['w_branch_attn'], out_dtype=BF16)

    def merge(gpv, a0, a1, a2, bias):
        g = _sigmoid(gpv.astype(F32) + bias)
        return (g[:, :D_MODEL] * a0.astype(F32) + g[:, D_MODEL:2 * D_MODEL] * a1.astype(F32)
                + g[:, 2 * D_MODEL:] * a2.astype(F32))

    merged = _rowwise("mix_merge", merge, [gp, p0, p1, p2], [w['b_merge_gate']], [(D_MODEL, BF16)])[0]
    xo = _mm("mix_out", merged, w['w_out'], add=x)
    saved = dict(x=x, h=h, z=z, gp=gp, xs=xs, y0=y0, t_glu=t_glu, y_s5=y_s5, la=(la_f, la_b), o=(o_f, o_b), s=(s_f, s_b),
                 y_gla=y_gla, q16=q16, kd=kd, vd=vd, lse=lse, y_at=y_at, p=(p0, p1, p2), merged=merged)
    return xo, saved


def _mixer_bwd(dy, dy16, sv, w, s5_raw):
    z, h = sv['z'], sv['h']
    n_tok = z.shape[0]
    grads = {}
    dmerged = _mm("mix_dmerged", dy16, w['w_out'], tb=True, out_dtype=BF16)
    grads['w_out'] = _mm("mix_dwout", sv['merged'], dy16, ta=True)

    def dmerge(dm, gpv, a0, a1, a2, bias):
        dm = dm.astype(F32)
        g = _sigmoid(gpv.astype(F32) + bias)
        ps = (a0.astype(F32), a1.astype(F32), a2.astype(F32))
        dps, dgs = [], []
        for k in range(3):
            gk = g[:, k * D_MODEL:(k + 1) * D_MODEL]
            dps.append(dm * gk)
            dgs.append(dm * ps[k] * gk * (1.0 - gk))
        dgp = jnp.concatenate(dgs, axis=1)
        return dps[0], dps[1], dps[2], dgp, jnp.sum(dgp, axis=0, keepdims=True)

    dp0, dp1, dp2, dgp, dbm = _rowwise("mix_dmerge", dmerge, [dmerged, sv['gp'], *sv['p']], [w['b_merge_gate']],
                                       [(D_MODEL, BF16)] * 3 + [(3 * D_MODEL, BF16)], accs=[(1, 3 * D_MODEL)])
    grads['b_merge_gate'] = dbm[0]
    grads['w_branch_s5'] = _mm("mix_dbranch", sv['y_s5'], dp0, ta=True)
    grads['w_branch_gla'] = _mm("mix_dbranch", sv['y_gla'], dp1, ta=True)
    grads['w_branch_attn'] = _mm("mix_dbranch_f32", sv['y_at'], dp2, ta=True)
    dy_s5 = _mm("mix_dy", dp0, w['w_branch_s5'], tb=True)
    dy_gla = _mm("mix_dy", dp1, w['w_branch_gla'], tb=True)
    dy_at = _mm("mix_dy", dp2, w['w_branch_attn'], tb=True)
    grads['w_merge_gate'] = _mm("mix_dwgate", h, dgp, ta=True)
    dh = _mm("mix_dh_gate", dgp, w['w_merge_gate'], tb=True)
    u = _cw(z, 0, S5_W)

    def s5_post_bwd(dyv, y0, t, uv, dv, wglu):
        sg = _sigmoid(t)
        y1 = _gelu(y0)
        dt = dyv * y1 * sg * (1.0 - sg)
        dy1 = dyv * sg + _dot(dt.astype(BF16), wglu, _NT)
        dy0 = dy1 * _gelu_grad(y0)
        return dy0, dt, y1, dy0 * dv, jnp.sum(dy0 * uv, axis=0, keepdims=True)

    dy0, dt_glu, y1, du, dd = _rowwise("s5_post_bwd", s5_post_bwd, [dy_s5, sv['y0'], sv['t_glu'], u],
                                       [w['s5_d'], w['s5_w_glu']],
                                       [(S5_W, BF16), (S5_W, BF16), (S5_W, BF16), (S5_W, F32)], accs=[(1, S5_W)])
    grads['s5_d'] = dd[0]
    grads['s5_w_glu'] = _mm("s5_dwglu", y1, dt_glu, ta=True)
    s5_cot = []
    for d, rev in ((0, False), (1, True)):
        a_re, a_im, wb, wc = w['s5'][d]
        gx = _mm("s5_gx", dy0, wc, tb=True)
        gs, da = _scan("s5_adj_fwd" if rev else "s5_adj_rev", gx, a_re, -a_im, rev=not rev, xin=sv['xs'][d])
        dwc = _mm("s5_dwc", sv['xs'][d], dy0, ta=True)
        dwb = _mm("s5_dwb", u, gs, ta=True)
        du = _mm("s5_du", gs, wb, tb=True, add=du)
        da = jnp.sum(da, axis=0, keepdims=True)
        s5_cot.append((da[:, :S5_N], da[:, S5_N:], dwb, dwc))
    for d in range(2):
        _, pull = jax.vjp(_s5_prep, *s5_raw[d])
        s5_cot[d] = pull(s5_cot[d])
    for k, nm in enumerate(('s5_lambda_re', 's5_lambda_im', 's5_log_dt', 's5_b_re', 's5_b_im', 's5_c_re', 's5_c_im')):
        grads[nm] = jnp.stack([s5_cot[0][k], s5_cot[1][k]])

    gate = _cw(z, 4, GLA_W)

    def gla_post_bwd(dyv, of, ob, gv, gn):
        o = of + ob
        dos, dgates = [], []
        dgn = jnp.zeros((1, GLA_D), F32)
        for hh in range(GLA_H):
            sl = slice(hh * GLA_D, (hh + 1) * GLA_D)
            gh, dyh = gv[:, sl], dyv[:, sl]
            sg = _sigmoid(gh)
            dgates.append(dyh * _rms(o[:, sl], gn) * sg * (1.0 + gh * (1.0 - sg)))
            dx, dg = _rms_bwd(o[:, sl], gn, dyh * gh * sg)
            dos.append(dx)
            dgn = dgn + dg
        return jnp.concatenate(dos, axis=1), jnp.concatenate(dgates, axis=1), dgn

    do, dgate, dgn = _rowwise("gla_post_bwd", gla_post_bwd, [dy_gla, *sv['o'], gate], [w['gla_norm']],
                              [(GLA_W, F32), (GLA_W, F32)], accs=[(1, GLA_D)])
    grads['gla_norm'] = dgn[0]
    dq_f, dk_f, dv_f, dla_f = _gla_bwd("gla_bwd_fwd", z, sv['la'][0], sv['s'][0], do, rev=False)
    dq_b, dk_b, dv_b, dla_b = _gla_bwd("gla_bwd_rev", z, sv['la'][1], sv['s'][1], do, rev=True)
    wa_f, wa_b = _alpha_pads(w['gla_w_alpha'])
    ba_f, ba_b = w['gla_b_alpha'][0:1], w['gla_b_alpha'][1:2]
    zz = _cw(z, ZC_Z, LANE)

    def gla_prep_bwd(dlf, dlb, zv, waf, wab, bf, bb):
        z16 = zv.astype(BF16)
        waf16, wab16 = waf.astype(BF16), wab.astype(BF16)
        xf = _dot(z16, waf16) + bf
        xb = _dot(z16, wab16) + bb
        df = dlf * (_sigmoid(-xf) / GLA_TAU)
        db = dlb * (_sigmoid(-xb) / GLA_TAU)
        dz = _dot(df.astype(BF16), waf16, _NT) + _dot(db.astype(BF16), wab16, _NT)
        return df, db, dz, jnp.sum(df, axis=0, keepdims=True), jnp.sum(db, axis=0, keepdims=True)

    dlog_f, dlog_b, dzz, dba_f, dba_b = _rowwise(
        "gla_prep_bwd", gla_prep_bwd, [dla_f, dla_b, zz], [wa_f, wa_b, ba_f, ba_b],
        [(GLA_W, BF16), (GLA_W, BF16), (LANE, F32)], accs=[(1, GLA_W), (1, GLA_W)])
    dwa_f = _mm("gla_dwa", zz, dlog_f, ta=True)
    dwa_b = _mm("gla_dwa", zz, dlog_b, ta=True)
    grads['gla_w_alpha'] = jnp.stack([dwa_f[:GLA_R], dwa_b[GLA_R:2 * GLA_R]])
    grads['gla_b_alpha'] = jnp.concatenate([dba_f, dba_b], axis=0)

    dq_r, dkd, dvd = _attn_bwd("attn_bwd", sv['q16'], sv['kd'], sv['vd'], sv['y_at'], sv['lse'], dy_at)
    cq, sq, mq, rq = _head_consts(n_tok, AT_QH)
    ck, sk, mk, rk = _head_consts(n_tok, AT_KH)
    gq = jnp.tile(w['attn_q_norm'], (1, AT_QH))
    gk = jnp.tile(w['attn_k_norm'], (1, AT_KH))

    def rope_bwd(dr, v, gain, cos, sin, mean, rot, scale):
        dr = dr * scale
        dvn = dr * cos + _dot(dr * sin, rot, _NT, precision=HI)
        r = lax.rsqrt(_dot(v * v, mean, precision=HI) + EPS)
        vh = v * r
        dg = jnp.sum(dvn * vh, axis=0, keepdims=True)
        dvh = dvn * gain
        return r * (dvh - vh * _dot(dvh * vh, mean, precision=HI)), dg

    def attn_prep_bwd(dq, dk2, dv2, qv, kv, cqv, sqv, ckv, skv, gqv, gkv, mqv, rqv, mkv, rkv):
        dqo, dgq = rope_bwd(dq, qv, gqv, cqv, sqv, mqv, rqv, AT_D ** -0.5)
        dko, dgk = rope_bwd(_fold_heads(dk2), kv, gkv, ckv, skv, mkv, rkv, 1.0)
        return dqo, dko, _fold_heads(dv2), dgq, dgk

    q_raw, k_raw = _cw(z, 5, AT_W), _cw(z, ZC_AK, AT_KW)
    daq, dak, dav, dgq, dgk = _rowwise("attn_prep_bwd", attn_prep_bwd, [dq_r, dkd, dvd, q_raw, k_raw, cq, sq, ck, sk],
                                       [gq, gk, mq, rq, mk, rk], [(AT_W, F32), (AT_KW, F32), (AT_KW, F32)],
                                       accs=[(1, AT_W), (1, AT_KW)])
    grads['attn_q_norm'] = dgq.reshape(AT_QH, AT_D).sum(axis=0)
    grads['attn_k_norm'] = dgk.reshape(AT_KH, AT_D).sum(axis=0)

    def assemble(duv, qf, qb, kf, kb, vf, vb, dg, aq, ak, av, zzv):
        return jnp.concatenate([duv, qf + qb, kf + kb, vf + vb, dg, aq, ak, av, zzv], axis=1)

    dz = _rowwise("mix_dz", assemble, [du, dq_f, dq_b, dk_f, dk_b, dv_f, dv_b, dgate, daq, dak, dav, dzz], [],
                  [(ZP_W, BF16)])[0]
    grads['w_in'] = _unpad_w_in(_mm("mix_dwin", h, dz, ta=True))
    dh = _mm("mix_dh_in", dz, w['w_in_p'], tb=True, add=dh)

    def dnorm(dyv, xv, dhv, gv):
        dx, dg = _rms_bwd(xv, gv, dhv)
        dx = dyv + dx
        return dx, dx, dg

    dx, dx16, dg = _rowwise("mix_dnorm", dnorm, [dy, sv['x'], dh], [w['mix_norm']], [(D_MODEL, F32), (D_MODEL, BF16)],
                            accs=[(1, D_MODEL)])
    grads['mix_norm'] = dg[0]
    return dx, dx16, grads


FFN_NAMES = [p + k for p in ('ffn1_', 'ffn2_') for k in ('w_gate', 'w_up', 'w_down')]
MISC_SHARDED = [n for n in SHARDED if n not in FFN_NAMES]
LARGE_SHARDED = FFN_NAMES + ['w_in', 'w_merge_gate']


def _layer_weights(ws, rep, i):
    w = {k: _from_shards(ws[k][:, i], SHARD_AXIS[k] - 1) for k in MISC_SHARDED}
    for k in ('ffn1_norm', 'mix_norm', 'ffn2_norm', 's5_d', 'gla_norm', 'attn_q_norm', 'attn_k_norm', 'b_merge_gate'):
        w[k] = rep[k][i].reshape(1, -1)
    w['w_in_p'] = _pad_w_in(w['w_in'])
    s5_raw = [tuple(rep[k][i, d] for k in ('s5_lambda_re', 's5_lambda_im', 's5_log_dt', 's5_b_re', 's5_b_im',
                                            's5_c_re', 's5_c_im')) for d in range(2)]
    w['s5'] = [_s5_prep(*s5_raw[d]) for d in range(2)]
    return w, s5_raw


def _device_step(x, target, ws, rep):
    lw = [_layer_weights(ws, rep, i) for i in range(DEPTH)]
    ffn = {p: (ws[p + 'w_gate'], ws[p + 'w_up'], ws[p + 'w_down']) for p in ('ffn1_', 'ffn2_')}
    saved = []
    for i in range(DEPTH):
        w, _ = lw[i]
        x, s1 = _ffn_fwd(x, w['ffn1_norm'], *ffn['ffn1_'], i)
        x, s2 = _mixer_fwd(x, w)
        x, s3 = _ffn_fwd(x, w['ffn2_norm'], *ffn['ffn2_'], i)
        saved.append((s1, s2, s3))
    gfin = rep['final_norm'].reshape(1, -1)

    def head(xv, tv, gv):
        e = _rms(xv, gv) - tv
        dx, dg = _rms_bwd(xv, gv, e * (1.0 / D_MODEL))
        part = jnp.sum(e * e, axis=0, keepdims=True)
        return dx, dx, dg, part

    dx, dx16, dgfin, part = _rowwise("loss_head", head, [x, target], [gfin], [(D_MODEL, F32), (D_MODEL, BF16)],
                                     accs=[(1, D_MODEL), (1, D_MODEL)])
    loss = (0.5 / D_MODEL) * jnp.sum(part)
    per_layer = []
    dffn = {'ffn1_': None, 'ffn2_': None}
    for i in reversed(range(DEPTH)):
        w, s5_raw = lw[i]
        s1, s2, s3 = saved[i]
        dx, dx16, n3, dffn['ffn2_'] = _ffn_bwd(dx, dx16, s3, w['ffn2_norm'], *ffn['ffn2_'], i, dffn['ffn2_'])
        dx, dx16, g = _mixer_bwd(dx, dx16, s2, w, s5_raw)
        dx, dx16, n1, dffn['ffn1_'] = _ffn_bwd(dx, dx16, s1, w['ffn1_norm'], *ffn['ffn1_'], i, dffn['ffn1_'])
        g['ffn2_norm'], g['ffn1_norm'] = n3, n1
        per_layer.append(g)
    per_layer.reverse()
    stacked = {k: jnp.stack([per_layer[i][k] for i in range(DEPTH)]) for k in per_layer[0]}
    gs = {k: _to_shards(stacked[k], SHARD_AXIS[k]) for k in MISC_SHARDED}
    for p in ('ffn1_', 'ffn2_'):
        gs[p + 'w_gate'], gs[p + 'w_up'], gs[p + 'w_down'] = dffn[p]
    grep = {k: stacked[k] for k in REPLICATED if k != 'final_norm'}
    grep['final_norm'] = dgfin[0]
    return loss, dx, gs, grep


PACK_C = 1024
ROW_TILE = 512


def _rows_of(shape):
    return -(-int(np.prod(shape)) // PACK_C)


def _as_rows(t, lead=0):
    head = t.shape[:lead]
    flat = t.reshape(head + (-1,))
    rows = -(-flat.shape[-1] // PACK_C)
    pad = rows * PACK_C - flat.shape[-1]
    if pad:
        flat = jnp.pad(flat, [(0, 0)] * lead + [(0, pad)])
    return flat.reshape(head + (rows, PACK_C))


def _pack(items, total_rows, lead=0):
    parts = [_as_rows(t, lead) for t in items]
    used = sum(p.shape[lead] for p in parts)
    if total_rows > used:
        parts.append(jnp.zeros(parts[0].shape[:lead] + (total_rows - used, PACK_C), parts[0].dtype))
    return jnp.concatenate(parts, axis=lead)


def _unpack(rows, shapes, lead=0):
    out, r0 = [], 0
    head = rows.shape[:lead]
    for shp in shapes:
        n, size = _rows_of(shp), int(np.prod(shp))
        piece = lax.slice_in_dim(rows, r0, r0 + n, axis=lead).reshape(head + (n * PACK_C,))
        out.append(lax.slice_in_dim(piece, 0, size, axis=lead).reshape(head + tuple(shp)))
        r0 += n
    return out


def _round_up(n, m):
    return -(-n // m) * m


def _to_shards(g, axis):
    shp = g.shape
    g = g.reshape(shp[:axis] + (N_CHIPS, shp[axis] // N_CHIPS) + shp[axis + 1:])
    return jnp.moveaxis(g, axis, 0)


def _from_shards(s, axis):
    s = jnp.moveaxis(s, 0, axis)
    shp = s.shape
    return s.reshape(shp[:axis] + (shp[axis] * shp[axis + 1],) + shp[axis + 2:])


_HBM = pl.BlockSpec(memory_space=pltpu.HBM)


def _place():
    x, y, c = lax.axis_index("x"), lax.axis_index("y"), lax.axis_index("c")
    chips = [(1 - x, y), (x, 1 - y), (1 - x, 1 - y)]
    return x, y, c, chips


def _rcopy(src, dst, send, recv, k, dev):
    return pltpu.make_async_remote_copy(src_ref=src, dst_ref=dst, send_sem=send.at[k], recv_sem=recv.at[k],
                                        device_id=dev, device_id_type=MESH)


def _comm_call(name, body, arrays, out_shapes, n_sems, n_local):
    return pl.pallas_call(
        body, name=name, in_specs=[_HBM] * len(arrays), out_specs=[_HBM] * len(out_shapes), out_shape=out_shapes,
        scratch_shapes=[pltpu.SemaphoreType.DMA((n_sems,)), pltpu.SemaphoreType.DMA((n_sems,)),
                        pltpu.SemaphoreType.DMA((max(n_local, 1),))],
    )(*arrays)


def _ici_peers(x, y, c):
    x_first = c == 0
    first = (jnp.where(x_first, 1 - x, x), jnp.where(x_first, y, 1 - y))
    second = (jnp.where(x_first, x, 1 - x), jnp.where(x_first, 1 - y, y))
    return first, second


def _ag_chips(name, packs):
    n = len(packs)

    def body(*refs):
        p, o, (send, recv, lsem) = refs[:n], refs[n:2 * n], refs[2 * n:]
        x, y, c, _ = _place()
        (ax, ay), (bx, by) = _ici_peers(x, y, c)
        me, na, nb, nd = 2 * x + y, 2 * ax + ay, 2 * bx + by, 2 * (1 - x) + (1 - y)
        sib = (x, y, 1 - c)
        mine = [pltpu.make_async_copy(p[a], o[a].at[me], lsem.at[a]) for a in range(n)]
        own = [_rcopy(p[a].at[c], o[a].at[me, c], send, recv, 6 * a, (ax, ay, c)) for a in range(n)]
        own += [_rcopy(p[a].at[c], o[a].at[me, c], send, recv, 6 * a + 1, (bx, by, c)) for a in range(n)]
        for cp in mine + own:
            cp.start()
        passed = []

        def landed(a, chip, k, fwd):
            blk = o[a].at[chip, c]
            _rcopy(blk, blk, send, recv, 6 * a + k, sib).wait_recv()
            for sem, dev in fwd:
                cp = _rcopy(blk, blk, send, recv, 6 * a + sem, dev)
                cp.start()
                passed.append(cp)

        for a in range(n):
            landed(a, na, 0, [(2, (bx, by, c)), (3, sib)])
        for a in range(n):
            landed(a, nb, 1, [(4, sib)])
        for a in range(n):
            landed(a, nd, 2, [(5, sib)])
        for a in range(n):
            for k, chip in ((3, nb), (4, na), (5, nd)):
                blk = o[a].at[chip, 1 - c]
                _rcopy(blk, blk, send, recv, 6 * a + k, sib).wait_recv()
        for cp in own + passed:
            cp.wait_send()
        for cp in mine:
            cp.wait()

    return _comm_call(name, body, packs, [jax.ShapeDtypeStruct((N_CHIPS,) + t.shape, t.dtype) for t in packs], 6 * n, n)


def _pair_swap(name, gs):
    n = len(gs)
    ns = gs[0].shape[0]

    def body(*refs):
        g, r, (send, recv, _) = refs[:n], refs[n:2 * n], refs[2 * n:]
        x, y, c, _ = _place()
        cps = [_rcopy(g[a].at[j, 1 - c], r[a].at[j], send, recv, ns * a + j, (x, y, 1 - c))
               for a in range(n) for j in range(ns)]
        for cp in cps:
            cp.start()
        for cp in cps:
            cp.wait_recv()
        for cp in cps:
            cp.wait_send()

    return _comm_call(name, body, gs, [jax.ShapeDtypeStruct((ns,) + t.shape[2:], t.dtype) for t in gs], ns * n, 0)


def _halve_first(name, parts):
    n = len(parts)

    def body(*refs):
        p, r, (send, recv, _) = refs[:n], refs[n:2 * n], refs[2 * n:]
        x, y, c, _ = _place()
        (ax, ay), _ = _ici_peers(x, y, c)
        cps = [_rcopy(p[a].at[jnp.where(c == 0, 2 * ax + k, 2 * k + ay)], r[a].at[k], send, recv, 2 * a + k, (ax, ay, c))
               for a in range(n) for k in range(2)]
        for cp in cps:
            cp.start()
        for cp in cps:
            cp.wait_recv()
        for cp in cps:
            cp.wait_send()

    return _comm_call(name, body, parts, [jax.ShapeDtypeStruct((2,) + t.shape[1:], t.dtype) for t in parts], 2 * n, 0)


def _halve_second(name, kept):
    n = len(kept)

    def body(*refs):
        p, r, (send, recv, _) = refs[:n], refs[n:2 * n], refs[2 * n:]
        x, y, c, _ = _place()
        _, (bx, by) = _ici_peers(x, y, c)
        cps = [_rcopy(p[a].at[jnp.where(c == 0, by, bx)], r[a], send, recv, a, (bx, by, c)) for a in range(n)]
        for cp in cps:
            cp.start()
        for cp in cps:
            cp.wait_recv()
        for cp in cps:
            cp.wait_send()

    return _comm_call(name, body, kept, [jax.ShapeDtypeStruct(t.shape[1:], t.dtype) for t in kept], n, 0)


def _add_selected(name, t, r, index, out_dtype):
    single = r.ndim == 2
    r3 = r[None] if single else r
    nk, rh, cc = r3.shape
    tl = _tile(rh, ROW_TILE, 16)

    def body(i_ref, t_ref, r_ref, o_ref):
        o_ref[...] = (t_ref[...].astype(F32) + r_ref[...].astype(F32)).astype(o_ref.dtype)

    out = pl.pallas_call(
        body, name=name,
        grid_spec=pltpu.PrefetchScalarGridSpec(
            num_scalar_prefetch=1, grid=(nk, rh // tl),
            in_specs=[pl.BlockSpec((1, tl, cc), lambda k, i, i_ref: (i_ref[k], i, 0)),
                      pl.BlockSpec((1, tl, cc), lambda k, i, i_ref: (k, i, 0))],
            out_specs=pl.BlockSpec((1, tl, cc), lambda k, i, i_ref: (k, i, 0))),
        out_shape=jax.ShapeDtypeStruct(r3.shape, out_dtype),
        compiler_params=_cparams(("parallel", "parallel"), 8 * tl * cc * 4),
    )(index, t, r3)
    return out[0] if single else out


def _pair_share(name, reds):
    n = len(reds)

    def body(*refs):
        r, o, (send, recv, lsem) = refs[:n], refs[n:2 * n], refs[2 * n:]
        x, y, c, _ = _place()
        mine = [pltpu.make_async_copy(r[a], o[a].at[c], lsem.at[a]) for a in range(n)]
        cps = [_rcopy(r[a], o[a].at[c], send, recv, a, (x, y, 1 - c)) for a in range(n)]
        for cp in mine + cps:
            cp.start()
        for a in range(n):
            blk = o[a].at[1 - c]
            _rcopy(blk, blk, send, recv, a, (x, y, 1 - c)).wait_recv()
        for cp in cps:
            cp.wait_send()
        for cp in mine:
            cp.wait()

    return _comm_call(name, body, reds, [jax.ShapeDtypeStruct((2,) + t.shape, t.dtype) for t in reds], n, n)


def _add_own_half(name, g, r):
    ns, _, rh, cc = g.shape
    tl = _tile(rh, ROW_TILE, 16)
    core = lax.axis_index("c").astype(jnp.int32).reshape(1)

    def body(c_ref, g_ref, r_ref, o_ref):
        o_ref[...] = (g_ref[0] + r_ref[...]).astype(o_ref.dtype)

    return pl.pallas_call(
        body, name=name,
        grid_spec=pltpu.PrefetchScalarGridSpec(
            num_scalar_prefetch=1, grid=(ns, rh // tl),
            in_specs=[pl.BlockSpec((1, 1, tl, cc), lambda j, i, c_ref: (j, c_ref[0], i, 0)),
                      pl.BlockSpec((1, tl, cc), lambda j, i, c_ref: (j, i, 0))],
            out_specs=pl.BlockSpec((1, tl, cc), lambda j, i, c_ref: (j, i, 0))),
        out_shape=jax.ShapeDtypeStruct(r.shape, BF16),
        compiler_params=_cparams(("parallel", "parallel"), 8 * tl * cc * 4),
    )(core, g, r)


def _reduce_scatter(gs):
    x, y, c = lax.axis_index("x"), lax.axis_index("y"), lax.axis_index("c")
    mine_first = jnp.stack([jnp.where(c == 0, 2 * x + k, 2 * k + y) for k in range(2)]).astype(jnp.int32)
    mine_second = jnp.where(c == 0, y, x).astype(jnp.int32).reshape(1)
    swapped = _pair_swap("rs_pair_swap", gs)
    parts = [_add_own_half("rs_add_pair", g, r) for g, r in zip(gs, swapped)]
    got = _halve_first("rs_halve_first", parts)
    kept = [_add_selected("rs_add_first", t, r, mine_first, BF16) for t, r in zip(parts, got)]
    got = _halve_second("rs_halve_second", kept)
    reds = [_add_selected("rs_add_second", t, r, mine_second, F32) for t, r in zip(kept, got)]
    return _pair_share("rs_pair_share", reds)


def _adamw(name, g, w, m, v):
    c1 = 1.0 - ADAM_B1 ** ADAM_STEP
    c2 = 1.0 - ADAM_B2 ** ADAM_STEP
    nb, rows, cc = g.shape
    tl = _tile(rows, ROW_TILE, SUBLANE)

    def body(g_ref, w_ref, m_ref, v_ref, d_ref, mo_ref, vo_ref):
        gv = g_ref[...]
        mn = ADAM_B1 * m_ref[...] + (1.0 - ADAM_B1) * gv
        vn = ADAM_B2 * v_ref[...] + (1.0 - ADAM_B2) * (gv * gv)
        d_ref[...] = -ADAM_LR * ((mn / c1) / (jnp.sqrt(vn / c2) + ADAM_EPS) + ADAM_WD * w_ref[...])
        mo_ref[...] = mn
        vo_ref[...] = vn

    spec = pl.BlockSpec((1, tl, cc), lambda b, i: (b, i, 0))
    return pl.pallas_call(
        body, name=name, grid=(nb, rows // tl), in_specs=[spec] * 4, out_specs=[spec] * 3,
        out_shape=[jax.ShapeDtypeStruct(g.shape, F32)] * 3,
        compiler_params=_cparams(("parallel", "parallel"), 16 * tl * cc * 4),
    )(g, w, m, v)


def _train_step(a):
    x, target = a['x'][0], a['loss_target'][0]

    names = BF16_GATHER + F32_GATHER
    got = _ag_chips("ag_weights", [a[n].astype(BF16) for n in BF16_GATHER] + [a[n] for n in F32_GATHER])
    ws = dict(zip(names, got))

    loss, gx, gs, grep = _device_step(x, target, ws, {n: a[n] for n in REPLICATED})
    loss = lax.psum(loss, ("x", "y", "c"))

    small = [n for n in MISC_SHARDED if n not in LARGE_SHARDED]
    small_shapes = [a[n].shape[1:] for n in small]
    rep_shapes = [a[n].shape for n in REPLICATED]
    small_rows = sum(_rows_of(s) for s in small_shapes)
    piece = _round_up(-(-sum(_rows_of(s) for s in rep_shapes) // (2 * N_CHIPS)), SUBLANE)
    rep_rows = 2 * N_CHIPS * piece
    rep_pack = _pack([grep[n] for n in REPLICATED], rep_rows).reshape(N_CHIPS, 2, piece, PACK_C)
    total = _round_up(small_rows + piece, ROW_TILE // 2)
    misc = _pack([gs[n] for n in small] + [rep_pack], total, lead=2)
    reds = _reduce_scatter([gs[n] for n in LARGE_SHARDED] + [misc])
    grad = dict(zip(LARGE_SHARDED, reds[:-1]))
    grad.update(zip(small, _unpack(reds[-1], small_shapes, lead=1)))
    rep_mine = reds[-1][:, small_rows:small_rows + piece]
    rep_all = _ag_chips("ag_replicated", [rep_mine])[0].reshape(1, rep_rows, PACK_C)

    outs = {}
    for n in SHARDED:
        shp = a[n].shape
        three_d = (shp[0], -1, shp[-1])
        res3 = _adamw("adamw", *(t.reshape(three_d) for t in (grad[n], a[n], a['m_' + n], a['v_' + n])))
        for kind, arr in zip(('grad', 'delta', 'new_m', 'new_v'), (grad[n], *res3)):
            outs[kind + '_' + n] = arr.reshape(shp)
    w, m, v = (_pack([a[p + n] for n in REPLICATED], rep_rows)[None] for p in ('', 'm_', 'v_'))
    res3 = _adamw("adamw_replicated", rep_all, w, m, v)
    for kind, arr in zip(('grad', 'delta', 'new_m', 'new_v'), (rep_all, *res3)):
        for n, t in zip(REPLICATED, _unpack(arr[0], rep_shapes)):
            outs[kind + '_' + n] = t
    res = [loss, gx[None]]
    for kind in ('grad', 'delta', 'new_m', 'new_v'):
        res += [outs[kind + '_' + n] for n in W_NAMES]
    return tuple(res)


def kernel(x, ffn1_norm, ffn1_w_gate, ffn1_w_up, ffn1_w_down, mix_norm, w_in, s5_lambda_re, s5_lambda_im, s5_log_dt, s5_b_re, s5_b_im, s5_c_re, s5_c_im, s5_d, s5_w_glu, gla_w_alpha, gla_b_alpha, gla_norm, attn_q_norm, attn_k_norm, w_branch_s5, w_branch_gla, w_branch_attn, w_merge_gate, b_merge_gate, w_out, ffn2_norm, ffn2_w_gate, ffn2_w_up, ffn2_w_down, final_norm, loss_target, m_ffn1_norm, m_ffn1_w_gate, m_ffn1_w_up, m_ffn1_w_down, m_mix_norm, m_w_in, m_s5_lambda_re, m_s5_lambda_im, m_s5_log_dt, m_s5_b_re, m_s5_b_im, m_s5_c_re, m_s5_c_im, m_s5_d, m_s5_w_glu, m_gla_w_alpha, m_gla_b_alpha, m_gla_norm, m_attn_q_norm, m_attn_k_norm, m_w_branch_s5, m_w_branch_gla, m_w_branch_attn, m_w_merge_gate, m_b_merge_gate, m_w_out, m_ffn2_norm, m_ffn2_w_gate, m_ffn2_w_up, m_ffn2_w_down, m_final_norm, v_ffn1_norm, v_ffn1_w_gate, v_ffn1_w_up, v_ffn1_w_down, v_mix_norm, v_w_in, v_s5_lambda_re, v_s5_lambda_im, v_s5_log_dt, v_s5_b_re, v_s5_b_im, v_s5_c_re, v_s5_c_im, v_s5_d, v_s5_w_glu, v_gla_w_alpha, v_gla_b_alpha, v_gla_norm, v_attn_q_norm, v_attn_k_norm, v_w_branch_s5, v_w_branch_gla, v_w_branch_attn, v_w_merge_gate, v_b_merge_gate, v_w_out, v_ffn2_norm, v_ffn2_w_gate, v_ffn2_w_up, v_ffn2_w_down, v_final_norm):
    return _train_step(dict(locals()))
```

```python
import functools
import math

import jax
import jax.numpy as jnp
import numpy as np
from jax import lax
from jax.experimental import pallas as pl
from jax.experimental.pallas import tpu as pltpu

F32 = jnp.float32
BF16 = jnp.bfloat16
HI = lax.Precision.HIGHEST

D_MODEL = 1024
D_FF = 2816
DEPTH = 2
EPS = 1e-6
S5_G, S5_H, S5_P = 32, 16, 64
S5_W = S5_G * S5_H
S5_N = S5_G * S5_P
GLA_H, GLA_D = 4, 128
GLA_W = GLA_H * GLA_D
GLA_R = 16
GLA_TAU = 16.0
GLA_C = 64
AT_QH, AT_KH, AT_D = 8, 2, 64
AT_W = AT_QH * AT_D
AT_KW = AT_KH * AT_D
GRID_W = 64
ROPE_BASE = 10000.0
ZP_W = 3456
ZC_AK, ZC_AV, ZC_Z = 24, 25, 26

ADAM_LR, ADAM_B1, ADAM_B2, ADAM_EPS, ADAM_WD, ADAM_STEP = 0.001, 0.9, 0.999, 1e-08, 0.01, 10

VMEM_LIMIT_V7X = 56 * 1024 * 1024
LANE = 128
SUBLANE = 8

W_NAMES = ['ffn1_norm', 'ffn1_w_gate', 'ffn1_w_up', 'ffn1_w_down', 'mix_norm', 'w_in', 's5_lambda_re', 's5_lambda_im',
           's5_log_dt', 's5_b_re', 's5_b_im', 's5_c_re', 's5_c_im', 's5_d', 's5_w_glu', 'gla_w_alpha', 'gla_b_alpha',
           'gla_norm', 'attn_q_norm', 'attn_k_norm', 'w_branch_s5', 'w_branch_gla', 'w_branch_attn', 'w_merge_gate',
           'b_merge_gate', 'w_out', 'ffn2_norm', 'ffn2_w_gate', 'ffn2_w_up', 'ffn2_w_down', 'final_norm']
SHARD_AXIS = {'ffn1_w_gate': 2, 'ffn1_w_up': 2, 'ffn1_w_down': 1, 'w_in': 2, 's5_w_glu': 1, 'gla_w_alpha': 3,
              'gla_b_alpha': 2, 'w_branch_s5': 2, 'w_branch_gla': 2, 'w_branch_attn': 2, 'w_merge_gate': 2,
              'w_out': 1, 'ffn2_w_gate': 2, 'ffn2_w_up': 2, 'ffn2_w_down': 1}
F32_GATHER = ['gla_w_alpha', 'gla_b_alpha']
BF16_GATHER = [n for n in W_NAMES if n in SHARD_AXIS and n not in F32_GATHER]
SHARDED = [n for n in W_NAMES if n in SHARD_AXIS]
REPLICATED = [n for n in W_NAMES if n not in SHARD_AXIS]
N_CHIPS = 4
MESH = pl.DeviceIdType.MESH


def _tile(n, target, mult):
    best = None
    for t in range(mult, min(n, target) + 1, mult):
        if n % t == 0:
            best = t
    return best if best is not None else n


def _cparams(sem, vmem_bytes):
    limit = int(min(VMEM_LIMIT_V7X, max(32 * 1024 * 1024, vmem_bytes * 5 // 4)))
    return pltpu.CompilerParams(dimension_semantics=sem, vmem_limit_bytes=limit)


def _pcall(body, **kw):
    shapes = kw.pop('out_shape')
    many = isinstance(shapes, (list, tuple))
    pinned = [pltpu.HBM(t.shape, t.dtype) for t in (shapes if many else [shapes])]
    call = pl.pallas_call(body, out_shape=pinned if many else pinned[0], **kw)

    def run(*ops):
        return call(*[o if jnp.issubdtype(o.dtype, jnp.integer) else pltpu.with_memory_space_constraint(o, pltpu.HBM)
                      for o in ops])

    return run


def _cw(arr, cb, width):
    return (arr, cb, width)


def _win(a):
    if isinstance(a, tuple):
        return a
    return (a, 0, a.shape[-1])


def _nbytes(shape, dtype):
    return int(np.prod(shape)) * jnp.dtype(dtype).itemsize


def _rowwise(name, fn, rows, fulls, outs, accs=(), tl=256):
    rows = [_win(r) for r in rows]
    n_tok = rows[0][0].shape[0]
    tl = _tile(n_tok, tl, 16)
    nr, nf, no = len(rows), len(fulls), len(outs)

    def body(*refs):
        vals = [r[...] for r in refs[:nr + nf]]
        res = fn(*vals)
        if not isinstance(res, (tuple, list)):
            res = (res,)
        for k in range(no):
            o = refs[nr + nf + k]
            o[...] = res[k].astype(o.dtype)
        if accs:
            @pl.when(pl.program_id(0) == 0)
            def _():
                for k in range(len(accs)):
                    a = refs[nr + nf + no + k]
                    a[...] = jnp.zeros(a.shape, a.dtype)
            for k in range(len(accs)):
                a = refs[nr + nf + no + k]
                a[...] += res[no + k]

    in_specs = [pl.BlockSpec((tl, w), functools.partial(lambda i, cb: (i, cb), cb=cb)) for (_, cb, w) in rows]
    in_specs += [pl.BlockSpec(f.shape, functools.partial(lambda i, nd: (0,) * nd, nd=f.ndim)) for f in fulls]
    out_specs = [pl.BlockSpec((tl, c), lambda i: (i, 0)) for (c, _) in outs]
    out_specs += [pl.BlockSpec((r, c), lambda i: (0, 0)) for (r, c) in accs]
    out_shape = [jax.ShapeDtypeStruct((n_tok, c), dt) for (c, dt) in outs]
    out_shape += [jax.ShapeDtypeStruct((r, c), F32) for (r, c) in accs]
    vm = 2 * sum(_nbytes((tl, w), a.dtype) for (a, _, w) in rows)
    vm += 2 * sum(_nbytes(f.shape, f.dtype) for f in fulls)
    vm += 2 * sum(_nbytes((tl, c), dt) for (c, dt) in outs)
    vm += 8 * max([_nbytes((tl, w), F32) for (_, _, w) in rows] + [_nbytes((tl, c), F32) for (c, _) in outs])
    return _pcall(
        body, name=name, grid=(n_tok // tl,), in_specs=in_specs, out_specs=out_specs, out_shape=out_shape,
        compiler_params=_cparams(("arbitrary",) if accs else ("parallel",), vm),
    )(*[r[0] for r in rows], *fulls)


def _mm(name, a, b, *, ta=False, tb=False, out_dtype=F32, alpha=1.0, add=None, tm=512, tn=512):
    a_arr, a_cb, a_w = _win(a)
    b_arr, b_cb, b_w = _win(b)
    if ta:
        kdim, m = a_arr.shape[0], a_w
    else:
        m, kdim = a_arr.shape[0], a_w
    if tb:
        n, kb = b_arr.shape[0], b_w
    else:
        kb, n = b_arr.shape[0], b_w
    assert kdim == kb, (name, kdim, kb)
    tm = _tile(m, tm, LANE if ta else 16)
    tn = _tile(n, tn, LANE)
    if ta:
        a_spec = pl.BlockSpec((kdim, tm), lambda i, j: (0, a_cb * (a_w // tm) + i))
        a_blk = (kdim, tm)
    else:
        a_spec = pl.BlockSpec((tm, kdim), lambda i, j: (i, a_cb))
        a_blk = (tm, kdim)
    if tb:
        b_spec = pl.BlockSpec((tn, kdim), lambda i, j: (j, b_cb))
        b_blk = (tn, kdim)
    else:
        b_spec = pl.BlockSpec((kdim, tn), lambda i, j: (0, b_cb * (b_w // tn) + j))
        b_blk = (kdim, tn)
    dims = (((0 if ta else 1,), (1 if tb else 0,)), ((), ()))
    has_add = add is not None

    def body(*refs):
        a_ref, b_ref = refs[0], refs[1]
        o_ref = refs[-1]
        acc = lax.dot_general(a_ref[...].astype(BF16), b_ref[...].astype(BF16), dims, preferred_element_type=F32)
        if alpha != 1.0:
            acc = acc * alpha
        if has_add:
            acc = acc + refs[2][...].astype(F32)
        o_ref[...] = acc.astype(o_ref.dtype)

    in_specs = [a_spec, b_spec]
    ops = [a_arr, b_arr]
    if has_add:
        in_specs.append(pl.BlockSpec((tm, tn), lambda i, j: (i, j)))
        ops.append(add)
    vm = 2 * (_nbytes(a_blk, a_arr.dtype) + _nbytes(b_blk, b_arr.dtype) + _nbytes((tm, tn), out_dtype))
    vm += _nbytes(a_blk, BF16) + _nbytes(b_blk, BF16) + 3 * _nbytes((tm, tn), F32)
    return _pcall(
        body, name=name, grid=(m // tm, n // tn), in_specs=in_specs,
        out_specs=pl.BlockSpec((tm, tn), lambda i, j: (i, j)),
        out_shape=jax.ShapeDtypeStruct((m, n), out_dtype),
        compiler_params=_cparams(("parallel", "parallel"), vm),
    )(*ops)


def _sigmoid(x):
    return 1.0 / (1.0 + jnp.exp(-x))


def _rms(x, gain):
    r = lax.rsqrt(jnp.mean(x * x, axis=-1, keepdims=True) + EPS)
    return x * r * gain


def _rms_bwd(x, gain, dh):
    r = lax.rsqrt(jnp.mean(x * x, axis=-1, keepdims=True) + EPS)
    xh = x * r
    dg = jnp.sum(dh * xh, axis=0, keepdims=True)
    dxh = dh * gain
    dx = r * (dxh - xh * jnp.mean(dxh * xh, axis=-1, keepdims=True))
    return dx, dg


_GELU_C = math.sqrt(2.0 / math.pi)


def _gelu(y):
    return 0.5 * y * (1.0 + jnp.tanh(_GELU_C * (y + 0.044715 * y * y * y)))


def _gelu_grad(y):
    th = jnp.tanh(_GELU_C * (y + 0.044715 * y * y * y))
    return 0.5 * (1.0 + th) + 0.5 * y * (1.0 - th * th) * _GELU_C * (1.0 + 3.0 * 0.044715 * y * y)


def _log_sigmoid(x):
    return jnp.minimum(x, 0.0) - jnp.log(1.0 + jnp.exp(-jnp.abs(x)))


def _dot(a, b, dims=(((1,), (0,)), ((), ())), precision=None):
    return lax.dot_general(a, b, dims, preferred_element_type=F32, precision=precision)


_NT = (((1,), (1,)), ((), ()))
_TN = (((0,), (0,)), ((), ()))


def _scan(name, b, a_re, a_im, *, rev, xin=None, tb=256, lc=512):
    n_tok = b.shape[0]
    tb = _tile(n_tok, tb, 16)
    nt, nb, nc = n_tok // tb, tb // SUBLANE, S5_N // lc
    with_da = xin is not None

    def body(*refs):
        if with_da:
            b_ref, are_ref, aim_ref, x_ref, o_ref, da_ref, pw, last, of32, xf32 = refs
        else:
            b_ref, are_ref, aim_ref, o_ref, pw, last, of32 = refs
        t = pl.program_id(0)
        row = lax.broadcasted_iota(jnp.int32, (SUBLANE, lc), 0)

        def cmul(ar, ai, xr, xi):
            return ar * xr - ai * xi, ar * xi + ai * xr

        def hs(c, xr, xi):
            cs = pl.ds(c * lc, lc)
            for k, slot in ((1, 0), (2, 2), (4, 4)):
                ar, ai = pw[slot, :, cs], pw[slot + 1, :, cs]
                if rev:
                    sr, si = pltpu.roll(xr, SUBLANE - k, 0), pltpu.roll(xi, SUBLANE - k, 0)
                    keep = row < SUBLANE - k
                else:
                    sr, si = pltpu.roll(xr, k, 0), pltpu.roll(xi, k, 0)
                    keep = row >= k
                sr, si = jnp.where(keep, sr, 0.0), jnp.where(keep, si, 0.0)
                pr, pi = cmul(ar, ai, sr, si)
                xr, xi = xr + pr, xi + pi
            return xr, xi

        @pl.when(t == 0)
        def _():
            last[...] = jnp.zeros(last.shape, F32)
            if with_da:
                da_ref[...] = jnp.zeros(da_ref.shape, F32)
            for c in range(nc):
                cs = pl.ds(c * lc, lc)
                a1r = jnp.broadcast_to(are_ref[:, cs], (SUBLANE, lc))
                a1i = jnp.broadcast_to(aim_ref[:, cs], (SUBLANE, lc))
                a2r, a2i = cmul(a1r, a1i, a1r, a1i)
                a4r, a4i = cmul(a2r, a2i, a2r, a2i)
                for slot, v in enumerate((a1r, a1i, a2r, a2i, a4r, a4i)):
                    pw[slot, :, cs] = v
                first = SUBLANE - 1 if rev else 0
                pr, pi = hs(c, jnp.where(row == first, a1r, 0.0), jnp.where(row == first, a1i, 0.0))
                pw[6, :, cs] = pr
                pw[7, :, cs] = pi

        if with_da:
            xf32[...] = x_ref[...].astype(F32)
        edge = 0 if rev else SUBLANE - 1

        def blk(j, carry):
            jj = (nb - 1 - j) if rev else j
            r0 = pl.multiple_of(jj * SUBLANE, SUBLANE)
            for c in range(nc):
                cre, cim = pl.ds(c * lc, lc), pl.ds(S5_N + c * lc, lc)
                xr, xi = hs(c, b_ref[pl.ds(r0, SUBLANE), cre], b_ref[pl.ds(r0, SUBLANE), cim])
                cr = jnp.broadcast_to(last[edge:edge + 1, cre], (SUBLANE, lc))
                ci = jnp.broadcast_to(last[edge:edge + 1, cim], (SUBLANE, lc))
                pr, pi = cmul(pw[6, :, cre], pw[7, :, cre], cr, ci)
                xr, xi = xr + pr, xi + pi
                if with_da:
                    if rev:
                        qr, qi = pltpu.roll(xr, SUBLANE - 1, 0), pltpu.roll(xi, SUBLANE - 1, 0)
                        fill = row == SUBLANE - 1
                    else:
                        qr, qi = pltpu.roll(xr, 1, 0), pltpu.roll(xi, 1, 0)
                        fill = row == 0
                    qr, qi = jnp.where(fill, cr, qr), jnp.where(fill, ci, qi)
                    zr, zi = xf32[pl.ds(r0, SUBLANE), cre], xf32[pl.ds(r0, SUBLANE), cim]
                    da_ref[:, cre] += zr * qr + zi * qi
                    da_ref[:, cim] += zr * qi - zi * qr
                last[:, cre] = xr
                last[:, cim] = xi
                of32[pl.ds(r0, SUBLANE), cre] = xr
                of32[pl.ds(r0, SUBLANE), cim] = xi
            return carry

        lax.fori_loop(0, nb, blk, 0)
        o_ref[...] = of32[...].astype(o_ref.dtype)

    tmap = (lambda t: (nt - 1 - t, 0)) if rev else (lambda t: (t, 0))
    in_specs = [pl.BlockSpec((tb, 2 * S5_N), tmap), pl.BlockSpec((1, S5_N), lambda t: (0, 0)),
                pl.BlockSpec((1, S5_N), lambda t: (0, 0))]
    ops = [b, a_re, a_im]
    out_specs = [pl.BlockSpec((tb, 2 * S5_N), tmap)]
    out_shape = [jax.ShapeDtypeStruct((n_tok, 2 * S5_N), BF16)]
    scratch = [pltpu.VMEM((8, SUBLANE, S5_N), F32), pltpu.VMEM((SUBLANE, 2 * S5_N), F32),
               pltpu.VMEM((tb, 2 * S5_N), F32)]
    if with_da:
        in_specs.append(pl.BlockSpec((tb, 2 * S5_N), tmap))
        ops.append(xin)
        out_specs.append(pl.BlockSpec((SUBLANE, 2 * S5_N), lambda t: (0, 0)))
        out_shape.append(jax.ShapeDtypeStruct((SUBLANE, 2 * S5_N), F32))
        scratch.append(pltpu.VMEM((tb, 2 * S5_N), F32))
    vm = 2 * _nbytes((tb, 2 * S5_N), F32) + 4 * _nbytes((tb, 2 * S5_N), BF16) + 2 * _nbytes((tb, 2 * S5_N), F32)
    res = _pcall(
        body, name=name, grid=(nt,), in_specs=in_specs, out_specs=out_specs, out_shape=out_shape,
        scratch_shapes=scratch, compiler_params=_cparams(("arbitrary",), vm),
    )(*ops)
    return res if with_da else res[0]


GLA_TB = 512


def _gla_block_terms(la, q, k, rev):
    tb = la.shape[0]
    ri = lax.broadcasted_iota(jnp.int32, (tb, tb), 0)
    ci = lax.broadcasted_iota(jnp.int32, (tb, tb), 1)
    same = (ri // GLA_C) == (ci // GLA_C)
    tri = same & ((ri <= ci) if rev else (ri >= ci))
    trif = tri.astype(F32)
    bcum = _dot(trif, la, precision=HI)
    blast = _dot(same.astype(F32), la, precision=HI)
    eb, enb, ee = jnp.exp(bcum), jnp.exp(-bcum), jnp.exp(blast - bcum)
    return trif, jnp.exp(blast), eb, enb, ee, q * eb, k * enb, k * ee


def _chunk_mask(rev):
    ri = lax.broadcasted_iota(jnp.int32, (GLA_C, GLA_C), 0)
    ci = lax.broadcasted_iota(jnp.int32, (GLA_C, GLA_C), 1)
    return (ri <= ci) if rev else (ri >= ci)


def _gla_fwd(name, z, la, *, rev):
    n_tok = z.shape[0]
    tb = _tile(n_tok, GLA_TB, GLA_C)
    nblk, per = n_tok // tb, tb // GLA_C
    scale = GLA_D ** -0.5

    def body(q_ref, k_ref, v_ref, la_ref, o_ref, s_ref, st, qd_s, kd_s, ke_s, v_s, dec_s):
        @pl.when(pl.program_id(0) == 0)
        def _():
            st[...] = jnp.zeros(st.shape, F32)

        _, dec, _, _, _, qd, kd, ke = _gla_block_terms(la_ref[...], q_ref[...] * scale, k_ref[...], rev)
        qd_s[...], kd_s[...], ke_s[...] = qd.astype(BF16), kd.astype(BF16), ke.astype(BF16)
        v_s[...] = v_ref[...].astype(BF16)
        dec_s[...] = dec
        tri = _chunk_mask(rev)
        for step in range(per):
            t = per - 1 - step if rev else step
            rows = pl.ds(t * GLA_C, GLA_C)
            for h in range(GLA_H):
                lanes = pl.ds(h * GLA_D, GLA_D)
                qd16, v16 = qd_s[rows, lanes], v_s[rows, lanes]
                sc = jnp.where(tri, _dot(qd16, kd_s[rows, lanes], _NT), 0.0)
                s16 = st[h].astype(BF16)
                s_ref[h, step] = s16
                o_ref[rows, lanes] = _dot(sc.astype(BF16), v16) + _dot(qd16, s16, _NT)
                st[h] = st[h] * dec_s[pl.ds(t * GLA_C, 1), lanes] + _dot(v16, ke_s[rows, lanes], _TN)

    bmap = (lambda n: nblk - 1 - n) if rev else (lambda n: n)

    def zspec(cb):
        return pl.BlockSpec((tb, GLA_W), lambda n: (bmap(n), cb))

    half = pltpu.VMEM((tb, GLA_W), BF16)
    return _pcall(
        body, name=name, grid=(nblk,),
        in_specs=[zspec(1), zspec(2), zspec(3), zspec(0)],
        out_specs=[zspec(0), pl.BlockSpec((GLA_H, per, GLA_D, GLA_D), lambda n: (0, n, 0, 0))],
        out_shape=[jax.ShapeDtypeStruct((n_tok, GLA_W), F32),
                   jax.ShapeDtypeStruct((GLA_H, n_tok // GLA_C, GLA_D, GLA_D), BF16)],
        scratch_shapes=[pltpu.VMEM((GLA_H, GLA_D, GLA_D), F32), half, half, half, half, pltpu.VMEM((tb, GLA_W), F32)],
        compiler_params=_cparams(("arbitrary",), 40 * tb * GLA_W * 4),
    )(z, z, z, la)


def _gla_bwd(name, z, la, s_prev, do, *, rev):
    n_tok = z.shape[0]
    tb = _tile(n_tok, GLA_TB, GLA_C)
    nblk, per = n_tok // tb, tb // GLA_C
    scale = GLA_D ** -0.5

    def body(q_ref, k_ref, v_ref, la_ref, s_ref, do_ref, dq_ref, dk_ref, dv_ref, dla_ref,
             dst, qd_s, kd_s, ke_s, v_s, do_s, dec_s, dqd_s, dkd_s, dke_s, dd_s):
        @pl.when(pl.program_id(0) == 0)
        def _():
            dst[...] = jnp.zeros(dst.shape, F32)

        trif, dec, eb, enb, ee, qd, kd, ke = _gla_block_terms(la_ref[...], q_ref[...] * scale, k_ref[...], rev)
        qd_s[...], kd_s[...], ke_s[...] = qd.astype(BF16), kd.astype(BF16), ke.astype(BF16)
        v_s[...] = v_ref[...].astype(BF16)
        do_s[...] = do_ref[...].astype(BF16)
        dec_s[...] = dec
        tri = _chunk_mask(rev)
        for step in range(per):
            t = step if rev else per - 1 - step
            rows = pl.ds(t * GLA_C, GLA_C)
            for h in range(GLA_H):
                lanes = pl.ds(h * GLA_D, GLA_D)
                qd16, kd16, ke16 = qd_s[rows, lanes], kd_s[rows, lanes], ke_s[rows, lanes]
                v16, do16 = v_s[rows, lanes], do_s[rows, lanes]
                sc16 = jnp.where(tri, _dot(qd16, kd16, _NT), 0.0).astype(BF16)
                dsc16 = jnp.where(tri, _dot(do16, v16, _NT), 0.0).astype(BF16)
                s16 = s_ref[h, per - 1 - step]
                dsn = dst[h]
                dsn16 = dsn.astype(BF16)
                dec_c = dec_s[pl.ds(t * GLA_C, 1), lanes]
                dqd_s[rows, lanes] = _dot(dsc16, kd16) + _dot(do16, s16)
                dkd_s[rows, lanes] = _dot(dsc16, qd16, _TN)
                dke_s[rows, lanes] = _dot(v16, dsn16)
                dv_ref[rows, lanes] = _dot(sc16, do16, _TN) + _dot(ke16, dsn16, _NT)
                ddec = jnp.sum(dsn * s16.astype(F32), axis=0, keepdims=True) * dec_c
                dd_s[rows, lanes] = jnp.broadcast_to(ddec, (GLA_C, GLA_D))
                dst[h] = dsn * dec_c + _dot(do16, qd16, _TN)
        dqd, dkd, dke = dqd_s[...], dkd_s[...], dke_s[...]
        dq_ref[...] = dqd * eb * scale
        dk_ref[...] = dkd * enb + dke * ee
        db = dqd * qd - dkd * kd - dke * ke
        ri = lax.broadcasted_iota(jnp.int32, (tb, tb), 0)
        ci = lax.broadcasted_iota(jnp.int32, (tb, tb), 1)
        same = ((ri // GLA_C) == (ci // GLA_C)).astype(F32)
        dla_ref[...] = _dot(trif, db, _TN, precision=HI) + _dot(same, dke * ke, precision=HI) + dd_s[...]

    bmap = (lambda n: n) if rev else (lambda n: nblk - 1 - n)

    def zspec(cb):
        return pl.BlockSpec((tb, GLA_W), lambda n: (bmap(n), cb))

    o4 = jax.ShapeDtypeStruct((n_tok, GLA_W), F32)
    half, full = pltpu.VMEM((tb, GLA_W), BF16), pltpu.VMEM((tb, GLA_W), F32)
    return _pcall(
        body, name=name, grid=(nblk,),
        in_specs=[zspec(1), zspec(2), zspec(3), zspec(0),
                  pl.BlockSpec((GLA_H, per, GLA_D, GLA_D), lambda n: (0, nblk - 1 - n, 0, 0)), zspec(0)],
        out_specs=[zspec(0)] * 4, out_shape=[o4] * 4,
        scratch_shapes=[pltpu.VMEM((GLA_H, GLA_D, GLA_D), F32), half, half, half, half, half, full, full, full, full, full],
        compiler_params=_cparams(("arbitrary",), 64 * tb * GLA_W * 4),
    )(z, z, z, la, s_prev, do)


_PAIRS = AT_W // LANE
_PAIRS_PER_KV = _PAIRS // AT_KH


def _attn_fwd(name, q, kd, vd, tq=256):
    n_tok = q.shape[0]
    tq = _tile(n_tok, tq, 16)

    def body(q_ref, k_ref, v_ref, o_ref, l_ref):
        qv, kv, vv = q_ref[...], k_ref[...], v_ref[...]
        low = lax.broadcasted_iota(jnp.int32, qv.shape, 1) < AT_D
        res = []
        for keep in (low, jnp.logical_not(low)):
            s = _dot(jnp.where(keep, qv, jnp.zeros_like(qv)), kv, _NT)
            m = jnp.max(s, axis=-1, keepdims=True)
            p = jnp.exp(s - m)
            den = jnp.sum(p, axis=-1, keepdims=True)
            res.append((_dot(p.astype(BF16), vv) / den, m + jnp.log(den)))
        o_ref[...] = jnp.where(low, res[0][0], res[1][0])
        l_ref[...] = jnp.where(low, res[0][1], res[1][1])

    qspec = pl.BlockSpec((tq, LANE), lambda p, i: (i, p))
    kspec = pl.BlockSpec((n_tok, LANE), lambda p, i: (0, p // _PAIRS_PER_KV))
    vm = 5 * _nbytes((tq, n_tok), F32) + 8 * _nbytes((n_tok, LANE), BF16)
    out = jax.ShapeDtypeStruct((n_tok, AT_W), F32)
    return _pcall(
        body, name=name, grid=(_PAIRS, n_tok // tq), in_specs=[qspec, kspec, kspec], out_specs=[qspec, qspec],
        out_shape=[out, out], compiler_params=_cparams(("parallel", "parallel"), vm),
    )(q, kd, vd)


def _attn_bwd(name, q, kd, vd, o, lse, do, tq=256):
    n_tok = q.shape[0]
    tq = _tile(n_tok, tq, 16)

    def body(q_ref, k_ref, v_ref, o_ref, l_ref, do_ref, dq_ref, dk_ref, dv_ref):
        @pl.when((pl.program_id(1) == 0) & (pl.program_id(2) == 0))
        def _():
            dk_ref[...] = jnp.zeros(dk_ref.shape, F32)
            dv_ref[...] = jnp.zeros(dv_ref.shape, F32)

        qv, kv, vv = q_ref[...], k_ref[...], v_ref[...]
        dof, lv = do_ref[...], l_ref[...]
        do16 = dof.astype(BF16)
        doo = dof * o_ref[...]
        low = lax.broadcasted_iota(jnp.int32, qv.shape, 1) < AT_D
        dq = jnp.zeros(qv.shape, F32)
        for keep in (low, jnp.logical_not(low)):
            qm = jnp.where(keep, qv, jnp.zeros_like(qv))
            dom = jnp.where(keep, do16, jnp.zeros_like(do16))
            lh = jnp.max(jnp.where(keep, lv, -1e30), axis=-1, keepdims=True)
            p = jnp.exp(_dot(qm, kv, _NT) - lh)
            delta = jnp.sum(jnp.where(keep, doo, 0.0), axis=-1, keepdims=True)
            ds16 = (p * (_dot(dom, vv, _NT) - delta)).astype(BF16)
            dq = jnp.where(keep, _dot(ds16, kv), dq)
            dk_ref[...] += _dot(ds16, qm, _TN)
            dv_ref[...] += _dot(p.astype(BF16), dom, _TN)
        dq_ref[...] = dq

    qspec = pl.BlockSpec((tq, LANE), lambda g, j, i: (i, g * _PAIRS_PER_KV + j))
    kspec = pl.BlockSpec((n_tok, LANE), lambda g, j, i: (0, g))
    vm = 7 * _nbytes((tq, n_tok), F32) + 12 * _nbytes((n_tok, LANE), F32)
    dup = jax.ShapeDtypeStruct((n_tok, AT_KH * LANE), F32)
    return _pcall(
        body, name=name, grid=(AT_KH, _PAIRS_PER_KV, n_tok // tq),
        in_specs=[qspec, kspec, kspec, qspec, qspec, qspec], out_specs=[qspec, kspec, kspec],
        out_shape=[jax.ShapeDtypeStruct((n_tok, AT_W), F32), dup, dup],
        compiler_params=_cparams(("parallel", "arbitrary", "arbitrary"), vm),
    )(q, kd, vd, o, lse, do)


def _block_diag(t):
    g, a, b = t.shape
    eye = jnp.eye(g, dtype=t.dtype)
    return (t[:, :, None, :] * eye[:, None, :, None]).reshape(g * a, g * b)


def _s5_prep(lam_re, lam_im, log_dt, b_re, b_im, c_re, c_im):
    dt = jnp.exp(log_dt)[:, None]
    er = jnp.exp(lam_re * dt)
    a_re, a_im = er * jnp.cos(lam_im * dt), er * jnp.sin(lam_im * dt)
    nr, ni = a_re - 1.0, a_im
    den = lam_re * lam_re + lam_im * lam_im
    f_re, f_im = (nr * lam_re + ni * lam_im) / den, (ni * lam_re - nr * lam_im) / den
    bb_re = f_re[..., None] * b_re - f_im[..., None] * b_im
    bb_im = f_re[..., None] * b_im + f_im[..., None] * b_re
    w_in = jnp.concatenate([_block_diag(bb_re.transpose(0, 2, 1)), _block_diag(bb_im.transpose(0, 2, 1))], axis=1)
    w_out = jnp.concatenate([_block_diag(c_re.transpose(0, 2, 1)), -_block_diag(c_im.transpose(0, 2, 1))], axis=0)
    return a_re.reshape(1, S5_N), a_im.reshape(1, S5_N), w_in, w_out


def _rope_tables(n_tok):
    quarter = AT_D // 4
    d = np.arange(AT_D)
    e = d % (AT_D // 2)
    inv = (ROPE_BASE ** (-(e % quarter).astype(np.float64) * 2.0 / (AT_D // 2))).astype(np.float32)
    sign = np.where(e < quarter, -1.0, 1.0).astype(np.float32)
    rot = np.zeros((AT_D, AT_D), np.float32)
    rot[np.where(e < quarter, d + quarter, d - quarter), d] = 1.0
    pos = jnp.arange(n_tok, dtype=jnp.int32)
    axis_pos = jnp.where(jnp.asarray(d // (AT_D // 2) == 0)[None, :], (pos // GRID_W)[:, None], (pos % GRID_W)[:, None])
    ang = axis_pos.astype(F32) * jnp.asarray(inv)[None, :]
    return jnp.cos(ang), jnp.sin(ang) * jnp.asarray(sign)[None, :], rot


def _head_consts(n_tok, heads):
    cos, sin, rot = _rope_tables(n_tok)
    eye = np.eye(heads, dtype=np.float32)
    mean = np.kron(eye, np.full((AT_D, AT_D), 1.0 / AT_D, np.float32))
    return jnp.tile(cos, (1, heads)), jnp.tile(sin, (1, heads)), jnp.asarray(mean), jnp.asarray(np.kron(eye, rot))


def _pad_w_in(w):
    return jnp.concatenate([w[:, :2560], w[:, 2592:3360], w[:, 2560:2592], jnp.zeros((w.shape[0], 96), w.dtype)], axis=1)


def _unpad_w_in(g):
    return jnp.concatenate([g[:, :2560], g[:, 3328:3360], g[:, 2560:3328]], axis=1)


def _ffn_fwd(x, g, wg, wu, wd, d):
    n_tok = x.shape[0]
    fq = wg.shape[-1]
    h = _rowwise("ffn_norm", lambda xv, gv: _rms(xv, gv), [x], [g], [(D_MODEL, BF16)])[0]
    tm = _tile(n_tok, 512, 16)

    def up(h_ref, wg_ref, wu_ref, a_ref, b_ref, s_ref):
        hv = h_ref[...]
        av, bv = _dot(hv, wg_ref[0, 0]), _dot(hv, wu_ref[0, 0])
        a_ref[0] = av.astype(BF16)
        b_ref[0] = bv.astype(BF16)
        s_ref[0] = (av * _sigmoid(av) * bv).astype(BF16)

    wspec = pl.BlockSpec((1, 1, D_MODEL, fq), lambda i, j: (j, d, 0, 0))
    aspec = pl.BlockSpec((1, tm, fq), lambda i, j: (j, i, 0))
    act = jax.ShapeDtypeStruct((N_CHIPS, n_tok, fq), BF16)
    a, b, s = _pcall(
        up, name="ffn_up", grid=(n_tok // tm, N_CHIPS),
        in_specs=[pl.BlockSpec((tm, D_MODEL), lambda i, j: (i, 0)), wspec, wspec],
        out_specs=[aspec, aspec, aspec], out_shape=[act, act, act],
        compiler_params=_cparams(("parallel", "parallel"), 4 * D_MODEL * fq * 2 + 16 * tm * fq * 4),
    )(h, wg, wu)
    tn = _tile(D_MODEL, 512, LANE)

    def down(s_ref, w_ref, x_ref, o_ref):
        acc = _dot(s_ref[0], w_ref[0, 0])
        for j in range(1, N_CHIPS):
            acc = acc + _dot(s_ref[j], w_ref[j, 0])
        o_ref[...] = x_ref[...] + 0.5 * acc

    xo = _pcall(
        down, name="ffn_down", grid=(n_tok // tm, D_MODEL // tn),
        in_specs=[pl.BlockSpec((N_CHIPS, tm, fq), lambda i, n: (0, i, 0)),
                  pl.BlockSpec((N_CHIPS, 1, fq, tn), lambda i, n: (0, d, 0, n)),
                  pl.BlockSpec((tm, tn), lambda i, n: (i, n))],
        out_specs=pl.BlockSpec((tm, tn), lambda i, n: (i, n)), out_shape=jax.ShapeDtypeStruct((n_tok, D_MODEL), F32),
        compiler_params=_cparams(("parallel", "parallel"), 4 * N_CHIPS * (tm + tn) * fq * 2 + 6 * tm * tn * 4),
    )(s, wd, x)
    return xo, (x, h, a, b, s)


def _ffn_bwd(dy, dy16, saved, g, wg, wu, wd, d, into):
    x, h, a, b, s = saved
    n_tok = x.shape[0]
    fq = wg.shape[-1]
    tm = _tile(n_tok, 512, 16)

    def dact(dy_ref, w_ref, a_ref, b_ref, da_ref, db_ref):
        ds = 0.5 * _dot(dy_ref[...], w_ref[0, 0], _NT)
        av, bv = a_ref[0].astype(F32), b_ref[0].astype(F32)
        sg = _sigmoid(av)
        da_ref[0] = (ds * bv * sg * (1.0 + av * (1.0 - sg))).astype(BF16)
        db_ref[0] = (ds * av * sg).astype(BF16)

    aspec = pl.BlockSpec((1, tm, fq), lambda i, j: (j, i, 0))
    act = jax.ShapeDtypeStruct((N_CHIPS, n_tok, fq), BF16)
    da, db = _pcall(
        dact, name="ffn_dact", grid=(n_tok // tm, N_CHIPS),
        in_specs=[pl.BlockSpec((tm, D_MODEL), lambda i, j: (i, 0)),
                  pl.BlockSpec((1, 1, fq, D_MODEL), lambda i, j: (j, d, 0, 0)), aspec, aspec],
        out_specs=[aspec, aspec], out_shape=[act, act],
        compiler_params=_cparams(("parallel", "parallel"), 4 * D_MODEL * fq * 2 + 16 * tm * fq * 4),
    )(dy16, wd, a, b)

    any_spec = pl.BlockSpec(memory_space=pl.ANY)
    alias = {} if into is None else {2: 0}
    tn = _tile(D_MODEL, 512, LANE)

    def dw_down(s_ref, dy_ref, *rest):
        rest[-1][0, 0] = 0.5 * _dot(s_ref[0], dy_ref[...], _TN)

    dwd = _pcall(
        dw_down, name="ffn_dw_down", grid=(N_CHIPS, D_MODEL // tn),
        in_specs=[pl.BlockSpec((1, n_tok, fq), lambda j, n: (j, 0, 0)), pl.BlockSpec((n_tok, tn), lambda j, n: (0, n))]
        + ([] if into is None else [any_spec]),
        out_specs=pl.BlockSpec((1, 1, fq, tn), lambda j, n: (j, d, 0, n)),
        out_shape=jax.ShapeDtypeStruct((N_CHIPS, DEPTH, fq, D_MODEL), F32), input_output_aliases=alias,
        compiler_params=_cparams(("parallel", "parallel"), 6 * n_tok * (fq + tn) * 2 + 4 * fq * tn * 4),
    )(s, dy16, *([] if into is None else [into[2]]))

    def dw_up_call(t, prev):
        def dw_up(h_ref, t_ref, *rest):
            rest[-1][0, 0] = _dot(h_ref[...], t_ref[0], _TN)

        return _pcall(
            dw_up, name="ffn_dw_up", grid=(N_CHIPS, D_MODEL // tn),
            in_specs=[pl.BlockSpec((n_tok, tn), lambda j, m: (0, m)), pl.BlockSpec((1, n_tok, fq), lambda j, m: (j, 0, 0))]
            + ([] if prev is None else [any_spec]),
            out_specs=pl.BlockSpec((1, 1, tn, fq), lambda j, m: (j, d, m, 0)),
            out_shape=jax.ShapeDtypeStruct((N_CHIPS, DEPTH, D_MODEL, fq), F32), input_output_aliases=alias,
            compiler_params=_cparams(("parallel", "parallel"), 6 * n_tok * (fq + tn) * 2 + 4 * fq * tn * 4),
        )(h, t, *([] if prev is None else [prev]))

    dwg = dw_up_call(da, None if into is None else into[0])
    dwu = dw_up_call(db, None if into is None else into[1])
    tr = _tile(n_tok, 256, 16)

    def dh_norm(da_ref, db_ref, wg_ref, wu_ref, dy_ref, x_ref, g_ref, dx_ref, dx16_ref, dg_ref):
        dh = _dot(da_ref[0], wg_ref[0, 0], _NT) + _dot(db_ref[0], wu_ref[0, 0], _NT)
        for j in range(1, N_CHIPS):
            dh = dh + _dot(da_ref[j], wg_ref[j, 0], _NT) + _dot(db_ref[j], wu_ref[j, 0], _NT)
        dx, dg = _rms_bwd(x_ref[...], g_ref[...], dh)
        dx = dy_ref[...] + dx
        dx_ref[...] = dx
        dx16_ref[...] = dx.astype(BF16)

        @pl.when(pl.program_id(0) == 0)
        def _():
            dg_ref[...] = jnp.zeros(dg_ref.shape, F32)

        dg_ref[...] += dg

    tspec = pl.BlockSpec((N_CHIPS, tr, fq), lambda i: (0, i, 0))
    wspec = pl.BlockSpec((N_CHIPS, 1, D_MODEL, fq), lambda i: (0, d, 0, 0))
    rspec = pl.BlockSpec((tr, D_MODEL), lambda i: (i, 0))
    vspec = pl.BlockSpec((1, D_MODEL), lambda i: (0, 0))
    dx, dx16, dg = _pcall(
        dh_norm, name="ffn_dh_norm", grid=(n_tok // tr,),
        in_specs=[tspec, tspec, wspec, wspec, rspec, rspec, vspec], out_specs=[rspec, rspec, vspec],
        out_shape=[jax.ShapeDtypeStruct((n_tok, D_MODEL), F32), jax.ShapeDtypeStruct((n_tok, D_MODEL), BF16),
                   jax.ShapeDtypeStruct((1, D_MODEL), F32)],
        compiler_params=_cparams(("arbitrary",), 4 * N_CHIPS * D_MODEL * fq * 2 + 4 * N_CHIPS * tr * fq * 2
                                 + 16 * tr * D_MODEL * 4),
    )(da, db, wg, wu, dy, x, g)
    return dx, dx16, dg[0], (dwg, dwu, dwd)


def _alpha_pads(w_alpha):
    z = jnp.zeros((LANE - 2 * GLA_R, GLA_W), F32)
    wf = jnp.concatenate([w_alpha[0], jnp.zeros((GLA_R, GLA_W), F32), z], axis=0)
    wb = jnp.concatenate([jnp.zeros((GLA_R, GLA_W), F32), w_alpha[1], z], axis=0)
    return wf, wb


def _dup_heads(t):
    low = lax.broadcasted_iota(jnp.int32, t.shape, 1) < AT_D
    h0, h1 = jnp.where(low, t, 0.0), jnp.where(low, 0.0, t)
    return jnp.concatenate([h0 + pltpu.roll(h0, AT_D, 1), h1 + pltpu.roll(h1, AT_D, 1)], axis=1)


def _fold_heads(t):
    low = lax.broadcasted_iota(jnp.int32, (t.shape[0], LANE), 1) < AT_D
    g0, g1 = t[:, :LANE], t[:, LANE:]
    return jnp.where(low, g0 + pltpu.roll(g0, AT_D, 1), g1 + pltpu.roll(g1, AT_D, 1))


def _mixer_fwd(x, w):
    n_tok = x.shape[0]
    h = _rowwise("mix_norm", lambda xv, gv: _rms(xv, gv), [x], [w['mix_norm']], [(D_MODEL, BF16)])[0]
    z = _mm("mix_in", h, w['w_in_p'])
    gp = _mm("mix_gate", h, w['w_merge_gate'], out_dtype=BF16)
    u = _cw(z, 0, S5_W)

    xs, yy = [], None
    for d, rev in ((0, False), (1, True)):
        a_re, a_im, wb, wc = w['s5'][d]
        bu = _mm("s5_bu", u, wb)
        xs.append(_scan("s5_scan_rev" if rev else "s5_scan_fwd", bu, a_re, a_im, rev=rev))
        yy = _mm("s5_y", xs[d], wc, add=yy)

    def s5_post(yv, uv, dv, wglu):
        y0 = yv + dv * uv
        y1 = _gelu(y0)
        t = _dot(y1.astype(BF16), wglu)
        return y0, t, y1 * _sigmoid(t)

    y0, t_glu, y_s5 = _rowwise("s5_post", s5_post, [yy, u], [w['s5_d'], w['s5_w_glu']],
                               [(S5_W, F32), (S5_W, F32), (S5_W, BF16)])

    wa_f, wa_b = _alpha_pads(w['gla_w_alpha'])
    ba_f, ba_b = w['gla_b_alpha'][0:1], w['gla_b_alpha'][1:2]
    zz = _cw(z, ZC_Z, LANE)

    def gla_prep(zv, waf, wab, bf, bb):
        z16 = zv.astype(BF16)
        lf = _dot(z16, waf.astype(BF16)) + bf
        lb = _dot(z16, wab.astype(BF16)) + bb
        return _log_sigmoid(lf) / GLA_TAU, _log_sigmoid(lb) / GLA_TAU

    la_f, la_b = _rowwise("gla_prep", gla_prep, [zz], [wa_f, wa_b, ba_f, ba_b], [(GLA_W, F32), (GLA_W, F32)])
    o_f, s_f = _gla_fwd("gla_fwd", z, la_f, rev=False)
    o_b, s_b = _gla_fwd("gla_rev", z, la_b, rev=True)
    gate = _cw(z, 4, GLA_W)

    def gla_post(of, ob, gv, gn):
        o = of + ob
        outs = []
        for hh in range(GLA_H):
            sl = slice(hh * GLA_D, (hh + 1) * GLA_D)
            gh = gv[:, sl]
            outs.append(_rms(o[:, sl], gn) * (gh * _sigmoid(gh)))
        return jnp.concatenate(outs, axis=1)

    y_gla = _rowwise("gla_post", gla_post, [o_f, o_b, gate], [w['gla_norm']], [(GLA_W, BF16)])[0]

    cq, sq, mq, rq = _head_consts(n_tok, AT_QH)
    ck, sk, mk, rk = _head_consts(n_tok, AT_KH)
    gq = jnp.tile(w['attn_q_norm'], (1, AT_QH))
    gk = jnp.tile(w['attn_k_norm'], (1, AT_KH))

    def rope_fwd(v, gain, cos, sin, mean, rot, scale):
        r = lax.rsqrt(_dot(v * v, mean, precision=HI) + EPS)
        vn = v * r * gain
        return (vn * cos + _dot(vn, rot, precision=HI) * sin) * scale

    def attn_prep(qv, kv, vv, cqv, sqv, ckv, skv, gqv, gkv, mqv, rqv, mkv, rkv):
        kr = rope_fwd(kv, gkv, ckv, skv, mkv, rkv, 1.0)
        return rope_fwd(qv, gqv, cqv, sqv, mqv, rqv, AT_D ** -0.5), _dup_heads(kr), _dup_heads(vv)

    q_raw, k_raw, v_raw = _cw(z, 5, AT_W), _cw(z, ZC_AK, AT_KW), _cw(z, ZC_AV, AT_KW)
    q16, kd, vd = _rowwise("attn_prep", attn_prep, [q_raw, k_raw, v_raw, cq, sq, ck, sk], [gq, gk, mq, rq, mk, rk],
                           [(AT_W, BF16), (2 * LANE, BF16), (2 * LANE, BF16)])
    y_at, lse = _attn_fwd("attn_fwd", q16, kd, vd)

    p0 = _mm("mix_branch", y_s5, w['w_branch_s5'], out_dtype=BF16)
    p1 = _mm("mix_branch", y_gla, w['w_branch_gla'], out_dtype=BF16)
    p2 = _mm("mix_branch_f32", y_at, w['w_branch_attn'], out_dtype=BF16)

    def merge(gpv, a0, a1, a2, bias):
        g = _sigmoid(gpv.astype(F32) + bias)
        return (g[:, :D_MODEL] * a0.astype(F32) + g[:, D_MODEL:2 * D_MODEL] * a1.astype(F32)
                + g[:, 2 * D_MODEL:] * a2.astype(F32))

    merged = _rowwise("mix_merge", merge, [gp, p0, p1, p2], [w['b_merge_gate']], [(D_MODEL, BF16)])[0]
    xo = _mm("mix_out", merged, w['w_out'], add=x)
    saved = dict(x=x, h=h, z=z, gp=gp, xs=xs, y0=y0, t_glu=t_glu, y_s5=y_s5, la=(la_f, la_b), o=(o_f, o_b), s=(s_f, s_b),
                 y_gla=y_gla, q16=q16, kd=kd, vd=vd, lse=lse, y_at=y_at, p=(p0, p1, p2), merged=merged)
    return xo, saved


def _mixer_bwd(dy, dy16, sv, w, s5_raw):
    z, h = sv['z'], sv['h']
    n_tok = z.shape[0]
    grads = {}
    dmerged = _mm("mix_dmerged", dy16, w['w_out'], tb=True, out_dtype=BF16)
    grads['w_out'] = _mm("mix_dwout", sv['merged'], dy16, ta=True)

    def dmerge(dm, gpv, a0, a1, a2, bias):
        dm = dm.astype(F32)
        g = _sigmoid(gpv.astype(F32) + bias)
        ps = (a0.astype(F32), a1.astype(F32), a2.astype(F32))
        dps, dgs = [], []
        for k in range(3):
            gk = g[:, k * D_MODEL:(k + 1) * D_MODEL]
            dps.append(dm * gk)
            dgs.append(dm * ps[k] * gk * (1.0 - gk))
        dgp = jnp.concatenate(dgs, axis=1)
        return dps[0], dps[1], dps[2], dgp, jnp.sum(dgp, axis=0, keepdims=True)

    dp0, dp1, dp2, dgp, dbm = _rowwise("mix_dmerge", dmerge, [dmerged, sv['gp'], *sv['p']], [w['b_merge_gate']],
                                       [(D_MODEL, BF16)] * 3 + [(3 * D_MODEL, BF16)], accs=[(1, 3 * D_MODEL)])
    grads['b_merge_gate'] = dbm[0]
    grads['w_branch_s5'] = _mm("mix_dbranch", sv['y_s5'], dp0, ta=True)
    grads['w_branch_gla'] = _mm("mix_dbranch", sv['y_gla'], dp1, ta=True)
    grads['w_branch_attn'] = _mm("mix_dbranch_f32", sv['y_at'], dp2, ta=True)
    dy_s5 = _mm("mix_dy", dp0, w['w_branch_s5'], tb=True)
    dy_gla = _mm("mix_dy", dp1, w['w_branch_gla'], tb=True)
    dy_at = _mm("mix_dy", dp2, w['w_branch_attn'], tb=True)
    grads['w_merge_gate'] = _mm("mix_dwgate", h, dgp, ta=True)
    dh = _mm("mix_dh_gate", dgp, w['w_merge_gate'], tb=True)
    u = _cw(z, 0, S5_W)

    def s5_post_bwd(dyv, y0, t, uv, dv, wglu):
        sg = _sigmoid(t)
        y1 = _gelu(y0)
        dt = dyv * y1 * sg * (1.0 - sg)
        dy1 = dyv * sg + _dot(dt.astype(BF16), wglu, _NT)
        dy0 = dy1 * _gelu_grad(y0)
        return dy0, dt, y1, dy0 * dv, jnp.sum(dy0 * uv, axis=0, keepdims=True)

    dy0, dt_glu, y1, du, dd = _rowwise("s5_post_bwd", s5_post_bwd, [dy_s5, sv['y0'], sv['t_glu'], u],
                                       [w['s5_d'], w['s5_w_glu']],
                                       [(S5_W, BF16), (S5_W, BF16), (S5_W, BF16), (S5_W, F32)], accs=[(1, S5_W)])
    grads['s5_d'] = dd[0]
    grads['s5_w_glu'] = _mm("s5_dwglu", y1, dt_glu, ta=True)
    s5_cot = []
    for d, rev in ((0, False), (1, True)):
        a_re, a_im, wb, wc = w['s5'][d]
        gx = _mm("s5_gx", dy0, wc, tb=True)
        gs, da = _scan("s5_adj_fwd" if rev else "s5_adj_rev", gx, a_re, -a_im, rev=not rev, xin=sv['xs'][d])
        dwc = _mm("s5_dwc", sv['xs'][d], dy0, ta=True)
        dwb = _mm("s5_dwb", u, gs, ta=True)
        du = _mm("s5_du", gs, wb, tb=True, add=du)
        da = jnp.sum(da, axis=0, keepdims=True)
        s5_cot.append((da[:, :S5_N], da[:, S5_N:], dwb, dwc))
    for d in range(2):
        _, pull = jax.vjp(_s5_prep, *s5_raw[d])
        s5_cot[d] = pull(s5_cot[d])
    for k, nm in enumerate(('s5_lambda_re', 's5_lambda_im', 's5_log_dt', 's5_b_re', 's5_b_im', 's5_c_re', 's5_c_im')):
        grads[nm] = jnp.stack([s5_cot[0][k], s5_cot[1][k]])

    gate = _cw(z, 4, GLA_W)

    def gla_post_bwd(dyv, of, ob, gv, gn):
        o = of + ob
        dos, dgates = [], []
        dgn = jnp.zeros((1, GLA_D), F32)
        for hh in range(GLA_H):
            sl = slice(hh * GLA_D, (hh + 1) * GLA_D)
            gh, dyh = gv[:, sl], dyv[:, sl]
            sg = _sigmoid(gh)
            dgates.append(dyh * _rms(o[:, sl], gn) * sg * (1.0 + gh * (1.0 - sg)))
            dx, dg = _rms_bwd(o[:, sl], gn, dyh * gh * sg)
            dos.append(dx)
            dgn = dgn + dg
        return jnp.concatenate(dos, axis=1), jnp.concatenate(dgates, axis=1), dgn

    do, dgate, dgn = _rowwise("gla_post_bwd", gla_post_bwd, [dy_gla, *sv['o'], gate], [w['gla_norm']],
                              [(GLA_W, F32), (GLA_W, F32)], accs=[(1, GLA_D)])
    grads['gla_norm'] = dgn[0]
    dq_f, dk_f, dv_f, dla_f = _gla_bwd("gla_bwd_fwd", z, sv['la'][0], sv['s'][0], do, rev=False)
    dq_b, dk_b, dv_b, dla_b = _gla_bwd("gla_bwd_rev", z, sv['la'][1], sv['s'][1], do, rev=True)
    wa_f, wa_b = _alpha_pads(w['gla_w_alpha'])
    ba_f, ba_b = w['gla_b_alpha'][0:1], w['gla_b_alpha'][1:2]
    zz = _cw(z, ZC_Z, LANE)

    def gla_prep_bwd(dlf, dlb, zv, waf, wab, bf, bb):
        z16 = zv.astype(BF16)
        waf16, wab16 = waf.astype(BF16), wab.astype(BF16)
        xf = _dot(z16, waf16) + bf
        xb = _dot(z16, wab16) + bb
        df = dlf * (_sigmoid(-xf) / GLA_TAU)
        db = dlb * (_sigmoid(-xb) / GLA_TAU)
        dz = _dot(df.astype(BF16), waf16, _NT) + _dot(db.astype(BF16), wab16, _NT)
        return df, db, dz, jnp.sum(df, axis=0, keepdims=True), jnp.sum(db, axis=0, keepdims=True)

    dlog_f, dlog_b, dzz, dba_f, dba_b = _rowwise(
        "gla_prep_bwd", gla_prep_bwd, [dla_f, dla_b, zz], [wa_f, wa_b, ba_f, ba_b],
        [(GLA_W, BF16), (GLA_W, BF16), (LANE, F32)], accs=[(1, GLA_W), (1, GLA_W)])
    dwa_f = _mm("gla_dwa", zz, dlog_f, ta=True)
    dwa_b = _mm("gla_dwa", zz, dlog_b, ta=True)
    grads['gla_w_alpha'] = jnp.stack([dwa_f[:GLA_R], dwa_b[GLA_R:2 * GLA_R]])
    grads['gla_b_alpha'] = jnp.concatenate([dba_f, dba_b], axis=0)

    dq_r, dkd, dvd = _attn_bwd("attn_bwd", sv['q16'], sv['kd'], sv['vd'], sv['y_at'], sv['lse'], dy_at)
    cq, sq, mq, rq = _head_consts(n_tok, AT_QH)
    ck, sk, mk, rk = _head_consts(n_tok, AT_KH)
    gq = jnp.tile(w['attn_q_norm'], (1, AT_QH))
    gk = jnp.tile(w['attn_k_norm'], (1, AT_KH))

    def rope_bwd(dr, v, gain, cos, sin, mean, rot, scale):
        dr = dr * scale
        dvn = dr * cos + _dot(dr * sin, rot, _NT, precision=HI)
        r = lax.rsqrt(_dot(v * v, mean, precision=HI) + EPS)
        vh = v * r
        dg = jnp.sum(dvn * vh, axis=0, keepdims=True)
        dvh = dvn * gain
        return r * (dvh - vh * _dot(dvh * vh, mean, precision=HI)), dg

    def attn_prep_bwd(dq, dk2, dv2, qv, kv, cqv, sqv, ckv, skv, gqv, gkv, mqv, rqv, mkv, rkv):
        dqo, dgq = rope_bwd(dq, qv, gqv, cqv, sqv, mqv, rqv, AT_D ** -0.5)
        dko, dgk = rope_bwd(_fold_heads(dk2), kv, gkv, ckv, skv, mkv, rkv, 1.0)
        return dqo, dko, _fold_heads(dv2), dgq, dgk

    q_raw, k_raw = _cw(z, 5, AT_W), _cw(z, ZC_AK, AT_KW)
    daq, dak, dav, dgq, dgk = _rowwise("attn_prep_bwd", attn_prep_bwd, [dq_r, dkd, dvd, q_raw, k_raw, cq, sq, ck, sk],
                                       [gq, gk, mq, rq, mk, rk], [(AT_W, F32), (AT_KW, F32), (AT_KW, F32)],
                                       accs=[(1, AT_W), (1, AT_KW)])
    grads['attn_q_norm'] = dgq.reshape(AT_QH, AT_D).sum(axis=0)
    grads['attn_k_norm'] = dgk.reshape(AT_KH, AT_D).sum(axis=0)

    def assemble(duv, qf, qb, kf, kb, vf, vb, dg, aq, ak, av, zzv):
        return jnp.concatenate([duv, qf + qb, kf + kb, vf + vb, dg, aq, ak, av, zzv], axis=1)

    dz = _rowwise("mix_dz", assemble, [du, dq_f, dq_b, dk_f, dk_b, dv_f, dv_b, dgate, daq, dak, dav, dzz], [],
                  [(ZP_W, BF16)])[0]
    grads['w_in'] = _unpad_w_in(_mm("mix_dwin", h, dz, ta=True))
    dh = _mm("mix_dh_in", dz, w['w_in_p'], tb=True, add=dh)

    def dnorm(dyv, xv, dhv, gv):
        dx, dg = _rms_bwd(xv, gv, dhv)
        dx = dyv + dx
        return dx, dx, dg

    dx, dx16, dg = _rowwise("mix_dnorm", dnorm, [dy, sv['x'], dh], [w['mix_norm']], [(D_MODEL, F32), (D_MODEL, BF16)],
                            accs=[(1, D_MODEL)])
    grads['mix_norm'] = dg[0]
    return dx, dx16, grads


FFN_NAMES = [p + k for p in ('ffn1_', 'ffn2_') for k in ('w_gate', 'w_up', 'w_down')]
MISC_SHARDED = [n for n in SHARDED if n not in FFN_NAMES]
LARGE_SHARDED = FFN_NAMES + ['w_in', 'w_merge_gate']


def _layer_weights(ws, rep, i):
    w = {k: _from_shards(ws[k][:, i], SHARD_AXIS[k] - 1) for k in MISC_SHARDED}
    for k in ('ffn1_norm', 'mix_norm', 'ffn2_norm', 's5_d', 'gla_norm', 'attn_q_norm', 'attn_k_norm', 'b_merge_gate'):
        w[k] = rep[k][i].reshape(1, -1)
    w['w_in_p'] = _pad_w_in(w['w_in'])
    s5_raw = [tuple(rep[k][i, d] for k in ('s5_lambda_re', 's5_lambda_im', 's5_log_dt', 's5_b_re', 's5_b_im',
                                            's5_c_re', 's5_c_im')) for d in range(2)]
    w['s5'] = [_s5_prep(*s5_raw[d]) for d in range(2)]
    return w, s5_raw


def _device_step(x, target, ws, rep):
    lw = [_layer_weights(ws, rep, i) for i in range(DEPTH)]
    ffn = {p: (ws[p + 'w_gate'], ws[p + 'w_up'], ws[p + 'w_down']) for p in ('ffn1_', 'ffn2_')}
    saved = []
    for i in range(DEPTH):
        w, _ = lw[i]
        x, s1 = _ffn_fwd(x, w['ffn1_norm'], *ffn['ffn1_'], i)
        x, s2 = _mixer_fwd(x, w)
        x, s3 = _ffn_fwd(x, w['ffn2_norm'], *ffn['ffn2_'], i)
        saved.append((s1, s2, s3))
    gfin = rep['final_norm'].reshape(1, -1)

    def head(xv, tv, gv):
        e = _rms(xv, gv) - tv
        dx, dg = _rms_bwd(xv, gv, e * (1.0 / D_MODEL))
        part = jnp.sum(e * e, axis=0, keepdims=True)
        return dx, dx, dg, part

    dx, dx16, dgfin, part = _rowwise("loss_head", head, [x, target], [gfin], [(D_MODEL, F32), (D_MODEL, BF16)],
                                     accs=[(1, D_MODEL), (1, D_MODEL)])
    loss = (0.5 / D_MODEL) * jnp.sum(part)
    per_layer = []
    dffn = {'ffn1_': None, 'ffn2_': None}
    for i in reversed(range(DEPTH)):
        w, s5_raw = lw[i]
        s1, s2, s3 = saved[i]
        dx, dx16, n3, dffn['ffn2_'] = _ffn_bwd(dx, dx16, s3, w['ffn2_norm'], *ffn['ffn2_'], i, dffn['ffn2_'])
        dx, dx16, g = _mixer_bwd(dx, dx16, s2, w, s5_raw)
        dx, dx16, n1, dffn['ffn1_'] = _ffn_bwd(dx, dx16, s1, w['ffn1_norm'], *ffn['ffn1_'], i, dffn['ffn1_'])
        g['ffn2_norm'], g['ffn1_norm'] = n3, n1
        per_layer.append(g)
    per_layer.reverse()
    stacked = {k: jnp.stack([per_layer[i][k] for i in range(DEPTH)]) for k in per_layer[0]}
    gs = {k: _to_shards(stacked[k], SHARD_AXIS[k]) for k in MISC_SHARDED}
    for p in ('ffn1_', 'ffn2_'):
        gs[p + 'w_gate'], gs[p + 'w_up'], gs[p + 'w_down'] = dffn[p]
    grep = {k: stacked[k] for k in REPLICATED if k != 'final_norm'}
    grep['final_norm'] = dgfin[0]
    return loss, dx, gs, grep


PACK_C = 1024
ROW_TILE = 512


def _rows_of(shape):
    return -(-int(np.prod(shape)) // PACK_C)


def _as_rows(t, lead=0):
    head = t.shape[:lead]
    flat = t.reshape(head + (-1,))
    rows = -(-flat.shape[-1] // PACK_C)
    pad = rows * PACK_C - flat.shape[-1]
    if pad:
        flat = jnp.pad(flat, [(0, 0)] * lead + [(0, pad)])
    return flat.reshape(head + (rows, PACK_C))


def _pack(items, total_rows, lead=0):
    parts = [_as_rows(t, lead) for t in items]
    used = sum(p.shape[lead] for p in parts)
    if total_rows > used:
        parts.append(jnp.zeros(parts[0].shape[:lead] + (total_rows - used, PACK_C), parts[0].dtype))
    return jnp.concatenate(parts, axis=lead)


def _unpack(rows, shapes, lead=0):
    out, r0 = [], 0
    head = rows.shape[:lead]
    for shp in shapes:
        n, size = _rows_of(shp), int(np.prod(shp))
        piece = lax.slice_in_dim(rows, r0, r0 + n, axis=lead).reshape(head + (n * PACK_C,))
        out.append(lax.slice_in_dim(piece, 0, size, axis=lead).reshape(head + tuple(shp)))
        r0 += n
    return out


def _round_up(n, m):
    return -(-n // m) * m


def _to_shards(g, axis):
    shp = g.shape
    g = g.reshape(shp[:axis] + (N_CHIPS, shp[axis] // N_CHIPS) + shp[axis + 1:])
    return jnp.moveaxis(g, axis, 0)


def _from_shards(s, axis):
    s = jnp.moveaxis(s, 0, axis)
    shp = s.shape
    return s.reshape(shp[:axis] + (shp[axis] * shp[axis + 1],) + shp[axis + 2:])


_HBM = pl.BlockSpec(memory_space=pltpu.HBM)


def _place():
    x, y, c = lax.axis_index("x"), lax.axis_index("y"), lax.axis_index("c")
    chips = [(1 - x, y), (x, 1 - y), (1 - x, 1 - y)]
    return x, y, c, chips


def _rcopy(src, dst, send, recv, k, dev):
    return pltpu.make_async_remote_copy(src_ref=src, dst_ref=dst, send_sem=send.at[k], recv_sem=recv.at[k],
                                        device_id=dev, device_id_type=MESH)


def _comm_call(name, body, arrays, out_shapes, n_sems, n_local):
    return pl.pallas_call(
        body, name=name, in_specs=[_HBM] * len(arrays), out_specs=[_HBM] * len(out_shapes), out_shape=out_shapes,
        scratch_shapes=[pltpu.SemaphoreType.DMA((n_sems,)), pltpu.SemaphoreType.DMA((n_sems,)),
                        pltpu.SemaphoreType.DMA((max(n_local, 1),))],
    )(*arrays)


def _ici_peers(x, y, c):
    x_first = c == 0
    first = (jnp.where(x_first, 1 - x, x), jnp.where(x_first, y, 1 - y))
    second = (jnp.where(x_first, x, 1 - x), jnp.where(x_first, 1 - y, y))
    return first, second


def _ag_chips(name, packs):
    n = len(packs)

    def body(*refs):
        p, o, (send, recv, _) = refs[:n], refs[n:2 * n], refs[2 * n:]
        x, y, c, _ = _place()
        (ax, ay), (bx, by) = _ici_peers(x, y, c)
        me, na, nb, nd = 2 * x + y, 2 * ax + ay, 2 * bx + by, 2 * (1 - x) + (1 - y)
        sib = (x, y, 1 - c)
        own = [_rcopy(p[a].at[c], o[a].at[me, c], send, recv, 6 * a, (ax, ay, c)) for a in range(n)]
        own += [_rcopy(p[a].at[c], o[a].at[me, c], send, recv, 6 * a + 1, (bx, by, c)) for a in range(n)]
        for cp in own:
            cp.start()
        passed = []

        def landed(a, chip, k, fwd):
            blk = o[a].at[chip, c]
            _rcopy(blk, blk, send, recv, 6 * a + k, sib).wait_recv()
            for sem, dev in fwd:
                cp = _rcopy(blk, blk, send, recv, 6 * a + sem, dev)
                cp.start()
                passed.append(cp)

        for a in range(n):
            landed(a, na, 0, [(2, (bx, by, c)), (3, sib)])
        for a in range(n):
            landed(a, nb, 1, [(4, sib)])
        for a in range(n):
            landed(a, nd, 2, [(5, sib)])
        for a in range(n):
            for k, chip in ((3, nb), (4, na), (5, nd)):
                blk = o[a].at[chip, 1 - c]
                _rcopy(blk, blk, send, recv, 6 * a + k, sib).wait_recv()
        for cp in own + passed:
            cp.wait_send()

    outs = _comm_call(name, body, packs, [jax.ShapeDtypeStruct((N_CHIPS,) + t.shape, t.dtype) for t in packs], 6 * n, 0)
    chip = 2 * lax.axis_index("x") + lax.axis_index("y")
    return [lax.dynamic_update_slice_in_dim(o, t[None], chip, axis=0) for o, t in zip(outs, packs)]


def _pair_swap(name, gs):
    n = len(gs)
    ns = gs[0].shape[0]

    def body(*refs):
        g, r, (send, recv, _) = refs[:n], refs[n:2 * n], refs[2 * n:]
        x, y, c, _ = _place()
        cps = [_rcopy(g[a].at[j, 1 - c], r[a].at[j], send, recv, ns * a + j, (x, y, 1 - c))
               for a in range(n) for j in range(ns)]
        for cp in cps:
            cp.start()
        for cp in cps:
            cp.wait_recv()
        for cp in cps:
            cp.wait_send()

    return _comm_call(name, body, gs, [jax.ShapeDtypeStruct((ns,) + t.shape[2:], t.dtype) for t in gs], ns * n, 0)


def _halve_first(name, parts):
    n = len(parts)

    def body(*refs):
        p, r, (send, recv, _) = refs[:n], refs[n:2 * n], refs[2 * n:]
        x, y, c, _ = _place()
        (ax, ay), _ = _ici_peers(x, y, c)
        cps = [_rcopy(p[a].at[jnp.where(c == 0, 2 * ax + k, 2 * k + ay)], r[a].at[k], send, recv, 2 * a + k, (ax, ay, c))
               for a in range(n) for k in range(2)]
        for cp in cps:
            cp.start()
        for cp in cps:
            cp.wait_recv()
        for cp in cps:
            cp.wait_send()

    return _comm_call(name, body, parts, [jax.ShapeDtypeStruct((2,) + t.shape[1:], t.dtype) for t in parts], 2 * n, 0)


def _halve_second(name, kept):
    n = len(kept)

    def body(*refs):
        p, r, (send, recv, _) = refs[:n], refs[n:2 * n], refs[2 * n:]
        x, y, c, _ = _place()
        _, (bx, by) = _ici_peers(x, y, c)
        cps = [_rcopy(p[a].at[jnp.where(c == 0, by, bx)], r[a], send, recv, a, (bx, by, c)) for a in range(n)]
        for cp in cps:
            cp.start()
        for cp in cps:
            cp.wait_recv()
        for cp in cps:
            cp.wait_send()

    return _comm_call(name, body, kept, [jax.ShapeDtypeStruct(t.shape[1:], t.dtype) for t in kept], n, 0)


def _add_selected(name, t, r, index, out_dtype):
    single = r.ndim == 2
    r3 = r[None] if single else r
    nk, rh, cc = r3.shape
    tl = _tile(rh, ROW_TILE, 16)

    def body(i_ref, t_ref, r_ref, o_ref):
        o_ref[...] = (t_ref[...].astype(F32) + r_ref[...].astype(F32)).astype(o_ref.dtype)

    out = _pcall(
        body, name=name,
        grid_spec=pltpu.PrefetchScalarGridSpec(
            num_scalar_prefetch=1, grid=(nk, rh // tl),
            in_specs=[pl.BlockSpec((1, tl, cc), lambda k, i, i_ref: (i_ref[k], i, 0)),
                      pl.BlockSpec((1, tl, cc), lambda k, i, i_ref: (k, i, 0))],
            out_specs=pl.BlockSpec((1, tl, cc), lambda k, i, i_ref: (k, i, 0))),
        out_shape=jax.ShapeDtypeStruct(r3.shape, out_dtype),
        compiler_params=_cparams(("parallel", "parallel"), 8 * tl * cc * 4),
    )(index, t, r3)
    return out[0] if single else out


def _pair_share(name, reds):
    n = len(reds)

    def body(*refs):
        r, o, (send, recv, _) = refs[:n], refs[n:2 * n], refs[2 * n:]
        x, y, c, _ = _place()
        cps = [_rcopy(r[a], o[a].at[c], send, recv, a, (x, y, 1 - c)) for a in range(n)]
        for cp in cps:
            cp.start()
        for a in range(n):
            blk = o[a].at[1 - c]
            _rcopy(blk, blk, send, recv, a, (x, y, 1 - c)).wait_recv()
        for cp in cps:
            cp.wait_send()

    outs = _comm_call(name, body, reds, [jax.ShapeDtypeStruct((2,) + t.shape, t.dtype) for t in reds], n, 0)
    core = lax.axis_index("c")
    return [lax.dynamic_update_slice_in_dim(o, t[None], core, axis=0) for o, t in zip(outs, reds)]


def _add_own_half(name, g, r):
    ns, _, rh, cc = g.shape
    tl = _tile(rh, ROW_TILE, 16)
    core = lax.axis_index("c").astype(jnp.int32).reshape(1)

    def body(c_ref, g_ref, r_ref, o_ref):
        o_ref[...] = (g_ref[0] + r_ref[...]).astype(o_ref.dtype)

    return _pcall(
        body, name=name,
        grid_spec=pltpu.PrefetchScalarGridSpec(
            num_scalar_prefetch=1, grid=(ns, rh // tl),
            in_specs=[pl.BlockSpec((1, 1, tl, cc), lambda j, i, c_ref: (j, c_ref[0], i, 0)),
                      pl.BlockSpec((1, tl, cc), lambda j, i, c_ref: (j, i, 0))],
            out_specs=pl.BlockSpec((1, tl, cc), lambda j, i, c_ref: (j, i, 0))),
        out_shape=jax.ShapeDtypeStruct(r.shape, BF16),
        compiler_params=_cparams(("parallel", "parallel"), 8 * tl * cc * 4),
    )(core, g, r)


def _reduce_scatter(gs):
    x, y, c = lax.axis_index("x"), lax.axis_index("y"), lax.axis_index("c")
    mine_first = jnp.stack([jnp.where(c == 0, 2 * x + k, 2 * k + y) for k in range(2)]).astype(jnp.int32)
    mine_second = jnp.where(c == 0, y, x).astype(jnp.int32).reshape(1)
    swapped = _pair_swap("rs_pair_swap", gs)
    parts = [_add_own_half("rs_add_pair", g, r) for g, r in zip(gs, swapped)]
    got = _halve_first("rs_halve_first", parts)
    kept = [_add_selected("rs_add_first", t, r, mine_first, BF16) for t, r in zip(parts, got)]
    got = _halve_second("rs_halve_second", kept)
    reds = [_add_selected("rs_add_second", t, r, mine_second, F32) for t, r in zip(kept, got)]
    return _pair_share("rs_pair_share", reds)


def _adamw(name, g, w, m, v):
    c1 = 1.0 - ADAM_B1 ** ADAM_STEP
    c2 = 1.0 - ADAM_B2 ** ADAM_STEP
    nb, rows, cc = g.shape
    tl = _tile(rows, ROW_TILE, SUBLANE)

    def body(g_ref, w_ref, m_ref, v_ref, d_ref, mo_ref, vo_ref):
        gv = g_ref[...]
        mn = ADAM_B1 * m_ref[...] + (1.0 - ADAM_B1) * gv
        vn = ADAM_B2 * v_ref[...] + (1.0 - ADAM_B2) * (gv * gv)
        d_ref[...] = -ADAM_LR * ((mn / c1) / (jnp.sqrt(vn / c2) + ADAM_EPS) + ADAM_WD * w_ref[...])
        mo_ref[...] = mn
        vo_ref[...] = vn

    spec = pl.BlockSpec((1, tl, cc), lambda b, i: (b, i, 0))
    return _pcall(
        body, name=name, grid=(nb, rows // tl), in_specs=[spec] * 4, out_specs=[spec] * 3,
        out_shape=[jax.ShapeDtypeStruct(g.shape, F32)] * 3,
        compiler_params=_cparams(("parallel", "parallel"), 16 * tl * cc * 4),
    )(g, w, m, v)


def _train_step(a):
    x, target = a['x'][0], a['loss_target'][0]

    names = BF16_GATHER + F32_GATHER
    got = _ag_chips("ag_weights", [a[n].astype(BF16) for n in BF16_GATHER] + [a[n] for n in F32_GATHER])
    ws = dict(zip(names, got))

    loss, gx, gs, grep = _device_step(x, target, ws, {n: a[n] for n in REPLICATED})
    loss = lax.psum(loss, ("x", "y", "c"))

    small = [n for n in MISC_SHARDED if n not in LARGE_SHARDED]
    small_shapes = [a[n].shape[1:] for n in small]
    rep_shapes = [a[n].shape for n in REPLICATED]
    small_rows = sum(_rows_of(s) for s in small_shapes)
    piece = _round_up(-(-sum(_rows_of(s) for s in rep_shapes) // (2 * N_CHIPS)), SUBLANE)
    rep_rows = 2 * N_CHIPS * piece
    rep_pack = _pack([grep[n] for n in REPLICATED], rep_rows).reshape(N_CHIPS, 2, piece, PACK_C)
    total = _round_up(small_rows + piece, ROW_TILE // 2)
    misc = _pack([gs[n] for n in small] + [rep_pack], total, lead=2)
    reds = _reduce_scatter([gs[n] for n in LARGE_SHARDED] + [misc])
    grad = dict(zip(LARGE_SHARDED, reds[:-1]))
    grad.update(zip(small, _unpack(reds[-1], small_shapes, lead=1)))
    rep_mine = reds[-1][:, small_rows:small_rows + piece]
    rep_all = _ag_chips("ag_replicated", [rep_mine])[0].reshape(1, rep_rows, PACK_C)

    outs = {}
    for n in SHARDED:
        shp = a[n].shape
        three_d = (shp[0], -1, shp[-1])
        res3 = _adamw("adamw", *(t.reshape(three_d) for t in (grad[n], a[n], a['m_' + n], a['v_' + n])))
        for kind, arr in zip(('grad', 'delta', 'new_m', 'new_v'), (grad[n], *res3)):
            outs[kind + '_' + n] = arr.reshape(shp)
    w, m, v = (_pack([a[p + n] for n in REPLICATED], rep_rows)[None] for p in ('', 'm_', 'v_'))
    res3 = _adamw("adamw_replicated", rep_all, w, m, v)
    for kind, arr in zip(('grad', 'delta', 'new_m', 'new_v'), (rep_all, *res3)):
        for n, t in zip(REPLICATED, _unpack(arr[0], rep_shapes)):
            outs[kind + '_' + n] = t
    res = [loss, gx[None]]
    for kind in ('grad', 'delta', 'new_m', 'new_v'):
        res += [outs[kind + '_' + n] for n in W_NAMES]
    return tuple(res)


def kernel(x, ffn1_norm, ffn1_w_gate, ffn1_w_up, ffn1_w_down, mix_norm, w_in, s5_lambda_re, s5_lambda_im, s5_log_dt, s5_b_re, s5_b_im, s5_c_re, s5_c_im, s5_d, s5_w_glu, gla_w_alpha, gla_b_alpha, gla_norm, attn_q_norm, attn_k_norm, w_branch_s5, w_branch_gla, w_branch_attn, w_merge_gate, b_merge_gate, w_out, ffn2_norm, ffn2_w_gate, ffn2_w_up, ffn2_w_down, final_norm, loss_target, m_ffn1_norm, m_ffn1_w_gate, m_ffn1_w_up, m_ffn1_w_down, m_mix_norm, m_w_in, m_s5_lambda_re, m_s5_lambda_im, m_s5_log_dt, m_s5_b_re, m_s5_b_im, m_s5_c_re, m_s5_c_im, m_s5_d, m_s5_w_glu, m_gla_w_alpha, m_gla_b_alpha, m_gla_norm, m_attn_q_norm, m_attn_k_norm, m_w_branch_s5, m_w_branch_gla, m_w_branch_attn, m_w_merge_gate, m_b_merge_gate, m_w_out, m_ffn2_norm, m_ffn2_w_gate, m_ffn2_w_up, m_ffn2_w_down, m_final_norm, v_ffn1_norm, v_ffn1_w_gate, v_ffn1_w_up, v_ffn1_w_down, v_mix_norm, v_w_in, v_s5_lambda_re, v_s5_lambda_im, v_s5_log_dt, v_s5_b_re, v_s5_b_im, v_s5_c_re, v_s5_c_im, v_s5_d, v_s5_w_glu, v_gla_w_alpha, v_gla_b_alpha, v_gla_norm, v_attn_q_norm, v_attn_k_norm, v_w_branch_s5, v_w_branch_gla, v_w_branch_attn, v_w_merge_gate, v_b_merge_gate, v_w_out, v_ffn2_norm, v_ffn2_w_gate, v_ffn2_w_up, v_ffn2_w_down, v_final_norm):
    return _train_step(dict(locals()))
```

```python
import functools
import math

import jax
import jax.numpy as jnp
import numpy as np
from jax import lax
from jax.experimental import pallas as pl
from jax.experimental.pallas import tpu as pltpu

F32 = jnp.float32
BF16 = jnp.bfloat16
HI = lax.Precision.HIGHEST

D_MODEL = 1024
D_FF = 2816
DEPTH = 2
EPS = 1e-6
S5_G, S5_H, S5_P = 32, 16, 64
S5_W = S5_G * S5_H
S5_N = S5_G * S5_P
S5_TG = 8
S5_TILES = S5_G // S5_TG
S5_TC, S5_TS = S5_TG * S5_H, S5_TG * S5_P
GLA_H, GLA_D = 4, 128
GLA_W = GLA_H * GLA_D
GLA_R = 16
GLA_TAU = 16.0
GLA_C = 64
AT_QH, AT_KH, AT_D = 8, 2, 64
AT_W = AT_QH * AT_D
AT_KW = AT_KH * AT_D
GRID_W = 64
ROPE_BASE = 10000.0
ZP_W = 3456
ZC_AK, ZC_AV, ZC_Z = 24, 25, 26

ADAM_LR, ADAM_B1, ADAM_B2, ADAM_EPS, ADAM_WD, ADAM_STEP = 0.001, 0.9, 0.999, 1e-08, 0.01, 10

VMEM_LIMIT_V7X = 56 * 1024 * 1024
LANE = 128
SUBLANE = 8

W_NAMES = ['ffn1_norm', 'ffn1_w_gate', 'ffn1_w_up', 'ffn1_w_down', 'mix_norm', 'w_in', 's5_lambda_re', 's5_lambda_im',
           's5_log_dt', 's5_b_re', 's5_b_im', 's5_c_re', 's5_c_im', 's5_d', 's5_w_glu', 'gla_w_alpha', 'gla_b_alpha',
           'gla_norm', 'attn_q_norm', 'attn_k_norm', 'w_branch_s5', 'w_branch_gla', 'w_branch_attn', 'w_merge_gate',
           'b_merge_gate', 'w_out', 'ffn2_norm', 'ffn2_w_gate', 'ffn2_w_up', 'ffn2_w_down', 'final_norm']
SHARD_AXIS = {'ffn1_w_gate': 2, 'ffn1_w_up': 2, 'ffn1_w_down': 1, 'w_in': 2, 's5_w_glu': 1, 'gla_w_alpha': 3,
              'gla_b_alpha': 2, 'w_branch_s5': 2, 'w_branch_gla': 2, 'w_branch_attn': 2, 'w_merge_gate': 2,
              'w_out': 1, 'ffn2_w_gate': 2, 'ffn2_w_up': 2, 'ffn2_w_down': 1}
F32_GATHER = ['gla_w_alpha', 'gla_b_alpha']
BF16_GATHER = [n for n in W_NAMES if n in SHARD_AXIS and n not in F32_GATHER]
SHARDED = [n for n in W_NAMES if n in SHARD_AXIS]
REPLICATED = [n for n in W_NAMES if n not in SHARD_AXIS]
N_CHIPS = 4
MESH = pl.DeviceIdType.MESH


def _tile(n, target, mult):
    best = None
    for t in range(mult, min(n, target) + 1, mult):
        if n % t == 0:
            best = t
    return best if best is not None else n


def _cparams(sem, vmem_bytes):
    limit = int(min(VMEM_LIMIT_V7X, max(32 * 1024 * 1024, vmem_bytes * 5 // 4)))
    return pltpu.CompilerParams(dimension_semantics=sem, vmem_limit_bytes=limit)


def _pcall(body, **kw):
    shapes = kw.pop('out_shape')
    many = isinstance(shapes, (list, tuple))
    pinned = [pltpu.HBM(t.shape, t.dtype) for t in (shapes if many else [shapes])]
    call = pl.pallas_call(body, out_shape=pinned if many else pinned[0], **kw)

    def run(*ops):
        return call(*[o if jnp.issubdtype(o.dtype, jnp.integer) else pltpu.with_memory_space_constraint(o, pltpu.HBM)
                      for o in ops])

    return run


def _cw(arr, cb, width):
    return (arr, cb, width)


def _win(a):
    if isinstance(a, tuple):
        return a
    return (a, 0, a.shape[-1])


def _nbytes(shape, dtype):
    return int(np.prod(shape)) * jnp.dtype(dtype).itemsize


def _rowwise(name, fn, rows, fulls, outs, accs=(), tl=256):
    rows = [_win(r) for r in rows]
    n_tok = rows[0][0].shape[0]
    tl = _tile(n_tok, tl, 16)
    nr, nf, no = len(rows), len(fulls), len(outs)

    def body(*refs):
        vals = [r[...] for r in refs[:nr + nf]]
        res = fn(*vals)
        if not isinstance(res, (tuple, list)):
            res = (res,)
        for k in range(no):
            o = refs[nr + nf + k]
            o[...] = res[k].astype(o.dtype)
        if accs:
            @pl.when(pl.program_id(0) == 0)
            def _():
                for k in range(len(accs)):
                    a = refs[nr + nf + no + k]
                    a[...] = jnp.zeros(a.shape, a.dtype)
            for k in range(len(accs)):
                a = refs[nr + nf + no + k]
                a[...] += res[no + k]

    in_specs = [pl.BlockSpec((tl, w), functools.partial(lambda i, cb: (i, cb), cb=cb)) for (_, cb, w) in rows]
    in_specs += [pl.BlockSpec(f.shape, functools.partial(lambda i, nd: (0,) * nd, nd=f.ndim)) for f in fulls]
    out_specs = [pl.BlockSpec((tl, c), lambda i: (i, 0)) for (c, _) in outs]
    out_specs += [pl.BlockSpec((r, c), lambda i: (0, 0)) for (r, c) in accs]
    out_shape = [jax.ShapeDtypeStruct((n_tok, c), dt) for (c, dt) in outs]
    out_shape += [jax.ShapeDtypeStruct((r, c), F32) for (r, c) in accs]
    vm = 2 * sum(_nbytes((tl, w), a.dtype) for (a, _, w) in rows)
    vm += 2 * sum(_nbytes(f.shape, f.dtype) for f in fulls)
    vm += 2 * sum(_nbytes((tl, c), dt) for (c, dt) in outs)
    vm += 8 * max([_nbytes((tl, w), F32) for (_, _, w) in rows] + [_nbytes((tl, c), F32) for (c, _) in outs])
    return _pcall(
        body, name=name, grid=(n_tok // tl,), in_specs=in_specs, out_specs=out_specs, out_shape=out_shape,
        compiler_params=_cparams(("arbitrary",) if accs else ("parallel",), vm),
    )(*[r[0] for r in rows], *fulls)


def _mm(name, a, b, *, ta=False, tb=False, out_dtype=F32, alpha=1.0, add=None, tm=512, tn=512):
    a_arr, a_cb, a_w = _win(a)
    b_arr, b_cb, b_w = _win(b)
    if ta:
        kdim, m = a_arr.shape[0], a_w
    else:
        m, kdim = a_arr.shape[0], a_w
    if tb:
        n, kb = b_arr.shape[0], b_w
    else:
        kb, n = b_arr.shape[0], b_w
    assert kdim == kb, (name, kdim, kb)
    tm = _tile(m, tm, LANE if ta else 16)
    tn = _tile(n, tn, LANE)
    if ta:
        a_spec = pl.BlockSpec((kdim, tm), lambda i, j: (0, a_cb * (a_w // tm) + i))
        a_blk = (kdim, tm)
    else:
        a_spec = pl.BlockSpec((tm, kdim), lambda i, j: (i, a_cb))
        a_blk = (tm, kdim)
    if tb:
        b_spec = pl.BlockSpec((tn, kdim), lambda i, j: (j, b_cb))
        b_blk = (tn, kdim)
    else:
        b_spec = pl.BlockSpec((kdim, tn), lambda i, j: (0, b_cb * (b_w // tn) + j))
        b_blk = (kdim, tn)
    dims = (((0 if ta else 1,), (1 if tb else 0,)), ((), ()))
    has_add = add is not None

    def body(*refs):
        a_ref, b_ref = refs[0], refs[1]
        o_ref = refs[-1]
        acc = lax.dot_general(a_ref[...].astype(BF16), b_ref[...].astype(BF16), dims, preferred_element_type=F32)
        if alpha != 1.0:
            acc = acc * alpha
        if has_add:
            acc = acc + refs[2][...].astype(F32)
        o_ref[...] = acc.astype(o_ref.dtype)

    in_specs = [a_spec, b_spec]
    ops = [a_arr, b_arr]
    if has_add:
        in_specs.append(pl.BlockSpec((tm, tn), lambda i, j: (i, j)))
        ops.append(add)
    vm = 2 * (_nbytes(a_blk, a_arr.dtype) + _nbytes(b_blk, b_arr.dtype) + _nbytes((tm, tn), out_dtype))
    vm += _nbytes(a_blk, BF16) + _nbytes(b_blk, BF16) + 3 * _nbytes((tm, tn), F32)
    return _pcall(
        body, name=name, grid=(m // tm, n // tn), in_specs=in_specs,
        out_specs=pl.BlockSpec((tm, tn), lambda i, j: (i, j)),
        out_shape=jax.ShapeDtypeStruct((m, n), out_dtype),
        compiler_params=_cparams(("parallel", "parallel"), vm),
    )(*ops)


def _bmm(name, grid, pairs, out_shape, o_blk, o_map, dims, add=None, out_dtype=F32):
    n = len(pairs)

    def body(*refs):
        acc = None
        for k in range(n):
            t = lax.dot_general(refs[2 * k][...].astype(BF16), refs[2 * k + 1][...].astype(BF16), dims,
                                preferred_element_type=F32)
            acc = t if acc is None else acc + t
        if add is not None:
            acc = acc + refs[2 * n][...].astype(F32)
        refs[-1][...] = acc.astype(out_dtype)

    in_specs, ops, vm = [], [], 3 * _nbytes(o_blk, F32)
    for a, a_blk, a_map, b, b_blk, b_map in pairs:
        in_specs += [pl.BlockSpec(a_blk, a_map), pl.BlockSpec(b_blk, b_map)]
        ops += [a, b]
        vm += 3 * (_nbytes(a_blk, a.dtype) + _nbytes(b_blk, b.dtype))
    if add is not None:
        in_specs.append(pl.BlockSpec(o_blk, o_map))
        ops.append(add)
    return _pcall(
        body, name=name, grid=grid, in_specs=in_specs, out_specs=pl.BlockSpec(o_blk, o_map),
        out_shape=jax.ShapeDtypeStruct(out_shape, out_dtype),
        compiler_params=_cparams(("parallel",) * len(grid), vm),
    )(*ops)


def _sigmoid(x):
    return 1.0 / (1.0 + jnp.exp(-x))


def _rms(x, gain):
    r = lax.rsqrt(jnp.mean(x * x, axis=-1, keepdims=True) + EPS)
    return x * r * gain


def _rms_bwd(x, gain, dh):
    r = lax.rsqrt(jnp.mean(x * x, axis=-1, keepdims=True) + EPS)
    xh = x * r
    dg = jnp.sum(dh * xh, axis=0, keepdims=True)
    dxh = dh * gain
    dx = r * (dxh - xh * jnp.mean(dxh * xh, axis=-1, keepdims=True))
    return dx, dg


_GELU_C = math.sqrt(2.0 / math.pi)


def _gelu(y):
    return 0.5 * y * (1.0 + jnp.tanh(_GELU_C * (y + 0.044715 * y * y * y)))


def _gelu_grad(y):
    th = jnp.tanh(_GELU_C * (y + 0.044715 * y * y * y))
    return 0.5 * (1.0 + th) + 0.5 * y * (1.0 - th * th) * _GELU_C * (1.0 + 3.0 * 0.044715 * y * y)


def _log_sigmoid(x):
    return jnp.minimum(x, 0.0) - jnp.log(1.0 + jnp.exp(-jnp.abs(x)))


def _dot(a, b, dims=(((1,), (0,)), ((), ())), precision=None):
    return lax.dot_general(a, b, dims, preferred_element_type=F32, precision=precision)


_NN = (((1,), (0,)), ((), ()))
_NT = (((1,), (1,)), ((), ()))
_TN = (((0,), (0,)), ((), ()))


def _s5_in(z, wb):
    n_tok = z.shape[0]
    tm = _tile(n_tok, 512, 16)
    return _bmm("s5_bu", (n_tok // tm, 2 * S5_TILES),
                [(z, (tm, S5_TC), lambda i, j: (i, j % S5_TILES), wb, (S5_TC, S5_TS), lambda i, j: (0, j))],
                (n_tok, 2 * S5_N), (tm, S5_TS), lambda i, j: (i, j), _NN)


def _s5_both(t, w, t_blk, w_blk, w_rows):
    wmap = (lambda off: (lambda i, k: (off + k, 0))) if w_rows else (lambda off: (lambda i, k: (0, off + k)))
    return [(t, t_blk, (lambda off: (lambda i, k: (i, off + k)))(off), w, w_blk, wmap(off)) for off in (0, S5_TILES)]


def _s5_out(xs, wc, add):
    n_tok = xs.shape[0]
    tm = _tile(n_tok, 512, 16)
    return _bmm("s5_y", (n_tok // tm, S5_TILES), _s5_both(xs, wc, (tm, S5_TS), (S5_TS, S5_TC), True),
                (n_tok, S5_W), (tm, S5_TC), lambda i, k: (i, k), _NN, add=add)


def _s5_out_t(dy, wc):
    n_tok = dy.shape[0]
    tm = _tile(n_tok, 512, 16)
    return _bmm("s5_gx", (n_tok // tm, 2 * S5_TILES),
                [(dy, (tm, S5_TC), lambda i, j: (i, j % S5_TILES), wc, (S5_TS, S5_TC), lambda i, j: (j, 0))],
                (n_tok, 2 * S5_N), (tm, S5_TS), lambda i, j: (i, j), _NT)


def _s5_dwc(xs, dy):
    n_tok = xs.shape[0]
    return _bmm("s5_dwc", (2 * S5_TILES,),
                [(xs, (n_tok, S5_TS), lambda j: (0, j), dy, (n_tok, S5_TC), lambda j: (0, j % S5_TILES))],
                (2 * S5_N, S5_TC), (S5_TS, S5_TC), lambda j: (j, 0), _TN)


def _s5_dwb(z, gs):
    n_tok = z.shape[0]
    return _bmm("s5_dwb", (2 * S5_TILES,),
                [(z, (n_tok, S5_TC), lambda j: (0, j % S5_TILES), gs, (n_tok, S5_TS), lambda j: (0, j))],
                (S5_TC, 2 * S5_N), (S5_TC, S5_TS), lambda j: (0, j), _TN)


def _s5_in_t(gs, wb, add):
    n_tok = gs.shape[0]
    tm = _tile(n_tok, 512, 16)
    return _bmm("s5_du", (n_tok // tm, S5_TILES), _s5_both(gs, wb, (tm, S5_TS), (S5_TC, S5_TS), False),
                (n_tok, S5_W), (tm, S5_TC), lambda i, k: (i, k), _NT, add=add)


def _scan(name, b, a_re, a_im, *, rev, xin=None, tb=256, lc=512):
    n_tok = b.shape[0]
    tb = _tile(n_tok, tb, 16)
    nt, nb, nc = n_tok // tb, tb // SUBLANE, S5_N // lc
    with_da = xin is not None

    def body(*refs):
        if with_da:
            b_ref, are_ref, aim_ref, x_ref, o_ref, da_ref, pw, last, of32, xf32 = refs
        else:
            b_ref, are_ref, aim_ref, o_ref, pw, last, of32 = refs
        t = pl.program_id(0)
        row = lax.broadcasted_iota(jnp.int32, (SUBLANE, lc), 0)

        def cmul(ar, ai, xr, xi):
            return ar * xr - ai * xi, ar * xi + ai * xr

        def hs(c, xr, xi):
            cs = pl.ds(c * lc, lc)
            for k, slot in ((1, 0), (2, 2), (4, 4)):
                ar, ai = pw[slot, :, cs], pw[slot + 1, :, cs]
                if rev:
                    sr, si = pltpu.roll(xr, SUBLANE - k, 0), pltpu.roll(xi, SUBLANE - k, 0)
                    keep = row < SUBLANE - k
                else:
                    sr, si = pltpu.roll(xr, k, 0), pltpu.roll(xi, k, 0)
                    keep = row >= k
                sr, si = jnp.where(keep, sr, 0.0), jnp.where(keep, si, 0.0)
                pr, pi = cmul(ar, ai, sr, si)
                xr, xi = xr + pr, xi + pi
            return xr, xi

        @pl.when(t == 0)
        def _():
            last[...] = jnp.zeros(last.shape, F32)
            if with_da:
                da_ref[...] = jnp.zeros(da_ref.shape, F32)
            for c in range(nc):
                cs = pl.ds(c * lc, lc)
                a1r = jnp.broadcast_to(are_ref[:, cs], (SUBLANE, lc))
                a1i = jnp.broadcast_to(aim_ref[:, cs], (SUBLANE, lc))
                a2r, a2i = cmul(a1r, a1i, a1r, a1i)
                a4r, a4i = cmul(a2r, a2i, a2r, a2i)
                for slot, v in enumerate((a1r, a1i, a2r, a2i, a4r, a4i)):
                    pw[slot, :, cs] = v
                first = SUBLANE - 1 if rev else 0
                pr, pi = hs(c, jnp.where(row == first, a1r, 0.0), jnp.where(row == first, a1i, 0.0))
                pw[6, :, cs] = pr
                pw[7, :, cs] = pi

        if with_da:
            xf32[...] = x_ref[...].astype(F32)
        edge = 0 if rev else SUBLANE - 1

        def blk(j, carry):
            jj = (nb - 1 - j) if rev else j
            r0 = pl.multiple_of(jj * SUBLANE, SUBLANE)
            for c in range(nc):
                cre, cim = pl.ds(c * lc, lc), pl.ds(S5_N + c * lc, lc)
                xr, xi = hs(c, b_ref[pl.ds(r0, SUBLANE), cre], b_ref[pl.ds(r0, SUBLANE), cim])
                cr = jnp.broadcast_to(last[edge:edge + 1, cre], (SUBLANE, lc))
                ci = jnp.broadcast_to(last[edge:edge + 1, cim], (SUBLANE, lc))
                pr, pi = cmul(pw[6, :, cre], pw[7, :, cre], cr, ci)
                xr, xi = xr + pr, xi + pi
                if with_da:
                    if rev:
                        qr, qi = pltpu.roll(xr, SUBLANE - 1, 0), pltpu.roll(xi, SUBLANE - 1, 0)
                        fill = row == SUBLANE - 1
                    else:
                        qr, qi = pltpu.roll(xr, 1, 0), pltpu.roll(xi, 1, 0)
                        fill = row == 0
                    qr, qi = jnp.where(fill, cr, qr), jnp.where(fill, ci, qi)
                    zr, zi = xf32[pl.ds(r0, SUBLANE), cre], xf32[pl.ds(r0, SUBLANE), cim]
                    da_ref[:, cre] += zr * qr + zi * qi
                    da_ref[:, cim] += zr * qi - zi * qr
                last[:, cre] = xr
                last[:, cim] = xi
                of32[pl.ds(r0, SUBLANE), cre] = xr
                of32[pl.ds(r0, SUBLANE), cim] = xi
            return carry

        lax.fori_loop(0, nb, blk, 0)
        o_ref[...] = of32[...].astype(o_ref.dtype)

    tmap = (lambda t: (nt - 1 - t, 0)) if rev else (lambda t: (t, 0))
    in_specs = [pl.BlockSpec((tb, 2 * S5_N), tmap), pl.BlockSpec((1, S5_N), lambda t: (0, 0)),
                pl.BlockSpec((1, S5_N), lambda t: (0, 0))]
    ops = [b, a_re, a_im]
    out_specs = [pl.BlockSpec((tb, 2 * S5_N), tmap)]
    out_shape = [jax.ShapeDtypeStruct((n_tok, 2 * S5_N), BF16)]
    scratch = [pltpu.VMEM((8, SUBLANE, S5_N), F32), pltpu.VMEM((SUBLANE, 2 * S5_N), F32),
               pltpu.VMEM((tb, 2 * S5_N), F32)]
    if with_da:
        in_specs.append(pl.BlockSpec((tb, 2 * S5_N), tmap))
        ops.append(xin)
        out_specs.append(pl.BlockSpec((SUBLANE, 2 * S5_N), lambda t: (0, 0)))
        out_shape.append(jax.ShapeDtypeStruct((SUBLANE, 2 * S5_N), F32))
        scratch.append(pltpu.VMEM((tb, 2 * S5_N), F32))
    vm = 2 * _nbytes((tb, 2 * S5_N), F32) + 4 * _nbytes((tb, 2 * S5_N), BF16) + 2 * _nbytes((tb, 2 * S5_N), F32)
    res = _pcall(
        body, name=name, grid=(nt,), in_specs=in_specs, out_specs=out_specs, out_shape=out_shape,
        scratch_shapes=scratch, compiler_params=_cparams(("arbitrary",), vm),
    )(*ops)
    return res if with_da else res[0]


GLA_TB = 512


def _gla_block_terms(la, q, k, rev):
    tb = la.shape[0]
    ri = lax.broadcasted_iota(jnp.int32, (tb, tb), 0)
    ci = lax.broadcasted_iota(jnp.int32, (tb, tb), 1)
    same = (ri // GLA_C) == (ci // GLA_C)
    tri = same & ((ri <= ci) if rev else (ri >= ci))
    trif = tri.astype(F32)
    bcum = _dot(trif, la, precision=HI)
    blast = _dot(same.astype(F32), la, precision=HI)
    eb, enb, ee = jnp.exp(bcum), jnp.exp(-bcum), jnp.exp(blast - bcum)
    return trif, jnp.exp(blast), eb, enb, ee, q * eb, k * enb, k * ee


def _chunk_mask(rev):
    ri = lax.broadcasted_iota(jnp.int32, (GLA_C, GLA_C), 0)
    ci = lax.broadcasted_iota(jnp.int32, (GLA_C, GLA_C), 1)
    return (ri <= ci) if rev else (ri >= ci)


def _gla_fwd(name, z, la, *, rev):
    n_tok = z.shape[0]
    tb = _tile(n_tok, GLA_TB, GLA_C)
    nblk, per = n_tok // tb, tb // GLA_C
    scale = GLA_D ** -0.5

    def body(q_ref, k_ref, v_ref, la_ref, o_ref, s_ref, st, qd_s, kd_s, ke_s, v_s, dec_s):
        @pl.when(pl.program_id(0) == 0)
        def _():
            st[...] = jnp.zeros(st.shape, F32)

        _, dec, _, _, _, qd, kd, ke = _gla_block_terms(la_ref[...], q_ref[...] * scale, k_ref[...], rev)
        qd_s[...], kd_s[...], ke_s[...] = qd.astype(BF16), kd.astype(BF16), ke.astype(BF16)
        v_s[...] = v_ref[...].astype(BF16)
        dec_s[...] = dec
        tri = _chunk_mask(rev)
        for step in range(per):
            t = per - 1 - step if rev else step
            rows = pl.ds(t * GLA_C, GLA_C)
            for h in range(GLA_H):
                lanes = pl.ds(h * GLA_D, GLA_D)
                qd16, v16 = qd_s[rows, lanes], v_s[rows, lanes]
                sc = jnp.where(tri, _dot(qd16, kd_s[rows, lanes], _NT), 0.0)
                s16 = st[h].astype(BF16)
                s_ref[h, step] = s16
                o_ref[rows, lanes] = _dot(sc.astype(BF16), v16) + _dot(qd16, s16, _NT)
                st[h] = st[h] * dec_s[pl.ds(t * GLA_C, 1), lanes] + _dot(v16, ke_s[rows, lanes], _TN)

    bmap = (lambda n: nblk - 1 - n) if rev else (lambda n: n)

    def zspec(cb):
        return pl.BlockSpec((tb, GLA_W), lambda n: (bmap(n), cb))

    half = pltpu.VMEM((tb, GLA_W), BF16)
    return _pcall(
        body, name=name, grid=(nblk,),
        in_specs=[zspec(1), zspec(2), zspec(3), zspec(0)],
        out_specs=[zspec(0), pl.BlockSpec((GLA_H, per, GLA_D, GLA_D), lambda n: (0, n, 0, 0))],
        out_shape=[jax.ShapeDtypeStruct((n_tok, GLA_W), F32),
                   jax.ShapeDtypeStruct((GLA_H, n_tok // GLA_C, GLA_D, GLA_D), BF16)],
        scratch_shapes=[pltpu.VMEM((GLA_H, GLA_D, GLA_D), F32), half, half, half, half, pltpu.VMEM((tb, GLA_W), F32)],
        compiler_params=_cparams(("arbitrary",), 40 * tb * GLA_W * 4),
    )(z, z, z, la)


def _gla_bwd(name, z, la, s_prev, do, *, rev):
    n_tok = z.shape[0]
    tb = _tile(n_tok, GLA_TB, GLA_C)
    nblk, per = n_tok // tb, tb // GLA_C
    scale = GLA_D ** -0.5

    def body(q_ref, k_ref, v_ref, la_ref, s_ref, do_ref, dq_ref, dk_ref, dv_ref, dla_ref,
             dst, qd_s, kd_s, ke_s, v_s, do_s, dec_s, dqd_s, dkd_s, dke_s, dd_s):
        @pl.when(pl.program_id(0) == 0)
        def _():
            dst[...] = jnp.zeros(dst.shape, F32)

        trif, dec, eb, enb, ee, qd, kd, ke = _gla_block_terms(la_ref[...], q_ref[...] * scale, k_ref[...], rev)
        qd_s[...], kd_s[...], ke_s[...] = qd.astype(BF16), kd.astype(BF16), ke.astype(BF16)
        v_s[...] = v_ref[...].astype(BF16)
        do_s[...] = do_ref[...].astype(BF16)
        dec_s[...] = dec
        tri = _chunk_mask(rev)
        for step in range(per):
            t = step if rev else per - 1 - step
            rows = pl.ds(t * GLA_C, GLA_C)
            for h in range(GLA_H):
                lanes = pl.ds(h * GLA_D, GLA_D)
                qd16, kd16, ke16 = qd_s[rows, lanes], kd_s[rows, lanes], ke_s[rows, lanes]
                v16, do16 = v_s[rows, lanes], do_s[rows, lanes]
                sc16 = jnp.where(tri, _dot(qd16, kd16, _NT), 0.0).astype(BF16)
                dsc16 = jnp.where(tri, _dot(do16, v16, _NT), 0.0).astype(BF16)
                s16 = s_ref[h, per - 1 - step]
                dsn = dst[h]
                dsn16 = dsn.astype(BF16)
                dec_c = dec_s[pl.ds(t * GLA_C, 1), lanes]
                dqd_s[rows, lanes] = _dot(dsc16, kd16) + _dot(do16, s16)
                dkd_s[rows, lanes] = _dot(dsc16, qd16, _TN)
                dke_s[rows, lanes] = _dot(v16, dsn16)
                dv_ref[rows, lanes] = _dot(sc16, do16, _TN) + _dot(ke16, dsn16, _NT)
                ddec = jnp.sum(dsn * s16.astype(F32), axis=0, keepdims=True) * dec_c
                dd_s[rows, lanes] = jnp.broadcast_to(ddec, (GLA_C, GLA_D))
                dst[h] = dsn * dec_c + _dot(do16, qd16, _TN)
        dqd, dkd, dke = dqd_s[...], dkd_s[...], dke_s[...]
        dq_ref[...] = dqd * eb * scale
        dk_ref[...] = dkd * enb + dke * ee
        db = dqd * qd - dkd * kd - dke * ke
        ri = lax.broadcasted_iota(jnp.int32, (tb, tb), 0)
        ci = lax.broadcasted_iota(jnp.int32, (tb, tb), 1)
        same = ((ri // GLA_C) == (ci // GLA_C)).astype(F32)
        dla_ref[...] = _dot(trif, db, _TN, precision=HI) + _dot(same, dke * ke, precision=HI) + dd_s[...]

    bmap = (lambda n: n) if rev else (lambda n: nblk - 1 - n)

    def zspec(cb):
        return pl.BlockSpec((tb, GLA_W), lambda n: (bmap(n), cb))

    o4 = jax.ShapeDtypeStruct((n_tok, GLA_W), F32)
    half, full = pltpu.VMEM((tb, GLA_W), BF16), pltpu.VMEM((tb, GLA_W), F32)
    return _pcall(
        body, name=name, grid=(nblk,),
        in_specs=[zspec(1), zspec(2), zspec(3), zspec(0),
                  pl.BlockSpec((GLA_H, per, GLA_D, GLA_D), lambda n: (0, nblk - 1 - n, 0, 0)), zspec(0)],
        out_specs=[zspec(0)] * 4, out_shape=[o4] * 4,
        scratch_shapes=[pltpu.VMEM((GLA_H, GLA_D, GLA_D), F32), half, half, half, half, half, full, full, full, full, full],
        compiler_params=_cparams(("arbitrary",), 64 * tb * GLA_W * 4),
    )(z, z, z, la, s_prev, do)


_PAIRS = AT_W // LANE
_PAIRS_PER_KV = _PAIRS // AT_KH


def _attn_fwd(name, q, kd, vd, tq=256):
    n_tok = q.shape[0]
    tq = _tile(n_tok, tq, 16)

    def body(q_ref, k_ref, v_ref, o_ref, l_ref):
        qv, kv, vv = q_ref[...], k_ref[...], v_ref[...]
        low = lax.broadcasted_iota(jnp.int32, qv.shape, 1) < AT_D
        res = []
        for keep in (low, jnp.logical_not(low)):
            s = _dot(jnp.where(keep, qv, jnp.zeros_like(qv)), kv, _NT)
            m = jnp.max(s, axis=-1, keepdims=True)
            p = jnp.exp(s - m)
            den = jnp.sum(p, axis=-1, keepdims=True)
            res.append((_dot(p.astype(BF16), vv) / den, m + jnp.log(den)))
        o_ref[...] = jnp.where(low, res[0][0], res[1][0])
        l_ref[...] = jnp.where(low, res[0][1], res[1][1])

    qspec = pl.BlockSpec((tq, LANE), lambda p, i: (i, p))
    kspec = pl.BlockSpec((n_tok, LANE), lambda p, i: (0, p // _PAIRS_PER_KV))
    vm = 5 * _nbytes((tq, n_tok), F32) + 8 * _nbytes((n_tok, LANE), BF16)
    out = jax.ShapeDtypeStruct((n_tok, AT_W), F32)
    return _pcall(
        body, name=name, grid=(_PAIRS, n_tok // tq), in_specs=[qspec, kspec, kspec], out_specs=[qspec, qspec],
        out_shape=[out, out], compiler_params=_cparams(("parallel", "parallel"), vm),
    )(q, kd, vd)


def _attn_bwd(name, q, kd, vd, o, lse, do, tq=256):
    n_tok = q.shape[0]
    tq = _tile(n_tok, tq, 16)

    def body(q_ref, k_ref, v_ref, o_ref, l_ref, do_ref, dq_ref, dk_ref, dv_ref):
        @pl.when((pl.program_id(1) == 0) & (pl.program_id(2) == 0))
        def _():
            dk_ref[...] = jnp.zeros(dk_ref.shape, F32)
            dv_ref[...] = jnp.zeros(dv_ref.shape, F32)

        qv, kv, vv = q_ref[...], k_ref[...], v_ref[...]
        dof, lv = do_ref[...], l_ref[...]
        do16 = dof.astype(BF16)
        doo = dof * o_ref[...]
        low = lax.broadcasted_iota(jnp.int32, qv.shape, 1) < AT_D
        dq = jnp.zeros(qv.shape, F32)
        for keep in (low, jnp.logical_not(low)):
            qm = jnp.where(keep, qv, jnp.zeros_like(qv))
            dom = jnp.where(keep, do16, jnp.zeros_like(do16))
            lh = jnp.max(jnp.where(keep, lv, -1e30), axis=-1, keepdims=True)
            p = jnp.exp(_dot(qm, kv, _NT) - lh)
            delta = jnp.sum(jnp.where(keep, doo, 0.0), axis=-1, keepdims=True)
            ds16 = (p * (_dot(dom, vv, _NT) - delta)).astype(BF16)
            dq = jnp.where(keep, _dot(ds16, kv), dq)
            dk_ref[...] += _dot(ds16, qm, _TN)
            dv_ref[...] += _dot(p.astype(BF16), dom, _TN)
        dq_ref[...] = dq

    qspec = pl.BlockSpec((tq, LANE), lambda g, j, i: (i, g * _PAIRS_PER_KV + j))
    kspec = pl.BlockSpec((n_tok, LANE), lambda g, j, i: (0, g))
    vm = 7 * _nbytes((tq, n_tok), F32) + 12 * _nbytes((n_tok, LANE), F32)
    dup = jax.ShapeDtypeStruct((n_tok, AT_KH * LANE), F32)
    return _pcall(
        body, name=name, grid=(AT_KH, _PAIRS_PER_KV, n_tok // tq),
        in_specs=[qspec, kspec, kspec, qspec, qspec, qspec], out_specs=[qspec, kspec, kspec],
        out_shape=[jax.ShapeDtypeStruct((n_tok, AT_W), F32), dup, dup],
        compiler_params=_cparams(("parallel", "arbitrary", "arbitrary"), vm),
    )(q, kd, vd, o, lse, do)


def _block_diag(t):
    g, a, b = t.shape
    eye = jnp.eye(g, dtype=t.dtype)
    return (t[:, :, None, :] * eye[:, None, :, None]).reshape(g * a, g * b)


def _s5_prep(lam_re, lam_im, log_dt, b_re, b_im, c_re, c_im):
    dt = jnp.exp(log_dt)[:, None]
    er = jnp.exp(lam_re * dt)
    a_re, a_im = er * jnp.cos(lam_im * dt), er * jnp.sin(lam_im * dt)
    nr, ni = a_re - 1.0, a_im
    den = lam_re * lam_re + lam_im * lam_im
    f_re, f_im = (nr * lam_re + ni * lam_im) / den, (ni * lam_re - nr * lam_im) / den
    bb_re = f_re[..., None] * b_re - f_im[..., None] * b_im
    bb_im = f_re[..., None] * b_im + f_im[..., None] * b_re
    def tiles(t, axis):
        return jnp.concatenate([_block_diag(t[k * S5_TG:(k + 1) * S5_TG]) for k in range(S5_TILES)], axis=axis)

    w_in = jnp.concatenate([tiles(bb_re.transpose(0, 2, 1), 1), tiles(bb_im.transpose(0, 2, 1), 1)], axis=1)
    w_out = jnp.concatenate([tiles(c_re.transpose(0, 2, 1), 0), -tiles(c_im.transpose(0, 2, 1), 0)], axis=0)
    return a_re.reshape(1, S5_N), a_im.reshape(1, S5_N), w_in, w_out


def _rope_tables(n_tok):
    quarter = AT_D // 4
    d = np.arange(AT_D)
    e = d % (AT_D // 2)
    inv = (ROPE_BASE ** (-(e % quarter).astype(np.float64) * 2.0 / (AT_D // 2))).astype(np.float32)
    sign = np.where(e < quarter, -1.0, 1.0).astype(np.float32)
    rot = np.zeros((AT_D, AT_D), np.float32)
    rot[np.where(e < quarter, d + quarter, d - quarter), d] = 1.0
    pos = jnp.arange(n_tok, dtype=jnp.int32)
    axis_pos = jnp.where(jnp.asarray(d // (AT_D // 2) == 0)[None, :], (pos // GRID_W)[:, None], (pos % GRID_W)[:, None])
    ang = axis_pos.astype(F32) * jnp.asarray(inv)[None, :]
    return jnp.cos(ang), jnp.sin(ang) * jnp.asarray(sign)[None, :], rot


def _head_consts(n_tok, heads):
    cos, sin, rot = _rope_tables(n_tok)
    eye = np.eye(heads, dtype=np.float32)
    mean = np.kron(eye, np.full((AT_D, AT_D), 1.0 / AT_D, np.float32))
    return jnp.tile(cos, (1, heads)), jnp.tile(sin, (1, heads)), jnp.asarray(mean), jnp.asarray(np.kron(eye, rot))


def _pad_w_in(w):
    return jnp.concatenate([w[:, :2560], w[:, 2592:3360], w[:, 2560:2592], jnp.zeros((w.shape[0], 96), w.dtype)], axis=1)


def _unpad_w_in(g):
    return jnp.concatenate([g[:, :2560], g[:, 3328:3360], g[:, 2560:3328]], axis=1)


def _ffn_fwd(x, g, wg, wu, wd, d):
    n_tok = x.shape[0]
    fq = wd.shape[-2]
    h = _rowwise("ffn_norm", lambda xv, gv: _rms(xv, gv), [x], [g], [(D_MODEL, BF16)])[0]
    tm = _tile(n_tok, 512, 16)

    def up(h_ref, wg_ref, wu_ref, a_ref, b_ref, s_ref):
        hv = h_ref[...]
        av, bv = _dot(hv, wg_ref[0, 0], _NT), _dot(hv, wu_ref[0, 0], _NT)
        a_ref[0] = av.astype(BF16)
        b_ref[0] = bv.astype(BF16)
        s_ref[0] = (av * _sigmoid(av) * bv).astype(BF16)

    wspec = pl.BlockSpec((1, 1, fq, D_MODEL), lambda i, j: (j, d, 0, 0))
    aspec = pl.BlockSpec((1, tm, fq), lambda i, j: (j, i, 0))
    act = jax.ShapeDtypeStruct((N_CHIPS, n_tok, fq), BF16)
    a, b, s = _pcall(
        up, name="ffn_up", grid=(n_tok // tm, N_CHIPS),
        in_specs=[pl.BlockSpec((tm, D_MODEL), lambda i, j: (i, 0)), wspec, wspec],
        out_specs=[aspec, aspec, aspec], out_shape=[act, act, act],
        compiler_params=_cparams(("parallel", "parallel"), 4 * D_MODEL * fq * 2 + 16 * tm * fq * 4),
    )(h, wg, wu)
    tn = _tile(D_MODEL, 512, LANE)

    def down(s_ref, w_ref, x_ref, o_ref):
        acc = _dot(s_ref[0], w_ref[0, 0])
        for j in range(1, N_CHIPS):
            acc = acc + _dot(s_ref[j], w_ref[j, 0])
        o_ref[...] = x_ref[...] + 0.5 * acc

    xo = _pcall(
        down, name="ffn_down", grid=(n_tok // tm, D_MODEL // tn),
        in_specs=[pl.BlockSpec((N_CHIPS, tm, fq), lambda i, n: (0, i, 0)),
                  pl.BlockSpec((N_CHIPS, 1, fq, tn), lambda i, n: (0, d, 0, n)),
                  pl.BlockSpec((tm, tn), lambda i, n: (i, n))],
        out_specs=pl.BlockSpec((tm, tn), lambda i, n: (i, n)), out_shape=jax.ShapeDtypeStruct((n_tok, D_MODEL), F32),
        compiler_params=_cparams(("parallel", "parallel"), 4 * N_CHIPS * (tm + tn) * fq * 2 + 6 * tm * tn * 4),
    )(s, wd, x)
    return xo, (x, h, a, b, s)


def _ffn_bwd(dy, dy16, saved, g, wg, wu, wd, d, into):
    x, h, a, b, s = saved
    n_tok = x.shape[0]
    fq = wd.shape[-2]
    tm = _tile(n_tok, 512, 16)

    def dact(dy_ref, w_ref, a_ref, b_ref, da_ref, db_ref):
        ds = 0.5 * _dot(dy_ref[...], w_ref[0, 0], _NT)
        av, bv = a_ref[0].astype(F32), b_ref[0].astype(F32)
        sg = _sigmoid(av)
        da_ref[0] = (ds * bv * sg * (1.0 + av * (1.0 - sg))).astype(BF16)
        db_ref[0] = (ds * av * sg).astype(BF16)

    aspec = pl.BlockSpec((1, tm, fq), lambda i, j: (j, i, 0))
    act = jax.ShapeDtypeStruct((N_CHIPS, n_tok, fq), BF16)
    da, db = _pcall(
        dact, name="ffn_dact", grid=(n_tok // tm, N_CHIPS),
        in_specs=[pl.BlockSpec((tm, D_MODEL), lambda i, j: (i, 0)),
                  pl.BlockSpec((1, 1, fq, D_MODEL), lambda i, j: (j, d, 0, 0)), aspec, aspec],
        out_specs=[aspec, aspec], out_shape=[act, act],
        compiler_params=_cparams(("parallel", "parallel"), 4 * D_MODEL * fq * 2 + 16 * tm * fq * 4),
    )(dy16, wd, a, b)

    any_spec = pl.BlockSpec(memory_space=pl.ANY)
    alias = {} if into is None else {2: 0}
    tn = _tile(D_MODEL, 512, LANE)

    def dw_call(t, rhs, scale, prev):
        def dw(t_ref, r_ref, *rest):
            rest[-1][0, 0] = scale * _dot(t_ref[0], r_ref[...], _TN)

        return _pcall(
            dw, name="ffn_dw", grid=(N_CHIPS, D_MODEL // tn),
            in_specs=[pl.BlockSpec((1, n_tok, fq), lambda j, n: (j, 0, 0)), pl.BlockSpec((n_tok, tn), lambda j, n: (0, n))]
            + ([] if prev is None else [any_spec]),
            out_specs=pl.BlockSpec((1, 1, fq, tn), lambda j, n: (j, d, 0, n)),
            out_shape=jax.ShapeDtypeStruct((N_CHIPS, DEPTH, fq, D_MODEL), F32), input_output_aliases=alias,
            compiler_params=_cparams(("parallel", "parallel"), 6 * n_tok * (fq + tn) * 2 + 4 * fq * tn * 4),
        )(t, rhs, *([] if prev is None else [prev]))

    dwg = dw_call(da, h, 1.0, None if into is None else into[0])
    dwu = dw_call(db, h, 1.0, None if into is None else into[1])
    dwd = dw_call(s, dy16, 0.5, None if into is None else into[2])
    tr = _tile(n_tok, 256, 16)

    def dh_norm(da_ref, db_ref, wg_ref, wu_ref, dy_ref, x_ref, g_ref, dx_ref, dx16_ref, dg_ref):
        dh = _dot(da_ref[0], wg_ref[0, 0]) + _dot(db_ref[0], wu_ref[0, 0])
        for j in range(1, N_CHIPS):
            dh = dh + _dot(da_ref[j], wg_ref[j, 0]) + _dot(db_ref[j], wu_ref[j, 0])
        dx, dg = _rms_bwd(x_ref[...], g_ref[...], dh)
        dx = dy_ref[...] + dx
        dx_ref[...] = dx
        dx16_ref[...] = dx.astype(BF16)

        @pl.when(pl.program_id(0) == 0)
        def _():
            dg_ref[...] = jnp.zeros(dg_ref.shape, F32)

        dg_ref[...] += dg

    tspec = pl.BlockSpec((N_CHIPS, tr, fq), lambda i: (0, i, 0))
    wspec = pl.BlockSpec((N_CHIPS, 1, fq, D_MODEL), lambda i: (0, d, 0, 0))
    rspec = pl.BlockSpec((tr, D_MODEL), lambda i: (i, 0))
    vspec = pl.BlockSpec((1, D_MODEL), lambda i: (0, 0))
    dx, dx16, dg = _pcall(
        dh_norm, name="ffn_dh_norm", grid=(n_tok // tr,),
        in_specs=[tspec, tspec, wspec, wspec, rspec, rspec, vspec], out_specs=[rspec, rspec, vspec],
        out_shape=[jax.ShapeDtypeStruct((n_tok, D_MODEL), F32), jax.ShapeDtypeStruct((n_tok, D_MODEL), BF16),
                   jax.ShapeDtypeStruct((1, D_MODEL), F32)],
        compiler_params=_cparams(("arbitrary",), 4 * N_CHIPS * D_MODEL * fq * 2 + 4 * N_CHIPS * tr * fq * 2
                                 + 16 * tr * D_MODEL * 4),
    )(da, db, wg, wu, dy, x, g)
    return dx, dx16, dg[0], (dwg, dwu, dwd)


def _alpha_pads(w_alpha):
    z = jnp.zeros((LANE - 2 * GLA_R, GLA_W), F32)
    wf = jnp.concatenate([w_alpha[0], jnp.zeros((GLA_R, GLA_W), F32), z], axis=0)
    wb = jnp.concatenate([jnp.zeros((GLA_R, GLA_W), F32), w_alpha[1], z], axis=0)
    return wf, wb


def _dup_heads(t):
    low = lax.broadcasted_iota(jnp.int32, t.shape, 1) < AT_D
    h0, h1 = jnp.where(low, t, 0.0), jnp.where(low, 0.0, t)
    return jnp.concatenate([h0 + pltpu.roll(h0, AT_D, 1), h1 + pltpu.roll(h1, AT_D, 1)], axis=1)


def _fold_heads(t):
    low = lax.broadcasted_iota(jnp.int32, (t.shape[0], LANE), 1) < AT_D
    g0, g1 = t[:, :LANE], t[:, LANE:]
    return jnp.where(low, g0 + pltpu.roll(g0, AT_D, 1), g1 + pltpu.roll(g1, AT_D, 1))


def _mixer_fwd(x, w):
    n_tok = x.shape[0]
    h = _rowwise("mix_norm", lambda xv, gv: _rms(xv, gv), [x], [w['mix_norm']], [(D_MODEL, BF16)])[0]
    z = _mm("mix_in", h, w['w_in_p'])
    gp = _mm("mix_gate", h, w['w_merge_gate'], out_dtype=BF16)
    u = _cw(z, 0, S5_W)

    xs, yy = [], None
    for d, rev in ((0, False), (1, True)):
        a_re, a_im, wb, wc = w['s5'][d]
        bu = _s5_in(z, wb)
        xs.append(_scan("s5_scan_rev" if rev else "s5_scan_fwd", bu, a_re, a_im, rev=rev))
        yy = _s5_out(xs[d], wc, yy)

    def s5_post(yv, uv, dv, wglu):
        y0 = yv + dv * uv
        y1 = _gelu(y0)
        t = _dot(y1.astype(BF16), wglu)
        return y0, t, y1 * _sigmoid(t)

    y0, t_glu, y_s5 = _rowwise("s5_post", s5_post, [yy, u], [w['s5_d'], w['s5_w_glu']],
                               [(S5_W, F32), (S5_W, F32), (S5_W, BF16)])

    wa_f, wa_b = _alpha_pads(w['gla_w_alpha'])
    ba_f, ba_b = w['gla_b_alpha'][0:1], w['gla_b_alpha'][1:2]
    zz = _cw(z, ZC_Z, LANE)

    def gla_prep(zv, waf, wab, bf, bb):
        z16 = zv.astype(BF16)
        lf = _dot(z16, waf.astype(BF16)) + bf
        lb = _dot(z16, wab.astype(BF16)) + bb
        return _log_sigmoid(lf) / GLA_TAU, _log_sigmoid(lb) / GLA_TAU

    la_f, la_b = _rowwise("gla_prep", gla_prep, [zz], [wa_f, wa_b, ba_f, ba_b], [(GLA_W, F32), (GLA_W, F32)])
    o_f, s_f = _gla_fwd("gla_fwd", z, la_f, rev=False)
    o_b, s_b = _gla_fwd("gla_rev", z, la_b, rev=True)
    gate = _cw(z, 4, GLA_W)

    def gla_post(of, ob, gv, gn):
        o = of + ob
        outs = []
        for hh in range(GLA_H):
            sl = slice(hh * GLA_D, (hh + 1) * GLA_D)
            gh = gv[:, sl]
            outs.append(_rms(o[:, sl], gn) * (gh * _sigmoid(gh)))
        return jnp.concatenate(outs, axis=1)

    y_gla = _rowwise("gla_post", gla_post, [o_f, o_b, gate], [w['gla_norm']], [(GLA_W, BF16)])[0]

    cq, sq, mq, rq = _head_consts(n_tok, AT_QH)
    ck, sk, mk, rk = _head_consts(n_tok, AT_KH)
    gq = jnp.tile(w['attn_q_norm'], (1, AT_QH))
    gk = jnp.tile(w['attn_k_norm'], (1, AT_KH))

    def rope_fwd(v, gain, cos, sin, mean, rot, scale):
        r = lax.rsqrt(_dot(v * v, mean, precision=HI) + EPS)
        vn = v * r * gain
        return (vn * cos + _dot(vn, rot, precision=HI) * sin) * scale

    def attn_prep(qv, kv, vv, cqv, sqv, ckv, skv, gqv, gkv, mqv, rqv, mkv, rkv):
        kr = rope_fwd(kv, gkv, ckv, skv, mkv, rkv, 1.0)
        return rope_fwd(qv, gqv, cqv, sqv, mqv, rqv, AT_D ** -0.5), _dup_heads(kr), _dup_heads(vv)

    q_raw, k_raw, v_raw = _cw(z, 5, AT_W), _cw(z, ZC_AK, AT_KW), _cw(z, ZC_AV, AT_KW)
    q16, kd, vd = _rowwise("attn_prep", attn_prep, [q_raw, k_raw, v_raw, cq, sq, ck, sk], [gq, gk, mq, rq, mk, rk],
                           [(AT_W, BF16), (2 * LANE, BF16), (2 * LANE, BF16)])
    y_at, lse = _attn_fwd("attn_fwd", q16, kd, vd)

    p0 = _mm("mix_branch", y_s5, w['w_branch_s5'], out_dtype=BF16)
    p1 = _mm("mix_branch", y_gla, w['w_branch_gla'], out_dtype=BF16)
    p2 = _mm("mix_branch_f32", y_at, w['w_branch_attn'], out_dtype=BF16)

    def merge(gpv, a0, a1, a2, bias):
        g = _sigmoid(gpv.astype(F32) + bias)
        return (g[:, :D_MODEL] * a0.astype(F32) + g[:, D_MODEL:2 * D_MODEL] * a1.astype(F32)
                + g[:, 2 * D_MODEL:] * a2.astype(F32))

    merged = _rowwise("mix_merge", merge, [gp, p0, p1, p2], [w['b_merge_gate']], [(D_MODEL, BF16)])[0]
    xo = _mm("mix_out", merged, w['w_out'], add=x)
    saved = dict(x=x, h=h, z=z, gp=gp, xs=xs, y0=y0, t_glu=t_glu, y_s5=y_s5, la=(la_f, la_b), o=(o_f, o_b), s=(s_f, s_b),
                 y_gla=y_gla, q16=q16, kd=kd, vd=vd, lse=lse, y_at=y_at, p=(p0, p1, p2), merged=merged)
    return xo, saved


def _mixer_bwd(dy, dy16, sv, w, s5_raw):
    z, h = sv['z'], sv['h']
    n_tok = z.shape[0]
    grads = {}
    dmerged = _mm("mix_dmerged", dy16, w['w_out'], tb=True, out_dtype=BF16)
    grads['w_out'] = _mm("mix_dwout", sv['merged'], dy16, ta=True)

    def dmerge(dm, gpv, a0, a1, a2, bias):
        dm = dm.astype(F32)
        g = _sigmoid(gpv.astype(F32) + bias)
        ps = (a0.astype(F32), a1.astype(F32), a2.astype(F32))
        dps, dgs = [], []
        for k in range(3):
            gk = g[:, k * D_MODEL:(k + 1) * D_MODEL]
            dps.append(dm * gk)
            dgs.append(dm * ps[k] * gk * (1.0 - gk))
        dgp = jnp.concatenate(dgs, axis=1)
        return dps[0], dps[1], dps[2], dgp, jnp.sum(dgp, axis=0, keepdims=True)

    dp0, dp1, dp2, dgp, dbm = _rowwise("mix_dmerge", dmerge, [dmerged, sv['gp'], *sv['p']], [w['b_merge_gate']],
                                       [(D_MODEL, BF16)] * 3 + [(3 * D_MODEL, BF16)], accs=[(1, 3 * D_MODEL)])
    grads['b_merge_gate'] = dbm[0]
    grads['w_branch_s5'] = _mm("mix_dbranch", sv['y_s5'], dp0, ta=True)
    grads['w_branch_gla'] = _mm("mix_dbranch", sv['y_gla'], dp1, ta=True)
    grads['w_branch_attn'] = _mm("mix_dbranch_f32", sv['y_at'], dp2, ta=True)
    dy_s5 = _mm("mix_dy", dp0, w['w_branch_s5'], tb=True)
    dy_gla = _mm("mix_dy", dp1, w['w_branch_gla'], tb=True)
    dy_at = _mm("mix_dy", dp2, w['w_branch_attn'], tb=True)
    grads['w_merge_gate'] = _mm("mix_dwgate", h, dgp, ta=True)
    dh = _mm("mix_dh_gate", dgp, w['w_merge_gate'], tb=True)
    u = _cw(z, 0, S5_W)

    def s5_post_bwd(dyv, y0, t, uv, dv, wglu):
        sg = _sigmoid(t)
        y1 = _gelu(y0)
        dt = dyv * y1 * sg * (1.0 - sg)
        dy1 = dyv * sg + _dot(dt.astype(BF16), wglu, _NT)
        dy0 = dy1 * _gelu_grad(y0)
        return dy0, dt, y1, dy0 * dv, jnp.sum(dy0 * uv, axis=0, keepdims=True)

    dy0, dt_glu, y1, du, dd = _rowwise("s5_post_bwd", s5_post_bwd, [dy_s5, sv['y0'], sv['t_glu'], u],
                                       [w['s5_d'], w['s5_w_glu']],
                                       [(S5_W, BF16), (S5_W, BF16), (S5_W, BF16), (S5_W, F32)], accs=[(1, S5_W)])
    grads['s5_d'] = dd[0]
    grads['s5_w_glu'] = _mm("s5_dwglu", y1, dt_glu, ta=True)
    s5_cot = []
    for d, rev in ((0, False), (1, True)):
        a_re, a_im, wb, wc = w['s5'][d]
        gx = _s5_out_t(dy0, wc)
        gs, da = _scan("s5_adj_fwd" if rev else "s5_adj_rev", gx, a_re, -a_im, rev=not rev, xin=sv['xs'][d])
        dwc = _s5_dwc(sv['xs'][d], dy0)
        dwb = _s5_dwb(z, gs)
        du = _s5_in_t(gs, wb, du)
        da = jnp.sum(da, axis=0, keepdims=True)
        s5_cot.append((da[:, :S5_N], da[:, S5_N:], dwb, dwc))
    for d in range(2):
        _, pull = jax.vjp(_s5_prep, *s5_raw[d])
        s5_cot[d] = pull(s5_cot[d])
    for k, nm in enumerate(('s5_lambda_re', 's5_lambda_im', 's5_log_dt', 's5_b_re', 's5_b_im', 's5_c_re', 's5_c_im')):
        grads[nm] = jnp.stack([s5_cot[0][k], s5_cot[1][k]])

    gate = _cw(z, 4, GLA_W)

    def gla_post_bwd(dyv, of, ob, gv, gn):
        o = of + ob
        dos, dgates = [], []
        dgn = jnp.zeros((1, GLA_D), F32)
        for hh in range(GLA_H):
            sl = slice(hh * GLA_D, (hh + 1) * GLA_D)
            gh, dyh = gv[:, sl], dyv[:, sl]
            sg = _sigmoid(gh)
            dgates.append(dyh * _rms(o[:, sl], gn) * sg * (1.0 + gh * (1.0 - sg)))
            dx, dg = _rms_bwd(o[:, sl], gn, dyh * gh * sg)
            dos.append(dx)
            dgn = dgn + dg
        return jnp.concatenate(dos, axis=1), jnp.concatenate(dgates, axis=1), dgn

    do, dgate, dgn = _rowwise("gla_post_bwd", gla_post_bwd, [dy_gla, *sv['o'], gate], [w['gla_norm']],
                              [(GLA_W, F32), (GLA_W, F32)], accs=[(1, GLA_D)])
    grads['gla_norm'] = dgn[0]
    dq_f, dk_f, dv_f, dla_f = _gla_bwd("gla_bwd_fwd", z, sv['la'][0], sv['s'][0], do, rev=False)
    dq_b, dk_b, dv_b, dla_b = _gla_bwd("gla_bwd_rev", z, sv['la'][1], sv['s'][1], do, rev=True)
    wa_f, wa_b = _alpha_pads(w['gla_w_alpha'])
    ba_f, ba_b = w['gla_b_alpha'][0:1], w['gla_b_alpha'][1:2]
    zz = _cw(z, ZC_Z, LANE)

    def gla_prep_bwd(dlf, dlb, zv, waf, wab, bf, bb):
        z16 = zv.astype(BF16)
        waf16, wab16 = waf.astype(BF16), wab.astype(BF16)
        xf = _dot(z16, waf16) + bf
        xb = _dot(z16, wab16) + bb
        df = dlf * (_sigmoid(-xf) / GLA_TAU)
        db = dlb * (_sigmoid(-xb) / GLA_TAU)
        dz = _dot(df.astype(BF16), waf16, _NT) + _dot(db.astype(BF16), wab16, _NT)
        return df, db, dz, jnp.sum(df, axis=0, keepdims=True), jnp.sum(db, axis=0, keepdims=True)

    dlog_f, dlog_b, dzz, dba_f, dba_b = _rowwise(
        "gla_prep_bwd", gla_prep_bwd, [dla_f, dla_b, zz], [wa_f, wa_b, ba_f, ba_b],
        [(GLA_W, BF16), (GLA_W, BF16), (LANE, F32)], accs=[(1, GLA_W), (1, GLA_W)])
    dwa_f = _mm("gla_dwa", zz, dlog_f, ta=True)
    dwa_b = _mm("gla_dwa", zz, dlog_b, ta=True)
    grads['gla_w_alpha'] = jnp.stack([dwa_f[:GLA_R], dwa_b[GLA_R:2 * GLA_R]])
    grads['gla_b_alpha'] = jnp.concatenate([dba_f, dba_b], axis=0)

    dq_r, dkd, dvd = _attn_bwd("attn_bwd", sv['q16'], sv['kd'], sv['vd'], sv['y_at'], sv['lse'], dy_at)
    cq, sq, mq, rq = _head_consts(n_tok, AT_QH)
    ck, sk, mk, rk = _head_consts(n_tok, AT_KH)
    gq = jnp.tile(w['attn_q_norm'], (1, AT_QH))
    gk = jnp.tile(w['attn_k_norm'], (1, AT_KH))

    def rope_bwd(dr, v, gain, cos, sin, mean, rot, scale):
        dr = dr * scale
        dvn = dr * cos + _dot(dr * sin, rot, _NT, precision=HI)
        r = lax.rsqrt(_dot(v * v, mean, precision=HI) + EPS)
        vh = v * r
        dg = jnp.sum(dvn * vh, axis=0, keepdims=True)
        dvh = dvn * gain
        return r * (dvh - vh * _dot(dvh * vh, mean, precision=HI)), dg

    def attn_prep_bwd(dq, dk2, dv2, qv, kv, cqv, sqv, ckv, skv, gqv, gkv, mqv, rqv, mkv, rkv):
        dqo, dgq = rope_bwd(dq, qv, gqv, cqv, sqv, mqv, rqv, AT_D ** -0.5)
        dko, dgk = rope_bwd(_fold_heads(dk2), kv, gkv, ckv, skv, mkv, rkv, 1.0)
        return dqo, dko, _fold_heads(dv2), dgq, dgk

    q_raw, k_raw = _cw(z, 5, AT_W), _cw(z, ZC_AK, AT_KW)
    daq, dak, dav, dgq, dgk = _rowwise("attn_prep_bwd", attn_prep_bwd, [dq_r, dkd, dvd, q_raw, k_raw, cq, sq, ck, sk],
                                       [gq, gk, mq, rq, mk, rk], [(AT_W, F32), (AT_KW, F32), (AT_KW, F32)],
                                       accs=[(1, AT_W), (1, AT_KW)])
    grads['attn_q_norm'] = dgq.reshape(AT_QH, AT_D).sum(axis=0)
    grads['attn_k_norm'] = dgk.reshape(AT_KH, AT_D).sum(axis=0)

    def assemble(duv, qf, qb, kf, kb, vf, vb, dg, aq, ak, av, zzv):
        return jnp.concatenate([duv, qf + qb, kf + kb, vf + vb, dg, aq, ak, av, zzv], axis=1)

    dz = _rowwise("mix_dz", assemble, [du, dq_f, dq_b, dk_f, dk_b, dv_f, dv_b, dgate, daq, dak, dav, dzz], [],
                  [(ZP_W, BF16)])[0]
    grads['w_in'] = _unpad_w_in(_mm("mix_dwin", h, dz, ta=True))
    dh = _mm("mix_dh_in", dz, w['w_in_p'], tb=True, add=dh)

    def dnorm(dyv, xv, dhv, gv):
        dx, dg = _rms_bwd(xv, gv, dhv)
        dx = dyv + dx
        return dx, dx, dg

    dx, dx16, dg = _rowwise("mix_dnorm", dnorm, [dy, sv['x'], dh], [w['mix_norm']], [(D_MODEL, F32), (D_MODEL, BF16)],
                            accs=[(1, D_MODEL)])
    grads['mix_norm'] = dg[0]
    return dx, dx16, grads


FFN_NAMES = [p + k for p in ('ffn1_', 'ffn2_') for k in ('w_gate', 'w_up', 'w_down')]
MISC_SHARDED = [n for n in SHARDED if n not in FFN_NAMES]
LARGE_SHARDED = FFN_NAMES + ['w_in', 'w_merge_gate']
KEPT_TRANSPOSED = [p + k for p in ('ffn1_', 'ffn2_') for k in ('w_gate', 'w_up')]


def _stored(n, t):
    return jnp.swapaxes(t, 1, 2) if n in KEPT_TRANSPOSED else t


def _layer_weights(ws, rep, i):
    w = {k: _from_shards(ws[k][:, i], SHARD_AXIS[k] - 1) for k in MISC_SHARDED}
    for k in ('ffn1_norm', 'mix_norm', 'ffn2_norm', 's5_d', 'gla_norm', 'attn_q_norm', 'attn_k_norm', 'b_merge_gate'):
        w[k] = rep[k][i].reshape(1, -1)
    w['w_in_p'] = _pad_w_in(w['w_in'])
    s5_raw = [tuple(rep[k][i, d] for k in ('s5_lambda_re', 's5_lambda_im', 's5_log_dt', 's5_b_re', 's5_b_im',
                                            's5_c_re', 's5_c_im')) for d in range(2)]
    w['s5'] = [_s5_prep(*s5_raw[d]) for d in range(2)]
    return w, s5_raw


def _device_step(x, target, ws, rep):
    lw = [_layer_weights(ws, rep, i) for i in range(DEPTH)]
    ffn = {p: (ws[p + 'w_gate'], ws[p + 'w_up'], ws[p + 'w_down']) for p in ('ffn1_', 'ffn2_')}
    saved = []
    for i in range(DEPTH):
        w, _ = lw[i]
        x, s1 = _ffn_fwd(x, w['ffn1_norm'], *ffn['ffn1_'], i)
        x, s2 = _mixer_fwd(x, w)
        x, s3 = _ffn_fwd(x, w['ffn2_norm'], *ffn['ffn2_'], i)
        saved.append((s1, s2, s3))
    gfin = rep['final_norm'].reshape(1, -1)

    def head(xv, tv, gv):
        e = _rms(xv, gv) - tv
        dx, dg = _rms_bwd(xv, gv, e * (1.0 / D_MODEL))
        part = jnp.sum(e * e, axis=0, keepdims=True)
        return dx, dx, dg, part

    dx, dx16, dgfin, part = _rowwise("loss_head", head, [x, target], [gfin], [(D_MODEL, F32), (D_MODEL, BF16)],
                                     accs=[(1, D_MODEL), (1, D_MODEL)])
    loss = (0.5 / D_MODEL) * jnp.sum(part)
    per_layer = []
    dffn = {'ffn1_': None, 'ffn2_': None}
    for i in reversed(range(DEPTH)):
        w, s5_raw = lw[i]
        s1, s2, s3 = saved[i]
        dx, dx16, n3, dffn['ffn2_'] = _ffn_bwd(dx, dx16, s3, w['ffn2_norm'], *ffn['ffn2_'], i, dffn['ffn2_'])
        dx, dx16, g = _mixer_bwd(dx, dx16, s2, w, s5_raw)
        dx, dx16, n1, dffn['ffn1_'] = _ffn_bwd(dx, dx16, s1, w['ffn1_norm'], *ffn['ffn1_'], i, dffn['ffn1_'])
        g['ffn2_norm'], g['ffn1_norm'] = n3, n1
        per_layer.append(g)
    per_layer.reverse()
    stacked = {k: jnp.stack([per_layer[i][k] for i in range(DEPTH)]) for k in per_layer[0]}
    gs = {k: _to_shards(stacked[k], SHARD_AXIS[k]) for k in MISC_SHARDED}
    for p in ('ffn1_', 'ffn2_'):
        gs[p + 'w_gate'], gs[p + 'w_up'], gs[p + 'w_down'] = dffn[p]
    grep = {k: stacked[k] for k in REPLICATED if k != 'final_norm'}
    grep['final_norm'] = dgfin[0]
    return loss, dx, gs, grep


PACK_C = 1024
ROW_TILE = 512


def _rows_of(shape):
    return -(-int(np.prod(shape)) // PACK_C)


def _as_rows(t, lead=0):
    head = t.shape[:lead]
    flat = t.reshape(head + (-1,))
    rows = -(-flat.shape[-1] // PACK_C)
    pad = rows * PACK_C - flat.shape[-1]
    if pad:
        flat = jnp.pad(flat, [(0, 0)] * lead + [(0, pad)])
    return flat.reshape(head + (rows, PACK_C))


def _pack(items, total_rows, lead=0):
    parts = [_as_rows(t, lead) for t in items]
    used = sum(p.shape[lead] for p in parts)
    if total_rows > used:
        parts.append(jnp.zeros(parts[0].shape[:lead] + (total_rows - used, PACK_C), parts[0].dtype))
    return jnp.concatenate(parts, axis=lead)


def _unpack(rows, shapes, lead=0):
    out, r0 = [], 0
    head = rows.shape[:lead]
    for shp in shapes:
        n, size = _rows_of(shp), int(np.prod(shp))
        piece = lax.slice_in_dim(rows, r0, r0 + n, axis=lead).reshape(head + (n * PACK_C,))
        out.append(lax.slice_in_dim(piece, 0, size, axis=lead).reshape(head + tuple(shp)))
        r0 += n
    return out


def _round_up(n, m):
    return -(-n // m) * m


def _to_shards(g, axis):
    shp = g.shape
    g = g.reshape(shp[:axis] + (N_CHIPS, shp[axis] // N_CHIPS) + shp[axis + 1:])
    return jnp.moveaxis(g, axis, 0)


def _from_shards(s, axis):
    s = jnp.moveaxis(s, 0, axis)
    shp = s.shape
    return s.reshape(shp[:axis] + (shp[axis] * shp[axis + 1],) + shp[axis + 2:])


_HBM = pl.BlockSpec(memory_space=pltpu.HBM)


def _place():
    x, y, c = lax.axis_index("x"), lax.axis_index("y"), lax.axis_index("c")
    chips = [(1 - x, y), (x, 1 - y), (1 - x, 1 - y)]
    return x, y, c, chips


def _rcopy(src, dst, send, recv, k, dev):
    return pltpu.make_async_remote_copy(src_ref=src, dst_ref=dst, send_sem=send.at[k], recv_sem=recv.at[k],
                                        device_id=dev, device_id_type=MESH)


def _comm_call(name, body, arrays, out_shapes, n_sems, n_local):
    return pl.pallas_call(
        body, name=name, in_specs=[_HBM] * len(arrays), out_specs=[_HBM] * len(out_shapes), out_shape=out_shapes,
        scratch_shapes=[pltpu.SemaphoreType.DMA((n_sems,)), pltpu.SemaphoreType.DMA((n_sems,)),
                        pltpu.SemaphoreType.DMA((max(n_local, 1),))],
    )(*arrays)


def _ici_peers(x, y, c):
    x_first = c == 0
    first = (jnp.where(x_first, 1 - x, x), jnp.where(x_first, y, 1 - y))
    second = (jnp.where(x_first, x, 1 - x), jnp.where(x_first, 1 - y, y))
    return first, second


def _ag_chips(name, packs):
    n = len(packs)

    def body(*refs):
        p, o, (send, recv, _) = refs[:n], refs[n:2 * n], refs[2 * n:]
        x, y, c, _ = _place()
        (ax, ay), (bx, by) = _ici_peers(x, y, c)
        me, na, nb, nd = 2 * x + y, 2 * ax + ay, 2 * bx + by, 2 * (1 - x) + (1 - y)
        sib = (x, y, 1 - c)
        own = [_rcopy(p[a].at[c], o[a].at[me, c], send, recv, 6 * a, (ax, ay, c)) for a in range(n)]
        own += [_rcopy(p[a].at[c], o[a].at[me, c], send, recv, 6 * a + 1, (bx, by, c)) for a in range(n)]
        for cp in own:
            cp.start()
        passed = []

        def landed(a, chip, k, fwd):
            blk = o[a].at[chip, c]
            _rcopy(blk, blk, send, recv, 6 * a + k, sib).wait_recv()
            for sem, dev in fwd:
                cp = _rcopy(blk, blk, send, recv, 6 * a + sem, dev)
                cp.start()
                passed.append(cp)

        for a in range(n):
            landed(a, na, 0, [(2, (bx, by, c)), (3, sib)])
        for a in range(n):
            landed(a, nb, 1, [(4, sib)])
        for a in range(n):
            landed(a, nd, 2, [(5, sib)])
        for a in range(n):
            for k, chip in ((3, nb), (4, na), (5, nd)):
                blk = o[a].at[chip, 1 - c]
                _rcopy(blk, blk, send, recv, 6 * a + k, sib).wait_recv()
        for cp in own + passed:
            cp.wait_send()

    outs = _comm_call(name, body, packs, [jax.ShapeDtypeStruct((N_CHIPS,) + t.shape, t.dtype) for t in packs], 6 * n, 0)
    chip = 2 * lax.axis_index("x") + lax.axis_index("y")
    return [lax.dynamic_update_slice_in_dim(o, t[None], chip, axis=0) for o, t in zip(outs, packs)]


def _pair_swap(name, gs):
    n = len(gs)
    ns = gs[0].shape[0]

    def body(*refs):
        g, r, (send, recv, _) = refs[:n], refs[n:2 * n], refs[2 * n:]
        x, y, c, _ = _place()
        cps = [_rcopy(g[a].at[j, 1 - c], r[a].at[j], send, recv, ns * a + j, (x, y, 1 - c))
               for a in range(n) for j in range(ns)]
        for cp in cps:
            cp.start()
        for cp in cps:
            cp.wait_recv()
        for cp in cps:
            cp.wait_send()

    return _comm_call(name, body, gs, [jax.ShapeDtypeStruct((ns,) + t.shape[2:], t.dtype) for t in gs], ns * n, 0)


def _halve_first(name, parts):
    n = len(parts)

    def body(*refs):
        p, r, (send, recv, _) = refs[:n], refs[n:2 * n], refs[2 * n:]
        x, y, c, _ = _place()
        (ax, ay), _ = _ici_peers(x, y, c)
        cps = [_rcopy(p[a].at[jnp.where(c == 0, 2 * ax + k, 2 * k + ay)], r[a].at[k], send, recv, 2 * a + k, (ax, ay, c))
               for a in range(n) for k in range(2)]
        for cp in cps:
            cp.start()
        for cp in cps:
            cp.wait_recv()
        for cp in cps:
            cp.wait_send()

    return _comm_call(name, body, parts, [jax.ShapeDtypeStruct((2,) + t.shape[1:], t.dtype) for t in parts], 2 * n, 0)


def _halve_second(name, kept):
    n = len(kept)

    def body(*refs):
        p, r, (send, recv, _) = refs[:n], refs[n:2 * n], refs[2 * n:]
        x, y, c, _ = _place()
        _, (bx, by) = _ici_peers(x, y, c)
        cps = [_rcopy(p[a].at[jnp.where(c == 0, by, bx)], r[a], send, recv, a, (bx, by, c)) for a in range(n)]
        for cp in cps:
            cp.start()
        for cp in cps:
            cp.wait_recv()
        for cp in cps:
            cp.wait_send()

    return _comm_call(name, body, kept, [jax.ShapeDtypeStruct(t.shape[1:], t.dtype) for t in kept], n, 0)


def _add_selected(name, t, r, index, out_dtype):
    single = r.ndim == 2
    r3 = r[None] if single else r
    nk, rh, cc = r3.shape
    tl = _tile(rh, ROW_TILE, 16)

    def body(i_ref, t_ref, r_ref, o_ref):
        o_ref[...] = (t_ref[...].astype(F32) + r_ref[...].astype(F32)).astype(o_ref.dtype)

    out = _pcall(
        body, name=name,
        grid_spec=pltpu.PrefetchScalarGridSpec(
            num_scalar_prefetch=1, grid=(nk, rh // tl),
            in_specs=[pl.BlockSpec((1, tl, cc), lambda k, i, i_ref: (i_ref[k], i, 0)),
                      pl.BlockSpec((1, tl, cc), lambda k, i, i_ref: (k, i, 0))],
            out_specs=pl.BlockSpec((1, tl, cc), lambda k, i, i_ref: (k, i, 0))),
        out_shape=jax.ShapeDtypeStruct(r3.shape, out_dtype),
        compiler_params=_cparams(("parallel", "parallel"), 8 * tl * cc * 4),
    )(index, t, r3)
    return out[0] if single else out


def _pair_share(name, reds):
    n = len(reds)

    def body(*refs):
        r, o, (send, recv, _) = refs[:n], refs[n:2 * n], refs[2 * n:]
        x, y, c, _ = _place()
        cps = [_rcopy(r[a], o[a].at[c], send, recv, a, (x, y, 1 - c)) for a in range(n)]
        for cp in cps:
            cp.start()
        for a in range(n):
            blk = o[a].at[1 - c]
            _rcopy(blk, blk, send, recv, a, (x, y, 1 - c)).wait_recv()
        for cp in cps:
            cp.wait_send()

    outs = _comm_call(name, body, reds, [jax.ShapeDtypeStruct((2,) + t.shape, t.dtype) for t in reds], n, 0)
    core = lax.axis_index("c")
    return [lax.dynamic_update_slice_in_dim(o, t[None], core, axis=0) for o, t in zip(outs, reds)]


def _add_own_half(name, g, r):
    ns, _, rh, cc = g.shape
    tl = _tile(rh, ROW_TILE, 16)
    core = lax.axis_index("c").astype(jnp.int32).reshape(1)

    def body(c_ref, g_ref, r_ref, o_ref):
        o_ref[...] = (g_ref[0] + r_ref[...]).astype(o_ref.dtype)

    return _pcall(
        body, name=name,
        grid_spec=pltpu.PrefetchScalarGridSpec(
            num_scalar_prefetch=1, grid=(ns, rh // tl),
            in_specs=[pl.BlockSpec((1, 1, tl, cc), lambda j, i, c_ref: (j, c_ref[0], i, 0)),
                      pl.BlockSpec((1, tl, cc), lambda j, i, c_ref: (j, i, 0))],
            out_specs=pl.BlockSpec((1, tl, cc), lambda j, i, c_ref: (j, i, 0))),
        out_shape=jax.ShapeDtypeStruct(r.shape, BF16),
        compiler_params=_cparams(("parallel", "parallel"), 8 * tl * cc * 4),
    )(core, g, r)


def _reduce_scatter(gs):
    x, y, c = lax.axis_index("x"), lax.axis_index("y"), lax.axis_index("c")
    mine_first = jnp.stack([jnp.where(c == 0, 2 * x + k, 2 * k + y) for k in range(2)]).astype(jnp.int32)
    mine_second = jnp.where(c == 0, y, x).astype(jnp.int32).reshape(1)
    swapped = _pair_swap("rs_pair_swap", gs)
    parts = [_add_own_half("rs_add_pair", g, r) for g, r in zip(gs, swapped)]
    got = _halve_first("rs_halve_first", parts)
    kept = [_add_selected("rs_add_first", t, r, mine_first, BF16) for t, r in zip(parts, got)]
    got = _halve_second("rs_halve_second", kept)
    reds = [_add_selected("rs_add_second", t, r, mine_second, F32) for t, r in zip(kept, got)]
    return _pair_share("rs_pair_share", reds)


def _adamw(name, g, w, m, v):
    c1 = 1.0 - ADAM_B1 ** ADAM_STEP
    c2 = 1.0 - ADAM_B2 ** ADAM_STEP
    nb, rows, cc = g.shape
    tl = _tile(rows, ROW_TILE, SUBLANE)

    def body(g_ref, w_ref, m_ref, v_ref, d_ref, mo_ref, vo_ref):
        gv = g_ref[...]
        mn = ADAM_B1 * m_ref[...] + (1.0 - ADAM_B1) * gv
        vn = ADAM_B2 * v_ref[...] + (1.0 - ADAM_B2) * (gv * gv)
        d_ref[...] = -ADAM_LR * ((mn / c1) / (jnp.sqrt(vn / c2) + ADAM_EPS) + ADAM_WD * w_ref[...])
        mo_ref[...] = mn
        vo_ref[...] = vn

    spec = pl.BlockSpec((1, tl, cc), lambda b, i: (b, i, 0))
    return _pcall(
        body, name=name, grid=(nb, rows // tl), in_specs=[spec] * 4, out_specs=[spec] * 3,
        out_shape=[jax.ShapeDtypeStruct(g.shape, F32)] * 3,
        compiler_params=_cparams(("parallel", "parallel"), 16 * tl * cc * 4),
    )(g, w, m, v)


def _train_step(a):
    x, target = a['x'][0], a['loss_target'][0]

    names = BF16_GATHER + F32_GATHER
    got = _ag_chips("ag_weights", [_stored(n, a[n]).astype(BF16) for n in BF16_GATHER] + [a[n] for n in F32_GATHER])
    ws = dict(zip(names, got))

    loss, gx, gs, grep = _device_step(x, target, ws, {n: a[n] for n in REPLICATED})
    loss = lax.psum(loss, ("x", "y", "c"))

    small = [n for n in MISC_SHARDED if n not in LARGE_SHARDED]
    small_shapes = [a[n].shape[1:] for n in small]
    rep_shapes = [a[n].shape for n in REPLICATED]
    small_rows = sum(_rows_of(s) for s in small_shapes)
    piece = _round_up(-(-sum(_rows_of(s) for s in rep_shapes) // (2 * N_CHIPS)), SUBLANE)
    rep_rows = 2 * N_CHIPS * piece
    rep_pack = _pack([grep[n] for n in REPLICATED], rep_rows).reshape(N_CHIPS, 2, piece, PACK_C)
    total = _round_up(small_rows + piece, ROW_TILE // 2)
    misc = _pack([gs[n] for n in small] + [rep_pack], total, lead=2)
    reds = _reduce_scatter([gs[n] for n in LARGE_SHARDED] + [misc])
    grad = dict(zip(LARGE_SHARDED, reds[:-1]))
    grad.update(zip(small, _unpack(reds[-1], small_shapes, lead=1)))
    rep_mine = reds[-1][:, small_rows:small_rows + piece]
    rep_all = _ag_chips("ag_replicated", [rep_mine])[0].reshape(1, rep_rows, PACK_C)

    outs = {}
    for n in SHARDED:
        shp = grad[n].shape
        three_d = (shp[0], -1, shp[-1])
        params = (grad[n], *(_stored(n, a[p + n]) for p in ('', 'm_', 'v_')))
        res3 = _adamw("adamw", *(t.reshape(three_d) for t in params))
        for kind, arr in zip(('grad', 'delta', 'new_m', 'new_v'), (grad[n], *res3)):
            outs[kind + '_' + n] = _stored(n, arr.reshape(shp))
    w, m, v = (_pack([a[p + n] for n in REPLICATED], rep_rows)[None] for p in ('', 'm_', 'v_'))
    res3 = _adamw("adamw_replicated", rep_all, w, m, v)
    for kind, arr in zip(('grad', 'delta', 'new_m', 'new_v'), (rep_all, *res3)):
        for n, t in zip(REPLICATED, _unpack(arr[0], rep_shapes)):
            outs[kind + '_' + n] = t
    res = [loss, gx[None]]
    for kind in ('grad', 'delta', 'new_m', 'new_v'):
        res += [outs[kind + '_' + n] for n in W_NAMES]
    return tuple(res)


def kernel(x, ffn1_norm, ffn1_w_gate, ffn1_w_up, ffn1_w_down, mix_norm, w_in, s5_lambda_re, s5_lambda_im, s5_log_dt, s5_b_re, s5_b_im, s5_c_re, s5_c_im, s5_d, s5_w_glu, gla_w_alpha, gla_b_alpha, gla_norm, attn_q_norm, attn_k_norm, w_branch_s5, w_branch_gla, w_branch_attn, w_merge_gate, b_merge_gate, w_out, ffn2_norm, ffn2_w_gate, ffn2_w_up, ffn2_w_down, final_norm, loss_target, m_ffn1_norm, m_ffn1_w_gate, m_ffn1_w_up, m_ffn1_w_down, m_mix_norm, m_w_in, m_s5_lambda_re, m_s5_lambda_im, m_s5_log_dt, m_s5_b_re, m_s5_b_im, m_s5_c_re, m_s5_c_im, m_s5_d, m_s5_w_glu, m_gla_w_alpha, m_gla_b_alpha, m_gla_norm, m_attn_q_norm, m_attn_k_norm, m_w_branch_s5, m_w_branch_gla, m_w_branch_attn, m_w_merge_gate, m_b_merge_gate, m_w_out, m_ffn2_norm, m_ffn2_w_gate, m_ffn2_w_up, m_ffn2_w_down, m_final_norm, v_ffn1_norm, v_ffn1_w_gate, v_ffn1_w_up, v_ffn1_w_down, v_mix_norm, v_w_in, v_s5_lambda_re, v_s5_lambda_im, v_s5_log_dt, v_s5_b_re, v_s5_b_im, v_s5_c_re, v_s5_c_im, v_s5_d, v_s5_w_glu, v_gla_w_alpha, v_gla_b_alpha, v_gla_norm, v_attn_q_norm, v_attn_k_norm, v_w_branch_s5, v_w_branch_gla, v_w_branch_attn, v_w_merge_gate, v_b_merge_gate, v_w_out, v_ffn2_norm, v_ffn2_w_gate, v_ffn2_w_up, v_ffn2_w_down, v_final_norm):
    return _train_step(dict(locals()))
```

```python
import functools
import math

import jax
import jax.numpy as jnp
import numpy as np
from jax import lax
from jax.experimental import pallas as pl
from jax.experimental.pallas import tpu as pltpu

F32 = jnp.float32
BF16 = jnp.bfloat16
HI = lax.Precision.HIGHEST

D_MODEL = 1024
D_FF = 2816
DEPTH = 2
EPS = 1e-6
S5_G, S5_H, S5_P = 32, 16, 64
S5_W = S5_G * S5_H
S5_N = S5_G * S5_P
S5_TG = 8
S5_TILES = S5_G // S5_TG
S5_TC, S5_TS = S5_TG * S5_H, S5_TG * S5_P
GLA_H, GLA_D = 4, 128
GLA_W = GLA_H * GLA_D
GLA_R = 16
GLA_TAU = 16.0
GLA_C = 64
AT_QH, AT_KH, AT_D = 8, 2, 64
AT_W = AT_QH * AT_D
AT_KW = AT_KH * AT_D
GRID_W = 64
ROPE_BASE = 10000.0
ZP_W = 3456
ZC_AK, ZC_AV, ZC_Z = 24, 25, 26

ADAM_LR, ADAM_B1, ADAM_B2, ADAM_EPS, ADAM_WD, ADAM_STEP = 0.001, 0.9, 0.999, 1e-08, 0.01, 10

VMEM_LIMIT_V7X = 56 * 1024 * 1024
LANE = 128
SUBLANE = 8

W_NAMES = ['ffn1_norm', 'ffn1_w_gate', 'ffn1_w_up', 'ffn1_w_down', 'mix_norm', 'w_in', 's5_lambda_re', 's5_lambda_im',
           's5_log_dt', 's5_b_re', 's5_b_im', 's5_c_re', 's5_c_im', 's5_d', 's5_w_glu', 'gla_w_alpha', 'gla_b_alpha',
           'gla_norm', 'attn_q_norm', 'attn_k_norm', 'w_branch_s5', 'w_branch_gla', 'w_branch_attn', 'w_merge_gate',
           'b_merge_gate', 'w_out', 'ffn2_norm', 'ffn2_w_gate', 'ffn2_w_up', 'ffn2_w_down', 'final_norm']
SHARD_AXIS = {'ffn1_w_gate': 2, 'ffn1_w_up': 2, 'ffn1_w_down': 1, 'w_in': 2, 's5_w_glu': 1, 'gla_w_alpha': 3,
              'gla_b_alpha': 2, 'w_branch_s5': 2, 'w_branch_gla': 2, 'w_branch_attn': 2, 'w_merge_gate': 2,
              'w_out': 1, 'ffn2_w_gate': 2, 'ffn2_w_up': 2, 'ffn2_w_down': 1}
F32_GATHER = ['gla_w_alpha', 'gla_b_alpha']
BF16_GATHER = [n for n in W_NAMES if n in SHARD_AXIS and n not in F32_GATHER]
SHARDED = [n for n in W_NAMES if n in SHARD_AXIS]
REPLICATED = [n for n in W_NAMES if n not in SHARD_AXIS]
N_CHIPS = 4
MESH = pl.DeviceIdType.MESH


def _tile(n, target, mult):
    best = None
    for t in range(mult, min(n, target) + 1, mult):
        if n % t == 0:
            best = t
    return best if best is not None else n


def _cparams(sem, vmem_bytes):
    limit = int(min(VMEM_LIMIT_V7X, max(32 * 1024 * 1024, vmem_bytes * 5 // 4)))
    return pltpu.CompilerParams(dimension_semantics=sem, vmem_limit_bytes=limit)


def _pcall(body, **kw):
    shapes = kw.pop('out_shape')
    many = isinstance(shapes, (list, tuple))
    pinned = [pltpu.HBM(t.shape, t.dtype) for t in (shapes if many else [shapes])]
    call = pl.pallas_call(body, out_shape=pinned if many else pinned[0], **kw)

    def run(*ops):
        return call(*[o if jnp.issubdtype(o.dtype, jnp.integer) else pltpu.with_memory_space_constraint(o, pltpu.HBM)
                      for o in ops])

    return run


def _cw(arr, cb, width):
    return (arr, cb, width)


def _win(a):
    if isinstance(a, tuple):
        return a
    return (a, 0, a.shape[-1])


def _nbytes(shape, dtype):
    return int(np.prod(shape)) * jnp.dtype(dtype).itemsize


def _rowwise(name, fn, rows, fulls, outs, accs=(), tl=256):
    rows = [_win(r) for r in rows]
    n_tok = rows[0][0].shape[0]
    tl = _tile(n_tok, tl, 16)
    nr, nf, no = len(rows), len(fulls), len(outs)

    def body(*refs):
        vals = [r[...] for r in refs[:nr + nf]]
        res = fn(*vals)
        if not isinstance(res, (tuple, list)):
            res = (res,)
        for k in range(no):
            o = refs[nr + nf + k]
            o[...] = res[k].astype(o.dtype)
        if accs:
            @pl.when(pl.program_id(0) == 0)
            def _():
                for k in range(len(accs)):
                    a = refs[nr + nf + no + k]
                    a[...] = jnp.zeros(a.shape, a.dtype)
            for k in range(len(accs)):
                a = refs[nr + nf + no + k]
                a[...] += res[no + k]

    in_specs = [pl.BlockSpec((tl, w), functools.partial(lambda i, cb: (i, cb), cb=cb)) for (_, cb, w) in rows]
    in_specs += [pl.BlockSpec(f.shape, functools.partial(lambda i, nd: (0,) * nd, nd=f.ndim)) for f in fulls]
    out_specs = [pl.BlockSpec((tl, c), lambda i: (i, 0)) for (c, _) in outs]
    out_specs += [pl.BlockSpec((r, c), lambda i: (0, 0)) for (r, c) in accs]
    out_shape = [jax.ShapeDtypeStruct((n_tok, c), dt) for (c, dt) in outs]
    out_shape += [jax.ShapeDtypeStruct((r, c), F32) for (r, c) in accs]
    vm = 2 * sum(_nbytes((tl, w), a.dtype) for (a, _, w) in rows)
    vm += 2 * sum(_nbytes(f.shape, f.dtype) for f in fulls)
    vm += 2 * sum(_nbytes((tl, c), dt) for (c, dt) in outs)
    vm += 8 * max([_nbytes((tl, w), F32) for (_, _, w) in rows] + [_nbytes((tl, c), F32) for (c, _) in outs])
    return _pcall(
        body, name=name, grid=(n_tok // tl,), in_specs=in_specs, out_specs=out_specs, out_shape=out_shape,
        compiler_params=_cparams(("arbitrary",) if accs else ("parallel",), vm),
    )(*[r[0] for r in rows], *fulls)


def _mm(name, a, b, *, ta=False, tb=False, out_dtype=F32, alpha=1.0, add=None, tm=512, tn=512):
    a_arr, a_cb, a_w = _win(a)
    b_arr, b_cb, b_w = _win(b)
    if ta:
        kdim, m = a_arr.shape[0], a_w
    else:
        m, kdim = a_arr.shape[0], a_w
    if tb:
        n, kb = b_arr.shape[0], b_w
    else:
        kb, n = b_arr.shape[0], b_w
    assert kdim == kb, (name, kdim, kb)
    tm = _tile(m, tm, LANE if ta else 16)
    tn = _tile(n, tn, LANE)
    if ta:
        a_spec = pl.BlockSpec((kdim, tm), lambda i, j: (0, a_cb * (a_w // tm) + i))
        a_blk = (kdim, tm)
    else:
        a_spec = pl.BlockSpec((tm, kdim), lambda i, j: (i, a_cb))
        a_blk = (tm, kdim)
    if tb:
        b_spec = pl.BlockSpec((tn, kdim), lambda i, j: (j, b_cb))
        b_blk = (tn, kdim)
    else:
        b_spec = pl.BlockSpec((kdim, tn), lambda i, j: (0, b_cb * (b_w // tn) + j))
        b_blk = (kdim, tn)
    dims = (((0 if ta else 1,), (1 if tb else 0,)), ((), ()))
    has_add = add is not None

    def body(*refs):
        a_ref, b_ref = refs[0], refs[1]
        o_ref = refs[-1]
        acc = lax.dot_general(a_ref[...].astype(BF16), b_ref[...].astype(BF16), dims, preferred_element_type=F32)
        if alpha != 1.0:
            acc = acc * alpha
        if has_add:
            acc = acc + refs[2][...].astype(F32)
        o_ref[...] = acc.astype(o_ref.dtype)

    in_specs = [a_spec, b_spec]
    ops = [a_arr, b_arr]
    if has_add:
        in_specs.append(pl.BlockSpec((tm, tn), lambda i, j: (i, j)))
        ops.append(add)
    vm = 2 * (_nbytes(a_blk, a_arr.dtype) + _nbytes(b_blk, b_arr.dtype) + _nbytes((tm, tn), out_dtype))
    vm += _nbytes(a_blk, BF16) + _nbytes(b_blk, BF16) + 3 * _nbytes((tm, tn), F32)
    return _pcall(
        body, name=name, grid=(m // tm, n // tn), in_specs=in_specs,
        out_specs=pl.BlockSpec((tm, tn), lambda i, j: (i, j)),
        out_shape=jax.ShapeDtypeStruct((m, n), out_dtype),
        compiler_params=_cparams(("parallel", "parallel"), vm),
    )(*ops)


def _sigmoid(x):
    return 1.0 / (1.0 + jnp.exp(-x))


def _rms(x, gain):
    r = lax.rsqrt(jnp.mean(x * x, axis=-1, keepdims=True) + EPS)
    return x * r * gain


def _rms_bwd(x, gain, dh):
    r = lax.rsqrt(jnp.mean(x * x, axis=-1, keepdims=True) + EPS)
    xh = x * r
    dg = jnp.sum(dh * xh, axis=0, keepdims=True)
    dxh = dh * gain
    dx = r * (dxh - xh * jnp.mean(dxh * xh, axis=-1, keepdims=True))
    return dx, dg


_GELU_C = math.sqrt(2.0 / math.pi)


def _gelu(y):
    return 0.5 * y * (1.0 + jnp.tanh(_GELU_C * (y + 0.044715 * y * y * y)))


def _gelu_grad(y):
    th = jnp.tanh(_GELU_C * (y + 0.044715 * y * y * y))
    return 0.5 * (1.0 + th) + 0.5 * y * (1.0 - th * th) * _GELU_C * (1.0 + 3.0 * 0.044715 * y * y)


def _log_sigmoid(x):
    return jnp.minimum(x, 0.0) - jnp.log(1.0 + jnp.exp(-jnp.abs(x)))


def _dot(a, b, dims=(((1,), (0,)), ((), ())), precision=None):
    return lax.dot_general(a, b, dims, preferred_element_type=F32, precision=precision)


_NT = (((1,), (1,)), ((), ()))
_TN = (((0,), (0,)), ((), ()))


def _scan_core(b_ref, are_ref, aim_ref, pw, last, of32, *, rev, xf32=None, da_ref=None, lc=512):
    nb, nc = b_ref.shape[0] // SUBLANE, S5_N // lc
    row = lax.broadcasted_iota(jnp.int32, (SUBLANE, lc), 0)

    def cmul(ar, ai, xr, xi):
        return ar * xr - ai * xi, ar * xi + ai * xr

    def hs(c, xr, xi):
        cs = pl.ds(c * lc, lc)
        for k, slot in ((1, 0), (2, 2), (4, 4)):
            ar, ai = pw[slot, :, cs], pw[slot + 1, :, cs]
            if rev:
                sr, si = pltpu.roll(xr, SUBLANE - k, 0), pltpu.roll(xi, SUBLANE - k, 0)
                keep = row < SUBLANE - k
            else:
                sr, si = pltpu.roll(xr, k, 0), pltpu.roll(xi, k, 0)
                keep = row >= k
            sr, si = jnp.where(keep, sr, 0.0), jnp.where(keep, si, 0.0)
            pr, pi = cmul(ar, ai, sr, si)
            xr, xi = xr + pr, xi + pi
        return xr, xi

    @pl.when(pl.program_id(0) == 0)
    def _():
        last[...] = jnp.zeros(last.shape, F32)
        if da_ref is not None:
            da_ref[...] = jnp.zeros(da_ref.shape, F32)
        for c in range(nc):
            cs = pl.ds(c * lc, lc)
            a1r = jnp.broadcast_to(are_ref[:, cs], (SUBLANE, lc))
            a1i = jnp.broadcast_to(aim_ref[:, cs], (SUBLANE, lc))
            a2r, a2i = cmul(a1r, a1i, a1r, a1i)
            a4r, a4i = cmul(a2r, a2i, a2r, a2i)
            for slot, v in enumerate((a1r, a1i, a2r, a2i, a4r, a4i)):
                pw[slot, :, cs] = v
            first = SUBLANE - 1 if rev else 0
            pr, pi = hs(c, jnp.where(row == first, a1r, 0.0), jnp.where(row == first, a1i, 0.0))
            pw[6, :, cs] = pr
            pw[7, :, cs] = pi

    edge = 0 if rev else SUBLANE - 1

    def blk(j, carry):
        jj = (nb - 1 - j) if rev else j
        r0 = pl.multiple_of(jj * SUBLANE, SUBLANE)
        for c in range(nc):
            cre, cim = pl.ds(c * lc, lc), pl.ds(S5_N + c * lc, lc)
            xr, xi = hs(c, b_ref[pl.ds(r0, SUBLANE), cre], b_ref[pl.ds(r0, SUBLANE), cim])
            cr = jnp.broadcast_to(last[edge:edge + 1, cre], (SUBLANE, lc))
            ci = jnp.broadcast_to(last[edge:edge + 1, cim], (SUBLANE, lc))
            pr, pi = cmul(pw[6, :, cre], pw[7, :, cre], cr, ci)
            xr, xi = xr + pr, xi + pi
            if da_ref is not None:
                if rev:
                    qr, qi = pltpu.roll(xr, SUBLANE - 1, 0), pltpu.roll(xi, SUBLANE - 1, 0)
                    fill = row == SUBLANE - 1
                else:
                    qr, qi = pltpu.roll(xr, 1, 0), pltpu.roll(xi, 1, 0)
                    fill = row == 0
                qr, qi = jnp.where(fill, cr, qr), jnp.where(fill, ci, qi)
                zr, zi = xf32[pl.ds(r0, SUBLANE), cre], xf32[pl.ds(r0, SUBLANE), cim]
                da_ref[:, cre] += zr * qr + zi * qi
                da_ref[:, cim] += zr * qi - zi * qr
            last[:, cre] = xr
            last[:, cim] = xi
            of32[pl.ds(r0, SUBLANE), cre] = xr
            of32[pl.ds(r0, SUBLANE), cim] = xi
        return carry

    lax.fori_loop(0, nb, blk, 0)


def _s5_tiles():
    return [(slice((j % S5_TILES) * S5_TC, (j % S5_TILES + 1) * S5_TC), slice(j * S5_TS, (j + 1) * S5_TS))
            for j in range(2 * S5_TILES)]


def _s5_scratch(tb):
    return [pltpu.VMEM((8, SUBLANE, S5_N), F32), pltpu.VMEM((SUBLANE, 2 * S5_N), F32),
            pltpu.VMEM((tb, 2 * S5_N), F32), pltpu.VMEM((tb, 2 * S5_N), F32),
            pltpu.VMEM((S5_TC, 2 * S5_N), BF16), pltpu.VMEM((2 * S5_N, S5_TC), BF16)]


def _s5_fwd(name, z, a_re, a_im, wb, wc, add, *, rev, tb=256):
    n_tok = z.shape[0]
    tb = _tile(n_tok, tb, 16)
    nt = n_tok // tb
    has_add = add is not None

    def body(*refs):
        u_ref, are_ref, aim_ref, wb_ref, wc_ref = refs[:5]
        x_ref, y_ref, pw, last, of32, bu, wb16, wc16 = refs[5 + has_add:]

        @pl.when(pl.program_id(0) == 0)
        def _():
            wb16[...] = wb_ref[...].astype(BF16)
            wc16[...] = wc_ref[...].astype(BF16)

        u16 = u_ref[...].astype(BF16)
        for ch, st in _s5_tiles():
            bu[:, st] = _dot(u16[:, ch], wb16[:, st])
        _scan_core(bu, are_ref, aim_ref, pw, last, of32, rev=rev)
        x16 = of32[...].astype(BF16)
        x_ref[...] = x16
        ys = []
        for k in range(S5_TILES):
            re, im = slice(k * S5_TS, (k + 1) * S5_TS), slice(S5_N + k * S5_TS, S5_N + (k + 1) * S5_TS)
            ys.append(_dot(x16[:, re], wc16[re, :]) + _dot(x16[:, im], wc16[im, :]))
        y = jnp.concatenate(ys, axis=1)
        y_ref[...] = y + refs[5][...] if has_add else y

    tmap = (lambda t: (nt - 1 - t, 0)) if rev else (lambda t: (t, 0))
    const = lambda t: (0, 0)
    in_specs = [pl.BlockSpec((tb, S5_W), tmap), pl.BlockSpec((1, S5_N), const), pl.BlockSpec((1, S5_N), const),
                pl.BlockSpec((S5_TC, 2 * S5_N), const), pl.BlockSpec((2 * S5_N, S5_TC), const)]
    ops = [z, a_re, a_im, wb, wc]
    if has_add:
        in_specs.append(pl.BlockSpec((tb, S5_W), tmap))
        ops.append(add)
    return _pcall(
        body, name=name, grid=(nt,), in_specs=in_specs,
        out_specs=[pl.BlockSpec((tb, 2 * S5_N), tmap), pl.BlockSpec((tb, S5_W), tmap)],
        out_shape=[jax.ShapeDtypeStruct((n_tok, 2 * S5_N), BF16), jax.ShapeDtypeStruct((n_tok, S5_W), F32)],
        scratch_shapes=_s5_scratch(tb), compiler_params=_cparams(("arbitrary",), 10 * tb * 2 * S5_N * 4),
    )(*ops)


def _s5_bwd(name, dy, z, xs, a_re, a_im, wb, wc, add, *, rev, tb=256):
    n_tok = z.shape[0]
    tb = _tile(n_tok, tb, 16)
    nt = n_tok // tb

    def body(dy_ref, u_ref, xs_ref, are_ref, aim_ref, wb_ref, wc_ref, add_ref, du_ref, dwb_ref, dwc_ref, da_ref,
             pw, last, of32, gx, wb16, wc16, xf32):
        @pl.when(pl.program_id(0) == 0)
        def _():
            wb16[...] = wb_ref[...].astype(BF16)
            wc16[...] = wc_ref[...].astype(BF16)
            dwb_ref[...] = jnp.zeros(dwb_ref.shape, F32)
            dwc_ref[...] = jnp.zeros(dwc_ref.shape, F32)

        dy16, u16, x16 = dy_ref[...].astype(BF16), u_ref[...].astype(BF16), xs_ref[...]
        for ch, st in _s5_tiles():
            gx[:, st] = _dot(dy16[:, ch], wc16[st, :], _NT)
        xf32[...] = x16.astype(F32)
        _scan_core(gx, are_ref, aim_ref, pw, last, of32, rev=rev, xf32=xf32, da_ref=da_ref)
        g16 = of32[...].astype(BF16)
        dus = []
        for k in range(S5_TILES):
            re, im = slice(k * S5_TS, (k + 1) * S5_TS), slice(S5_N + k * S5_TS, S5_N + (k + 1) * S5_TS)
            dus.append(_dot(g16[:, re], wb16[:, re], _NT) + _dot(g16[:, im], wb16[:, im], _NT))
        du_ref[...] = jnp.concatenate(dus, axis=1) + add_ref[...]
        for ch, st in _s5_tiles():
            dwb_ref[:, st] += _dot(u16[:, ch], g16[:, st], _TN)
            dwc_ref[st, :] += _dot(x16[:, st], dy16[:, ch], _TN)

    tmap = (lambda t: (nt - 1 - t, 0)) if rev else (lambda t: (t, 0))
    const = lambda t: (0, 0)
    row_w, row_s = pl.BlockSpec((tb, S5_W), tmap), pl.BlockSpec((tb, 2 * S5_N), tmap)
    return _pcall(
        body, name=name, grid=(nt,),
        in_specs=[row_w, row_w, row_s, pl.BlockSpec((1, S5_N), const), pl.BlockSpec((1, S5_N), const),
                  pl.BlockSpec((S5_TC, 2 * S5_N), const), pl.BlockSpec((2 * S5_N, S5_TC), const), row_w],
        out_specs=[row_w, pl.BlockSpec((S5_TC, 2 * S5_N), const), pl.BlockSpec((2 * S5_N, S5_TC), const),
                   pl.BlockSpec((SUBLANE, 2 * S5_N), const)],
        out_shape=[jax.ShapeDtypeStruct((n_tok, S5_W), F32), jax.ShapeDtypeStruct((S5_TC, 2 * S5_N), F32),
                   jax.ShapeDtypeStruct((2 * S5_N, S5_TC), F32), jax.ShapeDtypeStruct((SUBLANE, 2 * S5_N), F32)],
        scratch_shapes=_s5_scratch(tb) + [pltpu.VMEM((tb, 2 * S5_N), F32)],
        compiler_params=_cparams(("arbitrary",), 12 * tb * 2 * S5_N * 4),
    )(dy, z, xs, a_re, a_im, wb, wc, add)


GLA_TB = 512


def _gla_block_terms(la, q, k, rev):
    tb = la.shape[0]
    ri = lax.broadcasted_iota(jnp.int32, (tb, tb), 0)
    ci = lax.broadcasted_iota(jnp.int32, (tb, tb), 1)
    same = (ri // GLA_C) == (ci // GLA_C)
    tri = same & ((ri <= ci) if rev else (ri >= ci))
    trif = tri.astype(F32)
    bcum = _dot(trif, la, precision=HI)
    blast = _dot(same.astype(F32), la, precision=HI)
    eb, enb, ee = jnp.exp(bcum), jnp.exp(-bcum), jnp.exp(blast - bcum)
    return trif, jnp.exp(blast), eb, enb, ee, q * eb, k * enb, k * ee


def _chunk_mask(rev):
    ri = lax.broadcasted_iota(jnp.int32, (GLA_C, GLA_C), 0)
    ci = lax.broadcasted_iota(jnp.int32, (GLA_C, GLA_C), 1)
    return (ri <= ci) if rev else (ri >= ci)


def _gla_fwd(name, z, la, *, rev):
    n_tok = z.shape[0]
    tb = _tile(n_tok, GLA_TB, GLA_C)
    nblk, per = n_tok // tb, tb // GLA_C
    scale = GLA_D ** -0.5

    def body(q_ref, k_ref, v_ref, la_ref, o_ref, s_ref, st, qd_s, kd_s, ke_s, v_s, dec_s):
        @pl.when(pl.program_id(0) == 0)
        def _():
            st[...] = jnp.zeros(st.shape, F32)

        _, dec, _, _, _, qd, kd, ke = _gla_block_terms(la_ref[...], q_ref[...] * scale, k_ref[...], rev)
        qd_s[...], kd_s[...], ke_s[...] = qd.astype(BF16), kd.astype(BF16), ke.astype(BF16)
        v_s[...] = v_ref[...].astype(BF16)
        dec_s[...] = dec
        tri = _chunk_mask(rev)
        for step in range(per):
            t = per - 1 - step if rev else step
            rows = pl.ds(t * GLA_C, GLA_C)
            for h in range(GLA_H):
                lanes = pl.ds(h * GLA_D, GLA_D)
                qd16, v16 = qd_s[rows, lanes], v_s[rows, lanes]
                sc = jnp.where(tri, _dot(qd16, kd_s[rows, lanes], _NT), 0.0)
                s16 = st[h].astype(BF16)
                s_ref[h, step] = s16
                o_ref[rows, lanes] = _dot(sc.astype(BF16), v16) + _dot(qd16, s16, _NT)
                st[h] = st[h] * dec_s[pl.ds(t * GLA_C, 1), lanes] + _dot(v16, ke_s[rows, lanes], _TN)

    bmap = (lambda n: nblk - 1 - n) if rev else (lambda n: n)

    def zspec(cb):
        return pl.BlockSpec((tb, GLA_W), lambda n: (bmap(n), cb))

    half = pltpu.VMEM((tb, GLA_W), BF16)
    return _pcall(
        body, name=name, grid=(nblk,),
        in_specs=[zspec(1), zspec(2), zspec(3), zspec(0)],
        out_specs=[zspec(0), pl.BlockSpec((GLA_H, per, GLA_D, GLA_D), lambda n: (0, n, 0, 0))],
        out_shape=[jax.ShapeDtypeStruct((n_tok, GLA_W), F32),
                   jax.ShapeDtypeStruct((GLA_H, n_tok // GLA_C, GLA_D, GLA_D), BF16)],
        scratch_shapes=[pltpu.VMEM((GLA_H, GLA_D, GLA_D), F32), half, half, half, half, pltpu.VMEM((tb, GLA_W), F32)],
        compiler_params=_cparams(("arbitrary",), 40 * tb * GLA_W * 4),
    )(z, z, z, la)


def _gla_bwd(name, z, la, s_prev, do, *, rev):
    n_tok = z.shape[0]
    tb = _tile(n_tok, GLA_TB, GLA_C)
    nblk, per = n_tok // tb, tb // GLA_C
    scale = GLA_D ** -0.5

    def body(q_ref, k_ref, v_ref, la_ref, s_ref, do_ref, dq_ref, dk_ref, dv_ref, dla_ref,
             dst, qd_s, kd_s, ke_s, v_s, do_s, dec_s, dqd_s, dkd_s, dke_s, dd_s):
        @pl.when(pl.program_id(0) == 0)
        def _():
            dst[...] = jnp.zeros(dst.shape, F32)

        trif, dec, eb, enb, ee, qd, kd, ke = _gla_block_terms(la_ref[...], q_ref[...] * scale, k_ref[...], rev)
        qd_s[...], kd_s[...], ke_s[...] = qd.astype(BF16), kd.astype(BF16), ke.astype(BF16)
        v_s[...] = v_ref[...].astype(BF16)
        do_s[...] = do_ref[...].astype(BF16)
        dec_s[...] = dec
        tri = _chunk_mask(rev)
        for step in range(per):
            t = step if rev else per - 1 - step
            rows = pl.ds(t * GLA_C, GLA_C)
            for h in range(GLA_H):
                lanes = pl.ds(h * GLA_D, GLA_D)
                qd16, kd16, ke16 = qd_s[rows, lanes], kd_s[rows, lanes], ke_s[rows, lanes]
                v16, do16 = v_s[rows, lanes], do_s[rows, lanes]
                sc16 = jnp.where(tri, _dot(qd16, kd16, _NT), 0.0).astype(BF16)
                dsc16 = jnp.where(tri, _dot(do16, v16, _NT), 0.0).astype(BF16)
                s16 = s_ref[h, per - 1 - step]
                dsn = dst[h]
                dsn16 = dsn.astype(BF16)
                dec_c = dec_s[pl.ds(t * GLA_C, 1), lanes]
                dqd_s[rows, lanes] = _dot(dsc16, kd16) + _dot(do16, s16)
                dkd_s[rows, lanes] = _dot(dsc16, qd16, _TN)
                dke_s[rows, lanes] = _dot(v16, dsn16)
                dv_ref[rows, lanes] = _dot(sc16, do16, _TN) + _dot(ke16, dsn16, _NT)
                ddec = jnp.sum(dsn * s16.astype(F32), axis=0, keepdims=True) * dec_c
                dd_s[rows, lanes] = jnp.broadcast_to(ddec, (GLA_C, GLA_D))
                dst[h] = dsn * dec_c + _dot(do16, qd16, _TN)
        dqd, dkd, dke = dqd_s[...], dkd_s[...], dke_s[...]
        dq_ref[...] = dqd * eb * scale
        dk_ref[...] = dkd * enb + dke * ee
        db = dqd * qd - dkd * kd - dke * ke
        ri = lax.broadcasted_iota(jnp.int32, (tb, tb), 0)
        ci = lax.broadcasted_iota(jnp.int32, (tb, tb), 1)
        same = ((ri // GLA_C) == (ci // GLA_C)).astype(F32)
        dla_ref[...] = _dot(trif, db, _TN, precision=HI) + _dot(same, dke * ke, precision=HI) + dd_s[...]

    bmap = (lambda n: n) if rev else (lambda n: nblk - 1 - n)

    def zspec(cb):
        return pl.BlockSpec((tb, GLA_W), lambda n: (bmap(n), cb))

    o4 = jax.ShapeDtypeStruct((n_tok, GLA_W), F32)
    half, full = pltpu.VMEM((tb, GLA_W), BF16), pltpu.VMEM((tb, GLA_W), F32)
    return _pcall(
        body, name=name, grid=(nblk,),
        in_specs=[zspec(1), zspec(2), zspec(3), zspec(0),
                  pl.BlockSpec((GLA_H, per, GLA_D, GLA_D), lambda n: (0, nblk - 1 - n, 0, 0)), zspec(0)],
        out_specs=[zspec(0)] * 4, out_shape=[o4] * 4,
        scratch_shapes=[pltpu.VMEM((GLA_H, GLA_D, GLA_D), F32), half, half, half, half, half, full, full, full, full, full],
        compiler_params=_cparams(("arbitrary",), 64 * tb * GLA_W * 4),
    )(z, z, z, la, s_prev, do)


_PAIRS = AT_W // LANE
_PAIRS_PER_KV = _PAIRS // AT_KH


def _attn_fwd(name, q, kd, vd, tq=256):
    n_tok = q.shape[0]
    tq = _tile(n_tok, tq, 16)

    def body(q_ref, k_ref, v_ref, o_ref, l_ref):
        qv, kv, vv = q_ref[...], k_ref[...], v_ref[...]
        low = lax.broadcasted_iota(jnp.int32, qv.shape, 1) < AT_D
        res = []
        for keep in (low, jnp.logical_not(low)):
            s = _dot(jnp.where(keep, qv, jnp.zeros_like(qv)), kv, _NT)
            m = jnp.max(s, axis=-1, keepdims=True)
            p = jnp.exp(s - m)
            den = jnp.sum(p, axis=-1, keepdims=True)
            res.append((_dot(p.astype(BF16), vv) / den, m + jnp.log(den)))
        o_ref[...] = jnp.where(low, res[0][0], res[1][0])
        l_ref[...] = jnp.where(low, res[0][1], res[1][1])

    qspec = pl.BlockSpec((tq, LANE), lambda p, i: (i, p))
    kspec = pl.BlockSpec((n_tok, LANE), lambda p, i: (0, p // _PAIRS_PER_KV))
    vm = 5 * _nbytes((tq, n_tok), F32) + 8 * _nbytes((n_tok, LANE), BF16)
    out = jax.ShapeDtypeStruct((n_tok, AT_W), F32)
    return _pcall(
        body, name=name, grid=(_PAIRS, n_tok // tq), in_specs=[qspec, kspec, kspec], out_specs=[qspec, qspec],
        out_shape=[out, out], compiler_params=_cparams(("parallel", "parallel"), vm),
    )(q, kd, vd)


def _attn_bwd(name, q, kd, vd, o, lse, do, tq=256):
    n_tok = q.shape[0]
    tq = _tile(n_tok, tq, 16)

    def body(q_ref, k_ref, v_ref, o_ref, l_ref, do_ref, dq_ref, dk_ref, dv_ref):
        @pl.when((pl.program_id(1) == 0) & (pl.program_id(2) == 0))
        def _():
            dk_ref[...] = jnp.zeros(dk_ref.shape, F32)
            dv_ref[...] = jnp.zeros(dv_ref.shape, F32)

        qv, kv, vv = q_ref[...], k_ref[...], v_ref[...]
        dof, lv = do_ref[...], l_ref[...]
        do16 = dof.astype(BF16)
        doo = dof * o_ref[...]
        low = lax.broadcasted_iota(jnp.int32, qv.shape, 1) < AT_D
        dq = jnp.zeros(qv.shape, F32)
        for keep in (low, jnp.logical_not(low)):
            qm = jnp.where(keep, qv, jnp.zeros_like(qv))
            dom = jnp.where(keep, do16, jnp.zeros_like(do16))
            lh = jnp.max(jnp.where(keep, lv, -1e30), axis=-1, keepdims=True)
            p = jnp.exp(_dot(qm, kv, _NT) - lh)
            delta = jnp.sum(jnp.where(keep, doo, 0.0), axis=-1, keepdims=True)
            ds16 = (p * (_dot(dom, vv, _NT) - delta)).astype(BF16)
            dq = jnp.where(keep, _dot(ds16, kv), dq)
            dk_ref[...] += _dot(ds16, qm, _TN)
            dv_ref[...] += _dot(p.astype(BF16), dom, _TN)
        dq_ref[...] = dq

    qspec = pl.BlockSpec((tq, LANE), lambda g, j, i: (i, g * _PAIRS_PER_KV + j))
    kspec = pl.BlockSpec((n_tok, LANE), lambda g, j, i: (0, g))
    vm = 7 * _nbytes((tq, n_tok), F32) + 12 * _nbytes((n_tok, LANE), F32)
    dup = jax.ShapeDtypeStruct((n_tok, AT_KH * LANE), F32)
    return _pcall(
        body, name=name, grid=(AT_KH, _PAIRS_PER_KV, n_tok // tq),
        in_specs=[qspec, kspec, kspec, qspec, qspec, qspec], out_specs=[qspec, kspec, kspec],
        out_shape=[jax.ShapeDtypeStruct((n_tok, AT_W), F32), dup, dup],
        compiler_params=_cparams(("parallel", "arbitrary", "arbitrary"), vm),
    )(q, kd, vd, o, lse, do)


def _block_diag(t):
    g, a, b = t.shape
    eye = jnp.eye(g, dtype=t.dtype)
    return (t[:, :, None, :] * eye[:, None, :, None]).reshape(g * a, g * b)


def _s5_prep(lam_re, lam_im, log_dt, b_re, b_im, c_re, c_im):
    dt = jnp.exp(log_dt)[:, None]
    er = jnp.exp(lam_re * dt)
    a_re, a_im = er * jnp.cos(lam_im * dt), er * jnp.sin(lam_im * dt)
    nr, ni = a_re - 1.0, a_im
    den = lam_re * lam_re + lam_im * lam_im
    f_re, f_im = (nr * lam_re + ni * lam_im) / den, (ni * lam_re - nr * lam_im) / den
    bb_re = f_re[..., None] * b_re - f_im[..., None] * b_im
    bb_im = f_re[..., None] * b_im + f_im[..., None] * b_re
    def tiles(t, axis):
        return jnp.concatenate([_block_diag(t[k * S5_TG:(k + 1) * S5_TG]) for k in range(S5_TILES)], axis=axis)

    w_in = jnp.concatenate([tiles(bb_re.transpose(0, 2, 1), 1), tiles(bb_im.transpose(0, 2, 1), 1)], axis=1)
    w_out = jnp.concatenate([tiles(c_re.transpose(0, 2, 1), 0), -tiles(c_im.transpose(0, 2, 1), 0)], axis=0)
    return a_re.reshape(1, S5_N), a_im.reshape(1, S5_N), w_in, w_out


def _rope_tables(n_tok):
    quarter = AT_D // 4
    d = np.arange(AT_D)
    e = d % (AT_D // 2)
    inv = (ROPE_BASE ** (-(e % quarter).astype(np.float64) * 2.0 / (AT_D // 2))).astype(np.float32)
    sign = np.where(e < quarter, -1.0, 1.0).astype(np.float32)
    rot = np.zeros((AT_D, AT_D), np.float32)
    rot[np.where(e < quarter, d + quarter, d - quarter), d] = 1.0
    pos = jnp.arange(n_tok, dtype=jnp.int32)
    axis_pos = jnp.where(jnp.asarray(d // (AT_D // 2) == 0)[None, :], (pos // GRID_W)[:, None], (pos % GRID_W)[:, None])
    ang = axis_pos.astype(F32) * jnp.asarray(inv)[None, :]
    return jnp.cos(ang), jnp.sin(ang) * jnp.asarray(sign)[None, :], rot


def _head_consts(n_tok, heads):
    cos, sin, rot = _rope_tables(n_tok)
    eye = np.eye(heads, dtype=np.float32)
    mean = np.kron(eye, np.full((AT_D, AT_D), 1.0 / AT_D, np.float32))
    return jnp.tile(cos, (1, heads)), jnp.tile(sin, (1, heads)), jnp.asarray(mean), jnp.asarray(np.kron(eye, rot))


def _pad_w_in(w):
    return jnp.concatenate([w[:, :2560], w[:, 2592:3360], w[:, 2560:2592], jnp.zeros((w.shape[0], 96), w.dtype)], axis=1)


def _unpad_w_in(g):
    return jnp.concatenate([g[:, :2560], g[:, 3328:3360], g[:, 2560:3328]], axis=1)


def _ffn_fwd(x, g, wg, wu, wd, d):
    n_tok = x.shape[0]
    fq = wd.shape[-2]
    h = _rowwise("ffn_norm", lambda xv, gv: _rms(xv, gv), [x], [g], [(D_MODEL, BF16)])[0]
    tm = _tile(n_tok, 512, 16)

    def up(h_ref, wg_ref, wu_ref, a_ref, b_ref, s_ref):
        hv = h_ref[...]
        av, bv = _dot(hv, wg_ref[0, 0], _NT), _dot(hv, wu_ref[0, 0], _NT)
        a_ref[0] = av.astype(BF16)
        b_ref[0] = bv.astype(BF16)
        s_ref[0] = (av * _sigmoid(av) * bv).astype(BF16)

    wspec = pl.BlockSpec((1, 1, fq, D_MODEL), lambda i, j: (j, d, 0, 0))
    aspec = pl.BlockSpec((1, tm, fq), lambda i, j: (j, i, 0))
    act = jax.ShapeDtypeStruct((N_CHIPS, n_tok, fq), BF16)
    a, b, s = _pcall(
        up, name="ffn_up", grid=(n_tok // tm, N_CHIPS),
        in_specs=[pl.BlockSpec((tm, D_MODEL), lambda i, j: (i, 0)), wspec, wspec],
        out_specs=[aspec, aspec, aspec], out_shape=[act, act, act],
        compiler_params=_cparams(("parallel", "parallel"), 4 * D_MODEL * fq * 2 + 16 * tm * fq * 4),
    )(h, wg, wu)
    tn = _tile(D_MODEL, 512, LANE)

    def down(s_ref, w_ref, x_ref, o_ref):
        acc = _dot(s_ref[0], w_ref[0, 0])
        for j in range(1, N_CHIPS):
            acc = acc + _dot(s_ref[j], w_ref[j, 0])
        o_ref[...] = x_ref[...] + 0.5 * acc

    xo = _pcall(
        down, name="ffn_down", grid=(n_tok // tm, D_MODEL // tn),
        in_specs=[pl.BlockSpec((N_CHIPS, tm, fq), lambda i, n: (0, i, 0)),
                  pl.BlockSpec((N_CHIPS, 1, fq, tn), lambda i, n: (0, d, 0, n)),
                  pl.BlockSpec((tm, tn), lambda i, n: (i, n))],
        out_specs=pl.BlockSpec((tm, tn), lambda i, n: (i, n)), out_shape=jax.ShapeDtypeStruct((n_tok, D_MODEL), F32),
        compiler_params=_cparams(("parallel", "parallel"), 4 * N_CHIPS * (tm + tn) * fq * 2 + 6 * tm * tn * 4),
    )(s, wd, x)
    return xo, (x, h, a, b, s)


def _ffn_bwd(dy, dy16, saved, g, wg, wu, wd, d, into):
    x, h, a, b, s = saved
    n_tok = x.shape[0]
    fq = wd.shape[-2]
    tm = _tile(n_tok, 512, 16)

    def dact(dy_ref, w_ref, a_ref, b_ref, da_ref, db_ref):
        ds = 0.5 * _dot(dy_ref[...], w_ref[0, 0], _NT)
        av, bv = a_ref[0].astype(F32), b_ref[0].astype(F32)
        sg = _sigmoid(av)
        da_ref[0] = (ds * bv * sg * (1.0 + av * (1.0 - sg))).astype(BF16)
        db_ref[0] = (ds * av * sg).astype(BF16)

    aspec = pl.BlockSpec((1, tm, fq), lambda i, j: (j, i, 0))
    act = jax.ShapeDtypeStruct((N_CHIPS, n_tok, fq), BF16)
    da, db = _pcall(
        dact, name="ffn_dact", grid=(n_tok // tm, N_CHIPS),
        in_specs=[pl.BlockSpec((tm, D_MODEL), lambda i, j: (i, 0)),
                  pl.BlockSpec((1, 1, fq, D_MODEL), lambda i, j: (j, d, 0, 0)), aspec, aspec],
        out_specs=[aspec, aspec], out_shape=[act, act],
        compiler_params=_cparams(("parallel", "parallel"), 4 * D_MODEL * fq * 2 + 16 * tm * fq * 4),
    )(dy16, wd, a, b)

    any_spec = pl.BlockSpec(memory_space=pl.ANY)
    alias = {} if into is None else {2: 0}
    tn = _tile(D_MODEL, 512, LANE)

    def dw_call(t, rhs, scale, prev):
        def dw(t_ref, r_ref, *rest):
            rest[-1][0, 0] = scale * _dot(t_ref[0], r_ref[...], _TN)

        return _pcall(
            dw, name="ffn_dw", grid=(N_CHIPS, D_MODEL // tn),
            in_specs=[pl.BlockSpec((1, n_tok, fq), lambda j, n: (j, 0, 0)), pl.BlockSpec((n_tok, tn), lambda j, n: (0, n))]
            + ([] if prev is None else [any_spec]),
            out_specs=pl.BlockSpec((1, 1, fq, tn), lambda j, n: (j, d, 0, n)),
            out_shape=jax.ShapeDtypeStruct((N_CHIPS, DEPTH, fq, D_MODEL), F32), input_output_aliases=alias,
            compiler_params=_cparams(("parallel", "parallel"), 6 * n_tok * (fq + tn) * 2 + 4 * fq * tn * 4),
        )(t, rhs, *([] if prev is None else [prev]))

    dwg = dw_call(da, h, 1.0, None if into is None else into[0])
    dwu = dw_call(db, h, 1.0, None if into is None else into[1])
    dwd = dw_call(s, dy16, 0.5, None if into is None else into[2])
    tr = _tile(n_tok, 256, 16)

    def dh_norm(da_ref, db_ref, wg_ref, wu_ref, dy_ref, x_ref, g_ref, dx_ref, dx16_ref, dg_ref):
        dh = _dot(da_ref[0], wg_ref[0, 0]) + _dot(db_ref[0], wu_ref[0, 0])
        for j in range(1, N_CHIPS):
            dh = dh + _dot(da_ref[j], wg_ref[j, 0]) + _dot(db_ref[j], wu_ref[j, 0])
        dx, dg = _rms_bwd(x_ref[...], g_ref[...], dh)
        dx = dy_ref[...] + dx
        dx_ref[...] = dx
        dx16_ref[...] = dx.astype(BF16)

        @pl.when(pl.program_id(0) == 0)
        def _():
            dg_ref[...] = jnp.zeros(dg_ref.shape, F32)

        dg_ref[...] += dg

    tspec = pl.BlockSpec((N_CHIPS, tr, fq), lambda i: (0, i, 0))
    wspec = pl.BlockSpec((N_CHIPS, 1, fq, D_MODEL), lambda i: (0, d, 0, 0))
    rspec = pl.BlockSpec((tr, D_MODEL), lambda i: (i, 0))
    vspec = pl.BlockSpec((1, D_MODEL), lambda i: (0, 0))
    dx, dx16, dg = _pcall(
        dh_norm, name="ffn_dh_norm", grid=(n_tok // tr,),
        in_specs=[tspec, tspec, wspec, wspec, rspec, rspec, vspec], out_specs=[rspec, rspec, vspec],
        out_shape=[jax.ShapeDtypeStruct((n_tok, D_MODEL), F32), jax.ShapeDtypeStruct((n_tok, D_MODEL), BF16),
                   jax.ShapeDtypeStruct((1, D_MODEL), F32)],
        compiler_params=_cparams(("arbitrary",), 4 * N_CHIPS * D_MODEL * fq * 2 + 4 * N_CHIPS * tr * fq * 2
                                 + 16 * tr * D_MODEL * 4),
    )(da, db, wg, wu, dy, x, g)
    return dx, dx16, dg[0], (dwg, dwu, dwd)


def _alpha_pads(w_alpha):
    z = jnp.zeros((LANE - 2 * GLA_R, GLA_W), F32)
    wf = jnp.concatenate([w_alpha[0], jnp.zeros((GLA_R, GLA_W), F32), z], axis=0)
    wb = jnp.concatenate([jnp.zeros((GLA_R, GLA_W), F32), w_alpha[1], z], axis=0)
    return wf, wb


def _dup_heads(t):
    low = lax.broadcasted_iota(jnp.int32, t.shape, 1) < AT_D
    h0, h1 = jnp.where(low, t, 0.0), jnp.where(low, 0.0, t)
    return jnp.concatenate([h0 + pltpu.roll(h0, AT_D, 1), h1 + pltpu.roll(h1, AT_D, 1)], axis=1)


def _fold_heads(t):
    low = lax.broadcasted_iota(jnp.int32, (t.shape[0], LANE), 1) < AT_D
    g0, g1 = t[:, :LANE], t[:, LANE:]
    return jnp.where(low, g0 + pltpu.roll(g0, AT_D, 1), g1 + pltpu.roll(g1, AT_D, 1))


def _mixer_fwd(x, w):
    n_tok = x.shape[0]
    h = _rowwise("mix_norm", lambda xv, gv: _rms(xv, gv), [x], [w['mix_norm']], [(D_MODEL, BF16)])[0]
    z = _mm("mix_in", h, w['w_in_p'])
    gp = _mm("mix_gate", h, w['w_merge_gate'], out_dtype=BF16)
    u = _cw(z, 0, S5_W)

    xs, yy = [], None
    for d, rev in ((0, False), (1, True)):
        a_re, a_im, wb, wc = w['s5'][d]
        x_d, yy = _s5_fwd("s5_rev" if rev else "s5_fwd", z, a_re, a_im, wb, wc, yy, rev=rev)
        xs.append(x_d)

    def s5_post(yv, uv, dv, wglu):
        y0 = yv + dv * uv
        y1 = _gelu(y0)
        t = _dot(y1.astype(BF16), wglu)
        return y0, t, y1 * _sigmoid(t)

    y0, t_glu, y_s5 = _rowwise("s5_post", s5_post, [yy, u], [w['s5_d'], w['s5_w_glu']],
                               [(S5_W, F32), (S5_W, F32), (S5_W, BF16)])

    wa_f, wa_b = _alpha_pads(w['gla_w_alpha'])
    ba_f, ba_b = w['gla_b_alpha'][0:1], w['gla_b_alpha'][1:2]
    zz = _cw(z, ZC_Z, LANE)

    def gla_prep(zv, waf, wab, bf, bb):
        z16 = zv.astype(BF16)
        lf = _dot(z16, waf.astype(BF16)) + bf
        lb = _dot(z16, wab.astype(BF16)) + bb
        return _log_sigmoid(lf) / GLA_TAU, _log_sigmoid(lb) / GLA_TAU

    la_f, la_b = _rowwise("gla_prep", gla_prep, [zz], [wa_f, wa_b, ba_f, ba_b], [(GLA_W, F32), (GLA_W, F32)])
    o_f, s_f = _gla_fwd("gla_fwd", z, la_f, rev=False)
    o_b, s_b = _gla_fwd("gla_rev", z, la_b, rev=True)
    gate = _cw(z, 4, GLA_W)

    def gla_post(of, ob, gv, gn):
        o = of + ob
        outs = []
        for hh in range(GLA_H):
            sl = slice(hh * GLA_D, (hh + 1) * GLA_D)
            gh = gv[:, sl]
            outs.append(_rms(o[:, sl], gn) * (gh * _sigmoid(gh)))
        return jnp.concatenate(outs, axis=1)

    y_gla = _rowwise("gla_post", gla_post, [o_f, o_b, gate], [w['gla_norm']], [(GLA_W, BF16)])[0]

    cq, sq, mq, rq = _head_consts(n_tok, AT_QH)
    ck, sk, mk, rk = _head_consts(n_tok, AT_KH)
    gq = jnp.tile(w['attn_q_norm'], (1, AT_QH))
    gk = jnp.tile(w['attn_k_norm'], (1, AT_KH))

    def rope_fwd(v, gain, cos, sin, mean, rot, scale):
        r = lax.rsqrt(_dot(v * v, mean, precision=HI) + EPS)
        vn = v * r * gain
        return (vn * cos + _dot(vn, rot, precision=HI) * sin) * scale

    def attn_prep(qv, kv, vv, cqv, sqv, ckv, skv, gqv, gkv, mqv, rqv, mkv, rkv):
        kr = rope_fwd(kv, gkv, ckv, skv, mkv, rkv, 1.0)
        return rope_fwd(qv, gqv, cqv, sqv, mqv, rqv, AT_D ** -0.5), _dup_heads(kr), _dup_heads(vv)

    q_raw, k_raw, v_raw = _cw(z, 5, AT_W), _cw(z, ZC_AK, AT_KW), _cw(z, ZC_AV, AT_KW)
    q16, kd, vd = _rowwise("attn_prep", attn_prep, [q_raw, k_raw, v_raw, cq, sq, ck, sk], [gq, gk, mq, rq, mk, rk],
                           [(AT_W, BF16), (2 * LANE, BF16), (2 * LANE, BF16)])
    y_at, lse = _attn_fwd("attn_fwd", q16, kd, vd)

    p0 = _mm("mix_branch", y_s5, w['w_branch_s5'], out_dtype=BF16)
    p1 = _mm("mix_branch", y_gla, w['w_branch_gla'], out_dtype=BF16)
    p2 = _mm("mix_branch_f32", y_at, w['w_branch_attn'], out_dtype=BF16)

    def merge(gpv, a0, a1, a2, bias):
        g = _sigmoid(gpv.astype(F32) + bias)
        return (g[:, :D_MODEL] * a0.astype(F32) + g[:, D_MODEL:2 * D_MODEL] * a1.astype(F32)
                + g[:, 2 * D_MODEL:] * a2.astype(F32))

    merged = _rowwise("mix_merge", merge, [gp, p0, p1, p2], [w['b_merge_gate']], [(D_MODEL, BF16)])[0]
    xo = _mm("mix_out", merged, w['w_out'], add=x)
    saved = dict(x=x, h=h, z=z, gp=gp, xs=xs, y0=y0, t_glu=t_glu, y_s5=y_s5, la=(la_f, la_b), o=(o_f, o_b), s=(s_f, s_b),
                 y_gla=y_gla, q16=q16, kd=kd, vd=vd, lse=lse, y_at=y_at, p=(p0, p1, p2), merged=merged)
    return xo, saved


def _mixer_bwd(dy, dy16, sv, w, s5_raw):
    z, h = sv['z'], sv['h']
    n_tok = z.shape[0]
    grads = {}
    dmerged = _mm("mix_dmerged", dy16, w['w_out'], tb=True, out_dtype=BF16)
    grads['w_out'] = _mm("mix_dwout", sv['merged'], dy16, ta=True)

    def dmerge(dm, gpv, a0, a1, a2, bias):
        dm = dm.astype(F32)
        g = _sigmoid(gpv.astype(F32) + bias)
        ps = (a0.astype(F32), a1.astype(F32), a2.astype(F32))
        dps, dgs = [], []
        for k in range(3):
            gk = g[:, k * D_MODEL:(k + 1) * D_MODEL]
            dps.append(dm * gk)
            dgs.append(dm * ps[k] * gk * (1.0 - gk))
        dgp = jnp.concatenate(dgs, axis=1)
        return dps[0], dps[1], dps[2], dgp, jnp.sum(dgp, axis=0, keepdims=True)

    dp0, dp1, dp2, dgp, dbm = _rowwise("mix_dmerge", dmerge, [dmerged, sv['gp'], *sv['p']], [w['b_merge_gate']],
                                       [(D_MODEL, BF16)] * 3 + [(3 * D_MODEL, BF16)], accs=[(1, 3 * D_MODEL)])
    grads['b_merge_gate'] = dbm[0]
    grads['w_branch_s5'] = _mm("mix_dbranch", sv['y_s5'], dp0, ta=True)
    grads['w_branch_gla'] = _mm("mix_dbranch", sv['y_gla'], dp1, ta=True)
    grads['w_branch_attn'] = _mm("mix_dbranch_f32", sv['y_at'], dp2, ta=True)
    dy_s5 = _mm("mix_dy", dp0, w['w_branch_s5'], tb=True)
    dy_gla = _mm("mix_dy", dp1, w['w_branch_gla'], tb=True)
    dy_at = _mm("mix_dy", dp2, w['w_branch_attn'], tb=True)
    grads['w_merge_gate'] = _mm("mix_dwgate", h, dgp, ta=True)
    dh = _mm("mix_dh_gate", dgp, w['w_merge_gate'], tb=True)
    u = _cw(z, 0, S5_W)

    def s5_post_bwd(dyv, y0, t, uv, dv, wglu):
        sg = _sigmoid(t)
        y1 = _gelu(y0)
        dt = dyv * y1 * sg * (1.0 - sg)
        dy1 = dyv * sg + _dot(dt.astype(BF16), wglu, _NT)
        dy0 = dy1 * _gelu_grad(y0)
        return dy0, dt, y1, dy0 * dv, jnp.sum(dy0 * uv, axis=0, keepdims=True)

    dy0, dt_glu, y1, du, dd = _rowwise("s5_post_bwd", s5_post_bwd, [dy_s5, sv['y0'], sv['t_glu'], u],
                                       [w['s5_d'], w['s5_w_glu']],
                                       [(S5_W, BF16), (S5_W, BF16), (S5_W, BF16), (S5_W, F32)], accs=[(1, S5_W)])
    grads['s5_d'] = dd[0]
    grads['s5_w_glu'] = _mm("s5_dwglu", y1, dt_glu, ta=True)
    s5_cot = []
    for d, rev in ((0, False), (1, True)):
        a_re, a_im, wb, wc = w['s5'][d]
        du, dwb, dwc, da = _s5_bwd("s5_adj_fwd" if rev else "s5_adj_rev", dy0, z, sv['xs'][d], a_re, -a_im, wb, wc, du,
                                   rev=not rev)
        da = jnp.sum(da, axis=0, keepdims=True)
        s5_cot.append((da[:, :S5_N], da[:, S5_N:], dwb, dwc))
    for d in range(2):
        _, pull = jax.vjp(_s5_prep, *s5_raw[d])
        s5_cot[d] = pull(s5_cot[d])
    for k, nm in enumerate(('s5_lambda_re', 's5_lambda_im', 's5_log_dt', 's5_b_re', 's5_b_im', 's5_c_re', 's5_c_im')):
        grads[nm] = jnp.stack([s5_cot[0][k], s5_cot[1][k]])

    gate = _cw(z, 4, GLA_W)

    def gla_post_bwd(dyv, of, ob, gv, gn):
        o = of + ob
        dos, dgates = [], []
        dgn = jnp.zeros((1, GLA_D), F32)
        for hh in range(GLA_H):
            sl = slice(hh * GLA_D, (hh + 1) * GLA_D)
            gh, dyh = gv[:, sl], dyv[:, sl]
            sg = _sigmoid(gh)
            dgates.append(dyh * _rms(o[:, sl], gn) * sg * (1.0 + gh * (1.0 - sg)))
            dx, dg = _rms_bwd(o[:, sl], gn, dyh * gh * sg)
            dos.append(dx)
            dgn = dgn + dg
        return jnp.concatenate(dos, axis=1), jnp.concatenate(dgates, axis=1), dgn

    do, dgate, dgn = _rowwise("gla_post_bwd", gla_post_bwd, [dy_gla, *sv['o'], gate], [w['gla_norm']],
                              [(GLA_W, F32), (GLA_W, F32)], accs=[(1, GLA_D)])
    grads['gla_norm'] = dgn[0]
    dq_f, dk_f, dv_f, dla_f = _gla_bwd("gla_bwd_fwd", z, sv['la'][0], sv['s'][0], do, rev=False)
    dq_b, dk_b, dv_b, dla_b = _gla_bwd("gla_bwd_rev", z, sv['la'][1], sv['s'][1], do, rev=True)
    wa_f, wa_b = _alpha_pads(w['gla_w_alpha'])
    ba_f, ba_b = w['gla_b_alpha'][0:1], w['gla_b_alpha'][1:2]
    zz = _cw(z, ZC_Z, LANE)

    def gla_prep_bwd(dlf, dlb, zv, waf, wab, bf, bb):
        z16 = zv.astype(BF16)
        waf16, wab16 = waf.astype(BF16), wab.astype(BF16)
        xf = _dot(z16, waf16) + bf
        xb = _dot(z16, wab16) + bb
        df = dlf * (_sigmoid(-xf) / GLA_TAU)
        db = dlb * (_sigmoid(-xb) / GLA_TAU)
        dz = _dot(df.astype(BF16), waf16, _NT) + _dot(db.astype(BF16), wab16, _NT)
        return df, db, dz, jnp.sum(df, axis=0, keepdims=True), jnp.sum(db, axis=0, keepdims=True)

    dlog_f, dlog_b, dzz, dba_f, dba_b = _rowwise(
        "gla_prep_bwd", gla_prep_bwd, [dla_f, dla_b, zz], [wa_f, wa_b, ba_f, ba_b],
        [(GLA_W, BF16), (GLA_W, BF16), (LANE, F32)], accs=[(1, GLA_W), (1, GLA_W)])
    dwa_f = _mm("gla_dwa", zz, dlog_f, ta=True)
    dwa_b = _mm("gla_dwa", zz, dlog_b, ta=True)
    grads['gla_w_alpha'] = jnp.stack([dwa_f[:GLA_R], dwa_b[GLA_R:2 * GLA_R]])
    grads['gla_b_alpha'] = jnp.concatenate([dba_f, dba_b], axis=0)

    dq_r, dkd, dvd = _attn_bwd("attn_bwd", sv['q16'], sv['kd'], sv['vd'], sv['y_at'], sv['lse'], dy_at)
    cq, sq, mq, rq = _head_consts(n_tok, AT_QH)
    ck, sk, mk, rk = _head_consts(n_tok, AT_KH)
    gq = jnp.tile(w['attn_q_norm'], (1, AT_QH))
    gk = jnp.tile(w['attn_k_norm'], (1, AT_KH))

    def rope_bwd(dr, v, gain, cos, sin, mean, rot, scale):
        dr = dr * scale
        dvn = dr * cos + _dot(dr * sin, rot, _NT, precision=HI)
        r = lax.rsqrt(_dot(v * v, mean, precision=HI) + EPS)
        vh = v * r
        dg = jnp.sum(dvn * vh, axis=0, keepdims=True)
        dvh = dvn * gain
        return r * (dvh - vh * _dot(dvh * vh, mean, precision=HI)), dg

    def attn_prep_bwd(dq, dk2, dv2, qv, kv, cqv, sqv, ckv, skv, gqv, gkv, mqv, rqv, mkv, rkv):
        dqo, dgq = rope_bwd(dq, qv, gqv, cqv, sqv, mqv, rqv, AT_D ** -0.5)
        dko, dgk = rope_bwd(_fold_heads(dk2), kv, gkv, ckv, skv, mkv, rkv, 1.0)
        return dqo, dko, _fold_heads(dv2), dgq, dgk

    q_raw, k_raw = _cw(z, 5, AT_W), _cw(z, ZC_AK, AT_KW)
    daq, dak, dav, dgq, dgk = _rowwise("attn_prep_bwd", attn_prep_bwd, [dq_r, dkd, dvd, q_raw, k_raw, cq, sq, ck, sk],
                                       [gq, gk, mq, rq, mk, rk], [(AT_W, F32), (AT_KW, F32), (AT_KW, F32)],
                                       accs=[(1, AT_W), (1, AT_KW)])
    grads['attn_q_norm'] = dgq.reshape(AT_QH, AT_D).sum(axis=0)
    grads['attn_k_norm'] = dgk.reshape(AT_KH, AT_D).sum(axis=0)

    def assemble(duv, qf, qb, kf, kb, vf, vb, dg, aq, ak, av, zzv):
        return jnp.concatenate([duv, qf + qb, kf + kb, vf + vb, dg, aq, ak, av, zzv], axis=1)

    dz = _rowwise("mix_dz", assemble, [du, dq_f, dq_b, dk_f, dk_b, dv_f, dv_b, dgate, daq, dak, dav, dzz], [],
                  [(ZP_W, BF16)])[0]
    grads['w_in'] = _unpad_w_in(_mm("mix_dwin", h, dz, ta=True))
    dh = _mm("mix_dh_in", dz, w['w_in_p'], tb=True, add=dh)

    def dnorm(dyv, xv, dhv, gv):
        dx, dg = _rms_bwd(xv, gv, dhv)
        dx = dyv + dx
        return dx, dx, dg

    dx, dx16, dg = _rowwise("mix_dnorm", dnorm, [dy, sv['x'], dh], [w['mix_norm']], [(D_MODEL, F32), (D_MODEL, BF16)],
                            accs=[(1, D_MODEL)])
    grads['mix_norm'] = dg[0]
    return dx, dx16, grads


FFN_NAMES = [p + k for p in ('ffn1_', 'ffn2_') for k in ('w_gate', 'w_up', 'w_down')]
MISC_SHARDED = [n for n in SHARDED if n not in FFN_NAMES]
LARGE_SHARDED = FFN_NAMES + ['w_in', 'w_merge_gate']
KEPT_TRANSPOSED = [p + k for p in ('ffn1_', 'ffn2_') for k in ('w_gate', 'w_up')]


def _stored(n, t):
    return jnp.swapaxes(t, 1, 2) if n in KEPT_TRANSPOSED else t


def _layer_weights(ws, rep, i):
    w = {k: _from_shards(ws[k][:, i], SHARD_AXIS[k] - 1) for k in MISC_SHARDED}
    for k in ('ffn1_norm', 'mix_norm', 'ffn2_norm', 's5_d', 'gla_norm', 'attn_q_norm', 'attn_k_norm', 'b_merge_gate'):
        w[k] = rep[k][i].reshape(1, -1)
    w['w_in_p'] = _pad_w_in(w['w_in'])
    s5_raw = [tuple(rep[k][i, d] for k in ('s5_lambda_re', 's5_lambda_im', 's5_log_dt', 's5_b_re', 's5_b_im',
                                            's5_c_re', 's5_c_im')) for d in range(2)]
    w['s5'] = [_s5_prep(*s5_raw[d]) for d in range(2)]
    return w, s5_raw


def _device_step(x, target, ws, rep):
    lw = [_layer_weights(ws, rep, i) for i in range(DEPTH)]
    ffn = {p: (ws[p + 'w_gate'], ws[p + 'w_up'], ws[p + 'w_down']) for p in ('ffn1_', 'ffn2_')}
    saved = []
    for i in range(DEPTH):
        w, _ = lw[i]
        x, s1 = _ffn_fwd(x, w['ffn1_norm'], *ffn['ffn1_'], i)
        x, s2 = _mixer_fwd(x, w)
        x, s3 = _ffn_fwd(x, w['ffn2_norm'], *ffn['ffn2_'], i)
        saved.append((s1, s2, s3))
    gfin = rep['final_norm'].reshape(1, -1)

    def head(xv, tv, gv):
        e = _rms(xv, gv) - tv
        dx, dg = _rms_bwd(xv, gv, e * (1.0 / D_MODEL))
        part = jnp.sum(e * e, axis=0, keepdims=True)
        return dx, dx, dg, part

    dx, dx16, dgfin, part = _rowwise("loss_head", head, [x, target], [gfin], [(D_MODEL, F32), (D_MODEL, BF16)],
                                     accs=[(1, D_MODEL), (1, D_MODEL)])
    loss = (0.5 / D_MODEL) * jnp.sum(part)
    per_layer = []
    dffn = {'ffn1_': None, 'ffn2_': None}
    for i in reversed(range(DEPTH)):
        w, s5_raw = lw[i]
        s1, s2, s3 = saved[i]
        dx, dx16, n3, dffn['ffn2_'] = _ffn_bwd(dx, dx16, s3, w['ffn2_norm'], *ffn['ffn2_'], i, dffn['ffn2_'])
        dx, dx16, g = _mixer_bwd(dx, dx16, s2, w, s5_raw)
        dx, dx16, n1, dffn['ffn1_'] = _ffn_bwd(dx, dx16, s1, w['ffn1_norm'], *ffn['ffn1_'], i, dffn['ffn1_'])
        g['ffn2_norm'], g['ffn1_norm'] = n3, n1
        per_layer.append(g)
    per_layer.reverse()
    stacked = {k: jnp.stack([per_layer[i][k] for i in range(DEPTH)]) for k in per_layer[0]}
    gs = {k: _to_shards(stacked[k], SHARD_AXIS[k]) for k in MISC_SHARDED}
    for p in ('ffn1_', 'ffn2_'):
        gs[p + 'w_gate'], gs[p + 'w_up'], gs[p + 'w_down'] = dffn[p]
    grep = {k: stacked[k] for k in REPLICATED if k != 'final_norm'}
    grep['final_norm'] = dgfin[0]
    return loss, dx, gs, grep


PACK_C = 1024
ROW_TILE = 512


def _rows_of(shape):
    return -(-int(np.prod(shape)) // PACK_C)


def _as_rows(t, lead=0):
    head = t.shape[:lead]
    flat = t.reshape(head + (-1,))
    rows = -(-flat.shape[-1] // PACK_C)
    pad = rows * PACK_C - flat.shape[-1]
    if pad:
        flat = jnp.pad(flat, [(0, 0)] * lead + [(0, pad)])
    return flat.reshape(head + (rows, PACK_C))


def _pack(items, total_rows, lead=0):
    parts = [_as_rows(t, lead) for t in items]
    used = sum(p.shape[lead] for p in parts)
    if total_rows > used:
        parts.append(jnp.zeros(parts[0].shape[:lead] + (total_rows - used, PACK_C), parts[0].dtype))
    return jnp.concatenate(parts, axis=lead)


def _unpack(rows, shapes, lead=0):
    out, r0 = [], 0
    head = rows.shape[:lead]
    for shp in shapes:
        n, size = _rows_of(shp), int(np.prod(shp))
        piece = lax.slice_in_dim(rows, r0, r0 + n, axis=lead).reshape(head + (n * PACK_C,))
        out.append(lax.slice_in_dim(piece, 0, size, axis=lead).reshape(head + tuple(shp)))
        r0 += n
    return out


def _round_up(n, m):
    return -(-n // m) * m


def _to_shards(g, axis):
    shp = g.shape
    g = g.reshape(shp[:axis] + (N_CHIPS, shp[axis] // N_CHIPS) + shp[axis + 1:])
    return jnp.moveaxis(g, axis, 0)


def _from_shards(s, axis):
    s = jnp.moveaxis(s, 0, axis)
    shp = s.shape
    return s.reshape(shp[:axis] + (shp[axis] * shp[axis + 1],) + shp[axis + 2:])


_HBM = pl.BlockSpec(memory_space=pltpu.HBM)


def _place():
    x, y, c = lax.axis_index("x"), lax.axis_index("y"), lax.axis_index("c")
    chips = [(1 - x, y), (x, 1 - y), (1 - x, 1 - y)]
    return x, y, c, chips


def _rcopy(src, dst, send, recv, k, dev):
    return pltpu.make_async_remote_copy(src_ref=src, dst_ref=dst, send_sem=send.at[k], recv_sem=recv.at[k],
                                        device_id=dev, device_id_type=MESH)


def _comm_call(name, body, arrays, out_shapes, n_sems, n_local):
    return pl.pallas_call(
        body, name=name, in_specs=[_HBM] * len(arrays), out_specs=[_HBM] * len(out_shapes), out_shape=out_shapes,
        scratch_shapes=[pltpu.SemaphoreType.DMA((n_sems,)), pltpu.SemaphoreType.DMA((n_sems,)),
                        pltpu.SemaphoreType.DMA((max(n_local, 1),))],
    )(*arrays)


def _ici_peers(x, y, c):
    x_first = c == 0
    first = (jnp.where(x_first, 1 - x, x), jnp.where(x_first, y, 1 - y))
    second = (jnp.where(x_first, x, 1 - x), jnp.where(x_first, 1 - y, y))
    return first, second


def _ag_chips(name, packs):
    n = len(packs)

    def body(*refs):
        p, o, (send, recv, _) = refs[:n], refs[n:2 * n], refs[2 * n:]
        x, y, c, _ = _place()
        (ax, ay), (bx, by) = _ici_peers(x, y, c)
        me, na, nb, nd = 2 * x + y, 2 * ax + ay, 2 * bx + by, 2 * (1 - x) + (1 - y)
        sib = (x, y, 1 - c)
        own = [_rcopy(p[a].at[c], o[a].at[me, c], send, recv, 6 * a, (ax, ay, c)) for a in range(n)]
        own += [_rcopy(p[a].at[c], o[a].at[me, c], send, recv, 6 * a + 1, (bx, by, c)) for a in range(n)]
        for cp in own:
            cp.start()
        passed = []

        def landed(a, chip, k, fwd):
            blk = o[a].at[chip, c]
            _rcopy(blk, blk, send, recv, 6 * a + k, sib).wait_recv()
            for sem, dev in fwd:
                cp = _rcopy(blk, blk, send, recv, 6 * a + sem, dev)
                cp.start()
                passed.append(cp)

        for a in range(n):
            landed(a, na, 0, [(2, (bx, by, c)), (3, sib)])
        for a in range(n):
            landed(a, nb, 1, [(4, sib)])
        for a in range(n):
            landed(a, nd, 2, [(5, sib)])
        for a in range(n):
            for k, chip in ((3, nb), (4, na), (5, nd)):
                blk = o[a].at[chip, 1 - c]
                _rcopy(blk, blk, send, recv, 6 * a + k, sib).wait_recv()
        for cp in own + passed:
            cp.wait_send()

    outs = _comm_call(name, body, packs, [jax.ShapeDtypeStruct((N_CHIPS,) + t.shape, t.dtype) for t in packs], 6 * n, 0)
    chip = 2 * lax.axis_index("x") + lax.axis_index("y")
    return [lax.dynamic_update_slice_in_dim(o, t[None], chip, axis=0) for o, t in zip(outs, packs)]


def _pair_swap(name, gs):
    n = len(gs)
    ns = gs[0].shape[0]

    def body(*refs):
        g, r, (send, recv, _) = refs[:n], refs[n:2 * n], refs[2 * n:]
        x, y, c, _ = _place()
        cps = [_rcopy(g[a].at[j, 1 - c], r[a].at[j], send, recv, ns * a + j, (x, y, 1 - c))
               for a in range(n) for j in range(ns)]
        for cp in cps:
            cp.start()
        for cp in cps:
            cp.wait_recv()
        for cp in cps:
            cp.wait_send()

    return _comm_call(name, body, gs, [jax.ShapeDtypeStruct((ns,) + t.shape[2:], t.dtype) for t in gs], ns * n, 0)


def _halve_first(name, parts):
    n = len(parts)

    def body(*refs):
        p, r, (send, recv, _) = refs[:n], refs[n:2 * n], refs[2 * n:]
        x, y, c, _ = _place()
        (ax, ay), _ = _ici_peers(x, y, c)
        cps = [_rcopy(p[a].at[jnp.where(c == 0, 2 * ax + k, 2 * k + ay)], r[a].at[k], send, recv, 2 * a + k, (ax, ay, c))
               for a in range(n) for k in range(2)]
        for cp in cps:
            cp.start()
        for cp in cps:
            cp.wait_recv()
        for cp in cps:
            cp.wait_send()

    return _comm_call(name, body, parts, [jax.ShapeDtypeStruct((2,) + t.shape[1:], t.dtype) for t in parts], 2 * n, 0)


def _halve_second(name, kept):
    n = len(kept)

    def body(*refs):
        p, r, (send, recv, _) = refs[:n], refs[n:2 * n], refs[2 * n:]
        x, y, c, _ = _place()
        _, (bx, by) = _ici_peers(x, y, c)
        cps = [_rcopy(p[a].at[jnp.where(c == 0, by, bx)], r[a], send, recv, a, (bx, by, c)) for a in range(n)]
        for cp in cps:
            cp.start()
        for cp in cps:
            cp.wait_recv()
        for cp in cps:
            cp.wait_send()

    return _comm_call(name, body, kept, [jax.ShapeDtypeStruct(t.shape[1:], t.dtype) for t in kept], n, 0)


def _add_selected(name, t, r, index, out_dtype):
    single = r.ndim == 2
    r3 = r[None] if single else r
    nk, rh, cc = r3.shape
    tl = _tile(rh, ROW_TILE, 16)

    def body(i_ref, t_ref, r_ref, o_ref):
        o_ref[...] = (t_ref[...].astype(F32) + r_ref[...].astype(F32)).astype(o_ref.dtype)

    out = _pcall(
        body, name=name,
        grid_spec=pltpu.PrefetchScalarGridSpec(
            num_scalar_prefetch=1, grid=(nk, rh // tl),
            in_specs=[pl.BlockSpec((1, tl, cc), lambda k, i, i_ref: (i_ref[k], i, 0)),
                      pl.BlockSpec((1, tl, cc), lambda k, i, i_ref: (k, i, 0))],
            out_specs=pl.BlockSpec((1, tl, cc), lambda k, i, i_ref: (k, i, 0))),
        out_shape=jax.ShapeDtypeStruct(r3.shape, out_dtype),
        compiler_params=_cparams(("parallel", "parallel"), 8 * tl * cc * 4),
    )(index, t, r3)
    return out[0] if single else out


def _pair_share(name, reds):
    n = len(reds)

    def body(*refs):
        r, o, (send, recv, _) = refs[:n], refs[n:2 * n], refs[2 * n:]
        x, y, c, _ = _place()
        cps = [_rcopy(r[a], o[a].at[c], send, recv, a, (x, y, 1 - c)) for a in range(n)]
        for cp in cps:
            cp.start()
        for a in range(n):
            blk = o[a].at[1 - c]
            _rcopy(blk, blk, send, recv, a, (x, y, 1 - c)).wait_recv()
        for cp in cps:
            cp.wait_send()

    outs = _comm_call(name, body, reds, [jax.ShapeDtypeStruct((2,) + t.shape, t.dtype) for t in reds], n, 0)
    core = lax.axis_index("c")
    return [lax.dynamic_update_slice_in_dim(o, t[None], core, axis=0) for o, t in zip(outs, reds)]


def _add_own_half(name, g, r):
    ns, _, rh, cc = g.shape
    tl = _tile(rh, ROW_TILE, 16)
    core = lax.axis_index("c").astype(jnp.int32).reshape(1)

    def body(c_ref, g_ref, r_ref, o_ref):
        o_ref[...] = (g_ref[0] + r_ref[...]).astype(o_ref.dtype)

    return _pcall(
        body, name=name,
        grid_spec=pltpu.PrefetchScalarGridSpec(
            num_scalar_prefetch=1, grid=(ns, rh // tl),
            in_specs=[pl.BlockSpec((1, 1, tl, cc), lambda j, i, c_ref: (j, c_ref[0], i, 0)),
                      pl.BlockSpec((1, tl, cc), lambda j, i, c_ref: (j, i, 0))],
            out_specs=pl.BlockSpec((1, tl, cc), lambda j, i, c_ref: (j, i, 0))),
        out_shape=jax.ShapeDtypeStruct(r.shape, BF16),
        compiler_params=_cparams(("parallel", "parallel"), 8 * tl * cc * 4),
    )(core, g, r)


def _reduce_scatter(gs):
    x, y, c = lax.axis_index("x"), lax.axis_index("y"), lax.axis_index("c")
    mine_first = jnp.stack([jnp.where(c == 0, 2 * x + k, 2 * k + y) for k in range(2)]).astype(jnp.int32)
    mine_second = jnp.where(c == 0, y, x).astype(jnp.int32).reshape(1)
    swapped = _pair_swap("rs_pair_swap", gs)
    parts = [_add_own_half("rs_add_pair", g, r) for g, r in zip(gs, swapped)]
    got = _halve_first("rs_halve_first", parts)
    kept = [_add_selected("rs_add_first", t, r, mine_first, BF16) for t, r in zip(parts, got)]
    got = _halve_second("rs_halve_second", kept)
    reds = [_add_selected("rs_add_second", t, r, mine_second, F32) for t, r in zip(kept, got)]
    return _pair_share("rs_pair_share", reds)


def _adamw(name, g, w, m, v):
    c1 = 1.0 - ADAM_B1 ** ADAM_STEP
    c2 = 1.0 - ADAM_B2 ** ADAM_STEP
    nb, rows, cc = g.shape
    tl = _tile(rows, ROW_TILE, SUBLANE)

    def body(g_ref, w_ref, m_ref, v_ref, d_ref, mo_ref, vo_ref):
        gv = g_ref[...]
        mn = ADAM_B1 * m_ref[...] + (1.0 - ADAM_B1) * gv
        vn = ADAM_B2 * v_ref[...] + (1.0 - ADAM_B2) * (gv * gv)
        d_ref[...] = -ADAM_LR * ((mn / c1) / (jnp.sqrt(vn / c2) + ADAM_EPS) + ADAM_WD * w_ref[...])
        mo_ref[...] = mn
        vo_ref[...] = vn

    spec = pl.BlockSpec((1, tl, cc), lambda b, i: (b, i, 0))
    return _pcall(
        body, name=name, grid=(nb, rows // tl), in_specs=[spec] * 4, out_specs=[spec] * 3,
        out_shape=[jax.ShapeDtypeStruct(g.shape, F32)] * 3,
        compiler_params=_cparams(("parallel", "parallel"), 16 * tl * cc * 4),
    )(g, w, m, v)


def _train_step(a):
    x, target = a['x'][0], a['loss_target'][0]

    names = BF16_GATHER + F32_GATHER
    got = _ag_chips("ag_weights", [_stored(n, a[n]).astype(BF16) for n in BF16_GATHER] + [a[n] for n in F32_GATHER])
    ws = dict(zip(names, got))

    loss, gx, gs, grep = _device_step(x, target, ws, {n: a[n] for n in REPLICATED})
    loss = lax.psum(loss, ("x", "y", "c"))

    small = [n for n in MISC_SHARDED if n not in LARGE_SHARDED]
    small_shapes = [a[n].shape[1:] for n in small]
    rep_shapes = [a[n].shape for n in REPLICATED]
    small_rows = sum(_rows_of(s) for s in small_shapes)
    piece = _round_up(-(-sum(_rows_of(s) for s in rep_shapes) // (2 * N_CHIPS)), SUBLANE)
    rep_rows = 2 * N_CHIPS * piece
    rep_pack = _pack([grep[n] for n in REPLICATED], rep_rows).reshape(N_CHIPS, 2, piece, PACK_C)
    total = _round_up(small_rows + piece, ROW_TILE // 2)
    misc = _pack([gs[n] for n in small] + [rep_pack], total, lead=2)
    reds = _reduce_scatter([gs[n] for n in LARGE_SHARDED] + [misc])
    grad = dict(zip(LARGE_SHARDED, reds[:-1]))
    grad.update(zip(small, _unpack(reds[-1], small_shapes, lead=1)))
    rep_mine = reds[-1][:, small_rows:small_rows + piece]
    rep_all = _ag_chips("ag_replicated", [rep_mine])[0].reshape(1, rep_rows, PACK_C)

    outs = {}
    for n in SHARDED:
        shp = grad[n].shape
        three_d = (shp[0], -1, shp[-1])
        params = (grad[n], *(_stored(n, a[p + n]) for p in ('', 'm_', 'v_')))
        res3 = _adamw("adamw", *(t.reshape(three_d) for t in params))
        for kind, arr in zip(('grad', 'delta', 'new_m', 'new_v'), (grad[n], *res3)):
            outs[kind + '_' + n] = _stored(n, arr.reshape(shp))
    w, m, v = (_pack([a[p + n] for n in REPLICATED], rep_rows)[None] for p in ('', 'm_', 'v_'))
    res3 = _adamw("adamw_replicated", rep_all, w, m, v)
    for kind, arr in zip(('grad', 'delta', 'new_m', 'new_v'), (rep_all, *res3)):
        for n, t in zip(REPLICATED, _unpack(arr[0], rep_shapes)):
            outs[kind + '_' + n] = t
    res = [loss, gx[None]]
    for kind in ('grad', 'delta', 'new_m', 'new_v'):
        res += [outs[kind + '_' + n] for n in W_NAMES]
    return tuple(res)


def kernel(x, ffn1_norm, ffn1_w_gate, ffn1_w_up, ffn1_w_down, mix_norm, w_in, s5_lambda_re, s5_lambda_im, s5_log_dt, s5_b_re, s5_b_im, s5_c_re, s5_c_im, s5_d, s5_w_glu, gla_w_alpha, gla_b_alpha, gla_norm, attn_q_norm, attn_k_norm, w_branch_s5, w_branch_gla, w_branch_attn, w_merge_gate, b_merge_gate, w_out, ffn2_norm, ffn2_w_gate, ffn2_w_up, ffn2_w_down, final_norm, loss_target, m_ffn1_norm, m_ffn1_w_gate, m_ffn1_w_up, m_ffn1_w_down, m_mix_norm, m_w_in, m_s5_lambda_re, m_s5_lambda_im, m_s5_log_dt, m_s5_b_re, m_s5_b_im, m_s5_c_re, m_s5_c_im, m_s5_d, m_s5_w_glu, m_gla_w_alpha, m_gla_b_alpha, m_gla_norm, m_attn_q_norm, m_attn_k_norm, m_w_branch_s5, m_w_branch_gla, m_w_branch_attn, m_w_merge_gate, m_b_merge_gate, m_w_out, m_ffn2_norm, m_ffn2_w_gate, m_ffn2_w_up, m_ffn2_w_down, m_final_norm, v_ffn1_norm, v_ffn1_w_gate, v_ffn1_w_up, v_ffn1_w_down, v_mix_norm, v_w_in, v_s5_lambda_re, v_s5_lambda_im, v_s5_log_dt, v_s5_b_re, v_s5_b_im, v_s5_c_re, v_s5_c_im, v_s5_d, v_s5_w_glu, v_gla_w_alpha, v_gla_b_alpha, v_gla_norm, v_attn_q_norm, v_attn_k_norm, v_w_branch_s5, v_w_branch_gla, v_w_branch_attn, v_w_merge_gate, v_b_merge_gate, v_w_out, v_ffn2_norm, v_ffn2_w_gate, v_ffn2_w_up, v_ffn2_w_down, v_final_norm):
    return _train_step(dict(locals()))
```

```python
import functools
import math

import jax
import jax.numpy as jnp
import numpy as np
from jax import lax
from jax.experimental import pallas as pl
from jax.experimental.pallas import tpu as pltpu

F32 = jnp.float32
BF16 = jnp.bfloat16
HI = lax.Precision.HIGHEST

D_MODEL = 1024
D_FF = 2816
DEPTH = 2
EPS = 1e-6
S5_G, S5_H, S5_P = 32, 16, 64
S5_W = S5_G * S5_H
S5_N = S5_G * S5_P
S5_TG = 8
S5_TILES = S5_G // S5_TG
S5_TC, S5_TS = S5_TG * S5_H, S5_TG * S5_P
GLA_H, GLA_D = 4, 128
GLA_W = GLA_H * GLA_D
GLA_R = 16
GLA_TAU = 16.0
GLA_C = 64
AT_QH, AT_KH, AT_D = 8, 2, 64
AT_W = AT_QH * AT_D
AT_KW = AT_KH * AT_D
GRID_W = 64
ROPE_BASE = 10000.0
ZP_W = 3456
ZC_AK, ZC_AV, ZC_Z = 24, 25, 26

ADAM_LR, ADAM_B1, ADAM_B2, ADAM_EPS, ADAM_WD, ADAM_STEP = 0.001, 0.9, 0.999, 1e-08, 0.01, 10

VMEM_LIMIT_V7X = 56 * 1024 * 1024
LANE = 128
SUBLANE = 8

W_NAMES = ['ffn1_norm', 'ffn1_w_gate', 'ffn1_w_up', 'ffn1_w_down', 'mix_norm', 'w_in', 's5_lambda_re', 's5_lambda_im',
           's5_log_dt', 's5_b_re', 's5_b_im', 's5_c_re', 's5_c_im', 's5_d', 's5_w_glu', 'gla_w_alpha', 'gla_b_alpha',
           'gla_norm', 'attn_q_norm', 'attn_k_norm', 'w_branch_s5', 'w_branch_gla', 'w_branch_attn', 'w_merge_gate',
           'b_merge_gate', 'w_out', 'ffn2_norm', 'ffn2_w_gate', 'ffn2_w_up', 'ffn2_w_down', 'final_norm']
SHARD_AXIS = {'ffn1_w_gate': 2, 'ffn1_w_up': 2, 'ffn1_w_down': 1, 'w_in': 2, 's5_w_glu': 1, 'gla_w_alpha': 3,
              'gla_b_alpha': 2, 'w_branch_s5': 2, 'w_branch_gla': 2, 'w_branch_attn': 2, 'w_merge_gate': 2,
              'w_out': 1, 'ffn2_w_gate': 2, 'ffn2_w_up': 2, 'ffn2_w_down': 1}
F32_GATHER = ['gla_w_alpha', 'gla_b_alpha']
BF16_GATHER = [n for n in W_NAMES if n in SHARD_AXIS and n not in F32_GATHER]
SHARDED = [n for n in W_NAMES if n in SHARD_AXIS]
REPLICATED = [n for n in W_NAMES if n not in SHARD_AXIS]
N_CHIPS = 4
MESH = pl.DeviceIdType.MESH


def _tile(n, target, mult):
    best = None
    for t in range(mult, min(n, target) + 1, mult):
        if n % t == 0:
            best = t
    return best if best is not None else n


def _cparams(sem, vmem_bytes):
    limit = int(min(VMEM_LIMIT_V7X, max(32 * 1024 * 1024, vmem_bytes * 5 // 4)))
    return pltpu.CompilerParams(dimension_semantics=sem, vmem_limit_bytes=limit)


def _pcall(body, **kw):
    shapes = kw.pop('out_shape')
    many = isinstance(shapes, (list, tuple))
    pinned = [pltpu.HBM(t.shape, t.dtype) for t in (shapes if many else [shapes])]
    call = pl.pallas_call(body, out_shape=pinned if many else pinned[0], **kw)

    def run(*ops):
        return call(*[o if jnp.issubdtype(o.dtype, jnp.integer) else pltpu.with_memory_space_constraint(o, pltpu.HBM)
                      for o in ops])

    return run


def _cw(arr, cb, width):
    return (arr, cb, width)


def _win(a):
    if isinstance(a, tuple):
        return a
    return (a, 0, a.shape[-1])


def _nbytes(shape, dtype):
    return int(np.prod(shape)) * jnp.dtype(dtype).itemsize


def _rowwise(name, fn, rows, fulls, outs, accs=(), tl=256):
    rows = [_win(r) for r in rows]
    n_tok = rows[0][0].shape[0]
    tl = _tile(n_tok, tl, 16)
    nr, nf, no = len(rows), len(fulls), len(outs)

    def body(*refs):
        vals = [r[...] for r in refs[:nr + nf]]
        res = fn(*vals)
        if not isinstance(res, (tuple, list)):
            res = (res,)
        for k in range(no):
            o = refs[nr + nf + k]
            o[...] = res[k].astype(o.dtype)
        if accs:
            @pl.when(pl.program_id(0) == 0)
            def _():
                for k in range(len(accs)):
                    a = refs[nr + nf + no + k]
                    a[...] = jnp.zeros(a.shape, a.dtype)
            for k in range(len(accs)):
                a = refs[nr + nf + no + k]
                a[...] += res[no + k]

    in_specs = [pl.BlockSpec((tl, w), functools.partial(lambda i, cb: (i, cb), cb=cb)) for (_, cb, w) in rows]
    in_specs += [pl.BlockSpec(f.shape, functools.partial(lambda i, nd: (0,) * nd, nd=f.ndim)) for f in fulls]
    out_specs = [pl.BlockSpec((tl, c), lambda i: (i, 0)) for (c, _) in outs]
    out_specs += [pl.BlockSpec((r, c), lambda i: (0, 0)) for (r, c) in accs]
    out_shape = [jax.ShapeDtypeStruct((n_tok, c), dt) for (c, dt) in outs]
    out_shape += [jax.ShapeDtypeStruct((r, c), F32) for (r, c) in accs]
    vm = 2 * sum(_nbytes((tl, w), a.dtype) for (a, _, w) in rows)
    vm += 2 * sum(_nbytes(f.shape, f.dtype) for f in fulls)
    vm += 2 * sum(_nbytes((tl, c), dt) for (c, dt) in outs)
    vm += 8 * max([_nbytes((tl, w), F32) for (_, _, w) in rows] + [_nbytes((tl, c), F32) for (c, _) in outs])
    return _pcall(
        body, name=name, grid=(n_tok // tl,), in_specs=in_specs, out_specs=out_specs, out_shape=out_shape,
        compiler_params=_cparams(("arbitrary",) if accs else ("parallel",), vm),
    )(*[r[0] for r in rows], *fulls)


def _mm(name, a, b, *, ta=False, tb=False, out_dtype=F32, alpha=1.0, add=None, tm=512, tn=512):
    a_arr, a_cb, a_w = _win(a)
    b_arr, b_cb, b_w = _win(b)
    if ta:
        kdim, m = a_arr.shape[0], a_w
    else:
        m, kdim = a_arr.shape[0], a_w
    if tb:
        n, kb = b_arr.shape[0], b_w
    else:
        kb, n = b_arr.shape[0], b_w
    assert kdim == kb, (name, kdim, kb)
    tm = _tile(m, tm, LANE if ta else 16)
    tn = _tile(n, tn, LANE)
    if ta:
        a_spec = pl.BlockSpec((kdim, tm), lambda i, j: (0, a_cb * (a_w // tm) + i))
        a_blk = (kdim, tm)
    else:
        a_spec = pl.BlockSpec((tm, kdim), lambda i, j: (i, a_cb))
        a_blk = (tm, kdim)
    if tb:
        b_spec = pl.BlockSpec((tn, kdim), lambda i, j: (j, b_cb))
        b_blk = (tn, kdim)
    else:
        b_spec = pl.BlockSpec((kdim, tn), lambda i, j: (0, b_cb * (b_w // tn) + j))
        b_blk = (kdim, tn)
    dims = (((0 if ta else 1,), (1 if tb else 0,)), ((), ()))
    has_add = add is not None

    def body(*refs):
        a_ref, b_ref = refs[0], refs[1]
        o_ref = refs[-1]
        acc = lax.dot_general(a_ref[...].astype(BF16), b_ref[...].astype(BF16), dims, preferred_element_type=F32)
        if alpha != 1.0:
            acc = acc * alpha
        if has_add:
            acc = acc + refs[2][...].astype(F32)
        o_ref[...] = acc.astype(o_ref.dtype)

    in_specs = [a_spec, b_spec]
    ops = [a_arr, b_arr]
    if has_add:
        in_specs.append(pl.BlockSpec((tm, tn), lambda i, j: (i, j)))
        ops.append(add)
    vm = 2 * (_nbytes(a_blk, a_arr.dtype) + _nbytes(b_blk, b_arr.dtype) + _nbytes((tm, tn), out_dtype))
    vm += _nbytes(a_blk, BF16) + _nbytes(b_blk, BF16) + 3 * _nbytes((tm, tn), F32)
    return _pcall(
        body, name=name, grid=(m // tm, n // tn), in_specs=in_specs,
        out_specs=pl.BlockSpec((tm, tn), lambda i, j: (i, j)),
        out_shape=jax.ShapeDtypeStruct((m, n), out_dtype),
        compiler_params=_cparams(("parallel", "parallel"), vm),
    )(*ops)


def _sigmoid(x):
    return 1.0 / (1.0 + jnp.exp(-x))


def _rms(x, gain):
    r = lax.rsqrt(jnp.mean(x * x, axis=-1, keepdims=True) + EPS)
    return x * r * gain


def _rms_bwd(x, gain, dh):
    r = lax.rsqrt(jnp.mean(x * x, axis=-1, keepdims=True) + EPS)
    xh = x * r
    dg = jnp.sum(dh * xh, axis=0, keepdims=True)
    dxh = dh * gain
    dx = r * (dxh - xh * jnp.mean(dxh * xh, axis=-1, keepdims=True))
    return dx, dg


_GELU_C = math.sqrt(2.0 / math.pi)


def _gelu(y):
    return 0.5 * y * (1.0 + jnp.tanh(_GELU_C * (y + 0.044715 * y * y * y)))


def _gelu_grad(y):
    th = jnp.tanh(_GELU_C * (y + 0.044715 * y * y * y))
    return 0.5 * (1.0 + th) + 0.5 * y * (1.0 - th * th) * _GELU_C * (1.0 + 3.0 * 0.044715 * y * y)


def _log_sigmoid(x):
    return jnp.minimum(x, 0.0) - jnp.log(1.0 + jnp.exp(-jnp.abs(x)))


def _dot(a, b, dims=(((1,), (0,)), ((), ())), precision=None):
    return lax.dot_general(a, b, dims, preferred_element_type=F32, precision=precision)


_NT = (((1,), (1,)), ((), ()))
_TN = (((0,), (0,)), ((), ()))


def _scan_core(b_ref, are_ref, aim_ref, pw, last, of32, *, rev, xf32=None, da_ref=None, lc=512):
    nb, nc = b_ref.shape[0] // SUBLANE, S5_N // lc
    row = lax.broadcasted_iota(jnp.int32, (SUBLANE, lc), 0)

    def cmul(ar, ai, xr, xi):
        return ar * xr - ai * xi, ar * xi + ai * xr

    def hs(c, xr, xi):
        cs = pl.ds(c * lc, lc)
        for k, slot in ((1, 0), (2, 2), (4, 4)):
            ar, ai = pw[slot, :, cs], pw[slot + 1, :, cs]
            if rev:
                sr, si = pltpu.roll(xr, SUBLANE - k, 0), pltpu.roll(xi, SUBLANE - k, 0)
                keep = row < SUBLANE - k
            else:
                sr, si = pltpu.roll(xr, k, 0), pltpu.roll(xi, k, 0)
                keep = row >= k
            sr, si = jnp.where(keep, sr, 0.0), jnp.where(keep, si, 0.0)
            pr, pi = cmul(ar, ai, sr, si)
            xr, xi = xr + pr, xi + pi
        return xr, xi

    @pl.when(pl.program_id(0) == 0)
    def _():
        last[...] = jnp.zeros(last.shape, F32)
        if da_ref is not None:
            da_ref[...] = jnp.zeros(da_ref.shape, F32)
        for c in range(nc):
            cs = pl.ds(c * lc, lc)
            a1r = jnp.broadcast_to(are_ref[:, cs], (SUBLANE, lc))
            a1i = jnp.broadcast_to(aim_ref[:, cs], (SUBLANE, lc))
            a2r, a2i = cmul(a1r, a1i, a1r, a1i)
            a4r, a4i = cmul(a2r, a2i, a2r, a2i)
            for slot, v in enumerate((a1r, a1i, a2r, a2i, a4r, a4i)):
                pw[slot, :, cs] = v
            first = SUBLANE - 1 if rev else 0
            pr, pi = hs(c, jnp.where(row == first, a1r, 0.0), jnp.where(row == first, a1i, 0.0))
            pw[6, :, cs] = pr
            pw[7, :, cs] = pi

    edge = 0 if rev else SUBLANE - 1

    def blk(j, carry):
        jj = (nb - 1 - j) if rev else j
        r0 = pl.multiple_of(jj * SUBLANE, SUBLANE)
        for c in range(nc):
            cre, cim = pl.ds(c * lc, lc), pl.ds(S5_N + c * lc, lc)
            xr, xi = hs(c, b_ref[pl.ds(r0, SUBLANE), cre], b_ref[pl.ds(r0, SUBLANE), cim])
            cr = jnp.broadcast_to(last[edge:edge + 1, cre], (SUBLANE, lc))
            ci = jnp.broadcast_to(last[edge:edge + 1, cim], (SUBLANE, lc))
            pr, pi = cmul(pw[6, :, cre], pw[7, :, cre], cr, ci)
            xr, xi = xr + pr, xi + pi
            if da_ref is not None:
                if rev:
                    qr, qi = pltpu.roll(xr, SUBLANE - 1, 0), pltpu.roll(xi, SUBLANE - 1, 0)
                    fill = row == SUBLANE - 1
                else:
                    qr, qi = pltpu.roll(xr, 1, 0), pltpu.roll(xi, 1, 0)
                    fill = row == 0
                qr, qi = jnp.where(fill, cr, qr), jnp.where(fill, ci, qi)
                zr, zi = xf32[pl.ds(r0, SUBLANE), cre], xf32[pl.ds(r0, SUBLANE), cim]
                da_ref[:, cre] += zr * qr + zi * qi
                da_ref[:, cim] += zr * qi - zi * qr
            last[:, cre] = xr
            last[:, cim] = xi
            of32[pl.ds(r0, SUBLANE), cre] = xr
            of32[pl.ds(r0, SUBLANE), cim] = xi
        return carry

    lax.fori_loop(0, nb, blk, 0)


def _s5_tiles():
    return [(slice((j % S5_TILES) * S5_TC, (j % S5_TILES + 1) * S5_TC), slice(j * S5_TS, (j + 1) * S5_TS))
            for j in range(2 * S5_TILES)]


def _s5_scratch(tb):
    return [pltpu.VMEM((8, SUBLANE, S5_N), F32), pltpu.VMEM((SUBLANE, 2 * S5_N), F32),
            pltpu.VMEM((tb, 2 * S5_N), F32), pltpu.VMEM((tb, 2 * S5_N), F32),
            pltpu.VMEM((S5_TC, 2 * S5_N), BF16), pltpu.VMEM((2 * S5_N, S5_TC), BF16)]


def _s5_fwd(name, z, a_re, a_im, wb, wc, add, *, rev, tb=256):
    n_tok = z.shape[0]
    tb = _tile(n_tok, tb, 16)
    nt = n_tok // tb
    has_add = add is not None

    def body(*refs):
        u_ref, are_ref, aim_ref, wb_ref, wc_ref = refs[:5]
        x_ref, y_ref, pw, last, of32, bu, wb16, wc16 = refs[5 + has_add:]

        @pl.when(pl.program_id(0) == 0)
        def _():
            wb16[...] = wb_ref[...].astype(BF16)
            wc16[...] = wc_ref[...].astype(BF16)

        u16 = u_ref[...].astype(BF16)
        for ch, st in _s5_tiles():
            bu[:, st] = _dot(u16[:, ch], wb16[:, st])
        _scan_core(bu, are_ref, aim_ref, pw, last, of32, rev=rev)
        x16 = of32[...].astype(BF16)
        x_ref[...] = x16
        ys = []
        for k in range(S5_TILES):
            re, im = slice(k * S5_TS, (k + 1) * S5_TS), slice(S5_N + k * S5_TS, S5_N + (k + 1) * S5_TS)
            ys.append(_dot(x16[:, re], wc16[re, :]) + _dot(x16[:, im], wc16[im, :]))
        y = jnp.concatenate(ys, axis=1)
        y_ref[...] = y + refs[5][...] if has_add else y

    tmap = (lambda t: (nt - 1 - t, 0)) if rev else (lambda t: (t, 0))
    const = lambda t: (0, 0)
    in_specs = [pl.BlockSpec((tb, S5_W), tmap), pl.BlockSpec((1, S5_N), const), pl.BlockSpec((1, S5_N), const),
                pl.BlockSpec((S5_TC, 2 * S5_N), const), pl.BlockSpec((2 * S5_N, S5_TC), const)]
    ops = [z, a_re, a_im, wb, wc]
    if has_add:
        in_specs.append(pl.BlockSpec((tb, S5_W), tmap))
        ops.append(add)
    return _pcall(
        body, name=name, grid=(nt,), in_specs=in_specs,
        out_specs=[pl.BlockSpec((tb, 2 * S5_N), tmap), pl.BlockSpec((tb, S5_W), tmap)],
        out_shape=[jax.ShapeDtypeStruct((n_tok, 2 * S5_N), BF16), jax.ShapeDtypeStruct((n_tok, S5_W), F32)],
        scratch_shapes=_s5_scratch(tb), compiler_params=_cparams(("arbitrary",), 10 * tb * 2 * S5_N * 4),
    )(*ops)


def _s5_bwd(name, dy, z, xs, a_re, a_im, wb, wc, add, *, rev, tb=256):
    n_tok = z.shape[0]
    tb = _tile(n_tok, tb, 16)
    nt = n_tok // tb

    def body(dy_ref, u_ref, xs_ref, are_ref, aim_ref, wb_ref, wc_ref, add_ref, du_ref, dwb_ref, dwc_ref, da_ref,
             pw, last, of32, gx, wb16, wc16, xf32):
        @pl.when(pl.program_id(0) == 0)
        def _():
            wb16[...] = wb_ref[...].astype(BF16)
            wc16[...] = wc_ref[...].astype(BF16)
            dwb_ref[...] = jnp.zeros(dwb_ref.shape, F32)
            dwc_ref[...] = jnp.zeros(dwc_ref.shape, F32)

        dy16, u16, x16 = dy_ref[...].astype(BF16), u_ref[...].astype(BF16), xs_ref[...]
        for ch, st in _s5_tiles():
            gx[:, st] = _dot(dy16[:, ch], wc16[st, :], _NT)
        xf32[...] = x16.astype(F32)
        _scan_core(gx, are_ref, aim_ref, pw, last, of32, rev=rev, xf32=xf32, da_ref=da_ref)
        g16 = of32[...].astype(BF16)
        dus = []
        for k in range(S5_TILES):
            re, im = slice(k * S5_TS, (k + 1) * S5_TS), slice(S5_N + k * S5_TS, S5_N + (k + 1) * S5_TS)
            dus.append(_dot(g16[:, re], wb16[:, re], _NT) + _dot(g16[:, im], wb16[:, im], _NT))
        du_ref[...] = jnp.concatenate(dus, axis=1) + add_ref[...]
        for ch, st in _s5_tiles():
            dwb_ref[:, st] += _dot(u16[:, ch], g16[:, st], _TN)
            dwc_ref[st, :] += _dot(x16[:, st], dy16[:, ch], _TN)

    tmap = (lambda t: (nt - 1 - t, 0)) if rev else (lambda t: (t, 0))
    const = lambda t: (0, 0)
    row_w, row_s = pl.BlockSpec((tb, S5_W), tmap), pl.BlockSpec((tb, 2 * S5_N), tmap)
    return _pcall(
        body, name=name, grid=(nt,),
        in_specs=[row_w, row_w, row_s, pl.BlockSpec((1, S5_N), const), pl.BlockSpec((1, S5_N), const),
                  pl.BlockSpec((S5_TC, 2 * S5_N), const), pl.BlockSpec((2 * S5_N, S5_TC), const), row_w],
        out_specs=[row_w, pl.BlockSpec((S5_TC, 2 * S5_N), const), pl.BlockSpec((2 * S5_N, S5_TC), const),
                   pl.BlockSpec((SUBLANE, 2 * S5_N), const)],
        out_shape=[jax.ShapeDtypeStruct((n_tok, S5_W), F32), jax.ShapeDtypeStruct((S5_TC, 2 * S5_N), F32),
                   jax.ShapeDtypeStruct((2 * S5_N, S5_TC), F32), jax.ShapeDtypeStruct((SUBLANE, 2 * S5_N), F32)],
        scratch_shapes=_s5_scratch(tb) + [pltpu.VMEM((tb, 2 * S5_N), F32)],
        compiler_params=_cparams(("arbitrary",), 12 * tb * 2 * S5_N * 4),
    )(dy, z, xs, a_re, a_im, wb, wc, add)


GLA_TB = 512


def _gla_block_terms(la, q, k, rev):
    tb = la.shape[0]
    ri = lax.broadcasted_iota(jnp.int32, (tb, tb), 0)
    ci = lax.broadcasted_iota(jnp.int32, (tb, tb), 1)
    same = (ri // GLA_C) == (ci // GLA_C)
    tri = same & ((ri <= ci) if rev else (ri >= ci))
    trif = tri.astype(F32)
    bcum = _dot(trif, la, precision=HI)
    blast = _dot(same.astype(F32), la, precision=HI)
    eb, enb, ee = jnp.exp(bcum), jnp.exp(-bcum), jnp.exp(blast - bcum)
    return trif, jnp.exp(blast), eb, enb, ee, q * eb, k * enb, k * ee


def _chunk_mask(rev):
    ri = lax.broadcasted_iota(jnp.int32, (GLA_C, GLA_C), 0)
    ci = lax.broadcasted_iota(jnp.int32, (GLA_C, GLA_C), 1)
    return (ri <= ci) if rev else (ri >= ci)


def _gla_fwd(name, z, la, *, rev):
    n_tok = z.shape[0]
    tb = _tile(n_tok, GLA_TB, GLA_C)
    nblk, per = n_tok // tb, tb // GLA_C
    scale = GLA_D ** -0.5

    def body(q_ref, k_ref, v_ref, la_ref, o_ref, s_ref, st, qd_s, kd_s, ke_s, v_s, dec_s):
        @pl.when(pl.program_id(0) == 0)
        def _():
            st[...] = jnp.zeros(st.shape, F32)

        _, dec, _, _, _, qd, kd, ke = _gla_block_terms(la_ref[...], q_ref[...] * scale, k_ref[...], rev)
        qd_s[...], kd_s[...], ke_s[...] = qd.astype(BF16), kd.astype(BF16), ke.astype(BF16)
        v_s[...] = v_ref[...].astype(BF16)
        dec_s[...] = dec
        tri = _chunk_mask(rev)
        for step in range(per):
            t = per - 1 - step if rev else step
            rows = pl.ds(t * GLA_C, GLA_C)
            for h in range(GLA_H):
                lanes = pl.ds(h * GLA_D, GLA_D)
                qd16, v16 = qd_s[rows, lanes], v_s[rows, lanes]
                sc = jnp.where(tri, _dot(qd16, kd_s[rows, lanes], _NT), 0.0)
                s16 = st[h].astype(BF16)
                s_ref[h, step] = s16
                o_ref[rows, lanes] = _dot(sc.astype(BF16), v16) + _dot(qd16, s16, _NT)
                st[h] = st[h] * dec_s[pl.ds(t * GLA_C, 1), lanes] + _dot(v16, ke_s[rows, lanes], _TN)

    bmap = (lambda n: nblk - 1 - n) if rev else (lambda n: n)

    def zspec(cb):
        return pl.BlockSpec((tb, GLA_W), lambda n: (bmap(n), cb))

    half = pltpu.VMEM((tb, GLA_W), BF16)
    return _pcall(
        body, name=name, grid=(nblk,),
        in_specs=[zspec(1), zspec(2), zspec(3), zspec(0)],
        out_specs=[zspec(0), pl.BlockSpec((GLA_H, per, GLA_D, GLA_D), lambda n: (0, n, 0, 0))],
        out_shape=[jax.ShapeDtypeStruct((n_tok, GLA_W), F32),
                   jax.ShapeDtypeStruct((GLA_H, n_tok // GLA_C, GLA_D, GLA_D), BF16)],
        scratch_shapes=[pltpu.VMEM((GLA_H, GLA_D, GLA_D), F32), half, half, half, half, pltpu.VMEM((tb, GLA_W), F32)],
        compiler_params=_cparams(("arbitrary",), 40 * tb * GLA_W * 4),
    )(z, z, z, la)


def _gla_bwd(name, z, la, s_prev, do, *, rev):
    n_tok = z.shape[0]
    tb = _tile(n_tok, GLA_TB, GLA_C)
    nblk, per = n_tok // tb, tb // GLA_C
    scale = GLA_D ** -0.5

    def body(q_ref, k_ref, v_ref, la_ref, s_ref, do_ref, dq_ref, dk_ref, dv_ref, dla_ref,
             dst, qd_s, kd_s, ke_s, v_s, do_s, dec_s, dqd_s, dkd_s, dke_s, dd_s):
        @pl.when(pl.program_id(0) == 0)
        def _():
            dst[...] = jnp.zeros(dst.shape, F32)

        trif, dec, eb, enb, ee, qd, kd, ke = _gla_block_terms(la_ref[...], q_ref[...] * scale, k_ref[...], rev)
        qd_s[...], kd_s[...], ke_s[...] = qd.astype(BF16), kd.astype(BF16), ke.astype(BF16)
        v_s[...] = v_ref[...].astype(BF16)
        do_s[...] = do_ref[...].astype(BF16)
        dec_s[...] = dec
        tri = _chunk_mask(rev)
        for step in range(per):
            t = step if rev else per - 1 - step
            rows = pl.ds(t * GLA_C, GLA_C)
            for h in range(GLA_H):
                lanes = pl.ds(h * GLA_D, GLA_D)
                qd16, kd16, ke16 = qd_s[rows, lanes], kd_s[rows, lanes], ke_s[rows, lanes]
                v16, do16 = v_s[rows, lanes], do_s[rows, lanes]
                sc16 = jnp.where(tri, _dot(qd16, kd16, _NT), 0.0).astype(BF16)
                dsc16 = jnp.where(tri, _dot(do16, v16, _NT), 0.0).astype(BF16)
                s16 = s_ref[h, per - 1 - step]
                dsn = dst[h]
                dsn16 = dsn.astype(BF16)
                dec_c = dec_s[pl.ds(t * GLA_C, 1), lanes]
                dqd_s[rows, lanes] = _dot(dsc16, kd16) + _dot(do16, s16)
                dkd_s[rows, lanes] = _dot(dsc16, qd16, _TN)
                dke_s[rows, lanes] = _dot(v16, dsn16)
                dv_ref[rows, lanes] = _dot(sc16, do16, _TN) + _dot(ke16, dsn16, _NT)
                ddec = jnp.sum(dsn * s16.astype(F32), axis=0, keepdims=True) * dec_c
                dd_s[rows, lanes] = jnp.broadcast_to(ddec, (GLA_C, GLA_D))
                dst[h] = dsn * dec_c + _dot(do16, qd16, _TN)
        dqd, dkd, dke = dqd_s[...], dkd_s[...], dke_s[...]
        dq_ref[...] = dqd * eb * scale
        dk_ref[...] = dkd * enb + dke * ee
        db = dqd * qd - dkd * kd - dke * ke
        ri = lax.broadcasted_iota(jnp.int32, (tb, tb), 0)
        ci = lax.broadcasted_iota(jnp.int32, (tb, tb), 1)
        same = ((ri // GLA_C) == (ci // GLA_C)).astype(F32)
        dla_ref[...] = _dot(trif, db, _TN, precision=HI) + _dot(same, dke * ke, precision=HI) + dd_s[...]

    bmap = (lambda n: n) if rev else (lambda n: nblk - 1 - n)

    def zspec(cb):
        return pl.BlockSpec((tb, GLA_W), lambda n: (bmap(n), cb))

    o4 = jax.ShapeDtypeStruct((n_tok, GLA_W), F32)
    half, full = pltpu.VMEM((tb, GLA_W), BF16), pltpu.VMEM((tb, GLA_W), F32)
    return _pcall(
        body, name=name, grid=(nblk,),
        in_specs=[zspec(1), zspec(2), zspec(3), zspec(0),
                  pl.BlockSpec((GLA_H, per, GLA_D, GLA_D), lambda n: (0, nblk - 1 - n, 0, 0)), zspec(0)],
        out_specs=[zspec(0)] * 4, out_shape=[o4] * 4,
        scratch_shapes=[pltpu.VMEM((GLA_H, GLA_D, GLA_D), F32), half, half, half, half, half, full, full, full, full, full],
        compiler_params=_cparams(("arbitrary",), 64 * tb * GLA_W * 4),
    )(z, z, z, la, s_prev, do)


_PAIRS = AT_W // LANE
_PAIRS_PER_KV = _PAIRS // AT_KH


def _attn_fwd(name, q, kd, vd, tq=512):
    n_tok = q.shape[0]
    tq = _tile(n_tok, tq, 16)

    def body(q_ref, k_ref, v_ref, o_ref, l_ref):
        qv, kv, vv = q_ref[...], k_ref[...], v_ref[...]
        low = lax.broadcasted_iota(jnp.int32, qv.shape, 1) < AT_D
        res = []
        for keep in (low, jnp.logical_not(low)):
            s = _dot(jnp.where(keep, qv, jnp.zeros_like(qv)), kv, _NT)
            m = jnp.max(s, axis=-1, keepdims=True)
            p = jnp.exp((s - m).astype(BF16))
            den = jnp.sum(p.astype(F32), axis=-1, keepdims=True)
            res.append((_dot(p, vv) / den, m + jnp.log(den)))
        o_ref[...] = jnp.where(low, res[0][0], res[1][0])
        l_ref[...] = jnp.where(low, res[0][1], res[1][1])

    qspec = pl.BlockSpec((tq, LANE), lambda p, i: (i, p))
    kspec = pl.BlockSpec((n_tok, LANE), lambda p, i: (0, p // _PAIRS_PER_KV))
    vm = 5 * _nbytes((tq, n_tok), F32) + 8 * _nbytes((n_tok, LANE), BF16)
    out = jax.ShapeDtypeStruct((n_tok, AT_W), F32)
    return _pcall(
        body, name=name, grid=(_PAIRS, n_tok // tq), in_specs=[qspec, kspec, kspec], out_specs=[qspec, qspec],
        out_shape=[out, out], compiler_params=_cparams(("parallel", "parallel"), vm),
    )(q, kd, vd)


def _attn_bwd(name, q, kd, vd, o, lse, do, tq=512):
    n_tok = q.shape[0]
    tq = _tile(n_tok, tq, 16)

    def body(q_ref, k_ref, v_ref, o_ref, l_ref, do_ref, dq_ref, dk_ref, dv_ref):
        @pl.when((pl.program_id(1) == 0) & (pl.program_id(2) == 0))
        def _():
            dk_ref[...] = jnp.zeros(dk_ref.shape, F32)
            dv_ref[...] = jnp.zeros(dv_ref.shape, F32)

        qv, kv, vv = q_ref[...], k_ref[...], v_ref[...]
        dof, lv = do_ref[...], l_ref[...]
        do16 = dof.astype(BF16)
        doo = dof * o_ref[...]
        low = lax.broadcasted_iota(jnp.int32, qv.shape, 1) < AT_D
        dq = jnp.zeros(qv.shape, F32)
        for keep in (low, jnp.logical_not(low)):
            qm = jnp.where(keep, qv, jnp.zeros_like(qv))
            dom = jnp.where(keep, do16, jnp.zeros_like(do16))
            lh = jnp.max(jnp.where(keep, lv, -1e30), axis=-1, keepdims=True)
            p = jnp.exp((_dot(qm, kv, _NT) - lh).astype(BF16))
            delta = jnp.sum(jnp.where(keep, doo, 0.0), axis=-1, keepdims=True)
            ds16 = p * (_dot(dom, vv, _NT) - delta).astype(BF16)
            dq = jnp.where(keep, _dot(ds16, kv), dq)
            dk_ref[...] += _dot(ds16, qm, _TN)
            dv_ref[...] += _dot(p, dom, _TN)
        dq_ref[...] = dq

    qspec = pl.BlockSpec((tq, LANE), lambda g, j, i: (i, g * _PAIRS_PER_KV + j))
    kspec = pl.BlockSpec((n_tok, LANE), lambda g, j, i: (0, g))
    vm = 7 * _nbytes((tq, n_tok), F32) + 12 * _nbytes((n_tok, LANE), F32)
    dup = jax.ShapeDtypeStruct((n_tok, AT_KH * LANE), F32)
    return _pcall(
        body, name=name, grid=(AT_KH, _PAIRS_PER_KV, n_tok // tq),
        in_specs=[qspec, kspec, kspec, qspec, qspec, qspec], out_specs=[qspec, kspec, kspec],
        out_shape=[jax.ShapeDtypeStruct((n_tok, AT_W), F32), dup, dup],
        compiler_params=_cparams(("parallel", "arbitrary", "arbitrary"), vm),
    )(q, kd, vd, o, lse, do)


def _s5_prep(lam_re, lam_im, log_dt, b_re, b_im, c_re, c_im):
    dt = jnp.exp(log_dt)[:, None]
    er = jnp.exp(lam_re * dt)
    a_re, a_im = er * jnp.cos(lam_im * dt), er * jnp.sin(lam_im * dt)
    nr, ni = a_re - 1.0, a_im
    den = lam_re * lam_re + lam_im * lam_im
    f_re, f_im = (nr * lam_re + ni * lam_im) / den, (ni * lam_re - nr * lam_im) / den
    bb_re = f_re[..., None] * b_re - f_im[..., None] * b_im
    bb_im = f_re[..., None] * b_im + f_im[..., None] * b_re
    def tiles(t, axis):
        _, a, b = t.shape
        eye = jnp.eye(S5_TG, dtype=t.dtype)
        blocks = t.reshape(S5_TILES, S5_TG, a, 1, b) * eye[None, :, None, :, None]
        if axis == 0:
            return blocks.reshape(S5_TILES * S5_TG * a, S5_TG * b)
        return blocks.transpose(1, 2, 0, 3, 4).reshape(S5_TG * a, S5_TILES * S5_TG * b)

    w_in = jnp.concatenate([tiles(bb_re.transpose(0, 2, 1), 1), tiles(bb_im.transpose(0, 2, 1), 1)], axis=1)
    w_out = jnp.concatenate([tiles(c_re.transpose(0, 2, 1), 0), -tiles(c_im.transpose(0, 2, 1), 0)], axis=0)
    return a_re.reshape(1, S5_N), a_im.reshape(1, S5_N), w_in, w_out


def _rope_tables(n_tok):
    quarter = AT_D // 4
    d = np.arange(AT_D)
    e = d % (AT_D // 2)
    inv = (ROPE_BASE ** (-(e % quarter).astype(np.float64) * 2.0 / (AT_D // 2))).astype(np.float32)
    sign = np.where(e < quarter, -1.0, 1.0).astype(np.float32)
    rot = np.zeros((AT_D, AT_D), np.float32)
    rot[np.where(e < quarter, d + quarter, d - quarter), d] = 1.0
    pos = jnp.arange(n_tok, dtype=jnp.int32)
    axis_pos = jnp.where(jnp.asarray(d // (AT_D // 2) == 0)[None, :], (pos // GRID_W)[:, None], (pos % GRID_W)[:, None])
    ang = axis_pos.astype(F32) * jnp.asarray(inv)[None, :]
    return jnp.cos(ang), jnp.sin(ang) * jnp.asarray(sign)[None, :], rot


def _head_consts(n_tok, heads):
    cos, sin, rot = _rope_tables(n_tok)
    eye = np.eye(heads, dtype=np.float32)
    mean = np.kron(eye, np.full((AT_D, AT_D), 1.0 / AT_D, np.float32))
    return jnp.tile(cos, (1, heads)), jnp.tile(sin, (1, heads)), jnp.asarray(mean), jnp.asarray(np.kron(eye, rot))


def _pad_w_in(w):
    return jnp.concatenate([w[:, :2560], w[:, 2592:3360], w[:, 2560:2592], jnp.zeros((w.shape[0], 96), w.dtype)], axis=1)


def _unpad_w_in(g):
    return jnp.concatenate([g[:, :2560], g[:, 3328:3360], g[:, 2560:3328]], axis=1)


def _ffn_fwd(x, g, wg, wu, wd, d):
    n_tok = x.shape[0]
    fq = wd.shape[-2]
    tm = _tile(n_tok, 512, 16)

    def up(x_ref, g_ref, wg_ref, wu_ref, h_ref, a_ref, b_ref, s_ref):
        @pl.when(pl.program_id(1) == 0)
        def _():
            h_ref[...] = _rms(x_ref[...], g_ref[...]).astype(BF16)

        hv = h_ref[...]
        av, bv = _dot(hv, wg_ref[0, 0], _NT), _dot(hv, wu_ref[0, 0], _NT)
        a_ref[0] = av.astype(BF16)
        b_ref[0] = bv.astype(BF16)
        s_ref[0] = (av * _sigmoid(av) * bv).astype(BF16)

    wspec = pl.BlockSpec((1, 1, fq, D_MODEL), lambda i, j: (j, d, 0, 0))
    aspec = pl.BlockSpec((1, tm, fq), lambda i, j: (j, i, 0))
    xspec = pl.BlockSpec((tm, D_MODEL), lambda i, j: (i, 0))
    act = jax.ShapeDtypeStruct((N_CHIPS, n_tok, fq), BF16)
    h, a, b, s = _pcall(
        up, name="ffn_up", grid=(n_tok // tm, N_CHIPS),
        in_specs=[xspec, pl.BlockSpec((1, D_MODEL), lambda i, j: (0, 0)), wspec, wspec],
        out_specs=[xspec, aspec, aspec, aspec],
        out_shape=[jax.ShapeDtypeStruct((n_tok, D_MODEL), BF16), act, act, act],
        compiler_params=_cparams(("parallel", "arbitrary"), 4 * D_MODEL * fq * 2 + 16 * tm * fq * 4 + 8 * tm * D_MODEL * 4),
    )(x, g, wg, wu)
    tn = _tile(D_MODEL, 512, LANE)

    def down(s_ref, w_ref, x_ref, o_ref):
        acc = _dot(s_ref[0], w_ref[0, 0])
        for j in range(1, N_CHIPS):
            acc = acc + _dot(s_ref[j], w_ref[j, 0])
        o_ref[...] = x_ref[...] + 0.5 * acc

    xo = _pcall(
        down, name="ffn_down", grid=(n_tok // tm, D_MODEL // tn),
        in_specs=[pl.BlockSpec((N_CHIPS, tm, fq), lambda i, n: (0, i, 0)),
                  pl.BlockSpec((N_CHIPS, 1, fq, tn), lambda i, n: (0, d, 0, n)),
                  pl.BlockSpec((tm, tn), lambda i, n: (i, n))],
        out_specs=pl.BlockSpec((tm, tn), lambda i, n: (i, n)), out_shape=jax.ShapeDtypeStruct((n_tok, D_MODEL), F32),
        compiler_params=_cparams(("parallel", "parallel"), 4 * N_CHIPS * (tm + tn) * fq * 2 + 6 * tm * tn * 4),
    )(s, wd, x)
    return xo, (x, h, a, b, s)


def _ffn_bwd(dy, dy16, saved, g, wg, wu, wd, d, into):
    x, h, a, b, s = saved
    n_tok = x.shape[0]
    fq = wd.shape[-2]
    tm = _tile(n_tok, 512, 16)

    def dact(dy_ref, w_ref, a_ref, b_ref, da_ref, db_ref):
        ds = 0.5 * _dot(dy_ref[...], w_ref[0, 0], _NT)
        av, bv = a_ref[0].astype(F32), b_ref[0].astype(F32)
        sg = _sigmoid(av)
        da_ref[0] = (ds * bv * sg * (1.0 + av * (1.0 - sg))).astype(BF16)
        db_ref[0] = (ds * av * sg).astype(BF16)

    aspec = pl.BlockSpec((1, tm, fq), lambda i, j: (j, i, 0))
    act = jax.ShapeDtypeStruct((N_CHIPS, n_tok, fq), BF16)
    da, db = _pcall(
        dact, name="ffn_dact", grid=(n_tok // tm, N_CHIPS),
        in_specs=[pl.BlockSpec((tm, D_MODEL), lambda i, j: (i, 0)),
                  pl.BlockSpec((1, 1, fq, D_MODEL), lambda i, j: (j, d, 0, 0)), aspec, aspec],
        out_specs=[aspec, aspec], out_shape=[act, act],
        compiler_params=_cparams(("parallel", "parallel"), 4 * D_MODEL * fq * 2 + 16 * tm * fq * 4),
    )(dy16, wd, a, b)

    any_spec = pl.BlockSpec(memory_space=pl.ANY)
    alias = {} if into is None else {2: 0}
    tn = _tile(D_MODEL, 512, LANE)

    def dw_call(t, rhs, scale, prev):
        def dw(t_ref, r_ref, *rest):
            rest[-1][0, 0] = scale * _dot(t_ref[0], r_ref[...], _TN)

        return _pcall(
            dw, name="ffn_dw", grid=(N_CHIPS, D_MODEL // tn),
            in_specs=[pl.BlockSpec((1, n_tok, fq), lambda j, n: (j, 0, 0)), pl.BlockSpec((n_tok, tn), lambda j, n: (0, n))]
            + ([] if prev is None else [any_spec]),
            out_specs=pl.BlockSpec((1, 1, fq, tn), lambda j, n: (j, d, 0, n)),
            out_shape=jax.ShapeDtypeStruct((N_CHIPS, DEPTH, fq, D_MODEL), F32), input_output_aliases=alias,
            compiler_params=_cparams(("parallel", "parallel"), 6 * n_tok * (fq + tn) * 2 + 4 * fq * tn * 4),
        )(t, rhs, *([] if prev is None else [prev]))

    dwg = dw_call(da, h, 1.0, None if into is None else into[0])
    dwu = dw_call(db, h, 1.0, None if into is None else into[1])
    dwd = dw_call(s, dy16, 0.5, None if into is None else into[2])
    tr = _tile(n_tok, 256, 16)

    def dh_norm(da_ref, db_ref, wg_ref, wu_ref, dy_ref, x_ref, g_ref, dx_ref, dx16_ref, dg_ref):
        dh = _dot(da_ref[0], wg_ref[0, 0]) + _dot(db_ref[0], wu_ref[0, 0])
        for j in range(1, N_CHIPS):
            dh = dh + _dot(da_ref[j], wg_ref[j, 0]) + _dot(db_ref[j], wu_ref[j, 0])
        dx, dg = _rms_bwd(x_ref[...], g_ref[...], dh)
        dx = dy_ref[...] + dx
        dx_ref[...] = dx
        dx16_ref[...] = dx.astype(BF16)

        @pl.when(pl.program_id(0) == 0)
        def _():
            dg_ref[...] = jnp.zeros(dg_ref.shape, F32)

        dg_ref[...] += dg

    tspec = pl.BlockSpec((N_CHIPS, tr, fq), lambda i: (0, i, 0))
    wspec = pl.BlockSpec((N_CHIPS, 1, fq, D_MODEL), lambda i: (0, d, 0, 0))
    rspec = pl.BlockSpec((tr, D_MODEL), lambda i: (i, 0))
    vspec = pl.BlockSpec((1, D_MODEL), lambda i: (0, 0))
    dx, dx16, dg = _pcall(
        dh_norm, name="ffn_dh_norm", grid=(n_tok // tr,),
        in_specs=[tspec, tspec, wspec, wspec, rspec, rspec, vspec], out_specs=[rspec, rspec, vspec],
        out_shape=[jax.ShapeDtypeStruct((n_tok, D_MODEL), F32), jax.ShapeDtypeStruct((n_tok, D_MODEL), BF16),
                   jax.ShapeDtypeStruct((1, D_MODEL), F32)],
        compiler_params=_cparams(("arbitrary",), 4 * N_CHIPS * D_MODEL * fq * 2 + 4 * N_CHIPS * tr * fq * 2
                                 + 16 * tr * D_MODEL * 4),
    )(da, db, wg, wu, dy, x, g)
    return dx, dx16, dg[0], (dwg, dwu, dwd)


def _alpha_pads(w_alpha):
    z = jnp.zeros((LANE - 2 * GLA_R, GLA_W), F32)
    wf = jnp.concatenate([w_alpha[0], jnp.zeros((GLA_R, GLA_W), F32), z], axis=0)
    wb = jnp.concatenate([jnp.zeros((GLA_R, GLA_W), F32), w_alpha[1], z], axis=0)
    return wf, wb


def _dup_heads(t):
    low = lax.broadcasted_iota(jnp.int32, t.shape, 1) < AT_D
    h0, h1 = jnp.where(low, t, 0.0), jnp.where(low, 0.0, t)
    return jnp.concatenate([h0 + pltpu.roll(h0, AT_D, 1), h1 + pltpu.roll(h1, AT_D, 1)], axis=1)


def _fold_heads(t):
    low = lax.broadcasted_iota(jnp.int32, (t.shape[0], LANE), 1) < AT_D
    g0, g1 = t[:, :LANE], t[:, LANE:]
    return jnp.where(low, g0 + pltpu.roll(g0, AT_D, 1), g1 + pltpu.roll(g1, AT_D, 1))


def _mixer_fwd(x, w):
    n_tok = x.shape[0]
    h = _rowwise("mix_norm", lambda xv, gv: _rms(xv, gv), [x], [w['mix_norm']], [(D_MODEL, BF16)])[0]
    z = _mm("mix_in", h, w['w_in_p'])
    gp = _mm("mix_gate", h, w['w_merge_gate'], out_dtype=BF16)
    u = _cw(z, 0, S5_W)

    xs, yy = [], None
    for d, rev in ((0, False), (1, True)):
        a_re, a_im, wb, wc = w['s5'][d]
        x_d, yy = _s5_fwd("s5_rev" if rev else "s5_fwd", z, a_re, a_im, wb, wc, yy, rev=rev)
        xs.append(x_d)

    def s5_post(yv, uv, dv, wglu):
        y0 = yv + dv * uv
        y1 = _gelu(y0)
        t = _dot(y1.astype(BF16), wglu)
        return y0, t, y1 * _sigmoid(t)

    y0, t_glu, y_s5 = _rowwise("s5_post", s5_post, [yy, u], [w['s5_d'], w['s5_w_glu']],
                               [(S5_W, F32), (S5_W, F32), (S5_W, BF16)])

    wa_f, wa_b = _alpha_pads(w['gla_w_alpha'])
    ba_f, ba_b = w['gla_b_alpha'][0:1], w['gla_b_alpha'][1:2]
    zz = _cw(z, ZC_Z, LANE)

    def gla_prep(zv, waf, wab, bf, bb):
        z16 = zv.astype(BF16)
        lf = _dot(z16, waf.astype(BF16)) + bf
        lb = _dot(z16, wab.astype(BF16)) + bb
        return _log_sigmoid(lf) / GLA_TAU, _log_sigmoid(lb) / GLA_TAU

    la_f, la_b = _rowwise("gla_prep", gla_prep, [zz], [wa_f, wa_b, ba_f, ba_b], [(GLA_W, F32), (GLA_W, F32)])
    o_f, s_f = _gla_fwd("gla_fwd", z, la_f, rev=False)
    o_b, s_b = _gla_fwd("gla_rev", z, la_b, rev=True)
    gate = _cw(z, 4, GLA_W)

    def gla_post(of, ob, gv, gn):
        o = of + ob
        outs = []
        for hh in range(GLA_H):
            sl = slice(hh * GLA_D, (hh + 1) * GLA_D)
            gh = gv[:, sl]
            outs.append(_rms(o[:, sl], gn) * (gh * _sigmoid(gh)))
        return jnp.concatenate(outs, axis=1)

    y_gla = _rowwise("gla_post", gla_post, [o_f, o_b, gate], [w['gla_norm']], [(GLA_W, BF16)])[0]

    cq, sq, mq, rq = _head_consts(n_tok, AT_QH)
    ck, sk, mk, rk = _head_consts(n_tok, AT_KH)
    gq = jnp.tile(w['attn_q_norm'], (1, AT_QH))
    gk = jnp.tile(w['attn_k_norm'], (1, AT_KH))

    def rope_fwd(v, gain, cos, sin, mean, rot, scale):
        r = lax.rsqrt(_dot(v * v, mean, precision=HI) + EPS)
        vn = v * r * gain
        return (vn * cos + _dot(vn, rot, precision=HI) * sin) * scale

    def attn_prep(qv, kv, vv, cqv, sqv, ckv, skv, gqv, gkv, mqv, rqv, mkv, rkv):
        kr = rope_fwd(kv, gkv, ckv, skv, mkv, rkv, 1.0)
        return rope_fwd(qv, gqv, cqv, sqv, mqv, rqv, AT_D ** -0.5), _dup_heads(kr), _dup_heads(vv)

    q_raw, k_raw, v_raw = _cw(z, 5, AT_W), _cw(z, ZC_AK, AT_KW), _cw(z, ZC_AV, AT_KW)
    q16, kd, vd = _rowwise("attn_prep", attn_prep, [q_raw, k_raw, v_raw, cq, sq, ck, sk], [gq, gk, mq, rq, mk, rk],
                           [(AT_W, BF16), (2 * LANE, BF16), (2 * LANE, BF16)])
    y_at, lse = _attn_fwd("attn_fwd", q16, kd, vd)

    p0 = _mm("mix_branch", y_s5, w['w_branch_s5'], out_dtype=BF16)
    p1 = _mm("mix_branch", y_gla, w['w_branch_gla'], out_dtype=BF16)
    p2 = _mm("mix_branch_f32", y_at, w['w_branch_attn'], out_dtype=BF16)

    def merge(gpv, a0, a1, a2, bias):
        g = _sigmoid(gpv.astype(F32) + bias)
        return (g[:, :D_MODEL] * a0.astype(F32) + g[:, D_MODEL:2 * D_MODEL] * a1.astype(F32)
                + g[:, 2 * D_MODEL:] * a2.astype(F32))

    merged = _rowwise("mix_merge", merge, [gp, p0, p1, p2], [w['b_merge_gate']], [(D_MODEL, BF16)])[0]
    xo = _mm("mix_out", merged, w['w_out'], add=x)
    saved = dict(x=x, h=h, z=z, gp=gp, xs=xs, y0=y0, t_glu=t_glu, y_s5=y_s5, la=(la_f, la_b), o=(o_f, o_b), s=(s_f, s_b),
                 y_gla=y_gla, q16=q16, kd=kd, vd=vd, lse=lse, y_at=y_at, p=(p0, p1, p2), merged=merged)
    return xo, saved


def _mixer_bwd(dy, dy16, sv, w, s5_raw):
    z, h = sv['z'], sv['h']
    n_tok = z.shape[0]
    grads = {}
    dmerged = _mm("mix_dmerged", dy16, w['w_out'], tb=True, out_dtype=BF16)
    grads['w_out'] = _mm("mix_dwout", sv['merged'], dy16, ta=True)

    def dmerge(dm, gpv, a0, a1, a2, bias):
        dm = dm.astype(F32)
        g = _sigmoid(gpv.astype(F32) + bias)
        ps = (a0.astype(F32), a1.astype(F32), a2.astype(F32))
        dps, dgs = [], []
        for k in range(3):
            gk = g[:, k * D_MODEL:(k + 1) * D_MODEL]
            dps.append(dm * gk)
            dgs.append(dm * ps[k] * gk * (1.0 - gk))
        dgp = jnp.concatenate(dgs, axis=1)
        return dps[0], dps[1], dps[2], dgp, jnp.sum(dgp, axis=0, keepdims=True)

    dp0, dp1, dp2, dgp, dbm = _rowwise("mix_dmerge", dmerge, [dmerged, sv['gp'], *sv['p']], [w['b_merge_gate']],
                                       [(D_MODEL, BF16)] * 3 + [(3 * D_MODEL, BF16)], accs=[(1, 3 * D_MODEL)])
    grads['b_merge_gate'] = dbm[0]
    grads['w_branch_s5'] = _mm("mix_dbranch", sv['y_s5'], dp0, ta=True)
    grads['w_branch_gla'] = _mm("mix_dbranch", sv['y_gla'], dp1, ta=True)
    grads['w_branch_attn'] = _mm("mix_dbranch_f32", sv['y_at'], dp2, ta=True)
    dy_s5 = _mm("mix_dy", dp0, w['w_branch_s5'], tb=True)
    dy_gla = _mm("mix_dy", dp1, w['w_branch_gla'], tb=True)
    dy_at = _mm("mix_dy", dp2, w['w_branch_attn'], tb=True)
    grads['w_merge_gate'] = _mm("mix_dwgate", h, dgp, ta=True)
    dh = _mm("mix_dh_gate", dgp, w['w_merge_gate'], tb=True)
    u = _cw(z, 0, S5_W)

    def s5_post_bwd(dyv, y0, t, uv, dv, wglu):
        sg = _sigmoid(t)
        y1 = _gelu(y0)
        dt = dyv * y1 * sg * (1.0 - sg)
        dy1 = dyv * sg + _dot(dt.astype(BF16), wglu, _NT)
        dy0 = dy1 * _gelu_grad(y0)
        return dy0, dt, y1, dy0 * dv, jnp.sum(dy0 * uv, axis=0, keepdims=True)

    dy0, dt_glu, y1, du, dd = _rowwise("s5_post_bwd", s5_post_bwd, [dy_s5, sv['y0'], sv['t_glu'], u],
                                       [w['s5_d'], w['s5_w_glu']],
                                       [(S5_W, BF16), (S5_W, BF16), (S5_W, BF16), (S5_W, F32)], accs=[(1, S5_W)])
    grads['s5_d'] = dd[0]
    grads['s5_w_glu'] = _mm("s5_dwglu", y1, dt_glu, ta=True)
    s5_cot = []
    for d, rev in ((0, False), (1, True)):
        a_re, a_im, wb, wc = w['s5'][d]
        du, dwb, dwc, da = _s5_bwd("s5_adj_fwd" if rev else "s5_adj_rev", dy0, z, sv['xs'][d], a_re, -a_im, wb, wc, du,
                                   rev=not rev)
        da = jnp.sum(da, axis=0, keepdims=True)
        s5_cot.append((da[:, :S5_N], da[:, S5_N:], dwb, dwc))
    for d in range(2):
        _, pull = jax.vjp(_s5_prep, *s5_raw[d])
        s5_cot[d] = pull(s5_cot[d])
    for k, nm in enumerate(('s5_lambda_re', 's5_lambda_im', 's5_log_dt', 's5_b_re', 's5_b_im', 's5_c_re', 's5_c_im')):
        grads[nm] = jnp.stack([s5_cot[0][k], s5_cot[1][k]])

    gate = _cw(z, 4, GLA_W)

    def gla_post_bwd(dyv, of, ob, gv, gn):
        o = of + ob
        dos, dgates = [], []
        dgn = jnp.zeros((1, GLA_D), F32)
        for hh in range(GLA_H):
            sl = slice(hh * GLA_D, (hh + 1) * GLA_D)
            gh, dyh = gv[:, sl], dyv[:, sl]
            sg = _sigmoid(gh)
            dgates.append(dyh * _rms(o[:, sl], gn) * sg * (1.0 + gh * (1.0 - sg)))
            dx, dg = _rms_bwd(o[:, sl], gn, dyh * gh * sg)
            dos.append(dx)
            dgn = dgn + dg
        return jnp.concatenate(dos, axis=1), jnp.concatenate(dgates, axis=1), dgn

    do, dgate, dgn = _rowwise("gla_post_bwd", gla_post_bwd, [dy_gla, *sv['o'], gate], [w['gla_norm']],
                              [(GLA_W, F32), (GLA_W, F32)], accs=[(1, GLA_D)])
    grads['gla_norm'] = dgn[0]
    dq_f, dk_f, dv_f, dla_f = _gla_bwd("gla_bwd_fwd", z, sv['la'][0], sv['s'][0], do, rev=False)
    dq_b, dk_b, dv_b, dla_b = _gla_bwd("gla_bwd_rev", z, sv['la'][1], sv['s'][1], do, rev=True)
    wa_f, wa_b = _alpha_pads(w['gla_w_alpha'])
    ba_f, ba_b = w['gla_b_alpha'][0:1], w['gla_b_alpha'][1:2]
    zz = _cw(z, ZC_Z, LANE)

    def gla_prep_bwd(dlf, dlb, zv, waf, wab, bf, bb):
        z16 = zv.astype(BF16)
        waf16, wab16 = waf.astype(BF16), wab.astype(BF16)
        xf = _dot(z16, waf16) + bf
        xb = _dot(z16, wab16) + bb
        df = dlf * (_sigmoid(-xf) / GLA_TAU)
        db = dlb * (_sigmoid(-xb) / GLA_TAU)
        dz = _dot(df.astype(BF16), waf16, _NT) + _dot(db.astype(BF16), wab16, _NT)
        return df, db, dz, jnp.sum(df, axis=0, keepdims=True), jnp.sum(db, axis=0, keepdims=True)

    dlog_f, dlog_b, dzz, dba_f, dba_b = _rowwise(
        "gla_prep_bwd", gla_prep_bwd, [dla_f, dla_b, zz], [wa_f, wa_b, ba_f, ba_b],
        [(GLA_W, BF16), (GLA_W, BF16), (LANE, F32)], accs=[(1, GLA_W), (1, GLA_W)])
    dwa_f = _mm("gla_dwa", zz, dlog_f, ta=True)
    dwa_b = _mm("gla_dwa", zz, dlog_b, ta=True)
    grads['gla_w_alpha'] = jnp.stack([dwa_f[:GLA_R], dwa_b[GLA_R:2 * GLA_R]])
    grads['gla_b_alpha'] = jnp.concatenate([dba_f, dba_b], axis=0)

    dq_r, dkd, dvd = _attn_bwd("attn_bwd", sv['q16'], sv['kd'], sv['vd'], sv['y_at'], sv['lse'], dy_at)
    cq, sq, mq, rq = _head_consts(n_tok, AT_QH)
    ck, sk, mk, rk = _head_consts(n_tok, AT_KH)
    gq = jnp.tile(w['attn_q_norm'], (1, AT_QH))
    gk = jnp.tile(w['attn_k_norm'], (1, AT_KH))

    def rope_bwd(dr, v, gain, cos, sin, mean, rot, scale):
        dr = dr * scale
        dvn = dr * cos + _dot(dr * sin, rot, _NT, precision=HI)
        r = lax.rsqrt(_dot(v * v, mean, precision=HI) + EPS)
        vh = v * r
        dg = jnp.sum(dvn * vh, axis=0, keepdims=True)
        dvh = dvn * gain
        return r * (dvh - vh * _dot(dvh * vh, mean, precision=HI)), dg

    def attn_prep_bwd(dq, dk2, dv2, qv, kv, cqv, sqv, ckv, skv, gqv, gkv, mqv, rqv, mkv, rkv):
        dqo, dgq = rope_bwd(dq, qv, gqv, cqv, sqv, mqv, rqv, AT_D ** -0.5)
        dko, dgk = rope_bwd(_fold_heads(dk2), kv, gkv, ckv, skv, mkv, rkv, 1.0)
        return dqo, dko, _fold_heads(dv2), dgq, dgk

    q_raw, k_raw = _cw(z, 5, AT_W), _cw(z, ZC_AK, AT_KW)
    daq, dak, dav, dgq, dgk = _rowwise("attn_prep_bwd", attn_prep_bwd, [dq_r, dkd, dvd, q_raw, k_raw, cq, sq, ck, sk],
                                       [gq, gk, mq, rq, mk, rk], [(AT_W, F32), (AT_KW, F32), (AT_KW, F32)],
                                       accs=[(1, AT_W), (1, AT_KW)])
    grads['attn_q_norm'] = dgq.reshape(AT_QH, AT_D).sum(axis=0)
    grads['attn_k_norm'] = dgk.reshape(AT_KH, AT_D).sum(axis=0)

    def assemble(duv, qf, qb, kf, kb, vf, vb, dg, aq, ak, av, zzv):
        return jnp.concatenate([duv, qf + qb, kf + kb, vf + vb, dg, aq, ak, av, zzv], axis=1)

    dz = _rowwise("mix_dz", assemble, [du, dq_f, dq_b, dk_f, dk_b, dv_f, dv_b, dgate, daq, dak, dav, dzz], [],
                  [(ZP_W, BF16)])[0]
    grads['w_in'] = _unpad_w_in(_mm("mix_dwin", h, dz, ta=True))
    dh = _mm("mix_dh_in", dz, w['w_in_p'], tb=True, add=dh)

    def dnorm(dyv, xv, dhv, gv):
        dx, dg = _rms_bwd(xv, gv, dhv)
        dx = dyv + dx
        return dx, dx, dg

    dx, dx16, dg = _rowwise("mix_dnorm", dnorm, [dy, sv['x'], dh], [w['mix_norm']], [(D_MODEL, F32), (D_MODEL, BF16)],
                            accs=[(1, D_MODEL)])
    grads['mix_norm'] = dg[0]
    return dx, dx16, grads


FFN_NAMES = [p + k for p in ('ffn1_', 'ffn2_') for k in ('w_gate', 'w_up', 'w_down')]
MISC_SHARDED = [n for n in SHARDED if n not in FFN_NAMES]
LARGE_SHARDED = FFN_NAMES + ['w_in', 'w_merge_gate']
KEPT_TRANSPOSED = [p + k for p in ('ffn1_', 'ffn2_') for k in ('w_gate', 'w_up')]


def _stored(n, t):
    return jnp.swapaxes(t, 1, 2) if n in KEPT_TRANSPOSED else t


def _layer_weights(ws, rep, i):
    w = {k: _from_shards(ws[k][:, i], SHARD_AXIS[k] - 1) for k in MISC_SHARDED}
    for k in ('ffn1_norm', 'mix_norm', 'ffn2_norm', 's5_d', 'gla_norm', 'attn_q_norm', 'attn_k_norm', 'b_merge_gate'):
        w[k] = rep[k][i].reshape(1, -1)
    w['w_in_p'] = _pad_w_in(w['w_in'])
    s5_raw = [tuple(rep[k][i, d] for k in ('s5_lambda_re', 's5_lambda_im', 's5_log_dt', 's5_b_re', 's5_b_im',
                                            's5_c_re', 's5_c_im')) for d in range(2)]
    w['s5'] = [_s5_prep(*s5_raw[d]) for d in range(2)]
    return w, s5_raw


def _device_step(x, target, ws, rep):
    lw = [_layer_weights(ws, rep, i) for i in range(DEPTH)]
    ffn = {p: (ws[p + 'w_gate'], ws[p + 'w_up'], ws[p + 'w_down']) for p in ('ffn1_', 'ffn2_')}
    saved = []
    for i in range(DEPTH):
        w, _ = lw[i]
        x, s1 = _ffn_fwd(x, w['ffn1_norm'], *ffn['ffn1_'], i)
        x, s2 = _mixer_fwd(x, w)
        x, s3 = _ffn_fwd(x, w['ffn2_norm'], *ffn['ffn2_'], i)
        saved.append((s1, s2, s3))
    gfin = rep['final_norm'].reshape(1, -1)

    def head(xv, tv, gv):
        e = _rms(xv, gv) - tv
        dx, dg = _rms_bwd(xv, gv, e * (1.0 / D_MODEL))
        part = jnp.sum(e * e, axis=0, keepdims=True)
        return dx, dx, dg, part

    dx, dx16, dgfin, part = _rowwise("loss_head", head, [x, target], [gfin], [(D_MODEL, F32), (D_MODEL, BF16)],
                                     accs=[(1, D_MODEL), (1, D_MODEL)])
    loss = (0.5 / D_MODEL) * jnp.sum(part)
    per_layer = []
    dffn = {'ffn1_': None, 'ffn2_': None}
    for i in reversed(range(DEPTH)):
        w, s5_raw = lw[i]
        s1, s2, s3 = saved[i]
        dx, dx16, n3, dffn['ffn2_'] = _ffn_bwd(dx, dx16, s3, w['ffn2_norm'], *ffn['ffn2_'], i, dffn['ffn2_'])
        dx, dx16, g = _mixer_bwd(dx, dx16, s2, w, s5_raw)
        dx, dx16, n1, dffn['ffn1_'] = _ffn_bwd(dx, dx16, s1, w['ffn1_norm'], *ffn['ffn1_'], i, dffn['ffn1_'])
        g['ffn2_norm'], g['ffn1_norm'] = n3, n1
        per_layer.append(g)
    per_layer.reverse()
    stacked = {k: jnp.stack([per_layer[i][k] for i in range(DEPTH)]) for k in per_layer[0]}
    gs = {k: _to_shards(stacked[k], SHARD_AXIS[k]) for k in MISC_SHARDED}
    for p in ('ffn1_', 'ffn2_'):
        gs[p + 'w_gate'], gs[p + 'w_up'], gs[p + 'w_down'] = dffn[p]
    grep = {k: stacked[k] for k in REPLICATED if k != 'final_norm'}
    grep['final_norm'] = dgfin[0]
    return loss, dx, gs, grep


PACK_C = 1024
ROW_TILE = 512


def _rows_of(shape):
    return -(-int(np.prod(shape)) // PACK_C)


def _as_rows(t, lead=0):
    head = t.shape[:lead]
    flat = t.reshape(head + (-1,))
    rows = -(-flat.shape[-1] // PACK_C)
    pad = rows * PACK_C - flat.shape[-1]
    if pad:
        flat = jnp.pad(flat, [(0, 0)] * lead + [(0, pad)])
    return flat.reshape(head + (rows, PACK_C))


def _pack(items, total_rows, lead=0):
    parts = [_as_rows(t, lead) for t in items]
    used = sum(p.shape[lead] for p in parts)
    if total_rows > used:
        parts.append(jnp.zeros(parts[0].shape[:lead] + (total_rows - used, PACK_C), parts[0].dtype))
    return jnp.concatenate(parts, axis=lead)


def _unpack(rows, shapes, lead=0):
    out, r0 = [], 0
    head = rows.shape[:lead]
    for shp in shapes:
        n, size = _rows_of(shp), int(np.prod(shp))
        piece = lax.slice_in_dim(rows, r0, r0 + n, axis=lead).reshape(head + (n * PACK_C,))
        out.append(lax.slice_in_dim(piece, 0, size, axis=lead).reshape(head + tuple(shp)))
        r0 += n
    return out


def _round_up(n, m):
    return -(-n // m) * m


def _to_shards(g, axis):
    shp = g.shape
    g = g.reshape(shp[:axis] + (N_CHIPS, shp[axis] // N_CHIPS) + shp[axis + 1:])
    return jnp.moveaxis(g, axis, 0)


def _from_shards(s, axis):
    s = jnp.moveaxis(s, 0, axis)
    shp = s.shape
    return s.reshape(shp[:axis] + (shp[axis] * shp[axis + 1],) + shp[axis + 2:])


_HBM = pl.BlockSpec(memory_space=pltpu.HBM)


def _place():
    x, y, c = lax.axis_index("x"), lax.axis_index("y"), lax.axis_index("c")
    chips = [(1 - x, y), (x, 1 - y), (1 - x, 1 - y)]
    return x, y, c, chips


def _rcopy(src, dst, send, recv, k, dev):
    return pltpu.make_async_remote_copy(src_ref=src, dst_ref=dst, send_sem=send.at[k], recv_sem=recv.at[k],
                                        device_id=dev, device_id_type=MESH)


def _comm_call(name, body, arrays, out_shapes, n_sems, n_local):
    return pl.pallas_call(
        body, name=name, in_specs=[_HBM] * len(arrays), out_specs=[_HBM] * len(out_shapes), out_shape=out_shapes,
        scratch_shapes=[pltpu.SemaphoreType.DMA((n_sems,)), pltpu.SemaphoreType.DMA((n_sems,)),
                        pltpu.SemaphoreType.DMA((max(n_local, 1),))],
    )(*arrays)


def _ici_peers(x, y, c):
    x_first = c == 0
    first = (jnp.where(x_first, 1 - x, x), jnp.where(x_first, y, 1 - y))
    second = (jnp.where(x_first, x, 1 - x), jnp.where(x_first, 1 - y, y))
    return first, second


def _ag_chips(name, packs):
    n = len(packs)

    def body(*refs):
        p, o, (send, recv, _) = refs[:n], refs[n:2 * n], refs[2 * n:]
        x, y, c, _ = _place()
        (ax, ay), (bx, by) = _ici_peers(x, y, c)
        me, na, nb, nd = 2 * x + y, 2 * ax + ay, 2 * bx + by, 2 * (1 - x) + (1 - y)
        sib = (x, y, 1 - c)
        own = [_rcopy(p[a].at[c], o[a].at[me, c], send, recv, 6 * a, (ax, ay, c)) for a in range(n)]
        own += [_rcopy(p[a].at[c], o[a].at[me, c], send, recv, 6 * a + 1, (bx, by, c)) for a in range(n)]
        for cp in own:
            cp.start()
        passed = []

        def landed(a, chip, k, fwd):
            blk = o[a].at[chip, c]
            _rcopy(blk, blk, send, recv, 6 * a + k, sib).wait_recv()
            for sem, dev in fwd:
                cp = _rcopy(blk, blk, send, recv, 6 * a + sem, dev)
                cp.start()
                passed.append(cp)

        for a in range(n):
            landed(a, na, 0, [(2, (bx, by, c)), (3, sib)])
        for a in range(n):
            landed(a, nb, 1, [(4, sib)])
        for a in range(n):
            landed(a, nd, 2, [(5, sib)])
        for a in range(n):
            for k, chip in ((3, nb), (4, na), (5, nd)):
                blk = o[a].at[chip, 1 - c]
                _rcopy(blk, blk, send, recv, 6 * a + k, sib).wait_recv()
        for cp in own + passed:
            cp.wait_send()

    outs = _comm_call(name, body, packs, [jax.ShapeDtypeStruct((N_CHIPS,) + t.shape, t.dtype) for t in packs], 6 * n, 0)
    chip = 2 * lax.axis_index("x") + lax.axis_index("y")
    return [lax.dynamic_update_slice_in_dim(o, t[None], chip, axis=0) for o, t in zip(outs, packs)]


def _pair_swap(name, gs):
    n = len(gs)
    ns = gs[0].shape[0]

    def body(*refs):
        g, r, (send, recv, _) = refs[:n], refs[n:2 * n], refs[2 * n:]
        x, y, c, _ = _place()
        cps = [_rcopy(g[a].at[j, 1 - c], r[a].at[j], send, recv, ns * a + j, (x, y, 1 - c))
               for a in range(n) for j in range(ns)]
        for cp in cps:
            cp.start()
        for cp in cps:
            cp.wait_recv()
        for cp in cps:
            cp.wait_send()

    return _comm_call(name, body, gs, [jax.ShapeDtypeStruct((ns,) + t.shape[2:], t.dtype) for t in gs], ns * n, 0)


def _halve_first(name, parts):
    n = len(parts)

    def body(*refs):
        p, r, (send, recv, _) = refs[:n], refs[n:2 * n], refs[2 * n:]
        x, y, c, _ = _place()
        (ax, ay), _ = _ici_peers(x, y, c)
        cps = [_rcopy(p[a].at[jnp.where(c == 0, 2 * ax + k, 2 * k + ay)], r[a].at[k], send, recv, 2 * a + k, (ax, ay, c))
               for a in range(n) for k in range(2)]
        for cp in cps:
            cp.start()
        for cp in cps:
            cp.wait_recv()
        for cp in cps:
            cp.wait_send()

    return _comm_call(name, body, parts, [jax.ShapeDtypeStruct((2,) + t.shape[1:], t.dtype) for t in parts], 2 * n, 0)


def _halve_second(name, kept):
    n = len(kept)

    def body(*refs):
        p, r, (send, recv, _) = refs[:n], refs[n:2 * n], refs[2 * n:]
        x, y, c, _ = _place()
        _, (bx, by) = _ici_peers(x, y, c)
        cps = [_rcopy(p[a].at[jnp.where(c == 0, by, bx)], r[a], send, recv, a, (bx, by, c)) for a in range(n)]
        for cp in cps:
            cp.start()
        for cp in cps:
            cp.wait_recv()
        for cp in cps:
            cp.wait_send()

    return _comm_call(name, body, kept, [jax.ShapeDtypeStruct(t.shape[1:], t.dtype) for t in kept], n, 0)


def _add_selected(name, t, r, index, out_dtype):
    single = r.ndim == 2
    r3 = r[None] if single else r
    nk, rh, cc = r3.shape
    tl = _tile(rh, ROW_TILE, 16)

    def body(i_ref, t_ref, r_ref, o_ref):
        o_ref[...] = (t_ref[...].astype(F32) + r_ref[...].astype(F32)).astype(o_ref.dtype)

    out = _pcall(
        body, name=name,
        grid_spec=pltpu.PrefetchScalarGridSpec(
            num_scalar_prefetch=1, grid=(nk, rh // tl),
            in_specs=[pl.BlockSpec((1, tl, cc), lambda k, i, i_ref: (i_ref[k], i, 0)),
                      pl.BlockSpec((1, tl, cc), lambda k, i, i_ref: (k, i, 0))],
            out_specs=pl.BlockSpec((1, tl, cc), lambda k, i, i_ref: (k, i, 0))),
        out_shape=jax.ShapeDtypeStruct(r3.shape, out_dtype),
        compiler_params=_cparams(("parallel", "parallel"), 8 * tl * cc * 4),
    )(index, t, r3)
    return out[0] if single else out


def _pair_share(name, reds):
    n = len(reds)

    def body(*refs):
        r, o, (send, recv, _) = refs[:n], refs[n:2 * n], refs[2 * n:]
        x, y, c, _ = _place()
        cps = [_rcopy(r[a], o[a].at[c], send, recv, a, (x, y, 1 - c)) for a in range(n)]
        for cp in cps:
            cp.start()
        for a in range(n):
            blk = o[a].at[1 - c]
            _rcopy(blk, blk, send, recv, a, (x, y, 1 - c)).wait_recv()
        for cp in cps:
            cp.wait_send()

    outs = _comm_call(name, body, reds, [jax.ShapeDtypeStruct((2,) + t.shape, t.dtype) for t in reds], n, 0)
    core = lax.axis_index("c")
    return [lax.dynamic_update_slice_in_dim(o, t[None], core, axis=0) for o, t in zip(outs, reds)]


def _add_own_half(name, g, r):
    ns, _, rh, cc = g.shape
    tl = _tile(rh, ROW_TILE, 16)
    core = lax.axis_index("c").astype(jnp.int32).reshape(1)

    def body(c_ref, g_ref, r_ref, o_ref):
        o_ref[...] = (g_ref[0] + r_ref[...]).astype(o_ref.dtype)

    return _pcall(
        body, name=name,
        grid_spec=pltpu.PrefetchScalarGridSpec(
            num_scalar_prefetch=1, grid=(ns, rh // tl),
            in_specs=[pl.BlockSpec((1, 1, tl, cc), lambda j, i, c_ref: (j, c_ref[0], i, 0)),
                      pl.BlockSpec((1, tl, cc), lambda j, i, c_ref: (j, i, 0))],
            out_specs=pl.BlockSpec((1, tl, cc), lambda j, i, c_ref: (j, i, 0))),
        out_shape=jax.ShapeDtypeStruct(r.shape, BF16),
        compiler_params=_cparams(("parallel", "parallel"), 8 * tl * cc * 4),
    )(core, g, r)


def _reduce_scatter(gs):
    x, y, c = lax.axis_index("x"), lax.axis_index("y"), lax.axis_index("c")
    mine_first = jnp.stack([jnp.where(c == 0, 2 * x + k, 2 * k + y) for k in range(2)]).astype(jnp.int32)
    mine_second = jnp.where(c == 0, y, x).astype(jnp.int32).reshape(1)
    swapped = _pair_swap("rs_pair_swap", gs)
    parts = [_add_own_half("rs_add_pair", g, r) for g, r in zip(gs, swapped)]
    got = _halve_first("rs_halve_first", parts)
    kept = [_add_selected("rs_add_first", t, r, mine_first, BF16) for t, r in zip(parts, got)]
    got = _halve_second("rs_halve_second", kept)
    reds = [_add_selected("rs_add_second", t, r, mine_second, F32) for t, r in zip(kept, got)]
    return _pair_share("rs_pair_share", reds)


def _adamw(name, g, w, m, v):
    c1 = 1.0 - ADAM_B1 ** ADAM_STEP
    c2 = 1.0 - ADAM_B2 ** ADAM_STEP
    nb, rows, cc = g.shape
    tl = _tile(rows, ROW_TILE, SUBLANE)

    def body(g_ref, w_ref, m_ref, v_ref, d_ref, mo_ref, vo_ref):
        gv = g_ref[...]
        mn = ADAM_B1 * m_ref[...] + (1.0 - ADAM_B1) * gv
        vn = ADAM_B2 * v_ref[...] + (1.0 - ADAM_B2) * (gv * gv)
        d_ref[...] = -ADAM_LR * ((mn / c1) / (jnp.sqrt(vn / c2) + ADAM_EPS) + ADAM_WD * w_ref[...])
        mo_ref[...] = mn
        vo_ref[...] = vn

    spec = pl.BlockSpec((1, tl, cc), lambda b, i: (b, i, 0))
    return _pcall(
        body, name=name, grid=(nb, rows // tl), in_specs=[spec] * 4, out_specs=[spec] * 3,
        out_shape=[jax.ShapeDtypeStruct(g.shape, F32)] * 3,
        compiler_params=_cparams(("parallel", "parallel"), 16 * tl * cc * 4),
    )(g, w, m, v)


def _train_step(a):
    x, target = a['x'][0], a['loss_target'][0]

    names = BF16_GATHER + F32_GATHER
    got = _ag_chips("ag_weights", [_stored(n, a[n]).astype(BF16) for n in BF16_GATHER] + [a[n] for n in F32_GATHER])
    ws = dict(zip(names, got))

    loss, gx, gs, grep = _device_step(x, target, ws, {n: a[n] for n in REPLICATED})
    loss = lax.psum(loss, ("x", "y", "c"))

    small = [n for n in MISC_SHARDED if n not in LARGE_SHARDED]
    small_shapes = [a[n].shape[1:] for n in small]
    rep_shapes = [a[n].shape for n in REPLICATED]
    small_rows = sum(_rows_of(s) for s in small_shapes)
    piece = _round_up(-(-sum(_rows_of(s) for s in rep_shapes) // (2 * N_CHIPS)), SUBLANE)
    rep_rows = 2 * N_CHIPS * piece
    rep_pack = _pack([grep[n] for n in REPLICATED], rep_rows).reshape(N_CHIPS, 2, piece, PACK_C)
    total = _round_up(small_rows + piece, ROW_TILE // 2)
    misc = _pack([gs[n] for n in small] + [rep_pack], total, lead=2)
    reds = _reduce_scatter([gs[n] for n in LARGE_SHARDED] + [misc])
    grad = dict(zip(LARGE_SHARDED, reds[:-1]))
    grad.update(zip(small, _unpack(reds[-1], small_shapes, lead=1)))
    rep_mine = reds[-1][:, small_rows:small_rows + piece]
    rep_all = _ag_chips("ag_replicated", [rep_mine])[0].reshape(1, rep_rows, PACK_C)

    outs = {}
    for n in SHARDED:
        shp = grad[n].shape
        three_d = (shp[0], -1, shp[-1])
        params = (grad[n], *(_stored(n, a[p + n]) for p in ('', 'm_', 'v_')))
        res3 = _adamw("adamw", *(t.reshape(three_d) for t in params))
        for kind, arr in zip(('grad', 'delta', 'new_m', 'new_v'), (grad[n], *res3)):
            outs[kind + '_' + n] = _stored(n, arr.reshape(shp))
    w, m, v = (_pack([a[p + n] for n in REPLICATED], rep_rows)[None] for p in ('', 'm_', 'v_'))
    res3 = _adamw("adamw_replicated", rep_all, w, m, v)
    for kind, arr in zip(('grad', 'delta', 'new_m', 'new_v'), (rep_all, *res3)):
        for n, t in zip(REPLICATED, _unpack(arr[0], rep_shapes)):
            outs[kind + '_' + n] = t
    res = [loss, gx[None]]
    for kind in ('grad', 'delta', 'new_m', 'new_v'):
        res += [outs[kind + '_' + n] for n in W_NAMES]
    return tuple(res)


def kernel(x, ffn1_norm, ffn1_w_gate, ffn1_w_up, ffn1_w_down, mix_norm, w_in, s5_lambda_re, s5_lambda_im, s5_log_dt, s5_b_re, s5_b_im, s5_c_re, s5_c_im, s5_d, s5_w_glu, gla_w_alpha, gla_b_alpha, gla_norm, attn_q_norm, attn_k_norm, w_branch_s5, w_branch_gla, w_branch_attn, w_merge_gate, b_merge_gate, w_out, ffn2_norm, ffn2_w_gate, ffn2_w_up, ffn2_w_down, final_norm, loss_target, m_ffn1_norm, m_ffn1_w_gate, m_ffn1_w_up, m_ffn1_w_down, m_mix_norm, m_w_in, m_s5_lambda_re, m_s5_lambda_im, m_s5_log_dt, m_s5_b_re, m_s5_b_im, m_s5_c_re, m_s5_c_im, m_s5_d, m_s5_w_glu, m_gla_w_alpha, m_gla_b_alpha, m_gla_norm, m_attn_q_norm, m_attn_k_norm, m_w_branch_s5, m_w_branch_gla, m_w_branch_attn, m_w_merge_gate, m_b_merge_gate, m_w_out, m_ffn2_norm, m_ffn2_w_gate, m_ffn2_w_up, m_ffn2_w_down, m_final_norm, v_ffn1_norm, v_ffn1_w_gate, v_ffn1_w_up, v_ffn1_w_down, v_mix_norm, v_w_in, v_s5_lambda_re, v_s5_lambda_im, v_s5_log_dt, v_s5_b_re, v_s5_b_im, v_s5_c_re, v_s5_c_im, v_s5_d, v_s5_w_glu, v_gla_w_alpha, v_gla_b_alpha, v_gla_norm, v_attn_q_norm, v_attn_k_norm, v_w_branch_s5, v_w_branch_gla, v_w_branch_attn, v_w_merge_gate, v_b_merge_gate, v_w_out, v_ffn2_norm, v_ffn2_w_gate, v_ffn2_w_up, v_ffn2_w_down, v_final_norm):
    return _train_step(dict(locals()))
```

```python
import functools
import math

import jax
import jax.numpy as jnp
import numpy as np
from jax import lax
from jax.experimental import pallas as pl
from jax.experimental.pallas import tpu as pltpu

F32 = jnp.float32
BF16 = jnp.bfloat16
HI = lax.Precision.HIGHEST

D_MODEL = 1024
D_FF = 2816
DEPTH = 2
EPS = 1e-6
S5_G, S5_H, S5_P = 32, 16, 64
S5_W = S5_G * S5_H
S5_N = S5_G * S5_P
S5_TG = 8
S5_TILES = S5_G // S5_TG
S5_TC, S5_TS = S5_TG * S5_H, S5_TG * S5_P
GLA_H, GLA_D = 4, 128
GLA_W = GLA_H * GLA_D
GLA_R = 16
GLA_TAU = 16.0
GLA_C = 64
AT_QH, AT_KH, AT_D = 8, 2, 64
AT_W = AT_QH * AT_D
AT_KW = AT_KH * AT_D
GRID_W = 64
ROPE_BASE = 10000.0
ZP_W = 3456
ZC_AK, ZC_AV, ZC_Z = 24, 25, 26

ADAM_LR, ADAM_B1, ADAM_B2, ADAM_EPS, ADAM_WD, ADAM_STEP = 0.001, 0.9, 0.999, 1e-08, 0.01, 10

GRAD_DT = BF16
VMEM_LIMIT_V7X = 56 * 1024 * 1024
LANE = 128
SUBLANE = 8

W_NAMES = ['ffn1_norm', 'ffn1_w_gate', 'ffn1_w_up', 'ffn1_w_down', 'mix_norm', 'w_in', 's5_lambda_re', 's5_lambda_im',
           's5_log_dt', 's5_b_re', 's5_b_im', 's5_c_re', 's5_c_im', 's5_d', 's5_w_glu', 'gla_w_alpha', 'gla_b_alpha',
           'gla_norm', 'attn_q_norm', 'attn_k_norm', 'w_branch_s5', 'w_branch_gla', 'w_branch_attn', 'w_merge_gate',
           'b_merge_gate', 'w_out', 'ffn2_norm', 'ffn2_w_gate', 'ffn2_w_up', 'ffn2_w_down', 'final_norm']
SHARD_AXIS = {'ffn1_w_gate': 2, 'ffn1_w_up': 2, 'ffn1_w_down': 1, 'w_in': 2, 's5_w_glu': 1, 'gla_w_alpha': 3,
              'gla_b_alpha': 2, 'w_branch_s5': 2, 'w_branch_gla': 2, 'w_branch_attn': 2, 'w_merge_gate': 2,
              'w_out': 1, 'ffn2_w_gate': 2, 'ffn2_w_up': 2, 'ffn2_w_down': 1}
F32_GATHER = ['gla_w_alpha', 'gla_b_alpha']
BF16_GATHER = [n for n in W_NAMES if n in SHARD_AXIS and n not in F32_GATHER]
SHARDED = [n for n in W_NAMES if n in SHARD_AXIS]
REPLICATED = [n for n in W_NAMES if n not in SHARD_AXIS]
N_CHIPS = 4
MESH = pl.DeviceIdType.MESH


def _tile(n, target, mult):
    best = None
    for t in range(mult, min(n, target) + 1, mult):
        if n % t == 0:
            best = t
    return best if best is not None else n


def _cparams(sem, vmem_bytes):
    limit = int(min(VMEM_LIMIT_V7X, max(32 * 1024 * 1024, vmem_bytes * 5 // 4)))
    return pltpu.CompilerParams(dimension_semantics=sem, vmem_limit_bytes=limit)


def _pcall(body, **kw):
    shapes = kw.pop('out_shape')
    many = isinstance(shapes, (list, tuple))
    pinned = [pltpu.HBM(t.shape, t.dtype) for t in (shapes if many else [shapes])]
    call = pl.pallas_call(body, out_shape=pinned if many else pinned[0], **kw)

    def run(*ops):
        return call(*[o if jnp.issubdtype(o.dtype, jnp.integer) else pltpu.with_memory_space_constraint(o, pltpu.HBM)
                      for o in ops])

    return run


def _cw(arr, cb, width):
    return (arr, cb, width)


def _win(a):
    if isinstance(a, tuple):
        return a
    return (a, 0, a.shape[-1])


def _nbytes(shape, dtype):
    return int(np.prod(shape)) * jnp.dtype(dtype).itemsize


def _rowwise(name, fn, rows, fulls, outs, accs=(), tl=256):
    rows = [_win(r) for r in rows]
    n_tok = rows[0][0].shape[0]
    tl = _tile(n_tok, tl, 16)
    nr, nf, no = len(rows), len(fulls), len(outs)

    def body(*refs):
        vals = [r[...] for r in refs[:nr + nf]]
        res = fn(*vals)
        if not isinstance(res, (tuple, list)):
            res = (res,)
        for k in range(no):
            o = refs[nr + nf + k]
            o[...] = res[k].astype(o.dtype)
        if accs:
            @pl.when(pl.program_id(0) == 0)
            def _():
                for k in range(len(accs)):
                    a = refs[nr + nf + no + k]
                    a[...] = jnp.zeros(a.shape, a.dtype)
            for k in range(len(accs)):
                a = refs[nr + nf + no + k]
                a[...] += res[no + k]

    in_specs = [pl.BlockSpec((tl, w), functools.partial(lambda i, cb: (i, cb), cb=cb)) for (_, cb, w) in rows]
    in_specs += [pl.BlockSpec(f.shape, functools.partial(lambda i, nd: (0,) * nd, nd=f.ndim)) for f in fulls]
    out_specs = [pl.BlockSpec((tl, c), lambda i: (i, 0)) for (c, _) in outs]
    out_specs += [pl.BlockSpec((r, c), lambda i: (0, 0)) for (r, c) in accs]
    out_shape = [jax.ShapeDtypeStruct((n_tok, c), dt) for (c, dt) in outs]
    out_shape += [jax.ShapeDtypeStruct((r, c), F32) for (r, c) in accs]
    vm = 2 * sum(_nbytes((tl, w), a.dtype) for (a, _, w) in rows)
    vm += 2 * sum(_nbytes(f.shape, f.dtype) for f in fulls)
    vm += 2 * sum(_nbytes((tl, c), dt) for (c, dt) in outs)
    vm += 8 * max([_nbytes((tl, w), F32) for (_, _, w) in rows] + [_nbytes((tl, c), F32) for (c, _) in outs])
    return _pcall(
        body, name=name, grid=(n_tok // tl,), in_specs=in_specs, out_specs=out_specs, out_shape=out_shape,
        compiler_params=_cparams(("arbitrary",) if accs else ("parallel",), vm),
    )(*[r[0] for r in rows], *fulls)


def _mm(name, a, b, *, ta=False, tb=False, out_dtype=F32, alpha=1.0, add=None, tm=512, tn=512):
    a_arr, a_cb, a_w = _win(a)
    b_arr, b_cb, b_w = _win(b)
    if ta:
        kdim, m = a_arr.shape[0], a_w
    else:
        m, kdim = a_arr.shape[0], a_w
    if tb:
        n, kb = b_arr.shape[0], b_w
    else:
        kb, n = b_arr.shape[0], b_w
    assert kdim == kb, (name, kdim, kb)
    tm = _tile(m, tm, LANE if ta else 16)
    tn = _tile(n, tn, LANE)
    if ta:
        a_spec = pl.BlockSpec((kdim, tm), lambda i, j: (0, a_cb * (a_w // tm) + i))
        a_blk = (kdim, tm)
    else:
        a_spec = pl.BlockSpec((tm, kdim), lambda i, j: (i, a_cb))
        a_blk = (tm, kdim)
    if tb:
        b_spec = pl.BlockSpec((tn, kdim), lambda i, j: (j, b_cb))
        b_blk = (tn, kdim)
    else:
        b_spec = pl.BlockSpec((kdim, tn), lambda i, j: (0, b_cb * (b_w // tn) + j))
        b_blk = (kdim, tn)
    dims = (((0 if ta else 1,), (1 if tb else 0,)), ((), ()))
    has_add = add is not None

    def body(*refs):
        a_ref, b_ref = refs[0], refs[1]
        o_ref = refs[-1]
        acc = lax.dot_general(a_ref[...].astype(BF16), b_ref[...].astype(BF16), dims, preferred_element_type=F32)
        if alpha != 1.0:
            acc = acc * alpha
        if has_add:
            acc = acc + refs[2][...].astype(F32)
        o_ref[...] = acc.astype(o_ref.dtype)

    in_specs = [a_spec, b_spec]
    ops = [a_arr, b_arr]
    if has_add:
        in_specs.append(pl.BlockSpec((tm, tn), lambda i, j: (i, j)))
        ops.append(add)
    vm = 2 * (_nbytes(a_blk, a_arr.dtype) + _nbytes(b_blk, b_arr.dtype) + _nbytes((tm, tn), out_dtype))
    vm += _nbytes(a_blk, BF16) + _nbytes(b_blk, BF16) + 3 * _nbytes((tm, tn), F32)
    return _pcall(
        body, name=name, grid=(m // tm, n // tn), in_specs=in_specs,
        out_specs=pl.BlockSpec((tm, tn), lambda i, j: (i, j)),
        out_shape=jax.ShapeDtypeStruct((m, n), out_dtype),
        compiler_params=_cparams(("parallel", "parallel"), vm),
    )(*ops)


def _sigmoid(x):
    return 1.0 / (1.0 + jnp.exp(-x))


def _rms(x, gain):
    r = lax.rsqrt(jnp.mean(x * x, axis=-1, keepdims=True) + EPS)
    return x * r * gain


def _rms_bwd(x, gain, dh):
    r = lax.rsqrt(jnp.mean(x * x, axis=-1, keepdims=True) + EPS)
    xh = x * r
    dg = jnp.sum(dh * xh, axis=0, keepdims=True)
    dxh = dh * gain
    dx = r * (dxh - xh * jnp.mean(dxh * xh, axis=-1, keepdims=True))
    return dx, dg


_GELU_C = math.sqrt(2.0 / math.pi)


def _gelu(y):
    return 0.5 * y * (1.0 + jnp.tanh(_GELU_C * (y + 0.044715 * y * y * y)))


def _gelu_grad(y):
    th = jnp.tanh(_GELU_C * (y + 0.044715 * y * y * y))
    return 0.5 * (1.0 + th) + 0.5 * y * (1.0 - th * th) * _GELU_C * (1.0 + 3.0 * 0.044715 * y * y)


def _log_sigmoid(x):
    return jnp.minimum(x, 0.0) - jnp.log(1.0 + jnp.exp(-jnp.abs(x)))


def _dot(a, b, dims=(((1,), (0,)), ((), ())), precision=None):
    return lax.dot_general(a, b, dims, preferred_element_type=F32, precision=precision)


_NT = (((1,), (1,)), ((), ()))
_TN = (((0,), (0,)), ((), ()))


def _scan_core(b_ref, are_ref, aim_ref, pw, last, of32, *, rev, xf32=None, da_ref=None, lc=512):
    nb, nc = b_ref.shape[0] // SUBLANE, S5_N // lc
    row = lax.broadcasted_iota(jnp.int32, (SUBLANE, lc), 0)

    def cmul(ar, ai, xr, xi):
        return ar * xr - ai * xi, ar * xi + ai * xr

    def hs(c, xr, xi):
        cs = pl.ds(c * lc, lc)
        for k, slot in ((1, 0), (2, 2), (4, 4)):
            ar, ai = pw[slot, :, cs], pw[slot + 1, :, cs]
            if rev:
                sr, si = pltpu.roll(xr, SUBLANE - k, 0), pltpu.roll(xi, SUBLANE - k, 0)
                keep = row < SUBLANE - k
            else:
                sr, si = pltpu.roll(xr, k, 0), pltpu.roll(xi, k, 0)
                keep = row >= k
            sr, si = jnp.where(keep, sr, 0.0), jnp.where(keep, si, 0.0)
            pr, pi = cmul(ar, ai, sr, si)
            xr, xi = xr + pr, xi + pi
        return xr, xi

    @pl.when(pl.program_id(0) == 0)
    def _():
        last[...] = jnp.zeros(last.shape, F32)
        if da_ref is not None:
            da_ref[...] = jnp.zeros(da_ref.shape, F32)
        for c in range(nc):
            cs = pl.ds(c * lc, lc)
            a1r = jnp.broadcast_to(are_ref[:, cs], (SUBLANE, lc))
            a1i = jnp.broadcast_to(aim_ref[:, cs], (SUBLANE, lc))
            a2r, a2i = cmul(a1r, a1i, a1r, a1i)
            a4r, a4i = cmul(a2r, a2i, a2r, a2i)
            for slot, v in enumerate((a1r, a1i, a2r, a2i, a4r, a4i)):
                pw[slot, :, cs] = v
            first = SUBLANE - 1 if rev else 0
            pr, pi = hs(c, jnp.where(row == first, a1r, 0.0), jnp.where(row == first, a1i, 0.0))
            pw[6, :, cs] = pr
            pw[7, :, cs] = pi

    edge = 0 if rev else SUBLANE - 1

    def blk(j, carry):
        jj = (nb - 1 - j) if rev else j
        r0 = pl.multiple_of(jj * SUBLANE, SUBLANE)
        for c in range(nc):
            cre, cim = pl.ds(c * lc, lc), pl.ds(S5_N + c * lc, lc)
            xr, xi = hs(c, b_ref[pl.ds(r0, SUBLANE), cre], b_ref[pl.ds(r0, SUBLANE), cim])
            cr = jnp.broadcast_to(last[edge:edge + 1, cre], (SUBLANE, lc))
            ci = jnp.broadcast_to(last[edge:edge + 1, cim], (SUBLANE, lc))
            pr, pi = cmul(pw[6, :, cre], pw[7, :, cre], cr, ci)
            xr, xi = xr + pr, xi + pi
            if da_ref is not None:
                if rev:
                    qr, qi = pltpu.roll(xr, SUBLANE - 1, 0), pltpu.roll(xi, SUBLANE - 1, 0)
                    fill = row == SUBLANE - 1
                else:
                    qr, qi = pltpu.roll(xr, 1, 0), pltpu.roll(xi, 1, 0)
                    fill = row == 0
                qr, qi = jnp.where(fill, cr, qr), jnp.where(fill, ci, qi)
                zr, zi = xf32[pl.ds(r0, SUBLANE), cre], xf32[pl.ds(r0, SUBLANE), cim]
                da_ref[:, cre] += zr * qr + zi * qi
                da_ref[:, cim] += zr * qi - zi * qr
            last[:, cre] = xr
            last[:, cim] = xi
            of32[pl.ds(r0, SUBLANE), cre] = xr
            of32[pl.ds(r0, SUBLANE), cim] = xi
        return carry

    lax.fori_loop(0, nb, blk, 0)


def _s5_tiles():
    return [(slice((j % S5_TILES) * S5_TC, (j % S5_TILES + 1) * S5_TC), slice(j * S5_TS, (j + 1) * S5_TS))
            for j in range(2 * S5_TILES)]


def _s5_scratch(tb):
    return [pltpu.VMEM((8, SUBLANE, S5_N), F32), pltpu.VMEM((SUBLANE, 2 * S5_N), F32),
            pltpu.VMEM((tb, 2 * S5_N), F32), pltpu.VMEM((tb, 2 * S5_N), F32),
            pltpu.VMEM((S5_TC, 2 * S5_N), BF16), pltpu.VMEM((2 * S5_N, S5_TC), BF16)]


def _s5_fwd(name, z, a_re, a_im, wb, wc, add, *, rev, tb=256):
    n_tok = z.shape[0]
    tb = _tile(n_tok, tb, 16)
    nt = n_tok // tb
    has_add = add is not None

    def body(*refs):
        u_ref, are_ref, aim_ref, wb_ref, wc_ref = refs[:5]
        x_ref, y_ref, pw, last, of32, bu, wb16, wc16 = refs[5 + has_add:]

        @pl.when(pl.program_id(0) == 0)
        def _():
            wb16[...] = wb_ref[...].astype(BF16)
            wc16[...] = wc_ref[...].astype(BF16)

        u16 = u_ref[...].astype(BF16)
        for ch, st in _s5_tiles():
            bu[:, st] = _dot(u16[:, ch], wb16[:, st])
        _scan_core(bu, are_ref, aim_ref, pw, last, of32, rev=rev)
        x16 = of32[...].astype(BF16)
        x_ref[...] = x16
        ys = []
        for k in range(S5_TILES):
            re, im = slice(k * S5_TS, (k + 1) * S5_TS), slice(S5_N + k * S5_TS, S5_N + (k + 1) * S5_TS)
            ys.append(_dot(x16[:, re], wc16[re, :]) + _dot(x16[:, im], wc16[im, :]))
        y = jnp.concatenate(ys, axis=1)
        y_ref[...] = y + refs[5][...] if has_add else y

    tmap = (lambda t: (nt - 1 - t, 0)) if rev else (lambda t: (t, 0))
    const = lambda t: (0, 0)
    in_specs = [pl.BlockSpec((tb, S5_W), tmap), pl.BlockSpec((1, S5_N), const), pl.BlockSpec((1, S5_N), const),
                pl.BlockSpec((S5_TC, 2 * S5_N), const), pl.BlockSpec((2 * S5_N, S5_TC), const)]
    ops = [z, a_re, a_im, wb, wc]
    if has_add:
        in_specs.append(pl.BlockSpec((tb, S5_W), tmap))
        ops.append(add)
    return _pcall(
        body, name=name, grid=(nt,), in_specs=in_specs,
        out_specs=[pl.BlockSpec((tb, 2 * S5_N), tmap), pl.BlockSpec((tb, S5_W), tmap)],
        out_shape=[jax.ShapeDtypeStruct((n_tok, 2 * S5_N), BF16), jax.ShapeDtypeStruct((n_tok, S5_W), F32)],
        scratch_shapes=_s5_scratch(tb), compiler_params=_cparams(("arbitrary",), 10 * tb * 2 * S5_N * 4),
    )(*ops)


def _s5_bwd(name, dy, z, xs, a_re, a_im, wb, wc, add, *, rev, tb=256):
    n_tok = z.shape[0]
    tb = _tile(n_tok, tb, 16)
    nt = n_tok // tb

    def body(dy_ref, u_ref, xs_ref, are_ref, aim_ref, wb_ref, wc_ref, add_ref, du_ref, dwb_ref, dwc_ref, da_ref,
             pw, last, of32, gx, wb16, wc16, xf32):
        @pl.when(pl.program_id(0) == 0)
        def _():
            wb16[...] = wb_ref[...].astype(BF16)
            wc16[...] = wc_ref[...].astype(BF16)
            dwb_ref[...] = jnp.zeros(dwb_ref.shape, F32)
            dwc_ref[...] = jnp.zeros(dwc_ref.shape, F32)

        dy16, u16, x16 = dy_ref[...].astype(BF16), u_ref[...].astype(BF16), xs_ref[...]
        for ch, st in _s5_tiles():
            gx[:, st] = _dot(dy16[:, ch], wc16[st, :], _NT)
        xf32[...] = x16.astype(F32)
        _scan_core(gx, are_ref, aim_ref, pw, last, of32, rev=rev, xf32=xf32, da_ref=da_ref)
        g16 = of32[...].astype(BF16)
        dus = []
        for k in range(S5_TILES):
            re, im = slice(k * S5_TS, (k + 1) * S5_TS), slice(S5_N + k * S5_TS, S5_N + (k + 1) * S5_TS)
            dus.append(_dot(g16[:, re], wb16[:, re], _NT) + _dot(g16[:, im], wb16[:, im], _NT))
        du_ref[...] = jnp.concatenate(dus, axis=1) + add_ref[...]
        for ch, st in _s5_tiles():
            dwb_ref[:, st] += _dot(u16[:, ch], g16[:, st], _TN)
            dwc_ref[st, :] += _dot(x16[:, st], dy16[:, ch], _TN)

    tmap = (lambda t: (nt - 1 - t, 0)) if rev else (lambda t: (t, 0))
    const = lambda t: (0, 0)
    row_w, row_s = pl.BlockSpec((tb, S5_W), tmap), pl.BlockSpec((tb, 2 * S5_N), tmap)
    return _pcall(
        body, name=name, grid=(nt,),
        in_specs=[row_w, row_w, row_s, pl.BlockSpec((1, S5_N), const), pl.BlockSpec((1, S5_N), const),
                  pl.BlockSpec((S5_TC, 2 * S5_N), const), pl.BlockSpec((2 * S5_N, S5_TC), const), row_w],
        out_specs=[row_w, pl.BlockSpec((S5_TC, 2 * S5_N), const), pl.BlockSpec((2 * S5_N, S5_TC), const),
                   pl.BlockSpec((SUBLANE, 2 * S5_N), const)],
        out_shape=[jax.ShapeDtypeStruct((n_tok, S5_W), F32), jax.ShapeDtypeStruct((S5_TC, 2 * S5_N), F32),
                   jax.ShapeDtypeStruct((2 * S5_N, S5_TC), F32), jax.ShapeDtypeStruct((SUBLANE, 2 * S5_N), F32)],
        scratch_shapes=_s5_scratch(tb) + [pltpu.VMEM((tb, 2 * S5_N), F32)],
        compiler_params=_cparams(("arbitrary",), 12 * tb * 2 * S5_N * 4),
    )(dy, z, xs, a_re, a_im, wb, wc, add)


GLA_TB = 512


def _gla_block_terms(la, q, k, rev):
    tb = la.shape[0]
    ri = lax.broadcasted_iota(jnp.int32, (tb, tb), 0)
    ci = lax.broadcasted_iota(jnp.int32, (tb, tb), 1)
    same = (ri // GLA_C) == (ci // GLA_C)
    tri = same & ((ri <= ci) if rev else (ri >= ci))
    trif = tri.astype(F32)
    bcum = _dot(trif, la, precision=HI)
    blast = _dot(same.astype(F32), la, precision=HI)
    eb, enb, ee = jnp.exp(bcum), jnp.exp(-bcum), jnp.exp(blast - bcum)
    return trif, jnp.exp(blast), eb, enb, ee, q * eb, k * enb, k * ee


def _chunk_mask(rev):
    ri = lax.broadcasted_iota(jnp.int32, (GLA_C, GLA_C), 0)
    ci = lax.broadcasted_iota(jnp.int32, (GLA_C, GLA_C), 1)
    return (ri <= ci) if rev else (ri >= ci)


def _gla_fwd(name, z, la, *, rev):
    n_tok = z.shape[0]
    tb = _tile(n_tok, GLA_TB, GLA_C)
    nblk, per = n_tok // tb, tb // GLA_C
    scale = GLA_D ** -0.5

    def body(q_ref, k_ref, v_ref, la_ref, o_ref, s_ref, st, qd_s, kd_s, ke_s, v_s, dec_s):
        @pl.when(pl.program_id(0) == 0)
        def _():
            st[...] = jnp.zeros(st.shape, F32)

        _, dec, _, _, _, qd, kd, ke = _gla_block_terms(la_ref[...], q_ref[...] * scale, k_ref[...], rev)
        qd_s[...], kd_s[...], ke_s[...] = qd.astype(BF16), kd.astype(BF16), ke.astype(BF16)
        v_s[...] = v_ref[...].astype(BF16)
        dec_s[...] = dec
        tri = _chunk_mask(rev)
        for step in range(per):
            t = per - 1 - step if rev else step
            rows = pl.ds(t * GLA_C, GLA_C)
            for h in range(GLA_H):
                lanes = pl.ds(h * GLA_D, GLA_D)
                qd16, v16 = qd_s[rows, lanes], v_s[rows, lanes]
                sc = jnp.where(tri, _dot(qd16, kd_s[rows, lanes], _NT), 0.0)
                s16 = st[h].astype(BF16)
                s_ref[h, step] = s16
                o_ref[rows, lanes] = _dot(sc.astype(BF16), v16) + _dot(qd16, s16, _NT)
                st[h] = st[h] * dec_s[pl.ds(t * GLA_C, 1), lanes] + _dot(v16, ke_s[rows, lanes], _TN)

    bmap = (lambda n: nblk - 1 - n) if rev else (lambda n: n)

    def zspec(cb):
        return pl.BlockSpec((tb, GLA_W), lambda n: (bmap(n), cb))

    half = pltpu.VMEM((tb, GLA_W), BF16)
    return _pcall(
        body, name=name, grid=(nblk,),
        in_specs=[zspec(1), zspec(2), zspec(3), zspec(0)],
        out_specs=[zspec(0), pl.BlockSpec((GLA_H, per, GLA_D, GLA_D), lambda n: (0, n, 0, 0))],
        out_shape=[jax.ShapeDtypeStruct((n_tok, GLA_W), F32),
                   jax.ShapeDtypeStruct((GLA_H, n_tok // GLA_C, GLA_D, GLA_D), BF16)],
        scratch_shapes=[pltpu.VMEM((GLA_H, GLA_D, GLA_D), F32), half, half, half, half, pltpu.VMEM((tb, GLA_W), F32)],
        compiler_params=_cparams(("arbitrary",), 40 * tb * GLA_W * 4),
    )(z, z, z, la)


def _gla_bwd(name, z, la, s_prev, do, *, rev):
    n_tok = z.shape[0]
    tb = _tile(n_tok, GLA_TB, GLA_C)
    nblk, per = n_tok // tb, tb // GLA_C
    scale = GLA_D ** -0.5

    def body(q_ref, k_ref, v_ref, la_ref, s_ref, do_ref, dq_ref, dk_ref, dv_ref, dla_ref,
             dst, qd_s, kd_s, ke_s, v_s, do_s, dec_s, dqd_s, dkd_s, dke_s, dd_s):
        @pl.when(pl.program_id(0) == 0)
        def _():
            dst[...] = jnp.zeros(dst.shape, F32)

        trif, dec, eb, enb, ee, qd, kd, ke = _gla_block_terms(la_ref[...], q_ref[...] * scale, k_ref[...], rev)
        qd_s[...], kd_s[...], ke_s[...] = qd.astype(BF16), kd.astype(BF16), ke.astype(BF16)
        v_s[...] = v_ref[...].astype(BF16)
        do_s[...] = do_ref[...].astype(BF16)
        dec_s[...] = dec
        tri = _chunk_mask(rev)
        for step in range(per):
            t = step if rev else per - 1 - step
            rows = pl.ds(t * GLA_C, GLA_C)
            for h in range(GLA_H):
                lanes = pl.ds(h * GLA_D, GLA_D)
                qd16, kd16, ke16 = qd_s[rows, lanes], kd_s[rows, lanes], ke_s[rows, lanes]
                v16, do16 = v_s[rows, lanes], do_s[rows, lanes]
                sc16 = jnp.where(tri, _dot(qd16, kd16, _NT), 0.0).astype(BF16)
                dsc16 = jnp.where(tri, _dot(do16, v16, _NT), 0.0).astype(BF16)
                s16 = s_ref[h, per - 1 - step]
                dsn = dst[h]
                dsn16 = dsn.astype(BF16)
                dec_c = dec_s[pl.ds(t * GLA_C, 1), lanes]
                dqd_s[rows, lanes] = _dot(dsc16, kd16) + _dot(do16, s16)
                dkd_s[rows, lanes] = _dot(dsc16, qd16, _TN)
                dke_s[rows, lanes] = _dot(v16, dsn16)
                dv_ref[rows, lanes] = _dot(sc16, do16, _TN) + _dot(ke16, dsn16, _NT)
                ddec = jnp.sum(dsn * s16.astype(F32), axis=0, keepdims=True) * dec_c
                dd_s[rows, lanes] = jnp.broadcast_to(ddec, (GLA_C, GLA_D))
                dst[h] = dsn * dec_c + _dot(do16, qd16, _TN)
        dqd, dkd, dke = dqd_s[...], dkd_s[...], dke_s[...]
        dq_ref[...] = dqd * eb * scale
        dk_ref[...] = dkd * enb + dke * ee
        db = dqd * qd - dkd * kd - dke * ke
        ri = lax.broadcasted_iota(jnp.int32, (tb, tb), 0)
        ci = lax.broadcasted_iota(jnp.int32, (tb, tb), 1)
        same = ((ri // GLA_C) == (ci // GLA_C)).astype(F32)
        dla_ref[...] = _dot(trif, db, _TN, precision=HI) + _dot(same, dke * ke, precision=HI) + dd_s[...]

    bmap = (lambda n: n) if rev else (lambda n: nblk - 1 - n)

    def zspec(cb):
        return pl.BlockSpec((tb, GLA_W), lambda n: (bmap(n), cb))

    o4 = jax.ShapeDtypeStruct((n_tok, GLA_W), F32)
    half, full = pltpu.VMEM((tb, GLA_W), BF16), pltpu.VMEM((tb, GLA_W), F32)
    return _pcall(
        body, name=name, grid=(nblk,),
        in_specs=[zspec(1), zspec(2), zspec(3), zspec(0),
                  pl.BlockSpec((GLA_H, per, GLA_D, GLA_D), lambda n: (0, nblk - 1 - n, 0, 0)), zspec(0)],
        out_specs=[zspec(0)] * 4, out_shape=[o4] * 4,
        scratch_shapes=[pltpu.VMEM((GLA_H, GLA_D, GLA_D), F32), half, half, half, half, half, full, full, full, full, full],
        compiler_params=_cparams(("arbitrary",), 64 * tb * GLA_W * 4),
    )(z, z, z, la, s_prev, do)


_PAIRS = AT_W // LANE
_PAIRS_PER_KV = _PAIRS // AT_KH


def _attn_fwd(name, q, kd, vd, tq=256):
    n_tok = q.shape[0]
    tq = _tile(n_tok, tq, 16)

    def body(q_ref, k_ref, v_ref, o_ref, l_ref):
        qv, kv, vv = q_ref[...], k_ref[...], v_ref[...]
        low = lax.broadcasted_iota(jnp.int32, qv.shape, 1) < AT_D
        res = []
        for keep in (low, jnp.logical_not(low)):
            s = _dot(jnp.where(keep, qv, jnp.zeros_like(qv)), kv, _NT)
            m = jnp.max(s, axis=-1, keepdims=True)
            p = jnp.exp(s - m)
            den = jnp.sum(p, axis=-1, keepdims=True)
            res.append((_dot(p.astype(BF16), vv) / den, m + jnp.log(den)))
        o_ref[...] = jnp.where(low, res[0][0], res[1][0])
        l_ref[...] = jnp.where(low, res[0][1], res[1][1])

    qspec = pl.BlockSpec((tq, LANE), lambda p, i: (i, p))
    kspec = pl.BlockSpec((n_tok, LANE), lambda p, i: (0, p // _PAIRS_PER_KV))
    vm = 5 * _nbytes((tq, n_tok), F32) + 8 * _nbytes((n_tok, LANE), BF16)
    out = jax.ShapeDtypeStruct((n_tok, AT_W), F32)
    return _pcall(
        body, name=name, grid=(_PAIRS, n_tok // tq), in_specs=[qspec, kspec, kspec], out_specs=[qspec, qspec],
        out_shape=[out, out], compiler_params=_cparams(("parallel", "parallel"), vm),
    )(q, kd, vd)


def _attn_bwd(name, q, kd, vd, o, lse, do, tq=256):
    n_tok = q.shape[0]
    tq = _tile(n_tok, tq, 16)

    def body(q_ref, k_ref, v_ref, o_ref, l_ref, do_ref, dq_ref, dk_ref, dv_ref):
        @pl.when((pl.program_id(1) == 0) & (pl.program_id(2) == 0))
        def _():
            dk_ref[...] = jnp.zeros(dk_ref.shape, F32)
            dv_ref[...] = jnp.zeros(dv_ref.shape, F32)

        qv, kv, vv = q_ref[...], k_ref[...], v_ref[...]
        dof, lv = do_ref[...], l_ref[...]
        do16 = dof.astype(BF16)
        doo = dof * o_ref[...]
        low = lax.broadcasted_iota(jnp.int32, qv.shape, 1) < AT_D
        dq = jnp.zeros(qv.shape, F32)
        for keep in (low, jnp.logical_not(low)):
            qm = jnp.where(keep, qv, jnp.zeros_like(qv))
            dom = jnp.where(keep, do16, jnp.zeros_like(do16))
            lh = jnp.max(jnp.where(keep, lv, -1e30), axis=-1, keepdims=True)
            p = jnp.exp(_dot(qm, kv, _NT) - lh)
            delta = jnp.sum(jnp.where(keep, doo, 0.0), axis=-1, keepdims=True)
            ds16 = (p * (_dot(dom, vv, _NT) - delta)).astype(BF16)
            dq = jnp.where(keep, _dot(ds16, kv), dq)
            dk_ref[...] += _dot(ds16, qm, _TN)
            dv_ref[...] += _dot(p.astype(BF16), dom, _TN)
        dq_ref[...] = dq

    qspec = pl.BlockSpec((tq, LANE), lambda g, j, i: (i, g * _PAIRS_PER_KV + j))
    kspec = pl.BlockSpec((n_tok, LANE), lambda g, j, i: (0, g))
    vm = 7 * _nbytes((tq, n_tok), F32) + 12 * _nbytes((n_tok, LANE), F32)
    dup = jax.ShapeDtypeStruct((n_tok, AT_KH * LANE), F32)
    return _pcall(
        body, name=name, grid=(AT_KH, _PAIRS_PER_KV, n_tok // tq),
        in_specs=[qspec, kspec, kspec, qspec, qspec, qspec], out_specs=[qspec, kspec, kspec],
        out_shape=[jax.ShapeDtypeStruct((n_tok, AT_W), F32), dup, dup],
        compiler_params=_cparams(("parallel", "arbitrary", "arbitrary"), vm),
    )(q, kd, vd, o, lse, do)


def _s5_prep(lam_re, lam_im, log_dt, b_re, b_im, c_re, c_im):
    dt = jnp.exp(log_dt)[:, None]
    er = jnp.exp(lam_re * dt)
    a_re, a_im = er * jnp.cos(lam_im * dt), er * jnp.sin(lam_im * dt)
    nr, ni = a_re - 1.0, a_im
    den = lam_re * lam_re + lam_im * lam_im
    f_re, f_im = (nr * lam_re + ni * lam_im) / den, (ni * lam_re - nr * lam_im) / den
    bb_re = f_re[..., None] * b_re - f_im[..., None] * b_im
    bb_im = f_re[..., None] * b_im + f_im[..., None] * b_re
    def tiles(t, axis):
        _, a, b = t.shape
        eye = jnp.eye(S5_TG, dtype=t.dtype)
        blocks = t.reshape(S5_TILES, S5_TG, a, 1, b) * eye[None, :, None, :, None]
        if axis == 0:
            return blocks.reshape(S5_TILES * S5_TG * a, S5_TG * b)
        return blocks.transpose(1, 2, 0, 3, 4).reshape(S5_TG * a, S5_TILES * S5_TG * b)

    w_in = jnp.concatenate([tiles(bb_re.transpose(0, 2, 1), 1), tiles(bb_im.transpose(0, 2, 1), 1)], axis=1)
    w_out = jnp.concatenate([tiles(c_re.transpose(0, 2, 1), 0), -tiles(c_im.transpose(0, 2, 1), 0)], axis=0)
    return a_re.reshape(1, S5_N), a_im.reshape(1, S5_N), w_in, w_out


def _rope_tables(n_tok):
    quarter = AT_D // 4
    d = np.arange(AT_D)
    e = d % (AT_D // 2)
    inv = (ROPE_BASE ** (-(e % quarter).astype(np.float64) * 2.0 / (AT_D // 2))).astype(np.float32)
    sign = np.where(e < quarter, -1.0, 1.0).astype(np.float32)
    rot = np.zeros((AT_D, AT_D), np.float32)
    rot[np.where(e < quarter, d + quarter, d - quarter), d] = 1.0
    pos = jnp.arange(n_tok, dtype=jnp.int32)
    axis_pos = jnp.where(jnp.asarray(d // (AT_D // 2) == 0)[None, :], (pos // GRID_W)[:, None], (pos % GRID_W)[:, None])
    ang = axis_pos.astype(F32) * jnp.asarray(inv)[None, :]
    return jnp.cos(ang), jnp.sin(ang) * jnp.asarray(sign)[None, :], rot


def _head_consts(n_tok, heads):
    cos, sin, rot = _rope_tables(n_tok)
    eye = np.eye(heads, dtype=np.float32)
    mean = np.kron(eye, np.full((AT_D, AT_D), 1.0 / AT_D, np.float32))
    return jnp.tile(cos, (1, heads)), jnp.tile(sin, (1, heads)), jnp.asarray(mean), jnp.asarray(np.kron(eye, rot))


def _pad_w_in(w):
    return jnp.concatenate([w[:, :2560], w[:, 2592:3360], w[:, 2560:2592], jnp.zeros((w.shape[0], 96), w.dtype)], axis=1)


def _unpad_w_in(g):
    return jnp.concatenate([g[:, :2560], g[:, 3328:3360], g[:, 2560:3328]], axis=1)


def _ffn_fwd(x, g, wg, wu, wd, d):
    n_tok = x.shape[0]
    fq = wd.shape[-2]
    tm = _tile(n_tok, 512, 16)

    def up(x_ref, g_ref, wg_ref, wu_ref, h_ref, a_ref, b_ref, s_ref):
        @pl.when(pl.program_id(1) == 0)
        def _():
            h_ref[...] = _rms(x_ref[...], g_ref[...]).astype(BF16)

        hv = h_ref[...]
        av, bv = _dot(hv, wg_ref[0, 0], _NT), _dot(hv, wu_ref[0, 0], _NT)
        a_ref[0] = av.astype(BF16)
        b_ref[0] = bv.astype(BF16)
        s_ref[0] = (av * _sigmoid(av) * bv).astype(BF16)

    wspec = pl.BlockSpec((1, 1, fq, D_MODEL), lambda i, j: (j, d, 0, 0))
    aspec = pl.BlockSpec((1, tm, fq), lambda i, j: (j, i, 0))
    xspec = pl.BlockSpec((tm, D_MODEL), lambda i, j: (i, 0))
    act = jax.ShapeDtypeStruct((N_CHIPS, n_tok, fq), BF16)
    h, a, b, s = _pcall(
        up, name="ffn_up", grid=(n_tok // tm, N_CHIPS),
        in_specs=[xspec, pl.BlockSpec((1, D_MODEL), lambda i, j: (0, 0)), wspec, wspec],
        out_specs=[xspec, aspec, aspec, aspec],
        out_shape=[jax.ShapeDtypeStruct((n_tok, D_MODEL), BF16), act, act, act],
        compiler_params=_cparams(("parallel", "arbitrary"), 4 * D_MODEL * fq * 2 + 16 * tm * fq * 4 + 8 * tm * D_MODEL * 4),
    )(x, g, wg, wu)
    tn = _tile(D_MODEL, 512, LANE)

    def down(s_ref, w_ref, x_ref, o_ref):
        acc = _dot(s_ref[0], w_ref[0, 0])
        for j in range(1, N_CHIPS):
            acc = acc + _dot(s_ref[j], w_ref[j, 0])
        o_ref[...] = x_ref[...] + 0.5 * acc

    xo = _pcall(
        down, name="ffn_down", grid=(n_tok // tm, D_MODEL // tn),
        in_specs=[pl.BlockSpec((N_CHIPS, tm, fq), lambda i, n: (0, i, 0)),
                  pl.BlockSpec((N_CHIPS, 1, fq, tn), lambda i, n: (0, d, 0, n)),
                  pl.BlockSpec((tm, tn), lambda i, n: (i, n))],
        out_specs=pl.BlockSpec((tm, tn), lambda i, n: (i, n)), out_shape=jax.ShapeDtypeStruct((n_tok, D_MODEL), F32),
        compiler_params=_cparams(("parallel", "parallel"), 4 * N_CHIPS * (tm + tn) * fq * 2 + 6 * tm * tn * 4),
    )(s, wd, x)
    return xo, (x, h, a, b, s)


def _ffn_bwd(dy, dy16, saved, g, wg, wu, wd, d, into):
    x, h, a, b, s = saved
    n_tok = x.shape[0]
    fq = wd.shape[-2]
    tm = _tile(n_tok, 512, 16)

    def dact(dy_ref, w_ref, a_ref, b_ref, da_ref, db_ref):
        ds = 0.5 * _dot(dy_ref[...], w_ref[0, 0], _NT)
        av, bv = a_ref[0].astype(F32), b_ref[0].astype(F32)
        sg = _sigmoid(av)
        da_ref[0] = (ds * bv * sg * (1.0 + av * (1.0 - sg))).astype(BF16)
        db_ref[0] = (ds * av * sg).astype(BF16)

    aspec = pl.BlockSpec((1, tm, fq), lambda i, j: (j, i, 0))
    act = jax.ShapeDtypeStruct((N_CHIPS, n_tok, fq), BF16)
    da, db = _pcall(
        dact, name="ffn_dact", grid=(n_tok // tm, N_CHIPS),
        in_specs=[pl.BlockSpec((tm, D_MODEL), lambda i, j: (i, 0)),
                  pl.BlockSpec((1, 1, fq, D_MODEL), lambda i, j: (j, d, 0, 0)), aspec, aspec],
        out_specs=[aspec, aspec], out_shape=[act, act],
        compiler_params=_cparams(("parallel", "parallel"), 4 * D_MODEL * fq * 2 + 16 * tm * fq * 4),
    )(dy16, wd, a, b)

    any_spec = pl.BlockSpec(memory_space=pl.ANY)
    alias = {} if into is None else {2: 0}
    tn = _tile(D_MODEL, 512, LANE)

    def dw_call(t, rhs, scale, prev):
        def dw(t_ref, r_ref, *rest):
            rest[-1][0, 0] = (scale * _dot(t_ref[0], r_ref[...], _TN)).astype(GRAD_DT)

        return _pcall(
            dw, name="ffn_dw", grid=(N_CHIPS, D_MODEL // tn),
            in_specs=[pl.BlockSpec((1, n_tok, fq), lambda j, n: (j, 0, 0)), pl.BlockSpec((n_tok, tn), lambda j, n: (0, n))]
            + ([] if prev is None else [any_spec]),
            out_specs=pl.BlockSpec((1, 1, fq, tn), lambda j, n: (j, d, 0, n)),
            out_shape=jax.ShapeDtypeStruct((N_CHIPS, DEPTH, fq, D_MODEL), GRAD_DT), input_output_aliases=alias,
            compiler_params=_cparams(("parallel", "parallel"), 6 * n_tok * (fq + tn) * 2 + 4 * fq * tn * 4),
        )(t, rhs, *([] if prev is None else [prev]))

    dwg = dw_call(da, h, 1.0, None if into is None else into[0])
    dwu = dw_call(db, h, 1.0, None if into is None else into[1])
    dwd = dw_call(s, dy16, 0.5, None if into is None else into[2])
    tr = _tile(n_tok, 256, 16)

    def dh_norm(da_ref, db_ref, wg_ref, wu_ref, dy_ref, x_ref, g_ref, dx_ref, dx16_ref, dg_ref):
        dh = _dot(da_ref[0], wg_ref[0, 0]) + _dot(db_ref[0], wu_ref[0, 0])
        for j in range(1, N_CHIPS):
            dh = dh + _dot(da_ref[j], wg_ref[j, 0]) + _dot(db_ref[j], wu_ref[j, 0])
        dx, dg = _rms_bwd(x_ref[...], g_ref[...], dh)
        dx = dy_ref[...] + dx
        dx_ref[...] = dx
        dx16_ref[...] = dx.astype(BF16)

        @pl.when(pl.program_id(0) == 0)
        def _():
            dg_ref[...] = jnp.zeros(dg_ref.shape, F32)

        dg_ref[...] += dg

    tspec = pl.BlockSpec((N_CHIPS, tr, fq), lambda i: (0, i, 0))
    wspec = pl.BlockSpec((N_CHIPS, 1, fq, D_MODEL), lambda i: (0, d, 0, 0))
    rspec = pl.BlockSpec((tr, D_MODEL), lambda i: (i, 0))
    vspec = pl.BlockSpec((1, D_MODEL), lambda i: (0, 0))
    dx, dx16, dg = _pcall(
        dh_norm, name="ffn_dh_norm", grid=(n_tok // tr,),
        in_specs=[tspec, tspec, wspec, wspec, rspec, rspec, vspec], out_specs=[rspec, rspec, vspec],
        out_shape=[jax.ShapeDtypeStruct((n_tok, D_MODEL), F32), jax.ShapeDtypeStruct((n_tok, D_MODEL), BF16),
                   jax.ShapeDtypeStruct((1, D_MODEL), F32)],
        compiler_params=_cparams(("arbitrary",), 4 * N_CHIPS * D_MODEL * fq * 2 + 4 * N_CHIPS * tr * fq * 2
                                 + 16 * tr * D_MODEL * 4),
    )(da, db, wg, wu, dy, x, g)
    return dx, dx16, dg[0], (dwg, dwu, dwd)


def _alpha_pads(w_alpha):
    z = jnp.zeros((LANE - 2 * GLA_R, GLA_W), F32)
    wf = jnp.concatenate([w_alpha[0], jnp.zeros((GLA_R, GLA_W), F32), z], axis=0)
    wb = jnp.concatenate([jnp.zeros((GLA_R, GLA_W), F32), w_alpha[1], z], axis=0)
    return wf, wb


def _dup_heads(t):
    low = lax.broadcasted_iota(jnp.int32, t.shape, 1) < AT_D
    h0, h1 = jnp.where(low, t, 0.0), jnp.where(low, 0.0, t)
    return jnp.concatenate([h0 + pltpu.roll(h0, AT_D, 1), h1 + pltpu.roll(h1, AT_D, 1)], axis=1)


def _fold_heads(t):
    low = lax.broadcasted_iota(jnp.int32, (t.shape[0], LANE), 1) < AT_D
    g0, g1 = t[:, :LANE], t[:, LANE:]
    return jnp.where(low, g0 + pltpu.roll(g0, AT_D, 1), g1 + pltpu.roll(g1, AT_D, 1))


def _mixer_fwd(x, w):
    n_tok = x.shape[0]
    h = _rowwise("mix_norm", lambda xv, gv: _rms(xv, gv), [x], [w['mix_norm']], [(D_MODEL, BF16)])[0]
    z = _mm("mix_in", h, w['w_in_p'])
    gp = _mm("mix_gate", h, w['w_merge_gate'], out_dtype=BF16)
    u = _cw(z, 0, S5_W)

    xs, yy = [], None
    for d, rev in ((0, False), (1, True)):
        a_re, a_im, wb, wc = w['s5'][d]
        x_d, yy = _s5_fwd("s5_rev" if rev else "s5_fwd", z, a_re, a_im, wb, wc, yy, rev=rev)
        xs.append(x_d)

    def s5_post(yv, uv, dv, wglu):
        y0 = yv + dv * uv
        y1 = _gelu(y0)
        t = _dot(y1.astype(BF16), wglu)
        return y0, t, y1 * _sigmoid(t)

    y0, t_glu, y_s5 = _rowwise("s5_post", s5_post, [yy, u], [w['s5_d'], w['s5_w_glu']],
                               [(S5_W, F32), (S5_W, F32), (S5_W, BF16)])

    wa_f, wa_b = _alpha_pads(w['gla_w_alpha'])
    ba_f, ba_b = w['gla_b_alpha'][0:1], w['gla_b_alpha'][1:2]
    zz = _cw(z, ZC_Z, LANE)

    def gla_prep(zv, waf, wab, bf, bb):
        z16 = zv.astype(BF16)
        lf = _dot(z16, waf.astype(BF16)) + bf
        lb = _dot(z16, wab.astype(BF16)) + bb
        return _log_sigmoid(lf) / GLA_TAU, _log_sigmoid(lb) / GLA_TAU

    la_f, la_b = _rowwise("gla_prep", gla_prep, [zz], [wa_f, wa_b, ba_f, ba_b], [(GLA_W, F32), (GLA_W, F32)])
    o_f, s_f = _gla_fwd("gla_fwd", z, la_f, rev=False)
    o_b, s_b = _gla_fwd("gla_rev", z, la_b, rev=True)
    gate = _cw(z, 4, GLA_W)

    def gla_post(of, ob, gv, gn):
        o = of + ob
        outs = []
        for hh in range(GLA_H):
            sl = slice(hh * GLA_D, (hh + 1) * GLA_D)
            gh = gv[:, sl]
            outs.append(_rms(o[:, sl], gn) * (gh * _sigmoid(gh)))
        return jnp.concatenate(outs, axis=1)

    y_gla = _rowwise("gla_post", gla_post, [o_f, o_b, gate], [w['gla_norm']], [(GLA_W, BF16)])[0]

    cq, sq, mq, rq = _head_consts(n_tok, AT_QH)
    ck, sk, mk, rk = _head_consts(n_tok, AT_KH)
    gq = jnp.tile(w['attn_q_norm'], (1, AT_QH))
    gk = jnp.tile(w['attn_k_norm'], (1, AT_KH))

    def rope_fwd(v, gain, cos, sin, mean, rot, scale):
        r = lax.rsqrt(_dot(v * v, mean, precision=HI) + EPS)
        vn = v * r * gain
        return (vn * cos + _dot(vn, rot, precision=HI) * sin) * scale

    def attn_prep(qv, kv, vv, cqv, sqv, ckv, skv, gqv, gkv, mqv, rqv, mkv, rkv):
        kr = rope_fwd(kv, gkv, ckv, skv, mkv, rkv, 1.0)
        return rope_fwd(qv, gqv, cqv, sqv, mqv, rqv, AT_D ** -0.5), _dup_heads(kr), _dup_heads(vv)

    q_raw, k_raw, v_raw = _cw(z, 5, AT_W), _cw(z, ZC_AK, AT_KW), _cw(z, ZC_AV, AT_KW)
    q16, kd, vd = _rowwise("attn_prep", attn_prep, [q_raw, k_raw, v_raw, cq, sq, ck, sk], [gq, gk, mq, rq, mk, rk],
                           [(AT_W, BF16), (2 * LANE, BF16), (2 * LANE, BF16)])
    y_at, lse = _attn_fwd("attn_fwd", q16, kd, vd)

    p0 = _mm("mix_branch", y_s5, w['w_branch_s5'], out_dtype=BF16)
    p1 = _mm("mix_branch", y_gla, w['w_branch_gla'], out_dtype=BF16)
    p2 = _mm("mix_branch_f32", y_at, w['w_branch_attn'], out_dtype=BF16)

    def merge(gpv, a0, a1, a2, bias):
        g = _sigmoid(gpv.astype(F32) + bias)
        return (g[:, :D_MODEL] * a0.astype(F32) + g[:, D_MODEL:2 * D_MODEL] * a1.astype(F32)
                + g[:, 2 * D_MODEL:] * a2.astype(F32))

    merged = _rowwise("mix_merge", merge, [gp, p0, p1, p2], [w['b_merge_gate']], [(D_MODEL, BF16)])[0]
    xo = _mm("mix_out", merged, w['w_out'], add=x)
    saved = dict(x=x, h=h, z=z, gp=gp, xs=xs, y0=y0, t_glu=t_glu, y_s5=y_s5, la=(la_f, la_b), o=(o_f, o_b), s=(s_f, s_b),
                 y_gla=y_gla, q16=q16, kd=kd, vd=vd, lse=lse, y_at=y_at, p=(p0, p1, p2), merged=merged)
    return xo, saved


def _mixer_bwd(dy, dy16, sv, w, s5_raw):
    z, h = sv['z'], sv['h']
    n_tok = z.shape[0]
    grads = {}
    dmerged = _mm("mix_dmerged", dy16, w['w_out'], tb=True, out_dtype=BF16)
    grads['w_out'] = _mm("mix_dwout", sv['merged'], dy16, ta=True)

    def dmerge(dm, gpv, a0, a1, a2, bias):
        dm = dm.astype(F32)
        g = _sigmoid(gpv.astype(F32) + bias)
        ps = (a0.astype(F32), a1.astype(F32), a2.astype(F32))
        dps, dgs = [], []
        for k in range(3):
            gk = g[:, k * D_MODEL:(k + 1) * D_MODEL]
            dps.append(dm * gk)
            dgs.append(dm * ps[k] * gk * (1.0 - gk))
        dgp = jnp.concatenate(dgs, axis=1)
        return dps[0], dps[1], dps[2], dgp, jnp.sum(dgp, axis=0, keepdims=True)

    dp0, dp1, dp2, dgp, dbm = _rowwise("mix_dmerge", dmerge, [dmerged, sv['gp'], *sv['p']], [w['b_merge_gate']],
                                       [(D_MODEL, BF16)] * 3 + [(3 * D_MODEL, BF16)], accs=[(1, 3 * D_MODEL)])
    grads['b_merge_gate'] = dbm[0]
    grads['w_branch_s5'] = _mm("mix_dbranch", sv['y_s5'], dp0, ta=True)
    grads['w_branch_gla'] = _mm("mix_dbranch", sv['y_gla'], dp1, ta=True)
    grads['w_branch_attn'] = _mm("mix_dbranch_f32", sv['y_at'], dp2, ta=True)
    dy_s5 = _mm("mix_dy", dp0, w['w_branch_s5'], tb=True)
    dy_gla = _mm("mix_dy", dp1, w['w_branch_gla'], tb=True)
    dy_at = _mm("mix_dy", dp2, w['w_branch_attn'], tb=True)
    grads['w_merge_gate'] = _mm("mix_dwgate", h, dgp, ta=True, out_dtype=GRAD_DT)
    dh = _mm("mix_dh_gate", dgp, w['w_merge_gate'], tb=True)
    u = _cw(z, 0, S5_W)

    def s5_post_bwd(dyv, y0, t, uv, dv, wglu):
        sg = _sigmoid(t)
        y1 = _gelu(y0)
        dt = dyv * y1 * sg * (1.0 - sg)
        dy1 = dyv * sg + _dot(dt.astype(BF16), wglu, _NT)
        dy0 = dy1 * _gelu_grad(y0)
        return dy0, dt, y1, dy0 * dv, jnp.sum(dy0 * uv, axis=0, keepdims=True)

    dy0, dt_glu, y1, du, dd = _rowwise("s5_post_bwd", s5_post_bwd, [dy_s5, sv['y0'], sv['t_glu'], u],
                                       [w['s5_d'], w['s5_w_glu']],
                                       [(S5_W, BF16), (S5_W, BF16), (S5_W, BF16), (S5_W, F32)], accs=[(1, S5_W)])
    grads['s5_d'] = dd[0]
    grads['s5_w_glu'] = _mm("s5_dwglu", y1, dt_glu, ta=True)
    s5_cot = []
    for d, rev in ((0, False), (1, True)):
        a_re, a_im, wb, wc = w['s5'][d]
        du, dwb, dwc, da = _s5_bwd("s5_adj_fwd" if rev else "s5_adj_rev", dy0, z, sv['xs'][d], a_re, -a_im, wb, wc, du,
                                   rev=not rev)
        da = jnp.sum(da, axis=0, keepdims=True)
        s5_cot.append((da[:, :S5_N], da[:, S5_N:], dwb, dwc))
    for d in range(2):
        _, pull = jax.vjp(_s5_prep, *s5_raw[d])
        s5_cot[d] = pull(s5_cot[d])
    for k, nm in enumerate(('s5_lambda_re', 's5_lambda_im', 's5_log_dt', 's5_b_re', 's5_b_im', 's5_c_re', 's5_c_im')):
        grads[nm] = jnp.stack([s5_cot[0][k], s5_cot[1][k]])

    gate = _cw(z, 4, GLA_W)

    def gla_post_bwd(dyv, of, ob, gv, gn):
        o = of + ob
        dos, dgates = [], []
        dgn = jnp.zeros((1, GLA_D), F32)
        for hh in range(GLA_H):
            sl = slice(hh * GLA_D, (hh + 1) * GLA_D)
            gh, dyh = gv[:, sl], dyv[:, sl]
            sg = _sigmoid(gh)
            dgates.append(dyh * _rms(o[:, sl], gn) * sg * (1.0 + gh * (1.0 - sg)))
            dx, dg = _rms_bwd(o[:, sl], gn, dyh * gh * sg)
            dos.append(dx)
            dgn = dgn + dg
        return jnp.concatenate(dos, axis=1), jnp.concatenate(dgates, axis=1), dgn

    do, dgate, dgn = _rowwise("gla_post_bwd", gla_post_bwd, [dy_gla, *sv['o'], gate], [w['gla_norm']],
                              [(GLA_W, F32), (GLA_W, F32)], accs=[(1, GLA_D)])
    grads['gla_norm'] = dgn[0]
    dq_f, dk_f, dv_f, dla_f = _gla_bwd("gla_bwd_fwd", z, sv['la'][0], sv['s'][0], do, rev=False)
    dq_b, dk_b, dv_b, dla_b = _gla_bwd("gla_bwd_rev", z, sv['la'][1], sv['s'][1], do, rev=True)
    wa_f, wa_b = _alpha_pads(w['gla_w_alpha'])
    ba_f, ba_b = w['gla_b_alpha'][0:1], w['gla_b_alpha'][1:2]
    zz = _cw(z, ZC_Z, LANE)

    def gla_prep_bwd(dlf, dlb, zv, waf, wab, bf, bb):
        z16 = zv.astype(BF16)
        waf16, wab16 = waf.astype(BF16), wab.astype(BF16)
        xf = _dot(z16, waf16) + bf
        xb = _dot(z16, wab16) + bb
        df = dlf * (_sigmoid(-xf) / GLA_TAU)
        db = dlb * (_sigmoid(-xb) / GLA_TAU)
        dz = _dot(df.astype(BF16), waf16, _NT) + _dot(db.astype(BF16), wab16, _NT)
        return df, db, dz, jnp.sum(df, axis=0, keepdims=True), jnp.sum(db, axis=0, keepdims=True)

    dlog_f, dlog_b, dzz, dba_f, dba_b = _rowwise(
        "gla_prep_bwd", gla_prep_bwd, [dla_f, dla_b, zz], [wa_f, wa_b, ba_f, ba_b],
        [(GLA_W, BF16), (GLA_W, BF16), (LANE, F32)], accs=[(1, GLA_W), (1, GLA_W)])
    dwa_f = _mm("gla_dwa", zz, dlog_f, ta=True)
    dwa_b = _mm("gla_dwa", zz, dlog_b, ta=True)
    grads['gla_w_alpha'] = jnp.stack([dwa_f[:GLA_R], dwa_b[GLA_R:2 * GLA_R]])
    grads['gla_b_alpha'] = jnp.concatenate([dba_f, dba_b], axis=0)

    dq_r, dkd, dvd = _attn_bwd("attn_bwd", sv['q16'], sv['kd'], sv['vd'], sv['y_at'], sv['lse'], dy_at)
    cq, sq, mq, rq = _head_consts(n_tok, AT_QH)
    ck, sk, mk, rk = _head_consts(n_tok, AT_KH)
    gq = jnp.tile(w['attn_q_norm'], (1, AT_QH))
    gk = jnp.tile(w['attn_k_norm'], (1, AT_KH))

    def rope_bwd(dr, v, gain, cos, sin, mean, rot, scale):
        dr = dr * scale
        dvn = dr * cos + _dot(dr * sin, rot, _NT, precision=HI)
        r = lax.rsqrt(_dot(v * v, mean, precision=HI) + EPS)
        vh = v * r
        dg = jnp.sum(dvn * vh, axis=0, keepdims=True)
        dvh = dvn * gain
        return r * (dvh - vh * _dot(dvh * vh, mean, precision=HI)), dg

    def attn_prep_bwd(dq, dk2, dv2, qv, kv, cqv, sqv, ckv, skv, gqv, gkv, mqv, rqv, mkv, rkv):
        dqo, dgq = rope_bwd(dq, qv, gqv, cqv, sqv, mqv, rqv, AT_D ** -0.5)
        dko, dgk = rope_bwd(_fold_heads(dk2), kv, gkv, ckv, skv, mkv, rkv, 1.0)
        return dqo, dko, _fold_heads(dv2), dgq, dgk

    q_raw, k_raw = _cw(z, 5, AT_W), _cw(z, ZC_AK, AT_KW)
    daq, dak, dav, dgq, dgk = _rowwise("attn_prep_bwd", attn_prep_bwd, [dq_r, dkd, dvd, q_raw, k_raw, cq, sq, ck, sk],
                                       [gq, gk, mq, rq, mk, rk], [(AT_W, F32), (AT_KW, F32), (AT_KW, F32)],
                                       accs=[(1, AT_W), (1, AT_KW)])
    grads['attn_q_norm'] = dgq.reshape(AT_QH, AT_D).sum(axis=0)
    grads['attn_k_norm'] = dgk.reshape(AT_KH, AT_D).sum(axis=0)

    def assemble(duv, qf, qb, kf, kb, vf, vb, dg, aq, ak, av, zzv):
        return jnp.concatenate([duv, qf + qb, kf + kb, vf + vb, dg, aq, ak, av, zzv], axis=1)

    dz = _rowwise("mix_dz", assemble, [du, dq_f, dq_b, dk_f, dk_b, dv_f, dv_b, dgate, daq, dak, dav, dzz], [],
                  [(ZP_W, BF16)])[0]
    grads['w_in'] = _unpad_w_in(_mm("mix_dwin", h, dz, ta=True, out_dtype=GRAD_DT))
    dh = _mm("mix_dh_in", dz, w['w_in_p'], tb=True, add=dh)

    def dnorm(dyv, xv, dhv, gv):
        dx, dg = _rms_bwd(xv, gv, dhv)
        dx = dyv + dx
        return dx, dx, dg

    dx, dx16, dg = _rowwise("mix_dnorm", dnorm, [dy, sv['x'], dh], [w['mix_norm']], [(D_MODEL, F32), (D_MODEL, BF16)],
                            accs=[(1, D_MODEL)])
    grads['mix_norm'] = dg[0]
    return dx, dx16, grads


FFN_NAMES = [p + k for p in ('ffn1_', 'ffn2_') for k in ('w_gate', 'w_up', 'w_down')]
MISC_SHARDED = [n for n in SHARDED if n not in FFN_NAMES]
LARGE_SHARDED = FFN_NAMES + ['w_in', 'w_merge_gate']
KEPT_TRANSPOSED = [p + k for p in ('ffn1_', 'ffn2_') for k in ('w_gate', 'w_up')]


def _stored(n, t):
    return jnp.swapaxes(t, 1, 2) if n in KEPT_TRANSPOSED else t


def _layer_weights(ws, rep, i):
    w = {k: _from_shards(ws[k][:, i], SHARD_AXIS[k] - 1) for k in MISC_SHARDED}
    for k in ('ffn1_norm', 'mix_norm', 'ffn2_norm', 's5_d', 'gla_norm', 'attn_q_norm', 'attn_k_norm', 'b_merge_gate'):
        w[k] = rep[k][i].reshape(1, -1)
    w['w_in_p'] = _pad_w_in(w['w_in'])
    s5_raw = [tuple(rep[k][i, d] for k in ('s5_lambda_re', 's5_lambda_im', 's5_log_dt', 's5_b_re', 's5_b_im',
                                            's5_c_re', 's5_c_im')) for d in range(2)]
    w['s5'] = [_s5_prep(*s5_raw[d]) for d in range(2)]
    return w, s5_raw


def _device_step(x, target, ws, rep):
    lw = [_layer_weights(ws, rep, i) for i in range(DEPTH)]
    ffn = {p: (ws[p + 'w_gate'], ws[p + 'w_up'], ws[p + 'w_down']) for p in ('ffn1_', 'ffn2_')}
    saved = []
    for i in range(DEPTH):
        w, _ = lw[i]
        x, s1 = _ffn_fwd(x, w['ffn1_norm'], *ffn['ffn1_'], i)
        x, s2 = _mixer_fwd(x, w)
        x, s3 = _ffn_fwd(x, w['ffn2_norm'], *ffn['ffn2_'], i)
        saved.append((s1, s2, s3))
    gfin = rep['final_norm'].reshape(1, -1)

    def head(xv, tv, gv):
        e = _rms(xv, gv) - tv
        dx, dg = _rms_bwd(xv, gv, e * (1.0 / D_MODEL))
        part = jnp.sum(e * e, axis=0, keepdims=True)
        return dx, dx, dg, part

    dx, dx16, dgfin, part = _rowwise("loss_head", head, [x, target], [gfin], [(D_MODEL, F32), (D_MODEL, BF16)],
                                     accs=[(1, D_MODEL), (1, D_MODEL)])
    loss = (0.5 / D_MODEL) * jnp.sum(part)
    per_layer = []
    dffn = {'ffn1_': None, 'ffn2_': None}
    for i in reversed(range(DEPTH)):
        w, s5_raw = lw[i]
        s1, s2, s3 = saved[i]
        dx, dx16, n3, dffn['ffn2_'] = _ffn_bwd(dx, dx16, s3, w['ffn2_norm'], *ffn['ffn2_'], i, dffn['ffn2_'])
        dx, dx16, g = _mixer_bwd(dx, dx16, s2, w, s5_raw)
        dx, dx16, n1, dffn['ffn1_'] = _ffn_bwd(dx, dx16, s1, w['ffn1_norm'], *ffn['ffn1_'], i, dffn['ffn1_'])
        g['ffn2_norm'], g['ffn1_norm'] = n3, n1
        per_layer.append(g)
    per_layer.reverse()
    stacked = {k: jnp.stack([per_layer[i][k] for i in range(DEPTH)]) for k in per_layer[0]}
    gs = {k: _to_shards(stacked[k], SHARD_AXIS[k]) for k in MISC_SHARDED}
    for p in ('ffn1_', 'ffn2_'):
        gs[p + 'w_gate'], gs[p + 'w_up'], gs[p + 'w_down'] = dffn[p]
    grep = {k: stacked[k] for k in REPLICATED if k != 'final_norm'}
    grep['final_norm'] = dgfin[0]
    return loss, dx, gs, grep


PACK_C = 1024
ROW_TILE = 512


def _rows_of(shape):
    return -(-int(np.prod(shape)) // PACK_C)


def _as_rows(t, lead=0):
    head = t.shape[:lead]
    flat = t.reshape(head + (-1,))
    rows = -(-flat.shape[-1] // PACK_C)
    pad = rows * PACK_C - flat.shape[-1]
    if pad:
        flat = jnp.pad(flat, [(0, 0)] * lead + [(0, pad)])
    return flat.reshape(head + (rows, PACK_C))


def _pack(items, total_rows, lead=0):
    parts = [_as_rows(t, lead) for t in items]
    used = sum(p.shape[lead] for p in parts)
    if total_rows > used:
        parts.append(jnp.zeros(parts[0].shape[:lead] + (total_rows - used, PACK_C), parts[0].dtype))
    return jnp.concatenate(parts, axis=lead)


def _unpack(rows, shapes, lead=0):
    out, r0 = [], 0
    head = rows.shape[:lead]
    for shp in shapes:
        n, size = _rows_of(shp), int(np.prod(shp))
        piece = lax.slice_in_dim(rows, r0, r0 + n, axis=lead).reshape(head + (n * PACK_C,))
        out.append(lax.slice_in_dim(piece, 0, size, axis=lead).reshape(head + tuple(shp)))
        r0 += n
    return out


def _round_up(n, m):
    return -(-n // m) * m


def _to_shards(g, axis):
    shp = g.shape
    g = g.reshape(shp[:axis] + (N_CHIPS, shp[axis] // N_CHIPS) + shp[axis + 1:])
    return jnp.moveaxis(g, axis, 0)


def _from_shards(s, axis):
    s = jnp.moveaxis(s, 0, axis)
    shp = s.shape
    return s.reshape(shp[:axis] + (shp[axis] * shp[axis + 1],) + shp[axis + 2:])


_HBM = pl.BlockSpec(memory_space=pltpu.HBM)


def _place():
    x, y, c = lax.axis_index("x"), lax.axis_index("y"), lax.axis_index("c")
    chips = [(1 - x, y), (x, 1 - y), (1 - x, 1 - y)]
    return x, y, c, chips


def _rcopy(src, dst, send, recv, k, dev):
    return pltpu.make_async_remote_copy(src_ref=src, dst_ref=dst, send_sem=send.at[k], recv_sem=recv.at[k],
                                        device_id=dev, device_id_type=MESH)


def _comm_call(name, body, arrays, out_shapes, n_sems, n_local):
    return pl.pallas_call(
        body, name=name, in_specs=[_HBM] * len(arrays), out_specs=[_HBM] * len(out_shapes), out_shape=out_shapes,
        scratch_shapes=[pltpu.SemaphoreType.DMA((n_sems,)), pltpu.SemaphoreType.DMA((n_sems,)),
                        pltpu.SemaphoreType.DMA((max(n_local, 1),))],
    )(*arrays)


def _ici_peers(x, y, c):
    x_first = c == 0
    first = (jnp.where(x_first, 1 - x, x), jnp.where(x_first, y, 1 - y))
    second = (jnp.where(x_first, x, 1 - x), jnp.where(x_first, 1 - y, y))
    return first, second


def _ag_chips(name, packs):
    n = len(packs)

    def body(*refs):
        p, o, (send, recv, _) = refs[:n], refs[n:2 * n], refs[2 * n:]
        x, y, c, _ = _place()
        (ax, ay), (bx, by) = _ici_peers(x, y, c)
        me, na, nb, nd = 2 * x + y, 2 * ax + ay, 2 * bx + by, 2 * (1 - x) + (1 - y)
        sib = (x, y, 1 - c)
        own = [_rcopy(p[a].at[c], o[a].at[me, c], send, recv, 6 * a, (ax, ay, c)) for a in range(n)]
        own += [_rcopy(p[a].at[c], o[a].at[me, c], send, recv, 6 * a + 1, (bx, by, c)) for a in range(n)]
        for cp in own:
            cp.start()
        passed = []

        def landed(a, chip, k, fwd):
            blk = o[a].at[chip, c]
            _rcopy(blk, blk, send, recv, 6 * a + k, sib).wait_recv()
            for sem, dev in fwd:
                cp = _rcopy(blk, blk, send, recv, 6 * a + sem, dev)
                cp.start()
                passed.append(cp)

        for a in range(n):
            landed(a, na, 0, [(2, (bx, by, c)), (3, sib)])
        for a in range(n):
            landed(a, nb, 1, [(4, sib)])
        for a in range(n):
            landed(a, nd, 2, [(5, sib)])
        for a in range(n):
            for k, chip in ((3, nb), (4, na), (5, nd)):
                blk = o[a].at[chip, 1 - c]
                _rcopy(blk, blk, send, recv, 6 * a + k, sib).wait_recv()
        for cp in own + passed:
            cp.wait_send()

    outs = _comm_call(name, body, packs, [jax.ShapeDtypeStruct((N_CHIPS,) + t.shape, t.dtype) for t in packs], 6 * n, 0)
    chip = 2 * lax.axis_index("x") + lax.axis_index("y")
    return [lax.dynamic_update_slice_in_dim(o, t[None], chip, axis=0) for o, t in zip(outs, packs)]


def _pair_swap(name, gs):
    n = len(gs)
    ns = gs[0].shape[0]

    def body(*refs):
        g, r, (send, recv, _) = refs[:n], refs[n:2 * n], refs[2 * n:]
        x, y, c, _ = _place()
        cps = [_rcopy(g[a].at[j, 1 - c], r[a].at[j], send, recv, ns * a + j, (x, y, 1 - c))
               for a in range(n) for j in range(ns)]
        for cp in cps:
            cp.start()
        for cp in cps:
            cp.wait_recv()
        for cp in cps:
            cp.wait_send()

    return _comm_call(name, body, gs, [jax.ShapeDtypeStruct((ns,) + t.shape[2:], t.dtype) for t in gs], ns * n, 0)


def _halve_first(name, parts):
    n = len(parts)

    def body(*refs):
        p, r, (send, recv, _) = refs[:n], refs[n:2 * n], refs[2 * n:]
        x, y, c, _ = _place()
        (ax, ay), _ = _ici_peers(x, y, c)
        cps = [_rcopy(p[a].at[jnp.where(c == 0, 2 * ax + k, 2 * k + ay)], r[a].at[k], send, recv, 2 * a + k, (ax, ay, c))
               for a in range(n) for k in range(2)]
        for cp in cps:
            cp.start()
        for cp in cps:
            cp.wait_recv()
        for cp in cps:
            cp.wait_send()

    return _comm_call(name, body, parts, [jax.ShapeDtypeStruct((2,) + t.shape[1:], t.dtype) for t in parts], 2 * n, 0)


def _halve_second(name, kept):
    n = len(kept)

    def body(*refs):
        p, r, (send, recv, _) = refs[:n], refs[n:2 * n], refs[2 * n:]
        x, y, c, _ = _place()
        _, (bx, by) = _ici_peers(x, y, c)
        cps = [_rcopy(p[a].at[jnp.where(c == 0, by, bx)], r[a], send, recv, a, (bx, by, c)) for a in range(n)]
        for cp in cps:
            cp.start()
        for cp in cps:
            cp.wait_recv()
        for cp in cps:
            cp.wait_send()

    return _comm_call(name, body, kept, [jax.ShapeDtypeStruct(t.shape[1:], t.dtype) for t in kept], n, 0)


def _add_selected(name, t, r, index, out_dtype):
    single = r.ndim == 2
    r3 = r[None] if single else r
    nk, rh, cc = r3.shape
    tl = _tile(rh, ROW_TILE, 16)

    def body(i_ref, t_ref, r_ref, o_ref):
        o_ref[...] = (t_ref[...].astype(F32) + r_ref[...].astype(F32)).astype(o_ref.dtype)

    out = _pcall(
        body, name=name,
        grid_spec=pltpu.PrefetchScalarGridSpec(
            num_scalar_prefetch=1, grid=(nk, rh // tl),
            in_specs=[pl.BlockSpec((1, tl, cc), lambda k, i, i_ref: (i_ref[k], i, 0)),
                      pl.BlockSpec((1, tl, cc), lambda k, i, i_ref: (k, i, 0))],
            out_specs=pl.BlockSpec((1, tl, cc), lambda k, i, i_ref: (k, i, 0))),
        out_shape=jax.ShapeDtypeStruct(r3.shape, out_dtype),
        compiler_params=_cparams(("parallel", "parallel"), 8 * tl * cc * 4),
    )(index, t, r3)
    return out[0] if single else out


def _pair_share(name, reds):
    n = len(reds)

    def body(*refs):
        r, o, (send, recv, _) = refs[:n], refs[n:2 * n], refs[2 * n:]
        x, y, c, _ = _place()
        cps = [_rcopy(r[a], o[a].at[c], send, recv, a, (x, y, 1 - c)) for a in range(n)]
        for cp in cps:
            cp.start()
        for a in range(n):
            blk = o[a].at[1 - c]
            _rcopy(blk, blk, send, recv, a, (x, y, 1 - c)).wait_recv()
        for cp in cps:
            cp.wait_send()

    outs = _comm_call(name, body, reds, [jax.ShapeDtypeStruct((2,) + t.shape, t.dtype) for t in reds], n, 0)
    core = lax.axis_index("c")
    return [lax.dynamic_update_slice_in_dim(o, t[None], core, axis=0) for o, t in zip(outs, reds)]


def _add_own_half(name, g, r):
    ns, _, rh, cc = g.shape
    tl = _tile(rh, ROW_TILE, 16)
    core = lax.axis_index("c").astype(jnp.int32).reshape(1)

    def body(c_ref, g_ref, r_ref, o_ref):
        o_ref[...] = (g_ref[0].astype(F32) + r_ref[...].astype(F32)).astype(o_ref.dtype)

    return _pcall(
        body, name=name,
        grid_spec=pltpu.PrefetchScalarGridSpec(
            num_scalar_prefetch=1, grid=(ns, rh // tl),
            in_specs=[pl.BlockSpec((1, 1, tl, cc), lambda j, i, c_ref: (j, c_ref[0], i, 0)),
                      pl.BlockSpec((1, tl, cc), lambda j, i, c_ref: (j, i, 0))],
            out_specs=pl.BlockSpec((1, tl, cc), lambda j, i, c_ref: (j, i, 0))),
        out_shape=jax.ShapeDtypeStruct(r.shape, BF16),
        compiler_params=_cparams(("parallel", "parallel"), 8 * tl * cc * 4),
    )(core, g, r)


def _reduce_scatter(gs):
    x, y, c = lax.axis_index("x"), lax.axis_index("y"), lax.axis_index("c")
    mine_first = jnp.stack([jnp.where(c == 0, 2 * x + k, 2 * k + y) for k in range(2)]).astype(jnp.int32)
    mine_second = jnp.where(c == 0, y, x).astype(jnp.int32).reshape(1)
    swapped = _pair_swap("rs_pair_swap", gs)
    parts = [_add_own_half("rs_add_pair", g, r) for g, r in zip(gs, swapped)]
    got = _halve_first("rs_halve_first", parts)
    kept = [_add_selected("rs_add_first", t, r, mine_first, BF16) for t, r in zip(parts, got)]
    got = _halve_second("rs_halve_second", kept)
    reds = [_add_selected("rs_add_second", t, r, mine_second, F32) for t, r in zip(kept, got)]
    return _pair_share("rs_pair_share", reds)


def _adamw(name, g, w, m, v):
    c1 = 1.0 - ADAM_B1 ** ADAM_STEP
    c2 = 1.0 - ADAM_B2 ** ADAM_STEP
    nb, rows, cc = g.shape
    tl = _tile(rows, ROW_TILE, SUBLANE)

    def body(g_ref, w_ref, m_ref, v_ref, d_ref, mo_ref, vo_ref):
        gv = g_ref[...]
        mn = ADAM_B1 * m_ref[...] + (1.0 - ADAM_B1) * gv
        vn = ADAM_B2 * v_ref[...] + (1.0 - ADAM_B2) * (gv * gv)
        d_ref[...] = -ADAM_LR * ((mn / c1) / (jnp.sqrt(vn / c2) + ADAM_EPS) + ADAM_WD * w_ref[...])
        mo_ref[...] = mn
        vo_ref[...] = vn

    spec = pl.BlockSpec((1, tl, cc), lambda b, i: (b, i, 0))
    return _pcall(
        body, name=name, grid=(nb, rows // tl), in_specs=[spec] * 4, out_specs=[spec] * 3,
        out_shape=[jax.ShapeDtypeStruct(g.shape, F32)] * 3,
        compiler_params=_cparams(("parallel", "parallel"), 16 * tl * cc * 4),
    )(g, w, m, v)


def _train_step(a):
    x, target = a['x'][0], a['loss_target'][0]

    names = BF16_GATHER + F32_GATHER
    got = _ag_chips("ag_weights", [_stored(n, a[n]).astype(BF16) for n in BF16_GATHER] + [a[n] for n in F32_GATHER])
    ws = dict(zip(names, got))

    loss, gx, gs, grep = _device_step(x, target, ws, {n: a[n] for n in REPLICATED})
    loss = lax.psum(loss, ("x", "y", "c"))

    small = [n for n in MISC_SHARDED if n not in LARGE_SHARDED]
    small_shapes = [a[n].shape[1:] for n in small]
    rep_shapes = [a[n].shape for n in REPLICATED]
    small_rows = sum(_rows_of(s) for s in small_shapes)
    piece = _round_up(-(-sum(_rows_of(s) for s in rep_shapes) // (2 * N_CHIPS)), SUBLANE)
    rep_rows = 2 * N_CHIPS * piece
    rep_pack = _pack([grep[n] for n in REPLICATED], rep_rows).reshape(N_CHIPS, 2, piece, PACK_C)
    total = _round_up(small_rows + piece, ROW_TILE // 2)
    misc = _pack([gs[n] for n in small] + [rep_pack], total, lead=2)
    reds = _reduce_scatter([gs[n] for n in LARGE_SHARDED] + [misc])
    grad = dict(zip(LARGE_SHARDED, reds[:-1]))
    grad.update(zip(small, _unpack(reds[-1], small_shapes, lead=1)))
    rep_mine = reds[-1][:, small_rows:small_rows + piece]
    rep_all = _ag_chips("ag_replicated", [rep_mine])[0].reshape(1, rep_rows, PACK_C)

    outs = {}
    for n in SHARDED:
        shp = grad[n].shape
        three_d = (shp[0], -1, shp[-1])
        params = (grad[n], *(_stored(n, a[p + n]) for p in ('', 'm_', 'v_')))
        res3 = _adamw("adamw", *(t.reshape(three_d) for t in params))
        for kind, arr in zip(('grad', 'delta', 'new_m', 'new_v'), (grad[n], *res3)):
            outs[kind + '_' + n] = _stored(n, arr.reshape(shp))
    w, m, v = (_pack([a[p + n] for n in REPLICATED], rep_rows)[None] for p in ('', 'm_', 'v_'))
    res3 = _adamw("adamw_replicated", rep_all, w, m, v)
    for kind, arr in zip(('grad', 'delta', 'new_m', 'new_v'), (rep_all, *res3)):
        for n, t in zip(REPLICATED, _unpack(arr[0], rep_shapes)):
            outs[kind + '_' + n] = t
    res = [loss, gx[None]]
    for kind in ('grad', 'delta', 'new_m', 'new_v'):
        res += [outs[kind + '_' + n] for n in W_NAMES]
    return tuple(res)


def kernel(x, ffn1_norm, ffn1_w_gate, ffn1_w_up, ffn1_w_down, mix_norm, w_in, s5_lambda_re, s5_lambda_im, s5_log_dt, s5_b_re, s5_b_im, s5_c_re, s5_c_im, s5_d, s5_w_glu, gla_w_alpha, gla_b_alpha, gla_norm, attn_q_norm, attn_k_norm, w_branch_s5, w_branch_gla, w_branch_attn, w_merge_gate, b_merge_gate, w_out, ffn2_norm, ffn2_w_gate, ffn2_w_up, ffn2_w_down, final_norm, loss_target, m_ffn1_norm, m_ffn1_w_gate, m_ffn1_w_up, m_ffn1_w_down, m_mix_norm, m_w_in, m_s5_lambda_re, m_s5_lambda_im, m_s5_log_dt, m_s5_b_re, m_s5_b_im, m_s5_c_re, m_s5_c_im, m_s5_d, m_s5_w_glu, m_gla_w_alpha, m_gla_b_alpha, m_gla_norm, m_attn_q_norm, m_attn_k_norm, m_w_branch_s5, m_w_branch_gla, m_w_branch_attn, m_w_merge_gate, m_b_merge_gate, m_w_out, m_ffn2_norm, m_ffn2_w_gate, m_ffn2_w_up, m_ffn2_w_down, m_final_norm, v_ffn1_norm, v_ffn1_w_gate, v_ffn1_w_up, v_ffn1_w_down, v_mix_norm, v_w_in, v_s5_lambda_re, v_s5_lambda_im, v_s5_log_dt, v_s5_b_re, v_s5_b_im, v_s5_c_re, v_s5_c_im, v_s5_d, v_s5_w_glu, v_gla_w_alpha, v_gla_b_alpha, v_gla_norm, v_attn_q_norm, v_attn_k_norm, v_w_branch_s5, v_w_branch_gla, v_w_branch_attn, v_w_merge_gate, v_b_merge_gate, v_w_out, v_ffn2_norm, v_ffn2_w_gate, v_ffn2_w_up, v_ffn2_w_down, v_final_norm):
    return _train_step(dict(locals()))
```

```python
import functools
import math

import jax
import jax.numpy as jnp
import numpy as np
from jax import lax
from jax.experimental import pallas as pl
from jax.experimental.pallas import tpu as pltpu

F32 = jnp.float32
BF16 = jnp.bfloat16
HI = lax.Precision.HIGHEST

D_MODEL = 1024
D_FF = 2816
DEPTH = 2
EPS = 1e-6
S5_G, S5_H, S5_P = 32, 16, 64
S5_W = S5_G * S5_H
S5_N = S5_G * S5_P
S5_TG = 8
S5_TILES = S5_G // S5_TG
S5_TC, S5_TS = S5_TG * S5_H, S5_TG * S5_P
GLA_H, GLA_D = 4, 128
GLA_W = GLA_H * GLA_D
GLA_R = 16
GLA_TAU = 16.0
GLA_C = 64
AT_QH, AT_KH, AT_D = 8, 2, 64
AT_W = AT_QH * AT_D
AT_KW = AT_KH * AT_D
GRID_W = 64
ROPE_BASE = 10000.0
ZP_W = 3456
ZC_AK, ZC_AV, ZC_Z = 24, 25, 26

ADAM_LR, ADAM_B1, ADAM_B2, ADAM_EPS, ADAM_WD, ADAM_STEP = 0.001, 0.9, 0.999, 1e-08, 0.01, 10

GRAD_DT = BF16
VMEM_LIMIT_V7X = 56 * 1024 * 1024
LANE = 128
SUBLANE = 8

W_NAMES = ['ffn1_norm', 'ffn1_w_gate', 'ffn1_w_up', 'ffn1_w_down', 'mix_norm', 'w_in', 's5_lambda_re', 's5_lambda_im',
           's5_log_dt', 's5_b_re', 's5_b_im', 's5_c_re', 's5_c_im', 's5_d', 's5_w_glu', 'gla_w_alpha', 'gla_b_alpha',
           'gla_norm', 'attn_q_norm', 'attn_k_norm', 'w_branch_s5', 'w_branch_gla', 'w_branch_attn', 'w_merge_gate',
           'b_merge_gate', 'w_out', 'ffn2_norm', 'ffn2_w_gate', 'ffn2_w_up', 'ffn2_w_down', 'final_norm']
SHARD_AXIS = {'ffn1_w_gate': 2, 'ffn1_w_up': 2, 'ffn1_w_down': 1, 'w_in': 2, 's5_w_glu': 1, 'gla_w_alpha': 3,
              'gla_b_alpha': 2, 'w_branch_s5': 2, 'w_branch_gla': 2, 'w_branch_attn': 2, 'w_merge_gate': 2,
              'w_out': 1, 'ffn2_w_gate': 2, 'ffn2_w_up': 2, 'ffn2_w_down': 1}
F32_GATHER = ['gla_w_alpha', 'gla_b_alpha']
BF16_GATHER = [n for n in W_NAMES if n in SHARD_AXIS and n not in F32_GATHER]
SHARDED = [n for n in W_NAMES if n in SHARD_AXIS]
REPLICATED = [n for n in W_NAMES if n not in SHARD_AXIS]
N_CHIPS = 4
MESH = pl.DeviceIdType.MESH


def _tile(n, target, mult):
    best = None
    for t in range(mult, min(n, target) + 1, mult):
        if n % t == 0:
            best = t
    return best if best is not None else n


def _cparams(sem, vmem_bytes):
    limit = int(min(VMEM_LIMIT_V7X, max(32 * 1024 * 1024, vmem_bytes * 5 // 4)))
    return pltpu.CompilerParams(dimension_semantics=sem, vmem_limit_bytes=limit)


def _pcall(body, **kw):
    shapes = kw.pop('out_shape')
    many = isinstance(shapes, (list, tuple))
    pinned = [pltpu.HBM(t.shape, t.dtype) for t in (shapes if many else [shapes])]
    call = pl.pallas_call(body, out_shape=pinned if many else pinned[0], **kw)

    def run(*ops):
        return call(*[o if jnp.issubdtype(o.dtype, jnp.integer) else pltpu.with_memory_space_constraint(o, pltpu.HBM)
                      for o in ops])

    return run


def _cw(arr, cb, width):
    return (arr, cb, width)


def _win(a):
    if isinstance(a, tuple):
        return a
    return (a, 0, a.shape[-1])


def _nbytes(shape, dtype):
    return int(np.prod(shape)) * jnp.dtype(dtype).itemsize


def _rowwise(name, fn, rows, fulls, outs, accs=(), tl=256):
    rows = [_win(r) for r in rows]
    n_tok = rows[0][0].shape[0]
    tl = _tile(n_tok, tl, 16)
    nr, nf, no = len(rows), len(fulls), len(outs)

    def body(*refs):
        vals = [r[...] for r in refs[:nr + nf]]
        res = fn(*vals)
        if not isinstance(res, (tuple, list)):
            res = (res,)
        for k in range(no):
            o = refs[nr + nf + k]
            o[...] = res[k].astype(o.dtype)
        if accs:
            @pl.when(pl.program_id(0) == 0)
            def _():
                for k in range(len(accs)):
                    a = refs[nr + nf + no + k]
                    a[...] = jnp.zeros(a.shape, a.dtype)
            for k in range(len(accs)):
                a = refs[nr + nf + no + k]
                a[...] += res[no + k]

    in_specs = [pl.BlockSpec((tl, w), functools.partial(lambda i, cb: (i, cb), cb=cb)) for (_, cb, w) in rows]
    in_specs += [pl.BlockSpec(f.shape, functools.partial(lambda i, nd: (0,) * nd, nd=f.ndim)) for f in fulls]
    out_specs = [pl.BlockSpec((tl, c), lambda i: (i, 0)) for (c, _) in outs]
    out_specs += [pl.BlockSpec((r, c), lambda i: (0, 0)) for (r, c) in accs]
    out_shape = [jax.ShapeDtypeStruct((n_tok, c), dt) for (c, dt) in outs]
    out_shape += [jax.ShapeDtypeStruct((r, c), F32) for (r, c) in accs]
    vm = 2 * sum(_nbytes((tl, w), a.dtype) for (a, _, w) in rows)
    vm += 2 * sum(_nbytes(f.shape, f.dtype) for f in fulls)
    vm += 2 * sum(_nbytes((tl, c), dt) for (c, dt) in outs)
    vm += 8 * max([_nbytes((tl, w), F32) for (_, _, w) in rows] + [_nbytes((tl, c), F32) for (c, _) in outs])
    return _pcall(
        body, name=name, grid=(n_tok // tl,), in_specs=in_specs, out_specs=out_specs, out_shape=out_shape,
        compiler_params=_cparams(("arbitrary",) if accs else ("parallel",), vm),
    )(*[r[0] for r in rows], *fulls)


def _mm(name, a, b, *, ta=False, tb=False, out_dtype=F32, alpha=1.0, add=None, tm=512, tn=512):
    a_arr, a_cb, a_w = _win(a)
    b_arr, b_cb, b_w = _win(b)
    if ta:
        kdim, m = a_arr.shape[0], a_w
    else:
        m, kdim = a_arr.shape[0], a_w
    if tb:
        n, kb = b_arr.shape[0], b_w
    else:
        kb, n = b_arr.shape[0], b_w
    assert kdim == kb, (name, kdim, kb)
    if not ta and kdim <= 1024:
        tm = 2 * tm
    tm = _tile(m, tm, LANE if ta else 16)
    tn = _tile(n, tn, LANE)
    if ta:
        a_spec = pl.BlockSpec((kdim, tm), lambda i, j: (0, a_cb * (a_w // tm) + i))
        a_blk = (kdim, tm)
    else:
        a_spec = pl.BlockSpec((tm, kdim), lambda i, j: (i, a_cb))
        a_blk = (tm, kdim)
    if tb:
        b_spec = pl.BlockSpec((tn, kdim), lambda i, j: (j, b_cb))
        b_blk = (tn, kdim)
    else:
        b_spec = pl.BlockSpec((kdim, tn), lambda i, j: (0, b_cb * (b_w // tn) + j))
        b_blk = (kdim, tn)
    dims = (((0 if ta else 1,), (1 if tb else 0,)), ((), ()))
    has_add = add is not None

    def body(*refs):
        a_ref, b_ref = refs[0], refs[1]
        o_ref = refs[-1]
        acc = lax.dot_general(a_ref[...].astype(BF16), b_ref[...].astype(BF16), dims, preferred_element_type=F32)
        if alpha != 1.0:
            acc = acc * alpha
        if has_add:
            acc = acc + refs[2][...].astype(F32)
        o_ref[...] = acc.astype(o_ref.dtype)

    in_specs = [a_spec, b_spec]
    ops = [a_arr, b_arr]
    if has_add:
        in_specs.append(pl.BlockSpec((tm, tn), lambda i, j: (i, j)))
        ops.append(add)
    vm = 2 * (_nbytes(a_blk, a_arr.dtype) + _nbytes(b_blk, b_arr.dtype) + _nbytes((tm, tn), out_dtype))
    vm += _nbytes(a_blk, BF16) + _nbytes(b_blk, BF16) + 3 * _nbytes((tm, tn), F32)
    return _pcall(
        body, name=name, grid=(m // tm, n // tn), in_specs=in_specs,
        out_specs=pl.BlockSpec((tm, tn), lambda i, j: (i, j)),
        out_shape=jax.ShapeDtypeStruct((m, n), out_dtype),
        compiler_params=_cparams(("parallel", "parallel"), vm),
    )(*ops)


def _sigmoid(x):
    return 1.0 / (1.0 + jnp.exp(-x))


def _rms(x, gain):
    r = lax.rsqrt(jnp.mean(x * x, axis=-1, keepdims=True) + EPS)
    return x * r * gain


def _rms_bwd(x, gain, dh):
    r = lax.rsqrt(jnp.mean(x * x, axis=-1, keepdims=True) + EPS)
    xh = x * r
    dg = jnp.sum(dh * xh, axis=0, keepdims=True)
    dxh = dh * gain
    dx = r * (dxh - xh * jnp.mean(dxh * xh, axis=-1, keepdims=True))
    return dx, dg


_GELU_C = math.sqrt(2.0 / math.pi)


def _gelu(y):
    return 0.5 * y * (1.0 + jnp.tanh(_GELU_C * (y + 0.044715 * y * y * y)))


def _gelu_grad(y):
    th = jnp.tanh(_GELU_C * (y + 0.044715 * y * y * y))
    return 0.5 * (1.0 + th) + 0.5 * y * (1.0 - th * th) * _GELU_C * (1.0 + 3.0 * 0.044715 * y * y)


def _log_sigmoid(x):
    return jnp.minimum(x, 0.0) - jnp.log(1.0 + jnp.exp(-jnp.abs(x)))


def _dot(a, b, dims=(((1,), (0,)), ((), ())), precision=None):
    return lax.dot_general(a, b, dims, preferred_element_type=F32, precision=precision)


_NT = (((1,), (1,)), ((), ()))
_TN = (((0,), (0,)), ((), ()))


def _scan_core(b_ref, are_ref, aim_ref, pw, last, of32, *, rev, xf32=None, da_ref=None, lc=512):
    nb, nc = b_ref.shape[0] // SUBLANE, S5_N // lc
    row = lax.broadcasted_iota(jnp.int32, (SUBLANE, lc), 0)

    def cmul(ar, ai, xr, xi):
        return ar * xr - ai * xi, ar * xi + ai * xr

    def hs(c, xr, xi):
        cs = pl.ds(c * lc, lc)
        for k, slot in ((1, 0), (2, 2), (4, 4)):
            ar, ai = pw[slot, :, cs], pw[slot + 1, :, cs]
            if rev:
                sr, si = pltpu.roll(xr, SUBLANE - k, 0), pltpu.roll(xi, SUBLANE - k, 0)
                keep = row < SUBLANE - k
            else:
                sr, si = pltpu.roll(xr, k, 0), pltpu.roll(xi, k, 0)
                keep = row >= k
            sr, si = jnp.where(keep, sr, 0.0), jnp.where(keep, si, 0.0)
            pr, pi = cmul(ar, ai, sr, si)
            xr, xi = xr + pr, xi + pi
        return xr, xi

    @pl.when(pl.program_id(0) == 0)
    def _():
        last[...] = jnp.zeros(last.shape, F32)
        if da_ref is not None:
            da_ref[...] = jnp.zeros(da_ref.shape, F32)
        for c in range(nc):
            cs = pl.ds(c * lc, lc)
            a1r = jnp.broadcast_to(are_ref[:, cs], (SUBLANE, lc))
            a1i = jnp.broadcast_to(aim_ref[:, cs], (SUBLANE, lc))
            a2r, a2i = cmul(a1r, a1i, a1r, a1i)
            a4r, a4i = cmul(a2r, a2i, a2r, a2i)
            for slot, v in enumerate((a1r, a1i, a2r, a2i, a4r, a4i)):
                pw[slot, :, cs] = v
            first = SUBLANE - 1 if rev else 0
            pr, pi = hs(c, jnp.where(row == first, a1r, 0.0), jnp.where(row == first, a1i, 0.0))
            pw[6, :, cs] = pr
            pw[7, :, cs] = pi

    edge = 0 if rev else SUBLANE - 1

    def blk(j, carry):
        jj = (nb - 1 - j) if rev else j
        r0 = pl.multiple_of(jj * SUBLANE, SUBLANE)
        for c in range(nc):
            cre, cim = pl.ds(c * lc, lc), pl.ds(S5_N + c * lc, lc)
            xr, xi = hs(c, b_ref[pl.ds(r0, SUBLANE), cre], b_ref[pl.ds(r0, SUBLANE), cim])
            cr = jnp.broadcast_to(last[edge:edge + 1, cre], (SUBLANE, lc))
            ci = jnp.broadcast_to(last[edge:edge + 1, cim], (SUBLANE, lc))
            pr, pi = cmul(pw[6, :, cre], pw[7, :, cre], cr, ci)
            xr, xi = xr + pr, xi + pi
            if da_ref is not None:
                if rev:
                    qr, qi = pltpu.roll(xr, SUBLANE - 1, 0), pltpu.roll(xi, SUBLANE - 1, 0)
                    fill = row == SUBLANE - 1
                else:
                    qr, qi = pltpu.roll(xr, 1, 0), pltpu.roll(xi, 1, 0)
                    fill = row == 0
                qr, qi = jnp.where(fill, cr, qr), jnp.where(fill, ci, qi)
                zr, zi = xf32[pl.ds(r0, SUBLANE), cre], xf32[pl.ds(r0, SUBLANE), cim]
                da_ref[:, cre] += zr * qr + zi * qi
                da_ref[:, cim] += zr * qi - zi * qr
            last[:, cre] = xr
            last[:, cim] = xi
            of32[pl.ds(r0, SUBLANE), cre] = xr
            of32[pl.ds(r0, SUBLANE), cim] = xi
        return carry

    lax.fori_loop(0, nb, blk, 0)


def _s5_tiles():
    return [(slice((j % S5_TILES) * S5_TC, (j % S5_TILES + 1) * S5_TC), slice(j * S5_TS, (j + 1) * S5_TS))
            for j in range(2 * S5_TILES)]


def _s5_scratch(tb):
    return [pltpu.VMEM((8, SUBLANE, S5_N), F32), pltpu.VMEM((SUBLANE, 2 * S5_N), F32),
            pltpu.VMEM((tb, 2 * S5_N), F32), pltpu.VMEM((tb, 2 * S5_N), F32),
            pltpu.VMEM((S5_TC, 2 * S5_N), BF16), pltpu.VMEM((2 * S5_N, S5_TC), BF16)]


def _s5_fwd(name, z, a_re, a_im, wb, wc, add, *, rev, tb=256):
    n_tok = z.shape[0]
    tb = _tile(n_tok, tb, 16)
    nt = n_tok // tb
    has_add = add is not None

    def body(*refs):
        u_ref, are_ref, aim_ref, wb_ref, wc_ref = refs[:5]
        x_ref, y_ref, pw, last, of32, bu, wb16, wc16 = refs[5 + has_add:]

        @pl.when(pl.program_id(0) == 0)
        def _():
            wb16[...] = wb_ref[...].astype(BF16)
            wc16[...] = wc_ref[...].astype(BF16)

        u16 = u_ref[...].astype(BF16)
        for ch, st in _s5_tiles():
            bu[:, st] = _dot(u16[:, ch], wb16[:, st])
        _scan_core(bu, are_ref, aim_ref, pw, last, of32, rev=rev)
        x16 = of32[...].astype(BF16)
        x_ref[...] = x16
        ys = []
        for k in range(S5_TILES):
            re, im = slice(k * S5_TS, (k + 1) * S5_TS), slice(S5_N + k * S5_TS, S5_N + (k + 1) * S5_TS)
            ys.append(_dot(x16[:, re], wc16[re, :]) + _dot(x16[:, im], wc16[im, :]))
        y = jnp.concatenate(ys, axis=1)
        y_ref[...] = y + refs[5][...] if has_add else y

    tmap = (lambda t: (nt - 1 - t, 0)) if rev else (lambda t: (t, 0))
    const = lambda t: (0, 0)
    in_specs = [pl.BlockSpec((tb, S5_W), tmap), pl.BlockSpec((1, S5_N), const), pl.BlockSpec((1, S5_N), const),
                pl.BlockSpec((S5_TC, 2 * S5_N), const), pl.BlockSpec((2 * S5_N, S5_TC), const)]
    ops = [z, a_re, a_im, wb, wc]
    if has_add:
        in_specs.append(pl.BlockSpec((tb, S5_W), tmap))
        ops.append(add)
    return _pcall(
        body, name=name, grid=(nt,), in_specs=in_specs,
        out_specs=[pl.BlockSpec((tb, 2 * S5_N), tmap), pl.BlockSpec((tb, S5_W), tmap)],
        out_shape=[jax.ShapeDtypeStruct((n_tok, 2 * S5_N), BF16), jax.ShapeDtypeStruct((n_tok, S5_W), F32)],
        scratch_shapes=_s5_scratch(tb), compiler_params=_cparams(("arbitrary",), 10 * tb * 2 * S5_N * 4),
    )(*ops)


def _s5_bwd(name, dy, z, xs, a_re, a_im, wb, wc, add, *, rev, tb=256):
    n_tok = z.shape[0]
    tb = _tile(n_tok, tb, 16)
    nt = n_tok // tb

    def body(dy_ref, u_ref, xs_ref, are_ref, aim_ref, wb_ref, wc_ref, add_ref, du_ref, dwb_ref, dwc_ref, da_ref,
             pw, last, of32, gx, wb16, wc16, xf32):
        @pl.when(pl.program_id(0) == 0)
        def _():
            wb16[...] = wb_ref[...].astype(BF16)
            wc16[...] = wc_ref[...].astype(BF16)
            dwb_ref[...] = jnp.zeros(dwb_ref.shape, F32)
            dwc_ref[...] = jnp.zeros(dwc_ref.shape, F32)

        dy16, u16, x16 = dy_ref[...].astype(BF16), u_ref[...].astype(BF16), xs_ref[...]
        for ch, st in _s5_tiles():
            gx[:, st] = _dot(dy16[:, ch], wc16[st, :], _NT)
        xf32[...] = x16.astype(F32)
        _scan_core(gx, are_ref, aim_ref, pw, last, of32, rev=rev, xf32=xf32, da_ref=da_ref)
        g16 = of32[...].astype(BF16)
        dus = []
        for k in range(S5_TILES):
            re, im = slice(k * S5_TS, (k + 1) * S5_TS), slice(S5_N + k * S5_TS, S5_N + (k + 1) * S5_TS)
            dus.append(_dot(g16[:, re], wb16[:, re], _NT) + _dot(g16[:, im], wb16[:, im], _NT))
        du_ref[...] = jnp.concatenate(dus, axis=1) + add_ref[...]
        for ch, st in _s5_tiles():
            dwb_ref[:, st] += _dot(u16[:, ch], g16[:, st], _TN)
            dwc_ref[st, :] += _dot(x16[:, st], dy16[:, ch], _TN)

    tmap = (lambda t: (nt - 1 - t, 0)) if rev else (lambda t: (t, 0))
    const = lambda t: (0, 0)
    row_w, row_s = pl.BlockSpec((tb, S5_W), tmap), pl.BlockSpec((tb, 2 * S5_N), tmap)
    return _pcall(
        body, name=name, grid=(nt,),
        in_specs=[row_w, row_w, row_s, pl.BlockSpec((1, S5_N), const), pl.BlockSpec((1, S5_N), const),
                  pl.BlockSpec((S5_TC, 2 * S5_N), const), pl.BlockSpec((2 * S5_N, S5_TC), const), row_w],
        out_specs=[row_w, pl.BlockSpec((S5_TC, 2 * S5_N), const), pl.BlockSpec((2 * S5_N, S5_TC), const),
                   pl.BlockSpec((SUBLANE, 2 * S5_N), const)],
        out_shape=[jax.ShapeDtypeStruct((n_tok, S5_W), F32), jax.ShapeDtypeStruct((S5_TC, 2 * S5_N), F32),
                   jax.ShapeDtypeStruct((2 * S5_N, S5_TC), F32), jax.ShapeDtypeStruct((SUBLANE, 2 * S5_N), F32)],
        scratch_shapes=_s5_scratch(tb) + [pltpu.VMEM((tb, 2 * S5_N), F32)],
        compiler_params=_cparams(("arbitrary",), 12 * tb * 2 * S5_N * 4),
    )(dy, z, xs, a_re, a_im, wb, wc, add)


GLA_TB = 512


def _gla_block_terms(la, q, k, rev):
    tb = la.shape[0]
    ri = lax.broadcasted_iota(jnp.int32, (tb, tb), 0)
    ci = lax.broadcasted_iota(jnp.int32, (tb, tb), 1)
    same = (ri // GLA_C) == (ci // GLA_C)
    tri = same & ((ri <= ci) if rev else (ri >= ci))
    trif = tri.astype(F32)
    bcum = _dot(trif, la, precision=HI)
    blast = _dot(same.astype(F32), la, precision=HI)
    eb, enb, ee = jnp.exp(bcum), jnp.exp(-bcum), jnp.exp(blast - bcum)
    return trif, jnp.exp(blast), eb, enb, ee, q * eb, k * enb, k * ee


def _chunk_mask(rev):
    ri = lax.broadcasted_iota(jnp.int32, (GLA_C, GLA_C), 0)
    ci = lax.broadcasted_iota(jnp.int32, (GLA_C, GLA_C), 1)
    return (ri <= ci) if rev else (ri >= ci)


def _gla_fwd(name, z, la, *, rev):
    n_tok = z.shape[0]
    tb = _tile(n_tok, GLA_TB, GLA_C)
    nblk, per = n_tok // tb, tb // GLA_C
    scale = GLA_D ** -0.5

    def body(q_ref, k_ref, v_ref, la_ref, o_ref, s_ref, st, qd_s, kd_s, ke_s, v_s, dec_s):
        @pl.when(pl.program_id(0) == 0)
        def _():
            st[...] = jnp.zeros(st.shape, F32)

        _, dec, _, _, _, qd, kd, ke = _gla_block_terms(la_ref[...], q_ref[...].astype(F32) * scale,
                                                                 k_ref[...].astype(F32), rev)
        qd_s[...], kd_s[...], ke_s[...] = qd.astype(BF16), kd.astype(BF16), ke.astype(BF16)
        v_s[...] = v_ref[...].astype(BF16)
        dec_s[...] = dec
        tri = _chunk_mask(rev)
        for step in range(per):
            t = per - 1 - step if rev else step
            rows = pl.ds(t * GLA_C, GLA_C)
            for h in range(GLA_H):
                lanes = pl.ds(h * GLA_D, GLA_D)
                qd16, v16 = qd_s[rows, lanes], v_s[rows, lanes]
                sc = jnp.where(tri, _dot(qd16, kd_s[rows, lanes], _NT), 0.0)
                s16 = st[h].astype(BF16)
                s_ref[h, step] = s16
                o_ref[rows, lanes] = _dot(sc.astype(BF16), v16) + _dot(qd16, s16, _NT)
                st[h] = st[h] * dec_s[pl.ds(t * GLA_C, 1), lanes] + _dot(v16, ke_s[rows, lanes], _TN)

    bmap = (lambda n: nblk - 1 - n) if rev else (lambda n: n)

    def zspec(cb):
        return pl.BlockSpec((tb, GLA_W), lambda n: (bmap(n), cb))

    half = pltpu.VMEM((tb, GLA_W), BF16)
    return _pcall(
        body, name=name, grid=(nblk,),
        in_specs=[zspec(1), zspec(2), zspec(3), zspec(0)],
        out_specs=[zspec(0), pl.BlockSpec((GLA_H, per, GLA_D, GLA_D), lambda n: (0, n, 0, 0))],
        out_shape=[jax.ShapeDtypeStruct((n_tok, GLA_W), F32),
                   jax.ShapeDtypeStruct((GLA_H, n_tok // GLA_C, GLA_D, GLA_D), BF16)],
        scratch_shapes=[pltpu.VMEM((GLA_H, GLA_D, GLA_D), F32), half, half, half, half, pltpu.VMEM((tb, GLA_W), F32)],
        compiler_params=_cparams(("arbitrary",), 40 * tb * GLA_W * 4),
    )(z, z, z, la)


def _gla_bwd(name, z, la, s_prev, do, *, rev):
    n_tok = z.shape[0]
    tb = _tile(n_tok, GLA_TB, GLA_C)
    nblk, per = n_tok // tb, tb // GLA_C
    scale = GLA_D ** -0.5

    def body(q_ref, k_ref, v_ref, la_ref, s_ref, do_ref, dq_ref, dk_ref, dv_ref, dla_ref,
             dst, qd_s, kd_s, ke_s, v_s, do_s, dec_s, dqd_s, dkd_s, dke_s, dd_s):
        @pl.when(pl.program_id(0) == 0)
        def _():
            dst[...] = jnp.zeros(dst.shape, F32)

        trif, dec, eb, enb, ee, qd, kd, ke = _gla_block_terms(la_ref[...], q_ref[...].astype(F32) * scale,
                                                                 k_ref[...].astype(F32), rev)
        qd_s[...], kd_s[...], ke_s[...] = qd.astype(BF16), kd.astype(BF16), ke.astype(BF16)
        v_s[...] = v_ref[...].astype(BF16)
        do_s[...] = do_ref[...].astype(BF16)
        dec_s[...] = dec
        tri = _chunk_mask(rev)
        for step in range(per):
            t = step if rev else per - 1 - step
            rows = pl.ds(t * GLA_C, GLA_C)
            for h in range(GLA_H):
                lanes = pl.ds(h * GLA_D, GLA_D)
                qd16, kd16, ke16 = qd_s[rows, lanes], kd_s[rows, lanes], ke_s[rows, lanes]
                v16, do16 = v_s[rows, lanes], do_s[rows, lanes]
                sc16 = jnp.where(tri, _dot(qd16, kd16, _NT), 0.0).astype(BF16)
                dsc16 = jnp.where(tri, _dot(do16, v16, _NT), 0.0).astype(BF16)
                s16 = s_ref[h, per - 1 - step]
                dsn = dst[h]
                dsn16 = dsn.astype(BF16)
                dec_c = dec_s[pl.ds(t * GLA_C, 1), lanes]
                dqd_s[rows, lanes] = _dot(dsc16, kd16) + _dot(do16, s16)
                dkd_s[rows, lanes] = _dot(dsc16, qd16, _TN)
                dke_s[rows, lanes] = _dot(v16, dsn16)
                dv_ref[rows, lanes] = _dot(sc16, do16, _TN) + _dot(ke16, dsn16, _NT)
                ddec = jnp.sum(dsn * s16.astype(F32), axis=0, keepdims=True) * dec_c
                dd_s[rows, lanes] = jnp.broadcast_to(ddec, (GLA_C, GLA_D))
                dst[h] = dsn * dec_c + _dot(do16, qd16, _TN)
        dqd, dkd, dke = dqd_s[...], dkd_s[...], dke_s[...]
        dq_ref[...] = dqd * eb * scale
        dk_ref[...] = dkd * enb + dke * ee
        db = dqd * qd - dkd * kd - dke * ke
        ri = lax.broadcasted_iota(jnp.int32, (tb, tb), 0)
        ci = lax.broadcasted_iota(jnp.int32, (tb, tb), 1)
        same = ((ri // GLA_C) == (ci // GLA_C)).astype(F32)
        dla_ref[...] = _dot(trif, db, _TN, precision=HI) + _dot(same, dke * ke, precision=HI) + dd_s[...]

    bmap = (lambda n: n) if rev else (lambda n: nblk - 1 - n)

    def zspec(cb):
        return pl.BlockSpec((tb, GLA_W), lambda n: (bmap(n), cb))

    o4 = jax.ShapeDtypeStruct((n_tok, GLA_W), F32)
    half, full = pltpu.VMEM((tb, GLA_W), BF16), pltpu.VMEM((tb, GLA_W), F32)
    return _pcall(
        body, name=name, grid=(nblk,),
        in_specs=[zspec(1), zspec(2), zspec(3), zspec(0),
                  pl.BlockSpec((GLA_H, per, GLA_D, GLA_D), lambda n: (0, nblk - 1 - n, 0, 0)), zspec(0)],
        out_specs=[zspec(0)] * 4, out_shape=[o4] * 4,
        scratch_shapes=[pltpu.VMEM((GLA_H, GLA_D, GLA_D), F32), half, half, half, half, half, full, full, full, full, full],
        compiler_params=_cparams(("arbitrary",), 64 * tb * GLA_W * 4),
    )(z, z, z, la, s_prev, do)


_PAIRS = AT_W // LANE
_PAIRS_PER_KV = _PAIRS // AT_KH


def _attn_fwd(name, q, kd, vd, tq=256):
    n_tok = q.shape[0]
    tq = _tile(n_tok, tq, 16)

    def body(q_ref, k_ref, v_ref, o_ref, l_ref):
        qv, kv, vv = q_ref[...], k_ref[...], v_ref[...]
        low = lax.broadcasted_iota(jnp.int32, qv.shape, 1) < AT_D
        res = []
        for keep in (low, jnp.logical_not(low)):
            s = _dot(jnp.where(keep, qv, jnp.zeros_like(qv)), kv, _NT)
            m = jnp.max(s, axis=-1, keepdims=True)
            p = jnp.exp(s - m)
            den = jnp.sum(p, axis=-1, keepdims=True)
            res.append((_dot(p.astype(BF16), vv) / den, m + jnp.log(den)))
        o_ref[...] = jnp.where(low, res[0][0], res[1][0])
        l_ref[...] = jnp.where(low, res[0][1], res[1][1])

    qspec = pl.BlockSpec((tq, LANE), lambda p, i: (i, p))
    kspec = pl.BlockSpec((n_tok, LANE), lambda p, i: (0, p // _PAIRS_PER_KV))
    vm = 5 * _nbytes((tq, n_tok), F32) + 8 * _nbytes((n_tok, LANE), BF16)
    out = jax.ShapeDtypeStruct((n_tok, AT_W), F32)
    return _pcall(
        body, name=name, grid=(_PAIRS, n_tok // tq), in_specs=[qspec, kspec, kspec], out_specs=[qspec, qspec],
        out_shape=[out, out], compiler_params=_cparams(("parallel", "parallel"), vm),
    )(q, kd, vd)


def _attn_bwd(name, q, kd, vd, o, lse, do, tq=256):
    n_tok = q.shape[0]
    tq = _tile(n_tok, tq, 16)

    def body(q_ref, k_ref, v_ref, o_ref, l_ref, do_ref, dq_ref, dk_ref, dv_ref):
        @pl.when((pl.program_id(1) == 0) & (pl.program_id(2) == 0))
        def _():
            dk_ref[...] = jnp.zeros(dk_ref.shape, F32)
            dv_ref[...] = jnp.zeros(dv_ref.shape, F32)

        qv, kv, vv = q_ref[...], k_ref[...], v_ref[...]
        dof, lv = do_ref[...], l_ref[...]
        do16 = dof.astype(BF16)
        doo = dof * o_ref[...]
        low = lax.broadcasted_iota(jnp.int32, qv.shape, 1) < AT_D
        dq = jnp.zeros(qv.shape, F32)
        for keep in (low, jnp.logical_not(low)):
            qm = jnp.where(keep, qv, jnp.zeros_like(qv))
            dom = jnp.where(keep, do16, jnp.zeros_like(do16))
            lh = jnp.max(jnp.where(keep, lv, -1e30), axis=-1, keepdims=True)
            p = jnp.exp(_dot(qm, kv, _NT) - lh)
            delta = jnp.sum(jnp.where(keep, doo, 0.0), axis=-1, keepdims=True)
            ds16 = (p * (_dot(dom, vv, _NT) - delta)).astype(BF16)
            dq = jnp.where(keep, _dot(ds16, kv), dq)
            dk_ref[...] += _dot(ds16, qm, _TN)
            dv_ref[...] += _dot(p.astype(BF16), dom, _TN)
        dq_ref[...] = dq

    qspec = pl.BlockSpec((tq, LANE), lambda g, j, i: (i, g * _PAIRS_PER_KV + j))
    kspec = pl.BlockSpec((n_tok, LANE), lambda g, j, i: (0, g))
    vm = 7 * _nbytes((tq, n_tok), F32) + 12 * _nbytes((n_tok, LANE), F32)
    dup = jax.ShapeDtypeStruct((n_tok, AT_KH * LANE), F32)
    return _pcall(
        body, name=name, grid=(AT_KH, _PAIRS_PER_KV, n_tok // tq),
        in_specs=[qspec, kspec, kspec, qspec, qspec, qspec], out_specs=[qspec, kspec, kspec],
        out_shape=[jax.ShapeDtypeStruct((n_tok, AT_W), F32), dup, dup],
        compiler_params=_cparams(("parallel", "arbitrary", "arbitrary"), vm),
    )(q, kd, vd, o, lse, do)


def _s5_prep(lam_re, lam_im, log_dt, b_re, b_im, c_re, c_im):
    dt = jnp.exp(log_dt)[:, None]
    er = jnp.exp(lam_re * dt)
    a_re, a_im = er * jnp.cos(lam_im * dt), er * jnp.sin(lam_im * dt)
    nr, ni = a_re - 1.0, a_im
    den = lam_re * lam_re + lam_im * lam_im
    f_re, f_im = (nr * lam_re + ni * lam_im) / den, (ni * lam_re - nr * lam_im) / den
    bb_re = f_re[..., None] * b_re - f_im[..., None] * b_im
    bb_im = f_re[..., None] * b_im + f_im[..., None] * b_re
    def tiles(t, axis):
        _, a, b = t.shape
        eye = jnp.eye(S5_TG, dtype=t.dtype)
        blocks = t.reshape(S5_TILES, S5_TG, a, 1, b) * eye[None, :, None, :, None]
        if axis == 0:
            return blocks.reshape(S5_TILES * S5_TG * a, S5_TG * b)
        return blocks.transpose(1, 2, 0, 3, 4).reshape(S5_TG * a, S5_TILES * S5_TG * b)

    w_in = jnp.concatenate([tiles(bb_re.transpose(0, 2, 1), 1), tiles(bb_im.transpose(0, 2, 1), 1)], axis=1)
    w_out = jnp.concatenate([tiles(c_re.transpose(0, 2, 1), 0), -tiles(c_im.transpose(0, 2, 1), 0)], axis=0)
    return a_re.reshape(1, S5_N), a_im.reshape(1, S5_N), w_in, w_out


def _rope_tables(n_tok):
    quarter = AT_D // 4
    d = np.arange(AT_D)
    e = d % (AT_D // 2)
    inv = (ROPE_BASE ** (-(e % quarter).astype(np.float64) * 2.0 / (AT_D // 2))).astype(np.float32)
    sign = np.where(e < quarter, -1.0, 1.0).astype(np.float32)
    rot = np.zeros((AT_D, AT_D), np.float32)
    rot[np.where(e < quarter, d + quarter, d - quarter), d] = 1.0
    pos = jnp.arange(n_tok, dtype=jnp.int32)
    axis_pos = jnp.where(jnp.asarray(d // (AT_D // 2) == 0)[None, :], (pos // GRID_W)[:, None], (pos % GRID_W)[:, None])
    ang = axis_pos.astype(F32) * jnp.asarray(inv)[None, :]
    return jnp.cos(ang), jnp.sin(ang) * jnp.asarray(sign)[None, :], rot


def _head_consts(n_tok, heads):
    cos, sin, rot = _rope_tables(n_tok)
    eye = np.eye(heads, dtype=np.float32)
    mean = np.kron(eye, np.full((AT_D, AT_D), 1.0 / AT_D, np.float32))
    return jnp.tile(cos, (1, heads)), jnp.tile(sin, (1, heads)), jnp.asarray(mean), jnp.asarray(np.kron(eye, rot))


def _pad_w_in(w):
    return jnp.concatenate([w[:, :2560], w[:, 2592:3360], w[:, 2560:2592], jnp.zeros((w.shape[0], 96), w.dtype)], axis=1)


def _unpad_w_in(g):
    return jnp.concatenate([g[:, :2560], g[:, 3328:3360], g[:, 2560:3328]], axis=1)


def _ffn_fwd(x, g, wg, wu, wd, d):
    n_tok = x.shape[0]
    fq = wd.shape[-2]
    tm = _tile(n_tok, 256, 16)
    w_bytes = N_CHIPS * fq * D_MODEL * 2

    def up(x_ref, g_ref, wg_ref, wu_ref, h_ref, a_ref, b_ref, s_ref):
        hv = _rms(x_ref[...], g_ref[...]).astype(BF16)
        h_ref[...] = hv
        for j in range(N_CHIPS):
            av, bv = _dot(hv, wg_ref[j, 0], _NT), _dot(hv, wu_ref[j, 0], _NT)
            a_ref[j] = av.astype(BF16)
            b_ref[j] = bv.astype(BF16)
            s_ref[j] = (av * _sigmoid(av) * bv).astype(BF16)

    wspec = pl.BlockSpec((N_CHIPS, 1, fq, D_MODEL), lambda i: (0, d, 0, 0))
    aspec = pl.BlockSpec((N_CHIPS, tm, fq), lambda i: (0, i, 0))
    xspec = pl.BlockSpec((tm, D_MODEL), lambda i: (i, 0))
    act = jax.ShapeDtypeStruct((N_CHIPS, n_tok, fq), BF16)
    h, a, b, s = _pcall(
        up, name="ffn_up", grid=(n_tok // tm,),
        in_specs=[xspec, pl.BlockSpec((1, D_MODEL), lambda i: (0, 0)), wspec, wspec],
        out_specs=[xspec, aspec, aspec, aspec],
        out_shape=[jax.ShapeDtypeStruct((n_tok, D_MODEL), BF16), act, act, act],
        compiler_params=_cparams(("parallel",), 4 * w_bytes + 6 * N_CHIPS * tm * fq * 2 + 12 * tm * fq * 4
                                 + 8 * tm * D_MODEL * 4),
    )(x, g, wg, wu)

    def down(s_ref, w_ref, x_ref, o_ref):
        acc = _dot(s_ref[0], w_ref[0, 0])
        for j in range(1, N_CHIPS):
            acc = acc + _dot(s_ref[j], w_ref[j, 0])
        o_ref[...] = x_ref[...] + 0.5 * acc

    xo = _pcall(
        down, name="ffn_down", grid=(n_tok // tm,), in_specs=[aspec, wspec, xspec], out_specs=xspec,
        out_shape=jax.ShapeDtypeStruct((n_tok, D_MODEL), F32),
        compiler_params=_cparams(("parallel",), 2 * w_bytes + 2 * N_CHIPS * tm * fq * 2 + 8 * tm * D_MODEL * 4),
    )(s, wd, x)
    return xo, (x, h, a, b, s)


def _ffn_bwd(dy, dy16, saved, g, wg, wu, wd, d, into):
    x, h, a, b, s = saved
    n_tok = x.shape[0]
    fq = wd.shape[-2]
    tm = _tile(n_tok, 512, 16)

    ta_ = _tile(n_tok, 256, 16)

    def dact(dy_ref, w_ref, a_ref, b_ref, da_ref, db_ref):
        dyv = dy_ref[...]
        for j in range(N_CHIPS):
            ds = 0.5 * _dot(dyv, w_ref[j, 0], _NT)
            av, bv = a_ref[j].astype(F32), b_ref[j].astype(F32)
            sg = _sigmoid(av)
            da_ref[j] = (ds * bv * sg * (1.0 + av * (1.0 - sg))).astype(BF16)
            db_ref[j] = (ds * av * sg).astype(BF16)

    aspec = pl.BlockSpec((N_CHIPS, ta_, fq), lambda i: (0, i, 0))
    act = jax.ShapeDtypeStruct((N_CHIPS, n_tok, fq), BF16)
    da, db = _pcall(
        dact, name="ffn_dact", grid=(n_tok // ta_,),
        in_specs=[pl.BlockSpec((ta_, D_MODEL), lambda i: (i, 0)),
                  pl.BlockSpec((N_CHIPS, 1, fq, D_MODEL), lambda i: (0, d, 0, 0)), aspec, aspec],
        out_specs=[aspec, aspec], out_shape=[act, act],
        compiler_params=_cparams(("parallel",), 2 * N_CHIPS * fq * D_MODEL * 2 + 8 * N_CHIPS * ta_ * fq * 2
                                 + 12 * ta_ * fq * 4),
    )(dy16, wd, a, b)

    any_spec = pl.BlockSpec(memory_space=pl.ANY)
    alias = {} if into is None else {2: 0}
    tn = _tile(D_MODEL, 512, LANE)

    def dw_call(t, rhs, scale, prev):
        def dw(t_ref, r_ref, *rest):
            rest[-1][0, 0] = (scale * _dot(t_ref[0], r_ref[...], _TN)).astype(GRAD_DT)

        return _pcall(
            dw, name="ffn_dw", grid=(N_CHIPS, D_MODEL // tn),
            in_specs=[pl.BlockSpec((1, n_tok, fq), lambda j, n: (j, 0, 0)), pl.BlockSpec((n_tok, tn), lambda j, n: (0, n))]
            + ([] if prev is None else [any_spec]),
            out_specs=pl.BlockSpec((1, 1, fq, tn), lambda j, n: (j, d, 0, n)),
            out_shape=jax.ShapeDtypeStruct((N_CHIPS, DEPTH, fq, D_MODEL), GRAD_DT), input_output_aliases=alias,
            compiler_params=_cparams(("parallel", "parallel"), 6 * n_tok * (fq + tn) * 2 + 4 * fq * tn * 4),
        )(t, rhs, *([] if prev is None else [prev]))

    dwg = dw_call(da, h, 1.0, None if into is None else into[0])
    dwu = dw_call(db, h, 1.0, None if into is None else into[1])
    dwd = dw_call(s, dy16, 0.5, None if into is None else into[2])
    tr = _tile(n_tok, 256, 16)

    def dh_norm(da_ref, db_ref, wg_ref, wu_ref, dy_ref, x_ref, g_ref, dx_ref, dx16_ref, dg_ref):
        dh = _dot(da_ref[0], wg_ref[0, 0]) + _dot(db_ref[0], wu_ref[0, 0])
        for j in range(1, N_CHIPS):
            dh = dh + _dot(da_ref[j], wg_ref[j, 0]) + _dot(db_ref[j], wu_ref[j, 0])
        dx, dg = _rms_bwd(x_ref[...], g_ref[...], dh)
        dx = dy_ref[...] + dx
        dx_ref[...] = dx
        dx16_ref[...] = dx.astype(BF16)

        @pl.when(pl.program_id(0) == 0)
        def _():
            dg_ref[...] = jnp.zeros(dg_ref.shape, F32)

        dg_ref[...] += dg

    tspec = pl.BlockSpec((N_CHIPS, tr, fq), lambda i: (0, i, 0))
    wspec = pl.BlockSpec((N_CHIPS, 1, fq, D_MODEL), lambda i: (0, d, 0, 0))
    rspec = pl.BlockSpec((tr, D_MODEL), lambda i: (i, 0))
    vspec = pl.BlockSpec((1, D_MODEL), lambda i: (0, 0))
    dx, dx16, dg = _pcall(
        dh_norm, name="ffn_dh_norm", grid=(n_tok // tr,),
        in_specs=[tspec, tspec, wspec, wspec, rspec, rspec, vspec], out_specs=[rspec, rspec, vspec],
        out_shape=[jax.ShapeDtypeStruct((n_tok, D_MODEL), F32), jax.ShapeDtypeStruct((n_tok, D_MODEL), BF16),
                   jax.ShapeDtypeStruct((1, D_MODEL), F32)],
        compiler_params=_cparams(("arbitrary",), 4 * N_CHIPS * D_MODEL * fq * 2 + 4 * N_CHIPS * tr * fq * 2
                                 + 16 * tr * D_MODEL * 4),
    )(da, db, wg, wu, dy, x, g)
    return dx, dx16, dg[0], (dwg, dwu, dwd)


def _alpha_pads(w_alpha):
    z = jnp.zeros((LANE - 2 * GLA_R, GLA_W), F32)
    wf = jnp.concatenate([w_alpha[0], jnp.zeros((GLA_R, GLA_W), F32), z], axis=0)
    wb = jnp.concatenate([jnp.zeros((GLA_R, GLA_W), F32), w_alpha[1], z], axis=0)
    return wf, wb


def _dup_heads(t):
    low = lax.broadcasted_iota(jnp.int32, t.shape, 1) < AT_D
    h0, h1 = jnp.where(low, t, 0.0), jnp.where(low, 0.0, t)
    return jnp.concatenate([h0 + pltpu.roll(h0, AT_D, 1), h1 + pltpu.roll(h1, AT_D, 1)], axis=1)


def _fold_heads(t):
    low = lax.broadcasted_iota(jnp.int32, (t.shape[0], LANE), 1) < AT_D
    g0, g1 = t[:, :LANE], t[:, LANE:]
    return jnp.where(low, g0 + pltpu.roll(g0, AT_D, 1), g1 + pltpu.roll(g1, AT_D, 1))


def _mixer_fwd(x, w):
    n_tok = x.shape[0]
    h = _rowwise("mix_norm", lambda xv, gv: _rms(xv, gv), [x], [w['mix_norm']], [(D_MODEL, BF16)])[0]
    z = _mm("mix_in", h, w['w_in_p'], out_dtype=BF16)
    gp = _mm("mix_gate", h, w['w_merge_gate'], out_dtype=BF16)
    u = _cw(z, 0, S5_W)

    xs, yy = [], None
    for d, rev in ((0, False), (1, True)):
        a_re, a_im, wb, wc = w['s5'][d]
        x_d, yy = _s5_fwd("s5_rev" if rev else "s5_fwd", z, a_re, a_im, wb, wc, yy, rev=rev)
        xs.append(x_d)

    def s5_post(yv, uv, dv, wglu):
        y0 = yv + dv * uv
        y1 = _gelu(y0)
        t = _dot(y1.astype(BF16), wglu)
        return y0, t, y1 * _sigmoid(t)

    y0, t_glu, y_s5 = _rowwise("s5_post", s5_post, [yy, u], [w['s5_d'], w['s5_w_glu']],
                               [(S5_W, F32), (S5_W, F32), (S5_W, BF16)])

    wa_f, wa_b = _alpha_pads(w['gla_w_alpha'])
    ba_f, ba_b = w['gla_b_alpha'][0:1], w['gla_b_alpha'][1:2]
    zz = _cw(z, ZC_Z, LANE)

    def gla_prep(zv, waf, wab, bf, bb):
        z16 = zv.astype(BF16)
        lf = _dot(z16, waf.astype(BF16)) + bf
        lb = _dot(z16, wab.astype(BF16)) + bb
        return _log_sigmoid(lf) / GLA_TAU, _log_sigmoid(lb) / GLA_TAU

    la_f, la_b = _rowwise("gla_prep", gla_prep, [zz], [wa_f, wa_b, ba_f, ba_b], [(GLA_W, F32), (GLA_W, F32)])
    o_f, s_f = _gla_fwd("gla_fwd", z, la_f, rev=False)
    o_b, s_b = _gla_fwd("gla_rev", z, la_b, rev=True)
    gate = _cw(z, 4, GLA_W)

    def gla_post(of, ob, gv, gn):
        o, gv = of + ob, gv.astype(F32)
        outs = []
        for hh in range(GLA_H):
            sl = slice(hh * GLA_D, (hh + 1) * GLA_D)
            gh = gv[:, sl]
            outs.append(_rms(o[:, sl], gn) * (gh * _sigmoid(gh)))
        return jnp.concatenate(outs, axis=1)

    y_gla = _rowwise("gla_post", gla_post, [o_f, o_b, gate], [w['gla_norm']], [(GLA_W, BF16)])[0]

    cq, sq, mq, rq = _head_consts(n_tok, AT_QH)
    ck, sk, mk, rk = _head_consts(n_tok, AT_KH)
    gq = jnp.tile(w['attn_q_norm'], (1, AT_QH))
    gk = jnp.tile(w['attn_k_norm'], (1, AT_KH))

    def rope_fwd(v, gain, cos, sin, mean, rot, scale):
        r = lax.rsqrt(_dot(v * v, mean, precision=HI) + EPS)
        vn = v * r * gain
        return (vn * cos + _dot(vn, rot, precision=HI) * sin) * scale

    def attn_prep(qv, kv, vv, cqv, sqv, ckv, skv, gqv, gkv, mqv, rqv, mkv, rkv):
        qv, kv, vv = qv.astype(F32), kv.astype(F32), vv.astype(F32)
        kr = rope_fwd(kv, gkv, ckv, skv, mkv, rkv, 1.0)
        return rope_fwd(qv, gqv, cqv, sqv, mqv, rqv, AT_D ** -0.5), _dup_heads(kr), _dup_heads(vv)

    q_raw, k_raw, v_raw = _cw(z, 5, AT_W), _cw(z, ZC_AK, AT_KW), _cw(z, ZC_AV, AT_KW)
    q16, kd, vd = _rowwise("attn_prep", attn_prep, [q_raw, k_raw, v_raw, cq, sq, ck, sk], [gq, gk, mq, rq, mk, rk],
                           [(AT_W, BF16), (2 * LANE, BF16), (2 * LANE, BF16)])
    y_at, lse = _attn_fwd("attn_fwd", q16, kd, vd)

    p0 = _mm("mix_branch", y_s5, w['w_branch_s5'], out_dtype=BF16)
    p1 = _mm("mix_branch", y_gla, w['w_branch_gla'], out_dtype=BF16)
    p2 = _mm("mix_branch_f32", y_at, w['w_branch_attn'], out_dtype=BF16)

    def merge(gpv, a0, a1, a2, bias):
        g = _sigmoid(gpv.astype(F32) + bias)
        return (g[:, :D_MODEL] * a0.astype(F32) + g[:, D_MODEL:2 * D_MODEL] * a1.astype(F32)
                + g[:, 2 * D_MODEL:] * a2.astype(F32))

    merged = _rowwise("mix_merge", merge, [gp, p0, p1, p2], [w['b_merge_gate']], [(D_MODEL, BF16)])[0]
    xo = _mm("mix_out", merged, w['w_out'], add=x)
    saved = dict(x=x, h=h, z=z, gp=gp, xs=xs, y0=y0, t_glu=t_glu, y_s5=y_s5, la=(la_f, la_b), o=(o_f, o_b), s=(s_f, s_b),
                 y_gla=y_gla, q16=q16, kd=kd, vd=vd, lse=lse, y_at=y_at, p=(p0, p1, p2), merged=merged)
    return xo, saved


def _mixer_bwd(dy, dy16, sv, w, s5_raw):
    z, h = sv['z'], sv['h']
    n_tok = z.shape[0]
    grads = {}
    dmerged = _mm("mix_dmerged", dy16, w['w_out'], tb=True, out_dtype=BF16)
    grads['w_out'] = _mm("mix_dwout", sv['merged'], dy16, ta=True)

    def dmerge(dm, gpv, a0, a1, a2, bias):
        dm = dm.astype(F32)
        g = _sigmoid(gpv.astype(F32) + bias)
        ps = (a0.astype(F32), a1.astype(F32), a2.astype(F32))
        dps, dgs = [], []
        for k in range(3):
            gk = g[:, k * D_MODEL:(k + 1) * D_MODEL]
            dps.append(dm * gk)
            dgs.append(dm * ps[k] * gk * (1.0 - gk))
        dgp = jnp.concatenate(dgs, axis=1)
        return dps[0], dps[1], dps[2], dgp, jnp.sum(dgp, axis=0, keepdims=True)

    dp0, dp1, dp2, dgp, dbm = _rowwise("mix_dmerge", dmerge, [dmerged, sv['gp'], *sv['p']], [w['b_merge_gate']],
                                       [(D_MODEL, BF16)] * 3 + [(3 * D_MODEL, BF16)], accs=[(1, 3 * D_MODEL)])
    grads['b_merge_gate'] = dbm[0]
    grads['w_branch_s5'] = _mm("mix_dbranch", sv['y_s5'], dp0, ta=True)
    grads['w_branch_gla'] = _mm("mix_dbranch", sv['y_gla'], dp1, ta=True)
    grads['w_branch_attn'] = _mm("mix_dbranch_f32", sv['y_at'], dp2, ta=True)
    dy_s5 = _mm("mix_dy", dp0, w['w_branch_s5'], tb=True)
    dy_gla = _mm("mix_dy", dp1, w['w_branch_gla'], tb=True)
    dy_at = _mm("mix_dy", dp2, w['w_branch_attn'], tb=True)
    grads['w_merge_gate'] = _mm("mix_dwgate", h, dgp, ta=True, out_dtype=GRAD_DT)
    dh = _mm("mix_dh_gate", dgp, w['w_merge_gate'], tb=True)
    u = _cw(z, 0, S5_W)

    def s5_post_bwd(dyv, y0, t, uv, dv, wglu):
        sg = _sigmoid(t)
        y1 = _gelu(y0)
        dt = dyv * y1 * sg * (1.0 - sg)
        dy1 = dyv * sg + _dot(dt.astype(BF16), wglu, _NT)
        dy0 = dy1 * _gelu_grad(y0)
        return dy0, dt, y1, dy0 * dv, jnp.sum(dy0 * uv, axis=0, keepdims=True)

    dy0, dt_glu, y1, du, dd = _rowwise("s5_post_bwd", s5_post_bwd, [dy_s5, sv['y0'], sv['t_glu'], u],
                                       [w['s5_d'], w['s5_w_glu']],
                                       [(S5_W, BF16), (S5_W, BF16), (S5_W, BF16), (S5_W, F32)], accs=[(1, S5_W)])
    grads['s5_d'] = dd[0]
    grads['s5_w_glu'] = _mm("s5_dwglu", y1, dt_glu, ta=True)
    s5_cot = []
    for d, rev in ((0, False), (1, True)):
        a_re, a_im, wb, wc = w['s5'][d]
        du, dwb, dwc, da = _s5_bwd("s5_adj_fwd" if rev else "s5_adj_rev", dy0, z, sv['xs'][d], a_re, -a_im, wb, wc, du,
                                   rev=not rev)
        da = jnp.sum(da, axis=0, keepdims=True)
        s5_cot.append((da[:, :S5_N], da[:, S5_N:], dwb, dwc))
    for d in range(2):
        _, pull = jax.vjp(_s5_prep, *s5_raw[d])
        s5_cot[d] = pull(s5_cot[d])
    for k, nm in enumerate(('s5_lambda_re', 's5_lambda_im', 's5_log_dt', 's5_b_re', 's5_b_im', 's5_c_re', 's5_c_im')):
        grads[nm] = jnp.stack([s5_cot[0][k], s5_cot[1][k]])

    gate = _cw(z, 4, GLA_W)

    def gla_post_bwd(dyv, of, ob, gv, gn):
        o, gv = of + ob, gv.astype(F32)
        dos, dgates = [], []
        dgn = jnp.zeros((1, GLA_D), F32)
        for hh in range(GLA_H):
            sl = slice(hh * GLA_D, (hh + 1) * GLA_D)
            gh, dyh = gv[:, sl], dyv[:, sl]
            sg = _sigmoid(gh)
            dgates.append(dyh * _rms(o[:, sl], gn) * sg * (1.0 + gh * (1.0 - sg)))
            dx, dg = _rms_bwd(o[:, sl], gn, dyh * gh * sg)
            dos.append(dx)
            dgn = dgn + dg
        return jnp.concatenate(dos, axis=1), jnp.concatenate(dgates, axis=1), dgn

    do, dgate, dgn = _rowwise("gla_post_bwd", gla_post_bwd, [dy_gla, *sv['o'], gate], [w['gla_norm']],
                              [(GLA_W, F32), (GLA_W, F32)], accs=[(1, GLA_D)])
    grads['gla_norm'] = dgn[0]
    dq_f, dk_f, dv_f, dla_f = _gla_bwd("gla_bwd_fwd", z, sv['la'][0], sv['s'][0], do, rev=False)
    dq_b, dk_b, dv_b, dla_b = _gla_bwd("gla_bwd_rev", z, sv['la'][1], sv['s'][1], do, rev=True)
    wa_f, wa_b = _alpha_pads(w['gla_w_alpha'])
    ba_f, ba_b = w['gla_b_alpha'][0:1], w['gla_b_alpha'][1:2]
    zz = _cw(z, ZC_Z, LANE)

    def gla_prep_bwd(dlf, dlb, zv, waf, wab, bf, bb):
        z16 = zv.astype(BF16)
        waf16, wab16 = waf.astype(BF16), wab.astype(BF16)
        xf = _dot(z16, waf16) + bf
        xb = _dot(z16, wab16) + bb
        df = dlf * (_sigmoid(-xf) / GLA_TAU)
        db = dlb * (_sigmoid(-xb) / GLA_TAU)
        dz = _dot(df.astype(BF16), waf16, _NT) + _dot(db.astype(BF16), wab16, _NT)
        return df, db, dz, jnp.sum(df, axis=0, keepdims=True), jnp.sum(db, axis=0, keepdims=True)

    dlog_f, dlog_b, dzz, dba_f, dba_b = _rowwise(
        "gla_prep_bwd", gla_prep_bwd, [dla_f, dla_b, zz], [wa_f, wa_b, ba_f, ba_b],
        [(GLA_W, BF16), (GLA_W, BF16), (LANE, F32)], accs=[(1, GLA_W), (1, GLA_W)])
    dwa_f = _mm("gla_dwa", zz, dlog_f, ta=True)
    dwa_b = _mm("gla_dwa", zz, dlog_b, ta=True)
    grads['gla_w_alpha'] = jnp.stack([dwa_f[:GLA_R], dwa_b[GLA_R:2 * GLA_R]])
    grads['gla_b_alpha'] = jnp.concatenate([dba_f, dba_b], axis=0)

    dq_r, dkd, dvd = _attn_bwd("attn_bwd", sv['q16'], sv['kd'], sv['vd'], sv['y_at'], sv['lse'], dy_at)
    cq, sq, mq, rq = _head_consts(n_tok, AT_QH)
    ck, sk, mk, rk = _head_consts(n_tok, AT_KH)
    gq = jnp.tile(w['attn_q_norm'], (1, AT_QH))
    gk = jnp.tile(w['attn_k_norm'], (1, AT_KH))

    def rope_bwd(dr, v, gain, cos, sin, mean, rot, scale):
        dr = dr * scale
        dvn = dr * cos + _dot(dr * sin, rot, _NT, precision=HI)
        r = lax.rsqrt(_dot(v * v, mean, precision=HI) + EPS)
        vh = v * r
        dg = jnp.sum(dvn * vh, axis=0, keepdims=True)
        dvh = dvn * gain
        return r * (dvh - vh * _dot(dvh * vh, mean, precision=HI)), dg

    def attn_prep_bwd(dq, dk2, dv2, qv, kv, cqv, sqv, ckv, skv, gqv, gkv, mqv, rqv, mkv, rkv):
        qv, kv = qv.astype(F32), kv.astype(F32)
        dqo, dgq = rope_bwd(dq, qv, gqv, cqv, sqv, mqv, rqv, AT_D ** -0.5)
        dko, dgk = rope_bwd(_fold_heads(dk2), kv, gkv, ckv, skv, mkv, rkv, 1.0)
        return dqo, dko, _fold_heads(dv2), dgq, dgk

    q_raw, k_raw = _cw(z, 5, AT_W), _cw(z, ZC_AK, AT_KW)
    daq, dak, dav, dgq, dgk = _rowwise("attn_prep_bwd", attn_prep_bwd, [dq_r, dkd, dvd, q_raw, k_raw, cq, sq, ck, sk],
                                       [gq, gk, mq, rq, mk, rk], [(AT_W, F32), (AT_KW, F32), (AT_KW, F32)],
                                       accs=[(1, AT_W), (1, AT_KW)])
    grads['attn_q_norm'] = dgq.reshape(AT_QH, AT_D).sum(axis=0)
    grads['attn_k_norm'] = dgk.reshape(AT_KH, AT_D).sum(axis=0)

    def assemble(duv, qf, qb, kf, kb, vf, vb, dg, aq, ak, av, zzv):
        return jnp.concatenate([duv, qf + qb, kf + kb, vf + vb, dg, aq, ak, av, zzv], axis=1)

    dz = _rowwise("mix_dz", assemble, [du, dq_f, dq_b, dk_f, dk_b, dv_f, dv_b, dgate, daq, dak, dav, dzz], [],
                  [(ZP_W, BF16)])[0]
    grads['w_in'] = _unpad_w_in(_mm("mix_dwin", h, dz, ta=True, out_dtype=GRAD_DT))
    dh = _mm("mix_dh_in", dz, w['w_in_p'], tb=True, add=dh)

    def dnorm(dyv, xv, dhv, gv):
        dx, dg = _rms_bwd(xv, gv, dhv)
        dx = dyv + dx
        return dx, dx, dg

    dx, dx16, dg = _rowwise("mix_dnorm", dnorm, [dy, sv['x'], dh], [w['mix_norm']], [(D_MODEL, F32), (D_MODEL, BF16)],
                            accs=[(1, D_MODEL)])
    grads['mix_norm'] = dg[0]
    return dx, dx16, grads


FFN_NAMES = [p + k for p in ('ffn1_', 'ffn2_') for k in ('w_gate', 'w_up', 'w_down')]
MISC_SHARDED = [n for n in SHARDED if n not in FFN_NAMES]
LARGE_SHARDED = FFN_NAMES + ['w_in', 'w_merge_gate']
KEPT_TRANSPOSED = [p + k for p in ('ffn1_', 'ffn2_') for k in ('w_gate', 'w_up')]


def _stored(n, t):
    return jnp.swapaxes(t, 1, 2) if n in KEPT_TRANSPOSED else t


def _layer_weights(ws, rep, i):
    w = {k: _from_shards(ws[k][:, i], SHARD_AXIS[k] - 1) for k in MISC_SHARDED}
    for k in ('ffn1_norm', 'mix_norm', 'ffn2_norm', 's5_d', 'gla_norm', 'attn_q_norm', 'attn_k_norm', 'b_merge_gate'):
        w[k] = rep[k][i].reshape(1, -1)
    w['w_in_p'] = _pad_w_in(w['w_in'])
    s5_raw = [tuple(rep[k][i, d] for k in ('s5_lambda_re', 's5_lambda_im', 's5_log_dt', 's5_b_re', 's5_b_im',
                                            's5_c_re', 's5_c_im')) for d in range(2)]
    w['s5'] = [_s5_prep(*s5_raw[d]) for d in range(2)]
    return w, s5_raw


def _device_step(x, target, ws, rep):
    lw = [_layer_weights(ws, rep, i) for i in range(DEPTH)]
    ffn = {p: (ws[p + 'w_gate'], ws[p + 'w_up'], ws[p + 'w_down']) for p in ('ffn1_', 'ffn2_')}
    saved = []
    for i in range(DEPTH):
        w, _ = lw[i]
        x, s1 = _ffn_fwd(x, w['ffn1_norm'], *ffn['ffn1_'], i)
        x, s2 = _mixer_fwd(x, w)
        x, s3 = _ffn_fwd(x, w['ffn2_norm'], *ffn['ffn2_'], i)
        saved.append((s1, s2, s3))
    gfin = rep['final_norm'].reshape(1, -1)

    def head(xv, tv, gv):
        e = _rms(xv, gv) - tv
        dx, dg = _rms_bwd(xv, gv, e * (1.0 / D_MODEL))
        part = jnp.sum(e * e, axis=0, keepdims=True)
        return dx, dx, dg, part

    dx, dx16, dgfin, part = _rowwise("loss_head", head, [x, target], [gfin], [(D_MODEL, F32), (D_MODEL, BF16)],
                                     accs=[(1, D_MODEL), (1, D_MODEL)])
    loss = (0.5 / D_MODEL) * jnp.sum(part)
    per_layer = []
    dffn = {'ffn1_': None, 'ffn2_': None}
    for i in reversed(range(DEPTH)):
        w, s5_raw = lw[i]
        s1, s2, s3 = saved[i]
        dx, dx16, n3, dffn['ffn2_'] = _ffn_bwd(dx, dx16, s3, w['ffn2_norm'], *ffn['ffn2_'], i, dffn['ffn2_'])
        dx, dx16, g = _mixer_bwd(dx, dx16, s2, w, s5_raw)
        dx, dx16, n1, dffn['ffn1_'] = _ffn_bwd(dx, dx16, s1, w['ffn1_norm'], *ffn['ffn1_'], i, dffn['ffn1_'])
        g['ffn2_norm'], g['ffn1_norm'] = n3, n1
        per_layer.append(g)
    per_layer.reverse()
    stacked = {k: jnp.stack([per_layer[i][k] for i in range(DEPTH)]) for k in per_layer[0]}
    gs = {k: _to_shards(stacked[k], SHARD_AXIS[k]) for k in MISC_SHARDED}
    for p in ('ffn1_', 'ffn2_'):
        gs[p + 'w_gate'], gs[p + 'w_up'], gs[p + 'w_down'] = dffn[p]
    grep = {k: stacked[k] for k in REPLICATED if k != 'final_norm'}
    grep['final_norm'] = dgfin[0]
    return loss, dx, gs, grep


PACK_C = 1024
ROW_TILE = 512


def _rows_of(shape):
    return -(-int(np.prod(shape)) // PACK_C)


def _as_rows(t, lead=0):
    head = t.shape[:lead]
    flat = t.reshape(head + (-1,))
    rows = -(-flat.shape[-1] // PACK_C)
    pad = rows * PACK_C - flat.shape[-1]
    if pad:
        flat = jnp.pad(flat, [(0, 0)] * lead + [(0, pad)])
    return flat.reshape(head + (rows, PACK_C))


def _pack(items, total_rows, lead=0):
    parts = [_as_rows(t, lead) for t in items]
    used = sum(p.shape[lead] for p in parts)
    if total_rows > used:
        parts.append(jnp.zeros(parts[0].shape[:lead] + (total_rows - used, PACK_C), parts[0].dtype))
    return jnp.concatenate(parts, axis=lead)


def _unpack(rows, shapes, lead=0):
    out, r0 = [], 0
    head = rows.shape[:lead]
    for shp in shapes:
        n, size = _rows_of(shp), int(np.prod(shp))
        piece = lax.slice_in_dim(rows, r0, r0 + n, axis=lead).reshape(head + (n * PACK_C,))
        out.append(lax.slice_in_dim(piece, 0, size, axis=lead).reshape(head + tuple(shp)))
        r0 += n
    return out


def _round_up(n, m):
    return -(-n // m) * m


def _to_shards(g, axis):
    shp = g.shape
    g = g.reshape(shp[:axis] + (N_CHIPS, shp[axis] // N_CHIPS) + shp[axis + 1:])
    return jnp.moveaxis(g, axis, 0)


def _from_shards(s, axis):
    s = jnp.moveaxis(s, 0, axis)
    shp = s.shape
    return s.reshape(shp[:axis] + (shp[axis] * shp[axis + 1],) + shp[axis + 2:])


_HBM = pl.BlockSpec(memory_space=pltpu.HBM)


def _place():
    x, y, c = lax.axis_index("x"), lax.axis_index("y"), lax.axis_index("c")
    chips = [(1 - x, y), (x, 1 - y), (1 - x, 1 - y)]
    return x, y, c, chips


def _rcopy(src, dst, send, recv, k, dev):
    return pltpu.make_async_remote_copy(src_ref=src, dst_ref=dst, send_sem=send.at[k], recv_sem=recv.at[k],
                                        device_id=dev, device_id_type=MESH)


def _comm_call(name, body, arrays, out_shapes, n_sems, n_local):
    return pl.pallas_call(
        body, name=name, in_specs=[_HBM] * len(arrays), out_specs=[_HBM] * len(out_shapes), out_shape=out_shapes,
        scratch_shapes=[pltpu.SemaphoreType.DMA((n_sems,)), pltpu.SemaphoreType.DMA((n_sems,)),
                        pltpu.SemaphoreType.DMA((max(n_local, 1),))],
    )(*arrays)


def _ici_peers(x, y, c):
    x_first = c == 0
    first = (jnp.where(x_first, 1 - x, x), jnp.where(x_first, y, 1 - y))
    second = (jnp.where(x_first, x, 1 - x), jnp.where(x_first, 1 - y, y))
    return first, second


def _ag_chips(name, packs):
    n = len(packs)

    def body(*refs):
        p, o, (send, recv, _) = refs[:n], refs[n:2 * n], refs[2 * n:]
        x, y, c, _ = _place()
        (ax, ay), (bx, by) = _ici_peers(x, y, c)
        me, na, nb, nd = 2 * x + y, 2 * ax + ay, 2 * bx + by, 2 * (1 - x) + (1 - y)
        sib = (x, y, 1 - c)
        own = [_rcopy(p[a].at[c], o[a].at[me, c], send, recv, 6 * a, (ax, ay, c)) for a in range(n)]
        own += [_rcopy(p[a].at[c], o[a].at[me, c], send, recv, 6 * a + 1, (bx, by, c)) for a in range(n)]
        for cp in own:
            cp.start()
        passed = []

        def landed(a, chip, k, fwd):
            blk = o[a].at[chip, c]
            _rcopy(blk, blk, send, recv, 6 * a + k, sib).wait_recv()
            for sem, dev in fwd:
                cp = _rcopy(blk, blk, send, recv, 6 * a + sem, dev)
                cp.start()
                passed.append(cp)

        for a in range(n):
            landed(a, na, 0, [(2, (bx, by, c)), (3, sib)])
        for a in range(n):
            landed(a, nb, 1, [(4, sib)])
        for a in range(n):
            landed(a, nd, 2, [(5, sib)])
        for a in range(n):
            for k, chip in ((3, nb), (4, na), (5, nd)):
                blk = o[a].at[chip, 1 - c]
                _rcopy(blk, blk, send, recv, 6 * a + k, sib).wait_recv()
        for cp in own + passed:
            cp.wait_send()

    outs = _comm_call(name, body, packs, [jax.ShapeDtypeStruct((N_CHIPS,) + t.shape, t.dtype) for t in packs], 6 * n, 0)
    chip = 2 * lax.axis_index("x") + lax.axis_index("y")
    return [lax.dynamic_update_slice_in_dim(o, t[None], chip, axis=0) for o, t in zip(outs, packs)]


def _pair_swap(name, gs):
    n = len(gs)
    ns = gs[0].shape[0]

    def body(*refs):
        g, r, (send, recv, _) = refs[:n], refs[n:2 * n], refs[2 * n:]
        x, y, c, _ = _place()
        cps = [_rcopy(g[a].at[j, 1 - c], r[a].at[j], send, recv, ns * a + j, (x, y, 1 - c))
               for a in range(n) for j in range(ns)]
        for cp in cps:
            cp.start()
        for cp in cps:
            cp.wait_recv()
        for cp in cps:
            cp.wait_send()

    return _comm_call(name, body, gs, [jax.ShapeDtypeStruct((ns,) + t.shape[2:], t.dtype) for t in gs], ns * n, 0)


def _halve_first(name, parts):
    n = len(parts)

    def body(*refs):
        p, r, (send, recv, _) = refs[:n], refs[n:2 * n], refs[2 * n:]
        x, y, c, _ = _place()
        (ax, ay), _ = _ici_peers(x, y, c)
        cps = [_rcopy(p[a].at[jnp.where(c == 0, 2 * ax + k, 2 * k + ay)], r[a].at[k], send, recv, 2 * a + k, (ax, ay, c))
               for a in range(n) for k in range(2)]
        for cp in cps:
            cp.start()
        for cp in cps:
            cp.wait_recv()
        for cp in cps:
            cp.wait_send()

    return _comm_call(name, body, parts, [jax.ShapeDtypeStruct((2,) + t.shape[1:], t.dtype) for t in parts], 2 * n, 0)


def _halve_second(name, kept):
    n = len(kept)

    def body(*refs):
        p, r, (send, recv, _) = refs[:n], refs[n:2 * n], refs[2 * n:]
        x, y, c, _ = _place()
        _, (bx, by) = _ici_peers(x, y, c)
        cps = [_rcopy(p[a].at[jnp.where(c == 0, by, bx)], r[a], send, recv, a, (bx, by, c)) for a in range(n)]
        for cp in cps:
            cp.start()
        for cp in cps:
            cp.wait_recv()
        for cp in cps:
            cp.wait_send()

    return _comm_call(name, body, kept, [jax.ShapeDtypeStruct(t.shape[1:], t.dtype) for t in kept], n, 0)


def _add_selected(name, t, r, index, out_dtype):
    single = r.ndim == 2
    r3 = r[None] if single else r
    nk, rh, cc = r3.shape
    tl = _tile(rh, ROW_TILE, 16)

    def body(i_ref, t_ref, r_ref, o_ref):
        o_ref[...] = (t_ref[...].astype(F32) + r_ref[...].astype(F32)).astype(o_ref.dtype)

    out = _pcall(
        body, name=name,
        grid_spec=pltpu.PrefetchScalarGridSpec(
            num_scalar_prefetch=1, grid=(nk, rh // tl),
            in_specs=[pl.BlockSpec((1, tl, cc), lambda k, i, i_ref: (i_ref[k], i, 0)),
                      pl.BlockSpec((1, tl, cc), lambda k, i, i_ref: (k, i, 0))],
            out_specs=pl.BlockSpec((1, tl, cc), lambda k, i, i_ref: (k, i, 0))),
        out_shape=jax.ShapeDtypeStruct(r3.shape, out_dtype),
        compiler_params=_cparams(("parallel", "parallel"), 8 * tl * cc * 4),
    )(index, t, r3)
    return out[0] if single else out


def _pair_share(name, reds):
    n = len(reds)

    def body(*refs):
        r, o, (send, recv, _) = refs[:n], refs[n:2 * n], refs[2 * n:]
        x, y, c, _ = _place()
        cps = [_rcopy(r[a], o[a].at[c], send, recv, a, (x, y, 1 - c)) for a in range(n)]
        for cp in cps:
            cp.start()
        for a in range(n):
            blk = o[a].at[1 - c]
            _rcopy(blk, blk, send, recv, a, (x, y, 1 - c)).wait_recv()
        for cp in cps:
            cp.wait_send()

    outs = _comm_call(name, body, reds, [jax.ShapeDtypeStruct((2,) + t.shape, t.dtype) for t in reds], n, 0)
    core = lax.axis_index("c")
    return [lax.dynamic_update_slice_in_dim(o, t[None], core, axis=0) for o, t in zip(outs, reds)]


def _add_own_half(name, g, r):
    ns, _, rh, cc = g.shape
    tl = _tile(rh, ROW_TILE, 16)
    core = lax.axis_index("c").astype(jnp.int32).reshape(1)

    def body(c_ref, g_ref, r_ref, o_ref):
        o_ref[...] = (g_ref[0].astype(F32) + r_ref[...].astype(F32)).astype(o_ref.dtype)

    return _pcall(
        body, name=name,
        grid_spec=pltpu.PrefetchScalarGridSpec(
            num_scalar_prefetch=1, grid=(ns, rh // tl),
            in_specs=[pl.BlockSpec((1, 1, tl, cc), lambda j, i, c_ref: (j, c_ref[0], i, 0)),
                      pl.BlockSpec((1, tl, cc), lambda j, i, c_ref: (j, i, 0))],
            out_specs=pl.BlockSpec((1, tl, cc), lambda j, i, c_ref: (j, i, 0))),
        out_shape=jax.ShapeDtypeStruct(r.shape, BF16),
        compiler_params=_cparams(("parallel", "parallel"), 8 * tl * cc * 4),
    )(core, g, r)


def _reduce_scatter(gs):
    x, y, c = lax.axis_index("x"), lax.axis_index("y"), lax.axis_index("c")
    mine_first = jnp.stack([jnp.where(c == 0, 2 * x + k, 2 * k + y) for k in range(2)]).astype(jnp.int32)
    mine_second = jnp.where(c == 0, y, x).astype(jnp.int32).reshape(1)
    swapped = _pair_swap("rs_pair_swap", gs)
    parts = [_add_own_half("rs_add_pair", g, r) for g, r in zip(gs, swapped)]
    got = _halve_first("rs_halve_first", parts)
    kept = [_add_selected("rs_add_first", t, r, mine_first, BF16) for t, r in zip(parts, got)]
    got = _halve_second("rs_halve_second", kept)
    reds = [_add_selected("rs_add_second", t, r, mine_second, F32) for t, r in zip(kept, got)]
    return _pair_share("rs_pair_share", reds)


def _adamw(name, g, w, m, v):
    c1 = 1.0 - ADAM_B1 ** ADAM_STEP
    c2 = 1.0 - ADAM_B2 ** ADAM_STEP
    nb, rows, cc = g.shape
    tl = _tile(rows, ROW_TILE, SUBLANE)

    def body(g_ref, w_ref, m_ref, v_ref, d_ref, mo_ref, vo_ref):
        gv = g_ref[...]
        mn = ADAM_B1 * m_ref[...] + (1.0 - ADAM_B1) * gv
        vn = ADAM_B2 * v_ref[...] + (1.0 - ADAM_B2) * (gv * gv)
        d_ref[...] = -ADAM_LR * ((mn / c1) / (jnp.sqrt(vn / c2) + ADAM_EPS) + ADAM_WD * w_ref[...])
        mo_ref[...] = mn
        vo_ref[...] = vn

    spec = pl.BlockSpec((1, tl, cc), lambda b, i: (b, i, 0))
    return _pcall(
        body, name=name, grid=(nb, rows // tl), in_specs=[spec] * 4, out_specs=[spec] * 3,
        out_shape=[jax.ShapeDtypeStruct(g.shape, F32)] * 3,
        compiler_params=_cparams(("parallel", "parallel"), 16 * tl * cc * 4),
    )(g, w, m, v)


def _train_step(a):
    x, target = a['x'][0], a['loss_target'][0]

    names = BF16_GATHER + F32_GATHER
    got = _ag_chips("ag_weights", [_stored(n, a[n]).astype(BF16) for n in BF16_GATHER] + [a[n] for n in F32_GATHER])
    ws = dict(zip(names, got))

    loss, gx, gs, grep = _device_step(x, target, ws, {n: a[n] for n in REPLICATED})
    loss = lax.psum(loss, ("x", "y", "c"))

    small = [n for n in MISC_SHARDED if n not in LARGE_SHARDED]
    small_shapes = [a[n].shape[1:] for n in small]
    rep_shapes = [a[n].shape for n in REPLICATED]
    small_rows = sum(_rows_of(s) for s in small_shapes)
    piece = _round_up(-(-sum(_rows_of(s) for s in rep_shapes) // (2 * N_CHIPS)), SUBLANE)
    rep_rows = 2 * N_CHIPS * piece
    rep_pack = _pack([grep[n] for n in REPLICATED], rep_rows).reshape(N_CHIPS, 2, piece, PACK_C)
    total = _round_up(small_rows + piece, ROW_TILE // 2)
    misc = _pack([gs[n] for n in small] + [rep_pack], total, lead=2)
    reds = _reduce_scatter([gs[n] for n in LARGE_SHARDED] + [misc])
    grad = dict(zip(LARGE_SHARDED, reds[:-1]))
    grad.update(zip(small, _unpack(reds[-1], small_shapes, lead=1)))
    rep_mine = reds[-1][:, small_rows:small_rows + piece]
    rep_all = _ag_chips("ag_replicated", [rep_mine])[0].reshape(1, rep_rows, PACK_C)

    outs = {}
    for n in SHARDED:
        shp = grad[n].shape
        three_d = (shp[0], -1, shp[-1])
        params = (grad[n], *(_stored(n, a[p + n]) for p in ('', 'm_', 'v_')))
        res3 = _adamw("adamw", *(t.reshape(three_d) for t in params))
        for kind, arr in zip(('grad', 'delta', 'new_m', 'new_v'), (grad[n], *res3)):
            outs[kind + '_' + n] = _stored(n, arr.reshape(shp))
    w, m, v = (_pack([a[p + n] for n in REPLICATED], rep_rows)[None] for p in ('', 'm_', 'v_'))
    res3 = _adamw("adamw_replicated", rep_all, w, m, v)
    for kind, arr in zip(('grad', 'delta', 'new_m', 'new_v'), (rep_all, *res3)):
        for n, t in zip(REPLICATED, _unpack(arr[0], rep_shapes)):
            outs[kind + '_' + n] = t
    res = [loss, gx[None]]
    for kind in ('grad', 'delta', 'new_m', 'new_v'):
        res += [outs[kind + '_' + n] for n in W_NAMES]
    return tuple(res)


def kernel(x, ffn1_norm, ffn1_w_gate, ffn1_w_up, ffn1_w_down, mix_norm, w_in, s5_lambda_re, s5_lambda_im, s5_log_dt, s5_b_re, s5_b_im, s5_c_re, s5_c_im, s5_d, s5_w_glu, gla_w_alpha, gla_b_alpha, gla_norm, attn_q_norm, attn_k_norm, w_branch_s5, w_branch_gla, w_branch_attn, w_merge_gate, b_merge_gate, w_out, ffn2_norm, ffn2_w_gate, ffn2_w_up, ffn2_w_down, final_norm, loss_target, m_ffn1_norm, m_ffn1_w_gate, m_ffn1_w_up, m_ffn1_w_down, m_mix_norm, m_w_in, m_s5_lambda_re, m_s5_lambda_im, m_s5_log_dt, m_s5_b_re, m_s5_b_im, m_s5_c_re, m_s5_c_im, m_s5_d, m_s5_w_glu, m_gla_w_alpha, m_gla_b_alpha, m_gla_norm, m_attn_q_norm, m_attn_k_norm, m_w_branch_s5, m_w_branch_gla, m_w_branch_attn, m_w_merge_gate, m_b_merge_gate, m_w_out, m_ffn2_norm, m_ffn2_w_gate, m_ffn2_w_up, m_ffn2_w_down, m_final_norm, v_ffn1_norm, v_ffn1_w_gate, v_ffn1_w_up, v_ffn1_w_down, v_mix_norm, v_w_in, v_s5_lambda_re, v_s5_lambda_im, v_s5_log_dt, v_s5_b_re, v_s5_b_im, v_s5_c_re, v_s5_c_im, v_s5_d, v_s5_w_glu, v_gla_w_alpha, v_gla_b_alpha, v_gla_norm, v_attn_q_norm, v_attn_k_norm, v_w_branch_s5, v_w_branch_gla, v_w_branch_attn, v_w_merge_gate, v_b_merge_gate, v_w_out, v_ffn2_norm, v_ffn2_w_gate, v_ffn2_w_up, v_ffn2_w_down, v_final_norm):
    return _train_step(dict(locals()))
```

```python
import functools
import math

import jax
import jax.numpy as jnp
import numpy as np
from jax import lax
from jax.experimental import pallas as pl
from jax.experimental.pallas import tpu as pltpu

F32 = jnp.float32
BF16 = jnp.bfloat16
HI = lax.Precision.HIGHEST

D_MODEL = 1024
D_FF = 2816
DEPTH = 2
EPS = 1e-6
S5_G, S5_H, S5_P = 32, 16, 64
S5_W = S5_G * S5_H
S5_N = S5_G * S5_P
S5_TG = 8
S5_TILES = S5_G // S5_TG
S5_TC, S5_TS = S5_TG * S5_H, S5_TG * S5_P
GLA_H, GLA_D = 4, 128
GLA_W = GLA_H * GLA_D
GLA_R = 16
GLA_TAU = 16.0
GLA_C = 64
AT_QH, AT_KH, AT_D = 8, 2, 64
AT_W = AT_QH * AT_D
AT_KW = AT_KH * AT_D
GRID_W = 64
ROPE_BASE = 10000.0
ZP_W = 3456
ZC_AK, ZC_AV, ZC_Z = 24, 25, 26

ADAM_LR, ADAM_B1, ADAM_B2, ADAM_EPS, ADAM_WD, ADAM_STEP = 0.001, 0.9, 0.999, 1e-08, 0.01, 10

GRAD_DT = BF16
VMEM_LIMIT_V7X = 56 * 1024 * 1024
LANE = 128
SUBLANE = 8

W_NAMES = ['ffn1_norm', 'ffn1_w_gate', 'ffn1_w_up', 'ffn1_w_down', 'mix_norm', 'w_in', 's5_lambda_re', 's5_lambda_im',
           's5_log_dt', 's5_b_re', 's5_b_im', 's5_c_re', 's5_c_im', 's5_d', 's5_w_glu', 'gla_w_alpha', 'gla_b_alpha',
           'gla_norm', 'attn_q_norm', 'attn_k_norm', 'w_branch_s5', 'w_branch_gla', 'w_branch_attn', 'w_merge_gate',
           'b_merge_gate', 'w_out', 'ffn2_norm', 'ffn2_w_gate', 'ffn2_w_up', 'ffn2_w_down', 'final_norm']
SHARD_AXIS = {'ffn1_w_gate': 2, 'ffn1_w_up': 2, 'ffn1_w_down': 1, 'w_in': 2, 's5_w_glu': 1, 'gla_w_alpha': 3,
              'gla_b_alpha': 2, 'w_branch_s5': 2, 'w_branch_gla': 2, 'w_branch_attn': 2, 'w_merge_gate': 2,
              'w_out': 1, 'ffn2_w_gate': 2, 'ffn2_w_up': 2, 'ffn2_w_down': 1}
F32_GATHER = ['gla_w_alpha', 'gla_b_alpha']
BF16_GATHER = [n for n in W_NAMES if n in SHARD_AXIS and n not in F32_GATHER]
SHARDED = [n for n in W_NAMES if n in SHARD_AXIS]
REPLICATED = [n for n in W_NAMES if n not in SHARD_AXIS]
N_CHIPS = 4
MESH = pl.DeviceIdType.MESH


def _tile(n, target, mult):
    best = None
    for t in range(mult, min(n, target) + 1, mult):
        if n % t == 0:
            best = t
    return best if best is not None else n


def _cparams(sem, vmem_bytes):
    limit = int(min(VMEM_LIMIT_V7X, max(32 * 1024 * 1024, vmem_bytes * 5 // 4)))
    return pltpu.CompilerParams(dimension_semantics=sem, vmem_limit_bytes=limit)


def _pcall(body, **kw):
    shapes = kw.pop('out_shape')
    many = isinstance(shapes, (list, tuple))
    pinned = [pltpu.HBM(t.shape, t.dtype) for t in (shapes if many else [shapes])]
    call = pl.pallas_call(body, out_shape=pinned if many else pinned[0], **kw)

    def run(*ops):
        return call(*[o if jnp.issubdtype(o.dtype, jnp.integer) else pltpu.with_memory_space_constraint(o, pltpu.HBM)
                      for o in ops])

    return run


def _cw(arr, cb, width):
    return (arr, cb, width)


def _win(a):
    if isinstance(a, tuple):
        return a
    return (a, 0, a.shape[-1])


def _nbytes(shape, dtype):
    return int(np.prod(shape)) * jnp.dtype(dtype).itemsize


def _rowwise(name, fn, rows, fulls, outs, accs=(), tl=256):
    rows = [_win(r) for r in rows]
    n_tok = rows[0][0].shape[0]
    tl = _tile(n_tok, tl, 16)
    nr, nf, no = len(rows), len(fulls), len(outs)

    def body(*refs):
        vals = [r[...] for r in refs[:nr + nf]]
        res = fn(*vals)
        if not isinstance(res, (tuple, list)):
            res = (res,)
        for k in range(no):
            o = refs[nr + nf + k]
            o[...] = res[k].astype(o.dtype)
        if accs:
            @pl.when(pl.program_id(0) == 0)
            def _():
                for k in range(len(accs)):
                    a = refs[nr + nf + no + k]
                    a[...] = jnp.zeros(a.shape, a.dtype)
            for k in range(len(accs)):
                a = refs[nr + nf + no + k]
                a[...] += res[no + k]

    in_specs = [pl.BlockSpec((tl, w), functools.partial(lambda i, cb: (i, cb), cb=cb)) for (_, cb, w) in rows]
    in_specs += [pl.BlockSpec(f.shape, functools.partial(lambda i, nd: (0,) * nd, nd=f.ndim)) for f in fulls]
    out_specs = [pl.BlockSpec((tl, c), lambda i: (i, 0)) for (c, _) in outs]
    out_specs += [pl.BlockSpec((r, c), lambda i: (0, 0)) for (r, c) in accs]
    out_shape = [jax.ShapeDtypeStruct((n_tok, c), dt) for (c, dt) in outs]
    out_shape += [jax.ShapeDtypeStruct((r, c), F32) for (r, c) in accs]
    vm = 2 * sum(_nbytes((tl, w), a.dtype) for (a, _, w) in rows)
    vm += 2 * sum(_nbytes(f.shape, f.dtype) for f in fulls)
    vm += 2 * sum(_nbytes((tl, c), dt) for (c, dt) in outs)
    vm += 8 * max([_nbytes((tl, w), F32) for (_, _, w) in rows] + [_nbytes((tl, c), F32) for (c, _) in outs])
    return _pcall(
        body, name=name, grid=(n_tok // tl,), in_specs=in_specs, out_specs=out_specs, out_shape=out_shape,
        compiler_params=_cparams(("arbitrary",) if accs else ("parallel",), vm),
    )(*[r[0] for r in rows], *fulls)


def _mm(name, a, b, *, ta=False, tb=False, out_dtype=F32, alpha=1.0, add=None, tm=512, tn=512):
    a_arr, a_cb, a_w = _win(a)
    b_arr, b_cb, b_w = _win(b)
    if ta:
        kdim, m = a_arr.shape[0], a_w
    else:
        m, kdim = a_arr.shape[0], a_w
    if tb:
        n, kb = b_arr.shape[0], b_w
    else:
        kb, n = b_arr.shape[0], b_w
    assert kdim == kb, (name, kdim, kb)
    if not ta and kdim <= 1024:
        tm = 2 * tm
    tm = _tile(m, tm, LANE if ta else 16)
    tn = _tile(n, tn, LANE)
    if ta:
        a_spec = pl.BlockSpec((kdim, tm), lambda i, j: (0, a_cb * (a_w // tm) + i))
        a_blk = (kdim, tm)
    else:
        a_spec = pl.BlockSpec((tm, kdim), lambda i, j: (i, a_cb))
        a_blk = (tm, kdim)
    if tb:
        b_spec = pl.BlockSpec((tn, kdim), lambda i, j: (j, b_cb))
        b_blk = (tn, kdim)
    else:
        b_spec = pl.BlockSpec((kdim, tn), lambda i, j: (0, b_cb * (b_w // tn) + j))
        b_blk = (kdim, tn)
    dims = (((0 if ta else 1,), (1 if tb else 0,)), ((), ()))
    has_add = add is not None

    def body(*refs):
        a_ref, b_ref = refs[0], refs[1]
        o_ref = refs[-1]
        acc = lax.dot_general(a_ref[...].astype(BF16), b_ref[...].astype(BF16), dims, preferred_element_type=F32)
        if alpha != 1.0:
            acc = acc * alpha
        if has_add:
            acc = acc + refs[2][...].astype(F32)
        o_ref[...] = acc.astype(o_ref.dtype)

    in_specs = [a_spec, b_spec]
    ops = [a_arr, b_arr]
    if has_add:
        in_specs.append(pl.BlockSpec((tm, tn), lambda i, j: (i, j)))
        ops.append(add)
    vm = 2 * (_nbytes(a_blk, a_arr.dtype) + _nbytes(b_blk, b_arr.dtype) + _nbytes((tm, tn), out_dtype))
    vm += _nbytes(a_blk, BF16) + _nbytes(b_blk, BF16) + 3 * _nbytes((tm, tn), F32)
    return _pcall(
        body, name=name, grid=(m // tm, n // tn), in_specs=in_specs,
        out_specs=pl.BlockSpec((tm, tn), lambda i, j: (i, j)),
        out_shape=jax.ShapeDtypeStruct((m, n), out_dtype),
        compiler_params=_cparams(("parallel", "parallel"), vm),
    )(*ops)


def _sigmoid(x):
    return 1.0 / (1.0 + jnp.exp(-x))


def _rms(x, gain):
    r = lax.rsqrt(jnp.mean(x * x, axis=-1, keepdims=True) + EPS)
    return x * r * gain


def _rms_bwd(x, gain, dh):
    r = lax.rsqrt(jnp.mean(x * x, axis=-1, keepdims=True) + EPS)
    xh = x * r
    dg = jnp.sum(dh * xh, axis=0, keepdims=True)
    dxh = dh * gain
    dx = r * (dxh - xh * jnp.mean(dxh * xh, axis=-1, keepdims=True))
    return dx, dg


_GELU_C = math.sqrt(2.0 / math.pi)


def _gelu(y):
    return 0.5 * y * (1.0 + jnp.tanh(_GELU_C * (y + 0.044715 * y * y * y)))


def _gelu_grad(y):
    th = jnp.tanh(_GELU_C * (y + 0.044715 * y * y * y))
    return 0.5 * (1.0 + th) + 0.5 * y * (1.0 - th * th) * _GELU_C * (1.0 + 3.0 * 0.044715 * y * y)


def _log_sigmoid(x):
    return jnp.minimum(x, 0.0) - jnp.log(1.0 + jnp.exp(-jnp.abs(x)))


def _dot(a, b, dims=(((1,), (0,)), ((), ())), precision=None):
    return lax.dot_general(a, b, dims, preferred_element_type=F32, precision=precision)


_NT = (((1,), (1,)), ((), ()))
_TN = (((0,), (0,)), ((), ()))


def _scan_core(b_ref, are_ref, aim_ref, pw, last, of32, *, rev, xf32=None, da_ref=None, lc=512):
    nb, nc = b_ref.shape[0] // SUBLANE, S5_N // lc
    row = lax.broadcasted_iota(jnp.int32, (SUBLANE, lc), 0)

    def cmul(ar, ai, xr, xi):
        return ar * xr - ai * xi, ar * xi + ai * xr

    def hs(c, xr, xi):
        cs = pl.ds(c * lc, lc)
        for k, slot in ((1, 0), (2, 2), (4, 4)):
            ar, ai = pw[slot, :, cs], pw[slot + 1, :, cs]
            if rev:
                sr, si = pltpu.roll(xr, SUBLANE - k, 0), pltpu.roll(xi, SUBLANE - k, 0)
                keep = row < SUBLANE - k
            else:
                sr, si = pltpu.roll(xr, k, 0), pltpu.roll(xi, k, 0)
                keep = row >= k
            sr, si = jnp.where(keep, sr, 0.0), jnp.where(keep, si, 0.0)
            pr, pi = cmul(ar, ai, sr, si)
            xr, xi = xr + pr, xi + pi
        return xr, xi

    @pl.when(pl.program_id(0) == 0)
    def _():
        last[...] = jnp.zeros(last.shape, F32)
        if da_ref is not None:
            da_ref[...] = jnp.zeros(da_ref.shape, F32)
        for c in range(nc):
            cs = pl.ds(c * lc, lc)
            a1r = jnp.broadcast_to(are_ref[:, cs], (SUBLANE, lc))
            a1i = jnp.broadcast_to(aim_ref[:, cs], (SUBLANE, lc))
            a2r, a2i = cmul(a1r, a1i, a1r, a1i)
            a4r, a4i = cmul(a2r, a2i, a2r, a2i)
            for slot, v in enumerate((a1r, a1i, a2r, a2i, a4r, a4i)):
                pw[slot, :, cs] = v
            first = SUBLANE - 1 if rev else 0
            pr, pi = hs(c, jnp.where(row == first, a1r, 0.0), jnp.where(row == first, a1i, 0.0))
            pw[6, :, cs] = pr
            pw[7, :, cs] = pi

    edge = 0 if rev else SUBLANE - 1

    def blk(j, carry):
        jj = (nb - 1 - j) if rev else j
        r0 = pl.multiple_of(jj * SUBLANE, SUBLANE)
        for c in range(nc):
            cre, cim = pl.ds(c * lc, lc), pl.ds(S5_N + c * lc, lc)
            xr, xi = hs(c, b_ref[pl.ds(r0, SUBLANE), cre], b_ref[pl.ds(r0, SUBLANE), cim])
            cr = jnp.broadcast_to(last[edge:edge + 1, cre], (SUBLANE, lc))
            ci = jnp.broadcast_to(last[edge:edge + 1, cim], (SUBLANE, lc))
            pr, pi = cmul(pw[6, :, cre], pw[7, :, cre], cr, ci)
            xr, xi = xr + pr, xi + pi
            if da_ref is not None:
                if rev:
                    qr, qi = pltpu.roll(xr, SUBLANE - 1, 0), pltpu.roll(xi, SUBLANE - 1, 0)
                    fill = row == SUBLANE - 1
                else:
                    qr, qi = pltpu.roll(xr, 1, 0), pltpu.roll(xi, 1, 0)
                    fill = row == 0
                qr, qi = jnp.where(fill, cr, qr), jnp.where(fill, ci, qi)
                zr, zi = xf32[pl.ds(r0, SUBLANE), cre], xf32[pl.ds(r0, SUBLANE), cim]
                da_ref[:, cre] += zr * qr + zi * qi
                da_ref[:, cim] += zr * qi - zi * qr
            last[:, cre] = xr
            last[:, cim] = xi
            of32[pl.ds(r0, SUBLANE), cre] = xr
            of32[pl.ds(r0, SUBLANE), cim] = xi
        return carry

    lax.fori_loop(0, nb, blk, 0)


def _s5_tiles():
    return [(slice((j % S5_TILES) * S5_TC, (j % S5_TILES + 1) * S5_TC), slice(j * S5_TS, (j + 1) * S5_TS))
            for j in range(2 * S5_TILES)]


def _s5_scratch(tb):
    return [pltpu.VMEM((8, SUBLANE, S5_N), F32), pltpu.VMEM((SUBLANE, 2 * S5_N), F32),
            pltpu.VMEM((tb, 2 * S5_N), F32), pltpu.VMEM((tb, 2 * S5_N), F32),
            pltpu.VMEM((S5_TC, 2 * S5_N), BF16), pltpu.VMEM((2 * S5_N, S5_TC), BF16)]


def _s5_fwd(name, z, a_re, a_im, wb, wc, add, *, rev, tb=512):
    n_tok = z.shape[0]
    tb = _tile(n_tok, tb, 16)
    nt = n_tok // tb
    has_add = add is not None

    def body(*refs):
        u_ref, are_ref, aim_ref, wb_ref, wc_ref = refs[:5]
        x_ref, y_ref, pw, last, of32, bu, wb16, wc16 = refs[5 + has_add:]

        @pl.when(pl.program_id(0) == 0)
        def _():
            wb16[...] = wb_ref[...].astype(BF16)
            wc16[...] = wc_ref[...].astype(BF16)

        u16 = u_ref[...].astype(BF16)
        for ch, st in _s5_tiles():
            bu[:, st] = _dot(u16[:, ch], wb16[:, st])
        _scan_core(bu, are_ref, aim_ref, pw, last, of32, rev=rev)
        x16 = of32[...].astype(BF16)
        x_ref[...] = x16
        ys = []
        for k in range(S5_TILES):
            re, im = slice(k * S5_TS, (k + 1) * S5_TS), slice(S5_N + k * S5_TS, S5_N + (k + 1) * S5_TS)
            ys.append(_dot(x16[:, re], wc16[re, :]) + _dot(x16[:, im], wc16[im, :]))
        y = jnp.concatenate(ys, axis=1)
        y_ref[...] = y + refs[5][...] if has_add else y

    tmap = (lambda t: (nt - 1 - t, 0)) if rev else (lambda t: (t, 0))
    const = lambda t: (0, 0)
    in_specs = [pl.BlockSpec((tb, S5_W), tmap), pl.BlockSpec((1, S5_N), const), pl.BlockSpec((1, S5_N), const),
                pl.BlockSpec((S5_TC, 2 * S5_N), const), pl.BlockSpec((2 * S5_N, S5_TC), const)]
    ops = [z, a_re, a_im, wb, wc]
    if has_add:
        in_specs.append(pl.BlockSpec((tb, S5_W), tmap))
        ops.append(add)
    return _pcall(
        body, name=name, grid=(nt,), in_specs=in_specs,
        out_specs=[pl.BlockSpec((tb, 2 * S5_N), tmap), pl.BlockSpec((tb, S5_W), tmap)],
        out_shape=[jax.ShapeDtypeStruct((n_tok, 2 * S5_N), BF16), jax.ShapeDtypeStruct((n_tok, S5_W), F32)],
        scratch_shapes=_s5_scratch(tb), compiler_params=_cparams(("arbitrary",), 10 * tb * 2 * S5_N * 4),
    )(*ops)


def _s5_bwd(name, dy, z, xs, a_re, a_im, wb, wc, add, *, rev, tb=256):
    n_tok = z.shape[0]
    tb = _tile(n_tok, tb, 16)
    nt = n_tok // tb

    def body(dy_ref, u_ref, xs_ref, are_ref, aim_ref, wb_ref, wc_ref, add_ref, du_ref, dwb_ref, dwc_ref, da_ref,
             pw, last, of32, gx, wb16, wc16, xf32):
        @pl.when(pl.program_id(0) == 0)
        def _():
            wb16[...] = wb_ref[...].astype(BF16)
            wc16[...] = wc_ref[...].astype(BF16)
            dwb_ref[...] = jnp.zeros(dwb_ref.shape, F32)
            dwc_ref[...] = jnp.zeros(dwc_ref.shape, F32)

        dy16, u16, x16 = dy_ref[...].astype(BF16), u_ref[...].astype(BF16), xs_ref[...]
        for ch, st in _s5_tiles():
            gx[:, st] = _dot(dy16[:, ch], wc16[st, :], _NT)
        xf32[...] = x16.astype(F32)
        _scan_core(gx, are_ref, aim_ref, pw, last, of32, rev=rev, xf32=xf32, da_ref=da_ref)
        g16 = of32[...].astype(BF16)
        dus = []
        for k in range(S5_TILES):
            re, im = slice(k * S5_TS, (k + 1) * S5_TS), slice(S5_N + k * S5_TS, S5_N + (k + 1) * S5_TS)
            dus.append(_dot(g16[:, re], wb16[:, re], _NT) + _dot(g16[:, im], wb16[:, im], _NT))
        du_ref[...] = jnp.concatenate(dus, axis=1) + add_ref[...]
        for ch, st in _s5_tiles():
            dwb_ref[:, st] += _dot(u16[:, ch], g16[:, st], _TN)
            dwc_ref[st, :] += _dot(x16[:, st], dy16[:, ch], _TN)

    tmap = (lambda t: (nt - 1 - t, 0)) if rev else (lambda t: (t, 0))
    const = lambda t: (0, 0)
    row_w, row_s = pl.BlockSpec((tb, S5_W), tmap), pl.BlockSpec((tb, 2 * S5_N), tmap)
    return _pcall(
        body, name=name, grid=(nt,),
        in_specs=[row_w, row_w, row_s, pl.BlockSpec((1, S5_N), const), pl.BlockSpec((1, S5_N), const),
                  pl.BlockSpec((S5_TC, 2 * S5_N), const), pl.BlockSpec((2 * S5_N, S5_TC), const), row_w],
        out_specs=[row_w, pl.BlockSpec((S5_TC, 2 * S5_N), const), pl.BlockSpec((2 * S5_N, S5_TC), const),
                   pl.BlockSpec((SUBLANE, 2 * S5_N), const)],
        out_shape=[jax.ShapeDtypeStruct((n_tok, S5_W), F32), jax.ShapeDtypeStruct((S5_TC, 2 * S5_N), F32),
                   jax.ShapeDtypeStruct((2 * S5_N, S5_TC), F32), jax.ShapeDtypeStruct((SUBLANE, 2 * S5_N), F32)],
        scratch_shapes=_s5_scratch(tb) + [pltpu.VMEM((tb, 2 * S5_N), F32)],
        compiler_params=_cparams(("arbitrary",), 12 * tb * 2 * S5_N * 4),
    )(dy, z, xs, a_re, a_im, wb, wc, add)


GLA_TB = 512


def _gla_block_terms(la, q, k, rev):
    tb = la.shape[0]
    ri = lax.broadcasted_iota(jnp.int32, (tb, tb), 0)
    ci = lax.broadcasted_iota(jnp.int32, (tb, tb), 1)
    same = (ri // GLA_C) == (ci // GLA_C)
    tri = same & ((ri <= ci) if rev else (ri >= ci))
    trif = tri.astype(F32)
    bcum = _dot(trif, la, precision=HI)
    blast = _dot(same.astype(F32), la, precision=HI)
    eb, enb, ee = jnp.exp(bcum), jnp.exp(-bcum), jnp.exp(blast - bcum)
    return trif, jnp.exp(blast), eb, enb, ee, q * eb, k * enb, k * ee


def _chunk_mask(rev):
    ri = lax.broadcasted_iota(jnp.int32, (GLA_C, GLA_C), 0)
    ci = lax.broadcasted_iota(jnp.int32, (GLA_C, GLA_C), 1)
    return (ri <= ci) if rev else (ri >= ci)


def _gla_fwd(name, z, la, *, rev):
    n_tok = z.shape[0]
    tb = _tile(n_tok, GLA_TB, GLA_C)
    nblk, per = n_tok // tb, tb // GLA_C
    scale = GLA_D ** -0.5

    def body(q_ref, k_ref, v_ref, la_ref, o_ref, s_ref, st, qd_s, kd_s, ke_s, v_s, dec_s):
        @pl.when(pl.program_id(0) == 0)
        def _():
            st[...] = jnp.zeros(st.shape, F32)

        _, dec, _, _, _, qd, kd, ke = _gla_block_terms(la_ref[...], q_ref[...].astype(F32) * scale,
                                                                 k_ref[...].astype(F32), rev)
        qd_s[...], kd_s[...], ke_s[...] = qd.astype(BF16), kd.astype(BF16), ke.astype(BF16)
        v_s[...] = v_ref[...].astype(BF16)
        dec_s[...] = dec
        tri = _chunk_mask(rev)
        for step in range(per):
            t = per - 1 - step if rev else step
            rows = pl.ds(t * GLA_C, GLA_C)
            for h in range(GLA_H):
                lanes = pl.ds(h * GLA_D, GLA_D)
                qd16, v16 = qd_s[rows, lanes], v_s[rows, lanes]
                sc = jnp.where(tri, _dot(qd16, kd_s[rows, lanes], _NT), 0.0)
                s16 = st[h].astype(BF16)
                s_ref[h, step] = s16
                o_ref[rows, lanes] = _dot(sc.astype(BF16), v16) + _dot(qd16, s16, _NT)
                st[h] = st[h] * dec_s[pl.ds(t * GLA_C, 1), lanes] + _dot(v16, ke_s[rows, lanes], _TN)

    bmap = (lambda n: nblk - 1 - n) if rev else (lambda n: n)

    def zspec(cb):
        return pl.BlockSpec((tb, GLA_W), lambda n: (bmap(n), cb))

    half = pltpu.VMEM((tb, GLA_W), BF16)
    return _pcall(
        body, name=name, grid=(nblk,),
        in_specs=[zspec(1), zspec(2), zspec(3), zspec(0)],
        out_specs=[zspec(0), pl.BlockSpec((GLA_H, per, GLA_D, GLA_D), lambda n: (0, n, 0, 0))],
        out_shape=[jax.ShapeDtypeStruct((n_tok, GLA_W), F32),
                   jax.ShapeDtypeStruct((GLA_H, n_tok // GLA_C, GLA_D, GLA_D), BF16)],
        scratch_shapes=[pltpu.VMEM((GLA_H, GLA_D, GLA_D), F32), half, half, half, half, pltpu.VMEM((tb, GLA_W), F32)],
        compiler_params=_cparams(("arbitrary",), 40 * tb * GLA_W * 4),
    )(z, z, z, la)


def _gla_bwd(name, z, la, s_prev, do, *, rev):
    n_tok = z.shape[0]
    tb = _tile(n_tok, GLA_TB, GLA_C)
    nblk, per = n_tok // tb, tb // GLA_C
    scale = GLA_D ** -0.5

    def body(q_ref, k_ref, v_ref, la_ref, s_ref, do_ref, dq_ref, dk_ref, dv_ref, dla_ref,
             dst, qd_s, kd_s, ke_s, v_s, do_s, dec_s, dqd_s, dkd_s, dke_s, dd_s):
        @pl.when(pl.program_id(0) == 0)
        def _():
            dst[...] = jnp.zeros(dst.shape, F32)

        trif, dec, eb, enb, ee, qd, kd, ke = _gla_block_terms(la_ref[...], q_ref[...].astype(F32) * scale,
                                                                 k_ref[...].astype(F32), rev)
        qd_s[...], kd_s[...], ke_s[...] = qd.astype(BF16), kd.astype(BF16), ke.astype(BF16)
        v_s[...] = v_ref[...].astype(BF16)
        do_s[...] = do_ref[...].astype(BF16)
        dec_s[...] = dec
        tri = _chunk_mask(rev)
        for step in range(per):
            t = step if rev else per - 1 - step
            rows = pl.ds(t * GLA_C, GLA_C)
            for h in range(GLA_H):
                lanes = pl.ds(h * GLA_D, GLA_D)
                qd16, kd16, ke16 = qd_s[rows, lanes], kd_s[rows, lanes], ke_s[rows, lanes]
                v16, do16 = v_s[rows, lanes], do_s[rows, lanes]
                sc16 = jnp.where(tri, _dot(qd16, kd16, _NT), 0.0).astype(BF16)
                dsc16 = jnp.where(tri, _dot(do16, v16, _NT), 0.0).astype(BF16)
                s16 = s_ref[h, per - 1 - step]
                dsn = dst[h]
                dsn16 = dsn.astype(BF16)
                dec_c = dec_s[pl.ds(t * GLA_C, 1), lanes]
                dqd_s[rows, lanes] = _dot(dsc16, kd16) + _dot(do16, s16)
                dkd_s[rows, lanes] = _dot(dsc16, qd16, _TN)
                dke_s[rows, lanes] = _dot(v16, dsn16)
                dv_ref[rows, lanes] = _dot(sc16, do16, _TN) + _dot(ke16, dsn16, _NT)
                ddec = jnp.sum(dsn * s16.astype(F32), axis=0, keepdims=True) * dec_c
                dd_s[rows, lanes] = jnp.broadcast_to(ddec, (GLA_C, GLA_D))
                dst[h] = dsn * dec_c + _dot(do16, qd16, _TN)
        dqd, dkd, dke = dqd_s[...], dkd_s[...], dke_s[...]
        dq_ref[...] = dqd * eb * scale
        dk_ref[...] = dkd * enb + dke * ee
        db = dqd * qd - dkd * kd - dke * ke
        ri = lax.broadcasted_iota(jnp.int32, (tb, tb), 0)
        ci = lax.broadcasted_iota(jnp.int32, (tb, tb), 1)
        same = ((ri // GLA_C) == (ci // GLA_C)).astype(F32)
        dla_ref[...] = _dot(trif, db, _TN, precision=HI) + _dot(same, dke * ke, precision=HI) + dd_s[...]

    bmap = (lambda n: n) if rev else (lambda n: nblk - 1 - n)

    def zspec(cb):
        return pl.BlockSpec((tb, GLA_W), lambda n: (bmap(n), cb))

    o4 = jax.ShapeDtypeStruct((n_tok, GLA_W), F32)
    half, full = pltpu.VMEM((tb, GLA_W), BF16), pltpu.VMEM((tb, GLA_W), F32)
    return _pcall(
        body, name=name, grid=(nblk,),
        in_specs=[zspec(1), zspec(2), zspec(3), zspec(0),
                  pl.BlockSpec((GLA_H, per, GLA_D, GLA_D), lambda n: (0, nblk - 1 - n, 0, 0)), zspec(0)],
        out_specs=[zspec(0)] * 4, out_shape=[o4] * 4,
        scratch_shapes=[pltpu.VMEM((GLA_H, GLA_D, GLA_D), F32), half, half, half, half, half, full, full, full, full, full],
        compiler_params=_cparams(("arbitrary",), 64 * tb * GLA_W * 4),
    )(z, z, z, la, s_prev, do)


_PAIRS = AT_W // LANE
_PAIRS_PER_KV = _PAIRS // AT_KH


def _attn_fwd(name, q, kd, vd, tq=256):
    n_tok = q.shape[0]
    tq = _tile(n_tok, tq, 16)

    def body(q_ref, k_ref, v_ref, o_ref, l_ref):
        qv, kv, vv = q_ref[...], k_ref[...], v_ref[...]
        low = lax.broadcasted_iota(jnp.int32, qv.shape, 1) < AT_D
        res = []
        for keep in (low, jnp.logical_not(low)):
            s = _dot(jnp.where(keep, qv, jnp.zeros_like(qv)), kv, _NT)
            m = jnp.max(s, axis=-1, keepdims=True)
            p = jnp.exp(s - m)
            den = jnp.sum(p, axis=-1, keepdims=True)
            res.append((_dot(p.astype(BF16), vv) / den, m + jnp.log(den)))
        o_ref[...] = jnp.where(low, res[0][0], res[1][0])
        l_ref[...] = jnp.where(low, res[0][1], res[1][1])

    qspec = pl.BlockSpec((tq, LANE), lambda p, i: (i, p))
    kspec = pl.BlockSpec((n_tok, LANE), lambda p, i: (0, p // _PAIRS_PER_KV))
    vm = 5 * _nbytes((tq, n_tok), F32) + 8 * _nbytes((n_tok, LANE), BF16)
    out = jax.ShapeDtypeStruct((n_tok, AT_W), F32)
    return _pcall(
        body, name=name, grid=(_PAIRS, n_tok // tq), in_specs=[qspec, kspec, kspec], out_specs=[qspec, qspec],
        out_shape=[out, out], compiler_params=_cparams(("parallel", "parallel"), vm),
    )(q, kd, vd)


def _attn_bwd(name, q, kd, vd, o, lse, do, tq=256):
    n_tok = q.shape[0]
    tq = _tile(n_tok, tq, 16)

    def body(q_ref, k_ref, v_ref, o_ref, l_ref, do_ref, dq_ref, dk_ref, dv_ref):
        @pl.when((pl.program_id(1) == 0) & (pl.program_id(2) == 0))
        def _():
            dk_ref[...] = jnp.zeros(dk_ref.shape, F32)
            dv_ref[...] = jnp.zeros(dv_ref.shape, F32)

        qv, kv, vv = q_ref[...], k_ref[...], v_ref[...]
        dof, lv = do_ref[...], l_ref[...]
        do16 = dof.astype(BF16)
        doo = dof * o_ref[...]
        low = lax.broadcasted_iota(jnp.int32, qv.shape, 1) < AT_D
        dq = jnp.zeros(qv.shape, F32)
        for keep in (low, jnp.logical_not(low)):
            qm = jnp.where(keep, qv, jnp.zeros_like(qv))
            dom = jnp.where(keep, do16, jnp.zeros_like(do16))
            lh = jnp.max(jnp.where(keep, lv, -1e30), axis=-1, keepdims=True)
            p = jnp.exp(_dot(qm, kv, _NT) - lh)
            delta = jnp.sum(jnp.where(keep, doo, 0.0), axis=-1, keepdims=True)
            ds16 = (p * (_dot(dom, vv, _NT) - delta)).astype(BF16)
            dq = jnp.where(keep, _dot(ds16, kv), dq)
            dk_ref[...] += _dot(ds16, qm, _TN)
            dv_ref[...] += _dot(p.astype(BF16), dom, _TN)
        dq_ref[...] = dq

    qspec = pl.BlockSpec((tq, LANE), lambda g, j, i: (i, g * _PAIRS_PER_KV + j))
    kspec = pl.BlockSpec((n_tok, LANE), lambda g, j, i: (0, g))
    vm = 7 * _nbytes((tq, n_tok), F32) + 12 * _nbytes((n_tok, LANE), F32)
    dup = jax.ShapeDtypeStruct((n_tok, AT_KH * LANE), F32)
    return _pcall(
        body, name=name, grid=(AT_KH, _PAIRS_PER_KV, n_tok // tq),
        in_specs=[qspec, kspec, kspec, qspec, qspec, qspec], out_specs=[qspec, kspec, kspec],
        out_shape=[jax.ShapeDtypeStruct((n_tok, AT_W), F32), dup, dup],
        compiler_params=_cparams(("parallel", "arbitrary", "arbitrary"), vm),
    )(q, kd, vd, o, lse, do)


def _s5_prep(lam_re, lam_im, log_dt, b_re, b_im, c_re, c_im):
    dt = jnp.exp(log_dt)[:, None]
    er = jnp.exp(lam_re * dt)
    a_re, a_im = er * jnp.cos(lam_im * dt), er * jnp.sin(lam_im * dt)
    nr, ni = a_re - 1.0, a_im
    den = lam_re * lam_re + lam_im * lam_im
    f_re, f_im = (nr * lam_re + ni * lam_im) / den, (ni * lam_re - nr * lam_im) / den
    bb_re = f_re[..., None] * b_re - f_im[..., None] * b_im
    bb_im = f_re[..., None] * b_im + f_im[..., None] * b_re
    def tiles(t, axis):
        _, a, b = t.shape
        eye = jnp.eye(S5_TG, dtype=t.dtype)
        blocks = t.reshape(S5_TILES, S5_TG, a, 1, b) * eye[None, :, None, :, None]
        if axis == 0:
            return blocks.reshape(S5_TILES * S5_TG * a, S5_TG * b)
        return blocks.transpose(1, 2, 0, 3, 4).reshape(S5_TG * a, S5_TILES * S5_TG * b)

    w_in = jnp.concatenate([tiles(bb_re.transpose(0, 2, 1), 1), tiles(bb_im.transpose(0, 2, 1), 1)], axis=1)
    w_out = jnp.concatenate([tiles(c_re.transpose(0, 2, 1), 0), -tiles(c_im.transpose(0, 2, 1), 0)], axis=0)
    return a_re.reshape(1, S5_N), a_im.reshape(1, S5_N), w_in, w_out


def _rope_tables(n_tok):
    quarter = AT_D // 4
    d = np.arange(AT_D)
    e = d % (AT_D // 2)
    inv = (ROPE_BASE ** (-(e % quarter).astype(np.float64) * 2.0 / (AT_D // 2))).astype(np.float32)
    sign = np.where(e < quarter, -1.0, 1.0).astype(np.float32)
    rot = np.zeros((AT_D, AT_D), np.float32)
    rot[np.where(e < quarter, d + quarter, d - quarter), d] = 1.0
    pos = jnp.arange(n_tok, dtype=jnp.int32)
    axis_pos = jnp.where(jnp.asarray(d // (AT_D // 2) == 0)[None, :], (pos // GRID_W)[:, None], (pos % GRID_W)[:, None])
    ang = axis_pos.astype(F32) * jnp.asarray(inv)[None, :]
    return jnp.cos(ang), jnp.sin(ang) * jnp.asarray(sign)[None, :], rot


def _head_consts(n_tok, heads):
    cos, sin, rot = _rope_tables(n_tok)
    eye = np.eye(heads, dtype=np.float32)
    mean = np.kron(eye, np.full((AT_D, AT_D), 1.0 / AT_D, np.float32))
    return jnp.tile(cos, (1, heads)), jnp.tile(sin, (1, heads)), jnp.asarray(mean), jnp.asarray(np.kron(eye, rot))


def _pad_w_in(w):
    return jnp.concatenate([w[:, :2560], w[:, 2592:3360], w[:, 2560:2592], jnp.zeros((w.shape[0], 96), w.dtype)], axis=1)


def _unpad_w_in(g):
    return jnp.concatenate([g[:, :2560], g[:, 3328:3360], g[:, 2560:3328]], axis=1)


def _ffn_fwd(x, g, wg, wu, wd, d):
    n_tok = x.shape[0]
    fq = wd.shape[-2]
    tm = _tile(n_tok, 256, 16)
    w_bytes = N_CHIPS * fq * D_MODEL * 2

    def up(x_ref, g_ref, wg_ref, wu_ref, h_ref, a_ref, b_ref, s_ref):
        hv = _rms(x_ref[...], g_ref[...]).astype(BF16)
        h_ref[...] = hv
        for j in range(N_CHIPS):
            av, bv = _dot(hv, wg_ref[j, 0], _NT), _dot(hv, wu_ref[j, 0], _NT)
            a_ref[j] = av.astype(BF16)
            b_ref[j] = bv.astype(BF16)
            s_ref[j] = (av * _sigmoid(av) * bv).astype(BF16)

    wspec = pl.BlockSpec((N_CHIPS, 1, fq, D_MODEL), lambda i: (0, d, 0, 0))
    aspec = pl.BlockSpec((N_CHIPS, tm, fq), lambda i: (0, i, 0))
    xspec = pl.BlockSpec((tm, D_MODEL), lambda i: (i, 0))
    act = jax.ShapeDtypeStruct((N_CHIPS, n_tok, fq), BF16)
    h, a, b, s = _pcall(
        up, name="ffn_up", grid=(n_tok // tm,),
        in_specs=[xspec, pl.BlockSpec((1, D_MODEL), lambda i: (0, 0)), wspec, wspec],
        out_specs=[xspec, aspec, aspec, aspec],
        out_shape=[jax.ShapeDtypeStruct((n_tok, D_MODEL), BF16), act, act, act],
        compiler_params=_cparams(("parallel",), 4 * w_bytes + 6 * N_CHIPS * tm * fq * 2 + 12 * tm * fq * 4
                                 + 8 * tm * D_MODEL * 4),
    )(x, g, wg, wu)

    def down(s_ref, w_ref, x_ref, o_ref):
        acc = _dot(s_ref[0], w_ref[0, 0])
        for j in range(1, N_CHIPS):
            acc = acc + _dot(s_ref[j], w_ref[j, 0])
        o_ref[...] = x_ref[...] + 0.5 * acc

    xo = _pcall(
        down, name="ffn_down", grid=(n_tok // tm,), in_specs=[aspec, wspec, xspec], out_specs=xspec,
        out_shape=jax.ShapeDtypeStruct((n_tok, D_MODEL), F32),
        compiler_params=_cparams(("parallel",), 2 * w_bytes + 2 * N_CHIPS * tm * fq * 2 + 8 * tm * D_MODEL * 4),
    )(s, wd, x)
    return xo, (x, h, a, b, s)


def _ffn_bwd(dy, dy16, saved, g, wg, wu, wd, d, into):
    x, h, a, b, s = saved
    n_tok = x.shape[0]
    fq = wd.shape[-2]
    tm = _tile(n_tok, 512, 16)

    ta_ = _tile(n_tok, 256, 16)

    def dact(dy_ref, w_ref, a_ref, b_ref, da_ref, db_ref):
        dyv = dy_ref[...]
        for j in range(N_CHIPS):
            ds = 0.5 * _dot(dyv, w_ref[j, 0], _NT)
            av, bv = a_ref[j].astype(F32), b_ref[j].astype(F32)
            sg = _sigmoid(av)
            da_ref[j] = (ds * bv * sg * (1.0 + av * (1.0 - sg))).astype(BF16)
            db_ref[j] = (ds * av * sg).astype(BF16)

    aspec = pl.BlockSpec((N_CHIPS, ta_, fq), lambda i: (0, i, 0))
    act = jax.ShapeDtypeStruct((N_CHIPS, n_tok, fq), BF16)
    da, db = _pcall(
        dact, name="ffn_dact", grid=(n_tok // ta_,),
        in_specs=[pl.BlockSpec((ta_, D_MODEL), lambda i: (i, 0)),
                  pl.BlockSpec((N_CHIPS, 1, fq, D_MODEL), lambda i: (0, d, 0, 0)), aspec, aspec],
        out_specs=[aspec, aspec], out_shape=[act, act],
        compiler_params=_cparams(("parallel",), 2 * N_CHIPS * fq * D_MODEL * 2 + 8 * N_CHIPS * ta_ * fq * 2
                                 + 12 * ta_ * fq * 4),
    )(dy16, wd, a, b)

    any_spec = pl.BlockSpec(memory_space=pl.ANY)
    alias = {} if into is None else {2: 0}
    tn = _tile(D_MODEL, 512, LANE)

    def dw_call(t, rhs, scale, prev):
        def dw(t_ref, r_ref, *rest):
            rest[-1][0, 0] = (scale * _dot(t_ref[0], r_ref[...], _TN)).astype(GRAD_DT)

        return _pcall(
            dw, name="ffn_dw", grid=(N_CHIPS, D_MODEL // tn),
            in_specs=[pl.BlockSpec((1, n_tok, fq), lambda j, n: (j, 0, 0)), pl.BlockSpec((n_tok, tn), lambda j, n: (0, n))]
            + ([] if prev is None else [any_spec]),
            out_specs=pl.BlockSpec((1, 1, fq, tn), lambda j, n: (j, d, 0, n)),
            out_shape=jax.ShapeDtypeStruct((N_CHIPS, DEPTH, fq, D_MODEL), GRAD_DT), input_output_aliases=alias,
            compiler_params=_cparams(("parallel", "parallel"), 6 * n_tok * (fq + tn) * 2 + 4 * fq * tn * 4),
        )(t, rhs, *([] if prev is None else [prev]))

    dwg = dw_call(da, h, 1.0, None if into is None else into[0])
    dwu = dw_call(db, h, 1.0, None if into is None else into[1])
    dwd = dw_call(s, dy16, 0.5, None if into is None else into[2])
    tr = _tile(n_tok, 256, 16)

    def dh_norm(da_ref, db_ref, wg_ref, wu_ref, dy_ref, x_ref, g_ref, dx_ref, dx16_ref, dg_ref):
        dh = _dot(da_ref[0], wg_ref[0, 0]) + _dot(db_ref[0], wu_ref[0, 0])
        for j in range(1, N_CHIPS):
            dh = dh + _dot(da_ref[j], wg_ref[j, 0]) + _dot(db_ref[j], wu_ref[j, 0])
        dx, dg = _rms_bwd(x_ref[...], g_ref[...], dh)
        dx = dy_ref[...] + dx
        dx_ref[...] = dx
        dx16_ref[...] = dx.astype(BF16)

        @pl.when(pl.program_id(0) == 0)
        def _():
            dg_ref[...] = jnp.zeros(dg_ref.shape, F32)

        dg_ref[...] += dg

    tspec = pl.BlockSpec((N_CHIPS, tr, fq), lambda i: (0, i, 0))
    wspec = pl.BlockSpec((N_CHIPS, 1, fq, D_MODEL), lambda i: (0, d, 0, 0))
    rspec = pl.BlockSpec((tr, D_MODEL), lambda i: (i, 0))
    vspec = pl.BlockSpec((1, D_MODEL), lambda i: (0, 0))
    dx, dx16, dg = _pcall(
        dh_norm, name="ffn_dh_norm", grid=(n_tok // tr,),
        in_specs=[tspec, tspec, wspec, wspec, rspec, rspec, vspec], out_specs=[rspec, rspec, vspec],
        out_shape=[jax.ShapeDtypeStruct((n_tok, D_MODEL), F32), jax.ShapeDtypeStruct((n_tok, D_MODEL), BF16),
                   jax.ShapeDtypeStruct((1, D_MODEL), F32)],
        compiler_params=_cparams(("arbitrary",), 4 * N_CHIPS * D_MODEL * fq * 2 + 4 * N_CHIPS * tr * fq * 2
                                 + 16 * tr * D_MODEL * 4),
    )(da, db, wg, wu, dy, x, g)
    return dx, dx16, dg[0], (dwg, dwu, dwd)


def _alpha_pads(w_alpha):
    z = jnp.zeros((LANE - 2 * GLA_R, GLA_W), F32)
    wf = jnp.concatenate([w_alpha[0], jnp.zeros((GLA_R, GLA_W), F32), z], axis=0)
    wb = jnp.concatenate([jnp.zeros((GLA_R, GLA_W), F32), w_alpha[1], z], axis=0)
    return wf, wb


def _dup_heads(t):
    low = lax.broadcasted_iota(jnp.int32, t.shape, 1) < AT_D
    h0, h1 = jnp.where(low, t, 0.0), jnp.where(low, 0.0, t)
    return jnp.concatenate([h0 + pltpu.roll(h0, AT_D, 1), h1 + pltpu.roll(h1, AT_D, 1)], axis=1)


def _fold_heads(t):
    low = lax.broadcasted_iota(jnp.int32, (t.shape[0], LANE), 1) < AT_D
    g0, g1 = t[:, :LANE], t[:, LANE:]
    return jnp.where(low, g0 + pltpu.roll(g0, AT_D, 1), g1 + pltpu.roll(g1, AT_D, 1))


def _mixer_fwd(x, w):
    n_tok = x.shape[0]
    wip, wmg = w['w_in_p'], w['w_merge_gate']
    tr = _tile(n_tok, 256, 16)

    def project(x_ref, g_ref, wip_ref, wmg_ref, h_ref, z_ref, gp_ref):
        hv = _rms(x_ref[...], g_ref[...]).astype(BF16)
        h_ref[...] = hv
        z_ref[...] = _dot(hv, wip_ref[...]).astype(BF16)
        gp_ref[...] = _dot(hv, wmg_ref[...]).astype(BF16)

    def rows(c):
        return pl.BlockSpec((tr, c), lambda i: (i, 0))

    def whole(t):
        return pl.BlockSpec(t.shape, lambda i: (0, 0))

    h, z, gp = _pcall(
        project, name="mix_in", grid=(n_tok // tr,),
        in_specs=[rows(D_MODEL), whole(w['mix_norm']), whole(wip), whole(wmg)],
        out_specs=[rows(D_MODEL), rows(ZP_W), rows(3 * D_MODEL)],
        out_shape=[jax.ShapeDtypeStruct((n_tok, D_MODEL), BF16), jax.ShapeDtypeStruct((n_tok, ZP_W), BF16),
                   jax.ShapeDtypeStruct((n_tok, 3 * D_MODEL), BF16)],
        compiler_params=_cparams(("parallel",), 4 * (wip.size + wmg.size) + 16 * tr * (ZP_W + 3 * D_MODEL)),
    )(x, w['mix_norm'], wip, wmg)
    u = _cw(z, 0, S5_W)

    xs, yy = [], None
    for d, rev in ((0, False), (1, True)):
        a_re, a_im, wb, wc = w['s5'][d]
        x_d, yy = _s5_fwd("s5_rev" if rev else "s5_fwd", z, a_re, a_im, wb, wc, yy, rev=rev)
        xs.append(x_d)

    def s5_post(yv, uv, dv, wglu):
        y0 = yv + dv * uv
        y1 = _gelu(y0)
        t = _dot(y1.astype(BF16), wglu)
        return y0, t, y1 * _sigmoid(t)

    y0, t_glu, y_s5 = _rowwise("s5_post", s5_post, [yy, u], [w['s5_d'], w['s5_w_glu']],
                               [(S5_W, F32), (S5_W, F32), (S5_W, BF16)])

    wa_f, wa_b = _alpha_pads(w['gla_w_alpha'])
    ba_f, ba_b = w['gla_b_alpha'][0:1], w['gla_b_alpha'][1:2]
    zz = _cw(z, ZC_Z, LANE)

    def gla_prep(zv, waf, wab, bf, bb):
        z16 = zv.astype(BF16)
        lf = _dot(z16, waf.astype(BF16)) + bf
        lb = _dot(z16, wab.astype(BF16)) + bb
        return _log_sigmoid(lf) / GLA_TAU, _log_sigmoid(lb) / GLA_TAU

    la_f, la_b = _rowwise("gla_prep", gla_prep, [zz], [wa_f, wa_b, ba_f, ba_b], [(GLA_W, F32), (GLA_W, F32)])
    o_f, s_f = _gla_fwd("gla_fwd", z, la_f, rev=False)
    o_b, s_b = _gla_fwd("gla_rev", z, la_b, rev=True)
    gate = _cw(z, 4, GLA_W)

    def gla_post(of, ob, gv, gn):
        o, gv = of + ob, gv.astype(F32)
        outs = []
        for hh in range(GLA_H):
            sl = slice(hh * GLA_D, (hh + 1) * GLA_D)
            gh = gv[:, sl]
            outs.append(_rms(o[:, sl], gn) * (gh * _sigmoid(gh)))
        return jnp.concatenate(outs, axis=1)

    y_gla = _rowwise("gla_post", gla_post, [o_f, o_b, gate], [w['gla_norm']], [(GLA_W, BF16)])[0]

    cq, sq, mq, rq = _head_consts(n_tok, AT_QH)
    ck, sk, mk, rk = _head_consts(n_tok, AT_KH)
    gq = jnp.tile(w['attn_q_norm'], (1, AT_QH))
    gk = jnp.tile(w['attn_k_norm'], (1, AT_KH))

    def rope_fwd(v, gain, cos, sin, mean, rot, scale):
        r = lax.rsqrt(_dot(v * v, mean, precision=HI) + EPS)
        vn = v * r * gain
        return (vn * cos + _dot(vn, rot, precision=HI) * sin) * scale

    def attn_prep(qv, kv, vv, cqv, sqv, ckv, skv, gqv, gkv, mqv, rqv, mkv, rkv):
        qv, kv, vv = qv.astype(F32), kv.astype(F32), vv.astype(F32)
        kr = rope_fwd(kv, gkv, ckv, skv, mkv, rkv, 1.0)
        return rope_fwd(qv, gqv, cqv, sqv, mqv, rqv, AT_D ** -0.5), _dup_heads(kr), _dup_heads(vv)

    q_raw, k_raw, v_raw = _cw(z, 5, AT_W), _cw(z, ZC_AK, AT_KW), _cw(z, ZC_AV, AT_KW)
    q16, kd, vd = _rowwise("attn_prep", attn_prep, [q_raw, k_raw, v_raw, cq, sq, ck, sk], [gq, gk, mq, rq, mk, rk],
                           [(AT_W, BF16), (2 * LANE, BF16), (2 * LANE, BF16)])
    y_at, lse = _attn_fwd("attn_fwd", q16, kd, vd)

    p0 = _mm("mix_branch", y_s5, w['w_branch_s5'], out_dtype=BF16)
    p1 = _mm("mix_branch", y_gla, w['w_branch_gla'], out_dtype=BF16)
    p2 = _mm("mix_branch_f32", y_at, w['w_branch_attn'], out_dtype=BF16)

    def merge(gpv, a0, a1, a2, bias):
        g = _sigmoid(gpv.astype(F32) + bias)
        return (g[:, :D_MODEL] * a0.astype(F32) + g[:, D_MODEL:2 * D_MODEL] * a1.astype(F32)
                + g[:, 2 * D_MODEL:] * a2.astype(F32))

    merged = _rowwise("mix_merge", merge, [gp, p0, p1, p2], [w['b_merge_gate']], [(D_MODEL, BF16)])[0]
    xo = _mm("mix_out", merged, w['w_out'], add=x)
    saved = dict(x=x, h=h, z=z, gp=gp, xs=xs, y0=y0, t_glu=t_glu, y_s5=y_s5, la=(la_f, la_b), o=(o_f, o_b), s=(s_f, s_b),
                 y_gla=y_gla, q16=q16, kd=kd, vd=vd, lse=lse, y_at=y_at, p=(p0, p1, p2), merged=merged)
    return xo, saved


def _mixer_bwd(dy, dy16, sv, w, s5_raw):
    z, h = sv['z'], sv['h']
    n_tok = z.shape[0]
    grads = {}
    dmerged = _mm("mix_dmerged", dy16, w['w_out'], tb=True, out_dtype=BF16)
    grads['w_out'] = _mm("mix_dwout", sv['merged'], dy16, ta=True)

    def dmerge(dm, gpv, a0, a1, a2, bias):
        dm = dm.astype(F32)
        g = _sigmoid(gpv.astype(F32) + bias)
        ps = (a0.astype(F32), a1.astype(F32), a2.astype(F32))
        dps, dgs = [], []
        for k in range(3):
            gk = g[:, k * D_MODEL:(k + 1) * D_MODEL]
            dps.append(dm * gk)
            dgs.append(dm * ps[k] * gk * (1.0 - gk))
        dgp = jnp.concatenate(dgs, axis=1)
        return dps[0], dps[1], dps[2], dgp, jnp.sum(dgp, axis=0, keepdims=True)

    dp0, dp1, dp2, dgp, dbm = _rowwise("mix_dmerge", dmerge, [dmerged, sv['gp'], *sv['p']], [w['b_merge_gate']],
                                       [(D_MODEL, BF16)] * 3 + [(3 * D_MODEL, BF16)], accs=[(1, 3 * D_MODEL)])
    grads['b_merge_gate'] = dbm[0]
    grads['w_branch_s5'] = _mm("mix_dbranch", sv['y_s5'], dp0, ta=True)
    grads['w_branch_gla'] = _mm("mix_dbranch", sv['y_gla'], dp1, ta=True)
    grads['w_branch_attn'] = _mm("mix_dbranch_f32", sv['y_at'], dp2, ta=True)
    dy_s5 = _mm("mix_dy", dp0, w['w_branch_s5'], tb=True)
    dy_gla = _mm("mix_dy", dp1, w['w_branch_gla'], tb=True)
    dy_at = _mm("mix_dy", dp2, w['w_branch_attn'], tb=True)
    grads['w_merge_gate'] = _mm("mix_dwgate", h, dgp, ta=True, out_dtype=GRAD_DT)
    u = _cw(z, 0, S5_W)

    def s5_post_bwd(dyv, y0, t, uv, dv, wglu):
        sg = _sigmoid(t)
        y1 = _gelu(y0)
        dt = dyv * y1 * sg * (1.0 - sg)
        dy1 = dyv * sg + _dot(dt.astype(BF16), wglu, _NT)
        dy0 = dy1 * _gelu_grad(y0)
        return dy0, dt, y1, dy0 * dv, jnp.sum(dy0 * uv, axis=0, keepdims=True)

    dy0, dt_glu, y1, du, dd = _rowwise("s5_post_bwd", s5_post_bwd, [dy_s5, sv['y0'], sv['t_glu'], u],
                                       [w['s5_d'], w['s5_w_glu']],
                                       [(S5_W, BF16), (S5_W, BF16), (S5_W, BF16), (S5_W, F32)], accs=[(1, S5_W)])
    grads['s5_d'] = dd[0]
    grads['s5_w_glu'] = _mm("s5_dwglu", y1, dt_glu, ta=True)
    s5_cot = []
    for d, rev in ((0, False), (1, True)):
        a_re, a_im, wb, wc = w['s5'][d]
        du, dwb, dwc, da = _s5_bwd("s5_adj_fwd" if rev else "s5_adj_rev", dy0, z, sv['xs'][d], a_re, -a_im, wb, wc, du,
                                   rev=not rev)
        da = jnp.sum(da, axis=0, keepdims=True)
        s5_cot.append((da[:, :S5_N], da[:, S5_N:], dwb, dwc))
    for d in range(2):
        _, pull = jax.vjp(_s5_prep, *s5_raw[d])
        s5_cot[d] = pull(s5_cot[d])
    for k, nm in enumerate(('s5_lambda_re', 's5_lambda_im', 's5_log_dt', 's5_b_re', 's5_b_im', 's5_c_re', 's5_c_im')):
        grads[nm] = jnp.stack([s5_cot[0][k], s5_cot[1][k]])

    gate = _cw(z, 4, GLA_W)

    def gla_post_bwd(dyv, of, ob, gv, gn):
        o, gv = of + ob, gv.astype(F32)
        dos, dgates = [], []
        dgn = jnp.zeros((1, GLA_D), F32)
        for hh in range(GLA_H):
            sl = slice(hh * GLA_D, (hh + 1) * GLA_D)
            gh, dyh = gv[:, sl], dyv[:, sl]
            sg = _sigmoid(gh)
            dgates.append(dyh * _rms(o[:, sl], gn) * sg * (1.0 + gh * (1.0 - sg)))
            dx, dg = _rms_bwd(o[:, sl], gn, dyh * gh * sg)
            dos.append(dx)
            dgn = dgn + dg
        return jnp.concatenate(dos, axis=1), jnp.concatenate(dgates, axis=1), dgn

    do, dgate, dgn = _rowwise("gla_post_bwd", gla_post_bwd, [dy_gla, *sv['o'], gate], [w['gla_norm']],
                              [(GLA_W, F32), (GLA_W, F32)], accs=[(1, GLA_D)])
    grads['gla_norm'] = dgn[0]
    dq_f, dk_f, dv_f, dla_f = _gla_bwd("gla_bwd_fwd", z, sv['la'][0], sv['s'][0], do, rev=False)
    dq_b, dk_b, dv_b, dla_b = _gla_bwd("gla_bwd_rev", z, sv['la'][1], sv['s'][1], do, rev=True)
    wa_f, wa_b = _alpha_pads(w['gla_w_alpha'])
    ba_f, ba_b = w['gla_b_alpha'][0:1], w['gla_b_alpha'][1:2]
    zz = _cw(z, ZC_Z, LANE)

    def gla_prep_bwd(dlf, dlb, zv, waf, wab, bf, bb):
        z16 = zv.astype(BF16)
        waf16, wab16 = waf.astype(BF16), wab.astype(BF16)
        xf = _dot(z16, waf16) + bf
        xb = _dot(z16, wab16) + bb
        df = dlf * (_sigmoid(-xf) / GLA_TAU)
        db = dlb * (_sigmoid(-xb) / GLA_TAU)
        dz = _dot(df.astype(BF16), waf16, _NT) + _dot(db.astype(BF16), wab16, _NT)
        return df, db, dz, jnp.sum(df, axis=0, keepdims=True), jnp.sum(db, axis=0, keepdims=True)

    dlog_f, dlog_b, dzz, dba_f, dba_b = _rowwise(
        "gla_prep_bwd", gla_prep_bwd, [dla_f, dla_b, zz], [wa_f, wa_b, ba_f, ba_b],
        [(GLA_W, BF16), (GLA_W, BF16), (LANE, F32)], accs=[(1, GLA_W), (1, GLA_W)])
    dwa_f = _mm("gla_dwa", zz, dlog_f, ta=True)
    dwa_b = _mm("gla_dwa", zz, dlog_b, ta=True)
    grads['gla_w_alpha'] = jnp.stack([dwa_f[:GLA_R], dwa_b[GLA_R:2 * GLA_R]])
    grads['gla_b_alpha'] = jnp.concatenate([dba_f, dba_b], axis=0)

    dq_r, dkd, dvd = _attn_bwd("attn_bwd", sv['q16'], sv['kd'], sv['vd'], sv['y_at'], sv['lse'], dy_at)
    cq, sq, mq, rq = _head_consts(n_tok, AT_QH)
    ck, sk, mk, rk = _head_consts(n_tok, AT_KH)
    gq = jnp.tile(w['attn_q_norm'], (1, AT_QH))
    gk = jnp.tile(w['attn_k_norm'], (1, AT_KH))

    def rope_bwd(dr, v, gain, cos, sin, mean, rot, scale):
        dr = dr * scale
        dvn = dr * cos + _dot(dr * sin, rot, _NT, precision=HI)
        r = lax.rsqrt(_dot(v * v, mean, precision=HI) + EPS)
        vh = v * r
        dg = jnp.sum(dvn * vh, axis=0, keepdims=True)
        dvh = dvn * gain
        return r * (dvh - vh * _dot(dvh * vh, mean, precision=HI)), dg

    def attn_prep_bwd(dq, dk2, dv2, qv, kv, cqv, sqv, ckv, skv, gqv, gkv, mqv, rqv, mkv, rkv):
        qv, kv = qv.astype(F32), kv.astype(F32)
        dqo, dgq = rope_bwd(dq, qv, gqv, cqv, sqv, mqv, rqv, AT_D ** -0.5)
        dko, dgk = rope_bwd(_fold_heads(dk2), kv, gkv, ckv, skv, mkv, rkv, 1.0)
        return dqo, dko, _fold_heads(dv2), dgq, dgk

    q_raw, k_raw = _cw(z, 5, AT_W), _cw(z, ZC_AK, AT_KW)
    daq, dak, dav, dgq, dgk = _rowwise("attn_prep_bwd", attn_prep_bwd, [dq_r, dkd, dvd, q_raw, k_raw, cq, sq, ck, sk],
                                       [gq, gk, mq, rq, mk, rk], [(AT_W, F32), (AT_KW, F32), (AT_KW, F32)],
                                       accs=[(1, AT_W), (1, AT_KW)])
    grads['attn_q_norm'] = dgq.reshape(AT_QH, AT_D).sum(axis=0)
    grads['attn_k_norm'] = dgk.reshape(AT_KH, AT_D).sum(axis=0)

    def assemble(duv, qf, qb, kf, kb, vf, vb, dg, aq, ak, av, zzv):
        return jnp.concatenate([duv, qf + qb, kf + kb, vf + vb, dg, aq, ak, av, zzv], axis=1)

    dz = _rowwise("mix_dz", assemble, [du, dq_f, dq_b, dk_f, dk_b, dv_f, dv_b, dgate, daq, dak, dav, dzz], [],
                  [(ZP_W, BF16)])[0]
    grads['w_in'] = _unpad_w_in(_mm("mix_dwin", h, dz, ta=True, out_dtype=GRAD_DT))
    wmg, wip = w['w_merge_gate'], w['w_in_p']
    tr = _tile(n_tok, 256, 16)

    def dh_norm(dgp_ref, dz_ref, wmg_ref, wip_ref, dy_ref, x_ref, g_ref, dx_ref, dx16_ref, dg_ref):
        dh = _dot(dgp_ref[...], wmg_ref[...], _NT) + _dot(dz_ref[...], wip_ref[...], _NT)
        dxv, dgv = _rms_bwd(x_ref[...], g_ref[...], dh)
        dxv = dy_ref[...] + dxv
        dx_ref[...] = dxv
        dx16_ref[...] = dxv.astype(BF16)

        @pl.when(pl.program_id(0) == 0)
        def _():
            dg_ref[...] = jnp.zeros(dg_ref.shape, F32)

        dg_ref[...] += dgv

    def rows(c):
        return pl.BlockSpec((tr, c), lambda i: (i, 0))

    def whole(t):
        return pl.BlockSpec(t.shape, lambda i: (0, 0))

    dx, dx16, dg = _pcall(
        dh_norm, name="mix_dh_norm", grid=(n_tok // tr,),
        in_specs=[rows(dgp.shape[1]), rows(dz.shape[1]), whole(wmg), whole(wip), rows(D_MODEL), rows(D_MODEL),
                  whole(w['mix_norm'])],
        out_specs=[rows(D_MODEL), rows(D_MODEL), whole(w['mix_norm'])],
        out_shape=[jax.ShapeDtypeStruct((n_tok, D_MODEL), F32), jax.ShapeDtypeStruct((n_tok, D_MODEL), BF16),
                   jax.ShapeDtypeStruct((1, D_MODEL), F32)],
        compiler_params=_cparams(("arbitrary",), 4 * (wmg.size + wip.size) + 8 * tr * (dgp.shape[1] + dz.shape[1])
                                 + 16 * tr * D_MODEL * 4),
    )(dgp, dz, wmg, wip, dy, sv['x'], w['mix_norm'])
    grads['mix_norm'] = dg[0]
    return dx, dx16, grads


FFN_NAMES = [p + k for p in ('ffn1_', 'ffn2_') for k in ('w_gate', 'w_up', 'w_down')]
MISC_SHARDED = [n for n in SHARDED if n not in FFN_NAMES]
LARGE_SHARDED = FFN_NAMES + ['w_in', 'w_merge_gate']
KEPT_TRANSPOSED = [p + k for p in ('ffn1_', 'ffn2_') for k in ('w_gate', 'w_up')]


def _stored(n, t):
    return jnp.swapaxes(t, 1, 2) if n in KEPT_TRANSPOSED else t


def _layer_weights(ws, rep, i):
    w = {k: _from_shards(ws[k][:, i], SHARD_AXIS[k] - 1) for k in MISC_SHARDED}
    for k in ('ffn1_norm', 'mix_norm', 'ffn2_norm', 's5_d', 'gla_norm', 'attn_q_norm', 'attn_k_norm', 'b_merge_gate'):
        w[k] = rep[k][i].reshape(1, -1)
    w['w_in_p'] = _pad_w_in(w['w_in'])
    s5_raw = [tuple(rep[k][i, d] for k in ('s5_lambda_re', 's5_lambda_im', 's5_log_dt', 's5_b_re', 's5_b_im',
                                            's5_c_re', 's5_c_im')) for d in range(2)]
    w['s5'] = [_s5_prep(*s5_raw[d]) for d in range(2)]
    return w, s5_raw


def _device_step(x, target, ws, rep):
    lw = [_layer_weights(ws, rep, i) for i in range(DEPTH)]
    ffn = {p: (ws[p + 'w_gate'], ws[p + 'w_up'], ws[p + 'w_down']) for p in ('ffn1_', 'ffn2_')}
    saved = []
    for i in range(DEPTH):
        w, _ = lw[i]
        x, s1 = _ffn_fwd(x, w['ffn1_norm'], *ffn['ffn1_'], i)
        x, s2 = _mixer_fwd(x, w)
        x, s3 = _ffn_fwd(x, w['ffn2_norm'], *ffn['ffn2_'], i)
        saved.append((s1, s2, s3))
    gfin = rep['final_norm'].reshape(1, -1)

    def head(xv, tv, gv):
        e = _rms(xv, gv) - tv
        dx, dg = _rms_bwd(xv, gv, e * (1.0 / D_MODEL))
        part = jnp.sum(e * e, axis=0, keepdims=True)
        return dx, dx, dg, part

    dx, dx16, dgfin, part = _rowwise("loss_head", head, [x, target], [gfin], [(D_MODEL, F32), (D_MODEL, BF16)],
                                     accs=[(1, D_MODEL), (1, D_MODEL)])
    loss = (0.5 / D_MODEL) * jnp.sum(part)
    per_layer = []
    dffn = {'ffn1_': None, 'ffn2_': None}
    for i in reversed(range(DEPTH)):
        w, s5_raw = lw[i]
        s1, s2, s3 = saved[i]
        dx, dx16, n3, dffn['ffn2_'] = _ffn_bwd(dx, dx16, s3, w['ffn2_norm'], *ffn['ffn2_'], i, dffn['ffn2_'])
        dx, dx16, g = _mixer_bwd(dx, dx16, s2, w, s5_raw)
        dx, dx16, n1, dffn['ffn1_'] = _ffn_bwd(dx, dx16, s1, w['ffn1_norm'], *ffn['ffn1_'], i, dffn['ffn1_'])
        g['ffn2_norm'], g['ffn1_norm'] = n3, n1
        per_layer.append(g)
    per_layer.reverse()
    stacked = {k: jnp.stack([per_layer[i][k] for i in range(DEPTH)]) for k in per_layer[0]}
    gs = {k: _to_shards(stacked[k], SHARD_AXIS[k]) for k in MISC_SHARDED}
    for p in ('ffn1_', 'ffn2_'):
        gs[p + 'w_gate'], gs[p + 'w_up'], gs[p + 'w_down'] = dffn[p]
    grep = {k: stacked[k] for k in REPLICATED if k != 'final_norm'}
    grep['final_norm'] = dgfin[0]
    return loss, dx, gs, grep


PACK_C = 1024
ROW_TILE = 512


def _rows_of(shape):
    return -(-int(np.prod(shape)) // PACK_C)


def _as_rows(t, lead=0):
    head = t.shape[:lead]
    flat = t.reshape(head + (-1,))
    rows = -(-flat.shape[-1] // PACK_C)
    pad = rows * PACK_C - flat.shape[-1]
    if pad:
        flat = jnp.pad(flat, [(0, 0)] * lead + [(0, pad)])
    return flat.reshape(head + (rows, PACK_C))


def _pack(items, total_rows, lead=0):
    parts = [_as_rows(t, lead) for t in items]
    used = sum(p.shape[lead] for p in parts)
    if total_rows > used:
        parts.append(jnp.zeros(parts[0].shape[:lead] + (total_rows - used, PACK_C), parts[0].dtype))
    return jnp.concatenate(parts, axis=lead)


def _unpack(rows, shapes, lead=0):
    out, r0 = [], 0
    head = rows.shape[:lead]
    for shp in shapes:
        n, size = _rows_of(shp), int(np.prod(shp))
        piece = lax.slice_in_dim(rows, r0, r0 + n, axis=lead).reshape(head + (n * PACK_C,))
        out.append(lax.slice_in_dim(piece, 0, size, axis=lead).reshape(head + tuple(shp)))
        r0 += n
    return out


def _round_up(n, m):
    return -(-n // m) * m


def _to_shards(g, axis):
    shp = g.shape
    g = g.reshape(shp[:axis] + (N_CHIPS, shp[axis] // N_CHIPS) + shp[axis + 1:])
    return jnp.moveaxis(g, axis, 0)


def _from_shards(s, axis):
    s = jnp.moveaxis(s, 0, axis)
    shp = s.shape
    return s.reshape(shp[:axis] + (shp[axis] * shp[axis + 1],) + shp[axis + 2:])


_HBM = pl.BlockSpec(memory_space=pltpu.HBM)


def _place():
    x, y, c = lax.axis_index("x"), lax.axis_index("y"), lax.axis_index("c")
    chips = [(1 - x, y), (x, 1 - y), (1 - x, 1 - y)]
    return x, y, c, chips


def _rcopy(src, dst, send, recv, k, dev):
    return pltpu.make_async_remote_copy(src_ref=src, dst_ref=dst, send_sem=send.at[k], recv_sem=recv.at[k],
                                        device_id=dev, device_id_type=MESH)


def _comm_call(name, body, arrays, out_shapes, n_sems, n_local):
    return pl.pallas_call(
        body, name=name, in_specs=[_HBM] * len(arrays), out_specs=[_HBM] * len(out_shapes), out_shape=out_shapes,
        scratch_shapes=[pltpu.SemaphoreType.DMA((n_sems,)), pltpu.SemaphoreType.DMA((n_sems,)),
                        pltpu.SemaphoreType.DMA((max(n_local, 1),))],
    )(*arrays)


def _ici_peers(x, y, c):
    x_first = c == 0
    first = (jnp.where(x_first, 1 - x, x), jnp.where(x_first, y, 1 - y))
    second = (jnp.where(x_first, x, 1 - x), jnp.where(x_first, 1 - y, y))
    return first, second


def _ag_chips(name, packs):
    n = len(packs)

    def body(*refs):
        p, o, (send, recv, _) = refs[:n], refs[n:2 * n], refs[2 * n:]
        x, y, c, _ = _place()
        (ax, ay), (bx, by) = _ici_peers(x, y, c)
        me, na, nb, nd = 2 * x + y, 2 * ax + ay, 2 * bx + by, 2 * (1 - x) + (1 - y)
        sib = (x, y, 1 - c)
        own = [_rcopy(p[a].at[c], o[a].at[me, c], send, recv, 6 * a, (ax, ay, c)) for a in range(n)]
        own += [_rcopy(p[a].at[c], o[a].at[me, c], send, recv, 6 * a + 1, (bx, by, c)) for a in range(n)]
        for cp in own:
            cp.start()
        passed = []

        def landed(a, chip, k, fwd):
            blk = o[a].at[chip, c]
            _rcopy(blk, blk, send, recv, 6 * a + k, sib).wait_recv()
            for sem, dev in fwd:
                cp = _rcopy(blk, blk, send, recv, 6 * a + sem, dev)
                cp.start()
                passed.append(cp)

        for a in range(n):
            landed(a, na, 0, [(2, (bx, by, c)), (3, sib)])
        for a in range(n):
            landed(a, nb, 1, [(4, sib)])
        for a in range(n):
            landed(a, nd, 2, [(5, sib)])
        for a in range(n):
            for k, chip in ((3, nb), (4, na), (5, nd)):
                blk = o[a].at[chip, 1 - c]
                _rcopy(blk, blk, send, recv, 6 * a + k, sib).wait_recv()
        for cp in own + passed:
            cp.wait_send()

    outs = _comm_call(name, body, packs, [jax.ShapeDtypeStruct((N_CHIPS,) + t.shape, t.dtype) for t in packs], 6 * n, 0)
    chip = 2 * lax.axis_index("x") + lax.axis_index("y")
    return [lax.dynamic_update_slice_in_dim(o, t[None], chip, axis=0) for o, t in zip(outs, packs)]


def _pair_swap(name, gs):
    n = len(gs)
    ns = gs[0].shape[0]

    def body(*refs):
        g, r, (send, recv, _) = refs[:n], refs[n:2 * n], refs[2 * n:]
        x, y, c, _ = _place()
        cps = [_rcopy(g[a].at[j, 1 - c], r[a].at[j], send, recv, ns * a + j, (x, y, 1 - c))
               for a in range(n) for j in range(ns)]
        for cp in cps:
            cp.start()
        for cp in cps:
            cp.wait_recv()
        for cp in cps:
            cp.wait_send()

    return _comm_call(name, body, gs, [jax.ShapeDtypeStruct((ns,) + t.shape[2:], t.dtype) for t in gs], ns * n, 0)


def _halve_first(name, parts):
    n = len(parts)

    def body(*refs):
        p, r, (send, recv, _) = refs[:n], refs[n:2 * n], refs[2 * n:]
        x, y, c, _ = _place()
        (ax, ay), _ = _ici_peers(x, y, c)
        cps = [_rcopy(p[a].at[jnp.where(c == 0, 2 * ax + k, 2 * k + ay)], r[a].at[k], send, recv, 2 * a + k, (ax, ay, c))
               for a in range(n) for k in range(2)]
        for cp in cps:
            cp.start()
        for cp in cps:
            cp.wait_recv()
        for cp in cps:
            cp.wait_send()

    return _comm_call(name, body, parts, [jax.ShapeDtypeStruct((2,) + t.shape[1:], t.dtype) for t in parts], 2 * n, 0)


def _halve_second(name, kept):
    n = len(kept)

    def body(*refs):
        p, r, (send, recv, _) = refs[:n], refs[n:2 * n], refs[2 * n:]
        x, y, c, _ = _place()
        _, (bx, by) = _ici_peers(x, y, c)
        cps = [_rcopy(p[a].at[jnp.where(c == 0, by, bx)], r[a], send, recv, a, (bx, by, c)) for a in range(n)]
        for cp in cps:
            cp.start()
        for cp in cps:
            cp.wait_recv()
        for cp in cps:
            cp.wait_send()

    return _comm_call(name, body, kept, [jax.ShapeDtypeStruct(t.shape[1:], t.dtype) for t in kept], n, 0)


def _add_selected(name, t, r, index, out_dtype):
    single = r.ndim == 2
    r3 = r[None] if single else r
    nk, rh, cc = r3.shape
    tl = _tile(rh, ROW_TILE, 16)

    def body(i_ref, t_ref, r_ref, o_ref):
        o_ref[...] = (t_ref[...].astype(F32) + r_ref[...].astype(F32)).astype(o_ref.dtype)

    out = _pcall(
        body, name=name,
        grid_spec=pltpu.PrefetchScalarGridSpec(
            num_scalar_prefetch=1, grid=(nk, rh // tl),
            in_specs=[pl.BlockSpec((1, tl, cc), lambda k, i, i_ref: (i_ref[k], i, 0)),
                      pl.BlockSpec((1, tl, cc), lambda k, i, i_ref: (k, i, 0))],
            out_specs=pl.BlockSpec((1, tl, cc), lambda k, i, i_ref: (k, i, 0))),
        out_shape=jax.ShapeDtypeStruct(r3.shape, out_dtype),
        compiler_params=_cparams(("parallel", "parallel"), 8 * tl * cc * 4),
    )(index, t, r3)
    return out[0] if single else out


def _pair_share(name, reds):
    n = len(reds)

    def body(*refs):
        r, o, (send, recv, _) = refs[:n], refs[n:2 * n], refs[2 * n:]
        x, y, c, _ = _place()
        cps = [_rcopy(r[a], o[a].at[c], send, recv, a, (x, y, 1 - c)) for a in range(n)]
        for cp in cps:
            cp.start()
        for a in range(n):
            blk = o[a].at[1 - c]
            _rcopy(blk, blk, send, recv, a, (x, y, 1 - c)).wait_recv()
        for cp in cps:
            cp.wait_send()

    outs = _comm_call(name, body, reds, [jax.ShapeDtypeStruct((2,) + t.shape, t.dtype) for t in reds], n, 0)
    core = lax.axis_index("c")
    return [lax.dynamic_update_slice_in_dim(o, t[None], core, axis=0) for o, t in zip(outs, reds)]


def _add_own_half(name, g, r):
    ns, _, rh, cc = g.shape
    tl = _tile(rh, ROW_TILE, 16)
    core = lax.axis_index("c").astype(jnp.int32).reshape(1)

    def body(c_ref, g_ref, r_ref, o_ref):
        o_ref[...] = (g_ref[0].astype(F32) + r_ref[...].astype(F32)).astype(o_ref.dtype)

    return _pcall(
        body, name=name,
        grid_spec=pltpu.PrefetchScalarGridSpec(
            num_scalar_prefetch=1, grid=(ns, rh // tl),
            in_specs=[pl.BlockSpec((1, 1, tl, cc), lambda j, i, c_ref: (j, c_ref[0], i, 0)),
                      pl.BlockSpec((1, tl, cc), lambda j, i, c_ref: (j, i, 0))],
            out_specs=pl.BlockSpec((1, tl, cc), lambda j, i, c_ref: (j, i, 0))),
        out_shape=jax.ShapeDtypeStruct(r.shape, BF16),
        compiler_params=_cparams(("parallel", "parallel"), 8 * tl * cc * 4),
    )(core, g, r)


def _reduce_scatter(gs):
    x, y, c = lax.axis_index("x"), lax.axis_index("y"), lax.axis_index("c")
    mine_first = jnp.stack([jnp.where(c == 0, 2 * x + k, 2 * k + y) for k in range(2)]).astype(jnp.int32)
    mine_second = jnp.where(c == 0, y, x).astype(jnp.int32).reshape(1)
    swapped = _pair_swap("rs_pair_swap", gs)
    parts = [_add_own_half("rs_add_pair", g, r) for g, r in zip(gs, swapped)]
    got = _halve_first("rs_halve_first", parts)
    kept = [_add_selected("rs_add_first", t, r, mine_first, BF16) for t, r in zip(parts, got)]
    got = _halve_second("rs_halve_second", kept)
    reds = [_add_selected("rs_add_second", t, r, mine_second, F32) for t, r in zip(kept, got)]
    return _pair_share("rs_pair_share", reds)


def _adamw(name, g, w, m, v):
    c1 = 1.0 - ADAM_B1 ** ADAM_STEP
    c2 = 1.0 - ADAM_B2 ** ADAM_STEP
    nb, rows, cc = g.shape
    tl = _tile(rows, ROW_TILE, SUBLANE)

    def body(g_ref, w_ref, m_ref, v_ref, d_ref, mo_ref, vo_ref):
        gv = g_ref[...]
        mn = ADAM_B1 * m_ref[...] + (1.0 - ADAM_B1) * gv
        vn = ADAM_B2 * v_ref[...] + (1.0 - ADAM_B2) * (gv * gv)
        d_ref[...] = -ADAM_LR * ((mn / c1) / (jnp.sqrt(vn / c2) + ADAM_EPS) + ADAM_WD * w_ref[...])
        mo_ref[...] = mn
        vo_ref[...] = vn

    spec = pl.BlockSpec((1, tl, cc), lambda b, i: (b, i, 0))
    return _pcall(
        body, name=name, grid=(nb, rows // tl), in_specs=[spec] * 4, out_specs=[spec] * 3,
        out_shape=[jax.ShapeDtypeStruct(g.shape, F32)] * 3,
        compiler_params=_cparams(("parallel", "parallel"), 16 * tl * cc * 4),
    )(g, w, m, v)


def _train_step(a):
    x, target = a['x'][0], a['loss_target'][0]

    names = BF16_GATHER + F32_GATHER
    got = _ag_chips("ag_weights", [_stored(n, a[n]).astype(BF16) for n in BF16_GATHER] + [a[n] for n in F32_GATHER])
    ws = dict(zip(names, got))

    loss, gx, gs, grep = _device_step(x, target, ws, {n: a[n] for n in REPLICATED})
    loss = lax.psum(loss, ("x", "y", "c"))

    small = [n for n in MISC_SHARDED if n not in LARGE_SHARDED]
    small_shapes = [a[n].shape[1:] for n in small]
    rep_shapes = [a[n].shape for n in REPLICATED]
    small_rows = sum(_rows_of(s) for s in small_shapes)
    piece = _round_up(-(-sum(_rows_of(s) for s in rep_shapes) // (2 * N_CHIPS)), SUBLANE)
    rep_rows = 2 * N_CHIPS * piece
    rep_pack = _pack([grep[n] for n in REPLICATED], rep_rows).reshape(N_CHIPS, 2, piece, PACK_C)
    total = _round_up(small_rows + piece, ROW_TILE // 2)
    misc = _pack([gs[n] for n in small] + [rep_pack], total, lead=2)
    reds = _reduce_scatter([gs[n] for n in LARGE_SHARDED] + [misc])
    grad = dict(zip(LARGE_SHARDED, reds[:-1]))
    grad.update(zip(small, _unpack(reds[-1], small_shapes, lead=1)))
    rep_mine = reds[-1][:, small_rows:small_rows + piece]
    rep_all = _ag_chips("ag_replicated", [rep_mine])[0].reshape(1, rep_rows, PACK_C)

    outs = {}
    for n in SHARDED:
        shp = grad[n].shape
        three_d = (shp[0], -1, shp[-1])
        params = (grad[n], *(_stored(n, a[p + n]) for p in ('', 'm_', 'v_')))
        res3 = _adamw("adamw", *(t.reshape(three_d) for t in params))
        for kind, arr in zip(('grad', 'delta', 'new_m', 'new_v'), (grad[n], *res3)):
            outs[kind + '_' + n] = _stored(n, arr.reshape(shp))
    w, m, v = (_pack([a[p + n] for n in REPLICATED], rep_rows)[None] for p in ('', 'm_', 'v_'))
    res3 = _adamw("adamw_replicated", rep_all, w, m, v)
    for kind, arr in zip(('grad', 'delta', 'new_m', 'new_v'), (rep_all, *res3)):
        for n, t in zip(REPLICATED, _unpack(arr[0], rep_shapes)):
            outs[kind + '_' + n] = t
    res = [loss, gx[None]]
    for kind in ('grad', 'delta', 'new_m', 'new_v'):
        res += [outs[kind + '_' + n] for n in W_NAMES]
    return tuple(res)


def kernel(x, ffn1_norm, ffn1_w_gate, ffn1_w_up, ffn1_w_down, mix_norm, w_in, s5_lambda_re, s5_lambda_im, s5_log_dt, s5_b_re, s5_b_im, s5_c_re, s5_c_im, s5_d, s5_w_glu, gla_w_alpha, gla_b_alpha, gla_norm, attn_q_norm, attn_k_norm, w_branch_s5, w_branch_gla, w_branch_attn, w_merge_gate, b_merge_gate, w_out, ffn2_norm, ffn2_w_gate, ffn2_w_up, ffn2_w_down, final_norm, loss_target, m_ffn1_norm, m_ffn1_w_gate, m_ffn1_w_up, m_ffn1_w_down, m_mix_norm, m_w_in, m_s5_lambda_re, m_s5_lambda_im, m_s5_log_dt, m_s5_b_re, m_s5_b_im, m_s5_c_re, m_s5_c_im, m_s5_d, m_s5_w_glu, m_gla_w_alpha, m_gla_b_alpha, m_gla_norm, m_attn_q_norm, m_attn_k_norm, m_w_branch_s5, m_w_branch_gla, m_w_branch_attn, m_w_merge_gate, m_b_merge_gate, m_w_out, m_ffn2_norm, m_ffn2_w_gate, m_ffn2_w_up, m_ffn2_w_down, m_final_norm, v_ffn1_norm, v_ffn1_w_gate, v_ffn1_w_up, v_ffn1_w_down, v_mix_norm, v_w_in, v_s5_lambda_re, v_s5_lambda_im, v_s5_log_dt, v_s5_b_re, v_s5_b_im, v_s5_c_re, v_s5_c_im, v_s5_d, v_s5_w_glu, v_gla_w_alpha, v_gla_b_alpha, v_gla_norm, v_attn_q_norm, v_attn_k_norm, v_w_branch_s5, v_w_branch_gla, v_w_branch_attn, v_w_merge_gate, v_b_merge_gate, v_w_out, v_ffn2_norm, v_ffn2_w_gate, v_ffn2_w_up, v_ffn2_w_down, v_final_norm):
    return _train_step(dict(locals()))
```

```python
import functools
import math

import jax
import jax.numpy as jnp
import numpy as np
from jax import lax
from jax.experimental import pallas as pl
from jax.experimental.pallas import tpu as pltpu

F32 = jnp.float32
BF16 = jnp.bfloat16
HI = lax.Precision.HIGHEST

D_MODEL = 1024
D_FF = 2816
DEPTH = 2
EPS = 1e-6
S5_G, S5_H, S5_P = 32, 16, 64
S5_W = S5_G * S5_H
S5_N = S5_G * S5_P
S5_TG = 8
S5_TILES = S5_G // S5_TG
S5_TC, S5_TS = S5_TG * S5_H, S5_TG * S5_P
GLA_H, GLA_D = 4, 128
GLA_W = GLA_H * GLA_D
GLA_R = 16
GLA_TAU = 16.0
GLA_C = 64
AT_QH, AT_KH, AT_D = 8, 2, 64
AT_W = AT_QH * AT_D
AT_KW = AT_KH * AT_D
GRID_W = 64
ROPE_BASE = 10000.0
ZP_W = 3456
ZC_AK, ZC_AV, ZC_Z = 24, 25, 26

ADAM_LR, ADAM_B1, ADAM_B2, ADAM_EPS, ADAM_WD, ADAM_STEP = 0.001, 0.9, 0.999, 1e-08, 0.01, 10

GRAD_DT = BF16
VMEM_LIMIT_V7X = 56 * 1024 * 1024
LANE = 128
SUBLANE = 8

W_NAMES = ['ffn1_norm', 'ffn1_w_gate', 'ffn1_w_up', 'ffn1_w_down', 'mix_norm', 'w_in', 's5_lambda_re', 's5_lambda_im',
           's5_log_dt', 's5_b_re', 's5_b_im', 's5_c_re', 's5_c_im', 's5_d', 's5_w_glu', 'gla_w_alpha', 'gla_b_alpha',
           'gla_norm', 'attn_q_norm', 'attn_k_norm', 'w_branch_s5', 'w_branch_gla', 'w_branch_attn', 'w_merge_gate',
           'b_merge_gate', 'w_out', 'ffn2_norm', 'ffn2_w_gate', 'ffn2_w_up', 'ffn2_w_down', 'final_norm']
SHARD_AXIS = {'ffn1_w_gate': 2, 'ffn1_w_up': 2, 'ffn1_w_down': 1, 'w_in': 2, 's5_w_glu': 1, 'gla_w_alpha': 3,
              'gla_b_alpha': 2, 'w_branch_s5': 2, 'w_branch_gla': 2, 'w_branch_attn': 2, 'w_merge_gate': 2,
              'w_out': 1, 'ffn2_w_gate': 2, 'ffn2_w_up': 2, 'ffn2_w_down': 1}
F32_GATHER = ['gla_w_alpha', 'gla_b_alpha']
BF16_GATHER = [n for n in W_NAMES if n in SHARD_AXIS and n not in F32_GATHER]
SHARDED = [n for n in W_NAMES if n in SHARD_AXIS]
REPLICATED = [n for n in W_NAMES if n not in SHARD_AXIS]
N_CHIPS = 4
MESH = pl.DeviceIdType.MESH


def _tile(n, target, mult):
    best = None
    for t in range(mult, min(n, target) + 1, mult):
        if n % t == 0:
            best = t
    return best if best is not None else n


def _cparams(sem, vmem_bytes):
    limit = int(min(VMEM_LIMIT_V7X, max(32 * 1024 * 1024, vmem_bytes * 5 // 4)))
    return pltpu.CompilerParams(dimension_semantics=sem, vmem_limit_bytes=limit)


def _pcall(body, **kw):
    shapes = kw.pop('out_shape')
    many = isinstance(shapes, (list, tuple))
    pinned = [pltpu.HBM(t.shape, t.dtype) for t in (shapes if many else [shapes])]
    call = pl.pallas_call(body, out_shape=pinned if many else pinned[0], **kw)

    def run(*ops):
        return call(*[o if jnp.issubdtype(o.dtype, jnp.integer) else pltpu.with_memory_space_constraint(o, pltpu.HBM)
                      for o in ops])

    return run


def _cw(arr, cb, width):
    return (arr, cb, width)


def _win(a):
    if isinstance(a, tuple):
        return a
    return (a, 0, a.shape[-1])


def _nbytes(shape, dtype):
    return int(np.prod(shape)) * jnp.dtype(dtype).itemsize


def _rowwise(name, fn, rows, fulls, outs, accs=(), tl=256):
    rows = [_win(r) for r in rows]
    n_tok = rows[0][0].shape[0]
    tl = _tile(n_tok, tl, 16)
    nr, nf, no = len(rows), len(fulls), len(outs)

    def body(*refs):
        vals = [r[...] for r in refs[:nr + nf]]
        res = fn(*vals)
        if not isinstance(res, (tuple, list)):
            res = (res,)
        for k in range(no):
            o = refs[nr + nf + k]
            o[...] = res[k].astype(o.dtype)
        if accs:
            @pl.when(pl.program_id(0) == 0)
            def _():
                for k in range(len(accs)):
                    a = refs[nr + nf + no + k]
                    a[...] = jnp.zeros(a.shape, a.dtype)
            for k in range(len(accs)):
                a = refs[nr + nf + no + k]
                a[...] += res[no + k]

    in_specs = [pl.BlockSpec((tl, w), functools.partial(lambda i, cb: (i, cb), cb=cb)) for (_, cb, w) in rows]
    in_specs += [pl.BlockSpec(f.shape, functools.partial(lambda i, nd: (0,) * nd, nd=f.ndim)) for f in fulls]
    out_specs = [pl.BlockSpec((tl, c), lambda i: (i, 0)) for (c, _) in outs]
    out_specs += [pl.BlockSpec((r, c), lambda i: (0, 0)) for (r, c) in accs]
    out_shape = [jax.ShapeDtypeStruct((n_tok, c), dt) for (c, dt) in outs]
    out_shape += [jax.ShapeDtypeStruct((r, c), F32) for (r, c) in accs]
    vm = 2 * sum(_nbytes((tl, w), a.dtype) for (a, _, w) in rows)
    vm += 2 * sum(_nbytes(f.shape, f.dtype) for f in fulls)
    vm += 2 * sum(_nbytes((tl, c), dt) for (c, dt) in outs)
    vm += 8 * max([_nbytes((tl, w), F32) for (_, _, w) in rows] + [_nbytes((tl, c), F32) for (c, _) in outs])
    return _pcall(
        body, name=name, grid=(n_tok // tl,), in_specs=in_specs, out_specs=out_specs, out_shape=out_shape,
        compiler_params=_cparams(("arbitrary",) if accs else ("parallel",), vm),
    )(*[r[0] for r in rows], *fulls)


def _mm(name, a, b, *, ta=False, tb=False, out_dtype=F32, alpha=1.0, add=None, tm=512, tn=512):
    a_arr, a_cb, a_w = _win(a)
    b_arr, b_cb, b_w = _win(b)
    if ta:
        kdim, m = a_arr.shape[0], a_w
    else:
        m, kdim = a_arr.shape[0], a_w
    if tb:
        n, kb = b_arr.shape[0], b_w
    else:
        kb, n = b_arr.shape[0], b_w
    assert kdim == kb, (name, kdim, kb)
    if not ta and kdim <= 1024:
        tm = 2 * tm
    tm = _tile(m, tm, LANE if ta else 16)
    tn = _tile(n, tn, LANE)
    if ta:
        a_spec = pl.BlockSpec((kdim, tm), lambda i, j: (0, a_cb * (a_w // tm) + i))
        a_blk = (kdim, tm)
    else:
        a_spec = pl.BlockSpec((tm, kdim), lambda i, j: (i, a_cb))
        a_blk = (tm, kdim)
    if tb:
        b_spec = pl.BlockSpec((tn, kdim), lambda i, j: (j, b_cb))
        b_blk = (tn, kdim)
    else:
        b_spec = pl.BlockSpec((kdim, tn), lambda i, j: (0, b_cb * (b_w // tn) + j))
        b_blk = (kdim, tn)
    dims = (((0 if ta else 1,), (1 if tb else 0,)), ((), ()))
    has_add = add is not None

    def body(*refs):
        a_ref, b_ref = refs[0], refs[1]
        o_ref = refs[-1]
        acc = lax.dot_general(a_ref[...].astype(BF16), b_ref[...].astype(BF16), dims, preferred_element_type=F32)
        if alpha != 1.0:
            acc = acc * alpha
        if has_add:
            acc = acc + refs[2][...].astype(F32)
        o_ref[...] = acc.astype(o_ref.dtype)

    in_specs = [a_spec, b_spec]
    ops = [a_arr, b_arr]
    if has_add:
        in_specs.append(pl.BlockSpec((tm, tn), lambda i, j: (i, j)))
        ops.append(add)
    vm = 2 * (_nbytes(a_blk, a_arr.dtype) + _nbytes(b_blk, b_arr.dtype) + _nbytes((tm, tn), out_dtype))
    vm += _nbytes(a_blk, BF16) + _nbytes(b_blk, BF16) + 3 * _nbytes((tm, tn), F32)
    return _pcall(
        body, name=name, grid=(m // tm, n // tn), in_specs=in_specs,
        out_specs=pl.BlockSpec((tm, tn), lambda i, j: (i, j)),
        out_shape=jax.ShapeDtypeStruct((m, n), out_dtype),
        compiler_params=_cparams(("parallel", "parallel"), vm),
    )(*ops)


def _sigmoid(x):
    return 1.0 / (1.0 + jnp.exp(-x))


def _rms(x, gain):
    r = lax.rsqrt(jnp.mean(x * x, axis=-1, keepdims=True) + EPS)
    return x * r * gain


def _rms_bwd(x, gain, dh):
    r = lax.rsqrt(jnp.mean(x * x, axis=-1, keepdims=True) + EPS)
    xh = x * r
    dg = jnp.sum(dh * xh, axis=0, keepdims=True)
    dxh = dh * gain
    dx = r * (dxh - xh * jnp.mean(dxh * xh, axis=-1, keepdims=True))
    return dx, dg


_GELU_C = math.sqrt(2.0 / math.pi)


def _gelu(y):
    return 0.5 * y * (1.0 + jnp.tanh(_GELU_C * (y + 0.044715 * y * y * y)))


def _gelu_grad(y):
    th = jnp.tanh(_GELU_C * (y + 0.044715 * y * y * y))
    return 0.5 * (1.0 + th) + 0.5 * y * (1.0 - th * th) * _GELU_C * (1.0 + 3.0 * 0.044715 * y * y)


def _log_sigmoid(x):
    return jnp.minimum(x, 0.0) - jnp.log(1.0 + jnp.exp(-jnp.abs(x)))


def _dot(a, b, dims=(((1,), (0,)), ((), ())), precision=None):
    return lax.dot_general(a, b, dims, preferred_element_type=F32, precision=precision)


_NT = (((1,), (1,)), ((), ()))
_TN = (((0,), (0,)), ((), ()))


def _scan_core(b_ref, are_ref, aim_ref, pw, last, of32, *, rev, xf32=None, da_ref=None, lc=512):
    nb, nc = b_ref.shape[0] // SUBLANE, S5_N // lc
    row = lax.broadcasted_iota(jnp.int32, (SUBLANE, lc), 0)

    def cmul(ar, ai, xr, xi):
        return ar * xr - ai * xi, ar * xi + ai * xr

    def hs(c, xr, xi):
        cs = pl.ds(c * lc, lc)
        for k, slot in ((1, 0), (2, 2), (4, 4)):
            ar, ai = pw[slot, :, cs], pw[slot + 1, :, cs]
            if rev:
                sr, si = pltpu.roll(xr, SUBLANE - k, 0), pltpu.roll(xi, SUBLANE - k, 0)
                keep = row < SUBLANE - k
            else:
                sr, si = pltpu.roll(xr, k, 0), pltpu.roll(xi, k, 0)
                keep = row >= k
            sr, si = jnp.where(keep, sr, 0.0), jnp.where(keep, si, 0.0)
            pr, pi = cmul(ar, ai, sr, si)
            xr, xi = xr + pr, xi + pi
        return xr, xi

    @pl.when(pl.program_id(0) == 0)
    def _():
        last[...] = jnp.zeros(last.shape, F32)
        if da_ref is not None:
            da_ref[...] = jnp.zeros(da_ref.shape, F32)
        for c in range(nc):
            cs = pl.ds(c * lc, lc)
            a1r = jnp.broadcast_to(are_ref[:, cs], (SUBLANE, lc))
            a1i = jnp.broadcast_to(aim_ref[:, cs], (SUBLANE, lc))
            a2r, a2i = cmul(a1r, a1i, a1r, a1i)
            a4r, a4i = cmul(a2r, a2i, a2r, a2i)
            for slot, v in enumerate((a1r, a1i, a2r, a2i, a4r, a4i)):
                pw[slot, :, cs] = v
            first = SUBLANE - 1 if rev else 0
            pr, pi = hs(c, jnp.where(row == first, a1r, 0.0), jnp.where(row == first, a1i, 0.0))
            pw[6, :, cs] = pr
            pw[7, :, cs] = pi

    edge = 0 if rev else SUBLANE - 1

    def blk(j, carry):
        jj = (nb - 1 - j) if rev else j
        r0 = pl.multiple_of(jj * SUBLANE, SUBLANE)
        for c in range(nc):
            cre, cim = pl.ds(c * lc, lc), pl.ds(S5_N + c * lc, lc)
            xr, xi = hs(c, b_ref[pl.ds(r0, SUBLANE), cre], b_ref[pl.ds(r0, SUBLANE), cim])
            cr = jnp.broadcast_to(last[edge:edge + 1, cre], (SUBLANE, lc))
            ci = jnp.broadcast_to(last[edge:edge + 1, cim], (SUBLANE, lc))
            pr, pi = cmul(pw[6, :, cre], pw[7, :, cre], cr, ci)
            xr, xi = xr + pr, xi + pi
            if da_ref is not None:
                if rev:
                    qr, qi = pltpu.roll(xr, SUBLANE - 1, 0), pltpu.roll(xi, SUBLANE - 1, 0)
                    fill = row == SUBLANE - 1
                else:
                    qr, qi = pltpu.roll(xr, 1, 0), pltpu.roll(xi, 1, 0)
                    fill = row == 0
                qr, qi = jnp.where(fill, cr, qr), jnp.where(fill, ci, qi)
                zr, zi = xf32[pl.ds(r0, SUBLANE), cre], xf32[pl.ds(r0, SUBLANE), cim]
                da_ref[:, cre] += zr * qr + zi * qi
                da_ref[:, cim] += zr * qi - zi * qr
            last[:, cre] = xr
            last[:, cim] = xi
            of32[pl.ds(r0, SUBLANE), cre] = xr
            of32[pl.ds(r0, SUBLANE), cim] = xi
        return carry

    lax.fori_loop(0, nb, blk, 0)


def _s5_tiles():
    return [(slice((j % S5_TILES) * S5_TC, (j % S5_TILES + 1) * S5_TC), slice(j * S5_TS, (j + 1) * S5_TS))
            for j in range(2 * S5_TILES)]


def _s5_scratch(tb):
    return [pltpu.VMEM((8, SUBLANE, S5_N), F32), pltpu.VMEM((SUBLANE, 2 * S5_N), F32),
            pltpu.VMEM((tb, 2 * S5_N), F32), pltpu.VMEM((tb, 2 * S5_N), F32),
            pltpu.VMEM((S5_TC, 2 * S5_N), BF16), pltpu.VMEM((2 * S5_N, S5_TC), BF16)]


def _s5_fwd(name, z, a_re, a_im, wb, wc, add, *, rev, tb=512):
    n_tok = z.shape[0]
    tb = _tile(n_tok, tb, 16)
    nt = n_tok // tb
    has_add = add is not None

    def body(*refs):
        u_ref, are_ref, aim_ref, wb_ref, wc_ref = refs[:5]
        x_ref, y_ref, pw, last, of32, bu, wb16, wc16 = refs[5 + has_add:]

        @pl.when(pl.program_id(0) == 0)
        def _():
            wb16[...] = wb_ref[...].astype(BF16)
            wc16[...] = wc_ref[...].astype(BF16)

        u16 = u_ref[...].astype(BF16)
        for ch, st in _s5_tiles():
            bu[:, st] = _dot(u16[:, ch], wb16[:, st])
        _scan_core(bu, are_ref, aim_ref, pw, last, of32, rev=rev)
        x16 = of32[...].astype(BF16)
        x_ref[...] = x16
        ys = []
        for k in range(S5_TILES):
            re, im = slice(k * S5_TS, (k + 1) * S5_TS), slice(S5_N + k * S5_TS, S5_N + (k + 1) * S5_TS)
            ys.append(_dot(x16[:, re], wc16[re, :]) + _dot(x16[:, im], wc16[im, :]))
        y = jnp.concatenate(ys, axis=1)
        y_ref[...] = y + refs[5][...] if has_add else y

    tmap = (lambda t: (nt - 1 - t, 0)) if rev else (lambda t: (t, 0))
    const = lambda t: (0, 0)
    in_specs = [pl.BlockSpec((tb, S5_W), tmap), pl.BlockSpec((1, S5_N), const), pl.BlockSpec((1, S5_N), const),
                pl.BlockSpec((S5_TC, 2 * S5_N), const), pl.BlockSpec((2 * S5_N, S5_TC), const)]
    ops = [z, a_re, a_im, wb, wc]
    if has_add:
        in_specs.append(pl.BlockSpec((tb, S5_W), tmap))
        ops.append(add)
    return _pcall(
        body, name=name, grid=(nt,), in_specs=in_specs,
        out_specs=[pl.BlockSpec((tb, 2 * S5_N), tmap), pl.BlockSpec((tb, S5_W), tmap)],
        out_shape=[jax.ShapeDtypeStruct((n_tok, 2 * S5_N), BF16), jax.ShapeDtypeStruct((n_tok, S5_W), F32)],
        scratch_shapes=_s5_scratch(tb), compiler_params=_cparams(("arbitrary",), 10 * tb * 2 * S5_N * 4),
    )(*ops)


def _s5_bwd(name, dy, z, xs, a_re, a_im, wb, wc, add, *, rev, tb=256):
    n_tok = z.shape[0]
    tb = _tile(n_tok, tb, 16)
    nt = n_tok // tb

    def body(dy_ref, u_ref, xs_ref, are_ref, aim_ref, wb_ref, wc_ref, add_ref, du_ref, dwb_ref, dwc_ref, da_ref,
             pw, last, of32, gx, wb16, wc16, xf32):
        @pl.when(pl.program_id(0) == 0)
        def _():
            wb16[...] = wb_ref[...].astype(BF16)
            wc16[...] = wc_ref[...].astype(BF16)
            dwb_ref[...] = jnp.zeros(dwb_ref.shape, F32)
            dwc_ref[...] = jnp.zeros(dwc_ref.shape, F32)

        dy16, u16, x16 = dy_ref[...].astype(BF16), u_ref[...].astype(BF16), xs_ref[...]
        for ch, st in _s5_tiles():
            gx[:, st] = _dot(dy16[:, ch], wc16[st, :], _NT)
        xf32[...] = x16.astype(F32)
        _scan_core(gx, are_ref, aim_ref, pw, last, of32, rev=rev, xf32=xf32, da_ref=da_ref)
        g16 = of32[...].astype(BF16)
        dus = []
        for k in range(S5_TILES):
            re, im = slice(k * S5_TS, (k + 1) * S5_TS), slice(S5_N + k * S5_TS, S5_N + (k + 1) * S5_TS)
            dus.append(_dot(g16[:, re], wb16[:, re], _NT) + _dot(g16[:, im], wb16[:, im], _NT))
        du_ref[...] = jnp.concatenate(dus, axis=1) + add_ref[...]
        for ch, st in _s5_tiles():
            dwb_ref[:, st] += _dot(u16[:, ch], g16[:, st], _TN)
            dwc_ref[st, :] += _dot(x16[:, st], dy16[:, ch], _TN)

    tmap = (lambda t: (nt - 1 - t, 0)) if rev else (lambda t: (t, 0))
    const = lambda t: (0, 0)
    row_w, row_s = pl.BlockSpec((tb, S5_W), tmap), pl.BlockSpec((tb, 2 * S5_N), tmap)
    return _pcall(
        body, name=name, grid=(nt,),
        in_specs=[row_w, row_w, row_s, pl.BlockSpec((1, S5_N), const), pl.BlockSpec((1, S5_N), const),
                  pl.BlockSpec((S5_TC, 2 * S5_N), const), pl.BlockSpec((2 * S5_N, S5_TC), const), row_w],
        out_specs=[row_w, pl.BlockSpec((S5_TC, 2 * S5_N), const), pl.BlockSpec((2 * S5_N, S5_TC), const),
                   pl.BlockSpec((SUBLANE, 2 * S5_N), const)],
        out_shape=[jax.ShapeDtypeStruct((n_tok, S5_W), F32), jax.ShapeDtypeStruct((S5_TC, 2 * S5_N), F32),
                   jax.ShapeDtypeStruct((2 * S5_N, S5_TC), F32), jax.ShapeDtypeStruct((SUBLANE, 2 * S5_N), F32)],
        scratch_shapes=_s5_scratch(tb) + [pltpu.VMEM((tb, 2 * S5_N), F32)],
        compiler_params=_cparams(("arbitrary",), 12 * tb * 2 * S5_N * 4),
    )(dy, z, xs, a_re, a_im, wb, wc, add)


GLA_TB = 512


def _gla_block_terms(la, q, k, rev):
    trif = _chunk_mask(rev).astype(F32)
    chunks = [la[t:t + GLA_C] for t in range(0, la.shape[0], GLA_C)]
    bcum = jnp.concatenate([_dot(trif, c, precision=HI) for c in chunks], axis=0)
    blast = _per_chunk_sum(la)
    eb, enb, ee = jnp.exp(bcum), jnp.exp(-bcum), jnp.exp(blast - bcum)
    return trif, jnp.exp(blast), eb, enb, ee, q * eb, k * enb, k * ee


def _per_chunk_sum(t):
    return jnp.concatenate([jnp.broadcast_to(jnp.sum(t[r:r + GLA_C], axis=0, keepdims=True), (GLA_C, t.shape[1]))
                            for r in range(0, t.shape[0], GLA_C)], axis=0)


def _chunk_mask(rev):
    ri = lax.broadcasted_iota(jnp.int32, (GLA_C, GLA_C), 0)
    ci = lax.broadcasted_iota(jnp.int32, (GLA_C, GLA_C), 1)
    return (ri <= ci) if rev else (ri >= ci)


def _gla_fwd(name, z, la, *, rev):
    n_tok = z.shape[0]
    tb = _tile(n_tok, GLA_TB, GLA_C)
    nblk, per = n_tok // tb, tb // GLA_C
    scale = GLA_D ** -0.5

    def body(q_ref, k_ref, v_ref, la_ref, o_ref, s_ref, st, qd_s, kd_s, ke_s, v_s, dec_s):
        @pl.when(pl.program_id(0) == 0)
        def _():
            st[...] = jnp.zeros(st.shape, F32)

        _, dec, _, _, _, qd, kd, ke = _gla_block_terms(la_ref[...], q_ref[...].astype(F32) * scale,
                                                                 k_ref[...].astype(F32), rev)
        qd_s[...], kd_s[...], ke_s[...] = qd.astype(BF16), kd.astype(BF16), ke.astype(BF16)
        v_s[...] = v_ref[...].astype(BF16)
        dec_s[...] = dec
        tri = _chunk_mask(rev)
        for step in range(per):
            t = per - 1 - step if rev else step
            rows = pl.ds(t * GLA_C, GLA_C)
            for h in range(GLA_H):
                lanes = pl.ds(h * GLA_D, GLA_D)
                qd16, v16 = qd_s[rows, lanes], v_s[rows, lanes]
                sc = jnp.where(tri, _dot(qd16, kd_s[rows, lanes], _NT), 0.0)
                s16 = st[h].astype(BF16)
                s_ref[h, step] = s16
                o_ref[rows, lanes] = _dot(sc.astype(BF16), v16) + _dot(qd16, s16, _NT)
                st[h] = st[h] * dec_s[pl.ds(t * GLA_C, 1), lanes] + _dot(v16, ke_s[rows, lanes], _TN)

    bmap = (lambda n: nblk - 1 - n) if rev else (lambda n: n)

    def zspec(cb):
        return pl.BlockSpec((tb, GLA_W), lambda n: (bmap(n), cb))

    half = pltpu.VMEM((tb, GLA_W), BF16)
    return _pcall(
        body, name=name, grid=(nblk,),
        in_specs=[zspec(1), zspec(2), zspec(3), zspec(0)],
        out_specs=[zspec(0), pl.BlockSpec((GLA_H, per, GLA_D, GLA_D), lambda n: (0, n, 0, 0))],
        out_shape=[jax.ShapeDtypeStruct((n_tok, GLA_W), F32),
                   jax.ShapeDtypeStruct((GLA_H, n_tok // GLA_C, GLA_D, GLA_D), BF16)],
        scratch_shapes=[pltpu.VMEM((GLA_H, GLA_D, GLA_D), F32), half, half, half, half, pltpu.VMEM((tb, GLA_W), F32)],
        compiler_params=_cparams(("arbitrary",), 40 * tb * GLA_W * 4),
    )(z, z, z, la)


def _gla_bwd(name, z, la, s_prev, do, *, rev):
    n_tok = z.shape[0]
    tb = _tile(n_tok, GLA_TB, GLA_C)
    nblk, per = n_tok // tb, tb // GLA_C
    scale = GLA_D ** -0.5

    def body(q_ref, k_ref, v_ref, la_ref, s_ref, do_ref, dq_ref, dk_ref, dv_ref, dla_ref,
             dst, qd_s, kd_s, ke_s, v_s, do_s, dec_s, dqd_s, dkd_s, dke_s, dd_s):
        @pl.when(pl.program_id(0) == 0)
        def _():
            dst[...] = jnp.zeros(dst.shape, F32)

        trif, dec, eb, enb, ee, qd, kd, ke = _gla_block_terms(la_ref[...], q_ref[...].astype(F32) * scale,
                                                                 k_ref[...].astype(F32), rev)
        qd_s[...], kd_s[...], ke_s[...] = qd.astype(BF16), kd.astype(BF16), ke.astype(BF16)
        v_s[...] = v_ref[...].astype(BF16)
        do_s[...] = do_ref[...].astype(BF16)
        dec_s[...] = dec
        tri = _chunk_mask(rev)
        for step in range(per):
            t = step if rev else per - 1 - step
            rows = pl.ds(t * GLA_C, GLA_C)
            for h in range(GLA_H):
                lanes = pl.ds(h * GLA_D, GLA_D)
                qd16, kd16, ke16 = qd_s[rows, lanes], kd_s[rows, lanes], ke_s[rows, lanes]
                v16, do16 = v_s[rows, lanes], do_s[rows, lanes]
                sc16 = jnp.where(tri, _dot(qd16, kd16, _NT), 0.0).astype(BF16)
                dsc16 = jnp.where(tri, _dot(do16, v16, _NT), 0.0).astype(BF16)
                s16 = s_ref[h, per - 1 - step]
                dsn = dst[h]
                dsn16 = dsn.astype(BF16)
                dec_c = dec_s[pl.ds(t * GLA_C, 1), lanes]
                dqd_s[rows, lanes] = _dot(dsc16, kd16) + _dot(do16, s16)
                dkd_s[rows, lanes] = _dot(dsc16, qd16, _TN)
                dke_s[rows, lanes] = _dot(v16, dsn16)
                dv_ref[rows, lanes] = _dot(sc16, do16, _TN) + _dot(ke16, dsn16, _NT)
                ddec = jnp.sum(dsn * s16.astype(F32), axis=0, keepdims=True) * dec_c
                dd_s[rows, lanes] = jnp.broadcast_to(ddec, (GLA_C, GLA_D))
                dst[h] = dsn * dec_c + _dot(do16, qd16, _TN)
        dqd, dkd, dke = dqd_s[...], dkd_s[...], dke_s[...]
        dq_ref[...] = dqd * eb * scale
        dk_ref[...] = dkd * enb + dke * ee
        db = dqd * qd - dkd * kd - dke * ke
        dcum = jnp.concatenate([_dot(trif, db[r:r + GLA_C], _TN, precision=HI) for r in range(0, tb, GLA_C)], axis=0)
        dla_ref[...] = dcum + _per_chunk_sum(dke * ke) + dd_s[...]

    bmap = (lambda n: n) if rev else (lambda n: nblk - 1 - n)

    def zspec(cb):
        return pl.BlockSpec((tb, GLA_W), lambda n: (bmap(n), cb))

    o4 = jax.ShapeDtypeStruct((n_tok, GLA_W), F32)
    half, full = pltpu.VMEM((tb, GLA_W), BF16), pltpu.VMEM((tb, GLA_W), F32)
    return _pcall(
        body, name=name, grid=(nblk,),
        in_specs=[zspec(1), zspec(2), zspec(3), zspec(0),
                  pl.BlockSpec((GLA_H, per, GLA_D, GLA_D), lambda n: (0, nblk - 1 - n, 0, 0)), zspec(0)],
        out_specs=[zspec(0)] * 4, out_shape=[o4] * 4,
        scratch_shapes=[pltpu.VMEM((GLA_H, GLA_D, GLA_D), F32), half, half, half, half, half, full, full, full, full, full],
        compiler_params=_cparams(("arbitrary",), 64 * tb * GLA_W * 4),
    )(z, z, z, la, s_prev, do)


_PAIRS = AT_W // LANE
_PAIRS_PER_KV = _PAIRS // AT_KH


def _attn_fwd(name, q, kd, vd, tq=256):
    n_tok = q.shape[0]
    tq = _tile(n_tok, tq, 16)

    def body(q_ref, k_ref, v_ref, o_ref, l_ref):
        qv, kv, vv = q_ref[...], k_ref[...], v_ref[...]
        low = lax.broadcasted_iota(jnp.int32, qv.shape, 1) < AT_D
        res = []
        for keep in (low, jnp.logical_not(low)):
            s = _dot(jnp.where(keep, qv, jnp.zeros_like(qv)), kv, _NT)
            m = jnp.max(s, axis=-1, keepdims=True)
            p = jnp.exp(s - m)
            den = jnp.sum(p, axis=-1, keepdims=True)
            res.append((_dot(p.astype(BF16), vv) / den, m + jnp.log(den)))
        o_ref[...] = jnp.where(low, res[0][0], res[1][0])
        l_ref[...] = jnp.where(low, res[0][1], res[1][1])

    qspec = pl.BlockSpec((tq, LANE), lambda p, i: (i, p))
    kspec = pl.BlockSpec((n_tok, LANE), lambda p, i: (0, p // _PAIRS_PER_KV))
    vm = 5 * _nbytes((tq, n_tok), F32) + 8 * _nbytes((n_tok, LANE), BF16)
    out = jax.ShapeDtypeStruct((n_tok, AT_W), F32)
    return _pcall(
        body, name=name, grid=(_PAIRS, n_tok // tq), in_specs=[qspec, kspec, kspec], out_specs=[qspec, qspec],
        out_shape=[out, out], compiler_params=_cparams(("parallel", "parallel"), vm),
    )(q, kd, vd)


def _attn_bwd(name, q, kd, vd, o, lse, do, tq=256):
    n_tok = q.shape[0]
    tq = _tile(n_tok, tq, 16)

    def body(q_ref, k_ref, v_ref, o_ref, l_ref, do_ref, dq_ref, dk_ref, dv_ref):
        @pl.when((pl.program_id(1) == 0) & (pl.program_id(2) == 0))
        def _():
            dk_ref[...] = jnp.zeros(dk_ref.shape, F32)
            dv_ref[...] = jnp.zeros(dv_ref.shape, F32)

        qv, kv, vv = q_ref[...], k_ref[...], v_ref[...]
        dof, lv = do_ref[...], l_ref[...]
        do16 = dof.astype(BF16)
        doo = dof * o_ref[...]
        low = lax.broadcasted_iota(jnp.int32, qv.shape, 1) < AT_D
        dq = jnp.zeros(qv.shape, F32)
        for keep in (low, jnp.logical_not(low)):
            qm = jnp.where(keep, qv, jnp.zeros_like(qv))
            dom = jnp.where(keep, do16, jnp.zeros_like(do16))
            lh = jnp.max(jnp.where(keep, lv, -1e30), axis=-1, keepdims=True)
            p = jnp.exp(_dot(qm, kv, _NT) - lh)
            delta = jnp.sum(jnp.where(keep, doo, 0.0), axis=-1, keepdims=True)
            ds16 = (p * (_dot(dom, vv, _NT) - delta)).astype(BF16)
            dq = jnp.where(keep, _dot(ds16, kv), dq)
            dk_ref[...] += _dot(ds16, qm, _TN)
            dv_ref[...] += _dot(p.astype(BF16), dom, _TN)
        dq_ref[...] = dq

    qspec = pl.BlockSpec((tq, LANE), lambda g, j, i: (i, g * _PAIRS_PER_KV + j))
    kspec = pl.BlockSpec((n_tok, LANE), lambda g, j, i: (0, g))
    vm = 7 * _nbytes((tq, n_tok), F32) + 12 * _nbytes((n_tok, LANE), F32)
    dup = jax.ShapeDtypeStruct((n_tok, AT_KH * LANE), F32)
    return _pcall(
        body, name=name, grid=(AT_KH, _PAIRS_PER_KV, n_tok // tq),
        in_specs=[qspec, kspec, kspec, qspec, qspec, qspec], out_specs=[qspec, kspec, kspec],
        out_shape=[jax.ShapeDtypeStruct((n_tok, AT_W), F32), dup, dup],
        compiler_params=_cparams(("parallel", "arbitrary", "arbitrary"), vm),
    )(q, kd, vd, o, lse, do)


def _s5_prep(lam_re, lam_im, log_dt, b_re, b_im, c_re, c_im):
    dt = jnp.exp(log_dt)[:, None]
    er = jnp.exp(lam_re * dt)
    a_re, a_im = er * jnp.cos(lam_im * dt), er * jnp.sin(lam_im * dt)
    nr, ni = a_re - 1.0, a_im
    den = lam_re * lam_re + lam_im * lam_im
    f_re, f_im = (nr * lam_re + ni * lam_im) / den, (ni * lam_re - nr * lam_im) / den
    bb_re = f_re[..., None] * b_re - f_im[..., None] * b_im
    bb_im = f_re[..., None] * b_im + f_im[..., None] * b_re
    def tiles(t, axis):
        _, a, b = t.shape
        eye = jnp.eye(S5_TG, dtype=t.dtype)
        blocks = t.reshape(S5_TILES, S5_TG, a, 1, b) * eye[None, :, None, :, None]
        if axis == 0:
            return blocks.reshape(S5_TILES * S5_TG * a, S5_TG * b)
        return blocks.transpose(1, 2, 0, 3, 4).reshape(S5_TG * a, S5_TILES * S5_TG * b)

    w_in = jnp.concatenate([tiles(bb_re.transpose(0, 2, 1), 1), tiles(bb_im.transpose(0, 2, 1), 1)], axis=1)
    w_out = jnp.concatenate([tiles(c_re.transpose(0, 2, 1), 0), -tiles(c_im.transpose(0, 2, 1), 0)], axis=0)
    return a_re.reshape(1, S5_N), a_im.reshape(1, S5_N), w_in, w_out


def _rope_tables(n_tok):
    quarter = AT_D // 4
    d = np.arange(AT_D)
    e = d % (AT_D // 2)
    inv = (ROPE_BASE ** (-(e % quarter).astype(np.float64) * 2.0 / (AT_D // 2))).astype(np.float32)
    sign = np.where(e < quarter, -1.0, 1.0).astype(np.float32)
    rot = np.zeros((AT_D, AT_D), np.float32)
    rot[np.where(e < quarter, d + quarter, d - quarter), d] = 1.0
    pos = jnp.arange(n_tok, dtype=jnp.int32)
    axis_pos = jnp.where(jnp.asarray(d // (AT_D // 2) == 0)[None, :], (pos // GRID_W)[:, None], (pos % GRID_W)[:, None])
    ang = axis_pos.astype(F32) * jnp.asarray(inv)[None, :]
    return jnp.cos(ang), jnp.sin(ang) * jnp.asarray(sign)[None, :], rot


def _head_consts(n_tok, heads):
    cos, sin, rot = _rope_tables(n_tok)
    eye = np.eye(heads, dtype=np.float32)
    mean = np.kron(eye, np.full((AT_D, AT_D), 1.0 / AT_D, np.float32))
    return jnp.tile(cos, (1, heads)), jnp.tile(sin, (1, heads)), jnp.asarray(mean), jnp.asarray(np.kron(eye, rot))


def _pad_w_in(wt):
    return jnp.concatenate([wt[:2560], wt[2592:3360], wt[2560:2592], jnp.zeros((96, wt.shape[1]), wt.dtype)], axis=0)


def _unpad_w_in(gt):
    return jnp.concatenate([gt[:2560], gt[3328:3360], gt[2560:3328]], axis=0)


def _ffn_fwd(x, g, wg, wu, wd, d):
    n_tok = x.shape[0]
    fq = wd.shape[-2]
    tm = _tile(n_tok, 256, 16)
    w_bytes = N_CHIPS * fq * D_MODEL * 2

    def up(x_ref, g_ref, wg_ref, wu_ref, h_ref, a_ref, b_ref, s_ref):
        hv = _rms(x_ref[...], g_ref[...]).astype(BF16)
        h_ref[...] = hv
        for j in range(N_CHIPS):
            av, bv = _dot(hv, wg_ref[j, 0], _NT), _dot(hv, wu_ref[j, 0], _NT)
            a_ref[j] = av.astype(BF16)
            b_ref[j] = bv.astype(BF16)
            s_ref[j] = (av * _sigmoid(av) * bv).astype(BF16)

    wspec = pl.BlockSpec((N_CHIPS, 1, fq, D_MODEL), lambda i: (0, d, 0, 0))
    aspec = pl.BlockSpec((N_CHIPS, tm, fq), lambda i: (0, i, 0))
    xspec = pl.BlockSpec((tm, D_MODEL), lambda i: (i, 0))
    act = jax.ShapeDtypeStruct((N_CHIPS, n_tok, fq), BF16)
    h, a, b, s = _pcall(
        up, name="ffn_up", grid=(n_tok // tm,),
        in_specs=[xspec, pl.BlockSpec((1, D_MODEL), lambda i: (0, 0)), wspec, wspec],
        out_specs=[xspec, aspec, aspec, aspec],
        out_shape=[jax.ShapeDtypeStruct((n_tok, D_MODEL), BF16), act, act, act],
        compiler_params=_cparams(("parallel",), 4 * w_bytes + 6 * N_CHIPS * tm * fq * 2 + 12 * tm * fq * 4
                                 + 8 * tm * D_MODEL * 4),
    )(x, g, wg, wu)

    def down(s_ref, w_ref, x_ref, o_ref):
        acc = _dot(s_ref[0], w_ref[0, 0])
        for j in range(1, N_CHIPS):
            acc = acc + _dot(s_ref[j], w_ref[j, 0])
        o_ref[...] = x_ref[...] + 0.5 * acc

    xo = _pcall(
        down, name="ffn_down", grid=(n_tok // tm,), in_specs=[aspec, wspec, xspec], out_specs=xspec,
        out_shape=jax.ShapeDtypeStruct((n_tok, D_MODEL), F32),
        compiler_params=_cparams(("parallel",), 2 * w_bytes + 2 * N_CHIPS * tm * fq * 2 + 8 * tm * D_MODEL * 4),
    )(s, wd, x)
    return xo, (x, h, a, b, s)


def _ffn_bwd(dy, dy16, saved, g, wg, wu, wd, d, into):
    x, h, a, b, s = saved
    n_tok = x.shape[0]
    fq = wd.shape[-2]
    tm = _tile(n_tok, 512, 16)

    ta_ = _tile(n_tok, 256, 16)

    def dact(dy_ref, w_ref, a_ref, b_ref, da_ref, db_ref):
        dyv = dy_ref[...]
        for j in range(N_CHIPS):
            ds = 0.5 * _dot(dyv, w_ref[j, 0], _NT)
            av, bv = a_ref[j].astype(F32), b_ref[j].astype(F32)
            sg = _sigmoid(av)
            da_ref[j] = (ds * bv * sg * (1.0 + av * (1.0 - sg))).astype(BF16)
            db_ref[j] = (ds * av * sg).astype(BF16)

    aspec = pl.BlockSpec((N_CHIPS, ta_, fq), lambda i: (0, i, 0))
    act = jax.ShapeDtypeStruct((N_CHIPS, n_tok, fq), BF16)
    da, db = _pcall(
        dact, name="ffn_dact", grid=(n_tok // ta_,),
        in_specs=[pl.BlockSpec((ta_, D_MODEL), lambda i: (i, 0)),
                  pl.BlockSpec((N_CHIPS, 1, fq, D_MODEL), lambda i: (0, d, 0, 0)), aspec, aspec],
        out_specs=[aspec, aspec], out_shape=[act, act],
        compiler_params=_cparams(("parallel",), 2 * N_CHIPS * fq * D_MODEL * 2 + 8 * N_CHIPS * ta_ * fq * 2
                                 + 12 * ta_ * fq * 4),
    )(dy16, wd, a, b)

    any_spec = pl.BlockSpec(memory_space=pl.ANY)
    alias = {} if into is None else {2: 0}
    tn = _tile(D_MODEL, 512, LANE)

    def dw_call(t, rhs, scale, prev):
        def dw(t_ref, r_ref, *rest):
            rest[-1][0, 0] = (scale * _dot(t_ref[0], r_ref[...], _TN)).astype(GRAD_DT)

        return _pcall(
            dw, name="ffn_dw", grid=(N_CHIPS, D_MODEL // tn),
            in_specs=[pl.BlockSpec((1, n_tok, fq), lambda j, n: (j, 0, 0)), pl.BlockSpec((n_tok, tn), lambda j, n: (0, n))]
            + ([] if prev is None else [any_spec]),
            out_specs=pl.BlockSpec((1, 1, fq, tn), lambda j, n: (j, d, 0, n)),
            out_shape=jax.ShapeDtypeStruct((N_CHIPS, DEPTH, fq, D_MODEL), GRAD_DT), input_output_aliases=alias,
            compiler_params=_cparams(("parallel", "parallel"), 6 * n_tok * (fq + tn) * 2 + 4 * fq * tn * 4),
        )(t, rhs, *([] if prev is None else [prev]))

    dwg = dw_call(da, h, 1.0, None if into is None else into[0])
    dwu = dw_call(db, h, 1.0, None if into is None else into[1])
    dwd = dw_call(s, dy16, 0.5, None if into is None else into[2])
    tr = _tile(n_tok, 256, 16)

    def dh_norm(da_ref, db_ref, wg_ref, wu_ref, dy_ref, x_ref, g_ref, dx_ref, dx16_ref, dg_ref):
        dh = _dot(da_ref[0], wg_ref[0, 0]) + _dot(db_ref[0], wu_ref[0, 0])
        for j in range(1, N_CHIPS):
            dh = dh + _dot(da_ref[j], wg_ref[j, 0]) + _dot(db_ref[j], wu_ref[j, 0])
        dx, dg = _rms_bwd(x_ref[...], g_ref[...], dh)
        dx = dy_ref[...] + dx
        dx_ref[...] = dx
        dx16_ref[...] = dx.astype(BF16)

        @pl.when(pl.program_id(0) == 0)
        def _():
            dg_ref[...] = jnp.zeros(dg_ref.shape, F32)

        dg_ref[...] += dg

    tspec = pl.BlockSpec((N_CHIPS, tr, fq), lambda i: (0, i, 0))
    wspec = pl.BlockSpec((N_CHIPS, 1, fq, D_MODEL), lambda i: (0, d, 0, 0))
    rspec = pl.BlockSpec((tr, D_MODEL), lambda i: (i, 0))
    vspec = pl.BlockSpec((1, D_MODEL), lambda i: (0, 0))
    dx, dx16, dg = _pcall(
        dh_norm, name="ffn_dh_norm", grid=(n_tok // tr,),
        in_specs=[tspec, tspec, wspec, wspec, rspec, rspec, vspec], out_specs=[rspec, rspec, vspec],
        out_shape=[jax.ShapeDtypeStruct((n_tok, D_MODEL), F32), jax.ShapeDtypeStruct((n_tok, D_MODEL), BF16),
                   jax.ShapeDtypeStruct((1, D_MODEL), F32)],
        compiler_params=_cparams(("arbitrary",), 4 * N_CHIPS * D_MODEL * fq * 2 + 4 * N_CHIPS * tr * fq * 2
                                 + 16 * tr * D_MODEL * 4),
    )(da, db, wg, wu, dy, x, g)
    return dx, dx16, dg[0], (dwg, dwu, dwd)


def _alpha_pads(w_alpha):
    z = jnp.zeros((LANE - 2 * GLA_R, GLA_W), F32)
    wf = jnp.concatenate([w_alpha[0], jnp.zeros((GLA_R, GLA_W), F32), z], axis=0)
    wb = jnp.concatenate([jnp.zeros((GLA_R, GLA_W), F32), w_alpha[1], z], axis=0)
    return wf, wb


def _dup_heads(t):
    low = lax.broadcasted_iota(jnp.int32, t.shape, 1) < AT_D
    h0, h1 = jnp.where(low, t, 0.0), jnp.where(low, 0.0, t)
    return jnp.concatenate([h0 + pltpu.roll(h0, AT_D, 1), h1 + pltpu.roll(h1, AT_D, 1)], axis=1)


def _fold_heads(t):
    low = lax.broadcasted_iota(jnp.int32, (t.shape[0], LANE), 1) < AT_D
    g0, g1 = t[:, :LANE], t[:, LANE:]
    return jnp.where(low, g0 + pltpu.roll(g0, AT_D, 1), g1 + pltpu.roll(g1, AT_D, 1))


def _mixer_fwd(x, w):
    n_tok = x.shape[0]
    wip, wmg = w['w_in_p'], w['w_merge_gate']
    tr = _tile(n_tok, 256, 16)

    def project(x_ref, g_ref, wip_ref, wmg_ref, h_ref, z_ref, gp_ref):
        hv = _rms(x_ref[...], g_ref[...]).astype(BF16)
        h_ref[...] = hv
        z_ref[...] = _dot(hv, wip_ref[...], _NT).astype(BF16)
        gp_ref[...] = _dot(hv, wmg_ref[...]).astype(BF16)

    def rows(c):
        return pl.BlockSpec((tr, c), lambda i: (i, 0))

    def whole(t):
        return pl.BlockSpec(t.shape, lambda i: (0, 0))

    h, z, gp = _pcall(
        project, name="mix_in", grid=(n_tok // tr,),
        in_specs=[rows(D_MODEL), whole(w['mix_norm']), whole(wip), whole(wmg)],
        out_specs=[rows(D_MODEL), rows(ZP_W), rows(3 * D_MODEL)],
        out_shape=[jax.ShapeDtypeStruct((n_tok, D_MODEL), BF16), jax.ShapeDtypeStruct((n_tok, ZP_W), BF16),
                   jax.ShapeDtypeStruct((n_tok, 3 * D_MODEL), BF16)],
        compiler_params=_cparams(("parallel",), 4 * (wip.size + wmg.size) + 16 * tr * (ZP_W + 3 * D_MODEL)),
    )(x, w['mix_norm'], wip, wmg)
    u = _cw(z, 0, S5_W)

    xs, yy = [], None
    for d, rev in ((0, False), (1, True)):
        a_re, a_im, wb, wc = w['s5'][d]
        x_d, yy = _s5_fwd("s5_rev" if rev else "s5_fwd", z, a_re, a_im, wb, wc, yy, rev=rev)
        xs.append(x_d)

    def s5_post(yv, uv, dv, wglu):
        y0 = yv + dv * uv
        y1 = _gelu(y0)
        t = _dot(y1.astype(BF16), wglu)
        return y0, t, y1 * _sigmoid(t)

    y0, t_glu, y_s5 = _rowwise("s5_post", s5_post, [yy, u], [w['s5_d'], w['s5_w_glu']],
                               [(S5_W, F32), (S5_W, F32), (S5_W, BF16)])

    wa_f, wa_b = _alpha_pads(w['gla_w_alpha'])
    ba_f, ba_b = w['gla_b_alpha'][0:1], w['gla_b_alpha'][1:2]
    zz = _cw(z, ZC_Z, LANE)

    def gla_prep(zv, waf, wab, bf, bb):
        z16 = zv.astype(BF16)
        lf = _dot(z16, waf.astype(BF16)) + bf
        lb = _dot(z16, wab.astype(BF16)) + bb
        return _log_sigmoid(lf) / GLA_TAU, _log_sigmoid(lb) / GLA_TAU

    la_f, la_b = _rowwise("gla_prep", gla_prep, [zz], [wa_f, wa_b, ba_f, ba_b], [(GLA_W, F32), (GLA_W, F32)])
    o_f, s_f = _gla_fwd("gla_fwd", z, la_f, rev=False)
    o_b, s_b = _gla_fwd("gla_rev", z, la_b, rev=True)
    gate = _cw(z, 4, GLA_W)

    def gla_post(of, ob, gv, gn):
        o, gv = of + ob, gv.astype(F32)
        outs = []
        for hh in range(GLA_H):
            sl = slice(hh * GLA_D, (hh + 1) * GLA_D)
            gh = gv[:, sl]
            outs.append(_rms(o[:, sl], gn) * (gh * _sigmoid(gh)))
        return jnp.concatenate(outs, axis=1)

    y_gla = _rowwise("gla_post", gla_post, [o_f, o_b, gate], [w['gla_norm']], [(GLA_W, BF16)])[0]

    cq, sq, mq, rq = _head_consts(n_tok, AT_QH)
    ck, sk, mk, rk = _head_consts(n_tok, AT_KH)
    gq = jnp.tile(w['attn_q_norm'], (1, AT_QH))
    gk = jnp.tile(w['attn_k_norm'], (1, AT_KH))

    def rope_fwd(v, gain, cos, sin, mean, rot, scale):
        r = lax.rsqrt(_dot(v * v, mean, precision=HI) + EPS)
        vn = v * r * gain
        return (vn * cos + _dot(vn, rot, precision=HI) * sin) * scale

    def attn_prep(qv, kv, vv, cqv, sqv, ckv, skv, gqv, gkv, mqv, rqv, mkv, rkv):
        qv, kv, vv = qv.astype(F32), kv.astype(F32), vv.astype(F32)
        kr = rope_fwd(kv, gkv, ckv, skv, mkv, rkv, 1.0)
        return rope_fwd(qv, gqv, cqv, sqv, mqv, rqv, AT_D ** -0.5), _dup_heads(kr), _dup_heads(vv)

    q_raw, k_raw, v_raw = _cw(z, 5, AT_W), _cw(z, ZC_AK, AT_KW), _cw(z, ZC_AV, AT_KW)
    q16, kd, vd = _rowwise("attn_prep", attn_prep, [q_raw, k_raw, v_raw, cq, sq, ck, sk], [gq, gk, mq, rq, mk, rk],
                           [(AT_W, BF16), (2 * LANE, BF16), (2 * LANE, BF16)])
    y_at, lse = _attn_fwd("attn_fwd", q16, kd, vd)

    p0 = _mm("mix_branch", y_s5, w['w_branch_s5'], out_dtype=BF16)
    p1 = _mm("mix_branch", y_gla, w['w_branch_gla'], out_dtype=BF16)
    p2 = _mm("mix_branch_f32", y_at, w['w_branch_attn'], out_dtype=BF16)

    def merge(gpv, a0, a1, a2, bias):
        g = _sigmoid(gpv.astype(F32) + bias)
        return (g[:, :D_MODEL] * a0.astype(F32) + g[:, D_MODEL:2 * D_MODEL] * a1.astype(F32)
                + g[:, 2 * D_MODEL:] * a2.astype(F32))

    merged = _rowwise("mix_merge", merge, [gp, p0, p1, p2], [w['b_merge_gate']], [(D_MODEL, BF16)])[0]
    xo = _mm("mix_out", merged, w['w_out'], add=x)
    saved = dict(x=x, h=h, z=z, gp=gp, xs=xs, y0=y0, t_glu=t_glu, y_s5=y_s5, la=(la_f, la_b), o=(o_f, o_b), s=(s_f, s_b),
                 y_gla=y_gla, q16=q16, kd=kd, vd=vd, lse=lse, y_at=y_at, p=(p0, p1, p2), merged=merged)
    return xo, saved


def _mixer_bwd(dy, dy16, sv, w, s5_raw):
    z, h = sv['z'], sv['h']
    n_tok = z.shape[0]
    grads = {}
    dmerged = _mm("mix_dmerged", dy16, w['w_out'], tb=True, out_dtype=BF16)
    grads['w_out'] = _mm("mix_dwout", sv['merged'], dy16, ta=True)

    def dmerge(dm, gpv, a0, a1, a2, bias):
        dm = dm.astype(F32)
        g = _sigmoid(gpv.astype(F32) + bias)
        ps = (a0.astype(F32), a1.astype(F32), a2.astype(F32))
        dps, dgs = [], []
        for k in range(3):
            gk = g[:, k * D_MODEL:(k + 1) * D_MODEL]
            dps.append(dm * gk)
            dgs.append(dm * ps[k] * gk * (1.0 - gk))
        dgp = jnp.concatenate(dgs, axis=1)
        return dps[0], dps[1], dps[2], dgp, jnp.sum(dgp, axis=0, keepdims=True)

    dp0, dp1, dp2, dgp, dbm = _rowwise("mix_dmerge", dmerge, [dmerged, sv['gp'], *sv['p']], [w['b_merge_gate']],
                                       [(D_MODEL, BF16)] * 3 + [(3 * D_MODEL, BF16)], accs=[(1, 3 * D_MODEL)])
    grads['b_merge_gate'] = dbm[0]
    grads['w_branch_s5'] = _mm("mix_dbranch", sv['y_s5'], dp0, ta=True)
    grads['w_branch_gla'] = _mm("mix_dbranch", sv['y_gla'], dp1, ta=True)
    grads['w_branch_attn'] = _mm("mix_dbranch_f32", sv['y_at'], dp2, ta=True)
    dy_s5 = _mm("mix_dy", dp0, w['w_branch_s5'], tb=True)
    dy_gla = _mm("mix_dy", dp1, w['w_branch_gla'], tb=True)
    dy_at = _mm("mix_dy", dp2, w['w_branch_attn'], tb=True)
    grads['w_merge_gate'] = _mm("mix_dwgate", h, dgp, ta=True, out_dtype=GRAD_DT)
    u = _cw(z, 0, S5_W)

    def s5_post_bwd(dyv, y0, t, uv, dv, wglu):
        sg = _sigmoid(t)
        y1 = _gelu(y0)
        dt = dyv * y1 * sg * (1.0 - sg)
        dy1 = dyv * sg + _dot(dt.astype(BF16), wglu, _NT)
        dy0 = dy1 * _gelu_grad(y0)
        return dy0, dt, y1, dy0 * dv, jnp.sum(dy0 * uv, axis=0, keepdims=True)

    dy0, dt_glu, y1, du, dd = _rowwise("s5_post_bwd", s5_post_bwd, [dy_s5, sv['y0'], sv['t_glu'], u],
                                       [w['s5_d'], w['s5_w_glu']],
                                       [(S5_W, BF16), (S5_W, BF16), (S5_W, BF16), (S5_W, F32)], accs=[(1, S5_W)])
    grads['s5_d'] = dd[0]
    grads['s5_w_glu'] = _mm("s5_dwglu", y1, dt_glu, ta=True)
    s5_cot = []
    for d, rev in ((0, False), (1, True)):
        a_re, a_im, wb, wc = w['s5'][d]
        du, dwb, dwc, da = _s5_bwd("s5_adj_fwd" if rev else "s5_adj_rev", dy0, z, sv['xs'][d], a_re, -a_im, wb, wc, du,
                                   rev=not rev)
        da = jnp.sum(da, axis=0, keepdims=True)
        s5_cot.append((da[:, :S5_N], da[:, S5_N:], dwb, dwc))
    for d in range(2):
        _, pull = jax.vjp(_s5_prep, *s5_raw[d])
        s5_cot[d] = pull(s5_cot[d])
    for k, nm in enumerate(('s5_lambda_re', 's5_lambda_im', 's5_log_dt', 's5_b_re', 's5_b_im', 's5_c_re', 's5_c_im')):
        grads[nm] = jnp.stack([s5_cot[0][k], s5_cot[1][k]])

    gate = _cw(z, 4, GLA_W)

    def gla_post_bwd(dyv, of, ob, gv, gn):
        o, gv = of + ob, gv.astype(F32)
        dos, dgates = [], []
        dgn = jnp.zeros((1, GLA_D), F32)
        for hh in range(GLA_H):
            sl = slice(hh * GLA_D, (hh + 1) * GLA_D)
            gh, dyh = gv[:, sl], dyv[:, sl]
            sg = _sigmoid(gh)
            dgates.append(dyh * _rms(o[:, sl], gn) * sg * (1.0 + gh * (1.0 - sg)))
            dx, dg = _rms_bwd(o[:, sl], gn, dyh * gh * sg)
            dos.append(dx)
            dgn = dgn + dg
        return jnp.concatenate(dos, axis=1), jnp.concatenate(dgates, axis=1), dgn

    do, dgate, dgn = _rowwise("gla_post_bwd", gla_post_bwd, [dy_gla, *sv['o'], gate], [w['gla_norm']],
                              [(GLA_W, F32), (GLA_W, F32)], accs=[(1, GLA_D)])
    grads['gla_norm'] = dgn[0]
    dq_f, dk_f, dv_f, dla_f = _gla_bwd("gla_bwd_fwd", z, sv['la'][0], sv['s'][0], do, rev=False)
    dq_b, dk_b, dv_b, dla_b = _gla_bwd("gla_bwd_rev", z, sv['la'][1], sv['s'][1], do, rev=True)
    wa_f, wa_b = _alpha_pads(w['gla_w_alpha'])
    ba_f, ba_b = w['gla_b_alpha'][0:1], w['gla_b_alpha'][1:2]
    zz = _cw(z, ZC_Z, LANE)

    def gla_prep_bwd(dlf, dlb, zv, waf, wab, bf, bb):
        z16 = zv.astype(BF16)
        waf16, wab16 = waf.astype(BF16), wab.astype(BF16)
        xf = _dot(z16, waf16) + bf
        xb = _dot(z16, wab16) + bb
        df = dlf * (_sigmoid(-xf) / GLA_TAU)
        db = dlb * (_sigmoid(-xb) / GLA_TAU)
        dz = _dot(df.astype(BF16), waf16, _NT) + _dot(db.astype(BF16), wab16, _NT)
        return df, db, dz, jnp.sum(df, axis=0, keepdims=True), jnp.sum(db, axis=0, keepdims=True)

    dlog_f, dlog_b, dzz, dba_f, dba_b = _rowwise(
        "gla_prep_bwd", gla_prep_bwd, [dla_f, dla_b, zz], [wa_f, wa_b, ba_f, ba_b],
        [(GLA_W, BF16), (GLA_W, BF16), (LANE, F32)], accs=[(1, GLA_W), (1, GLA_W)])
    dwa_f = _mm("gla_dwa", zz, dlog_f, ta=True)
    dwa_b = _mm("gla_dwa", zz, dlog_b, ta=True)
    grads['gla_w_alpha'] = jnp.stack([dwa_f[:GLA_R], dwa_b[GLA_R:2 * GLA_R]])
    grads['gla_b_alpha'] = jnp.concatenate([dba_f, dba_b], axis=0)

    dq_r, dkd, dvd = _attn_bwd("attn_bwd", sv['q16'], sv['kd'], sv['vd'], sv['y_at'], sv['lse'], dy_at)
    cq, sq, mq, rq = _head_consts(n_tok, AT_QH)
    ck, sk, mk, rk = _head_consts(n_tok, AT_KH)
    gq = jnp.tile(w['attn_q_norm'], (1, AT_QH))
    gk = jnp.tile(w['attn_k_norm'], (1, AT_KH))

    def rope_bwd(dr, v, gain, cos, sin, mean, rot, scale):
        dr = dr * scale
        dvn = dr * cos + _dot(dr * sin, rot, _NT, precision=HI)
        r = lax.rsqrt(_dot(v * v, mean, precision=HI) + EPS)
        vh = v * r
        dg = jnp.sum(dvn * vh, axis=0, keepdims=True)
        dvh = dvn * gain
        return r * (dvh - vh * _dot(dvh * vh, mean, precision=HI)), dg

    def attn_prep_bwd(dq, dk2, dv2, qv, kv, cqv, sqv, ckv, skv, gqv, gkv, mqv, rqv, mkv, rkv):
        qv, kv = qv.astype(F32), kv.astype(F32)
        dqo, dgq = rope_bwd(dq, qv, gqv, cqv, sqv, mqv, rqv, AT_D ** -0.5)
        dko, dgk = rope_bwd(_fold_heads(dk2), kv, gkv, ckv, skv, mkv, rkv, 1.0)
        return dqo, dko, _fold_heads(dv2), dgq, dgk

    q_raw, k_raw = _cw(z, 5, AT_W), _cw(z, ZC_AK, AT_KW)
    daq, dak, dav, dgq, dgk = _rowwise("attn_prep_bwd", attn_prep_bwd, [dq_r, dkd, dvd, q_raw, k_raw, cq, sq, ck, sk],
                                       [gq, gk, mq, rq, mk, rk], [(AT_W, F32), (AT_KW, F32), (AT_KW, F32)],
                                       accs=[(1, AT_W), (1, AT_KW)])
    grads['attn_q_norm'] = dgq.reshape(AT_QH, AT_D).sum(axis=0)
    grads['attn_k_norm'] = dgk.reshape(AT_KH, AT_D).sum(axis=0)

    def assemble(duv, qf, qb, kf, kb, vf, vb, dg, aq, ak, av, zzv):
        return jnp.concatenate([duv, qf + qb, kf + kb, vf + vb, dg, aq, ak, av, zzv], axis=1)

    dz = _rowwise("mix_dz", assemble, [du, dq_f, dq_b, dk_f, dk_b, dv_f, dv_b, dgate, daq, dak, dav, dzz], [],
                  [(ZP_W, BF16)])[0]
    grads['w_in'] = _unpad_w_in(_mm("mix_dwin", dz, h, ta=True, out_dtype=GRAD_DT))
    wmg, wip = w['w_merge_gate'], w['w_in_p']
    tr = _tile(n_tok, 256, 16)

    def dh_norm(dgp_ref, dz_ref, wmg_ref, wip_ref, dy_ref, x_ref, g_ref, dx_ref, dx16_ref, dg_ref):
        dh = _dot(dgp_ref[...], wmg_ref[...], _NT) + _dot(dz_ref[...], wip_ref[...])
        dxv, dgv = _rms_bwd(x_ref[...], g_ref[...], dh)
        dxv = dy_ref[...] + dxv
        dx_ref[...] = dxv
        dx16_ref[...] = dxv.astype(BF16)

        @pl.when(pl.program_id(0) == 0)
        def _():
            dg_ref[...] = jnp.zeros(dg_ref.shape, F32)

        dg_ref[...] += dgv

    def rows(c):
        return pl.BlockSpec((tr, c), lambda i: (i, 0))

    def whole(t):
        return pl.BlockSpec(t.shape, lambda i: (0, 0))

    dx, dx16, dg = _pcall(
        dh_norm, name="mix_dh_norm", grid=(n_tok // tr,),
        in_specs=[rows(dgp.shape[1]), rows(dz.shape[1]), whole(wmg), whole(wip), rows(D_MODEL), rows(D_MODEL),
                  whole(w['mix_norm'])],
        out_specs=[rows(D_MODEL), rows(D_MODEL), whole(w['mix_norm'])],
        out_shape=[jax.ShapeDtypeStruct((n_tok, D_MODEL), F32), jax.ShapeDtypeStruct((n_tok, D_MODEL), BF16),
                   jax.ShapeDtypeStruct((1, D_MODEL), F32)],
        compiler_params=_cparams(("arbitrary",), 4 * (wmg.size + wip.size) + 8 * tr * (dgp.shape[1] + dz.shape[1])
                                 + 16 * tr * D_MODEL * 4),
    )(dgp, dz, wmg, wip, dy, sv['x'], w['mix_norm'])
    grads['mix_norm'] = dg[0]
    return dx, dx16, grads


FFN_NAMES = [p + k for p in ('ffn1_', 'ffn2_') for k in ('w_gate', 'w_up', 'w_down')]
MISC_SHARDED = [n for n in SHARDED if n not in FFN_NAMES]
LARGE_SHARDED = FFN_NAMES + ['w_in', 'w_merge_gate']
KEPT_TRANSPOSED = [p + k for p in ('ffn1_', 'ffn2_') for k in ('w_gate', 'w_up')] + ['w_in']


def _stored_axis(n):
    return 1 if n in KEPT_TRANSPOSED else SHARD_AXIS[n]


def _stored(n, t):
    return jnp.swapaxes(t, 1, 2) if n in KEPT_TRANSPOSED else t


def _layer_weights(ws, rep, i):
    w = {k: _from_shards(ws[k][:, i], _stored_axis(k) - 1) for k in MISC_SHARDED}
    for k in ('ffn1_norm', 'mix_norm', 'ffn2_norm', 's5_d', 'gla_norm', 'attn_q_norm', 'attn_k_norm', 'b_merge_gate'):
        w[k] = rep[k][i].reshape(1, -1)
    w['w_in_p'] = _pad_w_in(w['w_in'])
    s5_raw = [tuple(rep[k][i, d] for k in ('s5_lambda_re', 's5_lambda_im', 's5_log_dt', 's5_b_re', 's5_b_im',
                                            's5_c_re', 's5_c_im')) for d in range(2)]
    w['s5'] = [_s5_prep(*s5_raw[d]) for d in range(2)]
    return w, s5_raw


def _device_step(x, target, ws, rep):
    lw = [_layer_weights(ws, rep, i) for i in range(DEPTH)]
    ffn = {p: (ws[p + 'w_gate'], ws[p + 'w_up'], ws[p + 'w_down']) for p in ('ffn1_', 'ffn2_')}
    saved = []
    for i in range(DEPTH):
        w, _ = lw[i]
        x, s1 = _ffn_fwd(x, w['ffn1_norm'], *ffn['ffn1_'], i)
        x, s2 = _mixer_fwd(x, w)
        x, s3 = _ffn_fwd(x, w['ffn2_norm'], *ffn['ffn2_'], i)
        saved.append((s1, s2, s3))
    gfin = rep['final_norm'].reshape(1, -1)

    def head(xv, tv, gv):
        e = _rms(xv, gv) - tv
        dx, dg = _rms_bwd(xv, gv, e * (1.0 / D_MODEL))
        part = jnp.sum(e * e, axis=0, keepdims=True)
        return dx, dx, dg, part

    dx, dx16, dgfin, part = _rowwise("loss_head", head, [x, target], [gfin], [(D_MODEL, F32), (D_MODEL, BF16)],
                                     accs=[(1, D_MODEL), (1, D_MODEL)])
    loss = (0.5 / D_MODEL) * jnp.sum(part)
    per_layer = []
    dffn = {'ffn1_': None, 'ffn2_': None}
    for i in reversed(range(DEPTH)):
        w, s5_raw = lw[i]
        s1, s2, s3 = saved[i]
        dx, dx16, n3, dffn['ffn2_'] = _ffn_bwd(dx, dx16, s3, w['ffn2_norm'], *ffn['ffn2_'], i, dffn['ffn2_'])
        dx, dx16, g = _mixer_bwd(dx, dx16, s2, w, s5_raw)
        dx, dx16, n1, dffn['ffn1_'] = _ffn_bwd(dx, dx16, s1, w['ffn1_norm'], *ffn['ffn1_'], i, dffn['ffn1_'])
        g['ffn2_norm'], g['ffn1_norm'] = n3, n1
        per_layer.append(g)
    per_layer.reverse()
    stacked = {k: jnp.stack([per_layer[i][k] for i in range(DEPTH)]) for k in per_layer[0]}
    gs = {k: _to_shards(stacked[k], _stored_axis(k)) for k in MISC_SHARDED}
    for p in ('ffn1_', 'ffn2_'):
        gs[p + 'w_gate'], gs[p + 'w_up'], gs[p + 'w_down'] = dffn[p]
    grep = {k: stacked[k] for k in REPLICATED if k != 'final_norm'}
    grep['final_norm'] = dgfin[0]
    return loss, dx, gs, grep


PACK_C = 1024
ROW_TILE = 512


def _rows_of(shape):
    return -(-int(np.prod(shape)) // PACK_C)


def _as_rows(t, lead=0):
    head = t.shape[:lead]
    flat = t.reshape(head + (-1,))
    rows = -(-flat.shape[-1] // PACK_C)
    pad = rows * PACK_C - flat.shape[-1]
    if pad:
        flat = jnp.pad(flat, [(0, 0)] * lead + [(0, pad)])
    return flat.reshape(head + (rows, PACK_C))


def _pack(items, total_rows, lead=0):
    parts = [_as_rows(t, lead) for t in items]
    used = sum(p.shape[lead] for p in parts)
    if total_rows > used:
        parts.append(jnp.zeros(parts[0].shape[:lead] + (total_rows - used, PACK_C), parts[0].dtype))
    return jnp.concatenate(parts, axis=lead)


def _unpack(rows, shapes, lead=0):
    out, r0 = [], 0
    head = rows.shape[:lead]
    for shp in shapes:
        n, size = _rows_of(shp), int(np.prod(shp))
        piece = lax.slice_in_dim(rows, r0, r0 + n, axis=lead).reshape(head + (n * PACK_C,))
        out.append(lax.slice_in_dim(piece, 0, size, axis=lead).reshape(head + tuple(shp)))
        r0 += n
    return out


def _round_up(n, m):
    return -(-n // m) * m


def _to_shards(g, axis):
    shp = g.shape
    g = g.reshape(shp[:axis] + (N_CHIPS, shp[axis] // N_CHIPS) + shp[axis + 1:])
    return jnp.moveaxis(g, axis, 0)


def _from_shards(s, axis):
    s = jnp.moveaxis(s, 0, axis)
    shp = s.shape
    return s.reshape(shp[:axis] + (shp[axis] * shp[axis + 1],) + shp[axis + 2:])


_HBM = pl.BlockSpec(memory_space=pltpu.HBM)


def _place():
    x, y, c = lax.axis_index("x"), lax.axis_index("y"), lax.axis_index("c")
    chips = [(1 - x, y), (x, 1 - y), (1 - x, 1 - y)]
    return x, y, c, chips


def _rcopy(src, dst, send, recv, k, dev):
    return pltpu.make_async_remote_copy(src_ref=src, dst_ref=dst, send_sem=send.at[k], recv_sem=recv.at[k],
                                        device_id=dev, device_id_type=MESH)


def _comm_call(name, body, arrays, out_shapes, n_sems, n_local):
    return pl.pallas_call(
        body, name=name, in_specs=[_HBM] * len(arrays), out_specs=[_HBM] * len(out_shapes), out_shape=out_shapes,
        scratch_shapes=[pltpu.SemaphoreType.DMA((n_sems,)), pltpu.SemaphoreType.DMA((n_sems,)),
                        pltpu.SemaphoreType.DMA((max(n_local, 1),))],
    )(*arrays)


def _ici_peers(x, y, c):
    x_first = c == 0
    first = (jnp.where(x_first, 1 - x, x), jnp.where(x_first, y, 1 - y))
    second = (jnp.where(x_first, x, 1 - x), jnp.where(x_first, 1 - y, y))
    return first, second


def _ag_chips(name, packs):
    n = len(packs)

    def body(*refs):
        p, o, (send, recv, _) = refs[:n], refs[n:2 * n], refs[2 * n:]
        x, y, c, _ = _place()
        (ax, ay), (bx, by) = _ici_peers(x, y, c)
        me, na, nb, nd = 2 * x + y, 2 * ax + ay, 2 * bx + by, 2 * (1 - x) + (1 - y)
        sib = (x, y, 1 - c)
        own = [_rcopy(p[a].at[c], o[a].at[me, c], send, recv, 6 * a, (ax, ay, c)) for a in range(n)]
        own += [_rcopy(p[a].at[c], o[a].at[me, c], send, recv, 6 * a + 1, (bx, by, c)) for a in range(n)]
        for cp in own:
            cp.start()
        passed = []

        def landed(a, chip, k, fwd):
            blk = o[a].at[chip, c]
            _rcopy(blk, blk, send, recv, 6 * a + k, sib).wait_recv()
            for sem, dev in fwd:
                cp = _rcopy(blk, blk, send, recv, 6 * a + sem, dev)
                cp.start()
                passed.append(cp)

        for a in range(n):
            landed(a, na, 0, [(2, (bx, by, c)), (3, sib)])
        for a in range(n):
            landed(a, nb, 1, [(4, sib)])
        for a in range(n):
            landed(a, nd, 2, [(5, sib)])
        for a in range(n):
            for k, chip in ((3, nb), (4, na), (5, nd)):
                blk = o[a].at[chip, 1 - c]
                _rcopy(blk, blk, send, recv, 6 * a + k, sib).wait_recv()
        for cp in own + passed:
            cp.wait_send()

    outs = _comm_call(name, body, packs, [jax.ShapeDtypeStruct((N_CHIPS,) + t.shape, t.dtype) for t in packs], 6 * n, 0)
    chip = 2 * lax.axis_index("x") + lax.axis_index("y")
    return [lax.dynamic_update_slice_in_dim(o, t[None], chip, axis=0) for o, t in zip(outs, packs)]


def _pair_swap(name, gs):
    n = len(gs)
    ns = gs[0].shape[0]

    def body(*refs):
        g, r, (send, recv, _) = refs[:n], refs[n:2 * n], refs[2 * n:]
        x, y, c, _ = _place()
        cps = [_rcopy(g[a].at[j, 1 - c], r[a].at[j], send, recv, ns * a + j, (x, y, 1 - c))
               for a in range(n) for j in range(ns)]
        for cp in cps:
            cp.start()
        for cp in cps:
            cp.wait_recv()
        for cp in cps:
            cp.wait_send()

    return _comm_call(name, body, gs, [jax.ShapeDtypeStruct((ns,) + t.shape[2:], t.dtype) for t in gs], ns * n, 0)


def _halve_first(name, parts):
    n = len(parts)

    def body(*refs):
        p, r, (send, recv, _) = refs[:n], refs[n:2 * n], refs[2 * n:]
        x, y, c, _ = _place()
        (ax, ay), _ = _ici_peers(x, y, c)
        cps = [_rcopy(p[a].at[jnp.where(c == 0, 2 * ax + k, 2 * k + ay)], r[a].at[k], send, recv, 2 * a + k, (ax, ay, c))
               for a in range(n) for k in range(2)]
        for cp in cps:
            cp.start()
        for cp in cps:
            cp.wait_recv()
        for cp in cps:
            cp.wait_send()

    return _comm_call(name, body, parts, [jax.ShapeDtypeStruct((2,) + t.shape[1:], t.dtype) for t in parts], 2 * n, 0)


def _halve_second(name, kept):
    n = len(kept)

    def body(*refs):
        p, r, (send, recv, _) = refs[:n], refs[n:2 * n], refs[2 * n:]
        x, y, c, _ = _place()
        _, (bx, by) = _ici_peers(x, y, c)
        cps = [_rcopy(p[a].at[jnp.where(c == 0, by, bx)], r[a], send, recv, a, (bx, by, c)) for a in range(n)]
        for cp in cps:
            cp.start()
        for cp in cps:
            cp.wait_recv()
        for cp in cps:
            cp.wait_send()

    return _comm_call(name, body, kept, [jax.ShapeDtypeStruct(t.shape[1:], t.dtype) for t in kept], n, 0)


def _add_selected(name, t, r, index, out_dtype):
    single = r.ndim == 2
    r3 = r[None] if single else r
    nk, rh, cc = r3.shape
    tl = _tile(rh, ROW_TILE, 16)

    def body(i_ref, t_ref, r_ref, o_ref):
        o_ref[...] = (t_ref[...].astype(F32) + r_ref[...].astype(F32)).astype(o_ref.dtype)

    out = _pcall(
        body, name=name,
        grid_spec=pltpu.PrefetchScalarGridSpec(
            num_scalar_prefetch=1, grid=(nk, rh // tl),
            in_specs=[pl.BlockSpec((1, tl, cc), lambda k, i, i_ref: (i_ref[k], i, 0)),
                      pl.BlockSpec((1, tl, cc), lambda k, i, i_ref: (k, i, 0))],
            out_specs=pl.BlockSpec((1, tl, cc), lambda k, i, i_ref: (k, i, 0))),
        out_shape=jax.ShapeDtypeStruct(r3.shape, out_dtype),
        compiler_params=_cparams(("parallel", "parallel"), 8 * tl * cc * 4),
    )(index, t, r3)
    return out[0] if single else out


def _pair_share(name, reds):
    n = len(reds)

    def body(*refs):
        r, o, (send, recv, _) = refs[:n], refs[n:2 * n], refs[2 * n:]
        x, y, c, _ = _place()
        cps = [_rcopy(r[a], o[a].at[c], send, recv, a, (x, y, 1 - c)) for a in range(n)]
        for cp in cps:
            cp.start()
        for a in range(n):
            blk = o[a].at[1 - c]
            _rcopy(blk, blk, send, recv, a, (x, y, 1 - c)).wait_recv()
        for cp in cps:
            cp.wait_send()

    outs = _comm_call(name, body, reds, [jax.ShapeDtypeStruct((2,) + t.shape, t.dtype) for t in reds], n, 0)
    core = lax.axis_index("c")
    return [lax.dynamic_update_slice_in_dim(o, t[None], core, axis=0) for o, t in zip(outs, reds)]


def _add_own_half(name, g, r):
    ns, _, rh, cc = g.shape
    tl = _tile(rh, ROW_TILE, 16)
    core = lax.axis_index("c").astype(jnp.int32).reshape(1)

    def body(c_ref, g_ref, r_ref, o_ref):
        o_ref[...] = (g_ref[0].astype(F32) + r_ref[...].astype(F32)).astype(o_ref.dtype)

    return _pcall(
        body, name=name,
        grid_spec=pltpu.PrefetchScalarGridSpec(
            num_scalar_prefetch=1, grid=(ns, rh // tl),
            in_specs=[pl.BlockSpec((1, 1, tl, cc), lambda j, i, c_ref: (j, c_ref[0], i, 0)),
                      pl.BlockSpec((1, tl, cc), lambda j, i, c_ref: (j, i, 0))],
            out_specs=pl.BlockSpec((1, tl, cc), lambda j, i, c_ref: (j, i, 0))),
        out_shape=jax.ShapeDtypeStruct(r.shape, BF16),
        compiler_params=_cparams(("parallel", "parallel"), 8 * tl * cc * 4),
    )(core, g, r)


def _reduce_scatter(gs):
    x, y, c = lax.axis_index("x"), lax.axis_index("y"), lax.axis_index("c")
    mine_first = jnp.stack([jnp.where(c == 0, 2 * x + k, 2 * k + y) for k in range(2)]).astype(jnp.int32)
    mine_second = jnp.where(c == 0, y, x).astype(jnp.int32).reshape(1)
    swapped = _pair_swap("rs_pair_swap", gs)
    parts = [_add_own_half("rs_add_pair", g, r) for g, r in zip(gs, swapped)]
    got = _halve_first("rs_halve_first", parts)
    kept = [_add_selected("rs_add_first", t, r, mine_first, BF16) for t, r in zip(parts, got)]
    got = _halve_second("rs_halve_second", kept)
    reds = [_add_selected("rs_add_second", t, r, mine_second, F32) for t, r in zip(kept, got)]
    return _pair_share("rs_pair_share", reds)


def _adamw(name, g, w, m, v):
    c1 = 1.0 - ADAM_B1 ** ADAM_STEP
    c2 = 1.0 - ADAM_B2 ** ADAM_STEP
    nb, rows, cc = g.shape
    tl = _tile(rows, ROW_TILE, SUBLANE)

    def body(g_ref, w_ref, m_ref, v_ref, d_ref, mo_ref, vo_ref):
        gv = g_ref[...]
        mn = ADAM_B1 * m_ref[...] + (1.0 - ADAM_B1) * gv
        vn = ADAM_B2 * v_ref[...] + (1.0 - ADAM_B2) * (gv * gv)
        d_ref[...] = -ADAM_LR * ((mn / c1) / (jnp.sqrt(vn / c2) + ADAM_EPS) + ADAM_WD * w_ref[...])
        mo_ref[...] = mn
        vo_ref[...] = vn

    spec = pl.BlockSpec((1, tl, cc), lambda b, i: (b, i, 0))
    return _pcall(
        body, name=name, grid=(nb, rows // tl), in_specs=[spec] * 4, out_specs=[spec] * 3,
        out_shape=[jax.ShapeDtypeStruct(g.shape, F32)] * 3,
        compiler_params=_cparams(("parallel", "parallel"), 16 * tl * cc * 4),
    )(g, w, m, v)


def _train_step(a):
    x, target = a['x'][0], a['loss_target'][0]

    names = BF16_GATHER + F32_GATHER
    got = _ag_chips("ag_weights", [_stored(n, a[n]).astype(BF16) for n in BF16_GATHER] + [a[n] for n in F32_GATHER])
    ws = dict(zip(names, got))

    loss, gx, gs, grep = _device_step(x, target, ws, {n: a[n] for n in REPLICATED})
    loss = lax.psum(loss, ("x", "y", "c"))

    small = [n for n in MISC_SHARDED if n not in LARGE_SHARDED]
    small_shapes = [a[n].shape[1:] for n in small]
    rep_shapes = [a[n].shape for n in REPLICATED]
    small_rows = sum(_rows_of(s) for s in small_shapes)
    piece = _round_up(-(-sum(_rows_of(s) for s in rep_shapes) // (2 * N_CHIPS)), SUBLANE)
    rep_rows = 2 * N_CHIPS * piece
    rep_pack = _pack([grep[n] for n in REPLICATED], rep_rows).reshape(N_CHIPS, 2, piece, PACK_C)
    total = _round_up(small_rows + piece, ROW_TILE // 2)
    misc = _pack([gs[n] for n in small] + [rep_pack], total, lead=2)
    reds = _reduce_scatter([gs[n] for n in LARGE_SHARDED] + [misc])
    grad = dict(zip(LARGE_SHARDED, reds[:-1]))
    grad.update(zip(small, _unpack(reds[-1], small_shapes, lead=1)))
    rep_mine = reds[-1][:, small_rows:small_rows + piece]
    rep_all = _ag_chips("ag_replicated", [rep_mine])[0].reshape(1, rep_rows, PACK_C)

    outs = {}
    for n in SHARDED:
        shp = grad[n].shape
        three_d = (shp[0], -1, shp[-1])
        params = (grad[n], *(_stored(n, a[p + n]) for p in ('', 'm_', 'v_')))
        res3 = _adamw("adamw", *(t.reshape(three_d) for t in params))
        for kind, arr in zip(('grad', 'delta', 'new_m', 'new_v'), (grad[n], *res3)):
            outs[kind + '_' + n] = _stored(n, arr.reshape(shp))
    w, m, v = (_pack([a[p + n] for n in REPLICATED], rep_rows)[None] for p in ('', 'm_', 'v_'))
    res3 = _adamw("adamw_replicated", rep_all, w, m, v)
    for kind, arr in zip(('grad', 'delta', 'new_m', 'new_v'), (rep_all, *res3)):
        for n, t in zip(REPLICATED, _unpack(arr[0], rep_shapes)):
            outs[kind + '_' + n] = t
    res = [loss, gx[None]]
    for kind in ('grad', 'delta', 'new_m', 'new_v'):
        res += [outs[kind + '_' + n] for n in W_NAMES]
    return tuple(res)


def kernel(x, ffn1_norm, ffn1_w_gate, ffn1_w_up, ffn1_w_down, mix_norm, w_in, s5_lambda_re, s5_lambda_im, s5_log_dt, s5_b_re, s5_b_im, s5_c_re, s5_c_im, s5_d, s5_w_glu, gla_w_alpha, gla_b_alpha, gla_norm, attn_q_norm, attn_k_norm, w_branch_s5, w_branch_gla, w_branch_attn, w_merge_gate, b_merge_gate, w_out, ffn2_norm, ffn2_w_gate, ffn2_w_up, ffn2_w_down, final_norm, loss_target, m_ffn1_norm, m_ffn1_w_gate, m_ffn1_w_up, m_ffn1_w_down, m_mix_norm, m_w_in, m_s5_lambda_re, m_s5_lambda_im, m_s5_log_dt, m_s5_b_re, m_s5_b_im, m_s5_c_re, m_s5_c_im, m_s5_d, m_s5_w_glu, m_gla_w_alpha, m_gla_b_alpha, m_gla_norm, m_attn_q_norm, m_attn_k_norm, m_w_branch_s5, m_w_branch_gla, m_w_branch_attn, m_w_merge_gate, m_b_merge_gate, m_w_out, m_ffn2_norm, m_ffn2_w_gate, m_ffn2_w_up, m_ffn2_w_down, m_final_norm, v_ffn1_norm, v_ffn1_w_gate, v_ffn1_w_up, v_ffn1_w_down, v_mix_norm, v_w_in, v_s5_lambda_re, v_s5_lambda_im, v_s5_log_dt, v_s5_b_re, v_s5_b_im, v_s5_c_re, v_s5_c_im, v_s5_d, v_s5_w_glu, v_gla_w_alpha, v_gla_b_alpha, v_gla_norm, v_attn_q_norm, v_attn_k_norm, v_w_branch_s5, v_w_branch_gla, v_w_branch_attn, v_w_merge_gate, v_b_merge_gate, v_w_out, v_ffn2_norm, v_ffn2_w_gate, v_ffn2_w_up, v_ffn2_w_down, v_final_norm):
    return _train_step(dict(locals()))
```

```python
import functools
import math

import jax
import jax.numpy as jnp
import numpy as np
from jax import lax
from jax.experimental import pallas as pl
from jax.experimental.pallas import tpu as pltpu

F32 = jnp.float32
BF16 = jnp.bfloat16

D_MODEL = 1024
D_FF = 2816
DEPTH = 2
EPS = 1e-6
S5_G, S5_H, S5_P = 32, 16, 64
S5_W = S5_G * S5_H
S5_N = S5_G * S5_P
S5_TG = 8
S5_TILES = S5_G // S5_TG
S5_TC, S5_TS = S5_TG * S5_H, S5_TG * S5_P
GLA_H, GLA_D = 4, 128
GLA_W = GLA_H * GLA_D
GLA_R = 16
GLA_TAU = 16.0
GLA_C = 64
AT_QH, AT_KH, AT_D = 8, 2, 64
AT_W = AT_QH * AT_D
AT_KW = AT_KH * AT_D
GRID_W = 64
ROPE_BASE = 10000.0
ZP_W = 3456
ZC_AK, ZC_AV, ZC_Z = 24, 25, 26

ADAM_LR, ADAM_B1, ADAM_B2, ADAM_EPS, ADAM_WD, ADAM_STEP = 0.001, 0.9, 0.999, 1e-08, 0.01, 10

GRAD_DT = BF16
VMEM_LIMIT_V7X = 56 * 1024 * 1024
LANE = 128
SUBLANE = 8

W_NAMES = ['ffn1_norm', 'ffn1_w_gate', 'ffn1_w_up', 'ffn1_w_down', 'mix_norm', 'w_in', 's5_lambda_re', 's5_lambda_im',
           's5_log_dt', 's5_b_re', 's5_b_im', 's5_c_re', 's5_c_im', 's5_d', 's5_w_glu', 'gla_w_alpha', 'gla_b_alpha',
           'gla_norm', 'attn_q_norm', 'attn_k_norm', 'w_branch_s5', 'w_branch_gla', 'w_branch_attn', 'w_merge_gate',
           'b_merge_gate', 'w_out', 'ffn2_norm', 'ffn2_w_gate', 'ffn2_w_up', 'ffn2_w_down', 'final_norm']
SHARD_AXIS = {'ffn1_w_gate': 2, 'ffn1_w_up': 2, 'ffn1_w_down': 1, 'w_in': 2, 's5_w_glu': 1, 'gla_w_alpha': 3,
              'gla_b_alpha': 2, 'w_branch_s5': 2, 'w_branch_gla': 2, 'w_branch_attn': 2, 'w_merge_gate': 2,
              'w_out': 1, 'ffn2_w_gate': 2, 'ffn2_w_up': 2, 'ffn2_w_down': 1}
F32_GATHER = ['gla_w_alpha', 'gla_b_alpha']
BF16_GATHER = [n for n in W_NAMES if n in SHARD_AXIS and n not in F32_GATHER]
SHARDED = [n for n in W_NAMES if n in SHARD_AXIS]
REPLICATED = [n for n in W_NAMES if n not in SHARD_AXIS]
N_CHIPS = 4
MESH = pl.DeviceIdType.MESH


def _tile(n, target, mult):
    best = None
    for t in range(mult, min(n, target) + 1, mult):
        if n % t == 0:
            best = t
    return best if best is not None else n


def _cparams(sem, vmem_bytes):
    limit = int(min(VMEM_LIMIT_V7X, max(32 * 1024 * 1024, vmem_bytes * 5 // 4)))
    return pltpu.CompilerParams(dimension_semantics=sem, vmem_limit_bytes=limit)


def _pcall(body, **kw):
    shapes = kw.pop('out_shape')
    many = isinstance(shapes, (list, tuple))
    pinned = [pltpu.HBM(t.shape, t.dtype) for t in (shapes if many else [shapes])]
    call = pl.pallas_call(body, out_shape=pinned if many else pinned[0], **kw)

    def run(*ops):
        return call(*[o if jnp.issubdtype(o.dtype, jnp.integer) else pltpu.with_memory_space_constraint(o, pltpu.HBM)
                      for o in ops])

    return run


def _cw(arr, cb, width):
    return (arr, cb, width)


def _win(a):
    if isinstance(a, tuple):
        return a
    return (a, 0, a.shape[-1])


def _nbytes(shape, dtype):
    return int(np.prod(shape)) * jnp.dtype(dtype).itemsize


def _rowwise(name, fn, rows, fulls, outs, accs=(), tl=256):
    rows = [_win(r) for r in rows]
    n_tok = rows[0][0].shape[0]
    tl = _tile(n_tok, tl, 16)
    nr, nf, no = len(rows), len(fulls), len(outs)

    def body(*refs):
        vals = [r[...] for r in refs[:nr + nf]]
        res = fn(*vals)
        if not isinstance(res, (tuple, list)):
            res = (res,)
        for k in range(no):
            o = refs[nr + nf + k]
            o[...] = res[k].astype(o.dtype)
        if accs:
            @pl.when(pl.program_id(0) == 0)
            def _():
                for k in range(len(accs)):
                    a = refs[nr + nf + no + k]
                    a[...] = jnp.zeros(a.shape, a.dtype)
            for k in range(len(accs)):
                a = refs[nr + nf + no + k]
                a[...] += res[no + k]

    in_specs = [pl.BlockSpec((tl, w), functools.partial(lambda i, cb: (i, cb), cb=cb)) for (_, cb, w) in rows]
    in_specs += [pl.BlockSpec(f.shape, functools.partial(lambda i, nd: (0,) * nd, nd=f.ndim)) for f in fulls]
    out_specs = [pl.BlockSpec((tl, c), lambda i: (i, 0)) for (c, _) in outs]
    out_specs += [pl.BlockSpec((r, c), lambda i: (0, 0)) for (r, c) in accs]
    out_shape = [jax.ShapeDtypeStruct((n_tok, c), dt) for (c, dt) in outs]
    out_shape += [jax.ShapeDtypeStruct((r, c), F32) for (r, c) in accs]
    vm = 2 * sum(_nbytes((tl, w), a.dtype) for (a, _, w) in rows)
    vm += 2 * sum(_nbytes(f.shape, f.dtype) for f in fulls)
    vm += 2 * sum(_nbytes((tl, c), dt) for (c, dt) in outs)
    vm += 8 * max([_nbytes((tl, w), F32) for (_, _, w) in rows] + [_nbytes((tl, c), F32) for (c, _) in outs])
    return _pcall(
        body, name=name, grid=(n_tok // tl,), in_specs=in_specs, out_specs=out_specs, out_shape=out_shape,
        compiler_params=_cparams(("arbitrary",) if accs else ("parallel",), vm),
    )(*[r[0] for r in rows], *fulls)


def _mm(name, a, b, *, ta=False, tb=False, out_dtype=F32, alpha=1.0, add=None, tm=512, tn=512):
    a_arr, a_cb, a_w = _win(a)
    b_arr, b_cb, b_w = _win(b)
    if ta:
        kdim, m = a_arr.shape[0], a_w
    else:
        m, kdim = a_arr.shape[0], a_w
    if tb:
        n, kb = b_arr.shape[0], b_w
    else:
        kb, n = b_arr.shape[0], b_w
    assert kdim == kb, (name, kdim, kb)
    if not ta and kdim <= 1024:
        tm = 2 * tm
    tm = _tile(m, tm, LANE if ta else 16)
    tn = _tile(n, tn, LANE)
    if ta:
        a_spec = pl.BlockSpec((kdim, tm), lambda i, j: (0, a_cb * (a_w // tm) + i))
        a_blk = (kdim, tm)
    else:
        a_spec = pl.BlockSpec((tm, kdim), lambda i, j: (i, a_cb))
        a_blk = (tm, kdim)
    if tb:
        b_spec = pl.BlockSpec((tn, kdim), lambda i, j: (j, b_cb))
        b_blk = (tn, kdim)
    else:
        b_spec = pl.BlockSpec((kdim, tn), lambda i, j: (0, b_cb * (b_w // tn) + j))
        b_blk = (kdim, tn)
    dims = (((0 if ta else 1,), (1 if tb else 0,)), ((), ()))
    has_add = add is not None

    def body(*refs):
        a_ref, b_ref = refs[0], refs[1]
        o_ref = refs[-1]
        acc = lax.dot_general(a_ref[...].astype(BF16), b_ref[...].astype(BF16), dims, preferred_element_type=F32)
        if alpha != 1.0:
            acc = acc * alpha
        if has_add:
            acc = acc + refs[2][...].astype(F32)
        o_ref[...] = acc.astype(o_ref.dtype)

    in_specs = [a_spec, b_spec]
    ops = [a_arr, b_arr]
    if has_add:
        in_specs.append(pl.BlockSpec((tm, tn), lambda i, j: (i, j)))
        ops.append(add)
    vm = 2 * (_nbytes(a_blk, a_arr.dtype) + _nbytes(b_blk, b_arr.dtype) + _nbytes((tm, tn), out_dtype))
    vm += _nbytes(a_blk, BF16) + _nbytes(b_blk, BF16) + 3 * _nbytes((tm, tn), F32)
    return _pcall(
        body, name=name, grid=(m // tm, n // tn), in_specs=in_specs,
        out_specs=pl.BlockSpec((tm, tn), lambda i, j: (i, j)),
        out_shape=jax.ShapeDtypeStruct((m, n), out_dtype),
        compiler_params=_cparams(("parallel", "parallel"), vm),
    )(*ops)


def _sigmoid(x):
    return 1.0 / (1.0 + jnp.exp(-x))


def _rms(x, gain):
    r = lax.rsqrt(jnp.mean(x * x, axis=-1, keepdims=True) + EPS)
    return x * r * gain


def _rms_bwd(x, gain, dh):
    r = lax.rsqrt(jnp.mean(x * x, axis=-1, keepdims=True) + EPS)
    xh = x * r
    dg = jnp.sum(dh * xh, axis=0, keepdims=True)
    dxh = dh * gain
    dx = r * (dxh - xh * jnp.mean(dxh * xh, axis=-1, keepdims=True))
    return dx, dg


_GELU_C = math.sqrt(2.0 / math.pi)


def _gelu(y):
    return 0.5 * y * (1.0 + jnp.tanh(_GELU_C * (y + 0.044715 * y * y * y)))


def _gelu_grad(y):
    th = jnp.tanh(_GELU_C * (y + 0.044715 * y * y * y))
    return 0.5 * (1.0 + th) + 0.5 * y * (1.0 - th * th) * _GELU_C * (1.0 + 3.0 * 0.044715 * y * y)


def _log_sigmoid(x):
    return jnp.minimum(x, 0.0) - jnp.log(1.0 + jnp.exp(-jnp.abs(x)))


def _dot(a, b, dims=(((1,), (0,)), ((), ())), precision=None):
    return lax.dot_general(a, b, dims, preferred_element_type=F32, precision=precision)


def _dot_exact(x, m, dims=(((1,), (0,)), ((), ())), terms=3, m_first=False):
    m16 = m.astype(BF16)
    acc, rest = None, x
    for k in range(terms):
        piece = rest.astype(BF16)
        t = _dot(m16, piece, dims) if m_first else _dot(piece, m16, dims)
        acc = t if acc is None else acc + t
        if k + 1 < terms:
            rest = rest - piece.astype(F32)
    return acc


_NT = (((1,), (1,)), ((), ()))
_TN = (((0,), (0,)), ((), ()))


def _scan_core(b_ref, are_ref, aim_ref, pw, last, of32, *, rev, xf32=None, da_ref=None, lc=512):
    nb, nc = b_ref.shape[0] // SUBLANE, S5_N // lc
    row = lax.broadcasted_iota(jnp.int32, (SUBLANE, lc), 0)

    def cmul(ar, ai, xr, xi):
        return ar * xr - ai * xi, ar * xi + ai * xr

    def hs(c, xr, xi):
        cs = pl.ds(c * lc, lc)
        for k, slot in ((1, 0), (2, 2), (4, 4)):
            ar, ai = pw[slot, :, cs], pw[slot + 1, :, cs]
            if rev:
                sr, si = pltpu.roll(xr, SUBLANE - k, 0), pltpu.roll(xi, SUBLANE - k, 0)
                keep = row < SUBLANE - k
            else:
                sr, si = pltpu.roll(xr, k, 0), pltpu.roll(xi, k, 0)
                keep = row >= k
            sr, si = jnp.where(keep, sr, 0.0), jnp.where(keep, si, 0.0)
            pr, pi = cmul(ar, ai, sr, si)
            xr, xi = xr + pr, xi + pi
        return xr, xi

    @pl.when(pl.program_id(0) == 0)
    def _():
        last[...] = jnp.zeros(last.shape, F32)
        if da_ref is not None:
            da_ref[...] = jnp.zeros(da_ref.shape, F32)
        for c in range(nc):
            cs = pl.ds(c * lc, lc)
            a1r = jnp.broadcast_to(are_ref[:, cs], (SUBLANE, lc))
            a1i = jnp.broadcast_to(aim_ref[:, cs], (SUBLANE, lc))
            a2r, a2i = cmul(a1r, a1i, a1r, a1i)
            a4r, a4i = cmul(a2r, a2i, a2r, a2i)
            for slot, v in enumerate((a1r, a1i, a2r, a2i, a4r, a4i)):
                pw[slot, :, cs] = v
            first = SUBLANE - 1 if rev else 0
            pr, pi = hs(c, jnp.where(row == first, a1r, 0.0), jnp.where(row == first, a1i, 0.0))
            pw[6, :, cs] = pr
            pw[7, :, cs] = pi

    edge = 0 if rev else SUBLANE - 1

    def blk(j, carry):
        jj = (nb - 1 - j) if rev else j
        r0 = pl.multiple_of(jj * SUBLANE, SUBLANE)
        for c in range(nc):
            cre, cim = pl.ds(c * lc, lc), pl.ds(S5_N + c * lc, lc)
            xr, xi = hs(c, b_ref[pl.ds(r0, SUBLANE), cre], b_ref[pl.ds(r0, SUBLANE), cim])
            cr = jnp.broadcast_to(last[edge:edge + 1, cre], (SUBLANE, lc))
            ci = jnp.broadcast_to(last[edge:edge + 1, cim], (SUBLANE, lc))
            pr, pi = cmul(pw[6, :, cre], pw[7, :, cre], cr, ci)
            xr, xi = xr + pr, xi + pi
            if da_ref is not None:
                if rev:
                    qr, qi = pltpu.roll(xr, SUBLANE - 1, 0), pltpu.roll(xi, SUBLANE - 1, 0)
                    fill = row == SUBLANE - 1
                else:
                    qr, qi = pltpu.roll(xr, 1, 0), pltpu.roll(xi, 1, 0)
                    fill = row == 0
                qr, qi = jnp.where(fill, cr, qr), jnp.where(fill, ci, qi)
                zr, zi = xf32[pl.ds(r0, SUBLANE), cre], xf32[pl.ds(r0, SUBLANE), cim]
                da_ref[:, cre] += zr * qr + zi * qi
                da_ref[:, cim] += zr * qi - zi * qr
            last[:, cre] = xr
            last[:, cim] = xi
            of32[pl.ds(r0, SUBLANE), cre] = xr
            of32[pl.ds(r0, SUBLANE), cim] = xi
        return carry

    lax.fori_loop(0, nb, blk, 0)


def _s5_tiles():
    return [(slice((j % S5_TILES) * S5_TC, (j % S5_TILES + 1) * S5_TC), slice(j * S5_TS, (j + 1) * S5_TS))
            for j in range(2 * S5_TILES)]


def _s5_scratch(tb):
    return [pltpu.VMEM((8, SUBLANE, S5_N), F32), pltpu.VMEM((SUBLANE, 2 * S5_N), F32),
            pltpu.VMEM((tb, 2 * S5_N), F32), pltpu.VMEM((tb, 2 * S5_N), F32),
            pltpu.VMEM((S5_TC, 2 * S5_N), BF16), pltpu.VMEM((2 * S5_N, S5_TC), BF16)]


def _s5_fwd(name, z, a_re, a_im, wb, wc, add, *, rev, tb=512):
    n_tok = z.shape[0]
    tb = _tile(n_tok, tb, 16)
    nt = n_tok // tb
    has_add = add is not None

    def body(*refs):
        u_ref, are_ref, aim_ref, wb_ref, wc_ref = refs[:5]
        x_ref, y_ref, pw, last, of32, bu, wb16, wc16 = refs[5 + has_add:]

        @pl.when(pl.program_id(0) == 0)
        def _():
            wb16[...] = wb_ref[...].astype(BF16)
            wc16[...] = wc_ref[...].astype(BF16)

        u16 = u_ref[...].astype(BF16)
        for ch, st in _s5_tiles():
            bu[:, st] = _dot(u16[:, ch], wb16[:, st])
        _scan_core(bu, are_ref, aim_ref, pw, last, of32, rev=rev)
        x16 = of32[...].astype(BF16)
        x_ref[...] = x16
        ys = []
        for k in range(S5_TILES):
            re, im = slice(k * S5_TS, (k + 1) * S5_TS), slice(S5_N + k * S5_TS, S5_N + (k + 1) * S5_TS)
            ys.append(_dot(x16[:, re], wc16[re, :]) + _dot(x16[:, im], wc16[im, :]))
        y = jnp.concatenate(ys, axis=1)
        y_ref[...] = y + refs[5][...] if has_add else y

    tmap = (lambda t: (nt - 1 - t, 0)) if rev else (lambda t: (t, 0))
    const = lambda t: (0, 0)
    in_specs = [pl.BlockSpec((tb, S5_W), tmap), pl.BlockSpec((1, S5_N), const), pl.BlockSpec((1, S5_N), const),
                pl.BlockSpec((S5_TC, 2 * S5_N), const), pl.BlockSpec((2 * S5_N, S5_TC), const)]
    ops = [z, a_re, a_im, wb, wc]
    if has_add:
        in_specs.append(pl.BlockSpec((tb, S5_W), tmap))
        ops.append(add)
    return _pcall(
        body, name=name, grid=(nt,), in_specs=in_specs,
        out_specs=[pl.BlockSpec((tb, 2 * S5_N), tmap), pl.BlockSpec((tb, S5_W), tmap)],
        out_shape=[jax.ShapeDtypeStruct((n_tok, 2 * S5_N), BF16), jax.ShapeDtypeStruct((n_tok, S5_W), F32)],
        scratch_shapes=_s5_scratch(tb), compiler_params=_cparams(("arbitrary",), 10 * tb * 2 * S5_N * 4),
    )(*ops)


def _s5_bwd(name, dy, z, xs, a_re, a_im, wb, wc, add, *, rev, tb=256):
    n_tok = z.shape[0]
    tb = _tile(n_tok, tb, 16)
    nt = n_tok // tb

    def body(dy_ref, u_ref, xs_ref, are_ref, aim_ref, wb_ref, wc_ref, add_ref, du_ref, dwb_ref, dwc_ref, da_ref,
             pw, last, of32, gx, wb16, wc16, xf32):
        @pl.when(pl.program_id(0) == 0)
        def _():
            wb16[...] = wb_ref[...].astype(BF16)
            wc16[...] = wc_ref[...].astype(BF16)
            dwb_ref[...] = jnp.zeros(dwb_ref.shape, F32)
            dwc_ref[...] = jnp.zeros(dwc_ref.shape, F32)

        dy16, u16, x16 = dy_ref[...].astype(BF16), u_ref[...].astype(BF16), xs_ref[...]
        for ch, st in _s5_tiles():
            gx[:, st] = _dot(dy16[:, ch], wc16[st, :], _NT)
        xf32[...] = x16.astype(F32)
        _scan_core(gx, are_ref, aim_ref, pw, last, of32, rev=rev, xf32=xf32, da_ref=da_ref)
        g16 = of32[...].astype(BF16)
        dus = []
        for k in range(S5_TILES):
            re, im = slice(k * S5_TS, (k + 1) * S5_TS), slice(S5_N + k * S5_TS, S5_N + (k + 1) * S5_TS)
            dus.append(_dot(g16[:, re], wb16[:, re], _NT) + _dot(g16[:, im], wb16[:, im], _NT))
        du_ref[...] = jnp.concatenate(dus, axis=1) + add_ref[...]
        for ch, st in _s5_tiles():
            dwb_ref[:, st] += _dot(u16[:, ch], g16[:, st], _TN)
            dwc_ref[st, :] += _dot(x16[:, st], dy16[:, ch], _TN)

    tmap = (lambda t: (nt - 1 - t, 0)) if rev else (lambda t: (t, 0))
    const = lambda t: (0, 0)
    row_w, row_s = pl.BlockSpec((tb, S5_W), tmap), pl.BlockSpec((tb, 2 * S5_N), tmap)
    return _pcall(
        body, name=name, grid=(nt,),
        in_specs=[row_w, row_w, row_s, pl.BlockSpec((1, S5_N), const), pl.BlockSpec((1, S5_N), const),
                  pl.BlockSpec((S5_TC, 2 * S5_N), const), pl.BlockSpec((2 * S5_N, S5_TC), const), row_w],
        out_specs=[row_w, pl.BlockSpec((S5_TC, 2 * S5_N), const), pl.BlockSpec((2 * S5_N, S5_TC), const),
                   pl.BlockSpec((SUBLANE, 2 * S5_N), const)],
        out_shape=[jax.ShapeDtypeStruct((n_tok, S5_W), F32), jax.ShapeDtypeStruct((S5_TC, 2 * S5_N), F32),
                   jax.ShapeDtypeStruct((2 * S5_N, S5_TC), F32), jax.ShapeDtypeStruct((SUBLANE, 2 * S5_N), F32)],
        scratch_shapes=_s5_scratch(tb) + [pltpu.VMEM((tb, 2 * S5_N), F32)],
        compiler_params=_cparams(("arbitrary",), 12 * tb * 2 * S5_N * 4),
    )(dy, z, xs, a_re, a_im, wb, wc, add)


GLA_TB = 512


def _gla_block_terms(la, q, k, rev):
    trif = _chunk_mask(rev).astype(F32)
    chunks = [la[t:t + GLA_C] for t in range(0, la.shape[0], GLA_C)]
    bcum = jnp.concatenate([_dot_exact(c, trif, m_first=True) for c in chunks], axis=0)
    blast = _per_chunk_sum(la)
    eb, enb, ee = jnp.exp(bcum), jnp.exp(-bcum), jnp.exp(blast - bcum)
    return trif, jnp.exp(blast), eb, enb, ee, q * eb, k * enb, k * ee


def _per_chunk_sum(t):
    return jnp.concatenate([jnp.broadcast_to(jnp.sum(t[r:r + GLA_C], axis=0, keepdims=True), (GLA_C, t.shape[1]))
                            for r in range(0, t.shape[0], GLA_C)], axis=0)


def _chunk_mask(rev):
    ri = lax.broadcasted_iota(jnp.int32, (GLA_C, GLA_C), 0)
    ci = lax.broadcasted_iota(jnp.int32, (GLA_C, GLA_C), 1)
    return (ri <= ci) if rev else (ri >= ci)


def _gla_fwd(name, z, la, *, rev):
    n_tok = z.shape[0]
    tb = _tile(n_tok, GLA_TB, GLA_C)
    nblk, per = n_tok // tb, tb // GLA_C
    scale = GLA_D ** -0.5

    def body(q_ref, k_ref, v_ref, la_ref, o_ref, s_ref, st, qd_s, kd_s, ke_s, v_s, dec_s):
        @pl.when(pl.program_id(0) == 0)
        def _():
            st[...] = jnp.zeros(st.shape, F32)

        _, dec, _, _, _, qd, kd, ke = _gla_block_terms(la_ref[...], q_ref[...].astype(F32) * scale,
                                                                 k_ref[...].astype(F32), rev)
        qd_s[...], kd_s[...], ke_s[...] = qd.astype(BF16), kd.astype(BF16), ke.astype(BF16)
        v_s[...] = v_ref[...].astype(BF16)
        dec_s[...] = dec
        tri = _chunk_mask(rev)
        for step in range(per):
            t = per - 1 - step if rev else step
            rows = pl.ds(t * GLA_C, GLA_C)
            for h in range(GLA_H):
                lanes = pl.ds(h * GLA_D, GLA_D)
                qd16, v16 = qd_s[rows, lanes], v_s[rows, lanes]
                sc = jnp.where(tri, _dot(qd16, kd_s[rows, lanes], _NT), 0.0)
                s16 = st[h].astype(BF16)
                s_ref[h, step] = s16
                o_ref[rows, lanes] = _dot(sc.astype(BF16), v16) + _dot(qd16, s16, _NT)
                st[h] = st[h] * dec_s[pl.ds(t * GLA_C, 1), lanes] + _dot(v16, ke_s[rows, lanes], _TN)

    bmap = (lambda n: nblk - 1 - n) if rev else (lambda n: n)

    def zspec(cb):
        return pl.BlockSpec((tb, GLA_W), lambda n: (bmap(n), cb))

    half = pltpu.VMEM((tb, GLA_W), BF16)
    return _pcall(
        body, name=name, grid=(nblk,),
        in_specs=[zspec(1), zspec(2), zspec(3), zspec(0)],
        out_specs=[zspec(0), pl.BlockSpec((GLA_H, per, GLA_D, GLA_D), lambda n: (0, n, 0, 0))],
        out_shape=[jax.ShapeDtypeStruct((n_tok, GLA_W), F32),
                   jax.ShapeDtypeStruct((GLA_H, n_tok // GLA_C, GLA_D, GLA_D), BF16)],
        scratch_shapes=[pltpu.VMEM((GLA_H, GLA_D, GLA_D), F32), half, half, half, half, pltpu.VMEM((tb, GLA_W), F32)],
        compiler_params=_cparams(("arbitrary",), 40 * tb * GLA_W * 4),
    )(z, z, z, la)


def _gla_bwd(name, z, la, s_prev, do, *, rev):
    n_tok = z.shape[0]
    tb = _tile(n_tok, GLA_TB, GLA_C)
    nblk, per = n_tok // tb, tb // GLA_C
    scale = GLA_D ** -0.5

    def body(q_ref, k_ref, v_ref, la_ref, s_ref, do_ref, dq_ref, dk_ref, dv_ref, dla_ref,
             dst, qd_s, kd_s, ke_s, v_s, do_s, dec_s, dqd_s, dkd_s, dke_s, dd_s):
        @pl.when(pl.program_id(0) == 0)
        def _():
            dst[...] = jnp.zeros(dst.shape, F32)

        trif, dec, eb, enb, ee, qd, kd, ke = _gla_block_terms(la_ref[...], q_ref[...].astype(F32) * scale,
                                                                 k_ref[...].astype(F32), rev)
        qd_s[...], kd_s[...], ke_s[...] = qd.astype(BF16), kd.astype(BF16), ke.astype(BF16)
        v_s[...] = v_ref[...].astype(BF16)
        do_s[...] = do_ref[...].astype(BF16)
        dec_s[...] = dec
        tri = _chunk_mask(rev)
        for step in range(per):
            t = step if rev else per - 1 - step
            rows = pl.ds(t * GLA_C, GLA_C)
            for h in range(GLA_H):
                lanes = pl.ds(h * GLA_D, GLA_D)
                qd16, kd16, ke16 = qd_s[rows, lanes], kd_s[rows, lanes], ke_s[rows, lanes]
                v16, do16 = v_s[rows, lanes], do_s[rows, lanes]
                sc16 = jnp.where(tri, _dot(qd16, kd16, _NT), 0.0).astype(BF16)
                dsc16 = jnp.where(tri, _dot(do16, v16, _NT), 0.0).astype(BF16)
                s16 = s_ref[h, per - 1 - step]
                dsn = dst[h]
                dsn16 = dsn.astype(BF16)
                dec_c = dec_s[pl.ds(t * GLA_C, 1), lanes]
                dqd_s[rows, lanes] = _dot(dsc16, kd16) + _dot(do16, s16)
                dkd_s[rows, lanes] = _dot(dsc16, qd16, _TN)
                dke_s[rows, lanes] = _dot(v16, dsn16)
                dv_ref[rows, lanes] = _dot(sc16, do16, _TN) + _dot(ke16, dsn16, _NT)
                ddec = jnp.sum(dsn * s16.astype(F32), axis=0, keepdims=True) * dec_c
                dd_s[rows, lanes] = jnp.broadcast_to(ddec, (GLA_C, GLA_D))
                dst[h] = dsn * dec_c + _dot(do16, qd16, _TN)
        dqd, dkd, dke = dqd_s[...], dkd_s[...], dke_s[...]
        dq_ref[...] = dqd * eb * scale
        dk_ref[...] = dkd * enb + dke * ee
        db = dqd * qd - dkd * kd - dke * ke
        dcum = jnp.concatenate([_dot_exact(db[r:r + GLA_C], trif, _TN, m_first=True) for r in range(0, tb, GLA_C)], axis=0)
        dla_ref[...] = dcum + _per_chunk_sum(dke * ke) + dd_s[...]

    bmap = (lambda n: n) if rev else (lambda n: nblk - 1 - n)

    def zspec(cb):
        return pl.BlockSpec((tb, GLA_W), lambda n: (bmap(n), cb))

    o4 = jax.ShapeDtypeStruct((n_tok, GLA_W), F32)
    half, full = pltpu.VMEM((tb, GLA_W), BF16), pltpu.VMEM((tb, GLA_W), F32)
    return _pcall(
        body, name=name, grid=(nblk,),
        in_specs=[zspec(1), zspec(2), zspec(3), zspec(0),
                  pl.BlockSpec((GLA_H, per, GLA_D, GLA_D), lambda n: (0, nblk - 1 - n, 0, 0)), zspec(0)],
        out_specs=[zspec(0)] * 4, out_shape=[o4] * 4,
        scratch_shapes=[pltpu.VMEM((GLA_H, GLA_D, GLA_D), F32), half, half, half, half, half, full, full, full, full, full],
        compiler_params=_cparams(("arbitrary",), 64 * tb * GLA_W * 4),
    )(z, z, z, la, s_prev, do)


_PAIRS = AT_W // LANE
_PAIRS_PER_KV = _PAIRS // AT_KH


def _attn_fwd(name, q, kd, vd, tq=256):
    n_tok = q.shape[0]
    tq = _tile(n_tok, tq, 16)

    def body(q_ref, k_ref, v_ref, o_ref, l_ref):
        qv, kv, vv = q_ref[...], k_ref[...], v_ref[...]
        low = lax.broadcasted_iota(jnp.int32, qv.shape, 1) < AT_D
        res = []
        for keep in (low, jnp.logical_not(low)):
            s = _dot(jnp.where(keep, qv, jnp.zeros_like(qv)), kv, _NT)
            m = jnp.max(s, axis=-1, keepdims=True)
            p = jnp.exp(s - m)
            den = jnp.sum(p, axis=-1, keepdims=True)
            res.append((_dot(p.astype(BF16), vv) / den, m + jnp.log(den)))
        o_ref[...] = jnp.where(low, res[0][0], res[1][0])
        l_ref[...] = jnp.where(low, res[0][1], res[1][1])

    qspec = pl.BlockSpec((tq, LANE), lambda p, i: (i, p))
    kspec = pl.BlockSpec((n_tok, LANE), lambda p, i: (0, p // _PAIRS_PER_KV))
    vm = 5 * _nbytes((tq, n_tok), F32) + 8 * _nbytes((n_tok, LANE), BF16)
    out = jax.ShapeDtypeStruct((n_tok, AT_W), F32)
    return _pcall(
        body, name=name, grid=(_PAIRS, n_tok // tq), in_specs=[qspec, kspec, kspec], out_specs=[qspec, qspec],
        out_shape=[out, out], compiler_params=_cparams(("parallel", "parallel"), vm),
    )(q, kd, vd)


def _attn_bwd(name, q, kd, vd, o, lse, do, tq=256):
    n_tok = q.shape[0]
    tq = _tile(n_tok, tq, 16)

    def body(q_ref, k_ref, v_ref, o_ref, l_ref, do_ref, dq_ref, dk_ref, dv_ref):
        @pl.when((pl.program_id(1) == 0) & (pl.program_id(2) == 0))
        def _():
            dk_ref[...] = jnp.zeros(dk_ref.shape, F32)
            dv_ref[...] = jnp.zeros(dv_ref.shape, F32)

        qv, kv, vv = q_ref[...], k_ref[...], v_ref[...]
        dof, lv = do_ref[...], l_ref[...]
        do16 = dof.astype(BF16)
        doo = dof * o_ref[...]
        low = lax.broadcasted_iota(jnp.int32, qv.shape, 1) < AT_D
        dq = jnp.zeros(qv.shape, F32)
        for keep in (low, jnp.logical_not(low)):
            qm = jnp.where(keep, qv, jnp.zeros_like(qv))
            dom = jnp.where(keep, do16, jnp.zeros_like(do16))
            lh = jnp.max(jnp.where(keep, lv, -1e30), axis=-1, keepdims=True)
            p = jnp.exp(_dot(qm, kv, _NT) - lh)
            delta = jnp.sum(jnp.where(keep, doo, 0.0), axis=-1, keepdims=True)
            ds16 = (p * (_dot(dom, vv, _NT) - delta)).astype(BF16)
            dq = jnp.where(keep, _dot(ds16, kv), dq)
            dk_ref[...] += _dot(ds16, qm, _TN)
            dv_ref[...] += _dot(p.astype(BF16), dom, _TN)
        dq_ref[...] = dq

    qspec = pl.BlockSpec((tq, LANE), lambda g, j, i: (i, g * _PAIRS_PER_KV + j))
    kspec = pl.BlockSpec((n_tok, LANE), lambda g, j, i: (0, g))
    vm = 7 * _nbytes((tq, n_tok), F32) + 12 * _nbytes((n_tok, LANE), F32)
    dup = jax.ShapeDtypeStruct((n_tok, AT_KH * LANE), F32)
    return _pcall(
        body, name=name, grid=(AT_KH, _PAIRS_PER_KV, n_tok // tq),
        in_specs=[qspec, kspec, kspec, qspec, qspec, qspec], out_specs=[qspec, kspec, kspec],
        out_shape=[jax.ShapeDtypeStruct((n_tok, AT_W), F32), dup, dup],
        compiler_params=_cparams(("parallel", "arbitrary", "arbitrary"), vm),
    )(q, kd, vd, o, lse, do)


def _s5_prep(lam_re, lam_im, log_dt, b_re, b_im, c_re, c_im):
    dt = jnp.exp(log_dt)[:, None]
    er = jnp.exp(lam_re * dt)
    a_re, a_im = er * jnp.cos(lam_im * dt), er * jnp.sin(lam_im * dt)
    nr, ni = a_re - 1.0, a_im
    den = lam_re * lam_re + lam_im * lam_im
    f_re, f_im = (nr * lam_re + ni * lam_im) / den, (ni * lam_re - nr * lam_im) / den
    bb_re = f_re[..., None] * b_re - f_im[..., None] * b_im
    bb_im = f_re[..., None] * b_im + f_im[..., None] * b_re
    def tiles(t, axis):
        _, a, b = t.shape
        eye = jnp.eye(S5_TG, dtype=t.dtype)
        blocks = t.reshape(S5_TILES, S5_TG, a, 1, b) * eye[None, :, None, :, None]
        if axis == 0:
            return blocks.reshape(S5_TILES * S5_TG * a, S5_TG * b)
        return blocks.transpose(1, 2, 0, 3, 4).reshape(S5_TG * a, S5_TILES * S5_TG * b)

    w_in = jnp.concatenate([tiles(bb_re.transpose(0, 2, 1), 1), tiles(bb_im.transpose(0, 2, 1), 1)], axis=1)
    w_out = jnp.concatenate([tiles(c_re.transpose(0, 2, 1), 0), -tiles(c_im.transpose(0, 2, 1), 0)], axis=0)
    return a_re.reshape(1, S5_N), a_im.reshape(1, S5_N), w_in, w_out


def _rope_tables(n_tok):
    quarter = AT_D // 4
    d = np.arange(AT_D)
    e = d % (AT_D // 2)
    inv = (ROPE_BASE ** (-(e % quarter).astype(np.float64) * 2.0 / (AT_D // 2))).astype(np.float32)
    sign = np.where(e < quarter, -1.0, 1.0).astype(np.float32)
    rot = np.zeros((AT_D, AT_D), np.float32)
    rot[np.where(e < quarter, d + quarter, d - quarter), d] = 1.0
    pos = jnp.arange(n_tok, dtype=jnp.int32)
    axis_pos = jnp.where(jnp.asarray(d // (AT_D // 2) == 0)[None, :], (pos // GRID_W)[:, None], (pos % GRID_W)[:, None])
    ang = axis_pos.astype(F32) * jnp.asarray(inv)[None, :]
    return jnp.cos(ang), jnp.sin(ang) * jnp.asarray(sign)[None, :], rot


def _head_consts(n_tok, heads):
    cos, sin, rot = _rope_tables(n_tok)
    eye = np.eye(heads, dtype=np.float32)
    mean = np.kron(eye, np.full((AT_D, AT_D), 1.0 / AT_D, np.float32))
    return jnp.tile(cos, (1, heads)), jnp.tile(sin, (1, heads)), jnp.asarray(mean), jnp.asarray(np.kron(eye, rot))


def _pad_w_in(wt):
    return jnp.concatenate([wt[:2560], wt[2592:3360], wt[2560:2592], jnp.zeros((96, wt.shape[1]), wt.dtype)], axis=0)


def _unpad_w_in(gt):
    return jnp.concatenate([gt[:2560], gt[3328:3360], gt[2560:3328]], axis=0)


def _ffn_fwd(x, g, wg, wu, wd, d):
    n_tok = x.shape[0]
    fq = wd.shape[-2]
    tm = _tile(n_tok, 256, 16)
    w_bytes = N_CHIPS * fq * D_MODEL * 2

    def up(x_ref, g_ref, wg_ref, wu_ref, h_ref, a_ref, b_ref, s_ref):
        hv = _rms(x_ref[...], g_ref[...]).astype(BF16)
        h_ref[...] = hv
        for j in range(N_CHIPS):
            av, bv = _dot(hv, wg_ref[j, 0], _NT), _dot(hv, wu_ref[j, 0], _NT)
            a_ref[j] = av.astype(BF16)
            b_ref[j] = bv.astype(BF16)
            s_ref[j] = (av * _sigmoid(av) * bv).astype(BF16)

    wspec = pl.BlockSpec((N_CHIPS, 1, fq, D_MODEL), lambda i: (0, d, 0, 0))
    aspec = pl.BlockSpec((N_CHIPS, tm, fq), lambda i: (0, i, 0))
    xspec = pl.BlockSpec((tm, D_MODEL), lambda i: (i, 0))
    act = jax.ShapeDtypeStruct((N_CHIPS, n_tok, fq), BF16)
    h, a, b, s = _pcall(
        up, name="ffn_up", grid=(n_tok // tm,),
        in_specs=[xspec, pl.BlockSpec((1, D_MODEL), lambda i: (0, 0)), wspec, wspec],
        out_specs=[xspec, aspec, aspec, aspec],
        out_shape=[jax.ShapeDtypeStruct((n_tok, D_MODEL), BF16), act, act, act],
        compiler_params=_cparams(("parallel",), 4 * w_bytes + 6 * N_CHIPS * tm * fq * 2 + 12 * tm * fq * 4
                                 + 8 * tm * D_MODEL * 4),
    )(x, g, wg, wu)

    def down(s_ref, w_ref, x_ref, o_ref):
        acc = _dot(s_ref[0], w_ref[0, 0])
        for j in range(1, N_CHIPS):
            acc = acc + _dot(s_ref[j], w_ref[j, 0])
        o_ref[...] = x_ref[...] + 0.5 * acc

    xo = _pcall(
        down, name="ffn_down", grid=(n_tok // tm,), in_specs=[aspec, wspec, xspec], out_specs=xspec,
        out_shape=jax.ShapeDtypeStruct((n_tok, D_MODEL), F32),
        compiler_params=_cparams(("parallel",), 2 * w_bytes + 2 * N_CHIPS * tm * fq * 2 + 8 * tm * D_MODEL * 4),
    )(s, wd, x)
    return xo, (x, h, a, b, s)


def _ffn_bwd(dy, dy16, saved, g, wg, wu, wd, d, into):
    x, h, a, b, s = saved
    n_tok = x.shape[0]
    fq = wd.shape[-2]
    tm = _tile(n_tok, 512, 16)

    ta_ = _tile(n_tok, 256, 16)

    def dact(dy_ref, w_ref, a_ref, b_ref, da_ref, db_ref):
        dyv = dy_ref[...]
        for j in range(N_CHIPS):
            ds = 0.5 * _dot(dyv, w_ref[j, 0], _NT)
            av, bv = a_ref[j].astype(F32), b_ref[j].astype(F32)
            sg = _sigmoid(av)
            da_ref[j] = (ds * bv * sg * (1.0 + av * (1.0 - sg))).astype(BF16)
            db_ref[j] = (ds * av * sg).astype(BF16)

    aspec = pl.BlockSpec((N_CHIPS, ta_, fq), lambda i: (0, i, 0))
    act = jax.ShapeDtypeStruct((N_CHIPS, n_tok, fq), BF16)
    da, db = _pcall(
        dact, name="ffn_dact", grid=(n_tok // ta_,),
        in_specs=[pl.BlockSpec((ta_, D_MODEL), lambda i: (i, 0)),
                  pl.BlockSpec((N_CHIPS, 1, fq, D_MODEL), lambda i: (0, d, 0, 0)), aspec, aspec],
        out_specs=[aspec, aspec], out_shape=[act, act],
        compiler_params=_cparams(("parallel",), 2 * N_CHIPS * fq * D_MODEL * 2 + 8 * N_CHIPS * ta_ * fq * 2
                                 + 12 * ta_ * fq * 4),
    )(dy16, wd, a, b)

    any_spec = pl.BlockSpec(memory_space=pl.ANY)
    alias = {} if into is None else {2: 0}
    tn = _tile(D_MODEL, 512, LANE)

    def dw_call(t, rhs, scale, prev):
        def dw(t_ref, r_ref, *rest):
            rest[-1][0, 0] = (scale * _dot(t_ref[0], r_ref[...], _TN)).astype(GRAD_DT)

        return _pcall(
            dw, name="ffn_dw", grid=(N_CHIPS, D_MODEL // tn),
            in_specs=[pl.BlockSpec((1, n_tok, fq), lambda j, n: (j, 0, 0)), pl.BlockSpec((n_tok, tn), lambda j, n: (0, n))]
            + ([] if prev is None else [any_spec]),
            out_specs=pl.BlockSpec((1, 1, fq, tn), lambda j, n: (j, d, 0, n)),
            out_shape=jax.ShapeDtypeStruct((N_CHIPS, DEPTH, fq, D_MODEL), GRAD_DT), input_output_aliases=alias,
            compiler_params=_cparams(("parallel", "parallel"), 6 * n_tok * (fq + tn) * 2 + 4 * fq * tn * 4),
        )(t, rhs, *([] if prev is None else [prev]))

    dwg = dw_call(da, h, 1.0, None if into is None else into[0])
    dwu = dw_call(db, h, 1.0, None if into is None else into[1])
    dwd = dw_call(s, dy16, 0.5, None if into is None else into[2])
    tr = _tile(n_tok, 256, 16)

    def dh_norm(da_ref, db_ref, wg_ref, wu_ref, dy_ref, x_ref, g_ref, dx_ref, dx16_ref, dg_ref):
        dh = _dot(da_ref[0], wg_ref[0, 0]) + _dot(db_ref[0], wu_ref[0, 0])
        for j in range(1, N_CHIPS):
            dh = dh + _dot(da_ref[j], wg_ref[j, 0]) + _dot(db_ref[j], wu_ref[j, 0])
        dx, dg = _rms_bwd(x_ref[...], g_ref[...], dh)
        dx = dy_ref[...] + dx
        dx_ref[...] = dx
        dx16_ref[...] = dx.astype(BF16)

        @pl.when(pl.program_id(0) == 0)
        def _():
            dg_ref[...] = jnp.zeros(dg_ref.shape, F32)

        dg_ref[...] += dg

    tspec = pl.BlockSpec((N_CHIPS, tr, fq), lambda i: (0, i, 0))
    wspec = pl.BlockSpec((N_CHIPS, 1, fq, D_MODEL), lambda i: (0, d, 0, 0))
    rspec = pl.BlockSpec((tr, D_MODEL), lambda i: (i, 0))
    vspec = pl.BlockSpec((1, D_MODEL), lambda i: (0, 0))
    dx, dx16, dg = _pcall(
        dh_norm, name="ffn_dh_norm", grid=(n_tok // tr,),
        in_specs=[tspec, tspec, wspec, wspec, rspec, rspec, vspec], out_specs=[rspec, rspec, vspec],
        out_shape=[jax.ShapeDtypeStruct((n_tok, D_MODEL), F32), jax.ShapeDtypeStruct((n_tok, D_MODEL), BF16),
                   jax.ShapeDtypeStruct((1, D_MODEL), F32)],
        compiler_params=_cparams(("arbitrary",), 4 * N_CHIPS * D_MODEL * fq * 2 + 4 * N_CHIPS * tr * fq * 2
                                 + 16 * tr * D_MODEL * 4),
    )(da, db, wg, wu, dy, x, g)
    return dx, dx16, dg[0], (dwg, dwu, dwd)


def _alpha_pads(w_alpha):
    z = jnp.zeros((LANE - 2 * GLA_R, GLA_W), F32)
    wf = jnp.concatenate([w_alpha[0], jnp.zeros((GLA_R, GLA_W), F32), z], axis=0)
    wb = jnp.concatenate([jnp.zeros((GLA_R, GLA_W), F32), w_alpha[1], z], axis=0)
    return wf, wb


def _dup_heads(t):
    low = lax.broadcasted_iota(jnp.int32, t.shape, 1) < AT_D
    h0, h1 = jnp.where(low, t, 0.0), jnp.where(low, 0.0, t)
    return jnp.concatenate([h0 + pltpu.roll(h0, AT_D, 1), h1 + pltpu.roll(h1, AT_D, 1)], axis=1)


def _fold_heads(t):
    low = lax.broadcasted_iota(jnp.int32, (t.shape[0], LANE), 1) < AT_D
    g0, g1 = t[:, :LANE], t[:, LANE:]
    return jnp.where(low, g0 + pltpu.roll(g0, AT_D, 1), g1 + pltpu.roll(g1, AT_D, 1))


def _mixer_fwd(x, w):
    n_tok = x.shape[0]
    wip, wmg = w['w_in_p'], w['w_merge_gate']
    tr = _tile(n_tok, 256, 16)

    def project(x_ref, g_ref, wip_ref, wmg_ref, h_ref, z_ref, gp_ref):
        hv = _rms(x_ref[...], g_ref[...]).astype(BF16)
        h_ref[...] = hv
        z_ref[...] = _dot(hv, wip_ref[...], _NT).astype(BF16)
        gp_ref[...] = _dot(hv, wmg_ref[...]).astype(BF16)

    def rows(c):
        return pl.BlockSpec((tr, c), lambda i: (i, 0))

    def whole(t):
        return pl.BlockSpec(t.shape, lambda i: (0, 0))

    h, z, gp = _pcall(
        project, name="mix_in", grid=(n_tok // tr,),
        in_specs=[rows(D_MODEL), whole(w['mix_norm']), whole(wip), whole(wmg)],
        out_specs=[rows(D_MODEL), rows(ZP_W), rows(3 * D_MODEL)],
        out_shape=[jax.ShapeDtypeStruct((n_tok, D_MODEL), BF16), jax.ShapeDtypeStruct((n_tok, ZP_W), BF16),
                   jax.ShapeDtypeStruct((n_tok, 3 * D_MODEL), BF16)],
        compiler_params=_cparams(("parallel",), 4 * (wip.size + wmg.size) + 16 * tr * (ZP_W + 3 * D_MODEL)),
    )(x, w['mix_norm'], wip, wmg)
    u = _cw(z, 0, S5_W)

    xs, yy = [], None
    for d, rev in ((0, False), (1, True)):
        a_re, a_im, wb, wc = w['s5'][d]
        x_d, yy = _s5_fwd("s5_rev" if rev else "s5_fwd", z, a_re, a_im, wb, wc, yy, rev=rev)
        xs.append(x_d)

    def s5_post(yv, uv, dv, wglu):
        y0 = yv + dv * uv
        y1 = _gelu(y0)
        t = _dot(y1.astype(BF16), wglu)
        return y0, t, y1 * _sigmoid(t)

    y0, t_glu, y_s5 = _rowwise("s5_post", s5_post, [yy, u], [w['s5_d'], w['s5_w_glu']],
                               [(S5_W, F32), (S5_W, F32), (S5_W, BF16)])

    wa_f, wa_b = _alpha_pads(w['gla_w_alpha'])
    ba_f, ba_b = w['gla_b_alpha'][0:1], w['gla_b_alpha'][1:2]
    zz = _cw(z, ZC_Z, LANE)

    def gla_prep(zv, waf, wab, bf, bb):
        z16 = zv.astype(BF16)
        lf = _dot(z16, waf.astype(BF16)) + bf
        lb = _dot(z16, wab.astype(BF16)) + bb
        return _log_sigmoid(lf) / GLA_TAU, _log_sigmoid(lb) / GLA_TAU

    la_f, la_b = _rowwise("gla_prep", gla_prep, [zz], [wa_f, wa_b, ba_f, ba_b], [(GLA_W, F32), (GLA_W, F32)])
    o_f, s_f = _gla_fwd("gla_fwd", z, la_f, rev=False)
    o_b, s_b = _gla_fwd("gla_rev", z, la_b, rev=True)
    gate = _cw(z, 4, GLA_W)

    def gla_post(of, ob, gv, gn):
        o, gv = of + ob, gv.astype(F32)
        outs = []
        for hh in range(GLA_H):
            sl = slice(hh * GLA_D, (hh + 1) * GLA_D)
            gh = gv[:, sl]
            outs.append(_rms(o[:, sl], gn) * (gh * _sigmoid(gh)))
        return jnp.concatenate(outs, axis=1)

    y_gla = _rowwise("gla_post", gla_post, [o_f, o_b, gate], [w['gla_norm']], [(GLA_W, BF16)])[0]

    cq, sq, mq, rq = _head_consts(n_tok, AT_QH)
    ck, sk, mk, rk = _head_consts(n_tok, AT_KH)
    gq = jnp.tile(w['attn_q_norm'], (1, AT_QH))
    gk = jnp.tile(w['attn_k_norm'], (1, AT_KH))

    def rope_fwd(v, gain, cos, sin, mean, rot, scale):
        r = lax.rsqrt(_dot_exact(v * v, mean, terms=2) + EPS)
        vn = v * r * gain
        return (vn * cos + _dot_exact(vn, rot, terms=2) * sin) * scale

    def attn_prep(qv, kv, vv, cqv, sqv, ckv, skv, gqv, gkv, mqv, rqv, mkv, rkv):
        qv, kv, vv = qv.astype(F32), kv.astype(F32), vv.astype(F32)
        kr = rope_fwd(kv, gkv, ckv, skv, mkv, rkv, 1.0)
        return rope_fwd(qv, gqv, cqv, sqv, mqv, rqv, AT_D ** -0.5), _dup_heads(kr), _dup_heads(vv)

    q_raw, k_raw, v_raw = _cw(z, 5, AT_W), _cw(z, ZC_AK, AT_KW), _cw(z, ZC_AV, AT_KW)
    q16, kd, vd = _rowwise("attn_prep", attn_prep, [q_raw, k_raw, v_raw, cq, sq, ck, sk], [gq, gk, mq, rq, mk, rk],
                           [(AT_W, BF16), (2 * LANE, BF16), (2 * LANE, BF16)])
    y_at, lse = _attn_fwd("attn_fwd", q16, kd, vd)

    p0 = _mm("mix_branch", y_s5, w['w_branch_s5'], out_dtype=BF16)
    p1 = _mm("mix_branch", y_gla, w['w_branch_gla'], out_dtype=BF16)
    p2 = _mm("mix_branch_f32", y_at, w['w_branch_attn'], out_dtype=BF16)

    def merge(gpv, a0, a1, a2, bias):
        g = _sigmoid(gpv.astype(F32) + bias)
        return (g[:, :D_MODEL] * a0.astype(F32) + g[:, D_MODEL:2 * D_MODEL] * a1.astype(F32)
                + g[:, 2 * D_MODEL:] * a2.astype(F32))

    merged = _rowwise("mix_merge", merge, [gp, p0, p1, p2], [w['b_merge_gate']], [(D_MODEL, BF16)])[0]
    xo = _mm("mix_out", merged, w['w_out'], add=x)
    saved = dict(x=x, h=h, z=z, gp=gp, xs=xs, y0=y0, t_glu=t_glu, y_s5=y_s5, la=(la_f, la_b), o=(o_f, o_b), s=(s_f, s_b),
                 y_gla=y_gla, q16=q16, kd=kd, vd=vd, lse=lse, y_at=y_at, p=(p0, p1, p2), merged=merged)
    return xo, saved


def _mixer_bwd(dy, dy16, sv, w, s5_raw):
    z, h = sv['z'], sv['h']
    n_tok = z.shape[0]
    grads = {}
    dmerged = _mm("mix_dmerged", dy16, w['w_out'], tb=True, out_dtype=BF16)
    grads['w_out'] = _mm("mix_dwout", sv['merged'], dy16, ta=True, out_dtype=GRAD_DT)

    def dmerge(dm, gpv, a0, a1, a2, bias):
        dm = dm.astype(F32)
        g = _sigmoid(gpv.astype(F32) + bias)
        ps = (a0.astype(F32), a1.astype(F32), a2.astype(F32))
        dps, dgs = [], []
        for k in range(3):
            gk = g[:, k * D_MODEL:(k + 1) * D_MODEL]
            dps.append(dm * gk)
            dgs.append(dm * ps[k] * gk * (1.0 - gk))
        dgp = jnp.concatenate(dgs, axis=1)
        return dps[0], dps[1], dps[2], dgp, jnp.sum(dgp, axis=0, keepdims=True)

    dp0, dp1, dp2, dgp, dbm = _rowwise("mix_dmerge", dmerge, [dmerged, sv['gp'], *sv['p']], [w['b_merge_gate']],
                                       [(D_MODEL, BF16)] * 3 + [(3 * D_MODEL, BF16)], accs=[(1, 3 * D_MODEL)])
    grads['b_merge_gate'] = dbm[0]
    grads['w_branch_s5'] = _mm("mix_dbranch", sv['y_s5'], dp0, ta=True, out_dtype=GRAD_DT)
    grads['w_branch_gla'] = _mm("mix_dbranch", sv['y_gla'], dp1, ta=True, out_dtype=GRAD_DT)
    grads['w_branch_attn'] = _mm("mix_dbranch_f32", sv['y_at'], dp2, ta=True, out_dtype=GRAD_DT)
    dy_s5 = _mm("mix_dy", dp0, w['w_branch_s5'], tb=True)
    dy_gla = _mm("mix_dy", dp1, w['w_branch_gla'], tb=True)
    dy_at = _mm("mix_dy", dp2, w['w_branch_attn'], tb=True)
    grads['w_merge_gate'] = _mm("mix_dwgate", h, dgp, ta=True, out_dtype=GRAD_DT)
    u = _cw(z, 0, S5_W)

    def s5_post_bwd(dyv, y0, t, uv, dv, wglu):
        sg = _sigmoid(t)
        y1 = _gelu(y0)
        dt = dyv * y1 * sg * (1.0 - sg)
        dy1 = dyv * sg + _dot(dt.astype(BF16), wglu, _NT)
        dy0 = dy1 * _gelu_grad(y0)
        return dy0, dt, y1, dy0 * dv, jnp.sum(dy0 * uv, axis=0, keepdims=True)

    dy0, dt_glu, y1, du, dd = _rowwise("s5_post_bwd", s5_post_bwd, [dy_s5, sv['y0'], sv['t_glu'], u],
                                       [w['s5_d'], w['s5_w_glu']],
                                       [(S5_W, BF16), (S5_W, BF16), (S5_W, BF16), (S5_W, F32)], accs=[(1, S5_W)])
    grads['s5_d'] = dd[0]
    grads['s5_w_glu'] = _mm("s5_dwglu", y1, dt_glu, ta=True, out_dtype=GRAD_DT)
    s5_cot = []
    for d, rev in ((0, False), (1, True)):
        a_re, a_im, wb, wc = w['s5'][d]
        du, dwb, dwc, da = _s5_bwd("s5_adj_fwd" if rev else "s5_adj_rev", dy0, z, sv['xs'][d], a_re, -a_im, wb, wc, du,
                                   rev=not rev)
        da = jnp.sum(da, axis=0, keepdims=True)
        s5_cot.append((da[:, :S5_N], da[:, S5_N:], dwb, dwc))
    for d in range(2):
        _, pull = jax.vjp(_s5_prep, *s5_raw[d])
        s5_cot[d] = pull(s5_cot[d])
    for k, nm in enumerate(('s5_lambda_re', 's5_lambda_im', 's5_log_dt', 's5_b_re', 's5_b_im', 's5_c_re', 's5_c_im')):
        grads[nm] = jnp.stack([s5_cot[0][k], s5_cot[1][k]])

    gate = _cw(z, 4, GLA_W)

    def gla_post_bwd(dyv, of, ob, gv, gn):
        o, gv = of + ob, gv.astype(F32)
        dos, dgates = [], []
        dgn = jnp.zeros((1, GLA_D), F32)
        for hh in range(GLA_H):
            sl = slice(hh * GLA_D, (hh + 1) * GLA_D)
            gh, dyh = gv[:, sl], dyv[:, sl]
            sg = _sigmoid(gh)
            dgates.append(dyh * _rms(o[:, sl], gn) * sg * (1.0 + gh * (1.0 - sg)))
            dx, dg = _rms_bwd(o[:, sl], gn, dyh * gh * sg)
            dos.append(dx)
            dgn = dgn + dg
        return jnp.concatenate(dos, axis=1), jnp.concatenate(dgates, axis=1), dgn

    do, dgate, dgn = _rowwise("gla_post_bwd", gla_post_bwd, [dy_gla, *sv['o'], gate], [w['gla_norm']],
                              [(GLA_W, F32), (GLA_W, F32)], accs=[(1, GLA_D)])
    grads['gla_norm'] = dgn[0]
    dq_f, dk_f, dv_f, dla_f = _gla_bwd("gla_bwd_fwd", z, sv['la'][0], sv['s'][0], do, rev=False)
    dq_b, dk_b, dv_b, dla_b = _gla_bwd("gla_bwd_rev", z, sv['la'][1], sv['s'][1], do, rev=True)
    wa_f, wa_b = _alpha_pads(w['gla_w_alpha'])
    ba_f, ba_b = w['gla_b_alpha'][0:1], w['gla_b_alpha'][1:2]
    zz = _cw(z, ZC_Z, LANE)

    def gla_prep_bwd(dlf, dlb, zv, waf, wab, bf, bb):
        z16 = zv.astype(BF16)
        waf16, wab16 = waf.astype(BF16), wab.astype(BF16)
        xf = _dot(z16, waf16) + bf
        xb = _dot(z16, wab16) + bb
        df = dlf * (_sigmoid(-xf) / GLA_TAU)
        db = dlb * (_sigmoid(-xb) / GLA_TAU)
        dz = _dot(df.astype(BF16), waf16, _NT) + _dot(db.astype(BF16), wab16, _NT)
        return df, db, dz, jnp.sum(df, axis=0, keepdims=True), jnp.sum(db, axis=0, keepdims=True)

    dlog_f, dlog_b, dzz, dba_f, dba_b = _rowwise(
        "gla_prep_bwd", gla_prep_bwd, [dla_f, dla_b, zz], [wa_f, wa_b, ba_f, ba_b],
        [(GLA_W, BF16), (GLA_W, BF16), (LANE, F32)], accs=[(1, GLA_W), (1, GLA_W)])
    dwa_f = _mm("gla_dwa", zz, dlog_f, ta=True)
    dwa_b = _mm("gla_dwa", zz, dlog_b, ta=True)
    grads['gla_w_alpha'] = jnp.stack([dwa_f[:GLA_R], dwa_b[GLA_R:2 * GLA_R]])
    grads['gla_b_alpha'] = jnp.concatenate([dba_f, dba_b], axis=0)

    dq_r, dkd, dvd = _attn_bwd("attn_bwd", sv['q16'], sv['kd'], sv['vd'], sv['y_at'], sv['lse'], dy_at)
    cq, sq, mq, rq = _head_consts(n_tok, AT_QH)
    ck, sk, mk, rk = _head_consts(n_tok, AT_KH)
    gq = jnp.tile(w['attn_q_norm'], (1, AT_QH))
    gk = jnp.tile(w['attn_k_norm'], (1, AT_KH))

    def rope_bwd(dr, v, gain, cos, sin, mean, rot, scale):
        dr = dr * scale
        dvn = dr * cos + _dot_exact(dr * sin, rot, _NT, terms=2)
        r = lax.rsqrt(_dot_exact(v * v, mean, terms=2) + EPS)
        vh = v * r
        dg = jnp.sum(dvn * vh, axis=0, keepdims=True)
        dvh = dvn * gain
        return r * (dvh - vh * _dot_exact(dvh * vh, mean, terms=2)), dg

    def attn_prep_bwd(dq, dk2, dv2, qv, kv, cqv, sqv, ckv, skv, gqv, gkv, mqv, rqv, mkv, rkv):
        qv, kv = qv.astype(F32), kv.astype(F32)
        dqo, dgq = rope_bwd(dq, qv, gqv, cqv, sqv, mqv, rqv, AT_D ** -0.5)
        dko, dgk = rope_bwd(_fold_heads(dk2), kv, gkv, ckv, skv, mkv, rkv, 1.0)
        return dqo, dko, _fold_heads(dv2), dgq, dgk

    q_raw, k_raw = _cw(z, 5, AT_W), _cw(z, ZC_AK, AT_KW)
    daq, dak, dav, dgq, dgk = _rowwise("attn_prep_bwd", attn_prep_bwd, [dq_r, dkd, dvd, q_raw, k_raw, cq, sq, ck, sk],
                                       [gq, gk, mq, rq, mk, rk], [(AT_W, F32), (AT_KW, F32), (AT_KW, F32)],
                                       accs=[(1, AT_W), (1, AT_KW)])
    grads['attn_q_norm'] = dgq.reshape(AT_QH, AT_D).sum(axis=0)
    grads['attn_k_norm'] = dgk.reshape(AT_KH, AT_D).sum(axis=0)

    def assemble(duv, qf, qb, kf, kb, vf, vb, dg, aq, ak, av, zzv):
        return jnp.concatenate([duv, qf + qb, kf + kb, vf + vb, dg, aq, ak, av, zzv], axis=1)

    dz = _rowwise("mix_dz", assemble, [du, dq_f, dq_b, dk_f, dk_b, dv_f, dv_b, dgate, daq, dak, dav, dzz], [],
                  [(ZP_W, BF16)])[0]
    grads['w_in'] = _unpad_w_in(_mm("mix_dwin", dz, h, ta=True, out_dtype=GRAD_DT))
    wmg, wip = w['w_merge_gate'], w['w_in_p']
    tr = _tile(n_tok, 256, 16)

    def dh_norm(dgp_ref, dz_ref, wmg_ref, wip_ref, dy_ref, x_ref, g_ref, dx_ref, dx16_ref, dg_ref):
        dh = _dot(dgp_ref[...], wmg_ref[...], _NT) + _dot(dz_ref[...], wip_ref[...])
        dxv, dgv = _rms_bwd(x_ref[...], g_ref[...], dh)
        dxv = dy_ref[...] + dxv
        dx_ref[...] = dxv
        dx16_ref[...] = dxv.astype(BF16)

        @pl.when(pl.program_id(0) == 0)
        def _():
            dg_ref[...] = jnp.zeros(dg_ref.shape, F32)

        dg_ref[...] += dgv

    def rows(c):
        return pl.BlockSpec((tr, c), lambda i: (i, 0))

    def whole(t):
        return pl.BlockSpec(t.shape, lambda i: (0, 0))

    dx, dx16, dg = _pcall(
        dh_norm, name="mix_dh_norm", grid=(n_tok // tr,),
        in_specs=[rows(dgp.shape[1]), rows(dz.shape[1]), whole(wmg), whole(wip), rows(D_MODEL), rows(D_MODEL),
                  whole(w['mix_norm'])],
        out_specs=[rows(D_MODEL), rows(D_MODEL), whole(w['mix_norm'])],
        out_shape=[jax.ShapeDtypeStruct((n_tok, D_MODEL), F32), jax.ShapeDtypeStruct((n_tok, D_MODEL), BF16),
                   jax.ShapeDtypeStruct((1, D_MODEL), F32)],
        compiler_params=_cparams(("arbitrary",), 4 * (wmg.size + wip.size) + 8 * tr * (dgp.shape[1] + dz.shape[1])
                                 + 16 * tr * D_MODEL * 4),
    )(dgp, dz, wmg, wip, dy, sv['x'], w['mix_norm'])
    grads['mix_norm'] = dg[0]
    return dx, dx16, grads


FFN_NAMES = [p + k for p in ('ffn1_', 'ffn2_') for k in ('w_gate', 'w_up', 'w_down')]
MISC_SHARDED = [n for n in SHARDED if n not in FFN_NAMES]
LARGE_SHARDED = FFN_NAMES + ['w_in', 'w_merge_gate', 'w_out', 'w_branch_s5', 'w_branch_gla', 'w_branch_attn', 's5_w_glu']
KEPT_TRANSPOSED = [p + k for p in ('ffn1_', 'ffn2_') for k in ('w_gate', 'w_up')] + ['w_in']


def _stored_axis(n):
    return 1 if n in KEPT_TRANSPOSED else SHARD_AXIS[n]


def _stored(n, t):
    return jnp.swapaxes(t, 1, 2) if n in KEPT_TRANSPOSED else t


def _layer_weights(ws, rep, i):
    w = {k: _from_shards(ws[k][:, i], _stored_axis(k) - 1) for k in MISC_SHARDED}
    for k in ('ffn1_norm', 'mix_norm', 'ffn2_norm', 's5_d', 'gla_norm', 'attn_q_norm', 'attn_k_norm', 'b_merge_gate'):
        w[k] = rep[k][i].reshape(1, -1)
    w['w_in_p'] = _pad_w_in(w['w_in'])
    s5_raw = [tuple(rep[k][i, d] for k in ('s5_lambda_re', 's5_lambda_im', 's5_log_dt', 's5_b_re', 's5_b_im',
                                            's5_c_re', 's5_c_im')) for d in range(2)]
    w['s5'] = [_s5_prep(*s5_raw[d]) for d in range(2)]
    return w, s5_raw


def _device_step(x, target, ws, rep):
    lw = [_layer_weights(ws, rep, i) for i in range(DEPTH)]
    ffn = {p: (ws[p + 'w_gate'], ws[p + 'w_up'], ws[p + 'w_down']) for p in ('ffn1_', 'ffn2_')}
    saved = []
    for i in range(DEPTH):
        w, _ = lw[i]
        x, s1 = _ffn_fwd(x, w['ffn1_norm'], *ffn['ffn1_'], i)
        x, s2 = _mixer_fwd(x, w)
        x, s3 = _ffn_fwd(x, w['ffn2_norm'], *ffn['ffn2_'], i)
        saved.append((s1, s2, s3))
    gfin = rep['final_norm'].reshape(1, -1)

    def head(xv, tv, gv):
        e = _rms(xv, gv) - tv
        dx, dg = _rms_bwd(xv, gv, e * (1.0 / D_MODEL))
        part = jnp.sum(e * e, axis=0, keepdims=True)
        return dx, dx, dg, part

    dx, dx16, dgfin, part = _rowwise("loss_head", head, [x, target], [gfin], [(D_MODEL, F32), (D_MODEL, BF16)],
                                     accs=[(1, D_MODEL), (1, D_MODEL)])
    loss = (0.5 / D_MODEL) * jnp.sum(part)
    per_layer = []
    dffn = {'ffn1_': None, 'ffn2_': None}
    for i in reversed(range(DEPTH)):
        w, s5_raw = lw[i]
        s1, s2, s3 = saved[i]
        dx, dx16, n3, dffn['ffn2_'] = _ffn_bwd(dx, dx16, s3, w['ffn2_norm'], *ffn['ffn2_'], i, dffn['ffn2_'])
        dx, dx16, g = _mixer_bwd(dx, dx16, s2, w, s5_raw)
        dx, dx16, n1, dffn['ffn1_'] = _ffn_bwd(dx, dx16, s1, w['ffn1_norm'], *ffn['ffn1_'], i, dffn['ffn1_'])
        g['ffn2_norm'], g['ffn1_norm'] = n3, n1
        per_layer.append(g)
    per_layer.reverse()
    stacked = {k: jnp.stack([per_layer[i][k] for i in range(DEPTH)]) for k in per_layer[0]}
    gs = {k: _to_shards(stacked[k], _stored_axis(k)) for k in MISC_SHARDED}
    for p in ('ffn1_', 'ffn2_'):
        gs[p + 'w_gate'], gs[p + 'w_up'], gs[p + 'w_down'] = dffn[p]
    grep = {k: stacked[k] for k in REPLICATED if k != 'final_norm'}
    grep['final_norm'] = dgfin[0]
    return loss, dx, gs, grep


PACK_C = 1024
ROW_TILE = 512


def _rows_of(shape):
    return -(-int(np.prod(shape)) // PACK_C)


def _as_rows(t, lead=0):
    head = t.shape[:lead]
    flat = t.reshape(head + (-1,))
    rows = -(-flat.shape[-1] // PACK_C)
    pad = rows * PACK_C - flat.shape[-1]
    if pad:
        flat = jnp.pad(flat, [(0, 0)] * lead + [(0, pad)])
    return flat.reshape(head + (rows, PACK_C))


def _pack(items, total_rows, lead=0):
    parts = [_as_rows(t, lead) for t in items]
    used = sum(p.shape[lead] for p in parts)
    if total_rows > used:
        parts.append(jnp.zeros(parts[0].shape[:lead] + (total_rows - used, PACK_C), parts[0].dtype))
    return jnp.concatenate(parts, axis=lead)


def _unpack(rows, shapes, lead=0):
    out, r0 = [], 0
    head = rows.shape[:lead]
    for shp in shapes:
        n, size = _rows_of(shp), int(np.prod(shp))
        piece = lax.slice_in_dim(rows, r0, r0 + n, axis=lead).reshape(head + (n * PACK_C,))
        out.append(lax.slice_in_dim(piece, 0, size, axis=lead).reshape(head + tuple(shp)))
        r0 += n
    return out


def _round_up(n, m):
    return -(-n // m) * m


def _to_shards(g, axis):
    shp = g.shape
    g = g.reshape(shp[:axis] + (N_CHIPS, shp[axis] // N_CHIPS) + shp[axis + 1:])
    return jnp.moveaxis(g, axis, 0)


def _from_shards(s, axis):
    s = jnp.moveaxis(s, 0, axis)
    shp = s.shape
    return s.reshape(shp[:axis] + (shp[axis] * shp[axis + 1],) + shp[axis + 2:])


_HBM = pl.BlockSpec(memory_space=pltpu.HBM)


def _place():
    x, y, c = lax.axis_index("x"), lax.axis_index("y"), lax.axis_index("c")
    chips = [(1 - x, y), (x, 1 - y), (1 - x, 1 - y)]
    return x, y, c, chips


def _rcopy(src, dst, send, recv, k, dev):
    return pltpu.make_async_remote_copy(src_ref=src, dst_ref=dst, send_sem=send.at[k], recv_sem=recv.at[k],
                                        device_id=dev, device_id_type=MESH)


def _comm_call(name, body, arrays, out_shapes, n_sems, n_local):
    return pl.pallas_call(
        body, name=name, in_specs=[_HBM] * len(arrays), out_specs=[_HBM] * len(out_shapes), out_shape=out_shapes,
        scratch_shapes=[pltpu.SemaphoreType.DMA((n_sems,)), pltpu.SemaphoreType.DMA((n_sems,)),
                        pltpu.SemaphoreType.DMA((max(n_local, 1),))],
    )(*arrays)


def _ici_peers(x, y, c):
    x_first = c == 0
    first = (jnp.where(x_first, 1 - x, x), jnp.where(x_first, y, 1 - y))
    second = (jnp.where(x_first, x, 1 - x), jnp.where(x_first, 1 - y, y))
    return first, second


def _ag_chips(name, packs):
    n = len(packs)

    def body(*refs):
        p, o, (send, recv, _) = refs[:n], refs[n:2 * n], refs[2 * n:]
        x, y, c, _ = _place()
        (ax, ay), (bx, by) = _ici_peers(x, y, c)
        me, na, nb, nd = 2 * x + y, 2 * ax + ay, 2 * bx + by, 2 * (1 - x) + (1 - y)
        sib = (x, y, 1 - c)
        own = [_rcopy(p[a].at[c], o[a].at[me, c], send, recv, 6 * a, (ax, ay, c)) for a in range(n)]
        own += [_rcopy(p[a].at[c], o[a].at[me, c], send, recv, 6 * a + 1, (bx, by, c)) for a in range(n)]
        for cp in own:
            cp.start()
        passed = []

        def landed(a, chip, k, fwd):
            blk = o[a].at[chip, c]
            _rcopy(blk, blk, send, recv, 6 * a + k, sib).wait_recv()
            for sem, dev in fwd:
                cp = _rcopy(blk, blk, send, recv, 6 * a + sem, dev)
                cp.start()
                passed.append(cp)

        for a in range(n):
            landed(a, na, 0, [(2, (bx, by, c)), (3, sib)])
        for a in range(n):
            landed(a, nb, 1, [(4, sib)])
        for a in range(n):
            landed(a, nd, 2, [(5, sib)])
        for a in range(n):
            for k, chip in ((3, nb), (4, na), (5, nd)):
                blk = o[a].at[chip, 1 - c]
                _rcopy(blk, blk, send, recv, 6 * a + k, sib).wait_recv()
        for cp in own + passed:
            cp.wait_send()

    outs = _comm_call(name, body, packs, [jax.ShapeDtypeStruct((N_CHIPS,) + t.shape, t.dtype) for t in packs], 6 * n, 0)
    chip = 2 * lax.axis_index("x") + lax.axis_index("y")
    return [lax.dynamic_update_slice_in_dim(o, t[None], chip, axis=0) for o, t in zip(outs, packs)]


def _pair_swap(name, gs):
    n = len(gs)
    ns = gs[0].shape[0]

    def body(*refs):
        g, r, (send, recv, _) = refs[:n], refs[n:2 * n], refs[2 * n:]
        x, y, c, _ = _place()
        cps = [_rcopy(g[a].at[j, 1 - c], r[a].at[j], send, recv, ns * a + j, (x, y, 1 - c))
               for a in range(n) for j in range(ns)]
        for cp in cps:
            cp.start()
        for cp in cps:
            cp.wait_recv()
        for cp in cps:
            cp.wait_send()

    return _comm_call(name, body, gs, [jax.ShapeDtypeStruct((ns,) + t.shape[2:], t.dtype) for t in gs], ns * n, 0)


def _halve_first(name, parts):
    n = len(parts)

    def body(*refs):
        p, r, (send, recv, _) = refs[:n], refs[n:2 * n], refs[2 * n:]
        x, y, c, _ = _place()
        (ax, ay), _ = _ici_peers(x, y, c)
        cps = [_rcopy(p[a].at[jnp.where(c == 0, 2 * ax + k, 2 * k + ay)], r[a].at[k], send, recv, 2 * a + k, (ax, ay, c))
               for a in range(n) for k in range(2)]
        for cp in cps:
            cp.start()
        for cp in cps:
            cp.wait_recv()
        for cp in cps:
            cp.wait_send()

    return _comm_call(name, body, parts, [jax.ShapeDtypeStruct((2,) + t.shape[1:], t.dtype) for t in parts], 2 * n, 0)


def _halve_second(name, kept):
    n = len(kept)

    def body(*refs):
        p, r, (send, recv, _) = refs[:n], refs[n:2 * n], refs[2 * n:]
        x, y, c, _ = _place()
        _, (bx, by) = _ici_peers(x, y, c)
        cps = [_rcopy(p[a].at[jnp.where(c == 0, by, bx)], r[a], send, recv, a, (bx, by, c)) for a in range(n)]
        for cp in cps:
            cp.start()
        for cp in cps:
            cp.wait_recv()
        for cp in cps:
            cp.wait_send()

    return _comm_call(name, body, kept, [jax.ShapeDtypeStruct(t.shape[1:], t.dtype) for t in kept], n, 0)


def _add_selected(name, t, r, index, out_dtype):
    single = r.ndim == 2
    r3 = r[None] if single else r
    nk, rh, cc = r3.shape
    tl = _tile(rh, ROW_TILE, 16)

    def body(i_ref, t_ref, r_ref, o_ref):
        o_ref[...] = (t_ref[...].astype(F32) + r_ref[...].astype(F32)).astype(o_ref.dtype)

    out = _pcall(
        body, name=name,
        grid_spec=pltpu.PrefetchScalarGridSpec(
            num_scalar_prefetch=1, grid=(nk, rh // tl),
            in_specs=[pl.BlockSpec((1, tl, cc), lambda k, i, i_ref: (i_ref[k], i, 0)),
                      pl.BlockSpec((1, tl, cc), lambda k, i, i_ref: (k, i, 0))],
            out_specs=pl.BlockSpec((1, tl, cc), lambda k, i, i_ref: (k, i, 0))),
        out_shape=jax.ShapeDtypeStruct(r3.shape, out_dtype),
        compiler_params=_cparams(("parallel", "parallel"), 8 * tl * cc * 4),
    )(index, t, r3)
    return out[0] if single else out


def _pair_share(name, reds):
    n = len(reds)

    def body(*refs):
        r, o, (send, recv, _) = refs[:n], refs[n:2 * n], refs[2 * n:]
        x, y, c, _ = _place()
        cps = [_rcopy(r[a], o[a].at[c], send, recv, a, (x, y, 1 - c)) for a in range(n)]
        for cp in cps:
            cp.start()
        for a in range(n):
            blk = o[a].at[1 - c]
            _rcopy(blk, blk, send, recv, a, (x, y, 1 - c)).wait_recv()
        for cp in cps:
            cp.wait_send()

    outs = _comm_call(name, body, reds, [jax.ShapeDtypeStruct((2,) + t.shape, t.dtype) for t in reds], n, 0)
    core = lax.axis_index("c")
    return [lax.dynamic_update_slice_in_dim(o, t[None], core, axis=0) for o, t in zip(outs, reds)]


def _add_own_half(name, g, r):
    ns, _, rh, cc = g.shape
    tl = _tile(rh, ROW_TILE, 16)
    core = lax.axis_index("c").astype(jnp.int32).reshape(1)

    def body(c_ref, g_ref, r_ref, o_ref):
        o_ref[...] = (g_ref[0].astype(F32) + r_ref[...].astype(F32)).astype(o_ref.dtype)

    return _pcall(
        body, name=name,
        grid_spec=pltpu.PrefetchScalarGridSpec(
            num_scalar_prefetch=1, grid=(ns, rh // tl),
            in_specs=[pl.BlockSpec((1, 1, tl, cc), lambda j, i, c_ref: (j, c_ref[0], i, 0)),
                      pl.BlockSpec((1, tl, cc), lambda j, i, c_ref: (j, i, 0))],
            out_specs=pl.BlockSpec((1, tl, cc), lambda j, i, c_ref: (j, i, 0))),
        out_shape=jax.ShapeDtypeStruct(r.shape, BF16),
        compiler_params=_cparams(("parallel", "parallel"), 8 * tl * cc * 4),
    )(core, g, r)


def _reduce_scatter(gs):
    x, y, c = lax.axis_index("x"), lax.axis_index("y"), lax.axis_index("c")
    mine_first = jnp.stack([jnp.where(c == 0, 2 * x + k, 2 * k + y) for k in range(2)]).astype(jnp.int32)
    mine_second = jnp.where(c == 0, y, x).astype(jnp.int32).reshape(1)
    swapped = _pair_swap("rs_pair_swap", gs)
    parts = [_add_own_half("rs_add_pair", g, r) for g, r in zip(gs, swapped)]
    got = _halve_first("rs_halve_first", parts)
    kept = [_add_selected("rs_add_first", t, r, mine_first, BF16) for t, r in zip(parts, got)]
    got = _halve_second("rs_halve_second", kept)
    reds = [_add_selected("rs_add_second", t, r, mine_second, F32) for t, r in zip(kept, got)]
    return _pair_share("rs_pair_share", reds)


def _adamw(name, g, w, m, v):
    c1 = 1.0 - ADAM_B1 ** ADAM_STEP
    c2 = 1.0 - ADAM_B2 ** ADAM_STEP
    nb, rows, cc = g.shape
    tl = _tile(rows, ROW_TILE, SUBLANE)

    def body(g_ref, w_ref, m_ref, v_ref, d_ref, mo_ref, vo_ref):
        gv = g_ref[...]
        mn = ADAM_B1 * m_ref[...] + (1.0 - ADAM_B1) * gv
        vn = ADAM_B2 * v_ref[...] + (1.0 - ADAM_B2) * (gv * gv)
        d_ref[...] = -ADAM_LR * ((mn / c1) / (jnp.sqrt(vn / c2) + ADAM_EPS) + ADAM_WD * w_ref[...])
        mo_ref[...] = mn
        vo_ref[...] = vn

    spec = pl.BlockSpec((1, tl, cc), lambda b, i: (b, i, 0))
    return _pcall(
        body, name=name, grid=(nb, rows // tl), in_specs=[spec] * 4, out_specs=[spec] * 3,
        out_shape=[jax.ShapeDtypeStruct(g.shape, F32)] * 3,
        compiler_params=_cparams(("parallel", "parallel"), 16 * tl * cc * 4),
    )(g, w, m, v)


def _train_step(a):
    x, target = a['x'][0], a['loss_target'][0]

    names = BF16_GATHER + F32_GATHER
    got = _ag_chips("ag_weights", [_stored(n, a[n]).astype(BF16) for n in BF16_GATHER] + [a[n] for n in F32_GATHER])
    ws = dict(zip(names, got))

    loss, gx, gs, grep = _device_step(x, target, ws, {n: a[n] for n in REPLICATED})
    loss = lax.psum(loss, ("x", "y", "c"))

    small = [n for n in MISC_SHARDED if n not in LARGE_SHARDED]
    small_shapes = [a[n].shape[1:] for n in small]
    rep_shapes = [a[n].shape for n in REPLICATED]
    small_rows = sum(_rows_of(s) for s in small_shapes)
    piece = _round_up(-(-sum(_rows_of(s) for s in rep_shapes) // (2 * N_CHIPS)), SUBLANE)
    rep_rows = 2 * N_CHIPS * piece
    rep_pack = _pack([grep[n] for n in REPLICATED], rep_rows).reshape(N_CHIPS, 2, piece, PACK_C)
    total = _round_up(small_rows + piece, ROW_TILE // 2)
    misc = _pack([gs[n] for n in small] + [rep_pack], total, lead=2)
    reds = _reduce_scatter([gs[n] for n in LARGE_SHARDED] + [misc])
    grad = dict(zip(LARGE_SHARDED, reds[:-1]))
    grad.update(zip(small, _unpack(reds[-1], small_shapes, lead=1)))
    rep_mine = reds[-1][:, small_rows:small_rows + piece]
    rep_all = _ag_chips("ag_replicated", [rep_mine])[0].reshape(1, rep_rows, PACK_C)

    outs = {}
    for n in SHARDED:
        shp = grad[n].shape
        three_d = (shp[0], -1, shp[-1])
        params = (grad[n], *(_stored(n, a[p + n]) for p in ('', 'm_', 'v_')))
        res3 = _adamw("adamw", *(t.reshape(three_d) for t in params))
        for kind, arr in zip(('grad', 'delta', 'new_m', 'new_v'), (grad[n], *res3)):
            outs[kind + '_' + n] = _stored(n, arr.reshape(shp))
    w, m, v = (_pack([a[p + n] for n in REPLICATED], rep_rows)[None] for p in ('', 'm_', 'v_'))
    res3 = _adamw("adamw_replicated", rep_all, w, m, v)
    for kind, arr in zip(('grad', 'delta', 'new_m', 'new_v'), (rep_all, *res3)):
        for n, t in zip(REPLICATED, _unpack(arr[0], rep_shapes)):
            outs[kind + '_' + n] = t
    res = [loss, gx[None]]
    for kind in ('grad', 'delta', 'new_m', 'new_v'):
        res += [outs[kind + '_' + n] for n in W_NAMES]
    return tuple(res)


def kernel(x, ffn1_norm, ffn1_w_gate, ffn1_w_up, ffn1_w_down, mix_norm, w_in, s5_lambda_re, s5_lambda_im, s5_log_dt, s5_b_re, s5_b_im, s5_c_re, s5_c_im, s5_d, s5_w_glu, gla_w_alpha, gla_b_alpha, gla_norm, attn_q_norm, attn_k_norm, w_branch_s5, w_branch_gla, w_branch_attn, w_merge_gate, b_merge_gate, w_out, ffn2_norm, ffn2_w_gate, ffn2_w_up, ffn2_w_down, final_norm, loss_target, m_ffn1_norm, m_ffn1_w_gate, m_ffn1_w_up, m_ffn1_w_down, m_mix_norm, m_w_in, m_s5_lambda_re, m_s5_lambda_im, m_s5_log_dt, m_s5_b_re, m_s5_b_im, m_s5_c_re, m_s5_c_im, m_s5_d, m_s5_w_glu, m_gla_w_alpha, m_gla_b_alpha, m_gla_norm, m_attn_q_norm, m_attn_k_norm, m_w_branch_s5, m_w_branch_gla, m_w_branch_attn, m_w_merge_gate, m_b_merge_gate, m_w_out, m_ffn2_norm, m_ffn2_w_gate, m_ffn2_w_up, m_ffn2_w_down, m_final_norm, v_ffn1_norm, v_ffn1_w_gate, v_ffn1_w_up, v_ffn1_w_down, v_mix_norm, v_w_in, v_s5_lambda_re, v_s5_lambda_im, v_s5_log_dt, v_s5_b_re, v_s5_b_im, v_s5_c_re, v_s5_c_im, v_s5_d, v_s5_w_glu, v_gla_w_alpha, v_gla_b_alpha, v_gla_norm, v_attn_q_norm, v_attn_k_norm, v_w_branch_s5, v_w_branch_gla, v_w_branch_attn, v_w_merge_gate, v_b_merge_gate, v_w_out, v_ffn2_norm, v_ffn2_w_gate, v_ffn2_w_up, v_ffn2_w_down, v_final_norm):
    return _train_step(dict(locals()))
```

```python
import functools
import math

import jax
import jax.numpy as jnp
import numpy as np
from jax import lax
from jax.experimental import pallas as pl
from jax.experimental.pallas import tpu as pltpu

F32 = jnp.float32
BF16 = jnp.bfloat16

D_MODEL = 1024
D_FF = 2816
DEPTH = 2
EPS = 1e-6
S5_G, S5_H, S5_P = 32, 16, 64
S5_W = S5_G * S5_H
S5_N = S5_G * S5_P
S5_TG = 8
S5_TILES = S5_G // S5_TG
S5_TC, S5_TS = S5_TG * S5_H, S5_TG * S5_P
GLA_H, GLA_D = 4, 128
GLA_W = GLA_H * GLA_D
GLA_R = 16
GLA_TAU = 16.0
GLA_C = 64
AT_QH, AT_KH, AT_D = 8, 2, 64
AT_W = AT_QH * AT_D
AT_KW = AT_KH * AT_D
GRID_W = 64
ROPE_BASE = 10000.0
ZP_W = 3456
ZC_AK, ZC_AV, ZC_Z = 24, 25, 26

ADAM_LR, ADAM_B1, ADAM_B2, ADAM_EPS, ADAM_WD, ADAM_STEP = 0.001, 0.9, 0.999, 1e-08, 0.01, 10

GRAD_DT = BF16
TN_CHUNK = 512
VMEM_LIMIT_V7X = 56 * 1024 * 1024
LANE = 128
SUBLANE = 8

W_NAMES = ['ffn1_norm', 'ffn1_w_gate', 'ffn1_w_up', 'ffn1_w_down', 'mix_norm', 'w_in', 's5_lambda_re', 's5_lambda_im',
           's5_log_dt', 's5_b_re', 's5_b_im', 's5_c_re', 's5_c_im', 's5_d', 's5_w_glu', 'gla_w_alpha', 'gla_b_alpha',
           'gla_norm', 'attn_q_norm', 'attn_k_norm', 'w_branch_s5', 'w_branch_gla', 'w_branch_attn', 'w_merge_gate',
           'b_merge_gate', 'w_out', 'ffn2_norm', 'ffn2_w_gate', 'ffn2_w_up', 'ffn2_w_down', 'final_norm']
SHARD_AXIS = {'ffn1_w_gate': 2, 'ffn1_w_up': 2, 'ffn1_w_down': 1, 'w_in': 2, 's5_w_glu': 1, 'gla_w_alpha': 3,
              'gla_b_alpha': 2, 'w_branch_s5': 2, 'w_branch_gla': 2, 'w_branch_attn': 2, 'w_merge_gate': 2,
              'w_out': 1, 'ffn2_w_gate': 2, 'ffn2_w_up': 2, 'ffn2_w_down': 1}
F32_GATHER = ['gla_w_alpha', 'gla_b_alpha']
BF16_GATHER = [n for n in W_NAMES if n in SHARD_AXIS and n not in F32_GATHER]
SHARDED = [n for n in W_NAMES if n in SHARD_AXIS]
REPLICATED = [n for n in W_NAMES if n not in SHARD_AXIS]
N_CHIPS = 4
MESH = pl.DeviceIdType.MESH


def _tile(n, target, mult):
    best = None
    for t in range(mult, min(n, target) + 1, mult):
        if n % t == 0:
            best = t
    return best if best is not None else n


def _cparams(sem, vmem_bytes):
    limit = int(min(VMEM_LIMIT_V7X, max(32 * 1024 * 1024, vmem_bytes * 5 // 4)))
    return pltpu.CompilerParams(dimension_semantics=sem, vmem_limit_bytes=limit)


def _pcall(body, **kw):
    shapes = kw.pop('out_shape')
    many = isinstance(shapes, (list, tuple))
    pinned = [pltpu.HBM(t.shape, t.dtype) for t in (shapes if many else [shapes])]
    call = pl.pallas_call(body, out_shape=pinned if many else pinned[0], **kw)

    def run(*ops):
        return call(*[o if jnp.issubdtype(o.dtype, jnp.integer) else pltpu.with_memory_space_constraint(o, pltpu.HBM)
                      for o in ops])

    return run


def _cw(arr, cb, width):
    return (arr, cb, width)


def _win(a):
    if isinstance(a, tuple):
        return a
    return (a, 0, a.shape[-1])


def _nbytes(shape, dtype):
    return int(np.prod(shape)) * jnp.dtype(dtype).itemsize


def _rowwise(name, fn, rows, fulls, outs, accs=(), tl=256):
    rows = [_win(r) for r in rows]
    n_tok = rows[0][0].shape[0]
    tl = _tile(n_tok, tl, 16)
    nr, nf, no = len(rows), len(fulls), len(outs)

    def body(*refs):
        vals = [r[...] for r in refs[:nr + nf]]
        res = fn(*vals)
        if not isinstance(res, (tuple, list)):
            res = (res,)
        for k in range(no):
            o = refs[nr + nf + k]
            o[...] = res[k].astype(o.dtype)
        if accs:
            @pl.when(pl.program_id(0) == 0)
            def _():
                for k in range(len(accs)):
                    a = refs[nr + nf + no + k]
                    a[...] = jnp.zeros(a.shape, a.dtype)
            for k in range(len(accs)):
                a = refs[nr + nf + no + k]
                a[...] += res[no + k]

    in_specs = [pl.BlockSpec((tl, w), functools.partial(lambda i, cb: (i, cb), cb=cb)) for (_, cb, w) in rows]
    in_specs += [pl.BlockSpec(f.shape, functools.partial(lambda i, nd: (0,) * nd, nd=f.ndim)) for f in fulls]
    out_specs = [pl.BlockSpec((tl, c), lambda i: (i, 0)) for (c, _) in outs]
    out_specs += [pl.BlockSpec((r, c), lambda i: (0, 0)) for (r, c) in accs]
    out_shape = [jax.ShapeDtypeStruct((n_tok, c), dt) for (c, dt) in outs]
    out_shape += [jax.ShapeDtypeStruct((r, c), F32) for (r, c) in accs]
    vm = 2 * sum(_nbytes((tl, w), a.dtype) for (a, _, w) in rows)
    vm += 2 * sum(_nbytes(f.shape, f.dtype) for f in fulls)
    vm += 2 * sum(_nbytes((tl, c), dt) for (c, dt) in outs)
    vm += 8 * max([_nbytes((tl, w), F32) for (_, _, w) in rows] + [_nbytes((tl, c), F32) for (c, _) in outs])
    return _pcall(
        body, name=name, grid=(n_tok // tl,), in_specs=in_specs, out_specs=out_specs, out_shape=out_shape,
        compiler_params=_cparams(("arbitrary",) if accs else ("parallel",), vm),
    )(*[r[0] for r in rows], *fulls)


def _mm(name, a, b, *, ta=False, tb=False, out_dtype=F32, alpha=1.0, add=None, tm=512, tn=512):
    a_arr, a_cb, a_w = _win(a)
    b_arr, b_cb, b_w = _win(b)
    if ta:
        kdim, m = a_arr.shape[0], a_w
    else:
        m, kdim = a_arr.shape[0], a_w
    if tb:
        n, kb = b_arr.shape[0], b_w
    else:
        kb, n = b_arr.shape[0], b_w
    assert kdim == kb, (name, kdim, kb)
    if not ta and kdim <= 1024:
        tm = 2 * tm
    tm = _tile(m, tm, LANE if ta else 16)
    tn = _tile(n, tn, LANE)
    if ta:
        a_spec = pl.BlockSpec((kdim, tm), lambda i, j: (0, a_cb * (a_w // tm) + i))
        a_blk = (kdim, tm)
    else:
        a_spec = pl.BlockSpec((tm, kdim), lambda i, j: (i, a_cb))
        a_blk = (tm, kdim)
    if tb:
        b_spec = pl.BlockSpec((tn, kdim), lambda i, j: (j, b_cb))
        b_blk = (tn, kdim)
    else:
        b_spec = pl.BlockSpec((kdim, tn), lambda i, j: (0, b_cb * (b_w // tn) + j))
        b_blk = (kdim, tn)
    dims = (((0 if ta else 1,), (1 if tb else 0,)), ((), ()))
    has_add = add is not None

    def body(*refs):
        a_ref, b_ref = refs[0], refs[1]
        o_ref = refs[-1]
        if ta and not tb and kdim % TN_CHUNK == 0:
            acc = None
            for k0 in range(0, kdim, TN_CHUNK):
                t = lax.dot_general(a_ref[k0:k0 + TN_CHUNK, :].astype(BF16), b_ref[k0:k0 + TN_CHUNK, :].astype(BF16),
                                    dims, preferred_element_type=F32)
                acc = t if acc is None else acc + t
        else:
            acc = lax.dot_general(a_ref[...].astype(BF16), b_ref[...].astype(BF16), dims, preferred_element_type=F32)
        if alpha != 1.0:
            acc = acc * alpha
        if has_add:
            acc = acc + refs[2][...].astype(F32)
        o_ref[...] = acc.astype(o_ref.dtype)

    in_specs = [a_spec, b_spec]
    ops = [a_arr, b_arr]
    if has_add:
        in_specs.append(pl.BlockSpec((tm, tn), lambda i, j: (i, j)))
        ops.append(add)
    vm = 2 * (_nbytes(a_blk, a_arr.dtype) + _nbytes(b_blk, b_arr.dtype) + _nbytes((tm, tn), out_dtype))
    vm += _nbytes(a_blk, BF16) + _nbytes(b_blk, BF16) + 3 * _nbytes((tm, tn), F32)
    return _pcall(
        body, name=name, grid=(m // tm, n // tn), in_specs=in_specs,
        out_specs=pl.BlockSpec((tm, tn), lambda i, j: (i, j)),
        out_shape=jax.ShapeDtypeStruct((m, n), out_dtype),
        compiler_params=_cparams(("parallel", "parallel"), vm),
    )(*ops)


def _sigmoid(x):
    return 1.0 / (1.0 + jnp.exp(-x))


def _rms(x, gain):
    r = lax.rsqrt(jnp.mean(x * x, axis=-1, keepdims=True) + EPS)
    return x * r * gain


def _rms_bwd(x, gain, dh):
    r = lax.rsqrt(jnp.mean(x * x, axis=-1, keepdims=True) + EPS)
    xh = x * r
    dg = jnp.sum(dh * xh, axis=0, keepdims=True)
    dxh = dh * gain
    dx = r * (dxh - xh * jnp.mean(dxh * xh, axis=-1, keepdims=True))
    return dx, dg


_GELU_C = math.sqrt(2.0 / math.pi)


def _gelu(y):
    return 0.5 * y * (1.0 + jnp.tanh(_GELU_C * (y + 0.044715 * y * y * y)))


def _gelu_grad(y):
    th = jnp.tanh(_GELU_C * (y + 0.044715 * y * y * y))
    return 0.5 * (1.0 + th) + 0.5 * y * (1.0 - th * th) * _GELU_C * (1.0 + 3.0 * 0.044715 * y * y)


def _log_sigmoid(x):
    return jnp.minimum(x, 0.0) - jnp.log(1.0 + jnp.exp(-jnp.abs(x)))


def _dot(a, b, dims=(((1,), (0,)), ((), ())), precision=None):
    return lax.dot_general(a, b, dims, preferred_element_type=F32, precision=precision)


def _dot_exact(x, m, dims=(((1,), (0,)), ((), ())), terms=3, m_first=False):
    m16 = m.astype(BF16)
    acc, rest = None, x
    for k in range(terms):
        piece = rest.astype(BF16)
        t = _dot(m16, piece, dims) if m_first else _dot(piece, m16, dims)
        acc = t if acc is None else acc + t
        if k + 1 < terms:
            rest = rest - piece.astype(F32)
    return acc


_NT = (((1,), (1,)), ((), ()))
_TN = (((0,), (0,)), ((), ()))


def _scan_core(b_ref, are_ref, aim_ref, pw, last, of32, *, rev, xf32=None, da_ref=None, lc=512):
    nb, nc = b_ref.shape[0] // SUBLANE, S5_N // lc
    row = lax.broadcasted_iota(jnp.int32, (SUBLANE, lc), 0)

    def cmul(ar, ai, xr, xi):
        return ar * xr - ai * xi, ar * xi + ai * xr

    def hs(c, xr, xi):
        cs = pl.ds(c * lc, lc)
        for k, slot in ((1, 0), (2, 2), (4, 4)):
            ar, ai = pw[slot, :, cs], pw[slot + 1, :, cs]
            if rev:
                sr, si = pltpu.roll(xr, SUBLANE - k, 0), pltpu.roll(xi, SUBLANE - k, 0)
                keep = row < SUBLANE - k
            else:
                sr, si = pltpu.roll(xr, k, 0), pltpu.roll(xi, k, 0)
                keep = row >= k
            sr, si = jnp.where(keep, sr, 0.0), jnp.where(keep, si, 0.0)
            pr, pi = cmul(ar, ai, sr, si)
            xr, xi = xr + pr, xi + pi
        return xr, xi

    @pl.when(pl.program_id(0) == 0)
    def _():
        last[...] = jnp.zeros(last.shape, F32)
        if da_ref is not None:
            da_ref[...] = jnp.zeros(da_ref.shape, F32)
        for c in range(nc):
            cs = pl.ds(c * lc, lc)
            a1r = jnp.broadcast_to(are_ref[:, cs], (SUBLANE, lc))
            a1i = jnp.broadcast_to(aim_ref[:, cs], (SUBLANE, lc))
            a2r, a2i = cmul(a1r, a1i, a1r, a1i)
            a4r, a4i = cmul(a2r, a2i, a2r, a2i)
            for slot, v in enumerate((a1r, a1i, a2r, a2i, a4r, a4i)):
                pw[slot, :, cs] = v
            first = SUBLANE - 1 if rev else 0
            pr, pi = hs(c, jnp.where(row == first, a1r, 0.0), jnp.where(row == first, a1i, 0.0))
            pw[6, :, cs] = pr
            pw[7, :, cs] = pi

    edge = 0 if rev else SUBLANE - 1

    def blk(j, carry):
        jj = (nb - 1 - j) if rev else j
        r0 = pl.multiple_of(jj * SUBLANE, SUBLANE)
        for c in range(nc):
            cre, cim = pl.ds(c * lc, lc), pl.ds(S5_N + c * lc, lc)
            xr, xi = hs(c, b_ref[pl.ds(r0, SUBLANE), cre], b_ref[pl.ds(r0, SUBLANE), cim])
            cr = jnp.broadcast_to(last[edge:edge + 1, cre], (SUBLANE, lc))
            ci = jnp.broadcast_to(last[edge:edge + 1, cim], (SUBLANE, lc))
            pr, pi = cmul(pw[6, :, cre], pw[7, :, cre], cr, ci)
            xr, xi = xr + pr, xi + pi
            if da_ref is not None:
                if rev:
                    qr, qi = pltpu.roll(xr, SUBLANE - 1, 0), pltpu.roll(xi, SUBLANE - 1, 0)
                    fill = row == SUBLANE - 1
                else:
                    qr, qi = pltpu.roll(xr, 1, 0), pltpu.roll(xi, 1, 0)
                    fill = row == 0
                qr, qi = jnp.where(fill, cr, qr), jnp.where(fill, ci, qi)
                zr, zi = xf32[pl.ds(r0, SUBLANE), cre], xf32[pl.ds(r0, SUBLANE), cim]
                da_ref[:, cre] += zr * qr + zi * qi
                da_ref[:, cim] += zr * qi - zi * qr
            last[:, cre] = xr
            last[:, cim] = xi
            of32[pl.ds(r0, SUBLANE), cre] = xr
            of32[pl.ds(r0, SUBLANE), cim] = xi
        return carry

    lax.fori_loop(0, nb, blk, 0)


def _s5_tiles():
    return [(slice((j % S5_TILES) * S5_TC, (j % S5_TILES + 1) * S5_TC), slice(j * S5_TS, (j + 1) * S5_TS))
            for j in range(2 * S5_TILES)]


def _s5_scratch(tb):
    return [pltpu.VMEM((8, SUBLANE, S5_N), F32), pltpu.VMEM((SUBLANE, 2 * S5_N), F32),
            pltpu.VMEM((tb, 2 * S5_N), F32), pltpu.VMEM((tb, 2 * S5_N), F32),
            pltpu.VMEM((S5_TC, 2 * S5_N), BF16), pltpu.VMEM((2 * S5_N, S5_TC), BF16)]


def _s5_fwd(name, z, a_re, a_im, wb, wc, add, *, rev, tb=512):
    n_tok = z.shape[0]
    tb = _tile(n_tok, tb, 16)
    nt = n_tok // tb
    has_add = add is not None

    def body(*refs):
        u_ref, are_ref, aim_ref, wb_ref, wc_ref = refs[:5]
        x_ref, y_ref, pw, last, of32, bu, wb16, wc16 = refs[5 + has_add:]

        @pl.when(pl.program_id(0) == 0)
        def _():
            wb16[...] = wb_ref[...].astype(BF16)
            wc16[...] = wc_ref[...].astype(BF16)

        u16 = u_ref[...].astype(BF16)
        for ch, st in _s5_tiles():
            bu[:, st] = _dot(u16[:, ch], wb16[:, st])
        _scan_core(bu, are_ref, aim_ref, pw, last, of32, rev=rev)
        x16 = of32[...].astype(BF16)
        x_ref[...] = x16
        ys = []
        for k in range(S5_TILES):
            re, im = slice(k * S5_TS, (k + 1) * S5_TS), slice(S5_N + k * S5_TS, S5_N + (k + 1) * S5_TS)
            ys.append(_dot(x16[:, re], wc16[re, :]) + _dot(x16[:, im], wc16[im, :]))
        y = jnp.concatenate(ys, axis=1)
        y_ref[...] = y + refs[5][...] if has_add else y

    tmap = (lambda t: (nt - 1 - t, 0)) if rev else (lambda t: (t, 0))
    const = lambda t: (0, 0)
    in_specs = [pl.BlockSpec((tb, S5_W), tmap), pl.BlockSpec((1, S5_N), const), pl.BlockSpec((1, S5_N), const),
                pl.BlockSpec((S5_TC, 2 * S5_N), const), pl.BlockSpec((2 * S5_N, S5_TC), const)]
    ops = [z, a_re, a_im, wb, wc]
    if has_add:
        in_specs.append(pl.BlockSpec((tb, S5_W), tmap))
        ops.append(add)
    return _pcall(
        body, name=name, grid=(nt,), in_specs=in_specs,
        out_specs=[pl.BlockSpec((tb, 2 * S5_N), tmap), pl.BlockSpec((tb, S5_W), tmap)],
        out_shape=[jax.ShapeDtypeStruct((n_tok, 2 * S5_N), BF16), jax.ShapeDtypeStruct((n_tok, S5_W), F32)],
        scratch_shapes=_s5_scratch(tb), compiler_params=_cparams(("arbitrary",), 10 * tb * 2 * S5_N * 4),
    )(*ops)


def _s5_bwd(name, dy, z, xs, a_re, a_im, wb, wc, add, *, rev, tb=256):
    n_tok = z.shape[0]
    tb = _tile(n_tok, tb, 16)
    nt = n_tok // tb

    def body(dy_ref, u_ref, xs_ref, are_ref, aim_ref, wb_ref, wc_ref, add_ref, du_ref, dwb_ref, dwc_ref, da_ref,
             pw, last, of32, gx, wb16, wc16, xf32):
        @pl.when(pl.program_id(0) == 0)
        def _():
            wb16[...] = wb_ref[...].astype(BF16)
            wc16[...] = wc_ref[...].astype(BF16)
            dwb_ref[...] = jnp.zeros(dwb_ref.shape, F32)
            dwc_ref[...] = jnp.zeros(dwc_ref.shape, F32)

        dy16, u16, x16 = dy_ref[...].astype(BF16), u_ref[...].astype(BF16), xs_ref[...]
        for ch, st in _s5_tiles():
            gx[:, st] = _dot(dy16[:, ch], wc16[st, :], _NT)
        xf32[...] = x16.astype(F32)
        _scan_core(gx, are_ref, aim_ref, pw, last, of32, rev=rev, xf32=xf32, da_ref=da_ref)
        g16 = of32[...].astype(BF16)
        dus = []
        for k in range(S5_TILES):
            re, im = slice(k * S5_TS, (k + 1) * S5_TS), slice(S5_N + k * S5_TS, S5_N + (k + 1) * S5_TS)
            dus.append(_dot(g16[:, re], wb16[:, re], _NT) + _dot(g16[:, im], wb16[:, im], _NT))
        du_ref[...] = jnp.concatenate(dus, axis=1) + add_ref[...]
        for ch, st in _s5_tiles():
            dwb_ref[:, st] += _dot(u16[:, ch], g16[:, st], _TN)
            dwc_ref[st, :] += _dot(x16[:, st], dy16[:, ch], _TN)

    tmap = (lambda t: (nt - 1 - t, 0)) if rev else (lambda t: (t, 0))
    const = lambda t: (0, 0)
    row_w, row_s = pl.BlockSpec((tb, S5_W), tmap), pl.BlockSpec((tb, 2 * S5_N), tmap)
    return _pcall(
        body, name=name, grid=(nt,),
        in_specs=[row_w, row_w, row_s, pl.BlockSpec((1, S5_N), const), pl.BlockSpec((1, S5_N), const),
                  pl.BlockSpec((S5_TC, 2 * S5_N), const), pl.BlockSpec((2 * S5_N, S5_TC), const), row_w],
        out_specs=[row_w, pl.BlockSpec((S5_TC, 2 * S5_N), const), pl.BlockSpec((2 * S5_N, S5_TC), const),
                   pl.BlockSpec((SUBLANE, 2 * S5_N), const)],
        out_shape=[jax.ShapeDtypeStruct((n_tok, S5_W), F32), jax.ShapeDtypeStruct((S5_TC, 2 * S5_N), F32),
                   jax.ShapeDtypeStruct((2 * S5_N, S5_TC), F32), jax.ShapeDtypeStruct((SUBLANE, 2 * S5_N), F32)],
        scratch_shapes=_s5_scratch(tb) + [pltpu.VMEM((tb, 2 * S5_N), F32)],
        compiler_params=_cparams(("arbitrary",), 12 * tb * 2 * S5_N * 4),
    )(dy, z, xs, a_re, a_im, wb, wc, add)


GLA_TB = 512


def _gla_block_terms(la, q, k, rev):
    trif = _chunk_mask(rev).astype(F32)
    chunks = [la[t:t + GLA_C] for t in range(0, la.shape[0], GLA_C)]
    bcum = jnp.concatenate([_dot_exact(c, trif, m_first=True) for c in chunks], axis=0)
    blast = _per_chunk_sum(la)
    eb, enb, ee = jnp.exp(bcum), jnp.exp(-bcum), jnp.exp(blast - bcum)
    return trif, jnp.exp(blast), eb, enb, ee, q * eb, k * enb, k * ee


def _per_chunk_sum(t):
    return jnp.concatenate([jnp.broadcast_to(jnp.sum(t[r:r + GLA_C], axis=0, keepdims=True), (GLA_C, t.shape[1]))
                            for r in range(0, t.shape[0], GLA_C)], axis=0)


def _chunk_mask(rev):
    ri = lax.broadcasted_iota(jnp.int32, (GLA_C, GLA_C), 0)
    ci = lax.broadcasted_iota(jnp.int32, (GLA_C, GLA_C), 1)
    return (ri <= ci) if rev else (ri >= ci)


def _gla_fwd(name, z, la, *, rev):
    n_tok = z.shape[0]
    tb = _tile(n_tok, GLA_TB, GLA_C)
    nblk, per = n_tok // tb, tb // GLA_C
    scale = GLA_D ** -0.5

    def body(q_ref, k_ref, v_ref, la_ref, o_ref, s_ref, st, qd_s, kd_s, ke_s, v_s, dec_s):
        @pl.when(pl.program_id(0) == 0)
        def _():
            st[...] = jnp.zeros(st.shape, F32)

        _, dec, _, _, _, qd, kd, ke = _gla_block_terms(la_ref[...], q_ref[...].astype(F32) * scale,
                                                                 k_ref[...].astype(F32), rev)
        qd_s[...], kd_s[...], ke_s[...] = qd.astype(BF16), kd.astype(BF16), ke.astype(BF16)
        v_s[...] = v_ref[...].astype(BF16)
        dec_s[...] = dec
        tri = _chunk_mask(rev)
        for step in range(per):
            t = per - 1 - step if rev else step
            rows = pl.ds(t * GLA_C, GLA_C)
            for h in range(GLA_H):
                lanes = pl.ds(h * GLA_D, GLA_D)
                qd16, v16 = qd_s[rows, lanes], v_s[rows, lanes]
                sc = jnp.where(tri, _dot(qd16, kd_s[rows, lanes], _NT), 0.0)
                s16 = st[h].astype(BF16)
                s_ref[h, step] = s16
                o_ref[rows, lanes] = _dot(sc.astype(BF16), v16) + _dot(qd16, s16, _NT)
                st[h] = st[h] * dec_s[pl.ds(t * GLA_C, 1), lanes] + _dot(v16, ke_s[rows, lanes], _TN)

    bmap = (lambda n: nblk - 1 - n) if rev else (lambda n: n)

    def zspec(cb):
        return pl.BlockSpec((tb, GLA_W), lambda n: (bmap(n), cb))

    half = pltpu.VMEM((tb, GLA_W), BF16)
    return _pcall(
        body, name=name, grid=(nblk,),
        in_specs=[zspec(1), zspec(2), zspec(3), zspec(0)],
        out_specs=[zspec(0), pl.BlockSpec((GLA_H, per, GLA_D, GLA_D), lambda n: (0, n, 0, 0))],
        out_shape=[jax.ShapeDtypeStruct((n_tok, GLA_W), F32),
                   jax.ShapeDtypeStruct((GLA_H, n_tok // GLA_C, GLA_D, GLA_D), BF16)],
        scratch_shapes=[pltpu.VMEM((GLA_H, GLA_D, GLA_D), F32), half, half, half, half, pltpu.VMEM((tb, GLA_W), F32)],
        compiler_params=_cparams(("arbitrary",), 40 * tb * GLA_W * 4),
    )(z, z, z, la)


def _gla_bwd(name, z, la, s_prev, do, *, rev):
    n_tok = z.shape[0]
    tb = _tile(n_tok, GLA_TB, GLA_C)
    nblk, per = n_tok // tb, tb // GLA_C
    scale = GLA_D ** -0.5

    def body(q_ref, k_ref, v_ref, la_ref, s_ref, do_ref, dq_ref, dk_ref, dv_ref, dla_ref,
             dst, qd_s, kd_s, ke_s, v_s, do_s, dec_s, dqd_s, dkd_s, dke_s, dd_s):
        @pl.when(pl.program_id(0) == 0)
        def _():
            dst[...] = jnp.zeros(dst.shape, F32)

        trif, dec, eb, enb, ee, qd, kd, ke = _gla_block_terms(la_ref[...], q_ref[...].astype(F32) * scale,
                                                                 k_ref[...].astype(F32), rev)
        qd_s[...], kd_s[...], ke_s[...] = qd.astype(BF16), kd.astype(BF16), ke.astype(BF16)
        v_s[...] = v_ref[...].astype(BF16)
        do_s[...] = do_ref[...].astype(BF16)
        dec_s[...] = dec
        tri = _chunk_mask(rev)
        for step in range(per):
            t = step if rev else per - 1 - step
            rows = pl.ds(t * GLA_C, GLA_C)
            for h in range(GLA_H):
                lanes = pl.ds(h * GLA_D, GLA_D)
                qd16, kd16, ke16 = qd_s[rows, lanes], kd_s[rows, lanes], ke_s[rows, lanes]
                v16, do16 = v_s[rows, lanes], do_s[rows, lanes]
                sc16 = jnp.where(tri, _dot(qd16, kd16, _NT), 0.0).astype(BF16)
                dsc16 = jnp.where(tri, _dot(do16, v16, _NT), 0.0).astype(BF16)
                s16 = s_ref[h, per - 1 - step]
                dsn = dst[h]
                dsn16 = dsn.astype(BF16)
                dec_c = dec_s[pl.ds(t * GLA_C, 1), lanes]
                dqd_s[rows, lanes] = _dot(dsc16, kd16) + _dot(do16, s16)
                dkd_s[rows, lanes] = _dot(dsc16, qd16, _TN)
                dke_s[rows, lanes] = _dot(v16, dsn16)
                dv_ref[rows, lanes] = _dot(sc16, do16, _TN) + _dot(ke16, dsn16, _NT)
                ddec = jnp.sum(dsn * s16.astype(F32), axis=0, keepdims=True) * dec_c
                dd_s[rows, lanes] = jnp.broadcast_to(ddec, (GLA_C, GLA_D))
                dst[h] = dsn * dec_c + _dot(do16, qd16, _TN)
        dqd, dkd, dke = dqd_s[...], dkd_s[...], dke_s[...]
        dq_ref[...] = dqd * eb * scale
        dk_ref[...] = dkd * enb + dke * ee
        db = dqd * qd - dkd * kd - dke * ke
        dcum = jnp.concatenate([_dot_exact(db[r:r + GLA_C], trif, _TN, m_first=True) for r in range(0, tb, GLA_C)], axis=0)
        dla_ref[...] = dcum + _per_chunk_sum(dke * ke) + dd_s[...]

    bmap = (lambda n: n) if rev else (lambda n: nblk - 1 - n)

    def zspec(cb):
        return pl.BlockSpec((tb, GLA_W), lambda n: (bmap(n), cb))

    o4 = jax.ShapeDtypeStruct((n_tok, GLA_W), F32)
    half, full = pltpu.VMEM((tb, GLA_W), BF16), pltpu.VMEM((tb, GLA_W), F32)
    return _pcall(
        body, name=name, grid=(nblk,),
        in_specs=[zspec(1), zspec(2), zspec(3), zspec(0),
                  pl.BlockSpec((GLA_H, per, GLA_D, GLA_D), lambda n: (0, nblk - 1 - n, 0, 0)), zspec(0)],
        out_specs=[zspec(0)] * 4, out_shape=[o4] * 4,
        scratch_shapes=[pltpu.VMEM((GLA_H, GLA_D, GLA_D), F32), half, half, half, half, half, full, full, full, full, full],
        compiler_params=_cparams(("arbitrary",), 64 * tb * GLA_W * 4),
    )(z, z, z, la, s_prev, do)


_PAIRS = AT_W // LANE
_PAIRS_PER_KV = _PAIRS // AT_KH


def _attn_fwd(name, q, kd, vd, tq=256):
    n_tok = q.shape[0]
    tq = _tile(n_tok, tq, 16)

    def body(q_ref, k_ref, v_ref, o_ref, l_ref):
        qv, kv, vv = q_ref[...], k_ref[...], v_ref[...]
        low = lax.broadcasted_iota(jnp.int32, qv.shape, 1) < AT_D
        res = []
        for keep in (low, jnp.logical_not(low)):
            s = _dot(jnp.where(keep, qv, jnp.zeros_like(qv)), kv, _NT)
            m = jnp.max(s, axis=-1, keepdims=True)
            p = jnp.exp(s - m)
            den = jnp.sum(p, axis=-1, keepdims=True)
            res.append((_dot(p.astype(BF16), vv) / den, m + jnp.log(den)))
        o_ref[...] = jnp.where(low, res[0][0], res[1][0])
        l_ref[...] = jnp.where(low, res[0][1], res[1][1])

    qspec = pl.BlockSpec((tq, LANE), lambda p, i: (i, p))
    kspec = pl.BlockSpec((n_tok, LANE), lambda p, i: (0, p // _PAIRS_PER_KV))
    vm = 5 * _nbytes((tq, n_tok), F32) + 8 * _nbytes((n_tok, LANE), BF16)
    out = jax.ShapeDtypeStruct((n_tok, AT_W), F32)
    return _pcall(
        body, name=name, grid=(_PAIRS, n_tok // tq), in_specs=[qspec, kspec, kspec], out_specs=[qspec, qspec],
        out_shape=[out, out], compiler_params=_cparams(("parallel", "parallel"), vm),
    )(q, kd, vd)


def _attn_bwd(name, q, kd, vd, o, lse, do, tq=256):
    n_tok = q.shape[0]
    tq = _tile(n_tok, tq, 16)

    def body(q_ref, k_ref, v_ref, o_ref, l_ref, do_ref, dq_ref, dk_ref, dv_ref):
        @pl.when((pl.program_id(1) == 0) & (pl.program_id(2) == 0))
        def _():
            dk_ref[...] = jnp.zeros(dk_ref.shape, F32)
            dv_ref[...] = jnp.zeros(dv_ref.shape, F32)

        qv, kv, vv = q_ref[...], k_ref[...], v_ref[...]
        dof, lv = do_ref[...], l_ref[...]
        do16 = dof.astype(BF16)
        doo = dof * o_ref[...]
        low = lax.broadcasted_iota(jnp.int32, qv.shape, 1) < AT_D
        dq = jnp.zeros(qv.shape, F32)
        for keep in (low, jnp.logical_not(low)):
            qm = jnp.where(keep, qv, jnp.zeros_like(qv))
            dom = jnp.where(keep, do16, jnp.zeros_like(do16))
            lh = jnp.max(jnp.where(keep, lv, -1e30), axis=-1, keepdims=True)
            p = jnp.exp(_dot(qm, kv, _NT) - lh)
            delta = jnp.sum(jnp.where(keep, doo, 0.0), axis=-1, keepdims=True)
            ds16 = (p * (_dot(dom, vv, _NT) - delta)).astype(BF16)
            dq = jnp.where(keep, _dot(ds16, kv), dq)
            dk_ref[...] += _dot(ds16, qm, _TN)
            dv_ref[...] += _dot(p.astype(BF16), dom, _TN)
        dq_ref[...] = dq

    qspec = pl.BlockSpec((tq, LANE), lambda g, j, i: (i, g * _PAIRS_PER_KV + j))
    kspec = pl.BlockSpec((n_tok, LANE), lambda g, j, i: (0, g))
    vm = 7 * _nbytes((tq, n_tok), F32) + 12 * _nbytes((n_tok, LANE), F32)
    dup = jax.ShapeDtypeStruct((n_tok, AT_KH * LANE), F32)
    return _pcall(
        body, name=name, grid=(AT_KH, _PAIRS_PER_KV, n_tok // tq),
        in_specs=[qspec, kspec, kspec, qspec, qspec, qspec], out_specs=[qspec, kspec, kspec],
        out_shape=[jax.ShapeDtypeStruct((n_tok, AT_W), F32), dup, dup],
        compiler_params=_cparams(("parallel", "arbitrary", "arbitrary"), vm),
    )(q, kd, vd, o, lse, do)


def _s5_prep(lam_re, lam_im, log_dt, b_re, b_im, c_re, c_im):
    dt = jnp.exp(log_dt)[:, None]
    er = jnp.exp(lam_re * dt)
    a_re, a_im = er * jnp.cos(lam_im * dt), er * jnp.sin(lam_im * dt)
    nr, ni = a_re - 1.0, a_im
    den = lam_re * lam_re + lam_im * lam_im
    f_re, f_im = (nr * lam_re + ni * lam_im) / den, (ni * lam_re - nr * lam_im) / den
    bb_re = f_re[..., None] * b_re - f_im[..., None] * b_im
    bb_im = f_re[..., None] * b_im + f_im[..., None] * b_re
    def tiles(t, axis):
        _, a, b = t.shape
        eye = jnp.eye(S5_TG, dtype=t.dtype)
        blocks = t.reshape(S5_TILES, S5_TG, a, 1, b) * eye[None, :, None, :, None]
        if axis == 0:
            return blocks.reshape(S5_TILES * S5_TG * a, S5_TG * b)
        return blocks.transpose(1, 2, 0, 3, 4).reshape(S5_TG * a, S5_TILES * S5_TG * b)

    w_in = jnp.concatenate([tiles(bb_re.transpose(0, 2, 1), 1), tiles(bb_im.transpose(0, 2, 1), 1)], axis=1)
    w_out = jnp.concatenate([tiles(c_re.transpose(0, 2, 1), 0), -tiles(c_im.transpose(0, 2, 1), 0)], axis=0)
    return a_re.reshape(1, S5_N), a_im.reshape(1, S5_N), w_in, w_out


def _rope_tables(n_tok):
    quarter = AT_D // 4
    d = np.arange(AT_D)
    e = d % (AT_D // 2)
    inv = (ROPE_BASE ** (-(e % quarter).astype(np.float64) * 2.0 / (AT_D // 2))).astype(np.float32)
    sign = np.where(e < quarter, -1.0, 1.0).astype(np.float32)
    rot = np.zeros((AT_D, AT_D), np.float32)
    rot[np.where(e < quarter, d + quarter, d - quarter), d] = 1.0
    pos = jnp.arange(n_tok, dtype=jnp.int32)
    axis_pos = jnp.where(jnp.asarray(d // (AT_D // 2) == 0)[None, :], (pos // GRID_W)[:, None], (pos % GRID_W)[:, None])
    ang = axis_pos.astype(F32) * jnp.asarray(inv)[None, :]
    return jnp.cos(ang), jnp.sin(ang) * jnp.asarray(sign)[None, :], rot


def _head_consts(n_tok, heads):
    cos, sin, rot = _rope_tables(n_tok)
    eye = np.eye(heads, dtype=np.float32)
    mean = np.kron(eye, np.full((AT_D, AT_D), 1.0 / AT_D, np.float32))
    return jnp.tile(cos, (1, heads)), jnp.tile(sin, (1, heads)), jnp.asarray(mean), jnp.asarray(np.kron(eye, rot))


def _pad_w_in(wt):
    return jnp.concatenate([wt[:2560], wt[2592:3360], wt[2560:2592], jnp.zeros((96, wt.shape[1]), wt.dtype)], axis=0)


def _unpad_w_in(gt):
    return jnp.concatenate([gt[:2560], gt[3328:3360], gt[2560:3328]], axis=0)


def _ffn_fwd(x, g, wg, wu, wd, d):
    n_tok = x.shape[0]
    fq = wd.shape[-2]
    tm = _tile(n_tok, 256, 16)
    w_bytes = N_CHIPS * fq * D_MODEL * 2

    def up(x_ref, g_ref, wg_ref, wu_ref, h_ref, a_ref, b_ref, s_ref):
        hv = _rms(x_ref[...], g_ref[...]).astype(BF16)
        h_ref[...] = hv
        for j in range(N_CHIPS):
            av, bv = _dot(hv, wg_ref[j, 0], _NT), _dot(hv, wu_ref[j, 0], _NT)
            a_ref[j] = av.astype(BF16)
            b_ref[j] = bv.astype(BF16)
            s_ref[j] = (av * _sigmoid(av) * bv).astype(BF16)

    wspec = pl.BlockSpec((N_CHIPS, 1, fq, D_MODEL), lambda i: (0, d, 0, 0))
    aspec = pl.BlockSpec((N_CHIPS, tm, fq), lambda i: (0, i, 0))
    xspec = pl.BlockSpec((tm, D_MODEL), lambda i: (i, 0))
    act = jax.ShapeDtypeStruct((N_CHIPS, n_tok, fq), BF16)
    h, a, b, s = _pcall(
        up, name="ffn_up", grid=(n_tok // tm,),
        in_specs=[xspec, pl.BlockSpec((1, D_MODEL), lambda i: (0, 0)), wspec, wspec],
        out_specs=[xspec, aspec, aspec, aspec],
        out_shape=[jax.ShapeDtypeStruct((n_tok, D_MODEL), BF16), act, act, act],
        compiler_params=_cparams(("parallel",), 4 * w_bytes + 6 * N_CHIPS * tm * fq * 2 + 12 * tm * fq * 4
                                 + 8 * tm * D_MODEL * 4),
    )(x, g, wg, wu)

    def down(s_ref, w_ref, x_ref, o_ref):
        acc = _dot(s_ref[0], w_ref[0, 0])
        for j in range(1, N_CHIPS):
            acc = acc + _dot(s_ref[j], w_ref[j, 0])
        o_ref[...] = x_ref[...] + 0.5 * acc

    xo = _pcall(
        down, name="ffn_down", grid=(n_tok // tm,), in_specs=[aspec, wspec, xspec], out_specs=xspec,
        out_shape=jax.ShapeDtypeStruct((n_tok, D_MODEL), F32),
        compiler_params=_cparams(("parallel",), 2 * w_bytes + 2 * N_CHIPS * tm * fq * 2 + 8 * tm * D_MODEL * 4),
    )(s, wd, x)
    return xo, (x, h, a, b, s)


def _ffn_bwd(dy, dy16, saved, g, wg, wu, wd, d, into):
    x, h, a, b, s = saved
    n_tok = x.shape[0]
    fq = wd.shape[-2]
    tm = _tile(n_tok, 512, 16)

    ta_ = _tile(n_tok, 256, 16)

    def dact(dy_ref, w_ref, a_ref, b_ref, da_ref, db_ref):
        dyv = dy_ref[...]
        for j in range(N_CHIPS):
            ds = 0.5 * _dot(dyv, w_ref[j, 0], _NT)
            av, bv = a_ref[j].astype(F32), b_ref[j].astype(F32)
            sg = _sigmoid(av)
            da_ref[j] = (ds * bv * sg * (1.0 + av * (1.0 - sg))).astype(BF16)
            db_ref[j] = (ds * av * sg).astype(BF16)

    aspec = pl.BlockSpec((N_CHIPS, ta_, fq), lambda i: (0, i, 0))
    act = jax.ShapeDtypeStruct((N_CHIPS, n_tok, fq), BF16)
    da, db = _pcall(
        dact, name="ffn_dact", grid=(n_tok // ta_,),
        in_specs=[pl.BlockSpec((ta_, D_MODEL), lambda i: (i, 0)),
                  pl.BlockSpec((N_CHIPS, 1, fq, D_MODEL), lambda i: (0, d, 0, 0)), aspec, aspec],
        out_specs=[aspec, aspec], out_shape=[act, act],
        compiler_params=_cparams(("parallel",), 2 * N_CHIPS * fq * D_MODEL * 2 + 8 * N_CHIPS * ta_ * fq * 2
                                 + 12 * ta_ * fq * 4),
    )(dy16, wd, a, b)

    any_spec = pl.BlockSpec(memory_space=pl.ANY)
    alias = {} if into is None else {2: 0}
    tn = _tile(D_MODEL, 512, LANE)

    def dw_call(t, rhs, scale, prev):
        def dw(t_ref, r_ref, *rest):
            ck = _tile(n_tok, TN_CHUNK, 16)
            acc = _dot(t_ref[0, 0:ck, :], r_ref[0:ck, :], _TN)
            for k0 in range(ck, n_tok, ck):
                acc = acc + _dot(t_ref[0, k0:k0 + ck, :], r_ref[k0:k0 + ck, :], _TN)
            rest[-1][0, 0] = (scale * acc).astype(GRAD_DT)

        return _pcall(
            dw, name="ffn_dw", grid=(N_CHIPS, D_MODEL // tn),
            in_specs=[pl.BlockSpec((1, n_tok, fq), lambda j, n: (j, 0, 0)), pl.BlockSpec((n_tok, tn), lambda j, n: (0, n))]
            + ([] if prev is None else [any_spec]),
            out_specs=pl.BlockSpec((1, 1, fq, tn), lambda j, n: (j, d, 0, n)),
            out_shape=jax.ShapeDtypeStruct((N_CHIPS, DEPTH, fq, D_MODEL), GRAD_DT), input_output_aliases=alias,
            compiler_params=_cparams(("parallel", "parallel"), 6 * n_tok * (fq + tn) * 2 + 4 * fq * tn * 4),
        )(t, rhs, *([] if prev is None else [prev]))

    dwg = dw_call(da, h, 1.0, None if into is None else into[0])
    dwu = dw_call(db, h, 1.0, None if into is None else into[1])
    dwd = dw_call(s, dy16, 0.5, None if into is None else into[2])
    tr = _tile(n_tok, 256, 16)

    def dh_norm(da_ref, db_ref, wg_ref, wu_ref, dy_ref, x_ref, g_ref, dx_ref, dx16_ref, dg_ref):
        dh = _dot(da_ref[0], wg_ref[0, 0]) + _dot(db_ref[0], wu_ref[0, 0])
        for j in range(1, N_CHIPS):
            dh = dh + _dot(da_ref[j], wg_ref[j, 0]) + _dot(db_ref[j], wu_ref[j, 0])
        dx, dg = _rms_bwd(x_ref[...], g_ref[...], dh)
        dx = dy_ref[...] + dx
        dx_ref[...] = dx
        dx16_ref[...] = dx.astype(BF16)

        @pl.when(pl.program_id(0) == 0)
        def _():
            dg_ref[...] = jnp.zeros(dg_ref.shape, F32)

        dg_ref[...] += dg

    tspec = pl.BlockSpec((N_CHIPS, tr, fq), lambda i: (0, i, 0))
    wspec = pl.BlockSpec((N_CHIPS, 1, fq, D_MODEL), lambda i: (0, d, 0, 0))
    rspec = pl.BlockSpec((tr, D_MODEL), lambda i: (i, 0))
    vspec = pl.BlockSpec((1, D_MODEL), lambda i: (0, 0))
    dx, dx16, dg = _pcall(
        dh_norm, name="ffn_dh_norm", grid=(n_tok // tr,),
        in_specs=[tspec, tspec, wspec, wspec, rspec, rspec, vspec], out_specs=[rspec, rspec, vspec],
        out_shape=[jax.ShapeDtypeStruct((n_tok, D_MODEL), F32), jax.ShapeDtypeStruct((n_tok, D_MODEL), BF16),
                   jax.ShapeDtypeStruct((1, D_MODEL), F32)],
        compiler_params=_cparams(("arbitrary",), 4 * N_CHIPS * D_MODEL * fq * 2 + 4 * N_CHIPS * tr * fq * 2
                                 + 16 * tr * D_MODEL * 4),
    )(da, db, wg, wu, dy, x, g)
    return dx, dx16, dg[0], (dwg, dwu, dwd)


def _alpha_pads(w_alpha):
    z = jnp.zeros((LANE - 2 * GLA_R, GLA_W), F32)
    wf = jnp.concatenate([w_alpha[0], jnp.zeros((GLA_R, GLA_W), F32), z], axis=0)
    wb = jnp.concatenate([jnp.zeros((GLA_R, GLA_W), F32), w_alpha[1], z], axis=0)
    return wf, wb


def _dup_heads(t):
    low = lax.broadcasted_iota(jnp.int32, t.shape, 1) < AT_D
    h0, h1 = jnp.where(low, t, 0.0), jnp.where(low, 0.0, t)
    return jnp.concatenate([h0 + pltpu.roll(h0, AT_D, 1), h1 + pltpu.roll(h1, AT_D, 1)], axis=1)


def _fold_heads(t):
    low = lax.broadcasted_iota(jnp.int32, (t.shape[0], LANE), 1) < AT_D
    g0, g1 = t[:, :LANE], t[:, LANE:]
    return jnp.where(low, g0 + pltpu.roll(g0, AT_D, 1), g1 + pltpu.roll(g1, AT_D, 1))


def _mixer_fwd(x, w):
    n_tok = x.shape[0]
    wip, wmg = w['w_in_p'], w['w_merge_gate']
    tr = _tile(n_tok, 256, 16)

    def project(x_ref, g_ref, wip_ref, wmg_ref, h_ref, z_ref, gp_ref):
        hv = _rms(x_ref[...], g_ref[...]).astype(BF16)
        h_ref[...] = hv
        z_ref[...] = _dot(hv, wip_ref[...], _NT).astype(BF16)
        gp_ref[...] = _dot(hv, wmg_ref[...]).astype(BF16)

    def rows(c):
        return pl.BlockSpec((tr, c), lambda i: (i, 0))

    def whole(t):
        return pl.BlockSpec(t.shape, lambda i: (0, 0))

    h, z, gp = _pcall(
        project, name="mix_in", grid=(n_tok // tr,),
        in_specs=[rows(D_MODEL), whole(w['mix_norm']), whole(wip), whole(wmg)],
        out_specs=[rows(D_MODEL), rows(ZP_W), rows(3 * D_MODEL)],
        out_shape=[jax.ShapeDtypeStruct((n_tok, D_MODEL), BF16), jax.ShapeDtypeStruct((n_tok, ZP_W), BF16),
                   jax.ShapeDtypeStruct((n_tok, 3 * D_MODEL), BF16)],
        compiler_params=_cparams(("parallel",), 4 * (wip.size + wmg.size) + 16 * tr * (ZP_W + 3 * D_MODEL)),
    )(x, w['mix_norm'], wip, wmg)
    u = _cw(z, 0, S5_W)

    xs, yy = [], None
    for d, rev in ((0, False), (1, True)):
        a_re, a_im, wb, wc = w['s5'][d]
        x_d, yy = _s5_fwd("s5_rev" if rev else "s5_fwd", z, a_re, a_im, wb, wc, yy, rev=rev)
        xs.append(x_d)

    def s5_post(yv, uv, dv, wglu):
        y0 = yv + dv * uv
        y1 = _gelu(y0)
        t = _dot(y1.astype(BF16), wglu)
        return y0, t, y1 * _sigmoid(t)

    y0, t_glu, y_s5 = _rowwise("s5_post", s5_post, [yy, u], [w['s5_d'], w['s5_w_glu']],
                               [(S5_W, F32), (S5_W, F32), (S5_W, BF16)])

    wa_f, wa_b = _alpha_pads(w['gla_w_alpha'])
    ba_f, ba_b = w['gla_b_alpha'][0:1], w['gla_b_alpha'][1:2]
    zz = _cw(z, ZC_Z, LANE)

    def gla_prep(zv, waf, wab, bf, bb):
        z16 = zv.astype(BF16)
        lf = _dot(z16, waf.astype(BF16)) + bf
        lb = _dot(z16, wab.astype(BF16)) + bb
        return _log_sigmoid(lf) / GLA_TAU, _log_sigmoid(lb) / GLA_TAU

    la_f, la_b = _rowwise("gla_prep", gla_prep, [zz], [wa_f, wa_b, ba_f, ba_b], [(GLA_W, F32), (GLA_W, F32)])
    o_f, s_f = _gla_fwd("gla_fwd", z, la_f, rev=False)
    o_b, s_b = _gla_fwd("gla_rev", z, la_b, rev=True)
    gate = _cw(z, 4, GLA_W)

    def gla_post(of, ob, gv, gn):
        o, gv = of + ob, gv.astype(F32)
        outs = []
        for hh in range(GLA_H):
            sl = slice(hh * GLA_D, (hh + 1) * GLA_D)
            gh = gv[:, sl]
            outs.append(_rms(o[:, sl], gn) * (gh * _sigmoid(gh)))
        return jnp.concatenate(outs, axis=1)

    y_gla = _rowwise("gla_post", gla_post, [o_f, o_b, gate], [w['gla_norm']], [(GLA_W, BF16)])[0]

    cq, sq, mq, rq = _head_consts(n_tok, AT_QH)
    ck, sk, mk, rk = _head_consts(n_tok, AT_KH)
    gq = jnp.tile(w['attn_q_norm'], (1, AT_QH))
    gk = jnp.tile(w['attn_k_norm'], (1, AT_KH))

    def rope_fwd(v, gain, cos, sin, mean, rot, scale):
        r = lax.rsqrt(_dot_exact(v * v, mean, terms=2) + EPS)
        vn = v * r * gain
        return (vn * cos + _dot_exact(vn, rot, terms=2) * sin) * scale

    def attn_prep(qv, kv, vv, cqv, sqv, ckv, skv, gqv, gkv, mqv, rqv, mkv, rkv):
        qv, kv, vv = qv.astype(F32), kv.astype(F32), vv.astype(F32)
        kr = rope_fwd(kv, gkv, ckv, skv, mkv, rkv, 1.0)
        return rope_fwd(qv, gqv, cqv, sqv, mqv, rqv, AT_D ** -0.5), _dup_heads(kr), _dup_heads(vv)

    q_raw, k_raw, v_raw = _cw(z, 5, AT_W), _cw(z, ZC_AK, AT_KW), _cw(z, ZC_AV, AT_KW)
    q16, kd, vd = _rowwise("attn_prep", attn_prep, [q_raw, k_raw, v_raw, cq, sq, ck, sk], [gq, gk, mq, rq, mk, rk],
                           [(AT_W, BF16), (2 * LANE, BF16), (2 * LANE, BF16)])
    y_at, lse = _attn_fwd("attn_fwd", q16, kd, vd)

    p0 = _mm("mix_branch", y_s5, w['w_branch_s5'], out_dtype=BF16)
    p1 = _mm("mix_branch", y_gla, w['w_branch_gla'], out_dtype=BF16)
    p2 = _mm("mix_branch_f32", y_at, w['w_branch_attn'], out_dtype=BF16)

    def merge(gpv, a0, a1, a2, bias):
        g = _sigmoid(gpv.astype(F32) + bias)
        return (g[:, :D_MODEL] * a0.astype(F32) + g[:, D_MODEL:2 * D_MODEL] * a1.astype(F32)
                + g[:, 2 * D_MODEL:] * a2.astype(F32))

    merged = _rowwise("mix_merge", merge, [gp, p0, p1, p2], [w['b_merge_gate']], [(D_MODEL, BF16)])[0]
    xo = _mm("mix_out", merged, w['w_out'], add=x)
    saved = dict(x=x, h=h, z=z, gp=gp, xs=xs, y0=y0, t_glu=t_glu, y_s5=y_s5, la=(la_f, la_b), o=(o_f, o_b), s=(s_f, s_b),
                 y_gla=y_gla, q16=q16, kd=kd, vd=vd, lse=lse, y_at=y_at, p=(p0, p1, p2), merged=merged)
    return xo, saved


def _mixer_bwd(dy, dy16, sv, w, s5_raw):
    z, h = sv['z'], sv['h']
    n_tok = z.shape[0]
    grads = {}
    dmerged = _mm("mix_dmerged", dy16, w['w_out'], tb=True, out_dtype=BF16)
    grads['w_out'] = _mm("mix_dwout", sv['merged'], dy16, ta=True, out_dtype=GRAD_DT)

    def dmerge(dm, gpv, a0, a1, a2, bias):
        dm = dm.astype(F32)
        g = _sigmoid(gpv.astype(F32) + bias)
        ps = (a0.astype(F32), a1.astype(F32), a2.astype(F32))
        dps, dgs = [], []
        for k in range(3):
            gk = g[:, k * D_MODEL:(k + 1) * D_MODEL]
            dps.append(dm * gk)
            dgs.append(dm * ps[k] * gk * (1.0 - gk))
        dgp = jnp.concatenate(dgs, axis=1)
        return dps[0], dps[1], dps[2], dgp, jnp.sum(dgp, axis=0, keepdims=True)

    dp0, dp1, dp2, dgp, dbm = _rowwise("mix_dmerge", dmerge, [dmerged, sv['gp'], *sv['p']], [w['b_merge_gate']],
                                       [(D_MODEL, BF16)] * 3 + [(3 * D_MODEL, BF16)], accs=[(1, 3 * D_MODEL)])
    grads['b_merge_gate'] = dbm[0]
    grads['w_branch_s5'] = _mm("mix_dbranch", sv['y_s5'], dp0, ta=True, out_dtype=GRAD_DT)
    grads['w_branch_gla'] = _mm("mix_dbranch", sv['y_gla'], dp1, ta=True, out_dtype=GRAD_DT)
    grads['w_branch_attn'] = _mm("mix_dbranch_f32", sv['y_at'], dp2, ta=True, out_dtype=GRAD_DT)
    dy_s5 = _mm("mix_dy", dp0, w['w_branch_s5'], tb=True)
    dy_gla = _mm("mix_dy", dp1, w['w_branch_gla'], tb=True)
    dy_at = _mm("mix_dy", dp2, w['w_branch_attn'], tb=True)
    grads['w_merge_gate'] = _mm("mix_dwgate", h, dgp, ta=True, out_dtype=GRAD_DT)
    u = _cw(z, 0, S5_W)

    def s5_post_bwd(dyv, y0, t, uv, dv, wglu):
        sg = _sigmoid(t)
        y1 = _gelu(y0)
        dt = dyv * y1 * sg * (1.0 - sg)
        dy1 = dyv * sg + _dot(dt.astype(BF16), wglu, _NT)
        dy0 = dy1 * _gelu_grad(y0)
        return dy0, dt, y1, dy0 * dv, jnp.sum(dy0 * uv, axis=0, keepdims=True)

    dy0, dt_glu, y1, du, dd = _rowwise("s5_post_bwd", s5_post_bwd, [dy_s5, sv['y0'], sv['t_glu'], u],
                                       [w['s5_d'], w['s5_w_glu']],
                                       [(S5_W, BF16), (S5_W, BF16), (S5_W, BF16), (S5_W, F32)], accs=[(1, S5_W)])
    grads['s5_d'] = dd[0]
    grads['s5_w_glu'] = _mm("s5_dwglu", y1, dt_glu, ta=True, out_dtype=GRAD_DT)
    s5_cot = []
    for d, rev in ((0, False), (1, True)):
        a_re, a_im, wb, wc = w['s5'][d]
        du, dwb, dwc, da = _s5_bwd("s5_adj_fwd" if rev else "s5_adj_rev", dy0, z, sv['xs'][d], a_re, -a_im, wb, wc, du,
                                   rev=not rev)
        da = jnp.sum(da, axis=0, keepdims=True)
        s5_cot.append((da[:, :S5_N], da[:, S5_N:], dwb, dwc))
    for d in range(2):
        _, pull = jax.vjp(_s5_prep, *s5_raw[d])
        s5_cot[d] = pull(s5_cot[d])
    for k, nm in enumerate(('s5_lambda_re', 's5_lambda_im', 's5_log_dt', 's5_b_re', 's5_b_im', 's5_c_re', 's5_c_im')):
        grads[nm] = jnp.stack([s5_cot[0][k], s5_cot[1][k]])

    gate = _cw(z, 4, GLA_W)

    def gla_post_bwd(dyv, of, ob, gv, gn):
        o, gv = of + ob, gv.astype(F32)
        dos, dgates = [], []
        dgn = jnp.zeros((1, GLA_D), F32)
        for hh in range(GLA_H):
            sl = slice(hh * GLA_D, (hh + 1) * GLA_D)
            gh, dyh = gv[:, sl], dyv[:, sl]
            sg = _sigmoid(gh)
            dgates.append(dyh * _rms(o[:, sl], gn) * sg * (1.0 + gh * (1.0 - sg)))
            dx, dg = _rms_bwd(o[:, sl], gn, dyh * gh * sg)
            dos.append(dx)
            dgn = dgn + dg
        return jnp.concatenate(dos, axis=1), jnp.concatenate(dgates, axis=1), dgn

    do, dgate, dgn = _rowwise("gla_post_bwd", gla_post_bwd, [dy_gla, *sv['o'], gate], [w['gla_norm']],
                              [(GLA_W, F32), (GLA_W, F32)], accs=[(1, GLA_D)])
    grads['gla_norm'] = dgn[0]
    dq_f, dk_f, dv_f, dla_f = _gla_bwd("gla_bwd_fwd", z, sv['la'][0], sv['s'][0], do, rev=False)
    dq_b, dk_b, dv_b, dla_b = _gla_bwd("gla_bwd_rev", z, sv['la'][1], sv['s'][1], do, rev=True)
    wa_f, wa_b = _alpha_pads(w['gla_w_alpha'])
    ba_f, ba_b = w['gla_b_alpha'][0:1], w['gla_b_alpha'][1:2]
    zz = _cw(z, ZC_Z, LANE)

    def gla_prep_bwd(dlf, dlb, zv, waf, wab, bf, bb):
        z16 = zv.astype(BF16)
        waf16, wab16 = waf.astype(BF16), wab.astype(BF16)
        xf = _dot(z16, waf16) + bf
        xb = _dot(z16, wab16) + bb
        df = dlf * (_sigmoid(-xf) / GLA_TAU)
        db = dlb * (_sigmoid(-xb) / GLA_TAU)
        dz = _dot(df.astype(BF16), waf16, _NT) + _dot(db.astype(BF16), wab16, _NT)
        return df, db, dz, jnp.sum(df, axis=0, keepdims=True), jnp.sum(db, axis=0, keepdims=True)

    dlog_f, dlog_b, dzz, dba_f, dba_b = _rowwise(
        "gla_prep_bwd", gla_prep_bwd, [dla_f, dla_b, zz], [wa_f, wa_b, ba_f, ba_b],
        [(GLA_W, BF16), (GLA_W, BF16), (LANE, F32)], accs=[(1, GLA_W), (1, GLA_W)])
    dwa_f = _mm("gla_dwa", zz, dlog_f, ta=True)
    dwa_b = _mm("gla_dwa", zz, dlog_b, ta=True)
    grads['gla_w_alpha'] = jnp.stack([dwa_f[:GLA_R], dwa_b[GLA_R:2 * GLA_R]])
    grads['gla_b_alpha'] = jnp.concatenate([dba_f, dba_b], axis=0)

    dq_r, dkd, dvd = _attn_bwd("attn_bwd", sv['q16'], sv['kd'], sv['vd'], sv['y_at'], sv['lse'], dy_at)
    cq, sq, mq, rq = _head_consts(n_tok, AT_QH)
    ck, sk, mk, rk = _head_consts(n_tok, AT_KH)
    gq = jnp.tile(w['attn_q_norm'], (1, AT_QH))
    gk = jnp.tile(w['attn_k_norm'], (1, AT_KH))

    def rope_bwd(dr, v, gain, cos, sin, mean, rot, scale):
        dr = dr * scale
        dvn = dr * cos + _dot_exact(dr * sin, rot, _NT, terms=2)
        r = lax.rsqrt(_dot_exact(v * v, mean, terms=2) + EPS)
        vh = v * r
        dg = jnp.sum(dvn * vh, axis=0, keepdims=True)
        dvh = dvn * gain
        return r * (dvh - vh * _dot_exact(dvh * vh, mean, terms=2)), dg

    def attn_prep_bwd(dq, dk2, dv2, qv, kv, cqv, sqv, ckv, skv, gqv, gkv, mqv, rqv, mkv, rkv):
        qv, kv = qv.astype(F32), kv.astype(F32)
        dqo, dgq = rope_bwd(dq, qv, gqv, cqv, sqv, mqv, rqv, AT_D ** -0.5)
        dko, dgk = rope_bwd(_fold_heads(dk2), kv, gkv, ckv, skv, mkv, rkv, 1.0)
        return dqo, dko, _fold_heads(dv2), dgq, dgk

    q_raw, k_raw = _cw(z, 5, AT_W), _cw(z, ZC_AK, AT_KW)
    daq, dak, dav, dgq, dgk = _rowwise("attn_prep_bwd", attn_prep_bwd, [dq_r, dkd, dvd, q_raw, k_raw, cq, sq, ck, sk],
                                       [gq, gk, mq, rq, mk, rk], [(AT_W, F32), (AT_KW, F32), (AT_KW, F32)],
                                       accs=[(1, AT_W), (1, AT_KW)])
    grads['attn_q_norm'] = dgq.reshape(AT_QH, AT_D).sum(axis=0)
    grads['attn_k_norm'] = dgk.reshape(AT_KH, AT_D).sum(axis=0)

    def assemble(duv, qf, qb, kf, kb, vf, vb, dg, aq, ak, av, zzv):
        return jnp.concatenate([duv, qf + qb, kf + kb, vf + vb, dg, aq, ak, av, zzv], axis=1)

    dz = _rowwise("mix_dz", assemble, [du, dq_f, dq_b, dk_f, dk_b, dv_f, dv_b, dgate, daq, dak, dav, dzz], [],
                  [(ZP_W, BF16)])[0]
    grads['w_in'] = _unpad_w_in(_mm("mix_dwin", dz, h, ta=True, out_dtype=GRAD_DT))
    wmg, wip = w['w_merge_gate'], w['w_in_p']
    tr = _tile(n_tok, 256, 16)

    def dh_norm(dgp_ref, dz_ref, wmg_ref, wip_ref, dy_ref, x_ref, g_ref, dx_ref, dx16_ref, dg_ref):
        dh = _dot(dgp_ref[...], wmg_ref[...], _NT) + _dot(dz_ref[...], wip_ref[...])
        dxv, dgv = _rms_bwd(x_ref[...], g_ref[...], dh)
        dxv = dy_ref[...] + dxv
        dx_ref[...] = dxv
        dx16_ref[...] = dxv.astype(BF16)

        @pl.when(pl.program_id(0) == 0)
        def _():
            dg_ref[...] = jnp.zeros(dg_ref.shape, F32)

        dg_ref[...] += dgv

    def rows(c):
        return pl.BlockSpec((tr, c), lambda i: (i, 0))

    def whole(t):
        return pl.BlockSpec(t.shape, lambda i: (0, 0))

    dx, dx16, dg = _pcall(
        dh_norm, name="mix_dh_norm", grid=(n_tok // tr,),
        in_specs=[rows(dgp.shape[1]), rows(dz.shape[1]), whole(wmg), whole(wip), rows(D_MODEL), rows(D_MODEL),
                  whole(w['mix_norm'])],
        out_specs=[rows(D_MODEL), rows(D_MODEL), whole(w['mix_norm'])],
        out_shape=[jax.ShapeDtypeStruct((n_tok, D_MODEL), F32), jax.ShapeDtypeStruct((n_tok, D_MODEL), BF16),
                   jax.ShapeDtypeStruct((1, D_MODEL), F32)],
        compiler_params=_cparams(("arbitrary",), 4 * (wmg.size + wip.size) + 8 * tr * (dgp.shape[1] + dz.shape[1])
                                 + 16 * tr * D_MODEL * 4),
    )(dgp, dz, wmg, wip, dy, sv['x'], w['mix_norm'])
    grads['mix_norm'] = dg[0]
    return dx, dx16, grads


FFN_NAMES = [p + k for p in ('ffn1_', 'ffn2_') for k in ('w_gate', 'w_up', 'w_down')]
MISC_SHARDED = [n for n in SHARDED if n not in FFN_NAMES]
LARGE_SHARDED = FFN_NAMES + ['w_in', 'w_merge_gate', 'w_out', 'w_branch_s5', 'w_branch_gla', 'w_branch_attn', 's5_w_glu']
KEPT_TRANSPOSED = [p + k for p in ('ffn1_', 'ffn2_') for k in ('w_gate', 'w_up')] + ['w_in']


def _stored_axis(n):
    return 1 if n in KEPT_TRANSPOSED else SHARD_AXIS[n]


def _stored(n, t):
    return jnp.swapaxes(t, 1, 2) if n in KEPT_TRANSPOSED else t


def _layer_weights(ws, rep, i):
    w = {k: _from_shards(ws[k][:, i], _stored_axis(k) - 1) for k in MISC_SHARDED}
    for k in ('ffn1_norm', 'mix_norm', 'ffn2_norm', 's5_d', 'gla_norm', 'attn_q_norm', 'attn_k_norm', 'b_merge_gate'):
        w[k] = rep[k][i].reshape(1, -1)
    w['w_in_p'] = _pad_w_in(w['w_in'])
    s5_raw = [tuple(rep[k][i, d] for k in ('s5_lambda_re', 's5_lambda_im', 's5_log_dt', 's5_b_re', 's5_b_im',
                                            's5_c_re', 's5_c_im')) for d in range(2)]
    w['s5'] = [_s5_prep(*s5_raw[d]) for d in range(2)]
    return w, s5_raw


def _device_step(x, target, ws, rep):
    lw = [_layer_weights(ws, rep, i) for i in range(DEPTH)]
    ffn = {p: (ws[p + 'w_gate'], ws[p + 'w_up'], ws[p + 'w_down']) for p in ('ffn1_', 'ffn2_')}
    saved = []
    for i in range(DEPTH):
        w, _ = lw[i]
        x, s1 = _ffn_fwd(x, w['ffn1_norm'], *ffn['ffn1_'], i)
        x, s2 = _mixer_fwd(x, w)
        x, s3 = _ffn_fwd(x, w['ffn2_norm'], *ffn['ffn2_'], i)
        saved.append((s1, s2, s3))
    gfin = rep['final_norm'].reshape(1, -1)

    def head(xv, tv, gv):
        e = _rms(xv, gv) - tv
        dx, dg = _rms_bwd(xv, gv, e * (1.0 / D_MODEL))
        part = jnp.sum(e * e, axis=0, keepdims=True)
        return dx, dx, dg, part

    dx, dx16, dgfin, part = _rowwise("loss_head", head, [x, target], [gfin], [(D_MODEL, F32), (D_MODEL, BF16)],
                                     accs=[(1, D_MODEL), (1, D_MODEL)])
    loss = (0.5 / D_MODEL) * jnp.sum(part)
    per_layer = []
    dffn = {'ffn1_': None, 'ffn2_': None}
    for i in reversed(range(DEPTH)):
        w, s5_raw = lw[i]
        s1, s2, s3 = saved[i]
        dx, dx16, n3, dffn['ffn2_'] = _ffn_bwd(dx, dx16, s3, w['ffn2_norm'], *ffn['ffn2_'], i, dffn['ffn2_'])
        dx, dx16, g = _mixer_bwd(dx, dx16, s2, w, s5_raw)
        dx, dx16, n1, dffn['ffn1_'] = _ffn_bwd(dx, dx16, s1, w['ffn1_norm'], *ffn['ffn1_'], i, dffn['ffn1_'])
        g['ffn2_norm'], g['ffn1_norm'] = n3, n1
        per_layer.append(g)
    per_layer.reverse()
    stacked = {k: jnp.stack([per_layer[i][k] for i in range(DEPTH)]) for k in per_layer[0]}
    gs = {k: _to_shards(stacked[k], _stored_axis(k)) for k in MISC_SHARDED}
    for p in ('ffn1_', 'ffn2_'):
        gs[p + 'w_gate'], gs[p + 'w_up'], gs[p + 'w_down'] = dffn[p]
    grep = {k: stacked[k] for k in REPLICATED if k != 'final_norm'}
    grep['final_norm'] = dgfin[0]
    return loss, dx, gs, grep


PACK_C = 1024
ROW_TILE = 512


def _rows_of(shape):
    return -(-int(np.prod(shape)) // PACK_C)


def _as_rows(t, lead=0):
    head = t.shape[:lead]
    flat = t.reshape(head + (-1,))
    rows = -(-flat.shape[-1] // PACK_C)
    pad = rows * PACK_C - flat.shape[-1]
    if pad:
        flat = jnp.pad(flat, [(0, 0)] * lead + [(0, pad)])
    return flat.reshape(head + (rows, PACK_C))


def _pack(items, total_rows, lead=0):
    parts = [_as_rows(t, lead) for t in items]
    used = sum(p.shape[lead] for p in parts)
    if total_rows > used:
        parts.append(jnp.zeros(parts[0].shape[:lead] + (total_rows - used, PACK_C), parts[0].dtype))
    return jnp.concatenate(parts, axis=lead)


def _unpack(rows, shapes, lead=0):
    out, r0 = [], 0
    head = rows.shape[:lead]
    for shp in shapes:
        n, size = _rows_of(shp), int(np.prod(shp))
        piece = lax.slice_in_dim(rows, r0, r0 + n, axis=lead).reshape(head + (n * PACK_C,))
        out.append(lax.slice_in_dim(piece, 0, size, axis=lead).reshape(head + tuple(shp)))
        r0 += n
    return out


def _round_up(n, m):
    return -(-n // m) * m


def _to_shards(g, axis):
    shp = g.shape
    g = g.reshape(shp[:axis] + (N_CHIPS, shp[axis] // N_CHIPS) + shp[axis + 1:])
    return jnp.moveaxis(g, axis, 0)


def _from_shards(s, axis):
    s = jnp.moveaxis(s, 0, axis)
    shp = s.shape
    return s.reshape(shp[:axis] + (shp[axis] * shp[axis + 1],) + shp[axis + 2:])


_HBM = pl.BlockSpec(memory_space=pltpu.HBM)


def _place():
    x, y, c = lax.axis_index("x"), lax.axis_index("y"), lax.axis_index("c")
    chips = [(1 - x, y), (x, 1 - y), (1 - x, 1 - y)]
    return x, y, c, chips


def _rcopy(src, dst, send, recv, k, dev):
    return pltpu.make_async_remote_copy(src_ref=src, dst_ref=dst, send_sem=send.at[k], recv_sem=recv.at[k],
                                        device_id=dev, device_id_type=MESH)


def _comm_call(name, body, arrays, out_shapes, n_sems, n_local):
    return pl.pallas_call(
        body, name=name, in_specs=[_HBM] * len(arrays), out_specs=[_HBM] * len(out_shapes), out_shape=out_shapes,
        scratch_shapes=[pltpu.SemaphoreType.DMA((n_sems,)), pltpu.SemaphoreType.DMA((n_sems,)),
                        pltpu.SemaphoreType.DMA((max(n_local, 1),))],
    )(*arrays)


def _ici_peers(x, y, c):
    x_first = c == 0
    first = (jnp.where(x_first, 1 - x, x), jnp.where(x_first, y, 1 - y))
    second = (jnp.where(x_first, x, 1 - x), jnp.where(x_first, 1 - y, y))
    return first, second


def _ag_chips(name, packs):
    n = len(packs)

    def body(*refs):
        p, o, (send, recv, _) = refs[:n], refs[n:2 * n], refs[2 * n:]
        x, y, c, _ = _place()
        (ax, ay), (bx, by) = _ici_peers(x, y, c)
        me, na, nb, nd = 2 * x + y, 2 * ax + ay, 2 * bx + by, 2 * (1 - x) + (1 - y)
        sib = (x, y, 1 - c)
        own = [_rcopy(p[a].at[c], o[a].at[me, c], send, recv, 6 * a, (ax, ay, c)) for a in range(n)]
        own += [_rcopy(p[a].at[c], o[a].at[me, c], send, recv, 6 * a + 1, (bx, by, c)) for a in range(n)]
        for cp in own:
            cp.start()
        passed = []

        def landed(a, chip, k, fwd):
            blk = o[a].at[chip, c]
            _rcopy(blk, blk, send, recv, 6 * a + k, sib).wait_recv()
            for sem, dev in fwd:
                cp = _rcopy(blk, blk, send, recv, 6 * a + sem, dev)
                cp.start()
                passed.append(cp)

        for a in range(n):
            landed(a, na, 0, [(2, (bx, by, c)), (3, sib)])
        for a in range(n):
            landed(a, nb, 1, [(4, sib)])
        for a in range(n):
            landed(a, nd, 2, [(5, sib)])
        for a in range(n):
            for k, chip in ((3, nb), (4, na), (5, nd)):
                blk = o[a].at[chip, 1 - c]
                _rcopy(blk, blk, send, recv, 6 * a + k, sib).wait_recv()
        for cp in own + passed:
            cp.wait_send()

    outs = _comm_call(name, body, packs, [jax.ShapeDtypeStruct((N_CHIPS,) + t.shape, t.dtype) for t in packs], 6 * n, 0)
    chip = 2 * lax.axis_index("x") + lax.axis_index("y")
    return [lax.dynamic_update_slice_in_dim(o, t[None], chip, axis=0) for o, t in zip(outs, packs)]


def _pair_swap(name, gs):
    n = len(gs)
    ns = gs[0].shape[0]

    def body(*refs):
        g, r, (send, recv, _) = refs[:n], refs[n:2 * n], refs[2 * n:]
        x, y, c, _ = _place()
        cps = [_rcopy(g[a].at[j, 1 - c], r[a].at[j], send, recv, ns * a + j, (x, y, 1 - c))
               for a in range(n) for j in range(ns)]
        for cp in cps:
            cp.start()
        for cp in cps:
            cp.wait_recv()
        for cp in cps:
            cp.wait_send()

    return _comm_call(name, body, gs, [jax.ShapeDtypeStruct((ns,) + t.shape[2:], t.dtype) for t in gs], ns * n, 0)


def _halve_first(name, parts):
    n = len(parts)

    def body(*refs):
        p, r, (send, recv, _) = refs[:n], refs[n:2 * n], refs[2 * n:]
        x, y, c, _ = _place()
        (ax, ay), _ = _ici_peers(x, y, c)
        cps = [_rcopy(p[a].at[jnp.where(c == 0, 2 * ax + k, 2 * k + ay)], r[a].at[k], send, recv, 2 * a + k, (ax, ay, c))
               for a in range(n) for k in range(2)]
        for cp in cps:
            cp.start()
        for cp in cps:
            cp.wait_recv()
        for cp in cps:
            cp.wait_send()

    return _comm_call(name, body, parts, [jax.ShapeDtypeStruct((2,) + t.shape[1:], t.dtype) for t in parts], 2 * n, 0)


def _halve_second(name, kept):
    n = len(kept)

    def body(*refs):
        p, r, (send, recv, _) = refs[:n], refs[n:2 * n], refs[2 * n:]
        x, y, c, _ = _place()
        _, (bx, by) = _ici_peers(x, y, c)
        cps = [_rcopy(p[a].at[jnp.where(c == 0, by, bx)], r[a], send, recv, a, (bx, by, c)) for a in range(n)]
        for cp in cps:
            cp.start()
        for cp in cps:
            cp.wait_recv()
        for cp in cps:
            cp.wait_send()

    return _comm_call(name, body, kept, [jax.ShapeDtypeStruct(t.shape[1:], t.dtype) for t in kept], n, 0)


def _add_selected(name, t, r, index, out_dtype):
    single = r.ndim == 2
    r3 = r[None] if single else r
    nk, rh, cc = r3.shape
    tl = _tile(rh, ROW_TILE, 16)

    def body(i_ref, t_ref, r_ref, o_ref):
        o_ref[...] = (t_ref[...].astype(F32) + r_ref[...].astype(F32)).astype(o_ref.dtype)

    out = _pcall(
        body, name=name,
        grid_spec=pltpu.PrefetchScalarGridSpec(
            num_scalar_prefetch=1, grid=(nk, rh // tl),
            in_specs=[pl.BlockSpec((1, tl, cc), lambda k, i, i_ref: (i_ref[k], i, 0)),
                      pl.BlockSpec((1, tl, cc), lambda k, i, i_ref: (k, i, 0))],
            out_specs=pl.BlockSpec((1, tl, cc), lambda k, i, i_ref: (k, i, 0))),
        out_shape=jax.ShapeDtypeStruct(r3.shape, out_dtype),
        compiler_params=_cparams(("parallel", "parallel"), 8 * tl * cc * 4),
    )(index, t, r3)
    return out[0] if single else out


def _pair_share(name, reds):
    n = len(reds)

    def body(*refs):
        r, o, (send, recv, _) = refs[:n], refs[n:2 * n], refs[2 * n:]
        x, y, c, _ = _place()
        cps = [_rcopy(r[a], o[a].at[c], send, recv, a, (x, y, 1 - c)) for a in range(n)]
        for cp in cps:
            cp.start()
        for a in range(n):
            blk = o[a].at[1 - c]
            _rcopy(blk, blk, send, recv, a, (x, y, 1 - c)).wait_recv()
        for cp in cps:
            cp.wait_send()

    outs = _comm_call(name, body, reds, [jax.ShapeDtypeStruct((2,) + t.shape, t.dtype) for t in reds], n, 0)
    core = lax.axis_index("c")
    return [lax.dynamic_update_slice_in_dim(o, t[None], core, axis=0) for o, t in zip(outs, reds)]


def _add_own_half(name, g, r):
    ns, _, rh, cc = g.shape
    tl = _tile(rh, ROW_TILE, 16)
    core = lax.axis_index("c").astype(jnp.int32).reshape(1)

    def body(c_ref, g_ref, r_ref, o_ref):
        o_ref[...] = (g_ref[0].astype(F32) + r_ref[...].astype(F32)).astype(o_ref.dtype)

    return _pcall(
        body, name=name,
        grid_spec=pltpu.PrefetchScalarGridSpec(
            num_scalar_prefetch=1, grid=(ns, rh // tl),
            in_specs=[pl.BlockSpec((1, 1, tl, cc), lambda j, i, c_ref: (j, c_ref[0], i, 0)),
                      pl.BlockSpec((1, tl, cc), lambda j, i, c_ref: (j, i, 0))],
            out_specs=pl.BlockSpec((1, tl, cc), lambda j, i, c_ref: (j, i, 0))),
        out_shape=jax.ShapeDtypeStruct(r.shape, BF16),
        compiler_params=_cparams(("parallel", "parallel"), 8 * tl * cc * 4),
    )(core, g, r)


def _reduce_scatter(gs):
    x, y, c = lax.axis_index("x"), lax.axis_index("y"), lax.axis_index("c")
    mine_first = jnp.stack([jnp.where(c == 0, 2 * x + k, 2 * k + y) for k in range(2)]).astype(jnp.int32)
    mine_second = jnp.where(c == 0, y, x).astype(jnp.int32).reshape(1)
    swapped = _pair_swap("rs_pair_swap", gs)
    parts = [_add_own_half("rs_add_pair", g, r) for g, r in zip(gs, swapped)]
    got = _halve_first("rs_halve_first", parts)
    kept = [_add_selected("rs_add_first", t, r, mine_first, BF16) for t, r in zip(parts, got)]
    got = _halve_second("rs_halve_second", kept)
    reds = [_add_selected("rs_add_second", t, r, mine_second, F32) for t, r in zip(kept, got)]
    return _pair_share("rs_pair_share", reds)


def _adamw(name, g, w, m, v):
    c1 = 1.0 - ADAM_B1 ** ADAM_STEP
    c2 = 1.0 - ADAM_B2 ** ADAM_STEP
    nb, rows, cc = g.shape
    tl = _tile(rows, ROW_TILE, SUBLANE)

    def body(g_ref, w_ref, m_ref, v_ref, d_ref, mo_ref, vo_ref):
        gv = g_ref[...]
        mn = ADAM_B1 * m_ref[...] + (1.0 - ADAM_B1) * gv
        vn = ADAM_B2 * v_ref[...] + (1.0 - ADAM_B2) * (gv * gv)
        d_ref[...] = -ADAM_LR * ((mn / c1) / (jnp.sqrt(vn / c2) + ADAM_EPS) + ADAM_WD * w_ref[...])
        mo_ref[...] = mn
        vo_ref[...] = vn

    spec = pl.BlockSpec((1, tl, cc), lambda b, i: (b, i, 0))
    return _pcall(
        body, name=name, grid=(nb, rows // tl), in_specs=[spec] * 4, out_specs=[spec] * 3,
        out_shape=[jax.ShapeDtypeStruct(g.shape, F32)] * 3,
        compiler_params=_cparams(("parallel", "parallel"), 16 * tl * cc * 4),
    )(g, w, m, v)


def _train_step(a):
    x, target = a['x'][0], a['loss_target'][0]

    names = BF16_GATHER + F32_GATHER
    got = _ag_chips("ag_weights", [_stored(n, a[n]).astype(BF16) for n in BF16_GATHER] + [a[n] for n in F32_GATHER])
    ws = dict(zip(names, got))

    loss, gx, gs, grep = _device_step(x, target, ws, {n: a[n] for n in REPLICATED})
    loss = lax.psum(loss, ("x", "y", "c"))

    small = [n for n in MISC_SHARDED if n not in LARGE_SHARDED]
    small_shapes = [a[n].shape[1:] for n in small]
    rep_shapes = [a[n].shape for n in REPLICATED]
    small_rows = sum(_rows_of(s) for s in small_shapes)
    piece = _round_up(-(-sum(_rows_of(s) for s in rep_shapes) // (2 * N_CHIPS)), SUBLANE)
    rep_rows = 2 * N_CHIPS * piece
    rep_pack = _pack([grep[n] for n in REPLICATED], rep_rows).reshape(N_CHIPS, 2, piece, PACK_C)
    total = _round_up(small_rows + piece, ROW_TILE // 2)
    misc = _pack([gs[n] for n in small] + [rep_pack], total, lead=2)
    reds = _reduce_scatter([gs[n] for n in LARGE_SHARDED] + [misc])
    grad = dict(zip(LARGE_SHARDED, reds[:-1]))
    grad.update(zip(small, _unpack(reds[-1], small_shapes, lead=1)))
    rep_mine = reds[-1][:, small_rows:small_rows + piece]
    rep_all = _ag_chips("ag_replicated", [rep_mine])[0].reshape(1, rep_rows, PACK_C)

    outs = {}
    for n in SHARDED:
        shp = grad[n].shape
        three_d = (shp[0], -1, shp[-1])
        params = (grad[n], *(_stored(n, a[p + n]) for p in ('', 'm_', 'v_')))
        res3 = _adamw("adamw", *(t.reshape(three_d) for t in params))
        for kind, arr in zip(('grad', 'delta', 'new_m', 'new_v'), (grad[n], *res3)):
            outs[kind + '_' + n] = _stored(n, arr.reshape(shp))
    w, m, v = (_pack([a[p + n] for n in REPLICATED], rep_rows)[None] for p in ('', 'm_', 'v_'))
    res3 = _adamw("adamw_replicated", rep_all, w, m, v)
    for kind, arr in zip(('grad', 'delta', 'new_m', 'new_v'), (rep_all, *res3)):
        for n, t in zip(REPLICATED, _unpack(arr[0], rep_shapes)):
            outs[kind + '_' + n] = t
    res = [loss, gx[None]]
    for kind in ('grad', 'delta', 'new_m', 'new_v'):
        res += [outs[kind + '_' + n] for n in W_NAMES]
    return tuple(res)


def kernel(x, ffn1_norm, ffn1_w_gate, ffn1_w_up, ffn1_w_down, mix_norm, w_in, s5_lambda_re, s5_lambda_im, s5_log_dt, s5_b_re, s5_b_im, s5_c_re, s5_c_im, s5_d, s5_w_glu, gla_w_alpha, gla_b_alpha, gla_norm, attn_q_norm, attn_k_norm, w_branch_s5, w_branch_gla, w_branch_attn, w_merge_gate, b_merge_gate, w_out, ffn2_norm, ffn2_w_gate, ffn2_w_up, ffn2_w_down, final_norm, loss_target, m_ffn1_norm, m_ffn1_w_gate, m_ffn1_w_up, m_ffn1_w_down, m_mix_norm, m_w_in, m_s5_lambda_re, m_s5_lambda_im, m_s5_log_dt, m_s5_b_re, m_s5_b_im, m_s5_c_re, m_s5_c_im, m_s5_d, m_s5_w_glu, m_gla_w_alpha, m_gla_b_alpha, m_gla_norm, m_attn_q_norm, m_attn_k_norm, m_w_branch_s5, m_w_branch_gla, m_w_branch_attn, m_w_merge_gate, m_b_merge_gate, m_w_out, m_ffn2_norm, m_ffn2_w_gate, m_ffn2_w_up, m_ffn2_w_down, m_final_norm, v_ffn1_norm, v_ffn1_w_gate, v_ffn1_w_up, v_ffn1_w_down, v_mix_norm, v_w_in, v_s5_lambda_re, v_s5_lambda_im, v_s5_log_dt, v_s5_b_re, v_s5_b_im, v_s5_c_re, v_s5_c_im, v_s5_d, v_s5_w_glu, v_gla_w_alpha, v_gla_b_alpha, v_gla_norm, v_attn_q_norm, v_attn_k_norm, v_w_branch_s5, v_w_branch_gla, v_w_branch_attn, v_w_merge_gate, v_b_merge_gate, v_w_out, v_ffn2_norm, v_ffn2_w_gate, v_ffn2_w_up, v_ffn2_w_down, v_final_norm):
    return _train_step(dict(locals()))
```

```python
import functools
import math

import jax
import jax.numpy as jnp
import numpy as np
from jax import lax
from jax.experimental import pallas as pl
from jax.experimental.pallas import tpu as pltpu

F32 = jnp.float32
BF16 = jnp.bfloat16

D_MODEL = 1024
D_FF = 2816
DEPTH = 2
EPS = 1e-6
S5_G, S5_H, S5_P = 32, 16, 64
S5_W = S5_G * S5_H
S5_N = S5_G * S5_P
S5_TG = 8
S5_TILES = S5_G // S5_TG
S5_TC, S5_TS = S5_TG * S5_H, S5_TG * S5_P
GLA_H, GLA_D = 4, 128
GLA_W = GLA_H * GLA_D
GLA_R = 16
GLA_TAU = 16.0
GLA_C = 64
AT_QH, AT_KH, AT_D = 8, 2, 64
AT_W = AT_QH * AT_D
AT_KW = AT_KH * AT_D
GRID_W = 64
ROPE_BASE = 10000.0
ZP_W = 3456
ZC_AK, ZC_AV, ZC_Z = 24, 25, 26

ADAM_LR, ADAM_B1, ADAM_B2, ADAM_EPS, ADAM_WD, ADAM_STEP = 0.001, 0.9, 0.999, 1e-08, 0.01, 10

GRAD_DT = BF16
VMEM_LIMIT_V7X = 56 * 1024 * 1024
LANE = 128
SUBLANE = 8

W_NAMES = ['ffn1_norm', 'ffn1_w_gate', 'ffn1_w_up', 'ffn1_w_down', 'mix_norm', 'w_in', 's5_lambda_re', 's5_lambda_im',
           's5_log_dt', 's5_b_re', 's5_b_im', 's5_c_re', 's5_c_im', 's5_d', 's5_w_glu', 'gla_w_alpha', 'gla_b_alpha',
           'gla_norm', 'attn_q_norm', 'attn_k_norm', 'w_branch_s5', 'w_branch_gla', 'w_branch_attn', 'w_merge_gate',
           'b_merge_gate', 'w_out', 'ffn2_norm', 'ffn2_w_gate', 'ffn2_w_up', 'ffn2_w_down', 'final_norm']
SHARD_AXIS = {'ffn1_w_gate': 2, 'ffn1_w_up': 2, 'ffn1_w_down': 1, 'w_in': 2, 's5_w_glu': 1, 'gla_w_alpha': 3,
              'gla_b_alpha': 2, 'w_branch_s5': 2, 'w_branch_gla': 2, 'w_branch_attn': 2, 'w_merge_gate': 2,
              'w_out': 1, 'ffn2_w_gate': 2, 'ffn2_w_up': 2, 'ffn2_w_down': 1}
F32_GATHER = ['gla_w_alpha', 'gla_b_alpha']
BF16_GATHER = [n for n in W_NAMES if n in SHARD_AXIS and n not in F32_GATHER]
SHARDED = [n for n in W_NAMES if n in SHARD_AXIS]
REPLICATED = [n for n in W_NAMES if n not in SHARD_AXIS]
N_CHIPS = 4
MESH = pl.DeviceIdType.MESH


def _tile(n, target, mult):
    best = None
    for t in range(mult, min(n, target) + 1, mult):
        if n % t == 0:
            best = t
    return best if best is not None else n


def _cparams(sem, vmem_bytes):
    limit = int(min(VMEM_LIMIT_V7X, max(32 * 1024 * 1024, vmem_bytes * 5 // 4)))
    return pltpu.CompilerParams(dimension_semantics=sem, vmem_limit_bytes=limit)


def _pcall(body, **kw):
    shapes = kw.pop('out_shape')
    many = isinstance(shapes, (list, tuple))
    pinned = [pltpu.HBM(t.shape, t.dtype) for t in (shapes if many else [shapes])]
    call = pl.pallas_call(body, out_shape=pinned if many else pinned[0], **kw)

    def run(*ops):
        return call(*[o if jnp.issubdtype(o.dtype, jnp.integer) else pltpu.with_memory_space_constraint(o, pltpu.HBM)
                      for o in ops])

    return run


def _cw(arr, cb, width):
    return (arr, cb, width)


def _win(a):
    if isinstance(a, tuple):
        return a
    return (a, 0, a.shape[-1])


def _nbytes(shape, dtype):
    return int(np.prod(shape)) * jnp.dtype(dtype).itemsize


def _rowwise(name, fn, rows, fulls, outs, accs=(), tl=256):
    rows = [_win(r) for r in rows]
    n_tok = rows[0][0].shape[0]
    tl = _tile(n_tok, tl, 16)
    nr, nf, no = len(rows), len(fulls), len(outs)

    def body(*refs):
        vals = [r[...] for r in refs[:nr + nf]]
        res = fn(*vals)
        if not isinstance(res, (tuple, list)):
            res = (res,)
        for k in range(no):
            o = refs[nr + nf + k]
            o[...] = res[k].astype(o.dtype)
        if accs:
            @pl.when(pl.program_id(0) == 0)
            def _():
                for k in range(len(accs)):
                    a = refs[nr + nf + no + k]
                    a[...] = jnp.zeros(a.shape, a.dtype)
            for k in range(len(accs)):
                a = refs[nr + nf + no + k]
                a[...] += res[no + k]

    in_specs = [pl.BlockSpec((tl, w), functools.partial(lambda i, cb: (i, cb), cb=cb)) for (_, cb, w) in rows]
    in_specs += [pl.BlockSpec(f.shape, functools.partial(lambda i, nd: (0,) * nd, nd=f.ndim)) for f in fulls]
    out_specs = [pl.BlockSpec((tl, c), lambda i: (i, 0)) for (c, _) in outs]
    out_specs += [pl.BlockSpec((r, c), lambda i: (0, 0)) for (r, c) in accs]
    out_shape = [jax.ShapeDtypeStruct((n_tok, c), dt) for (c, dt) in outs]
    out_shape += [jax.ShapeDtypeStruct((r, c), F32) for (r, c) in accs]
    vm = 2 * sum(_nbytes((tl, w), a.dtype) for (a, _, w) in rows)
    vm += 2 * sum(_nbytes(f.shape, f.dtype) for f in fulls)
    vm += 2 * sum(_nbytes((tl, c), dt) for (c, dt) in outs)
    vm += 8 * max([_nbytes((tl, w), F32) for (_, _, w) in rows] + [_nbytes((tl, c), F32) for (c, _) in outs])
    return _pcall(
        body, name=name, grid=(n_tok // tl,), in_specs=in_specs, out_specs=out_specs, out_shape=out_shape,
        compiler_params=_cparams(("arbitrary",) if accs else ("parallel",), vm),
    )(*[r[0] for r in rows], *fulls)


def _mm(name, a, b, *, ta=False, tb=False, out_dtype=F32, alpha=1.0, add=None, tm=512, tn=512):
    a_arr, a_cb, a_w = _win(a)
    b_arr, b_cb, b_w = _win(b)
    if ta:
        kdim, m = a_arr.shape[0], a_w
    else:
        m, kdim = a_arr.shape[0], a_w
    if tb:
        n, kb = b_arr.shape[0], b_w
    else:
        kb, n = b_arr.shape[0], b_w
    assert kdim == kb, (name, kdim, kb)
    if not ta and kdim <= 1024:
        tm = 2 * tm
    tm = _tile(m, tm, LANE if ta else 16)
    tn = _tile(n, tn, LANE)
    if ta:
        a_spec = pl.BlockSpec((kdim, tm), lambda i, j: (0, a_cb * (a_w // tm) + i))
        a_blk = (kdim, tm)
    else:
        a_spec = pl.BlockSpec((tm, kdim), lambda i, j: (i, a_cb))
        a_blk = (tm, kdim)
    if tb:
        b_spec = pl.BlockSpec((tn, kdim), lambda i, j: (j, b_cb))
        b_blk = (tn, kdim)
    else:
        b_spec = pl.BlockSpec((kdim, tn), lambda i, j: (0, b_cb * (b_w // tn) + j))
        b_blk = (kdim, tn)
    dims = (((0 if ta else 1,), (1 if tb else 0,)), ((), ()))
    has_add = add is not None

    def body(*refs):
        a_ref, b_ref = refs[0], refs[1]
        o_ref = refs[-1]
        acc = lax.dot_general(a_ref[...].astype(BF16), b_ref[...].astype(BF16), dims, preferred_element_type=F32)
        if alpha != 1.0:
            acc = acc * alpha
        if has_add:
            acc = acc + refs[2][...].astype(F32)
        o_ref[...] = acc.astype(o_ref.dtype)

    in_specs = [a_spec, b_spec]
    ops = [a_arr, b_arr]
    if has_add:
        in_specs.append(pl.BlockSpec((tm, tn), lambda i, j: (i, j)))
        ops.append(add)
    vm = 2 * (_nbytes(a_blk, a_arr.dtype) + _nbytes(b_blk, b_arr.dtype) + _nbytes((tm, tn), out_dtype))
    vm += _nbytes(a_blk, BF16) + _nbytes(b_blk, BF16) + 3 * _nbytes((tm, tn), F32)
    return _pcall(
        body, name=name, grid=(m // tm, n // tn), in_specs=in_specs,
        out_specs=pl.BlockSpec((tm, tn), lambda i, j: (i, j)),
        out_shape=jax.ShapeDtypeStruct((m, n), out_dtype),
        compiler_params=_cparams(("parallel", "parallel"), vm),
    )(*ops)


def _sigmoid(x):
    return 0.5 * jnp.tanh(0.5 * x) + 0.5


def _rms(x, gain):
    r = lax.rsqrt(jnp.mean(x * x, axis=-1, keepdims=True) + EPS)
    return x * r * gain


def _rms_bwd(x, gain, dh):
    r = lax.rsqrt(jnp.mean(x * x, axis=-1, keepdims=True) + EPS)
    xh = x * r
    dg = jnp.sum(dh * xh, axis=0, keepdims=True)
    dxh = dh * gain
    dx = r * (dxh - xh * jnp.mean(dxh * xh, axis=-1, keepdims=True))
    return dx, dg


_GELU_C = math.sqrt(2.0 / math.pi)


def _gelu(y):
    return 0.5 * y * (1.0 + jnp.tanh(_GELU_C * (y + 0.044715 * y * y * y)))


def _gelu_grad(y):
    th = jnp.tanh(_GELU_C * (y + 0.044715 * y * y * y))
    return 0.5 * (1.0 + th) + 0.5 * y * (1.0 - th * th) * _GELU_C * (1.0 + 3.0 * 0.044715 * y * y)


def _log_sigmoid(x):
    return jnp.minimum(x, 0.0) - jnp.log(1.0 + jnp.exp(-jnp.abs(x)))


def _dot(a, b, dims=(((1,), (0,)), ((), ())), precision=None):
    return lax.dot_general(a, b, dims, preferred_element_type=F32, precision=precision)


def _dot_exact(x, m, dims=(((1,), (0,)), ((), ())), terms=3, m_first=False):
    m16 = m.astype(BF16)
    acc, rest = None, x
    for k in range(terms):
        piece = rest.astype(BF16)
        t = _dot(m16, piece, dims) if m_first else _dot(piece, m16, dims)
        acc = t if acc is None else acc + t
        if k + 1 < terms:
            rest = rest - piece.astype(F32)
    return acc


_NT = (((1,), (1,)), ((), ()))
_TN = (((0,), (0,)), ((), ()))


def _scan_core(b_ref, are_ref, aim_ref, pw, last, of32, *, rev, xf32=None, da_ref=None, lc=512):
    nb, nc = b_ref.shape[0] // SUBLANE, S5_N // lc
    row = lax.broadcasted_iota(jnp.int32, (SUBLANE, lc), 0)

    def cmul(ar, ai, xr, xi):
        return ar * xr - ai * xi, ar * xi + ai * xr

    def hs(c, xr, xi):
        cs = pl.ds(c * lc, lc)
        for k, slot in ((1, 0), (2, 2), (4, 4)):
            ar, ai = pw[slot, :, cs], pw[slot + 1, :, cs]
            if rev:
                sr, si = pltpu.roll(xr, SUBLANE - k, 0), pltpu.roll(xi, SUBLANE - k, 0)
                keep = row < SUBLANE - k
            else:
                sr, si = pltpu.roll(xr, k, 0), pltpu.roll(xi, k, 0)
                keep = row >= k
            sr, si = jnp.where(keep, sr, 0.0), jnp.where(keep, si, 0.0)
            pr, pi = cmul(ar, ai, sr, si)
            xr, xi = xr + pr, xi + pi
        return xr, xi

    @pl.when(pl.program_id(0) == 0)
    def _():
        last[...] = jnp.zeros(last.shape, F32)
        if da_ref is not None:
            da_ref[...] = jnp.zeros(da_ref.shape, F32)
        for c in range(nc):
            cs = pl.ds(c * lc, lc)
            a1r = jnp.broadcast_to(are_ref[:, cs], (SUBLANE, lc))
            a1i = jnp.broadcast_to(aim_ref[:, cs], (SUBLANE, lc))
            a2r, a2i = cmul(a1r, a1i, a1r, a1i)
            a4r, a4i = cmul(a2r, a2i, a2r, a2i)
            for slot, v in enumerate((a1r, a1i, a2r, a2i, a4r, a4i)):
                pw[slot, :, cs] = v
            first = SUBLANE - 1 if rev else 0
            pr, pi = hs(c, jnp.where(row == first, a1r, 0.0), jnp.where(row == first, a1i, 0.0))
            pw[6, :, cs] = pr
            pw[7, :, cs] = pi

    edge = 0 if rev else SUBLANE - 1

    def blk(j, carry):
        jj = (nb - 1 - j) if rev else j
        r0 = pl.multiple_of(jj * SUBLANE, SUBLANE)
        for c in range(nc):
            cre, cim = pl.ds(c * lc, lc), pl.ds(S5_N + c * lc, lc)
            xr, xi = hs(c, b_ref[pl.ds(r0, SUBLANE), cre], b_ref[pl.ds(r0, SUBLANE), cim])
            cr = jnp.broadcast_to(last[edge:edge + 1, cre], (SUBLANE, lc))
            ci = jnp.broadcast_to(last[edge:edge + 1, cim], (SUBLANE, lc))
            pr, pi = cmul(pw[6, :, cre], pw[7, :, cre], cr, ci)
            xr, xi = xr + pr, xi + pi
            if da_ref is not None:
                if rev:
                    qr, qi = pltpu.roll(xr, SUBLANE - 1, 0), pltpu.roll(xi, SUBLANE - 1, 0)
                    fill = row == SUBLANE - 1
                else:
                    qr, qi = pltpu.roll(xr, 1, 0), pltpu.roll(xi, 1, 0)
                    fill = row == 0
                qr, qi = jnp.where(fill, cr, qr), jnp.where(fill, ci, qi)
                zr, zi = xf32[pl.ds(r0, SUBLANE), cre], xf32[pl.ds(r0, SUBLANE), cim]
                da_ref[:, cre] += zr * qr + zi * qi
                da_ref[:, cim] += zr * qi - zi * qr
            last[:, cre] = xr
            last[:, cim] = xi
            of32[pl.ds(r0, SUBLANE), cre] = xr
            of32[pl.ds(r0, SUBLANE), cim] = xi
        return carry

    lax.fori_loop(0, nb, blk, 0)


def _s5_tiles():
    return [(slice((j % S5_TILES) * S5_TC, (j % S5_TILES + 1) * S5_TC), slice(j * S5_TS, (j + 1) * S5_TS))
            for j in range(2 * S5_TILES)]


def _s5_scratch(tb):
    return [pltpu.VMEM((8, SUBLANE, S5_N), F32), pltpu.VMEM((SUBLANE, 2 * S5_N), F32),
            pltpu.VMEM((tb, 2 * S5_N), F32), pltpu.VMEM((tb, 2 * S5_N), F32),
            pltpu.VMEM((S5_TC, 2 * S5_N), BF16), pltpu.VMEM((2 * S5_N, S5_TC), BF16)]


def _s5_fwd(name, z, a_re, a_im, wb, wc, add, *, rev, tb=512):
    n_tok = z.shape[0]
    tb = _tile(n_tok, tb, 16)
    nt = n_tok // tb
    has_add = add is not None

    def body(*refs):
        u_ref, are_ref, aim_ref, wb_ref, wc_ref = refs[:5]
        x_ref, y_ref, pw, last, of32, bu, wb16, wc16 = refs[5 + has_add:]

        @pl.when(pl.program_id(0) == 0)
        def _():
            wb16[...] = wb_ref[...].astype(BF16)
            wc16[...] = wc_ref[...].astype(BF16)

        u16 = u_ref[...].astype(BF16)
        for ch, st in _s5_tiles():
            bu[:, st] = _dot(u16[:, ch], wb16[:, st])
        _scan_core(bu, are_ref, aim_ref, pw, last, of32, rev=rev)
        x16 = of32[...].astype(BF16)
        x_ref[...] = x16
        ys = []
        for k in range(S5_TILES):
            re, im = slice(k * S5_TS, (k + 1) * S5_TS), slice(S5_N + k * S5_TS, S5_N + (k + 1) * S5_TS)
            ys.append(_dot(x16[:, re], wc16[re, :]) + _dot(x16[:, im], wc16[im, :]))
        y = jnp.concatenate(ys, axis=1)
        y_ref[...] = y + refs[5][...] if has_add else y

    tmap = (lambda t: (nt - 1 - t, 0)) if rev else (lambda t: (t, 0))
    const = lambda t: (0, 0)
    in_specs = [pl.BlockSpec((tb, S5_W), tmap), pl.BlockSpec((1, S5_N), const), pl.BlockSpec((1, S5_N), const),
                pl.BlockSpec((S5_TC, 2 * S5_N), const), pl.BlockSpec((2 * S5_N, S5_TC), const)]
    ops = [z, a_re, a_im, wb, wc]
    if has_add:
        in_specs.append(pl.BlockSpec((tb, S5_W), tmap))
        ops.append(add)
    return _pcall(
        body, name=name, grid=(nt,), in_specs=in_specs,
        out_specs=[pl.BlockSpec((tb, 2 * S5_N), tmap), pl.BlockSpec((tb, S5_W), tmap)],
        out_shape=[jax.ShapeDtypeStruct((n_tok, 2 * S5_N), BF16), jax.ShapeDtypeStruct((n_tok, S5_W), F32)],
        scratch_shapes=_s5_scratch(tb), compiler_params=_cparams(("arbitrary",), 10 * tb * 2 * S5_N * 4),
    )(*ops)


def _s5_bwd(name, dy, z, xs, a_re, a_im, wb, wc, add, *, rev, tb=256):
    n_tok = z.shape[0]
    tb = _tile(n_tok, tb, 16)
    nt = n_tok // tb

    def body(dy_ref, u_ref, xs_ref, are_ref, aim_ref, wb_ref, wc_ref, add_ref, du_ref, dwb_ref, dwc_ref, da_ref,
             pw, last, of32, gx, wb16, wc16, xf32):
        @pl.when(pl.program_id(0) == 0)
        def _():
            wb16[...] = wb_ref[...].astype(BF16)
            wc16[...] = wc_ref[...].astype(BF16)
            dwb_ref[...] = jnp.zeros(dwb_ref.shape, F32)
            dwc_ref[...] = jnp.zeros(dwc_ref.shape, F32)

        dy16, u16, x16 = dy_ref[...].astype(BF16), u_ref[...].astype(BF16), xs_ref[...]
        for ch, st in _s5_tiles():
            gx[:, st] = _dot(dy16[:, ch], wc16[st, :], _NT)
        xf32[...] = x16.astype(F32)
        _scan_core(gx, are_ref, aim_ref, pw, last, of32, rev=rev, xf32=xf32, da_ref=da_ref)
        g16 = of32[...].astype(BF16)
        dus = []
        for k in range(S5_TILES):
            re, im = slice(k * S5_TS, (k + 1) * S5_TS), slice(S5_N + k * S5_TS, S5_N + (k + 1) * S5_TS)
            dus.append(_dot(g16[:, re], wb16[:, re], _NT) + _dot(g16[:, im], wb16[:, im], _NT))
        du_ref[...] = jnp.concatenate(dus, axis=1) + add_ref[...]
        for ch, st in _s5_tiles():
            dwb_ref[:, st] += _dot(u16[:, ch], g16[:, st], _TN)
            dwc_ref[st, :] += _dot(x16[:, st], dy16[:, ch], _TN)

    tmap = (lambda t: (nt - 1 - t, 0)) if rev else (lambda t: (t, 0))
    const = lambda t: (0, 0)
    row_w, row_s = pl.BlockSpec((tb, S5_W), tmap), pl.BlockSpec((tb, 2 * S5_N), tmap)
    return _pcall(
        body, name=name, grid=(nt,),
        in_specs=[row_w, row_w, row_s, pl.BlockSpec((1, S5_N), const), pl.BlockSpec((1, S5_N), const),
                  pl.BlockSpec((S5_TC, 2 * S5_N), const), pl.BlockSpec((2 * S5_N, S5_TC), const), row_w],
        out_specs=[row_w, pl.BlockSpec((S5_TC, 2 * S5_N), const), pl.BlockSpec((2 * S5_N, S5_TC), const),
                   pl.BlockSpec((SUBLANE, 2 * S5_N), const)],
        out_shape=[jax.ShapeDtypeStruct((n_tok, S5_W), F32), jax.ShapeDtypeStruct((S5_TC, 2 * S5_N), F32),
                   jax.ShapeDtypeStruct((2 * S5_N, S5_TC), F32), jax.ShapeDtypeStruct((SUBLANE, 2 * S5_N), F32)],
        scratch_shapes=_s5_scratch(tb) + [pltpu.VMEM((tb, 2 * S5_N), F32)],
        compiler_params=_cparams(("arbitrary",), 12 * tb * 2 * S5_N * 4),
    )(dy, z, xs, a_re, a_im, wb, wc, add)


GLA_TB = 512


def _gla_block_terms(la, q, k, rev):
    trif = _chunk_mask(rev).astype(F32)
    chunks = [la[t:t + GLA_C] for t in range(0, la.shape[0], GLA_C)]
    bcum = jnp.concatenate([_dot_exact(c, trif, m_first=True) for c in chunks], axis=0)
    blast = _per_chunk_sum(la)
    eb, enb, ee = jnp.exp(bcum), jnp.exp(-bcum), jnp.exp(blast - bcum)
    return trif, jnp.exp(blast), eb, enb, ee, q * eb, k * enb, k * ee


def _per_chunk_sum(t):
    return jnp.concatenate([jnp.broadcast_to(jnp.sum(t[r:r + GLA_C], axis=0, keepdims=True), (GLA_C, t.shape[1]))
                            for r in range(0, t.shape[0], GLA_C)], axis=0)


def _chunk_mask(rev):
    ri = lax.broadcasted_iota(jnp.int32, (GLA_C, GLA_C), 0)
    ci = lax.broadcasted_iota(jnp.int32, (GLA_C, GLA_C), 1)
    return (ri <= ci) if rev else (ri >= ci)


def _gla_fwd(name, z, la, *, rev):
    n_tok = z.shape[0]
    tb = _tile(n_tok, GLA_TB, GLA_C)
    nblk, per = n_tok // tb, tb // GLA_C
    scale = GLA_D ** -0.5

    def body(q_ref, k_ref, v_ref, la_ref, o_ref, s_ref, st, qd_s, kd_s, ke_s, v_s, dec_s):
        @pl.when(pl.program_id(0) == 0)
        def _():
            st[...] = jnp.zeros(st.shape, F32)

        _, dec, _, _, _, qd, kd, ke = _gla_block_terms(la_ref[...], q_ref[...].astype(F32) * scale,
                                                                 k_ref[...].astype(F32), rev)
        qd_s[...], kd_s[...], ke_s[...] = qd.astype(BF16), kd.astype(BF16), ke.astype(BF16)
        v_s[...] = v_ref[...].astype(BF16)
        dec_s[...] = dec
        tri = _chunk_mask(rev)
        for step in range(per):
            t = per - 1 - step if rev else step
            rows = pl.ds(t * GLA_C, GLA_C)
            for h in range(GLA_H):
                lanes = pl.ds(h * GLA_D, GLA_D)
                qd16, v16 = qd_s[rows, lanes], v_s[rows, lanes]
                sc = jnp.where(tri, _dot(qd16, kd_s[rows, lanes], _NT), 0.0)
                s16 = st[h].astype(BF16)
                s_ref[h, step] = s16
                o_ref[rows, lanes] = _dot(sc.astype(BF16), v16) + _dot(qd16, s16, _NT)
                st[h] = st[h] * dec_s[pl.ds(t * GLA_C, 1), lanes] + _dot(v16, ke_s[rows, lanes], _TN)

    bmap = (lambda n: nblk - 1 - n) if rev else (lambda n: n)

    def zspec(cb):
        return pl.BlockSpec((tb, GLA_W), lambda n: (bmap(n), cb))

    half = pltpu.VMEM((tb, GLA_W), BF16)
    return _pcall(
        body, name=name, grid=(nblk,),
        in_specs=[zspec(1), zspec(2), zspec(3), zspec(0)],
        out_specs=[zspec(0), pl.BlockSpec((GLA_H, per, GLA_D, GLA_D), lambda n: (0, n, 0, 0))],
        out_shape=[jax.ShapeDtypeStruct((n_tok, GLA_W), F32),
                   jax.ShapeDtypeStruct((GLA_H, n_tok // GLA_C, GLA_D, GLA_D), BF16)],
        scratch_shapes=[pltpu.VMEM((GLA_H, GLA_D, GLA_D), F32), half, half, half, half, pltpu.VMEM((tb, GLA_W), F32)],
        compiler_params=_cparams(("arbitrary",), 40 * tb * GLA_W * 4),
    )(z, z, z, la)


def _gla_bwd(name, z, la, s_prev, do, *, rev):
    n_tok = z.shape[0]
    tb = _tile(n_tok, GLA_TB, GLA_C)
    nblk, per = n_tok // tb, tb // GLA_C
    scale = GLA_D ** -0.5

    def body(q_ref, k_ref, v_ref, la_ref, s_ref, do_ref, dq_ref, dk_ref, dv_ref, dla_ref,
             dst, qd_s, kd_s, ke_s, v_s, do_s, dec_s, dqd_s, dkd_s, dke_s, dd_s):
        @pl.when(pl.program_id(0) == 0)
        def _():
            dst[...] = jnp.zeros(dst.shape, F32)

        trif, dec, eb, enb, ee, qd, kd, ke = _gla_block_terms(la_ref[...], q_ref[...].astype(F32) * scale,
                                                                 k_ref[...].astype(F32), rev)
        qd_s[...], kd_s[...], ke_s[...] = qd.astype(BF16), kd.astype(BF16), ke.astype(BF16)
        v_s[...] = v_ref[...].astype(BF16)
        do_s[...] = do_ref[...].astype(BF16)
        dec_s[...] = dec
        tri = _chunk_mask(rev)
        for step in range(per):
            t = step if rev else per - 1 - step
            rows = pl.ds(t * GLA_C, GLA_C)
            for h in range(GLA_H):
                lanes = pl.ds(h * GLA_D, GLA_D)
                qd16, kd16, ke16 = qd_s[rows, lanes], kd_s[rows, lanes], ke_s[rows, lanes]
                v16, do16 = v_s[rows, lanes], do_s[rows, lanes]
                sc16 = jnp.where(tri, _dot(qd16, kd16, _NT), 0.0).astype(BF16)
                dsc16 = jnp.where(tri, _dot(do16, v16, _NT), 0.0).astype(BF16)
                s16 = s_ref[h, per - 1 - step]
                dsn = dst[h]
                dsn16 = dsn.astype(BF16)
                dec_c = dec_s[pl.ds(t * GLA_C, 1), lanes]
                dqd_s[rows, lanes] = _dot(dsc16, kd16) + _dot(do16, s16)
                dkd_s[rows, lanes] = _dot(dsc16, qd16, _TN)
                dke_s[rows, lanes] = _dot(v16, dsn16)
                dv_ref[rows, lanes] = _dot(sc16, do16, _TN) + _dot(ke16, dsn16, _NT)
                ddec = jnp.sum(dsn * s16.astype(F32), axis=0, keepdims=True) * dec_c
                dd_s[rows, lanes] = jnp.broadcast_to(ddec, (GLA_C, GLA_D))
                dst[h] = dsn * dec_c + _dot(do16, qd16, _TN)
        dqd, dkd, dke = dqd_s[...], dkd_s[...], dke_s[...]
        dq_ref[...] = dqd * eb * scale
        dk_ref[...] = dkd * enb + dke * ee
        db = dqd * qd - dkd * kd - dke * ke
        dcum = jnp.concatenate([_dot_exact(db[r:r + GLA_C], trif, _TN, m_first=True) for r in range(0, tb, GLA_C)], axis=0)
        dla_ref[...] = dcum + _per_chunk_sum(dke * ke) + dd_s[...]

    bmap = (lambda n: n) if rev else (lambda n: nblk - 1 - n)

    def zspec(cb):
        return pl.BlockSpec((tb, GLA_W), lambda n: (bmap(n), cb))

    o4 = jax.ShapeDtypeStruct((n_tok, GLA_W), F32)
    half, full = pltpu.VMEM((tb, GLA_W), BF16), pltpu.VMEM((tb, GLA_W), F32)
    return _pcall(
        body, name=name, grid=(nblk,),
        in_specs=[zspec(1), zspec(2), zspec(3), zspec(0),
                  pl.BlockSpec((GLA_H, per, GLA_D, GLA_D), lambda n: (0, nblk - 1 - n, 0, 0)), zspec(0)],
        out_specs=[zspec(0)] * 4, out_shape=[o4] * 4,
        scratch_shapes=[pltpu.VMEM((GLA_H, GLA_D, GLA_D), F32), half, half, half, half, half, full, full, full, full, full],
        compiler_params=_cparams(("arbitrary",), 64 * tb * GLA_W * 4),
    )(z, z, z, la, s_prev, do)


_PAIRS = AT_W // LANE
_PAIRS_PER_KV = _PAIRS // AT_KH


def _attn_fwd(name, q, kd, vd, tq=256):
    n_tok = q.shape[0]
    tq = _tile(n_tok, tq, 16)

    def body(q_ref, k_ref, v_ref, o_ref, l_ref):
        qv, kv, vv = q_ref[...], k_ref[...], v_ref[...]
        low = lax.broadcasted_iota(jnp.int32, qv.shape, 1) < AT_D
        res = []
        for keep in (low, jnp.logical_not(low)):
            s = _dot(jnp.where(keep, qv, jnp.zeros_like(qv)), kv, _NT)
            m = jnp.max(s, axis=-1, keepdims=True)
            p = jnp.exp(s - m)
            den = jnp.sum(p, axis=-1, keepdims=True)
            res.append((_dot(p.astype(BF16), vv) / den, m + jnp.log(den)))
        o_ref[...] = jnp.where(low, res[0][0], res[1][0])
        l_ref[...] = jnp.where(low, res[0][1], res[1][1])

    qspec = pl.BlockSpec((tq, LANE), lambda p, i: (i, p))
    kspec = pl.BlockSpec((n_tok, LANE), lambda p, i: (0, p // _PAIRS_PER_KV))
    vm = 5 * _nbytes((tq, n_tok), F32) + 8 * _nbytes((n_tok, LANE), BF16)
    out = jax.ShapeDtypeStruct((n_tok, AT_W), F32)
    return _pcall(
        body, name=name, grid=(_PAIRS, n_tok // tq), in_specs=[qspec, kspec, kspec], out_specs=[qspec, qspec],
        out_shape=[out, out], compiler_params=_cparams(("parallel", "parallel"), vm),
    )(q, kd, vd)


def _attn_bwd(name, q, kd, vd, o, lse, do, tq=256):
    n_tok = q.shape[0]
    tq = _tile(n_tok, tq, 16)

    def body(q_ref, k_ref, v_ref, o_ref, l_ref, do_ref, dq_ref, dk_ref, dv_ref):
        @pl.when((pl.program_id(1) == 0) & (pl.program_id(2) == 0))
        def _():
            dk_ref[...] = jnp.zeros(dk_ref.shape, F32)
            dv_ref[...] = jnp.zeros(dv_ref.shape, F32)

        qv, kv, vv = q_ref[...], k_ref[...], v_ref[...]
        dof, lv = do_ref[...], l_ref[...]
        do16 = dof.astype(BF16)
        doo = dof * o_ref[...]
        low = lax.broadcasted_iota(jnp.int32, qv.shape, 1) < AT_D
        dq = jnp.zeros(qv.shape, F32)
        for keep in (low, jnp.logical_not(low)):
            qm = jnp.where(keep, qv, jnp.zeros_like(qv))
            dom = jnp.where(keep, do16, jnp.zeros_like(do16))
            lh = jnp.max(jnp.where(keep, lv, -1e30), axis=-1, keepdims=True)
            p = jnp.exp(_dot(qm, kv, _NT) - lh)
            delta = jnp.sum(jnp.where(keep, doo, 0.0), axis=-1, keepdims=True)
            ds16 = (p * (_dot(dom, vv, _NT) - delta)).astype(BF16)
            dq = jnp.where(keep, _dot(ds16, kv), dq)
            dk_ref[...] += _dot(ds16, qm, _TN)
            dv_ref[...] += _dot(p.astype(BF16), dom, _TN)
        dq_ref[...] = dq

    qspec = pl.BlockSpec((tq, LANE), lambda g, j, i: (i, g * _PAIRS_PER_KV + j))
    kspec = pl.BlockSpec((n_tok, LANE), lambda g, j, i: (0, g))
    vm = 7 * _nbytes((tq, n_tok), F32) + 12 * _nbytes((n_tok, LANE), F32)
    dup = jax.ShapeDtypeStruct((n_tok, AT_KH * LANE), F32)
    return _pcall(
        body, name=name, grid=(AT_KH, _PAIRS_PER_KV, n_tok // tq),
        in_specs=[qspec, kspec, kspec, qspec, qspec, qspec], out_specs=[qspec, kspec, kspec],
        out_shape=[jax.ShapeDtypeStruct((n_tok, AT_W), F32), dup, dup],
        compiler_params=_cparams(("parallel", "arbitrary", "arbitrary"), vm),
    )(q, kd, vd, o, lse, do)


def _s5_prep(lam_re, lam_im, log_dt, b_re, b_im, c_re, c_im):
    dt = jnp.exp(log_dt)[:, None]
    er = jnp.exp(lam_re * dt)
    a_re, a_im = er * jnp.cos(lam_im * dt), er * jnp.sin(lam_im * dt)
    nr, ni = a_re - 1.0, a_im
    den = lam_re * lam_re + lam_im * lam_im
    f_re, f_im = (nr * lam_re + ni * lam_im) / den, (ni * lam_re - nr * lam_im) / den
    bb_re = f_re[..., None] * b_re - f_im[..., None] * b_im
    bb_im = f_re[..., None] * b_im + f_im[..., None] * b_re
    def tiles(t, axis):
        _, a, b = t.shape
        eye = jnp.eye(S5_TG, dtype=t.dtype)
        blocks = t.reshape(S5_TILES, S5_TG, a, 1, b) * eye[None, :, None, :, None]
        if axis == 0:
            return blocks.reshape(S5_TILES * S5_TG * a, S5_TG * b)
        return blocks.transpose(1, 2, 0, 3, 4).reshape(S5_TG * a, S5_TILES * S5_TG * b)

    w_in = jnp.concatenate([tiles(bb_re.transpose(0, 2, 1), 1), tiles(bb_im.transpose(0, 2, 1), 1)], axis=1)
    w_out = jnp.concatenate([tiles(c_re.transpose(0, 2, 1), 0), -tiles(c_im.transpose(0, 2, 1), 0)], axis=0)
    return a_re.reshape(1, S5_N), a_im.reshape(1, S5_N), w_in, w_out


def _rope_tables(n_tok):
    quarter = AT_D // 4
    d = np.arange(AT_D)
    e = d % (AT_D // 2)
    inv = (ROPE_BASE ** (-(e % quarter).astype(np.float64) * 2.0 / (AT_D // 2))).astype(np.float32)
    sign = np.where(e < quarter, -1.0, 1.0).astype(np.float32)
    rot = np.zeros((AT_D, AT_D), np.float32)
    rot[np.where(e < quarter, d + quarter, d - quarter), d] = 1.0
    pos = jnp.arange(n_tok, dtype=jnp.int32)
    axis_pos = jnp.where(jnp.asarray(d // (AT_D // 2) == 0)[None, :], (pos // GRID_W)[:, None], (pos % GRID_W)[:, None])
    ang = axis_pos.astype(F32) * jnp.asarray(inv)[None, :]
    return jnp.cos(ang), jnp.sin(ang) * jnp.asarray(sign)[None, :], rot


def _head_consts(n_tok, heads):
    cos, sin, rot = _rope_tables(n_tok)
    eye = np.eye(heads, dtype=np.float32)
    mean = np.kron(eye, np.full((AT_D, AT_D), 1.0 / AT_D, np.float32))
    return jnp.tile(cos, (1, heads)), jnp.tile(sin, (1, heads)), jnp.asarray(mean), jnp.asarray(np.kron(eye, rot))


def _pad_w_in(wt):
    return jnp.concatenate([wt[:2560], wt[2592:3360], wt[2560:2592], jnp.zeros((96, wt.shape[1]), wt.dtype)], axis=0)


def _unpad_w_in(gt):
    return jnp.concatenate([gt[:2560], gt[3328:3360], gt[2560:3328]], axis=0)


def _ffn_fwd(x, g, wg, wu, wd, d):
    n_tok = x.shape[0]
    fq = wd.shape[-2]
    tm = _tile(n_tok, 256, 16)
    w_bytes = N_CHIPS * fq * D_MODEL * 2

    def up(x_ref, g_ref, wg_ref, wu_ref, h_ref, a_ref, b_ref, s_ref):
        hv = _rms(x_ref[...], g_ref[...]).astype(BF16)
        h_ref[...] = hv
        for j in range(N_CHIPS):
            av, bv = _dot(hv, wg_ref[j, 0], _NT), _dot(hv, wu_ref[j, 0], _NT)
            a_ref[j] = av.astype(BF16)
            b_ref[j] = bv.astype(BF16)
            s_ref[j] = (av * _sigmoid(av) * bv).astype(BF16)

    wspec = pl.BlockSpec((N_CHIPS, 1, fq, D_MODEL), lambda i: (0, d, 0, 0))
    aspec = pl.BlockSpec((N_CHIPS, tm, fq), lambda i: (0, i, 0))
    xspec = pl.BlockSpec((tm, D_MODEL), lambda i: (i, 0))
    act = jax.ShapeDtypeStruct((N_CHIPS, n_tok, fq), BF16)
    h, a, b, s = _pcall(
        up, name="ffn_up", grid=(n_tok // tm,),
        in_specs=[xspec, pl.BlockSpec((1, D_MODEL), lambda i: (0, 0)), wspec, wspec],
        out_specs=[xspec, aspec, aspec, aspec],
        out_shape=[jax.ShapeDtypeStruct((n_tok, D_MODEL), BF16), act, act, act],
        compiler_params=_cparams(("parallel",), 4 * w_bytes + 6 * N_CHIPS * tm * fq * 2 + 12 * tm * fq * 4
                                 + 8 * tm * D_MODEL * 4),
    )(x, g, wg, wu)

    def down(s_ref, w_ref, x_ref, o_ref):
        acc = _dot(s_ref[0], w_ref[0, 0])
        for j in range(1, N_CHIPS):
            acc = acc + _dot(s_ref[j], w_ref[j, 0])
        o_ref[...] = x_ref[...] + 0.5 * acc

    xo = _pcall(
        down, name="ffn_down", grid=(n_tok // tm,), in_specs=[aspec, wspec, xspec], out_specs=xspec,
        out_shape=jax.ShapeDtypeStruct((n_tok, D_MODEL), F32),
        compiler_params=_cparams(("parallel",), 2 * w_bytes + 2 * N_CHIPS * tm * fq * 2 + 8 * tm * D_MODEL * 4),
    )(s, wd, x)
    return xo, (x, h, a, b, s)


def _ffn_bwd(dy, dy16, saved, g, wg, wu, wd, d, into):
    x, h, a, b, s = saved
    n_tok = x.shape[0]
    fq = wd.shape[-2]
    tm = _tile(n_tok, 512, 16)

    ta_ = _tile(n_tok, 256, 16)

    def dact(dy_ref, w_ref, a_ref, b_ref, da_ref, db_ref):
        dyv = dy_ref[...]
        for j in range(N_CHIPS):
            ds = 0.5 * _dot(dyv, w_ref[j, 0], _NT)
            av, bv = a_ref[j].astype(F32), b_ref[j].astype(F32)
            sg = _sigmoid(av)
            da_ref[j] = (ds * bv * sg * (1.0 + av * (1.0 - sg))).astype(BF16)
            db_ref[j] = (ds * av * sg).astype(BF16)

    aspec = pl.BlockSpec((N_CHIPS, ta_, fq), lambda i: (0, i, 0))
    act = jax.ShapeDtypeStruct((N_CHIPS, n_tok, fq), BF16)
    da, db = _pcall(
        dact, name="ffn_dact", grid=(n_tok // ta_,),
        in_specs=[pl.BlockSpec((ta_, D_MODEL), lambda i: (i, 0)),
                  pl.BlockSpec((N_CHIPS, 1, fq, D_MODEL), lambda i: (0, d, 0, 0)), aspec, aspec],
        out_specs=[aspec, aspec], out_shape=[act, act],
        compiler_params=_cparams(("parallel",), 2 * N_CHIPS * fq * D_MODEL * 2 + 8 * N_CHIPS * ta_ * fq * 2
                                 + 12 * ta_ * fq * 4),
    )(dy16, wd, a, b)

    any_spec = pl.BlockSpec(memory_space=pl.ANY)
    alias = {} if into is None else {2: 0}
    tn = _tile(D_MODEL, 512, LANE)

    def dw_call(t, rhs, scale, prev):
        def dw(t_ref, r_ref, *rest):
            rest[-1][0, 0] = (scale * _dot(t_ref[0], r_ref[...], _TN)).astype(GRAD_DT)

        return _pcall(
            dw, name="ffn_dw", grid=(N_CHIPS, D_MODEL // tn),
            in_specs=[pl.BlockSpec((1, n_tok, fq), lambda j, n: (j, 0, 0)), pl.BlockSpec((n_tok, tn), lambda j, n: (0, n))]
            + ([] if prev is None else [any_spec]),
            out_specs=pl.BlockSpec((1, 1, fq, tn), lambda j, n: (j, d, 0, n)),
            out_shape=jax.ShapeDtypeStruct((N_CHIPS, DEPTH, fq, D_MODEL), GRAD_DT), input_output_aliases=alias,
            compiler_params=_cparams(("parallel", "parallel"), 6 * n_tok * (fq + tn) * 2 + 4 * fq * tn * 4),
        )(t, rhs, *([] if prev is None else [prev]))

    dwg = dw_call(da, h, 1.0, None if into is None else into[0])
    dwu = dw_call(db, h, 1.0, None if into is None else into[1])
    dwd = dw_call(s, dy16, 0.5, None if into is None else into[2])
    tr = _tile(n_tok, 256, 16)

    def dh_norm(da_ref, db_ref, wg_ref, wu_ref, dy_ref, x_ref, g_ref, dx_ref, dx16_ref, dg_ref):
        dh = _dot(da_ref[0], wg_ref[0, 0]) + _dot(db_ref[0], wu_ref[0, 0])
        for j in range(1, N_CHIPS):
            dh = dh + _dot(da_ref[j], wg_ref[j, 0]) + _dot(db_ref[j], wu_ref[j, 0])
        dx, dg = _rms_bwd(x_ref[...], g_ref[...], dh)
        dx = dy_ref[...] + dx
        dx_ref[...] = dx
        dx16_ref[...] = dx.astype(BF16)

        @pl.when(pl.program_id(0) == 0)
        def _():
            dg_ref[...] = jnp.zeros(dg_ref.shape, F32)

        dg_ref[...] += dg

    tspec = pl.BlockSpec((N_CHIPS, tr, fq), lambda i: (0, i, 0))
    wspec = pl.BlockSpec((N_CHIPS, 1, fq, D_MODEL), lambda i: (0, d, 0, 0))
    rspec = pl.BlockSpec((tr, D_MODEL), lambda i: (i, 0))
    vspec = pl.BlockSpec((1, D_MODEL), lambda i: (0, 0))
    dx, dx16, dg = _pcall(
        dh_norm, name="ffn_dh_norm", grid=(n_tok // tr,),
        in_specs=[tspec, tspec, wspec, wspec, rspec, rspec, vspec], out_specs=[rspec, rspec, vspec],
        out_shape=[jax.ShapeDtypeStruct((n_tok, D_MODEL), F32), jax.ShapeDtypeStruct((n_tok, D_MODEL), BF16),
                   jax.ShapeDtypeStruct((1, D_MODEL), F32)],
        compiler_params=_cparams(("arbitrary",), 4 * N_CHIPS * D_MODEL * fq * 2 + 4 * N_CHIPS * tr * fq * 2
                                 + 16 * tr * D_MODEL * 4),
    )(da, db, wg, wu, dy, x, g)
    return dx, dx16, dg[0], (dwg, dwu, dwd)


def _alpha_pads(w_alpha):
    z = jnp.zeros((LANE - 2 * GLA_R, GLA_W), F32)
    wf = jnp.concatenate([w_alpha[0], jnp.zeros((GLA_R, GLA_W), F32), z], axis=0)
    wb = jnp.concatenate([jnp.zeros((GLA_R, GLA_W), F32), w_alpha[1], z], axis=0)
    return wf, wb


def _dup_heads(t):
    low = lax.broadcasted_iota(jnp.int32, t.shape, 1) < AT_D
    h0, h1 = jnp.where(low, t, 0.0), jnp.where(low, 0.0, t)
    return jnp.concatenate([h0 + pltpu.roll(h0, AT_D, 1), h1 + pltpu.roll(h1, AT_D, 1)], axis=1)


def _fold_heads(t):
    low = lax.broadcasted_iota(jnp.int32, (t.shape[0], LANE), 1) < AT_D
    g0, g1 = t[:, :LANE], t[:, LANE:]
    return jnp.where(low, g0 + pltpu.roll(g0, AT_D, 1), g1 + pltpu.roll(g1, AT_D, 1))


def _mixer_fwd(x, w):
    n_tok = x.shape[0]
    wip, wmg = w['w_in_p'], w['w_merge_gate']
    tr = _tile(n_tok, 256, 16)

    def project(x_ref, g_ref, wip_ref, wmg_ref, h_ref, z_ref, gp_ref):
        hv = _rms(x_ref[...], g_ref[...]).astype(BF16)
        h_ref[...] = hv
        z_ref[...] = _dot(hv, wip_ref[...], _NT).astype(BF16)
        gp_ref[...] = _dot(hv, wmg_ref[...]).astype(BF16)

    def rows(c):
        return pl.BlockSpec((tr, c), lambda i: (i, 0))

    def whole(t):
        return pl.BlockSpec(t.shape, lambda i: (0, 0))

    h, z, gp = _pcall(
        project, name="mix_in", grid=(n_tok // tr,),
        in_specs=[rows(D_MODEL), whole(w['mix_norm']), whole(wip), whole(wmg)],
        out_specs=[rows(D_MODEL), rows(ZP_W), rows(3 * D_MODEL)],
        out_shape=[jax.ShapeDtypeStruct((n_tok, D_MODEL), BF16), jax.ShapeDtypeStruct((n_tok, ZP_W), BF16),
                   jax.ShapeDtypeStruct((n_tok, 3 * D_MODEL), BF16)],
        compiler_params=_cparams(("parallel",), 4 * (wip.size + wmg.size) + 16 * tr * (ZP_W + 3 * D_MODEL)),
    )(x, w['mix_norm'], wip, wmg)
    u = _cw(z, 0, S5_W)

    xs, yy = [], None
    for d, rev in ((0, False), (1, True)):
        a_re, a_im, wb, wc = w['s5'][d]
        x_d, yy = _s5_fwd("s5_rev" if rev else "s5_fwd", z, a_re, a_im, wb, wc, yy, rev=rev)
        xs.append(x_d)

    def s5_post(yv, uv, dv, wglu):
        y0 = yv + dv * uv
        y1 = _gelu(y0)
        t = _dot(y1.astype(BF16), wglu)
        return y0, t, y1 * _sigmoid(t)

    y0, t_glu, y_s5 = _rowwise("s5_post", s5_post, [yy, u], [w['s5_d'], w['s5_w_glu']],
                               [(S5_W, F32), (S5_W, F32), (S5_W, BF16)])

    wa_f, wa_b = _alpha_pads(w['gla_w_alpha'])
    ba_f, ba_b = w['gla_b_alpha'][0:1], w['gla_b_alpha'][1:2]
    zz = _cw(z, ZC_Z, LANE)

    def gla_prep(zv, waf, wab, bf, bb):
        z16 = zv.astype(BF16)
        lf = _dot(z16, waf.astype(BF16)) + bf
        lb = _dot(z16, wab.astype(BF16)) + bb
        return _log_sigmoid(lf) / GLA_TAU, _log_sigmoid(lb) / GLA_TAU

    la_f, la_b = _rowwise("gla_prep", gla_prep, [zz], [wa_f, wa_b, ba_f, ba_b], [(GLA_W, F32), (GLA_W, F32)])
    o_f, s_f = _gla_fwd("gla_fwd", z, la_f, rev=False)
    o_b, s_b = _gla_fwd("gla_rev", z, la_b, rev=True)
    gate = _cw(z, 4, GLA_W)

    def gla_post(of, ob, gv, gn):
        o, gv = of + ob, gv.astype(F32)
        outs = []
        for hh in range(GLA_H):
            sl = slice(hh * GLA_D, (hh + 1) * GLA_D)
            gh = gv[:, sl]
            outs.append(_rms(o[:, sl], gn) * (gh * _sigmoid(gh)))
        return jnp.concatenate(outs, axis=1)

    y_gla = _rowwise("gla_post", gla_post, [o_f, o_b, gate], [w['gla_norm']], [(GLA_W, BF16)])[0]

    cq, sq, mq, rq = _head_consts(n_tok, AT_QH)
    ck, sk, mk, rk = _head_consts(n_tok, AT_KH)
    gq = jnp.tile(w['attn_q_norm'], (1, AT_QH))
    gk = jnp.tile(w['attn_k_norm'], (1, AT_KH))

    def rope_fwd(v, gain, cos, sin, mean, rot, scale):
        r = lax.rsqrt(_dot_exact(v * v, mean, terms=2) + EPS)
        vn = v * r * gain
        return (vn * cos + _dot_exact(vn, rot, terms=2) * sin) * scale

    def attn_prep(qv, kv, vv, cqv, sqv, ckv, skv, gqv, gkv, mqv, rqv, mkv, rkv):
        qv, kv, vv = qv.astype(F32), kv.astype(F32), vv.astype(F32)
        kr = rope_fwd(kv, gkv, ckv, skv, mkv, rkv, 1.0)
        return rope_fwd(qv, gqv, cqv, sqv, mqv, rqv, AT_D ** -0.5), _dup_heads(kr), _dup_heads(vv)

    q_raw, k_raw, v_raw = _cw(z, 5, AT_W), _cw(z, ZC_AK, AT_KW), _cw(z, ZC_AV, AT_KW)
    q16, kd, vd = _rowwise("attn_prep", attn_prep, [q_raw, k_raw, v_raw, cq, sq, ck, sk], [gq, gk, mq, rq, mk, rk],
                           [(AT_W, BF16), (2 * LANE, BF16), (2 * LANE, BF16)])
    y_at, lse = _attn_fwd("attn_fwd", q16, kd, vd)

    p0 = _mm("mix_branch", y_s5, w['w_branch_s5'], out_dtype=BF16)
    p1 = _mm("mix_branch", y_gla, w['w_branch_gla'], out_dtype=BF16)
    p2 = _mm("mix_branch_f32", y_at, w['w_branch_attn'], out_dtype=BF16)

    def merge(gpv, a0, a1, a2, bias):
        g = _sigmoid(gpv.astype(F32) + bias)
        return (g[:, :D_MODEL] * a0.astype(F32) + g[:, D_MODEL:2 * D_MODEL] * a1.astype(F32)
                + g[:, 2 * D_MODEL:] * a2.astype(F32))

    merged = _rowwise("mix_merge", merge, [gp, p0, p1, p2], [w['b_merge_gate']], [(D_MODEL, BF16)])[0]
    xo = _mm("mix_out", merged, w['w_out'], add=x)
    saved = dict(x=x, h=h, z=z, gp=gp, xs=xs, y0=y0, t_glu=t_glu, y_s5=y_s5, la=(la_f, la_b), o=(o_f, o_b), s=(s_f, s_b),
                 y_gla=y_gla, q16=q16, kd=kd, vd=vd, lse=lse, y_at=y_at, p=(p0, p1, p2), merged=merged)
    return xo, saved


def _mixer_bwd(dy, dy16, sv, w, s5_raw):
    z, h = sv['z'], sv['h']
    n_tok = z.shape[0]
    grads = {}
    dmerged = _mm("mix_dmerged", dy16, w['w_out'], tb=True, out_dtype=BF16)
    grads['w_out'] = _mm("mix_dwout", sv['merged'], dy16, ta=True, out_dtype=GRAD_DT)

    def dmerge(dm, gpv, a0, a1, a2, bias):
        dm = dm.astype(F32)
        g = _sigmoid(gpv.astype(F32) + bias)
        ps = (a0.astype(F32), a1.astype(F32), a2.astype(F32))
        dps, dgs = [], []
        for k in range(3):
            gk = g[:, k * D_MODEL:(k + 1) * D_MODEL]
            dps.append(dm * gk)
            dgs.append(dm * ps[k] * gk * (1.0 - gk))
        dgp = jnp.concatenate(dgs, axis=1)
        return dps[0], dps[1], dps[2], dgp, jnp.sum(dgp, axis=0, keepdims=True)

    dp0, dp1, dp2, dgp, dbm = _rowwise("mix_dmerge", dmerge, [dmerged, sv['gp'], *sv['p']], [w['b_merge_gate']],
                                       [(D_MODEL, BF16)] * 3 + [(3 * D_MODEL, BF16)], accs=[(1, 3 * D_MODEL)])
    grads['b_merge_gate'] = dbm[0]
    grads['w_branch_s5'] = _mm("mix_dbranch", sv['y_s5'], dp0, ta=True, out_dtype=GRAD_DT)
    grads['w_branch_gla'] = _mm("mix_dbranch", sv['y_gla'], dp1, ta=True, out_dtype=GRAD_DT)
    grads['w_branch_attn'] = _mm("mix_dbranch_f32", sv['y_at'], dp2, ta=True, out_dtype=GRAD_DT)
    dy_s5 = _mm("mix_dy", dp0, w['w_branch_s5'], tb=True)
    dy_gla = _mm("mix_dy", dp1, w['w_branch_gla'], tb=True)
    dy_at = _mm("mix_dy", dp2, w['w_branch_attn'], tb=True)
    grads['w_merge_gate'] = _mm("mix_dwgate", h, dgp, ta=True, out_dtype=GRAD_DT)
    u = _cw(z, 0, S5_W)

    def s5_post_bwd(dyv, y0, t, uv, dv, wglu):
        sg = _sigmoid(t)
        y1 = _gelu(y0)
        dt = dyv * y1 * sg * (1.0 - sg)
        dy1 = dyv * sg + _dot(dt.astype(BF16), wglu, _NT)
        dy0 = dy1 * _gelu_grad(y0)
        return dy0, dt, y1, dy0 * dv, jnp.sum(dy0 * uv, axis=0, keepdims=True)

    dy0, dt_glu, y1, du, dd = _rowwise("s5_post_bwd", s5_post_bwd, [dy_s5, sv['y0'], sv['t_glu'], u],
                                       [w['s5_d'], w['s5_w_glu']],
                                       [(S5_W, BF16), (S5_W, BF16), (S5_W, BF16), (S5_W, F32)], accs=[(1, S5_W)])
    grads['s5_d'] = dd[0]
    grads['s5_w_glu'] = _mm("s5_dwglu", y1, dt_glu, ta=True, out_dtype=GRAD_DT)
    s5_cot = []
    for d, rev in ((0, False), (1, True)):
        a_re, a_im, wb, wc = w['s5'][d]
        du, dwb, dwc, da = _s5_bwd("s5_adj_fwd" if rev else "s5_adj_rev", dy0, z, sv['xs'][d], a_re, -a_im, wb, wc, du,
                                   rev=not rev)
        da = jnp.sum(da, axis=0, keepdims=True)
        s5_cot.append((da[:, :S5_N], da[:, S5_N:], dwb, dwc))
    for d in range(2):
        _, pull = jax.vjp(_s5_prep, *s5_raw[d])
        s5_cot[d] = pull(s5_cot[d])
    for k, nm in enumerate(('s5_lambda_re', 's5_lambda_im', 's5_log_dt', 's5_b_re', 's5_b_im', 's5_c_re', 's5_c_im')):
        grads[nm] = jnp.stack([s5_cot[0][k], s5_cot[1][k]])

    gate = _cw(z, 4, GLA_W)

    def gla_post_bwd(dyv, of, ob, gv, gn):
        o, gv = of + ob, gv.astype(F32)
        dos, dgates = [], []
        dgn = jnp.zeros((1, GLA_D), F32)
        for hh in range(GLA_H):
            sl = slice(hh * GLA_D, (hh + 1) * GLA_D)
            gh, dyh = gv[:, sl], dyv[:, sl]
            sg = _sigmoid(gh)
            dgates.append(dyh * _rms(o[:, sl], gn) * sg * (1.0 + gh * (1.0 - sg)))
            dx, dg = _rms_bwd(o[:, sl], gn, dyh * gh * sg)
            dos.append(dx)
            dgn = dgn + dg
        return jnp.concatenate(dos, axis=1), jnp.concatenate(dgates, axis=1), dgn

    do, dgate, dgn = _rowwise("gla_post_bwd", gla_post_bwd, [dy_gla, *sv['o'], gate], [w['gla_norm']],
                              [(GLA_W, F32), (GLA_W, F32)], accs=[(1, GLA_D)])
    grads['gla_norm'] = dgn[0]
    dq_f, dk_f, dv_f, dla_f = _gla_bwd("gla_bwd_fwd", z, sv['la'][0], sv['s'][0], do, rev=False)
    dq_b, dk_b, dv_b, dla_b = _gla_bwd("gla_bwd_rev", z, sv['la'][1], sv['s'][1], do, rev=True)
    wa_f, wa_b = _alpha_pads(w['gla_w_alpha'])
    ba_f, ba_b = w['gla_b_alpha'][0:1], w['gla_b_alpha'][1:2]
    zz = _cw(z, ZC_Z, LANE)

    def gla_prep_bwd(dlf, dlb, zv, waf, wab, bf, bb):
        z16 = zv.astype(BF16)
        waf16, wab16 = waf.astype(BF16), wab.astype(BF16)
        xf = _dot(z16, waf16) + bf
        xb = _dot(z16, wab16) + bb
        df = dlf * (_sigmoid(-xf) / GLA_TAU)
        db = dlb * (_sigmoid(-xb) / GLA_TAU)
        dz = _dot(df.astype(BF16), waf16, _NT) + _dot(db.astype(BF16), wab16, _NT)
        return df, db, dz, jnp.sum(df, axis=0, keepdims=True), jnp.sum(db, axis=0, keepdims=True)

    dlog_f, dlog_b, dzz, dba_f, dba_b = _rowwise(
        "gla_prep_bwd", gla_prep_bwd, [dla_f, dla_b, zz], [wa_f, wa_b, ba_f, ba_b],
        [(GLA_W, BF16), (GLA_W, BF16), (LANE, F32)], accs=[(1, GLA_W), (1, GLA_W)])
    dwa_f = _mm("gla_dwa", zz, dlog_f, ta=True)
    dwa_b = _mm("gla_dwa", zz, dlog_b, ta=True)
    grads['gla_w_alpha'] = jnp.stack([dwa_f[:GLA_R], dwa_b[GLA_R:2 * GLA_R]])
    grads['gla_b_alpha'] = jnp.concatenate([dba_f, dba_b], axis=0)

    dq_r, dkd, dvd = _attn_bwd("attn_bwd", sv['q16'], sv['kd'], sv['vd'], sv['y_at'], sv['lse'], dy_at)
    cq, sq, mq, rq = _head_consts(n_tok, AT_QH)
    ck, sk, mk, rk = _head_consts(n_tok, AT_KH)
    gq = jnp.tile(w['attn_q_norm'], (1, AT_QH))
    gk = jnp.tile(w['attn_k_norm'], (1, AT_KH))

    def rope_bwd(dr, v, gain, cos, sin, mean, rot, scale):
        dr = dr * scale
        dvn = dr * cos + _dot_exact(dr * sin, rot, _NT, terms=2)
        r = lax.rsqrt(_dot_exact(v * v, mean, terms=2) + EPS)
        vh = v * r
        dg = jnp.sum(dvn * vh, axis=0, keepdims=True)
        dvh = dvn * gain
        return r * (dvh - vh * _dot_exact(dvh * vh, mean, terms=2)), dg

    def attn_prep_bwd(dq, dk2, dv2, qv, kv, cqv, sqv, ckv, skv, gqv, gkv, mqv, rqv, mkv, rkv):
        qv, kv = qv.astype(F32), kv.astype(F32)
        dqo, dgq = rope_bwd(dq, qv, gqv, cqv, sqv, mqv, rqv, AT_D ** -0.5)
        dko, dgk = rope_bwd(_fold_heads(dk2), kv, gkv, ckv, skv, mkv, rkv, 1.0)
        return dqo, dko, _fold_heads(dv2), dgq, dgk

    q_raw, k_raw = _cw(z, 5, AT_W), _cw(z, ZC_AK, AT_KW)
    daq, dak, dav, dgq, dgk = _rowwise("attn_prep_bwd", attn_prep_bwd, [dq_r, dkd, dvd, q_raw, k_raw, cq, sq, ck, sk],
                                       [gq, gk, mq, rq, mk, rk], [(AT_W, F32), (AT_KW, F32), (AT_KW, F32)],
                                       accs=[(1, AT_W), (1, AT_KW)])
    grads['attn_q_norm'] = dgq.reshape(AT_QH, AT_D).sum(axis=0)
    grads['attn_k_norm'] = dgk.reshape(AT_KH, AT_D).sum(axis=0)

    def assemble(duv, qf, qb, kf, kb, vf, vb, dg, aq, ak, av, zzv):
        return jnp.concatenate([duv, qf + qb, kf + kb, vf + vb, dg, aq, ak, av, zzv], axis=1)

    dz = _rowwise("mix_dz", assemble, [du, dq_f, dq_b, dk_f, dk_b, dv_f, dv_b, dgate, daq, dak, dav, dzz], [],
                  [(ZP_W, BF16)])[0]
    grads['w_in'] = _unpad_w_in(_mm("mix_dwin", dz, h, ta=True, out_dtype=GRAD_DT))
    wmg, wip = w['w_merge_gate'], w['w_in_p']
    tr = _tile(n_tok, 256, 16)

    def dh_norm(dgp_ref, dz_ref, wmg_ref, wip_ref, dy_ref, x_ref, g_ref, dx_ref, dx16_ref, dg_ref):
        dh = _dot(dgp_ref[...], wmg_ref[...], _NT) + _dot(dz_ref[...], wip_ref[...])
        dxv, dgv = _rms_bwd(x_ref[...], g_ref[...], dh)
        dxv = dy_ref[...] + dxv
        dx_ref[...] = dxv
        dx16_ref[...] = dxv.astype(BF16)

        @pl.when(pl.program_id(0) == 0)
        def _():
            dg_ref[...] = jnp.zeros(dg_ref.shape, F32)

        dg_ref[...] += dgv

    def rows(c):
        return pl.BlockSpec((tr, c), lambda i: (i, 0))

    def whole(t):
        return pl.BlockSpec(t.shape, lambda i: (0, 0))

    dx, dx16, dg = _pcall(
        dh_norm, name="mix_dh_norm", grid=(n_tok // tr,),
        in_specs=[rows(dgp.shape[1]), rows(dz.shape[1]), whole(wmg), whole(wip), rows(D_MODEL), rows(D_MODEL),
                  whole(w['mix_norm'])],
        out_specs=[rows(D_MODEL), rows(D_MODEL), whole(w['mix_norm'])],
        out_shape=[jax.ShapeDtypeStruct((n_tok, D_MODEL), F32), jax.ShapeDtypeStruct((n_tok, D_MODEL), BF16),
                   jax.ShapeDtypeStruct((1, D_MODEL), F32)],
        compiler_params=_cparams(("arbitrary",), 4 * (wmg.size + wip.size) + 8 * tr * (dgp.shape[1] + dz.shape[1])
                                 + 16 * tr * D_MODEL * 4),
    )(dgp, dz, wmg, wip, dy, sv['x'], w['mix_norm'])
    grads['mix_norm'] = dg[0]
    return dx, dx16, grads


FFN_NAMES = [p + k for p in ('ffn1_', 'ffn2_') for k in ('w_gate', 'w_up', 'w_down')]
MISC_SHARDED = [n for n in SHARDED if n not in FFN_NAMES]
LARGE_SHARDED = FFN_NAMES + ['w_in', 'w_merge_gate', 'w_out', 'w_branch_s5', 'w_branch_gla', 'w_branch_attn', 's5_w_glu']
KEPT_TRANSPOSED = [p + k for p in ('ffn1_', 'ffn2_') for k in ('w_gate', 'w_up')] + ['w_in']


def _stored_axis(n):
    return 1 if n in KEPT_TRANSPOSED else SHARD_AXIS[n]


def _stored(n, t):
    return jnp.swapaxes(t, 1, 2) if n in KEPT_TRANSPOSED else t


def _layer_weights(ws, rep, i):
    w = {k: _from_shards(ws[k][:, i], _stored_axis(k) - 1) for k in MISC_SHARDED}
    for k in ('ffn1_norm', 'mix_norm', 'ffn2_norm', 's5_d', 'gla_norm', 'attn_q_norm', 'attn_k_norm', 'b_merge_gate'):
        w[k] = rep[k][i].reshape(1, -1)
    w['w_in_p'] = _pad_w_in(w['w_in'])
    s5_raw = [tuple(rep[k][i, d] for k in ('s5_lambda_re', 's5_lambda_im', 's5_log_dt', 's5_b_re', 's5_b_im',
                                            's5_c_re', 's5_c_im')) for d in range(2)]
    w['s5'] = [_s5_prep(*s5_raw[d]) for d in range(2)]
    return w, s5_raw


def _device_step(x, target, ws, rep):
    lw = [_layer_weights(ws, rep, i) for i in range(DEPTH)]
    ffn = {p: (ws[p + 'w_gate'], ws[p + 'w_up'], ws[p + 'w_down']) for p in ('ffn1_', 'ffn2_')}
    saved = []
    for i in range(DEPTH):
        w, _ = lw[i]
        x, s1 = _ffn_fwd(x, w['ffn1_norm'], *ffn['ffn1_'], i)
        x, s2 = _mixer_fwd(x, w)
        x, s3 = _ffn_fwd(x, w['ffn2_norm'], *ffn['ffn2_'], i)
        saved.append((s1, s2, s3))
    gfin = rep['final_norm'].reshape(1, -1)

    def head(xv, tv, gv):
        e = _rms(xv, gv) - tv
        dx, dg = _rms_bwd(xv, gv, e * (1.0 / D_MODEL))
        part = jnp.sum(e * e, axis=0, keepdims=True)
        return dx, dx, dg, part

    dx, dx16, dgfin, part = _rowwise("loss_head", head, [x, target], [gfin], [(D_MODEL, F32), (D_MODEL, BF16)],
                                     accs=[(1, D_MODEL), (1, D_MODEL)])
    loss = (0.5 / D_MODEL) * jnp.sum(part)
    per_layer = []
    dffn = {'ffn1_': None, 'ffn2_': None}
    for i in reversed(range(DEPTH)):
        w, s5_raw = lw[i]
        s1, s2, s3 = saved[i]
        dx, dx16, n3, dffn['ffn2_'] = _ffn_bwd(dx, dx16, s3, w['ffn2_norm'], *ffn['ffn2_'], i, dffn['ffn2_'])
        dx, dx16, g = _mixer_bwd(dx, dx16, s2, w, s5_raw)
        dx, dx16, n1, dffn['ffn1_'] = _ffn_bwd(dx, dx16, s1, w['ffn1_norm'], *ffn['ffn1_'], i, dffn['ffn1_'])
        g['ffn2_norm'], g['ffn1_norm'] = n3, n1
        per_layer.append(g)
    per_layer.reverse()
    stacked = {k: jnp.stack([per_layer[i][k] for i in range(DEPTH)]) for k in per_layer[0]}
    gs = {k: _to_shards(stacked[k], _stored_axis(k)) for k in MISC_SHARDED}
    for p in ('ffn1_', 'ffn2_'):
        gs[p + 'w_gate'], gs[p + 'w_up'], gs[p + 'w_down'] = dffn[p]
    grep = {k: stacked[k] for k in REPLICATED if k != 'final_norm'}
    grep['final_norm'] = dgfin[0]
    return loss, dx, gs, grep


PACK_C = 1024
ROW_TILE = 512


def _rows_of(shape):
    return -(-int(np.prod(shape)) // PACK_C)


def _as_rows(t, lead=0):
    head = t.shape[:lead]
    flat = t.reshape(head + (-1,))
    rows = -(-flat.shape[-1] // PACK_C)
    pad = rows * PACK_C - flat.shape[-1]
    if pad:
        flat = jnp.pad(flat, [(0, 0)] * lead + [(0, pad)])
    return flat.reshape(head + (rows, PACK_C))


def _pack(items, total_rows, lead=0):
    parts = [_as_rows(t, lead) for t in items]
    used = sum(p.shape[lead] for p in parts)
    if total_rows > used:
        parts.append(jnp.zeros(parts[0].shape[:lead] + (total_rows - used, PACK_C), parts[0].dtype))
    return jnp.concatenate(parts, axis=lead)


def _unpack(rows, shapes, lead=0):
    out, r0 = [], 0
    head = rows.shape[:lead]
    for shp in shapes:
        n, size = _rows_of(shp), int(np.prod(shp))
        piece = lax.slice_in_dim(rows, r0, r0 + n, axis=lead).reshape(head + (n * PACK_C,))
        out.append(lax.slice_in_dim(piece, 0, size, axis=lead).reshape(head + tuple(shp)))
        r0 += n
    return out


def _round_up(n, m):
    return -(-n // m) * m


def _to_shards(g, axis):
    shp = g.shape
    g = g.reshape(shp[:axis] + (N_CHIPS, shp[axis] // N_CHIPS) + shp[axis + 1:])
    return jnp.moveaxis(g, axis, 0)


def _from_shards(s, axis):
    s = jnp.moveaxis(s, 0, axis)
    shp = s.shape
    return s.reshape(shp[:axis] + (shp[axis] * shp[axis + 1],) + shp[axis + 2:])


_HBM = pl.BlockSpec(memory_space=pltpu.HBM)


def _place():
    x, y, c = lax.axis_index("x"), lax.axis_index("y"), lax.axis_index("c")
    chips = [(1 - x, y), (x, 1 - y), (1 - x, 1 - y)]
    return x, y, c, chips


def _rcopy(src, dst, send, recv, k, dev):
    return pltpu.make_async_remote_copy(src_ref=src, dst_ref=dst, send_sem=send.at[k], recv_sem=recv.at[k],
                                        device_id=dev, device_id_type=MESH)


def _comm_call(name, body, arrays, out_shapes, n_sems, n_local):
    return pl.pallas_call(
        body, name=name, in_specs=[_HBM] * len(arrays), out_specs=[_HBM] * len(out_shapes), out_shape=out_shapes,
        scratch_shapes=[pltpu.SemaphoreType.DMA((n_sems,)), pltpu.SemaphoreType.DMA((n_sems,)),
                        pltpu.SemaphoreType.DMA((max(n_local, 1),))],
    )(*arrays)


def _ici_peers(x, y, c):
    x_first = c == 0
    first = (jnp.where(x_first, 1 - x, x), jnp.where(x_first, y, 1 - y))
    second = (jnp.where(x_first, x, 1 - x), jnp.where(x_first, 1 - y, y))
    return first, second


def _ag_chips(name, packs):
    n = len(packs)

    def body(*refs):
        p, o, (send, recv, _) = refs[:n], refs[n:2 * n], refs[2 * n:]
        x, y, c, _ = _place()
        (ax, ay), (bx, by) = _ici_peers(x, y, c)
        me, na, nb, nd = 2 * x + y, 2 * ax + ay, 2 * bx + by, 2 * (1 - x) + (1 - y)
        sib = (x, y, 1 - c)
        own = [_rcopy(p[a].at[c], o[a].at[me, c], send, recv, 6 * a, (ax, ay, c)) for a in range(n)]
        own += [_rcopy(p[a].at[c], o[a].at[me, c], send, recv, 6 * a + 1, (bx, by, c)) for a in range(n)]
        for cp in own:
            cp.start()
        passed = []

        def landed(a, chip, k, fwd):
            blk = o[a].at[chip, c]
            _rcopy(blk, blk, send, recv, 6 * a + k, sib).wait_recv()
            for sem, dev in fwd:
                cp = _rcopy(blk, blk, send, recv, 6 * a + sem, dev)
                cp.start()
                passed.append(cp)

        for a in range(n):
            landed(a, na, 0, [(2, (bx, by, c)), (3, sib)])
        for a in range(n):
            landed(a, nb, 1, [(4, sib)])
        for a in range(n):
            landed(a, nd, 2, [(5, sib)])
        for a in range(n):
            for k, chip in ((3, nb), (4, na), (5, nd)):
                blk = o[a].at[chip, 1 - c]
                _rcopy(blk, blk, send, recv, 6 * a + k, sib).wait_recv()
        for cp in own + passed:
            cp.wait_send()

    outs = _comm_call(name, body, packs, [jax.ShapeDtypeStruct((N_CHIPS,) + t.shape, t.dtype) for t in packs], 6 * n, 0)
    chip = 2 * lax.axis_index("x") + lax.axis_index("y")
    return [lax.dynamic_update_slice_in_dim(o, t[None], chip, axis=0) for o, t in zip(outs, packs)]


def _pair_swap(name, gs):
    n = len(gs)
    ns = gs[0].shape[0]

    def body(*refs):
        g, r, (send, recv, _) = refs[:n], refs[n:2 * n], refs[2 * n:]
        x, y, c, _ = _place()
        cps = [_rcopy(g[a].at[j, 1 - c], r[a].at[j], send, recv, ns * a + j, (x, y, 1 - c))
               for a in range(n) for j in range(ns)]
        for cp in cps:
            cp.start()
        for cp in cps:
            cp.wait_recv()
        for cp in cps:
            cp.wait_send()

    return _comm_call(name, body, gs, [jax.ShapeDtypeStruct((ns,) + t.shape[2:], t.dtype) for t in gs], ns * n, 0)


def _halve_first(name, parts):
    n = len(parts)

    def body(*refs):
        p, r, (send, recv, _) = refs[:n], refs[n:2 * n], refs[2 * n:]
        x, y, c, _ = _place()
        (ax, ay), _ = _ici_peers(x, y, c)
        cps = [_rcopy(p[a].at[jnp.where(c == 0, 2 * ax + k, 2 * k + ay)], r[a].at[k], send, recv, 2 * a + k, (ax, ay, c))
               for a in range(n) for k in range(2)]
        for cp in cps:
            cp.start()
        for cp in cps:
            cp.wait_recv()
        for cp in cps:
            cp.wait_send()

    return _comm_call(name, body, parts, [jax.ShapeDtypeStruct((2,) + t.shape[1:], t.dtype) for t in parts], 2 * n, 0)


def _halve_second(name, kept):
    n = len(kept)

    def body(*refs):
        p, r, (send, recv, _) = refs[:n], refs[n:2 * n], refs[2 * n:]
        x, y, c, _ = _place()
        _, (bx, by) = _ici_peers(x, y, c)
        cps = [_rcopy(p[a].at[jnp.where(c == 0, by, bx)], r[a], send, recv, a, (bx, by, c)) for a in range(n)]
        for cp in cps:
            cp.start()
        for cp in cps:
            cp.wait_recv()
        for cp in cps:
            cp.wait_send()

    return _comm_call(name, body, kept, [jax.ShapeDtypeStruct(t.shape[1:], t.dtype) for t in kept], n, 0)


def _add_selected(name, t, r, index, out_dtype):
    single = r.ndim == 2
    r3 = r[None] if single else r
    nk, rh, cc = r3.shape
    tl = _tile(rh, ROW_TILE, 16)

    def body(i_ref, t_ref, r_ref, o_ref):
        o_ref[...] = (t_ref[...].astype(F32) + r_ref[...].astype(F32)).astype(o_ref.dtype)

    out = _pcall(
        body, name=name,
        grid_spec=pltpu.PrefetchScalarGridSpec(
            num_scalar_prefetch=1, grid=(nk, rh // tl),
            in_specs=[pl.BlockSpec((1, tl, cc), lambda k, i, i_ref: (i_ref[k], i, 0)),
                      pl.BlockSpec((1, tl, cc), lambda k, i, i_ref: (k, i, 0))],
            out_specs=pl.BlockSpec((1, tl, cc), lambda k, i, i_ref: (k, i, 0))),
        out_shape=jax.ShapeDtypeStruct(r3.shape, out_dtype),
        compiler_params=_cparams(("parallel", "parallel"), 8 * tl * cc * 4),
    )(index, t, r3)
    return out[0] if single else out


def _pair_share(name, reds):
    n = len(reds)

    def body(*refs):
        r, o, (send, recv, _) = refs[:n], refs[n:2 * n], refs[2 * n:]
        x, y, c, _ = _place()
        cps = [_rcopy(r[a], o[a].at[c], send, recv, a, (x, y, 1 - c)) for a in range(n)]
        for cp in cps:
            cp.start()
        for a in range(n):
            blk = o[a].at[1 - c]
            _rcopy(blk, blk, send, recv, a, (x, y, 1 - c)).wait_recv()
        for cp in cps:
            cp.wait_send()

    outs = _comm_call(name, body, reds, [jax.ShapeDtypeStruct((2,) + t.shape, t.dtype) for t in reds], n, 0)
    core = lax.axis_index("c")
    return [lax.dynamic_update_slice_in_dim(o, t[None], core, axis=0) for o, t in zip(outs, reds)]


def _add_own_half(name, g, r):
    ns, _, rh, cc = g.shape
    tl = _tile(rh, ROW_TILE, 16)
    core = lax.axis_index("c").astype(jnp.int32).reshape(1)

    def body(c_ref, g_ref, r_ref, o_ref):
        o_ref[...] = (g_ref[0].astype(F32) + r_ref[...].astype(F32)).astype(o_ref.dtype)

    return _pcall(
        body, name=name,
        grid_spec=pltpu.PrefetchScalarGridSpec(
            num_scalar_prefetch=1, grid=(ns, rh // tl),
            in_specs=[pl.BlockSpec((1, 1, tl, cc), lambda j, i, c_ref: (j, c_ref[0], i, 0)),
                      pl.BlockSpec((1, tl, cc), lambda j, i, c_ref: (j, i, 0))],
            out_specs=pl.BlockSpec((1, tl, cc), lambda j, i, c_ref: (j, i, 0))),
        out_shape=jax.ShapeDtypeStruct(r.shape, BF16),
        compiler_params=_cparams(("parallel", "parallel"), 8 * tl * cc * 4),
    )(core, g, r)


def _reduce_scatter(gs):
    x, y, c = lax.axis_index("x"), lax.axis_index("y"), lax.axis_index("c")
    mine_first = jnp.stack([jnp.where(c == 0, 2 * x + k, 2 * k + y) for k in range(2)]).astype(jnp.int32)
    mine_second = jnp.where(c == 0, y, x).astype(jnp.int32).reshape(1)
    swapped = _pair_swap("rs_pair_swap", gs)
    parts = [_add_own_half("rs_add_pair", g, r) for g, r in zip(gs, swapped)]
    got = _halve_first("rs_halve_first", parts)
    kept = [_add_selected("rs_add_first", t, r, mine_first, BF16) for t, r in zip(parts, got)]
    got = _halve_second("rs_halve_second", kept)
    reds = [_add_selected("rs_add_second", t, r, mine_second, F32) for t, r in zip(kept, got)]
    return _pair_share("rs_pair_share", reds)


def _adamw(name, g, w, m, v):
    c1 = 1.0 - ADAM_B1 ** ADAM_STEP
    c2 = 1.0 - ADAM_B2 ** ADAM_STEP
    nb, rows, cc = g.shape
    tl = _tile(rows, ROW_TILE, SUBLANE)

    def body(g_ref, w_ref, m_ref, v_ref, d_ref, mo_ref, vo_ref):
        gv = g_ref[...]
        mn = ADAM_B1 * m_ref[...] + (1.0 - ADAM_B1) * gv
        vn = ADAM_B2 * v_ref[...] + (1.0 - ADAM_B2) * (gv * gv)
        d_ref[...] = -ADAM_LR * ((mn / c1) / (jnp.sqrt(vn / c2) + ADAM_EPS) + ADAM_WD * w_ref[...])
        mo_ref[...] = mn
        vo_ref[...] = vn

    spec = pl.BlockSpec((1, tl, cc), lambda b, i: (b, i, 0))
    return _pcall(
        body, name=name, grid=(nb, rows // tl), in_specs=[spec] * 4, out_specs=[spec] * 3,
        out_shape=[jax.ShapeDtypeStruct(g.shape, F32)] * 3,
        compiler_params=_cparams(("parallel", "parallel"), 16 * tl * cc * 4),
    )(g, w, m, v)


def _train_step(a):
    x, target = a['x'][0], a['loss_target'][0]

    names = BF16_GATHER + F32_GATHER
    got = _ag_chips("ag_weights", [_stored(n, a[n]).astype(BF16) for n in BF16_GATHER] + [a[n] for n in F32_GATHER])
    ws = dict(zip(names, got))

    loss, gx, gs, grep = _device_step(x, target, ws, {n: a[n] for n in REPLICATED})
    loss = lax.psum(loss, ("x", "y", "c"))

    small = [n for n in MISC_SHARDED if n not in LARGE_SHARDED]
    small_shapes = [a[n].shape[1:] for n in small]
    rep_shapes = [a[n].shape for n in REPLICATED]
    small_rows = sum(_rows_of(s) for s in small_shapes)
    piece = _round_up(-(-sum(_rows_of(s) for s in rep_shapes) // (2 * N_CHIPS)), SUBLANE)
    rep_rows = 2 * N_CHIPS * piece
    rep_pack = _pack([grep[n] for n in REPLICATED], rep_rows).reshape(N_CHIPS, 2, piece, PACK_C)
    total = _round_up(small_rows + piece, ROW_TILE // 2)
    misc = _pack([gs[n] for n in small] + [rep_pack], total, lead=2)
    reds = _reduce_scatter([gs[n] for n in LARGE_SHARDED] + [misc])
    grad = dict(zip(LARGE_SHARDED, reds[:-1]))
    grad.update(zip(small, _unpack(reds[-1], small_shapes, lead=1)))
    rep_mine = reds[-1][:, small_rows:small_rows + piece]
    rep_all = _ag_chips("ag_replicated", [rep_mine])[0].reshape(1, rep_rows, PACK_C)

    outs = {}
    for n in SHARDED:
        shp = grad[n].shape
        three_d = (shp[0], -1, shp[-1])
        params = (grad[n], *(_stored(n, a[p + n]) for p in ('', 'm_', 'v_')))
        res3 = _adamw("adamw", *(t.reshape(three_d) for t in params))
        for kind, arr in zip(('grad', 'delta', 'new_m', 'new_v'), (grad[n], *res3)):
            outs[kind + '_' + n] = _stored(n, arr.reshape(shp))
    w, m, v = (_pack([a[p + n] for n in REPLICATED], rep_rows)[None] for p in ('', 'm_', 'v_'))
    res3 = _adamw("adamw_replicated", rep_all, w, m, v)
    for kind, arr in zip(('grad', 'delta', 'new_m', 'new_v'), (rep_all, *res3)):
        for n, t in zip(REPLICATED, _unpack(arr[0], rep_shapes)):
            outs[kind + '_' + n] = t
    res = [loss, gx[None]]
    for kind in ('grad', 'delta', 'new_m', 'new_v'):
        res += [outs[kind + '_' + n] for n in W_NAMES]
    return tuple(res)


def kernel(x, ffn1_norm, ffn1_w_gate, ffn1_w_up, ffn1_w_down, mix_norm, w_in, s5_lambda_re, s5_lambda_im, s5_log_dt, s5_b_re, s5_b_im, s5_c_re, s5_c_im, s5_d, s5_w_glu, gla_w_alpha, gla_b_alpha, gla_norm, attn_q_norm, attn_k_norm, w_branch_s5, w_branch_gla, w_branch_attn, w_merge_gate, b_merge_gate, w_out, ffn2_norm, ffn2_w_gate, ffn2_w_up, ffn2_w_down, final_norm, loss_target, m_ffn1_norm, m_ffn1_w_gate, m_ffn1_w_up, m_ffn1_w_down, m_mix_norm, m_w_in, m_s5_lambda_re, m_s5_lambda_im, m_s5_log_dt, m_s5_b_re, m_s5_b_im, m_s5_c_re, m_s5_c_im, m_s5_d, m_s5_w_glu, m_gla_w_alpha, m_gla_b_alpha, m_gla_norm, m_attn_q_norm, m_attn_k_norm, m_w_branch_s5, m_w_branch_gla, m_w_branch_attn, m_w_merge_gate, m_b_merge_gate, m_w_out, m_ffn2_norm, m_ffn2_w_gate, m_ffn2_w_up, m_ffn2_w_down, m_final_norm, v_ffn1_norm, v_ffn1_w_gate, v_ffn1_w_up, v_ffn1_w_down, v_mix_norm, v_w_in, v_s5_lambda_re, v_s5_lambda_im, v_s5_log_dt, v_s5_b_re, v_s5_b_im, v_s5_c_re, v_s5_c_im, v_s5_d, v_s5_w_glu, v_gla_w_alpha, v_gla_b_alpha, v_gla_norm, v_attn_q_norm, v_attn_k_norm, v_w_branch_s5, v_w_branch_gla, v_w_branch_attn, v_w_merge_gate, v_b_merge_gate, v_w_out, v_ffn2_norm, v_ffn2_w_gate, v_ffn2_w_up, v_ffn2_w_down, v_final_norm):
    return _train_step(dict(locals()))
```

```python
import functools
import math

import jax
import jax.numpy as jnp
import numpy as np
from jax import lax
from jax.experimental import pallas as pl
from jax.experimental.pallas import tpu as pltpu

F32 = jnp.float32
BF16 = jnp.bfloat16

D_MODEL = 1024
DEPTH = 2
EPS = 1e-6
S5_G, S5_H, S5_P = 32, 16, 64
S5_W = S5_G * S5_H
S5_N = S5_G * S5_P
S5_TG = 8
S5_TILES = S5_G // S5_TG
S5_TC, S5_TS = S5_TG * S5_H, S5_TG * S5_P
GLA_H, GLA_D = 4, 128
GLA_W = GLA_H * GLA_D
GLA_R = 16
GLA_TAU = 16.0
GLA_C = 64
AT_QH, AT_KH, AT_D = 8, 2, 64
AT_W = AT_QH * AT_D
AT_KW = AT_KH * AT_D
GRID_W = 64
ROPE_BASE = 10000.0
ZP_W = 3456
ZC_AK, ZC_AV, ZC_Z = 24, 25, 26

ADAM_LR, ADAM_B1, ADAM_B2, ADAM_EPS, ADAM_WD, ADAM_STEP = 0.001, 0.9, 0.999, 1e-08, 0.01, 10

GRAD_DT = BF16
VMEM_LIMIT_V7X = 56 * 1024 * 1024
LANE = 128
SUBLANE = 8

W_NAMES = ['ffn1_norm', 'ffn1_w_gate', 'ffn1_w_up', 'ffn1_w_down', 'mix_norm', 'w_in', 's5_lambda_re', 's5_lambda_im',
           's5_log_dt', 's5_b_re', 's5_b_im', 's5_c_re', 's5_c_im', 's5_d', 's5_w_glu', 'gla_w_alpha', 'gla_b_alpha',
           'gla_norm', 'attn_q_norm', 'attn_k_norm', 'w_branch_s5', 'w_branch_gla', 'w_branch_attn', 'w_merge_gate',
           'b_merge_gate', 'w_out', 'ffn2_norm', 'ffn2_w_gate', 'ffn2_w_up', 'ffn2_w_down', 'final_norm']
SHARD_AXIS = {'ffn1_w_gate': 2, 'ffn1_w_up': 2, 'ffn1_w_down': 1, 'w_in': 2, 's5_w_glu': 1, 'gla_w_alpha': 3,
              'gla_b_alpha': 2, 'w_branch_s5': 2, 'w_branch_gla': 2, 'w_branch_attn': 2, 'w_merge_gate': 2,
              'w_out': 1, 'ffn2_w_gate': 2, 'ffn2_w_up': 2, 'ffn2_w_down': 1}
F32_GATHER = ['gla_w_alpha', 'gla_b_alpha']
BF16_GATHER = [n for n in W_NAMES if n in SHARD_AXIS and n not in F32_GATHER]
SHARDED = [n for n in W_NAMES if n in SHARD_AXIS]
REPLICATED = [n for n in W_NAMES if n not in SHARD_AXIS]
N_CHIPS = 4
MESH = pl.DeviceIdType.MESH


def _tile(n, target, mult):
    best = None
    for t in range(mult, min(n, target) + 1, mult):
        if n % t == 0:
            best = t
    return best if best is not None else n


def _cparams(sem, vmem_bytes):
    limit = int(min(VMEM_LIMIT_V7X, max(32 * 1024 * 1024, vmem_bytes * 5 // 4)))
    return pltpu.CompilerParams(dimension_semantics=sem, vmem_limit_bytes=limit)


def _pcall(body, **kw):
    shapes = kw.pop('out_shape')
    many = isinstance(shapes, (list, tuple))
    pinned = [pltpu.HBM(t.shape, t.dtype) for t in (shapes if many else [shapes])]
    call = pl.pallas_call(body, out_shape=pinned if many else pinned[0], **kw)

    def run(*ops):
        return call(*[o if jnp.issubdtype(o.dtype, jnp.integer) else pltpu.with_memory_space_constraint(o, pltpu.HBM)
                      for o in ops])

    return run


def _cw(arr, cb, width):
    return (arr, cb, width)


def _win(a):
    if isinstance(a, tuple):
        return a
    return (a, 0, a.shape[-1])


def _nbytes(shape, dtype):
    return int(np.prod(shape)) * jnp.dtype(dtype).itemsize


def _rowwise(name, fn, rows, fulls, outs, accs=(), tl=256):
    rows = [_win(r) for r in rows]
    n_tok = rows[0][0].shape[0]
    tl = _tile(n_tok, tl, 16)
    nr, nf, no = len(rows), len(fulls), len(outs)

    def body(*refs):
        vals = [r[...] for r in refs[:nr + nf]]
        res = fn(*vals)
        if not isinstance(res, (tuple, list)):
            res = (res,)
        for k in range(no):
            o = refs[nr + nf + k]
            o[...] = res[k].astype(o.dtype)
        if accs:
            @pl.when(pl.program_id(0) == 0)
            def _():
                for k in range(len(accs)):
                    a = refs[nr + nf + no + k]
                    a[...] = jnp.zeros(a.shape, a.dtype)
            for k in range(len(accs)):
                a = refs[nr + nf + no + k]
                a[...] += res[no + k]

    in_specs = [pl.BlockSpec((tl, w), functools.partial(lambda i, cb: (i, cb), cb=cb)) for (_, cb, w) in rows]
    in_specs += [pl.BlockSpec(f.shape, functools.partial(lambda i, nd: (0,) * nd, nd=f.ndim)) for f in fulls]
    out_specs = [pl.BlockSpec((tl, c), lambda i: (i, 0)) for (c, _) in outs]
    out_specs += [pl.BlockSpec((r, c), lambda i: (0, 0)) for (r, c) in accs]
    out_shape = [jax.ShapeDtypeStruct((n_tok, c), dt) for (c, dt) in outs]
    out_shape += [jax.ShapeDtypeStruct((r, c), F32) for (r, c) in accs]
    vm = 2 * sum(_nbytes((tl, w), a.dtype) for (a, _, w) in rows)
    vm += 2 * sum(_nbytes(f.shape, f.dtype) for f in fulls)
    vm += 2 * sum(_nbytes((tl, c), dt) for (c, dt) in outs)
    vm += 8 * max([_nbytes((tl, w), F32) for (_, _, w) in rows] + [_nbytes((tl, c), F32) for (c, _) in outs])
    return _pcall(
        body, name=name, grid=(n_tok // tl,), in_specs=in_specs, out_specs=out_specs, out_shape=out_shape,
        compiler_params=_cparams(("arbitrary",) if accs else ("parallel",), vm),
    )(*[r[0] for r in rows], *fulls)


def _mm(name, a, b, *, ta=False, tb=False, out_dtype=F32, alpha=1.0, add=None, tm=512, tn=512):
    a_arr, a_cb, a_w = _win(a)
    b_arr, b_cb, b_w = _win(b)
    if ta:
        kdim, m = a_arr.shape[0], a_w
    else:
        m, kdim = a_arr.shape[0], a_w
    if tb:
        n, kb = b_arr.shape[0], b_w
    else:
        kb, n = b_arr.shape[0], b_w
    assert kdim == kb, (name, kdim, kb)
    if not ta and kdim <= 1024:
        tm = 2 * tm
    tm = _tile(m, tm, LANE if ta else 16)
    tn = _tile(n, tn, LANE)
    if ta:
        a_spec = pl.BlockSpec((kdim, tm), lambda i, j: (0, a_cb * (a_w // tm) + i))
        a_blk = (kdim, tm)
    else:
        a_spec = pl.BlockSpec((tm, kdim), lambda i, j: (i, a_cb))
        a_blk = (tm, kdim)
    if tb:
        b_spec = pl.BlockSpec((tn, kdim), lambda i, j: (j, b_cb))
        b_blk = (tn, kdim)
    else:
        b_spec = pl.BlockSpec((kdim, tn), lambda i, j: (0, b_cb * (b_w // tn) + j))
        b_blk = (kdim, tn)
    dims = (((0 if ta else 1,), (1 if tb else 0,)), ((), ()))
    has_add = add is not None

    def body(*refs):
        a_ref, b_ref = refs[0], refs[1]
        o_ref = refs[-1]
        acc = lax.dot_general(a_ref[...].astype(BF16), b_ref[...].astype(BF16), dims, preferred_element_type=F32)
        if alpha != 1.0:
            acc = acc * alpha
        if has_add:
            acc = acc + refs[2][...].astype(F32)
        o_ref[...] = acc.astype(o_ref.dtype)

    in_specs = [a_spec, b_spec]
    ops = [a_arr, b_arr]
    if has_add:
        in_specs.append(pl.BlockSpec((tm, tn), lambda i, j: (i, j)))
        ops.append(add)
    vm = 2 * (_nbytes(a_blk, a_arr.dtype) + _nbytes(b_blk, b_arr.dtype) + _nbytes((tm, tn), out_dtype))
    vm += _nbytes(a_blk, BF16) + _nbytes(b_blk, BF16) + 3 * _nbytes((tm, tn), F32)
    return _pcall(
        body, name=name, grid=(m // tm, n // tn), in_specs=in_specs,
        out_specs=pl.BlockSpec((tm, tn), lambda i, j: (i, j)),
        out_shape=jax.ShapeDtypeStruct((m, n), out_dtype),
        compiler_params=_cparams(("parallel", "parallel"), vm),
    )(*ops)


def _sigmoid(x):
    return 0.5 * jnp.tanh(0.5 * x) + 0.5


def _rms(x, gain):
    r = lax.rsqrt(jnp.mean(x * x, axis=-1, keepdims=True) + EPS)
    return x * r * gain


def _rms_bwd(x, gain, dh):
    r = lax.rsqrt(jnp.mean(x * x, axis=-1, keepdims=True) + EPS)
    xh = x * r
    dg = jnp.sum(dh * xh, axis=0, keepdims=True)
    dxh = dh * gain
    dx = r * (dxh - xh * jnp.mean(dxh * xh, axis=-1, keepdims=True))
    return dx, dg


_GELU_C = math.sqrt(2.0 / math.pi)


def _gelu(y):
    return 0.5 * y * (1.0 + jnp.tanh(_GELU_C * (y + 0.044715 * y * y * y)))


def _gelu_grad(y):
    th = jnp.tanh(_GELU_C * (y + 0.044715 * y * y * y))
    return 0.5 * (1.0 + th) + 0.5 * y * (1.0 - th * th) * _GELU_C * (1.0 + 3.0 * 0.044715 * y * y)


def _log_sigmoid(x):
    return jnp.minimum(x, 0.0) - jnp.log(1.0 + jnp.exp(-jnp.abs(x)))


def _dot(a, b, dims=(((1,), (0,)), ((), ())), precision=None):
    return lax.dot_general(a, b, dims, preferred_element_type=F32, precision=precision)


def _dot_exact(x, m, dims=(((1,), (0,)), ((), ())), terms=3, m_first=False):
    m16 = m.astype(BF16)
    acc, rest = None, x
    for k in range(terms):
        piece = rest.astype(BF16)
        t = _dot(m16, piece, dims) if m_first else _dot(piece, m16, dims)
        acc = t if acc is None else acc + t
        if k + 1 < terms:
            rest = rest - piece.astype(F32)
    return acc


_NT = (((1,), (1,)), ((), ()))
_TN = (((0,), (0,)), ((), ()))


def _scan_core(b_ref, are_ref, aim_ref, pw, last, of32, *, rev, xf32=None, da_ref=None, lc=512):
    nb, nc = b_ref.shape[0] // SUBLANE, S5_N // lc
    row = lax.broadcasted_iota(jnp.int32, (SUBLANE, lc), 0)

    def cmul(ar, ai, xr, xi):
        return ar * xr - ai * xi, ar * xi + ai * xr

    def hs(c, xr, xi):
        cs = pl.ds(c * lc, lc)
        for k, slot in ((1, 0), (2, 2), (4, 4)):
            ar, ai = pw[slot, :, cs], pw[slot + 1, :, cs]
            if rev:
                sr, si = pltpu.roll(xr, SUBLANE - k, 0), pltpu.roll(xi, SUBLANE - k, 0)
                keep = row < SUBLANE - k
            else:
                sr, si = pltpu.roll(xr, k, 0), pltpu.roll(xi, k, 0)
                keep = row >= k
            sr, si = jnp.where(keep, sr, 0.0), jnp.where(keep, si, 0.0)
            pr, pi = cmul(ar, ai, sr, si)
            xr, xi = xr + pr, xi + pi
        return xr, xi

    @pl.when(pl.program_id(0) == 0)
    def _():
        last[...] = jnp.zeros(last.shape, F32)
        if da_ref is not None:
            da_ref[...] = jnp.zeros(da_ref.shape, F32)
        for c in range(nc):
            cs = pl.ds(c * lc, lc)
            a1r = jnp.broadcast_to(are_ref[:, cs], (SUBLANE, lc))
            a1i = jnp.broadcast_to(aim_ref[:, cs], (SUBLANE, lc))
            a2r, a2i = cmul(a1r, a1i, a1r, a1i)
            a4r, a4i = cmul(a2r, a2i, a2r, a2i)
            for slot, v in enumerate((a1r, a1i, a2r, a2i, a4r, a4i)):
                pw[slot, :, cs] = v
            first = SUBLANE - 1 if rev else 0
            pr, pi = hs(c, jnp.where(row == first, a1r, 0.0), jnp.where(row == first, a1i, 0.0))
            pw[6, :, cs] = pr
            pw[7, :, cs] = pi

    edge = 0 if rev else SUBLANE - 1

    def blk(j, carry):
        jj = (nb - 1 - j) if rev else j
        r0 = pl.multiple_of(jj * SUBLANE, SUBLANE)
        for c in range(nc):
            cre, cim = pl.ds(c * lc, lc), pl.ds(S5_N + c * lc, lc)
            xr, xi = hs(c, b_ref[pl.ds(r0, SUBLANE), cre], b_ref[pl.ds(r0, SUBLANE), cim])
            cr = jnp.broadcast_to(last[edge:edge + 1, cre], (SUBLANE, lc))
            ci = jnp.broadcast_to(last[edge:edge + 1, cim], (SUBLANE, lc))
            pr, pi = cmul(pw[6, :, cre], pw[7, :, cre], cr, ci)
            xr, xi = xr + pr, xi + pi
            if da_ref is not None:
                if rev:
                    qr, qi = pltpu.roll(xr, SUBLANE - 1, 0), pltpu.roll(xi, SUBLANE - 1, 0)
                    fill = row == SUBLANE - 1
                else:
                    qr, qi = pltpu.roll(xr, 1, 0), pltpu.roll(xi, 1, 0)
                    fill = row == 0
                qr, qi = jnp.where(fill, cr, qr), jnp.where(fill, ci, qi)
                zr, zi = xf32[pl.ds(r0, SUBLANE), cre], xf32[pl.ds(r0, SUBLANE), cim]
                da_ref[:, cre] += zr * qr + zi * qi
                da_ref[:, cim] += zr * qi - zi * qr
            last[:, cre] = xr
            last[:, cim] = xi
            of32[pl.ds(r0, SUBLANE), cre] = xr
            of32[pl.ds(r0, SUBLANE), cim] = xi
        return carry

    lax.fori_loop(0, nb, blk, 0)


def _s5_tiles():
    return [(slice((j % S5_TILES) * S5_TC, (j % S5_TILES + 1) * S5_TC), slice(j * S5_TS, (j + 1) * S5_TS))
            for j in range(2 * S5_TILES)]


def _s5_scratch(tb):
    return [pltpu.VMEM((8, SUBLANE, S5_N), F32), pltpu.VMEM((SUBLANE, 2 * S5_N), F32),
            pltpu.VMEM((tb, 2 * S5_N), F32), pltpu.VMEM((tb, 2 * S5_N), F32),
            pltpu.VMEM((S5_TC, 2 * S5_N), BF16), pltpu.VMEM((2 * S5_N, S5_TC), BF16)]


def _s5_fwd(name, z, a_re, a_im, wb, wc, add, *, rev, tb=512):
    n_tok = z.shape[0]
    tb = _tile(n_tok, tb, 16)
    nt = n_tok // tb
    has_add = add is not None

    def body(*refs):
        u_ref, are_ref, aim_ref, wb_ref, wc_ref = refs[:5]
        x_ref, y_ref, pw, last, of32, bu, wb16, wc16 = refs[5 + has_add:]

        @pl.when(pl.program_id(0) == 0)
        def _():
            wb16[...] = wb_ref[...].astype(BF16)
            wc16[...] = wc_ref[...].astype(BF16)

        u16 = u_ref[...].astype(BF16)
        for ch, st in _s5_tiles():
            bu[:, st] = _dot(u16[:, ch], wb16[:, st])
        _scan_core(bu, are_ref, aim_ref, pw, last, of32, rev=rev)
        x16 = of32[...].astype(BF16)
        x_ref[...] = x16
        ys = []
        for k in range(S5_TILES):
            re, im = slice(k * S5_TS, (k + 1) * S5_TS), slice(S5_N + k * S5_TS, S5_N + (k + 1) * S5_TS)
            ys.append(_dot(x16[:, re], wc16[re, :]) + _dot(x16[:, im], wc16[im, :]))
        y = jnp.concatenate(ys, axis=1)
        y_ref[...] = y + refs[5][...] if has_add else y

    tmap = (lambda t: (nt - 1 - t, 0)) if rev else (lambda t: (t, 0))
    const = lambda t: (0, 0)
    in_specs = [pl.BlockSpec((tb, S5_W), tmap), pl.BlockSpec((1, S5_N), const), pl.BlockSpec((1, S5_N), const),
                pl.BlockSpec((S5_TC, 2 * S5_N), const), pl.BlockSpec((2 * S5_N, S5_TC), const)]
    ops = [z, a_re, a_im, wb, wc]
    if has_add:
        in_specs.append(pl.BlockSpec((tb, S5_W), tmap))
        ops.append(add)
    return _pcall(
        body, name=name, grid=(nt,), in_specs=in_specs,
        out_specs=[pl.BlockSpec((tb, 2 * S5_N), tmap), pl.BlockSpec((tb, S5_W), tmap)],
        out_shape=[jax.ShapeDtypeStruct((n_tok, 2 * S5_N), BF16), jax.ShapeDtypeStruct((n_tok, S5_W), F32)],
        scratch_shapes=_s5_scratch(tb), compiler_params=_cparams(("arbitrary",), 10 * tb * 2 * S5_N * 4),
    )(*ops)


def _s5_bwd(name, dy, z, xs, a_re, a_im, wb, wc, add, *, rev, tb=256):
    n_tok = z.shape[0]
    tb = _tile(n_tok, tb, 16)
    nt = n_tok // tb

    def body(dy_ref, u_ref, xs_ref, are_ref, aim_ref, wb_ref, wc_ref, add_ref, du_ref, dwb_ref, dwc_ref, da_ref,
             pw, last, of32, gx, wb16, wc16, xf32):
        @pl.when(pl.program_id(0) == 0)
        def _():
            wb16[...] = wb_ref[...].astype(BF16)
            wc16[...] = wc_ref[...].astype(BF16)
            dwb_ref[...] = jnp.zeros(dwb_ref.shape, F32)
            dwc_ref[...] = jnp.zeros(dwc_ref.shape, F32)

        dy16, u16, x16 = dy_ref[...].astype(BF16), u_ref[...].astype(BF16), xs_ref[...]
        for ch, st in _s5_tiles():
            gx[:, st] = _dot(dy16[:, ch], wc16[st, :], _NT)
        xf32[...] = x16.astype(F32)
        _scan_core(gx, are_ref, aim_ref, pw, last, of32, rev=rev, xf32=xf32, da_ref=da_ref)
        g16 = of32[...].astype(BF16)
        dus = []
        for k in range(S5_TILES):
            re, im = slice(k * S5_TS, (k + 1) * S5_TS), slice(S5_N + k * S5_TS, S5_N + (k + 1) * S5_TS)
            dus.append(_dot(g16[:, re], wb16[:, re], _NT) + _dot(g16[:, im], wb16[:, im], _NT))
        du_ref[...] = jnp.concatenate(dus, axis=1) + add_ref[...]
        for ch, st in _s5_tiles():
            dwb_ref[:, st] += _dot(u16[:, ch], g16[:, st], _TN)
            dwc_ref[st, :] += _dot(x16[:, st], dy16[:, ch], _TN)

    tmap = (lambda t: (nt - 1 - t, 0)) if rev else (lambda t: (t, 0))
    const = lambda t: (0, 0)
    row_w, row_s = pl.BlockSpec((tb, S5_W), tmap), pl.BlockSpec((tb, 2 * S5_N), tmap)
    return _pcall(
        body, name=name, grid=(nt,),
        in_specs=[row_w, row_w, row_s, pl.BlockSpec((1, S5_N), const), pl.BlockSpec((1, S5_N), const),
                  pl.BlockSpec((S5_TC, 2 * S5_N), const), pl.BlockSpec((2 * S5_N, S5_TC), const), row_w],
        out_specs=[row_w, pl.BlockSpec((S5_TC, 2 * S5_N), const), pl.BlockSpec((2 * S5_N, S5_TC), const),
                   pl.BlockSpec((SUBLANE, 2 * S5_N), const)],
        out_shape=[jax.ShapeDtypeStruct((n_tok, S5_W), F32), jax.ShapeDtypeStruct((S5_TC, 2 * S5_N), F32),
                   jax.ShapeDtypeStruct((2 * S5_N, S5_TC), F32), jax.ShapeDtypeStruct((SUBLANE, 2 * S5_N), F32)],
        scratch_shapes=_s5_scratch(tb) + [pltpu.VMEM((tb, 2 * S5_N), F32)],
        compiler_params=_cparams(("arbitrary",), 12 * tb * 2 * S5_N * 4),
    )(dy, z, xs, a_re, a_im, wb, wc, add)


GLA_TB = 512


def _gla_block_terms(la, q, k, rev):
    trif = _chunk_mask(rev).astype(F32)
    chunks = [la[t:t + GLA_C] for t in range(0, la.shape[0], GLA_C)]
    bcum = jnp.concatenate([_dot_exact(c, trif, m_first=True) for c in chunks], axis=0)
    blast = _per_chunk_sum(la)
    eb, enb, ee = jnp.exp(bcum), jnp.exp(-bcum), jnp.exp(blast - bcum)
    return trif, jnp.exp(blast), eb, enb, ee, q * eb, k * enb, k * ee


def _per_chunk_sum(t):
    return jnp.concatenate([jnp.broadcast_to(jnp.sum(t[r:r + GLA_C], axis=0, keepdims=True), (GLA_C, t.shape[1]))
                            for r in range(0, t.shape[0], GLA_C)], axis=0)


def _chunk_mask(rev):
    ri = lax.broadcasted_iota(jnp.int32, (GLA_C, GLA_C), 0)
    ci = lax.broadcasted_iota(jnp.int32, (GLA_C, GLA_C), 1)
    return (ri <= ci) if rev else (ri >= ci)


def _gla_fwd(name, z, la, *, rev):
    n_tok = z.shape[0]
    tb = _tile(n_tok, GLA_TB, GLA_C)
    nblk, per = n_tok // tb, tb // GLA_C
    scale = GLA_D ** -0.5

    def body(q_ref, k_ref, v_ref, la_ref, o_ref, s_ref, st, qd_s, kd_s, ke_s, v_s, dec_s):
        @pl.when(pl.program_id(0) == 0)
        def _():
            st[...] = jnp.zeros(st.shape, F32)

        _, dec, _, _, _, qd, kd, ke = _gla_block_terms(la_ref[...], q_ref[...].astype(F32) * scale,
                                                                 k_ref[...].astype(F32), rev)
        qd_s[...], kd_s[...], ke_s[...] = qd.astype(BF16), kd.astype(BF16), ke.astype(BF16)
        v_s[...] = v_ref[...].astype(BF16)
        dec_s[...] = dec
        tri = _chunk_mask(rev)
        for step in range(per):
            t = per - 1 - step if rev else step
            rows = pl.ds(t * GLA_C, GLA_C)
            for h in range(GLA_H):
                lanes = pl.ds(h * GLA_D, GLA_D)
                qd16, v16 = qd_s[rows, lanes], v_s[rows, lanes]
                sc = jnp.where(tri, _dot(qd16, kd_s[rows, lanes], _NT), 0.0)
                s16 = st[h].astype(BF16)
                s_ref[h, step] = s16
                o_ref[rows, lanes] = _dot(sc.astype(BF16), v16) + _dot(qd16, s16, _NT)
                st[h] = st[h] * dec_s[pl.ds(t * GLA_C, 1), lanes] + _dot(v16, ke_s[rows, lanes], _TN)

    bmap = (lambda n: nblk - 1 - n) if rev else (lambda n: n)

    def zspec(cb):
        return pl.BlockSpec((tb, GLA_W), lambda n: (bmap(n), cb))

    half = pltpu.VMEM((tb, GLA_W), BF16)
    return _pcall(
        body, name=name, grid=(nblk,),
        in_specs=[zspec(1), zspec(2), zspec(3), zspec(0)],
        out_specs=[zspec(0), pl.BlockSpec((GLA_H, per, GLA_D, GLA_D), lambda n: (0, n, 0, 0))],
        out_shape=[jax.ShapeDtypeStruct((n_tok, GLA_W), F32),
                   jax.ShapeDtypeStruct((GLA_H, n_tok // GLA_C, GLA_D, GLA_D), BF16)],
        scratch_shapes=[pltpu.VMEM((GLA_H, GLA_D, GLA_D), F32), half, half, half, half, pltpu.VMEM((tb, GLA_W), F32)],
        compiler_params=_cparams(("arbitrary",), 40 * tb * GLA_W * 4),
    )(z, z, z, la)


def _gla_bwd(name, z, la, s_prev, do, *, rev):
    n_tok = z.shape[0]
    tb = _tile(n_tok, GLA_TB, GLA_C)
    nblk, per = n_tok // tb, tb // GLA_C
    scale = GLA_D ** -0.5

    def body(q_ref, k_ref, v_ref, la_ref, s_ref, do_ref, dq_ref, dk_ref, dv_ref, dla_ref,
             dst, qd_s, kd_s, ke_s, v_s, do_s, dec_s, dqd_s, dkd_s, dke_s, dd_s):
        @pl.when(pl.program_id(0) == 0)
        def _():
            dst[...] = jnp.zeros(dst.shape, F32)

        trif, dec, eb, enb, ee, qd, kd, ke = _gla_block_terms(la_ref[...], q_ref[...].astype(F32) * scale,
                                                                 k_ref[...].astype(F32), rev)
        qd_s[...], kd_s[...], ke_s[...] = qd.astype(BF16), kd.astype(BF16), ke.astype(BF16)
        v_s[...] = v_ref[...].astype(BF16)
        do_s[...] = do_ref[...].astype(BF16)
        dec_s[...] = dec
        tri = _chunk_mask(rev)
        for step in range(per):
            t = step if rev else per - 1 - step
            rows = pl.ds(t * GLA_C, GLA_C)
            for h in range(GLA_H):
                lanes = pl.ds(h * GLA_D, GLA_D)
                qd16, kd16, ke16 = qd_s[rows, lanes], kd_s[rows, lanes], ke_s[rows, lanes]
                v16, do16 = v_s[rows, lanes], do_s[rows, lanes]
                sc16 = jnp.where(tri, _dot(qd16, kd16, _NT), 0.0).astype(BF16)
                dsc16 = jnp.where(tri, _dot(do16, v16, _NT), 0.0).astype(BF16)
                s16 = s_ref[h, per - 1 - step]
                dsn = dst[h]
                dsn16 = dsn.astype(BF16)
                dec_c = dec_s[pl.ds(t * GLA_C, 1), lanes]
                dqd_s[rows, lanes] = _dot(dsc16, kd16) + _dot(do16, s16)
                dkd_s[rows, lanes] = _dot(dsc16, qd16, _TN)
                dke_s[rows, lanes] = _dot(v16, dsn16)
                dv_ref[rows, lanes] = _dot(sc16, do16, _TN) + _dot(ke16, dsn16, _NT)
                ddec = jnp.sum(dsn * s16.astype(F32), axis=0, keepdims=True) * dec_c
                dd_s[rows, lanes] = jnp.broadcast_to(ddec, (GLA_C, GLA_D))
                dst[h] = dsn * dec_c + _dot(do16, qd16, _TN)
        dqd, dkd, dke = dqd_s[...], dkd_s[...], dke_s[...]
        dq_ref[...] = dqd * eb * scale
        dk_ref[...] = dkd * enb + dke * ee
        db = dqd * qd - dkd * kd - dke * ke
        dcum = jnp.concatenate([_dot_exact(db[r:r + GLA_C], trif, _TN, m_first=True) for r in range(0, tb, GLA_C)], axis=0)
        dla_ref[...] = dcum + _per_chunk_sum(dke * ke) + dd_s[...]

    bmap = (lambda n: n) if rev else (lambda n: nblk - 1 - n)

    def zspec(cb):
        return pl.BlockSpec((tb, GLA_W), lambda n: (bmap(n), cb))

    o4 = jax.ShapeDtypeStruct((n_tok, GLA_W), F32)
    half, full = pltpu.VMEM((tb, GLA_W), BF16), pltpu.VMEM((tb, GLA_W), F32)
    return _pcall(
        body, name=name, grid=(nblk,),
        in_specs=[zspec(1), zspec(2), zspec(3), zspec(0),
                  pl.BlockSpec((GLA_H, per, GLA_D, GLA_D), lambda n: (0, nblk - 1 - n, 0, 0)), zspec(0)],
        out_specs=[zspec(0)] * 4, out_shape=[o4] * 4,
        scratch_shapes=[pltpu.VMEM((GLA_H, GLA_D, GLA_D), F32), half, half, half, half, half, full, full, full, full, full],
        compiler_params=_cparams(("arbitrary",), 64 * tb * GLA_W * 4),
    )(z, z, z, la, s_prev, do)


_PAIRS = AT_W // LANE
_PAIRS_PER_KV = _PAIRS // AT_KH


def _attn_fwd(name, q, kd, vd, tq=256):
    n_tok = q.shape[0]
    tq = _tile(n_tok, tq, 16)

    def body(q_ref, k_ref, v_ref, o_ref, l_ref):
        qv, kv, vv = q_ref[...], k_ref[...], v_ref[...]
        low = lax.broadcasted_iota(jnp.int32, qv.shape, 1) < AT_D
        res = []
        for keep in (low, jnp.logical_not(low)):
            s = _dot(jnp.where(keep, qv, jnp.zeros_like(qv)), kv, _NT)
            m = jnp.max(s, axis=-1, keepdims=True)
            p = jnp.exp(s - m)
            den = jnp.sum(p, axis=-1, keepdims=True)
            res.append((_dot(p.astype(BF16), vv) / den, m + jnp.log(den)))
        o_ref[...] = jnp.where(low, res[0][0], res[1][0])
        l_ref[...] = jnp.where(low, res[0][1], res[1][1])

    qspec = pl.BlockSpec((tq, LANE), lambda p, i: (i, p))
    kspec = pl.BlockSpec((n_tok, LANE), lambda p, i: (0, p // _PAIRS_PER_KV))
    vm = 5 * _nbytes((tq, n_tok), F32) + 8 * _nbytes((n_tok, LANE), BF16)
    out = jax.ShapeDtypeStruct((n_tok, AT_W), F32)
    return _pcall(
        body, name=name, grid=(_PAIRS, n_tok // tq), in_specs=[qspec, kspec, kspec], out_specs=[qspec, qspec],
        out_shape=[out, out], compiler_params=_cparams(("parallel", "parallel"), vm),
    )(q, kd, vd)


def _attn_bwd(name, q, kd, vd, o, lse, do, tq=256):
    n_tok = q.shape[0]
    tq = _tile(n_tok, tq, 16)

    def body(q_ref, k_ref, v_ref, o_ref, l_ref, do_ref, dq_ref, dk_ref, dv_ref):
        @pl.when((pl.program_id(1) == 0) & (pl.program_id(2) == 0))
        def _():
            dk_ref[...] = jnp.zeros(dk_ref.shape, F32)
            dv_ref[...] = jnp.zeros(dv_ref.shape, F32)

        qv, kv, vv = q_ref[...], k_ref[...], v_ref[...]
        dof, lv = do_ref[...], l_ref[...]
        do16 = dof.astype(BF16)
        doo = dof * o_ref[...]
        low = lax.broadcasted_iota(jnp.int32, qv.shape, 1) < AT_D
        dq = jnp.zeros(qv.shape, F32)
        for keep in (low, jnp.logical_not(low)):
            qm = jnp.where(keep, qv, jnp.zeros_like(qv))
            dom = jnp.where(keep, do16, jnp.zeros_like(do16))
            lh = jnp.max(jnp.where(keep, lv, -1e30), axis=-1, keepdims=True)
            p = jnp.exp(_dot(qm, kv, _NT) - lh)
            delta = jnp.sum(jnp.where(keep, doo, 0.0), axis=-1, keepdims=True)
            ds16 = (p * (_dot(dom, vv, _NT) - delta)).astype(BF16)
            dq = jnp.where(keep, _dot(ds16, kv), dq)
            dk_ref[...] += _dot(ds16, qm, _TN)
            dv_ref[...] += _dot(p.astype(BF16), dom, _TN)
        dq_ref[...] = dq

    qspec = pl.BlockSpec((tq, LANE), lambda g, j, i: (i, g * _PAIRS_PER_KV + j))
    kspec = pl.BlockSpec((n_tok, LANE), lambda g, j, i: (0, g))
    vm = 7 * _nbytes((tq, n_tok), F32) + 12 * _nbytes((n_tok, LANE), F32)
    dup = jax.ShapeDtypeStruct((n_tok, AT_KH * LANE), F32)
    return _pcall(
        body, name=name, grid=(AT_KH, _PAIRS_PER_KV, n_tok // tq),
        in_specs=[qspec, kspec, kspec, qspec, qspec, qspec], out_specs=[qspec, kspec, kspec],
        out_shape=[jax.ShapeDtypeStruct((n_tok, AT_W), F32), dup, dup],
        compiler_params=_cparams(("parallel", "arbitrary", "arbitrary"), vm),
    )(q, kd, vd, o, lse, do)


def _s5_prep(lam_re, lam_im, log_dt, b_re, b_im, c_re, c_im):
    dt = jnp.exp(log_dt)[:, None]
    er = jnp.exp(lam_re * dt)
    a_re, a_im = er * jnp.cos(lam_im * dt), er * jnp.sin(lam_im * dt)
    nr, ni = a_re - 1.0, a_im
    den = lam_re * lam_re + lam_im * lam_im
    f_re, f_im = (nr * lam_re + ni * lam_im) / den, (ni * lam_re - nr * lam_im) / den
    bb_re = f_re[..., None] * b_re - f_im[..., None] * b_im
    bb_im = f_re[..., None] * b_im + f_im[..., None] * b_re
    def tiles(t, axis):
        _, a, b = t.shape
        eye = jnp.eye(S5_TG, dtype=t.dtype)
        blocks = t.reshape(S5_TILES, S5_TG, a, 1, b) * eye[None, :, None, :, None]
        if axis == 0:
            return blocks.reshape(S5_TILES * S5_TG * a, S5_TG * b)
        return blocks.transpose(1, 2, 0, 3, 4).reshape(S5_TG * a, S5_TILES * S5_TG * b)

    w_in = jnp.concatenate([tiles(bb_re.transpose(0, 2, 1), 1), tiles(bb_im.transpose(0, 2, 1), 1)], axis=1)
    w_out = jnp.concatenate([tiles(c_re.transpose(0, 2, 1), 0), -tiles(c_im.transpose(0, 2, 1), 0)], axis=0)
    return a_re.reshape(1, S5_N), a_im.reshape(1, S5_N), w_in, w_out


def _rope_tables(n_tok):
    quarter = AT_D // 4
    d = np.arange(AT_D)
    e = d % (AT_D // 2)
    inv = (ROPE_BASE ** (-(e % quarter).astype(np.float64) * 2.0 / (AT_D // 2))).astype(np.float32)
    sign = np.where(e < quarter, -1.0, 1.0).astype(np.float32)
    rot = np.zeros((AT_D, AT_D), np.float32)
    rot[np.where(e < quarter, d + quarter, d - quarter), d] = 1.0
    pos = jnp.arange(n_tok, dtype=jnp.int32)
    axis_pos = jnp.where(jnp.asarray(d // (AT_D // 2) == 0)[None, :], (pos // GRID_W)[:, None], (pos % GRID_W)[:, None])
    ang = axis_pos.astype(F32) * jnp.asarray(inv)[None, :]
    return jnp.cos(ang), jnp.sin(ang) * jnp.asarray(sign)[None, :], rot


def _head_consts(n_tok, heads):
    cos, sin, rot = _rope_tables(n_tok)
    eye = np.eye(heads, dtype=np.float32)
    mean = np.kron(eye, np.full((AT_D, AT_D), 1.0 / AT_D, np.float32))
    return jnp.tile(cos, (1, heads)), jnp.tile(sin, (1, heads)), jnp.asarray(mean), jnp.asarray(np.kron(eye, rot))


def _pad_w_in(wt):
    return jnp.concatenate([wt[:2560], wt[2592:3360], wt[2560:2592], jnp.zeros((96, wt.shape[1]), wt.dtype)], axis=0)


def _unpad_w_in(gt):
    return jnp.concatenate([gt[:2560], gt[3328:3360], gt[2560:3328]], axis=0)


def _ffn_fwd(x, g, wg, wu, wd, d):
    n_tok = x.shape[0]
    fq = wd.shape[-2]
    tm = _tile(n_tok, 512, 16)
    w_bytes = N_CHIPS * fq * D_MODEL * 2

    def up(x_ref, g_ref, wg_ref, wu_ref, h_ref, a_ref, b_ref, s_ref):
        hv = _rms(x_ref[...], g_ref[...]).astype(BF16)
        h_ref[...] = hv
        for j in range(N_CHIPS):
            av, bv = _dot(hv, wg_ref[j, 0], _NT), _dot(hv, wu_ref[j, 0], _NT)
            a_ref[j] = av.astype(BF16)
            b_ref[j] = bv.astype(BF16)
            s_ref[j] = (av * _sigmoid(av) * bv).astype(BF16)

    wspec = pl.BlockSpec((N_CHIPS, 1, fq, D_MODEL), lambda i: (0, d, 0, 0))
    aspec = pl.BlockSpec((N_CHIPS, tm, fq), lambda i: (0, i, 0))
    xspec = pl.BlockSpec((tm, D_MODEL), lambda i: (i, 0))
    act = jax.ShapeDtypeStruct((N_CHIPS, n_tok, fq), BF16)
    h, a, b, s = _pcall(
        up, name="ffn_up", grid=(n_tok // tm,),
        in_specs=[xspec, pl.BlockSpec((1, D_MODEL), lambda i: (0, 0)), wspec, wspec],
        out_specs=[xspec, aspec, aspec, aspec],
        out_shape=[jax.ShapeDtypeStruct((n_tok, D_MODEL), BF16), act, act, act],
        compiler_params=_cparams(("parallel",), 4 * w_bytes + 6 * N_CHIPS * tm * fq * 2 + 12 * tm * fq * 4
                                 + 8 * tm * D_MODEL * 4),
    )(x, g, wg, wu)

    def down(s_ref, w_ref, x_ref, o_ref):
        acc = _dot(s_ref[0], w_ref[0, 0])
        for j in range(1, N_CHIPS):
            acc = acc + _dot(s_ref[j], w_ref[j, 0])
        o_ref[...] = x_ref[...] + 0.5 * acc

    xo = _pcall(
        down, name="ffn_down", grid=(n_tok // tm,), in_specs=[aspec, wspec, xspec], out_specs=xspec,
        out_shape=jax.ShapeDtypeStruct((n_tok, D_MODEL), F32),
        compiler_params=_cparams(("parallel",), 2 * w_bytes + 2 * N_CHIPS * tm * fq * 2 + 8 * tm * D_MODEL * 4),
    )(s, wd, x)
    return xo, (x, h, a, b, s)


def _ffn_bwd(dy, dy16, saved, g, wg, wu, wd, d, into):
    x, h, a, b, s = saved
    n_tok = x.shape[0]
    fq = wd.shape[-2]
    tm = _tile(n_tok, 512, 16)

    ta_ = _tile(n_tok, 512, 16)

    def dact(dy_ref, w_ref, a_ref, b_ref, da_ref, db_ref):
        dyv = dy_ref[...]
        for j in range(N_CHIPS):
            ds = 0.5 * _dot(dyv, w_ref[j, 0], _NT)
            av, bv = a_ref[j].astype(F32), b_ref[j].astype(F32)
            sg = _sigmoid(av)
            da_ref[j] = (ds * bv * sg * (1.0 + av * (1.0 - sg))).astype(BF16)
            db_ref[j] = (ds * av * sg).astype(BF16)

    aspec = pl.BlockSpec((N_CHIPS, ta_, fq), lambda i: (0, i, 0))
    act = jax.ShapeDtypeStruct((N_CHIPS, n_tok, fq), BF16)
    da, db = _pcall(
        dact, name="ffn_dact", grid=(n_tok // ta_,),
        in_specs=[pl.BlockSpec((ta_, D_MODEL), lambda i: (i, 0)),
                  pl.BlockSpec((N_CHIPS, 1, fq, D_MODEL), lambda i: (0, d, 0, 0)), aspec, aspec],
        out_specs=[aspec, aspec], out_shape=[act, act],
        compiler_params=_cparams(("parallel",), 2 * N_CHIPS * fq * D_MODEL * 2 + 8 * N_CHIPS * ta_ * fq * 2
                                 + 12 * ta_ * fq * 4),
    )(dy16, wd, a, b)

    any_spec = pl.BlockSpec(memory_space=pl.ANY)
    alias = {} if into is None else {2: 0}
    tn = _tile(D_MODEL, 512, LANE)

    def dw_call(t, rhs, scale, prev):
        def dw(t_ref, r_ref, *rest):
            rest[-1][0, 0] = (scale * _dot(t_ref[0], r_ref[...], _TN)).astype(GRAD_DT)

        return _pcall(
            dw, name="ffn_dw", grid=(N_CHIPS, D_MODEL // tn),
            in_specs=[pl.BlockSpec((1, n_tok, fq), lambda j, n: (j, 0, 0)), pl.BlockSpec((n_tok, tn), lambda j, n: (0, n))]
            + ([] if prev is None else [any_spec]),
            out_specs=pl.BlockSpec((1, 1, fq, tn), lambda j, n: (j, d, 0, n)),
            out_shape=jax.ShapeDtypeStruct((N_CHIPS, DEPTH, fq, D_MODEL), GRAD_DT), input_output_aliases=alias,
            compiler_params=_cparams(("parallel", "parallel"), 6 * n_tok * (fq + tn) * 2 + 4 * fq * tn * 4),
        )(t, rhs, *([] if prev is None else [prev]))

    dwg = dw_call(da, h, 1.0, None if into is None else into[0])
    dwu = dw_call(db, h, 1.0, None if into is None else into[1])
    dwd = dw_call(s, dy16, 0.5, None if into is None else into[2])
    tr = _tile(n_tok, 256, 16)

    def dh_norm(da_ref, db_ref, wg_ref, wu_ref, dy_ref, x_ref, g_ref, dx_ref, dx16_ref, dg_ref):
        dh = _dot(da_ref[0], wg_ref[0, 0]) + _dot(db_ref[0], wu_ref[0, 0])
        for j in range(1, N_CHIPS):
            dh = dh + _dot(da_ref[j], wg_ref[j, 0]) + _dot(db_ref[j], wu_ref[j, 0])
        dx, dg = _rms_bwd(x_ref[...], g_ref[...], dh)
        dx = dy_ref[...] + dx
        dx_ref[...] = dx
        dx16_ref[...] = dx.astype(BF16)

        @pl.when(pl.program_id(0) == 0)
        def _():
            dg_ref[...] = jnp.zeros(dg_ref.shape, F32)

        dg_ref[...] += dg

    tspec = pl.BlockSpec((N_CHIPS, tr, fq), lambda i: (0, i, 0))
    wspec = pl.BlockSpec((N_CHIPS, 1, fq, D_MODEL), lambda i: (0, d, 0, 0))
    rspec = pl.BlockSpec((tr, D_MODEL), lambda i: (i, 0))
    vspec = pl.BlockSpec((1, D_MODEL), lambda i: (0, 0))
    dx, dx16, dg = _pcall(
        dh_norm, name="ffn_dh_norm", grid=(n_tok // tr,),
        in_specs=[tspec, tspec, wspec, wspec, rspec, rspec, vspec], out_specs=[rspec, rspec, vspec],
        out_shape=[jax.ShapeDtypeStruct((n_tok, D_MODEL), F32), jax.ShapeDtypeStruct((n_tok, D_MODEL), BF16),
                   jax.ShapeDtypeStruct((1, D_MODEL), F32)],
        compiler_params=_cparams(("arbitrary",), 4 * N_CHIPS * D_MODEL * fq * 2 + 4 * N_CHIPS * tr * fq * 2
                                 + 16 * tr * D_MODEL * 4),
    )(da, db, wg, wu, dy, x, g)
    return dx, dx16, dg[0], (dwg, dwu, dwd)


def _alpha_pads(w_alpha):
    z = jnp.zeros((LANE - 2 * GLA_R, GLA_W), F32)
    wf = jnp.concatenate([w_alpha[0], jnp.zeros((GLA_R, GLA_W), F32), z], axis=0)
    wb = jnp.concatenate([jnp.zeros((GLA_R, GLA_W), F32), w_alpha[1], z], axis=0)
    return wf, wb


def _dup_heads(t):
    low = lax.broadcasted_iota(jnp.int32, t.shape, 1) < AT_D
    h0, h1 = jnp.where(low, t, 0.0), jnp.where(low, 0.0, t)
    return jnp.concatenate([h0 + pltpu.roll(h0, AT_D, 1), h1 + pltpu.roll(h1, AT_D, 1)], axis=1)


def _fold_heads(t):
    low = lax.broadcasted_iota(jnp.int32, (t.shape[0], LANE), 1) < AT_D
    g0, g1 = t[:, :LANE], t[:, LANE:]
    return jnp.where(low, g0 + pltpu.roll(g0, AT_D, 1), g1 + pltpu.roll(g1, AT_D, 1))


def _mixer_fwd(x, w):
    n_tok = x.shape[0]
    wip, wmg = w['w_in_p'], w['w_merge_gate']
    tr = _tile(n_tok, 256, 16)

    def project(x_ref, g_ref, wip_ref, wmg_ref, h_ref, z_ref, gp_ref):
        hv = _rms(x_ref[...], g_ref[...]).astype(BF16)
        h_ref[...] = hv
        z_ref[...] = _dot(hv, wip_ref[...], _NT).astype(BF16)
        gp_ref[...] = _dot(hv, wmg_ref[...]).astype(BF16)

    def rows(c):
        return pl.BlockSpec((tr, c), lambda i: (i, 0))

    def whole(t):
        return pl.BlockSpec(t.shape, lambda i: (0, 0))

    h, z, gp = _pcall(
        project, name="mix_in", grid=(n_tok // tr,),
        in_specs=[rows(D_MODEL), whole(w['mix_norm']), whole(wip), whole(wmg)],
        out_specs=[rows(D_MODEL), rows(ZP_W), rows(3 * D_MODEL)],
        out_shape=[jax.ShapeDtypeStruct((n_tok, D_MODEL), BF16), jax.ShapeDtypeStruct((n_tok, ZP_W), BF16),
                   jax.ShapeDtypeStruct((n_tok, 3 * D_MODEL), BF16)],
        compiler_params=_cparams(("parallel",), 4 * (wip.size + wmg.size) + 16 * tr * (ZP_W + 3 * D_MODEL)),
    )(x, w['mix_norm'], wip, wmg)
    u = _cw(z, 0, S5_W)

    xs, yy = [], None
    for d, rev in ((0, False), (1, True)):
        a_re, a_im, wb, wc = w['s5'][d]
        x_d, yy = _s5_fwd("s5_rev" if rev else "s5_fwd", z, a_re, a_im, wb, wc, yy, rev=rev)
        xs.append(x_d)

    def s5_post(yv, uv, dv, wglu):
        y0 = yv + dv * uv
        y1 = _gelu(y0)
        t = _dot(y1.astype(BF16), wglu)
        return y0, t, y1 * _sigmoid(t)

    y0, t_glu, y_s5 = _rowwise("s5_post", s5_post, [yy, u], [w['s5_d'], w['s5_w_glu']],
                               [(S5_W, F32), (S5_W, F32), (S5_W, BF16)])

    wa_f, wa_b = _alpha_pads(w['gla_w_alpha'])
    ba_f, ba_b = w['gla_b_alpha'][0:1], w['gla_b_alpha'][1:2]
    zz = _cw(z, ZC_Z, LANE)

    def gla_prep(zv, waf, wab, bf, bb):
        z16 = zv.astype(BF16)
        lf = _dot(z16, waf.astype(BF16)) + bf
        lb = _dot(z16, wab.astype(BF16)) + bb
        return _log_sigmoid(lf) / GLA_TAU, _log_sigmoid(lb) / GLA_TAU

    la_f, la_b = _rowwise("gla_prep", gla_prep, [zz], [wa_f, wa_b, ba_f, ba_b], [(GLA_W, F32), (GLA_W, F32)])
    o_f, s_f = _gla_fwd("gla_fwd", z, la_f, rev=False)
    o_b, s_b = _gla_fwd("gla_rev", z, la_b, rev=True)
    gate = _cw(z, 4, GLA_W)

    def gla_post(of, ob, gv, gn):
        o, gv = of + ob, gv.astype(F32)
        outs = []
        for hh in range(GLA_H):
            sl = slice(hh * GLA_D, (hh + 1) * GLA_D)
            gh = gv[:, sl]
            outs.append(_rms(o[:, sl], gn) * (gh * _sigmoid(gh)))
        return jnp.concatenate(outs, axis=1)

    y_gla = _rowwise("gla_post", gla_post, [o_f, o_b, gate], [w['gla_norm']], [(GLA_W, BF16)])[0]

    cq, sq, mq, rq = _head_consts(n_tok, AT_QH)
    ck, sk, mk, rk = _head_consts(n_tok, AT_KH)
    gq = jnp.tile(w['attn_q_norm'], (1, AT_QH))
    gk = jnp.tile(w['attn_k_norm'], (1, AT_KH))

    def rope_fwd(v, gain, cos, sin, mean, rot, scale):
        r = lax.rsqrt(_dot_exact(v * v, mean, terms=2) + EPS)
        vn = v * r * gain
        return (vn * cos + _dot_exact(vn, rot, terms=2) * sin) * scale

    def attn_prep(qv, kv, vv, cqv, sqv, ckv, skv, gqv, gkv, mqv, rqv, mkv, rkv):
        qv, kv, vv = qv.astype(F32), kv.astype(F32), vv.astype(F32)
        kr = rope_fwd(kv, gkv, ckv, skv, mkv, rkv, 1.0)
        return rope_fwd(qv, gqv, cqv, sqv, mqv, rqv, AT_D ** -0.5), _dup_heads(kr), _dup_heads(vv)

    q_raw, k_raw, v_raw = _cw(z, 5, AT_W), _cw(z, ZC_AK, AT_KW), _cw(z, ZC_AV, AT_KW)
    q16, kd, vd = _rowwise("attn_prep", attn_prep, [q_raw, k_raw, v_raw, cq, sq, ck, sk], [gq, gk, mq, rq, mk, rk],
                           [(AT_W, BF16), (2 * LANE, BF16), (2 * LANE, BF16)])
    y_at, lse = _attn_fwd("attn_fwd", q16, kd, vd)

    p0 = _mm("mix_branch", y_s5, w['w_branch_s5'], out_dtype=BF16)
    p1 = _mm("mix_branch", y_gla, w['w_branch_gla'], out_dtype=BF16)
    p2 = _mm("mix_branch_f32", y_at, w['w_branch_attn'], out_dtype=BF16)

    def merge(gpv, a0, a1, a2, bias):
        g = _sigmoid(gpv.astype(F32) + bias)
        return (g[:, :D_MODEL] * a0.astype(F32) + g[:, D_MODEL:2 * D_MODEL] * a1.astype(F32)
                + g[:, 2 * D_MODEL:] * a2.astype(F32))

    merged = _rowwise("mix_merge", merge, [gp, p0, p1, p2], [w['b_merge_gate']], [(D_MODEL, BF16)])[0]
    xo = _mm("mix_out", merged, w['w_out'], add=x)
    saved = dict(x=x, h=h, z=z, gp=gp, xs=xs, y0=y0, t_glu=t_glu, y_s5=y_s5, la=(la_f, la_b), o=(o_f, o_b), s=(s_f, s_b),
                 y_gla=y_gla, q16=q16, kd=kd, vd=vd, lse=lse, y_at=y_at, p=(p0, p1, p2), merged=merged)
    return xo, saved


def _mixer_bwd(dy, dy16, sv, w, s5_raw):
    z, h = sv['z'], sv['h']
    n_tok = z.shape[0]
    grads = {}
    dmerged = _mm("mix_dmerged", dy16, w['w_out'], tb=True, out_dtype=BF16)
    grads['w_out'] = _mm("mix_dwout", sv['merged'], dy16, ta=True, out_dtype=GRAD_DT)

    def dmerge(dm, gpv, a0, a1, a2, bias):
        dm = dm.astype(F32)
        g = _sigmoid(gpv.astype(F32) + bias)
        ps = (a0.astype(F32), a1.astype(F32), a2.astype(F32))
        dps, dgs = [], []
        for k in range(3):
            gk = g[:, k * D_MODEL:(k + 1) * D_MODEL]
            dps.append(dm * gk)
            dgs.append(dm * ps[k] * gk * (1.0 - gk))
        dgp = jnp.concatenate(dgs, axis=1)
        return dps[0], dps[1], dps[2], dgp, jnp.sum(dgp, axis=0, keepdims=True)

    dp0, dp1, dp2, dgp, dbm = _rowwise("mix_dmerge", dmerge, [dmerged, sv['gp'], *sv['p']], [w['b_merge_gate']],
                                       [(D_MODEL, BF16)] * 3 + [(3 * D_MODEL, BF16)], accs=[(1, 3 * D_MODEL)])
    grads['b_merge_gate'] = dbm[0]
    grads['w_branch_s5'] = _mm("mix_dbranch", sv['y_s5'], dp0, ta=True, out_dtype=GRAD_DT)
    grads['w_branch_gla'] = _mm("mix_dbranch", sv['y_gla'], dp1, ta=True, out_dtype=GRAD_DT)
    grads['w_branch_attn'] = _mm("mix_dbranch_f32", sv['y_at'], dp2, ta=True, out_dtype=GRAD_DT)
    dy_s5 = _mm("mix_dy", dp0, w['w_branch_s5'], tb=True)
    dy_gla = _mm("mix_dy", dp1, w['w_branch_gla'], tb=True)
    dy_at = _mm("mix_dy", dp2, w['w_branch_attn'], tb=True)
    grads['w_merge_gate'] = _mm("mix_dwgate", h, dgp, ta=True, out_dtype=GRAD_DT)
    u = _cw(z, 0, S5_W)

    def s5_post_bwd(dyv, y0, t, uv, dv, wglu):
        sg = _sigmoid(t)
        y1 = _gelu(y0)
        dt = dyv * y1 * sg * (1.0 - sg)
        dy1 = dyv * sg + _dot(dt.astype(BF16), wglu, _NT)
        dy0 = dy1 * _gelu_grad(y0)
        return dy0, dt, y1, dy0 * dv, jnp.sum(dy0 * uv, axis=0, keepdims=True)

    dy0, dt_glu, y1, du, dd = _rowwise("s5_post_bwd", s5_post_bwd, [dy_s5, sv['y0'], sv['t_glu'], u],
                                       [w['s5_d'], w['s5_w_glu']],
                                       [(S5_W, BF16), (S5_W, BF16), (S5_W, BF16), (S5_W, F32)], accs=[(1, S5_W)])
    grads['s5_d'] = dd[0]
    grads['s5_w_glu'] = _mm("s5_dwglu", y1, dt_glu, ta=True, out_dtype=GRAD_DT)
    s5_cot = []
    for d, rev in ((0, False), (1, True)):
        a_re, a_im, wb, wc = w['s5'][d]
        du, dwb, dwc, da = _s5_bwd("s5_adj_fwd" if rev else "s5_adj_rev", dy0, z, sv['xs'][d], a_re, -a_im, wb, wc, du,
                                   rev=not rev)
        da = jnp.sum(da, axis=0, keepdims=True)
        s5_cot.append((da[:, :S5_N], da[:, S5_N:], dwb, dwc))
    for d in range(2):
        _, pull = jax.vjp(_s5_prep, *s5_raw[d])
        s5_cot[d] = pull(s5_cot[d])
    for k, nm in enumerate(('s5_lambda_re', 's5_lambda_im', 's5_log_dt', 's5_b_re', 's5_b_im', 's5_c_re', 's5_c_im')):
        grads[nm] = jnp.stack([s5_cot[0][k], s5_cot[1][k]])

    gate = _cw(z, 4, GLA_W)

    def gla_post_bwd(dyv, of, ob, gv, gn):
        o, gv = of + ob, gv.astype(F32)
        dos, dgates = [], []
        dgn = jnp.zeros((1, GLA_D), F32)
        for hh in range(GLA_H):
            sl = slice(hh * GLA_D, (hh + 1) * GLA_D)
            gh, dyh = gv[:, sl], dyv[:, sl]
            sg = _sigmoid(gh)
            dgates.append(dyh * _rms(o[:, sl], gn) * sg * (1.0 + gh * (1.0 - sg)))
            dx, dg = _rms_bwd(o[:, sl], gn, dyh * gh * sg)
            dos.append(dx)
            dgn = dgn + dg
        return jnp.concatenate(dos, axis=1), jnp.concatenate(dgates, axis=1), dgn

    do, dgate, dgn = _rowwise("gla_post_bwd", gla_post_bwd, [dy_gla, *sv['o'], gate], [w['gla_norm']],
                              [(GLA_W, F32), (GLA_W, F32)], accs=[(1, GLA_D)])
    grads['gla_norm'] = dgn[0]
    dq_f, dk_f, dv_f, dla_f = _gla_bwd("gla_bwd_fwd", z, sv['la'][0], sv['s'][0], do, rev=False)
    dq_b, dk_b, dv_b, dla_b = _gla_bwd("gla_bwd_rev", z, sv['la'][1], sv['s'][1], do, rev=True)
    wa_f, wa_b = _alpha_pads(w['gla_w_alpha'])
    ba_f, ba_b = w['gla_b_alpha'][0:1], w['gla_b_alpha'][1:2]
    zz = _cw(z, ZC_Z, LANE)

    def gla_prep_bwd(dlf, dlb, zv, waf, wab, bf, bb):
        z16 = zv.astype(BF16)
        waf16, wab16 = waf.astype(BF16), wab.astype(BF16)
        xf = _dot(z16, waf16) + bf
        xb = _dot(z16, wab16) + bb
        df = dlf * (_sigmoid(-xf) / GLA_TAU)
        db = dlb * (_sigmoid(-xb) / GLA_TAU)
        dz = _dot(df.astype(BF16), waf16, _NT) + _dot(db.astype(BF16), wab16, _NT)
        return df, db, dz, jnp.sum(df, axis=0, keepdims=True), jnp.sum(db, axis=0, keepdims=True)

    dlog_f, dlog_b, dzz, dba_f, dba_b = _rowwise(
        "gla_prep_bwd", gla_prep_bwd, [dla_f, dla_b, zz], [wa_f, wa_b, ba_f, ba_b],
        [(GLA_W, BF16), (GLA_W, BF16), (LANE, F32)], accs=[(1, GLA_W), (1, GLA_W)])
    dwa_f = _mm("gla_dwa", zz, dlog_f, ta=True)
    dwa_b = _mm("gla_dwa", zz, dlog_b, ta=True)
    grads['gla_w_alpha'] = jnp.stack([dwa_f[:GLA_R], dwa_b[GLA_R:2 * GLA_R]])
    grads['gla_b_alpha'] = jnp.concatenate([dba_f, dba_b], axis=0)

    dq_r, dkd, dvd = _attn_bwd("attn_bwd", sv['q16'], sv['kd'], sv['vd'], sv['y_at'], sv['lse'], dy_at)
    cq, sq, mq, rq = _head_consts(n_tok, AT_QH)
    ck, sk, mk, rk = _head_consts(n_tok, AT_KH)
    gq = jnp.tile(w['attn_q_norm'], (1, AT_QH))
    gk = jnp.tile(w['attn_k_norm'], (1, AT_KH))

    def rope_bwd(dr, v, gain, cos, sin, mean, rot, scale):
        dr = dr * scale
        dvn = dr * cos + _dot_exact(dr * sin, rot, _NT, terms=2)
        r = lax.rsqrt(_dot_exact(v * v, mean, terms=2) + EPS)
        vh = v * r
        dg = jnp.sum(dvn * vh, axis=0, keepdims=True)
        dvh = dvn * gain
        return r * (dvh - vh * _dot_exact(dvh * vh, mean, terms=2)), dg

    def attn_prep_bwd(dq, dk2, dv2, qv, kv, cqv, sqv, ckv, skv, gqv, gkv, mqv, rqv, mkv, rkv):
        qv, kv = qv.astype(F32), kv.astype(F32)
        dqo, dgq = rope_bwd(dq, qv, gqv, cqv, sqv, mqv, rqv, AT_D ** -0.5)
        dko, dgk = rope_bwd(_fold_heads(dk2), kv, gkv, ckv, skv, mkv, rkv, 1.0)
        return dqo, dko, _fold_heads(dv2), dgq, dgk

    q_raw, k_raw = _cw(z, 5, AT_W), _cw(z, ZC_AK, AT_KW)
    daq, dak, dav, dgq, dgk = _rowwise("attn_prep_bwd", attn_prep_bwd, [dq_r, dkd, dvd, q_raw, k_raw, cq, sq, ck, sk],
                                       [gq, gk, mq, rq, mk, rk], [(AT_W, F32), (AT_KW, F32), (AT_KW, F32)],
                                       accs=[(1, AT_W), (1, AT_KW)])
    grads['attn_q_norm'] = dgq.reshape(AT_QH, AT_D).sum(axis=0)
    grads['attn_k_norm'] = dgk.reshape(AT_KH, AT_D).sum(axis=0)

    def assemble(duv, qf, qb, kf, kb, vf, vb, dg, aq, ak, av, zzv):
        return jnp.concatenate([duv, qf + qb, kf + kb, vf + vb, dg, aq, ak, av, zzv], axis=1)

    dz = _rowwise("mix_dz", assemble, [du, dq_f, dq_b, dk_f, dk_b, dv_f, dv_b, dgate, daq, dak, dav, dzz], [],
                  [(ZP_W, BF16)])[0]
    grads['w_in'] = _unpad_w_in(_mm("mix_dwin", dz, h, ta=True, out_dtype=GRAD_DT))
    wmg, wip = w['w_merge_gate'], w['w_in_p']
    tr = _tile(n_tok, 256, 16)

    def dh_norm(dgp_ref, dz_ref, wmg_ref, wip_ref, dy_ref, x_ref, g_ref, dx_ref, dx16_ref, dg_ref):
        dh = _dot(dgp_ref[...], wmg_ref[...], _NT) + _dot(dz_ref[...], wip_ref[...])
        dxv, dgv = _rms_bwd(x_ref[...], g_ref[...], dh)
        dxv = dy_ref[...] + dxv
        dx_ref[...] = dxv
        dx16_ref[...] = dxv.astype(BF16)

        @pl.when(pl.program_id(0) == 0)
        def _():
            dg_ref[...] = jnp.zeros(dg_ref.shape, F32)

        dg_ref[...] += dgv

    def rows(c):
        return pl.BlockSpec((tr, c), lambda i: (i, 0))

    def whole(t):
        return pl.BlockSpec(t.shape, lambda i: (0, 0))

    dx, dx16, dg = _pcall(
        dh_norm, name="mix_dh_norm", grid=(n_tok // tr,),
        in_specs=[rows(dgp.shape[1]), rows(dz.shape[1]), whole(wmg), whole(wip), rows(D_MODEL), rows(D_MODEL),
                  whole(w['mix_norm'])],
        out_specs=[rows(D_MODEL), rows(D_MODEL), whole(w['mix_norm'])],
        out_shape=[jax.ShapeDtypeStruct((n_tok, D_MODEL), F32), jax.ShapeDtypeStruct((n_tok, D_MODEL), BF16),
                   jax.ShapeDtypeStruct((1, D_MODEL), F32)],
        compiler_params=_cparams(("arbitrary",), 4 * (wmg.size + wip.size) + 8 * tr * (dgp.shape[1] + dz.shape[1])
                                 + 16 * tr * D_MODEL * 4),
    )(dgp, dz, wmg, wip, dy, sv['x'], w['mix_norm'])
    grads['mix_norm'] = dg[0]
    return dx, dx16, grads


FFN_NAMES = [p + k for p in ('ffn1_', 'ffn2_') for k in ('w_gate', 'w_up', 'w_down')]
MISC_SHARDED = [n for n in SHARDED if n not in FFN_NAMES]
LARGE_SHARDED = FFN_NAMES + ['w_in', 'w_merge_gate', 'w_out', 'w_branch_s5', 'w_branch_gla', 'w_branch_attn', 's5_w_glu']
KEPT_TRANSPOSED = [p + k for p in ('ffn1_', 'ffn2_') for k in ('w_gate', 'w_up')] + ['w_in']


def _stored_axis(n):
    return 1 if n in KEPT_TRANSPOSED else SHARD_AXIS[n]


def _stored(n, t):
    return jnp.swapaxes(t, 1, 2) if n in KEPT_TRANSPOSED else t


def _layer_weights(ws, rep, i):
    w = {k: _from_shards(ws[k][:, i], _stored_axis(k) - 1) for k in MISC_SHARDED}
    for k in ('ffn1_norm', 'mix_norm', 'ffn2_norm', 's5_d', 'gla_norm', 'attn_q_norm', 'attn_k_norm', 'b_merge_gate'):
        w[k] = rep[k][i].reshape(1, -1)
    w['w_in_p'] = _pad_w_in(w['w_in'])
    s5_raw = [tuple(rep[k][i, d] for k in ('s5_lambda_re', 's5_lambda_im', 's5_log_dt', 's5_b_re', 's5_b_im',
                                            's5_c_re', 's5_c_im')) for d in range(2)]
    w['s5'] = [_s5_prep(*s5_raw[d]) for d in range(2)]
    return w, s5_raw


def _device_step(x, target, ws, rep):
    lw = [_layer_weights(ws, rep, i) for i in range(DEPTH)]
    ffn = {p: (ws[p + 'w_gate'], ws[p + 'w_up'], ws[p + 'w_down']) for p in ('ffn1_', 'ffn2_')}
    saved = []
    for i in range(DEPTH):
        w, _ = lw[i]
        x, s1 = _ffn_fwd(x, w['ffn1_norm'], *ffn['ffn1_'], i)
        x, s2 = _mixer_fwd(x, w)
        x, s3 = _ffn_fwd(x, w['ffn2_norm'], *ffn['ffn2_'], i)
        saved.append((s1, s2, s3))
    gfin = rep['final_norm'].reshape(1, -1)

    def head(xv, tv, gv):
        e = _rms(xv, gv) - tv
        dx, dg = _rms_bwd(xv, gv, e * (1.0 / D_MODEL))
        part = jnp.sum(e * e, axis=0, keepdims=True)
        return dx, dx, dg, part

    dx, dx16, dgfin, part = _rowwise("loss_head", head, [x, target], [gfin], [(D_MODEL, F32), (D_MODEL, BF16)],
                                     accs=[(1, D_MODEL), (1, D_MODEL)])
    loss = (0.5 / D_MODEL) * jnp.sum(part)
    per_layer = []
    dffn = {'ffn1_': None, 'ffn2_': None}
    for i in reversed(range(DEPTH)):
        w, s5_raw = lw[i]
        s1, s2, s3 = saved[i]
        dx, dx16, n3, dffn['ffn2_'] = _ffn_bwd(dx, dx16, s3, w['ffn2_norm'], *ffn['ffn2_'], i, dffn['ffn2_'])
        dx, dx16, g = _mixer_bwd(dx, dx16, s2, w, s5_raw)
        dx, dx16, n1, dffn['ffn1_'] = _ffn_bwd(dx, dx16, s1, w['ffn1_norm'], *ffn['ffn1_'], i, dffn['ffn1_'])
        g['ffn2_norm'], g['ffn1_norm'] = n3, n1
        per_layer.append(g)
    per_layer.reverse()
    stacked = {k: jnp.stack([per_layer[i][k] for i in range(DEPTH)]) for k in per_layer[0]}
    gs = {k: _to_shards(stacked[k], _stored_axis(k)) for k in MISC_SHARDED}
    for p in ('ffn1_', 'ffn2_'):
        gs[p + 'w_gate'], gs[p + 'w_up'], gs[p + 'w_down'] = dffn[p]
    grep = {k: stacked[k] for k in REPLICATED if k != 'final_norm'}
    grep['final_norm'] = dgfin[0]
    return loss, dx, gs, grep


PACK_C = 1024
ROW_TILE = 512


def _rows_of(shape):
    return -(-int(np.prod(shape)) // PACK_C)


def _as_rows(t, lead=0):
    head = t.shape[:lead]
    flat = t.reshape(head + (-1,))
    rows = -(-flat.shape[-1] // PACK_C)
    pad = rows * PACK_C - flat.shape[-1]
    if pad:
        flat = jnp.pad(flat, [(0, 0)] * lead + [(0, pad)])
    return flat.reshape(head + (rows, PACK_C))


def _pack(items, total_rows, lead=0):
    parts = [_as_rows(t, lead) for t in items]
    used = sum(p.shape[lead] for p in parts)
    if total_rows > used:
        parts.append(jnp.zeros(parts[0].shape[:lead] + (total_rows - used, PACK_C), parts[0].dtype))
    return jnp.concatenate(parts, axis=lead)


def _unpack(rows, shapes, lead=0):
    out, r0 = [], 0
    head = rows.shape[:lead]
    for shp in shapes:
        n, size = _rows_of(shp), int(np.prod(shp))
        piece = lax.slice_in_dim(rows, r0, r0 + n, axis=lead).reshape(head + (n * PACK_C,))
        out.append(lax.slice_in_dim(piece, 0, size, axis=lead).reshape(head + tuple(shp)))
        r0 += n
    return out


def _round_up(n, m):
    return -(-n // m) * m


def _to_shards(g, axis):
    shp = g.shape
    g = g.reshape(shp[:axis] + (N_CHIPS, shp[axis] // N_CHIPS) + shp[axis + 1:])
    return jnp.moveaxis(g, axis, 0)


def _from_shards(s, axis):
    s = jnp.moveaxis(s, 0, axis)
    shp = s.shape
    return s.reshape(shp[:axis] + (shp[axis] * shp[axis + 1],) + shp[axis + 2:])


_HBM = pl.BlockSpec(memory_space=pltpu.HBM)


def _place():
    return lax.axis_index("x"), lax.axis_index("y"), lax.axis_index("c")


def _rcopy(src, dst, send, recv, k, dev):
    return pltpu.make_async_remote_copy(src_ref=src, dst_ref=dst, send_sem=send.at[k], recv_sem=recv.at[k],
                                        device_id=dev, device_id_type=MESH)


def _comm_call(name, body, arrays, out_shapes, n_sems, n_local):
    return pl.pallas_call(
        body, name=name, in_specs=[_HBM] * len(arrays), out_specs=[_HBM] * len(out_shapes), out_shape=out_shapes,
        scratch_shapes=[pltpu.SemaphoreType.DMA((n_sems,)), pltpu.SemaphoreType.DMA((n_sems,)),
                        pltpu.SemaphoreType.DMA((max(n_local, 1),))],
    )(*arrays)


def _ici_peers(x, y, c):
    x_first = c == 0
    first = (jnp.where(x_first, 1 - x, x), jnp.where(x_first, y, 1 - y))
    second = (jnp.where(x_first, x, 1 - x), jnp.where(x_first, 1 - y, y))
    return first, second


def _ag_chips(name, packs):
    n = len(packs)

    def body(*refs):
        p, o, (send, recv, _) = refs[:n], refs[n:2 * n], refs[2 * n:]
        x, y, c = _place()
        (ax, ay), (bx, by) = _ici_peers(x, y, c)
        me, na, nb, nd = 2 * x + y, 2 * ax + ay, 2 * bx + by, 2 * (1 - x) + (1 - y)
        sib = (x, y, 1 - c)
        own = [_rcopy(p[a].at[c], o[a].at[me, c], send, recv, 6 * a, (ax, ay, c)) for a in range(n)]
        own += [_rcopy(p[a].at[c], o[a].at[me, c], send, recv, 6 * a + 1, (bx, by, c)) for a in range(n)]
        for cp in own:
            cp.start()
        passed = []

        def landed(a, chip, k, fwd):
            blk = o[a].at[chip, c]
            _rcopy(blk, blk, send, recv, 6 * a + k, sib).wait_recv()
            for sem, dev in fwd:
                cp = _rcopy(blk, blk, send, recv, 6 * a + sem, dev)
                cp.start()
                passed.append(cp)

        for a in range(n):
            landed(a, na, 0, [(2, (bx, by, c)), (3, sib)])
        for a in range(n):
            landed(a, nb, 1, [(4, sib)])
        for a in range(n):
            landed(a, nd, 2, [(5, sib)])
        for a in range(n):
            for k, chip in ((3, nb), (4, na), (5, nd)):
                blk = o[a].at[chip, 1 - c]
                _rcopy(blk, blk, send, recv, 6 * a + k, sib).wait_recv()
        for cp in own + passed:
            cp.wait_send()

    outs = _comm_call(name, body, packs, [jax.ShapeDtypeStruct((N_CHIPS,) + t.shape, t.dtype) for t in packs], 6 * n, 0)
    chip = 2 * lax.axis_index("x") + lax.axis_index("y")
    return [lax.dynamic_update_slice_in_dim(o, t[None], chip, axis=0) for o, t in zip(outs, packs)]


def _pair_swap(name, gs):
    n = len(gs)
    ns = gs[0].shape[0]

    def body(*refs):
        g, r, (send, recv, _) = refs[:n], refs[n:2 * n], refs[2 * n:]
        x, y, c = _place()
        cps = [_rcopy(g[a].at[j, 1 - c], r[a].at[j], send, recv, ns * a + j, (x, y, 1 - c))
               for a in range(n) for j in range(ns)]
        for cp in cps:
            cp.start()
        for cp in cps:
            cp.wait_recv()
        for cp in cps:
            cp.wait_send()

    return _comm_call(name, body, gs, [jax.ShapeDtypeStruct((ns,) + t.shape[2:], t.dtype) for t in gs], ns * n, 0)


def _halve_first(name, parts):
    n = len(parts)

    def body(*refs):
        p, r, (send, recv, _) = refs[:n], refs[n:2 * n], refs[2 * n:]
        x, y, c = _place()
        (ax, ay), _ = _ici_peers(x, y, c)
        cps = [_rcopy(p[a].at[jnp.where(c == 0, 2 * ax + k, 2 * k + ay)], r[a].at[k], send, recv, 2 * a + k, (ax, ay, c))
               for a in range(n) for k in range(2)]
        for cp in cps:
            cp.start()
        for cp in cps:
            cp.wait_recv()
        for cp in cps:
            cp.wait_send()

    return _comm_call(name, body, parts, [jax.ShapeDtypeStruct((2,) + t.shape[1:], t.dtype) for t in parts], 2 * n, 0)


def _halve_second(name, kept):
    n = len(kept)

    def body(*refs):
        p, r, (send, recv, _) = refs[:n], refs[n:2 * n], refs[2 * n:]
        x, y, c = _place()
        _, (bx, by) = _ici_peers(x, y, c)
        cps = [_rcopy(p[a].at[jnp.where(c == 0, by, bx)], r[a], send, recv, a, (bx, by, c)) for a in range(n)]
        for cp in cps:
            cp.start()
        for cp in cps:
            cp.wait_recv()
        for cp in cps:
            cp.wait_send()

    return _comm_call(name, body, kept, [jax.ShapeDtypeStruct(t.shape[1:], t.dtype) for t in kept], n, 0)


def _add_selected(name, t, r, index, out_dtype):
    single = r.ndim == 2
    r3 = r[None] if single else r
    nk, rh, cc = r3.shape
    tl = _tile(rh, ROW_TILE, 16)

    def body(i_ref, t_ref, r_ref, o_ref):
        o_ref[...] = (t_ref[...].astype(F32) + r_ref[...].astype(F32)).astype(o_ref.dtype)

    out = _pcall(
        body, name=name,
        grid_spec=pltpu.PrefetchScalarGridSpec(
            num_scalar_prefetch=1, grid=(nk, rh // tl),
            in_specs=[pl.BlockSpec((1, tl, cc), lambda k, i, i_ref: (i_ref[k], i, 0)),
                      pl.BlockSpec((1, tl, cc), lambda k, i, i_ref: (k, i, 0))],
            out_specs=pl.BlockSpec((1, tl, cc), lambda k, i, i_ref: (k, i, 0))),
        out_shape=jax.ShapeDtypeStruct(r3.shape, out_dtype),
        compiler_params=_cparams(("parallel", "parallel"), 8 * tl * cc * 4),
    )(index, t, r3)
    return out[0] if single else out


def _pair_share(name, reds):
    n = len(reds)

    def body(*refs):
        r, o, (send, recv, _) = refs[:n], refs[n:2 * n], refs[2 * n:]
        x, y, c = _place()
        cps = [_rcopy(r[a], o[a].at[c], send, recv, a, (x, y, 1 - c)) for a in range(n)]
        for cp in cps:
            cp.start()
        for a in range(n):
            blk = o[a].at[1 - c]
            _rcopy(blk, blk, send, recv, a, (x, y, 1 - c)).wait_recv()
        for cp in cps:
            cp.wait_send()

    outs = _comm_call(name, body, reds, [jax.ShapeDtypeStruct((2,) + t.shape, t.dtype) for t in reds], n, 0)
    core = lax.axis_index("c")
    return [lax.dynamic_update_slice_in_dim(o, t[None], core, axis=0) for o, t in zip(outs, reds)]


def _add_own_half(name, g, r):
    ns, _, rh, cc = g.shape
    tl = _tile(rh, ROW_TILE, 16)
    core = lax.axis_index("c").astype(jnp.int32).reshape(1)

    def body(c_ref, g_ref, r_ref, o_ref):
        o_ref[...] = (g_ref[0].astype(F32) + r_ref[...].astype(F32)).astype(o_ref.dtype)

    return _pcall(
        body, name=name,
        grid_spec=pltpu.PrefetchScalarGridSpec(
            num_scalar_prefetch=1, grid=(ns, rh // tl),
            in_specs=[pl.BlockSpec((1, 1, tl, cc), lambda j, i, c_ref: (j, c_ref[0], i, 0)),
                      pl.BlockSpec((1, tl, cc), lambda j, i, c_ref: (j, i, 0))],
            out_specs=pl.BlockSpec((1, tl, cc), lambda j, i, c_ref: (j, i, 0))),
        out_shape=jax.ShapeDtypeStruct(r.shape, BF16),
        compiler_params=_cparams(("parallel", "parallel"), 8 * tl * cc * 4),
    )(core, g, r)


def _reduce_scatter(gs):
    x, y, c = lax.axis_index("x"), lax.axis_index("y"), lax.axis_index("c")
    mine_first = jnp.stack([jnp.where(c == 0, 2 * x + k, 2 * k + y) for k in range(2)]).astype(jnp.int32)
    mine_second = jnp.where(c == 0, y, x).astype(jnp.int32).reshape(1)
    swapped = _pair_swap("rs_pair_swap", gs)
    parts = [_add_own_half("rs_add_pair", g, r) for g, r in zip(gs, swapped)]
    got = _halve_first("rs_halve_first", parts)
    kept = [_add_selected("rs_add_first", t, r, mine_first, BF16) for t, r in zip(parts, got)]
    got = _halve_second("rs_halve_second", kept)
    reds = [_add_selected("rs_add_second", t, r, mine_second, F32) for t, r in zip(kept, got)]
    return _pair_share("rs_pair_share", reds)


def _adamw(name, g, w, m, v):
    c1 = 1.0 - ADAM_B1 ** ADAM_STEP
    c2 = 1.0 - ADAM_B2 ** ADAM_STEP
    nb, rows, cc = g.shape
    tl = _tile(rows, ROW_TILE, SUBLANE)

    def body(g_ref, w_ref, m_ref, v_ref, d_ref, mo_ref, vo_ref):
        gv = g_ref[...]
        mn = ADAM_B1 * m_ref[...] + (1.0 - ADAM_B1) * gv
        vn = ADAM_B2 * v_ref[...] + (1.0 - ADAM_B2) * (gv * gv)
        d_ref[...] = -ADAM_LR * ((mn / c1) / (jnp.sqrt(vn / c2) + ADAM_EPS) + ADAM_WD * w_ref[...])
        mo_ref[...] = mn
        vo_ref[...] = vn

    spec = pl.BlockSpec((1, tl, cc), lambda b, i: (b, i, 0))
    return _pcall(
        body, name=name, grid=(nb, rows // tl), in_specs=[spec] * 4, out_specs=[spec] * 3,
        out_shape=[jax.ShapeDtypeStruct(g.shape, F32)] * 3,
        compiler_params=_cparams(("parallel", "parallel"), 16 * tl * cc * 4),
    )(g, w, m, v)


def _train_step(a):
    x, target = a['x'][0], a['loss_target'][0]

    names = BF16_GATHER + F32_GATHER
    got = _ag_chips("ag_weights", [_stored(n, a[n]).astype(BF16) for n in BF16_GATHER] + [a[n] for n in F32_GATHER])
    ws = dict(zip(names, got))

    loss, gx, gs, grep = _device_step(x, target, ws, {n: a[n] for n in REPLICATED})
    loss = lax.psum(loss, ("x", "y", "c"))

    small = [n for n in MISC_SHARDED if n not in LARGE_SHARDED]
    small_shapes = [a[n].shape[1:] for n in small]
    rep_shapes = [a[n].shape for n in REPLICATED]
    small_rows = sum(_rows_of(s) for s in small_shapes)
    piece = _round_up(-(-sum(_rows_of(s) for s in rep_shapes) // (2 * N_CHIPS)), SUBLANE)
    rep_rows = 2 * N_CHIPS * piece
    rep_pack = _pack([grep[n] for n in REPLICATED], rep_rows).reshape(N_CHIPS, 2, piece, PACK_C)
    total = _round_up(small_rows + piece, ROW_TILE // 2)
    misc = _pack([gs[n] for n in small] + [rep_pack], total, lead=2)
    reds = _reduce_scatter([gs[n] for n in LARGE_SHARDED] + [misc])
    grad = dict(zip(LARGE_SHARDED, reds[:-1]))
    grad.update(zip(small, _unpack(reds[-1], small_shapes, lead=1)))
    rep_mine = reds[-1][:, small_rows:small_rows + piece]
    rep_all = _ag_chips("ag_replicated", [rep_mine])[0].reshape(1, rep_rows, PACK_C)

    outs = {}
    for n in SHARDED:
        shp = grad[n].shape
        three_d = (shp[0], -1, shp[-1])
        params = (grad[n], *(_stored(n, a[p + n]) for p in ('', 'm_', 'v_')))
        res3 = _adamw("adamw", *(t.reshape(three_d) for t in params))
        for kind, arr in zip(('grad', 'delta', 'new_m', 'new_v'), (grad[n], *res3)):
            outs[kind + '_' + n] = _stored(n, arr.reshape(shp))
    w, m, v = (_pack([a[p + n] for n in REPLICATED], rep_rows)[None] for p in ('', 'm_', 'v_'))
    res3 = _adamw("adamw_replicated", rep_all, w, m, v)
    for kind, arr in zip(('grad', 'delta', 'new_m', 'new_v'), (rep_all, *res3)):
        for n, t in zip(REPLICATED, _unpack(arr[0], rep_shapes)):
            outs[kind + '_' + n] = t
    res = [loss, gx[None]]
    for kind in ('grad', 'delta', 'new_m', 'new_v'):
        res += [outs[kind + '_' + n] for n in W_NAMES]
    return tuple(res)


def kernel(x, ffn1_norm, ffn1_w_gate, ffn1_w_up, ffn1_w_down, mix_norm, w_in, s5_lambda_re, s5_lambda_im, s5_log_dt, s5_b_re, s5_b_im, s5_c_re, s5_c_im, s5_d, s5_w_glu, gla_w_alpha, gla_b_alpha, gla_norm, attn_q_norm, attn_k_norm, w_branch_s5, w_branch_gla, w_branch_attn, w_merge_gate, b_merge_gate, w_out, ffn2_norm, ffn2_w_gate, ffn2_w_up, ffn2_w_down, final_norm, loss_target, m_ffn1_norm, m_ffn1_w_gate, m_ffn1_w_up, m_ffn1_w_down, m_mix_norm, m_w_in, m_s5_lambda_re, m_s5_lambda_im, m_s5_log_dt, m_s5_b_re, m_s5_b_im, m_s5_c_re, m_s5_c_im, m_s5_d, m_s5_w_glu, m_gla_w_alpha, m_gla_b_alpha, m_gla_norm, m_attn_q_norm, m_attn_k_norm, m_w_branch_s5, m_w_branch_gla, m_w_branch_attn, m_w_merge_gate, m_b_merge_gate, m_w_out, m_ffn2_norm, m_ffn2_w_gate, m_ffn2_w_up, m_ffn2_w_down, m_final_norm, v_ffn1_norm, v_ffn1_w_gate, v_ffn1_w_up, v_ffn1_w_down, v_mix_norm, v_w_in, v_s5_lambda_re, v_s5_lambda_im, v_s5_log_dt, v_s5_b_re, v_s5_b_im, v_s5_c_re, v_s5_c_im, v_s5_d, v_s5_w_glu, v_gla_w_alpha, v_gla_b_alpha, v_gla_norm, v_attn_q_norm, v_attn_k_norm, v_w_branch_s5, v_w_branch_gla, v_w_branch_attn, v_w_merge_gate, v_b_merge_gate, v_w_out, v_ffn2_norm, v_ffn2_w_gate, v_ffn2_w_up, v_ffn2_w_down, v_final_norm):
    return _train_step(dict(locals()))
```

```python
import functools
import math

import jax
import jax.numpy as jnp
import numpy as np
from jax import lax
from jax.experimental import pallas as pl
from jax.experimental.pallas import tpu as pltpu

F32 = jnp.float32
BF16 = jnp.bfloat16

D_MODEL = 1024
DEPTH = 2
EPS = 1e-6
S5_G, S5_H, S5_P = 32, 16, 64
S5_W = S5_G * S5_H
S5_N = S5_G * S5_P
S5_TG = 8
S5_TILES = S5_G // S5_TG
S5_TC, S5_TS = S5_TG * S5_H, S5_TG * S5_P
GLA_H, GLA_D = 4, 128
GLA_W = GLA_H * GLA_D
GLA_R = 16
GLA_TAU = 16.0
GLA_C = 64
AT_QH, AT_KH, AT_D = 8, 2, 64
AT_W = AT_QH * AT_D
AT_KW = AT_KH * AT_D
GRID_W = 64
ROPE_BASE = 10000.0
ZP_W = 3456
ZC_AK, ZC_AV, ZC_Z = 24, 25, 26

ADAM_LR, ADAM_B1, ADAM_B2, ADAM_EPS, ADAM_WD, ADAM_STEP = 0.001, 0.9, 0.999, 1e-08, 0.01, 10

GRAD_DT = BF16
VMEM_LIMIT_V7X = 56 * 1024 * 1024
LANE = 128
SUBLANE = 8

W_NAMES = ['ffn1_norm', 'ffn1_w_gate', 'ffn1_w_up', 'ffn1_w_down', 'mix_norm', 'w_in', 's5_lambda_re', 's5_lambda_im',
           's5_log_dt', 's5_b_re', 's5_b_im', 's5_c_re', 's5_c_im', 's5_d', 's5_w_glu', 'gla_w_alpha', 'gla_b_alpha',
           'gla_norm', 'attn_q_norm', 'attn_k_norm', 'w_branch_s5', 'w_branch_gla', 'w_branch_attn', 'w_merge_gate',
           'b_merge_gate', 'w_out', 'ffn2_norm', 'ffn2_w_gate', 'ffn2_w_up', 'ffn2_w_down', 'final_norm']
SHARD_AXIS = {'ffn1_w_gate': 2, 'ffn1_w_up': 2, 'ffn1_w_down': 1, 'w_in': 2, 's5_w_glu': 1, 'gla_w_alpha': 3,
              'gla_b_alpha': 2, 'w_branch_s5': 2, 'w_branch_gla': 2, 'w_branch_attn': 2, 'w_merge_gate': 2,
              'w_out': 1, 'ffn2_w_gate': 2, 'ffn2_w_up': 2, 'ffn2_w_down': 1}
F32_GATHER = ['gla_w_alpha', 'gla_b_alpha']
BF16_GATHER = [n for n in W_NAMES if n in SHARD_AXIS and n not in F32_GATHER]
SHARDED = [n for n in W_NAMES if n in SHARD_AXIS]
REPLICATED = [n for n in W_NAMES if n not in SHARD_AXIS]
N_CHIPS = 4
MESH = pl.DeviceIdType.MESH


def _tile(n, target, mult):
    best = None
    for t in range(mult, min(n, target) + 1, mult):
        if n % t == 0:
            best = t
    return best if best is not None else n


def _cparams(sem, vmem_bytes):
    limit = int(min(VMEM_LIMIT_V7X, max(32 * 1024 * 1024, vmem_bytes * 5 // 4)))
    return pltpu.CompilerParams(dimension_semantics=sem, vmem_limit_bytes=limit)


def _pcall(body, **kw):
    shapes = kw.pop('out_shape')
    many = isinstance(shapes, (list, tuple))
    pinned = [pltpu.HBM(t.shape, t.dtype) for t in (shapes if many else [shapes])]
    call = pl.pallas_call(body, out_shape=pinned if many else pinned[0], **kw)

    def run(*ops):
        return call(*[o if jnp.issubdtype(o.dtype, jnp.integer) else pltpu.with_memory_space_constraint(o, pltpu.HBM)
                      for o in ops])

    return run


def _cw(arr, cb, width):
    return (arr, cb, width)


def _win(a):
    if isinstance(a, tuple):
        return a
    return (a, 0, a.shape[-1])


def _nbytes(shape, dtype):
    return int(np.prod(shape)) * jnp.dtype(dtype).itemsize


def _rowwise(name, fn, rows, fulls, outs, accs=(), tl=256):
    rows = [_win(r) for r in rows]
    n_tok = rows[0][0].shape[0]
    tl = _tile(n_tok, tl, 16)
    nr, nf, no = len(rows), len(fulls), len(outs)

    def body(*refs):
        vals = [r[...] for r in refs[:nr + nf]]
        res = fn(*vals)
        if not isinstance(res, (tuple, list)):
            res = (res,)
        for k in range(no):
            o = refs[nr + nf + k]
            o[...] = res[k].astype(o.dtype)
        if accs:
            @pl.when(pl.program_id(0) == 0)
            def _():
                for k in range(len(accs)):
                    a = refs[nr + nf + no + k]
                    a[...] = jnp.zeros(a.shape, a.dtype)
            for k in range(len(accs)):
                a = refs[nr + nf + no + k]
                a[...] += res[no + k]

    in_specs = [pl.BlockSpec((tl, w), functools.partial(lambda i, cb: (i, cb), cb=cb)) for (_, cb, w) in rows]
    in_specs += [pl.BlockSpec(f.shape, functools.partial(lambda i, nd: (0,) * nd, nd=f.ndim)) for f in fulls]
    out_specs = [pl.BlockSpec((tl, c), lambda i: (i, 0)) for (c, _) in outs]
    out_specs += [pl.BlockSpec((r, c), lambda i: (0, 0)) for (r, c) in accs]
    out_shape = [jax.ShapeDtypeStruct((n_tok, c), dt) for (c, dt) in outs]
    out_shape += [jax.ShapeDtypeStruct((r, c), F32) for (r, c) in accs]
    vm = 2 * sum(_nbytes((tl, w), a.dtype) for (a, _, w) in rows)
    vm += 2 * sum(_nbytes(f.shape, f.dtype) for f in fulls)
    vm += 2 * sum(_nbytes((tl, c), dt) for (c, dt) in outs)
    vm += 8 * max([_nbytes((tl, w), F32) for (_, _, w) in rows] + [_nbytes((tl, c), F32) for (c, _) in outs])
    return _pcall(
        body, name=name, grid=(n_tok // tl,), in_specs=in_specs, out_specs=out_specs, out_shape=out_shape,
        compiler_params=_cparams(("arbitrary",) if accs else ("parallel",), vm),
    )(*[r[0] for r in rows], *fulls)


def _mm(name, a, b, *, ta=False, tb=False, out_dtype=F32, alpha=1.0, add=None, tm=512, tn=512):
    a_arr, a_cb, a_w = _win(a)
    b_arr, b_cb, b_w = _win(b)
    if ta:
        kdim, m = a_arr.shape[0], a_w
    else:
        m, kdim = a_arr.shape[0], a_w
    if tb:
        n, kb = b_arr.shape[0], b_w
    else:
        kb, n = b_arr.shape[0], b_w
    assert kdim == kb, (name, kdim, kb)
    if not ta and kdim <= 1024:
        tm = 2 * tm
    tm = _tile(m, tm, LANE if ta else 16)
    tn = _tile(n, tn, LANE)
    if ta:
        a_spec = pl.BlockSpec((kdim, tm), lambda i, j: (0, a_cb * (a_w // tm) + i))
        a_blk = (kdim, tm)
    else:
        a_spec = pl.BlockSpec((tm, kdim), lambda i, j: (i, a_cb))
        a_blk = (tm, kdim)
    if tb:
        b_spec = pl.BlockSpec((tn, kdim), lambda i, j: (j, b_cb))
        b_blk = (tn, kdim)
    else:
        b_spec = pl.BlockSpec((kdim, tn), lambda i, j: (0, b_cb * (b_w // tn) + j))
        b_blk = (kdim, tn)
    dims = (((0 if ta else 1,), (1 if tb else 0,)), ((), ()))
    has_add = add is not None

    def body(*refs):
        a_ref, b_ref = refs[0], refs[1]
        o_ref = refs[-1]
        acc = lax.dot_general(a_ref[...].astype(BF16), b_ref[...].astype(BF16), dims, preferred_element_type=F32)
        if alpha != 1.0:
            acc = acc * alpha
        if has_add:
            acc = acc + refs[2][...].astype(F32)
        o_ref[...] = acc.astype(o_ref.dtype)

    in_specs = [a_spec, b_spec]
    ops = [a_arr, b_arr]
    if has_add:
        in_specs.append(pl.BlockSpec((tm, tn), lambda i, j: (i, j)))
        ops.append(add)
    vm = 2 * (_nbytes(a_blk, a_arr.dtype) + _nbytes(b_blk, b_arr.dtype) + _nbytes((tm, tn), out_dtype))
    vm += _nbytes(a_blk, BF16) + _nbytes(b_blk, BF16) + 3 * _nbytes((tm, tn), F32)
    return _pcall(
        body, name=name, grid=(m // tm, n // tn), in_specs=in_specs,
        out_specs=pl.BlockSpec((tm, tn), lambda i, j: (i, j)),
        out_shape=jax.ShapeDtypeStruct((m, n), out_dtype),
        compiler_params=_cparams(("parallel", "parallel"), vm),
    )(*ops)


def _sigmoid(x):
    return 0.5 * jnp.tanh(0.5 * x) + 0.5


def _rms(x, gain):
    r = lax.rsqrt(jnp.mean(x * x, axis=-1, keepdims=True) + EPS)
    return x * r * gain


def _rms_bwd(x, gain, dh):
    r = lax.rsqrt(jnp.mean(x * x, axis=-1, keepdims=True) + EPS)
    xh = x * r
    dg = jnp.sum(dh * xh, axis=0, keepdims=True)
    dxh = dh * gain
    dx = r * (dxh - xh * jnp.mean(dxh * xh, axis=-1, keepdims=True))
    return dx, dg


_GELU_C = math.sqrt(2.0 / math.pi)


def _gelu(y):
    return 0.5 * y * (1.0 + jnp.tanh(_GELU_C * (y + 0.044715 * y * y * y)))


def _gelu_grad(y):
    th = jnp.tanh(_GELU_C * (y + 0.044715 * y * y * y))
    return 0.5 * (1.0 + th) + 0.5 * y * (1.0 - th * th) * _GELU_C * (1.0 + 3.0 * 0.044715 * y * y)


def _log_sigmoid(x):
    return jnp.minimum(x, 0.0) - jnp.log(1.0 + jnp.exp(-jnp.abs(x)))


def _dot(a, b, dims=(((1,), (0,)), ((), ())), precision=None):
    return lax.dot_general(a, b, dims, preferred_element_type=F32, precision=precision)


def _dot_exact(x, m, dims=(((1,), (0,)), ((), ())), terms=3, m_first=False):
    m16 = m.astype(BF16)
    acc, rest = None, x
    for k in range(terms):
        piece = rest.astype(BF16)
        t = _dot(m16, piece, dims) if m_first else _dot(piece, m16, dims)
        acc = t if acc is None else acc + t
        if k + 1 < terms:
            rest = rest - piece.astype(F32)
    return acc


_NT = (((1,), (1,)), ((), ()))
_TN = (((0,), (0,)), ((), ()))


def _scan_core(b_ref, are_ref, aim_ref, pw, last, of32, *, rev, xf32=None, da_ref=None, lc=512):
    nb, nc = b_ref.shape[0] // SUBLANE, S5_N // lc
    row = lax.broadcasted_iota(jnp.int32, (SUBLANE, lc), 0)

    def cmul(ar, ai, xr, xi):
        return ar * xr - ai * xi, ar * xi + ai * xr

    def hs(c, xr, xi):
        cs = pl.ds(c * lc, lc)
        for k, slot in ((1, 0), (2, 2), (4, 4)):
            ar, ai = pw[slot, :, cs], pw[slot + 1, :, cs]
            if rev:
                sr, si = pltpu.roll(xr, SUBLANE - k, 0), pltpu.roll(xi, SUBLANE - k, 0)
                keep = row < SUBLANE - k
            else:
                sr, si = pltpu.roll(xr, k, 0), pltpu.roll(xi, k, 0)
                keep = row >= k
            sr, si = jnp.where(keep, sr, 0.0), jnp.where(keep, si, 0.0)
            pr, pi = cmul(ar, ai, sr, si)
            xr, xi = xr + pr, xi + pi
        return xr, xi

    @pl.when(pl.program_id(0) == 0)
    def _():
        last[...] = jnp.zeros(last.shape, F32)
        if da_ref is not None:
            da_ref[...] = jnp.zeros(da_ref.shape, F32)
        for c in range(nc):
            cs = pl.ds(c * lc, lc)
            a1r = jnp.broadcast_to(are_ref[:, cs], (SUBLANE, lc))
            a1i = jnp.broadcast_to(aim_ref[:, cs], (SUBLANE, lc))
            a2r, a2i = cmul(a1r, a1i, a1r, a1i)
            a4r, a4i = cmul(a2r, a2i, a2r, a2i)
            for slot, v in enumerate((a1r, a1i, a2r, a2i, a4r, a4i)):
                pw[slot, :, cs] = v
            first = SUBLANE - 1 if rev else 0
            pr, pi = hs(c, jnp.where(row == first, a1r, 0.0), jnp.where(row == first, a1i, 0.0))
            pw[6, :, cs] = pr
            pw[7, :, cs] = pi

    edge = 0 if rev else SUBLANE - 1

    def blk(j, carry):
        jj = (nb - 1 - j) if rev else j
        r0 = pl.multiple_of(jj * SUBLANE, SUBLANE)
        for c in range(nc):
            cre, cim = pl.ds(c * lc, lc), pl.ds(S5_N + c * lc, lc)
            xr, xi = hs(c, b_ref[pl.ds(r0, SUBLANE), cre], b_ref[pl.ds(r0, SUBLANE), cim])
            cr = jnp.broadcast_to(last[edge:edge + 1, cre], (SUBLANE, lc))
            ci = jnp.broadcast_to(last[edge:edge + 1, cim], (SUBLANE, lc))
            pr, pi = cmul(pw[6, :, cre], pw[7, :, cre], cr, ci)
            xr, xi = xr + pr, xi + pi
            if da_ref is not None:
                if rev:
                    qr, qi = pltpu.roll(xr, SUBLANE - 1, 0), pltpu.roll(xi, SUBLANE - 1, 0)
                    fill = row == SUBLANE - 1
                else:
                    qr, qi = pltpu.roll(xr, 1, 0), pltpu.roll(xi, 1, 0)
                    fill = row == 0
                qr, qi = jnp.where(fill, cr, qr), jnp.where(fill, ci, qi)
                zr, zi = xf32[pl.ds(r0, SUBLANE), cre], xf32[pl.ds(r0, SUBLANE), cim]
                da_ref[:, cre] += zr * qr + zi * qi
                da_ref[:, cim] += zr * qi - zi * qr
            last[:, cre] = xr
            last[:, cim] = xi
            of32[pl.ds(r0, SUBLANE), cre] = xr
            of32[pl.ds(r0, SUBLANE), cim] = xi
        return carry

    lax.fori_loop(0, nb, blk, 0)


def _s5_tiles():
    return [(slice((j % S5_TILES) * S5_TC, (j % S5_TILES + 1) * S5_TC), slice(j * S5_TS, (j + 1) * S5_TS))
            for j in range(2 * S5_TILES)]


def _s5_scratch(tb):
    return [pltpu.VMEM((8, SUBLANE, S5_N), F32), pltpu.VMEM((SUBLANE, 2 * S5_N), F32),
            pltpu.VMEM((tb, 2 * S5_N), F32), pltpu.VMEM((tb, 2 * S5_N), F32),
            pltpu.VMEM((S5_TC, 2 * S5_N), BF16), pltpu.VMEM((2 * S5_N, S5_TC), BF16)]


def _s5_fwd(name, z, a_re, a_im, wb, wc, add, *, rev, tb=512):
    n_tok = z.shape[0]
    tb = _tile(n_tok, tb, 16)
    nt = n_tok // tb
    has_add = add is not None

    def body(*refs):
        u_ref, are_ref, aim_ref, wb_ref, wc_ref = refs[:5]
        x_ref, y_ref, pw, last, of32, bu, wb16, wc16 = refs[5 + has_add:]

        @pl.when(pl.program_id(0) == 0)
        def _():
            wb16[...] = wb_ref[...].astype(BF16)
            wc16[...] = wc_ref[...].astype(BF16)

        u16 = u_ref[...].astype(BF16)
        for ch, st in _s5_tiles():
            bu[:, st] = _dot(u16[:, ch], wb16[:, st])
        _scan_core(bu, are_ref, aim_ref, pw, last, of32, rev=rev)
        x16 = of32[...].astype(BF16)
        x_ref[...] = x16
        ys = []
        for k in range(S5_TILES):
            re, im = slice(k * S5_TS, (k + 1) * S5_TS), slice(S5_N + k * S5_TS, S5_N + (k + 1) * S5_TS)
            ys.append(_dot(x16[:, re], wc16[re, :]) + _dot(x16[:, im], wc16[im, :]))
        y = jnp.concatenate(ys, axis=1)
        y_ref[...] = y + refs[5][...] if has_add else y

    tmap = (lambda t: (nt - 1 - t, 0)) if rev else (lambda t: (t, 0))
    const = lambda t: (0, 0)
    in_specs = [pl.BlockSpec((tb, S5_W), tmap), pl.BlockSpec((1, S5_N), const), pl.BlockSpec((1, S5_N), const),
                pl.BlockSpec((S5_TC, 2 * S5_N), const), pl.BlockSpec((2 * S5_N, S5_TC), const)]
    ops = [z, a_re, a_im, wb, wc]
    if has_add:
        in_specs.append(pl.BlockSpec((tb, S5_W), tmap))
        ops.append(add)
    return _pcall(
        body, name=name, grid=(nt,), in_specs=in_specs,
        out_specs=[pl.BlockSpec((tb, 2 * S5_N), tmap), pl.BlockSpec((tb, S5_W), tmap)],
        out_shape=[jax.ShapeDtypeStruct((n_tok, 2 * S5_N), BF16), jax.ShapeDtypeStruct((n_tok, S5_W), F32)],
        scratch_shapes=_s5_scratch(tb), compiler_params=_cparams(("arbitrary",), 10 * tb * 2 * S5_N * 4),
    )(*ops)


def _s5_bwd(name, dy, z, xs, a_re, a_im, wb, wc, add, *, rev, tb=256):
    n_tok = z.shape[0]
    tb = _tile(n_tok, tb, 16)
    nt = n_tok // tb

    def body(dy_ref, u_ref, xs_ref, are_ref, aim_ref, wb_ref, wc_ref, add_ref, du_ref, dwb_ref, dwc_ref, da_ref,
             pw, last, of32, gx, wb16, wc16, xf32):
        @pl.when(pl.program_id(0) == 0)
        def _():
            wb16[...] = wb_ref[...].astype(BF16)
            wc16[...] = wc_ref[...].astype(BF16)
            dwb_ref[...] = jnp.zeros(dwb_ref.shape, F32)
            dwc_ref[...] = jnp.zeros(dwc_ref.shape, F32)

        dy16, u16, x16 = dy_ref[...].astype(BF16), u_ref[...].astype(BF16), xs_ref[...]
        for ch, st in _s5_tiles():
            gx[:, st] = _dot(dy16[:, ch], wc16[st, :], _NT)
        xf32[...] = x16.astype(F32)
        _scan_core(gx, are_ref, aim_ref, pw, last, of32, rev=rev, xf32=xf32, da_ref=da_ref)
        g16 = of32[...].astype(BF16)
        dus = []
        for k in range(S5_TILES):
            re, im = slice(k * S5_TS, (k + 1) * S5_TS), slice(S5_N + k * S5_TS, S5_N + (k + 1) * S5_TS)
            dus.append(_dot(g16[:, re], wb16[:, re], _NT) + _dot(g16[:, im], wb16[:, im], _NT))
        du_ref[...] = jnp.concatenate(dus, axis=1) + add_ref[...]
        for ch, st in _s5_tiles():
            dwb_ref[:, st] += _dot(u16[:, ch], g16[:, st], _TN)
            dwc_ref[st, :] += _dot(x16[:, st], dy16[:, ch], _TN)

    tmap = (lambda t: (nt - 1 - t, 0)) if rev else (lambda t: (t, 0))
    const = lambda t: (0, 0)
    row_w, row_s = pl.BlockSpec((tb, S5_W), tmap), pl.BlockSpec((tb, 2 * S5_N), tmap)
    return _pcall(
        body, name=name, grid=(nt,),
        in_specs=[row_w, row_w, row_s, pl.BlockSpec((1, S5_N), const), pl.BlockSpec((1, S5_N), const),
                  pl.BlockSpec((S5_TC, 2 * S5_N), const), pl.BlockSpec((2 * S5_N, S5_TC), const), row_w],
        out_specs=[row_w, pl.BlockSpec((S5_TC, 2 * S5_N), const), pl.BlockSpec((2 * S5_N, S5_TC), const),
                   pl.BlockSpec((SUBLANE, 2 * S5_N), const)],
        out_shape=[jax.ShapeDtypeStruct((n_tok, S5_W), F32), jax.ShapeDtypeStruct((S5_TC, 2 * S5_N), F32),
                   jax.ShapeDtypeStruct((2 * S5_N, S5_TC), F32), jax.ShapeDtypeStruct((SUBLANE, 2 * S5_N), F32)],
        scratch_shapes=_s5_scratch(tb) + [pltpu.VMEM((tb, 2 * S5_N), F32)],
        compiler_params=_cparams(("arbitrary",), 12 * tb * 2 * S5_N * 4),
    )(dy, z, xs, a_re, a_im, wb, wc, add)


GLA_TB = 512


def _gla_block_terms(la, q, k, rev):
    trif = _chunk_mask(rev).astype(F32)
    chunks = [la[t:t + GLA_C] for t in range(0, la.shape[0], GLA_C)]
    bcum = jnp.concatenate([_dot_exact(c, trif, m_first=True) for c in chunks], axis=0)
    blast = _per_chunk_sum(la)
    eb, enb, ee = jnp.exp(bcum), jnp.exp(-bcum), jnp.exp(blast - bcum)
    return trif, jnp.exp(blast), eb, enb, ee, q * eb, k * enb, k * ee


def _per_chunk_sum(t):
    return jnp.concatenate([jnp.broadcast_to(jnp.sum(t[r:r + GLA_C], axis=0, keepdims=True), (GLA_C, t.shape[1]))
                            for r in range(0, t.shape[0], GLA_C)], axis=0)


def _chunk_mask(rev):
    ri = lax.broadcasted_iota(jnp.int32, (GLA_C, GLA_C), 0)
    ci = lax.broadcasted_iota(jnp.int32, (GLA_C, GLA_C), 1)
    return (ri <= ci) if rev else (ri >= ci)


def _gla_fwd(name, z, la, *, rev):
    n_tok = z.shape[0]
    tb = _tile(n_tok, GLA_TB, GLA_C)
    nblk, per = n_tok // tb, tb // GLA_C
    scale = GLA_D ** -0.5

    def body(q_ref, k_ref, v_ref, la_ref, o_ref, s_ref, st, qd_s, kd_s, ke_s, v_s, dec_s):
        @pl.when(pl.program_id(0) == 0)
        def _():
            st[...] = jnp.zeros(st.shape, F32)

        _, dec, _, _, _, qd, kd, ke = _gla_block_terms(la_ref[...], q_ref[...].astype(F32) * scale,
                                                                 k_ref[...].astype(F32), rev)
        qd_s[...], kd_s[...], ke_s[...] = qd.astype(BF16), kd.astype(BF16), ke.astype(BF16)
        v_s[...] = v_ref[...].astype(BF16)
        dec_s[...] = dec
        tri = _chunk_mask(rev)
        for step in range(per):
            t = per - 1 - step if rev else step
            rows = pl.ds(t * GLA_C, GLA_C)
            for h in range(GLA_H):
                lanes = pl.ds(h * GLA_D, GLA_D)
                qd16, v16 = qd_s[rows, lanes], v_s[rows, lanes]
                sc = jnp.where(tri, _dot(qd16, kd_s[rows, lanes], _NT), 0.0)
                s16 = st[h].astype(BF16)
                s_ref[h, step] = s16
                o_ref[rows, lanes] = _dot(sc.astype(BF16), v16) + _dot(qd16, s16, _NT)
                st[h] = st[h] * dec_s[pl.ds(t * GLA_C, 1), lanes] + _dot(v16, ke_s[rows, lanes], _TN)

    bmap = (lambda n: nblk - 1 - n) if rev else (lambda n: n)

    def zspec(cb):
        return pl.BlockSpec((tb, GLA_W), lambda n: (bmap(n), cb))

    half = pltpu.VMEM((tb, GLA_W), BF16)
    return _pcall(
        body, name=name, grid=(nblk,),
        in_specs=[zspec(1), zspec(2), zspec(3), zspec(0)],
        out_specs=[zspec(0), pl.BlockSpec((GLA_H, per, GLA_D, GLA_D), lambda n: (0, n, 0, 0))],
        out_shape=[jax.ShapeDtypeStruct((n_tok, GLA_W), F32),
                   jax.ShapeDtypeStruct((GLA_H, n_tok // GLA_C, GLA_D, GLA_D), BF16)],
        scratch_shapes=[pltpu.VMEM((GLA_H, GLA_D, GLA_D), F32), half, half, half, half, pltpu.VMEM((tb, GLA_W), F32)],
        compiler_params=_cparams(("arbitrary",), 40 * tb * GLA_W * 4),
    )(z, z, z, la)


def _gla_bwd(name, z, la, s_prev, do, *, rev):
    n_tok = z.shape[0]
    tb = _tile(n_tok, GLA_TB, GLA_C)
    nblk, per = n_tok // tb, tb // GLA_C
    scale = GLA_D ** -0.5

    def body(q_ref, k_ref, v_ref, la_ref, s_ref, do_ref, dq_ref, dk_ref, dv_ref, dla_ref,
             dst, qd_s, kd_s, ke_s, v_s, do_s, dec_s, dqd_s, dkd_s, dke_s, dd_s):
        @pl.when(pl.program_id(0) == 0)
        def _():
            dst[...] = jnp.zeros(dst.shape, F32)

        trif, dec, eb, enb, ee, qd, kd, ke = _gla_block_terms(la_ref[...], q_ref[...].astype(F32) * scale,
                                                                 k_ref[...].astype(F32), rev)
        qd_s[...], kd_s[...], ke_s[...] = qd.astype(BF16), kd.astype(BF16), ke.astype(BF16)
        v_s[...] = v_ref[...].astype(BF16)
        do_s[...] = do_ref[...].astype(BF16)
        dec_s[...] = dec
        tri = _chunk_mask(rev)
        for step in range(per):
            t = step if rev else per - 1 - step
            rows = pl.ds(t * GLA_C, GLA_C)
            for h in range(GLA_H):
                lanes = pl.ds(h * GLA_D, GLA_D)
                qd16, kd16, ke16 = qd_s[rows, lanes], kd_s[rows, lanes], ke_s[rows, lanes]
                v16, do16 = v_s[rows, lanes], do_s[rows, lanes]
                sc16 = jnp.where(tri, _dot(qd16, kd16, _NT), 0.0).astype(BF16)
                dsc16 = jnp.where(tri, _dot(do16, v16, _NT), 0.0).astype(BF16)
                s16 = s_ref[h, per - 1 - step]
                dsn = dst[h]
                dsn16 = dsn.astype(BF16)
                dec_c = dec_s[pl.ds(t * GLA_C, 1), lanes]
                dqd_s[rows, lanes] = _dot(dsc16, kd16) + _dot(do16, s16)
                dkd_s[rows, lanes] = _dot(dsc16, qd16, _TN)
                dke_s[rows, lanes] = _dot(v16, dsn16)
                dv_ref[rows, lanes] = _dot(sc16, do16, _TN) + _dot(ke16, dsn16, _NT)
                ddec = jnp.sum(dsn * s16.astype(F32), axis=0, keepdims=True) * dec_c
                dd_s[rows, lanes] = jnp.broadcast_to(ddec, (GLA_C, GLA_D))
                dst[h] = dsn * dec_c + _dot(do16, qd16, _TN)
        dqd, dkd, dke = dqd_s[...], dkd_s[...], dke_s[...]
        dq_ref[...] = dqd * eb * scale
        dk_ref[...] = dkd * enb + dke * ee
        db = dqd * qd - dkd * kd - dke * ke
        dcum = jnp.concatenate([_dot_exact(db[r:r + GLA_C], trif, _TN, m_first=True) for r in range(0, tb, GLA_C)], axis=0)
        dla_ref[...] = dcum + _per_chunk_sum(dke * ke) + dd_s[...]

    bmap = (lambda n: n) if rev else (lambda n: nblk - 1 - n)

    def zspec(cb):
        return pl.BlockSpec((tb, GLA_W), lambda n: (bmap(n), cb))

    o4 = jax.ShapeDtypeStruct((n_tok, GLA_W), F32)
    half, full = pltpu.VMEM((tb, GLA_W), BF16), pltpu.VMEM((tb, GLA_W), F32)
    return _pcall(
        body, name=name, grid=(nblk,),
        in_specs=[zspec(1), zspec(2), zspec(3), zspec(0),
                  pl.BlockSpec((GLA_H, per, GLA_D, GLA_D), lambda n: (0, nblk - 1 - n, 0, 0)), zspec(0)],
        out_specs=[zspec(0)] * 4, out_shape=[o4] * 4,
        scratch_shapes=[pltpu.VMEM((GLA_H, GLA_D, GLA_D), F32), half, half, half, half, half, full, full, full, full, full],
        compiler_params=_cparams(("arbitrary",), 64 * tb * GLA_W * 4),
    )(z, z, z, la, s_prev, do)


_PAIRS = AT_W // LANE
_PAIRS_PER_KV = _PAIRS // AT_KH


def _attn_fwd(name, q, kd, vd, tq=512):
    n_tok = q.shape[0]
    tq = _tile(n_tok, tq, 16)

    def body(q_ref, k_ref, v_ref, o_ref, l_ref):
        qv, kv, vv = q_ref[...], k_ref[...], v_ref[...]
        low = lax.broadcasted_iota(jnp.int32, qv.shape, 1) < AT_D
        res = []
        for keep in (low, jnp.logical_not(low)):
            s = _dot(jnp.where(keep, qv, jnp.zeros_like(qv)), kv, _NT)
            m = jnp.max(s, axis=-1, keepdims=True)
            p = jnp.exp(s - m)
            den = jnp.sum(p, axis=-1, keepdims=True)
            res.append((_dot(p.astype(BF16), vv) / den, m + jnp.log(den)))
        o_ref[...] = jnp.where(low, res[0][0], res[1][0])
        l_ref[...] = jnp.where(low, res[0][1], res[1][1])

    qspec = pl.BlockSpec((tq, LANE), lambda p, i: (i, p))
    kspec = pl.BlockSpec((n_tok, LANE), lambda p, i: (0, p // _PAIRS_PER_KV))
    vm = 5 * _nbytes((tq, n_tok), F32) + 8 * _nbytes((n_tok, LANE), BF16)
    out = jax.ShapeDtypeStruct((n_tok, AT_W), F32)
    return _pcall(
        body, name=name, grid=(_PAIRS, n_tok // tq), in_specs=[qspec, kspec, kspec], out_specs=[qspec, qspec],
        out_shape=[out, out], compiler_params=_cparams(("parallel", "parallel"), vm),
    )(q, kd, vd)


def _attn_bwd(name, q, kd, vd, o, lse, do, tq=256):
    n_tok = q.shape[0]
    tq = _tile(n_tok, tq, 16)

    def body(q_ref, k_ref, v_ref, o_ref, l_ref, do_ref, dq_ref, dk_ref, dv_ref):
        @pl.when((pl.program_id(1) == 0) & (pl.program_id(2) == 0))
        def _():
            dk_ref[...] = jnp.zeros(dk_ref.shape, F32)
            dv_ref[...] = jnp.zeros(dv_ref.shape, F32)

        qv, kv, vv = q_ref[...], k_ref[...], v_ref[...]
        dof, lv = do_ref[...], l_ref[...]
        do16 = dof.astype(BF16)
        doo = dof * o_ref[...]
        low = lax.broadcasted_iota(jnp.int32, qv.shape, 1) < AT_D
        dq = jnp.zeros(qv.shape, F32)
        for keep in (low, jnp.logical_not(low)):
            qm = jnp.where(keep, qv, jnp.zeros_like(qv))
            dom = jnp.where(keep, do16, jnp.zeros_like(do16))
            lh = jnp.max(jnp.where(keep, lv, -1e30), axis=-1, keepdims=True)
            p = jnp.exp(_dot(qm, kv, _NT) - lh)
            delta = jnp.sum(jnp.where(keep, doo, 0.0), axis=-1, keepdims=True)
            ds16 = (p * (_dot(dom, vv, _NT) - delta)).astype(BF16)
            dq = jnp.where(keep, _dot(ds16, kv), dq)
            dk_ref[...] += _dot(ds16, qm, _TN)
            dv_ref[...] += _dot(p.astype(BF16), dom, _TN)
        dq_ref[...] = dq

    qspec = pl.BlockSpec((tq, LANE), lambda g, j, i: (i, g * _PAIRS_PER_KV + j))
    kspec = pl.BlockSpec((n_tok, LANE), lambda g, j, i: (0, g))
    vm = 7 * _nbytes((tq, n_tok), F32) + 12 * _nbytes((n_tok, LANE), F32)
    dup = jax.ShapeDtypeStruct((n_tok, AT_KH * LANE), F32)
    return _pcall(
        body, name=name, grid=(AT_KH, _PAIRS_PER_KV, n_tok // tq),
        in_specs=[qspec, kspec, kspec, qspec, qspec, qspec], out_specs=[qspec, kspec, kspec],
        out_shape=[jax.ShapeDtypeStruct((n_tok, AT_W), F32), dup, dup],
        compiler_params=_cparams(("parallel", "arbitrary", "arbitrary"), vm),
    )(q, kd, vd, o, lse, do)


def _s5_prep(lam_re, lam_im, log_dt, b_re, b_im, c_re, c_im):
    dt = jnp.exp(log_dt)[:, None]
    er = jnp.exp(lam_re * dt)
    a_re, a_im = er * jnp.cos(lam_im * dt), er * jnp.sin(lam_im * dt)
    nr, ni = a_re - 1.0, a_im
    den = lam_re * lam_re + lam_im * lam_im
    f_re, f_im = (nr * lam_re + ni * lam_im) / den, (ni * lam_re - nr * lam_im) / den
    bb_re = f_re[..., None] * b_re - f_im[..., None] * b_im
    bb_im = f_re[..., None] * b_im + f_im[..., None] * b_re
    def tiles(t, axis):
        _, a, b = t.shape
        eye = jnp.eye(S5_TG, dtype=t.dtype)
        blocks = t.reshape(S5_TILES, S5_TG, a, 1, b) * eye[None, :, None, :, None]
        if axis == 0:
            return blocks.reshape(S5_TILES * S5_TG * a, S5_TG * b)
        return blocks.transpose(1, 2, 0, 3, 4).reshape(S5_TG * a, S5_TILES * S5_TG * b)

    w_in = jnp.concatenate([tiles(bb_re.transpose(0, 2, 1), 1), tiles(bb_im.transpose(0, 2, 1), 1)], axis=1)
    w_out = jnp.concatenate([tiles(c_re.transpose(0, 2, 1), 0), -tiles(c_im.transpose(0, 2, 1), 0)], axis=0)
    return a_re.reshape(1, S5_N), a_im.reshape(1, S5_N), w_in, w_out


def _rope_tables(n_tok):
    quarter = AT_D // 4
    d = np.arange(AT_D)
    e = d % (AT_D // 2)
    inv = (ROPE_BASE ** (-(e % quarter).astype(np.float64) * 2.0 / (AT_D // 2))).astype(np.float32)
    sign = np.where(e < quarter, -1.0, 1.0).astype(np.float32)
    rot = np.zeros((AT_D, AT_D), np.float32)
    rot[np.where(e < quarter, d + quarter, d - quarter), d] = 1.0
    pos = jnp.arange(n_tok, dtype=jnp.int32)
    axis_pos = jnp.where(jnp.asarray(d // (AT_D // 2) == 0)[None, :], (pos // GRID_W)[:, None], (pos % GRID_W)[:, None])
    ang = axis_pos.astype(F32) * jnp.asarray(inv)[None, :]
    return jnp.cos(ang), jnp.sin(ang) * jnp.asarray(sign)[None, :], rot


def _head_consts(n_tok, heads):
    cos, sin, rot = _rope_tables(n_tok)
    eye = np.eye(heads, dtype=np.float32)
    mean = np.kron(eye, np.full((AT_D, AT_D), 1.0 / AT_D, np.float32))
    return jnp.tile(cos, (1, heads)), jnp.tile(sin, (1, heads)), jnp.asarray(mean), jnp.asarray(np.kron(eye, rot))


def _pad_w_in(wt):
    return jnp.concatenate([wt[:2560], wt[2592:3360], wt[2560:2592], jnp.zeros((96, wt.shape[1]), wt.dtype)], axis=0)


def _unpad_w_in(gt):
    return jnp.concatenate([gt[:2560], gt[3328:3360], gt[2560:3328]], axis=0)


def _ffn_fwd(x, g, wg, wu, wd, d):
    n_tok = x.shape[0]
    fq = wd.shape[-2]
    tm = _tile(n_tok, 512, 16)
    w_bytes = N_CHIPS * fq * D_MODEL * 2

    def up(x_ref, g_ref, wg_ref, wu_ref, h_ref, a_ref, b_ref, s_ref):
        hv = _rms(x_ref[...], g_ref[...]).astype(BF16)
        h_ref[...] = hv
        for j in range(N_CHIPS):
            av, bv = _dot(hv, wg_ref[j, 0], _NT), _dot(hv, wu_ref[j, 0], _NT)
            a_ref[j] = av.astype(BF16)
            b_ref[j] = bv.astype(BF16)
            s_ref[j] = (av * _sigmoid(av) * bv).astype(BF16)

    wspec = pl.BlockSpec((N_CHIPS, 1, fq, D_MODEL), lambda i: (0, d, 0, 0))
    aspec = pl.BlockSpec((N_CHIPS, tm, fq), lambda i: (0, i, 0))
    xspec = pl.BlockSpec((tm, D_MODEL), lambda i: (i, 0))
    act = jax.ShapeDtypeStruct((N_CHIPS, n_tok, fq), BF16)
    h, a, b, s = _pcall(
        up, name="ffn_up", grid=(n_tok // tm,),
        in_specs=[xspec, pl.BlockSpec((1, D_MODEL), lambda i: (0, 0)), wspec, wspec],
        out_specs=[xspec, aspec, aspec, aspec],
        out_shape=[jax.ShapeDtypeStruct((n_tok, D_MODEL), BF16), act, act, act],
        compiler_params=_cparams(("parallel",), 4 * w_bytes + 6 * N_CHIPS * tm * fq * 2 + 12 * tm * fq * 4
                                 + 8 * tm * D_MODEL * 4),
    )(x, g, wg, wu)

    def down(s_ref, w_ref, x_ref, o_ref):
        acc = _dot(s_ref[0], w_ref[0, 0])
        for j in range(1, N_CHIPS):
            acc = acc + _dot(s_ref[j], w_ref[j, 0])
        o_ref[...] = x_ref[...] + 0.5 * acc

    xo = _pcall(
        down, name="ffn_down", grid=(n_tok // tm,), in_specs=[aspec, wspec, xspec], out_specs=xspec,
        out_shape=jax.ShapeDtypeStruct((n_tok, D_MODEL), F32),
        compiler_params=_cparams(("parallel",), 2 * w_bytes + 2 * N_CHIPS * tm * fq * 2 + 8 * tm * D_MODEL * 4),
    )(s, wd, x)
    return xo, (x, h, a, b, s)


def _ffn_bwd(dy, dy16, saved, g, wg, wu, wd, d, into):
    x, h, a, b, s = saved
    n_tok = x.shape[0]
    fq = wd.shape[-2]
    tm = _tile(n_tok, 512, 16)

    ta_ = _tile(n_tok, 512, 16)

    def dact(dy_ref, w_ref, a_ref, b_ref, da_ref, db_ref):
        dyv = dy_ref[...]
        for j in range(N_CHIPS):
            ds = 0.5 * _dot(dyv, w_ref[j, 0], _NT)
            av, bv = a_ref[j].astype(F32), b_ref[j].astype(F32)
            sg = _sigmoid(av)
            da_ref[j] = (ds * bv * sg * (1.0 + av * (1.0 - sg))).astype(BF16)
            db_ref[j] = (ds * av * sg).astype(BF16)

    aspec = pl.BlockSpec((N_CHIPS, ta_, fq), lambda i: (0, i, 0))
    act = jax.ShapeDtypeStruct((N_CHIPS, n_tok, fq), BF16)
    da, db = _pcall(
        dact, name="ffn_dact", grid=(n_tok // ta_,),
        in_specs=[pl.BlockSpec((ta_, D_MODEL), lambda i: (i, 0)),
                  pl.BlockSpec((N_CHIPS, 1, fq, D_MODEL), lambda i: (0, d, 0, 0)), aspec, aspec],
        out_specs=[aspec, aspec], out_shape=[act, act],
        compiler_params=_cparams(("parallel",), 2 * N_CHIPS * fq * D_MODEL * 2 + 8 * N_CHIPS * ta_ * fq * 2
                                 + 12 * ta_ * fq * 4),
    )(dy16, wd, a, b)

    any_spec = pl.BlockSpec(memory_space=pl.ANY)
    alias = {} if into is None else {2: 0}
    tn = _tile(D_MODEL, 512, LANE)

    def dw_call(t, rhs, scale, prev):
        def dw(t_ref, r_ref, *rest):
            rest[-1][0, 0] = (scale * _dot(t_ref[0], r_ref[...], _TN)).astype(GRAD_DT)

        return _pcall(
            dw, name="ffn_dw", grid=(N_CHIPS, D_MODEL // tn),
            in_specs=[pl.BlockSpec((1, n_tok, fq), lambda j, n: (j, 0, 0)), pl.BlockSpec((n_tok, tn), lambda j, n: (0, n))]
            + ([] if prev is None else [any_spec]),
            out_specs=pl.BlockSpec((1, 1, fq, tn), lambda j, n: (j, d, 0, n)),
            out_shape=jax.ShapeDtypeStruct((N_CHIPS, DEPTH, fq, D_MODEL), GRAD_DT), input_output_aliases=alias,
            compiler_params=_cparams(("parallel", "parallel"), 6 * n_tok * (fq + tn) * 2 + 4 * fq * tn * 4),
        )(t, rhs, *([] if prev is None else [prev]))

    dwg = dw_call(da, h, 1.0, None if into is None else into[0])
    dwu = dw_call(db, h, 1.0, None if into is None else into[1])
    dwd = dw_call(s, dy16, 0.5, None if into is None else into[2])
    tr = _tile(n_tok, 256, 16)

    def dh_norm(da_ref, db_ref, wg_ref, wu_ref, dy_ref, x_ref, g_ref, dx_ref, dx16_ref, dg_ref):
        dh = _dot(da_ref[0], wg_ref[0, 0]) + _dot(db_ref[0], wu_ref[0, 0])
        for j in range(1, N_CHIPS):
            dh = dh + _dot(da_ref[j], wg_ref[j, 0]) + _dot(db_ref[j], wu_ref[j, 0])
        dx, dg = _rms_bwd(x_ref[...], g_ref[...], dh)
        dx = dy_ref[...] + dx
        dx_ref[...] = dx
        dx16_ref[...] = dx.astype(BF16)

        @pl.when(pl.program_id(0) == 0)
        def _():
            dg_ref[...] = jnp.zeros(dg_ref.shape, F32)

        dg_ref[...] += dg

    tspec = pl.BlockSpec((N_CHIPS, tr, fq), lambda i: (0, i, 0))
    wspec = pl.BlockSpec((N_CHIPS, 1, fq, D_MODEL), lambda i: (0, d, 0, 0))
    rspec = pl.BlockSpec((tr, D_MODEL), lambda i: (i, 0))
    vspec = pl.BlockSpec((1, D_MODEL), lambda i: (0, 0))
    dx, dx16, dg = _pcall(
        dh_norm, name="ffn_dh_norm", grid=(n_tok // tr,),
        in_specs=[tspec, tspec, wspec, wspec, rspec, rspec, vspec], out_specs=[rspec, rspec, vspec],
        out_shape=[jax.ShapeDtypeStruct((n_tok, D_MODEL), F32), jax.ShapeDtypeStruct((n_tok, D_MODEL), BF16),
                   jax.ShapeDtypeStruct((1, D_MODEL), F32)],
        compiler_params=_cparams(("arbitrary",), 4 * N_CHIPS * D_MODEL * fq * 2 + 4 * N_CHIPS * tr * fq * 2
                                 + 16 * tr * D_MODEL * 4),
    )(da, db, wg, wu, dy, x, g)
    return dx, dx16, dg[0], (dwg, dwu, dwd)


def _alpha_pads(w_alpha):
    z = jnp.zeros((LANE - 2 * GLA_R, GLA_W), F32)
    wf = jnp.concatenate([w_alpha[0], jnp.zeros((GLA_R, GLA_W), F32), z], axis=0)
    wb = jnp.concatenate([jnp.zeros((GLA_R, GLA_W), F32), w_alpha[1], z], axis=0)
    return wf, wb


def _dup_heads(t):
    low = lax.broadcasted_iota(jnp.int32, t.shape, 1) < AT_D
    h0, h1 = jnp.where(low, t, 0.0), jnp.where(low, 0.0, t)
    return jnp.concatenate([h0 + pltpu.roll(h0, AT_D, 1), h1 + pltpu.roll(h1, AT_D, 1)], axis=1)


def _fold_heads(t):
    low = lax.broadcasted_iota(jnp.int32, (t.shape[0], LANE), 1) < AT_D
    g0, g1 = t[:, :LANE], t[:, LANE:]
    return jnp.where(low, g0 + pltpu.roll(g0, AT_D, 1), g1 + pltpu.roll(g1, AT_D, 1))


def _mixer_fwd(x, w):
    n_tok = x.shape[0]
    wip, wmg = w['w_in_p'], w['w_merge_gate']
    tr = _tile(n_tok, 256, 16)

    def project(x_ref, g_ref, wip_ref, wmg_ref, h_ref, z_ref, gp_ref):
        hv = _rms(x_ref[...], g_ref[...]).astype(BF16)
        h_ref[...] = hv
        z_ref[...] = _dot(hv, wip_ref[...], _NT).astype(BF16)
        gp_ref[...] = _dot(hv, wmg_ref[...]).astype(BF16)

    def rows(c):
        return pl.BlockSpec((tr, c), lambda i: (i, 0))

    def whole(t):
        return pl.BlockSpec(t.shape, lambda i: (0, 0))

    h, z, gp = _pcall(
        project, name="mix_in", grid=(n_tok // tr,),
        in_specs=[rows(D_MODEL), whole(w['mix_norm']), whole(wip), whole(wmg)],
        out_specs=[rows(D_MODEL), rows(ZP_W), rows(3 * D_MODEL)],
        out_shape=[jax.ShapeDtypeStruct((n_tok, D_MODEL), BF16), jax.ShapeDtypeStruct((n_tok, ZP_W), BF16),
                   jax.ShapeDtypeStruct((n_tok, 3 * D_MODEL), BF16)],
        compiler_params=_cparams(("parallel",), 4 * (wip.size + wmg.size) + 16 * tr * (ZP_W + 3 * D_MODEL)),
    )(x, w['mix_norm'], wip, wmg)
    u = _cw(z, 0, S5_W)

    xs, yy = [], None
    for d, rev in ((0, False), (1, True)):
        a_re, a_im, wb, wc = w['s5'][d]
        x_d, yy = _s5_fwd("s5_rev" if rev else "s5_fwd", z, a_re, a_im, wb, wc, yy, rev=rev)
        xs.append(x_d)

    def s5_post(yv, uv, dv, wglu):
        y0 = yv + dv * uv
        y1 = _gelu(y0)
        t = _dot(y1.astype(BF16), wglu)
        return y0, t, y1 * _sigmoid(t)

    y0, t_glu, y_s5 = _rowwise("s5_post", s5_post, [yy, u], [w['s5_d'], w['s5_w_glu']],
                               [(S5_W, F32), (S5_W, F32), (S5_W, BF16)])

    wa_f, wa_b = _alpha_pads(w['gla_w_alpha'])
    ba_f, ba_b = w['gla_b_alpha'][0:1], w['gla_b_alpha'][1:2]
    zz = _cw(z, ZC_Z, LANE)

    def gla_prep(zv, waf, wab, bf, bb):
        z16 = zv.astype(BF16)
        lf = _dot(z16, waf.astype(BF16)) + bf
        lb = _dot(z16, wab.astype(BF16)) + bb
        return _log_sigmoid(lf) / GLA_TAU, _log_sigmoid(lb) / GLA_TAU

    la_f, la_b = _rowwise("gla_prep", gla_prep, [zz], [wa_f, wa_b, ba_f, ba_b], [(GLA_W, F32), (GLA_W, F32)])
    o_f, s_f = _gla_fwd("gla_fwd", z, la_f, rev=False)
    o_b, s_b = _gla_fwd("gla_rev", z, la_b, rev=True)
    gate = _cw(z, 4, GLA_W)

    def gla_post(of, ob, gv, gn):
        o, gv = of + ob, gv.astype(F32)
        outs = []
        for hh in range(GLA_H):
            sl = slice(hh * GLA_D, (hh + 1) * GLA_D)
            gh = gv[:, sl]
            outs.append(_rms(o[:, sl], gn) * (gh * _sigmoid(gh)))
        return jnp.concatenate(outs, axis=1)

    y_gla = _rowwise("gla_post", gla_post, [o_f, o_b, gate], [w['gla_norm']], [(GLA_W, BF16)])[0]

    cq, sq, mq, rq = _head_consts(n_tok, AT_QH)
    ck, sk, mk, rk = _head_consts(n_tok, AT_KH)
    gq = jnp.tile(w['attn_q_norm'], (1, AT_QH))
    gk = jnp.tile(w['attn_k_norm'], (1, AT_KH))

    def rope_fwd(v, gain, cos, sin, mean, rot, scale):
        r = lax.rsqrt(_dot_exact(v * v, mean, terms=2) + EPS)
        vn = v * r * gain
        return (vn * cos + _dot_exact(vn, rot, terms=2) * sin) * scale

    def attn_prep(qv, kv, vv, cqv, sqv, ckv, skv, gqv, gkv, mqv, rqv, mkv, rkv):
        qv, kv, vv = qv.astype(F32), kv.astype(F32), vv.astype(F32)
        kr = rope_fwd(kv, gkv, ckv, skv, mkv, rkv, 1.0)
        return rope_fwd(qv, gqv, cqv, sqv, mqv, rqv, AT_D ** -0.5), _dup_heads(kr), _dup_heads(vv)

    q_raw, k_raw, v_raw = _cw(z, 5, AT_W), _cw(z, ZC_AK, AT_KW), _cw(z, ZC_AV, AT_KW)
    q16, kd, vd = _rowwise("attn_prep", attn_prep, [q_raw, k_raw, v_raw, cq, sq, ck, sk], [gq, gk, mq, rq, mk, rk],
                           [(AT_W, BF16), (2 * LANE, BF16), (2 * LANE, BF16)])
    y_at, lse = _attn_fwd("attn_fwd", q16, kd, vd)

    p0 = _mm("mix_branch", y_s5, w['w_branch_s5'], out_dtype=BF16)
    p1 = _mm("mix_branch", y_gla, w['w_branch_gla'], out_dtype=BF16)
    p2 = _mm("mix_branch_f32", y_at, w['w_branch_attn'], out_dtype=BF16)

    def merge(gpv, a0, a1, a2, bias):
        g = _sigmoid(gpv.astype(F32) + bias)
        return (g[:, :D_MODEL] * a0.astype(F32) + g[:, D_MODEL:2 * D_MODEL] * a1.astype(F32)
                + g[:, 2 * D_MODEL:] * a2.astype(F32))

    merged = _rowwise("mix_merge", merge, [gp, p0, p1, p2], [w['b_merge_gate']], [(D_MODEL, BF16)])[0]
    xo = _mm("mix_out", merged, w['w_out'], add=x)
    saved = dict(x=x, h=h, z=z, gp=gp, xs=xs, y0=y0, t_glu=t_glu, y_s5=y_s5, la=(la_f, la_b), o=(o_f, o_b), s=(s_f, s_b),
                 y_gla=y_gla, q16=q16, kd=kd, vd=vd, lse=lse, y_at=y_at, p=(p0, p1, p2), merged=merged)
    return xo, saved


def _mixer_bwd(dy, dy16, sv, w, s5_raw):
    z, h = sv['z'], sv['h']
    n_tok = z.shape[0]
    grads = {}
    dmerged = _mm("mix_dmerged", dy16, w['w_out'], tb=True, out_dtype=BF16)
    grads['w_out'] = _mm("mix_dwout", sv['merged'], dy16, ta=True, out_dtype=GRAD_DT)

    def dmerge(dm, gpv, a0, a1, a2, bias):
        dm = dm.astype(F32)
        g = _sigmoid(gpv.astype(F32) + bias)
        ps = (a0.astype(F32), a1.astype(F32), a2.astype(F32))
        dps, dgs = [], []
        for k in range(3):
            gk = g[:, k * D_MODEL:(k + 1) * D_MODEL]
            dps.append(dm * gk)
            dgs.append(dm * ps[k] * gk * (1.0 - gk))
        dgp = jnp.concatenate(dgs, axis=1)
        return dps[0], dps[1], dps[2], dgp, jnp.sum(dgp, axis=0, keepdims=True)

    dp0, dp1, dp2, dgp, dbm = _rowwise("mix_dmerge", dmerge, [dmerged, sv['gp'], *sv['p']], [w['b_merge_gate']],
                                       [(D_MODEL, BF16)] * 3 + [(3 * D_MODEL, BF16)], accs=[(1, 3 * D_MODEL)])
    grads['b_merge_gate'] = dbm[0]
    grads['w_branch_s5'] = _mm("mix_dbranch", sv['y_s5'], dp0, ta=True, out_dtype=GRAD_DT)
    grads['w_branch_gla'] = _mm("mix_dbranch", sv['y_gla'], dp1, ta=True, out_dtype=GRAD_DT)
    grads['w_branch_attn'] = _mm("mix_dbranch_f32", sv['y_at'], dp2, ta=True, out_dtype=GRAD_DT)
    dy_s5 = _mm("mix_dy", dp0, w['w_branch_s5'], tb=True)
    dy_gla = _mm("mix_dy", dp1, w['w_branch_gla'], tb=True)
    dy_at = _mm("mix_dy", dp2, w['w_branch_attn'], tb=True)
    grads['w_merge_gate'] = _mm("mix_dwgate", h, dgp, ta=True, out_dtype=GRAD_DT)
    u = _cw(z, 0, S5_W)

    def s5_post_bwd(dyv, y0, t, uv, dv, wglu):
        sg = _sigmoid(t)
        y1 = _gelu(y0)
        dt = dyv * y1 * sg * (1.0 - sg)
        dy1 = dyv * sg + _dot(dt.astype(BF16), wglu, _NT)
        dy0 = dy1 * _gelu_grad(y0)
        return dy0, dt, y1, dy0 * dv, jnp.sum(dy0 * uv, axis=0, keepdims=True)

    dy0, dt_glu, y1, du, dd = _rowwise("s5_post_bwd", s5_post_bwd, [dy_s5, sv['y0'], sv['t_glu'], u],
                                       [w['s5_d'], w['s5_w_glu']],
                                       [(S5_W, BF16), (S5_W, BF16), (S5_W, BF16), (S5_W, F32)], accs=[(1, S5_W)])
    grads['s5_d'] = dd[0]
    grads['s5_w_glu'] = _mm("s5_dwglu", y1, dt_glu, ta=True, out_dtype=GRAD_DT)
    s5_cot = []
    for d, rev in ((0, False), (1, True)):
        a_re, a_im, wb, wc = w['s5'][d]
        du, dwb, dwc, da = _s5_bwd("s5_adj_fwd" if rev else "s5_adj_rev", dy0, z, sv['xs'][d], a_re, -a_im, wb, wc, du,
                                   rev=not rev)
        da = jnp.sum(da, axis=0, keepdims=True)
        s5_cot.append((da[:, :S5_N], da[:, S5_N:], dwb, dwc))
    for d in range(2):
        _, pull = jax.vjp(_s5_prep, *s5_raw[d])
        s5_cot[d] = pull(s5_cot[d])
    for k, nm in enumerate(('s5_lambda_re', 's5_lambda_im', 's5_log_dt', 's5_b_re', 's5_b_im', 's5_c_re', 's5_c_im')):
        grads[nm] = jnp.stack([s5_cot[0][k], s5_cot[1][k]])

    gate = _cw(z, 4, GLA_W)

    def gla_post_bwd(dyv, of, ob, gv, gn):
        o, gv = of + ob, gv.astype(F32)
        dos, dgates = [], []
        dgn = jnp.zeros((1, GLA_D), F32)
        for hh in range(GLA_H):
            sl = slice(hh * GLA_D, (hh + 1) * GLA_D)
            gh, dyh = gv[:, sl], dyv[:, sl]
            sg = _sigmoid(gh)
            dgates.append(dyh * _rms(o[:, sl], gn) * sg * (1.0 + gh * (1.0 - sg)))
            dx, dg = _rms_bwd(o[:, sl], gn, dyh * gh * sg)
            dos.append(dx)
            dgn = dgn + dg
        return jnp.concatenate(dos, axis=1), jnp.concatenate(dgates, axis=1), dgn

    do, dgate, dgn = _rowwise("gla_post_bwd", gla_post_bwd, [dy_gla, *sv['o'], gate], [w['gla_norm']],
                              [(GLA_W, F32), (GLA_W, F32)], accs=[(1, GLA_D)])
    grads['gla_norm'] = dgn[0]
    dq_f, dk_f, dv_f, dla_f = _gla_bwd("gla_bwd_fwd", z, sv['la'][0], sv['s'][0], do, rev=False)
    dq_b, dk_b, dv_b, dla_b = _gla_bwd("gla_bwd_rev", z, sv['la'][1], sv['s'][1], do, rev=True)
    wa_f, wa_b = _alpha_pads(w['gla_w_alpha'])
    ba_f, ba_b = w['gla_b_alpha'][0:1], w['gla_b_alpha'][1:2]
    zz = _cw(z, ZC_Z, LANE)

    def gla_prep_bwd(dlf, dlb, zv, waf, wab, bf, bb):
        z16 = zv.astype(BF16)
        waf16, wab16 = waf.astype(BF16), wab.astype(BF16)
        xf = _dot(z16, waf16) + bf
        xb = _dot(z16, wab16) + bb
        df = dlf * (_sigmoid(-xf) / GLA_TAU)
        db = dlb * (_sigmoid(-xb) / GLA_TAU)
        dz = _dot(df.astype(BF16), waf16, _NT) + _dot(db.astype(BF16), wab16, _NT)
        return df, db, dz, jnp.sum(df, axis=0, keepdims=True), jnp.sum(db, axis=0, keepdims=True)

    dlog_f, dlog_b, dzz, dba_f, dba_b = _rowwise(
        "gla_prep_bwd", gla_prep_bwd, [dla_f, dla_b, zz], [wa_f, wa_b, ba_f, ba_b],
        [(GLA_W, BF16), (GLA_W, BF16), (LANE, F32)], accs=[(1, GLA_W), (1, GLA_W)])
    dwa_f = _mm("gla_dwa", zz, dlog_f, ta=True)
    dwa_b = _mm("gla_dwa", zz, dlog_b, ta=True)
    grads['gla_w_alpha'] = jnp.stack([dwa_f[:GLA_R], dwa_b[GLA_R:2 * GLA_R]])
    grads['gla_b_alpha'] = jnp.concatenate([dba_f, dba_b], axis=0)

    dq_r, dkd, dvd = _attn_bwd("attn_bwd", sv['q16'], sv['kd'], sv['vd'], sv['y_at'], sv['lse'], dy_at)
    cq, sq, mq, rq = _head_consts(n_tok, AT_QH)
    ck, sk, mk, rk = _head_consts(n_tok, AT_KH)
    gq = jnp.tile(w['attn_q_norm'], (1, AT_QH))
    gk = jnp.tile(w['attn_k_norm'], (1, AT_KH))

    def rope_bwd(dr, v, gain, cos, sin, mean, rot, scale):
        dr = dr * scale
        dvn = dr * cos + _dot_exact(dr * sin, rot, _NT, terms=2)
        r = lax.rsqrt(_dot_exact(v * v, mean, terms=2) + EPS)
        vh = v * r
        dg = jnp.sum(dvn * vh, axis=0, keepdims=True)
        dvh = dvn * gain
        return r * (dvh - vh * _dot_exact(dvh * vh, mean, terms=2)), dg

    def attn_prep_bwd(dq, dk2, dv2, qv, kv, cqv, sqv, ckv, skv, gqv, gkv, mqv, rqv, mkv, rkv):
        qv, kv = qv.astype(F32), kv.astype(F32)
        dqo, dgq = rope_bwd(dq, qv, gqv, cqv, sqv, mqv, rqv, AT_D ** -0.5)
        dko, dgk = rope_bwd(_fold_heads(dk2), kv, gkv, ckv, skv, mkv, rkv, 1.0)
        return dqo, dko, _fold_heads(dv2), dgq, dgk

    q_raw, k_raw = _cw(z, 5, AT_W), _cw(z, ZC_AK, AT_KW)
    daq, dak, dav, dgq, dgk = _rowwise("attn_prep_bwd", attn_prep_bwd, [dq_r, dkd, dvd, q_raw, k_raw, cq, sq, ck, sk],
                                       [gq, gk, mq, rq, mk, rk], [(AT_W, F32), (AT_KW, F32), (AT_KW, F32)],
                                       accs=[(1, AT_W), (1, AT_KW)])
    grads['attn_q_norm'] = dgq.reshape(AT_QH, AT_D).sum(axis=0)
    grads['attn_k_norm'] = dgk.reshape(AT_KH, AT_D).sum(axis=0)

    def assemble(duv, qf, qb, kf, kb, vf, vb, dg, aq, ak, av, zzv):
        return jnp.concatenate([duv, qf + qb, kf + kb, vf + vb, dg, aq, ak, av, zzv], axis=1)

    dz = _rowwise("mix_dz", assemble, [du, dq_f, dq_b, dk_f, dk_b, dv_f, dv_b, dgate, daq, dak, dav, dzz], [],
                  [(ZP_W, BF16)])[0]
    grads['w_in'] = _unpad_w_in(_mm("mix_dwin", dz, h, ta=True, out_dtype=GRAD_DT))
    wmg, wip = w['w_merge_gate'], w['w_in_p']
    tr = _tile(n_tok, 256, 16)

    def dh_norm(dgp_ref, dz_ref, wmg_ref, wip_ref, dy_ref, x_ref, g_ref, dx_ref, dx16_ref, dg_ref):
        dh = _dot(dgp_ref[...], wmg_ref[...], _NT) + _dot(dz_ref[...], wip_ref[...])
        dxv, dgv = _rms_bwd(x_ref[...], g_ref[...], dh)
        dxv = dy_ref[...] + dxv
        dx_ref[...] = dxv
        dx16_ref[...] = dxv.astype(BF16)

        @pl.when(pl.program_id(0) == 0)
        def _():
            dg_ref[...] = jnp.zeros(dg_ref.shape, F32)

        dg_ref[...] += dgv

    def rows(c):
        return pl.BlockSpec((tr, c), lambda i: (i, 0))

    def whole(t):
        return pl.BlockSpec(t.shape, lambda i: (0, 0))

    dx, dx16, dg = _pcall(
        dh_norm, name="mix_dh_norm", grid=(n_tok // tr,),
        in_specs=[rows(dgp.shape[1]), rows(dz.shape[1]), whole(wmg), whole(wip), rows(D_MODEL), rows(D_MODEL),
                  whole(w['mix_norm'])],
        out_specs=[rows(D_MODEL), rows(D_MODEL), whole(w['mix_norm'])],
        out_shape=[jax.ShapeDtypeStruct((n_tok, D_MODEL), F32), jax.ShapeDtypeStruct((n_tok, D_MODEL), BF16),
                   jax.ShapeDtypeStruct((1, D_MODEL), F32)],
        compiler_params=_cparams(("arbitrary",), 4 * (wmg.size + wip.size) + 8 * tr * (dgp.shape[1] + dz.shape[1])
                                 + 16 * tr * D_MODEL * 4),
    )(dgp, dz, wmg, wip, dy, sv['x'], w['mix_norm'])
    grads['mix_norm'] = dg[0]
    return dx, dx16, grads


FFN_NAMES = [p + k for p in ('ffn1_', 'ffn2_') for k in ('w_gate', 'w_up', 'w_down')]
MISC_SHARDED = [n for n in SHARDED if n not in FFN_NAMES]
LARGE_SHARDED = FFN_NAMES + ['w_in', 'w_merge_gate', 'w_out', 'w_branch_s5', 'w_branch_gla', 'w_branch_attn', 's5_w_glu']
KEPT_TRANSPOSED = [p + k for p in ('ffn1_', 'ffn2_') for k in ('w_gate', 'w_up')] + ['w_in']


def _stored_axis(n):
    return 1 if n in KEPT_TRANSPOSED else SHARD_AXIS[n]


def _stored(n, t):
    return jnp.swapaxes(t, 1, 2) if n in KEPT_TRANSPOSED else t


def _layer_weights(ws, rep, i):
    w = {k: _from_shards(ws[k][:, i], _stored_axis(k) - 1) for k in MISC_SHARDED}
    for k in ('ffn1_norm', 'mix_norm', 'ffn2_norm', 's5_d', 'gla_norm', 'attn_q_norm', 'attn_k_norm', 'b_merge_gate'):
        w[k] = rep[k][i].reshape(1, -1)
    w['w_in_p'] = _pad_w_in(w['w_in'])
    s5_raw = [tuple(rep[k][i, d] for k in ('s5_lambda_re', 's5_lambda_im', 's5_log_dt', 's5_b_re', 's5_b_im',
                                            's5_c_re', 's5_c_im')) for d in range(2)]
    w['s5'] = [_s5_prep(*s5_raw[d]) for d in range(2)]
    return w, s5_raw


def _device_step(x, target, ws, rep):
    lw = [_layer_weights(ws, rep, i) for i in range(DEPTH)]
    ffn = {p: (ws[p + 'w_gate'], ws[p + 'w_up'], ws[p + 'w_down']) for p in ('ffn1_', 'ffn2_')}
    saved = []
    for i in range(DEPTH):
        w, _ = lw[i]
        x, s1 = _ffn_fwd(x, w['ffn1_norm'], *ffn['ffn1_'], i)
        x, s2 = _mixer_fwd(x, w)
        x, s3 = _ffn_fwd(x, w['ffn2_norm'], *ffn['ffn2_'], i)
        saved.append((s1, s2, s3))
    gfin = rep['final_norm'].reshape(1, -1)

    def head(xv, tv, gv):
        e = _rms(xv, gv) - tv
        dx, dg = _rms_bwd(xv, gv, e * (1.0 / D_MODEL))
        part = jnp.sum(e * e, axis=0, keepdims=True)
        return dx, dx, dg, part

    dx, dx16, dgfin, part = _rowwise("loss_head", head, [x, target], [gfin], [(D_MODEL, F32), (D_MODEL, BF16)],
                                     accs=[(1, D_MODEL), (1, D_MODEL)])
    loss = (0.5 / D_MODEL) * jnp.sum(part)
    per_layer = []
    dffn = {'ffn1_': None, 'ffn2_': None}
    for i in reversed(range(DEPTH)):
        w, s5_raw = lw[i]
        s1, s2, s3 = saved[i]
        dx, dx16, n3, dffn['ffn2_'] = _ffn_bwd(dx, dx16, s3, w['ffn2_norm'], *ffn['ffn2_'], i, dffn['ffn2_'])
        dx, dx16, g = _mixer_bwd(dx, dx16, s2, w, s5_raw)
        dx, dx16, n1, dffn['ffn1_'] = _ffn_bwd(dx, dx16, s1, w['ffn1_norm'], *ffn['ffn1_'], i, dffn['ffn1_'])
        g['ffn2_norm'], g['ffn1_norm'] = n3, n1
        per_layer.append(g)
    per_layer.reverse()
    stacked = {k: jnp.stack([per_layer[i][k] for i in range(DEPTH)]) for k in per_layer[0]}
    gs = {k: _to_shards(stacked[k], _stored_axis(k)) for k in MISC_SHARDED}
    for p in ('ffn1_', 'ffn2_'):
        gs[p + 'w_gate'], gs[p + 'w_up'], gs[p + 'w_down'] = dffn[p]
    grep = {k: stacked[k] for k in REPLICATED if k != 'final_norm'}
    grep['final_norm'] = dgfin[0]
    return loss, dx, gs, grep


PACK_C = 1024
ROW_TILE = 512


def _rows_of(shape):
    return -(-int(np.prod(shape)) // PACK_C)


def _as_rows(t, lead=0):
    head = t.shape[:lead]
    flat = t.reshape(head + (-1,))
    rows = -(-flat.shape[-1] // PACK_C)
    pad = rows * PACK_C - flat.shape[-1]
    if pad:
        flat = jnp.pad(flat, [(0, 0)] * lead + [(0, pad)])
    return flat.reshape(head + (rows, PACK_C))


def _pack(items, total_rows, lead=0):
    parts = [_as_rows(t, lead) for t in items]
    used = sum(p.shape[lead] for p in parts)
    if total_rows > used:
        parts.append(jnp.zeros(parts[0].shape[:lead] + (total_rows - used, PACK_C), parts[0].dtype))
    return jnp.concatenate(parts, axis=lead)


def _unpack(rows, shapes, lead=0):
    out, r0 = [], 0
    head = rows.shape[:lead]
    for shp in shapes:
        n, size = _rows_of(shp), int(np.prod(shp))
        piece = lax.slice_in_dim(rows, r0, r0 + n, axis=lead).reshape(head + (n * PACK_C,))
        out.append(lax.slice_in_dim(piece, 0, size, axis=lead).reshape(head + tuple(shp)))
        r0 += n
    return out


def _round_up(n, m):
    return -(-n // m) * m


def _to_shards(g, axis):
    shp = g.shape
    g = g.reshape(shp[:axis] + (N_CHIPS, shp[axis] // N_CHIPS) + shp[axis + 1:])
    return jnp.moveaxis(g, axis, 0)


def _from_shards(s, axis):
    s = jnp.moveaxis(s, 0, axis)
    shp = s.shape
    return s.reshape(shp[:axis] + (shp[axis] * shp[axis + 1],) + shp[axis + 2:])


_HBM = pl.BlockSpec(memory_space=pltpu.HBM)


def _place():
    return lax.axis_index("x"), lax.axis_index("y"), lax.axis_index("c")


def _rcopy(src, dst, send, recv, k, dev):
    return pltpu.make_async_remote_copy(src_ref=src, dst_ref=dst, send_sem=send.at[k], recv_sem=recv.at[k],
                                        device_id=dev, device_id_type=MESH)


def _comm_call(name, body, arrays, out_shapes, n_sems, n_local):
    return pl.pallas_call(
        body, name=name, in_specs=[_HBM] * len(arrays), out_specs=[_HBM] * len(out_shapes), out_shape=out_shapes,
        scratch_shapes=[pltpu.SemaphoreType.DMA((n_sems,)), pltpu.SemaphoreType.DMA((n_sems,)),
                        pltpu.SemaphoreType.DMA((max(n_local, 1),))],
    )(*arrays)


def _ici_peers(x, y, c):
    x_first = c == 0
    first = (jnp.where(x_first, 1 - x, x), jnp.where(x_first, y, 1 - y))
    second = (jnp.where(x_first, x, 1 - x), jnp.where(x_first, 1 - y, y))
    return first, second


def _ag_chips(name, packs):
    n = len(packs)

    def body(*refs):
        p, o, (send, recv, _) = refs[:n], refs[n:2 * n], refs[2 * n:]
        x, y, c = _place()
        (ax, ay), (bx, by) = _ici_peers(x, y, c)
        me, na, nb, nd = 2 * x + y, 2 * ax + ay, 2 * bx + by, 2 * (1 - x) + (1 - y)
        sib = (x, y, 1 - c)
        own = [_rcopy(p[a].at[c], o[a].at[me, c], send, recv, 6 * a, (ax, ay, c)) for a in range(n)]
        own += [_rcopy(p[a].at[c], o[a].at[me, c], send, recv, 6 * a + 1, (bx, by, c)) for a in range(n)]
        for cp in own:
            cp.start()
        passed = []

        def landed(a, chip, k, fwd):
            blk = o[a].at[chip, c]
            _rcopy(blk, blk, send, recv, 6 * a + k, sib).wait_recv()
            for sem, dev in fwd:
                cp = _rcopy(blk, blk, send, recv, 6 * a + sem, dev)
                cp.start()
                passed.append(cp)

        for a in range(n):
            landed(a, na, 0, [(2, (bx, by, c)), (3, sib)])
        for a in range(n):
            landed(a, nb, 1, [(4, sib)])
        for a in range(n):
            landed(a, nd, 2, [(5, sib)])
        for a in range(n):
            for k, chip in ((3, nb), (4, na), (5, nd)):
                blk = o[a].at[chip, 1 - c]
                _rcopy(blk, blk, send, recv, 6 * a + k, sib).wait_recv()
        for cp in own + passed:
            cp.wait_send()

    outs = _comm_call(name, body, packs, [jax.ShapeDtypeStruct((N_CHIPS,) + t.shape, t.dtype) for t in packs], 6 * n, 0)
    chip = 2 * lax.axis_index("x") + lax.axis_index("y")
    return [lax.dynamic_update_slice_in_dim(o, t[None], chip, axis=0) for o, t in zip(outs, packs)]


def _pair_swap(name, gs):
    n = len(gs)
    ns = gs[0].shape[0]

    def body(*refs):
        g, r, (send, recv, _) = refs[:n], refs[n:2 * n], refs[2 * n:]
        x, y, c = _place()
        cps = [_rcopy(g[a].at[j, 1 - c], r[a].at[j], send, recv, ns * a + j, (x, y, 1 - c))
               for a in range(n) for j in range(ns)]
        for cp in cps:
            cp.start()
        for cp in cps:
            cp.wait_recv()
        for cp in cps:
            cp.wait_send()

    return _comm_call(name, body, gs, [jax.ShapeDtypeStruct((ns,) + t.shape[2:], t.dtype) for t in gs], ns * n, 0)


def _halve_first(name, parts):
    n = len(parts)

    def body(*refs):
        p, r, (send, recv, _) = refs[:n], refs[n:2 * n], refs[2 * n:]
        x, y, c = _place()
        (ax, ay), _ = _ici_peers(x, y, c)
        cps = [_rcopy(p[a].at[jnp.where(c == 0, 2 * ax + k, 2 * k + ay)], r[a].at[k], send, recv, 2 * a + k, (ax, ay, c))
               for a in range(n) for k in range(2)]
        for cp in cps:
            cp.start()
        for cp in cps:
            cp.wait_recv()
        for cp in cps:
            cp.wait_send()

    return _comm_call(name, body, parts, [jax.ShapeDtypeStruct((2,) + t.shape[1:], t.dtype) for t in parts], 2 * n, 0)


def _halve_second(name, kept):
    n = len(kept)

    def body(*refs):
        p, r, (send, recv, _) = refs[:n], refs[n:2 * n], refs[2 * n:]
        x, y, c = _place()
        _, (bx, by) = _ici_peers(x, y, c)
        cps = [_rcopy(p[a].at[jnp.where(c == 0, by, bx)], r[a], send, recv, a, (bx, by, c)) for a in range(n)]
        for cp in cps:
            cp.start()
        for cp in cps:
            cp.wait_recv()
        for cp in cps:
            cp.wait_send()

    return _comm_call(name, body, kept, [jax.ShapeDtypeStruct(t.shape[1:], t.dtype) for t in kept], n, 0)


def _add_selected(name, t, r, index, out_dtype):
    single = r.ndim == 2
    r3 = r[None] if single else r
    nk, rh, cc = r3.shape
    tl = _tile(rh, ROW_TILE, 16)

    def body(i_ref, t_ref, r_ref, o_ref):
        o_ref[...] = (t_ref[...].astype(F32) + r_ref[...].astype(F32)).astype(o_ref.dtype)

    out = _pcall(
        body, name=name,
        grid_spec=pltpu.PrefetchScalarGridSpec(
            num_scalar_prefetch=1, grid=(nk, rh // tl),
            in_specs=[pl.BlockSpec((1, tl, cc), lambda k, i, i_ref: (i_ref[k], i, 0)),
                      pl.BlockSpec((1, tl, cc), lambda k, i, i_ref: (k, i, 0))],
            out_specs=pl.BlockSpec((1, tl, cc), lambda k, i, i_ref: (k, i, 0))),
        out_shape=jax.ShapeDtypeStruct(r3.shape, out_dtype),
        compiler_params=_cparams(("parallel", "parallel"), 8 * tl * cc * 4),
    )(index, t, r3)
    return out[0] if single else out


def _pair_share(name, reds):
    n = len(reds)

    def body(*refs):
        r, o, (send, recv, _) = refs[:n], refs[n:2 * n], refs[2 * n:]
        x, y, c = _place()
        cps = [_rcopy(r[a], o[a].at[c], send, recv, a, (x, y, 1 - c)) for a in range(n)]
        for cp in cps:
            cp.start()
        for a in range(n):
            blk = o[a].at[1 - c]
            _rcopy(blk, blk, send, recv, a, (x, y, 1 - c)).wait_recv()
        for cp in cps:
            cp.wait_send()

    outs = _comm_call(name, body, reds, [jax.ShapeDtypeStruct((2,) + t.shape, t.dtype) for t in reds], n, 0)
    core = lax.axis_index("c")
    return [lax.dynamic_update_slice_in_dim(o, t[None], core, axis=0) for o, t in zip(outs, reds)]


def _add_own_half(name, g, r):
    ns, _, rh, cc = g.shape
    tl = _tile(rh, ROW_TILE, 16)
    core = lax.axis_index("c").astype(jnp.int32).reshape(1)

    def body(c_ref, g_ref, r_ref, o_ref):
        o_ref[...] = (g_ref[0].astype(F32) + r_ref[...].astype(F32)).astype(o_ref.dtype)

    return _pcall(
        body, name=name,
        grid_spec=pltpu.PrefetchScalarGridSpec(
            num_scalar_prefetch=1, grid=(ns, rh // tl),
            in_specs=[pl.BlockSpec((1, 1, tl, cc), lambda j, i, c_ref: (j, c_ref[0], i, 0)),
                      pl.BlockSpec((1, tl, cc), lambda j, i, c_ref: (j, i, 0))],
            out_specs=pl.BlockSpec((1, tl, cc), lambda j, i, c_ref: (j, i, 0))),
        out_shape=jax.ShapeDtypeStruct(r.shape, BF16),
        compiler_params=_cparams(("parallel", "parallel"), 8 * tl * cc * 4),
    )(core, g, r)


def _reduce_scatter(gs):
    x, y, c = lax.axis_index("x"), lax.axis_index("y"), lax.axis_index("c")
    mine_first = jnp.stack([jnp.where(c == 0, 2 * x + k, 2 * k + y) for k in range(2)]).astype(jnp.int32)
    mine_second = jnp.where(c == 0, y, x).astype(jnp.int32).reshape(1)
    swapped = _pair_swap("rs_pair_swap", gs)
    parts = [_add_own_half("rs_add_pair", g, r) for g, r in zip(gs, swapped)]
    got = _halve_first("rs_halve_first", parts)
    kept = [_add_selected("rs_add_first", t, r, mine_first, BF16) for t, r in zip(parts, got)]
    got = _halve_second("rs_halve_second", kept)
    reds = [_add_selected("rs_add_second", t, r, mine_second, F32) for t, r in zip(kept, got)]
    return _pair_share("rs_pair_share", reds)


def _adamw(name, g, w, m, v):
    c1 = 1.0 - ADAM_B1 ** ADAM_STEP
    c2 = 1.0 - ADAM_B2 ** ADAM_STEP
    nb, rows, cc = g.shape
    tl = _tile(rows, ROW_TILE, SUBLANE)

    def body(g_ref, w_ref, m_ref, v_ref, d_ref, mo_ref, vo_ref):
        gv = g_ref[...]
        mn = ADAM_B1 * m_ref[...] + (1.0 - ADAM_B1) * gv
        vn = ADAM_B2 * v_ref[...] + (1.0 - ADAM_B2) * (gv * gv)
        d_ref[...] = -ADAM_LR * ((mn / c1) / (jnp.sqrt(vn / c2) + ADAM_EPS) + ADAM_WD * w_ref[...])
        mo_ref[...] = mn
        vo_ref[...] = vn

    spec = pl.BlockSpec((1, tl, cc), lambda b, i: (b, i, 0))
    return _pcall(
        body, name=name, grid=(nb, rows // tl), in_specs=[spec] * 4, out_specs=[spec] * 3,
        out_shape=[jax.ShapeDtypeStruct(g.shape, F32)] * 3,
        compiler_params=_cparams(("parallel", "parallel"), 16 * tl * cc * 4),
    )(g, w, m, v)


def _train_step(a):
    x, target = a['x'][0], a['loss_target'][0]

    names = BF16_GATHER + F32_GATHER
    got = _ag_chips("ag_weights", [_stored(n, a[n]).astype(BF16) for n in BF16_GATHER] + [a[n] for n in F32_GATHER])
    ws = dict(zip(names, got))

    loss, gx, gs, grep = _device_step(x, target, ws, {n: a[n] for n in REPLICATED})
    loss = lax.psum(loss, ("x", "y", "c"))

    small = [n for n in MISC_SHARDED if n not in LARGE_SHARDED]
    small_shapes = [a[n].shape[1:] for n in small]
    rep_shapes = [a[n].shape for n in REPLICATED]
    small_rows = sum(_rows_of(s) for s in small_shapes)
    piece = _round_up(-(-sum(_rows_of(s) for s in rep_shapes) // (2 * N_CHIPS)), SUBLANE)
    rep_rows = 2 * N_CHIPS * piece
    rep_pack = _pack([grep[n] for n in REPLICATED], rep_rows).reshape(N_CHIPS, 2, piece, PACK_C)
    total = _round_up(small_rows + piece, ROW_TILE // 2)
    misc = _pack([gs[n] for n in small] + [rep_pack], total, lead=2)
    reds = _reduce_scatter([gs[n] for n in LARGE_SHARDED] + [misc])
    grad = dict(zip(LARGE_SHARDED, reds[:-1]))
    grad.update(zip(small, _unpack(reds[-1], small_shapes, lead=1)))
    rep_mine = reds[-1][:, small_rows:small_rows + piece]
    rep_all = _ag_chips("ag_replicated", [rep_mine])[0].reshape(1, rep_rows, PACK_C)

    outs = {}
    for n in SHARDED:
        shp = grad[n].shape
        three_d = (shp[0], -1, shp[-1])
        params = (grad[n], *(_stored(n, a[p + n]) for p in ('', 'm_', 'v_')))
        res3 = _adamw("adamw", *(t.reshape(three_d) for t in params))
        for kind, arr in zip(('grad', 'delta', 'new_m', 'new_v'), (grad[n], *res3)):
            outs[kind + '_' + n] = _stored(n, arr.reshape(shp))
    w, m, v = (_pack([a[p + n] for n in REPLICATED], rep_rows)[None] for p in ('', 'm_', 'v_'))
    res3 = _adamw("adamw_replicated", rep_all, w, m, v)
    for kind, arr in zip(('grad', 'delta', 'new_m', 'new_v'), (rep_all, *res3)):
        for n, t in zip(REPLICATED, _unpack(arr[0], rep_shapes)):
            outs[kind + '_' + n] = t
    res = [loss, gx[None]]
    for kind in ('grad', 'delta', 'new_m', 'new_v'):
        res += [outs[kind + '_' + n] for n in W_NAMES]
    return tuple(res)


def kernel(x, ffn1_norm, ffn1_w_gate, ffn1_w_up, ffn1_w_down, mix_norm, w_in, s5_lambda_re, s5_lambda_im, s5_log_dt, s5_b_re, s5_b_im, s5_c_re, s5_c_im, s5_d, s5_w_glu, gla_w_alpha, gla_b_alpha, gla_norm, attn_q_norm, attn_k_norm, w_branch_s5, w_branch_gla, w_branch_attn, w_merge_gate, b_merge_gate, w_out, ffn2_norm, ffn2_w_gate, ffn2_w_up, ffn2_w_down, final_norm, loss_target, m_ffn1_norm, m_ffn1_w_gate, m_ffn1_w_up, m_ffn1_w_down, m_mix_norm, m_w_in, m_s5_lambda_re, m_s5_lambda_im, m_s5_log_dt, m_s5_b_re, m_s5_b_im, m_s5_c_re, m_s5_c_im, m_s5_d, m_s5_w_glu, m_gla_w_alpha, m_gla_b_alpha, m_gla_norm, m_attn_q_norm, m_attn_k_norm, m_w_branch_s5, m_w_branch_gla, m_w_branch_attn, m_w_merge_gate, m_b_merge_gate, m_w_out, m_ffn2_norm, m_ffn2_w_gate, m_ffn2_w_up, m_ffn2_w_down, m_final_norm, v_ffn1_norm, v_ffn1_w_gate, v_ffn1_w_up, v_ffn1_w_down, v_mix_norm, v_w_in, v_s5_lambda_re, v_s5_lambda_im, v_s5_log_dt, v_s5_b_re, v_s5_b_im, v_s5_c_re, v_s5_c_im, v_s5_d, v_s5_w_glu, v_gla_w_alpha, v_gla_b_alpha, v_gla_norm, v_attn_q_norm, v_attn_k_norm, v_w_branch_s5, v_w_branch_gla, v_w_branch_attn, v_w_merge_gate, v_b_merge_gate, v_w_out, v_ffn2_norm, v_ffn2_w_gate, v_ffn2_w_up, v_ffn2_w_down, v_final_norm):
    return _train_step(dict(locals()))
```

```python
import functools
import math

import jax
import jax.numpy as jnp
import numpy as np
from jax import lax
from jax.experimental import pallas as pl
from jax.experimental.pallas import tpu as pltpu

F32 = jnp.float32
BF16 = jnp.bfloat16

D_MODEL = 1024
DEPTH = 2
EPS = 1e-6
S5_G, S5_H, S5_P = 32, 16, 64
S5_W = S5_G * S5_H
S5_N = S5_G * S5_P
S5_TG = 8
S5_TILES = S5_G // S5_TG
S5_TC, S5_TS = S5_TG * S5_H, S5_TG * S5_P
GLA_H, GLA_D = 4, 128
GLA_W = GLA_H * GLA_D
GLA_R = 16
GLA_TAU = 16.0
GLA_C = 64
AT_QH, AT_KH, AT_D = 8, 2, 64
AT_W = AT_QH * AT_D
AT_KW = AT_KH * AT_D
GRID_W = 64
ROPE_BASE = 10000.0
ZP_W = 3456
ZC_AK, ZC_AV, ZC_Z = 24, 25, 26

ADAM_LR, ADAM_B1, ADAM_B2, ADAM_EPS, ADAM_WD, ADAM_STEP = 0.001, 0.9, 0.999, 1e-08, 0.01, 10

GRAD_DT = BF16
VMEM_LIMIT_V7X = 56 * 1024 * 1024
LANE = 128
SUBLANE = 8

W_NAMES = ['ffn1_norm', 'ffn1_w_gate', 'ffn1_w_up', 'ffn1_w_down', 'mix_norm', 'w_in', 's5_lambda_re', 's5_lambda_im',
           's5_log_dt', 's5_b_re', 's5_b_im', 's5_c_re', 's5_c_im', 's5_d', 's5_w_glu', 'gla_w_alpha', 'gla_b_alpha',
           'gla_norm', 'attn_q_norm', 'attn_k_norm', 'w_branch_s5', 'w_branch_gla', 'w_branch_attn', 'w_merge_gate',
           'b_merge_gate', 'w_out', 'ffn2_norm', 'ffn2_w_gate', 'ffn2_w_up', 'ffn2_w_down', 'final_norm']
SHARD_AXIS = {'ffn1_w_gate': 2, 'ffn1_w_up': 2, 'ffn1_w_down': 1, 'w_in': 2, 's5_w_glu': 1, 'gla_w_alpha': 3,
              'gla_b_alpha': 2, 'w_branch_s5': 2, 'w_branch_gla': 2, 'w_branch_attn': 2, 'w_merge_gate': 2,
              'w_out': 1, 'ffn2_w_gate': 2, 'ffn2_w_up': 2, 'ffn2_w_down': 1}
F32_GATHER = ['gla_w_alpha', 'gla_b_alpha']
BF16_GATHER = [n for n in W_NAMES if n in SHARD_AXIS and n not in F32_GATHER]
SHARDED = [n for n in W_NAMES if n in SHARD_AXIS]
REPLICATED = [n for n in W_NAMES if n not in SHARD_AXIS]
N_CHIPS = 4
MESH = pl.DeviceIdType.MESH


def _tile(n, target, mult):
    best = None
    for t in range(mult, min(n, target) + 1, mult):
        if n % t == 0:
            best = t
    return best if best is not None else n


def _cparams(sem, vmem_bytes):
    limit = int(min(VMEM_LIMIT_V7X, max(32 * 1024 * 1024, vmem_bytes * 5 // 4)))
    return pltpu.CompilerParams(dimension_semantics=sem, vmem_limit_bytes=limit)


def _pcall(body, **kw):
    shapes = kw.pop('out_shape')
    many = isinstance(shapes, (list, tuple))
    pinned = [pltpu.HBM(t.shape, t.dtype) for t in (shapes if many else [shapes])]
    call = pl.pallas_call(body, out_shape=pinned if many else pinned[0], **kw)

    def run(*ops):
        return call(*[o if jnp.issubdtype(o.dtype, jnp.integer) else pltpu.with_memory_space_constraint(o, pltpu.HBM)
                      for o in ops])

    return run


def _cw(arr, cb, width):
    return (arr, cb, width)


def _win(a):
    if isinstance(a, tuple):
        return a
    return (a, 0, a.shape[-1])


def _nbytes(shape, dtype):
    return int(np.prod(shape)) * jnp.dtype(dtype).itemsize


def _rowwise(name, fn, rows, fulls, outs, accs=(), tl=256):
    rows = [_win(r) for r in rows]
    n_tok = rows[0][0].shape[0]
    tl = _tile(n_tok, tl, 16)
    nr, nf, no = len(rows), len(fulls), len(outs)

    def body(*refs):
        vals = [r[...] for r in refs[:nr + nf]]
        res = fn(*vals)
        if not isinstance(res, (tuple, list)):
            res = (res,)
        for k in range(no):
            o = refs[nr + nf + k]
            o[...] = res[k].astype(o.dtype)
        if accs:
            @pl.when(pl.program_id(0) == 0)
            def _():
                for k in range(len(accs)):
                    a = refs[nr + nf + no + k]
                    a[...] = jnp.zeros(a.shape, a.dtype)
            for k in range(len(accs)):
                a = refs[nr + nf + no + k]
                a[...] += res[no + k]

    in_specs = [pl.BlockSpec((tl, w), functools.partial(lambda i, cb: (i, cb), cb=cb)) for (_, cb, w) in rows]
    in_specs += [pl.BlockSpec(f.shape, functools.partial(lambda i, nd: (0,) * nd, nd=f.ndim)) for f in fulls]
    out_specs = [pl.BlockSpec((tl, c), lambda i: (i, 0)) for (c, _) in outs]
    out_specs += [pl.BlockSpec((r, c), lambda i: (0, 0)) for (r, c) in accs]
    out_shape = [jax.ShapeDtypeStruct((n_tok, c), dt) for (c, dt) in outs]
    out_shape += [jax.ShapeDtypeStruct((r, c), F32) for (r, c) in accs]
    vm = 2 * sum(_nbytes((tl, w), a.dtype) for (a, _, w) in rows)
    vm += 2 * sum(_nbytes(f.shape, f.dtype) for f in fulls)
    vm += 2 * sum(_nbytes((tl, c), dt) for (c, dt) in outs)
    vm += 8 * max([_nbytes((tl, w), F32) for (_, _, w) in rows] + [_nbytes((tl, c), F32) for (c, _) in outs])
    return _pcall(
        body, name=name, grid=(n_tok // tl,), in_specs=in_specs, out_specs=out_specs, out_shape=out_shape,
        compiler_params=_cparams(("arbitrary",) if accs else ("parallel",), vm),
    )(*[r[0] for r in rows], *fulls)


def _mm(name, a, b, *, ta=False, tb=False, out_dtype=F32, alpha=1.0, add=None, tm=512, tn=512):
    a_arr, a_cb, a_w = _win(a)
    b_arr, b_cb, b_w = _win(b)
    if ta:
        kdim, m = a_arr.shape[0], a_w
    else:
        m, kdim = a_arr.shape[0], a_w
    if tb:
        n, kb = b_arr.shape[0], b_w
    else:
        kb, n = b_arr.shape[0], b_w
    assert kdim == kb, (name, kdim, kb)
    if not ta and kdim <= 1024:
        tm = 2 * tm
    tm = _tile(m, tm, LANE if ta else 16)
    tn = _tile(n, tn, LANE)
    if ta:
        a_spec = pl.BlockSpec((kdim, tm), lambda i, j: (0, a_cb * (a_w // tm) + i))
        a_blk = (kdim, tm)
    else:
        a_spec = pl.BlockSpec((tm, kdim), lambda i, j: (i, a_cb))
        a_blk = (tm, kdim)
    if tb:
        b_spec = pl.BlockSpec((tn, kdim), lambda i, j: (j, b_cb))
        b_blk = (tn, kdim)
    else:
        b_spec = pl.BlockSpec((kdim, tn), lambda i, j: (0, b_cb * (b_w // tn) + j))
        b_blk = (kdim, tn)
    dims = (((0 if ta else 1,), (1 if tb else 0,)), ((), ()))
    has_add = add is not None

    def body(*refs):
        a_ref, b_ref = refs[0], refs[1]
        o_ref = refs[-1]
        acc = lax.dot_general(a_ref[...].astype(BF16), b_ref[...].astype(BF16), dims, preferred_element_type=F32)
        if alpha != 1.0:
            acc = acc * alpha
        if has_add:
            acc = acc + refs[2][...].astype(F32)
        o_ref[...] = acc.astype(o_ref.dtype)

    in_specs = [a_spec, b_spec]
    ops = [a_arr, b_arr]
    if has_add:
        in_specs.append(pl.BlockSpec((tm, tn), lambda i, j: (i, j)))
        ops.append(add)
    vm = 2 * (_nbytes(a_blk, a_arr.dtype) + _nbytes(b_blk, b_arr.dtype) + _nbytes((tm, tn), out_dtype))
    vm += _nbytes(a_blk, BF16) + _nbytes(b_blk, BF16) + 3 * _nbytes((tm, tn), F32)
    return _pcall(
        body, name=name, grid=(m // tm, n // tn), in_specs=in_specs,
        out_specs=pl.BlockSpec((tm, tn), lambda i, j: (i, j)),
        out_shape=jax.ShapeDtypeStruct((m, n), out_dtype),
        compiler_params=_cparams(("parallel", "parallel"), vm),
    )(*ops)


def _sigmoid(x):
    return 0.5 * jnp.tanh(0.5 * x) + 0.5


def _rms(x, gain):
    r = lax.rsqrt(jnp.mean(x * x, axis=-1, keepdims=True) + EPS)
    return x * r * gain


def _rms_bwd(x, gain, dh):
    r = lax.rsqrt(jnp.mean(x * x, axis=-1, keepdims=True) + EPS)
    xh = x * r
    dg = jnp.sum(dh * xh, axis=0, keepdims=True)
    dxh = dh * gain
    dx = r * (dxh - xh * jnp.mean(dxh * xh, axis=-1, keepdims=True))
    return dx, dg


_GELU_C = math.sqrt(2.0 / math.pi)


def _gelu(y):
    return 0.5 * y * (1.0 + jnp.tanh(_GELU_C * (y + 0.044715 * y * y * y)))


def _gelu_grad(y):
    th = jnp.tanh(_GELU_C * (y + 0.044715 * y * y * y))
    return 0.5 * (1.0 + th) + 0.5 * y * (1.0 - th * th) * _GELU_C * (1.0 + 3.0 * 0.044715 * y * y)


def _log_sigmoid(x):
    return jnp.minimum(x, 0.0) - jnp.log(1.0 + jnp.exp(-jnp.abs(x)))


def _dot(a, b, dims=(((1,), (0,)), ((), ())), precision=None):
    return lax.dot_general(a, b, dims, preferred_element_type=F32, precision=precision)


def _dot_exact(x, m, dims=(((1,), (0,)), ((), ())), terms=3, m_first=False):
    m16 = m.astype(BF16)
    acc, rest = None, x
    for k in range(terms):
        piece = rest.astype(BF16)
        t = _dot(m16, piece, dims) if m_first else _dot(piece, m16, dims)
        acc = t if acc is None else acc + t
        if k + 1 < terms:
            rest = rest - piece.astype(F32)
    return acc


_NT = (((1,), (1,)), ((), ()))
_TN = (((0,), (0,)), ((), ()))


def _scan_core(b_ref, are_ref, aim_ref, pw, last, of32, *, rev, xf32=None, da_ref=None, lc=512):
    nb, nc = b_ref.shape[0] // SUBLANE, S5_N // lc
    row = lax.broadcasted_iota(jnp.int32, (SUBLANE, lc), 0)

    def cmul(ar, ai, xr, xi):
        return ar * xr - ai * xi, ar * xi + ai * xr

    def hs(c, xr, xi):
        cs = pl.ds(c * lc, lc)
        for k, slot in ((1, 0), (2, 2), (4, 4)):
            ar, ai = pw[slot, :, cs], pw[slot + 1, :, cs]
            if rev:
                sr, si = pltpu.roll(xr, SUBLANE - k, 0), pltpu.roll(xi, SUBLANE - k, 0)
                keep = row < SUBLANE - k
            else:
                sr, si = pltpu.roll(xr, k, 0), pltpu.roll(xi, k, 0)
                keep = row >= k
            sr, si = jnp.where(keep, sr, 0.0), jnp.where(keep, si, 0.0)
            pr, pi = cmul(ar, ai, sr, si)
            xr, xi = xr + pr, xi + pi
        return xr, xi

    @pl.when(pl.program_id(0) == 0)
    def _():
        last[...] = jnp.zeros(last.shape, F32)
        if da_ref is not None:
            da_ref[...] = jnp.zeros(da_ref.shape, F32)
        for c in range(nc):
            cs = pl.ds(c * lc, lc)
            a1r = jnp.broadcast_to(are_ref[:, cs], (SUBLANE, lc))
            a1i = jnp.broadcast_to(aim_ref[:, cs], (SUBLANE, lc))
            a2r, a2i = cmul(a1r, a1i, a1r, a1i)
            a4r, a4i = cmul(a2r, a2i, a2r, a2i)
            for slot, v in enumerate((a1r, a1i, a2r, a2i, a4r, a4i)):
                pw[slot, :, cs] = v
            first = SUBLANE - 1 if rev else 0
            pr, pi = hs(c, jnp.where(row == first, a1r, 0.0), jnp.where(row == first, a1i, 0.0))
            pw[6, :, cs] = pr
            pw[7, :, cs] = pi

    edge = 0 if rev else SUBLANE - 1

    def blk(j, carry):
        jj = (nb - 1 - j) if rev else j
        r0 = pl.multiple_of(jj * SUBLANE, SUBLANE)
        for c in range(nc):
            cre, cim = pl.ds(c * lc, lc), pl.ds(S5_N + c * lc, lc)
            xr, xi = hs(c, b_ref[pl.ds(r0, SUBLANE), cre], b_ref[pl.ds(r0, SUBLANE), cim])
            cr = jnp.broadcast_to(last[edge:edge + 1, cre], (SUBLANE, lc))
            ci = jnp.broadcast_to(last[edge:edge + 1, cim], (SUBLANE, lc))
            pr, pi = cmul(pw[6, :, cre], pw[7, :, cre], cr, ci)
            xr, xi = xr + pr, xi + pi
            if da_ref is not None:
                if rev:
                    qr, qi = pltpu.roll(xr, SUBLANE - 1, 0), pltpu.roll(xi, SUBLANE - 1, 0)
                    fill = row == SUBLANE - 1
                else:
                    qr, qi = pltpu.roll(xr, 1, 0), pltpu.roll(xi, 1, 0)
                    fill = row == 0
                qr, qi = jnp.where(fill, cr, qr), jnp.where(fill, ci, qi)
                zr, zi = xf32[pl.ds(r0, SUBLANE), cre], xf32[pl.ds(r0, SUBLANE), cim]
                da_ref[:, cre] += zr * qr + zi * qi
                da_ref[:, cim] += zr * qi - zi * qr
            last[:, cre] = xr
            last[:, cim] = xi
            of32[pl.ds(r0, SUBLANE), cre] = xr
            of32[pl.ds(r0, SUBLANE), cim] = xi
        return carry

    lax.fori_loop(0, nb, blk, 0)


def _s5_tiles():
    return [(slice((j % S5_TILES) * S5_TC, (j % S5_TILES + 1) * S5_TC), slice(j * S5_TS, (j + 1) * S5_TS))
            for j in range(2 * S5_TILES)]


def _s5_scratch(tb):
    return [pltpu.VMEM((8, SUBLANE, S5_N), F32), pltpu.VMEM((SUBLANE, 2 * S5_N), F32),
            pltpu.VMEM((tb, 2 * S5_N), F32), pltpu.VMEM((tb, 2 * S5_N), F32),
            pltpu.VMEM((S5_TC, 2 * S5_N), BF16), pltpu.VMEM((2 * S5_N, S5_TC), BF16)]


def _s5_fwd(name, z, a_re, a_im, wb, wc, add, *, rev, tb=512):
    n_tok = z.shape[0]
    tb = _tile(n_tok, tb, 16)
    nt = n_tok // tb
    has_add = add is not None

    def body(*refs):
        u_ref, are_ref, aim_ref, wb_ref, wc_ref = refs[:5]
        x_ref, y_ref, pw, last, of32, bu, wb16, wc16 = refs[5 + has_add:]

        @pl.when(pl.program_id(0) == 0)
        def _():
            wb16[...] = wb_ref[...].astype(BF16)
            wc16[...] = wc_ref[...].astype(BF16)

        u16 = u_ref[...].astype(BF16)
        for ch, st in _s5_tiles():
            bu[:, st] = _dot(u16[:, ch], wb16[:, st])
        _scan_core(bu, are_ref, aim_ref, pw, last, of32, rev=rev)
        x16 = of32[...].astype(BF16)
        x_ref[...] = x16
        ys = []
        for k in range(S5_TILES):
            re, im = slice(k * S5_TS, (k + 1) * S5_TS), slice(S5_N + k * S5_TS, S5_N + (k + 1) * S5_TS)
            ys.append(_dot(x16[:, re], wc16[re, :]) + _dot(x16[:, im], wc16[im, :]))
        y = jnp.concatenate(ys, axis=1)
        y_ref[...] = y + refs[5][...] if has_add else y

    tmap = (lambda t: (nt - 1 - t, 0)) if rev else (lambda t: (t, 0))
    const = lambda t: (0, 0)
    in_specs = [pl.BlockSpec((tb, S5_W), tmap), pl.BlockSpec((1, S5_N), const), pl.BlockSpec((1, S5_N), const),
                pl.BlockSpec((S5_TC, 2 * S5_N), const), pl.BlockSpec((2 * S5_N, S5_TC), const)]
    ops = [z, a_re, a_im, wb, wc]
    if has_add:
        in_specs.append(pl.BlockSpec((tb, S5_W), tmap))
        ops.append(add)
    return _pcall(
        body, name=name, grid=(nt,), in_specs=in_specs,
        out_specs=[pl.BlockSpec((tb, 2 * S5_N), tmap), pl.BlockSpec((tb, S5_W), tmap)],
        out_shape=[jax.ShapeDtypeStruct((n_tok, 2 * S5_N), BF16), jax.ShapeDtypeStruct((n_tok, S5_W), F32)],
        scratch_shapes=_s5_scratch(tb), compiler_params=_cparams(("arbitrary",), 10 * tb * 2 * S5_N * 4),
    )(*ops)


def _s5_bwd(name, dy, z, xs, a_re, a_im, wb, wc, add, *, rev, tb=256):
    n_tok = z.shape[0]
    tb = _tile(n_tok, tb, 16)
    nt = n_tok // tb

    def body(dy_ref, u_ref, xs_ref, are_ref, aim_ref, wb_ref, wc_ref, add_ref, du_ref, dwb_ref, dwc_ref, da_ref,
             pw, last, of32, gx, wb16, wc16, xf32):
        @pl.when(pl.program_id(0) == 0)
        def _():
            wb16[...] = wb_ref[...].astype(BF16)
            wc16[...] = wc_ref[...].astype(BF16)
            dwb_ref[...] = jnp.zeros(dwb_ref.shape, F32)
            dwc_ref[...] = jnp.zeros(dwc_ref.shape, F32)

        dy16, u16, x16 = dy_ref[...].astype(BF16), u_ref[...].astype(BF16), xs_ref[...]
        for ch, st in _s5_tiles():
            gx[:, st] = _dot(dy16[:, ch], wc16[st, :], _NT)
        xf32[...] = x16.astype(F32)
        _scan_core(gx, are_ref, aim_ref, pw, last, of32, rev=rev, xf32=xf32, da_ref=da_ref)
        g16 = of32[...].astype(BF16)
        dus = []
        for k in range(S5_TILES):
            re, im = slice(k * S5_TS, (k + 1) * S5_TS), slice(S5_N + k * S5_TS, S5_N + (k + 1) * S5_TS)
            dus.append(_dot(g16[:, re], wb16[:, re], _NT) + _dot(g16[:, im], wb16[:, im], _NT))
        du_ref[...] = jnp.concatenate(dus, axis=1) + add_ref[...]
        for ch, st in _s5_tiles():
            dwb_ref[:, st] += _dot(u16[:, ch], g16[:, st], _TN)
            dwc_ref[st, :] += _dot(x16[:, st], dy16[:, ch], _TN)

    tmap = (lambda t: (nt - 1 - t, 0)) if rev else (lambda t: (t, 0))
    const = lambda t: (0, 0)
    row_w, row_s = pl.BlockSpec((tb, S5_W), tmap), pl.BlockSpec((tb, 2 * S5_N), tmap)
    return _pcall(
        body, name=name, grid=(nt,),
        in_specs=[row_w, row_w, row_s, pl.BlockSpec((1, S5_N), const), pl.BlockSpec((1, S5_N), const),
                  pl.BlockSpec((S5_TC, 2 * S5_N), const), pl.BlockSpec((2 * S5_N, S5_TC), const), row_w],
        out_specs=[row_w, pl.BlockSpec((S5_TC, 2 * S5_N), const), pl.BlockSpec((2 * S5_N, S5_TC), const),
                   pl.BlockSpec((SUBLANE, 2 * S5_N), const)],
        out_shape=[jax.ShapeDtypeStruct((n_tok, S5_W), F32), jax.ShapeDtypeStruct((S5_TC, 2 * S5_N), F32),
                   jax.ShapeDtypeStruct((2 * S5_N, S5_TC), F32), jax.ShapeDtypeStruct((SUBLANE, 2 * S5_N), F32)],
        scratch_shapes=_s5_scratch(tb) + [pltpu.VMEM((tb, 2 * S5_N), F32)],
        compiler_params=_cparams(("arbitrary",), 12 * tb * 2 * S5_N * 4),
    )(dy, z, xs, a_re, a_im, wb, wc, add)


GLA_TB = 512


def _gla_block_terms(la, q, k, rev):
    trif = _chunk_mask(rev).astype(F32)
    chunks = [la[t:t + GLA_C] for t in range(0, la.shape[0], GLA_C)]
    bcum = jnp.concatenate([_dot_exact(c, trif, m_first=True) for c in chunks], axis=0)
    blast = _per_chunk_sum(la)
    eb, enb, ee = jnp.exp(bcum), jnp.exp(-bcum), jnp.exp(blast - bcum)
    return trif, jnp.exp(blast), eb, enb, ee, q * eb, k * enb, k * ee


def _per_chunk_sum(t):
    return jnp.concatenate([jnp.broadcast_to(jnp.sum(t[r:r + GLA_C], axis=0, keepdims=True), (GLA_C, t.shape[1]))
                            for r in range(0, t.shape[0], GLA_C)], axis=0)


def _chunk_mask(rev):
    ri = lax.broadcasted_iota(jnp.int32, (GLA_C, GLA_C), 0)
    ci = lax.broadcasted_iota(jnp.int32, (GLA_C, GLA_C), 1)
    return (ri <= ci) if rev else (ri >= ci)


def _gla_fwd(name, z, la, *, rev):
    n_tok = z.shape[0]
    tb = _tile(n_tok, GLA_TB, GLA_C)
    nblk, per = n_tok // tb, tb // GLA_C
    scale = GLA_D ** -0.5

    def body(q_ref, k_ref, v_ref, la_ref, o_ref, s_ref, st, qd_s, kd_s, ke_s, v_s, dec_s):
        @pl.when(pl.program_id(0) == 0)
        def _():
            st[...] = jnp.zeros(st.shape, F32)

        _, dec, _, _, _, qd, kd, ke = _gla_block_terms(la_ref[...], q_ref[...].astype(F32) * scale,
                                                                 k_ref[...].astype(F32), rev)
        qd_s[...], kd_s[...], ke_s[...] = qd.astype(BF16), kd.astype(BF16), ke.astype(BF16)
        v_s[...] = v_ref[...].astype(BF16)
        dec_s[...] = dec
        tri = _chunk_mask(rev)
        for step in range(per):
            t = per - 1 - step if rev else step
            rows = pl.ds(t * GLA_C, GLA_C)
            for h in range(GLA_H):
                lanes = pl.ds(h * GLA_D, GLA_D)
                qd16, v16 = qd_s[rows, lanes], v_s[rows, lanes]
                sc = jnp.where(tri, _dot(qd16, kd_s[rows, lanes], _NT), 0.0)
                s16 = st[h].astype(BF16)
                s_ref[h, step] = s16
                o_ref[rows, lanes] = _dot(sc.astype(BF16), v16) + _dot(qd16, s16, _NT)
                st[h] = st[h] * dec_s[pl.ds(t * GLA_C, 1), lanes] + _dot(v16, ke_s[rows, lanes], _TN)

    bmap = (lambda n: nblk - 1 - n) if rev else (lambda n: n)

    def zspec(cb):
        return pl.BlockSpec((tb, GLA_W), lambda n: (bmap(n), cb))

    half = pltpu.VMEM((tb, GLA_W), BF16)
    return _pcall(
        body, name=name, grid=(nblk,),
        in_specs=[zspec(1), zspec(2), zspec(3), zspec(0)],
        out_specs=[zspec(0), pl.BlockSpec((GLA_H, per, GLA_D, GLA_D), lambda n: (0, n, 0, 0))],
        out_shape=[jax.ShapeDtypeStruct((n_tok, GLA_W), F32),
                   jax.ShapeDtypeStruct((GLA_H, n_tok // GLA_C, GLA_D, GLA_D), BF16)],
        scratch_shapes=[pltpu.VMEM((GLA_H, GLA_D, GLA_D), F32), half, half, half, half, pltpu.VMEM((tb, GLA_W), F32)],
        compiler_params=_cparams(("arbitrary",), 40 * tb * GLA_W * 4),
    )(z, z, z, la)


def _gla_bwd(name, z, la, s_prev, do, *, rev):
    n_tok = z.shape[0]
    tb = _tile(n_tok, GLA_TB, GLA_C)
    nblk, per = n_tok // tb, tb // GLA_C
    scale = GLA_D ** -0.5

    def body(q_ref, k_ref, v_ref, la_ref, s_ref, do_ref, dq_ref, dk_ref, dv_ref, dla_ref,
             dst, qd_s, kd_s, ke_s, v_s, do_s, dec_s, dqd_s, dkd_s, dke_s, dd_s):
        @pl.when(pl.program_id(0) == 0)
        def _():
            dst[...] = jnp.zeros(dst.shape, F32)

        trif, dec, eb, enb, ee, qd, kd, ke = _gla_block_terms(la_ref[...], q_ref[...].astype(F32) * scale,
                                                                 k_ref[...].astype(F32), rev)
        qd_s[...], kd_s[...], ke_s[...] = qd.astype(BF16), kd.astype(BF16), ke.astype(BF16)
        v_s[...] = v_ref[...].astype(BF16)
        do_s[...] = do_ref[...].astype(BF16)
        dec_s[...] = dec
        tri = _chunk_mask(rev)
        for step in range(per):
            t = step if rev else per - 1 - step
            rows = pl.ds(t * GLA_C, GLA_C)
            for h in range(GLA_H):
                lanes = pl.ds(h * GLA_D, GLA_D)
                qd16, kd16, ke16 = qd_s[rows, lanes], kd_s[rows, lanes], ke_s[rows, lanes]
                v16, do16 = v_s[rows, lanes], do_s[rows, lanes]
                sc16 = jnp.where(tri, _dot(qd16, kd16, _NT), 0.0).astype(BF16)
                dsc16 = jnp.where(tri, _dot(do16, v16, _NT), 0.0).astype(BF16)
                s16 = s_ref[h, per - 1 - step]
                dsn = dst[h]
                dsn16 = dsn.astype(BF16)
                dec_c = dec_s[pl.ds(t * GLA_C, 1), lanes]
                dqd_s[rows, lanes] = _dot(dsc16, kd16) + _dot(do16, s16)
                dkd_s[rows, lanes] = _dot(dsc16, qd16, _TN)
                dke_s[rows, lanes] = _dot(v16, dsn16)
                dv_ref[rows, lanes] = _dot(sc16, do16, _TN) + _dot(ke16, dsn16, _NT)
                ddec = jnp.sum(dsn * s16.astype(F32), axis=0, keepdims=True) * dec_c
                dd_s[rows, lanes] = jnp.broadcast_to(ddec, (GLA_C, GLA_D))
                dst[h] = dsn * dec_c + _dot(do16, qd16, _TN)
        dqd, dkd, dke = dqd_s[...], dkd_s[...], dke_s[...]
        dq_ref[...] = dqd * eb * scale
        dk_ref[...] = dkd * enb + dke * ee
        db = dqd * qd - dkd * kd - dke * ke
        dcum = jnp.concatenate([_dot_exact(db[r:r + GLA_C], trif, _TN, m_first=True) for r in range(0, tb, GLA_C)], axis=0)
        dla_ref[...] = dcum + _per_chunk_sum(dke * ke) + dd_s[...]

    bmap = (lambda n: n) if rev else (lambda n: nblk - 1 - n)

    def zspec(cb):
        return pl.BlockSpec((tb, GLA_W), lambda n: (bmap(n), cb))

    o4 = jax.ShapeDtypeStruct((n_tok, GLA_W), F32)
    half, full = pltpu.VMEM((tb, GLA_W), BF16), pltpu.VMEM((tb, GLA_W), F32)
    return _pcall(
        body, name=name, grid=(nblk,),
        in_specs=[zspec(1), zspec(2), zspec(3), zspec(0),
                  pl.BlockSpec((GLA_H, per, GLA_D, GLA_D), lambda n: (0, nblk - 1 - n, 0, 0)), zspec(0)],
        out_specs=[zspec(0)] * 4, out_shape=[o4] * 4,
        scratch_shapes=[pltpu.VMEM((GLA_H, GLA_D, GLA_D), F32), half, half, half, half, half, full, full, full, full, full],
        compiler_params=_cparams(("arbitrary",), 64 * tb * GLA_W * 4),
    )(z, z, z, la, s_prev, do)


_PAIRS = AT_W // LANE
_PAIRS_PER_KV = _PAIRS // AT_KH


def _attn_fwd(name, q, kd, vd, tq=128):
    n_tok = q.shape[0]
    tq = _tile(n_tok, tq, 16)

    def body(q_ref, k_ref, v_ref, o_ref, l_ref):
        qv, kv, vv = q_ref[...], k_ref[...], v_ref[...]
        low = lax.broadcasted_iota(jnp.int32, qv.shape, 1) < AT_D
        res = []
        for keep in (low, jnp.logical_not(low)):
            s = _dot(jnp.where(keep, qv, jnp.zeros_like(qv)), kv, _NT)
            m = jnp.max(s, axis=-1, keepdims=True)
            p = jnp.exp(s - m)
            den = jnp.sum(p, axis=-1, keepdims=True)
            res.append((_dot(p.astype(BF16), vv) / den, m + jnp.log(den)))
        o_ref[...] = jnp.where(low, res[0][0], res[1][0])
        l_ref[...] = jnp.where(low, res[0][1], res[1][1])

    qspec = pl.BlockSpec((tq, LANE), lambda p, i: (i, p))
    kspec = pl.BlockSpec((n_tok, LANE), lambda p, i: (0, p // _PAIRS_PER_KV))
    vm = 5 * _nbytes((tq, n_tok), F32) + 8 * _nbytes((n_tok, LANE), BF16)
    out = jax.ShapeDtypeStruct((n_tok, AT_W), F32)
    return _pcall(
        body, name=name, grid=(_PAIRS, n_tok // tq), in_specs=[qspec, kspec, kspec], out_specs=[qspec, qspec],
        out_shape=[out, out], compiler_params=_cparams(("parallel", "parallel"), vm),
    )(q, kd, vd)


def _attn_bwd(name, q, kd, vd, o, lse, do, tq=256):
    n_tok = q.shape[0]
    tq = _tile(n_tok, tq, 16)

    def body(q_ref, k_ref, v_ref, o_ref, l_ref, do_ref, dq_ref, dk_ref, dv_ref):
        @pl.when((pl.program_id(1) == 0) & (pl.program_id(2) == 0))
        def _():
            dk_ref[...] = jnp.zeros(dk_ref.shape, F32)
            dv_ref[...] = jnp.zeros(dv_ref.shape, F32)

        qv, kv, vv = q_ref[...], k_ref[...], v_ref[...]
        dof, lv = do_ref[...], l_ref[...]
        do16 = dof.astype(BF16)
        doo = dof * o_ref[...]
        low = lax.broadcasted_iota(jnp.int32, qv.shape, 1) < AT_D
        dq = jnp.zeros(qv.shape, F32)
        for keep in (low, jnp.logical_not(low)):
            qm = jnp.where(keep, qv, jnp.zeros_like(qv))
            dom = jnp.where(keep, do16, jnp.zeros_like(do16))
            lh = jnp.max(jnp.where(keep, lv, -1e30), axis=-1, keepdims=True)
            p = jnp.exp(_dot(qm, kv, _NT) - lh)
            delta = jnp.sum(jnp.where(keep, doo, 0.0), axis=-1, keepdims=True)
            ds16 = (p * (_dot(dom, vv, _NT) - delta)).astype(BF16)
            dq = jnp.where(keep, _dot(ds16, kv), dq)
            dk_ref[...] += _dot(ds16, qm, _TN)
            dv_ref[...] += _dot(p.astype(BF16), dom, _TN)
        dq_ref[...] = dq

    qspec = pl.BlockSpec((tq, LANE), lambda g, j, i: (i, g * _PAIRS_PER_KV + j))
    kspec = pl.BlockSpec((n_tok, LANE), lambda g, j, i: (0, g))
    vm = 7 * _nbytes((tq, n_tok), F32) + 12 * _nbytes((n_tok, LANE), F32)
    dup = jax.ShapeDtypeStruct((n_tok, AT_KH * LANE), F32)
    return _pcall(
        body, name=name, grid=(AT_KH, _PAIRS_PER_KV, n_tok // tq),
        in_specs=[qspec, kspec, kspec, qspec, qspec, qspec], out_specs=[qspec, kspec, kspec],
        out_shape=[jax.ShapeDtypeStruct((n_tok, AT_W), F32), dup, dup],
        compiler_params=_cparams(("parallel", "arbitrary", "arbitrary"), vm),
    )(q, kd, vd, o, lse, do)


def _s5_prep(lam_re, lam_im, log_dt, b_re, b_im, c_re, c_im):
    dt = jnp.exp(log_dt)[:, None]
    er = jnp.exp(lam_re * dt)
    a_re, a_im = er * jnp.cos(lam_im * dt), er * jnp.sin(lam_im * dt)
    nr, ni = a_re - 1.0, a_im
    den = lam_re * lam_re + lam_im * lam_im
    f_re, f_im = (nr * lam_re + ni * lam_im) / den, (ni * lam_re - nr * lam_im) / den
    bb_re = f_re[..., None] * b_re - f_im[..., None] * b_im
    bb_im = f_re[..., None] * b_im + f_im[..., None] * b_re
    def tiles(t, axis):
        _, a, b = t.shape
        eye = jnp.eye(S5_TG, dtype=t.dtype)
        blocks = t.reshape(S5_TILES, S5_TG, a, 1, b) * eye[None, :, None, :, None]
        if axis == 0:
            return blocks.reshape(S5_TILES * S5_TG * a, S5_TG * b)
        return blocks.transpose(1, 2, 0, 3, 4).reshape(S5_TG * a, S5_TILES * S5_TG * b)

    w_in = jnp.concatenate([tiles(bb_re.transpose(0, 2, 1), 1), tiles(bb_im.transpose(0, 2, 1), 1)], axis=1)
    w_out = jnp.concatenate([tiles(c_re.transpose(0, 2, 1), 0), -tiles(c_im.transpose(0, 2, 1), 0)], axis=0)
    return a_re.reshape(1, S5_N), a_im.reshape(1, S5_N), w_in, w_out


def _rope_tables(n_tok):
    quarter = AT_D // 4
    d = np.arange(AT_D)
    e = d % (AT_D // 2)
    inv = (ROPE_BASE ** (-(e % quarter).astype(np.float64) * 2.0 / (AT_D // 2))).astype(np.float32)
    sign = np.where(e < quarter, -1.0, 1.0).astype(np.float32)
    rot = np.zeros((AT_D, AT_D), np.float32)
    rot[np.where(e < quarter, d + quarter, d - quarter), d] = 1.0
    pos = jnp.arange(n_tok, dtype=jnp.int32)
    axis_pos = jnp.where(jnp.asarray(d // (AT_D // 2) == 0)[None, :], (pos // GRID_W)[:, None], (pos % GRID_W)[:, None])
    ang = axis_pos.astype(F32) * jnp.asarray(inv)[None, :]
    return jnp.cos(ang), jnp.sin(ang) * jnp.asarray(sign)[None, :], rot


def _head_consts(n_tok, heads):
    cos, sin, rot = _rope_tables(n_tok)
    eye = np.eye(heads, dtype=np.float32)
    mean = np.kron(eye, np.full((AT_D, AT_D), 1.0 / AT_D, np.float32))
    return jnp.tile(cos, (1, heads)), jnp.tile(sin, (1, heads)), jnp.asarray(mean), jnp.asarray(np.kron(eye, rot))


def _pad_w_in(wt):
    return jnp.concatenate([wt[:2560], wt[2592:3360], wt[2560:2592], jnp.zeros((96, wt.shape[1]), wt.dtype)], axis=0)


def _unpad_w_in(gt):
    return jnp.concatenate([gt[:2560], gt[3328:3360], gt[2560:3328]], axis=0)


def _ffn_fwd(x, g, wg, wu, wd, d):
    n_tok = x.shape[0]
    fq = wd.shape[-2]
    tm = _tile(n_tok, 512, 16)
    w_bytes = N_CHIPS * fq * D_MODEL * 2

    def up(x_ref, g_ref, wg_ref, wu_ref, h_ref, a_ref, b_ref, s_ref):
        hv = _rms(x_ref[...], g_ref[...]).astype(BF16)
        h_ref[...] = hv
        for j in range(N_CHIPS):
            av, bv = _dot(hv, wg_ref[j, 0], _NT), _dot(hv, wu_ref[j, 0], _NT)
            a_ref[j] = av.astype(BF16)
            b_ref[j] = bv.astype(BF16)
            s_ref[j] = (av * _sigmoid(av) * bv).astype(BF16)

    wspec = pl.BlockSpec((N_CHIPS, 1, fq, D_MODEL), lambda i: (0, d, 0, 0))
    aspec = pl.BlockSpec((N_CHIPS, tm, fq), lambda i: (0, i, 0))
    xspec = pl.BlockSpec((tm, D_MODEL), lambda i: (i, 0))
    act = jax.ShapeDtypeStruct((N_CHIPS, n_tok, fq), BF16)
    h, a, b, s = _pcall(
        up, name="ffn_up", grid=(n_tok // tm,),
        in_specs=[xspec, pl.BlockSpec((1, D_MODEL), lambda i: (0, 0)), wspec, wspec],
        out_specs=[xspec, aspec, aspec, aspec],
        out_shape=[jax.ShapeDtypeStruct((n_tok, D_MODEL), BF16), act, act, act],
        compiler_params=_cparams(("parallel",), 4 * w_bytes + 6 * N_CHIPS * tm * fq * 2 + 12 * tm * fq * 4
                                 + 8 * tm * D_MODEL * 4),
    )(x, g, wg, wu)

    def down(s_ref, w_ref, x_ref, o_ref):
        acc = _dot(s_ref[0], w_ref[0, 0])
        for j in range(1, N_CHIPS):
            acc = acc + _dot(s_ref[j], w_ref[j, 0])
        o_ref[...] = x_ref[...] + 0.5 * acc

    xo = _pcall(
        down, name="ffn_down", grid=(n_tok // tm,), in_specs=[aspec, wspec, xspec], out_specs=xspec,
        out_shape=jax.ShapeDtypeStruct((n_tok, D_MODEL), F32),
        compiler_params=_cparams(("parallel",), 2 * w_bytes + 2 * N_CHIPS * tm * fq * 2 + 8 * tm * D_MODEL * 4),
    )(s, wd, x)
    return xo, (x, h, a, b, s)


def _ffn_bwd(dy, dy16, saved, g, wg, wu, wd, d, into):
    x, h, a, b, s = saved
    n_tok = x.shape[0]
    fq = wd.shape[-2]
    tm = _tile(n_tok, 512, 16)

    ta_ = _tile(n_tok, 512, 16)

    def dact(dy_ref, w_ref, a_ref, b_ref, da_ref, db_ref):
        dyv = dy_ref[...]
        for j in range(N_CHIPS):
            ds = 0.5 * _dot(dyv, w_ref[j, 0], _NT)
            av, bv = a_ref[j].astype(F32), b_ref[j].astype(F32)
            sg = _sigmoid(av)
            da_ref[j] = (ds * bv * sg * (1.0 + av * (1.0 - sg))).astype(BF16)
            db_ref[j] = (ds * av * sg).astype(BF16)

    aspec = pl.BlockSpec((N_CHIPS, ta_, fq), lambda i: (0, i, 0))
    act = jax.ShapeDtypeStruct((N_CHIPS, n_tok, fq), BF16)
    da, db = _pcall(
        dact, name="ffn_dact", grid=(n_tok // ta_,),
        in_specs=[pl.BlockSpec((ta_, D_MODEL), lambda i: (i, 0)),
                  pl.BlockSpec((N_CHIPS, 1, fq, D_MODEL), lambda i: (0, d, 0, 0)), aspec, aspec],
        out_specs=[aspec, aspec], out_shape=[act, act],
        compiler_params=_cparams(("parallel",), 2 * N_CHIPS * fq * D_MODEL * 2 + 8 * N_CHIPS * ta_ * fq * 2
                                 + 12 * ta_ * fq * 4),
    )(dy16, wd, a, b)

    any_spec = pl.BlockSpec(memory_space=pl.ANY)
    alias = {} if into is None else {2: 0}
    tn = _tile(D_MODEL, 512, LANE)

    def dw_call(t, rhs, scale, prev):
        def dw(t_ref, r_ref, *rest):
            rest[-1][0, 0] = (scale * _dot(t_ref[0], r_ref[...], _TN)).astype(GRAD_DT)

        return _pcall(
            dw, name="ffn_dw", grid=(N_CHIPS, D_MODEL // tn),
            in_specs=[pl.BlockSpec((1, n_tok, fq), lambda j, n: (j, 0, 0)), pl.BlockSpec((n_tok, tn), lambda j, n: (0, n))]
            + ([] if prev is None else [any_spec]),
            out_specs=pl.BlockSpec((1, 1, fq, tn), lambda j, n: (j, d, 0, n)),
            out_shape=jax.ShapeDtypeStruct((N_CHIPS, DEPTH, fq, D_MODEL), GRAD_DT), input_output_aliases=alias,
            compiler_params=_cparams(("parallel", "parallel"), 6 * n_tok * (fq + tn) * 2 + 4 * fq * tn * 4),
        )(t, rhs, *([] if prev is None else [prev]))

    dwg = dw_call(da, h, 1.0, None if into is None else into[0])
    dwu = dw_call(db, h, 1.0, None if into is None else into[1])
    dwd = dw_call(s, dy16, 0.5, None if into is None else into[2])
    tr = _tile(n_tok, 256, 16)

    def dh_norm(da_ref, db_ref, wg_ref, wu_ref, dy_ref, x_ref, g_ref, dx_ref, dx16_ref, dg_ref):
        dh = _dot(da_ref[0], wg_ref[0, 0]) + _dot(db_ref[0], wu_ref[0, 0])
        for j in range(1, N_CHIPS):
            dh = dh + _dot(da_ref[j], wg_ref[j, 0]) + _dot(db_ref[j], wu_ref[j, 0])
        dx, dg = _rms_bwd(x_ref[...], g_ref[...], dh)
        dx = dy_ref[...] + dx
        dx_ref[...] = dx
        dx16_ref[...] = dx.astype(BF16)

        @pl.when(pl.program_id(0) == 0)
        def _():
            dg_ref[...] = jnp.zeros(dg_ref.shape, F32)

        dg_ref[...] += dg

    tspec = pl.BlockSpec((N_CHIPS, tr, fq), lambda i: (0, i, 0))
    wspec = pl.BlockSpec((N_CHIPS, 1, fq, D_MODEL), lambda i: (0, d, 0, 0))
    rspec = pl.BlockSpec((tr, D_MODEL), lambda i: (i, 0))
    vspec = pl.BlockSpec((1, D_MODEL), lambda i: (0, 0))
    dx, dx16, dg = _pcall(
        dh_norm, name="ffn_dh_norm", grid=(n_tok // tr,),
        in_specs=[tspec, tspec, wspec, wspec, rspec, rspec, vspec], out_specs=[rspec, rspec, vspec],
        out_shape=[jax.ShapeDtypeStruct((n_tok, D_MODEL), F32), jax.ShapeDtypeStruct((n_tok, D_MODEL), BF16),
                   jax.ShapeDtypeStruct((1, D_MODEL), F32)],
        compiler_params=_cparams(("arbitrary",), 4 * N_CHIPS * D_MODEL * fq * 2 + 4 * N_CHIPS * tr * fq * 2
                                 + 16 * tr * D_MODEL * 4),
    )(da, db, wg, wu, dy, x, g)
    return dx, dx16, dg[0], (dwg, dwu, dwd)


def _alpha_pads(w_alpha):
    z = jnp.zeros((LANE - 2 * GLA_R, GLA_W), F32)
    wf = jnp.concatenate([w_alpha[0], jnp.zeros((GLA_R, GLA_W), F32), z], axis=0)
    wb = jnp.concatenate([jnp.zeros((GLA_R, GLA_W), F32), w_alpha[1], z], axis=0)
    return wf, wb


def _dup_heads(t):
    low = lax.broadcasted_iota(jnp.int32, t.shape, 1) < AT_D
    h0, h1 = jnp.where(low, t, 0.0), jnp.where(low, 0.0, t)
    return jnp.concatenate([h0 + pltpu.roll(h0, AT_D, 1), h1 + pltpu.roll(h1, AT_D, 1)], axis=1)


def _fold_heads(t):
    low = lax.broadcasted_iota(jnp.int32, (t.shape[0], LANE), 1) < AT_D
    g0, g1 = t[:, :LANE], t[:, LANE:]
    return jnp.where(low, g0 + pltpu.roll(g0, AT_D, 1), g1 + pltpu.roll(g1, AT_D, 1))


def _mixer_fwd(x, w):
    n_tok = x.shape[0]
    wip, wmg = w['w_in_p'], w['w_merge_gate']
    tr = _tile(n_tok, 256, 16)

    def project(x_ref, g_ref, wip_ref, wmg_ref, h_ref, z_ref, gp_ref):
        hv = _rms(x_ref[...], g_ref[...]).astype(BF16)
        h_ref[...] = hv
        z_ref[...] = _dot(hv, wip_ref[...], _NT).astype(BF16)
        gp_ref[...] = _dot(hv, wmg_ref[...]).astype(BF16)

    def rows(c):
        return pl.BlockSpec((tr, c), lambda i: (i, 0))

    def whole(t):
        return pl.BlockSpec(t.shape, lambda i: (0, 0))

    h, z, gp = _pcall(
        project, name="mix_in", grid=(n_tok // tr,),
        in_specs=[rows(D_MODEL), whole(w['mix_norm']), whole(wip), whole(wmg)],
        out_specs=[rows(D_MODEL), rows(ZP_W), rows(3 * D_MODEL)],
        out_shape=[jax.ShapeDtypeStruct((n_tok, D_MODEL), BF16), jax.ShapeDtypeStruct((n_tok, ZP_W), BF16),
                   jax.ShapeDtypeStruct((n_tok, 3 * D_MODEL), BF16)],
        compiler_params=_cparams(("parallel",), 4 * (wip.size + wmg.size) + 16 * tr * (ZP_W + 3 * D_MODEL)),
    )(x, w['mix_norm'], wip, wmg)
    u = _cw(z, 0, S5_W)

    xs, yy = [], None
    for d, rev in ((0, False), (1, True)):
        a_re, a_im, wb, wc = w['s5'][d]
        x_d, yy = _s5_fwd("s5_rev" if rev else "s5_fwd", z, a_re, a_im, wb, wc, yy, rev=rev)
        xs.append(x_d)

    def s5_post(yv, uv, dv, wglu):
        y0 = yv + dv * uv
        y1 = _gelu(y0)
        t = _dot(y1.astype(BF16), wglu)
        return y0, t, y1 * _sigmoid(t)

    y0, t_glu, y_s5 = _rowwise("s5_post", s5_post, [yy, u], [w['s5_d'], w['s5_w_glu']],
                               [(S5_W, F32), (S5_W, F32), (S5_W, BF16)])

    wa_f, wa_b = _alpha_pads(w['gla_w_alpha'])
    ba_f, ba_b = w['gla_b_alpha'][0:1], w['gla_b_alpha'][1:2]
    zz = _cw(z, ZC_Z, LANE)

    def gla_prep(zv, waf, wab, bf, bb):
        z16 = zv.astype(BF16)
        lf = _dot(z16, waf.astype(BF16)) + bf
        lb = _dot(z16, wab.astype(BF16)) + bb
        return _log_sigmoid(lf) / GLA_TAU, _log_sigmoid(lb) / GLA_TAU

    la_f, la_b = _rowwise("gla_prep", gla_prep, [zz], [wa_f, wa_b, ba_f, ba_b], [(GLA_W, F32), (GLA_W, F32)])
    o_f, s_f = _gla_fwd("gla_fwd", z, la_f, rev=False)
    o_b, s_b = _gla_fwd("gla_rev", z, la_b, rev=True)
    gate = _cw(z, 4, GLA_W)

    def gla_post(of, ob, gv, gn):
        o, gv = of + ob, gv.astype(F32)
        outs = []
        for hh in range(GLA_H):
            sl = slice(hh * GLA_D, (hh + 1) * GLA_D)
            gh = gv[:, sl]
            outs.append(_rms(o[:, sl], gn) * (gh * _sigmoid(gh)))
        return jnp.concatenate(outs, axis=1)

    y_gla = _rowwise("gla_post", gla_post, [o_f, o_b, gate], [w['gla_norm']], [(GLA_W, BF16)])[0]

    cq, sq, mq, rq = _head_consts(n_tok, AT_QH)
    ck, sk, mk, rk = _head_consts(n_tok, AT_KH)
    gq = jnp.tile(w['attn_q_norm'], (1, AT_QH))
    gk = jnp.tile(w['attn_k_norm'], (1, AT_KH))

    def rope_fwd(v, gain, cos, sin, mean, rot, scale):
        r = lax.rsqrt(_dot_exact(v * v, mean, terms=2) + EPS)
        vn = v * r * gain
        return (vn * cos + _dot_exact(vn, rot, terms=2) * sin) * scale

    def attn_prep(qv, kv, vv, cqv, sqv, ckv, skv, gqv, gkv, mqv, rqv, mkv, rkv):
        qv, kv, vv = qv.astype(F32), kv.astype(F32), vv.astype(F32)
        kr = rope_fwd(kv, gkv, ckv, skv, mkv, rkv, 1.0)
        return rope_fwd(qv, gqv, cqv, sqv, mqv, rqv, AT_D ** -0.5), _dup_heads(kr), _dup_heads(vv)

    q_raw, k_raw, v_raw = _cw(z, 5, AT_W), _cw(z, ZC_AK, AT_KW), _cw(z, ZC_AV, AT_KW)
    q16, kd, vd = _rowwise("attn_prep", attn_prep, [q_raw, k_raw, v_raw, cq, sq, ck, sk], [gq, gk, mq, rq, mk, rk],
                           [(AT_W, BF16), (2 * LANE, BF16), (2 * LANE, BF16)])
    y_at, lse = _attn_fwd("attn_fwd", q16, kd, vd)

    p0 = _mm("mix_branch", y_s5, w['w_branch_s5'], out_dtype=BF16)
    p1 = _mm("mix_branch", y_gla, w['w_branch_gla'], out_dtype=BF16)
    p2 = _mm("mix_branch_f32", y_at, w['w_branch_attn'], out_dtype=BF16)

    def merge(gpv, a0, a1, a2, bias):
        g = _sigmoid(gpv.astype(F32) + bias)
        return (g[:, :D_MODEL] * a0.astype(F32) + g[:, D_MODEL:2 * D_MODEL] * a1.astype(F32)
                + g[:, 2 * D_MODEL:] * a2.astype(F32))

    merged = _rowwise("mix_merge", merge, [gp, p0, p1, p2], [w['b_merge_gate']], [(D_MODEL, BF16)])[0]
    xo = _mm("mix_out", merged, w['w_out'], add=x)
    saved = dict(x=x, h=h, z=z, gp=gp, xs=xs, y0=y0, t_glu=t_glu, y_s5=y_s5, la=(la_f, la_b), o=(o_f, o_b), s=(s_f, s_b),
                 y_gla=y_gla, q16=q16, kd=kd, vd=vd, lse=lse, y_at=y_at, p=(p0, p1, p2), merged=merged)
    return xo, saved


def _mixer_bwd(dy, dy16, sv, w, s5_raw):
    z, h = sv['z'], sv['h']
    n_tok = z.shape[0]
    grads = {}
    dmerged = _mm("mix_dmerged", dy16, w['w_out'], tb=True, out_dtype=BF16)
    grads['w_out'] = _mm("mix_dwout", sv['merged'], dy16, ta=True, out_dtype=GRAD_DT)

    def dmerge(dm, gpv, a0, a1, a2, bias):
        dm = dm.astype(F32)
        g = _sigmoid(gpv.astype(F32) + bias)
        ps = (a0.astype(F32), a1.astype(F32), a2.astype(F32))
        dps, dgs = [], []
        for k in range(3):
            gk = g[:, k * D_MODEL:(k + 1) * D_MODEL]
            dps.append(dm * gk)
            dgs.append(dm * ps[k] * gk * (1.0 - gk))
        dgp = jnp.concatenate(dgs, axis=1)
        return dps[0], dps[1], dps[2], dgp, jnp.sum(dgp, axis=0, keepdims=True)

    dp0, dp1, dp2, dgp, dbm = _rowwise("mix_dmerge", dmerge, [dmerged, sv['gp'], *sv['p']], [w['b_merge_gate']],
                                       [(D_MODEL, BF16)] * 3 + [(3 * D_MODEL, BF16)], accs=[(1, 3 * D_MODEL)])
    grads['b_merge_gate'] = dbm[0]
    grads['w_branch_s5'] = _mm("mix_dbranch", sv['y_s5'], dp0, ta=True, out_dtype=GRAD_DT)
    grads['w_branch_gla'] = _mm("mix_dbranch", sv['y_gla'], dp1, ta=True, out_dtype=GRAD_DT)
    grads['w_branch_attn'] = _mm("mix_dbranch_f32", sv['y_at'], dp2, ta=True, out_dtype=GRAD_DT)
    dy_s5 = _mm("mix_dy", dp0, w['w_branch_s5'], tb=True)
    dy_gla = _mm("mix_dy", dp1, w['w_branch_gla'], tb=True)
    dy_at = _mm("mix_dy", dp2, w['w_branch_attn'], tb=True)
    grads['w_merge_gate'] = _mm("mix_dwgate", h, dgp, ta=True, out_dtype=GRAD_DT)
    u = _cw(z, 0, S5_W)

    def s5_post_bwd(dyv, y0, t, uv, dv, wglu):
        sg = _sigmoid(t)
        y1 = _gelu(y0)
        dt = dyv * y1 * sg * (1.0 - sg)
        dy1 = dyv * sg + _dot(dt.astype(BF16), wglu, _NT)
        dy0 = dy1 * _gelu_grad(y0)
        return dy0, dt, y1, dy0 * dv, jnp.sum(dy0 * uv, axis=0, keepdims=True)

    dy0, dt_glu, y1, du, dd = _rowwise("s5_post_bwd", s5_post_bwd, [dy_s5, sv['y0'], sv['t_glu'], u],
                                       [w['s5_d'], w['s5_w_glu']],
                                       [(S5_W, BF16), (S5_W, BF16), (S5_W, BF16), (S5_W, F32)], accs=[(1, S5_W)])
    grads['s5_d'] = dd[0]
    grads['s5_w_glu'] = _mm("s5_dwglu", y1, dt_glu, ta=True, out_dtype=GRAD_DT)
    s5_cot = []
    for d, rev in ((0, False), (1, True)):
        a_re, a_im, wb, wc = w['s5'][d]
        du, dwb, dwc, da = _s5_bwd("s5_adj_fwd" if rev else "s5_adj_rev", dy0, z, sv['xs'][d], a_re, -a_im, wb, wc, du,
                                   rev=not rev)
        da = jnp.sum(da, axis=0, keepdims=True)
        s5_cot.append((da[:, :S5_N], da[:, S5_N:], dwb, dwc))
    for d in range(2):
        _, pull = jax.vjp(_s5_prep, *s5_raw[d])
        s5_cot[d] = pull(s5_cot[d])
    for k, nm in enumerate(('s5_lambda_re', 's5_lambda_im', 's5_log_dt', 's5_b_re', 's5_b_im', 's5_c_re', 's5_c_im')):
        grads[nm] = jnp.stack([s5_cot[0][k], s5_cot[1][k]])

    gate = _cw(z, 4, GLA_W)

    def gla_post_bwd(dyv, of, ob, gv, gn):
        o, gv = of + ob, gv.astype(F32)
        dos, dgates = [], []
        dgn = jnp.zeros((1, GLA_D), F32)
        for hh in range(GLA_H):
            sl = slice(hh * GLA_D, (hh + 1) * GLA_D)
            gh, dyh = gv[:, sl], dyv[:, sl]
            sg = _sigmoid(gh)
            dgates.append(dyh * _rms(o[:, sl], gn) * sg * (1.0 + gh * (1.0 - sg)))
            dx, dg = _rms_bwd(o[:, sl], gn, dyh * gh * sg)
            dos.append(dx)
            dgn = dgn + dg
        return jnp.concatenate(dos, axis=1), jnp.concatenate(dgates, axis=1), dgn

    do, dgate, dgn = _rowwise("gla_post_bwd", gla_post_bwd, [dy_gla, *sv['o'], gate], [w['gla_norm']],
                              [(GLA_W, F32), (GLA_W, F32)], accs=[(1, GLA_D)])
    grads['gla_norm'] = dgn[0]
    dq_f, dk_f, dv_f, dla_f = _gla_bwd("gla_bwd_fwd", z, sv['la'][0], sv['s'][0], do, rev=False)
    dq_b, dk_b, dv_b, dla_b = _gla_bwd("gla_bwd_rev", z, sv['la'][1], sv['s'][1], do, rev=True)
    wa_f, wa_b = _alpha_pads(w['gla_w_alpha'])
    ba_f, ba_b = w['gla_b_alpha'][0:1], w['gla_b_alpha'][1:2]
    zz = _cw(z, ZC_Z, LANE)

    def gla_prep_bwd(dlf, dlb, zv, waf, wab, bf, bb):
        z16 = zv.astype(BF16)
        waf16, wab16 = waf.astype(BF16), wab.astype(BF16)
        xf = _dot(z16, waf16) + bf
        xb = _dot(z16, wab16) + bb
        df = dlf * (_sigmoid(-xf) / GLA_TAU)
        db = dlb * (_sigmoid(-xb) / GLA_TAU)
        dz = _dot(df.astype(BF16), waf16, _NT) + _dot(db.astype(BF16), wab16, _NT)
        return df, db, dz, jnp.sum(df, axis=0, keepdims=True), jnp.sum(db, axis=0, keepdims=True)

    dlog_f, dlog_b, dzz, dba_f, dba_b = _rowwise(
        "gla_prep_bwd", gla_prep_bwd, [dla_f, dla_b, zz], [wa_f, wa_b, ba_f, ba_b],
        [(GLA_W, BF16), (GLA_W, BF16), (LANE, F32)], accs=[(1, GLA_W), (1, GLA_W)])
    dwa_f = _mm("gla_dwa", zz, dlog_f, ta=True)
    dwa_b = _mm("gla_dwa", zz, dlog_b, ta=True)
    grads['gla_w_alpha'] = jnp.stack([dwa_f[:GLA_R], dwa_b[GLA_R:2 * GLA_R]])
    grads['gla_b_alpha'] = jnp.concatenate([dba_f, dba_b], axis=0)

    dq_r, dkd, dvd = _attn_bwd("attn_bwd", sv['q16'], sv['kd'], sv['vd'], sv['y_at'], sv['lse'], dy_at)
    cq, sq, mq, rq = _head_consts(n_tok, AT_QH)
    ck, sk, mk, rk = _head_consts(n_tok, AT_KH)
    gq = jnp.tile(w['attn_q_norm'], (1, AT_QH))
    gk = jnp.tile(w['attn_k_norm'], (1, AT_KH))

    def rope_bwd(dr, v, gain, cos, sin, mean, rot, scale):
        dr = dr * scale
        dvn = dr * cos + _dot_exact(dr * sin, rot, _NT, terms=2)
        r = lax.rsqrt(_dot_exact(v * v, mean, terms=2) + EPS)
        vh = v * r
        dg = jnp.sum(dvn * vh, axis=0, keepdims=True)
        dvh = dvn * gain
        return r * (dvh - vh * _dot_exact(dvh * vh, mean, terms=2)), dg

    def attn_prep_bwd(dq, dk2, dv2, qv, kv, cqv, sqv, ckv, skv, gqv, gkv, mqv, rqv, mkv, rkv):
        qv, kv = qv.astype(F32), kv.astype(F32)
        dqo, dgq = rope_bwd(dq, qv, gqv, cqv, sqv, mqv, rqv, AT_D ** -0.5)
        dko, dgk = rope_bwd(_fold_heads(dk2), kv, gkv, ckv, skv, mkv, rkv, 1.0)
        return dqo, dko, _fold_heads(dv2), dgq, dgk

    q_raw, k_raw = _cw(z, 5, AT_W), _cw(z, ZC_AK, AT_KW)
    daq, dak, dav, dgq, dgk = _rowwise("attn_prep_bwd", attn_prep_bwd, [dq_r, dkd, dvd, q_raw, k_raw, cq, sq, ck, sk],
                                       [gq, gk, mq, rq, mk, rk], [(AT_W, F32), (AT_KW, F32), (AT_KW, F32)],
                                       accs=[(1, AT_W), (1, AT_KW)])
    grads['attn_q_norm'] = dgq.reshape(AT_QH, AT_D).sum(axis=0)
    grads['attn_k_norm'] = dgk.reshape(AT_KH, AT_D).sum(axis=0)

    def assemble(duv, qf, qb, kf, kb, vf, vb, dg, aq, ak, av, zzv):
        return jnp.concatenate([duv, qf + qb, kf + kb, vf + vb, dg, aq, ak, av, zzv], axis=1)

    dz = _rowwise("mix_dz", assemble, [du, dq_f, dq_b, dk_f, dk_b, dv_f, dv_b, dgate, daq, dak, dav, dzz], [],
                  [(ZP_W, BF16)])[0]
    grads['w_in'] = _unpad_w_in(_mm("mix_dwin", dz, h, ta=True, out_dtype=GRAD_DT))
    wmg, wip = w['w_merge_gate'], w['w_in_p']
    tr = _tile(n_tok, 256, 16)

    def dh_norm(dgp_ref, dz_ref, wmg_ref, wip_ref, dy_ref, x_ref, g_ref, dx_ref, dx16_ref, dg_ref):
        dh = _dot(dgp_ref[...], wmg_ref[...], _NT) + _dot(dz_ref[...], wip_ref[...])
        dxv, dgv = _rms_bwd(x_ref[...], g_ref[...], dh)
        dxv = dy_ref[...] + dxv
        dx_ref[...] = dxv
        dx16_ref[...] = dxv.astype(BF16)

        @pl.when(pl.program_id(0) == 0)
        def _():
            dg_ref[...] = jnp.zeros(dg_ref.shape, F32)

        dg_ref[...] += dgv

    def rows(c):
        return pl.BlockSpec((tr, c), lambda i: (i, 0))

    def whole(t):
        return pl.BlockSpec(t.shape, lambda i: (0, 0))

    dx, dx16, dg = _pcall(
        dh_norm, name="mix_dh_norm", grid=(n_tok // tr,),
        in_specs=[rows(dgp.shape[1]), rows(dz.shape[1]), whole(wmg), whole(wip), rows(D_MODEL), rows(D_MODEL),
                  whole(w['mix_norm'])],
        out_specs=[rows(D_MODEL), rows(D_MODEL), whole(w['mix_norm'])],
        out_shape=[jax.ShapeDtypeStruct((n_tok, D_MODEL), F32), jax.ShapeDtypeStruct((n_tok, D_MODEL), BF16),
                   jax.ShapeDtypeStruct((1, D_MODEL), F32)],
        compiler_params=_cparams(("arbitrary",), 4 * (wmg.size + wip.size) + 8 * tr * (dgp.shape[1] + dz.shape[1])
                                 + 16 * tr * D_MODEL * 4),
    )(dgp, dz, wmg, wip, dy, sv['x'], w['mix_norm'])
    grads['mix_norm'] = dg[0]
    return dx, dx16, grads


FFN_NAMES = [p + k for p in ('ffn1_', 'ffn2_') for k in ('w_gate', 'w_up', 'w_down')]
MISC_SHARDED = [n for n in SHARDED if n not in FFN_NAMES]
LARGE_SHARDED = FFN_NAMES + ['w_in', 'w_merge_gate', 'w_out', 'w_branch_s5', 'w_branch_gla', 'w_branch_attn', 's5_w_glu']
KEPT_TRANSPOSED = [p + k for p in ('ffn1_', 'ffn2_') for k in ('w_gate', 'w_up')] + ['w_in']


def _stored_axis(n):
    return 1 if n in KEPT_TRANSPOSED else SHARD_AXIS[n]


def _stored(n, t):
    return jnp.swapaxes(t, 1, 2) if n in KEPT_TRANSPOSED else t


def _layer_weights(ws, rep, i):
    w = {k: _from_shards(ws[k][:, i], _stored_axis(k) - 1) for k in MISC_SHARDED}
    for k in ('ffn1_norm', 'mix_norm', 'ffn2_norm', 's5_d', 'gla_norm', 'attn_q_norm', 'attn_k_norm', 'b_merge_gate'):
        w[k] = rep[k][i].reshape(1, -1)
    w['w_in_p'] = _pad_w_in(w['w_in'])
    s5_raw = [tuple(rep[k][i, d] for k in ('s5_lambda_re', 's5_lambda_im', 's5_log_dt', 's5_b_re', 's5_b_im',
                                            's5_c_re', 's5_c_im')) for d in range(2)]
    w['s5'] = [_s5_prep(*s5_raw[d]) for d in range(2)]
    return w, s5_raw


def _device_step(x, target, ws, rep):
    lw = [_layer_weights(ws, rep, i) for i in range(DEPTH)]
    ffn = {p: (ws[p + 'w_gate'], ws[p + 'w_up'], ws[p + 'w_down']) for p in ('ffn1_', 'ffn2_')}
    saved = []
    for i in range(DEPTH):
        w, _ = lw[i]
        x, s1 = _ffn_fwd(x, w['ffn1_norm'], *ffn['ffn1_'], i)
        x, s2 = _mixer_fwd(x, w)
        x, s3 = _ffn_fwd(x, w['ffn2_norm'], *ffn['ffn2_'], i)
        saved.append((s1, s2, s3))
    gfin = rep['final_norm'].reshape(1, -1)

    def head(xv, tv, gv):
        e = _rms(xv, gv) - tv
        dx, dg = _rms_bwd(xv, gv, e * (1.0 / D_MODEL))
        part = jnp.sum(e * e, axis=0, keepdims=True)
        return dx, dx, dg, part

    dx, dx16, dgfin, part = _rowwise("loss_head", head, [x, target], [gfin], [(D_MODEL, F32), (D_MODEL, BF16)],
                                     accs=[(1, D_MODEL), (1, D_MODEL)])
    loss = (0.5 / D_MODEL) * jnp.sum(part)
    per_layer = []
    dffn = {'ffn1_': None, 'ffn2_': None}
    for i in reversed(range(DEPTH)):
        w, s5_raw = lw[i]
        s1, s2, s3 = saved[i]
        dx, dx16, n3, dffn['ffn2_'] = _ffn_bwd(dx, dx16, s3, w['ffn2_norm'], *ffn['ffn2_'], i, dffn['ffn2_'])
        dx, dx16, g = _mixer_bwd(dx, dx16, s2, w, s5_raw)
        dx, dx16, n1, dffn['ffn1_'] = _ffn_bwd(dx, dx16, s1, w['ffn1_norm'], *ffn['ffn1_'], i, dffn['ffn1_'])
        g['ffn2_norm'], g['ffn1_norm'] = n3, n1
        per_layer.append(g)
    per_layer.reverse()
    stacked = {k: jnp.stack([per_layer[i][k] for i in range(DEPTH)]) for k in per_layer[0]}
    gs = {k: _to_shards(stacked[k], _stored_axis(k)) for k in MISC_SHARDED}
    for p in ('ffn1_', 'ffn2_'):
        gs[p + 'w_gate'], gs[p + 'w_up'], gs[p + 'w_down'] = dffn[p]
    grep = {k: stacked[k] for k in REPLICATED if k != 'final_norm'}
    grep['final_norm'] = dgfin[0]
    return loss, dx, gs, grep


PACK_C = 1024
ROW_TILE = 512


def _rows_of(shape):
    return -(-int(np.prod(shape)) // PACK_C)


def _as_rows(t, lead=0):
    head = t.shape[:lead]
    flat = t.reshape(head + (-1,))
    rows = -(-flat.shape[-1] // PACK_C)
    pad = rows * PACK_C - flat.shape[-1]
    if pad:
        flat = jnp.pad(flat, [(0, 0)] * lead + [(0, pad)])
    return flat.reshape(head + (rows, PACK_C))


def _pack(items, total_rows, lead=0):
    parts = [_as_rows(t, lead) for t in items]
    used = sum(p.shape[lead] for p in parts)
    if total_rows > used:
        parts.append(jnp.zeros(parts[0].shape[:lead] + (total_rows - used, PACK_C), parts[0].dtype))
    return jnp.concatenate(parts, axis=lead)


def _unpack(rows, shapes, lead=0):
    out, r0 = [], 0
    head = rows.shape[:lead]
    for shp in shapes:
        n, size = _rows_of(shp), int(np.prod(shp))
        piece = lax.slice_in_dim(rows, r0, r0 + n, axis=lead).reshape(head + (n * PACK_C,))
        out.append(lax.slice_in_dim(piece, 0, size, axis=lead).reshape(head + tuple(shp)))
        r0 += n
    return out


def _round_up(n, m):
    return -(-n // m) * m


def _to_shards(g, axis):
    shp = g.shape
    g = g.reshape(shp[:axis] + (N_CHIPS, shp[axis] // N_CHIPS) + shp[axis + 1:])
    return jnp.moveaxis(g, axis, 0)


def _from_shards(s, axis):
    s = jnp.moveaxis(s, 0, axis)
    shp = s.shape
    return s.reshape(shp[:axis] + (shp[axis] * shp[axis + 1],) + shp[axis + 2:])


_HBM = pl.BlockSpec(memory_space=pltpu.HBM)


def _place():
    return lax.axis_index("x"), lax.axis_index("y"), lax.axis_index("c")


def _rcopy(src, dst, send, recv, k, dev):
    return pltpu.make_async_remote_copy(src_ref=src, dst_ref=dst, send_sem=send.at[k], recv_sem=recv.at[k],
                                        device_id=dev, device_id_type=MESH)


def _comm_call(name, body, arrays, out_shapes, n_sems, n_local):
    return pl.pallas_call(
        body, name=name, in_specs=[_HBM] * len(arrays), out_specs=[_HBM] * len(out_shapes), out_shape=out_shapes,
        scratch_shapes=[pltpu.SemaphoreType.DMA((n_sems,)), pltpu.SemaphoreType.DMA((n_sems,)),
                        pltpu.SemaphoreType.DMA((max(n_local, 1),))],
    )(*arrays)


def _ici_peers(x, y, c):
    x_first = c == 0
    first = (jnp.where(x_first, 1 - x, x), jnp.where(x_first, y, 1 - y))
    second = (jnp.where(x_first, x, 1 - x), jnp.where(x_first, 1 - y, y))
    return first, second


def _ag_chips(name, packs):
    n = len(packs)

    def body(*refs):
        p, o, (send, recv, _) = refs[:n], refs[n:2 * n], refs[2 * n:]
        x, y, c = _place()
        (ax, ay), (bx, by) = _ici_peers(x, y, c)
        me, na, nb, nd = 2 * x + y, 2 * ax + ay, 2 * bx + by, 2 * (1 - x) + (1 - y)
        sib = (x, y, 1 - c)
        own = [_rcopy(p[a].at[c], o[a].at[me, c], send, recv, 6 * a, (ax, ay, c)) for a in range(n)]
        own += [_rcopy(p[a].at[c], o[a].at[me, c], send, recv, 6 * a + 1, (bx, by, c)) for a in range(n)]
        for cp in own:
            cp.start()
        passed = []

        def landed(a, chip, k, fwd):
            blk = o[a].at[chip, c]
            _rcopy(blk, blk, send, recv, 6 * a + k, sib).wait_recv()
            for sem, dev in fwd:
                cp = _rcopy(blk, blk, send, recv, 6 * a + sem, dev)
                cp.start()
                passed.append(cp)

        for a in range(n):
            landed(a, na, 0, [(2, (bx, by, c)), (3, sib)])
        for a in range(n):
            landed(a, nb, 1, [(4, sib)])
        for a in range(n):
            landed(a, nd, 2, [(5, sib)])
        for a in range(n):
            for k, chip in ((3, nb), (4, na), (5, nd)):
                blk = o[a].at[chip, 1 - c]
                _rcopy(blk, blk, send, recv, 6 * a + k, sib).wait_recv()
        for cp in own + passed:
            cp.wait_send()

    outs = _comm_call(name, body, packs, [jax.ShapeDtypeStruct((N_CHIPS,) + t.shape, t.dtype) for t in packs], 6 * n, 0)
    chip = 2 * lax.axis_index("x") + lax.axis_index("y")
    return [lax.dynamic_update_slice_in_dim(o, t[None], chip, axis=0) for o, t in zip(outs, packs)]


def _pair_swap(name, gs):
    n = len(gs)
    ns = gs[0].shape[0]

    def body(*refs):
        g, r, (send, recv, _) = refs[:n], refs[n:2 * n], refs[2 * n:]
        x, y, c = _place()
        cps = [_rcopy(g[a].at[j, 1 - c], r[a].at[j], send, recv, ns * a + j, (x, y, 1 - c))
               for a in range(n) for j in range(ns)]
        for cp in cps:
            cp.start()
        for cp in cps:
            cp.wait_recv()
        for cp in cps:
            cp.wait_send()

    return _comm_call(name, body, gs, [jax.ShapeDtypeStruct((ns,) + t.shape[2:], t.dtype) for t in gs], ns * n, 0)


def _halve_first(name, parts):
    n = len(parts)

    def body(*refs):
        p, r, (send, recv, _) = refs[:n], refs[n:2 * n], refs[2 * n:]
        x, y, c = _place()
        (ax, ay), _ = _ici_peers(x, y, c)
        cps = [_rcopy(p[a].at[jnp.where(c == 0, 2 * ax + k, 2 * k + ay)], r[a].at[k], send, recv, 2 * a + k, (ax, ay, c))
               for a in range(n) for k in range(2)]
        for cp in cps:
            cp.start()
        for cp in cps:
            cp.wait_recv()
        for cp in cps:
            cp.wait_send()

    return _comm_call(name, body, parts, [jax.ShapeDtypeStruct((2,) + t.shape[1:], t.dtype) for t in parts], 2 * n, 0)


def _halve_second(name, kept):
    n = len(kept)

    def body(*refs):
        p, r, (send, recv, _) = refs[:n], refs[n:2 * n], refs[2 * n:]
        x, y, c = _place()
        _, (bx, by) = _ici_peers(x, y, c)
        cps = [_rcopy(p[a].at[jnp.where(c == 0, by, bx)], r[a], send, recv, a, (bx, by, c)) for a in range(n)]
        for cp in cps:
            cp.start()
        for cp in cps:
            cp.wait_recv()
        for cp in cps:
            cp.wait_send()

    return _comm_call(name, body, kept, [jax.ShapeDtypeStruct(t.shape[1:], t.dtype) for t in kept], n, 0)


def _add_selected(name, t, r, index, out_dtype):
    single = r.ndim == 2
    r3 = r[None] if single else r
    nk, rh, cc = r3.shape
    tl = _tile(rh, ROW_TILE, 16)

    def body(i_ref, t_ref, r_ref, o_ref):
        o_ref[...] = (t_ref[...].astype(F32) + r_ref[...].astype(F32)).astype(o_ref.dtype)

    out = _pcall(
        body, name=name,
        grid_spec=pltpu.PrefetchScalarGridSpec(
            num_scalar_prefetch=1, grid=(nk, rh // tl),
            in_specs=[pl.BlockSpec((1, tl, cc), lambda k, i, i_ref: (i_ref[k], i, 0)),
                      pl.BlockSpec((1, tl, cc), lambda k, i, i_ref: (k, i, 0))],
            out_specs=pl.BlockSpec((1, tl, cc), lambda k, i, i_ref: (k, i, 0))),
        out_shape=jax.ShapeDtypeStruct(r3.shape, out_dtype),
        compiler_params=_cparams(("parallel", "parallel"), 8 * tl * cc * 4),
    )(index, t, r3)
    return out[0] if single else out


def _pair_share(name, reds):
    n = len(reds)

    def body(*refs):
        r, o, (send, recv, _) = refs[:n], refs[n:2 * n], refs[2 * n:]
        x, y, c = _place()
        cps = [_rcopy(r[a], o[a].at[c], send, recv, a, (x, y, 1 - c)) for a in range(n)]
        for cp in cps:
            cp.start()
        for a in range(n):
            blk = o[a].at[1 - c]
            _rcopy(blk, blk, send, recv, a, (x, y, 1 - c)).wait_recv()
        for cp in cps:
            cp.wait_send()

    outs = _comm_call(name, body, reds, [jax.ShapeDtypeStruct((2,) + t.shape, t.dtype) for t in reds], n, 0)
    core = lax.axis_index("c")
    return [lax.dynamic_update_slice_in_dim(o, t[None], core, axis=0) for o, t in zip(outs, reds)]


def _add_own_half(name, g, r):
    ns, _, rh, cc = g.shape
    tl = _tile(rh, ROW_TILE, 16)
    core = lax.axis_index("c").astype(jnp.int32).reshape(1)

    def body(c_ref, g_ref, r_ref, o_ref):
        o_ref[...] = (g_ref[0].astype(F32) + r_ref[...].astype(F32)).astype(o_ref.dtype)

    return _pcall(
        body, name=name,
        grid_spec=pltpu.PrefetchScalarGridSpec(
            num_scalar_prefetch=1, grid=(ns, rh // tl),
            in_specs=[pl.BlockSpec((1, 1, tl, cc), lambda j, i, c_ref: (j, c_ref[0], i, 0)),
                      pl.BlockSpec((1, tl, cc), lambda j, i, c_ref: (j, i, 0))],
            out_specs=pl.BlockSpec((1, tl, cc), lambda j, i, c_ref: (j, i, 0))),
        out_shape=jax.ShapeDtypeStruct(r.shape, BF16),
        compiler_params=_cparams(("parallel", "parallel"), 8 * tl * cc * 4),
    )(core, g, r)


def _reduce_scatter(gs):
    x, y, c = lax.axis_index("x"), lax.axis_index("y"), lax.axis_index("c")
    mine_first = jnp.stack([jnp.where(c == 0, 2 * x + k, 2 * k + y) for k in range(2)]).astype(jnp.int32)
    mine_second = jnp.where(c == 0, y, x).astype(jnp.int32).reshape(1)
    swapped = _pair_swap("rs_pair_swap", gs)
    parts = [_add_own_half("rs_add_pair", g, r) for g, r in zip(gs, swapped)]
    got = _halve_first("rs_halve_first", parts)
    kept = [_add_selected("rs_add_first", t, r, mine_first, BF16) for t, r in zip(parts, got)]
    got = _halve_second("rs_halve_second", kept)
    reds = [_add_selected("rs_add_second", t, r, mine_second, F32) for t, r in zip(kept, got)]
    return _pair_share("rs_pair_share", reds)


def _adamw(name, g, w, m, v):
    c1 = 1.0 - ADAM_B1 ** ADAM_STEP
    c2 = 1.0 - ADAM_B2 ** ADAM_STEP
    nb, rows, cc = g.shape
    tl = _tile(rows, ROW_TILE, SUBLANE)

    def body(g_ref, w_ref, m_ref, v_ref, d_ref, mo_ref, vo_ref):
        gv = g_ref[...]
        mn = ADAM_B1 * m_ref[...] + (1.0 - ADAM_B1) * gv
        vn = ADAM_B2 * v_ref[...] + (1.0 - ADAM_B2) * (gv * gv)
        d_ref[...] = -ADAM_LR * ((mn / c1) / (jnp.sqrt(vn / c2) + ADAM_EPS) + ADAM_WD * w_ref[...])
        mo_ref[...] = mn
        vo_ref[...] = vn

    spec = pl.BlockSpec((1, tl, cc), lambda b, i: (b, i, 0))
    return _pcall(
        body, name=name, grid=(nb, rows // tl), in_specs=[spec] * 4, out_specs=[spec] * 3,
        out_shape=[jax.ShapeDtypeStruct(g.shape, F32)] * 3,
        compiler_params=_cparams(("parallel", "parallel"), 16 * tl * cc * 4),
    )(g, w, m, v)


def _train_step(a):
    x, target = a['x'][0], a['loss_target'][0]

    names = BF16_GATHER + F32_GATHER
    got = _ag_chips("ag_weights", [_stored(n, a[n]).astype(BF16) for n in BF16_GATHER] + [a[n] for n in F32_GATHER])
    ws = dict(zip(names, got))

    loss, gx, gs, grep = _device_step(x, target, ws, {n: a[n] for n in REPLICATED})
    loss = lax.psum(loss, ("x", "y", "c"))

    small = [n for n in MISC_SHARDED if n not in LARGE_SHARDED]
    small_shapes = [a[n].shape[1:] for n in small]
    rep_shapes = [a[n].shape for n in REPLICATED]
    small_rows = sum(_rows_of(s) for s in small_shapes)
    piece = _round_up(-(-sum(_rows_of(s) for s in rep_shapes) // (2 * N_CHIPS)), SUBLANE)
    rep_rows = 2 * N_CHIPS * piece
    rep_pack = _pack([grep[n] for n in REPLICATED], rep_rows).reshape(N_CHIPS, 2, piece, PACK_C)
    total = _round_up(small_rows + piece, ROW_TILE // 2)
    misc = _pack([gs[n] for n in small] + [rep_pack], total, lead=2)
    reds = _reduce_scatter([gs[n] for n in LARGE_SHARDED] + [misc])
    grad = dict(zip(LARGE_SHARDED, reds[:-1]))
    grad.update(zip(small, _unpack(reds[-1], small_shapes, lead=1)))
    rep_mine = reds[-1][:, small_rows:small_rows + piece]
    rep_all = _ag_chips("ag_replicated", [rep_mine])[0].reshape(1, rep_rows, PACK_C)

    outs = {}
    for n in SHARDED:
        shp = grad[n].shape
        three_d = (shp[0], -1, shp[-1])
        params = (grad[n], *(_stored(n, a[p + n]) for p in ('', 'm_', 'v_')))
        res3 = _adamw("adamw", *(t.reshape(three_d) for t in params))
        for kind, arr in zip(('grad', 'delta', 'new_m', 'new_v'), (grad[n], *res3)):
            outs[kind + '_' + n] = _stored(n, arr.reshape(shp))
    w, m, v = (_pack([a[p + n] for n in REPLICATED], rep_rows)[None] for p in ('', 'm_', 'v_'))
    res3 = _adamw("adamw_replicated", rep_all, w, m, v)
    for kind, arr in zip(('grad', 'delta', 'new_m', 'new_v'), (rep_all, *res3)):
        for n, t in zip(REPLICATED, _unpack(arr[0], rep_shapes)):
            outs[kind + '_' + n] = t
    res = [loss, gx[None]]
    for kind in ('grad', 'delta', 'new_m', 'new_v'):
        res += [outs[kind + '_' + n] for n in W_NAMES]
    return tuple(res)


def kernel(x, ffn1_norm, ffn1_w_gate, ffn1_w_up, ffn1_w_down, mix_norm, w_in, s5_lambda_re, s5_lambda_im, s5_log_dt, s5_b_re, s5_b_im, s5_c_re, s5_c_im, s5_d, s5_w_glu, gla_w_alpha, gla_b_alpha, gla_norm, attn_q_norm, attn_k_norm, w_branch_s5, w_branch_gla, w_branch_attn, w_merge_gate, b_merge_gate, w_out, ffn2_norm, ffn2_w_gate, ffn2_w_up, ffn2_w_down, final_norm, loss_target, m_ffn1_norm, m_ffn1_w_gate, m_ffn1_w_up, m_ffn1_w_down, m_mix_norm, m_w_in, m_s5_lambda_re, m_s5_lambda_im, m_s5_log_dt, m_s5_b_re, m_s5_b_im, m_s5_c_re, m_s5_c_im, m_s5_d, m_s5_w_glu, m_gla_w_alpha, m_gla_b_alpha, m_gla_norm, m_attn_q_norm, m_attn_k_norm, m_w_branch_s5, m_w_branch_gla, m_w_branch_attn, m_w_merge_gate, m_b_merge_gate, m_w_out, m_ffn2_norm, m_ffn2_w_gate, m_ffn2_w_up, m_ffn2_w_down, m_final_norm, v_ffn1_norm, v_ffn1_w_gate, v_ffn1_w_up, v_ffn1_w_down, v_mix_norm, v_w_in, v_s5_lambda_re, v_s5_lambda_im, v_s5_log_dt, v_s5_b_re, v_s5_b_im, v_s5_c_re, v_s5_c_im, v_s5_d, v_s5_w_glu, v_gla_w_alpha, v_gla_b_alpha, v_gla_norm, v_attn_q_norm, v_attn_k_norm, v_w_branch_s5, v_w_branch_gla, v_w_branch_attn, v_w_merge_gate, v_b_merge_gate, v_w_out, v_ffn2_norm, v_ffn2_w_gate, v_ffn2_w_up, v_ffn2_w_down, v_final_norm):
    return _train_step(dict(locals()))
```

```python
import functools
import math

import jax
import jax.numpy as jnp
import numpy as np
from jax import lax
from jax.experimental import pallas as pl
from jax.experimental.pallas import tpu as pltpu

F32 = jnp.float32
BF16 = jnp.bfloat16

D_MODEL = 1024
DEPTH = 2
EPS = 1e-6
S5_G, S5_H, S5_P = 32, 16, 64
S5_W = S5_G * S5_H
S5_N = S5_G * S5_P
S5_TG = 8
S5_TILES = S5_G // S5_TG
S5_TC, S5_TS = S5_TG * S5_H, S5_TG * S5_P
GLA_H, GLA_D = 4, 128
GLA_W = GLA_H * GLA_D
GLA_R = 16
GLA_TAU = 16.0
GLA_C = 64
AT_QH, AT_KH, AT_D = 8, 2, 64
AT_W = AT_QH * AT_D
AT_KW = AT_KH * AT_D
GRID_W = 64
ROPE_BASE = 10000.0
ZP_W = 3456
ZC_AK, ZC_AV, ZC_Z = 24, 25, 26

ADAM_LR, ADAM_B1, ADAM_B2, ADAM_EPS, ADAM_WD, ADAM_STEP = 0.001, 0.9, 0.999, 1e-08, 0.01, 10

GRAD_DT = BF16
VMEM_LIMIT_V7X = 56 * 1024 * 1024
LANE = 128
SUBLANE = 8

W_NAMES = ['ffn1_norm', 'ffn1_w_gate', 'ffn1_w_up', 'ffn1_w_down', 'mix_norm', 'w_in', 's5_lambda_re', 's5_lambda_im',
           's5_log_dt', 's5_b_re', 's5_b_im', 's5_c_re', 's5_c_im', 's5_d', 's5_w_glu', 'gla_w_alpha', 'gla_b_alpha',
           'gla_norm', 'attn_q_norm', 'attn_k_norm', 'w_branch_s5', 'w_branch_gla', 'w_branch_attn', 'w_merge_gate',
           'b_merge_gate', 'w_out', 'ffn2_norm', 'ffn2_w_gate', 'ffn2_w_up', 'ffn2_w_down', 'final_norm']
SHARD_AXIS = {'ffn1_w_gate': 2, 'ffn1_w_up': 2, 'ffn1_w_down': 1, 'w_in': 2, 's5_w_glu': 1, 'gla_w_alpha': 3,
              'gla_b_alpha': 2, 'w_branch_s5': 2, 'w_branch_gla': 2, 'w_branch_attn': 2, 'w_merge_gate': 2,
              'w_out': 1, 'ffn2_w_gate': 2, 'ffn2_w_up': 2, 'ffn2_w_down': 1}
F32_GATHER = ['gla_w_alpha', 'gla_b_alpha']
BF16_GATHER = [n for n in W_NAMES if n in SHARD_AXIS and n not in F32_GATHER]
SHARDED = [n for n in W_NAMES if n in SHARD_AXIS]
REPLICATED = [n for n in W_NAMES if n not in SHARD_AXIS]
N_CHIPS = 4
MESH = pl.DeviceIdType.MESH


def _tile(n, target, mult):
    best = None
    for t in range(mult, min(n, target) + 1, mult):
        if n % t == 0:
            best = t
    return best if best is not None else n


def _cparams(sem, vmem_bytes):
    limit = int(min(VMEM_LIMIT_V7X, max(32 * 1024 * 1024, vmem_bytes * 5 // 4)))
    return pltpu.CompilerParams(dimension_semantics=sem, vmem_limit_bytes=limit)


def _pcall(body, **kw):
    shapes = kw.pop('out_shape')
    many = isinstance(shapes, (list, tuple))
    pinned = [pltpu.HBM(t.shape, t.dtype) for t in (shapes if many else [shapes])]
    call = pl.pallas_call(body, out_shape=pinned if many else pinned[0], **kw)

    def run(*ops):
        return call(*[o if jnp.issubdtype(o.dtype, jnp.integer) else pltpu.with_memory_space_constraint(o, pltpu.HBM)
                      for o in ops])

    return run


def _cw(arr, cb, width):
    return (arr, cb, width)


def _win(a):
    if isinstance(a, tuple):
        return a
    return (a, 0, a.shape[-1])


def _nbytes(shape, dtype):
    return int(np.prod(shape)) * jnp.dtype(dtype).itemsize


def _rowwise(name, fn, rows, fulls, outs, accs=(), tl=256):
    rows = [_win(r) for r in rows]
    n_tok = rows[0][0].shape[0]
    tl = _tile(n_tok, tl, 16)
    nr, nf, no = len(rows), len(fulls), len(outs)

    def body(*refs):
        vals = [r[...] for r in refs[:nr + nf]]
        res = fn(*vals)
        if not isinstance(res, (tuple, list)):
            res = (res,)
        for k in range(no):
            o = refs[nr + nf + k]
            o[...] = res[k].astype(o.dtype)
        if accs:
            @pl.when(pl.program_id(0) == 0)
            def _():
                for k in range(len(accs)):
                    a = refs[nr + nf + no + k]
                    a[...] = jnp.zeros(a.shape, a.dtype)
            for k in range(len(accs)):
                a = refs[nr + nf + no + k]
                a[...] += res[no + k]

    in_specs = [pl.BlockSpec((tl, w), functools.partial(lambda i, cb: (i, cb), cb=cb)) for (_, cb, w) in rows]
    in_specs += [pl.BlockSpec(f.shape, functools.partial(lambda i, nd: (0,) * nd, nd=f.ndim)) for f in fulls]
    out_specs = [pl.BlockSpec((tl, c), lambda i: (i, 0)) for (c, _) in outs]
    out_specs += [pl.BlockSpec((r, c), lambda i: (0, 0)) for (r, c) in accs]
    out_shape = [jax.ShapeDtypeStruct((n_tok, c), dt) for (c, dt) in outs]
    out_shape += [jax.ShapeDtypeStruct((r, c), F32) for (r, c) in accs]
    vm = 2 * sum(_nbytes((tl, w), a.dtype) for (a, _, w) in rows)
    vm += 2 * sum(_nbytes(f.shape, f.dtype) for f in fulls)
    vm += 2 * sum(_nbytes((tl, c), dt) for (c, dt) in outs)
    vm += 8 * max([_nbytes((tl, w), F32) for (_, _, w) in rows] + [_nbytes((tl, c), F32) for (c, _) in outs])
    return _pcall(
        body, name=name, grid=(n_tok // tl,), in_specs=in_specs, out_specs=out_specs, out_shape=out_shape,
        compiler_params=_cparams(("arbitrary",) if accs else ("parallel",), vm),
    )(*[r[0] for r in rows], *fulls)


def _mm(name, a, b, *, ta=False, tb=False, out_dtype=F32, alpha=1.0, add=None, tm=512, tn=512):
    a_arr, a_cb, a_w = _win(a)
    b_arr, b_cb, b_w = _win(b)
    if ta:
        kdim, m = a_arr.shape[0], a_w
    else:
        m, kdim = a_arr.shape[0], a_w
    if tb:
        n, kb = b_arr.shape[0], b_w
    else:
        kb, n = b_arr.shape[0], b_w
    assert kdim == kb, (name, kdim, kb)
    if not ta and kdim <= 1024:
        tm = 2 * tm
    tm = _tile(m, tm, LANE if ta else 16)
    tn = _tile(n, tn, LANE)
    if ta:
        a_spec = pl.BlockSpec((kdim, tm), lambda i, j: (0, a_cb * (a_w // tm) + i))
        a_blk = (kdim, tm)
    else:
        a_spec = pl.BlockSpec((tm, kdim), lambda i, j: (i, a_cb))
        a_blk = (tm, kdim)
    if tb:
        b_spec = pl.BlockSpec((tn, kdim), lambda i, j: (j, b_cb))
        b_blk = (tn, kdim)
    else:
        b_spec = pl.BlockSpec((kdim, tn), lambda i, j: (0, b_cb * (b_w // tn) + j))
        b_blk = (kdim, tn)
    dims = (((0 if ta else 1,), (1 if tb else 0,)), ((), ()))
    has_add = add is not None

    def body(*refs):
        a_ref, b_ref = refs[0], refs[1]
        o_ref = refs[-1]
        acc = lax.dot_general(a_ref[...].astype(BF16), b_ref[...].astype(BF16), dims, preferred_element_type=F32)
        if alpha != 1.0:
            acc = acc * alpha
        if has_add:
            acc = acc + refs[2][...].astype(F32)
        o_ref[...] = acc.astype(o_ref.dtype)

    in_specs = [a_spec, b_spec]
    ops = [a_arr, b_arr]
    if has_add:
        in_specs.append(pl.BlockSpec((tm, tn), lambda i, j: (i, j)))
        ops.append(add)
    vm = 2 * (_nbytes(a_blk, a_arr.dtype) + _nbytes(b_blk, b_arr.dtype) + _nbytes((tm, tn), out_dtype))
    vm += _nbytes(a_blk, BF16) + _nbytes(b_blk, BF16) + 3 * _nbytes((tm, tn), F32)
    return _pcall(
        body, name=name, grid=(m // tm, n // tn), in_specs=in_specs,
        out_specs=pl.BlockSpec((tm, tn), lambda i, j: (i, j)),
        out_shape=jax.ShapeDtypeStruct((m, n), out_dtype),
        compiler_params=_cparams(("parallel", "parallel"), vm),
    )(*ops)


def _sigmoid(x):
    return 0.5 * jnp.tanh(0.5 * x) + 0.5


def _rms(x, gain):
    r = lax.rsqrt(jnp.mean(x * x, axis=-1, keepdims=True) + EPS)
    return x * r * gain


def _rms_bwd(x, gain, dh):
    r = lax.rsqrt(jnp.mean(x * x, axis=-1, keepdims=True) + EPS)
    xh = x * r
    dg = jnp.sum(dh * xh, axis=0, keepdims=True)
    dxh = dh * gain
    dx = r * (dxh - xh * jnp.mean(dxh * xh, axis=-1, keepdims=True))
    return dx, dg


_GELU_C = math.sqrt(2.0 / math.pi)


def _gelu(y):
    return 0.5 * y * (1.0 + jnp.tanh(_GELU_C * (y + 0.044715 * y * y * y)))


def _gelu_grad(y):
    th = jnp.tanh(_GELU_C * (y + 0.044715 * y * y * y))
    return 0.5 * (1.0 + th) + 0.5 * y * (1.0 - th * th) * _GELU_C * (1.0 + 3.0 * 0.044715 * y * y)


def _log_sigmoid(x):
    return jnp.minimum(x, 0.0) - jnp.log(1.0 + jnp.exp(-jnp.abs(x)))


def _dot(a, b, dims=(((1,), (0,)), ((), ())), precision=None):
    return lax.dot_general(a, b, dims, preferred_element_type=F32, precision=precision)


def _dot_exact(x, m, dims=(((1,), (0,)), ((), ())), terms=3, m_first=False):
    m16 = m.astype(BF16)
    acc, rest = None, x
    for k in range(terms):
        piece = rest.astype(BF16)
        t = _dot(m16, piece, dims) if m_first else _dot(piece, m16, dims)
        acc = t if acc is None else acc + t
        if k + 1 < terms:
            rest = rest - piece.astype(F32)
    return acc


_NT = (((1,), (1,)), ((), ()))
_TN = (((0,), (0,)), ((), ()))


def _scan_core(b_ref, are_ref, aim_ref, pw, last, of32, *, rev, xf32=None, da_ref=None, lc=512):
    nb, nc = b_ref.shape[0] // SUBLANE, S5_N // lc
    row = lax.broadcasted_iota(jnp.int32, (SUBLANE, lc), 0)

    def cmul(ar, ai, xr, xi):
        return ar * xr - ai * xi, ar * xi + ai * xr

    def hs(c, xr, xi):
        cs = pl.ds(c * lc, lc)
        for k, slot in ((1, 0), (2, 2), (4, 4)):
            ar, ai = pw[slot, :, cs], pw[slot + 1, :, cs]
            if rev:
                sr, si = pltpu.roll(xr, SUBLANE - k, 0), pltpu.roll(xi, SUBLANE - k, 0)
                keep = row < SUBLANE - k
            else:
                sr, si = pltpu.roll(xr, k, 0), pltpu.roll(xi, k, 0)
                keep = row >= k
            sr, si = jnp.where(keep, sr, 0.0), jnp.where(keep, si, 0.0)
            pr, pi = cmul(ar, ai, sr, si)
            xr, xi = xr + pr, xi + pi
        return xr, xi

    @pl.when(pl.program_id(0) == 0)
    def _():
        last[...] = jnp.zeros(last.shape, F32)
        if da_ref is not None:
            da_ref[...] = jnp.zeros(da_ref.shape, F32)
        for c in range(nc):
            cs = pl.ds(c * lc, lc)
            a1r = jnp.broadcast_to(are_ref[:, cs], (SUBLANE, lc))
            a1i = jnp.broadcast_to(aim_ref[:, cs], (SUBLANE, lc))
            a2r, a2i = cmul(a1r, a1i, a1r, a1i)
            a4r, a4i = cmul(a2r, a2i, a2r, a2i)
            for slot, v in enumerate((a1r, a1i, a2r, a2i, a4r, a4i)):
                pw[slot, :, cs] = v
            first = SUBLANE - 1 if rev else 0
            pr, pi = hs(c, jnp.where(row == first, a1r, 0.0), jnp.where(row == first, a1i, 0.0))
            pw[6, :, cs] = pr
            pw[7, :, cs] = pi

    edge = 0 if rev else SUBLANE - 1

    def blk(j, carry):
        jj = (nb - 1 - j) if rev else j
        r0 = pl.multiple_of(jj * SUBLANE, SUBLANE)
        for c in range(nc):
            cre, cim = pl.ds(c * lc, lc), pl.ds(S5_N + c * lc, lc)
            xr, xi = hs(c, b_ref[pl.ds(r0, SUBLANE), cre], b_ref[pl.ds(r0, SUBLANE), cim])
            cr = jnp.broadcast_to(last[edge:edge + 1, cre], (SUBLANE, lc))
            ci = jnp.broadcast_to(last[edge:edge + 1, cim], (SUBLANE, lc))
            pr, pi = cmul(pw[6, :, cre], pw[7, :, cre], cr, ci)
            xr, xi = xr + pr, xi + pi
            if da_ref is not None:
                if rev:
                    qr, qi = pltpu.roll(xr, SUBLANE - 1, 0), pltpu.roll(xi, SUBLANE - 1, 0)
                    fill = row == SUBLANE - 1
                else:
                    qr, qi = pltpu.roll(xr, 1, 0), pltpu.roll(xi, 1, 0)
                    fill = row == 0
                qr, qi = jnp.where(fill, cr, qr), jnp.where(fill, ci, qi)
                zr, zi = xf32[pl.ds(r0, SUBLANE), cre], xf32[pl.ds(r0, SUBLANE), cim]
                da_ref[:, cre] += zr * qr + zi * qi
                da_ref[:, cim] += zr * qi - zi * qr
            last[:, cre] = xr
            last[:, cim] = xi
            of32[pl.ds(r0, SUBLANE), cre] = xr
            of32[pl.ds(r0, SUBLANE), cim] = xi
        return carry

    lax.fori_loop(0, nb, blk, 0)


def _s5_tiles():
    return [(slice((j % S5_TILES) * S5_TC, (j % S5_TILES + 1) * S5_TC), slice(j * S5_TS, (j + 1) * S5_TS))
            for j in range(2 * S5_TILES)]


def _s5_scratch(tb):
    return [pltpu.VMEM((8, SUBLANE, S5_N), F32), pltpu.VMEM((SUBLANE, 2 * S5_N), F32),
            pltpu.VMEM((tb, 2 * S5_N), F32), pltpu.VMEM((tb, 2 * S5_N), F32),
            pltpu.VMEM((S5_TC, 2 * S5_N), BF16), pltpu.VMEM((2 * S5_N, S5_TC), BF16)]


def _s5_fwd(name, z, a_re, a_im, wb, wc, add, *, rev, tb=512):
    n_tok = z.shape[0]
    tb = _tile(n_tok, tb, 16)
    nt = n_tok // tb
    has_add = add is not None

    def body(*refs):
        u_ref, are_ref, aim_ref, wb_ref, wc_ref = refs[:5]
        x_ref, y_ref, pw, last, of32, bu, wb16, wc16 = refs[5 + has_add:]

        @pl.when(pl.program_id(0) == 0)
        def _():
            wb16[...] = wb_ref[...].astype(BF16)
            wc16[...] = wc_ref[...].astype(BF16)

        u16 = u_ref[...].astype(BF16)
        for ch, st in _s5_tiles():
            bu[:, st] = _dot(u16[:, ch], wb16[:, st])
        _scan_core(bu, are_ref, aim_ref, pw, last, of32, rev=rev)
        x16 = of32[...].astype(BF16)
        x_ref[...] = x16
        ys = []
        for k in range(S5_TILES):
            re, im = slice(k * S5_TS, (k + 1) * S5_TS), slice(S5_N + k * S5_TS, S5_N + (k + 1) * S5_TS)
            ys.append(_dot(x16[:, re], wc16[re, :]) + _dot(x16[:, im], wc16[im, :]))
        y = jnp.concatenate(ys, axis=1)
        y_ref[...] = y + refs[5][...] if has_add else y

    tmap = (lambda t: (nt - 1 - t, 0)) if rev else (lambda t: (t, 0))
    const = lambda t: (0, 0)
    in_specs = [pl.BlockSpec((tb, S5_W), tmap), pl.BlockSpec((1, S5_N), const), pl.BlockSpec((1, S5_N), const),
                pl.BlockSpec((S5_TC, 2 * S5_N), const), pl.BlockSpec((2 * S5_N, S5_TC), const)]
    ops = [z, a_re, a_im, wb, wc]
    if has_add:
        in_specs.append(pl.BlockSpec((tb, S5_W), tmap))
        ops.append(add)
    return _pcall(
        body, name=name, grid=(nt,), in_specs=in_specs,
        out_specs=[pl.BlockSpec((tb, 2 * S5_N), tmap), pl.BlockSpec((tb, S5_W), tmap)],
        out_shape=[jax.ShapeDtypeStruct((n_tok, 2 * S5_N), BF16), jax.ShapeDtypeStruct((n_tok, S5_W), F32)],
        scratch_shapes=_s5_scratch(tb), compiler_params=_cparams(("arbitrary",), 10 * tb * 2 * S5_N * 4),
    )(*ops)


def _s5_bwd(name, dy, z, xs, a_re, a_im, wb, wc, add, *, rev, tb=256):
    n_tok = z.shape[0]
    tb = _tile(n_tok, tb, 16)
    nt = n_tok // tb

    def body(dy_ref, u_ref, xs_ref, are_ref, aim_ref, wb_ref, wc_ref, add_ref, du_ref, dwb_ref, dwc_ref, da_ref,
             pw, last, of32, gx, wb16, wc16, xf32):
        @pl.when(pl.program_id(0) == 0)
        def _():
            wb16[...] = wb_ref[...].astype(BF16)
            wc16[...] = wc_ref[...].astype(BF16)
            dwb_ref[...] = jnp.zeros(dwb_ref.shape, F32)
            dwc_ref[...] = jnp.zeros(dwc_ref.shape, F32)

        dy16, u16, x16 = dy_ref[...].astype(BF16), u_ref[...].astype(BF16), xs_ref[...]
        for ch, st in _s5_tiles():
            gx[:, st] = _dot(dy16[:, ch], wc16[st, :], _NT)
        xf32[...] = x16.astype(F32)
        _scan_core(gx, are_ref, aim_ref, pw, last, of32, rev=rev, xf32=xf32, da_ref=da_ref)
        g16 = of32[...].astype(BF16)
        dus = []
        for k in range(S5_TILES):
            re, im = slice(k * S5_TS, (k + 1) * S5_TS), slice(S5_N + k * S5_TS, S5_N + (k + 1) * S5_TS)
            dus.append(_dot(g16[:, re], wb16[:, re], _NT) + _dot(g16[:, im], wb16[:, im], _NT))
        du_ref[...] = jnp.concatenate(dus, axis=1) + add_ref[...]
        for ch, st in _s5_tiles():
            dwb_ref[:, st] += _dot(u16[:, ch], g16[:, st], _TN)
            dwc_ref[st, :] += _dot(x16[:, st], dy16[:, ch], _TN)

    tmap = (lambda t: (nt - 1 - t, 0)) if rev else (lambda t: (t, 0))
    const = lambda t: (0, 0)
    row_w, row_s = pl.BlockSpec((tb, S5_W), tmap), pl.BlockSpec((tb, 2 * S5_N), tmap)
    return _pcall(
        body, name=name, grid=(nt,),
        in_specs=[row_w, row_w, row_s, pl.BlockSpec((1, S5_N), const), pl.BlockSpec((1, S5_N), const),
                  pl.BlockSpec((S5_TC, 2 * S5_N), const), pl.BlockSpec((2 * S5_N, S5_TC), const), row_w],
        out_specs=[row_w, pl.BlockSpec((S5_TC, 2 * S5_N), const), pl.BlockSpec((2 * S5_N, S5_TC), const),
                   pl.BlockSpec((SUBLANE, 2 * S5_N), const)],
        out_shape=[jax.ShapeDtypeStruct((n_tok, S5_W), F32), jax.ShapeDtypeStruct((S5_TC, 2 * S5_N), F32),
                   jax.ShapeDtypeStruct((2 * S5_N, S5_TC), F32), jax.ShapeDtypeStruct((SUBLANE, 2 * S5_N), F32)],
        scratch_shapes=_s5_scratch(tb) + [pltpu.VMEM((tb, 2 * S5_N), F32)],
        compiler_params=_cparams(("arbitrary",), 12 * tb * 2 * S5_N * 4),
    )(dy, z, xs, a_re, a_im, wb, wc, add)


GLA_TB = 512


def _gla_block_terms(la, q, k, rev):
    trif = _chunk_mask(rev).astype(F32)
    chunks = [la[t:t + GLA_C] for t in range(0, la.shape[0], GLA_C)]
    bcum = jnp.concatenate([_dot_exact(c, trif, m_first=True) for c in chunks], axis=0)
    blast = _per_chunk_sum(la)
    eb, enb, ee = jnp.exp(bcum), jnp.exp(-bcum), jnp.exp(blast - bcum)
    return trif, jnp.exp(blast), eb, enb, ee, q * eb, k * enb, k * ee


def _per_chunk_sum(t):
    return jnp.concatenate([jnp.broadcast_to(jnp.sum(t[r:r + GLA_C], axis=0, keepdims=True), (GLA_C, t.shape[1]))
                            for r in range(0, t.shape[0], GLA_C)], axis=0)


def _chunk_mask(rev):
    ri = lax.broadcasted_iota(jnp.int32, (GLA_C, GLA_C), 0)
    ci = lax.broadcasted_iota(jnp.int32, (GLA_C, GLA_C), 1)
    return (ri <= ci) if rev else (ri >= ci)


def _gla_fwd(name, z, la, *, rev):
    n_tok = z.shape[0]
    tb = _tile(n_tok, GLA_TB, GLA_C)
    nblk, per = n_tok // tb, tb // GLA_C
    scale = GLA_D ** -0.5

    def body(q_ref, k_ref, v_ref, la_ref, o_ref, s_ref, st, qd_s, kd_s, ke_s, v_s, dec_s):
        @pl.when(pl.program_id(0) == 0)
        def _():
            st[...] = jnp.zeros(st.shape, F32)

        _, dec, _, _, _, qd, kd, ke = _gla_block_terms(la_ref[...], q_ref[...].astype(F32) * scale,
                                                                 k_ref[...].astype(F32), rev)
        qd_s[...], kd_s[...], ke_s[...] = qd.astype(BF16), kd.astype(BF16), ke.astype(BF16)
        v_s[...] = v_ref[...].astype(BF16)
        dec_s[...] = dec
        tri = _chunk_mask(rev)
        for step in range(per):
            t = per - 1 - step if rev else step
            rows = pl.ds(t * GLA_C, GLA_C)
            for h in range(GLA_H):
                lanes = pl.ds(h * GLA_D, GLA_D)
                qd16, v16 = qd_s[rows, lanes], v_s[rows, lanes]
                sc = jnp.where(tri, _dot(qd16, kd_s[rows, lanes], _NT), 0.0)
                s16 = st[h].astype(BF16)
                s_ref[h, step] = s16
                o_ref[rows, lanes] = _dot(sc.astype(BF16), v16) + _dot(qd16, s16, _NT)
                st[h] = st[h] * dec_s[pl.ds(t * GLA_C, 1), lanes] + _dot(v16, ke_s[rows, lanes], _TN)

    bmap = (lambda n: nblk - 1 - n) if rev else (lambda n: n)

    def zspec(cb):
        return pl.BlockSpec((tb, GLA_W), lambda n: (bmap(n), cb))

    half = pltpu.VMEM((tb, GLA_W), BF16)
    return _pcall(
        body, name=name, grid=(nblk,),
        in_specs=[zspec(1), zspec(2), zspec(3), zspec(0)],
        out_specs=[zspec(0), pl.BlockSpec((GLA_H, per, GLA_D, GLA_D), lambda n: (0, n, 0, 0))],
        out_shape=[jax.ShapeDtypeStruct((n_tok, GLA_W), F32),
                   jax.ShapeDtypeStruct((GLA_H, n_tok // GLA_C, GLA_D, GLA_D), BF16)],
        scratch_shapes=[pltpu.VMEM((GLA_H, GLA_D, GLA_D), F32), half, half, half, half, pltpu.VMEM((tb, GLA_W), F32)],
        compiler_params=_cparams(("arbitrary",), 40 * tb * GLA_W * 4),
    )(z, z, z, la)


def _gla_bwd(name, z, la, s_prev, do, *, rev):
    n_tok = z.shape[0]
    tb = _tile(n_tok, GLA_TB, GLA_C)
    nblk, per = n_tok // tb, tb // GLA_C
    scale = GLA_D ** -0.5

    def body(q_ref, k_ref, v_ref, la_ref, s_ref, do_ref, dq_ref, dk_ref, dv_ref, dla_ref,
             dst, qd_s, kd_s, ke_s, v_s, do_s, dec_s, dqd_s, dkd_s, dke_s, dd_s):
        @pl.when(pl.program_id(0) == 0)
        def _():
            dst[...] = jnp.zeros(dst.shape, F32)

        trif, dec, eb, enb, ee, qd, kd, ke = _gla_block_terms(la_ref[...], q_ref[...].astype(F32) * scale,
                                                                 k_ref[...].astype(F32), rev)
        qd_s[...], kd_s[...], ke_s[...] = qd.astype(BF16), kd.astype(BF16), ke.astype(BF16)
        v_s[...] = v_ref[...].astype(BF16)
        do_s[...] = do_ref[...].astype(BF16)
        dec_s[...] = dec
        tri = _chunk_mask(rev)
        for step in range(per):
            t = step if rev else per - 1 - step
            rows = pl.ds(t * GLA_C, GLA_C)
            for h in range(GLA_H):
                lanes = pl.ds(h * GLA_D, GLA_D)
                qd16, kd16, ke16 = qd_s[rows, lanes], kd_s[rows, lanes], ke_s[rows, lanes]
                v16, do16 = v_s[rows, lanes], do_s[rows, lanes]
                sc16 = jnp.where(tri, _dot(qd16, kd16, _NT), 0.0).astype(BF16)
                dsc16 = jnp.where(tri, _dot(do16, v16, _NT), 0.0).astype(BF16)
                s16 = s_ref[h, per - 1 - step]
                dsn = dst[h]
                dsn16 = dsn.astype(BF16)
                dec_c = dec_s[pl.ds(t * GLA_C, 1), lanes]
                dqd_s[rows, lanes] = _dot(dsc16, kd16) + _dot(do16, s16)
                dkd_s[rows, lanes] = _dot(dsc16, qd16, _TN)
                dke_s[rows, lanes] = _dot(v16, dsn16)
                dv_ref[rows, lanes] = _dot(sc16, do16, _TN) + _dot(ke16, dsn16, _NT)
                ddec = jnp.sum(dsn * s16.astype(F32), axis=0, keepdims=True) * dec_c
                dd_s[rows, lanes] = jnp.broadcast_to(ddec, (GLA_C, GLA_D))
                dst[h] = dsn * dec_c + _dot(do16, qd16, _TN)
        dqd, dkd, dke = dqd_s[...], dkd_s[...], dke_s[...]
        dq_ref[...] = dqd * eb * scale
        dk_ref[...] = dkd * enb + dke * ee
        db = dqd * qd - dkd * kd - dke * ke
        dcum = jnp.concatenate([_dot_exact(db[r:r + GLA_C], trif, _TN, m_first=True) for r in range(0, tb, GLA_C)], axis=0)
        dla_ref[...] = dcum + _per_chunk_sum(dke * ke) + dd_s[...]

    bmap = (lambda n: n) if rev else (lambda n: nblk - 1 - n)

    def zspec(cb):
        return pl.BlockSpec((tb, GLA_W), lambda n: (bmap(n), cb))

    o4 = jax.ShapeDtypeStruct((n_tok, GLA_W), F32)
    half, full = pltpu.VMEM((tb, GLA_W), BF16), pltpu.VMEM((tb, GLA_W), F32)
    return _pcall(
        body, name=name, grid=(nblk,),
        in_specs=[zspec(1), zspec(2), zspec(3), zspec(0),
                  pl.BlockSpec((GLA_H, per, GLA_D, GLA_D), lambda n: (0, nblk - 1 - n, 0, 0)), zspec(0)],
        out_specs=[zspec(0)] * 4, out_shape=[o4] * 4,
        scratch_shapes=[pltpu.VMEM((GLA_H, GLA_D, GLA_D), F32), half, half, half, half, half, full, full, full, full, full],
        compiler_params=_cparams(("arbitrary",), 64 * tb * GLA_W * 4),
    )(z, z, z, la, s_prev, do)


_PAIRS = AT_W // LANE
_PAIRS_PER_KV = _PAIRS // AT_KH


def _attn_fwd(name, q, kd, vd, tq=256):
    n_tok = q.shape[0]
    tq = _tile(n_tok, tq, 16)

    def body(q_ref, k_ref, v_ref, o_ref, l_ref):
        qv, kv, vv = q_ref[...], k_ref[...], v_ref[...]
        low = lax.broadcasted_iota(jnp.int32, qv.shape, 1) < AT_D
        res = []
        for keep in (low, jnp.logical_not(low)):
            s = _dot(jnp.where(keep, qv, jnp.zeros_like(qv)), kv, _NT)
            m = jnp.max(s, axis=-1, keepdims=True)
            p = jnp.exp(s - m)
            den = jnp.sum(p, axis=-1, keepdims=True)
            res.append((_dot(p.astype(BF16), vv) / den, m + jnp.log(den)))
        o_ref[...] = jnp.where(low, res[0][0], res[1][0])
        l_ref[...] = jnp.where(low, res[0][1], res[1][1])

    qspec = pl.BlockSpec((tq, LANE), lambda p, i: (i, p))
    kspec = pl.BlockSpec((n_tok, LANE), lambda p, i: (0, p // _PAIRS_PER_KV))
    vm = 5 * _nbytes((tq, n_tok), F32) + 8 * _nbytes((n_tok, LANE), BF16)
    out = jax.ShapeDtypeStruct((n_tok, AT_W), F32)
    return _pcall(
        body, name=name, grid=(_PAIRS, n_tok // tq), in_specs=[qspec, kspec, kspec], out_specs=[qspec, qspec],
        out_shape=[out, out], compiler_params=_cparams(("parallel", "parallel"), vm),
    )(q, kd, vd)


def _attn_bwd(name, q, kd, vd, o, lse, do, tq=512):
    n_tok = q.shape[0]
    tq = _tile(n_tok, tq, 16)

    def body(q_ref, k_ref, v_ref, o_ref, l_ref, do_ref, dq_ref, dk_ref, dv_ref):
        @pl.when((pl.program_id(1) == 0) & (pl.program_id(2) == 0))
        def _():
            dk_ref[...] = jnp.zeros(dk_ref.shape, F32)
            dv_ref[...] = jnp.zeros(dv_ref.shape, F32)

        qv, kv, vv = q_ref[...], k_ref[...], v_ref[...]
        dof, lv = do_ref[...], l_ref[...]
        do16 = dof.astype(BF16)
        doo = dof * o_ref[...]
        low = lax.broadcasted_iota(jnp.int32, qv.shape, 1) < AT_D
        dq = jnp.zeros(qv.shape, F32)
        for keep in (low, jnp.logical_not(low)):
            qm = jnp.where(keep, qv, jnp.zeros_like(qv))
            dom = jnp.where(keep, do16, jnp.zeros_like(do16))
            lh = jnp.max(jnp.where(keep, lv, -1e30), axis=-1, keepdims=True)
            p = jnp.exp(_dot(qm, kv, _NT) - lh)
            delta = jnp.sum(jnp.where(keep, doo, 0.0), axis=-1, keepdims=True)
            ds16 = (p * (_dot(dom, vv, _NT) - delta)).astype(BF16)
            dq = jnp.where(keep, _dot(ds16, kv), dq)
            dk_ref[...] += _dot(ds16, qm, _TN)
            dv_ref[...] += _dot(p.astype(BF16), dom, _TN)
        dq_ref[...] = dq

    qspec = pl.BlockSpec((tq, LANE), lambda g, j, i: (i, g * _PAIRS_PER_KV + j))
    kspec = pl.BlockSpec((n_tok, LANE), lambda g, j, i: (0, g))
    vm = 7 * _nbytes((tq, n_tok), F32) + 12 * _nbytes((n_tok, LANE), F32)
    dup = jax.ShapeDtypeStruct((n_tok, AT_KH * LANE), F32)
    return _pcall(
        body, name=name, grid=(AT_KH, _PAIRS_PER_KV, n_tok // tq),
        in_specs=[qspec, kspec, kspec, qspec, qspec, qspec], out_specs=[qspec, kspec, kspec],
        out_shape=[jax.ShapeDtypeStruct((n_tok, AT_W), F32), dup, dup],
        compiler_params=_cparams(("parallel", "arbitrary", "arbitrary"), vm),
    )(q, kd, vd, o, lse, do)


def _s5_prep(lam_re, lam_im, log_dt, b_re, b_im, c_re, c_im):
    dt = jnp.exp(log_dt)[:, None]
    er = jnp.exp(lam_re * dt)
    a_re, a_im = er * jnp.cos(lam_im * dt), er * jnp.sin(lam_im * dt)
    nr, ni = a_re - 1.0, a_im
    den = lam_re * lam_re + lam_im * lam_im
    f_re, f_im = (nr * lam_re + ni * lam_im) / den, (ni * lam_re - nr * lam_im) / den
    bb_re = f_re[..., None] * b_re - f_im[..., None] * b_im
    bb_im = f_re[..., None] * b_im + f_im[..., None] * b_re
    def tiles(t, axis):
        _, a, b = t.shape
        eye = jnp.eye(S5_TG, dtype=t.dtype)
        blocks = t.reshape(S5_TILES, S5_TG, a, 1, b) * eye[None, :, None, :, None]
        if axis == 0:
            return blocks.reshape(S5_TILES * S5_TG * a, S5_TG * b)
        return blocks.transpose(1, 2, 0, 3, 4).reshape(S5_TG * a, S5_TILES * S5_TG * b)

    w_in = jnp.concatenate([tiles(bb_re.transpose(0, 2, 1), 1), tiles(bb_im.transpose(0, 2, 1), 1)], axis=1)
    w_out = jnp.concatenate([tiles(c_re.transpose(0, 2, 1), 0), -tiles(c_im.transpose(0, 2, 1), 0)], axis=0)
    return a_re.reshape(1, S5_N), a_im.reshape(1, S5_N), w_in, w_out


def _rope_tables(n_tok):
    quarter = AT_D // 4
    d = np.arange(AT_D)
    e = d % (AT_D // 2)
    inv = (ROPE_BASE ** (-(e % quarter).astype(np.float64) * 2.0 / (AT_D // 2))).astype(np.float32)
    sign = np.where(e < quarter, -1.0, 1.0).astype(np.float32)
    rot = np.zeros((AT_D, AT_D), np.float32)
    rot[np.where(e < quarter, d + quarter, d - quarter), d] = 1.0
    pos = jnp.arange(n_tok, dtype=jnp.int32)
    axis_pos = jnp.where(jnp.asarray(d // (AT_D // 2) == 0)[None, :], (pos // GRID_W)[:, None], (pos % GRID_W)[:, None])
    ang = axis_pos.astype(F32) * jnp.asarray(inv)[None, :]
    return jnp.cos(ang), jnp.sin(ang) * jnp.asarray(sign)[None, :], rot


def _head_consts(n_tok, heads):
    cos, sin, rot = _rope_tables(n_tok)
    eye = np.eye(heads, dtype=np.float32)
    mean = np.kron(eye, np.full((AT_D, AT_D), 1.0 / AT_D, np.float32))
    return jnp.tile(cos, (1, heads)), jnp.tile(sin, (1, heads)), jnp.asarray(mean), jnp.asarray(np.kron(eye, rot))


def _pad_w_in(wt):
    return jnp.concatenate([wt[:2560], wt[2592:3360], wt[2560:2592], jnp.zeros((96, wt.shape[1]), wt.dtype)], axis=0)


def _unpad_w_in(gt):
    return jnp.concatenate([gt[:2560], gt[3328:3360], gt[2560:3328]], axis=0)


def _ffn_fwd(x, g, wg, wu, wd, d):
    n_tok = x.shape[0]
    fq = wd.shape[-2]
    tm = _tile(n_tok, 512, 16)
    w_bytes = N_CHIPS * fq * D_MODEL * 2

    def up(x_ref, g_ref, wg_ref, wu_ref, h_ref, a_ref, b_ref, s_ref):
        hv = _rms(x_ref[...], g_ref[...]).astype(BF16)
        h_ref[...] = hv
        for j in range(N_CHIPS):
            av, bv = _dot(hv, wg_ref[j, 0], _NT), _dot(hv, wu_ref[j, 0], _NT)
            a_ref[j] = av.astype(BF16)
            b_ref[j] = bv.astype(BF16)
            s_ref[j] = (av * _sigmoid(av) * bv).astype(BF16)

    wspec = pl.BlockSpec((N_CHIPS, 1, fq, D_MODEL), lambda i: (0, d, 0, 0))
    aspec = pl.BlockSpec((N_CHIPS, tm, fq), lambda i: (0, i, 0))
    xspec = pl.BlockSpec((tm, D_MODEL), lambda i: (i, 0))
    act = jax.ShapeDtypeStruct((N_CHIPS, n_tok, fq), BF16)
    h, a, b, s = _pcall(
        up, name="ffn_up", grid=(n_tok // tm,),
        in_specs=[xspec, pl.BlockSpec((1, D_MODEL), lambda i: (0, 0)), wspec, wspec],
        out_specs=[xspec, aspec, aspec, aspec],
        out_shape=[jax.ShapeDtypeStruct((n_tok, D_MODEL), BF16), act, act, act],
        compiler_params=_cparams(("parallel",), 4 * w_bytes + 6 * N_CHIPS * tm * fq * 2 + 12 * tm * fq * 4
                                 + 8 * tm * D_MODEL * 4),
    )(x, g, wg, wu)

    def down(s_ref, w_ref, x_ref, o_ref):
        acc = _dot(s_ref[0], w_ref[0, 0])
        for j in range(1, N_CHIPS):
            acc = acc + _dot(s_ref[j], w_ref[j, 0])
        o_ref[...] = x_ref[...] + 0.5 * acc

    xo = _pcall(
        down, name="ffn_down", grid=(n_tok // tm,), in_specs=[aspec, wspec, xspec], out_specs=xspec,
        out_shape=jax.ShapeDtypeStruct((n_tok, D_MODEL), F32),
        compiler_params=_cparams(("parallel",), 2 * w_bytes + 2 * N_CHIPS * tm * fq * 2 + 8 * tm * D_MODEL * 4),
    )(s, wd, x)
    return xo, (x, h, a, b, s)


def _ffn_bwd(dy, dy16, saved, g, wg, wu, wd, d, into):
    x, h, a, b, s = saved
    n_tok = x.shape[0]
    fq = wd.shape[-2]
    tm = _tile(n_tok, 512, 16)

    ta_ = _tile(n_tok, 512, 16)

    def dact(dy_ref, w_ref, a_ref, b_ref, da_ref, db_ref):
        dyv = dy_ref[...]
        for j in range(N_CHIPS):
            ds = 0.5 * _dot(dyv, w_ref[j, 0], _NT)
            av, bv = a_ref[j].astype(F32), b_ref[j].astype(F32)
            sg = _sigmoid(av)
            da_ref[j] = (ds * bv * sg * (1.0 + av * (1.0 - sg))).astype(BF16)
            db_ref[j] = (ds * av * sg).astype(BF16)

    aspec = pl.BlockSpec((N_CHIPS, ta_, fq), lambda i: (0, i, 0))
    act = jax.ShapeDtypeStruct((N_CHIPS, n_tok, fq), BF16)
    da, db = _pcall(
        dact, name="ffn_dact", grid=(n_tok // ta_,),
        in_specs=[pl.BlockSpec((ta_, D_MODEL), lambda i: (i, 0)),
                  pl.BlockSpec((N_CHIPS, 1, fq, D_MODEL), lambda i: (0, d, 0, 0)), aspec, aspec],
        out_specs=[aspec, aspec], out_shape=[act, act],
        compiler_params=_cparams(("parallel",), 2 * N_CHIPS * fq * D_MODEL * 2 + 8 * N_CHIPS * ta_ * fq * 2
                                 + 12 * ta_ * fq * 4),
    )(dy16, wd, a, b)

    any_spec = pl.BlockSpec(memory_space=pl.ANY)
    alias = {} if into is None else {2: 0}
    tn = _tile(D_MODEL, 512, LANE)

    def dw_call(t, rhs, scale, prev):
        def dw(t_ref, r_ref, *rest):
            rest[-1][0, 0] = (scale * _dot(t_ref[0], r_ref[...], _TN)).astype(GRAD_DT)

        return _pcall(
            dw, name="ffn_dw", grid=(N_CHIPS, D_MODEL // tn),
            in_specs=[pl.BlockSpec((1, n_tok, fq), lambda j, n: (j, 0, 0)), pl.BlockSpec((n_tok, tn), lambda j, n: (0, n))]
            + ([] if prev is None else [any_spec]),
            out_specs=pl.BlockSpec((1, 1, fq, tn), lambda j, n: (j, d, 0, n)),
            out_shape=jax.ShapeDtypeStruct((N_CHIPS, DEPTH, fq, D_MODEL), GRAD_DT), input_output_aliases=alias,
            compiler_params=_cparams(("parallel", "parallel"), 6 * n_tok * (fq + tn) * 2 + 4 * fq * tn * 4),
        )(t, rhs, *([] if prev is None else [prev]))

    dwg = dw_call(da, h, 1.0, None if into is None else into[0])
    dwu = dw_call(db, h, 1.0, None if into is None else into[1])
    dwd = dw_call(s, dy16, 0.5, None if into is None else into[2])
    tr = _tile(n_tok, 256, 16)

    def dh_norm(da_ref, db_ref, wg_ref, wu_ref, dy_ref, x_ref, g_ref, dx_ref, dx16_ref, dg_ref):
        dh = _dot(da_ref[0], wg_ref[0, 0]) + _dot(db_ref[0], wu_ref[0, 0])
        for j in range(1, N_CHIPS):
            dh = dh + _dot(da_ref[j], wg_ref[j, 0]) + _dot(db_ref[j], wu_ref[j, 0])
        dx, dg = _rms_bwd(x_ref[...], g_ref[...], dh)
        dx = dy_ref[...] + dx
        dx_ref[...] = dx
        dx16_ref[...] = dx.astype(BF16)

        @pl.when(pl.program_id(0) == 0)
        def _():
            dg_ref[...] = jnp.zeros(dg_ref.shape, F32)

        dg_ref[...] += dg

    tspec = pl.BlockSpec((N_CHIPS, tr, fq), lambda i: (0, i, 0))
    wspec = pl.BlockSpec((N_CHIPS, 1, fq, D_MODEL), lambda i: (0, d, 0, 0))
    rspec = pl.BlockSpec((tr, D_MODEL), lambda i: (i, 0))
    vspec = pl.BlockSpec((1, D_MODEL), lambda i: (0, 0))
    dx, dx16, dg = _pcall(
        dh_norm, name="ffn_dh_norm", grid=(n_tok // tr,),
        in_specs=[tspec, tspec, wspec, wspec, rspec, rspec, vspec], out_specs=[rspec, rspec, vspec],
        out_shape=[jax.ShapeDtypeStruct((n_tok, D_MODEL), F32), jax.ShapeDtypeStruct((n_tok, D_MODEL), BF16),
                   jax.ShapeDtypeStruct((1, D_MODEL), F32)],
        compiler_params=_cparams(("arbitrary",), 4 * N_CHIPS * D_MODEL * fq * 2 + 4 * N_CHIPS * tr * fq * 2
                                 + 16 * tr * D_MODEL * 4),
    )(da, db, wg, wu, dy, x, g)
    return dx, dx16, dg[0], (dwg, dwu, dwd)


def _alpha_pads(w_alpha):
    z = jnp.zeros((LANE - 2 * GLA_R, GLA_W), F32)
    wf = jnp.concatenate([w_alpha[0], jnp.zeros((GLA_R, GLA_W), F32), z], axis=0)
    wb = jnp.concatenate([jnp.zeros((GLA_R, GLA_W), F32), w_alpha[1], z], axis=0)
    return wf, wb


def _dup_heads(t):
    low = lax.broadcasted_iota(jnp.int32, t.shape, 1) < AT_D
    h0, h1 = jnp.where(low, t, 0.0), jnp.where(low, 0.0, t)
    return jnp.concatenate([h0 + pltpu.roll(h0, AT_D, 1), h1 + pltpu.roll(h1, AT_D, 1)], axis=1)


def _fold_heads(t):
    low = lax.broadcasted_iota(jnp.int32, (t.shape[0], LANE), 1) < AT_D
    g0, g1 = t[:, :LANE], t[:, LANE:]
    return jnp.where(low, g0 + pltpu.roll(g0, AT_D, 1), g1 + pltpu.roll(g1, AT_D, 1))


def _mixer_fwd(x, w):
    n_tok = x.shape[0]
    wip, wmg = w['w_in_p'], w['w_merge_gate']
    tr = _tile(n_tok, 256, 16)

    def project(x_ref, g_ref, wip_ref, wmg_ref, h_ref, z_ref, gp_ref):
        hv = _rms(x_ref[...], g_ref[...]).astype(BF16)
        h_ref[...] = hv
        z_ref[...] = _dot(hv, wip_ref[...], _NT).astype(BF16)
        gp_ref[...] = _dot(hv, wmg_ref[...]).astype(BF16)

    def rows(c):
        return pl.BlockSpec((tr, c), lambda i: (i, 0))

    def whole(t):
        return pl.BlockSpec(t.shape, lambda i: (0, 0))

    h, z, gp = _pcall(
        project, name="mix_in", grid=(n_tok // tr,),
        in_specs=[rows(D_MODEL), whole(w['mix_norm']), whole(wip), whole(wmg)],
        out_specs=[rows(D_MODEL), rows(ZP_W), rows(3 * D_MODEL)],
        out_shape=[jax.ShapeDtypeStruct((n_tok, D_MODEL), BF16), jax.ShapeDtypeStruct((n_tok, ZP_W), BF16),
                   jax.ShapeDtypeStruct((n_tok, 3 * D_MODEL), BF16)],
        compiler_params=_cparams(("parallel",), 4 * (wip.size + wmg.size) + 16 * tr * (ZP_W + 3 * D_MODEL)),
    )(x, w['mix_norm'], wip, wmg)
    u = _cw(z, 0, S5_W)

    xs, yy = [], None
    for d, rev in ((0, False), (1, True)):
        a_re, a_im, wb, wc = w['s5'][d]
        x_d, yy = _s5_fwd("s5_rev" if rev else "s5_fwd", z, a_re, a_im, wb, wc, yy, rev=rev)
        xs.append(x_d)

    def s5_post(yv, uv, dv, wglu):
        y0 = yv + dv * uv
        y1 = _gelu(y0)
        t = _dot(y1.astype(BF16), wglu)
        return y0, t, y1 * _sigmoid(t)

    y0, t_glu, y_s5 = _rowwise("s5_post", s5_post, [yy, u], [w['s5_d'], w['s5_w_glu']],
                               [(S5_W, F32), (S5_W, F32), (S5_W, BF16)])

    wa_f, wa_b = _alpha_pads(w['gla_w_alpha'])
    ba_f, ba_b = w['gla_b_alpha'][0:1], w['gla_b_alpha'][1:2]
    zz = _cw(z, ZC_Z, LANE)

    def gla_prep(zv, waf, wab, bf, bb):
        z16 = zv.astype(BF16)
        lf = _dot(z16, waf.astype(BF16)) + bf
        lb = _dot(z16, wab.astype(BF16)) + bb
        return _log_sigmoid(lf) / GLA_TAU, _log_sigmoid(lb) / GLA_TAU

    la_f, la_b = _rowwise("gla_prep", gla_prep, [zz], [wa_f, wa_b, ba_f, ba_b], [(GLA_W, F32), (GLA_W, F32)])
    o_f, s_f = _gla_fwd("gla_fwd", z, la_f, rev=False)
    o_b, s_b = _gla_fwd("gla_rev", z, la_b, rev=True)
    gate = _cw(z, 4, GLA_W)

    def gla_post(of, ob, gv, gn):
        o, gv = of + ob, gv.astype(F32)
        outs = []
        for hh in range(GLA_H):
            sl = slice(hh * GLA_D, (hh + 1) * GLA_D)
            gh = gv[:, sl]
            outs.append(_rms(o[:, sl], gn) * (gh * _sigmoid(gh)))
        return jnp.concatenate(outs, axis=1)

    y_gla = _rowwise("gla_post", gla_post, [o_f, o_b, gate], [w['gla_norm']], [(GLA_W, BF16)])[0]

    cq, sq, mq, rq = _head_consts(n_tok, AT_QH)
    ck, sk, mk, rk = _head_consts(n_tok, AT_KH)
    gq = jnp.tile(w['attn_q_norm'], (1, AT_QH))
    gk = jnp.tile(w['attn_k_norm'], (1, AT_KH))

    def rope_fwd(v, gain, cos, sin, mean, rot, scale):
        r = lax.rsqrt(_dot_exact(v * v, mean, terms=2) + EPS)
        vn = v * r * gain
        return (vn * cos + _dot_exact(vn, rot, terms=2) * sin) * scale

    def attn_prep(qv, kv, vv, cqv, sqv, ckv, skv, gqv, gkv, mqv, rqv, mkv, rkv):
        qv, kv, vv = qv.astype(F32), kv.astype(F32), vv.astype(F32)
        kr = rope_fwd(kv, gkv, ckv, skv, mkv, rkv, 1.0)
        return rope_fwd(qv, gqv, cqv, sqv, mqv, rqv, AT_D ** -0.5), _dup_heads(kr), _dup_heads(vv)

    q_raw, k_raw, v_raw = _cw(z, 5, AT_W), _cw(z, ZC_AK, AT_KW), _cw(z, ZC_AV, AT_KW)
    q16, kd, vd = _rowwise("attn_prep", attn_prep, [q_raw, k_raw, v_raw, cq, sq, ck, sk], [gq, gk, mq, rq, mk, rk],
                           [(AT_W, BF16), (2 * LANE, BF16), (2 * LANE, BF16)])
    y_at, lse = _attn_fwd("attn_fwd", q16, kd, vd)

    p0 = _mm("mix_branch", y_s5, w['w_branch_s5'], out_dtype=BF16)
    p1 = _mm("mix_branch", y_gla, w['w_branch_gla'], out_dtype=BF16)
    p2 = _mm("mix_branch_f32", y_at, w['w_branch_attn'], out_dtype=BF16)

    def merge(gpv, a0, a1, a2, bias):
        g = _sigmoid(gpv.astype(F32) + bias)
        return (g[:, :D_MODEL] * a0.astype(F32) + g[:, D_MODEL:2 * D_MODEL] * a1.astype(F32)
                + g[:, 2 * D_MODEL:] * a2.astype(F32))

    merged = _rowwise("mix_merge", merge, [gp, p0, p1, p2], [w['b_merge_gate']], [(D_MODEL, BF16)])[0]
    xo = _mm("mix_out", merged, w['w_out'], add=x)
    saved = dict(x=x, h=h, z=z, gp=gp, xs=xs, y0=y0, t_glu=t_glu, y_s5=y_s5, la=(la_f, la_b), o=(o_f, o_b), s=(s_f, s_b),
                 y_gla=y_gla, q16=q16, kd=kd, vd=vd, lse=lse, y_at=y_at, p=(p0, p1, p2), merged=merged)
    return xo, saved


def _mixer_bwd(dy, dy16, sv, w, s5_raw):
    z, h = sv['z'], sv['h']
    n_tok = z.shape[0]
    grads = {}
    dmerged = _mm("mix_dmerged", dy16, w['w_out'], tb=True, out_dtype=BF16)
    grads['w_out'] = _mm("mix_dwout", sv['merged'], dy16, ta=True, out_dtype=GRAD_DT)

    def dmerge(dm, gpv, a0, a1, a2, bias):
        dm = dm.astype(F32)
        g = _sigmoid(gpv.astype(F32) + bias)
        ps = (a0.astype(F32), a1.astype(F32), a2.astype(F32))
        dps, dgs = [], []
        for k in range(3):
            gk = g[:, k * D_MODEL:(k + 1) * D_MODEL]
            dps.append(dm * gk)
            dgs.append(dm * ps[k] * gk * (1.0 - gk))
        dgp = jnp.concatenate(dgs, axis=1)
        return dps[0], dps[1], dps[2], dgp, jnp.sum(dgp, axis=0, keepdims=True)

    dp0, dp1, dp2, dgp, dbm = _rowwise("mix_dmerge", dmerge, [dmerged, sv['gp'], *sv['p']], [w['b_merge_gate']],
                                       [(D_MODEL, BF16)] * 3 + [(3 * D_MODEL, BF16)], accs=[(1, 3 * D_MODEL)])
    grads['b_merge_gate'] = dbm[0]
    grads['w_branch_s5'] = _mm("mix_dbranch", sv['y_s5'], dp0, ta=True, out_dtype=GRAD_DT)
    grads['w_branch_gla'] = _mm("mix_dbranch", sv['y_gla'], dp1, ta=True, out_dtype=GRAD_DT)
    grads['w_branch_attn'] = _mm("mix_dbranch_f32", sv['y_at'], dp2, ta=True, out_dtype=GRAD_DT)
    dy_s5 = _mm("mix_dy", dp0, w['w_branch_s5'], tb=True)
    dy_gla = _mm("mix_dy", dp1, w['w_branch_gla'], tb=True)
    dy_at = _mm("mix_dy", dp2, w['w_branch_attn'], tb=True)
    grads['w_merge_gate'] = _mm("mix_dwgate", h, dgp, ta=True, out_dtype=GRAD_DT)
    u = _cw(z, 0, S5_W)

    def s5_post_bwd(dyv, y0, t, uv, dv, wglu):
        sg = _sigmoid(t)
        y1 = _gelu(y0)
        dt = dyv * y1 * sg * (1.0 - sg)
        dy1 = dyv * sg + _dot(dt.astype(BF16), wglu, _NT)
        dy0 = dy1 * _gelu_grad(y0)
        return dy0, dt, y1, dy0 * dv, jnp.sum(dy0 * uv, axis=0, keepdims=True)

    dy0, dt_glu, y1, du, dd = _rowwise("s5_post_bwd", s5_post_bwd, [dy_s5, sv['y0'], sv['t_glu'], u],
                                       [w['s5_d'], w['s5_w_glu']],
                                       [(S5_W, BF16), (S5_W, BF16), (S5_W, BF16), (S5_W, F32)], accs=[(1, S5_W)])
    grads['s5_d'] = dd[0]
    grads['s5_w_glu'] = _mm("s5_dwglu", y1, dt_glu, ta=True, out_dtype=GRAD_DT)
    s5_cot = []
    for d, rev in ((0, False), (1, True)):
        a_re, a_im, wb, wc = w['s5'][d]
        du, dwb, dwc, da = _s5_bwd("s5_adj_fwd" if rev else "s5_adj_rev", dy0, z, sv['xs'][d], a_re, -a_im, wb, wc, du,
                                   rev=not rev)
        da = jnp.sum(da, axis=0, keepdims=True)
        s5_cot.append((da[:, :S5_N], da[:, S5_N:], dwb, dwc))
    for d in range(2):
        _, pull = jax.vjp(_s5_prep, *s5_raw[d])
        s5_cot[d] = pull(s5_cot[d])
    for k, nm in enumerate(('s5_lambda_re', 's5_lambda_im', 's5_log_dt', 's5_b_re', 's5_b_im', 's5_c_re', 's5_c_im')):
        grads[nm] = jnp.stack([s5_cot[0][k], s5_cot[1][k]])

    gate = _cw(z, 4, GLA_W)

    def gla_post_bwd(dyv, of, ob, gv, gn):
        o, gv = of + ob, gv.astype(F32)
        dos, dgates = [], []
        dgn = jnp.zeros((1, GLA_D), F32)
        for hh in range(GLA_H):
            sl = slice(hh * GLA_D, (hh + 1) * GLA_D)
            gh, dyh = gv[:, sl], dyv[:, sl]
            sg = _sigmoid(gh)
            dgates.append(dyh * _rms(o[:, sl], gn) * sg * (1.0 + gh * (1.0 - sg)))
            dx, dg = _rms_bwd(o[:, sl], gn, dyh * gh * sg)
            dos.append(dx)
            dgn = dgn + dg
        return jnp.concatenate(dos, axis=1), jnp.concatenate(dgates, axis=1), dgn

    do, dgate, dgn = _rowwise("gla_post_bwd", gla_post_bwd, [dy_gla, *sv['o'], gate], [w['gla_norm']],
                              [(GLA_W, F32), (GLA_W, F32)], accs=[(1, GLA_D)])
    grads['gla_norm'] = dgn[0]
    dq_f, dk_f, dv_f, dla_f = _gla_bwd("gla_bwd_fwd", z, sv['la'][0], sv['s'][0], do, rev=False)
    dq_b, dk_b, dv_b, dla_b = _gla_bwd("gla_bwd_rev", z, sv['la'][1], sv['s'][1], do, rev=True)
    wa_f, wa_b = _alpha_pads(w['gla_w_alpha'])
    ba_f, ba_b = w['gla_b_alpha'][0:1], w['gla_b_alpha'][1:2]
    zz = _cw(z, ZC_Z, LANE)

    def gla_prep_bwd(dlf, dlb, zv, waf, wab, bf, bb):
        z16 = zv.astype(BF16)
        waf16, wab16 = waf.astype(BF16), wab.astype(BF16)
        xf = _dot(z16, waf16) + bf
        xb = _dot(z16, wab16) + bb
        df = dlf * (_sigmoid(-xf) / GLA_TAU)
        db = dlb * (_sigmoid(-xb) / GLA_TAU)
        dz = _dot(df.astype(BF16), waf16, _NT) + _dot(db.astype(BF16), wab16, _NT)
        return df, db, dz, jnp.sum(df, axis=0, keepdims=True), jnp.sum(db, axis=0, keepdims=True)

    dlog_f, dlog_b, dzz, dba_f, dba_b = _rowwise(
        "gla_prep_bwd", gla_prep_bwd, [dla_f, dla_b, zz], [wa_f, wa_b, ba_f, ba_b],
        [(GLA_W, BF16), (GLA_W, BF16), (LANE, F32)], accs=[(1, GLA_W), (1, GLA_W)])
    dwa_f = _mm("gla_dwa", zz, dlog_f, ta=True)
    dwa_b = _mm("gla_dwa", zz, dlog_b, ta=True)
    grads['gla_w_alpha'] = jnp.stack([dwa_f[:GLA_R], dwa_b[GLA_R:2 * GLA_R]])
    grads['gla_b_alpha'] = jnp.concatenate([dba_f, dba_b], axis=0)

    dq_r, dkd, dvd = _attn_bwd("attn_bwd", sv['q16'], sv['kd'], sv['vd'], sv['y_at'], sv['lse'], dy_at)
    cq, sq, mq, rq = _head_consts(n_tok, AT_QH)
    ck, sk, mk, rk = _head_consts(n_tok, AT_KH)
    gq = jnp.tile(w['attn_q_norm'], (1, AT_QH))
    gk = jnp.tile(w['attn_k_norm'], (1, AT_KH))

    def rope_bwd(dr, v, gain, cos, sin, mean, rot, scale):
        dr = dr * scale
        dvn = dr * cos + _dot_exact(dr * sin, rot, _NT, terms=2)
        r = lax.rsqrt(_dot_exact(v * v, mean, terms=2) + EPS)
        vh = v * r
        dg = jnp.sum(dvn * vh, axis=0, keepdims=True)
        dvh = dvn * gain
        return r * (dvh - vh * _dot_exact(dvh * vh, mean, terms=2)), dg

    def attn_prep_bwd(dq, dk2, dv2, qv, kv, cqv, sqv, ckv, skv, gqv, gkv, mqv, rqv, mkv, rkv):
        qv, kv = qv.astype(F32), kv.astype(F32)
        dqo, dgq = rope_bwd(dq, qv, gqv, cqv, sqv, mqv, rqv, AT_D ** -0.5)
        dko, dgk = rope_bwd(_fold_heads(dk2), kv, gkv, ckv, skv, mkv, rkv, 1.0)
        return dqo, dko, _fold_heads(dv2), dgq, dgk

    q_raw, k_raw = _cw(z, 5, AT_W), _cw(z, ZC_AK, AT_KW)
    daq, dak, dav, dgq, dgk = _rowwise("attn_prep_bwd", attn_prep_bwd, [dq_r, dkd, dvd, q_raw, k_raw, cq, sq, ck, sk],
                                       [gq, gk, mq, rq, mk, rk], [(AT_W, F32), (AT_KW, F32), (AT_KW, F32)],
                                       accs=[(1, AT_W), (1, AT_KW)])
    grads['attn_q_norm'] = dgq.reshape(AT_QH, AT_D).sum(axis=0)
    grads['attn_k_norm'] = dgk.reshape(AT_KH, AT_D).sum(axis=0)

    def assemble(duv, qf, qb, kf, kb, vf, vb, dg, aq, ak, av, zzv):
        return jnp.concatenate([duv, qf + qb, kf + kb, vf + vb, dg, aq, ak, av, zzv], axis=1)

    dz = _rowwise("mix_dz", assemble, [du, dq_f, dq_b, dk_f, dk_b, dv_f, dv_b, dgate, daq, dak, dav, dzz], [],
                  [(ZP_W, BF16)])[0]
    grads['w_in'] = _unpad_w_in(_mm("mix_dwin", dz, h, ta=True, out_dtype=GRAD_DT))
    wmg, wip = w['w_merge_gate'], w['w_in_p']
    tr = _tile(n_tok, 256, 16)

    def dh_norm(dgp_ref, dz_ref, wmg_ref, wip_ref, dy_ref, x_ref, g_ref, dx_ref, dx16_ref, dg_ref):
        dh = _dot(dgp_ref[...], wmg_ref[...], _NT) + _dot(dz_ref[...], wip_ref[...])
        dxv, dgv = _rms_bwd(x_ref[...], g_ref[...], dh)
        dxv = dy_ref[...] + dxv
        dx_ref[...] = dxv
        dx16_ref[...] = dxv.astype(BF16)

        @pl.when(pl.program_id(0) == 0)
        def _():
            dg_ref[...] = jnp.zeros(dg_ref.shape, F32)

        dg_ref[...] += dgv

    def rows(c):
        return pl.BlockSpec((tr, c), lambda i: (i, 0))

    def whole(t):
        return pl.BlockSpec(t.shape, lambda i: (0, 0))

    dx, dx16, dg = _pcall(
        dh_norm, name="mix_dh_norm", grid=(n_tok // tr,),
        in_specs=[rows(dgp.shape[1]), rows(dz.shape[1]), whole(wmg), whole(wip), rows(D_MODEL), rows(D_MODEL),
                  whole(w['mix_norm'])],
        out_specs=[rows(D_MODEL), rows(D_MODEL), whole(w['mix_norm'])],
        out_shape=[jax.ShapeDtypeStruct((n_tok, D_MODEL), F32), jax.ShapeDtypeStruct((n_tok, D_MODEL), BF16),
                   jax.ShapeDtypeStruct((1, D_MODEL), F32)],
        compiler_params=_cparams(("arbitrary",), 4 * (wmg.size + wip.size) + 8 * tr * (dgp.shape[1] + dz.shape[1])
                                 + 16 * tr * D_MODEL * 4),
    )(dgp, dz, wmg, wip, dy, sv['x'], w['mix_norm'])
    grads['mix_norm'] = dg[0]
    return dx, dx16, grads


FFN_NAMES = [p + k for p in ('ffn1_', 'ffn2_') for k in ('w_gate', 'w_up', 'w_down')]
MISC_SHARDED = [n for n in SHARDED if n not in FFN_NAMES]
LARGE_SHARDED = FFN_NAMES + ['w_in', 'w_merge_gate', 'w_out', 'w_branch_s5', 'w_branch_gla', 'w_branch_attn', 's5_w_glu']
KEPT_TRANSPOSED = [p + k for p in ('ffn1_', 'ffn2_') for k in ('w_gate', 'w_up')] + ['w_in']


def _stored_axis(n):
    return 1 if n in KEPT_TRANSPOSED else SHARD_AXIS[n]


def _stored(n, t):
    return jnp.swapaxes(t, 1, 2) if n in KEPT_TRANSPOSED else t


def _layer_weights(ws, rep, i):
    w = {k: _from_shards(ws[k][:, i], _stored_axis(k) - 1) for k in MISC_SHARDED}
    for k in ('ffn1_norm', 'mix_norm', 'ffn2_norm', 's5_d', 'gla_norm', 'attn_q_norm', 'attn_k_norm', 'b_merge_gate'):
        w[k] = rep[k][i].reshape(1, -1)
    w['w_in_p'] = _pad_w_in(w['w_in'])
    s5_raw = [tuple(rep[k][i, d] for k in ('s5_lambda_re', 's5_lambda_im', 's5_log_dt', 's5_b_re', 's5_b_im',
                                            's5_c_re', 's5_c_im')) for d in range(2)]
    w['s5'] = [_s5_prep(*s5_raw[d]) for d in range(2)]
    return w, s5_raw


def _device_step(x, target, ws, rep):
    lw = [_layer_weights(ws, rep, i) for i in range(DEPTH)]
    ffn = {p: (ws[p + 'w_gate'], ws[p + 'w_up'], ws[p + 'w_down']) for p in ('ffn1_', 'ffn2_')}
    saved = []
    for i in range(DEPTH):
        w, _ = lw[i]
        x, s1 = _ffn_fwd(x, w['ffn1_norm'], *ffn['ffn1_'], i)
        x, s2 = _mixer_fwd(x, w)
        x, s3 = _ffn_fwd(x, w['ffn2_norm'], *ffn['ffn2_'], i)
        saved.append((s1, s2, s3))
    gfin = rep['final_norm'].reshape(1, -1)

    def head(xv, tv, gv):
        e = _rms(xv, gv) - tv
        dx, dg = _rms_bwd(xv, gv, e * (1.0 / D_MODEL))
        part = jnp.sum(e * e, axis=0, keepdims=True)
        return dx, dx, dg, part

    dx, dx16, dgfin, part = _rowwise("loss_head", head, [x, target], [gfin], [(D_MODEL, F32), (D_MODEL, BF16)],
                                     accs=[(1, D_MODEL), (1, D_MODEL)])
    loss = (0.5 / D_MODEL) * jnp.sum(part)
    per_layer = []
    dffn = {'ffn1_': None, 'ffn2_': None}
    for i in reversed(range(DEPTH)):
        w, s5_raw = lw[i]
        s1, s2, s3 = saved[i]
        dx, dx16, n3, dffn['ffn2_'] = _ffn_bwd(dx, dx16, s3, w['ffn2_norm'], *ffn['ffn2_'], i, dffn['ffn2_'])
        dx, dx16, g = _mixer_bwd(dx, dx16, s2, w, s5_raw)
        dx, dx16, n1, dffn['ffn1_'] = _ffn_bwd(dx, dx16, s1, w['ffn1_norm'], *ffn['ffn1_'], i, dffn['ffn1_'])
        g['ffn2_norm'], g['ffn1_norm'] = n3, n1
        per_layer.append(g)
    per_layer.reverse()
    stacked = {k: jnp.stack([per_layer[i][k] for i in range(DEPTH)]) for k in per_layer[0]}
    gs = {k: _to_shards(stacked[k], _stored_axis(k)) for k in MISC_SHARDED}
    for p in ('ffn1_', 'ffn2_'):
        gs[p + 'w_gate'], gs[p + 'w_up'], gs[p + 'w_down'] = dffn[p]
    grep = {k: stacked[k] for k in REPLICATED if k != 'final_norm'}
    grep['final_norm'] = dgfin[0]
    return loss, dx, gs, grep


PACK_C = 1024
ROW_TILE = 512


def _rows_of(shape):
    return -(-int(np.prod(shape)) // PACK_C)


def _as_rows(t, lead=0):
    head = t.shape[:lead]
    flat = t.reshape(head + (-1,))
    rows = -(-flat.shape[-1] // PACK_C)
    pad = rows * PACK_C - flat.shape[-1]
    if pad:
        flat = jnp.pad(flat, [(0, 0)] * lead + [(0, pad)])
    return flat.reshape(head + (rows, PACK_C))


def _pack(items, total_rows, lead=0):
    parts = [_as_rows(t, lead) for t in items]
    used = sum(p.shape[lead] for p in parts)
    if total_rows > used:
        parts.append(jnp.zeros(parts[0].shape[:lead] + (total_rows - used, PACK_C), parts[0].dtype))
    return jnp.concatenate(parts, axis=lead)


def _unpack(rows, shapes, lead=0):
    out, r0 = [], 0
    head = rows.shape[:lead]
    for shp in shapes:
        n, size = _rows_of(shp), int(np.prod(shp))
        piece = lax.slice_in_dim(rows, r0, r0 + n, axis=lead).reshape(head + (n * PACK_C,))
        out.append(lax.slice_in_dim(piece, 0, size, axis=lead).reshape(head + tuple(shp)))
        r0 += n
    return out


def _round_up(n, m):
    return -(-n // m) * m


def _to_shards(g, axis):
    shp = g.shape
    g = g.reshape(shp[:axis] + (N_CHIPS, shp[axis] // N_CHIPS) + shp[axis + 1:])
    return jnp.moveaxis(g, axis, 0)


def _from_shards(s, axis):
    s = jnp.moveaxis(s, 0, axis)
    shp = s.shape
    return s.reshape(shp[:axis] + (shp[axis] * shp[axis + 1],) + shp[axis + 2:])


_HBM = pl.BlockSpec(memory_space=pltpu.HBM)


def _place():
    return lax.axis_index("x"), lax.axis_index("y"), lax.axis_index("c")


def _rcopy(src, dst, send, recv, k, dev):
    return pltpu.make_async_remote_copy(src_ref=src, dst_ref=dst, send_sem=send.at[k], recv_sem=recv.at[k],
                                        device_id=dev, device_id_type=MESH)


def _comm_call(name, body, arrays, out_shapes, n_sems, n_local):
    return pl.pallas_call(
        body, name=name, in_specs=[_HBM] * len(arrays), out_specs=[_HBM] * len(out_shapes), out_shape=out_shapes,
        scratch_shapes=[pltpu.SemaphoreType.DMA((n_sems,)), pltpu.SemaphoreType.DMA((n_sems,)),
                        pltpu.SemaphoreType.DMA((max(n_local, 1),))],
    )(*arrays)


def _ici_peers(x, y, c):
    x_first = c == 0
    first = (jnp.where(x_first, 1 - x, x), jnp.where(x_first, y, 1 - y))
    second = (jnp.where(x_first, x, 1 - x), jnp.where(x_first, 1 - y, y))
    return first, second


def _ag_chips(name, packs):
    n = len(packs)

    def body(*refs):
        p, o, (send, recv, _) = refs[:n], refs[n:2 * n], refs[2 * n:]
        x, y, c = _place()
        (ax, ay), (bx, by) = _ici_peers(x, y, c)
        me, na, nb, nd = 2 * x + y, 2 * ax + ay, 2 * bx + by, 2 * (1 - x) + (1 - y)
        sib = (x, y, 1 - c)
        own = [_rcopy(p[a].at[c], o[a].at[me, c], send, recv, 6 * a, (ax, ay, c)) for a in range(n)]
        own += [_rcopy(p[a].at[c], o[a].at[me, c], send, recv, 6 * a + 1, (bx, by, c)) for a in range(n)]
        for cp in own:
            cp.start()
        passed = []

        def landed(a, chip, k, fwd):
            blk = o[a].at[chip, c]
            _rcopy(blk, blk, send, recv, 6 * a + k, sib).wait_recv()
            for sem, dev in fwd:
                cp = _rcopy(blk, blk, send, recv, 6 * a + sem, dev)
                cp.start()
                passed.append(cp)

        for a in range(n):
            landed(a, na, 0, [(2, (bx, by, c)), (3, sib)])
        for a in range(n):
            landed(a, nb, 1, [(4, sib)])
        for a in range(n):
            landed(a, nd, 2, [(5, sib)])
        for a in range(n):
            for k, chip in ((3, nb), (4, na), (5, nd)):
                blk = o[a].at[chip, 1 - c]
                _rcopy(blk, blk, send, recv, 6 * a + k, sib).wait_recv()
        for cp in own + passed:
            cp.wait_send()

    outs = _comm_call(name, body, packs, [jax.ShapeDtypeStruct((N_CHIPS,) + t.shape, t.dtype) for t in packs], 6 * n, 0)
    chip = 2 * lax.axis_index("x") + lax.axis_index("y")
    return [lax.dynamic_update_slice_in_dim(o, t[None], chip, axis=0) for o, t in zip(outs, packs)]


def _pair_swap(name, gs):
    n = len(gs)
    ns = gs[0].shape[0]

    def body(*refs):
        g, r, (send, recv, _) = refs[:n], refs[n:2 * n], refs[2 * n:]
        x, y, c = _place()
        cps = [_rcopy(g[a].at[j, 1 - c], r[a].at[j], send, recv, ns * a + j, (x, y, 1 - c))
               for a in range(n) for j in range(ns)]
        for cp in cps:
            cp.start()
        for cp in cps:
            cp.wait_recv()
        for cp in cps:
            cp.wait_send()

    return _comm_call(name, body, gs, [jax.ShapeDtypeStruct((ns,) + t.shape[2:], t.dtype) for t in gs], ns * n, 0)


def _halve_first(name, parts):
    n = len(parts)

    def body(*refs):
        p, r, (send, recv, _) = refs[:n], refs[n:2 * n], refs[2 * n:]
        x, y, c = _place()
        (ax, ay), _ = _ici_peers(x, y, c)
        cps = [_rcopy(p[a].at[jnp.where(c == 0, 2 * ax + k, 2 * k + ay)], r[a].at[k], send, recv, 2 * a + k, (ax, ay, c))
               for a in range(n) for k in range(2)]
        for cp in cps:
            cp.start()
        for cp in cps:
            cp.wait_recv()
        for cp in cps:
            cp.wait_send()

    return _comm_call(name, body, parts, [jax.ShapeDtypeStruct((2,) + t.shape[1:], t.dtype) for t in parts], 2 * n, 0)


def _halve_second(name, kept):
    n = len(kept)

    def body(*refs):
        p, r, (send, recv, _) = refs[:n], refs[n:2 * n], refs[2 * n:]
        x, y, c = _place()
        _, (bx, by) = _ici_peers(x, y, c)
        cps = [_rcopy(p[a].at[jnp.where(c == 0, by, bx)], r[a], send, recv, a, (bx, by, c)) for a in range(n)]
        for cp in cps:
            cp.start()
        for cp in cps:
            cp.wait_recv()
        for cp in cps:
            cp.wait_send()

    return _comm_call(name, body, kept, [jax.ShapeDtypeStruct(t.shape[1:], t.dtype) for t in kept], n, 0)


def _add_selected(name, t, r, index, out_dtype):
    single = r.ndim == 2
    r3 = r[None] if single else r
    nk, rh, cc = r3.shape
    tl = _tile(rh, ROW_TILE, 16)

    def body(i_ref, t_ref, r_ref, o_ref):
        o_ref[...] = (t_ref[...].astype(F32) + r_ref[...].astype(F32)).astype(o_ref.dtype)

    out = _pcall(
        body, name=name,
        grid_spec=pltpu.PrefetchScalarGridSpec(
            num_scalar_prefetch=1, grid=(nk, rh // tl),
            in_specs=[pl.BlockSpec((1, tl, cc), lambda k, i, i_ref: (i_ref[k], i, 0)),
                      pl.BlockSpec((1, tl, cc), lambda k, i, i_ref: (k, i, 0))],
            out_specs=pl.BlockSpec((1, tl, cc), lambda k, i, i_ref: (k, i, 0))),
        out_shape=jax.ShapeDtypeStruct(r3.shape, out_dtype),
        compiler_params=_cparams(("parallel", "parallel"), 8 * tl * cc * 4),
    )(index, t, r3)
    return out[0] if single else out


def _pair_share(name, reds):
    n = len(reds)

    def body(*refs):
        r, o, (send, recv, _) = refs[:n], refs[n:2 * n], refs[2 * n:]
        x, y, c = _place()
        cps = [_rcopy(r[a], o[a].at[c], send, recv, a, (x, y, 1 - c)) for a in range(n)]
        for cp in cps:
            cp.start()
        for a in range(n):
            blk = o[a].at[1 - c]
            _rcopy(blk, blk, send, recv, a, (x, y, 1 - c)).wait_recv()
        for cp in cps:
            cp.wait_send()

    outs = _comm_call(name, body, reds, [jax.ShapeDtypeStruct((2,) + t.shape, t.dtype) for t in reds], n, 0)
    core = lax.axis_index("c")
    return [lax.dynamic_update_slice_in_dim(o, t[None], core, axis=0) for o, t in zip(outs, reds)]


def _add_own_half(name, g, r):
    ns, _, rh, cc = g.shape
    tl = _tile(rh, ROW_TILE, 16)
    core = lax.axis_index("c").astype(jnp.int32).reshape(1)

    def body(c_ref, g_ref, r_ref, o_ref):
        o_ref[...] = (g_ref[0].astype(F32) + r_ref[...].astype(F32)).astype(o_ref.dtype)

    return _pcall(
        body, name=name,
        grid_spec=pltpu.PrefetchScalarGridSpec(
            num_scalar_prefetch=1, grid=(ns, rh // tl),
            in_specs=[pl.BlockSpec((1, 1, tl, cc), lambda j, i, c_ref: (j, c_ref[0], i, 0)),
                      pl.BlockSpec((1, tl, cc), lambda j, i, c_ref: (j, i, 0))],
            out_specs=pl.BlockSpec((1, tl, cc), lambda j, i, c_ref: (j, i, 0))),
        out_shape=jax.ShapeDtypeStruct(r.shape, BF16),
        compiler_params=_cparams(("parallel", "parallel"), 8 * tl * cc * 4),
    )(core, g, r)


def _reduce_scatter(gs):
    x, y, c = lax.axis_index("x"), lax.axis_index("y"), lax.axis_index("c")
    mine_first = jnp.stack([jnp.where(c == 0, 2 * x + k, 2 * k + y) for k in range(2)]).astype(jnp.int32)
    mine_second = jnp.where(c == 0, y, x).astype(jnp.int32).reshape(1)
    swapped = _pair_swap("rs_pair_swap", gs)
    parts = [_add_own_half("rs_add_pair", g, r) for g, r in zip(gs, swapped)]
    got = _halve_first("rs_halve_first", parts)
    kept = [_add_selected("rs_add_first", t, r, mine_first, BF16) for t, r in zip(parts, got)]
    got = _halve_second("rs_halve_second", kept)
    reds = [_add_selected("rs_add_second", t, r, mine_second, F32) for t, r in zip(kept, got)]
    return _pair_share("rs_pair_share", reds)


def _adamw(name, g, w, m, v):
    c1 = 1.0 - ADAM_B1 ** ADAM_STEP
    c2 = 1.0 - ADAM_B2 ** ADAM_STEP
    nb, rows, cc = g.shape
    tl = _tile(rows, ROW_TILE, SUBLANE)

    def body(g_ref, w_ref, m_ref, v_ref, d_ref, mo_ref, vo_ref):
        gv = g_ref[...]
        mn = ADAM_B1 * m_ref[...] + (1.0 - ADAM_B1) * gv
        vn = ADAM_B2 * v_ref[...] + (1.0 - ADAM_B2) * (gv * gv)
        d_ref[...] = -ADAM_LR * ((mn / c1) / (jnp.sqrt(vn / c2) + ADAM_EPS) + ADAM_WD * w_ref[...])
        mo_ref[...] = mn
        vo_ref[...] = vn

    spec = pl.BlockSpec((1, tl, cc), lambda b, i: (b, i, 0))
    return _pcall(
        body, name=name, grid=(nb, rows // tl), in_specs=[spec] * 4, out_specs=[spec] * 3,
        out_shape=[jax.ShapeDtypeStruct(g.shape, F32)] * 3,
        compiler_params=_cparams(("parallel", "parallel"), 16 * tl * cc * 4),
    )(g, w, m, v)


def _train_step(a):
    x, target = a['x'][0], a['loss_target'][0]

    names = BF16_GATHER + F32_GATHER
    got = _ag_chips("ag_weights", [_stored(n, a[n]).astype(BF16) for n in BF16_GATHER] + [a[n] for n in F32_GATHER])
    ws = dict(zip(names, got))

    loss, gx, gs, grep = _device_step(x, target, ws, {n: a[n] for n in REPLICATED})
    loss = lax.psum(loss, ("x", "y", "c"))

    small = [n for n in MISC_SHARDED if n not in LARGE_SHARDED]
    small_shapes = [a[n].shape[1:] for n in small]
    rep_shapes = [a[n].shape for n in REPLICATED]
    small_rows = sum(_rows_of(s) for s in small_shapes)
    piece = _round_up(-(-sum(_rows_of(s) for s in rep_shapes) // (2 * N_CHIPS)), SUBLANE)
    rep_rows = 2 * N_CHIPS * piece
    rep_pack = _pack([grep[n] for n in REPLICATED], rep_rows).reshape(N_CHIPS, 2, piece, PACK_C)
    total = _round_up(small_rows + piece, ROW_TILE // 2)
    misc = _pack([gs[n] for n in small] + [rep_pack], total, lead=2)
    reds = _reduce_scatter([gs[n] for n in LARGE_SHARDED] + [misc])
    grad = dict(zip(LARGE_SHARDED, reds[:-1]))
    grad.update(zip(small, _unpack(reds[-1], small_shapes, lead=1)))
    rep_mine = reds[-1][:, small_rows:small_rows + piece]
    rep_all = _ag_chips("ag_replicated", [rep_mine])[0].reshape(1, rep_rows, PACK_C)

    outs = {}
    for n in SHARDED:
        shp = grad[n].shape
        three_d = (shp[0], -1, shp[-1])
        params = (grad[n], *(_stored(n, a[p + n]) for p in ('', 'm_', 'v_')))
        res3 = _adamw("adamw", *(t.reshape(three_d) for t in params))
        for kind, arr in zip(('grad', 'delta', 'new_m', 'new_v'), (grad[n], *res3)):
            outs[kind + '_' + n] = _stored(n, arr.reshape(shp))
    w, m, v = (_pack([a[p + n] for n in REPLICATED], rep_rows)[None] for p in ('', 'm_', 'v_'))
    res3 = _adamw("adamw_replicated", rep_all, w, m, v)
    for kind, arr in zip(('grad', 'delta', 'new_m', 'new_v'), (rep_all, *res3)):
        for n, t in zip(REPLICATED, _unpack(arr[0], rep_shapes)):
            outs[kind + '_' + n] = t
    res = [loss, gx[None]]
    for kind in ('grad', 'delta', 'new_m', 'new_v'):
        res += [outs[kind + '_' + n] for n in W_NAMES]
    return tuple(res)


def kernel(x, ffn1_norm, ffn1_w_gate, ffn1_w_up, ffn1_w_down, mix_norm, w_in, s5_lambda_re, s5_lambda_im, s5_log_dt, s5_b_re, s5_b_im, s5_c_re, s5_c_im, s5_d, s5_w_glu, gla_w_alpha, gla_b_alpha, gla_norm, attn_q_norm, attn_k_norm, w_branch_s5, w_branch_gla, w_branch_attn, w_merge_gate, b_merge_gate, w_out, ffn2_norm, ffn2_w_gate, ffn2_w_up, ffn2_w_down, final_norm, loss_target, m_ffn1_norm, m_ffn1_w_gate, m_ffn1_w_up, m_ffn1_w_down, m_mix_norm, m_w_in, m_s5_lambda_re, m_s5_lambda_im, m_s5_log_dt, m_s5_b_re, m_s5_b_im, m_s5_c_re, m_s5_c_im, m_s5_d, m_s5_w_glu, m_gla_w_alpha, m_gla_b_alpha, m_gla_norm, m_attn_q_norm, m_attn_k_norm, m_w_branch_s5, m_w_branch_gla, m_w_branch_attn, m_w_merge_gate, m_b_merge_gate, m_w_out, m_ffn2_norm, m_ffn2_w_gate, m_ffn2_w_up, m_ffn2_w_down, m_final_norm, v_ffn1_norm, v_ffn1_w_gate, v_ffn1_w_up, v_ffn1_w_down, v_mix_norm, v_w_in, v_s5_lambda_re, v_s5_lambda_im, v_s5_log_dt, v_s5_b_re, v_s5_b_im, v_s5_c_re, v_s5_c_im, v_s5_d, v_s5_w_glu, v_gla_w_alpha, v_gla_b_alpha, v_gla_norm, v_attn_q_norm, v_attn_k_norm, v_w_branch_s5, v_w_branch_gla, v_w_branch_attn, v_w_merge_gate, v_b_merge_gate, v_w_out, v_ffn2_norm, v_ffn2_w_gate, v_ffn2_w_up, v_ffn2_w_down, v_final_norm):
    return _train_step(dict(locals()))
```

```python
import functools
import math

import jax
import jax.numpy as jnp
import numpy as np
from jax import lax
from jax.experimental import pallas as pl
from jax.experimental.pallas import tpu as pltpu

F32 = jnp.float32
BF16 = jnp.bfloat16

D_MODEL = 1024
DEPTH = 2
EPS = 1e-6
S5_G, S5_H, S5_P = 32, 16, 64
S5_W = S5_G * S5_H
S5_N = S5_G * S5_P
S5_TG = 8
S5_TILES = S5_G // S5_TG
S5_TC, S5_TS = S5_TG * S5_H, S5_TG * S5_P
GLA_H, GLA_D = 4, 128
GLA_W = GLA_H * GLA_D
GLA_R = 16
GLA_TAU = 16.0
GLA_C = 64
AT_QH, AT_KH, AT_D = 8, 2, 64
AT_W = AT_QH * AT_D
AT_KW = AT_KH * AT_D
GRID_W = 64
ROPE_BASE = 10000.0
ZP_W = 3456
ZC_AK, ZC_AV, ZC_Z = 24, 25, 26

ADAM_LR, ADAM_B1, ADAM_B2, ADAM_EPS, ADAM_WD, ADAM_STEP = 0.001, 0.9, 0.999, 1e-08, 0.01, 10

GRAD_DT = BF16
VMEM_LIMIT_V7X = 56 * 1024 * 1024
LANE = 128
SUBLANE = 8

W_NAMES = ['ffn1_norm', 'ffn1_w_gate', 'ffn1_w_up', 'ffn1_w_down', 'mix_norm', 'w_in', 's5_lambda_re', 's5_lambda_im',
           's5_log_dt', 's5_b_re', 's5_b_im', 's5_c_re', 's5_c_im', 's5_d', 's5_w_glu', 'gla_w_alpha', 'gla_b_alpha',
           'gla_norm', 'attn_q_norm', 'attn_k_norm', 'w_branch_s5', 'w_branch_gla', 'w_branch_attn', 'w_merge_gate',
           'b_merge_gate', 'w_out', 'ffn2_norm', 'ffn2_w_gate', 'ffn2_w_up', 'ffn2_w_down', 'final_norm']
SHARD_AXIS = {'ffn1_w_gate': 2, 'ffn1_w_up': 2, 'ffn1_w_down': 1, 'w_in': 2, 's5_w_glu': 1, 'gla_w_alpha': 3,
              'gla_b_alpha': 2, 'w_branch_s5': 2, 'w_branch_gla': 2, 'w_branch_attn': 2, 'w_merge_gate': 2,
              'w_out': 1, 'ffn2_w_gate': 2, 'ffn2_w_up': 2, 'ffn2_w_down': 1}
F32_GATHER = ['gla_w_alpha', 'gla_b_alpha']
BF16_GATHER = [n for n in W_NAMES if n in SHARD_AXIS and n not in F32_GATHER]
SHARDED = [n for n in W_NAMES if n in SHARD_AXIS]
REPLICATED = [n for n in W_NAMES if n not in SHARD_AXIS]
N_CHIPS = 4
MESH = pl.DeviceIdType.MESH


def _tile(n, target, mult):
    best = None
    for t in range(mult, min(n, target) + 1, mult):
        if n % t == 0:
            best = t
    return best if best is not None else n


def _cparams(sem, vmem_bytes):
    limit = int(min(VMEM_LIMIT_V7X, max(32 * 1024 * 1024, vmem_bytes * 5 // 4)))
    return pltpu.CompilerParams(dimension_semantics=sem, vmem_limit_bytes=limit)


def _pcall(body, **kw):
    shapes = kw.pop('out_shape')
    many = isinstance(shapes, (list, tuple))
    pinned = [pltpu.HBM(t.shape, t.dtype) for t in (shapes if many else [shapes])]
    call = pl.pallas_call(body, out_shape=pinned if many else pinned[0], **kw)

    def run(*ops):
        return call(*[o if jnp.issubdtype(o.dtype, jnp.integer) else pltpu.with_memory_space_constraint(o, pltpu.HBM)
                      for o in ops])

    return run


def _cw(arr, cb, width):
    return (arr, cb, width)


def _win(a):
    if isinstance(a, tuple):
        return a
    return (a, 0, a.shape[-1])


def _nbytes(shape, dtype):
    return int(np.prod(shape)) * jnp.dtype(dtype).itemsize


def _rowwise(name, fn, rows, fulls, outs, accs=(), tl=256):
    rows = [_win(r) for r in rows]
    n_tok = rows[0][0].shape[0]
    tl = _tile(n_tok, tl, 16)
    nr, nf, no = len(rows), len(fulls), len(outs)

    def body(*refs):
        vals = [r[...] for r in refs[:nr + nf]]
        res = fn(*vals)
        if not isinstance(res, (tuple, list)):
            res = (res,)
        for k in range(no):
            o = refs[nr + nf + k]
            o[...] = res[k].astype(o.dtype)
        if accs:
            @pl.when(pl.program_id(0) == 0)
            def _():
                for k in range(len(accs)):
                    a = refs[nr + nf + no + k]
                    a[...] = jnp.zeros(a.shape, a.dtype)
            for k in range(len(accs)):
                a = refs[nr + nf + no + k]
                a[...] += res[no + k]

    in_specs = [pl.BlockSpec((tl, w), functools.partial(lambda i, cb: (i, cb), cb=cb)) for (_, cb, w) in rows]
    in_specs += [pl.BlockSpec(f.shape, functools.partial(lambda i, nd: (0,) * nd, nd=f.ndim)) for f in fulls]
    out_specs = [pl.BlockSpec((tl, c), lambda i: (i, 0)) for (c, _) in outs]
    out_specs += [pl.BlockSpec((r, c), lambda i: (0, 0)) for (r, c) in accs]
    out_shape = [jax.ShapeDtypeStruct((n_tok, c), dt) for (c, dt) in outs]
    out_shape += [jax.ShapeDtypeStruct((r, c), F32) for (r, c) in accs]
    vm = 2 * sum(_nbytes((tl, w), a.dtype) for (a, _, w) in rows)
    vm += 2 * sum(_nbytes(f.shape, f.dtype) for f in fulls)
    vm += 2 * sum(_nbytes((tl, c), dt) for (c, dt) in outs)
    vm += 8 * max([_nbytes((tl, w), F32) for (_, _, w) in rows] + [_nbytes((tl, c), F32) for (c, _) in outs])
    return _pcall(
        body, name=name, grid=(n_tok // tl,), in_specs=in_specs, out_specs=out_specs, out_shape=out_shape,
        compiler_params=_cparams(("arbitrary",) if accs else ("parallel",), vm),
    )(*[r[0] for r in rows], *fulls)


def _mm(name, a, b, *, ta=False, tb=False, out_dtype=F32, alpha=1.0, add=None, tm=512, tn=512):
    a_arr, a_cb, a_w = _win(a)
    b_arr, b_cb, b_w = _win(b)
    if ta:
        kdim, m = a_arr.shape[0], a_w
    else:
        m, kdim = a_arr.shape[0], a_w
    if tb:
        n, kb = b_arr.shape[0], b_w
    else:
        kb, n = b_arr.shape[0], b_w
    assert kdim == kb, (name, kdim, kb)
    if not ta and kdim <= 1024:
        tm = 2 * tm
    tm = _tile(m, tm, LANE if ta else 16)
    tn = _tile(n, tn, LANE)
    if ta:
        a_spec = pl.BlockSpec((kdim, tm), lambda i, j: (0, a_cb * (a_w // tm) + i))
        a_blk = (kdim, tm)
    else:
        a_spec = pl.BlockSpec((tm, kdim), lambda i, j: (i, a_cb))
        a_blk = (tm, kdim)
    if tb:
        b_spec = pl.BlockSpec((tn, kdim), lambda i, j: (j, b_cb))
        b_blk = (tn, kdim)
    else:
        b_spec = pl.BlockSpec((kdim, tn), lambda i, j: (0, b_cb * (b_w // tn) + j))
        b_blk = (kdim, tn)
    dims = (((0 if ta else 1,), (1 if tb else 0,)), ((), ()))
    has_add = add is not None

    def body(*refs):
        a_ref, b_ref = refs[0], refs[1]
        o_ref = refs[-1]
        acc = lax.dot_general(a_ref[...].astype(BF16), b_ref[...].astype(BF16), dims, preferred_element_type=F32)
        if alpha != 1.0:
            acc = acc * alpha
        if has_add:
            acc = acc + refs[2][...].astype(F32)
        o_ref[...] = acc.astype(o_ref.dtype)

    in_specs = [a_spec, b_spec]
    ops = [a_arr, b_arr]
    if has_add:
        in_specs.append(pl.BlockSpec((tm, tn), lambda i, j: (i, j)))
        ops.append(add)
    vm = 2 * (_nbytes(a_blk, a_arr.dtype) + _nbytes(b_blk, b_arr.dtype) + _nbytes((tm, tn), out_dtype))
    vm += _nbytes(a_blk, BF16) + _nbytes(b_blk, BF16) + 3 * _nbytes((tm, tn), F32)
    return _pcall(
        body, name=name, grid=(m // tm, n // tn), in_specs=in_specs,
        out_specs=pl.BlockSpec((tm, tn), lambda i, j: (i, j)),
        out_shape=jax.ShapeDtypeStruct((m, n), out_dtype),
        compiler_params=_cparams(("parallel", "parallel"), vm),
    )(*ops)


def _sigmoid(x):
    return 0.5 * jnp.tanh(0.5 * x) + 0.5


def _rms(x, gain):
    r = lax.rsqrt(jnp.mean(x * x, axis=-1, keepdims=True) + EPS)
    return x * r * gain


def _rms_bwd(x, gain, dh):
    r = lax.rsqrt(jnp.mean(x * x, axis=-1, keepdims=True) + EPS)
    xh = x * r
    dg = jnp.sum(dh * xh, axis=0, keepdims=True)
    dxh = dh * gain
    dx = r * (dxh - xh * jnp.mean(dxh * xh, axis=-1, keepdims=True))
    return dx, dg


_GELU_C = math.sqrt(2.0 / math.pi)


def _gelu(y):
    return 0.5 * y * (1.0 + jnp.tanh(_GELU_C * (y + 0.044715 * y * y * y)))


def _gelu_grad(y):
    th = jnp.tanh(_GELU_C * (y + 0.044715 * y * y * y))
    return 0.5 * (1.0 + th) + 0.5 * y * (1.0 - th * th) * _GELU_C * (1.0 + 3.0 * 0.044715 * y * y)


def _log_sigmoid(x):
    return jnp.minimum(x, 0.0) - jnp.log(1.0 + jnp.exp(-jnp.abs(x)))


def _dot(a, b, dims=(((1,), (0,)), ((), ())), precision=None):
    return lax.dot_general(a, b, dims, preferred_element_type=F32, precision=precision)


def _dot_exact(x, m, dims=(((1,), (0,)), ((), ())), terms=3, m_first=False):
    m16 = m.astype(BF16)
    acc, rest = None, x
    for k in range(terms):
        piece = rest.astype(BF16)
        t = _dot(m16, piece, dims) if m_first else _dot(piece, m16, dims)
        acc = t if acc is None else acc + t
        if k + 1 < terms:
            rest = rest - piece.astype(F32)
    return acc


_NT = (((1,), (1,)), ((), ()))
_TN = (((0,), (0,)), ((), ()))


def _scan_core(b_ref, are_ref, aim_ref, pw, last, of32, *, rev, xf32=None, da_ref=None, lc=512):
    nb, nc = b_ref.shape[0] // SUBLANE, S5_N // lc
    row = lax.broadcasted_iota(jnp.int32, (SUBLANE, lc), 0)

    def cmul(ar, ai, xr, xi):
        return ar * xr - ai * xi, ar * xi + ai * xr

    def hs(c, xr, xi):
        cs = pl.ds(c * lc, lc)
        for k, slot in ((1, 0), (2, 2), (4, 4)):
            ar, ai = pw[slot, :, cs], pw[slot + 1, :, cs]
            if rev:
                sr, si = pltpu.roll(xr, SUBLANE - k, 0), pltpu.roll(xi, SUBLANE - k, 0)
                keep = row < SUBLANE - k
            else:
                sr, si = pltpu.roll(xr, k, 0), pltpu.roll(xi, k, 0)
                keep = row >= k
            sr, si = jnp.where(keep, sr, 0.0), jnp.where(keep, si, 0.0)
            pr, pi = cmul(ar, ai, sr, si)
            xr, xi = xr + pr, xi + pi
        return xr, xi

    @pl.when(pl.program_id(0) == 0)
    def _():
        last[...] = jnp.zeros(last.shape, F32)
        if da_ref is not None:
            da_ref[...] = jnp.zeros(da_ref.shape, F32)
        for c in range(nc):
            cs = pl.ds(c * lc, lc)
            a1r = jnp.broadcast_to(are_ref[:, cs], (SUBLANE, lc))
            a1i = jnp.broadcast_to(aim_ref[:, cs], (SUBLANE, lc))
            a2r, a2i = cmul(a1r, a1i, a1r, a1i)
            a4r, a4i = cmul(a2r, a2i, a2r, a2i)
            for slot, v in enumerate((a1r, a1i, a2r, a2i, a4r, a4i)):
                pw[slot, :, cs] = v
            first = SUBLANE - 1 if rev else 0
            pr, pi = hs(c, jnp.where(row == first, a1r, 0.0), jnp.where(row == first, a1i, 0.0))
            pw[6, :, cs] = pr
            pw[7, :, cs] = pi

    edge = 0 if rev else SUBLANE - 1

    def blk(j, carry):
        jj = (nb - 1 - j) if rev else j
        r0 = pl.multiple_of(jj * SUBLANE, SUBLANE)
        for c in range(nc):
            cre, cim = pl.ds(c * lc, lc), pl.ds(S5_N + c * lc, lc)
            xr, xi = hs(c, b_ref[pl.ds(r0, SUBLANE), cre], b_ref[pl.ds(r0, SUBLANE), cim])
            cr = jnp.broadcast_to(last[edge:edge + 1, cre], (SUBLANE, lc))
            ci = jnp.broadcast_to(last[edge:edge + 1, cim], (SUBLANE, lc))
            pr, pi = cmul(pw[6, :, cre], pw[7, :, cre], cr, ci)
            xr, xi = xr + pr, xi + pi
            if da_ref is not None:
                if rev:
                    qr, qi = pltpu.roll(xr, SUBLANE - 1, 0), pltpu.roll(xi, SUBLANE - 1, 0)
                    fill = row == SUBLANE - 1
                else:
                    qr, qi = pltpu.roll(xr, 1, 0), pltpu.roll(xi, 1, 0)
                    fill = row == 0
                qr, qi = jnp.where(fill, cr, qr), jnp.where(fill, ci, qi)
                zr, zi = xf32[pl.ds(r0, SUBLANE), cre], xf32[pl.ds(r0, SUBLANE), cim]
                da_ref[:, cre] += zr * qr + zi * qi
                da_ref[:, cim] += zr * qi - zi * qr
            last[:, cre] = xr
            last[:, cim] = xi
            of32[pl.ds(r0, SUBLANE), cre] = xr
            of32[pl.ds(r0, SUBLANE), cim] = xi
        return carry

    lax.fori_loop(0, nb, blk, 0)


def _s5_tiles():
    return [(slice((j % S5_TILES) * S5_TC, (j % S5_TILES + 1) * S5_TC), slice(j * S5_TS, (j + 1) * S5_TS))
            for j in range(2 * S5_TILES)]


def _s5_scratch(tb):
    return [pltpu.VMEM((8, SUBLANE, S5_N), F32), pltpu.VMEM((SUBLANE, 2 * S5_N), F32),
            pltpu.VMEM((tb, 2 * S5_N), F32), pltpu.VMEM((tb, 2 * S5_N), F32),
            pltpu.VMEM((S5_TC, 2 * S5_N), BF16), pltpu.VMEM((2 * S5_N, S5_TC), BF16)]


def _s5_fwd(name, z, a_re, a_im, wb, wc, add, *, rev, tb=512):
    n_tok = z.shape[0]
    tb = _tile(n_tok, tb, 16)
    nt = n_tok // tb
    has_add = add is not None

    def body(*refs):
        u_ref, are_ref, aim_ref, wb_ref, wc_ref = refs[:5]
        x_ref, y_ref, pw, last, of32, bu, wb16, wc16 = refs[5 + has_add:]

        @pl.when(pl.program_id(0) == 0)
        def _():
            wb16[...] = wb_ref[...].astype(BF16)
            wc16[...] = wc_ref[...].astype(BF16)

        u16 = u_ref[...].astype(BF16)
        for ch, st in _s5_tiles():
            bu[:, st] = _dot(u16[:, ch], wb16[:, st])
        _scan_core(bu, are_ref, aim_ref, pw, last, of32, rev=rev)
        x16 = of32[...].astype(BF16)
        x_ref[...] = x16
        ys = []
        for k in range(S5_TILES):
            re, im = slice(k * S5_TS, (k + 1) * S5_TS), slice(S5_N + k * S5_TS, S5_N + (k + 1) * S5_TS)
            ys.append(_dot(x16[:, re], wc16[re, :]) + _dot(x16[:, im], wc16[im, :]))
        y = jnp.concatenate(ys, axis=1)
        y_ref[...] = y + refs[5][...] if has_add else y

    tmap = (lambda t: (nt - 1 - t, 0)) if rev else (lambda t: (t, 0))
    const = lambda t: (0, 0)
    in_specs = [pl.BlockSpec((tb, S5_W), tmap), pl.BlockSpec((1, S5_N), const), pl.BlockSpec((1, S5_N), const),
                pl.BlockSpec((S5_TC, 2 * S5_N), const), pl.BlockSpec((2 * S5_N, S5_TC), const)]
    ops = [z, a_re, a_im, wb, wc]
    if has_add:
        in_specs.append(pl.BlockSpec((tb, S5_W), tmap))
        ops.append(add)
    return _pcall(
        body, name=name, grid=(nt,), in_specs=in_specs,
        out_specs=[pl.BlockSpec((tb, 2 * S5_N), tmap), pl.BlockSpec((tb, S5_W), tmap)],
        out_shape=[jax.ShapeDtypeStruct((n_tok, 2 * S5_N), BF16), jax.ShapeDtypeStruct((n_tok, S5_W), F32)],
        scratch_shapes=_s5_scratch(tb), compiler_params=_cparams(("arbitrary",), 10 * tb * 2 * S5_N * 4),
    )(*ops)


def _s5_bwd(name, dy, z, xs, a_re, a_im, wb, wc, add, *, rev, tb=256):
    n_tok = z.shape[0]
    tb = _tile(n_tok, tb, 16)
    nt = n_tok // tb

    def body(dy_ref, u_ref, xs_ref, are_ref, aim_ref, wb_ref, wc_ref, add_ref, du_ref, dwb_ref, dwc_ref, da_ref,
             pw, last, of32, gx, wb16, wc16, xf32):
        @pl.when(pl.program_id(0) == 0)
        def _():
            wb16[...] = wb_ref[...].astype(BF16)
            wc16[...] = wc_ref[...].astype(BF16)
            dwb_ref[...] = jnp.zeros(dwb_ref.shape, F32)
            dwc_ref[...] = jnp.zeros(dwc_ref.shape, F32)

        dy16, u16, x16 = dy_ref[...].astype(BF16), u_ref[...].astype(BF16), xs_ref[...]
        for ch, st in _s5_tiles():
            gx[:, st] = _dot(dy16[:, ch], wc16[st, :], _NT)
        xf32[...] = x16.astype(F32)
        _scan_core(gx, are_ref, aim_ref, pw, last, of32, rev=rev, xf32=xf32, da_ref=da_ref)
        g16 = of32[...].astype(BF16)
        dus = []
        for k in range(S5_TILES):
            re, im = slice(k * S5_TS, (k + 1) * S5_TS), slice(S5_N + k * S5_TS, S5_N + (k + 1) * S5_TS)
            dus.append(_dot(g16[:, re], wb16[:, re], _NT) + _dot(g16[:, im], wb16[:, im], _NT))
        du_ref[...] = jnp.concatenate(dus, axis=1) + add_ref[...]
        for ch, st in _s5_tiles():
            dwb_ref[:, st] += _dot(u16[:, ch], g16[:, st], _TN)
            dwc_ref[st, :] += _dot(x16[:, st], dy16[:, ch], _TN)

    tmap = (lambda t: (nt - 1 - t, 0)) if rev else (lambda t: (t, 0))
    const = lambda t: (0, 0)
    row_w, row_s = pl.BlockSpec((tb, S5_W), tmap), pl.BlockSpec((tb, 2 * S5_N), tmap)
    return _pcall(
        body, name=name, grid=(nt,),
        in_specs=[row_w, row_w, row_s, pl.BlockSpec((1, S5_N), const), pl.BlockSpec((1, S5_N), const),
                  pl.BlockSpec((S5_TC, 2 * S5_N), const), pl.BlockSpec((2 * S5_N, S5_TC), const), row_w],
        out_specs=[row_w, pl.BlockSpec((S5_TC, 2 * S5_N), const), pl.BlockSpec((2 * S5_N, S5_TC), const),
                   pl.BlockSpec((SUBLANE, 2 * S5_N), const)],
        out_shape=[jax.ShapeDtypeStruct((n_tok, S5_W), F32), jax.ShapeDtypeStruct((S5_TC, 2 * S5_N), F32),
                   jax.ShapeDtypeStruct((2 * S5_N, S5_TC), F32), jax.ShapeDtypeStruct((SUBLANE, 2 * S5_N), F32)],
        scratch_shapes=_s5_scratch(tb) + [pltpu.VMEM((tb, 2 * S5_N), F32)],
        compiler_params=_cparams(("arbitrary",), 12 * tb * 2 * S5_N * 4),
    )(dy, z, xs, a_re, a_im, wb, wc, add)


GLA_TB = 512


def _gla_block_terms(la, q, k, rev):
    trif = _chunk_mask(rev).astype(F32)
    chunks = [la[t:t + GLA_C] for t in range(0, la.shape[0], GLA_C)]
    bcum = jnp.concatenate([_dot_exact(c, trif, m_first=True) for c in chunks], axis=0)
    blast = _per_chunk_sum(la)
    eb, enb, ee = jnp.exp(bcum), jnp.exp(-bcum), jnp.exp(blast - bcum)
    return trif, jnp.exp(blast), eb, enb, ee, q * eb, k * enb, k * ee


def _per_chunk_sum(t):
    return jnp.concatenate([jnp.broadcast_to(jnp.sum(t[r:r + GLA_C], axis=0, keepdims=True), (GLA_C, t.shape[1]))
                            for r in range(0, t.shape[0], GLA_C)], axis=0)


def _chunk_mask(rev):
    ri = lax.broadcasted_iota(jnp.int32, (GLA_C, GLA_C), 0)
    ci = lax.broadcasted_iota(jnp.int32, (GLA_C, GLA_C), 1)
    return (ri <= ci) if rev else (ri >= ci)


def _gla_fwd(name, z, la, *, rev):
    n_tok = z.shape[0]
    tb = _tile(n_tok, GLA_TB, GLA_C)
    nblk, per = n_tok // tb, tb // GLA_C
    scale = GLA_D ** -0.5

    def body(q_ref, k_ref, v_ref, la_ref, o_ref, s_ref, st, qd_s, kd_s, ke_s, v_s, dec_s):
        @pl.when(pl.program_id(0) == 0)
        def _():
            st[...] = jnp.zeros(st.shape, F32)

        _, dec, _, _, _, qd, kd, ke = _gla_block_terms(la_ref[...], q_ref[...].astype(F32) * scale,
                                                                 k_ref[...].astype(F32), rev)
        qd_s[...], kd_s[...], ke_s[...] = qd.astype(BF16), kd.astype(BF16), ke.astype(BF16)
        v_s[...] = v_ref[...].astype(BF16)
        dec_s[...] = dec
        tri = _chunk_mask(rev)
        for step in range(per):
            t = per - 1 - step if rev else step
            rows = pl.ds(t * GLA_C, GLA_C)
            for h in range(GLA_H):
                lanes = pl.ds(h * GLA_D, GLA_D)
                qd16, v16 = qd_s[rows, lanes], v_s[rows, lanes]
                sc = jnp.where(tri, _dot(qd16, kd_s[rows, lanes], _NT), 0.0)
                s16 = st[h].astype(BF16)
                s_ref[h, step] = s16
                o_ref[rows, lanes] = _dot(sc.astype(BF16), v16) + _dot(qd16, s16, _NT)
                st[h] = st[h] * dec_s[pl.ds(t * GLA_C, 1), lanes] + _dot(v16, ke_s[rows, lanes], _TN)

    bmap = (lambda n: nblk - 1 - n) if rev else (lambda n: n)

    def zspec(cb):
        return pl.BlockSpec((tb, GLA_W), lambda n: (bmap(n), cb))

    half = pltpu.VMEM((tb, GLA_W), BF16)
    return _pcall(
        body, name=name, grid=(nblk,),
        in_specs=[zspec(1), zspec(2), zspec(3), zspec(0)],
        out_specs=[zspec(0), pl.BlockSpec((GLA_H, per, GLA_D, GLA_D), lambda n: (0, n, 0, 0))],
        out_shape=[jax.ShapeDtypeStruct((n_tok, GLA_W), F32),
                   jax.ShapeDtypeStruct((GLA_H, n_tok // GLA_C, GLA_D, GLA_D), BF16)],
        scratch_shapes=[pltpu.VMEM((GLA_H, GLA_D, GLA_D), F32), half, half, half, half, pltpu.VMEM((tb, GLA_W), F32)],
        compiler_params=_cparams(("arbitrary",), 40 * tb * GLA_W * 4),
    )(z, z, z, la)


def _gla_bwd(name, z, la, s_prev, do, *, rev):
    n_tok = z.shape[0]
    tb = _tile(n_tok, GLA_TB, GLA_C)
    nblk, per = n_tok // tb, tb // GLA_C
    scale = GLA_D ** -0.5

    def body(q_ref, k_ref, v_ref, la_ref, s_ref, do_ref, dq_ref, dk_ref, dv_ref, dla_ref,
             dst, qd_s, kd_s, ke_s, v_s, do_s, dec_s, dqd_s, dkd_s, dke_s, dd_s):
        @pl.when(pl.program_id(0) == 0)
        def _():
            dst[...] = jnp.zeros(dst.shape, F32)

        trif, dec, eb, enb, ee, qd, kd, ke = _gla_block_terms(la_ref[...], q_ref[...].astype(F32) * scale,
                                                                 k_ref[...].astype(F32), rev)
        qd_s[...], kd_s[...], ke_s[...] = qd.astype(BF16), kd.astype(BF16), ke.astype(BF16)
        v_s[...] = v_ref[...].astype(BF16)
        do_s[...] = do_ref[...].astype(BF16)
        dec_s[...] = dec
        tri = _chunk_mask(rev)
        for step in range(per):
            t = step if rev else per - 1 - step
            rows = pl.ds(t * GLA_C, GLA_C)
            for h in range(GLA_H):
                lanes = pl.ds(h * GLA_D, GLA_D)
                qd16, kd16, ke16 = qd_s[rows, lanes], kd_s[rows, lanes], ke_s[rows, lanes]
                v16, do16 = v_s[rows, lanes], do_s[rows, lanes]
                sc16 = jnp.where(tri, _dot(qd16, kd16, _NT), 0.0).astype(BF16)
                dsc16 = jnp.where(tri, _dot(do16, v16, _NT), 0.0).astype(BF16)
                s16 = s_ref[h, per - 1 - step]
                dsn = dst[h]
                dsn16 = dsn.astype(BF16)
                dec_c = dec_s[pl.ds(t * GLA_C, 1), lanes]
                dqd_s[rows, lanes] = _dot(dsc16, kd16) + _dot(do16, s16)
                dkd_s[rows, lanes] = _dot(dsc16, qd16, _TN)
                dke_s[rows, lanes] = _dot(v16, dsn16)
                dv_ref[rows, lanes] = _dot(sc16, do16, _TN) + _dot(ke16, dsn16, _NT)
                ddec = jnp.sum(dsn * s16.astype(F32), axis=0, keepdims=True) * dec_c
                dd_s[rows, lanes] = jnp.broadcast_to(ddec, (GLA_C, GLA_D))
                dst[h] = dsn * dec_c + _dot(do16, qd16, _TN)
        dqd, dkd, dke = dqd_s[...], dkd_s[...], dke_s[...]
        dq_ref[...] = dqd * eb * scale
        dk_ref[...] = dkd * enb + dke * ee
        db = dqd * qd - dkd * kd - dke * ke
        dcum = jnp.concatenate([_dot_exact(db[r:r + GLA_C], trif, _TN, m_first=True) for r in range(0, tb, GLA_C)], axis=0)
        dla_ref[...] = dcum + _per_chunk_sum(dke * ke) + dd_s[...]

    bmap = (lambda n: n) if rev else (lambda n: nblk - 1 - n)

    def zspec(cb):
        return pl.BlockSpec((tb, GLA_W), lambda n: (bmap(n), cb))

    o4 = jax.ShapeDtypeStruct((n_tok, GLA_W), F32)
    half, full = pltpu.VMEM((tb, GLA_W), BF16), pltpu.VMEM((tb, GLA_W), F32)
    return _pcall(
        body, name=name, grid=(nblk,),
        in_specs=[zspec(1), zspec(2), zspec(3), zspec(0),
                  pl.BlockSpec((GLA_H, per, GLA_D, GLA_D), lambda n: (0, nblk - 1 - n, 0, 0)), zspec(0)],
        out_specs=[zspec(0)] * 4, out_shape=[o4] * 4,
        scratch_shapes=[pltpu.VMEM((GLA_H, GLA_D, GLA_D), F32), half, half, half, half, half, full, full, full, full, full],
        compiler_params=_cparams(("arbitrary",), 64 * tb * GLA_W * 4),
    )(z, z, z, la, s_prev, do)


_PAIRS = AT_W // LANE
_PAIRS_PER_KV = _PAIRS // AT_KH


def _attn_fwd(name, q, kd, vd, tq=256):
    n_tok = q.shape[0]
    tq = _tile(n_tok, tq, 16)

    def body(q_ref, k_ref, v_ref, o_ref, l_ref):
        qv, kv, vv = q_ref[...], k_ref[...], v_ref[...]
        low = lax.broadcasted_iota(jnp.int32, qv.shape, 1) < AT_D
        res = []
        for keep in (low, jnp.logical_not(low)):
            s = _dot(jnp.where(keep, qv, jnp.zeros_like(qv)), kv, _NT)
            m = jnp.max(s, axis=-1, keepdims=True)
            p = jnp.exp(s - m)
            den = jnp.sum(p, axis=-1, keepdims=True)
            res.append((_dot(p.astype(BF16), vv) / den, m + jnp.log(den)))
        o_ref[...] = jnp.where(low, res[0][0], res[1][0])
        l_ref[...] = jnp.where(low, res[0][1], res[1][1])

    qspec = pl.BlockSpec((tq, LANE), lambda p, i: (i, p))
    kspec = pl.BlockSpec((n_tok, LANE), lambda p, i: (0, p // _PAIRS_PER_KV))
    vm = 5 * _nbytes((tq, n_tok), F32) + 8 * _nbytes((n_tok, LANE), BF16)
    out = jax.ShapeDtypeStruct((n_tok, AT_W), F32)
    return _pcall(
        body, name=name, grid=(_PAIRS, n_tok // tq), in_specs=[qspec, kspec, kspec], out_specs=[qspec, qspec],
        out_shape=[out, out], compiler_params=_cparams(("parallel", "parallel"), vm),
    )(q, kd, vd)


def _attn_bwd(name, q, kd, vd, o, lse, do, tq=256):
    n_tok = q.shape[0]
    tq = _tile(n_tok, tq, 16)

    def body(q_ref, k_ref, v_ref, o_ref, l_ref, do_ref, dq_ref, dk_ref, dv_ref):
        @pl.when((pl.program_id(1) == 0) & (pl.program_id(2) == 0))
        def _():
            dk_ref[...] = jnp.zeros(dk_ref.shape, F32)
            dv_ref[...] = jnp.zeros(dv_ref.shape, F32)

        qv, kv, vv = q_ref[...], k_ref[...], v_ref[...]
        dof, lv = do_ref[...], l_ref[...]
        do16 = dof.astype(BF16)
        doo = dof * o_ref[...]
        low = lax.broadcasted_iota(jnp.int32, qv.shape, 1) < AT_D
        dq = jnp.zeros(qv.shape, F32)
        for keep in (low, jnp.logical_not(low)):
            qm = jnp.where(keep, qv, jnp.zeros_like(qv))
            dom = jnp.where(keep, do16, jnp.zeros_like(do16))
            lh = jnp.max(jnp.where(keep, lv, -1e30), axis=-1, keepdims=True)
            p = jnp.exp(_dot(qm, kv, _NT) - lh)
            delta = jnp.sum(jnp.where(keep, doo, 0.0), axis=-1, keepdims=True)
            ds16 = (p * (_dot(dom, vv, _NT) - delta)).astype(BF16)
            dq = jnp.where(keep, _dot(ds16, kv), dq)
            dk_ref[...] += _dot(ds16, qm, _TN)
            dv_ref[...] += _dot(p.astype(BF16), dom, _TN)
        dq_ref[...] = dq

    qspec = pl.BlockSpec((tq, LANE), lambda g, j, i: (i, g * _PAIRS_PER_KV + j))
    kspec = pl.BlockSpec((n_tok, LANE), lambda g, j, i: (0, g))
    vm = 7 * _nbytes((tq, n_tok), F32) + 12 * _nbytes((n_tok, LANE), F32)
    dup = jax.ShapeDtypeStruct((n_tok, AT_KH * LANE), F32)
    return _pcall(
        body, name=name, grid=(AT_KH, _PAIRS_PER_KV, n_tok // tq),
        in_specs=[qspec, kspec, kspec, qspec, qspec, qspec], out_specs=[qspec, kspec, kspec],
        out_shape=[jax.ShapeDtypeStruct((n_tok, AT_W), F32), dup, dup],
        compiler_params=_cparams(("parallel", "arbitrary", "arbitrary"), vm),
    )(q, kd, vd, o, lse, do)


def _s5_prep(lam_re, lam_im, log_dt, b_re, b_im, c_re, c_im):
    dt = jnp.exp(log_dt)[:, None]
    er = jnp.exp(lam_re * dt)
    a_re, a_im = er * jnp.cos(lam_im * dt), er * jnp.sin(lam_im * dt)
    nr, ni = a_re - 1.0, a_im
    den = lam_re * lam_re + lam_im * lam_im
    f_re, f_im = (nr * lam_re + ni * lam_im) / den, (ni * lam_re - nr * lam_im) / den
    bb_re = f_re[..., None] * b_re - f_im[..., None] * b_im
    bb_im = f_re[..., None] * b_im + f_im[..., None] * b_re
    def tiles(t, axis):
        _, a, b = t.shape
        eye = jnp.eye(S5_TG, dtype=t.dtype)
        blocks = t.reshape(S5_TILES, S5_TG, a, 1, b) * eye[None, :, None, :, None]
        if axis == 0:
            return blocks.reshape(S5_TILES * S5_TG * a, S5_TG * b)
        return blocks.transpose(1, 2, 0, 3, 4).reshape(S5_TG * a, S5_TILES * S5_TG * b)

    w_in = jnp.concatenate([tiles(bb_re.transpose(0, 2, 1), 1), tiles(bb_im.transpose(0, 2, 1), 1)], axis=1)
    w_out = jnp.concatenate([tiles(c_re.transpose(0, 2, 1), 0), -tiles(c_im.transpose(0, 2, 1), 0)], axis=0)
    return a_re.reshape(1, S5_N), a_im.reshape(1, S5_N), w_in, w_out


def _rope_tables(n_tok):
    quarter = AT_D // 4
    d = np.arange(AT_D)
    e = d % (AT_D // 2)
    inv = (ROPE_BASE ** (-(e % quarter).astype(np.float64) * 2.0 / (AT_D // 2))).astype(np.float32)
    sign = np.where(e < quarter, -1.0, 1.0).astype(np.float32)
    rot = np.zeros((AT_D, AT_D), np.float32)
    rot[np.where(e < quarter, d + quarter, d - quarter), d] = 1.0
    pos = jnp.arange(n_tok, dtype=jnp.int32)
    axis_pos = jnp.where(jnp.asarray(d // (AT_D // 2) == 0)[None, :], (pos // GRID_W)[:, None], (pos % GRID_W)[:, None])
    ang = axis_pos.astype(F32) * jnp.asarray(inv)[None, :]
    return jnp.cos(ang), jnp.sin(ang) * jnp.asarray(sign)[None, :], rot


def _head_consts(n_tok, heads):
    cos, sin, rot = _rope_tables(n_tok)
    eye = np.eye(heads, dtype=np.float32)
    mean = np.kron(eye, np.full((AT_D, AT_D), 1.0 / AT_D, np.float32))
    return jnp.tile(cos, (1, heads)), jnp.tile(sin, (1, heads)), jnp.asarray(mean), jnp.asarray(np.kron(eye, rot))


def _pad_w_in(wt):
    return jnp.concatenate([wt[:2560], wt[2592:3360], wt[2560:2592], jnp.zeros((96, wt.shape[1]), wt.dtype)], axis=0)


def _unpad_w_in(gt):
    return jnp.concatenate([gt[:2560], gt[3328:3360], gt[2560:3328]], axis=0)


def _ffn_fwd(x, g, wg, wu, wd, d):
    n_tok = x.shape[0]
    fq = wd.shape[-2]
    tm = _tile(n_tok, 512, 16)
    w_bytes = N_CHIPS * fq * D_MODEL * 2

    def up(x_ref, g_ref, wg_ref, wu_ref, h_ref, a_ref, b_ref, s_ref):
        hv = _rms(x_ref[...], g_ref[...]).astype(BF16)
        h_ref[...] = hv
        for j in range(N_CHIPS):
            av, bv = _dot(hv, wg_ref[j, 0], _NT), _dot(hv, wu_ref[j, 0], _NT)
            a_ref[j] = av.astype(BF16)
            b_ref[j] = bv.astype(BF16)
            s_ref[j] = (av * _sigmoid(av) * bv).astype(BF16)

    wspec = pl.BlockSpec((N_CHIPS, 1, fq, D_MODEL), lambda i: (0, d, 0, 0))
    aspec = pl.BlockSpec((N_CHIPS, tm, fq), lambda i: (0, i, 0))
    xspec = pl.BlockSpec((tm, D_MODEL), lambda i: (i, 0))
    act = jax.ShapeDtypeStruct((N_CHIPS, n_tok, fq), BF16)
    h, a, b, s = _pcall(
        up, name="ffn_up", grid=(n_tok // tm,),
        in_specs=[xspec, pl.BlockSpec((1, D_MODEL), lambda i: (0, 0)), wspec, wspec],
        out_specs=[xspec, aspec, aspec, aspec],
        out_shape=[jax.ShapeDtypeStruct((n_tok, D_MODEL), BF16), act, act, act],
        compiler_params=_cparams(("parallel",), 4 * w_bytes + 6 * N_CHIPS * tm * fq * 2 + 12 * tm * fq * 4
                                 + 8 * tm * D_MODEL * 4),
    )(x, g, wg, wu)

    def down(s_ref, w_ref, x_ref, o_ref):
        acc = _dot(s_ref[0], w_ref[0, 0])
        for j in range(1, N_CHIPS):
            acc = acc + _dot(s_ref[j], w_ref[j, 0])
        o_ref[...] = x_ref[...] + 0.5 * acc

    xo = _pcall(
        down, name="ffn_down", grid=(n_tok // tm,), in_specs=[aspec, wspec, xspec], out_specs=xspec,
        out_shape=jax.ShapeDtypeStruct((n_tok, D_MODEL), F32),
        compiler_params=_cparams(("parallel",), 2 * w_bytes + 2 * N_CHIPS * tm * fq * 2 + 8 * tm * D_MODEL * 4),
    )(s, wd, x)
    return xo, (x, h, a, b, s)


def _ffn_bwd(dy, dy16, saved, g, wg, wu, wd, d, into):
    x, h, a, b, s = saved
    n_tok = x.shape[0]
    fq = wd.shape[-2]
    tm = _tile(n_tok, 512, 16)

    ta_ = _tile(n_tok, 512, 16)

    def dact(dy_ref, w_ref, a_ref, b_ref, da_ref, db_ref):
        dyv = dy_ref[...]
        for j in range(N_CHIPS):
            ds = 0.5 * _dot(dyv, w_ref[j, 0], _NT)
            av, bv = a_ref[j].astype(F32), b_ref[j].astype(F32)
            sg = _sigmoid(av)
            da_ref[j] = (ds * bv * sg * (1.0 + av * (1.0 - sg))).astype(BF16)
            db_ref[j] = (ds * av * sg).astype(BF16)

    aspec = pl.BlockSpec((N_CHIPS, ta_, fq), lambda i: (0, i, 0))
    act = jax.ShapeDtypeStruct((N_CHIPS, n_tok, fq), BF16)
    da, db = _pcall(
        dact, name="ffn_dact", grid=(n_tok // ta_,),
        in_specs=[pl.BlockSpec((ta_, D_MODEL), lambda i: (i, 0)),
                  pl.BlockSpec((N_CHIPS, 1, fq, D_MODEL), lambda i: (0, d, 0, 0)), aspec, aspec],
        out_specs=[aspec, aspec], out_shape=[act, act],
        compiler_params=_cparams(("parallel",), 2 * N_CHIPS * fq * D_MODEL * 2 + 8 * N_CHIPS * ta_ * fq * 2
                                 + 12 * ta_ * fq * 4),
    )(dy16, wd, a, b)

    any_spec = pl.BlockSpec(memory_space=pl.ANY)
    alias = {} if into is None else {2: 0}
    tn = _tile(D_MODEL, 512, LANE)

    def dw_call(t, rhs, scale, prev):
        def dw(t_ref, r_ref, *rest):
            rest[-1][0, 0] = (scale * _dot(t_ref[0], r_ref[...], _TN)).astype(GRAD_DT)

        return _pcall(
            dw, name="ffn_dw", grid=(N_CHIPS, D_MODEL // tn),
            in_specs=[pl.BlockSpec((1, n_tok, fq), lambda j, n: (j, 0, 0)), pl.BlockSpec((n_tok, tn), lambda j, n: (0, n))]
            + ([] if prev is None else [any_spec]),
            out_specs=pl.BlockSpec((1, 1, fq, tn), lambda j, n: (j, d, 0, n)),
            out_shape=jax.ShapeDtypeStruct((N_CHIPS, DEPTH, fq, D_MODEL), GRAD_DT), input_output_aliases=alias,
            compiler_params=_cparams(("parallel", "parallel"), 6 * n_tok * (fq + tn) * 2 + 4 * fq * tn * 4),
        )(t, rhs, *([] if prev is None else [prev]))

    dwg = dw_call(da, h, 1.0, None if into is None else into[0])
    dwu = dw_call(db, h, 1.0, None if into is None else into[1])
    dwd = dw_call(s, dy16, 0.5, None if into is None else into[2])
    tr = _tile(n_tok, 256, 16)

    def dh_norm(da_ref, db_ref, wg_ref, wu_ref, dy_ref, x_ref, g_ref, dx_ref, dx16_ref, dg_ref):
        dh = _dot(da_ref[0], wg_ref[0, 0]) + _dot(db_ref[0], wu_ref[0, 0])
        for j in range(1, N_CHIPS):
            dh = dh + _dot(da_ref[j], wg_ref[j, 0]) + _dot(db_ref[j], wu_ref[j, 0])
        dx, dg = _rms_bwd(x_ref[...], g_ref[...], dh)
        dx = dy_ref[...] + dx
        dx_ref[...] = dx
        dx16_ref[...] = dx.astype(BF16)

        @pl.when(pl.program_id(0) == 0)
        def _():
            dg_ref[...] = jnp.zeros(dg_ref.shape, F32)

        dg_ref[...] += dg

    tspec = pl.BlockSpec((N_CHIPS, tr, fq), lambda i: (0, i, 0))
    wspec = pl.BlockSpec((N_CHIPS, 1, fq, D_MODEL), lambda i: (0, d, 0, 0))
    rspec = pl.BlockSpec((tr, D_MODEL), lambda i: (i, 0))
    vspec = pl.BlockSpec((1, D_MODEL), lambda i: (0, 0))
    dx, dx16, dg = _pcall(
        dh_norm, name="ffn_dh_norm", grid=(n_tok // tr,),
        in_specs=[tspec, tspec, wspec, wspec, rspec, rspec, vspec], out_specs=[rspec, rspec, vspec],
        out_shape=[jax.ShapeDtypeStruct((n_tok, D_MODEL), F32), jax.ShapeDtypeStruct((n_tok, D_MODEL), BF16),
                   jax.ShapeDtypeStruct((1, D_MODEL), F32)],
        compiler_params=_cparams(("arbitrary",), 4 * N_CHIPS * D_MODEL * fq * 2 + 4 * N_CHIPS * tr * fq * 2
                                 + 16 * tr * D_MODEL * 4),
    )(da, db, wg, wu, dy, x, g)
    return dx, dx16, dg[0], (dwg, dwu, dwd)


def _alpha_pads(w_alpha):
    z = jnp.zeros((LANE - 2 * GLA_R, GLA_W), F32)
    wf = jnp.concatenate([w_alpha[0], jnp.zeros((GLA_R, GLA_W), F32), z], axis=0)
    wb = jnp.concatenate([jnp.zeros((GLA_R, GLA_W), F32), w_alpha[1], z], axis=0)
    return wf, wb


def _dup_heads(t):
    low = lax.broadcasted_iota(jnp.int32, t.shape, 1) < AT_D
    h0, h1 = jnp.where(low, t, 0.0), jnp.where(low, 0.0, t)
    return jnp.concatenate([h0 + pltpu.roll(h0, AT_D, 1), h1 + pltpu.roll(h1, AT_D, 1)], axis=1)


def _fold_heads(t):
    low = lax.broadcasted_iota(jnp.int32, (t.shape[0], LANE), 1) < AT_D
    g0, g1 = t[:, :LANE], t[:, LANE:]
    return jnp.where(low, g0 + pltpu.roll(g0, AT_D, 1), g1 + pltpu.roll(g1, AT_D, 1))


def _mixer_fwd(x, w):
    n_tok = x.shape[0]
    wip, wmg = w['w_in_p'], w['w_merge_gate']
    tr = _tile(n_tok, 256, 16)

    def project(x_ref, g_ref, wip_ref, wmg_ref, h_ref, z_ref, gp_ref):
        hv = _rms(x_ref[...], g_ref[...]).astype(BF16)
        h_ref[...] = hv
        z_ref[...] = _dot(hv, wip_ref[...], _NT).astype(BF16)
        gp_ref[...] = _dot(hv, wmg_ref[...]).astype(BF16)

    def rows(c):
        return pl.BlockSpec((tr, c), lambda i: (i, 0))

    def whole(t):
        return pl.BlockSpec(t.shape, lambda i: (0, 0))

    h, z, gp = _pcall(
        project, name="mix_in", grid=(n_tok // tr,),
        in_specs=[rows(D_MODEL), whole(w['mix_norm']), whole(wip), whole(wmg)],
        out_specs=[rows(D_MODEL), rows(ZP_W), rows(3 * D_MODEL)],
        out_shape=[jax.ShapeDtypeStruct((n_tok, D_MODEL), BF16), jax.ShapeDtypeStruct((n_tok, ZP_W), BF16),
                   jax.ShapeDtypeStruct((n_tok, 3 * D_MODEL), BF16)],
        compiler_params=_cparams(("parallel",), 4 * (wip.size + wmg.size) + 16 * tr * (ZP_W + 3 * D_MODEL)),
    )(x, w['mix_norm'], wip, wmg)
    u = _cw(z, 0, S5_W)

    xs, yy = [], None
    for d, rev in ((0, False), (1, True)):
        a_re, a_im, wb, wc = w['s5'][d]
        x_d, yy = _s5_fwd("s5_rev" if rev else "s5_fwd", z, a_re, a_im, wb, wc, yy, rev=rev)
        xs.append(x_d)

    def s5_post(yv, uv, dv, wglu):
        y0 = yv + dv * uv
        y1 = _gelu(y0)
        t = _dot(y1.astype(BF16), wglu)
        return y0, t, y1 * _sigmoid(t)

    y0, t_glu, y_s5 = _rowwise("s5_post", s5_post, [yy, u], [w['s5_d'], w['s5_w_glu']],
                               [(S5_W, F32), (S5_W, F32), (S5_W, BF16)])

    wa_f, wa_b = _alpha_pads(w['gla_w_alpha'])
    ba_f, ba_b = w['gla_b_alpha'][0:1], w['gla_b_alpha'][1:2]
    zz = _cw(z, ZC_Z, LANE)

    def gla_prep(zv, waf, wab, bf, bb):
        z16 = zv.astype(BF16)
        lf = _dot(z16, waf.astype(BF16)) + bf
        lb = _dot(z16, wab.astype(BF16)) + bb
        return _log_sigmoid(lf) / GLA_TAU, _log_sigmoid(lb) / GLA_TAU

    la_f, la_b = _rowwise("gla_prep", gla_prep, [zz], [wa_f, wa_b, ba_f, ba_b], [(GLA_W, F32), (GLA_W, F32)])
    o_f, s_f = _gla_fwd("gla_fwd", z, la_f, rev=False)
    o_b, s_b = _gla_fwd("gla_rev", z, la_b, rev=True)
    gate = _cw(z, 4, GLA_W)

    def gla_post(of, ob, gv, gn):
        o, gv = of + ob, gv.astype(F32)
        outs = []
        for hh in range(GLA_H):
            sl = slice(hh * GLA_D, (hh + 1) * GLA_D)
            gh = gv[:, sl]
            outs.append(_rms(o[:, sl], gn) * (gh * _sigmoid(gh)))
        return jnp.concatenate(outs, axis=1)

    y_gla = _rowwise("gla_post", gla_post, [o_f, o_b, gate], [w['gla_norm']], [(GLA_W, BF16)])[0]

    cq, sq, mq, rq = _head_consts(n_tok, AT_QH)
    ck, sk, mk, rk = _head_consts(n_tok, AT_KH)
    gq = jnp.tile(w['attn_q_norm'], (1, AT_QH))
    gk = jnp.tile(w['attn_k_norm'], (1, AT_KH))

    def rope_fwd(v, gain, cos, sin, mean, rot, scale):
        r = lax.rsqrt(_dot_exact(v * v, mean, terms=2) + EPS)
        vn = v * r * gain
        return (vn * cos + _dot_exact(vn, rot, terms=2) * sin) * scale

    def attn_prep(qv, kv, vv, cqv, sqv, ckv, skv, gqv, gkv, mqv, rqv, mkv, rkv):
        qv, kv, vv = qv.astype(F32), kv.astype(F32), vv.astype(F32)
        kr = rope_fwd(kv, gkv, ckv, skv, mkv, rkv, 1.0)
        return rope_fwd(qv, gqv, cqv, sqv, mqv, rqv, AT_D ** -0.5), _dup_heads(kr), _dup_heads(vv)

    q_raw, k_raw, v_raw = _cw(z, 5, AT_W), _cw(z, ZC_AK, AT_KW), _cw(z, ZC_AV, AT_KW)
    q16, kd, vd = _rowwise("attn_prep", attn_prep, [q_raw, k_raw, v_raw, cq, sq, ck, sk], [gq, gk, mq, rq, mk, rk],
                           [(AT_W, BF16), (2 * LANE, BF16), (2 * LANE, BF16)])
    y_at, lse = _attn_fwd("attn_fwd", q16, kd, vd)

    p0 = _mm("mix_branch", y_s5, w['w_branch_s5'], out_dtype=BF16)
    p1 = _mm("mix_branch", y_gla, w['w_branch_gla'], out_dtype=BF16)
    p2 = _mm("mix_branch_f32", y_at, w['w_branch_attn'], out_dtype=BF16)

    def merge(gpv, a0, a1, a2, bias):
        g = _sigmoid(gpv.astype(F32) + bias)
        return (g[:, :D_MODEL] * a0.astype(F32) + g[:, D_MODEL:2 * D_MODEL] * a1.astype(F32)
                + g[:, 2 * D_MODEL:] * a2.astype(F32))

    merged = _rowwise("mix_merge", merge, [gp, p0, p1, p2], [w['b_merge_gate']], [(D_MODEL, BF16)])[0]
    xo = _mm("mix_out", merged, w['w_out'], add=x)
    saved = dict(x=x, h=h, z=z, gp=gp, xs=xs, y0=y0, t_glu=t_glu, y_s5=y_s5, la=(la_f, la_b), o=(o_f, o_b), s=(s_f, s_b),
                 y_gla=y_gla, q16=q16, kd=kd, vd=vd, lse=lse, y_at=y_at, p=(p0, p1, p2), merged=merged)
    return xo, saved


def _mixer_bwd(dy, dy16, sv, w, s5_raw):
    z, h = sv['z'], sv['h']
    n_tok = z.shape[0]
    grads = {}
    dmerged = _mm("mix_dmerged", dy16, w['w_out'], tb=True, out_dtype=BF16)
    grads['w_out'] = _mm("mix_dwout", sv['merged'], dy16, ta=True, out_dtype=GRAD_DT)

    def dmerge(dm, gpv, a0, a1, a2, bias):
        dm = dm.astype(F32)
        g = _sigmoid(gpv.astype(F32) + bias)
        ps = (a0.astype(F32), a1.astype(F32), a2.astype(F32))
        dps, dgs = [], []
        for k in range(3):
            gk = g[:, k * D_MODEL:(k + 1) * D_MODEL]
            dps.append(dm * gk)
            dgs.append(dm * ps[k] * gk * (1.0 - gk))
        dgp = jnp.concatenate(dgs, axis=1)
        return dps[0], dps[1], dps[2], dgp, jnp.sum(dgp, axis=0, keepdims=True)

    dp0, dp1, dp2, dgp, dbm = _rowwise("mix_dmerge", dmerge, [dmerged, sv['gp'], *sv['p']], [w['b_merge_gate']],
                                       [(D_MODEL, BF16)] * 3 + [(3 * D_MODEL, BF16)], accs=[(1, 3 * D_MODEL)])
    grads['b_merge_gate'] = dbm[0]
    grads['w_branch_s5'] = _mm("mix_dbranch", sv['y_s5'], dp0, ta=True, out_dtype=GRAD_DT)
    grads['w_branch_gla'] = _mm("mix_dbranch", sv['y_gla'], dp1, ta=True, out_dtype=GRAD_DT)
    grads['w_branch_attn'] = _mm("mix_dbranch_f32", sv['y_at'], dp2, ta=True, out_dtype=GRAD_DT)
    dy_s5 = _mm("mix_dy", dp0, w['w_branch_s5'], tb=True)
    dy_gla = _mm("mix_dy", dp1, w['w_branch_gla'], tb=True)
    dy_at = _mm("mix_dy", dp2, w['w_branch_attn'], tb=True)
    grads['w_merge_gate'] = _mm("mix_dwgate", h, dgp, ta=True, out_dtype=GRAD_DT)
    u = _cw(z, 0, S5_W)

    def s5_post_bwd(dyv, y0, t, uv, dv, wglu):
        sg = _sigmoid(t)
        y1 = _gelu(y0)
        dt = dyv * y1 * sg * (1.0 - sg)
        dy1 = dyv * sg + _dot(dt.astype(BF16), wglu, _NT)
        dy0 = dy1 * _gelu_grad(y0)
        return dy0, dt, y1, dy0 * dv, jnp.sum(dy0 * uv, axis=0, keepdims=True)

    dy0, dt_glu, y1, du, dd = _rowwise("s5_post_bwd", s5_post_bwd, [dy_s5, sv['y0'], sv['t_glu'], u],
                                       [w['s5_d'], w['s5_w_glu']],
                                       [(S5_W, BF16), (S5_W, BF16), (S5_W, BF16), (S5_W, F32)], accs=[(1, S5_W)])
    grads['s5_d'] = dd[0]
    grads['s5_w_glu'] = _mm("s5_dwglu", y1, dt_glu, ta=True, out_dtype=GRAD_DT)
    s5_cot = []
    for d, rev in ((0, False), (1, True)):
        a_re, a_im, wb, wc = w['s5'][d]
        du, dwb, dwc, da = _s5_bwd("s5_adj_fwd" if rev else "s5_adj_rev", dy0, z, sv['xs'][d], a_re, -a_im, wb, wc, du,
                                   rev=not rev)
        da = jnp.sum(da, axis=0, keepdims=True)
        s5_cot.append((da[:, :S5_N], da[:, S5_N:], dwb, dwc))
    for d in range(2):
        _, pull = jax.vjp(_s5_prep, *s5_raw[d])
        s5_cot[d] = pull(s5_cot[d])
    for k, nm in enumerate(('s5_lambda_re', 's5_lambda_im', 's5_log_dt', 's5_b_re', 's5_b_im', 's5_c_re', 's5_c_im')):
        grads[nm] = jnp.stack([s5_cot[0][k], s5_cot[1][k]])

    gate = _cw(z, 4, GLA_W)

    def gla_post_bwd(dyv, of, ob, gv, gn):
        o, gv = of + ob, gv.astype(F32)
        dos, dgates = [], []
        dgn = jnp.zeros((1, GLA_D), F32)
        for hh in range(GLA_H):
            sl = slice(hh * GLA_D, (hh + 1) * GLA_D)
            gh, dyh = gv[:, sl], dyv[:, sl]
            sg = _sigmoid(gh)
            dgates.append(dyh * _rms(o[:, sl], gn) * sg * (1.0 + gh * (1.0 - sg)))
            dx, dg = _rms_bwd(o[:, sl], gn, dyh * gh * sg)
            dos.append(dx)
            dgn = dgn + dg
        return jnp.concatenate(dos, axis=1), jnp.concatenate(dgates, axis=1), dgn

    do, dgate, dgn = _rowwise("gla_post_bwd", gla_post_bwd, [dy_gla, *sv['o'], gate], [w['gla_norm']],
                              [(GLA_W, F32), (GLA_W, F32)], accs=[(1, GLA_D)])
    grads['gla_norm'] = dgn[0]
    dq_f, dk_f, dv_f, dla_f = _gla_bwd("gla_bwd_fwd", z, sv['la'][0], sv['s'][0], do, rev=False)
    dq_b, dk_b, dv_b, dla_b = _gla_bwd("gla_bwd_rev", z, sv['la'][1], sv['s'][1], do, rev=True)
    wa_f, wa_b = _alpha_pads(w['gla_w_alpha'])
    ba_f, ba_b = w['gla_b_alpha'][0:1], w['gla_b_alpha'][1:2]
    zz = _cw(z, ZC_Z, LANE)

    def gla_prep_bwd(dlf, dlb, zv, waf, wab, bf, bb):
        z16 = zv.astype(BF16)
        waf16, wab16 = waf.astype(BF16), wab.astype(BF16)
        xf = _dot(z16, waf16) + bf
        xb = _dot(z16, wab16) + bb
        df = dlf * (_sigmoid(-xf) / GLA_TAU)
        db = dlb * (_sigmoid(-xb) / GLA_TAU)
        dz = _dot(df.astype(BF16), waf16, _NT) + _dot(db.astype(BF16), wab16, _NT)
        return df, db, dz, jnp.sum(df, axis=0, keepdims=True), jnp.sum(db, axis=0, keepdims=True)

    dlog_f, dlog_b, dzz, dba_f, dba_b = _rowwise(
        "gla_prep_bwd", gla_prep_bwd, [dla_f, dla_b, zz], [wa_f, wa_b, ba_f, ba_b],
        [(GLA_W, BF16), (GLA_W, BF16), (LANE, F32)], accs=[(1, GLA_W), (1, GLA_W)])
    dwa_f = _mm("gla_dwa", zz, dlog_f, ta=True)
    dwa_b = _mm("gla_dwa", zz, dlog_b, ta=True)
    grads['gla_w_alpha'] = jnp.stack([dwa_f[:GLA_R], dwa_b[GLA_R:2 * GLA_R]])
    grads['gla_b_alpha'] = jnp.concatenate([dba_f, dba_b], axis=0)

    dq_r, dkd, dvd = _attn_bwd("attn_bwd", sv['q16'], sv['kd'], sv['vd'], sv['y_at'], sv['lse'], dy_at)
    cq, sq, mq, rq = _head_consts(n_tok, AT_QH)
    ck, sk, mk, rk = _head_consts(n_tok, AT_KH)
    gq = jnp.tile(w['attn_q_norm'], (1, AT_QH))
    gk = jnp.tile(w['attn_k_norm'], (1, AT_KH))

    def rope_bwd(dr, v, gain, cos, sin, mean, rot, scale):
        dr = dr * scale
        dvn = dr * cos + _dot_exact(dr * sin, rot, _NT, terms=2)
        r = lax.rsqrt(_dot_exact(v * v, mean, terms=2) + EPS)
        vh = v * r
        dg = jnp.sum(dvn * vh, axis=0, keepdims=True)
        dvh = dvn * gain
        return r * (dvh - vh * _dot_exact(dvh * vh, mean, terms=2)), dg

    def attn_prep_bwd(dq, dk2, dv2, qv, kv, cqv, sqv, ckv, skv, gqv, gkv, mqv, rqv, mkv, rkv):
        qv, kv = qv.astype(F32), kv.astype(F32)
        dqo, dgq = rope_bwd(dq, qv, gqv, cqv, sqv, mqv, rqv, AT_D ** -0.5)
        dko, dgk = rope_bwd(_fold_heads(dk2), kv, gkv, ckv, skv, mkv, rkv, 1.0)
        return dqo, dko, _fold_heads(dv2), dgq, dgk

    q_raw, k_raw = _cw(z, 5, AT_W), _cw(z, ZC_AK, AT_KW)
    daq, dak, dav, dgq, dgk = _rowwise("attn_prep_bwd", attn_prep_bwd, [dq_r, dkd, dvd, q_raw, k_raw, cq, sq, ck, sk],
                                       [gq, gk, mq, rq, mk, rk], [(AT_W, F32), (AT_KW, F32), (AT_KW, F32)],
                                       accs=[(1, AT_W), (1, AT_KW)])
    grads['attn_q_norm'] = dgq.reshape(AT_QH, AT_D).sum(axis=0)
    grads['attn_k_norm'] = dgk.reshape(AT_KH, AT_D).sum(axis=0)

    def assemble(duv, qf, qb, kf, kb, vf, vb, dg, aq, ak, av, zzv):
        return jnp.concatenate([duv, qf + qb, kf + kb, vf + vb, dg, aq, ak, av, zzv], axis=1)

    dz = _rowwise("mix_dz", assemble, [du, dq_f, dq_b, dk_f, dk_b, dv_f, dv_b, dgate, daq, dak, dav, dzz], [],
                  [(ZP_W, BF16)])[0]
    grads['w_in'] = _unpad_w_in(_mm("mix_dwin", dz, h, ta=True, out_dtype=GRAD_DT))
    wmg, wip = w['w_merge_gate'], w['w_in_p']
    tr = _tile(n_tok, 256, 16)

    def dh_norm(dgp_ref, dz_ref, wmg_ref, wip_ref, dy_ref, x_ref, g_ref, dx_ref, dx16_ref, dg_ref):
        dh = _dot(dgp_ref[...], wmg_ref[...], _NT) + _dot(dz_ref[...], wip_ref[...])
        dxv, dgv = _rms_bwd(x_ref[...], g_ref[...], dh)
        dxv = dy_ref[...] + dxv
        dx_ref[...] = dxv
        dx16_ref[...] = dxv.astype(BF16)

        @pl.when(pl.program_id(0) == 0)
        def _():
            dg_ref[...] = jnp.zeros(dg_ref.shape, F32)

        dg_ref[...] += dgv

    def rows(c):
        return pl.BlockSpec((tr, c), lambda i: (i, 0))

    def whole(t):
        return pl.BlockSpec(t.shape, lambda i: (0, 0))

    dx, dx16, dg = _pcall(
        dh_norm, name="mix_dh_norm", grid=(n_tok // tr,),
        in_specs=[rows(dgp.shape[1]), rows(dz.shape[1]), whole(wmg), whole(wip), rows(D_MODEL), rows(D_MODEL),
                  whole(w['mix_norm'])],
        out_specs=[rows(D_MODEL), rows(D_MODEL), whole(w['mix_norm'])],
        out_shape=[jax.ShapeDtypeStruct((n_tok, D_MODEL), F32), jax.ShapeDtypeStruct((n_tok, D_MODEL), BF16),
                   jax.ShapeDtypeStruct((1, D_MODEL), F32)],
        compiler_params=_cparams(("arbitrary",), 4 * (wmg.size + wip.size) + 8 * tr * (dgp.shape[1] + dz.shape[1])
                                 + 16 * tr * D_MODEL * 4),
    )(dgp, dz, wmg, wip, dy, sv['x'], w['mix_norm'])
    grads['mix_norm'] = dg[0]
    return dx, dx16, grads


FFN_NAMES = [p + k for p in ('ffn1_', 'ffn2_') for k in ('w_gate', 'w_up', 'w_down')]
MISC_SHARDED = [n for n in SHARDED if n not in FFN_NAMES]
LARGE_SHARDED = FFN_NAMES + ['w_in', 'w_merge_gate', 'w_out', 'w_branch_s5', 'w_branch_gla', 'w_branch_attn', 's5_w_glu']
KEPT_TRANSPOSED = [p + k for p in ('ffn1_', 'ffn2_') for k in ('w_gate', 'w_up')] + ['w_in']


def _stored_axis(n):
    return 1 if n in KEPT_TRANSPOSED else SHARD_AXIS[n]


def _stored(n, t):
    return jnp.swapaxes(t, 1, 2) if n in KEPT_TRANSPOSED else t


def _layer_weights(ws, rep, i):
    w = {k: _from_shards(ws[k][:, i], _stored_axis(k) - 1) for k in MISC_SHARDED}
    for k in ('ffn1_norm', 'mix_norm', 'ffn2_norm', 's5_d', 'gla_norm', 'attn_q_norm', 'attn_k_norm', 'b_merge_gate'):
        w[k] = rep[k][i].reshape(1, -1)
    w['w_in_p'] = _pad_w_in(w['w_in'])
    s5_raw = [tuple(rep[k][i, d] for k in ('s5_lambda_re', 's5_lambda_im', 's5_log_dt', 's5_b_re', 's5_b_im',
                                            's5_c_re', 's5_c_im')) for d in range(2)]
    w['s5'] = [_s5_prep(*s5_raw[d]) for d in range(2)]
    return w, s5_raw


def _device_step(x, target, ws, rep):
    lw = [_layer_weights(ws, rep, i) for i in range(DEPTH)]
    ffn = {p: (ws[p + 'w_gate'], ws[p + 'w_up'], ws[p + 'w_down']) for p in ('ffn1_', 'ffn2_')}
    saved = []
    for i in range(DEPTH):
        w, _ = lw[i]
        x, s1 = _ffn_fwd(x, w['ffn1_norm'], *ffn['ffn1_'], i)
        x, s2 = _mixer_fwd(x, w)
        x, s3 = _ffn_fwd(x, w['ffn2_norm'], *ffn['ffn2_'], i)
        saved.append((s1, s2, s3))
    gfin = rep['final_norm'].reshape(1, -1)

    def head(xv, tv, gv):
        e = _rms(xv, gv) - tv
        dx, dg = _rms_bwd(xv, gv, e * (1.0 / D_MODEL))
        part = jnp.sum(e * e, axis=0, keepdims=True)
        return dx, dx, dg, part

    dx, dx16, dgfin, part = _rowwise("loss_head", head, [x, target], [gfin], [(D_MODEL, F32), (D_MODEL, BF16)],
                                     accs=[(1, D_MODEL), (1, D_MODEL)])
    loss = (0.5 / D_MODEL) * jnp.sum(part)
    per_layer = []
    dffn = {'ffn1_': None, 'ffn2_': None}
    for i in reversed(range(DEPTH)):
        w, s5_raw = lw[i]
        s1, s2, s3 = saved[i]
        dx, dx16, n3, dffn['ffn2_'] = _ffn_bwd(dx, dx16, s3, w['ffn2_norm'], *ffn['ffn2_'], i, dffn['ffn2_'])
        dx, dx16, g = _mixer_bwd(dx, dx16, s2, w, s5_raw)
        dx, dx16, n1, dffn['ffn1_'] = _ffn_bwd(dx, dx16, s1, w['ffn1_norm'], *ffn['ffn1_'], i, dffn['ffn1_'])
        g['ffn2_norm'], g['ffn1_norm'] = n3, n1
        per_layer.append(g)
    per_layer.reverse()
    stacked = {k: jnp.stack([per_layer[i][k] for i in range(DEPTH)]) for k in per_layer[0]}
    gs = {k: _to_shards(stacked[k], _stored_axis(k)) for k in MISC_SHARDED}
    for p in ('ffn1_', 'ffn2_'):
        gs[p + 'w_gate'], gs[p + 'w_up'], gs[p + 'w_down'] = dffn[p]
    grep = {k: stacked[k] for k in REPLICATED if k != 'final_norm'}
    grep['final_norm'] = dgfin[0]
    return loss, dx, gs, grep


PACK_C = 1024
ROW_TILE = 512


def _rows_of(shape):
    return -(-int(np.prod(shape)) // PACK_C)


def _as_rows(t, lead=0):
    head = t.shape[:lead]
    flat = t.reshape(head + (-1,))
    rows = -(-flat.shape[-1] // PACK_C)
    pad = rows * PACK_C - flat.shape[-1]
    if pad:
        flat = jnp.pad(flat, [(0, 0)] * lead + [(0, pad)])
    return flat.reshape(head + (rows, PACK_C))


def _pack(items, total_rows, lead=0):
    parts = [_as_rows(t, lead) for t in items]
    used = sum(p.shape[lead] for p in parts)
    if total_rows > used:
        parts.append(jnp.zeros(parts[0].shape[:lead] + (total_rows - used, PACK_C), parts[0].dtype))
    return jnp.concatenate(parts, axis=lead)


def _unpack(rows, shapes, lead=0):
    out, r0 = [], 0
    head = rows.shape[:lead]
    for shp in shapes:
        n, size = _rows_of(shp), int(np.prod(shp))
        piece = lax.slice_in_dim(rows, r0, r0 + n, axis=lead).reshape(head + (n * PACK_C,))
        out.append(lax.slice_in_dim(piece, 0, size, axis=lead).reshape(head + tuple(shp)))
        r0 += n
    return out


def _round_up(n, m):
    return -(-n // m) * m


def _to_shards(g, axis):
    shp = g.shape
    g = g.reshape(shp[:axis] + (N_CHIPS, shp[axis] // N_CHIPS) + shp[axis + 1:])
    return jnp.moveaxis(g, axis, 0)


def _from_shards(s, axis):
    s = jnp.moveaxis(s, 0, axis)
    shp = s.shape
    return s.reshape(shp[:axis] + (shp[axis] * shp[axis + 1],) + shp[axis + 2:])


_HBM = pl.BlockSpec(memory_space=pltpu.HBM)


def _place():
    return lax.axis_index("x"), lax.axis_index("y"), lax.axis_index("c")


def _rcopy(src, dst, send, recv, k, dev):
    return pltpu.make_async_remote_copy(src_ref=src, dst_ref=dst, send_sem=send.at[k], recv_sem=recv.at[k],
                                        device_id=dev, device_id_type=MESH)


def _comm_call(name, body, arrays, out_shapes, n_sems, n_local):
    return pl.pallas_call(
        body, name=name, in_specs=[_HBM] * len(arrays), out_specs=[_HBM] * len(out_shapes), out_shape=out_shapes,
        scratch_shapes=[pltpu.SemaphoreType.DMA((n_sems,)), pltpu.SemaphoreType.DMA((n_sems,)),
                        pltpu.SemaphoreType.DMA((max(n_local, 1),))],
    )(*arrays)


def _ici_peers(x, y, c):
    x_first = c == 0
    first = (jnp.where(x_first, 1 - x, x), jnp.where(x_first, y, 1 - y))
    second = (jnp.where(x_first, x, 1 - x), jnp.where(x_first, 1 - y, y))
    return first, second


def _ag_chips(name, packs):
    n = len(packs)

    def body(*refs):
        p, o, (send, recv, _) = refs[:n], refs[n:2 * n], refs[2 * n:]
        x, y, c = _place()
        (ax, ay), (bx, by) = _ici_peers(x, y, c)
        me, na, nb, nd = 2 * x + y, 2 * ax + ay, 2 * bx + by, 2 * (1 - x) + (1 - y)
        sib = (x, y, 1 - c)
        own = [_rcopy(p[a].at[c], o[a].at[me, c], send, recv, 6 * a, (ax, ay, c)) for a in range(n)]
        own += [_rcopy(p[a].at[c], o[a].at[me, c], send, recv, 6 * a + 1, (bx, by, c)) for a in range(n)]
        for cp in own:
            cp.start()
        passed = []

        def landed(a, chip, k, fwd):
            blk = o[a].at[chip, c]
            _rcopy(blk, blk, send, recv, 6 * a + k, sib).wait_recv()
            for sem, dev in fwd:
                cp = _rcopy(blk, blk, send, recv, 6 * a + sem, dev)
                cp.start()
                passed.append(cp)

        for a in range(n):
            landed(a, na, 0, [(2, (bx, by, c)), (3, sib)])
        for a in range(n):
            landed(a, nb, 1, [(4, sib)])
        for a in range(n):
            landed(a, nd, 2, [(5, sib)])
        for a in range(n):
            for k, chip in ((3, nb), (4, na), (5, nd)):
                blk = o[a].at[chip, 1 - c]
                _rcopy(blk, blk, send, recv, 6 * a + k, sib).wait_recv()
        for cp in own + passed:
            cp.wait_send()

    outs = _comm_call(name, body, packs, [jax.ShapeDtypeStruct((N_CHIPS,) + t.shape, t.dtype) for t in packs], 6 * n, 0)
    chip = 2 * lax.axis_index("x") + lax.axis_index("y")
    return [lax.dynamic_update_slice_in_dim(o, t[None], chip, axis=0) for o, t in zip(outs, packs)]


def _pair_swap(name, gs):
    n = len(gs)
    ns = gs[0].shape[0]

    def body(*refs):
        g, r, (send, recv, _) = refs[:n], refs[n:2 * n], refs[2 * n:]
        x, y, c = _place()
        cps = [_rcopy(g[a].at[j, 1 - c], r[a].at[j], send, recv, ns * a + j, (x, y, 1 - c))
               for a in range(n) for j in range(ns)]
        for cp in cps:
            cp.start()
        for cp in cps:
            cp.wait_recv()
        for cp in cps:
            cp.wait_send()

    return _comm_call(name, body, gs, [jax.ShapeDtypeStruct((ns,) + t.shape[2:], t.dtype) for t in gs], ns * n, 0)


def _halve_first(name, parts):
    n = len(parts)

    def body(*refs):
        p, r, (send, recv, _) = refs[:n], refs[n:2 * n], refs[2 * n:]
        x, y, c = _place()
        (ax, ay), _ = _ici_peers(x, y, c)
        cps = [_rcopy(p[a].at[jnp.where(c == 0, 2 * ax + k, 2 * k + ay)], r[a].at[k], send, recv, 2 * a + k, (ax, ay, c))
               for a in range(n) for k in range(2)]
        for cp in cps:
            cp.start()
        for cp in cps:
            cp.wait_recv()
        for cp in cps:
            cp.wait_send()

    return _comm_call(name, body, parts, [jax.ShapeDtypeStruct((2,) + t.shape[1:], t.dtype) for t in parts], 2 * n, 0)


def _halve_second(name, kept):
    n = len(kept)

    def body(*refs):
        p, r, (send, recv, _) = refs[:n], refs[n:2 * n], refs[2 * n:]
        x, y, c = _place()
        _, (bx, by) = _ici_peers(x, y, c)
        cps = [_rcopy(p[a].at[jnp.where(c == 0, by, bx)], r[a], send, recv, a, (bx, by, c)) for a in range(n)]
        for cp in cps:
            cp.start()
        for cp in cps:
            cp.wait_recv()
        for cp in cps:
            cp.wait_send()

    return _comm_call(name, body, kept, [jax.ShapeDtypeStruct(t.shape[1:], t.dtype) for t in kept], n, 0)


def _add_selected(name, t, r, index, out_dtype):
    single = r.ndim == 2
    r3 = r[None] if single else r
    nk, rh, cc = r3.shape
    tl = _tile(rh, ROW_TILE, 16)

    def body(i_ref, t_ref, r_ref, o_ref):
        o_ref[...] = (t_ref[...].astype(F32) + r_ref[...].astype(F32)).astype(o_ref.dtype)

    out = _pcall(
        body, name=name,
        grid_spec=pltpu.PrefetchScalarGridSpec(
            num_scalar_prefetch=1, grid=(nk, rh // tl),
            in_specs=[pl.BlockSpec((1, tl, cc), lambda k, i, i_ref: (i_ref[k], i, 0)),
                      pl.BlockSpec((1, tl, cc), lambda k, i, i_ref: (k, i, 0))],
            out_specs=pl.BlockSpec((1, tl, cc), lambda k, i, i_ref: (k, i, 0))),
        out_shape=jax.ShapeDtypeStruct(r3.shape, out_dtype),
        compiler_params=_cparams(("parallel", "parallel"), 8 * tl * cc * 4),
    )(index, t, r3)
    return out[0] if single else out


def _pair_share(name, reds):
    n = len(reds)

    def body(*refs):
        r, o, (send, recv, _) = refs[:n], refs[n:2 * n], refs[2 * n:]
        x, y, c = _place()
        cps = [_rcopy(r[a], o[a].at[c], send, recv, a, (x, y, 1 - c)) for a in range(n)]
        for cp in cps:
            cp.start()
        for a in range(n):
            blk = o[a].at[1 - c]
            _rcopy(blk, blk, send, recv, a, (x, y, 1 - c)).wait_recv()
        for cp in cps:
            cp.wait_send()

    outs = _comm_call(name, body, reds, [jax.ShapeDtypeStruct((2,) + t.shape, t.dtype) for t in reds], n, 0)
    core = lax.axis_index("c")
    return [lax.dynamic_update_slice_in_dim(o, t[None], core, axis=0) for o, t in zip(outs, reds)]


def _add_own_half(name, g, r):
    ns, _, rh, cc = g.shape
    tl = _tile(rh, ROW_TILE, 16)
    core = lax.axis_index("c").astype(jnp.int32).reshape(1)

    def body(c_ref, g_ref, r_ref, o_ref):
        o_ref[...] = (g_ref[0].astype(F32) + r_ref[...].astype(F32)).astype(o_ref.dtype)

    return _pcall(
        body, name=name,
        grid_spec=pltpu.PrefetchScalarGridSpec(
            num_scalar_prefetch=1, grid=(ns, rh // tl),
            in_specs=[pl.BlockSpec((1, 1, tl, cc), lambda j, i, c_ref: (j, c_ref[0], i, 0)),
                      pl.BlockSpec((1, tl, cc), lambda j, i, c_ref: (j, i, 0))],
            out_specs=pl.BlockSpec((1, tl, cc), lambda j, i, c_ref: (j, i, 0))),
        out_shape=jax.ShapeDtypeStruct(r.shape, BF16),
        compiler_params=_cparams(("parallel", "parallel"), 8 * tl * cc * 4),
    )(core, g, r)


def _reduce_scatter(gs):
    x, y, c = lax.axis_index("x"), lax.axis_index("y"), lax.axis_index("c")
    mine_first = jnp.stack([jnp.where(c == 0, 2 * x + k, 2 * k + y) for k in range(2)]).astype(jnp.int32)
    mine_second = jnp.where(c == 0, y, x).astype(jnp.int32).reshape(1)
    swapped = _pair_swap("rs_pair_swap", gs)
    parts = [_add_own_half("rs_add_pair", g, r) for g, r in zip(gs, swapped)]
    got = _halve_first("rs_halve_first", parts)
    kept = [_add_selected("rs_add_first", t, r, mine_first, BF16) for t, r in zip(parts, got)]
    got = _halve_second("rs_halve_second", kept)
    reds = [_add_selected("rs_add_second", t, r, mine_second, F32) for t, r in zip(kept, got)]
    return _pair_share("rs_pair_share", reds)


def _adamw(name, g, w, m, v):
    c1 = 1.0 - ADAM_B1 ** ADAM_STEP
    c2 = 1.0 - ADAM_B2 ** ADAM_STEP
    nb, rows, cc = g.shape
    tl = _tile(rows, ROW_TILE, SUBLANE)

    def body(g_ref, w_ref, m_ref, v_ref, d_ref, mo_ref, vo_ref):
        gv = g_ref[...]
        mn = ADAM_B1 * m_ref[...] + (1.0 - ADAM_B1) * gv
        vn = ADAM_B2 * v_ref[...] + (1.0 - ADAM_B2) * (gv * gv)
        d_ref[...] = -ADAM_LR * ((mn / c1) / (jnp.sqrt(vn / c2) + ADAM_EPS) + ADAM_WD * w_ref[...])
        mo_ref[...] = mn
        vo_ref[...] = vn

    spec = pl.BlockSpec((1, tl, cc), lambda b, i: (b, i, 0))
    return pl.pallas_call(
        body, name=name, grid=(nb, rows // tl), in_specs=[spec] * 4, out_specs=[spec] * 3,
        out_shape=[jax.ShapeDtypeStruct(g.shape, F32)] * 3,
        compiler_params=_cparams(("parallel", "parallel"), 16 * tl * cc * 4),
    )(g, w, m, v)


def _train_step(a):
    x, target = a['x'][0], a['loss_target'][0]

    names = BF16_GATHER + F32_GATHER
    got = _ag_chips("ag_weights", [_stored(n, a[n]).astype(BF16) for n in BF16_GATHER] + [a[n] for n in F32_GATHER])
    ws = dict(zip(names, got))

    loss, gx, gs, grep = _device_step(x, target, ws, {n: a[n] for n in REPLICATED})
    loss = lax.psum(loss, ("x", "y", "c"))

    small = [n for n in MISC_SHARDED if n not in LARGE_SHARDED]
    small_shapes = [a[n].shape[1:] for n in small]
    rep_shapes = [a[n].shape for n in REPLICATED]
    small_rows = sum(_rows_of(s) for s in small_shapes)
    piece = _round_up(-(-sum(_rows_of(s) for s in rep_shapes) // (2 * N_CHIPS)), SUBLANE)
    rep_rows = 2 * N_CHIPS * piece
    rep_pack = _pack([grep[n] for n in REPLICATED], rep_rows).reshape(N_CHIPS, 2, piece, PACK_C)
    total = _round_up(small_rows + piece, ROW_TILE // 2)
    misc = _pack([gs[n] for n in small] + [rep_pack], total, lead=2)
    reds = _reduce_scatter([gs[n] for n in LARGE_SHARDED] + [misc])
    grad = dict(zip(LARGE_SHARDED, reds[:-1]))
    grad.update(zip(small, _unpack(reds[-1], small_shapes, lead=1)))
    rep_mine = reds[-1][:, small_rows:small_rows + piece]
    rep_all = _ag_chips("ag_replicated", [rep_mine])[0].reshape(1, rep_rows, PACK_C)

    outs = {}
    for n in SHARDED:
        shp = grad[n].shape
        three_d = (shp[0], -1, shp[-1])
        params = (grad[n], *(_stored(n, a[p + n]) for p in ('', 'm_', 'v_')))
        res3 = _adamw("adamw", *(t.reshape(three_d) for t in params))
        for kind, arr in zip(('grad', 'delta', 'new_m', 'new_v'), (grad[n], *res3)):
            outs[kind + '_' + n] = _stored(n, arr.reshape(shp))
    w, m, v = (_pack([a[p + n] for n in REPLICATED], rep_rows)[None] for p in ('', 'm_', 'v_'))
    res3 = _adamw("adamw_replicated", rep_all, w, m, v)
    for kind, arr in zip(('grad', 'delta', 'new_m', 'new_v'), (rep_all, *res3)):
        for n, t in zip(REPLICATED, _unpack(arr[0], rep_shapes)):
            outs[kind + '_' + n] = t
    res = [loss, gx[None]]
    for kind in ('grad', 'delta', 'new_m', 'new_v'):
        res += [outs[kind + '_' + n] for n in W_NAMES]
    return tuple(res)


def kernel(x, ffn1_norm, ffn1_w_gate, ffn1_w_up, ffn1_w_down, mix_norm, w_in, s5_lambda_re, s5_lambda_im, s5_log_dt, s5_b_re, s5_b_im, s5_c_re, s5_c_im, s5_d, s5_w_glu, gla_w_alpha, gla_b_alpha, gla_norm, attn_q_norm, attn_k_norm, w_branch_s5, w_branch_gla, w_branch_attn, w_merge_gate, b_merge_gate, w_out, ffn2_norm, ffn2_w_gate, ffn2_w_up, ffn2_w_down, final_norm, loss_target, m_ffn1_norm, m_ffn1_w_gate, m_ffn1_w_up, m_ffn1_w_down, m_mix_norm, m_w_in, m_s5_lambda_re, m_s5_lambda_im, m_s5_log_dt, m_s5_b_re, m_s5_b_im, m_s5_c_re, m_s5_c_im, m_s5_d, m_s5_w_glu, m_gla_w_alpha, m_gla_b_alpha, m_gla_norm, m_attn_q_norm, m_attn_k_norm, m_w_branch_s5, m_w_branch_gla, m_w_branch_attn, m_w_merge_gate, m_b_merge_gate, m_w_out, m_ffn2_norm, m_ffn2_w_gate, m_ffn2_w_up, m_ffn2_w_down, m_final_norm, v_ffn1_norm, v_ffn1_w_gate, v_ffn1_w_up, v_ffn1_w_down, v_mix_norm, v_w_in, v_s5_lambda_re, v_s5_lambda_im, v_s5_log_dt, v_s5_b_re, v_s5_b_im, v_s5_c_re, v_s5_c_im, v_s5_d, v_s5_w_glu, v_gla_w_alpha, v_gla_b_alpha, v_gla_norm, v_attn_q_norm, v_attn_k_norm, v_w_branch_s5, v_w_branch_gla, v_w_branch_attn, v_w_merge_gate, v_b_merge_gate, v_w_out, v_ffn2_norm, v_ffn2_w_gate, v_ffn2_w_up, v_ffn2_w_down, v_final_norm):
    return _train_step(dict(locals()))
```

```python
import functools
import math

import jax
import jax.numpy as jnp
import numpy as np
from jax import lax
from jax.experimental import pallas as pl
from jax.experimental.pallas import tpu as pltpu

F32 = jnp.float32
BF16 = jnp.bfloat16

D_MODEL = 1024
DEPTH = 2
EPS = 1e-6
S5_G, S5_H, S5_P = 32, 16, 64
S5_W = S5_G * S5_H
S5_N = S5_G * S5_P
S5_TG = 8
S5_TILES = S5_G // S5_TG
S5_TC, S5_TS = S5_TG * S5_H, S5_TG * S5_P
GLA_H, GLA_D = 4, 128
GLA_W = GLA_H * GLA_D
GLA_R = 16
GLA_TAU = 16.0
GLA_C = 64
AT_QH, AT_KH, AT_D = 8, 2, 64
AT_W = AT_QH * AT_D
AT_KW = AT_KH * AT_D
GRID_W = 64
ROPE_BASE = 10000.0
ZP_W = 3456
ZC_AK, ZC_AV, ZC_Z = 24, 25, 26

ADAM_LR, ADAM_B1, ADAM_B2, ADAM_EPS, ADAM_WD, ADAM_STEP = 0.001, 0.9, 0.999, 1e-08, 0.01, 10

GRAD_DT = BF16
RESIDENT_WEIGHT_BYTES = 4 * 1024 * 1024
VMEM_LIMIT_V7X = 56 * 1024 * 1024
LANE = 128
SUBLANE = 8

W_NAMES = ['ffn1_norm', 'ffn1_w_gate', 'ffn1_w_up', 'ffn1_w_down', 'mix_norm', 'w_in', 's5_lambda_re', 's5_lambda_im',
           's5_log_dt', 's5_b_re', 's5_b_im', 's5_c_re', 's5_c_im', 's5_d', 's5_w_glu', 'gla_w_alpha', 'gla_b_alpha',
           'gla_norm', 'attn_q_norm', 'attn_k_norm', 'w_branch_s5', 'w_branch_gla', 'w_branch_attn', 'w_merge_gate',
           'b_merge_gate', 'w_out', 'ffn2_norm', 'ffn2_w_gate', 'ffn2_w_up', 'ffn2_w_down', 'final_norm']
SHARD_AXIS = {'ffn1_w_gate': 2, 'ffn1_w_up': 2, 'ffn1_w_down': 1, 'w_in': 2, 's5_w_glu': 1, 'gla_w_alpha': 3,
              'gla_b_alpha': 2, 'w_branch_s5': 2, 'w_branch_gla': 2, 'w_branch_attn': 2, 'w_merge_gate': 2,
              'w_out': 1, 'ffn2_w_gate': 2, 'ffn2_w_up': 2, 'ffn2_w_down': 1}
F32_GATHER = ['gla_w_alpha', 'gla_b_alpha']
BF16_GATHER = [n for n in W_NAMES if n in SHARD_AXIS and n not in F32_GATHER]
SHARDED = [n for n in W_NAMES if n in SHARD_AXIS]
REPLICATED = [n for n in W_NAMES if n not in SHARD_AXIS]
N_CHIPS = 4
MESH = pl.DeviceIdType.MESH


def _tile(n, target, mult):
    best = None
    for t in range(mult, min(n, target) + 1, mult):
        if n % t == 0:
            best = t
    return best if best is not None else n


def _cparams(sem, vmem_bytes):
    limit = int(min(VMEM_LIMIT_V7X, max(32 * 1024 * 1024, vmem_bytes * 5 // 4)))
    return pltpu.CompilerParams(dimension_semantics=sem, vmem_limit_bytes=limit)


def _pcall(body, **kw):
    shapes = kw.pop('out_shape')
    many = isinstance(shapes, (list, tuple))
    pinned = [pltpu.HBM(t.shape, t.dtype) for t in (shapes if many else [shapes])]
    call = pl.pallas_call(body, out_shape=pinned if many else pinned[0], **kw)

    def run(*ops):
        return call(*[o if jnp.issubdtype(o.dtype, jnp.integer) else pltpu.with_memory_space_constraint(o, pltpu.HBM)
                      for o in ops])

    return run


def _cw(arr, cb, width):
    return (arr, cb, width)


def _win(a):
    if isinstance(a, tuple):
        return a
    return (a, 0, a.shape[-1])


def _nbytes(shape, dtype):
    return int(np.prod(shape)) * jnp.dtype(dtype).itemsize


def _rowwise(name, fn, rows, fulls, outs, accs=(), tl=256):
    rows = [_win(r) for r in rows]
    n_tok = rows[0][0].shape[0]
    tl = _tile(n_tok, tl, 16)
    nr, nf, no = len(rows), len(fulls), len(outs)

    def body(*refs):
        vals = [r[...] for r in refs[:nr + nf]]
        res = fn(*vals)
        if not isinstance(res, (tuple, list)):
            res = (res,)
        for k in range(no):
            o = refs[nr + nf + k]
            o[...] = res[k].astype(o.dtype)
        if accs:
            @pl.when(pl.program_id(0) == 0)
            def _():
                for k in range(len(accs)):
                    a = refs[nr + nf + no + k]
                    a[...] = jnp.zeros(a.shape, a.dtype)
            for k in range(len(accs)):
                a = refs[nr + nf + no + k]
                a[...] += res[no + k]

    in_specs = [pl.BlockSpec((tl, w), functools.partial(lambda i, cb: (i, cb), cb=cb)) for (_, cb, w) in rows]
    in_specs += [pl.BlockSpec(f.shape, functools.partial(lambda i, nd: (0,) * nd, nd=f.ndim)) for f in fulls]
    out_specs = [pl.BlockSpec((tl, c), lambda i: (i, 0)) for (c, _) in outs]
    out_specs += [pl.BlockSpec((r, c), lambda i: (0, 0)) for (r, c) in accs]
    out_shape = [jax.ShapeDtypeStruct((n_tok, c), dt) for (c, dt) in outs]
    out_shape += [jax.ShapeDtypeStruct((r, c), F32) for (r, c) in accs]
    vm = 2 * sum(_nbytes((tl, w), a.dtype) for (a, _, w) in rows)
    vm += 2 * sum(_nbytes(f.shape, f.dtype) for f in fulls)
    vm += 2 * sum(_nbytes((tl, c), dt) for (c, dt) in outs)
    vm += 8 * max([_nbytes((tl, w), F32) for (_, _, w) in rows] + [_nbytes((tl, c), F32) for (c, _) in outs])
    return _pcall(
        body, name=name, grid=(n_tok // tl,), in_specs=in_specs, out_specs=out_specs, out_shape=out_shape,
        compiler_params=_cparams(("arbitrary",) if accs else ("parallel",), vm),
    )(*[r[0] for r in rows], *fulls)


def _mm(name, a, b, *, ta=False, tb=False, out_dtype=F32, alpha=1.0, add=None, tm=512, tn=512):
    a_arr, a_cb, a_w = _win(a)
    b_arr, b_cb, b_w = _win(b)
    if ta:
        kdim, m = a_arr.shape[0], a_w
    else:
        m, kdim = a_arr.shape[0], a_w
    if tb:
        n, kb = b_arr.shape[0], b_w
    else:
        kb, n = b_arr.shape[0], b_w
    assert kdim == kb, (name, kdim, kb)
    if not ta and kdim <= 1024:
        tm = 2 * tm
    tm = _tile(m, tm, LANE if ta else 16)
    if not ta and _nbytes((kdim, n), b_arr.dtype) <= RESIDENT_WEIGHT_BYTES:
        tn = n
    tn = _tile(n, tn, LANE)
    if ta:
        a_spec = pl.BlockSpec((kdim, tm), lambda i, j: (0, a_cb * (a_w // tm) + i))
        a_blk = (kdim, tm)
    else:
        a_spec = pl.BlockSpec((tm, kdim), lambda i, j: (i, a_cb))
        a_blk = (tm, kdim)
    if tb:
        b_spec = pl.BlockSpec((tn, kdim), lambda i, j: (j, b_cb))
        b_blk = (tn, kdim)
    else:
        b_spec = pl.BlockSpec((kdim, tn), lambda i, j: (0, b_cb * (b_w // tn) + j))
        b_blk = (kdim, tn)
    dims = (((0 if ta else 1,), (1 if tb else 0,)), ((), ()))
    has_add = add is not None

    def body(*refs):
        a_ref, b_ref = refs[0], refs[1]
        o_ref = refs[-1]
        acc = lax.dot_general(a_ref[...].astype(BF16), b_ref[...].astype(BF16), dims, preferred_element_type=F32)
        if alpha != 1.0:
            acc = acc * alpha
        if has_add:
            acc = acc + refs[2][...].astype(F32)
        o_ref[...] = acc.astype(o_ref.dtype)

    in_specs = [a_spec, b_spec]
    ops = [a_arr, b_arr]
    if has_add:
        in_specs.append(pl.BlockSpec((tm, tn), lambda i, j: (i, j)))
        ops.append(add)
    vm = 2 * (_nbytes(a_blk, a_arr.dtype) + _nbytes(b_blk, b_arr.dtype) + _nbytes((tm, tn), out_dtype))
    vm += _nbytes(a_blk, BF16) + _nbytes(b_blk, BF16) + 3 * _nbytes((tm, tn), F32)
    return _pcall(
        body, name=name, grid=(m // tm, n // tn), in_specs=in_specs,
        out_specs=pl.BlockSpec((tm, tn), lambda i, j: (i, j)),
        out_shape=jax.ShapeDtypeStruct((m, n), out_dtype),
        compiler_params=_cparams(("parallel", "parallel"), vm),
    )(*ops)


def _sigmoid(x):
    return 0.5 * jnp.tanh(0.5 * x) + 0.5


def _rms(x, gain):
    r = lax.rsqrt(jnp.mean(x * x, axis=-1, keepdims=True) + EPS)
    return x * r * gain


def _rms_bwd(x, gain, dh):
    r = lax.rsqrt(jnp.mean(x * x, axis=-1, keepdims=True) + EPS)
    xh = x * r
    dg = jnp.sum(dh * xh, axis=0, keepdims=True)
    dxh = dh * gain
    dx = r * (dxh - xh * jnp.mean(dxh * xh, axis=-1, keepdims=True))
    return dx, dg


_GELU_C = math.sqrt(2.0 / math.pi)


def _gelu(y):
    return 0.5 * y * (1.0 + jnp.tanh(_GELU_C * (y + 0.044715 * y * y * y)))


def _gelu_grad(y):
    th = jnp.tanh(_GELU_C * (y + 0.044715 * y * y * y))
    return 0.5 * (1.0 + th) + 0.5 * y * (1.0 - th * th) * _GELU_C * (1.0 + 3.0 * 0.044715 * y * y)


def _log_sigmoid(x):
    return jnp.minimum(x, 0.0) - jnp.log(1.0 + jnp.exp(-jnp.abs(x)))


def _dot(a, b, dims=(((1,), (0,)), ((), ())), precision=None):
    return lax.dot_general(a, b, dims, preferred_element_type=F32, precision=precision)


def _dot_exact(x, m, dims=(((1,), (0,)), ((), ())), terms=3, m_first=False):
    m16 = m.astype(BF16)
    acc, rest = None, x
    for k in range(terms):
        piece = rest.astype(BF16)
        t = _dot(m16, piece, dims) if m_first else _dot(piece, m16, dims)
        acc = t if acc is None else acc + t
        if k + 1 < terms:
            rest = rest - piece.astype(F32)
    return acc


_NT = (((1,), (1,)), ((), ()))
_TN = (((0,), (0,)), ((), ()))


def _scan_core(b_ref, are_ref, aim_ref, pw, last, of32, *, rev, xf32=None, da_ref=None, lc=512):
    nb, nc = b_ref.shape[0] // SUBLANE, S5_N // lc
    row = lax.broadcasted_iota(jnp.int32, (SUBLANE, lc), 0)

    def cmul(ar, ai, xr, xi):
        return ar * xr - ai * xi, ar * xi + ai * xr

    def hs(c, xr, xi):
        cs = pl.ds(c * lc, lc)
        for k, slot in ((1, 0), (2, 2), (4, 4)):
            ar, ai = pw[slot, :, cs], pw[slot + 1, :, cs]
            if rev:
                sr, si = pltpu.roll(xr, SUBLANE - k, 0), pltpu.roll(xi, SUBLANE - k, 0)
                keep = row < SUBLANE - k
            else:
                sr, si = pltpu.roll(xr, k, 0), pltpu.roll(xi, k, 0)
                keep = row >= k
            sr, si = jnp.where(keep, sr, 0.0), jnp.where(keep, si, 0.0)
            pr, pi = cmul(ar, ai, sr, si)
            xr, xi = xr + pr, xi + pi
        return xr, xi

    @pl.when(pl.program_id(0) == 0)
    def _():
        last[...] = jnp.zeros(last.shape, F32)
        if da_ref is not None:
            da_ref[...] = jnp.zeros(da_ref.shape, F32)
        for c in range(nc):
            cs = pl.ds(c * lc, lc)
            a1r = jnp.broadcast_to(are_ref[:, cs], (SUBLANE, lc))
            a1i = jnp.broadcast_to(aim_ref[:, cs], (SUBLANE, lc))
            a2r, a2i = cmul(a1r, a1i, a1r, a1i)
            a4r, a4i = cmul(a2r, a2i, a2r, a2i)
            for slot, v in enumerate((a1r, a1i, a2r, a2i, a4r, a4i)):
                pw[slot, :, cs] = v
            first = SUBLANE - 1 if rev else 0
            pr, pi = hs(c, jnp.where(row == first, a1r, 0.0), jnp.where(row == first, a1i, 0.0))
            pw[6, :, cs] = pr
            pw[7, :, cs] = pi

    edge = 0 if rev else SUBLANE - 1

    def blk(j, carry):
        jj = (nb - 1 - j) if rev else j
        r0 = pl.multiple_of(jj * SUBLANE, SUBLANE)
        for c in range(nc):
            cre, cim = pl.ds(c * lc, lc), pl.ds(S5_N + c * lc, lc)
            xr, xi = hs(c, b_ref[pl.ds(r0, SUBLANE), cre], b_ref[pl.ds(r0, SUBLANE), cim])
            cr = jnp.broadcast_to(last[edge:edge + 1, cre], (SUBLANE, lc))
            ci = jnp.broadcast_to(last[edge:edge + 1, cim], (SUBLANE, lc))
            pr, pi = cmul(pw[6, :, cre], pw[7, :, cre], cr, ci)
            xr, xi = xr + pr, xi + pi
            if da_ref is not None:
                if rev:
                    qr, qi = pltpu.roll(xr, SUBLANE - 1, 0), pltpu.roll(xi, SUBLANE - 1, 0)
                    fill = row == SUBLANE - 1
                else:
                    qr, qi = pltpu.roll(xr, 1, 0), pltpu.roll(xi, 1, 0)
                    fill = row == 0
                qr, qi = jnp.where(fill, cr, qr), jnp.where(fill, ci, qi)
                zr, zi = xf32[pl.ds(r0, SUBLANE), cre], xf32[pl.ds(r0, SUBLANE), cim]
                da_ref[:, cre] += zr * qr + zi * qi
                da_ref[:, cim] += zr * qi - zi * qr
            last[:, cre] = xr
            last[:, cim] = xi
            of32[pl.ds(r0, SUBLANE), cre] = xr
            of32[pl.ds(r0, SUBLANE), cim] = xi
        return carry

    lax.fori_loop(0, nb, blk, 0)


def _s5_tiles():
    return [(slice((j % S5_TILES) * S5_TC, (j % S5_TILES + 1) * S5_TC), slice(j * S5_TS, (j + 1) * S5_TS))
            for j in range(2 * S5_TILES)]


def _s5_scratch(tb):
    return [pltpu.VMEM((8, SUBLANE, S5_N), F32), pltpu.VMEM((SUBLANE, 2 * S5_N), F32),
            pltpu.VMEM((tb, 2 * S5_N), F32), pltpu.VMEM((tb, 2 * S5_N), F32),
            pltpu.VMEM((S5_TC, 2 * S5_N), BF16), pltpu.VMEM((2 * S5_N, S5_TC), BF16)]


def _s5_fwd(name, z, a_re, a_im, wb, wc, add, *, rev, tb=512):
    n_tok = z.shape[0]
    tb = _tile(n_tok, tb, 16)
    nt = n_tok // tb
    has_add = add is not None

    def body(*refs):
        u_ref, are_ref, aim_ref, wb_ref, wc_ref = refs[:5]
        x_ref, y_ref, pw, last, of32, bu, wb16, wc16 = refs[5 + has_add:]

        @pl.when(pl.program_id(0) == 0)
        def _():
            wb16[...] = wb_ref[...].astype(BF16)
            wc16[...] = wc_ref[...].astype(BF16)

        u16 = u_ref[...].astype(BF16)
        for ch, st in _s5_tiles():
            bu[:, st] = _dot(u16[:, ch], wb16[:, st])
        _scan_core(bu, are_ref, aim_ref, pw, last, of32, rev=rev)
        x16 = of32[...].astype(BF16)
        x_ref[...] = x16
        ys = []
        for k in range(S5_TILES):
            re, im = slice(k * S5_TS, (k + 1) * S5_TS), slice(S5_N + k * S5_TS, S5_N + (k + 1) * S5_TS)
            ys.append(_dot(x16[:, re], wc16[re, :]) + _dot(x16[:, im], wc16[im, :]))
        y = jnp.concatenate(ys, axis=1)
        y_ref[...] = y + refs[5][...] if has_add else y

    tmap = (lambda t: (nt - 1 - t, 0)) if rev else (lambda t: (t, 0))
    const = lambda t: (0, 0)
    in_specs = [pl.BlockSpec((tb, S5_W), tmap), pl.BlockSpec((1, S5_N), const), pl.BlockSpec((1, S5_N), const),
                pl.BlockSpec((S5_TC, 2 * S5_N), const), pl.BlockSpec((2 * S5_N, S5_TC), const)]
    ops = [z, a_re, a_im, wb, wc]
    if has_add:
        in_specs.append(pl.BlockSpec((tb, S5_W), tmap))
        ops.append(add)
    return _pcall(
        body, name=name, grid=(nt,), in_specs=in_specs,
        out_specs=[pl.BlockSpec((tb, 2 * S5_N), tmap), pl.BlockSpec((tb, S5_W), tmap)],
        out_shape=[jax.ShapeDtypeStruct((n_tok, 2 * S5_N), BF16), jax.ShapeDtypeStruct((n_tok, S5_W), F32)],
        scratch_shapes=_s5_scratch(tb), compiler_params=_cparams(("arbitrary",), 10 * tb * 2 * S5_N * 4),
    )(*ops)


def _s5_bwd(name, dy, z, xs, a_re, a_im, wb, wc, add, *, rev, tb=256):
    n_tok = z.shape[0]
    tb = _tile(n_tok, tb, 16)
    nt = n_tok // tb

    def body(dy_ref, u_ref, xs_ref, are_ref, aim_ref, wb_ref, wc_ref, add_ref, du_ref, dwb_ref, dwc_ref, da_ref,
             pw, last, of32, gx, wb16, wc16, xf32):
        @pl.when(pl.program_id(0) == 0)
        def _():
            wb16[...] = wb_ref[...].astype(BF16)
            wc16[...] = wc_ref[...].astype(BF16)
            dwb_ref[...] = jnp.zeros(dwb_ref.shape, F32)
            dwc_ref[...] = jnp.zeros(dwc_ref.shape, F32)

        dy16, u16, x16 = dy_ref[...].astype(BF16), u_ref[...].astype(BF16), xs_ref[...]
        for ch, st in _s5_tiles():
            gx[:, st] = _dot(dy16[:, ch], wc16[st, :], _NT)
        xf32[...] = x16.astype(F32)
        _scan_core(gx, are_ref, aim_ref, pw, last, of32, rev=rev, xf32=xf32, da_ref=da_ref)
        g16 = of32[...].astype(BF16)
        dus = []
        for k in range(S5_TILES):
            re, im = slice(k * S5_TS, (k + 1) * S5_TS), slice(S5_N + k * S5_TS, S5_N + (k + 1) * S5_TS)
            dus.append(_dot(g16[:, re], wb16[:, re], _NT) + _dot(g16[:, im], wb16[:, im], _NT))
        du_ref[...] = jnp.concatenate(dus, axis=1) + add_ref[...]
        for ch, st in _s5_tiles():
            dwb_ref[:, st] += _dot(u16[:, ch], g16[:, st], _TN)
            dwc_ref[st, :] += _dot(x16[:, st], dy16[:, ch], _TN)

    tmap = (lambda t: (nt - 1 - t, 0)) if rev else (lambda t: (t, 0))
    const = lambda t: (0, 0)
    row_w, row_s = pl.BlockSpec((tb, S5_W), tmap), pl.BlockSpec((tb, 2 * S5_N), tmap)
    return _pcall(
        body, name=name, grid=(nt,),
        in_specs=[row_w, row_w, row_s, pl.BlockSpec((1, S5_N), const), pl.BlockSpec((1, S5_N), const),
                  pl.BlockSpec((S5_TC, 2 * S5_N), const), pl.BlockSpec((2 * S5_N, S5_TC), const), row_w],
        out_specs=[row_w, pl.BlockSpec((S5_TC, 2 * S5_N), const), pl.BlockSpec((2 * S5_N, S5_TC), const),
                   pl.BlockSpec((SUBLANE, 2 * S5_N), const)],
        out_shape=[jax.ShapeDtypeStruct((n_tok, S5_W), F32), jax.ShapeDtypeStruct((S5_TC, 2 * S5_N), F32),
                   jax.ShapeDtypeStruct((2 * S5_N, S5_TC), F32), jax.ShapeDtypeStruct((SUBLANE, 2 * S5_N), F32)],
        scratch_shapes=_s5_scratch(tb) + [pltpu.VMEM((tb, 2 * S5_N), F32)],
        compiler_params=_cparams(("arbitrary",), 12 * tb * 2 * S5_N * 4),
    )(dy, z, xs, a_re, a_im, wb, wc, add)


GLA_TB = 512


def _gla_block_terms(la, q, k, rev):
    trif = _chunk_mask(rev).astype(F32)
    chunks = [la[t:t + GLA_C] for t in range(0, la.shape[0], GLA_C)]
    bcum = jnp.concatenate([_dot_exact(c, trif, m_first=True) for c in chunks], axis=0)
    blast = _per_chunk_sum(la)
    eb, enb, ee = jnp.exp(bcum), jnp.exp(-bcum), jnp.exp(blast - bcum)
    return trif, jnp.exp(blast), eb, enb, ee, q * eb, k * enb, k * ee


def _per_chunk_sum(t):
    return jnp.concatenate([jnp.broadcast_to(jnp.sum(t[r:r + GLA_C], axis=0, keepdims=True), (GLA_C, t.shape[1]))
                            for r in range(0, t.shape[0], GLA_C)], axis=0)


def _chunk_mask(rev):
    ri = lax.broadcasted_iota(jnp.int32, (GLA_C, GLA_C), 0)
    ci = lax.broadcasted_iota(jnp.int32, (GLA_C, GLA_C), 1)
    return (ri <= ci) if rev else (ri >= ci)


def _gla_fwd(name, z, la, *, rev):
    n_tok = z.shape[0]
    tb = _tile(n_tok, GLA_TB, GLA_C)
    nblk, per = n_tok // tb, tb // GLA_C
    scale = GLA_D ** -0.5

    def body(q_ref, k_ref, v_ref, la_ref, o_ref, s_ref, st, qd_s, kd_s, ke_s, v_s, dec_s):
        @pl.when(pl.program_id(0) == 0)
        def _():
            st[...] = jnp.zeros(st.shape, F32)

        _, dec, _, _, _, qd, kd, ke = _gla_block_terms(la_ref[...], q_ref[...].astype(F32) * scale,
                                                                 k_ref[...].astype(F32), rev)
        qd_s[...], kd_s[...], ke_s[...] = qd.astype(BF16), kd.astype(BF16), ke.astype(BF16)
        v_s[...] = v_ref[...].astype(BF16)
        dec_s[...] = dec
        tri = _chunk_mask(rev)
        for step in range(per):
            t = per - 1 - step if rev else step
            rows = pl.ds(t * GLA_C, GLA_C)
            for h in range(GLA_H):
                lanes = pl.ds(h * GLA_D, GLA_D)
                qd16, v16 = qd_s[rows, lanes], v_s[rows, lanes]
                sc = jnp.where(tri, _dot(qd16, kd_s[rows, lanes], _NT), 0.0)
                s16 = st[h].astype(BF16)
                s_ref[h, step] = s16
                o_ref[rows, lanes] = _dot(sc.astype(BF16), v16) + _dot(qd16, s16, _NT)
                st[h] = st[h] * dec_s[pl.ds(t * GLA_C, 1), lanes] + _dot(v16, ke_s[rows, lanes], _TN)

    bmap = (lambda n: nblk - 1 - n) if rev else (lambda n: n)

    def zspec(cb):
        return pl.BlockSpec((tb, GLA_W), lambda n: (bmap(n), cb))

    half = pltpu.VMEM((tb, GLA_W), BF16)
    return _pcall(
        body, name=name, grid=(nblk,),
        in_specs=[zspec(1), zspec(2), zspec(3), zspec(0)],
        out_specs=[zspec(0), pl.BlockSpec((GLA_H, per, GLA_D, GLA_D), lambda n: (0, n, 0, 0))],
        out_shape=[jax.ShapeDtypeStruct((n_tok, GLA_W), F32),
                   jax.ShapeDtypeStruct((GLA_H, n_tok // GLA_C, GLA_D, GLA_D), BF16)],
        scratch_shapes=[pltpu.VMEM((GLA_H, GLA_D, GLA_D), F32), half, half, half, half, pltpu.VMEM((tb, GLA_W), F32)],
        compiler_params=_cparams(("arbitrary",), 40 * tb * GLA_W * 4),
    )(z, z, z, la)


def _gla_bwd(name, z, la, s_prev, do, *, rev):
    n_tok = z.shape[0]
    tb = _tile(n_tok, GLA_TB, GLA_C)
    nblk, per = n_tok // tb, tb // GLA_C
    scale = GLA_D ** -0.5

    def body(q_ref, k_ref, v_ref, la_ref, s_ref, do_ref, dq_ref, dk_ref, dv_ref, dla_ref,
             dst, qd_s, kd_s, ke_s, v_s, do_s, dec_s, dqd_s, dkd_s, dke_s, dd_s):
        @pl.when(pl.program_id(0) == 0)
        def _():
            dst[...] = jnp.zeros(dst.shape, F32)

        trif, dec, eb, enb, ee, qd, kd, ke = _gla_block_terms(la_ref[...], q_ref[...].astype(F32) * scale,
                                                                 k_ref[...].astype(F32), rev)
        qd_s[...], kd_s[...], ke_s[...] = qd.astype(BF16), kd.astype(BF16), ke.astype(BF16)
        v_s[...] = v_ref[...].astype(BF16)
        do_s[...] = do_ref[...].astype(BF16)
        dec_s[...] = dec
        tri = _chunk_mask(rev)
        for step in range(per):
            t = step if rev else per - 1 - step
            rows = pl.ds(t * GLA_C, GLA_C)
            for h in range(GLA_H):
                lanes = pl.ds(h * GLA_D, GLA_D)
                qd16, kd16, ke16 = qd_s[rows, lanes], kd_s[rows, lanes], ke_s[rows, lanes]
                v16, do16 = v_s[rows, lanes], do_s[rows, lanes]
                sc16 = jnp.where(tri, _dot(qd16, kd16, _NT), 0.0).astype(BF16)
                dsc16 = jnp.where(tri, _dot(do16, v16, _NT), 0.0).astype(BF16)
                s16 = s_ref[h, per - 1 - step]
                dsn = dst[h]
                dsn16 = dsn.astype(BF16)
                dec_c = dec_s[pl.ds(t * GLA_C, 1), lanes]
                dqd_s[rows, lanes] = _dot(dsc16, kd16) + _dot(do16, s16)
                dkd_s[rows, lanes] = _dot(dsc16, qd16, _TN)
                dke_s[rows, lanes] = _dot(v16, dsn16)
                dv_ref[rows, lanes] = _dot(sc16, do16, _TN) + _dot(ke16, dsn16, _NT)
                ddec = jnp.sum(dsn * s16.astype(F32), axis=0, keepdims=True) * dec_c
                dd_s[rows, lanes] = jnp.broadcast_to(ddec, (GLA_C, GLA_D))
                dst[h] = dsn * dec_c + _dot(do16, qd16, _TN)
        dqd, dkd, dke = dqd_s[...], dkd_s[...], dke_s[...]
        dq_ref[...] = dqd * eb * scale
        dk_ref[...] = dkd * enb + dke * ee
        db = dqd * qd - dkd * kd - dke * ke
        dcum = jnp.concatenate([_dot_exact(db[r:r + GLA_C], trif, _TN, m_first=True) for r in range(0, tb, GLA_C)], axis=0)
        dla_ref[...] = dcum + _per_chunk_sum(dke * ke) + dd_s[...]

    bmap = (lambda n: n) if rev else (lambda n: nblk - 1 - n)

    def zspec(cb):
        return pl.BlockSpec((tb, GLA_W), lambda n: (bmap(n), cb))

    o4 = jax.ShapeDtypeStruct((n_tok, GLA_W), F32)
    half, full = pltpu.VMEM((tb, GLA_W), BF16), pltpu.VMEM((tb, GLA_W), F32)
    return _pcall(
        body, name=name, grid=(nblk,),
        in_specs=[zspec(1), zspec(2), zspec(3), zspec(0),
                  pl.BlockSpec((GLA_H, per, GLA_D, GLA_D), lambda n: (0, nblk - 1 - n, 0, 0)), zspec(0)],
        out_specs=[zspec(0)] * 4, out_shape=[o4] * 4,
        scratch_shapes=[pltpu.VMEM((GLA_H, GLA_D, GLA_D), F32), half, half, half, half, half, full, full, full, full, full],
        compiler_params=_cparams(("arbitrary",), 64 * tb * GLA_W * 4),
    )(z, z, z, la, s_prev, do)


_PAIRS = AT_W // LANE
_PAIRS_PER_KV = _PAIRS // AT_KH


def _attn_fwd(name, q, kd, vd, tq=256):
    n_tok = q.shape[0]
    tq = _tile(n_tok, tq, 16)

    def body(q_ref, k_ref, v_ref, o_ref, l_ref):
        qv, kv, vv = q_ref[...], k_ref[...], v_ref[...]
        low = lax.broadcasted_iota(jnp.int32, qv.shape, 1) < AT_D
        res = []
        for keep in (low, jnp.logical_not(low)):
            s = _dot(jnp.where(keep, qv, jnp.zeros_like(qv)), kv, _NT)
            m = jnp.max(s, axis=-1, keepdims=True)
            p = jnp.exp(s - m)
            den = jnp.sum(p, axis=-1, keepdims=True)
            res.append((_dot(p.astype(BF16), vv) / den, m + jnp.log(den)))
        o_ref[...] = jnp.where(low, res[0][0], res[1][0])
        l_ref[...] = jnp.where(low, res[0][1], res[1][1])

    qspec = pl.BlockSpec((tq, LANE), lambda p, i: (i, p))
    kspec = pl.BlockSpec((n_tok, LANE), lambda p, i: (0, p // _PAIRS_PER_KV))
    vm = 5 * _nbytes((tq, n_tok), F32) + 8 * _nbytes((n_tok, LANE), BF16)
    out = jax.ShapeDtypeStruct((n_tok, AT_W), F32)
    return _pcall(
        body, name=name, grid=(_PAIRS, n_tok // tq), in_specs=[qspec, kspec, kspec], out_specs=[qspec, qspec],
        out_shape=[out, out], compiler_params=_cparams(("parallel", "parallel"), vm),
    )(q, kd, vd)


def _attn_bwd(name, q, kd, vd, o, lse, do, tq=256):
    n_tok = q.shape[0]
    tq = _tile(n_tok, tq, 16)

    def body(q_ref, k_ref, v_ref, o_ref, l_ref, do_ref, dq_ref, dk_ref, dv_ref):
        @pl.when((pl.program_id(1) == 0) & (pl.program_id(2) == 0))
        def _():
            dk_ref[...] = jnp.zeros(dk_ref.shape, F32)
            dv_ref[...] = jnp.zeros(dv_ref.shape, F32)

        qv, kv, vv = q_ref[...], k_ref[...], v_ref[...]
        dof, lv = do_ref[...], l_ref[...]
        do16 = dof.astype(BF16)
        doo = dof * o_ref[...]
        low = lax.broadcasted_iota(jnp.int32, qv.shape, 1) < AT_D
        dq = jnp.zeros(qv.shape, F32)
        for keep in (low, jnp.logical_not(low)):
            qm = jnp.where(keep, qv, jnp.zeros_like(qv))
            dom = jnp.where(keep, do16, jnp.zeros_like(do16))
            lh = jnp.max(jnp.where(keep, lv, -1e30), axis=-1, keepdims=True)
            p = jnp.exp(_dot(qm, kv, _NT) - lh)
            delta = jnp.sum(jnp.where(keep, doo, 0.0), axis=-1, keepdims=True)
            ds16 = (p * (_dot(dom, vv, _NT) - delta)).astype(BF16)
            dq = jnp.where(keep, _dot(ds16, kv), dq)
            dk_ref[...] += _dot(ds16, qm, _TN)
            dv_ref[...] += _dot(p.astype(BF16), dom, _TN)
        dq_ref[...] = dq

    qspec = pl.BlockSpec((tq, LANE), lambda g, j, i: (i, g * _PAIRS_PER_KV + j))
    kspec = pl.BlockSpec((n_tok, LANE), lambda g, j, i: (0, g))
    vm = 7 * _nbytes((tq, n_tok), F32) + 12 * _nbytes((n_tok, LANE), F32)
    dup = jax.ShapeDtypeStruct((n_tok, AT_KH * LANE), F32)
    return _pcall(
        body, name=name, grid=(AT_KH, _PAIRS_PER_KV, n_tok // tq),
        in_specs=[qspec, kspec, kspec, qspec, qspec, qspec], out_specs=[qspec, kspec, kspec],
        out_shape=[jax.ShapeDtypeStruct((n_tok, AT_W), F32), dup, dup],
        compiler_params=_cparams(("parallel", "arbitrary", "arbitrary"), vm),
    )(q, kd, vd, o, lse, do)


def _s5_prep(lam_re, lam_im, log_dt, b_re, b_im, c_re, c_im):
    dt = jnp.exp(log_dt)[:, None]
    er = jnp.exp(lam_re * dt)
    a_re, a_im = er * jnp.cos(lam_im * dt), er * jnp.sin(lam_im * dt)
    nr, ni = a_re - 1.0, a_im
    den = lam_re * lam_re + lam_im * lam_im
    f_re, f_im = (nr * lam_re + ni * lam_im) / den, (ni * lam_re - nr * lam_im) / den
    bb_re = f_re[..., None] * b_re - f_im[..., None] * b_im
    bb_im = f_re[..., None] * b_im + f_im[..., None] * b_re
    def tiles(t, axis):
        _, a, b = t.shape
        eye = jnp.eye(S5_TG, dtype=t.dtype)
        blocks = t.reshape(S5_TILES, S5_TG, a, 1, b) * eye[None, :, None, :, None]
        if axis == 0:
            return blocks.reshape(S5_TILES * S5_TG * a, S5_TG * b)
        return blocks.transpose(1, 2, 0, 3, 4).reshape(S5_TG * a, S5_TILES * S5_TG * b)

    w_in = jnp.concatenate([tiles(bb_re.transpose(0, 2, 1), 1), tiles(bb_im.transpose(0, 2, 1), 1)], axis=1)
    w_out = jnp.concatenate([tiles(c_re.transpose(0, 2, 1), 0), -tiles(c_im.transpose(0, 2, 1), 0)], axis=0)
    return a_re.reshape(1, S5_N), a_im.reshape(1, S5_N), w_in, w_out


def _rope_tables(n_tok):
    quarter = AT_D // 4
    d = np.arange(AT_D)
    e = d % (AT_D // 2)
    inv = (ROPE_BASE ** (-(e % quarter).astype(np.float64) * 2.0 / (AT_D // 2))).astype(np.float32)
    sign = np.where(e < quarter, -1.0, 1.0).astype(np.float32)
    rot = np.zeros((AT_D, AT_D), np.float32)
    rot[np.where(e < quarter, d + quarter, d - quarter), d] = 1.0
    pos = jnp.arange(n_tok, dtype=jnp.int32)
    axis_pos = jnp.where(jnp.asarray(d // (AT_D // 2) == 0)[None, :], (pos // GRID_W)[:, None], (pos % GRID_W)[:, None])
    ang = axis_pos.astype(F32) * jnp.asarray(inv)[None, :]
    return jnp.cos(ang), jnp.sin(ang) * jnp.asarray(sign)[None, :], rot


def _head_consts(n_tok, heads):
    cos, sin, rot = _rope_tables(n_tok)
    eye = np.eye(heads, dtype=np.float32)
    mean = np.kron(eye, np.full((AT_D, AT_D), 1.0 / AT_D, np.float32))
    return jnp.tile(cos, (1, heads)), jnp.tile(sin, (1, heads)), jnp.asarray(mean), jnp.asarray(np.kron(eye, rot))


def _pad_w_in(wt):
    return jnp.concatenate([wt[:2560], wt[2592:3360], wt[2560:2592], jnp.zeros((96, wt.shape[1]), wt.dtype)], axis=0)


def _unpad_w_in(gt):
    return jnp.concatenate([gt[:2560], gt[3328:3360], gt[2560:3328]], axis=0)


def _ffn_fwd(x, g, wg, wu, wd, d):
    n_tok = x.shape[0]
    fq = wd.shape[-2]
    tm = _tile(n_tok, 512, 16)
    w_bytes = N_CHIPS * fq * D_MODEL * 2

    def up(x_ref, g_ref, wg_ref, wu_ref, h_ref, a_ref, b_ref, s_ref):
        hv = _rms(x_ref[...], g_ref[...]).astype(BF16)
        h_ref[...] = hv
        for j in range(N_CHIPS):
            av, bv = _dot(hv, wg_ref[j, 0], _NT), _dot(hv, wu_ref[j, 0], _NT)
            a_ref[j] = av.astype(BF16)
            b_ref[j] = bv.astype(BF16)
            s_ref[j] = (av * _sigmoid(av) * bv).astype(BF16)

    wspec = pl.BlockSpec((N_CHIPS, 1, fq, D_MODEL), lambda i: (0, d, 0, 0))
    aspec = pl.BlockSpec((N_CHIPS, tm, fq), lambda i: (0, i, 0))
    xspec = pl.BlockSpec((tm, D_MODEL), lambda i: (i, 0))
    act = jax.ShapeDtypeStruct((N_CHIPS, n_tok, fq), BF16)
    h, a, b, s = _pcall(
        up, name="ffn_up", grid=(n_tok // tm,),
        in_specs=[xspec, pl.BlockSpec((1, D_MODEL), lambda i: (0, 0)), wspec, wspec],
        out_specs=[xspec, aspec, aspec, aspec],
        out_shape=[jax.ShapeDtypeStruct((n_tok, D_MODEL), BF16), act, act, act],
        compiler_params=_cparams(("parallel",), 4 * w_bytes + 6 * N_CHIPS * tm * fq * 2 + 12 * tm * fq * 4
                                 + 8 * tm * D_MODEL * 4),
    )(x, g, wg, wu)

    def down(s_ref, w_ref, x_ref, o_ref):
        acc = _dot(s_ref[0], w_ref[0, 0])
        for j in range(1, N_CHIPS):
            acc = acc + _dot(s_ref[j], w_ref[j, 0])
        o_ref[...] = x_ref[...] + 0.5 * acc

    xo = _pcall(
        down, name="ffn_down", grid=(n_tok // tm,), in_specs=[aspec, wspec, xspec], out_specs=xspec,
        out_shape=jax.ShapeDtypeStruct((n_tok, D_MODEL), F32),
        compiler_params=_cparams(("parallel",), 2 * w_bytes + 2 * N_CHIPS * tm * fq * 2 + 8 * tm * D_MODEL * 4),
    )(s, wd, x)
    return xo, (x, h, a, b, s)


def _ffn_bwd(dy, dy16, saved, g, wg, wu, wd, d, into):
    x, h, a, b, s = saved
    n_tok = x.shape[0]
    fq = wd.shape[-2]
    tm = _tile(n_tok, 512, 16)

    ta_ = _tile(n_tok, 512, 16)

    def dact(dy_ref, w_ref, a_ref, b_ref, da_ref, db_ref):
        dyv = dy_ref[...]
        for j in range(N_CHIPS):
            ds = 0.5 * _dot(dyv, w_ref[j, 0], _NT)
            av, bv = a_ref[j].astype(F32), b_ref[j].astype(F32)
            sg = _sigmoid(av)
            da_ref[j] = (ds * bv * sg * (1.0 + av * (1.0 - sg))).astype(BF16)
            db_ref[j] = (ds * av * sg).astype(BF16)

    aspec = pl.BlockSpec((N_CHIPS, ta_, fq), lambda i: (0, i, 0))
    act = jax.ShapeDtypeStruct((N_CHIPS, n_tok, fq), BF16)
    da, db = _pcall(
        dact, name="ffn_dact", grid=(n_tok // ta_,),
        in_specs=[pl.BlockSpec((ta_, D_MODEL), lambda i: (i, 0)),
                  pl.BlockSpec((N_CHIPS, 1, fq, D_MODEL), lambda i: (0, d, 0, 0)), aspec, aspec],
        out_specs=[aspec, aspec], out_shape=[act, act],
        compiler_params=_cparams(("parallel",), 2 * N_CHIPS * fq * D_MODEL * 2 + 8 * N_CHIPS * ta_ * fq * 2
                                 + 12 * ta_ * fq * 4),
    )(dy16, wd, a, b)

    any_spec = pl.BlockSpec(memory_space=pl.ANY)
    alias = {} if into is None else {2: 0}
    tn = _tile(D_MODEL, 512, LANE)

    def dw_call(t, rhs, scale, prev):
        def dw(t_ref, r_ref, *rest):
            rest[-1][0, 0] = (scale * _dot(t_ref[0], r_ref[...], _TN)).astype(GRAD_DT)

        return _pcall(
            dw, name="ffn_dw", grid=(N_CHIPS, D_MODEL // tn),
            in_specs=[pl.BlockSpec((1, n_tok, fq), lambda j, n: (j, 0, 0)), pl.BlockSpec((n_tok, tn), lambda j, n: (0, n))]
            + ([] if prev is None else [any_spec]),
            out_specs=pl.BlockSpec((1, 1, fq, tn), lambda j, n: (j, d, 0, n)),
            out_shape=jax.ShapeDtypeStruct((N_CHIPS, DEPTH, fq, D_MODEL), GRAD_DT), input_output_aliases=alias,
            compiler_params=_cparams(("parallel", "parallel"), 6 * n_tok * (fq + tn) * 2 + 4 * fq * tn * 4),
        )(t, rhs, *([] if prev is None else [prev]))

    dwg = dw_call(da, h, 1.0, None if into is None else into[0])
    dwu = dw_call(db, h, 1.0, None if into is None else into[1])
    dwd = dw_call(s, dy16, 0.5, None if into is None else into[2])
    tr = _tile(n_tok, 256, 16)

    def dh_norm(da_ref, db_ref, wg_ref, wu_ref, dy_ref, x_ref, g_ref, dx_ref, dx16_ref, dg_ref):
        dh = _dot(da_ref[0], wg_ref[0, 0]) + _dot(db_ref[0], wu_ref[0, 0])
        for j in range(1, N_CHIPS):
            dh = dh + _dot(da_ref[j], wg_ref[j, 0]) + _dot(db_ref[j], wu_ref[j, 0])
        dx, dg = _rms_bwd(x_ref[...], g_ref[...], dh)
        dx = dy_ref[...] + dx
        dx_ref[...] = dx
        dx16_ref[...] = dx.astype(BF16)

        @pl.when(pl.program_id(0) == 0)
        def _():
            dg_ref[...] = jnp.zeros(dg_ref.shape, F32)

        dg_ref[...] += dg

    tspec = pl.BlockSpec((N_CHIPS, tr, fq), lambda i: (0, i, 0))
    wspec = pl.BlockSpec((N_CHIPS, 1, fq, D_MODEL), lambda i: (0, d, 0, 0))
    rspec = pl.BlockSpec((tr, D_MODEL), lambda i: (i, 0))
    vspec = pl.BlockSpec((1, D_MODEL), lambda i: (0, 0))
    dx, dx16, dg = _pcall(
        dh_norm, name="ffn_dh_norm", grid=(n_tok // tr,),
        in_specs=[tspec, tspec, wspec, wspec, rspec, rspec, vspec], out_specs=[rspec, rspec, vspec],
        out_shape=[jax.ShapeDtypeStruct((n_tok, D_MODEL), F32), jax.ShapeDtypeStruct((n_tok, D_MODEL), BF16),
                   jax.ShapeDtypeStruct((1, D_MODEL), F32)],
        compiler_params=_cparams(("arbitrary",), 4 * N_CHIPS * D_MODEL * fq * 2 + 4 * N_CHIPS * tr * fq * 2
                                 + 16 * tr * D_MODEL * 4),
    )(da, db, wg, wu, dy, x, g)
    return dx, dx16, dg[0], (dwg, dwu, dwd)


def _alpha_pads(w_alpha):
    z = jnp.zeros((LANE - 2 * GLA_R, GLA_W), F32)
    wf = jnp.concatenate([w_alpha[0], jnp.zeros((GLA_R, GLA_W), F32), z], axis=0)
    wb = jnp.concatenate([jnp.zeros((GLA_R, GLA_W), F32), w_alpha[1], z], axis=0)
    return wf, wb


def _dup_heads(t):
    low = lax.broadcasted_iota(jnp.int32, t.shape, 1) < AT_D
    h0, h1 = jnp.where(low, t, 0.0), jnp.where(low, 0.0, t)
    return jnp.concatenate([h0 + pltpu.roll(h0, AT_D, 1), h1 + pltpu.roll(h1, AT_D, 1)], axis=1)


def _fold_heads(t):
    low = lax.broadcasted_iota(jnp.int32, (t.shape[0], LANE), 1) < AT_D
    g0, g1 = t[:, :LANE], t[:, LANE:]
    return jnp.where(low, g0 + pltpu.roll(g0, AT_D, 1), g1 + pltpu.roll(g1, AT_D, 1))


def _mixer_fwd(x, w):
    n_tok = x.shape[0]
    wip, wmg = w['w_in_p'], w['w_merge_gate']
    tr = _tile(n_tok, 256, 16)

    def project(x_ref, g_ref, wip_ref, wmg_ref, h_ref, z_ref, gp_ref):
        hv = _rms(x_ref[...], g_ref[...]).astype(BF16)
        h_ref[...] = hv
        z_ref[...] = _dot(hv, wip_ref[...], _NT).astype(BF16)
        gp_ref[...] = _dot(hv, wmg_ref[...]).astype(BF16)

    def rows(c):
        return pl.BlockSpec((tr, c), lambda i: (i, 0))

    def whole(t):
        return pl.BlockSpec(t.shape, lambda i: (0, 0))

    h, z, gp = _pcall(
        project, name="mix_in", grid=(n_tok // tr,),
        in_specs=[rows(D_MODEL), whole(w['mix_norm']), whole(wip), whole(wmg)],
        out_specs=[rows(D_MODEL), rows(ZP_W), rows(3 * D_MODEL)],
        out_shape=[jax.ShapeDtypeStruct((n_tok, D_MODEL), BF16), jax.ShapeDtypeStruct((n_tok, ZP_W), BF16),
                   jax.ShapeDtypeStruct((n_tok, 3 * D_MODEL), BF16)],
        compiler_params=_cparams(("parallel",), 4 * (wip.size + wmg.size) + 16 * tr * (ZP_W + 3 * D_MODEL)),
    )(x, w['mix_norm'], wip, wmg)
    u = _cw(z, 0, S5_W)

    xs, yy = [], None
    for d, rev in ((0, False), (1, True)):
        a_re, a_im, wb, wc = w['s5'][d]
        x_d, yy = _s5_fwd("s5_rev" if rev else "s5_fwd", z, a_re, a_im, wb, wc, yy, rev=rev)
        xs.append(x_d)

    def s5_post(yv, uv, dv, wglu):
        y0 = yv + dv * uv
        y1 = _gelu(y0)
        t = _dot(y1.astype(BF16), wglu)
        return y0, t, y1 * _sigmoid(t)

    y0, t_glu, y_s5 = _rowwise("s5_post", s5_post, [yy, u], [w['s5_d'], w['s5_w_glu']],
                               [(S5_W, F32), (S5_W, F32), (S5_W, BF16)])

    wa_f, wa_b = _alpha_pads(w['gla_w_alpha'])
    ba_f, ba_b = w['gla_b_alpha'][0:1], w['gla_b_alpha'][1:2]
    zz = _cw(z, ZC_Z, LANE)

    def gla_prep(zv, waf, wab, bf, bb):
        z16 = zv.astype(BF16)
        lf = _dot(z16, waf.astype(BF16)) + bf
        lb = _dot(z16, wab.astype(BF16)) + bb
        return _log_sigmoid(lf) / GLA_TAU, _log_sigmoid(lb) / GLA_TAU

    la_f, la_b = _rowwise("gla_prep", gla_prep, [zz], [wa_f, wa_b, ba_f, ba_b], [(GLA_W, F32), (GLA_W, F32)])
    o_f, s_f = _gla_fwd("gla_fwd", z, la_f, rev=False)
    o_b, s_b = _gla_fwd("gla_rev", z, la_b, rev=True)
    gate = _cw(z, 4, GLA_W)

    def gla_post(of, ob, gv, gn):
        o, gv = of + ob, gv.astype(F32)
        outs = []
        for hh in range(GLA_H):
            sl = slice(hh * GLA_D, (hh + 1) * GLA_D)
            gh = gv[:, sl]
            outs.append(_rms(o[:, sl], gn) * (gh * _sigmoid(gh)))
        return jnp.concatenate(outs, axis=1)

    y_gla = _rowwise("gla_post", gla_post, [o_f, o_b, gate], [w['gla_norm']], [(GLA_W, BF16)])[0]

    cq, sq, mq, rq = _head_consts(n_tok, AT_QH)
    ck, sk, mk, rk = _head_consts(n_tok, AT_KH)
    gq = jnp.tile(w['attn_q_norm'], (1, AT_QH))
    gk = jnp.tile(w['attn_k_norm'], (1, AT_KH))

    def rope_fwd(v, gain, cos, sin, mean, rot, scale):
        r = lax.rsqrt(_dot_exact(v * v, mean, terms=2) + EPS)
        vn = v * r * gain
        return (vn * cos + _dot_exact(vn, rot, terms=2) * sin) * scale

    def attn_prep(qv, kv, vv, cqv, sqv, ckv, skv, gqv, gkv, mqv, rqv, mkv, rkv):
        qv, kv, vv = qv.astype(F32), kv.astype(F32), vv.astype(F32)
        kr = rope_fwd(kv, gkv, ckv, skv, mkv, rkv, 1.0)
        return rope_fwd(qv, gqv, cqv, sqv, mqv, rqv, AT_D ** -0.5), _dup_heads(kr), _dup_heads(vv)

    q_raw, k_raw, v_raw = _cw(z, 5, AT_W), _cw(z, ZC_AK, AT_KW), _cw(z, ZC_AV, AT_KW)
    q16, kd, vd = _rowwise("attn_prep", attn_prep, [q_raw, k_raw, v_raw, cq, sq, ck, sk], [gq, gk, mq, rq, mk, rk],
                           [(AT_W, BF16), (2 * LANE, BF16), (2 * LANE, BF16)])
    y_at, lse = _attn_fwd("attn_fwd", q16, kd, vd)

    p0 = _mm("mix_branch", y_s5, w['w_branch_s5'], out_dtype=BF16)
    p1 = _mm("mix_branch", y_gla, w['w_branch_gla'], out_dtype=BF16)
    p2 = _mm("mix_branch_f32", y_at, w['w_branch_attn'], out_dtype=BF16)

    def merge(gpv, a0, a1, a2, bias):
        g = _sigmoid(gpv.astype(F32) + bias)
        return (g[:, :D_MODEL] * a0.astype(F32) + g[:, D_MODEL:2 * D_MODEL] * a1.astype(F32)
                + g[:, 2 * D_MODEL:] * a2.astype(F32))

    merged = _rowwise("mix_merge", merge, [gp, p0, p1, p2], [w['b_merge_gate']], [(D_MODEL, BF16)])[0]
    xo = _mm("mix_out", merged, w['w_out'], add=x)
    saved = dict(x=x, h=h, z=z, gp=gp, xs=xs, y0=y0, t_glu=t_glu, y_s5=y_s5, la=(la_f, la_b), o=(o_f, o_b), s=(s_f, s_b),
                 y_gla=y_gla, q16=q16, kd=kd, vd=vd, lse=lse, y_at=y_at, p=(p0, p1, p2), merged=merged)
    return xo, saved


def _mixer_bwd(dy, dy16, sv, w, s5_raw):
    z, h = sv['z'], sv['h']
    n_tok = z.shape[0]
    grads = {}
    dmerged = _mm("mix_dmerged", dy16, w['w_out'], tb=True, out_dtype=BF16)
    grads['w_out'] = _mm("mix_dwout", sv['merged'], dy16, ta=True, out_dtype=GRAD_DT)

    def dmerge(dm, gpv, a0, a1, a2, bias):
        dm = dm.astype(F32)
        g = _sigmoid(gpv.astype(F32) + bias)
        ps = (a0.astype(F32), a1.astype(F32), a2.astype(F32))
        dps, dgs = [], []
        for k in range(3):
            gk = g[:, k * D_MODEL:(k + 1) * D_MODEL]
            dps.append(dm * gk)
            dgs.append(dm * ps[k] * gk * (1.0 - gk))
        dgp = jnp.concatenate(dgs, axis=1)
        return dps[0], dps[1], dps[2], dgp, jnp.sum(dgp, axis=0, keepdims=True)

    dp0, dp1, dp2, dgp, dbm = _rowwise("mix_dmerge", dmerge, [dmerged, sv['gp'], *sv['p']], [w['b_merge_gate']],
                                       [(D_MODEL, BF16)] * 3 + [(3 * D_MODEL, BF16)], accs=[(1, 3 * D_MODEL)])
    grads['b_merge_gate'] = dbm[0]
    grads['w_branch_s5'] = _mm("mix_dbranch", sv['y_s5'], dp0, ta=True, out_dtype=GRAD_DT)
    grads['w_branch_gla'] = _mm("mix_dbranch", sv['y_gla'], dp1, ta=True, out_dtype=GRAD_DT)
    grads['w_branch_attn'] = _mm("mix_dbranch_f32", sv['y_at'], dp2, ta=True, out_dtype=GRAD_DT)
    dy_s5 = _mm("mix_dy", dp0, w['w_branch_s5'], tb=True)
    dy_gla = _mm("mix_dy", dp1, w['w_branch_gla'], tb=True)
    dy_at = _mm("mix_dy", dp2, w['w_branch_attn'], tb=True)
    grads['w_merge_gate'] = _mm("mix_dwgate", h, dgp, ta=True, out_dtype=GRAD_DT)
    u = _cw(z, 0, S5_W)

    def s5_post_bwd(dyv, y0, t, uv, dv, wglu):
        sg = _sigmoid(t)
        y1 = _gelu(y0)
        dt = dyv * y1 * sg * (1.0 - sg)
        dy1 = dyv * sg + _dot(dt.astype(BF16), wglu, _NT)
        dy0 = dy1 * _gelu_grad(y0)
        return dy0, dt, y1, dy0 * dv, jnp.sum(dy0 * uv, axis=0, keepdims=True)

    dy0, dt_glu, y1, du, dd = _rowwise("s5_post_bwd", s5_post_bwd, [dy_s5, sv['y0'], sv['t_glu'], u],
                                       [w['s5_d'], w['s5_w_glu']],
                                       [(S5_W, BF16), (S5_W, BF16), (S5_W, BF16), (S5_W, F32)], accs=[(1, S5_W)])
    grads['s5_d'] = dd[0]
    grads['s5_w_glu'] = _mm("s5_dwglu", y1, dt_glu, ta=True, out_dtype=GRAD_DT)
    s5_cot = []
    for d, rev in ((0, False), (1, True)):
        a_re, a_im, wb, wc = w['s5'][d]
        du, dwb, dwc, da = _s5_bwd("s5_adj_fwd" if rev else "s5_adj_rev", dy0, z, sv['xs'][d], a_re, -a_im, wb, wc, du,
                                   rev=not rev)
        da = jnp.sum(da, axis=0, keepdims=True)
        s5_cot.append((da[:, :S5_N], da[:, S5_N:], dwb, dwc))
    for d in range(2):
        _, pull = jax.vjp(_s5_prep, *s5_raw[d])
        s5_cot[d] = pull(s5_cot[d])
    for k, nm in enumerate(('s5_lambda_re', 's5_lambda_im', 's5_log_dt', 's5_b_re', 's5_b_im', 's5_c_re', 's5_c_im')):
        grads[nm] = jnp.stack([s5_cot[0][k], s5_cot[1][k]])

    gate = _cw(z, 4, GLA_W)

    def gla_post_bwd(dyv, of, ob, gv, gn):
        o, gv = of + ob, gv.astype(F32)
        dos, dgates = [], []
        dgn = jnp.zeros((1, GLA_D), F32)
        for hh in range(GLA_H):
            sl = slice(hh * GLA_D, (hh + 1) * GLA_D)
            gh, dyh = gv[:, sl], dyv[:, sl]
            sg = _sigmoid(gh)
            dgates.append(dyh * _rms(o[:, sl], gn) * sg * (1.0 + gh * (1.0 - sg)))
            dx, dg = _rms_bwd(o[:, sl], gn, dyh * gh * sg)
            dos.append(dx)
            dgn = dgn + dg
        return jnp.concatenate(dos, axis=1), jnp.concatenate(dgates, axis=1), dgn

    do, dgate, dgn = _rowwise("gla_post_bwd", gla_post_bwd, [dy_gla, *sv['o'], gate], [w['gla_norm']],
                              [(GLA_W, F32), (GLA_W, F32)], accs=[(1, GLA_D)])
    grads['gla_norm'] = dgn[0]
    dq_f, dk_f, dv_f, dla_f = _gla_bwd("gla_bwd_fwd", z, sv['la'][0], sv['s'][0], do, rev=False)
    dq_b, dk_b, dv_b, dla_b = _gla_bwd("gla_bwd_rev", z, sv['la'][1], sv['s'][1], do, rev=True)
    wa_f, wa_b = _alpha_pads(w['gla_w_alpha'])
    ba_f, ba_b = w['gla_b_alpha'][0:1], w['gla_b_alpha'][1:2]
    zz = _cw(z, ZC_Z, LANE)

    def gla_prep_bwd(dlf, dlb, zv, waf, wab, bf, bb):
        z16 = zv.astype(BF16)
        waf16, wab16 = waf.astype(BF16), wab.astype(BF16)
        xf = _dot(z16, waf16) + bf
        xb = _dot(z16, wab16) + bb
        df = dlf * (_sigmoid(-xf) / GLA_TAU)
        db = dlb * (_sigmoid(-xb) / GLA_TAU)
        dz = _dot(df.astype(BF16), waf16, _NT) + _dot(db.astype(BF16), wab16, _NT)
        return df, db, dz, jnp.sum(df, axis=0, keepdims=True), jnp.sum(db, axis=0, keepdims=True)

    dlog_f, dlog_b, dzz, dba_f, dba_b = _rowwise(
        "gla_prep_bwd", gla_prep_bwd, [dla_f, dla_b, zz], [wa_f, wa_b, ba_f, ba_b],
        [(GLA_W, BF16), (GLA_W, BF16), (LANE, F32)], accs=[(1, GLA_W), (1, GLA_W)])
    dwa_f = _mm("gla_dwa", zz, dlog_f, ta=True)
    dwa_b = _mm("gla_dwa", zz, dlog_b, ta=True)
    grads['gla_w_alpha'] = jnp.stack([dwa_f[:GLA_R], dwa_b[GLA_R:2 * GLA_R]])
    grads['gla_b_alpha'] = jnp.concatenate([dba_f, dba_b], axis=0)

    dq_r, dkd, dvd = _attn_bwd("attn_bwd", sv['q16'], sv['kd'], sv['vd'], sv['y_at'], sv['lse'], dy_at)
    cq, sq, mq, rq = _head_consts(n_tok, AT_QH)
    ck, sk, mk, rk = _head_consts(n_tok, AT_KH)
    gq = jnp.tile(w['attn_q_norm'], (1, AT_QH))
    gk = jnp.tile(w['attn_k_norm'], (1, AT_KH))

    def rope_bwd(dr, v, gain, cos, sin, mean, rot, scale):
        dr = dr * scale
        dvn = dr * cos + _dot_exact(dr * sin, rot, _NT, terms=2)
        r = lax.rsqrt(_dot_exact(v * v, mean, terms=2) + EPS)
        vh = v * r
        dg = jnp.sum(dvn * vh, axis=0, keepdims=True)
        dvh = dvn * gain
        return r * (dvh - vh * _dot_exact(dvh * vh, mean, terms=2)), dg

    def attn_prep_bwd(dq, dk2, dv2, qv, kv, cqv, sqv, ckv, skv, gqv, gkv, mqv, rqv, mkv, rkv):
        qv, kv = qv.astype(F32), kv.astype(F32)
        dqo, dgq = rope_bwd(dq, qv, gqv, cqv, sqv, mqv, rqv, AT_D ** -0.5)
        dko, dgk = rope_bwd(_fold_heads(dk2), kv, gkv, ckv, skv, mkv, rkv, 1.0)
        return dqo, dko, _fold_heads(dv2), dgq, dgk

    q_raw, k_raw = _cw(z, 5, AT_W), _cw(z, ZC_AK, AT_KW)
    daq, dak, dav, dgq, dgk = _rowwise("attn_prep_bwd", attn_prep_bwd, [dq_r, dkd, dvd, q_raw, k_raw, cq, sq, ck, sk],
                                       [gq, gk, mq, rq, mk, rk], [(AT_W, F32), (AT_KW, F32), (AT_KW, F32)],
                                       accs=[(1, AT_W), (1, AT_KW)])
    grads['attn_q_norm'] = dgq.reshape(AT_QH, AT_D).sum(axis=0)
    grads['attn_k_norm'] = dgk.reshape(AT_KH, AT_D).sum(axis=0)

    def assemble(duv, qf, qb, kf, kb, vf, vb, dg, aq, ak, av, zzv):
        return jnp.concatenate([duv, qf + qb, kf + kb, vf + vb, dg, aq, ak, av, zzv], axis=1)

    dz = _rowwise("mix_dz", assemble, [du, dq_f, dq_b, dk_f, dk_b, dv_f, dv_b, dgate, daq, dak, dav, dzz], [],
                  [(ZP_W, BF16)])[0]
    grads['w_in'] = _unpad_w_in(_mm("mix_dwin", dz, h, ta=True, out_dtype=GRAD_DT))
    wmg, wip = w['w_merge_gate'], w['w_in_p']
    tr = _tile(n_tok, 256, 16)

    def dh_norm(dgp_ref, dz_ref, wmg_ref, wip_ref, dy_ref, x_ref, g_ref, dx_ref, dx16_ref, dg_ref):
        dh = _dot(dgp_ref[...], wmg_ref[...], _NT) + _dot(dz_ref[...], wip_ref[...])
        dxv, dgv = _rms_bwd(x_ref[...], g_ref[...], dh)
        dxv = dy_ref[...] + dxv
        dx_ref[...] = dxv
        dx16_ref[...] = dxv.astype(BF16)

        @pl.when(pl.program_id(0) == 0)
        def _():
            dg_ref[...] = jnp.zeros(dg_ref.shape, F32)

        dg_ref[...] += dgv

    def rows(c):
        return pl.BlockSpec((tr, c), lambda i: (i, 0))

    def whole(t):
        return pl.BlockSpec(t.shape, lambda i: (0, 0))

    dx, dx16, dg = _pcall(
        dh_norm, name="mix_dh_norm", grid=(n_tok // tr,),
        in_specs=[rows(dgp.shape[1]), rows(dz.shape[1]), whole(wmg), whole(wip), rows(D_MODEL), rows(D_MODEL),
                  whole(w['mix_norm'])],
        out_specs=[rows(D_MODEL), rows(D_MODEL), whole(w['mix_norm'])],
        out_shape=[jax.ShapeDtypeStruct((n_tok, D_MODEL), F32), jax.ShapeDtypeStruct((n_tok, D_MODEL), BF16),
                   jax.ShapeDtypeStruct((1, D_MODEL), F32)],
        compiler_params=_cparams(("arbitrary",), 4 * (wmg.size + wip.size) + 8 * tr * (dgp.shape[1] + dz.shape[1])
                                 + 16 * tr * D_MODEL * 4),
    )(dgp, dz, wmg, wip, dy, sv['x'], w['mix_norm'])
    grads['mix_norm'] = dg[0]
    return dx, dx16, grads


FFN_NAMES = [p + k for p in ('ffn1_', 'ffn2_') for k in ('w_gate', 'w_up', 'w_down')]
MISC_SHARDED = [n for n in SHARDED if n not in FFN_NAMES]
LARGE_SHARDED = FFN_NAMES + ['w_in', 'w_merge_gate', 'w_out', 'w_branch_s5', 'w_branch_gla', 'w_branch_attn', 's5_w_glu']
KEPT_TRANSPOSED = [p + k for p in ('ffn1_', 'ffn2_') for k in ('w_gate', 'w_up')] + ['w_in']


def _stored_axis(n):
    return 1 if n in KEPT_TRANSPOSED else SHARD_AXIS[n]


def _stored(n, t):
    return jnp.swapaxes(t, 1, 2) if n in KEPT_TRANSPOSED else t


def _layer_weights(ws, rep, i):
    w = {k: _from_shards(ws[k][:, i], _stored_axis(k) - 1) for k in MISC_SHARDED}
    for k in ('ffn1_norm', 'mix_norm', 'ffn2_norm', 's5_d', 'gla_norm', 'attn_q_norm', 'attn_k_norm', 'b_merge_gate'):
        w[k] = rep[k][i].reshape(1, -1)
    w['w_in_p'] = _pad_w_in(w['w_in'])
    s5_raw = [tuple(rep[k][i, d] for k in ('s5_lambda_re', 's5_lambda_im', 's5_log_dt', 's5_b_re', 's5_b_im',
                                            's5_c_re', 's5_c_im')) for d in range(2)]
    w['s5'] = [_s5_prep(*s5_raw[d]) for d in range(2)]
    return w, s5_raw


def _device_step(x, target, ws, rep):
    lw = [_layer_weights(ws, rep, i) for i in range(DEPTH)]
    ffn = {p: (ws[p + 'w_gate'], ws[p + 'w_up'], ws[p + 'w_down']) for p in ('ffn1_', 'ffn2_')}
    saved = []
    for i in range(DEPTH):
        w, _ = lw[i]
        x, s1 = _ffn_fwd(x, w['ffn1_norm'], *ffn['ffn1_'], i)
        x, s2 = _mixer_fwd(x, w)
        x, s3 = _ffn_fwd(x, w['ffn2_norm'], *ffn['ffn2_'], i)
        saved.append((s1, s2, s3))
    gfin = rep['final_norm'].reshape(1, -1)

    def head(xv, tv, gv):
        e = _rms(xv, gv) - tv
        dx, dg = _rms_bwd(xv, gv, e * (1.0 / D_MODEL))
        part = jnp.sum(e * e, axis=0, keepdims=True)
        return dx, dx, dg, part

    dx, dx16, dgfin, part = _rowwise("loss_head", head, [x, target], [gfin], [(D_MODEL, F32), (D_MODEL, BF16)],
                                     accs=[(1, D_MODEL), (1, D_MODEL)])
    loss = (0.5 / D_MODEL) * jnp.sum(part)
    per_layer = []
    dffn = {'ffn1_': None, 'ffn2_': None}
    for i in reversed(range(DEPTH)):
        w, s5_raw = lw[i]
        s1, s2, s3 = saved[i]
        dx, dx16, n3, dffn['ffn2_'] = _ffn_bwd(dx, dx16, s3, w['ffn2_norm'], *ffn['ffn2_'], i, dffn['ffn2_'])
        dx, dx16, g = _mixer_bwd(dx, dx16, s2, w, s5_raw)
        dx, dx16, n1, dffn['ffn1_'] = _ffn_bwd(dx, dx16, s1, w['ffn1_norm'], *ffn['ffn1_'], i, dffn['ffn1_'])
        g['ffn2_norm'], g['ffn1_norm'] = n3, n1
        per_layer.append(g)
    per_layer.reverse()
    stacked = {k: jnp.stack([per_layer[i][k] for i in range(DEPTH)]) for k in per_layer[0]}
    gs = {k: _to_shards(stacked[k], _stored_axis(k)) for k in MISC_SHARDED}
    for p in ('ffn1_', 'ffn2_'):
        gs[p + 'w_gate'], gs[p + 'w_up'], gs[p + 'w_down'] = dffn[p]
    grep = {k: stacked[k] for k in REPLICATED if k != 'final_norm'}
    grep['final_norm'] = dgfin[0]
    return loss, dx, gs, grep


PACK_C = 1024
ROW_TILE = 512


def _rows_of(shape):
    return -(-int(np.prod(shape)) // PACK_C)


def _as_rows(t, lead=0):
    head = t.shape[:lead]
    flat = t.reshape(head + (-1,))
    rows = -(-flat.shape[-1] // PACK_C)
    pad = rows * PACK_C - flat.shape[-1]
    if pad:
        flat = jnp.pad(flat, [(0, 0)] * lead + [(0, pad)])
    return flat.reshape(head + (rows, PACK_C))


def _pack(items, total_rows, lead=0):
    parts = [_as_rows(t, lead) for t in items]
    used = sum(p.shape[lead] for p in parts)
    if total_rows > used:
        parts.append(jnp.zeros(parts[0].shape[:lead] + (total_rows - used, PACK_C), parts[0].dtype))
    return jnp.concatenate(parts, axis=lead)


def _unpack(rows, shapes, lead=0):
    out, r0 = [], 0
    head = rows.shape[:lead]
    for shp in shapes:
        n, size = _rows_of(shp), int(np.prod(shp))
        piece = lax.slice_in_dim(rows, r0, r0 + n, axis=lead).reshape(head + (n * PACK_C,))
        out.append(lax.slice_in_dim(piece, 0, size, axis=lead).reshape(head + tuple(shp)))
        r0 += n
    return out


def _round_up(n, m):
    return -(-n // m) * m


def _to_shards(g, axis):
    shp = g.shape
    g = g.reshape(shp[:axis] + (N_CHIPS, shp[axis] // N_CHIPS) + shp[axis + 1:])
    return jnp.moveaxis(g, axis, 0)


def _from_shards(s, axis):
    s = jnp.moveaxis(s, 0, axis)
    shp = s.shape
    return s.reshape(shp[:axis] + (shp[axis] * shp[axis + 1],) + shp[axis + 2:])


_HBM = pl.BlockSpec(memory_space=pltpu.HBM)


def _place():
    return lax.axis_index("x"), lax.axis_index("y"), lax.axis_index("c")


def _rcopy(src, dst, send, recv, k, dev):
    return pltpu.make_async_remote_copy(src_ref=src, dst_ref=dst, send_sem=send.at[k], recv_sem=recv.at[k],
                                        device_id=dev, device_id_type=MESH)


def _comm_call(name, body, arrays, out_shapes, n_sems, n_local):
    return pl.pallas_call(
        body, name=name, in_specs=[_HBM] * len(arrays), out_specs=[_HBM] * len(out_shapes), out_shape=out_shapes,
        scratch_shapes=[pltpu.SemaphoreType.DMA((n_sems,)), pltpu.SemaphoreType.DMA((n_sems,)),
                        pltpu.SemaphoreType.DMA((max(n_local, 1),))],
    )(*arrays)


def _ici_peers(x, y, c):
    x_first = c == 0
    first = (jnp.where(x_first, 1 - x, x), jnp.where(x_first, y, 1 - y))
    second = (jnp.where(x_first, x, 1 - x), jnp.where(x_first, 1 - y, y))
    return first, second


def _ag_chips(name, packs):
    n = len(packs)

    def body(*refs):
        p, o, (send, recv, _) = refs[:n], refs[n:2 * n], refs[2 * n:]
        x, y, c = _place()
        (ax, ay), (bx, by) = _ici_peers(x, y, c)
        me, na, nb, nd = 2 * x + y, 2 * ax + ay, 2 * bx + by, 2 * (1 - x) + (1 - y)
        sib = (x, y, 1 - c)
        own = [_rcopy(p[a].at[c], o[a].at[me, c], send, recv, 6 * a, (ax, ay, c)) for a in range(n)]
        own += [_rcopy(p[a].at[c], o[a].at[me, c], send, recv, 6 * a + 1, (bx, by, c)) for a in range(n)]
        for cp in own:
            cp.start()
        passed = []

        def landed(a, chip, k, fwd):
            blk = o[a].at[chip, c]
            _rcopy(blk, blk, send, recv, 6 * a + k, sib).wait_recv()
            for sem, dev in fwd:
                cp = _rcopy(blk, blk, send, recv, 6 * a + sem, dev)
                cp.start()
                passed.append(cp)

        for a in range(n):
            landed(a, na, 0, [(2, (bx, by, c)), (3, sib)])
        for a in range(n):
            landed(a, nb, 1, [(4, sib)])
        for a in range(n):
            landed(a, nd, 2, [(5, sib)])
        for a in range(n):
            for k, chip in ((3, nb), (4, na), (5, nd)):
                blk = o[a].at[chip, 1 - c]
                _rcopy(blk, blk, send, recv, 6 * a + k, sib).wait_recv()
        for cp in own + passed:
            cp.wait_send()

    outs = _comm_call(name, body, packs, [jax.ShapeDtypeStruct((N_CHIPS,) + t.shape, t.dtype) for t in packs], 6 * n, 0)
    chip = 2 * lax.axis_index("x") + lax.axis_index("y")
    return [lax.dynamic_update_slice_in_dim(o, t[None], chip, axis=0) for o, t in zip(outs, packs)]


def _pair_swap(name, gs):
    n = len(gs)
    ns = gs[0].shape[0]

    def body(*refs):
        g, r, (send, recv, _) = refs[:n], refs[n:2 * n], refs[2 * n:]
        x, y, c = _place()
        cps = [_rcopy(g[a].at[j, 1 - c], r[a].at[j], send, recv, ns * a + j, (x, y, 1 - c))
               for a in range(n) for j in range(ns)]
        for cp in cps:
            cp.start()
        for cp in cps:
            cp.wait_recv()
        for cp in cps:
            cp.wait_send()

    return _comm_call(name, body, gs, [jax.ShapeDtypeStruct((ns,) + t.shape[2:], t.dtype) for t in gs], ns * n, 0)


def _halve_first(name, parts):
    n = len(parts)

    def body(*refs):
        p, r, (send, recv, _) = refs[:n], refs[n:2 * n], refs[2 * n:]
        x, y, c = _place()
        (ax, ay), _ = _ici_peers(x, y, c)
        cps = [_rcopy(p[a].at[jnp.where(c == 0, 2 * ax + k, 2 * k + ay)], r[a].at[k], send, recv, 2 * a + k, (ax, ay, c))
               for a in range(n) for k in range(2)]
        for cp in cps:
            cp.start()
        for cp in cps:
            cp.wait_recv()
        for cp in cps:
            cp.wait_send()

    return _comm_call(name, body, parts, [jax.ShapeDtypeStruct((2,) + t.shape[1:], t.dtype) for t in parts], 2 * n, 0)


def _halve_second(name, kept):
    n = len(kept)

    def body(*refs):
        p, r, (send, recv, _) = refs[:n], refs[n:2 * n], refs[2 * n:]
        x, y, c = _place()
        _, (bx, by) = _ici_peers(x, y, c)
        cps = [_rcopy(p[a].at[jnp.where(c == 0, by, bx)], r[a], send, recv, a, (bx, by, c)) for a in range(n)]
        for cp in cps:
            cp.start()
        for cp in cps:
            cp.wait_recv()
        for cp in cps:
            cp.wait_send()

    return _comm_call(name, body, kept, [jax.ShapeDtypeStruct(t.shape[1:], t.dtype) for t in kept], n, 0)


def _add_selected(name, t, r, index, out_dtype):
    single = r.ndim == 2
    r3 = r[None] if single else r
    nk, rh, cc = r3.shape
    tl = _tile(rh, ROW_TILE, 16)

    def body(i_ref, t_ref, r_ref, o_ref):
        o_ref[...] = (t_ref[...].astype(F32) + r_ref[...].astype(F32)).astype(o_ref.dtype)

    out = _pcall(
        body, name=name,
        grid_spec=pltpu.PrefetchScalarGridSpec(
            num_scalar_prefetch=1, grid=(nk, rh // tl),
            in_specs=[pl.BlockSpec((1, tl, cc), lambda k, i, i_ref: (i_ref[k], i, 0)),
                      pl.BlockSpec((1, tl, cc), lambda k, i, i_ref: (k, i, 0))],
            out_specs=pl.BlockSpec((1, tl, cc), lambda k, i, i_ref: (k, i, 0))),
        out_shape=jax.ShapeDtypeStruct(r3.shape, out_dtype),
        compiler_params=_cparams(("parallel", "parallel"), 8 * tl * cc * 4),
    )(index, t, r3)
    return out[0] if single else out


def _pair_share(name, reds):
    n = len(reds)

    def body(*refs):
        r, o, (send, recv, _) = refs[:n], refs[n:2 * n], refs[2 * n:]
        x, y, c = _place()
        cps = [_rcopy(r[a], o[a].at[c], send, recv, a, (x, y, 1 - c)) for a in range(n)]
        for cp in cps:
            cp.start()
        for a in range(n):
            blk = o[a].at[1 - c]
            _rcopy(blk, blk, send, recv, a, (x, y, 1 - c)).wait_recv()
        for cp in cps:
            cp.wait_send()

    outs = _comm_call(name, body, reds, [jax.ShapeDtypeStruct((2,) + t.shape, t.dtype) for t in reds], n, 0)
    core = lax.axis_index("c")
    return [lax.dynamic_update_slice_in_dim(o, t[None], core, axis=0) for o, t in zip(outs, reds)]


def _add_own_half(name, g, r):
    ns, _, rh, cc = g.shape
    tl = _tile(rh, ROW_TILE, 16)
    core = lax.axis_index("c").astype(jnp.int32).reshape(1)

    def body(c_ref, g_ref, r_ref, o_ref):
        o_ref[...] = (g_ref[0].astype(F32) + r_ref[...].astype(F32)).astype(o_ref.dtype)

    return _pcall(
        body, name=name,
        grid_spec=pltpu.PrefetchScalarGridSpec(
            num_scalar_prefetch=1, grid=(ns, rh // tl),
            in_specs=[pl.BlockSpec((1, 1, tl, cc), lambda j, i, c_ref: (j, c_ref[0], i, 0)),
                      pl.BlockSpec((1, tl, cc), lambda j, i, c_ref: (j, i, 0))],
            out_specs=pl.BlockSpec((1, tl, cc), lambda j, i, c_ref: (j, i, 0))),
        out_shape=jax.ShapeDtypeStruct(r.shape, BF16),
        compiler_params=_cparams(("parallel", "parallel"), 8 * tl * cc * 4),
    )(core, g, r)


def _reduce_scatter(gs):
    x, y, c = lax.axis_index("x"), lax.axis_index("y"), lax.axis_index("c")
    mine_first = jnp.stack([jnp.where(c == 0, 2 * x + k, 2 * k + y) for k in range(2)]).astype(jnp.int32)
    mine_second = jnp.where(c == 0, y, x).astype(jnp.int32).reshape(1)
    swapped = _pair_swap("rs_pair_swap", gs)
    parts = [_add_own_half("rs_add_pair", g, r) for g, r in zip(gs, swapped)]
    got = _halve_first("rs_halve_first", parts)
    kept = [_add_selected("rs_add_first", t, r, mine_first, BF16) for t, r in zip(parts, got)]
    got = _halve_second("rs_halve_second", kept)
    reds = [_add_selected("rs_add_second", t, r, mine_second, F32) for t, r in zip(kept, got)]
    return _pair_share("rs_pair_share", reds)


def _adamw(name, g, w, m, v):
    c1 = 1.0 - ADAM_B1 ** ADAM_STEP
    c2 = 1.0 - ADAM_B2 ** ADAM_STEP
    nb, rows, cc = g.shape
    tl = _tile(rows, ROW_TILE, SUBLANE)

    def body(g_ref, w_ref, m_ref, v_ref, d_ref, mo_ref, vo_ref):
        gv = g_ref[...]
        mn = ADAM_B1 * m_ref[...] + (1.0 - ADAM_B1) * gv
        vn = ADAM_B2 * v_ref[...] + (1.0 - ADAM_B2) * (gv * gv)
        d_ref[...] = -ADAM_LR * ((mn / c1) / (jnp.sqrt(vn / c2) + ADAM_EPS) + ADAM_WD * w_ref[...])
        mo_ref[...] = mn
        vo_ref[...] = vn

    spec = pl.BlockSpec((1, tl, cc), lambda b, i: (b, i, 0))
    return _pcall(
        body, name=name, grid=(nb, rows // tl), in_specs=[spec] * 4, out_specs=[spec] * 3,
        out_shape=[jax.ShapeDtypeStruct(g.shape, F32)] * 3,
        compiler_params=_cparams(("parallel", "parallel"), 16 * tl * cc * 4),
    )(g, w, m, v)


def _train_step(a):
    x, target = a['x'][0], a['loss_target'][0]

    names = BF16_GATHER + F32_GATHER
    got = _ag_chips("ag_weights", [_stored(n, a[n]).astype(BF16) for n in BF16_GATHER] + [a[n] for n in F32_GATHER])
    ws = dict(zip(names, got))

    loss, gx, gs, grep = _device_step(x, target, ws, {n: a[n] for n in REPLICATED})
    loss = lax.psum(loss, ("x", "y", "c"))

    small = [n for n in MISC_SHARDED if n not in LARGE_SHARDED]
    small_shapes = [a[n].shape[1:] for n in small]
    rep_shapes = [a[n].shape for n in REPLICATED]
    small_rows = sum(_rows_of(s) for s in small_shapes)
    piece = _round_up(-(-sum(_rows_of(s) for s in rep_shapes) // (2 * N_CHIPS)), SUBLANE)
    rep_rows = 2 * N_CHIPS * piece
    rep_pack = _pack([grep[n] for n in REPLICATED], rep_rows).reshape(N_CHIPS, 2, piece, PACK_C)
    total = _round_up(small_rows + piece, ROW_TILE // 2)
    misc = _pack([gs[n] for n in small] + [rep_pack], total, lead=2)
    reds = _reduce_scatter([gs[n] for n in LARGE_SHARDED] + [misc])
    grad = dict(zip(LARGE_SHARDED, reds[:-1]))
    grad.update(zip(small, _unpack(reds[-1], small_shapes, lead=1)))
    rep_mine = reds[-1][:, small_rows:small_rows + piece]
    rep_all = _ag_chips("ag_replicated", [rep_mine])[0].reshape(1, rep_rows, PACK_C)

    outs = {}
    for n in SHARDED:
        shp = grad[n].shape
        three_d = (shp[0], -1, shp[-1])
        params = (grad[n], *(_stored(n, a[p + n]) for p in ('', 'm_', 'v_')))
        res3 = _adamw("adamw", *(t.reshape(three_d) for t in params))
        for kind, arr in zip(('grad', 'delta', 'new_m', 'new_v'), (grad[n], *res3)):
            outs[kind + '_' + n] = _stored(n, arr.reshape(shp))
    w, m, v = (_pack([a[p + n] for n in REPLICATED], rep_rows)[None] for p in ('', 'm_', 'v_'))
    res3 = _adamw("adamw_replicated", rep_all, w, m, v)
    for kind, arr in zip(('grad', 'delta', 'new_m', 'new_v'), (rep_all, *res3)):
        for n, t in zip(REPLICATED, _unpack(arr[0], rep_shapes)):
            outs[kind + '_' + n] = t
    res = [loss, gx[None]]
    for kind in ('grad', 'delta', 'new_m', 'new_v'):
        res += [outs[kind + '_' + n] for n in W_NAMES]
    return tuple(res)


def kernel(x, ffn1_norm, ffn1_w_gate, ffn1_w_up, ffn1_w_down, mix_norm, w_in, s5_lambda_re, s5_lambda_im, s5_log_dt, s5_b_re, s5_b_im, s5_c_re, s5_c_im, s5_d, s5_w_glu, gla_w_alpha, gla_b_alpha, gla_norm, attn_q_norm, attn_k_norm, w_branch_s5, w_branch_gla, w_branch_attn, w_merge_gate, b_merge_gate, w_out, ffn2_norm, ffn2_w_gate, ffn2_w_up, ffn2_w_down, final_norm, loss_target, m_ffn1_norm, m_ffn1_w_gate, m_ffn1_w_up, m_ffn1_w_down, m_mix_norm, m_w_in, m_s5_lambda_re, m_s5_lambda_im, m_s5_log_dt, m_s5_b_re, m_s5_b_im, m_s5_c_re, m_s5_c_im, m_s5_d, m_s5_w_glu, m_gla_w_alpha, m_gla_b_alpha, m_gla_norm, m_attn_q_norm, m_attn_k_norm, m_w_branch_s5, m_w_branch_gla, m_w_branch_attn, m_w_merge_gate, m_b_merge_gate, m_w_out, m_ffn2_norm, m_ffn2_w_gate, m_ffn2_w_up, m_ffn2_w_down, m_final_norm, v_ffn1_norm, v_ffn1_w_gate, v_ffn1_w_up, v_ffn1_w_down, v_mix_norm, v_w_in, v_s5_lambda_re, v_s5_lambda_im, v_s5_log_dt, v_s5_b_re, v_s5_b_im, v_s5_c_re, v_s5_c_im, v_s5_d, v_s5_w_glu, v_gla_w_alpha, v_gla_b_alpha, v_gla_norm, v_attn_q_norm, v_attn_k_norm, v_w_branch_s5, v_w_branch_gla, v_w_branch_attn, v_w_merge_gate, v_b_merge_gate, v_w_out, v_ffn2_norm, v_ffn2_w_gate, v_ffn2_w_up, v_ffn2_w_down, v_final_norm):
    return _train_step(dict(locals()))
```
